```python
import math
import jax, jax.numpy as jnp
from jax import lax
import numpy as np

D_MODEL = 1024
BATCH = 8
SEQ = 8192
DEPTH = 2

CHUNK = 64
Q_BLOCK = 128
EPS = 1e-6

MLA_HEADS = 8
QK_NOPE_DIM = 64
QK_ROPE_DIM = 32
V_HEAD_DIM = 64
Q_LORA_RANK = 768
KV_LORA_RANK = 256
ROPE_THETA = 10000.0
MLA_WIDTH = MLA_HEADS * V_HEAD_DIM

SSD_HEADS = 8
SSD_HEAD_DIM = 64
SSD_INNER = SSD_HEADS * SSD_HEAD_DIM
SSD_GROUPS = 2
SSD_STATE = 128
CONV_WIDTH = 4
CONV_DIM = SSD_INNER + 2 * SSD_GROUPS * SSD_STATE

MIX_WIDTH = MLA_WIDTH + SSD_INNER
D_FF = 4 * D_MODEL
IN_PROJ_DIM = Q_LORA_RANK + KV_LORA_RANK + QK_ROPE_DIM + SSD_INNER + CONV_DIM + SSD_HEADS

kernel_name = "hymba_mla_ssd_sandwich_trunk"


def rms_norm(x, w):
    xf = x.astype(jnp.float32)
    xf = xf * lax.rsqrt(jnp.mean(xf * xf, axis=-1, keepdims=True) + EPS)
    return xf.astype(x.dtype) * w


def rope_tables(positions):
    inv_freq = ROPE_THETA ** (-jnp.arange(0, QK_ROPE_DIM, 2, dtype=jnp.float32) / QK_ROPE_DIM)
    ang = positions[..., None].astype(jnp.float32) * inv_freq
    return jnp.cos(ang), jnp.sin(ang)


def apply_rope(t, cos, sin):
    half = t.shape[-1] // 2
    t1 = t[..., :half].astype(jnp.float32)
    t2 = t[..., half:].astype(jnp.float32)
    return jnp.concatenate([t1 * cos - t2 * sin, t2 * cos + t1 * sin], axis=-1).astype(t.dtype)


def mla_mixer(c_q, c_kv, k_rope, cos, sin, q_norm_w, w_uq, kv_norm_w, w_ukv):
    B, S, _ = c_q.shape
    q = (rms_norm(c_q, q_norm_w) @ w_uq).reshape(B, S, MLA_HEADS, QK_NOPE_DIM + QK_ROPE_DIM)
    q_nope = q[..., :QK_NOPE_DIM]
    q_rope = apply_rope(q[..., QK_NOPE_DIM:], cos[:, :, None], sin[:, :, None])
    k_rope = apply_rope(k_rope, cos, sin)
    kv = (rms_norm(c_kv, kv_norm_w) @ w_ukv).reshape(B, S, MLA_HEADS, QK_NOPE_DIM + V_HEAD_DIM)
    k_nope = kv[..., :QK_NOPE_DIM]
    v = kv[..., QK_NOPE_DIM:]
    scale = (QK_NOPE_DIM + QK_ROPE_DIM) ** -0.5
    nb = S // Q_BLOCK
    key_chunk = jnp.arange(S) // CHUNK

    def to_blocks(t):
        return jnp.moveaxis(t.reshape(B, nb, Q_BLOCK, *t.shape[2:]), 1, 0)

    def attend(args):
        blk, qn, qr = args
        s = (jnp.einsum("bqhd,bkhd->bhqk", qn, k_nope, preferred_element_type=jnp.float32)
             + jnp.einsum("bqhr,bkr->bhqk", qr, k_rope, preferred_element_type=jnp.float32)) * scale
        q_chunk = (blk * Q_BLOCK + jnp.arange(Q_BLOCK)) // CHUNK
        allowed = key_chunk[None, :] <= q_chunk[:, None]
        s = jnp.where(allowed[None, None], s, -jnp.inf)
        p = jax.nn.softmax(s, axis=-1).astype(v.dtype)
        return jnp.einsum("bhqk,bkhd->bqhd", p, v)

    o = lax.map(attend, (jnp.arange(nb), to_blocks(q_nope), to_blocks(q_rope)))
    return jnp.moveaxis(o, 0, 1).reshape(B, S, MLA_WIDTH)


def causal_depthwise_conv(x, w, b):
    S = x.shape[1]
    xp = jnp.pad(x, ((0, 0), (CONV_WIDTH - 1, 0), (0, 0)))
    y = xp[:, 0:S] * w[0]
    for k in range(1, CONV_WIDTH):
        y = y + xp[:, k:k + S] * w[k]
    return y + b


def segsum(a):
    L = a.shape[-1]
    cs = jnp.cumsum(a, axis=-1)
    diff = cs[..., :, None] - cs[..., None, :]
    tril = jnp.tril(jnp.ones((L, L), dtype=bool))
    return jnp.where(tril, diff, -jnp.inf)


def ssd_chunked(x, dt, A, Bm, Cm):
    Bsz, S, H, P = x.shape
    nc = S // CHUNK
    rep = H // SSD_GROUPS
    xdt = (x * dt[..., None]).reshape(Bsz, nc, CHUNK, H, P)
    a = jnp.transpose((dt * A).reshape(Bsz, nc, CHUNK, H), (0, 3, 1, 2))
    Bh = jnp.repeat(Bm, rep, axis=2).reshape(Bsz, nc, CHUNK, H, SSD_STATE)
    Ch = jnp.repeat(Cm, rep, axis=2).reshape(Bsz, nc, CHUNK, H, SSD_STATE)
    a_cs = jnp.cumsum(a, axis=-1)
    decay_in = jnp.exp(segsum(a))
    scores = jnp.einsum("bclhn,bcshn->bhcls", Ch, Bh) * decay_in
    y_diag = jnp.einsum("bhcls,bcshp->bclhp", scores, xdt)
    decay_states = jnp.exp(a_cs[..., -1:] - a_cs)
    states = jnp.einsum("bcshn,bhcs,bcshp->bchpn", Bh, decay_states, xdt)
    chunk_decay = jnp.exp(a_cs[..., -1])

    def step(h, inp):
        st, dec = inp
        return h * dec[..., None, None] + st, h

    init = jnp.zeros((Bsz, H, P, SSD_STATE), dtype=x.dtype)
    _, prev = lax.scan(step, init, (jnp.moveaxis(states, 1, 0), jnp.moveaxis(chunk_decay, 2, 0)))
    prev = jnp.moveaxis(prev, 0, 1)
    y_off = jnp.einsum("bclhn,bchpn,bhcl->bclhp", Ch, prev, jnp.exp(a_cs))
    return (y_diag + y_off).reshape(Bsz, S, H, P)


def ssd_mixer(z, xbc, dt_raw, conv_w, conv_b, dt_bias, a_log, d_skip, ssd_norm_w):
    B, S, _ = z.shape
    out_dtype = z.dtype
    xbc = jax.nn.silu(causal_depthwise_conv(xbc, conv_w, conv_b)).astype(jnp.float32)
    xs = xbc[..., :SSD_INNER].reshape(B, S, SSD_HEADS, SSD_HEAD_DIM)
    Bm = xbc[..., SSD_INNER:SSD_INNER + SSD_GROUPS * SSD_STATE].reshape(B, S, SSD_GROUPS, SSD_STATE)
    Cm = xbc[..., SSD_INNER + SSD_GROUPS * SSD_STATE:].reshape(B, S, SSD_GROUPS, SSD_STATE)
    dt = jax.nn.softplus(dt_raw.astype(jnp.float32) + dt_bias.astype(jnp.float32))
    A = -jnp.exp(a_log.astype(jnp.float32))
    y = ssd_chunked(xs, dt, A, Bm, Cm) + d_skip.astype(jnp.float32)[:, None] * xs
    y = y.reshape(B, S, SSD_INNER) * jax.nn.silu(z.astype(jnp.float32))
    yg = y.reshape(B, S, SSD_GROUPS, SSD_INNER // SSD_GROUPS)
    yg = yg * lax.rsqrt(jnp.mean(yg * yg, axis=-1, keepdims=True) + EPS)
    return (yg.reshape(B, S, SSD_INNER) * ssd_norm_w.astype(jnp.float32)).astype(out_dtype)


def _fwd_setup_inputs(seed: int = 0) -> dict:
    key = jax.random.key(seed)
    ks = jax.random.split(key, 24)
    f32 = jnp.float32

    def normal(k, shape, scale):
        return jax.random.normal(k, shape, f32) * scale

    def gain(k, shape):
        return 1.0 + 0.02 * jax.random.normal(k, shape, f32)

    x = jax.random.normal(ks[0], (BATCH, SEQ, D_MODEL), f32)
    offset = jax.random.randint(ks[1], (BATCH, 1), 0, 4096, dtype=jnp.int32)
    positions = (offset + jnp.arange(SEQ, dtype=jnp.int32)[None, :]).astype(jnp.int32)
    dt0 = jnp.exp(jax.random.uniform(ks[2], (DEPTH, SSD_HEADS), f32)
                  * (math.log(0.1) - math.log(0.001)) + math.log(0.001))
    dt_bias = dt0 + jnp.log(-jnp.expm1(-dt0))
    a_log = jnp.log(jax.random.uniform(ks[3], (DEPTH, SSD_HEADS), f32, minval=1.0, maxval=16.0))
    return {
        "x": x,
        "positions": positions,
        "pre_mix_norm": gain(ks[4], (DEPTH, D_MODEL)),
        "w_in": normal(ks[5], (DEPTH, D_MODEL, IN_PROJ_DIM), D_MODEL ** -0.5),
        "q_norm": gain(ks[6], (DEPTH, Q_LORA_RANK)),
        "w_uq": normal(ks[7], (DEPTH, Q_LORA_RANK, MLA_HEADS * (QK_NOPE_DIM + QK_ROPE_DIM)), Q_LORA_RANK ** -0.5),
        "kv_norm": gain(ks[8], (DEPTH, KV_LORA_RANK)),
        "w_ukv": normal(ks[9], (DEPTH, KV_LORA_RANK, MLA_HEADS * (QK_NOPE_DIM + V_HEAD_DIM)), KV_LORA_RANK ** -0.5),
        "conv_w": normal(ks[10], (DEPTH, CONV_WIDTH, CONV_DIM), CONV_WIDTH ** -0.5),
        "conv_b": normal(ks[11], (DEPTH, CONV_DIM), 0.01),
        "dt_bias": dt_bias,
        "a_log": a_log,
        "d_skip": gain(ks[12], (DEPTH, SSD_HEADS)),
        "ssd_norm": gain(ks[13], (DEPTH, SSD_INNER)),
        "w_out": normal(ks[14], (DEPTH, MIX_WIDTH, D_MODEL), MIX_WIDTH ** -0.5),
        "post_mix_norm": gain(ks[15], (DEPTH, D_MODEL)),
        "pre_mlp_norm": gain(ks[16], (DEPTH, D_MODEL)),
        "w_up": normal(ks[17], (DEPTH, D_MODEL, D_FF), D_MODEL ** -0.5),
        "w_down": normal(ks[18], (DEPTH, D_FF, D_MODEL), D_FF ** -0.5),
        "post_mlp_norm": gain(ks[19], (DEPTH, D_MODEL)),
    }


def _fwd_reference(x, positions, pre_mix_norm, w_in, q_norm, w_uq, kv_norm, w_ukv, conv_w, conv_b,
              dt_bias, a_log, d_skip, ssd_norm, w_out, post_mix_norm, pre_mlp_norm, w_up,
              w_down, post_mlp_norm):
    cos, sin = rope_tables(positions)
    s1 = Q_LORA_RANK
    s2 = s1 + KV_LORA_RANK
    s3 = s2 + QK_ROPE_DIM
    s4 = s3 + SSD_INNER
    s5 = s4 + CONV_DIM
    h = x
    for l in range(DEPTH):
        u = rms_norm(h, pre_mix_norm[l])
        proj = u @ w_in[l]
        c_q, c_kv, k_rope = proj[..., :s1], proj[..., s1:s2], proj[..., s2:s3]
        z, xbc, dt_raw = proj[..., s3:s4], proj[..., s4:s5], proj[..., s5:]
        y_att = mla_mixer(c_q, c_kv, k_rope, cos, sin, q_norm[l], w_uq[l], kv_norm[l], w_ukv[l])
        y_ssd = ssd_mixer(z, xbc, dt_raw, conv_w[l], conv_b[l], dt_bias[l], a_log[l], d_skip[l], ssd_norm[l])
        mixed = jnp.concatenate([y_att, y_ssd], axis=-1) @ w_out[l]
        h = h + rms_norm(mixed, post_mix_norm[l])
        m = rms_norm(h, pre_mlp_norm[l])
        m = jnp.square(jax.nn.relu(m @ w_up[l])) @ w_down[l]
        h = h + rms_norm(m, post_mlp_norm[l])
    return h


import jax as _jax
import jax.numpy as _jnp

TWIN_FORMAT = 'train_step'
FWD_PARAMS = ['x', 'positions', 'pre_mix_norm', 'w_in', 'q_norm', 'w_uq', 'kv_norm', 'w_ukv', 'conv_w', 'conv_b', 'dt_bias', 'a_log', 'd_skip', 'ssd_norm', 'w_out', 'post_mix_norm', 'pre_mlp_norm', 'w_up', 'w_down', 'post_mlp_norm']
TWIN_WEIGHTS = ['pre_mix_norm', 'w_in', 'q_norm', 'w_uq', 'kv_norm', 'w_ukv', 'conv_w', 'conv_b', 'dt_bias', 'a_log', 'd_skip', 'ssd_norm', 'w_out', 'post_mix_norm', 'pre_mlp_norm', 'w_up', 'w_down', 'post_mlp_norm']
TWIN_DIFF_INPUT = 'x'
TWIN_INPUTS = ['x', 'positions', 'pre_mix_norm', 'w_in', 'q_norm', 'w_uq', 'kv_norm', 'w_ukv', 'conv_w', 'conv_b', 'dt_bias', 'a_log', 'd_skip', 'ssd_norm', 'w_out', 'post_mix_norm', 'pre_mlp_norm', 'w_up', 'w_down', 'post_mlp_norm', 'loss_target', 'm_pre_mix_norm', 'm_w_in', 'm_q_norm', 'm_w_uq', 'm_kv_norm', 'm_w_ukv', 'm_conv_w', 'm_conv_b', 'm_dt_bias', 'm_a_log', 'm_d_skip', 'm_ssd_norm', 'm_w_out', 'm_post_mix_norm', 'm_pre_mlp_norm', 'm_w_up', 'm_w_down', 'm_post_mlp_norm', 'v_pre_mix_norm', 'v_w_in', 'v_q_norm', 'v_w_uq', 'v_kv_norm', 'v_w_ukv', 'v_conv_w', 'v_conv_b', 'v_dt_bias', 'v_a_log', 'v_d_skip', 'v_ssd_norm', 'v_w_out', 'v_post_mix_norm', 'v_pre_mlp_norm', 'v_w_up', 'v_w_down', 'v_post_mlp_norm']
TWIN_OUTPUTS = ['loss', 'grad_x', 'grad_pre_mix_norm', 'grad_w_in', 'grad_q_norm', 'grad_w_uq', 'grad_kv_norm', 'grad_w_ukv', 'grad_conv_w', 'grad_conv_b', 'grad_dt_bias', 'grad_a_log', 'grad_d_skip', 'grad_ssd_norm', 'grad_w_out', 'grad_post_mix_norm', 'grad_pre_mlp_norm', 'grad_w_up', 'grad_w_down', 'grad_post_mlp_norm', 'delta_pre_mix_norm', 'delta_w_in', 'delta_q_norm', 'delta_w_uq', 'delta_kv_norm', 'delta_w_ukv', 'delta_conv_w', 'delta_conv_b', 'delta_dt_bias', 'delta_a_log', 'delta_d_skip', 'delta_ssd_norm', 'delta_w_out', 'delta_post_mix_norm', 'delta_pre_mlp_norm', 'delta_w_up', 'delta_w_down', 'delta_post_mlp_norm', 'new_m_pre_mix_norm', 'new_m_w_in', 'new_m_q_norm', 'new_m_w_uq', 'new_m_kv_norm', 'new_m_w_ukv', 'new_m_conv_w', 'new_m_conv_b', 'new_m_dt_bias', 'new_m_a_log', 'new_m_d_skip', 'new_m_ssd_norm', 'new_m_w_out', 'new_m_post_mix_norm', 'new_m_pre_mlp_norm', 'new_m_w_up', 'new_m_w_down', 'new_m_post_mlp_norm', 'new_v_pre_mix_norm', 'new_v_w_in', 'new_v_q_norm', 'new_v_w_uq', 'new_v_kv_norm', 'new_v_w_ukv', 'new_v_conv_w', 'new_v_conv_b', 'new_v_dt_bias', 'new_v_a_log', 'new_v_d_skip', 'new_v_ssd_norm', 'new_v_w_out', 'new_v_post_mix_norm', 'new_v_pre_mlp_norm', 'new_v_w_up', 'new_v_w_down', 'new_v_post_mlp_norm']
TWIN_LEAF_KINDS = {'loss': 'loss', 'grad_x': 'grad_x', 'grad_pre_mix_norm': 'grad_w', 'grad_w_in': 'grad_w', 'grad_q_norm': 'grad_w', 'grad_w_uq': 'grad_w', 'grad_kv_norm': 'grad_w', 'grad_w_ukv': 'grad_w', 'grad_conv_w': 'grad_w', 'grad_conv_b': 'grad_w', 'grad_dt_bias': 'grad_w', 'grad_a_log': 'grad_w', 'grad_d_skip': 'grad_w', 'grad_ssd_norm': 'grad_w', 'grad_w_out': 'grad_w', 'grad_post_mix_norm': 'grad_w', 'grad_pre_mlp_norm': 'grad_w', 'grad_w_up': 'grad_w', 'grad_w_down': 'grad_w', 'grad_post_mlp_norm': 'grad_w', 'delta_pre_mix_norm': 'delta_w', 'delta_w_in': 'delta_w', 'delta_q_norm': 'delta_w', 'delta_w_uq': 'delta_w', 'delta_kv_norm': 'delta_w', 'delta_w_ukv': 'delta_w', 'delta_conv_w': 'delta_w', 'delta_conv_b': 'delta_w', 'delta_dt_bias': 'delta_w', 'delta_a_log': 'delta_w', 'delta_d_skip': 'delta_w', 'delta_ssd_norm': 'delta_w', 'delta_w_out': 'delta_w', 'delta_post_mix_norm': 'delta_w', 'delta_pre_mlp_norm': 'delta_w', 'delta_w_up': 'delta_w', 'delta_w_down': 'delta_w', 'delta_post_mlp_norm': 'delta_w', 'new_m_pre_mix_norm': 'new_m', 'new_m_w_in': 'new_m', 'new_m_q_norm': 'new_m', 'new_m_w_uq': 'new_m', 'new_m_kv_norm': 'new_m', 'new_m_w_ukv': 'new_m', 'new_m_conv_w': 'new_m', 'new_m_conv_b': 'new_m', 'new_m_dt_bias': 'new_m', 'new_m_a_log': 'new_m', 'new_m_d_skip': 'new_m', 'new_m_ssd_norm': 'new_m', 'new_m_w_out': 'new_m', 'new_m_post_mix_norm': 'new_m', 'new_m_pre_mlp_norm': 'new_m', 'new_m_w_up': 'new_m', 'new_m_w_down': 'new_m', 'new_m_post_mlp_norm': 'new_m', 'new_v_pre_mix_norm': 'new_v', 'new_v_w_in': 'new_v', 'new_v_q_norm': 'new_v', 'new_v_w_uq': 'new_v', 'new_v_kv_norm': 'new_v', 'new_v_w_ukv': 'new_v', 'new_v_conv_w': 'new_v', 'new_v_conv_b': 'new_v', 'new_v_dt_bias': 'new_v', 'new_v_a_log': 'new_v', 'new_v_d_skip': 'new_v', 'new_v_ssd_norm': 'new_v', 'new_v_w_out': 'new_v', 'new_v_post_mix_norm': 'new_v', 'new_v_pre_mlp_norm': 'new_v', 'new_v_w_up': 'new_v', 'new_v_w_down': 'new_v', 'new_v_post_mlp_norm': 'new_v'}


def _forward(args):
    return _fwd_reference(*[args[k] for k in FWD_PARAMS])


def _output_shape():
    out = _jax.eval_shape(lambda: _forward(_fwd_setup_inputs(0)))
    return out.shape, out.dtype

N_MICROBATCH = 1
ADAM_LR = 0.001
ADAM_B1 = 0.9
ADAM_B2 = 0.999
ADAM_EPS = 1e-08
ADAM_WD = 0.01
ADAM_STEP = 10
PER_EXAMPLE_BATCH_AXIS = {'x': 0, 'positions': 0, 'loss_target': 0}
SHARED_INPUTS = []
_WEIGHT_DTYPES = {'pre_mix_norm': _jnp.float32, 'w_in': _jnp.float32, 'q_norm': _jnp.float32, 'w_uq': _jnp.float32, 'kv_norm': _jnp.float32, 'w_ukv': _jnp.float32, 'conv_w': _jnp.float32, 'conv_b': _jnp.float32, 'dt_bias': _jnp.float32, 'a_log': _jnp.float32, 'd_skip': _jnp.float32, 'ssd_norm': _jnp.float32, 'w_out': _jnp.float32, 'post_mix_norm': _jnp.float32, 'pre_mlp_norm': _jnp.float32, 'w_up': _jnp.float32, 'w_down': _jnp.float32, 'post_mlp_norm': _jnp.float32}
MOMENT_SCALE = {'pre_mix_norm': 8.349052e+00, 'w_in': 5.079308e+00, 'q_norm': 3.092627e-01, 'w_uq': 2.736221e-01, 'kv_norm': 1.876618e+01, 'w_ukv': 8.554311e+00, 'conv_w': 7.458541e+00, 'conv_b': 2.650120e+01, 'dt_bias': 8.232489e+00, 'a_log': 3.259099e+01, 'd_skip': 3.238726e+01, 'ssd_norm': 1.802584e+01, 'w_out': 1.474289e+01, 'post_mix_norm': 6.524124e+01, 'pre_mlp_norm': 5.972379e+00, 'w_up': 3.066061e+00, 'w_down': 1.989170e+01, 'post_mlp_norm': 6.913547e+01}


def _to_microbatches(a, axis):
    t = _jnp.moveaxis(a, axis, 0)
    t = t.reshape((N_MICROBATCH, t.shape[0] // N_MICROBATCH) + t.shape[1:])
    return _jnp.moveaxis(t, 1, axis + 1)


def setup_inputs(seed: int = 0) -> dict:
    inp = _fwd_setup_inputs(seed)
    key = _jax.random.fold_in(_jax.random.key(seed), 7919)
    shape, _ = _output_shape()
    out = dict(inp)
    out["loss_target"] = _jax.random.normal(_jax.random.fold_in(key, 0), shape, _jnp.float32)
    for i, name in enumerate(TWIN_WEIGHTS):
        w = inp[name].astype(_jnp.float32)
        if MOMENT_SCALE is None:
            s = _jnp.sqrt(_jnp.mean(_jnp.square(w)) + 1e-30)
        else:
            s = MOMENT_SCALE[name]
        km, kv = _jax.random.split(_jax.random.fold_in(key, i + 1))
        out[name] = w
        out["m_" + name] = s * _jax.random.normal(km, w.shape, _jnp.float32)
        out["v_" + name] = (s * s) * _jax.random.uniform(kv, w.shape, _jnp.float32, 0.5, 1.5)
    if N_MICROBATCH > 1:
        for name, axis in PER_EXAMPLE_BATCH_AXIS.items():
            out[name] = _to_microbatches(out[name], axis)
    return {'x': out['x'], 'positions': out['positions'], 'pre_mix_norm': out['pre_mix_norm'], 'w_in': out['w_in'], 'q_norm': out['q_norm'], 'w_uq': out['w_uq'], 'kv_norm': out['kv_norm'], 'w_ukv': out['w_ukv'], 'conv_w': out['conv_w'], 'conv_b': out['conv_b'], 'dt_bias': out['dt_bias'], 'a_log': out['a_log'], 'd_skip': out['d_skip'], 'ssd_norm': out['ssd_norm'], 'w_out': out['w_out'], 'post_mix_norm': out['post_mix_norm'], 'pre_mlp_norm': out['pre_mlp_norm'], 'w_up': out['w_up'], 'w_down': out['w_down'], 'post_mlp_norm': out['post_mlp_norm'], 'loss_target': out['loss_target'], 'm_pre_mix_norm': out['m_pre_mix_norm'], 'm_w_in': out['m_w_in'], 'm_q_norm': out['m_q_norm'], 'm_w_uq': out['m_w_uq'], 'm_kv_norm': out['m_kv_norm'], 'm_w_ukv': out['m_w_ukv'], 'm_conv_w': out['m_conv_w'], 'm_conv_b': out['m_conv_b'], 'm_dt_bias': out['m_dt_bias'], 'm_a_log': out['m_a_log'], 'm_d_skip': out['m_d_skip'], 'm_ssd_norm': out['m_ssd_norm'], 'm_w_out': out['m_w_out'], 'm_post_mix_norm': out['m_post_mix_norm'], 'm_pre_mlp_norm': out['m_pre_mlp_norm'], 'm_w_up': out['m_w_up'], 'm_w_down': out['m_w_down'], 'm_post_mlp_norm': out['m_post_mlp_norm'], 'v_pre_mix_norm': out['v_pre_mix_norm'], 'v_w_in': out['v_w_in'], 'v_q_norm': out['v_q_norm'], 'v_w_uq': out['v_w_uq'], 'v_kv_norm': out['v_kv_norm'], 'v_w_ukv': out['v_w_ukv'], 'v_conv_w': out['v_conv_w'], 'v_conv_b': out['v_conv_b'], 'v_dt_bias': out['v_dt_bias'], 'v_a_log': out['v_a_log'], 'v_d_skip': out['v_d_skip'], 'v_ssd_norm': out['v_ssd_norm'], 'v_w_out': out['v_w_out'], 'v_post_mix_norm': out['v_post_mix_norm'], 'v_pre_mlp_norm': out['v_pre_mlp_norm'], 'v_w_up': out['v_w_up'], 'v_w_down': out['v_w_down'], 'v_post_mlp_norm': out['v_post_mlp_norm']}


def _loss(weights, diff, rest, loss_target):
    with _jax.named_scope("forward"):
        args = {**rest, TWIN_DIFF_INPUT: diff, **{k: w.astype(_WEIGHT_DTYPES[k]) for k, w in weights.items()}}
        y = _forward(args)
    with _jax.named_scope("loss_head"):
        err = _jnp.square(y.astype(_jnp.float32) - loss_target)
        return 0.5 * _jnp.sum(_jnp.mean(err, axis=-1)) if err.ndim else 0.5 * err


def _adamw(w, g, m, v):
    m = ADAM_B1 * m + (1.0 - ADAM_B1) * g
    v = ADAM_B2 * v + (1.0 - ADAM_B2) * _jnp.square(g)
    m_hat = m / (1.0 - ADAM_B1 ** ADAM_STEP)
    v_hat = v / (1.0 - ADAM_B2 ** ADAM_STEP)
    delta = -ADAM_LR * (m_hat / (_jnp.sqrt(v_hat) + ADAM_EPS) + ADAM_WD * w)
    return delta, m, v


def reference(x, positions, pre_mix_norm, w_in, q_norm, w_uq, kv_norm, w_ukv, conv_w, conv_b, dt_bias, a_log, d_skip, ssd_norm, w_out, post_mix_norm, pre_mlp_norm, w_up, w_down, post_mlp_norm, loss_target, m_pre_mix_norm, m_w_in, m_q_norm, m_w_uq, m_kv_norm, m_w_ukv, m_conv_w, m_conv_b, m_dt_bias, m_a_log, m_d_skip, m_ssd_norm, m_w_out, m_post_mix_norm, m_pre_mlp_norm, m_w_up, m_w_down, m_post_mlp_norm, v_pre_mix_norm, v_w_in, v_q_norm, v_w_uq, v_kv_norm, v_w_ukv, v_conv_w, v_conv_b, v_dt_bias, v_a_log, v_d_skip, v_ssd_norm, v_w_out, v_post_mix_norm, v_pre_mlp_norm, v_w_up, v_w_down, v_post_mlp_norm):
    given = dict(x=x, positions=positions, pre_mix_norm=pre_mix_norm, w_in=w_in, q_norm=q_norm, w_uq=w_uq, kv_norm=kv_norm, w_ukv=w_ukv, conv_w=conv_w, conv_b=conv_b, dt_bias=dt_bias, a_log=a_log, d_skip=d_skip, ssd_norm=ssd_norm, w_out=w_out, post_mix_norm=post_mix_norm, pre_mlp_norm=pre_mlp_norm, w_up=w_up, w_down=w_down, post_mlp_norm=post_mlp_norm, loss_target=loss_target, m_pre_mix_norm=m_pre_mix_norm, m_w_in=m_w_in, m_q_norm=m_q_norm, m_w_uq=m_w_uq, m_kv_norm=m_kv_norm, m_w_ukv=m_w_ukv, m_conv_w=m_conv_w, m_conv_b=m_conv_b, m_dt_bias=m_dt_bias, m_a_log=m_a_log, m_d_skip=m_d_skip, m_ssd_norm=m_ssd_norm, m_w_out=m_w_out, m_post_mix_norm=m_post_mix_norm, m_pre_mlp_norm=m_pre_mlp_norm, m_w_up=m_w_up, m_w_down=m_w_down, m_post_mlp_norm=m_post_mlp_norm, v_pre_mix_norm=v_pre_mix_norm, v_w_in=v_w_in, v_q_norm=v_q_norm, v_w_uq=v_w_uq, v_kv_norm=v_kv_norm, v_w_ukv=v_w_ukv, v_conv_w=v_conv_w, v_conv_b=v_conv_b, v_dt_bias=v_dt_bias, v_a_log=v_a_log, v_d_skip=v_d_skip, v_ssd_norm=v_ssd_norm, v_w_out=v_w_out, v_post_mix_norm=v_post_mix_norm, v_pre_mlp_norm=v_pre_mlp_norm, v_w_up=v_w_up, v_w_down=v_w_down, v_post_mlp_norm=v_post_mlp_norm)
    weights = {n: given[n] for n in TWIN_WEIGHTS}
    shared = {n: given[n] for n in SHARED_INPUTS}
    per_example = {n: given[n] for n in ['x', 'positions']}
    grad_fn = _jax.value_and_grad(_loss, argnums=(0, 1))

    def one_microbatch(ex, loss_target):
        ex = dict(ex)
        diff = ex.pop(TWIN_DIFF_INPUT)
        return grad_fn(weights, diff, {**shared, **ex}, loss_target)

    if N_MICROBATCH == 1:
        loss, (grad_w, grad_x) = one_microbatch(per_example, given["loss_target"])
    else:
        def body(carry, xs):
            loss_sum, grad_sum = carry
            l_k, (gw_k, gx_k) = one_microbatch(xs[0], xs[1])
            with _jax.named_scope("update"):
                return (loss_sum + l_k, _jax.tree.map(_jnp.add, grad_sum, gw_k)), gx_k

        init = (_jnp.zeros((), _jnp.float32), _jax.tree.map(_jnp.zeros_like, weights))
        (loss, grad_w), grad_x = _jax.lax.scan(body, init, (per_example, given["loss_target"]))
    with _jax.named_scope("update"):
        delta_w, new_m, new_v = {}, {}, {}
        for n in TWIN_WEIGHTS:
            delta_w[n], new_m[n], new_v[n] = _adamw(weights[n], grad_w[n], given["m_" + n], given["v_" + n])
    return (loss, grad_x, *[grad_w[n] for n in TWIN_WEIGHTS], *[delta_w[n] for n in TWIN_WEIGHTS],
            *[new_m[n] for n in TWIN_WEIGHTS], *[new_v[n] for n in TWIN_WEIGHTS])
```

```python
import functools
import math

import jax
import jax.numpy as jnp
import numpy as np
from jax import lax
from jax.experimental import pallas as pl
from jax.experimental.pallas import tpu as pltpu

F32 = jnp.float32
BF16 = jnp.bfloat16
HI = lax.Precision.HIGHEST

D_MODEL = 1024
DEPTH = 2
N_DEV = 8
CHUNK = 64
EPS = 1e-6
MLA_HEADS = 8
QK_NOPE = 64
QK_ROPE = 32
V_DIM = 64
Q_RANK = 768
KV_RANK = 256
ROPE_THETA = 10000.0
SSD_HEADS = 8
SSD_P = 64
SSD_INNER = 512
SSD_GROUPS = 2
SSD_N = 128
CONV_W = 4
CONV_DIM = 1024
D_FF = 4096
IN_PROJ = 2600
HEAD_PAD = 128
IN_PAD = 2688
MISC_ROPE = 64
MISC_DT = 96
ATT_SCALE = (QK_NOPE + QK_ROPE) ** -0.5

ADAM_LR = 0.001
ADAM_B1 = 0.9
ADAM_B2 = 0.999
ADAM_EPS = 1e-08
ADAM_WD = 0.01
ADAM_STEP = 10

TM = 512
TQ = 256
SSD_ROWS = 256
VMEM_LIMIT = 56 * 1024 * 1024

_NT = (((1,), (1,)), ((), ()))
_TN = (((0,), (0,)), ((), ()))


def _params(**kw):
    return pltpu.CompilerParams(vmem_limit_bytes=VMEM_LIMIT, **kw)


def _dot(a, b, precision=None):
    return jnp.dot(a, b, preferred_element_type=F32, precision=precision)


def _dot_nt(a, b, precision=None):
    return lax.dot_general(a, b, _NT, preferred_element_type=F32, precision=precision)


def _dot_tn(a, b, precision=None):
    return lax.dot_general(a, b, _TN, preferred_element_type=F32, precision=precision)


def _full(shape):
    n = len(shape)
    return pl.BlockSpec(shape, lambda *_: (0,) * n)


def _resident(shape):
    n = len(shape)
    return pl.BlockSpec(shape, lambda *_: (0,) * n, pipeline_mode=pl.Buffered(1))


def _rows(tm, width):
    return pl.BlockSpec((tm, width), lambda i: (i, 0))


def _rms_fwd(x, w):
    r = lax.rsqrt(jnp.mean(x * x, axis=-1, keepdims=True) + EPS)
    return (x * r) * w


def _rms_bwd(x, w, dy):
    r = lax.rsqrt(jnp.mean(x * x, axis=-1, keepdims=True) + EPS)
    xh = x * r
    dxn = dy * w
    dx = r * (dxn - xh * jnp.mean(dxn * xh, axis=-1, keepdims=True))
    return dx, dy * xh


def _acc_rows(ref, val, first):
    s = jnp.sum(val, axis=0, keepdims=True)

    @pl.when(first)
    def _():
        ref[...] = s

    @pl.when(jnp.logical_not(first))
    def _():
        ref[...] += s


def _rope(t, cosf, sinf, sign):
    lane = lax.broadcasted_iota(jnp.int32, t.shape, 1)
    rot = jnp.where(lane < MISC_ROPE + QK_ROPE // 2, -pltpu.roll(t, HEAD_PAD - QK_ROPE // 2, 1), pltpu.roll(t, QK_ROPE // 2, 1))
    return t * cosf + sign * (rot * sinf)


def _rope_tables(pos, invf):
    s = pos.shape[0]

    def body(pos_ref, invf_ref, cos_ref, sin_ref):
        ang = pos_ref[...].astype(F32) * invf_ref[...]
        cos_ref[...] = jnp.cos(ang)
        sin_ref[...] = jnp.sin(ang)

    return pl.pallas_call(
        body, name="rope_tables", grid=(s // TM,),
        in_specs=[_rows(TM, 1), _full((1, HEAD_PAD))],
        out_specs=[_rows(TM, HEAD_PAD), _rows(TM, HEAD_PAD)],
        out_shape=[jax.ShapeDtypeStruct((s, HEAD_PAD), F32)] * 2,
    )(pos, invf)


def _inproj_fwd(h, nw, win):
    s = h.shape[0]

    def body(h_ref, nw_ref, w_ref, ub_ref, cq_ref, ckv_ref, misc_ref, z_ref, xbc_ref):
        ub = _rms_fwd(h_ref[...], nw_ref[...]).astype(BF16)
        ub_ref[...] = ub
        proj = _dot(ub, w_ref[...])
        cq_ref[...] = proj[:, 0:768]
        ckv_ref[...] = proj[:, 768:1024]
        misc_ref[...] = proj[:, 1024:1152]
        z_ref[...] = proj[:, 1152:1664]
        xbc_ref[...] = proj[:, 1664:2688]

    widths = (768, 256, 128, 512, 1024)
    return pl.pallas_call(
        body, name="inproj_fwd", grid=(s // TM,),
        in_specs=[_rows(TM, D_MODEL), _full((1, D_MODEL)), _resident((D_MODEL, IN_PAD))],
        out_specs=[_rows(TM, D_MODEL)] + [_rows(TM, w) for w in widths],
        out_shape=[jax.ShapeDtypeStruct((s, D_MODEL), BF16)] + [jax.ShapeDtypeStruct((s, w), F32) for w in widths],
        compiler_params=_params(),
    )(h, nw, win)


def _qkv_fwd(cq, ckv, misc, qnw, kvnw, wuq, wkv, cosf, sinf):
    s = cq.shape[0]

    def body(cq_ref, ckv_ref, misc_ref, qnw_ref, kvnw_ref, wuq_ref, wkv_ref, cos_ref, sin_ref,
             cqn_ref, ckvn_ref, q_ref, k_ref, v_ref):
        cosf, sinf = cos_ref[...], sin_ref[...]
        cqn = _rms_fwd(cq_ref[...], qnw_ref[...]).astype(BF16)
        cqn_ref[...] = cqn
        q = _dot(cqn, wuq_ref[...])
        ckvn = _rms_fwd(ckv_ref[...], kvnw_ref[...]).astype(BF16)
        ckvn_ref[...] = ckvn
        kv = _dot(ckvn, wkv_ref[...])
        m = misc_ref[...]
        lane = lax.broadcasted_iota(jnp.int32, m.shape, 1)
        in_rope = jnp.logical_and(lane >= MISC_ROPE, lane < MISC_ROPE + QK_ROPE)
        kr = jnp.where(in_rope, _rope(m, cosf, sinf, 1.0), 0.0)
        for hd in range(MLA_HEADS):
            cols = slice(hd * HEAD_PAD, (hd + 1) * HEAD_PAD)
            q_ref[:, cols] = _rope(q[:, cols], cosf, sinf, 1.0).astype(BF16)
            k_ref[:, cols] = (kv[:, cols] + kr).astype(BF16)
        v_ref[...] = kv[:, MLA_HEADS * HEAD_PAD:].astype(BF16)

    wide = MLA_HEADS * HEAD_PAD
    return pl.pallas_call(
        body, name="qkv_fwd", grid=(s // TM,),
        in_specs=[_rows(TM, Q_RANK), _rows(TM, KV_RANK), _rows(TM, HEAD_PAD), _full((1, Q_RANK)), _full((1, KV_RANK)),
                  _resident((Q_RANK, wide)), _resident((KV_RANK, 2 * wide)), _rows(TM, HEAD_PAD), _rows(TM, HEAD_PAD)],
        out_specs=[_rows(TM, Q_RANK), _rows(TM, KV_RANK), _rows(TM, wide), _rows(TM, wide), _rows(TM, wide)],
        out_shape=[jax.ShapeDtypeStruct((s, Q_RANK), BF16), jax.ShapeDtypeStruct((s, KV_RANK), BF16)]
        + [jax.ShapeDtypeStruct((s, wide), BF16)] * 3,
        compiler_params=_params(),
    )(cq, ckv, misc, qnw, kvnw, wuq, wkv, cosf, sinf)


def _chunk_mask():
    row = lax.broadcasted_iota(jnp.int32, (TQ, TQ), 0) // CHUNK
    col = lax.broadcasted_iota(jnp.int32, (TQ, TQ), 1) // CHUNK
    return col <= row


def _attn_fwd(q, k, v):
    s = q.shape[0]
    nq = s // TQ

    def body(q_ref, k_ref, v_ref, o_ref, lse_ref, m_s, l_s, acc_s):
        qi = pl.program_id(1)
        qb = q_ref[...]
        m_s[...] = jnp.full(m_s.shape, -jnp.inf, F32)
        l_s[...] = jnp.zeros(l_s.shape, F32)
        acc_s[...] = jnp.zeros(acc_s.shape, F32)

        def step(kb, masked):
            r0 = pl.multiple_of(kb * TQ, TQ)
            sc = _dot_nt(qb, k_ref[pl.ds(r0, TQ), :]) * ATT_SCALE
            if masked:
                sc = jnp.where(_chunk_mask(), sc, -jnp.inf)
            m_old = m_s[...]
            m_new = jnp.maximum(m_old, jnp.max(sc, axis=-1, keepdims=True))
            alpha = jnp.exp(m_old - m_new)
            p = jnp.exp(sc - m_new)
            l_s[...] = alpha * l_s[...] + jnp.sum(p, axis=-1, keepdims=True)
            acc_s[...] = alpha * acc_s[...] + _dot(p.astype(BF16), v_ref[pl.ds(r0, TQ), :])
            m_s[...] = m_new

        def loop(kb, c):
            step(kb, False)
            return c

        lax.fori_loop(0, qi, loop, 0)
        step(qi, True)
        o_ref[...] = (acc_s[...] / l_s[...]).astype(BF16)
        lse_ref[0] = m_s[...] + jnp.log(l_s[...])

    return pl.pallas_call(
        body, name="attn_fwd", grid=(MLA_HEADS, nq),
        in_specs=[pl.BlockSpec((TQ, HEAD_PAD), lambda h, i: (i, h)),
                  pl.BlockSpec((s, HEAD_PAD), lambda h, i: (0, h)),
                  pl.BlockSpec((s, HEAD_PAD), lambda h, i: (0, h))],
        out_specs=[pl.BlockSpec((TQ, HEAD_PAD), lambda h, i: (i, h)),
                   pl.BlockSpec((1, TQ, 1), lambda h, i: (h, i, 0))],
        out_shape=[jax.ShapeDtypeStruct((s, MLA_HEADS * HEAD_PAD), BF16), jax.ShapeDtypeStruct((MLA_HEADS, s, 1), F32)],
        scratch_shapes=[pltpu.VMEM((TQ, 1), F32), pltpu.VMEM((TQ, 1), F32), pltpu.VMEM((TQ, HEAD_PAD), F32)],
        compiler_params=_params(),
    )(q, k, v)


def _ssd_consts():
    emisc = np.zeros((HEAD_PAD, SSD_INNER), np.float32)
    for hd in range(SSD_HEADS):
        emisc[MISC_DT + hd, hd * SSD_P:(hd + 1) * SSD_P] = 1.0
    idx = np.arange(CHUNK)
    tri = (idx[:, None] >= idx[None, :]).astype(np.float32)
    return jnp.asarray(emisc), jnp.asarray(emisc.T.copy()), jnp.asarray(tri), jnp.asarray(tri.T.copy())


def _ssd_chunk_common(cc, misc, emisc, tri, trit, dtb, a_exp):
    xa = cc * jax.nn.sigmoid(cc)
    dtr = _dot(misc, emisc, HI) + dtb
    dt = jax.nn.softplus(dtr)
    a = dt * a_exp
    acs = _dot(tri, a, HI)
    acs_t = _dot_tn(a, trit, HI)
    alast = acs[CHUNK - 1:CHUNK, :]
    return xa, dtr, dt, acs, acs_t, alast


def _decay(acs, acs_t, hd):
    row = lax.broadcasted_iota(jnp.int32, (CHUNK, CHUNK), 0)
    col = lax.broadcasted_iota(jnp.int32, (CHUNK, CHUNK), 1)
    diff = acs[:, hd * SSD_P:hd * SSD_P + 1] - acs_t[hd * SSD_P:hd * SSD_P + 1, :]
    return jnp.exp(jnp.where(row >= col, diff, -jnp.inf))


def _half_mask(hh):
    lane = lax.broadcasted_iota(jnp.int32, (CHUNK, 2 * SSD_P), 1)
    return (lane >= SSD_P) if hh else (lane < SSD_P)


def _gate_norm(y, zz, nw):
    yz = y * (zz * jax.nn.sigmoid(zz))
    outs, rs = [], []
    half = SSD_INNER // SSD_GROUPS
    for g in range(SSD_GROUPS):
        yg = yz[:, g * half:(g + 1) * half]
        r = lax.rsqrt(jnp.mean(yg * yg, axis=-1, keepdims=True) + EPS)
        outs.append(yg * r)
        rs.append(r)
    return yz, jnp.concatenate(outs, axis=1), rs


def _ssd_fwd(xraw, misc, z, cw, cb, dtb, a_exp, d_exp, nw, consts):
    s = xraw.shape[0]
    nb = s // SSD_ROWS
    ncb = SSD_ROWS // CHUNK
    emisc, _, tri, trit = consts

    def body(x_ref, misc_ref, z_ref, cw_ref, cb_ref, dtb_ref, a_ref, d_ref, nw_ref, emisc_ref, tri_ref, trit_ref,
             c_ref, prev_ref, ypre_ref, yssd_ref, tail_s, state_s):
        i = pl.program_id(0)

        @pl.when(i == 0)
        def _():
            tail_s[...] = jnp.zeros(tail_s.shape, F32)
            state_s[...] = jnp.zeros(state_s.shape, F32)

        x = x_ref[...]
        xext = jnp.concatenate([tail_s[...], x], axis=0)
        acc = x * cw_ref[CONV_W - 1:CONV_W, :] + cb_ref[...]
        for j in range(1, CONV_W):
            acc = acc + pltpu.roll(xext, j, 0)[8:, :] * cw_ref[CONV_W - 1 - j:CONV_W - j, :]
        tail_s[...] = x[SSD_ROWS - 8:, :]
        c_ref[...] = acc

        def chunk(ci, carry):
            r0 = pl.multiple_of(ci * CHUNK, CHUNK)
            xa, _, dt, acs, acs_t, alast = _ssd_chunk_common(
                c_ref[pl.ds(r0, CHUNK), :], misc_ref[pl.ds(r0, CHUNK), :], emisc_ref[...], tri_ref[...], trit_ref[...],
                dtb_ref[...], a_ref[...])
            xs = xa[:, :SSD_INNER]
            xdt = xs * dt
            prev = state_s[...]
            prev_ref[ci] = prev
            wgt = (xdt * jnp.exp(alast - acs)).astype(BF16)
            e = jnp.exp(acs)
            ys, new_states = [], []
            for g in range(SSD_GROUPS):
                bm = xa[:, SSD_INNER + g * SSD_N:SSD_INNER + (g + 1) * SSD_N].astype(BF16)
                cm = xa[:, SSD_INNER + SSD_GROUPS * SSD_N + g * SSD_N:SSD_INNER + SSD_GROUPS * SSD_N + (g + 1) * SSD_N].astype(BF16)
                cb_g = _dot_nt(cm, bm)
                gl = slice(g * 256, (g + 1) * 256)
                new_states.append(_dot_tn(bm, wgt[:, gl]))
                yoff = _dot(cm, prev[:, gl].astype(BF16)) * e[:, gl]
                for jj in range(2):
                    pair = 2 * g + jj
                    pl_ = slice(pair * 128, (pair + 1) * 128)
                    xp = xdt[:, pl_]
                    yp = yoff[:, jj * 128:(jj + 1) * 128]
                    for hh in range(2):
                        sc = (cb_g * _decay(acs, acs_t, 2 * pair + hh)).astype(BF16)
                        yp = yp + _dot(sc, jnp.where(_half_mask(hh), xp, 0.0).astype(BF16))
                    ys.append(yp)
            y = jnp.concatenate(ys, axis=1) + d_ref[...] * xs
            state_s[...] = prev * jnp.exp(alast) + jnp.concatenate(new_states, axis=1)
            ypre_ref[pl.ds(r0, CHUNK), :] = y
            _, yn, _ = _gate_norm(y, z_ref[pl.ds(r0, CHUNK), :], None)
            yssd_ref[pl.ds(r0, CHUNK), :] = (yn * nw_ref[...]).astype(BF16)
            return carry

        lax.fori_loop(0, ncb, chunk, 0)

    return pl.pallas_call(
        body, name="ssd_fwd", grid=(nb,),
        in_specs=[_rows(SSD_ROWS, CONV_DIM), _rows(SSD_ROWS, HEAD_PAD), _rows(SSD_ROWS, SSD_INNER),
                  _full((CONV_W, CONV_DIM)), _full((1, CONV_DIM)), _full((1, SSD_INNER)), _full((1, SSD_INNER)),
                  _full((1, SSD_INNER)), _full((1, SSD_INNER)), _full((HEAD_PAD, SSD_INNER)), _full((CHUNK, CHUNK)),
                  _full((CHUNK, CHUNK))],
        out_specs=[_rows(SSD_ROWS, CONV_DIM), pl.BlockSpec((ncb, SSD_N, SSD_INNER), lambda i: (i, 0, 0)),
                   _rows(SSD_ROWS, SSD_INNER), _rows(SSD_ROWS, SSD_INNER)],
        out_shape=[jax.ShapeDtypeStruct((s, CONV_DIM), F32), jax.ShapeDtypeStruct((s // CHUNK, SSD_N, SSD_INNER), F32),
                   jax.ShapeDtypeStruct((s, SSD_INNER), F32), jax.ShapeDtypeStruct((s, SSD_INNER), BF16)],
        scratch_shapes=[pltpu.VMEM((8, CONV_DIM), F32), pltpu.VMEM((SSD_N, SSD_INNER), F32)],
        compiler_params=_params(),
    )(xraw, misc, z, cw, cb, dtb, a_exp, d_exp, nw, emisc, tri, trit)


def _outproj_fwd(oe, yssd, wout, h, nw):
    s = h.shape[0]
    wide = MLA_HEADS * HEAD_PAD

    def body(oe_ref, y_ref, w_ref, h_ref, nw_ref, mixed_ref, h1_ref):
        mixed = _dot(oe_ref[...], w_ref[0:wide, :]) + _dot(y_ref[...], w_ref[wide:, :])
        mixed_ref[...] = mixed
        h1_ref[...] = h_ref[...] + _rms_fwd(mixed, nw_ref[...])

    return pl.pallas_call(
        body, name="outproj_fwd", grid=(s // TM,),
        in_specs=[_rows(TM, wide), _rows(TM, SSD_INNER), _resident((wide + SSD_INNER, D_MODEL)), _rows(TM, D_MODEL),
                  _full((1, D_MODEL))],
        out_specs=[_rows(TM, D_MODEL), _rows(TM, D_MODEL)],
        out_shape=[jax.ShapeDtypeStruct((s, D_MODEL), F32)] * 2,
        compiler_params=_params(),
    )(oe, yssd, wout, h, nw)


def _mlp_fwd(h1, prew, wup, wdown, postw):
    s = h1.shape[0]
    fb = D_FF // N_DEV

    def body(h_ref, prew_ref, up_ref, down_ref, postw_ref, mb_ref, d_ref, h2_ref):
        hh = h_ref[...]
        mb = _rms_fwd(hh, prew_ref[...]).astype(BF16)
        mb_ref[...] = mb
        d = jnp.zeros((TM, D_MODEL), F32)
        for j in range(N_DEV):
            a = _dot(mb, up_ref[j])
            r = jnp.square(jnp.maximum(a, 0.0)).astype(BF16)
            d = d + _dot(r, down_ref[j])
        d_ref[...] = d
        h2_ref[...] = hh + _rms_fwd(d, postw_ref[...])

    return pl.pallas_call(
        body, name="mlp_fwd", grid=(s // TM,),
        in_specs=[_rows(TM, D_MODEL), _full((1, D_MODEL)), _resident((N_DEV, D_MODEL, fb)), _resident((N_DEV, fb, D_MODEL)),
                  _full((1, D_MODEL))],
        out_specs=[_rows(TM, D_MODEL)] * 3,
        out_shape=[jax.ShapeDtypeStruct((s, D_MODEL), BF16), jax.ShapeDtypeStruct((s, D_MODEL), F32),
                   jax.ShapeDtypeStruct((s, D_MODEL), F32)],
        compiler_params=_params(),
    )(h1, prew, wup, wdown, postw)


def _loss_grad(h, target):
    s = h.shape[0]

    def body(h_ref, t_ref, dh_ref, loss_ref):
        diff = h_ref[...] - t_ref[...]
        dh_ref[...] = diff * (1.0 / D_MODEL)
        part = 0.5 * jnp.sum(jnp.mean(diff * diff, axis=-1, keepdims=True), axis=0, keepdims=True)
        _acc_rows(loss_ref, part, pl.program_id(0) == 0)

    return pl.pallas_call(
        body, name="loss_grad", grid=(s // TM,),
        in_specs=[_rows(TM, D_MODEL)] * 2,
        out_specs=[_rows(TM, D_MODEL), _full((1, 1))],
        out_shape=[jax.ShapeDtypeStruct((s, D_MODEL), F32), jax.ShapeDtypeStruct((1, 1), F32)],
    )(h, target)


def _mlp_bwd(dh2, d, h1, mb, prew, wup, wdown, postw):
    s = dh2.shape[0]
    fb = D_FF // N_DEV
    tm = TM // 2

    def body(dh2_ref, d_ref, h1_ref, mb_ref, prew_ref, up_ref, down_ref, postw_ref,
             dh1_ref, da_ref, r_ref, dd_ref, gpost_ref, gpre_ref):
        first = pl.program_id(0) == 0
        dh2 = dh2_ref[...]
        dd, gpost = _rms_bwd(d_ref[...], postw_ref[...], dh2)
        _acc_rows(gpost_ref, gpost, first)
        ddb = dd.astype(BF16)
        dd_ref[...] = ddb
        mb = mb_ref[...]
        dm = jnp.zeros((tm, D_MODEL), F32)
        for j in range(N_DEV):
            a = jnp.maximum(_dot(mb, up_ref[j]), 0.0)
            r_ref[j] = jnp.square(a).astype(BF16)
            da = (_dot_nt(ddb, down_ref[j]) * (2.0 * a)).astype(BF16)
            da_ref[j] = da
            dm = dm + _dot_nt(da, up_ref[j])
        dx, gpre = _rms_bwd(h1_ref[...], prew_ref[...], dm)
        _acc_rows(gpre_ref, gpre, first)
        dh1_ref[...] = dh2 + dx

    stacked = pl.BlockSpec((N_DEV, tm, fb), lambda i: (0, i, 0))
    return pl.pallas_call(
        body, name="mlp_bwd", grid=(s // tm,),
        in_specs=[_rows(tm, D_MODEL)] * 4 + [_full((1, D_MODEL)), _resident((N_DEV, D_MODEL, fb)), _resident((N_DEV, fb, D_MODEL)),
                                              _full((1, D_MODEL))],
        out_specs=[_rows(tm, D_MODEL), stacked, stacked, _rows(tm, D_MODEL), _full((1, D_MODEL)), _full((1, D_MODEL))],
        out_shape=[jax.ShapeDtypeStruct((s, D_MODEL), F32), jax.ShapeDtypeStruct((N_DEV, s, fb), BF16),
                   jax.ShapeDtypeStruct((N_DEV, s, fb), BF16), jax.ShapeDtypeStruct((s, D_MODEL), BF16),
                   jax.ShapeDtypeStruct((1, D_MODEL), F32), jax.ShapeDtypeStruct((1, D_MODEL), F32)],
        compiler_params=_params(),
    )(dh2, d, h1, mb, prew, wup, wdown, postw)


def _matmul_tn(a, b, name, tk=512):
    s, m = a.shape
    n = b.shape[1]
    tn = n if n <= 1024 else (n // 2 if (n // 2) % 128 == 0 else n // 3)
    assert n % tn == 0 and tn % 128 == 0 and s % tk == 0

    def body(a_ref, b_ref, o_ref):
        part = _dot_tn(a_ref[...], b_ref[...])

        @pl.when(pl.program_id(1) == 0)
        def _():
            o_ref[...] = part

        @pl.when(pl.program_id(1) != 0)
        def _():
            o_ref[...] += part

    return pl.pallas_call(
        body, name=name, grid=(n // tn, s // tk),
        in_specs=[pl.BlockSpec((tk, m), lambda j, k: (k, 0)), pl.BlockSpec((tk, tn), lambda j, k: (k, j))],
        out_specs=pl.BlockSpec((m, tn), lambda j, k: (0, j)),
        out_shape=jax.ShapeDtypeStruct((m, n), F32),
        compiler_params=_params(),
    )(a, b)


def _matmul_tn_stacked(a, b, name, a_stacked, tk=512):
    if a_stacked:
        _, s, m = a.shape
        n = b.shape[1]
        in_specs = [pl.BlockSpec((1, tk, m), lambda j, k: (j, k, 0)), pl.BlockSpec((tk, n), lambda j, k: (k, 0))]
    else:
        s, m = a.shape
        n = b.shape[2]
        in_specs = [pl.BlockSpec((tk, m), lambda j, k: (k, 0)), pl.BlockSpec((1, tk, n), lambda j, k: (j, k, 0))]

    def body(a_ref, b_ref, o_ref):
        av = a_ref[0] if a_stacked else a_ref[...]
        bv = b_ref[...] if a_stacked else b_ref[0]
        part = _dot_tn(av, bv)

        @pl.when(pl.program_id(1) == 0)
        def _():
            o_ref[0] = part

        @pl.when(pl.program_id(1) != 0)
        def _():
            o_ref[0] += part

    return pl.pallas_call(
        body, name=name, grid=(N_DEV, s // tk),
        in_specs=in_specs,
        out_specs=pl.BlockSpec((1, m, n), lambda j, k: (j, 0, 0)),
        out_shape=jax.ShapeDtypeStruct((N_DEV, m, n), F32),
        compiler_params=_params(),
    )(a, b)


def _outproj_bwd(dh1, mixed, nw, wout):
    s = dh1.shape[0]
    wide = MLA_HEADS * HEAD_PAD

    def body(dh1_ref, mixed_ref, nw_ref, w_ref, dmix_ref, doe_ref, dy_ref, gnw_ref):
        dmix, gnw = _rms_bwd(mixed_ref[...], nw_ref[...], dh1_ref[...])
        _acc_rows(gnw_ref, gnw, pl.program_id(0) == 0)
        dmb = dmix.astype(BF16)
        dmix_ref[...] = dmb
        doe_ref[...] = _dot_nt(dmb, w_ref[0:wide, :]).astype(BF16)
        dy_ref[...] = _dot_nt(dmb, w_ref[wide:, :])

    return pl.pallas_call(
        body, name="outproj_bwd", grid=(s // TM,),
        in_specs=[_rows(TM, D_MODEL), _rows(TM, D_MODEL), _full((1, D_MODEL)), _resident((wide + SSD_INNER, D_MODEL))],
        out_specs=[_rows(TM, D_MODEL), _rows(TM, wide), _rows(TM, SSD_INNER), _full((1, D_MODEL))],
        out_shape=[jax.ShapeDtypeStruct((s, D_MODEL), BF16), jax.ShapeDtypeStruct((s, wide), BF16),
                   jax.ShapeDtypeStruct((s, SSD_INNER), F32), jax.ShapeDtypeStruct((1, D_MODEL), F32)],
        compiler_params=_params(),
    )(dh1, mixed, nw, wout)


def _attn_bwd(q, k, v, o, do, lse):
    s = q.shape[0]
    nq = s // TQ

    def body(q_ref, k_ref, v_ref, o_ref, do_ref, lse_ref, dq_ref, dk_ref, dv_ref, delta_s):
        kb = pl.program_id(1)

        @pl.when(kb == 0)
        def _():
            dq_ref[...] = jnp.zeros(dq_ref.shape, F32)

            def dl(b, c):
                r0 = pl.multiple_of(b * TQ, TQ)
                prod = o_ref[pl.ds(r0, TQ), :].astype(F32) * do_ref[pl.ds(r0, TQ), :].astype(F32)
                delta_s[pl.ds(r0, TQ), :] = jnp.sum(prod, axis=-1, keepdims=True)
                return c

            lax.fori_loop(0, nq, dl, 0)

        kk = k_ref[...]
        vv = v_ref[...]

        def step(qb, masked, dk, dv):
            r0 = pl.multiple_of(qb * TQ, TQ)
            qq = q_ref[pl.ds(r0, TQ), :]
            dd = do_ref[pl.ds(r0, TQ), :]
            sc = _dot_nt(qq, kk) * ATT_SCALE
            if masked:
                sc = jnp.where(_chunk_mask(), sc, -jnp.inf)
            p = jnp.exp(sc - lse_ref[0, pl.ds(r0, TQ), :])
            dv = dv + _dot_tn(p.astype(BF16), dd)
            dp = _dot_nt(dd, vv)
            ds = (p * (dp - delta_s[pl.ds(r0, TQ), :]) * ATT_SCALE).astype(BF16)
            dk = dk + _dot_tn(ds, qq)
            dq_ref[pl.ds(r0, TQ), :] += _dot(ds, kk)
            return dk, dv

        zero = jnp.zeros((TQ, HEAD_PAD), F32)
        dk, dv = step(kb, True, zero, zero)
        dk, dv = lax.fori_loop(kb + 1, nq, lambda qb, c: step(qb, False, *c), (dk, dv))
        dk_ref[...] = dk
        dv_ref[...] = dv

    whole = pl.BlockSpec((s, HEAD_PAD), lambda h, i: (0, h))
    tile = pl.BlockSpec((TQ, HEAD_PAD), lambda h, i: (i, h))
    wide = MLA_HEADS * HEAD_PAD
    return pl.pallas_call(
        body, name="attn_bwd", grid=(MLA_HEADS, nq),
        in_specs=[whole, tile, tile, whole, whole, pl.BlockSpec((1, s, 1), lambda h, i: (h, 0, 0))],
        out_specs=[whole, tile, tile],
        out_shape=[jax.ShapeDtypeStruct((s, wide), F32)] * 3,
        scratch_shapes=[pltpu.VMEM((s, 1), F32)],
        compiler_params=_params(),
    )(q, k, v, o, do, lse)


def _ssd_bwd(dy, ypre, z, c, xraw, misc, prev, cw, dtb, a_exp, d_exp, nw, consts):
    s = dy.shape[0]
    nb = s // SSD_ROWS
    ncb = SSD_ROWS // CHUNK
    emisc, emisc_t, tri, trit = consts

    def body(dy_ref, ypre_ref, z_ref, c_ref, x_ref, xprev_ref, misc_ref, prev_ref, cw_ref, dtb_ref, a_ref, d_ref, nw_ref,
             emisc_ref, emisct_ref, tri_ref, trit_ref,
             dz_ref, dx_ref, dmisc_ref, gnw_ref, gd_ref, galog_ref, gdtb_ref, gcw_ref, gcb_ref,
             dst_s, dc_s, head_s):
        i = pl.program_id(0)
        first = i == 0

        @pl.when(first)
        def _():
            dst_s[...] = jnp.zeros(dst_s.shape, F32)
            head_s[...] = jnp.zeros(head_s.shape, F32)
            gnw_ref[...] = jnp.zeros(gnw_ref.shape, F32)
            gd_ref[...] = jnp.zeros(gd_ref.shape, F32)
            galog_ref[...] = jnp.zeros(galog_ref.shape, F32)
            gdtb_ref[...] = jnp.zeros(gdtb_ref.shape, F32)

        a_exp_v = a_ref[...]
        a8 = _dot(a_exp_v, emisct_ref[...], HI) * (1.0 / SSD_P)

        def chunk(cr, carry):
            ci = ncb - 1 - cr
            r0 = pl.multiple_of(ci * CHUNK, CHUNK)
            cc = c_ref[pl.ds(r0, CHUNK), :]
            mm = misc_ref[pl.ds(r0, CHUNK), :]
            xa, dtr, dt, acs, acs_t, alast = _ssd_chunk_common(cc, mm, emisc_ref[...], tri_ref[...], trit_ref[...],
                                                              dtb_ref[...], a_exp_v)
            xs = xa[:, :SSD_INNER]
            xdt = xs * dt
            y = ypre_ref[pl.ds(r0, CHUNK), :]
            zz = z_ref[pl.ds(r0, CHUNK), :]
            yz, yn, rs = _gate_norm(y, zz, None)
            dyo = dy_ref[pl.ds(r0, CHUNK), :]
            gnw_ref[...] += jnp.sum(dyo * yn, axis=0, keepdims=True)
            dyn = dyo * nw_ref[...]
            half = SSD_INNER // SSD_GROUPS
            dyz_parts = []
            for g in range(SSD_GROUPS):
                gl = slice(g * half, (g + 1) * half)
                dyz_parts.append(rs[g] * (dyn[:, gl] - yn[:, gl] * jnp.mean(dyn[:, gl] * yn[:, gl], axis=-1, keepdims=True)))
            dyz = jnp.concatenate(dyz_parts, axis=1)
            sg = jax.nn.sigmoid(zz)
            dz_ref[pl.ds(r0, CHUNK), :] = dyz * y * (sg * (1.0 + zz * (1.0 - sg)))
            dyp = dyz * (zz * sg)
            dypb = dyp.astype(BF16)
            gd_ref[...] += jnp.sum(dyp * xs, axis=0, keepdims=True)
            prev = prev_ref[ci]
            dst = dst_s[...]
            cd = jnp.exp(alast)
            e = jnp.exp(acs)
            dsx = jnp.exp(alast - acs)
            wgt = (xdt * dsx).astype(BF16)
            dze = (dyp * e).astype(BF16)
            glast = jnp.sum(dst * prev, axis=0, keepdims=True) * cd
            dprev_parts, dxdt_parts, dxdt_state_parts, dbm, dcm, yoff_parts = [], [], [], [], [], []
            lane8 = lax.broadcasted_iota(jnp.int32, (CHUNK, HEAD_PAD), 1)
            diag8 = jnp.zeros((CHUNK, HEAD_PAD), F32)
            for g in range(SSD_GROUPS):
                gl = slice(g * 256, (g + 1) * 256)
                bm = xa[:, SSD_INNER + g * SSD_N:SSD_INNER + (g + 1) * SSD_N].astype(BF16)
                cm = xa[:, SSD_INNER + SSD_GROUPS * SSD_N + g * SSD_N:SSD_INNER + SSD_GROUPS * SSD_N + (g + 1) * SSD_N].astype(BF16)
                prev_g = prev[:, gl].astype(BF16)
                dst_g = dst[:, gl].astype(BF16)
                dcm_g = _dot_nt(dze[:, gl], prev_g)
                dprev_parts.append(_dot_tn(cm, dze[:, gl]))
                dxs_state = _dot(bm, dst_g) * dsx[:, gl]
                dbm_g = _dot_nt(wgt[:, gl], dst_g)
                cb_g = _dot_nt(cm, bm)
                dcb = jnp.zeros((CHUNK, CHUNK), F32)
                diag_parts = []
                for jj in range(2):
                    pair = 2 * g + jj
                    pl_ = slice(pair * 128, (pair + 1) * 128)
                    xp = xdt[:, pl_]
                    dyp_p = dypb[:, pl_]
                    dxp = jnp.zeros((CHUNK, 128), F32)
                    for hh in range(2):
                        hd = 2 * pair + hh
                        dec = _decay(acs, acs_t, hd)
                        xm = jnp.where(_half_mask(hh), xp, 0.0).astype(BF16)
                        dsc = _dot_nt(dyp_p, xm) * dec
                        dcb = dcb + dsc
                        sc = (cb_g * dec).astype(BF16)
                        dxp = dxp + jnp.where(_half_mask(hh), _dot_tn(sc, dyp_p), 0.0)
                        dm = dsc * cb_g
                        diag8 = diag8 + jnp.where(lane8 == MISC_DT + hd, jnp.sum(dm - dm.T, axis=1, keepdims=True), 0.0)
                    diag_parts.append(dxp)
                dcbb = dcb.astype(BF16)
                dcm.append(dcm_g + _dot(dcbb, bm))
                dbm.append(dbm_g + _dot_tn(dcbb, cm))
                dxdt_state_parts.append(dxs_state)
                dxdt_parts.append(jnp.concatenate(diag_parts, axis=1) + dxs_state)
                yoff_parts.append(_dot(cm, prev_g) * e[:, gl])
            dxdt = jnp.concatenate(dxdt_parts, axis=1)
            dxdt_state = jnp.concatenate(dxdt_state_parts, axis=1)
            dst_s[...] = dst * cd + jnp.concatenate(dprev_parts, axis=1)
            dacs = dyp * jnp.concatenate(yoff_parts, axis=1) - xdt * dxdt_state
            last = jnp.sum(xdt * dxdt_state, axis=0, keepdims=True) + glast
            row = lax.broadcasted_iota(jnp.int32, (CHUNK, SSD_INNER), 0)
            dacs = dacs + jnp.where(row == CHUNK - 1, last, 0.0)
            dacs8 = _dot(dacs, emisct_ref[...], HI) + diag8
            da8 = _dot(trit_ref[...], dacs8, HI)
            ddt8 = da8 * a8 + _dot(dxdt * xs, emisct_ref[...], HI)
            dtr8 = mm + _dot(dtb_ref[...], emisct_ref[...], HI) * (1.0 / SSD_P)
            dt8 = jax.nn.softplus(dtr8)
            lane = lax.broadcasted_iota(jnp.int32, (CHUNK, HEAD_PAD), 1)
            on_dt = jnp.logical_and(lane >= MISC_DT, lane < MISC_DT + SSD_HEADS)
            ddtr8 = jnp.where(on_dt, ddt8 * jax.nn.sigmoid(dtr8), 0.0)
            dmisc_ref[pl.ds(r0, CHUNK), :] = ddtr8
            gdtb_ref[...] += jnp.sum(ddtr8, axis=0, keepdims=True)
            galog_ref[...] += jnp.sum(jnp.where(on_dt, da8 * dt8, 0.0), axis=0, keepdims=True) * a8
            dxs = d_ref[...] * dyp + dxdt * dt
            dxa = jnp.concatenate([dxs] + dbm + dcm, axis=1)
            sc_ = jax.nn.sigmoid(cc)
            dc_s[pl.ds(r0, CHUNK), :] = dxa * (sc_ * (1.0 + cc * (1.0 - sc_)))
            return carry

        lax.fori_loop(0, ncb, chunk, 0)

        dc = dc_s[...]
        dcext = jnp.concatenate([dc, head_s[...]], axis=0)
        dx = dc * cw_ref[CONV_W - 1:CONV_W, :]
        for j in range(1, CONV_W):
            dx = dx + pltpu.roll(dcext, SSD_ROWS + 8 - j, 0)[:SSD_ROWS, :] * cw_ref[CONV_W - 1 - j:CONV_W - j, :]
        dx_ref[...] = dx
        head_s[...] = dc[:8, :]
        xprev = jnp.where(i == nb - 1, 0.0, xprev_ref[...])
        xext = jnp.concatenate([xprev, x_ref[...]], axis=0)
        rows = [jnp.sum(dc * pltpu.roll(xext, CONV_W - 1 - kk, 0)[8:, :], axis=0, keepdims=True) for kk in range(CONV_W)]
        gcw = jnp.concatenate(rows, axis=0)

        @pl.when(first)
        def _():
            gcw_ref[...] = gcw
            gcb_ref[...] = jnp.sum(dc, axis=0, keepdims=True)

        @pl.when(jnp.logical_not(first))
        def _():
            gcw_ref[...] += gcw
            gcb_ref[...] += jnp.sum(dc, axis=0, keepdims=True)

    def rev(width):
        return pl.BlockSpec((SSD_ROWS, width), lambda i: (nb - 1 - i, 0))

    per8 = SSD_ROWS // 8
    return pl.pallas_call(
        body, name="ssd_bwd", grid=(nb,),
        in_specs=[rev(SSD_INNER), rev(SSD_INNER), rev(SSD_INNER), rev(CONV_DIM), rev(CONV_DIM),
                  pl.BlockSpec((8, CONV_DIM), lambda i: (jnp.maximum((nb - 1 - i) * per8 - 1, 0), 0)),
                  rev(HEAD_PAD), pl.BlockSpec((ncb, SSD_N, SSD_INNER), lambda i: (nb - 1 - i, 0, 0)),
                  _full((CONV_W, CONV_DIM)), _full((1, SSD_INNER)), _full((1, SSD_INNER)), _full((1, SSD_INNER)),
                  _full((1, SSD_INNER)), _full((HEAD_PAD, SSD_INNER)), _full((SSD_INNER, HEAD_PAD)), _full((CHUNK, CHUNK)),
                  _full((CHUNK, CHUNK))],
        out_specs=[rev(SSD_INNER), rev(CONV_DIM), rev(HEAD_PAD), _full((1, SSD_INNER)), _full((1, SSD_INNER)),
                   _full((1, HEAD_PAD)), _full((1, HEAD_PAD)), _full((CONV_W, CONV_DIM)), _full((1, CONV_DIM))],
        out_shape=[jax.ShapeDtypeStruct((s, SSD_INNER), F32), jax.ShapeDtypeStruct((s, CONV_DIM), F32),
                   jax.ShapeDtypeStruct((s, HEAD_PAD), F32), jax.ShapeDtypeStruct((1, SSD_INNER), F32),
                   jax.ShapeDtypeStruct((1, SSD_INNER), F32), jax.ShapeDtypeStruct((1, HEAD_PAD), F32),
                   jax.ShapeDtypeStruct((1, HEAD_PAD), F32), jax.ShapeDtypeStruct((CONV_W, CONV_DIM), F32),
                   jax.ShapeDtypeStruct((1, CONV_DIM), F32)],
        scratch_shapes=[pltpu.VMEM((SSD_N, SSD_INNER), F32), pltpu.VMEM((SSD_ROWS, CONV_DIM), F32), pltpu.VMEM((8, CONV_DIM), F32)],
        compiler_params=_params(),
    )(dy, ypre, z, c, xraw, xraw, misc, prev, cw, dtb, a_exp, d_exp, nw, emisc, emisc_t, tri, trit)


def _qkv_bwd(dq, dk, dv, cq, ckv, qnw, kvnw, wuq, wkv, cosf, sinf):
    s = dq.shape[0]
    wide = MLA_HEADS * HEAD_PAD

    def body(dq_ref, dk_ref, dv_ref, cq_ref, ckv_ref, qnw_ref, kvnw_ref, wuq_ref, wkv_ref, cos_ref, sin_ref,
             dqb_ref, dkvb_ref, dcq_ref, dckv_ref, dmisc_ref, gq_ref, gkv_ref):
        first = pl.program_id(0) == 0
        cosf, sinf = cos_ref[...], sin_ref[...]
        dkr = jnp.zeros((TM, HEAD_PAD), F32)
        for hd in range(MLA_HEADS):
            cols = slice(hd * HEAD_PAD, (hd + 1) * HEAD_PAD)
            dqb_ref[:, cols] = _rope(dq_ref[:, cols], cosf, sinf, -1.0).astype(BF16)
            dkh = dk_ref[:, cols]
            dkvb_ref[:, cols] = dkh.astype(BF16)
            dkr = dkr + dkh
        dkvb_ref[:, wide:] = dv_ref[...].astype(BF16)
        lane = lax.broadcasted_iota(jnp.int32, dkr.shape, 1)
        in_rope = jnp.logical_and(lane >= MISC_ROPE, lane < MISC_ROPE + QK_ROPE)
        dmisc_ref[...] = jnp.where(in_rope, _rope(jnp.where(in_rope, dkr, 0.0), cosf, sinf, -1.0), 0.0)
        dcq, gq = _rms_bwd(cq_ref[...], qnw_ref[...], _dot_nt(dqb_ref[...], wuq_ref[...]))
        dcq_ref[...] = dcq
        _acc_rows(gq_ref, gq, first)
        dckv, gkv = _rms_bwd(ckv_ref[...], kvnw_ref[...], _dot_nt(dkvb_ref[...], wkv_ref[...]))
        dckv_ref[...] = dckv
        _acc_rows(gkv_ref, gkv, first)

    return pl.pallas_call(
        body, name="qkv_bwd", grid=(s // TM,),
        in_specs=[_rows(TM, wide)] * 3 + [_rows(TM, Q_RANK), _rows(TM, KV_RANK), _full((1, Q_RANK)), _full((1, KV_RANK)),
                                          _resident((Q_RANK, wide)), _resident((KV_RANK, 2 * wide)), _rows(TM, HEAD_PAD), _rows(TM, HEAD_PAD)],
        out_specs=[_rows(TM, wide), _rows(TM, 2 * wide), _rows(TM, Q_RANK), _rows(TM, KV_RANK), _rows(TM, HEAD_PAD),
                   _full((1, Q_RANK)), _full((1, KV_RANK))],
        out_shape=[jax.ShapeDtypeStruct((s, wide), BF16), jax.ShapeDtypeStruct((s, 2 * wide), BF16),
                   jax.ShapeDtypeStruct((s, Q_RANK), F32), jax.ShapeDtypeStruct((s, KV_RANK), F32),
                   jax.ShapeDtypeStruct((s, HEAD_PAD), F32), jax.ShapeDtypeStruct((1, Q_RANK), F32),
                   jax.ShapeDtypeStruct((1, KV_RANK), F32)],
        compiler_params=_params(),
    )(dq, dk, dv, cq, ckv, qnw, kvnw, wuq, wkv, cosf, sinf)


def _inproj_bwd(dcq, dckv, dmisc_rope, dmisc_dt, dz, dxbc, h, dh1, nw, win):
    s = h.shape[0]

    def body(dcq_ref, dckv_ref, dma_ref, dmb_ref, dz_ref, dxbc_ref, h_ref, dh1_ref, nw_ref, w_ref, dproj_ref, dh0_ref, gnw_ref):
        dproj_ref[:, 0:768] = dcq_ref[...].astype(BF16)
        dproj_ref[:, 768:1024] = dckv_ref[...].astype(BF16)
        dproj_ref[:, 1024:1152] = (dma_ref[...] + dmb_ref[...]).astype(BF16)
        dproj_ref[:, 1152:1664] = dz_ref[...].astype(BF16)
        dproj_ref[:, 1664:2688] = dxbc_ref[...].astype(BF16)
        du = _dot_nt(dproj_ref[...], w_ref[...])
        dx, gnw = _rms_bwd(h_ref[...], nw_ref[...], du)
        _acc_rows(gnw_ref, gnw, pl.program_id(0) == 0)
        dh0_ref[...] = dh1_ref[...] + dx

    return pl.pallas_call(
        body, name="inproj_bwd", grid=(s // TM,),
        in_specs=[_rows(TM, Q_RANK), _rows(TM, KV_RANK), _rows(TM, HEAD_PAD), _rows(TM, HEAD_PAD), _rows(TM, SSD_INNER),
                  _rows(TM, CONV_DIM), _rows(TM, D_MODEL), _rows(TM, D_MODEL), _full((1, D_MODEL)), _resident((D_MODEL, IN_PAD))],
        out_specs=[_rows(TM, IN_PAD), _rows(TM, D_MODEL), _full((1, D_MODEL))],
        out_shape=[jax.ShapeDtypeStruct((s, IN_PAD), BF16), jax.ShapeDtypeStruct((s, D_MODEL), F32),
                   jax.ShapeDtypeStruct((1, D_MODEL), F32)],
        compiler_params=_params(),
    )(dcq, dckv, dmisc_rope, dmisc_dt, dz, dxbc, h, dh1, nw, win)


def _row_tile(rows, cols):
    cap = max(8, (1 << 18) // max(cols, 128))
    best = None
    for t in range(8, rows + 1, 8):
        if rows % t == 0 and t <= cap:
            best = t
    return best if best is not None else rows


def _adamw(w, g, m, v, name):
    rows, cols = w.shape
    tr = _row_tile(rows, cols)

    def body(w_ref, g_ref, m_ref, v_ref, d_ref, m2_ref, v2_ref):
        gg = g_ref[...]
        m2 = ADAM_B1 * m_ref[...] + (1.0 - ADAM_B1) * gg
        v2 = ADAM_B2 * v_ref[...] + (1.0 - ADAM_B2) * jnp.square(gg)
        m_hat = m2 / (1.0 - ADAM_B1 ** ADAM_STEP)
        v_hat = v2 / (1.0 - ADAM_B2 ** ADAM_STEP)
        d_ref[...] = -ADAM_LR * (m_hat / (jnp.sqrt(v_hat) + ADAM_EPS) + ADAM_WD * w_ref[...])
        m2_ref[...] = m2
        v2_ref[...] = v2

    spec = pl.BlockSpec((tr, cols), lambda i: (i, 0))
    return pl.pallas_call(
        body, name=name, grid=(rows // tr,),
        in_specs=[spec] * 4, out_specs=[spec] * 3,
        out_shape=[jax.ShapeDtypeStruct((rows, cols), F32)] * 3,
    )(w, g, m, v)


_MESH = pl.DeviceIdType.MESH
_ANY = pl.BlockSpec(memory_space=pl.ANY)


def _my_place():
    return lax.axis_index("x"), lax.axis_index("y"), lax.axis_index("c")


def _flip(place, k):
    x, y, c = place
    return (1 - x if k & 4 else x, 1 - y if k & 2 else y, 1 - c if k & 1 else c)


def _block_id(place):
    return 4 * place[0] + 2 * place[1] + place[2]


def _all_gather(shard):
    rows, lanes = shard.shape

    def body(x_ref, out_ref, send_sems, recv_sems, local_sem):
        me = _my_place()
        x, y, c = me
        sibling = (x, y, 1 - c)
        chips = [(1 - x, y), (x, 1 - y), (1 - x, 1 - y)]

        def block(place):
            return out_ref.at[_block_id(place)]

        def copy(k, place, to, src=None):
            return pltpu.make_async_remote_copy(
                src_ref=block(place) if src is None else src, dst_ref=block(place),
                send_sem=send_sems.at[k], recv_sem=recv_sems.at[k], device_id=to, device_id_type=_MESH)

        mine = pltpu.make_async_copy(x_ref, block(me), local_sem)
        mine.start()
        first = [copy(0, me, sibling, src=x_ref)]
        first += [copy(1 + j, me, (*chip, c), src=x_ref) for j, chip in enumerate(chips)]
        for cp in first:
            cp.start()
        passed = [copy(4 + j, (*chip, c), sibling) for j, chip in enumerate(chips)]
        for j, chip in enumerate(chips):
            copy(1 + j, (*chip, c), me).wait_recv()
            passed[j].start()
        copy(0, sibling, me).wait_recv()
        for j, chip in enumerate(chips):
            copy(4 + j, (*chip, 1 - c), me).wait_recv()
        for cp in first + passed:
            cp.wait_send()
        mine.wait()

    return pl.pallas_call(
        body, name="weight_all_gather",
        out_shape=jax.ShapeDtypeStruct((N_DEV, rows, lanes), shard.dtype),
        in_specs=[_ANY], out_specs=_ANY,
        scratch_shapes=[pltpu.SemaphoreType.DMA((7,)), pltpu.SemaphoreType.DMA((7,)), pltpu.SemaphoreType.DMA],
    )(shard)


def _exchange(blocks):
    def body(x_ref, out_ref, send_sems, recv_sems, local_sem):
        me = _my_place()
        my = _block_id(me)
        own = pltpu.make_async_copy(x_ref.at[my], out_ref.at[my], local_sem)
        own.start()
        copies = []
        for k in range(1, N_DEV):
            peer = _flip(me, k)
            cp = pltpu.make_async_remote_copy(
                src_ref=x_ref.at[_block_id(peer)], dst_ref=out_ref.at[my],
                send_sem=send_sems.at[k - 1], recv_sem=recv_sems.at[k - 1], device_id=peer, device_id_type=_MESH)
            cp.start()
            copies.append(cp)
        for cp in copies:
            cp.wait()
        own.wait()

    return pl.pallas_call(
        body, name="grad_exchange",
        out_shape=jax.ShapeDtypeStruct(blocks.shape, blocks.dtype),
        in_specs=[_ANY], out_specs=_ANY,
        scratch_shapes=[pltpu.SemaphoreType.DMA((7,)), pltpu.SemaphoreType.DMA((7,)), pltpu.SemaphoreType.DMA],
    )(blocks)


def _sum_slots(slots):
    _, rows, lanes = slots.shape
    tr = _row_tile(rows, 8 * lanes)

    def body(x_ref, o_ref):
        acc = x_ref[0].astype(F32)
        for i in range(1, N_DEV):
            acc = acc + x_ref[i].astype(F32)
        o_ref[...] = acc

    return pl.pallas_call(
        body, name="grad_sum", grid=(rows // tr,),
        in_specs=[pl.BlockSpec((N_DEV, tr, lanes), lambda i: (0, i, 0))],
        out_specs=pl.BlockSpec((tr, lanes), lambda i: (i, 0)),
        out_shape=jax.ShapeDtypeStruct((rows, lanes), F32),
    )(slots)


def _all_reduce_small(part):
    rows, lanes = part.shape
    vmem = pl.BlockSpec(memory_space=pltpu.VMEM)

    def body(x_ref, gath_ref, sum_ref, send_sems, recv_sems):
        me = _my_place()
        my = _block_id(me)
        gath_ref[my] = x_ref[...]
        copies = []
        for k in range(1, N_DEV):
            cp = pltpu.make_async_remote_copy(
                src_ref=x_ref, dst_ref=gath_ref.at[my], send_sem=send_sems.at[k - 1], recv_sem=recv_sems.at[k - 1],
                device_id=_flip(me, k), device_id_type=_MESH)
            cp.start()
            copies.append(cp)
        for cp in copies:
            cp.wait()
        acc = gath_ref[0]
        for i in range(1, N_DEV):
            acc = acc + gath_ref[i]
        sum_ref[...] = acc

    return pl.pallas_call(
        body, name="small_grad_all_reduce",
        out_shape=[jax.ShapeDtypeStruct((N_DEV, rows, lanes), F32), jax.ShapeDtypeStruct((rows, lanes), F32)],
        in_specs=[vmem], out_specs=[vmem, vmem],
        scratch_shapes=[pltpu.SemaphoreType.DMA((7,)), pltpu.SemaphoreType.DMA((7,))],
    )(part)[1]


_SHARDED = (("w_in", (D_MODEL, IN_PROJ // N_DEV)), ("w_uq", (Q_RANK // N_DEV, Q_RANK)), ("w_ukv", (KV_RANK, HEAD_PAD)),
            ("conv_w", (CONV_W, CONV_DIM // N_DEV)), ("w_out", (D_MODEL // N_DEV, D_MODEL)),
            ("w_up", (D_MODEL, D_FF // N_DEV)), ("w_down", (D_FF // N_DEV, D_MODEL)))
_SMALL = (("pre_mix_norm", D_MODEL), ("q_norm", Q_RANK), ("kv_norm", KV_RANK), ("conv_b", CONV_DIM), ("dt_bias", SSD_HEADS),
          ("a_log", SSD_HEADS), ("d_skip", SSD_HEADS), ("ssd_norm", SSD_INNER), ("post_mix_norm", D_MODEL),
          ("pre_mlp_norm", D_MODEL), ("post_mlp_norm", D_MODEL))
_WEIGHT_ORDER = ("pre_mix_norm", "w_in", "q_norm", "w_uq", "kv_norm", "w_ukv", "conv_w", "conv_b", "dt_bias", "a_log", "d_skip",
                 "ssd_norm", "w_out", "post_mix_norm", "pre_mlp_norm", "w_up", "w_down", "post_mlp_norm")
_FLAT_TILE = 16 * 128


def _pad_flat(flat):
    n = flat.shape[-1]
    padded = -(-n // _FLAT_TILE) * _FLAT_TILE
    return jnp.pad(flat, [(0, 0)] * (flat.ndim - 1) + [(0, padded - n)])


def _pack_shards(w):
    pieces = []
    for l in range(DEPTH):
        for name, _ in _SHARDED:
            a = w[name][l]
            a = lax.bitcast_convert_type(a, BF16) if name == "conv_w" else a.astype(BF16)
            pieces.append(a.reshape(-1))
    return _pad_flat(jnp.concatenate(pieces)).reshape(-1, 128)


def _unpack_gathered(g):
    flat = g.reshape(N_DEV, -1)
    out, off = [], 0
    for l in range(DEPTH):
        d = {}
        for name, shape in _SHARDED:
            n = math.prod(shape) * (2 if name == "conv_w" else 1)
            seg = flat[:, off:off + n]
            off += n
            if name == "conv_w":
                d[name] = lax.bitcast_convert_type(seg.reshape(N_DEV, *shape, 2), F32)
            else:
                d[name] = seg.reshape(N_DEV, *shape)
        out.append(d)
    return out


def _cols(stacked):
    return jnp.transpose(stacked, (1, 0, 2)).reshape(stacked.shape[1], -1)


def _expand_pairs(a, axis_len):
    eye = jnp.eye(2, dtype=a.dtype).reshape(1, 2, 2, 1, 1)
    return a[:, :, None] * eye


def _kernel_weights(sh):
    w_in = _cols(sh["w_in"])
    zeros = lambda n: jnp.zeros((D_MODEL, n), BF16)
    s1, s2, s3, s4, s5 = 768, 1024, 1056, 1568, 2592
    win = jnp.concatenate([w_in[:, :s2], zeros(MISC_ROPE), w_in[:, s2:s3], w_in[:, s5:], zeros(HEAD_PAD - MISC_DT - SSD_HEADS),
                           w_in[:, s3:s5]], axis=1)
    w_uq = sh["w_uq"].reshape(Q_RANK, MLA_HEADS, QK_NOPE + QK_ROPE)
    wuq = jnp.pad(w_uq, ((0, 0), (0, 0), (0, HEAD_PAD - QK_NOPE - QK_ROPE))).reshape(Q_RANK, -1)
    w_ukv = _cols(sh["w_ukv"]).reshape(KV_RANK, MLA_HEADS, QK_NOPE + V_DIM)
    wkn = jnp.pad(w_ukv[..., :QK_NOPE], ((0, 0), (0, 0), (0, HEAD_PAD - QK_NOPE))).reshape(KV_RANK, -1)
    wv = w_ukv[..., QK_NOPE:].reshape(KV_RANK, 4, 2, 1, V_DIM) * jnp.eye(2, dtype=BF16).reshape(1, 1, 2, 2, 1)
    wkv = jnp.concatenate([wkn, wv.reshape(KV_RANK, -1)], axis=1)
    w_out = sh["w_out"].reshape(D_MODEL, D_MODEL)
    watt = w_out[:SSD_INNER].reshape(4, 2, 1, V_DIM, D_MODEL) * jnp.eye(2, dtype=BF16).reshape(1, 2, 2, 1, 1)
    wout = jnp.concatenate([watt.reshape(MLA_HEADS * HEAD_PAD, D_MODEL), w_out[SSD_INNER:]], axis=0)
    return dict(win=win, wuq=wuq, wkv=wkv, wout=wout, wup=sh["w_up"], wdown=sh["w_down"], conv_w=_cols(sh["conv_w"]))


def _shard_grads(g):
    dwin = g["win"]
    s1, s2 = 768, 1024
    m0 = s2
    w_in = jnp.concatenate([dwin[:, :s2], dwin[:, m0 + MISC_ROPE:m0 + MISC_ROPE + QK_ROPE], dwin[:, 1152:2688],
                            dwin[:, m0 + MISC_DT:m0 + MISC_DT + SSD_HEADS]], axis=1)
    out = {"w_in": jnp.transpose(w_in.reshape(D_MODEL, N_DEV, -1), (1, 0, 2))}
    w_uq = g["wuq"].reshape(Q_RANK, MLA_HEADS, HEAD_PAD)[..., :QK_NOPE + QK_ROPE].reshape(Q_RANK, Q_RANK)
    out["w_uq"] = w_uq.reshape(N_DEV, Q_RANK // N_DEV, Q_RANK)
    wide = MLA_HEADS * HEAD_PAD
    kn = g["wkv"][:, :wide].reshape(KV_RANK, MLA_HEADS, HEAD_PAD)[..., :QK_NOPE]
    ve = g["wkv"][:, wide:].reshape(KV_RANK, 4, 2, 2, V_DIM)
    vv = jnp.stack([ve[:, :, 0, 0], ve[:, :, 1, 1]], axis=2).reshape(KV_RANK, MLA_HEADS, V_DIM)
    out["w_ukv"] = jnp.transpose(jnp.concatenate([kn, vv], axis=-1), (1, 0, 2))
    out["conv_w"] = jnp.transpose(g["conv_w"].reshape(CONV_W, N_DEV, -1), (1, 0, 2))
    ae = g["wout_att"].reshape(4, 2, 2, V_DIM, D_MODEL)
    att = jnp.stack([ae[:, 0, 0], ae[:, 1, 1]], axis=1).reshape(SSD_INNER, D_MODEL)
    out["w_out"] = jnp.concatenate([att, g["wout_ssd"]], axis=0).reshape(N_DEV, D_MODEL // N_DEV, D_MODEL)
    out["w_up"] = g["wup"]
    out["w_down"] = g["wdown"]
    return out


def _pack_grad_blocks(per_layer):
    pieces = [per_layer[l][name].reshape(N_DEV, -1) for l in range(DEPTH) for name, _ in _SHARDED]
    flat = _pad_flat(jnp.concatenate(pieces, axis=1).astype(BF16))
    return flat.reshape(N_DEV, -1, 128)


def _unpack_grad_shard(flat2d):
    flat = flat2d.reshape(-1)
    out, off = {name: [] for name, _ in _SHARDED}, 0
    for l in range(DEPTH):
        for name, shape in _SHARDED:
            n = math.prod(shape)
            out[name].append(flat[off:off + n].reshape(shape))
            off += n
    return {name: jnp.stack(v) for name, v in out.items()}


def _small_rows(n):
    return -(-n // 128)


def _pack_small(vals):
    rows = []
    for l in range(DEPTH):
        for name, n in _SMALL:
            r = _small_rows(n)
            rows.append(jnp.pad(vals[name][l].reshape(-1), (0, r * 128 - n)).reshape(r, 128))
    total = sum(r.shape[0] for r in rows)
    rows.append(jnp.zeros((-total % 8, 128), F32))
    return jnp.concatenate(rows, axis=0)


def _unpack_small(packed):
    out, off = {name: [] for name, _ in _SMALL}, 0
    for l in range(DEPTH):
        for name, n in _SMALL:
            r = _small_rows(n)
            out[name].append(packed[off:off + r].reshape(-1)[:n])
            off += r
    return {name: jnp.stack(v) for name, v in out.items()}


def _lane_rows(vec8):
    return jnp.repeat(vec8, SSD_P).reshape(1, SSD_INNER)


def _layer_fwd(h, kw, sm, l, cosf, sinf, consts):
    row = lambda name: sm[name][l].reshape(1, -1)
    t = {}
    t["h0"] = h
    t["ub"], t["cq"], t["ckv"], t["misc"], t["z"], t["xraw"] = _inproj_fwd(h, row("pre_mix_norm"), kw["win"])
    t["cqn"], t["ckvn"], t["q"], t["k"], t["v"] = _qkv_fwd(t["cq"], t["ckv"], t["misc"], row("q_norm"), row("kv_norm"),
                                                         kw["wuq"], kw["wkv"], cosf, sinf)
    t["oe"], t["lse"] = _attn_fwd(t["q"], t["k"], t["v"])
    t["dtb"] = _lane_rows(sm["dt_bias"][l])
    t["a_exp"] = _lane_rows(-jnp.exp(sm["a_log"][l]))
    t["d_exp"] = _lane_rows(sm["d_skip"][l])
    t["c"], t["prev"], t["ypre"], t["yssd"] = _ssd_fwd(t["xraw"], t["misc"], t["z"], kw["conv_w"], row("conv_b"), t["dtb"],
                                                     t["a_exp"], t["d_exp"], row("ssd_norm"), consts)
    t["mixed"], t["h1"] = _outproj_fwd(t["oe"], t["yssd"], kw["wout"], h, row("post_mix_norm"))
    t["mb"], t["d"], h2 = _mlp_fwd(t["h1"], row("pre_mlp_norm"), kw["wup"], kw["wdown"], row("post_mlp_norm"))
    return h2, t


def _layer_bwd(dh2, t, kw, sm, l, cosf, sinf, consts):
    row = lambda name: sm[name][l].reshape(1, -1)
    g, gs = {}, {}
    dh1, dab, rb, ddb, gs["post_mlp_norm"], gs["pre_mlp_norm"] = _mlp_bwd(
        dh2, t["d"], t["h1"], t["mb"], row("pre_mlp_norm"), kw["wup"], kw["wdown"], row("post_mlp_norm"))
    g["wup"] = _matmul_tn_stacked(t["mb"], dab, f"dw_up_{l}", a_stacked=False)
    g["wdown"] = _matmul_tn_stacked(rb, ddb, f"dw_down_{l}", a_stacked=True)
    dmixb, doe, dyssd, gs["post_mix_norm"] = _outproj_bwd(dh1, t["mixed"], row("post_mix_norm"), kw["wout"])
    g["wout_att"] = _matmul_tn(t["oe"], dmixb, f"dw_out_att_{l}")
    g["wout_ssd"] = _matmul_tn(t["yssd"], dmixb, f"dw_out_ssd_{l}")
    dz, dxraw, dmisc_dt, gs["ssd_norm"], gd, galog, gdtb, g["conv_w"], gs["conv_b"] = _ssd_bwd(
        dyssd, t["ypre"], t["z"], t["c"], t["xraw"], t["misc"], t["prev"], kw["conv_w"], t["dtb"], t["a_exp"], t["d_exp"],
        row("ssd_norm"), consts)
    gs["d_skip"] = jnp.sum(gd.reshape(SSD_HEADS, SSD_P), axis=1)
    gs["a_log"] = galog[0, MISC_DT:MISC_DT + SSD_HEADS]
    gs["dt_bias"] = gdtb[0, MISC_DT:MISC_DT + SSD_HEADS]
    dq, dk, dv = _attn_bwd(t["q"], t["k"], t["v"], t["oe"], doe, t["lse"])
    dqb, dkvb, dcq, dckv, dmisc_rope, gs["q_norm"], gs["kv_norm"] = _qkv_bwd(
        dq, dk, dv, t["cq"], t["ckv"], row("q_norm"), row("kv_norm"), kw["wuq"], kw["wkv"], cosf, sinf)
    g["wuq"] = _matmul_tn(t["cqn"], dqb, f"dw_uq_{l}")
    g["wkv"] = _matmul_tn(t["ckvn"], dkvb, f"dw_kv_{l}")
    dprojb, dh0, gs["pre_mix_norm"] = _inproj_bwd(dcq, dckv, dmisc_rope, dmisc_dt, dz, dxraw, t["h0"], dh1,
                                                  row("pre_mix_norm"), kw["win"])
    g["win"] = _matmul_tn(t["ub"], dprojb, f"dw_in_{l}")
    return dh0, g, {k: v.reshape(-1) for k, v in gs.items()}


def _local_step(x, positions, kws, sm, target):
    inv_freq = ROPE_THETA ** (-jnp.arange(0, QK_ROPE, 2, dtype=F32) / QK_ROPE)
    invf = jnp.zeros((HEAD_PAD,), F32).at[MISC_ROPE:MISC_ROPE + QK_ROPE].set(jnp.concatenate([inv_freq, inv_freq]))
    cosf, sinf = _rope_tables(positions.reshape(-1, 1), invf.reshape(1, HEAD_PAD))
    consts = _ssd_consts()
    h, saved = x, []
    for l in range(DEPTH):
        h, t = _layer_fwd(h, kws[l], sm, l, cosf, sinf, consts)
        saved.append(t)
    dh, loss = _loss_grad(h, target)
    grads, small = [None] * DEPTH, [None] * DEPTH
    for l in reversed(range(DEPTH)):
        dh, grads[l], small[l] = _layer_bwd(dh, saved[l], kws[l], sm, l, cosf, sinf, consts)
    return loss[0, 0], dh, grads, small


def kernel(x, positions, pre_mix_norm, w_in, q_norm, w_uq, kv_norm, w_ukv, conv_w, conv_b, dt_bias, a_log, d_skip, ssd_norm, w_out, post_mix_norm, pre_mlp_norm, w_up, w_down, post_mlp_norm, loss_target, m_pre_mix_norm, m_w_in, m_q_norm, m_w_uq, m_kv_norm, m_w_ukv, m_conv_w, m_conv_b, m_dt_bias, m_a_log, m_d_skip, m_ssd_norm, m_w_out, m_post_mix_norm, m_pre_mlp_norm, m_w_up, m_w_down, m_post_mlp_norm, v_pre_mix_norm, v_w_in, v_q_norm, v_w_uq, v_kv_norm, v_w_ukv, v_conv_w, v_conv_b, v_dt_bias, v_a_log, v_d_skip, v_ssd_norm, v_w_out, v_post_mix_norm, v_pre_mlp_norm, v_w_up, v_w_down, v_post_mlp_norm):
    w = dict(pre_mix_norm=pre_mix_norm, w_in=w_in, q_norm=q_norm, w_uq=w_uq, kv_norm=kv_norm, w_ukv=w_ukv, conv_w=conv_w,
             conv_b=conv_b, dt_bias=dt_bias, a_log=a_log, d_skip=d_skip, ssd_norm=ssd_norm, w_out=w_out,
             post_mix_norm=post_mix_norm, pre_mlp_norm=pre_mlp_norm, w_up=w_up, w_down=w_down, post_mlp_norm=post_mlp_norm)
    m = dict(pre_mix_norm=m_pre_mix_norm, w_in=m_w_in, q_norm=m_q_norm, w_uq=m_w_uq, kv_norm=m_kv_norm, w_ukv=m_w_ukv,
             conv_w=m_conv_w, conv_b=m_conv_b, dt_bias=m_dt_bias, a_log=m_a_log, d_skip=m_d_skip, ssd_norm=m_ssd_norm,
             w_out=m_w_out, post_mix_norm=m_post_mix_norm, pre_mlp_norm=m_pre_mlp_norm, w_up=m_w_up, w_down=m_w_down,
             post_mlp_norm=m_post_mlp_norm)
    v = dict(pre_mix_norm=v_pre_mix_norm, w_in=v_w_in, q_norm=v_q_norm, w_uq=v_w_uq, kv_norm=v_kv_norm, w_ukv=v_w_ukv,
             conv_w=v_conv_w, conv_b=v_conv_b, dt_bias=v_dt_bias, a_log=v_a_log, d_skip=v_d_skip, ssd_norm=v_ssd_norm,
             w_out=v_w_out, post_mix_norm=v_post_mix_norm, pre_mlp_norm=v_pre_mlp_norm, w_up=v_w_up, w_down=v_w_down,
             post_mlp_norm=v_post_mlp_norm)
    sm = {name: w[name] for name, _ in _SMALL}

    gathered = _all_gather(_pack_shards(w))
    kws = [_kernel_weights(sh) for sh in _unpack_gathered(gathered)]
    loss_part, dx, grads, small = _local_step(x[0], positions[0], kws, sm, loss_target[0])

    blocks = _pack_grad_blocks([_shard_grads(g) for g in grads])
    g_sharded = _unpack_grad_shard(_sum_slots(_exchange(blocks)))
    g_small = _unpack_small(_all_reduce_small(_pack_small({name: jnp.stack([small[l][name] for l in range(DEPTH)])
                                                           for name, _ in _SMALL})))
    loss = lax.psum(loss_part, ("x", "y", "c"))

    grad, delta, new_m, new_v = {}, {}, {}, {}
    for name, _ in _SHARDED:
        shape = w[name].shape
        flat = lambda a: a.reshape(-1, shape[-1])
        d_, m_, v_ = _adamw(flat(w[name]), flat(g_sharded[name]), flat(m[name]), flat(v[name]), f"adamw_{name}")
        grad[name] = g_sharded[name]
        delta[name], new_m[name], new_v[name] = d_.reshape(shape), m_.reshape(shape), v_.reshape(shape)
    pk = lambda d: _pack_small({name: d[name] for name, _ in _SMALL})
    d_, m_, v_ = _adamw(pk(w), pk(g_small), pk(m), pk(v), "adamw_small")
    for dst, packed in ((delta, d_), (new_m, m_), (new_v, v_)):
        dst.update(_unpack_small(packed))
    grad.update(g_small)

    outs = [loss, dx[None]]
    for d in (grad, delta, new_m, new_v):
        outs += [d[name] for name in _WEIGHT_ORDER]
    return tuple(outs)
```

```python
import functools
import math

import jax
import jax.numpy as jnp
import numpy as np
from jax import lax
from jax.experimental import pallas as pl
from jax.experimental.pallas import tpu as pltpu

F32 = jnp.float32
BF16 = jnp.bfloat16
HI = lax.Precision.HIGHEST

D_MODEL = 1024
DEPTH = 2
N_DEV = 8
CHUNK = 64
EPS = 1e-6
MLA_HEADS = 8
QK_NOPE = 64
QK_ROPE = 32
V_DIM = 64
Q_RANK = 768
KV_RANK = 256
ROPE_THETA = 10000.0
SSD_HEADS = 8
SSD_P = 64
SSD_INNER = 512
SSD_GROUPS = 2
SSD_N = 128
CONV_W = 4
CONV_DIM = 1024
D_FF = 4096
IN_PROJ = 2600
HEAD_PAD = 128
IN_PAD = 2688
MISC_ROPE = 64
MISC_DT = 96
ATT_SCALE = (QK_NOPE + QK_ROPE) ** -0.5

ADAM_LR = 0.001
ADAM_B1 = 0.9
ADAM_B2 = 0.999
ADAM_EPS = 1e-08
ADAM_WD = 0.01
ADAM_STEP = 10

TM = 512
TQ = 256
ATT_T = 512
SSD_ROWS = 256
VMEM_LIMIT = 56 * 1024 * 1024

_NT = (((1,), (1,)), ((), ()))
_TN = (((0,), (0,)), ((), ()))


def _params(**kw):
    return pltpu.CompilerParams(vmem_limit_bytes=VMEM_LIMIT, **kw)


def _dot(a, b, precision=None):
    return jnp.dot(a, b, preferred_element_type=F32, precision=precision)


def _dot_nt(a, b, precision=None):
    return lax.dot_general(a, b, _NT, preferred_element_type=F32, precision=precision)


def _dot_tn(a, b, precision=None):
    return lax.dot_general(a, b, _TN, preferred_element_type=F32, precision=precision)


def _full(shape):
    n = len(shape)
    return pl.BlockSpec(shape, lambda *_: (0,) * n)


def _resident(shape):
    n = len(shape)
    return pl.BlockSpec(shape, lambda *_: (0,) * n, pipeline_mode=pl.Buffered(1))


def _rows(tm, width):
    return pl.BlockSpec((tm, width), lambda i: (i, 0))


def _rms_fwd(x, w):
    r = lax.rsqrt(jnp.mean(x * x, axis=-1, keepdims=True) + EPS)
    return (x * r) * w


def _rms_bwd(x, w, dy):
    r = lax.rsqrt(jnp.mean(x * x, axis=-1, keepdims=True) + EPS)
    xh = x * r
    dxn = dy * w
    dx = r * (dxn - xh * jnp.mean(dxn * xh, axis=-1, keepdims=True))
    return dx, dy * xh


def _acc_rows(ref, val, first):
    s = jnp.sum(val, axis=0, keepdims=True)

    @pl.when(first)
    def _():
        ref[...] = s

    @pl.when(jnp.logical_not(first))
    def _():
        ref[...] += s


def _rope(t, cosf, sinf, sign):
    lane = lax.broadcasted_iota(jnp.int32, t.shape, 1)
    rot = jnp.where(lane < MISC_ROPE + QK_ROPE // 2, -pltpu.roll(t, HEAD_PAD - QK_ROPE // 2, 1), pltpu.roll(t, QK_ROPE // 2, 1))
    return t * cosf + sign * (rot * sinf)


def _rope_tables(pos, invf):
    s = pos.shape[0]

    def body(pos_ref, invf_ref, cos_ref, sin_ref):
        ang = pos_ref[...].astype(F32) * invf_ref[...]
        cos_ref[...] = jnp.cos(ang)
        sin_ref[...] = jnp.sin(ang)

    return pl.pallas_call(
        body, name="rope_tables", grid=(s // TM,),
        in_specs=[_rows(TM, 1), _full((1, HEAD_PAD))],
        out_specs=[_rows(TM, HEAD_PAD), _rows(TM, HEAD_PAD)],
        out_shape=[jax.ShapeDtypeStruct((s, HEAD_PAD), F32)] * 2,
    )(pos, invf)


def _inproj_fwd(h, nw, win):
    s = h.shape[0]

    def body(h_ref, nw_ref, w_ref, ub_ref, cq_ref, ckv_ref, misc_ref, z_ref, xbc_ref):
        ub = _rms_fwd(h_ref[...], nw_ref[...]).astype(BF16)
        ub_ref[...] = ub
        proj = _dot(ub, w_ref[...])
        cq_ref[...] = proj[:, 0:768]
        ckv_ref[...] = proj[:, 768:1024]
        misc_ref[...] = proj[:, 1024:1152]
        z_ref[...] = proj[:, 1152:1664]
        xbc_ref[...] = proj[:, 1664:2688]

    widths = (768, 256, 128, 512, 1024)
    return pl.pallas_call(
        body, name="inproj_fwd", grid=(s // TM,),
        in_specs=[_rows(TM, D_MODEL), _full((1, D_MODEL)), _resident((D_MODEL, IN_PAD))],
        out_specs=[_rows(TM, D_MODEL)] + [_rows(TM, w) for w in widths],
        out_shape=[jax.ShapeDtypeStruct((s, D_MODEL), BF16)] + [jax.ShapeDtypeStruct((s, w), F32) for w in widths],
        compiler_params=_params(),
    )(h, nw, win)


def _qkv_fwd(cq, ckv, misc, qnw, kvnw, wuq, wkv, cosf, sinf):
    s = cq.shape[0]

    def body(cq_ref, ckv_ref, misc_ref, qnw_ref, kvnw_ref, wuq_ref, wkv_ref, cos_ref, sin_ref,
             cqn_ref, ckvn_ref, q_ref, k_ref, v_ref):
        cosf, sinf = cos_ref[...], sin_ref[...]
        cqn = _rms_fwd(cq_ref[...], qnw_ref[...]).astype(BF16)
        cqn_ref[...] = cqn
        q = _dot(cqn, wuq_ref[...])
        ckvn = _rms_fwd(ckv_ref[...], kvnw_ref[...]).astype(BF16)
        ckvn_ref[...] = ckvn
        kv = _dot(ckvn, wkv_ref[...])
        m = misc_ref[...]
        lane = lax.broadcasted_iota(jnp.int32, m.shape, 1)
        in_rope = jnp.logical_and(lane >= MISC_ROPE, lane < MISC_ROPE + QK_ROPE)
        kr = jnp.where(in_rope, _rope(m, cosf, sinf, 1.0), 0.0)
        for hd in range(MLA_HEADS):
            cols = slice(hd * HEAD_PAD, (hd + 1) * HEAD_PAD)
            q_ref[:, cols] = _rope(q[:, cols], cosf, sinf, 1.0).astype(BF16)
            k_ref[:, cols] = (kv[:, cols] + kr).astype(BF16)
        v_ref[...] = kv[:, MLA_HEADS * HEAD_PAD:].astype(BF16)

    wide = MLA_HEADS * HEAD_PAD
    return pl.pallas_call(
        body, name="qkv_fwd", grid=(s // TM,),
        in_specs=[_rows(TM, Q_RANK), _rows(TM, KV_RANK), _rows(TM, HEAD_PAD), _full((1, Q_RANK)), _full((1, KV_RANK)),
                  _resident((Q_RANK, wide)), _resident((KV_RANK, 2 * wide)), _rows(TM, HEAD_PAD), _rows(TM, HEAD_PAD)],
        out_specs=[_rows(TM, Q_RANK), _rows(TM, KV_RANK), _rows(TM, wide), _rows(TM, wide), _rows(TM, wide)],
        out_shape=[jax.ShapeDtypeStruct((s, Q_RANK), BF16), jax.ShapeDtypeStruct((s, KV_RANK), BF16)]
        + [jax.ShapeDtypeStruct((s, wide), BF16)] * 3,
        compiler_params=_params(),
    )(cq, ckv, misc, qnw, kvnw, wuq, wkv, cosf, sinf)


def _chunk_mask(t, keys_on_rows=False):
    row = lax.broadcasted_iota(jnp.int32, (t, t), 0) // CHUNK
    col = lax.broadcasted_iota(jnp.int32, (t, t), 1) // CHUNK
    return (row <= col) if keys_on_rows else (col <= row)


def _attn_fwd(q, k, v):
    s = q.shape[0]
    t = ATT_T
    nq = s // t
    pair = 2 * HEAD_PAD

    def body(q_ref, k_ref, v_ref, o_ref, lse_ref, m_s, l_s, acc_s):
        qi = pl.program_id(1)
        m_s[...] = jnp.full(m_s.shape, -jnp.inf, F32)
        l_s[...] = jnp.zeros(l_s.shape, F32)
        acc_s[...] = jnp.zeros(acc_s.shape, F32)

        def step(kb, masked):
            r0 = pl.multiple_of(kb * t, t)
            for hh in range(2):
                cols = slice(hh * HEAD_PAD, (hh + 1) * HEAD_PAD)
                sc = _dot_nt(q_ref[:, cols], k_ref[pl.ds(r0, t), cols]) * ATT_SCALE
                if masked:
                    sc = jnp.where(_chunk_mask(t), sc, -jnp.inf)
                m_old = m_s[hh]
                m_new = jnp.maximum(m_old, jnp.max(sc, axis=-1, keepdims=True))
                alpha = jnp.exp(m_old - m_new)
                p = jnp.exp(sc - jnp.tile(m_new, (1, t // HEAD_PAD)))
                l_s[hh] = alpha * l_s[hh] + jnp.sum(p, axis=-1, keepdims=True)
                acc_s[hh] = alpha * acc_s[hh] + _dot(p.astype(BF16), v_ref[pl.ds(r0, t), cols])
                m_s[hh] = m_new

        def loop(kb, c):
            step(kb, False)
            return c

        lax.fori_loop(0, qi, loop, 0)
        step(qi, True)
        for hh in range(2):
            cols = slice(hh * HEAD_PAD, (hh + 1) * HEAD_PAD)
            o_ref[:, cols] = (acc_s[hh] / l_s[hh]).astype(BF16)
            lse_ref[hh] = (m_s[hh] + jnp.log(l_s[hh])).T[0:8, :]

    return pl.pallas_call(
        body, name="attn_fwd", grid=(MLA_HEADS // 2, nq),
        in_specs=[pl.BlockSpec((t, pair), lambda h, i: (i, h)),
                  pl.BlockSpec((s, pair), lambda h, i: (0, h)),
                  pl.BlockSpec((s, pair), lambda h, i: (0, h))],
        out_specs=[pl.BlockSpec((t, pair), lambda h, i: (i, h)),
                   pl.BlockSpec((2, 8, t), lambda h, i: (h, 0, i))],
        out_shape=[jax.ShapeDtypeStruct((s, MLA_HEADS * HEAD_PAD), BF16), jax.ShapeDtypeStruct((MLA_HEADS, 8, s), F32)],
        scratch_shapes=[pltpu.VMEM((2, t, HEAD_PAD), F32), pltpu.VMEM((2, t, HEAD_PAD), F32), pltpu.VMEM((2, t, HEAD_PAD), F32)],
        compiler_params=_params(),
    )(q, k, v)


def _ssd_consts():
    emisc = np.zeros((HEAD_PAD, SSD_INNER), np.float32)
    for hd in range(SSD_HEADS):
        emisc[MISC_DT + hd, hd * SSD_P:(hd + 1) * SSD_P] = 1.0
    idx = np.arange(CHUNK)
    tri = (idx[:, None] >= idx[None, :]).astype(np.float32)
    return jnp.asarray(emisc), jnp.asarray(emisc.T.copy()), jnp.asarray(tri), jnp.asarray(tri.T.copy())


def _ssd_chunk_common(cc, misc, emisc, tri, trit, dtb, a_exp):
    xa = cc * jax.nn.sigmoid(cc)
    dtr = _dot(misc, emisc, HI) + dtb
    dt = jax.nn.softplus(dtr)
    a = dt * a_exp
    acs = _dot(tri, a, HI)
    acs_t = _dot_tn(a, trit, HI)
    alast = acs[CHUNK - 1:CHUNK, :]
    return xa, dtr, dt, acs, acs_t, alast


def _decay(acs, acs_t, hd):
    row = lax.broadcasted_iota(jnp.int32, (CHUNK, CHUNK), 0)
    col = lax.broadcasted_iota(jnp.int32, (CHUNK, CHUNK), 1)
    diff = acs[:, hd * SSD_P:hd * SSD_P + 1] - acs_t[hd * SSD_P:hd * SSD_P + 1, :]
    return jnp.exp(jnp.where(row >= col, diff, -jnp.inf))


def _half_mask(hh):
    lane = lax.broadcasted_iota(jnp.int32, (CHUNK, 2 * SSD_P), 1)
    return (lane >= SSD_P) if hh else (lane < SSD_P)


def _gate_norm(y, zz, nw):
    yz = y * (zz * jax.nn.sigmoid(zz))
    outs, rs = [], []
    half = SSD_INNER // SSD_GROUPS
    for g in range(SSD_GROUPS):
        yg = yz[:, g * half:(g + 1) * half]
        r = lax.rsqrt(jnp.mean(yg * yg, axis=-1, keepdims=True) + EPS)
        outs.append(yg * r)
        rs.append(r)
    return yz, jnp.concatenate(outs, axis=1), rs


def _ssd_fwd(xraw, misc, z, cw, cb, dtb, a_exp, d_exp, nw, consts):
    s = xraw.shape[0]
    nb = s // SSD_ROWS
    ncb = SSD_ROWS // CHUNK
    emisc, _, tri, trit = consts

    def body(x_ref, misc_ref, z_ref, cw_ref, cb_ref, dtb_ref, a_ref, d_ref, nw_ref, emisc_ref, tri_ref, trit_ref,
             c_ref, prev_ref, ypre_ref, yssd_ref, tail_s, state_s):
        i = pl.program_id(0)

        @pl.when(i == 0)
        def _():
            tail_s[...] = jnp.zeros(tail_s.shape, F32)
            state_s[...] = jnp.zeros(state_s.shape, F32)

        x = x_ref[...]
        xext = jnp.concatenate([tail_s[...], x], axis=0)
        acc = x * cw_ref[CONV_W - 1:CONV_W, :] + cb_ref[...]
        for j in range(1, CONV_W):
            acc = acc + pltpu.roll(xext, j, 0)[8:, :] * cw_ref[CONV_W - 1 - j:CONV_W - j, :]
        tail_s[...] = x[SSD_ROWS - 8:, :]
        c_ref[...] = acc

        def chunk(ci, carry):
            r0 = pl.multiple_of(ci * CHUNK, CHUNK)
            xa, _, dt, acs, acs_t, alast = _ssd_chunk_common(
                c_ref[pl.ds(r0, CHUNK), :], misc_ref[pl.ds(r0, CHUNK), :], emisc_ref[...], tri_ref[...], trit_ref[...],
                dtb_ref[...], a_ref[...])
            xs = xa[:, :SSD_INNER]
            xdt = xs * dt
            prev = state_s[...]
            prev_ref[ci] = prev
            wgt = (xdt * jnp.exp(alast - acs)).astype(BF16)
            e = jnp.exp(acs)
            ys, new_states = [], []
            for g in range(SSD_GROUPS):
                bm = xa[:, SSD_INNER + g * SSD_N:SSD_INNER + (g + 1) * SSD_N].astype(BF16)
                cm = xa[:, SSD_INNER + SSD_GROUPS * SSD_N + g * SSD_N:SSD_INNER + SSD_GROUPS * SSD_N + (g + 1) * SSD_N].astype(BF16)
                cb_g = _dot_nt(cm, bm)
                gl = slice(g * 256, (g + 1) * 256)
                new_states.append(_dot_tn(bm, wgt[:, gl]))
                yoff = _dot(cm, prev[:, gl].astype(BF16)) * e[:, gl]
                for jj in range(2):
                    pair = 2 * g + jj
                    pl_ = slice(pair * 128, (pair + 1) * 128)
                    xp = xdt[:, pl_]
                    yp = yoff[:, jj * 128:(jj + 1) * 128]
                    for hh in range(2):
                        sc = (cb_g * _decay(acs, acs_t, 2 * pair + hh)).astype(BF16)
                        yp = yp + _dot(sc, jnp.where(_half_mask(hh), xp, 0.0).astype(BF16))
                    ys.append(yp)
            y = jnp.concatenate(ys, axis=1) + d_ref[...] * xs
            state_s[...] = prev * jnp.exp(alast) + jnp.concatenate(new_states, axis=1)
            ypre_ref[pl.ds(r0, CHUNK), :] = y
            _, yn, _ = _gate_norm(y, z_ref[pl.ds(r0, CHUNK), :], None)
            yssd_ref[pl.ds(r0, CHUNK), :] = (yn * nw_ref[...]).astype(BF16)
            return carry

        lax.fori_loop(0, ncb, chunk, 0)

    return pl.pallas_call(
        body, name="ssd_fwd", grid=(nb,),
        in_specs=[_rows(SSD_ROWS, CONV_DIM), _rows(SSD_ROWS, HEAD_PAD), _rows(SSD_ROWS, SSD_INNER),
                  _full((CONV_W, CONV_DIM)), _full((1, CONV_DIM)), _full((1, SSD_INNER)), _full((1, SSD_INNER)),
                  _full((1, SSD_INNER)), _full((1, SSD_INNER)), _full((HEAD_PAD, SSD_INNER)), _full((CHUNK, CHUNK)),
                  _full((CHUNK, CHUNK))],
        out_specs=[_rows(SSD_ROWS, CONV_DIM), pl.BlockSpec((ncb, SSD_N, SSD_INNER), lambda i: (i, 0, 0)),
                   _rows(SSD_ROWS, SSD_INNER), _rows(SSD_ROWS, SSD_INNER)],
        out_shape=[jax.ShapeDtypeStruct((s, CONV_DIM), F32), jax.ShapeDtypeStruct((s // CHUNK, SSD_N, SSD_INNER), F32),
                   jax.ShapeDtypeStruct((s, SSD_INNER), F32), jax.ShapeDtypeStruct((s, SSD_INNER), BF16)],
        scratch_shapes=[pltpu.VMEM((8, CONV_DIM), F32), pltpu.VMEM((SSD_N, SSD_INNER), F32)],
        compiler_params=_params(),
    )(xraw, misc, z, cw, cb, dtb, a_exp, d_exp, nw, emisc, tri, trit)


def _outproj_fwd(oe, yssd, wout, h, nw):
    s = h.shape[0]
    wide = MLA_HEADS * HEAD_PAD

    def body(oe_ref, y_ref, w_ref, h_ref, nw_ref, mixed_ref, h1_ref):
        mixed = _dot(oe_ref[...], w_ref[0:wide, :]) + _dot(y_ref[...], w_ref[wide:, :])
        mixed_ref[...] = mixed
        h1_ref[...] = h_ref[...] + _rms_fwd(mixed, nw_ref[...])

    return pl.pallas_call(
        body, name="outproj_fwd", grid=(s // TM,),
        in_specs=[_rows(TM, wide), _rows(TM, SSD_INNER), _resident((wide + SSD_INNER, D_MODEL)), _rows(TM, D_MODEL),
                  _full((1, D_MODEL))],
        out_specs=[_rows(TM, D_MODEL), _rows(TM, D_MODEL)],
        out_shape=[jax.ShapeDtypeStruct((s, D_MODEL), F32)] * 2,
        compiler_params=_params(),
    )(oe, yssd, wout, h, nw)


def _mlp_fwd(h1, prew, wup, wdown, postw):
    s = h1.shape[0]
    fb = D_FF // N_DEV

    def body(h_ref, prew_ref, up_ref, down_ref, postw_ref, mb_ref, d_ref, h2_ref):
        hh = h_ref[...]
        mb = _rms_fwd(hh, prew_ref[...]).astype(BF16)
        mb_ref[...] = mb
        d = jnp.zeros((TM, D_MODEL), F32)
        for j in range(N_DEV):
            a = _dot(mb, up_ref[j])
            r = jnp.square(jnp.maximum(a, 0.0)).astype(BF16)
            d = d + _dot(r, down_ref[j])
        d_ref[...] = d
        h2_ref[...] = hh + _rms_fwd(d, postw_ref[...])

    return pl.pallas_call(
        body, name="mlp_fwd", grid=(s // TM,),
        in_specs=[_rows(TM, D_MODEL), _full((1, D_MODEL)), _resident((N_DEV, D_MODEL, fb)), _resident((N_DEV, fb, D_MODEL)),
                  _full((1, D_MODEL))],
        out_specs=[_rows(TM, D_MODEL)] * 3,
        out_shape=[jax.ShapeDtypeStruct((s, D_MODEL), BF16), jax.ShapeDtypeStruct((s, D_MODEL), F32),
                   jax.ShapeDtypeStruct((s, D_MODEL), F32)],
        compiler_params=_params(),
    )(h1, prew, wup, wdown, postw)


def _loss_grad(h, target):
    s = h.shape[0]

    def body(h_ref, t_ref, dh_ref, loss_ref):
        diff = h_ref[...] - t_ref[...]
        dh_ref[...] = diff * (1.0 / D_MODEL)
        part = 0.5 * jnp.sum(jnp.mean(diff * diff, axis=-1, keepdims=True), axis=0, keepdims=True)
        _acc_rows(loss_ref, part, pl.program_id(0) == 0)

    return pl.pallas_call(
        body, name="loss_grad", grid=(s // TM,),
        in_specs=[_rows(TM, D_MODEL)] * 2,
        out_specs=[_rows(TM, D_MODEL), _full((1, 1))],
        out_shape=[jax.ShapeDtypeStruct((s, D_MODEL), F32), jax.ShapeDtypeStruct((1, 1), F32)],
    )(h, target)


def _mlp_bwd(dh2, d, h1, mb, prew, wup, wdown, postw):
    s = dh2.shape[0]
    fb = D_FF // N_DEV
    tm = TM // 2

    def body(dh2_ref, d_ref, h1_ref, mb_ref, prew_ref, up_ref, down_ref, postw_ref,
             dh1_ref, da_ref, r_ref, dd_ref, gpost_ref, gpre_ref):
        first = pl.program_id(0) == 0
        dh2 = dh2_ref[...]
        dd, gpost = _rms_bwd(d_ref[...], postw_ref[...], dh2)
        _acc_rows(gpost_ref, gpost, first)
        ddb = dd.astype(BF16)
        dd_ref[...] = ddb
        mb = mb_ref[...]
        dm = jnp.zeros((tm, D_MODEL), F32)
        for j in range(N_DEV):
            a = jnp.maximum(_dot(mb, up_ref[j]), 0.0)
            r_ref[j] = jnp.square(a).astype(BF16)
            da = (_dot_nt(ddb, down_ref[j]) * (2.0 * a)).astype(BF16)
            da_ref[j] = da
            dm = dm + _dot_nt(da, up_ref[j])
        dx, gpre = _rms_bwd(h1_ref[...], prew_ref[...], dm)
        _acc_rows(gpre_ref, gpre, first)
        dh1_ref[...] = dh2 + dx

    stacked = pl.BlockSpec((N_DEV, tm, fb), lambda i: (0, i, 0))
    return pl.pallas_call(
        body, name="mlp_bwd", grid=(s // tm,),
        in_specs=[_rows(tm, D_MODEL)] * 4 + [_full((1, D_MODEL)), _resident((N_DEV, D_MODEL, fb)), _resident((N_DEV, fb, D_MODEL)),
                                              _full((1, D_MODEL))],
        out_specs=[_rows(tm, D_MODEL), stacked, stacked, _rows(tm, D_MODEL), _full((1, D_MODEL)), _full((1, D_MODEL))],
        out_shape=[jax.ShapeDtypeStruct((s, D_MODEL), F32), jax.ShapeDtypeStruct((N_DEV, s, fb), BF16),
                   jax.ShapeDtypeStruct((N_DEV, s, fb), BF16), jax.ShapeDtypeStruct((s, D_MODEL), BF16),
                   jax.ShapeDtypeStruct((1, D_MODEL), F32), jax.ShapeDtypeStruct((1, D_MODEL), F32)],
        compiler_params=_params(),
    )(dh2, d, h1, mb, prew, wup, wdown, postw)


def _matmul_tn(a, b, name, tk=512):
    s, m = a.shape
    n = b.shape[1]
    tn = n if n <= 1024 else (n // 2 if (n // 2) % 128 == 0 else n // 3)
    assert n % tn == 0 and tn % 128 == 0 and s % tk == 0

    def body(a_ref, b_ref, o_ref):
        part = _dot_tn(a_ref[...], b_ref[...])

        @pl.when(pl.program_id(1) == 0)
        def _():
            o_ref[...] = part

        @pl.when(pl.program_id(1) != 0)
        def _():
            o_ref[...] += part

    return pl.pallas_call(
        body, name=name, grid=(n // tn, s // tk),
        in_specs=[pl.BlockSpec((tk, m), lambda j, k: (k, 0)), pl.BlockSpec((tk, tn), lambda j, k: (k, j))],
        out_specs=pl.BlockSpec((m, tn), lambda j, k: (0, j)),
        out_shape=jax.ShapeDtypeStruct((m, n), F32),
        compiler_params=_params(),
    )(a, b)


def _matmul_tn_stacked(a, b, name, a_stacked, tk=512):
    if a_stacked:
        _, s, m = a.shape
        n = b.shape[1]
        in_specs = [pl.BlockSpec((1, tk, m), lambda j, k: (j, k, 0)), pl.BlockSpec((tk, n), lambda j, k: (k, 0))]
    else:
        s, m = a.shape
        n = b.shape[2]
        in_specs = [pl.BlockSpec((tk, m), lambda j, k: (k, 0)), pl.BlockSpec((1, tk, n), lambda j, k: (j, k, 0))]

    def body(a_ref, b_ref, o_ref):
        av = a_ref[0] if a_stacked else a_ref[...]
        bv = b_ref[...] if a_stacked else b_ref[0]
        part = _dot_tn(av, bv)

        @pl.when(pl.program_id(1) == 0)
        def _():
            o_ref[0] = part

        @pl.when(pl.program_id(1) != 0)
        def _():
            o_ref[0] += part

    return pl.pallas_call(
        body, name=name, grid=(N_DEV, s // tk),
        in_specs=in_specs,
        out_specs=pl.BlockSpec((1, m, n), lambda j, k: (j, 0, 0)),
        out_shape=jax.ShapeDtypeStruct((N_DEV, m, n), F32),
        compiler_params=_params(),
    )(a, b)


def _outproj_bwd(dh1, mixed, nw, wout):
    s = dh1.shape[0]
    wide = MLA_HEADS * HEAD_PAD

    def body(dh1_ref, mixed_ref, nw_ref, w_ref, dmix_ref, doe_ref, dy_ref, gnw_ref):
        dmix, gnw = _rms_bwd(mixed_ref[...], nw_ref[...], dh1_ref[...])
        _acc_rows(gnw_ref, gnw, pl.program_id(0) == 0)
        dmb = dmix.astype(BF16)
        dmix_ref[...] = dmb
        doe_ref[...] = _dot_nt(dmb, w_ref[0:wide, :]).astype(BF16)
        dy_ref[...] = _dot_nt(dmb, w_ref[wide:, :])

    return pl.pallas_call(
        body, name="outproj_bwd", grid=(s // TM,),
        in_specs=[_rows(TM, D_MODEL), _rows(TM, D_MODEL), _full((1, D_MODEL)), _resident((wide + SSD_INNER, D_MODEL))],
        out_specs=[_rows(TM, D_MODEL), _rows(TM, wide), _rows(TM, SSD_INNER), _full((1, D_MODEL))],
        out_shape=[jax.ShapeDtypeStruct((s, D_MODEL), BF16), jax.ShapeDtypeStruct((s, wide), BF16),
                   jax.ShapeDtypeStruct((s, SSD_INNER), F32), jax.ShapeDtypeStruct((1, D_MODEL), F32)],
        compiler_params=_params(),
    )(dh1, mixed, nw, wout)


def _attn_delta(o, do):
    s = o.shape[0]
    wide = MLA_HEADS * HEAD_PAD

    def body(o_ref, do_ref, d_ref):
        ones = jnp.ones((8, HEAD_PAD), F32)
        for hd in range(MLA_HEADS):
            cols = slice(hd * HEAD_PAD, (hd + 1) * HEAD_PAD)
            prod = o_ref[:, cols].astype(F32) * do_ref[:, cols].astype(F32)
            d_ref[hd] = _dot_nt(ones, prod, HI)

    return pl.pallas_call(
        body, name="attn_delta", grid=(s // TM,),
        in_specs=[_rows(TM, wide), _rows(TM, wide)],
        out_specs=pl.BlockSpec((MLA_HEADS, 8, TM), lambda i: (0, 0, i)),
        out_shape=jax.ShapeDtypeStruct((MLA_HEADS, 8, s), F32),
    )(o, do)


def _attn_bwd(q, k, v, do, lse, delta):
    s = q.shape[0]
    t = ATT_T
    nq = s // t
    pair = 2 * HEAD_PAD

    def body(q_ref, k_ref, v_ref, do_ref, lse_ref, delta_ref, dq_ref, dk_ref, dv_ref):
        kb = pl.program_id(1)

        @pl.when(kb == 0)
        def _():
            dq_ref[...] = jnp.zeros(dq_ref.shape, F32)

        dk_ref[...] = jnp.zeros(dk_ref.shape, F32)
        dv_ref[...] = jnp.zeros(dv_ref.shape, F32)

        def step(qb, masked):
            r0 = pl.multiple_of(qb * t, t)
            for hh in range(2):
                cols = slice(hh * HEAD_PAD, (hh + 1) * HEAD_PAD)
                kk = k_ref[:, cols]
                qq = q_ref[pl.ds(r0, t), cols]
                dd = do_ref[pl.ds(r0, t), cols]
                sc = _dot_nt(kk, qq) * ATT_SCALE
                if masked:
                    sc = jnp.where(_chunk_mask(t, keys_on_rows=True), sc, -jnp.inf)
                p = jnp.exp(sc - lse_ref[hh, 0:1, pl.ds(r0, t)])
                dv_ref[:, cols] += _dot(p.astype(BF16), dd)
                dp = _dot_nt(v_ref[:, cols], dd)
                ds = (p * (dp - delta_ref[hh, 0:1, pl.ds(r0, t)]) * ATT_SCALE).astype(BF16)
                dk_ref[:, cols] += _dot(ds, qq)
                dq_ref[pl.ds(r0, t), cols] += _dot_tn(ds, kk)

        def loop(qb, c):
            step(qb, False)
            return c

        step(kb, True)
        lax.fori_loop(kb + 1, nq, loop, 0)

    whole = pl.BlockSpec((s, pair), lambda h, i: (0, h))
    tile = pl.BlockSpec((t, pair), lambda h, i: (i, h))
    rowvec = pl.BlockSpec((2, 8, s), lambda h, i: (h, 0, 0))
    wide = MLA_HEADS * HEAD_PAD
    return pl.pallas_call(
        body, name="attn_bwd", grid=(MLA_HEADS // 2, nq),
        in_specs=[whole, tile, tile, whole, rowvec, rowvec],
        out_specs=[whole, tile, tile],
        out_shape=[jax.ShapeDtypeStruct((s, wide), F32)] * 3,
        compiler_params=_params(),
    )(q, k, v, do, lse, delta)


def _ssd_bwd(dy, ypre, z, c, xraw, misc, prev, cw, dtb, a_exp, d_exp, nw, consts):
    s = dy.shape[0]
    nb = s // SSD_ROWS
    ncb = SSD_ROWS // CHUNK
    emisc, emisc_t, tri, trit = consts

    def body(dy_ref, ypre_ref, z_ref, c_ref, x_ref, xprev_ref, misc_ref, prev_ref, cw_ref, dtb_ref, a_ref, d_ref, nw_ref,
             emisc_ref, emisct_ref, tri_ref, trit_ref,
             dz_ref, dx_ref, dmisc_ref, gnw_ref, gd_ref, galog_ref, gdtb_ref, gcw_ref, gcb_ref,
             dst_s, dc_s, head_s):
        i = pl.program_id(0)
        first = i == 0

        @pl.when(first)
        def _():
            dst_s[...] = jnp.zeros(dst_s.shape, F32)
            head_s[...] = jnp.zeros(head_s.shape, F32)
            gnw_ref[...] = jnp.zeros(gnw_ref.shape, F32)
            gd_ref[...] = jnp.zeros(gd_ref.shape, F32)
            galog_ref[...] = jnp.zeros(galog_ref.shape, F32)
            gdtb_ref[...] = jnp.zeros(gdtb_ref.shape, F32)

        a_exp_v = a_ref[...]
        a8 = _dot(a_exp_v, emisct_ref[...], HI) * (1.0 / SSD_P)

        def chunk(cr, carry):
            ci = ncb - 1 - cr
            r0 = pl.multiple_of(ci * CHUNK, CHUNK)
            cc = c_ref[pl.ds(r0, CHUNK), :]
            mm = misc_ref[pl.ds(r0, CHUNK), :]
            xa, dtr, dt, acs, acs_t, alast = _ssd_chunk_common(cc, mm, emisc_ref[...], tri_ref[...], trit_ref[...],
                                                              dtb_ref[...], a_exp_v)
            xs = xa[:, :SSD_INNER]
            xdt = xs * dt
            y = ypre_ref[pl.ds(r0, CHUNK), :]
            zz = z_ref[pl.ds(r0, CHUNK), :]
            yz, yn, rs = _gate_norm(y, zz, None)
            dyo = dy_ref[pl.ds(r0, CHUNK), :]
            gnw_ref[...] += jnp.sum(dyo * yn, axis=0, keepdims=True)
            dyn = dyo * nw_ref[...]
            half = SSD_INNER // SSD_GROUPS
            dyz_parts = []
            for g in range(SSD_GROUPS):
                gl = slice(g * half, (g + 1) * half)
                dyz_parts.append(rs[g] * (dyn[:, gl] - yn[:, gl] * jnp.mean(dyn[:, gl] * yn[:, gl], axis=-1, keepdims=True)))
            dyz = jnp.concatenate(dyz_parts, axis=1)
            sg = jax.nn.sigmoid(zz)
            dz_ref[pl.ds(r0, CHUNK), :] = dyz * y * (sg * (1.0 + zz * (1.0 - sg)))
            dyp = dyz * (zz * sg)
            dypb = dyp.astype(BF16)
            gd_ref[...] += jnp.sum(dyp * xs, axis=0, keepdims=True)
            prev = prev_ref[ci]
            dst = dst_s[...]
            cd = jnp.exp(alast)
            e = jnp.exp(acs)
            dsx = jnp.exp(alast - acs)
            wgt = (xdt * dsx).astype(BF16)
            dze = (dyp * e).astype(BF16)
            glast = jnp.sum(dst * prev, axis=0, keepdims=True) * cd
            dprev_parts, dxdt_parts, dxdt_state_parts, dbm, dcm, yoff_parts = [], [], [], [], [], []
            lane8 = lax.broadcasted_iota(jnp.int32, (CHUNK, HEAD_PAD), 1)
            diag8 = jnp.zeros((CHUNK, HEAD_PAD), F32)
            for g in range(SSD_GROUPS):
                gl = slice(g * 256, (g + 1) * 256)
                bm = xa[:, SSD_INNER + g * SSD_N:SSD_INNER + (g + 1) * SSD_N].astype(BF16)
                cm = xa[:, SSD_INNER + SSD_GROUPS * SSD_N + g * SSD_N:SSD_INNER + SSD_GROUPS * SSD_N + (g + 1) * SSD_N].astype(BF16)
                prev_g = prev[:, gl].astype(BF16)
                dst_g = dst[:, gl].astype(BF16)
                dcm_g = _dot_nt(dze[:, gl], prev_g)
                dprev_parts.append(_dot_tn(cm, dze[:, gl]))
                dxs_state = _dot(bm, dst_g) * dsx[:, gl]
                dbm_g = _dot_nt(wgt[:, gl], dst_g)
                cb_g = _dot_nt(cm, bm)
                dcb = jnp.zeros((CHUNK, CHUNK), F32)
                diag_parts = []
                for jj in range(2):
                    pair = 2 * g + jj
                    pl_ = slice(pair * 128, (pair + 1) * 128)
                    xp = xdt[:, pl_]
                    dyp_p = dypb[:, pl_]
                    dxp = jnp.zeros((CHUNK, 128), F32)
                    for hh in range(2):
                        hd = 2 * pair + hh
                        dec = _decay(acs, acs_t, hd)
                        xm = jnp.where(_half_mask(hh), xp, 0.0).astype(BF16)
                        dsc = _dot_nt(dyp_p, xm) * dec
                        dcb = dcb + dsc
                        sc = (cb_g * dec).astype(BF16)
                        dxp = dxp + jnp.where(_half_mask(hh), _dot_tn(sc, dyp_p), 0.0)
                        dm = dsc * cb_g
                        diag8 = diag8 + jnp.where(lane8 == MISC_DT + hd, jnp.sum(dm - dm.T, axis=1, keepdims=True), 0.0)
                    diag_parts.append(dxp)
                dcbb = dcb.astype(BF16)
                dcm.append(dcm_g + _dot(dcbb, bm))
                dbm.append(dbm_g + _dot_tn(dcbb, cm))
                dxdt_state_parts.append(dxs_state)
                dxdt_parts.append(jnp.concatenate(diag_parts, axis=1) + dxs_state)
                yoff_parts.append(_dot(cm, prev_g) * e[:, gl])
            dxdt = jnp.concatenate(dxdt_parts, axis=1)
            dxdt_state = jnp.concatenate(dxdt_state_parts, axis=1)
            dst_s[...] = dst * cd + jnp.concatenate(dprev_parts, axis=1)
            dacs = dyp * jnp.concatenate(yoff_parts, axis=1) - xdt * dxdt_state
            last = jnp.sum(xdt * dxdt_state, axis=0, keepdims=True) + glast
            row = lax.broadcasted_iota(jnp.int32, (CHUNK, SSD_INNER), 0)
            dacs = dacs + jnp.where(row == CHUNK - 1, last, 0.0)
            dacs8 = _dot(dacs, emisct_ref[...], HI) + diag8
            da8 = _dot(trit_ref[...], dacs8, HI)
            ddt8 = da8 * a8 + _dot(dxdt * xs, emisct_ref[...], HI)
            dtr8 = mm + _dot(dtb_ref[...], emisct_ref[...], HI) * (1.0 / SSD_P)
            dt8 = jax.nn.softplus(dtr8)
            lane = lax.broadcasted_iota(jnp.int32, (CHUNK, HEAD_PAD), 1)
            on_dt = jnp.logical_and(lane >= MISC_DT, lane < MISC_DT + SSD_HEADS)
            ddtr8 = jnp.where(on_dt, ddt8 * jax.nn.sigmoid(dtr8), 0.0)
            dmisc_ref[pl.ds(r0, CHUNK), :] = ddtr8
            gdtb_ref[...] += jnp.sum(ddtr8, axis=0, keepdims=True)
            galog_ref[...] += jnp.sum(jnp.where(on_dt, da8 * dt8, 0.0), axis=0, keepdims=True) * a8
            dxs = d_ref[...] * dyp + dxdt * dt
            dxa = jnp.concatenate([dxs] + dbm + dcm, axis=1)
            sc_ = jax.nn.sigmoid(cc)
            dc_s[pl.ds(r0, CHUNK), :] = dxa * (sc_ * (1.0 + cc * (1.0 - sc_)))
            return carry

        lax.fori_loop(0, ncb, chunk, 0)

        dc = dc_s[...]
        dcext = jnp.concatenate([dc, head_s[...]], axis=0)
        dx = dc * cw_ref[CONV_W - 1:CONV_W, :]
        for j in range(1, CONV_W):
            dx = dx + pltpu.roll(dcext, SSD_ROWS + 8 - j, 0)[:SSD_ROWS, :] * cw_ref[CONV_W - 1 - j:CONV_W - j, :]
        dx_ref[...] = dx
        head_s[...] = dc[:8, :]
        xprev = jnp.where(i == nb - 1, 0.0, xprev_ref[...])
        xext = jnp.concatenate([xprev, x_ref[...]], axis=0)
        rows = [jnp.sum(dc * pltpu.roll(xext, CONV_W - 1 - kk, 0)[8:, :], axis=0, keepdims=True) for kk in range(CONV_W)]
        gcw = jnp.concatenate(rows, axis=0)

        @pl.when(first)
        def _():
            gcw_ref[...] = gcw
            gcb_ref[...] = jnp.sum(dc, axis=0, keepdims=True)

        @pl.when(jnp.logical_not(first))
        def _():
            gcw_ref[...] += gcw
            gcb_ref[...] += jnp.sum(dc, axis=0, keepdims=True)

    def rev(width):
        return pl.BlockSpec((SSD_ROWS, width), lambda i: (nb - 1 - i, 0))

    per8 = SSD_ROWS // 8
    return pl.pallas_call(
        body, name="ssd_bwd", grid=(nb,),
        in_specs=[rev(SSD_INNER), rev(SSD_INNER), rev(SSD_INNER), rev(CONV_DIM), rev(CONV_DIM),
                  pl.BlockSpec((8, CONV_DIM), lambda i: (jnp.maximum((nb - 1 - i) * per8 - 1, 0), 0)),
                  rev(HEAD_PAD), pl.BlockSpec((ncb, SSD_N, SSD_INNER), lambda i: (nb - 1 - i, 0, 0)),
                  _full((CONV_W, CONV_DIM)), _full((1, SSD_INNER)), _full((1, SSD_INNER)), _full((1, SSD_INNER)),
                  _full((1, SSD_INNER)), _full((HEAD_PAD, SSD_INNER)), _full((SSD_INNER, HEAD_PAD)), _full((CHUNK, CHUNK)),
                  _full((CHUNK, CHUNK))],
        out_specs=[rev(SSD_INNER), rev(CONV_DIM), rev(HEAD_PAD), _full((1, SSD_INNER)), _full((1, SSD_INNER)),
                   _full((1, HEAD_PAD)), _full((1, HEAD_PAD)), _full((CONV_W, CONV_DIM)), _full((1, CONV_DIM))],
        out_shape=[jax.ShapeDtypeStruct((s, SSD_INNER), F32), jax.ShapeDtypeStruct((s, CONV_DIM), F32),
                   jax.ShapeDtypeStruct((s, HEAD_PAD), F32), jax.ShapeDtypeStruct((1, SSD_INNER), F32),
                   jax.ShapeDtypeStruct((1, SSD_INNER), F32), jax.ShapeDtypeStruct((1, HEAD_PAD), F32),
                   jax.ShapeDtypeStruct((1, HEAD_PAD), F32), jax.ShapeDtypeStruct((CONV_W, CONV_DIM), F32),
                   jax.ShapeDtypeStruct((1, CONV_DIM), F32)],
        scratch_shapes=[pltpu.VMEM((SSD_N, SSD_INNER), F32), pltpu.VMEM((SSD_ROWS, CONV_DIM), F32), pltpu.VMEM((8, CONV_DIM), F32)],
        compiler_params=_params(),
    )(dy, ypre, z, c, xraw, xraw, misc, prev, cw, dtb, a_exp, d_exp, nw, emisc, emisc_t, tri, trit)


def _qkv_bwd(dq, dk, dv, cq, ckv, qnw, kvnw, wuq, wkv, cosf, sinf):
    s = dq.shape[0]
    wide = MLA_HEADS * HEAD_PAD

    def body(dq_ref, dk_ref, dv_ref, cq_ref, ckv_ref, qnw_ref, kvnw_ref, wuq_ref, wkv_ref, cos_ref, sin_ref,
             dqb_ref, dkvb_ref, dcq_ref, dckv_ref, dmisc_ref, gq_ref, gkv_ref):
        first = pl.program_id(0) == 0
        cosf, sinf = cos_ref[...], sin_ref[...]
        dkr = jnp.zeros((TM, HEAD_PAD), F32)
        for hd in range(MLA_HEADS):
            cols = slice(hd * HEAD_PAD, (hd + 1) * HEAD_PAD)
            dqb_ref[:, cols] = _rope(dq_ref[:, cols], cosf, sinf, -1.0).astype(BF16)
            dkh = dk_ref[:, cols]
            dkvb_ref[:, cols] = dkh.astype(BF16)
            dkr = dkr + dkh
        dkvb_ref[:, wide:] = dv_ref[...].astype(BF16)
        lane = lax.broadcasted_iota(jnp.int32, dkr.shape, 1)
        in_rope = jnp.logical_and(lane >= MISC_ROPE, lane < MISC_ROPE + QK_ROPE)
        dmisc_ref[...] = jnp.where(in_rope, _rope(jnp.where(in_rope, dkr, 0.0), cosf, sinf, -1.0), 0.0)
        dcq, gq = _rms_bwd(cq_ref[...], qnw_ref[...], _dot_nt(dqb_ref[...], wuq_ref[...]))
        dcq_ref[...] = dcq
        _acc_rows(gq_ref, gq, first)
        dckv, gkv = _rms_bwd(ckv_ref[...], kvnw_ref[...], _dot_nt(dkvb_ref[...], wkv_ref[...]))
        dckv_ref[...] = dckv
        _acc_rows(gkv_ref, gkv, first)

    return pl.pallas_call(
        body, name="qkv_bwd", grid=(s // TM,),
        in_specs=[_rows(TM, wide)] * 3 + [_rows(TM, Q_RANK), _rows(TM, KV_RANK), _full((1, Q_RANK)), _full((1, KV_RANK)),
                                          _resident((Q_RANK, wide)), _resident((KV_RANK, 2 * wide)), _rows(TM, HEAD_PAD), _rows(TM, HEAD_PAD)],
        out_specs=[_rows(TM, wide), _rows(TM, 2 * wide), _rows(TM, Q_RANK), _rows(TM, KV_RANK), _rows(TM, HEAD_PAD),
                   _full((1, Q_RANK)), _full((1, KV_RANK))],
        out_shape=[jax.ShapeDtypeStruct((s, wide), BF16), jax.ShapeDtypeStruct((s, 2 * wide), BF16),
                   jax.ShapeDtypeStruct((s, Q_RANK), F32), jax.ShapeDtypeStruct((s, KV_RANK), F32),
                   jax.ShapeDtypeStruct((s, HEAD_PAD), F32), jax.ShapeDtypeStruct((1, Q_RANK), F32),
                   jax.ShapeDtypeStruct((1, KV_RANK), F32)],
        compiler_params=_params(),
    )(dq, dk, dv, cq, ckv, qnw, kvnw, wuq, wkv, cosf, sinf)


def _inproj_bwd(dcq, dckv, dmisc_rope, dmisc_dt, dz, dxbc, h, dh1, nw, win):
    s = h.shape[0]

    def body(dcq_ref, dckv_ref, dma_ref, dmb_ref, dz_ref, dxbc_ref, h_ref, dh1_ref, nw_ref, w_ref, dproj_ref, dh0_ref, gnw_ref):
        dproj_ref[:, 0:768] = dcq_ref[...].astype(BF16)
        dproj_ref[:, 768:1024] = dckv_ref[...].astype(BF16)
        dproj_ref[:, 1024:1152] = (dma_ref[...] + dmb_ref[...]).astype(BF16)
        dproj_ref[:, 1152:1664] = dz_ref[...].astype(BF16)
        dproj_ref[:, 1664:2688] = dxbc_ref[...].astype(BF16)
        du = _dot_nt(dproj_ref[...], w_ref[...])
        dx, gnw = _rms_bwd(h_ref[...], nw_ref[...], du)
        _acc_rows(gnw_ref, gnw, pl.program_id(0) == 0)
        dh0_ref[...] = dh1_ref[...] + dx

    return pl.pallas_call(
        body, name="inproj_bwd", grid=(s // TM,),
        in_specs=[_rows(TM, Q_RANK), _rows(TM, KV_RANK), _rows(TM, HEAD_PAD), _rows(TM, HEAD_PAD), _rows(TM, SSD_INNER),
                  _rows(TM, CONV_DIM), _rows(TM, D_MODEL), _rows(TM, D_MODEL), _full((1, D_MODEL)), _resident((D_MODEL, IN_PAD))],
        out_specs=[_rows(TM, IN_PAD), _rows(TM, D_MODEL), _full((1, D_MODEL))],
        out_shape=[jax.ShapeDtypeStruct((s, IN_PAD), BF16), jax.ShapeDtypeStruct((s, D_MODEL), F32),
                   jax.ShapeDtypeStruct((1, D_MODEL), F32)],
        compiler_params=_params(),
    )(dcq, dckv, dmisc_rope, dmisc_dt, dz, dxbc, h, dh1, nw, win)


def _row_tile(rows, cols):
    cap = max(8, (1 << 18) // max(cols, 128))
    best = None
    for t in range(8, rows + 1, 8):
        if rows % t == 0 and t <= cap:
            best = t
    return best if best is not None else rows


def _adamw(w, g, m, v, name):
    rows, cols = w.shape
    tr = _row_tile(rows, cols)

    def body(w_ref, g_ref, m_ref, v_ref, d_ref, m2_ref, v2_ref):
        gg = g_ref[...]
        m2 = ADAM_B1 * m_ref[...] + (1.0 - ADAM_B1) * gg
        v2 = ADAM_B2 * v_ref[...] + (1.0 - ADAM_B2) * jnp.square(gg)
        m_hat = m2 / (1.0 - ADAM_B1 ** ADAM_STEP)
        v_hat = v2 / (1.0 - ADAM_B2 ** ADAM_STEP)
        d_ref[...] = -ADAM_LR * (m_hat / (jnp.sqrt(v_hat) + ADAM_EPS) + ADAM_WD * w_ref[...])
        m2_ref[...] = m2
        v2_ref[...] = v2

    spec = pl.BlockSpec((tr, cols), lambda i: (i, 0))
    return pl.pallas_call(
        body, name=name, grid=(rows // tr,),
        in_specs=[spec] * 4, out_specs=[spec] * 3,
        out_shape=[jax.ShapeDtypeStruct((rows, cols), F32)] * 3,
    )(w, g, m, v)


_MESH = pl.DeviceIdType.MESH
_ANY = pl.BlockSpec(memory_space=pl.ANY)


def _my_place():
    return lax.axis_index("x"), lax.axis_index("y"), lax.axis_index("c")


def _flip(place, k):
    x, y, c = place
    return (1 - x if k & 4 else x, 1 - y if k & 2 else y, 1 - c if k & 1 else c)


def _block_id(place):
    return 4 * place[0] + 2 * place[1] + place[2]


def _all_gather(shard):
    rows, lanes = shard.shape

    def body(x_ref, out_ref, send_sems, recv_sems, local_sem):
        me = _my_place()
        x, y, c = me
        sibling = (x, y, 1 - c)
        chips = [(1 - x, y), (x, 1 - y), (1 - x, 1 - y)]

        def block(place):
            return out_ref.at[_block_id(place)]

        def copy(k, place, to, src=None):
            return pltpu.make_async_remote_copy(
                src_ref=block(place) if src is None else src, dst_ref=block(place),
                send_sem=send_sems.at[k], recv_sem=recv_sems.at[k], device_id=to, device_id_type=_MESH)

        mine = pltpu.make_async_copy(x_ref, block(me), local_sem)
        mine.start()
        first = [copy(0, me, sibling, src=x_ref)]
        first += [copy(1 + j, me, (*chip, c), src=x_ref) for j, chip in enumerate(chips)]
        for cp in first:
            cp.start()
        passed = [copy(4 + j, (*chip, c), sibling) for j, chip in enumerate(chips)]
        for j, chip in enumerate(chips):
            copy(1 + j, (*chip, c), me).wait_recv()
            passed[j].start()
        copy(0, sibling, me).wait_recv()
        for j, chip in enumerate(chips):
            copy(4 + j, (*chip, 1 - c), me).wait_recv()
        for cp in first + passed:
            cp.wait_send()
        mine.wait()

    return pl.pallas_call(
        body, name="weight_all_gather",
        out_shape=jax.ShapeDtypeStruct((N_DEV, rows, lanes), shard.dtype),
        in_specs=[_ANY], out_specs=_ANY,
        scratch_shapes=[pltpu.SemaphoreType.DMA((7,)), pltpu.SemaphoreType.DMA((7,)), pltpu.SemaphoreType.DMA],
    )(shard)


def _exchange(blocks):
    def body(x_ref, out_ref, send_sems, recv_sems, local_sem):
        me = _my_place()
        my = _block_id(me)
        own = pltpu.make_async_copy(x_ref.at[my], out_ref.at[my], local_sem)
        own.start()
        copies = []
        for k in range(1, N_DEV):
            peer = _flip(me, k)
            cp = pltpu.make_async_remote_copy(
                src_ref=x_ref.at[_block_id(peer)], dst_ref=out_ref.at[my],
                send_sem=send_sems.at[k - 1], recv_sem=recv_sems.at[k - 1], device_id=peer, device_id_type=_MESH)
            cp.start()
            copies.append(cp)
        for cp in copies:
            cp.wait()
        own.wait()

    return pl.pallas_call(
        body, name="grad_exchange",
        out_shape=jax.ShapeDtypeStruct(blocks.shape, blocks.dtype),
        in_specs=[_ANY], out_specs=_ANY,
        scratch_shapes=[pltpu.SemaphoreType.DMA((7,)), pltpu.SemaphoreType.DMA((7,)), pltpu.SemaphoreType.DMA],
    )(blocks)


def _sum_slots(slots):
    _, rows, lanes = slots.shape
    tr = _row_tile(rows, 8 * lanes)

    def body(x_ref, o_ref):
        acc = x_ref[0].astype(F32)
        for i in range(1, N_DEV):
            acc = acc + x_ref[i].astype(F32)
        o_ref[...] = acc

    return pl.pallas_call(
        body, name="grad_sum", grid=(rows // tr,),
        in_specs=[pl.BlockSpec((N_DEV, tr, lanes), lambda i: (0, i, 0))],
        out_specs=pl.BlockSpec((tr, lanes), lambda i: (i, 0)),
        out_shape=jax.ShapeDtypeStruct((rows, lanes), F32),
    )(slots)


def _all_reduce_small(part):
    rows, lanes = part.shape
    vmem = pl.BlockSpec(memory_space=pltpu.VMEM)

    def body(x_ref, gath_ref, sum_ref, send_sems, recv_sems):
        me = _my_place()
        my = _block_id(me)
        gath_ref[my] = x_ref[...]
        copies = []
        for k in range(1, N_DEV):
            cp = pltpu.make_async_remote_copy(
                src_ref=x_ref, dst_ref=gath_ref.at[my], send_sem=send_sems.at[k - 1], recv_sem=recv_sems.at[k - 1],
                device_id=_flip(me, k), device_id_type=_MESH)
            cp.start()
            copies.append(cp)
        for cp in copies:
            cp.wait()
        acc = gath_ref[0]
        for i in range(1, N_DEV):
            acc = acc + gath_ref[i]
        sum_ref[...] = acc

    return pl.pallas_call(
        body, name="small_grad_all_reduce",
        out_shape=[jax.ShapeDtypeStruct((N_DEV, rows, lanes), F32), jax.ShapeDtypeStruct((rows, lanes), F32)],
        in_specs=[vmem], out_specs=[vmem, vmem],
        scratch_shapes=[pltpu.SemaphoreType.DMA((7,)), pltpu.SemaphoreType.DMA((7,))],
    )(part)[1]


_SHARDED = (("w_in", (D_MODEL, IN_PROJ // N_DEV)), ("w_uq", (Q_RANK // N_DEV, Q_RANK)), ("w_ukv", (KV_RANK, HEAD_PAD)),
            ("conv_w", (CONV_W, CONV_DIM // N_DEV)), ("w_out", (D_MODEL // N_DEV, D_MODEL)),
            ("w_up", (D_MODEL, D_FF // N_DEV)), ("w_down", (D_FF // N_DEV, D_MODEL)))
_SMALL = (("pre_mix_norm", D_MODEL), ("q_norm", Q_RANK), ("kv_norm", KV_RANK), ("conv_b", CONV_DIM), ("dt_bias", SSD_HEADS),
          ("a_log", SSD_HEADS), ("d_skip", SSD_HEADS), ("ssd_norm", SSD_INNER), ("post_mix_norm", D_MODEL),
          ("pre_mlp_norm", D_MODEL), ("post_mlp_norm", D_MODEL))
_WEIGHT_ORDER = ("pre_mix_norm", "w_in", "q_norm", "w_uq", "kv_norm", "w_ukv", "conv_w", "conv_b", "dt_bias", "a_log", "d_skip",
                 "ssd_norm", "w_out", "post_mix_norm", "pre_mlp_norm", "w_up", "w_down", "post_mlp_norm")
_FLAT_TILE = 16 * 128


def _pad_flat(flat):
    n = flat.shape[-1]
    padded = -(-n // _FLAT_TILE) * _FLAT_TILE
    return jnp.pad(flat, [(0, 0)] * (flat.ndim - 1) + [(0, padded - n)])


def _pack_shards(w):
    pieces = []
    for l in range(DEPTH):
        for name, _ in _SHARDED:
            a = w[name][l]
            a = lax.bitcast_convert_type(a, BF16) if name == "conv_w" else a.astype(BF16)
            pieces.append(a.reshape(-1))
    return _pad_flat(jnp.concatenate(pieces)).reshape(-1, 128)


def _unpack_gathered(g):
    flat = g.reshape(N_DEV, -1)
    out, off = [], 0
    for l in range(DEPTH):
        d = {}
        for name, shape in _SHARDED:
            n = math.prod(shape) * (2 if name == "conv_w" else 1)
            seg = flat[:, off:off + n]
            off += n
            if name == "conv_w":
                d[name] = lax.bitcast_convert_type(seg.reshape(N_DEV, *shape, 2), F32)
            else:
                d[name] = seg.reshape(N_DEV, *shape)
        out.append(d)
    return out


def _cols(stacked):
    return jnp.transpose(stacked, (1, 0, 2)).reshape(stacked.shape[1], -1)


def _expand_pairs(a, axis_len):
    eye = jnp.eye(2, dtype=a.dtype).reshape(1, 2, 2, 1, 1)
    return a[:, :, None] * eye


def _kernel_weights(sh):
    w_in = _cols(sh["w_in"])
    zeros = lambda n: jnp.zeros((D_MODEL, n), BF16)
    s1, s2, s3, s4, s5 = 768, 1024, 1056, 1568, 2592
    win = jnp.concatenate([w_in[:, :s2], zeros(MISC_ROPE), w_in[:, s2:s3], w_in[:, s5:], zeros(HEAD_PAD - MISC_DT - SSD_HEADS),
                           w_in[:, s3:s5]], axis=1)
    w_uq = sh["w_uq"].reshape(Q_RANK, MLA_HEADS, QK_NOPE + QK_ROPE)
    wuq = jnp.pad(w_uq, ((0, 0), (0, 0), (0, HEAD_PAD - QK_NOPE - QK_ROPE))).reshape(Q_RANK, -1)
    w_ukv = _cols(sh["w_ukv"]).reshape(KV_RANK, MLA_HEADS, QK_NOPE + V_DIM)
    wkn = jnp.pad(w_ukv[..., :QK_NOPE], ((0, 0), (0, 0), (0, HEAD_PAD - QK_NOPE))).reshape(KV_RANK, -1)
    wv = w_ukv[..., QK_NOPE:].reshape(KV_RANK, 4, 2, 1, V_DIM) * jnp.eye(2, dtype=BF16).reshape(1, 1, 2, 2, 1)
    wkv = jnp.concatenate([wkn, wv.reshape(KV_RANK, -1)], axis=1)
    w_out = sh["w_out"].reshape(D_MODEL, D_MODEL)
    watt = w_out[:SSD_INNER].reshape(4, 2, 1, V_DIM, D_MODEL) * jnp.eye(2, dtype=BF16).reshape(1, 2, 2, 1, 1)
    wout = jnp.concatenate([watt.reshape(MLA_HEADS * HEAD_PAD, D_MODEL), w_out[SSD_INNER:]], axis=0)
    return dict(win=win, wuq=wuq, wkv=wkv, wout=wout, wup=sh["w_up"], wdown=sh["w_down"], conv_w=_cols(sh["conv_w"]))


def _shard_grads(g):
    dwin = g["win"]
    s1, s2 = 768, 1024
    m0 = s2
    w_in = jnp.concatenate([dwin[:, :s2], dwin[:, m0 + MISC_ROPE:m0 + MISC_ROPE + QK_ROPE], dwin[:, 1152:2688],
                            dwin[:, m0 + MISC_DT:m0 + MISC_DT + SSD_HEADS]], axis=1)
    out = {"w_in": jnp.transpose(w_in.reshape(D_MODEL, N_DEV, -1), (1, 0, 2))}
    w_uq = g["wuq"].reshape(Q_RANK, MLA_HEADS, HEAD_PAD)[..., :QK_NOPE + QK_ROPE].reshape(Q_RANK, Q_RANK)
    out["w_uq"] = w_uq.reshape(N_DEV, Q_RANK // N_DEV, Q_RANK)
    wide = MLA_HEADS * HEAD_PAD
    kn = g["wkv"][:, :wide].reshape(KV_RANK, MLA_HEADS, HEAD_PAD)[..., :QK_NOPE]
    ve = g["wkv"][:, wide:].reshape(KV_RANK, 4, 2, 2, V_DIM)
    vv = jnp.stack([ve[:, :, 0, 0], ve[:, :, 1, 1]], axis=2).reshape(KV_RANK, MLA_HEADS, V_DIM)
    out["w_ukv"] = jnp.transpose(jnp.concatenate([kn, vv], axis=-1), (1, 0, 2))
    out["conv_w"] = jnp.transpose(g["conv_w"].reshape(CONV_W, N_DEV, -1), (1, 0, 2))
    ae = g["wout_att"].reshape(4, 2, 2, V_DIM, D_MODEL)
    att = jnp.stack([ae[:, 0, 0], ae[:, 1, 1]], axis=1).reshape(SSD_INNER, D_MODEL)
    out["w_out"] = jnp.concatenate([att, g["wout_ssd"]], axis=0).reshape(N_DEV, D_MODEL // N_DEV, D_MODEL)
    out["w_up"] = g["wup"]
    out["w_down"] = g["wdown"]
    return out


def _pack_grad_blocks(per_layer):
    pieces = [per_layer[l][name].reshape(N_DEV, -1) for l in range(DEPTH) for name, _ in _SHARDED]
    flat = _pad_flat(jnp.concatenate(pieces, axis=1).astype(BF16))
    return flat.reshape(N_DEV, -1, 128)


def _unpack_grad_shard(flat2d):
    flat = flat2d.reshape(-1)
    out, off = {name: [] for name, _ in _SHARDED}, 0
    for l in range(DEPTH):
        for name, shape in _SHARDED:
            n = math.prod(shape)
            out[name].append(flat[off:off + n].reshape(shape))
            off += n
    return {name: jnp.stack(v) for name, v in out.items()}


def _small_rows(n):
    return -(-n // 128)


def _pack_small(vals):
    rows = []
    for l in range(DEPTH):
        for name, n in _SMALL:
            r = _small_rows(n)
            rows.append(jnp.pad(vals[name][l].reshape(-1), (0, r * 128 - n)).reshape(r, 128))
    total = sum(r.shape[0] for r in rows)
    rows.append(jnp.zeros((-total % 8, 128), F32))
    return jnp.concatenate(rows, axis=0)


def _unpack_small(packed):
    out, off = {name: [] for name, _ in _SMALL}, 0
    for l in range(DEPTH):
        for name, n in _SMALL:
            r = _small_rows(n)
            out[name].append(packed[off:off + r].reshape(-1)[:n])
            off += r
    return {name: jnp.stack(v) for name, v in out.items()}


def _lane_rows(vec8):
    return jnp.repeat(vec8, SSD_P).reshape(1, SSD_INNER)


def _layer_fwd(h, kw, sm, l, cosf, sinf, consts):
    row = lambda name: sm[name][l].reshape(1, -1)
    t = {}
    t["h0"] = h
    t["ub"], t["cq"], t["ckv"], t["misc"], t["z"], t["xraw"] = _inproj_fwd(h, row("pre_mix_norm"), kw["win"])
    t["cqn"], t["ckvn"], t["q"], t["k"], t["v"] = _qkv_fwd(t["cq"], t["ckv"], t["misc"], row("q_norm"), row("kv_norm"),
                                                         kw["wuq"], kw["wkv"], cosf, sinf)
    t["oe"], t["lse"] = _attn_fwd(t["q"], t["k"], t["v"])
    t["dtb"] = _lane_rows(sm["dt_bias"][l])
    t["a_exp"] = _lane_rows(-jnp.exp(sm["a_log"][l]))
    t["d_exp"] = _lane_rows(sm["d_skip"][l])
    t["c"], t["prev"], t["ypre"], t["yssd"] = _ssd_fwd(t["xraw"], t["misc"], t["z"], kw["conv_w"], row("conv_b"), t["dtb"],
                                                     t["a_exp"], t["d_exp"], row("ssd_norm"), consts)
    t["mixed"], t["h1"] = _outproj_fwd(t["oe"], t["yssd"], kw["wout"], h, row("post_mix_norm"))
    t["mb"], t["d"], h2 = _mlp_fwd(t["h1"], row("pre_mlp_norm"), kw["wup"], kw["wdown"], row("post_mlp_norm"))
    return h2, t


def _layer_bwd(dh2, t, kw, sm, l, cosf, sinf, consts):
    row = lambda name: sm[name][l].reshape(1, -1)
    g, gs = {}, {}
    dh1, dab, rb, ddb, gs["post_mlp_norm"], gs["pre_mlp_norm"] = _mlp_bwd(
        dh2, t["d"], t["h1"], t["mb"], row("pre_mlp_norm"), kw["wup"], kw["wdown"], row("post_mlp_norm"))
    g["wup"] = _matmul_tn_stacked(t["mb"], dab, f"dw_up_{l}", a_stacked=False)
    g["wdown"] = _matmul_tn_stacked(rb, ddb, f"dw_down_{l}", a_stacked=True)
    dmixb, doe, dyssd, gs["post_mix_norm"] = _outproj_bwd(dh1, t["mixed"], row("post_mix_norm"), kw["wout"])
    g["wout_att"] = _matmul_tn(t["oe"], dmixb, f"dw_out_att_{l}")
    g["wout_ssd"] = _matmul_tn(t["yssd"], dmixb, f"dw_out_ssd_{l}")
    dz, dxraw, dmisc_dt, gs["ssd_norm"], gd, galog, gdtb, g["conv_w"], gs["conv_b"] = _ssd_bwd(
        dyssd, t["ypre"], t["z"], t["c"], t["xraw"], t["misc"], t["prev"], kw["conv_w"], t["dtb"], t["a_exp"], t["d_exp"],
        row("ssd_norm"), consts)
    gs["d_skip"] = jnp.sum(gd.reshape(SSD_HEADS, SSD_P), axis=1)
    gs["a_log"] = galog[0, MISC_DT:MISC_DT + SSD_HEADS]
    gs["dt_bias"] = gdtb[0, MISC_DT:MISC_DT + SSD_HEADS]
    dq, dk, dv = _attn_bwd(t["q"], t["k"], t["v"], doe, t["lse"], _attn_delta(t["oe"], doe))
    dqb, dkvb, dcq, dckv, dmisc_rope, gs["q_norm"], gs["kv_norm"] = _qkv_bwd(
        dq, dk, dv, t["cq"], t["ckv"], row("q_norm"), row("kv_norm"), kw["wuq"], kw["wkv"], cosf, sinf)
    g["wuq"] = _matmul_tn(t["cqn"], dqb, f"dw_uq_{l}")
    g["wkv"] = _matmul_tn(t["ckvn"], dkvb, f"dw_kv_{l}")
    dprojb, dh0, gs["pre_mix_norm"] = _inproj_bwd(dcq, dckv, dmisc_rope, dmisc_dt, dz, dxraw, t["h0"], dh1,
                                                  row("pre_mix_norm"), kw["win"])
    g["win"] = _matmul_tn(t["ub"], dprojb, f"dw_in_{l}")
    return dh0, g, {k: v.reshape(-1) for k, v in gs.items()}


def _local_step(x, positions, kws, sm, target):
    inv_freq = ROPE_THETA ** (-jnp.arange(0, QK_ROPE, 2, dtype=F32) / QK_ROPE)
    invf = jnp.zeros((HEAD_PAD,), F32).at[MISC_ROPE:MISC_ROPE + QK_ROPE].set(jnp.concatenate([inv_freq, inv_freq]))
    cosf, sinf = _rope_tables(positions.reshape(-1, 1), invf.reshape(1, HEAD_PAD))
    consts = _ssd_consts()
    h, saved = x, []
    for l in range(DEPTH):
        h, t = _layer_fwd(h, kws[l], sm, l, cosf, sinf, consts)
        saved.append(t)
    dh, loss = _loss_grad(h, target)
    grads, small = [None] * DEPTH, [None] * DEPTH
    for l in reversed(range(DEPTH)):
        dh, grads[l], small[l] = _layer_bwd(dh, saved[l], kws[l], sm, l, cosf, sinf, consts)
    return loss[0, 0], dh, grads, small


def kernel(x, positions, pre_mix_norm, w_in, q_norm, w_uq, kv_norm, w_ukv, conv_w, conv_b, dt_bias, a_log, d_skip, ssd_norm, w_out, post_mix_norm, pre_mlp_norm, w_up, w_down, post_mlp_norm, loss_target, m_pre_mix_norm, m_w_in, m_q_norm, m_w_uq, m_kv_norm, m_w_ukv, m_conv_w, m_conv_b, m_dt_bias, m_a_log, m_d_skip, m_ssd_norm, m_w_out, m_post_mix_norm, m_pre_mlp_norm, m_w_up, m_w_down, m_post_mlp_norm, v_pre_mix_norm, v_w_in, v_q_norm, v_w_uq, v_kv_norm, v_w_ukv, v_conv_w, v_conv_b, v_dt_bias, v_a_log, v_d_skip, v_ssd_norm, v_w_out, v_post_mix_norm, v_pre_mlp_norm, v_w_up, v_w_down, v_post_mlp_norm):
    w = dict(pre_mix_norm=pre_mix_norm, w_in=w_in, q_norm=q_norm, w_uq=w_uq, kv_norm=kv_norm, w_ukv=w_ukv, conv_w=conv_w,
             conv_b=conv_b, dt_bias=dt_bias, a_log=a_log, d_skip=d_skip, ssd_norm=ssd_norm, w_out=w_out,
             post_mix_norm=post_mix_norm, pre_mlp_norm=pre_mlp_norm, w_up=w_up, w_down=w_down, post_mlp_norm=post_mlp_norm)
    m = dict(pre_mix_norm=m_pre_mix_norm, w_in=m_w_in, q_norm=m_q_norm, w_uq=m_w_uq, kv_norm=m_kv_norm, w_ukv=m_w_ukv,
             conv_w=m_conv_w, conv_b=m_conv_b, dt_bias=m_dt_bias, a_log=m_a_log, d_skip=m_d_skip, ssd_norm=m_ssd_norm,
             w_out=m_w_out, post_mix_norm=m_post_mix_norm, pre_mlp_norm=m_pre_mlp_norm, w_up=m_w_up, w_down=m_w_down,
             post_mlp_norm=m_post_mlp_norm)
    v = dict(pre_mix_norm=v_pre_mix_norm, w_in=v_w_in, q_norm=v_q_norm, w_uq=v_w_uq, kv_norm=v_kv_norm, w_ukv=v_w_ukv,
             conv_w=v_conv_w, conv_b=v_conv_b, dt_bias=v_dt_bias, a_log=v_a_log, d_skip=v_d_skip, ssd_norm=v_ssd_norm,
             w_out=v_w_out, post_mix_norm=v_post_mix_norm, pre_mlp_norm=v_pre_mlp_norm, w_up=v_w_up, w_down=v_w_down,
             post_mlp_norm=v_post_mlp_norm)
    sm = {name: w[name] for name, _ in _SMALL}

    gathered = _all_gather(_pack_shards(w))
    kws = [_kernel_weights(sh) for sh in _unpack_gathered(gathered)]
    loss_part, dx, grads, small = _local_step(x[0], positions[0], kws, sm, loss_target[0])

    blocks = _pack_grad_blocks([_shard_grads(g) for g in grads])
    g_sharded = _unpack_grad_shard(_sum_slots(_exchange(blocks)))
    g_small = _unpack_small(_all_reduce_small(_pack_small({name: jnp.stack([small[l][name] for l in range(DEPTH)])
                                                           for name, _ in _SMALL})))
    loss = lax.psum(loss_part, ("x", "y", "c"))

    grad, delta, new_m, new_v = {}, {}, {}, {}
    for name, _ in _SHARDED:
        shape = w[name].shape
        flat = lambda a: a.reshape(-1, shape[-1])
        d_, m_, v_ = _adamw(flat(w[name]), flat(g_sharded[name]), flat(m[name]), flat(v[name]), f"adamw_{name}")
        grad[name] = g_sharded[name]
        delta[name], new_m[name], new_v[name] = d_.reshape(shape), m_.reshape(shape), v_.reshape(shape)
    pk = lambda d: _pack_small({name: d[name] for name, _ in _SMALL})
    d_, m_, v_ = _adamw(pk(w), pk(g_small), pk(m), pk(v), "adamw_small")
    for dst, packed in ((delta, d_), (new_m, m_), (new_v, v_)):
        dst.update(_unpack_small(packed))
    grad.update(g_small)

    outs = [loss, dx[None]]
    for d in (grad, delta, new_m, new_v):
        outs += [d[name] for name in _WEIGHT_ORDER]
    return tuple(outs)
```

```python
import functools
import math

import jax
import jax.numpy as jnp
import numpy as np
from jax import lax
from jax.experimental import pallas as pl
from jax.experimental.pallas import tpu as pltpu

F32 = jnp.float32
BF16 = jnp.bfloat16
HI = lax.Precision.HIGHEST

D_MODEL = 1024
DEPTH = 2
N_DEV = 8
CHUNK = 64
EPS = 1e-6
MLA_HEADS = 8
QK_NOPE = 64
QK_ROPE = 32
V_DIM = 64
Q_RANK = 768
KV_RANK = 256
ROPE_THETA = 10000.0
SSD_HEADS = 8
SSD_P = 64
SSD_INNER = 512
SSD_GROUPS = 2
SSD_N = 128
CONV_W = 4
CONV_DIM = 1024
D_FF = 4096
IN_PROJ = 2600
HEAD_PAD = 128
IN_PAD = 2688
MISC_ROPE = 64
MISC_DT = 96
ATT_SCALE = (QK_NOPE + QK_ROPE) ** -0.5

ADAM_LR = 0.001
ADAM_B1 = 0.9
ADAM_B2 = 0.999
ADAM_EPS = 1e-08
ADAM_WD = 0.01
ADAM_STEP = 10

TM = 512
TQ = 256
ATT_T = 512
SSD_ROWS = 256
TK_DW = 2048
VMEM_LIMIT = 56 * 1024 * 1024

_NT = (((1,), (1,)), ((), ()))
_TN = (((0,), (0,)), ((), ()))


def _params(**kw):
    return pltpu.CompilerParams(vmem_limit_bytes=VMEM_LIMIT, **kw)


def _dot(a, b, precision=None):
    return jnp.dot(a, b, preferred_element_type=F32, precision=precision)


def _dot_nt(a, b, precision=None):
    return lax.dot_general(a, b, _NT, preferred_element_type=F32, precision=precision)


def _dot_tn(a, b, precision=None):
    return lax.dot_general(a, b, _TN, preferred_element_type=F32, precision=precision)


def _split3(x):
    hi = x.astype(BF16)
    r = x - hi.astype(F32)
    mid = r.astype(BF16)
    return hi, mid, (r - mid.astype(F32)).astype(BF16)


def _dot01(x, m01, dot=_dot, left=False):
    parts = [dot(m01, p) if left else dot(p, m01) for p in _split3(x)]
    return parts[0] + parts[1] + parts[2]


def _full(shape):
    n = len(shape)
    return pl.BlockSpec(shape, lambda *_: (0,) * n)


def _resident(shape):
    n = len(shape)
    return pl.BlockSpec(shape, lambda *_: (0,) * n, pipeline_mode=pl.Buffered(1))


def _rows(tm, width):
    return pl.BlockSpec((tm, width), lambda i: (i, 0))


def _rms_fwd(x, w):
    r = lax.rsqrt(jnp.mean(x * x, axis=-1, keepdims=True) + EPS)
    return (x * r) * w


def _rms_bwd(x, w, dy):
    r = lax.rsqrt(jnp.mean(x * x, axis=-1, keepdims=True) + EPS)
    xh = x * r
    dxn = dy * w
    dx = r * (dxn - xh * jnp.mean(dxn * xh, axis=-1, keepdims=True))
    return dx, dy * xh


def _acc_rows(ref, val, first):
    s = jnp.sum(val, axis=0, keepdims=True)

    @pl.when(first)
    def _():
        ref[...] = s

    @pl.when(jnp.logical_not(first))
    def _():
        ref[...] += s


def _rope(t, cosf, sinf, sign):
    lane = lax.broadcasted_iota(jnp.int32, t.shape, 1)
    rot = jnp.where(lane < MISC_ROPE + QK_ROPE // 2, -pltpu.roll(t, HEAD_PAD - QK_ROPE // 2, 1), pltpu.roll(t, QK_ROPE // 2, 1))
    return t * cosf + sign * (rot * sinf)


def _rope_tables(pos, invf):
    s = pos.shape[0]

    def body(pos_ref, invf_ref, cos_ref, sin_ref):
        ang = pos_ref[...].astype(F32) * invf_ref[...]
        cos_ref[...] = jnp.cos(ang)
        sin_ref[...] = jnp.sin(ang)

    return pl.pallas_call(
        body, name="rope_tables", grid=(s // TM,),
        in_specs=[_rows(TM, 1), _full((1, HEAD_PAD))],
        out_specs=[_rows(TM, HEAD_PAD), _rows(TM, HEAD_PAD)],
        out_shape=[jax.ShapeDtypeStruct((s, HEAD_PAD), F32)] * 2,
    )(pos, invf)


def _inproj_fwd(h, nw, win):
    s = h.shape[0]

    def body(h_ref, nw_ref, w_ref, ub_ref, cq_ref, ckv_ref, misc_ref, z_ref, xbc_ref):
        ub = _rms_fwd(h_ref[...], nw_ref[...]).astype(BF16)
        ub_ref[...] = ub
        proj = _dot(ub, w_ref[...])
        cq_ref[...] = proj[:, 0:768]
        ckv_ref[...] = proj[:, 768:1024]
        misc_ref[...] = proj[:, 1024:1152]
        z_ref[...] = proj[:, 1152:1664]
        xbc_ref[...] = proj[:, 1664:2688]

    widths = (768, 256, 128, 512, 1024)
    return pl.pallas_call(
        body, name="inproj_fwd", grid=(s // TM,),
        in_specs=[_rows(TM, D_MODEL), _full((1, D_MODEL)), _resident((D_MODEL, IN_PAD))],
        out_specs=[_rows(TM, D_MODEL)] + [_rows(TM, w) for w in widths],
        out_shape=[jax.ShapeDtypeStruct((s, D_MODEL), BF16)] + [jax.ShapeDtypeStruct((s, w), F32) for w in widths],
        compiler_params=_params(),
    )(h, nw, win)


def _qkv_fwd(cq, ckv, misc, qnw, kvnw, wuq, wkv, cosf, sinf):
    s = cq.shape[0]

    def body(cq_ref, ckv_ref, misc_ref, qnw_ref, kvnw_ref, wuq_ref, wkv_ref, cos_ref, sin_ref,
             cqn_ref, ckvn_ref, q_ref, k_ref, v_ref):
        cosf, sinf = cos_ref[...], sin_ref[...]
        cqn = _rms_fwd(cq_ref[...], qnw_ref[...]).astype(BF16)
        cqn_ref[...] = cqn
        q = _dot(cqn, wuq_ref[...])
        ckvn = _rms_fwd(ckv_ref[...], kvnw_ref[...]).astype(BF16)
        ckvn_ref[...] = ckvn
        kv = _dot(ckvn, wkv_ref[...])
        m = misc_ref[...]
        lane = lax.broadcasted_iota(jnp.int32, m.shape, 1)
        in_rope = jnp.logical_and(lane >= MISC_ROPE, lane < MISC_ROPE + QK_ROPE)
        kr = jnp.where(in_rope, _rope(m, cosf, sinf, 1.0), 0.0)
        for hd in range(MLA_HEADS):
            cols = slice(hd * HEAD_PAD, (hd + 1) * HEAD_PAD)
            q_ref[:, cols] = _rope(q[:, cols], cosf, sinf, 1.0).astype(BF16)
            k_ref[:, cols] = (kv[:, cols] + kr).astype(BF16)
        v_ref[...] = kv[:, MLA_HEADS * HEAD_PAD:].astype(BF16)

    wide = MLA_HEADS * HEAD_PAD
    return pl.pallas_call(
        body, name="qkv_fwd", grid=(s // TM,),
        in_specs=[_rows(TM, Q_RANK), _rows(TM, KV_RANK), _rows(TM, HEAD_PAD), _full((1, Q_RANK)), _full((1, KV_RANK)),
                  _resident((Q_RANK, wide)), _resident((KV_RANK, 2 * wide)), _rows(TM, HEAD_PAD), _rows(TM, HEAD_PAD)],
        out_specs=[_rows(TM, Q_RANK), _rows(TM, KV_RANK), _rows(TM, wide), _rows(TM, wide), _rows(TM, wide)],
        out_shape=[jax.ShapeDtypeStruct((s, Q_RANK), BF16), jax.ShapeDtypeStruct((s, KV_RANK), BF16)]
        + [jax.ShapeDtypeStruct((s, wide), BF16)] * 3,
        compiler_params=_params(),
    )(cq, ckv, misc, qnw, kvnw, wuq, wkv, cosf, sinf)


def _chunk_mask(t, keys_on_rows=False):
    row = lax.broadcasted_iota(jnp.int32, (t, t), 0) // CHUNK
    col = lax.broadcasted_iota(jnp.int32, (t, t), 1) // CHUNK
    return (row <= col) if keys_on_rows else (col <= row)


def _attn_fwd(q, k, v):
    s = q.shape[0]
    t = ATT_T
    nq = s // t
    pair = 2 * HEAD_PAD

    def body(q_ref, k_ref, v_ref, o_ref, lse_ref, m_s, l_s, acc_s):
        qi = pl.program_id(1)
        m_s[...] = jnp.full(m_s.shape, -jnp.inf, F32)
        l_s[...] = jnp.zeros(l_s.shape, F32)
        acc_s[...] = jnp.zeros(acc_s.shape, F32)

        def step(kb, masked):
            r0 = pl.multiple_of(kb * t, t)
            for hh in range(2):
                cols = slice(hh * HEAD_PAD, (hh + 1) * HEAD_PAD)
                sc = _dot_nt(q_ref[:, cols], k_ref[pl.ds(r0, t), cols]) * ATT_SCALE
                if masked:
                    sc = jnp.where(_chunk_mask(t), sc, -jnp.inf)
                m_old = m_s[hh]
                m_new = jnp.maximum(m_old, jnp.max(sc, axis=-1, keepdims=True))
                alpha = jnp.exp(m_old - m_new)
                p = jnp.exp(sc - jnp.tile(m_new, (1, t // HEAD_PAD)))
                l_s[hh] = alpha * l_s[hh] + jnp.sum(p, axis=-1, keepdims=True)
                acc_s[hh] = alpha * acc_s[hh] + _dot(p.astype(BF16), v_ref[pl.ds(r0, t), cols])
                m_s[hh] = m_new

        def loop(kb, c):
            step(kb, False)
            return c

        lax.fori_loop(0, qi, loop, 0)
        step(qi, True)
        for hh in range(2):
            cols = slice(hh * HEAD_PAD, (hh + 1) * HEAD_PAD)
            o_ref[:, cols] = (acc_s[hh] / l_s[hh]).astype(BF16)
            lse_ref[hh] = (m_s[hh] + jnp.log(l_s[hh])).T[0:8, :]

    return pl.pallas_call(
        body, name="attn_fwd", grid=(MLA_HEADS // 2, nq),
        in_specs=[pl.BlockSpec((t, pair), lambda h, i: (i, h)),
                  pl.BlockSpec((s, pair), lambda h, i: (0, h)),
                  pl.BlockSpec((s, pair), lambda h, i: (0, h))],
        out_specs=[pl.BlockSpec((t, pair), lambda h, i: (i, h)),
                   pl.BlockSpec((2, 8, t), lambda h, i: (h, 0, i))],
        out_shape=[jax.ShapeDtypeStruct((s, MLA_HEADS * HEAD_PAD), BF16), jax.ShapeDtypeStruct((MLA_HEADS, 8, s), F32)],
        scratch_shapes=[pltpu.VMEM((2, t, HEAD_PAD), F32), pltpu.VMEM((2, t, HEAD_PAD), F32), pltpu.VMEM((2, t, HEAD_PAD), F32)],
        compiler_params=_params(),
    )(q, k, v)


def _ssd_consts():
    emisc = np.zeros((HEAD_PAD, SSD_INNER), np.float32)
    for hd in range(SSD_HEADS):
        emisc[MISC_DT + hd, hd * SSD_P:(hd + 1) * SSD_P] = 1.0
    idx = np.arange(CHUNK)
    tri = (idx[:, None] >= idx[None, :]).astype(np.float32)
    return tuple(jnp.asarray(m, BF16) for m in (emisc, emisc.T.copy(), tri, tri.T.copy()))


def _ssd_chunk_common(cc, misc, emisc, tri, trit, dtb, a_exp):
    xa = cc * jax.nn.sigmoid(cc)
    dtr = _dot01(misc, emisc) + dtb
    dt = jax.nn.softplus(dtr)
    a = dt * a_exp
    acs = _dot01(a, tri, left=True)
    acs_t = _dot01(a, trit, dot=_dot_tn)
    alast = acs[CHUNK - 1:CHUNK, :]
    return xa, dtr, dt, acs, acs_t, alast


def _decay(acs, acs_t, hd):
    row = lax.broadcasted_iota(jnp.int32, (CHUNK, CHUNK), 0)
    col = lax.broadcasted_iota(jnp.int32, (CHUNK, CHUNK), 1)
    diff = acs[:, hd * SSD_P:hd * SSD_P + 1] - acs_t[hd * SSD_P:hd * SSD_P + 1, :]
    return jnp.exp(jnp.where(row >= col, diff, -jnp.inf))


def _half_mask(hh):
    lane = lax.broadcasted_iota(jnp.int32, (CHUNK, 2 * SSD_P), 1)
    return (lane >= SSD_P) if hh else (lane < SSD_P)


def _gate_norm(y, zz, nw):
    yz = y * (zz * jax.nn.sigmoid(zz))
    outs, rs = [], []
    half = SSD_INNER // SSD_GROUPS
    for g in range(SSD_GROUPS):
        yg = yz[:, g * half:(g + 1) * half]
        r = lax.rsqrt(jnp.mean(yg * yg, axis=-1, keepdims=True) + EPS)
        outs.append(yg * r)
        rs.append(r)
    return yz, jnp.concatenate(outs, axis=1), rs


def _ssd_fwd(xraw, misc, z, cw, cb, dtb, a_exp, d_exp, nw, consts):
    s = xraw.shape[0]
    nb = s // SSD_ROWS
    ncb = SSD_ROWS // CHUNK
    emisc, _, tri, trit = consts

    def body(x_ref, misc_ref, z_ref, cw_ref, cb_ref, dtb_ref, a_ref, d_ref, nw_ref, emisc_ref, tri_ref, trit_ref,
             c_ref, prev_ref, ypre_ref, yssd_ref, tail_s, state_s):
        i = pl.program_id(0)

        @pl.when(i == 0)
        def _():
            tail_s[...] = jnp.zeros(tail_s.shape, F32)
            state_s[...] = jnp.zeros(state_s.shape, F32)

        x = x_ref[...]
        xext = jnp.concatenate([tail_s[...], x], axis=0)
        acc = x * cw_ref[CONV_W - 1:CONV_W, :] + cb_ref[...]
        for j in range(1, CONV_W):
            acc = acc + pltpu.roll(xext, j, 0)[8:, :] * cw_ref[CONV_W - 1 - j:CONV_W - j, :]
        tail_s[...] = x[SSD_ROWS - 8:, :]
        c_ref[...] = acc

        def chunk(ci, carry):
            r0 = pl.multiple_of(ci * CHUNK, CHUNK)
            xa, _, dt, acs, acs_t, alast = _ssd_chunk_common(
                c_ref[pl.ds(r0, CHUNK), :], misc_ref[pl.ds(r0, CHUNK), :], emisc_ref[...], tri_ref[...], trit_ref[...],
                dtb_ref[...], a_ref[...])
            xs = xa[:, :SSD_INNER]
            xdt = xs * dt
            prev = state_s[...]
            prev_ref[ci] = prev
            wgt = (xdt * jnp.exp(alast - acs)).astype(BF16)
            e = jnp.exp(acs)
            ys, new_states = [], []
            for g in range(SSD_GROUPS):
                bm = xa[:, SSD_INNER + g * SSD_N:SSD_INNER + (g + 1) * SSD_N].astype(BF16)
                cm = xa[:, SSD_INNER + SSD_GROUPS * SSD_N + g * SSD_N:SSD_INNER + SSD_GROUPS * SSD_N + (g + 1) * SSD_N].astype(BF16)
                cb_g = _dot_nt(cm, bm)
                gl = slice(g * 256, (g + 1) * 256)
                new_states.append(_dot_tn(bm, wgt[:, gl]))
                yoff = _dot(cm, prev[:, gl].astype(BF16)) * e[:, gl]
                for jj in range(2):
                    pair = 2 * g + jj
                    pl_ = slice(pair * 128, (pair + 1) * 128)
                    xp = xdt[:, pl_]
                    yp = yoff[:, jj * 128:(jj + 1) * 128]
                    for hh in range(2):
                        sc = (cb_g * _decay(acs, acs_t, 2 * pair + hh)).astype(BF16)
                        yp = yp + _dot(sc, jnp.where(_half_mask(hh), xp, 0.0).astype(BF16))
                    ys.append(yp)
            y = jnp.concatenate(ys, axis=1) + d_ref[...] * xs
            state_s[...] = prev * jnp.exp(alast) + jnp.concatenate(new_states, axis=1)
            ypre_ref[pl.ds(r0, CHUNK), :] = y
            _, yn, _ = _gate_norm(y, z_ref[pl.ds(r0, CHUNK), :], None)
            yssd_ref[pl.ds(r0, CHUNK), :] = (yn * nw_ref[...]).astype(BF16)
            return carry

        lax.fori_loop(0, ncb, chunk, 0)

    return pl.pallas_call(
        body, name="ssd_fwd", grid=(nb,),
        in_specs=[_rows(SSD_ROWS, CONV_DIM), _rows(SSD_ROWS, HEAD_PAD), _rows(SSD_ROWS, SSD_INNER),
                  _full((CONV_W, CONV_DIM)), _full((1, CONV_DIM)), _full((1, SSD_INNER)), _full((1, SSD_INNER)),
                  _full((1, SSD_INNER)), _full((1, SSD_INNER)), _full((HEAD_PAD, SSD_INNER)), _full((CHUNK, CHUNK)),
                  _full((CHUNK, CHUNK))],
        out_specs=[_rows(SSD_ROWS, CONV_DIM), pl.BlockSpec((ncb, SSD_N, SSD_INNER), lambda i: (i, 0, 0)),
                   _rows(SSD_ROWS, SSD_INNER), _rows(SSD_ROWS, SSD_INNER)],
        out_shape=[jax.ShapeDtypeStruct((s, CONV_DIM), F32), jax.ShapeDtypeStruct((s // CHUNK, SSD_N, SSD_INNER), F32),
                   jax.ShapeDtypeStruct((s, SSD_INNER), F32), jax.ShapeDtypeStruct((s, SSD_INNER), BF16)],
        scratch_shapes=[pltpu.VMEM((8, CONV_DIM), F32), pltpu.VMEM((SSD_N, SSD_INNER), F32)],
        compiler_params=_params(),
    )(xraw, misc, z, cw, cb, dtb, a_exp, d_exp, nw, emisc, tri, trit)


def _outproj_fwd(oe, yssd, wout, h, nw):
    s = h.shape[0]
    wide = MLA_HEADS * HEAD_PAD

    def body(oe_ref, y_ref, w_ref, h_ref, nw_ref, mixed_ref, h1_ref):
        mixed = _dot(oe_ref[...], w_ref[0:wide, :]) + _dot(y_ref[...], w_ref[wide:, :])
        mixed_ref[...] = mixed
        h1_ref[...] = h_ref[...] + _rms_fwd(mixed, nw_ref[...])

    return pl.pallas_call(
        body, name="outproj_fwd", grid=(s // TM,),
        in_specs=[_rows(TM, wide), _rows(TM, SSD_INNER), _resident((wide + SSD_INNER, D_MODEL)), _rows(TM, D_MODEL),
                  _full((1, D_MODEL))],
        out_specs=[_rows(TM, D_MODEL), _rows(TM, D_MODEL)],
        out_shape=[jax.ShapeDtypeStruct((s, D_MODEL), F32)] * 2,
        compiler_params=_params(),
    )(oe, yssd, wout, h, nw)


def _mlp_fwd(h1, prew, wup, wdown, postw):
    s = h1.shape[0]
    fb = D_FF // N_DEV

    def body(h_ref, prew_ref, up_ref, down_ref, postw_ref, mb_ref, d_ref, h2_ref):
        hh = h_ref[...]
        mb = _rms_fwd(hh, prew_ref[...]).astype(BF16)
        mb_ref[...] = mb
        d = jnp.zeros((TM, D_MODEL), F32)
        for j in range(N_DEV):
            a = _dot(mb, up_ref[j])
            r = jnp.square(jnp.maximum(a, 0.0)).astype(BF16)
            d = d + _dot(r, down_ref[j])
        d_ref[...] = d
        h2_ref[...] = hh + _rms_fwd(d, postw_ref[...])

    return pl.pallas_call(
        body, name="mlp_fwd", grid=(s // TM,),
        in_specs=[_rows(TM, D_MODEL), _full((1, D_MODEL)), _resident((N_DEV, D_MODEL, fb)), _resident((N_DEV, fb, D_MODEL)),
                  _full((1, D_MODEL))],
        out_specs=[_rows(TM, D_MODEL)] * 3,
        out_shape=[jax.ShapeDtypeStruct((s, D_MODEL), BF16), jax.ShapeDtypeStruct((s, D_MODEL), F32),
                   jax.ShapeDtypeStruct((s, D_MODEL), F32)],
        compiler_params=_params(),
    )(h1, prew, wup, wdown, postw)


def _loss_grad(h, target):
    s = h.shape[0]

    def body(h_ref, t_ref, dh_ref, loss_ref):
        diff = h_ref[...] - t_ref[...]
        dh_ref[...] = diff * (1.0 / D_MODEL)
        part = 0.5 * jnp.sum(jnp.mean(diff * diff, axis=-1, keepdims=True), axis=0, keepdims=True)
        _acc_rows(loss_ref, part, pl.program_id(0) == 0)

    return pl.pallas_call(
        body, name="loss_grad", grid=(s // TM,),
        in_specs=[_rows(TM, D_MODEL)] * 2,
        out_specs=[_rows(TM, D_MODEL), _full((1, 1))],
        out_shape=[jax.ShapeDtypeStruct((s, D_MODEL), F32), jax.ShapeDtypeStruct((1, 1), F32)],
    )(h, target)


def _mlp_bwd(dh2, d, h1, mb, prew, wup, wdown, postw):
    s = dh2.shape[0]
    fb = D_FF // N_DEV
    tm = TM // 2

    def body(dh2_ref, d_ref, h1_ref, mb_ref, prew_ref, up_ref, down_ref, postw_ref,
             dh1_ref, da_ref, r_ref, dd_ref, gpost_ref, gpre_ref):
        first = pl.program_id(0) == 0
        dh2 = dh2_ref[...]
        dd, gpost = _rms_bwd(d_ref[...], postw_ref[...], dh2)
        _acc_rows(gpost_ref, gpost, first)
        ddb = dd.astype(BF16)
        dd_ref[...] = ddb
        mb = mb_ref[...]
        dm = jnp.zeros((tm, D_MODEL), F32)
        for j in range(N_DEV):
            a = jnp.maximum(_dot(mb, up_ref[j]), 0.0)
            r_ref[j] = jnp.square(a).astype(BF16)
            da = (_dot_nt(ddb, down_ref[j]) * (2.0 * a)).astype(BF16)
            da_ref[j] = da
            dm = dm + _dot_nt(da, up_ref[j])
        dx, gpre = _rms_bwd(h1_ref[...], prew_ref[...], dm)
        _acc_rows(gpre_ref, gpre, first)
        dh1_ref[...] = dh2 + dx

    stacked = pl.BlockSpec((N_DEV, tm, fb), lambda i: (0, i, 0))
    return pl.pallas_call(
        body, name="mlp_bwd", grid=(s // tm,),
        in_specs=[_rows(tm, D_MODEL)] * 4 + [_full((1, D_MODEL)), _resident((N_DEV, D_MODEL, fb)), _resident((N_DEV, fb, D_MODEL)),
                                              _full((1, D_MODEL))],
        out_specs=[_rows(tm, D_MODEL), stacked, stacked, _rows(tm, D_MODEL), _full((1, D_MODEL)), _full((1, D_MODEL))],
        out_shape=[jax.ShapeDtypeStruct((s, D_MODEL), F32), jax.ShapeDtypeStruct((N_DEV, s, fb), BF16),
                   jax.ShapeDtypeStruct((N_DEV, s, fb), BF16), jax.ShapeDtypeStruct((s, D_MODEL), BF16),
                   jax.ShapeDtypeStruct((1, D_MODEL), F32), jax.ShapeDtypeStruct((1, D_MODEL), F32)],
        compiler_params=_params(),
    )(dh2, d, h1, mb, prew, wup, wdown, postw)


def _matmul_tn(a, b, name, tk=TK_DW):
    s, m = a.shape
    n = b.shape[1]
    tn = n if n <= 1024 else (n // 2 if (n // 2) % 128 == 0 else n // 3)
    tk = min(tk, s)
    assert n % tn == 0 and tn % 128 == 0 and s % tk == 0

    def body(a_ref, b_ref, o_ref):
        part = _dot_tn(a_ref[...], b_ref[...])

        @pl.when(pl.program_id(1) == 0)
        def _():
            o_ref[...] = part

        @pl.when(pl.program_id(1) != 0)
        def _():
            o_ref[...] += part

    return pl.pallas_call(
        body, name=name, grid=(n // tn, s // tk),
        in_specs=[pl.BlockSpec((tk, m), lambda j, k: (k, 0)), pl.BlockSpec((tk, tn), lambda j, k: (k, j))],
        out_specs=pl.BlockSpec((m, tn), lambda j, k: (0, j)),
        out_shape=jax.ShapeDtypeStruct((m, n), F32),
        compiler_params=_params(),
    )(a, b)


def _matmul_tn_stacked(a, b, name, a_stacked, tk=TK_DW):
    tk = min(tk, a.shape[-2])
    if a_stacked:
        _, s, m = a.shape
        n = b.shape[1]
        in_specs = [pl.BlockSpec((1, tk, m), lambda j, k: (j, k, 0)), pl.BlockSpec((tk, n), lambda j, k: (k, 0))]
    else:
        s, m = a.shape
        n = b.shape[2]
        in_specs = [pl.BlockSpec((tk, m), lambda j, k: (k, 0)), pl.BlockSpec((1, tk, n), lambda j, k: (j, k, 0))]

    def body(a_ref, b_ref, o_ref):
        av = a_ref[0] if a_stacked else a_ref[...]
        bv = b_ref[...] if a_stacked else b_ref[0]
        part = _dot_tn(av, bv)

        @pl.when(pl.program_id(1) == 0)
        def _():
            o_ref[0] = part

        @pl.when(pl.program_id(1) != 0)
        def _():
            o_ref[0] += part

    return pl.pallas_call(
        body, name=name, grid=(N_DEV, s // tk),
        in_specs=in_specs,
        out_specs=pl.BlockSpec((1, m, n), lambda j, k: (j, 0, 0)),
        out_shape=jax.ShapeDtypeStruct((N_DEV, m, n), F32),
        compiler_params=_params(),
    )(a, b)


def _outproj_bwd(dh1, mixed, nw, wout):
    s = dh1.shape[0]
    wide = MLA_HEADS * HEAD_PAD

    def body(dh1_ref, mixed_ref, nw_ref, w_ref, dmix_ref, doe_ref, dy_ref, gnw_ref):
        dmix, gnw = _rms_bwd(mixed_ref[...], nw_ref[...], dh1_ref[...])
        _acc_rows(gnw_ref, gnw, pl.program_id(0) == 0)
        dmb = dmix.astype(BF16)
        dmix_ref[...] = dmb
        doe_ref[...] = _dot_nt(dmb, w_ref[0:wide, :]).astype(BF16)
        dy_ref[...] = _dot_nt(dmb, w_ref[wide:, :])

    return pl.pallas_call(
        body, name="outproj_bwd", grid=(s // TM,),
        in_specs=[_rows(TM, D_MODEL), _rows(TM, D_MODEL), _full((1, D_MODEL)), _resident((wide + SSD_INNER, D_MODEL))],
        out_specs=[_rows(TM, D_MODEL), _rows(TM, wide), _rows(TM, SSD_INNER), _full((1, D_MODEL))],
        out_shape=[jax.ShapeDtypeStruct((s, D_MODEL), BF16), jax.ShapeDtypeStruct((s, wide), BF16),
                   jax.ShapeDtypeStruct((s, SSD_INNER), F32), jax.ShapeDtypeStruct((1, D_MODEL), F32)],
        compiler_params=_params(),
    )(dh1, mixed, nw, wout)


def _attn_delta(o, do):
    s = o.shape[0]
    wide = MLA_HEADS * HEAD_PAD

    def body(o_ref, do_ref, d_ref):
        ones = jnp.ones((8, HEAD_PAD), BF16)
        for hd in range(MLA_HEADS):
            cols = slice(hd * HEAD_PAD, (hd + 1) * HEAD_PAD)
            prod = o_ref[:, cols].astype(F32) * do_ref[:, cols].astype(F32)
            d_ref[hd] = _dot01(prod, ones, dot=_dot_nt, left=True)

    return pl.pallas_call(
        body, name="attn_delta", grid=(s // TM,),
        in_specs=[_rows(TM, wide), _rows(TM, wide)],
        out_specs=pl.BlockSpec((MLA_HEADS, 8, TM), lambda i: (0, 0, i)),
        out_shape=jax.ShapeDtypeStruct((MLA_HEADS, 8, s), F32),
    )(o, do)


def _attn_bwd(q, k, v, do, lse, delta):
    s = q.shape[0]
    t = ATT_T
    nq = s // t
    pair = 2 * HEAD_PAD

    def body(q_ref, k_ref, v_ref, do_ref, lse_ref, delta_ref, dq_ref, dk_ref, dv_ref):
        kb = pl.program_id(1)

        @pl.when(kb == 0)
        def _():
            dq_ref[...] = jnp.zeros(dq_ref.shape, F32)

        dk_ref[...] = jnp.zeros(dk_ref.shape, F32)
        dv_ref[...] = jnp.zeros(dv_ref.shape, F32)

        def step(qb, masked):
            r0 = pl.multiple_of(qb * t, t)
            for hh in range(2):
                cols = slice(hh * HEAD_PAD, (hh + 1) * HEAD_PAD)
                kk = k_ref[:, cols]
                qq = q_ref[pl.ds(r0, t), cols]
                dd = do_ref[pl.ds(r0, t), cols]
                sc = _dot_nt(kk, qq) * ATT_SCALE
                if masked:
                    sc = jnp.where(_chunk_mask(t, keys_on_rows=True), sc, -jnp.inf)
                p = jnp.exp(sc - lse_ref[hh, 0:1, pl.ds(r0, t)])
                dv_ref[:, cols] += _dot(p.astype(BF16), dd)
                dp = _dot_nt(v_ref[:, cols], dd)
                ds = (p * (dp - delta_ref[hh, 0:1, pl.ds(r0, t)]) * ATT_SCALE).astype(BF16)
                dk_ref[:, cols] += _dot(ds, qq)
                dq_ref[pl.ds(r0, t), cols] += _dot_tn(ds, kk)

        def loop(qb, c):
            step(qb, False)
            return c

        step(kb, True)
        lax.fori_loop(kb + 1, nq, loop, 0)

    whole = pl.BlockSpec((s, pair), lambda h, i: (0, h))
    tile = pl.BlockSpec((t, pair), lambda h, i: (i, h))
    rowvec = pl.BlockSpec((2, 8, s), lambda h, i: (h, 0, 0))
    wide = MLA_HEADS * HEAD_PAD
    return pl.pallas_call(
        body, name="attn_bwd", grid=(MLA_HEADS // 2, nq),
        in_specs=[whole, tile, tile, whole, rowvec, rowvec],
        out_specs=[whole, tile, tile],
        out_shape=[jax.ShapeDtypeStruct((s, wide), F32)] * 3,
        compiler_params=_params(),
    )(q, k, v, do, lse, delta)


def _ssd_bwd(dy, ypre, z, c, xraw, misc, prev, cw, dtb, a_exp, d_exp, nw, consts):
    s = dy.shape[0]
    nb = s // SSD_ROWS
    ncb = SSD_ROWS // CHUNK
    emisc, emisc_t, tri, trit = consts

    def body(dy_ref, ypre_ref, z_ref, c_ref, x_ref, xprev_ref, misc_ref, prev_ref, cw_ref, dtb_ref, a_ref, d_ref, nw_ref,
             emisc_ref, emisct_ref, tri_ref, trit_ref,
             dz_ref, dx_ref, dmisc_ref, gnw_ref, gd_ref, galog_ref, gdtb_ref, gcw_ref, gcb_ref,
             dst_s, dc_s, head_s):
        i = pl.program_id(0)
        first = i == 0

        @pl.when(first)
        def _():
            dst_s[...] = jnp.zeros(dst_s.shape, F32)
            head_s[...] = jnp.zeros(head_s.shape, F32)
            gnw_ref[...] = jnp.zeros(gnw_ref.shape, F32)
            gd_ref[...] = jnp.zeros(gd_ref.shape, F32)
            galog_ref[...] = jnp.zeros(galog_ref.shape, F32)
            gdtb_ref[...] = jnp.zeros(gdtb_ref.shape, F32)

        a_exp_v = a_ref[...]
        a8 = _dot01(a_exp_v, emisct_ref[...]) * (1.0 / SSD_P)

        def chunk(cr, carry):
            ci = ncb - 1 - cr
            r0 = pl.multiple_of(ci * CHUNK, CHUNK)
            cc = c_ref[pl.ds(r0, CHUNK), :]
            mm = misc_ref[pl.ds(r0, CHUNK), :]
            xa, dtr, dt, acs, acs_t, alast = _ssd_chunk_common(cc, mm, emisc_ref[...], tri_ref[...], trit_ref[...],
                                                              dtb_ref[...], a_exp_v)
            xs = xa[:, :SSD_INNER]
            xdt = xs * dt
            y = ypre_ref[pl.ds(r0, CHUNK), :]
            zz = z_ref[pl.ds(r0, CHUNK), :]
            yz, yn, rs = _gate_norm(y, zz, None)
            dyo = dy_ref[pl.ds(r0, CHUNK), :]
            gnw_ref[...] += jnp.sum(dyo * yn, axis=0, keepdims=True)
            dyn = dyo * nw_ref[...]
            half = SSD_INNER // SSD_GROUPS
            dyz_parts = []
            for g in range(SSD_GROUPS):
                gl = slice(g * half, (g + 1) * half)
                dyz_parts.append(rs[g] * (dyn[:, gl] - yn[:, gl] * jnp.mean(dyn[:, gl] * yn[:, gl], axis=-1, keepdims=True)))
            dyz = jnp.concatenate(dyz_parts, axis=1)
            sg = jax.nn.sigmoid(zz)
            dz_ref[pl.ds(r0, CHUNK), :] = dyz * y * (sg * (1.0 + zz * (1.0 - sg)))
            dyp = dyz * (zz * sg)
            dypb = dyp.astype(BF16)
            gd_ref[...] += jnp.sum(dyp * xs, axis=0, keepdims=True)
            prev = prev_ref[ci]
            dst = dst_s[...]
            cd = jnp.exp(alast)
            e = jnp.exp(acs)
            dsx = jnp.exp(alast - acs)
            wgt = (xdt * dsx).astype(BF16)
            dze = (dyp * e).astype(BF16)
            glast = jnp.sum(dst * prev, axis=0, keepdims=True) * cd
            dprev_parts, dxdt_parts, dxdt_state_parts, dbm, dcm, yoff_parts = [], [], [], [], [], []
            lane8 = lax.broadcasted_iota(jnp.int32, (CHUNK, HEAD_PAD), 1)
            diag8 = jnp.zeros((CHUNK, HEAD_PAD), F32)
            for g in range(SSD_GROUPS):
                gl = slice(g * 256, (g + 1) * 256)
                bm = xa[:, SSD_INNER + g * SSD_N:SSD_INNER + (g + 1) * SSD_N].astype(BF16)
                cm = xa[:, SSD_INNER + SSD_GROUPS * SSD_N + g * SSD_N:SSD_INNER + SSD_GROUPS * SSD_N + (g + 1) * SSD_N].astype(BF16)
                prev_g = prev[:, gl].astype(BF16)
                dst_g = dst[:, gl].astype(BF16)
                dcm_g = _dot_nt(dze[:, gl], prev_g)
                dprev_parts.append(_dot_tn(cm, dze[:, gl]))
                dxs_state = _dot(bm, dst_g) * dsx[:, gl]
                dbm_g = _dot_nt(wgt[:, gl], dst_g)
                cb_g = _dot_nt(cm, bm)
                dcb = jnp.zeros((CHUNK, CHUNK), F32)
                diag_parts = []
                for jj in range(2):
                    pair = 2 * g + jj
                    pl_ = slice(pair * 128, (pair + 1) * 128)
                    xp = xdt[:, pl_]
                    dyp_p = dypb[:, pl_]
                    dxp = jnp.zeros((CHUNK, 128), F32)
                    for hh in range(2):
                        hd = 2 * pair + hh
                        dec = _decay(acs, acs_t, hd)
                        xm = jnp.where(_half_mask(hh), xp, 0.0).astype(BF16)
                        dsc = _dot_nt(dyp_p, xm) * dec
                        dcb = dcb + dsc
                        sc = (cb_g * dec).astype(BF16)
                        dxp = dxp + jnp.where(_half_mask(hh), _dot_tn(sc, dyp_p), 0.0)
                        dm = dsc * cb_g
                        diag8 = diag8 + jnp.where(lane8 == MISC_DT + hd, jnp.sum(dm - dm.T, axis=1, keepdims=True), 0.0)
                    diag_parts.append(dxp)
                dcbb = dcb.astype(BF16)
                dcm.append(dcm_g + _dot(dcbb, bm))
                dbm.append(dbm_g + _dot_tn(dcbb, cm))
                dxdt_state_parts.append(dxs_state)
                dxdt_parts.append(jnp.concatenate(diag_parts, axis=1) + dxs_state)
                yoff_parts.append(_dot(cm, prev_g) * e[:, gl])
            dxdt = jnp.concatenate(dxdt_parts, axis=1)
            dxdt_state = jnp.concatenate(dxdt_state_parts, axis=1)
            dst_s[...] = dst * cd + jnp.concatenate(dprev_parts, axis=1)
            dacs = dyp * jnp.concatenate(yoff_parts, axis=1) - xdt * dxdt_state
            last = jnp.sum(xdt * dxdt_state, axis=0, keepdims=True) + glast
            row = lax.broadcasted_iota(jnp.int32, (CHUNK, SSD_INNER), 0)
            dacs = dacs + jnp.where(row == CHUNK - 1, last, 0.0)
            dacs8 = _dot01(dacs, emisct_ref[...]) + diag8
            da8 = _dot01(dacs8, trit_ref[...], left=True)
            ddt8 = da8 * a8 + _dot01(dxdt * xs, emisct_ref[...])
            dtr8 = mm + _dot01(dtb_ref[...], emisct_ref[...]) * (1.0 / SSD_P)
            dt8 = jax.nn.softplus(dtr8)
            lane = lax.broadcasted_iota(jnp.int32, (CHUNK, HEAD_PAD), 1)
            on_dt = jnp.logical_and(lane >= MISC_DT, lane < MISC_DT + SSD_HEADS)
            ddtr8 = jnp.where(on_dt, ddt8 * jax.nn.sigmoid(dtr8), 0.0)
            dmisc_ref[pl.ds(r0, CHUNK), :] = ddtr8
            gdtb_ref[...] += jnp.sum(ddtr8, axis=0, keepdims=True)
            galog_ref[...] += jnp.sum(jnp.where(on_dt, da8 * dt8, 0.0), axis=0, keepdims=True) * a8
            dxs = d_ref[...] * dyp + dxdt * dt
            dxa = jnp.concatenate([dxs] + dbm + dcm, axis=1)
            sc_ = jax.nn.sigmoid(cc)
            dc_s[pl.ds(r0, CHUNK), :] = dxa * (sc_ * (1.0 + cc * (1.0 - sc_)))
            return carry

        lax.fori_loop(0, ncb, chunk, 0)

        dc = dc_s[...]
        dcext = jnp.concatenate([dc, head_s[...]], axis=0)
        dx = dc * cw_ref[CONV_W - 1:CONV_W, :]
        for j in range(1, CONV_W):
            dx = dx + pltpu.roll(dcext, SSD_ROWS + 8 - j, 0)[:SSD_ROWS, :] * cw_ref[CONV_W - 1 - j:CONV_W - j, :]
        dx_ref[...] = dx
        head_s[...] = dc[:8, :]
        xprev = jnp.where(i == nb - 1, 0.0, xprev_ref[...])
        xext = jnp.concatenate([xprev, x_ref[...]], axis=0)
        rows = [jnp.sum(dc * pltpu.roll(xext, CONV_W - 1 - kk, 0)[8:, :], axis=0, keepdims=True) for kk in range(CONV_W)]
        gcw = jnp.concatenate(rows, axis=0)

        @pl.when(first)
        def _():
            gcw_ref[...] = gcw
            gcb_ref[...] = jnp.sum(dc, axis=0, keepdims=True)

        @pl.when(jnp.logical_not(first))
        def _():
            gcw_ref[...] += gcw
            gcb_ref[...] += jnp.sum(dc, axis=0, keepdims=True)

    def rev(width):
        return pl.BlockSpec((SSD_ROWS, width), lambda i: (nb - 1 - i, 0))

    per8 = SSD_ROWS // 8
    return pl.pallas_call(
        body, name="ssd_bwd", grid=(nb,),
        in_specs=[rev(SSD_INNER), rev(SSD_INNER), rev(SSD_INNER), rev(CONV_DIM), rev(CONV_DIM),
                  pl.BlockSpec((8, CONV_DIM), lambda i: (jnp.maximum((nb - 1 - i) * per8 - 1, 0), 0)),
                  rev(HEAD_PAD), pl.BlockSpec((ncb, SSD_N, SSD_INNER), lambda i: (nb - 1 - i, 0, 0)),
                  _full((CONV_W, CONV_DIM)), _full((1, SSD_INNER)), _full((1, SSD_INNER)), _full((1, SSD_INNER)),
                  _full((1, SSD_INNER)), _full((HEAD_PAD, SSD_INNER)), _full((SSD_INNER, HEAD_PAD)), _full((CHUNK, CHUNK)),
                  _full((CHUNK, CHUNK))],
        out_specs=[rev(SSD_INNER), rev(CONV_DIM), rev(HEAD_PAD), _full((1, SSD_INNER)), _full((1, SSD_INNER)),
                   _full((1, HEAD_PAD)), _full((1, HEAD_PAD)), _full((CONV_W, CONV_DIM)), _full((1, CONV_DIM))],
        out_shape=[jax.ShapeDtypeStruct((s, SSD_INNER), F32), jax.ShapeDtypeStruct((s, CONV_DIM), F32),
                   jax.ShapeDtypeStruct((s, HEAD_PAD), F32), jax.ShapeDtypeStruct((1, SSD_INNER), F32),
                   jax.ShapeDtypeStruct((1, SSD_INNER), F32), jax.ShapeDtypeStruct((1, HEAD_PAD), F32),
                   jax.ShapeDtypeStruct((1, HEAD_PAD), F32), jax.ShapeDtypeStruct((CONV_W, CONV_DIM), F32),
                   jax.ShapeDtypeStruct((1, CONV_DIM), F32)],
        scratch_shapes=[pltpu.VMEM((SSD_N, SSD_INNER), F32), pltpu.VMEM((SSD_ROWS, CONV_DIM), F32), pltpu.VMEM((8, CONV_DIM), F32)],
        compiler_params=_params(),
    )(dy, ypre, z, c, xraw, xraw, misc, prev, cw, dtb, a_exp, d_exp, nw, emisc, emisc_t, tri, trit)


def _qkv_bwd(dq, dk, dv, cq, ckv, qnw, kvnw, wuq, wkv, cosf, sinf):
    s = dq.shape[0]
    wide = MLA_HEADS * HEAD_PAD

    def body(dq_ref, dk_ref, dv_ref, cq_ref, ckv_ref, qnw_ref, kvnw_ref, wuq_ref, wkv_ref, cos_ref, sin_ref,
             dqb_ref, dkvb_ref, dcq_ref, dckv_ref, dmisc_ref, gq_ref, gkv_ref):
        first = pl.program_id(0) == 0
        cosf, sinf = cos_ref[...], sin_ref[...]
        dkr = jnp.zeros((TM, HEAD_PAD), F32)
        for hd in range(MLA_HEADS):
            cols = slice(hd * HEAD_PAD, (hd + 1) * HEAD_PAD)
            dqb_ref[:, cols] = _rope(dq_ref[:, cols], cosf, sinf, -1.0).astype(BF16)
            dkh = dk_ref[:, cols]
            dkvb_ref[:, cols] = dkh.astype(BF16)
            dkr = dkr + dkh
        dkvb_ref[:, wide:] = dv_ref[...].astype(BF16)
        lane = lax.broadcasted_iota(jnp.int32, dkr.shape, 1)
        in_rope = jnp.logical_and(lane >= MISC_ROPE, lane < MISC_ROPE + QK_ROPE)
        dmisc_ref[...] = jnp.where(in_rope, _rope(jnp.where(in_rope, dkr, 0.0), cosf, sinf, -1.0), 0.0)
        dcq, gq = _rms_bwd(cq_ref[...], qnw_ref[...], _dot_nt(dqb_ref[...], wuq_ref[...]))
        dcq_ref[...] = dcq
        _acc_rows(gq_ref, gq, first)
        dckv, gkv = _rms_bwd(ckv_ref[...], kvnw_ref[...], _dot_nt(dkvb_ref[...], wkv_ref[...]))
        dckv_ref[...] = dckv
        _acc_rows(gkv_ref, gkv, first)

    return pl.pallas_call(
        body, name="qkv_bwd", grid=(s // TM,),
        in_specs=[_rows(TM, wide)] * 3 + [_rows(TM, Q_RANK), _rows(TM, KV_RANK), _full((1, Q_RANK)), _full((1, KV_RANK)),
                                          _resident((Q_RANK, wide)), _resident((KV_RANK, 2 * wide)), _rows(TM, HEAD_PAD), _rows(TM, HEAD_PAD)],
        out_specs=[_rows(TM, wide), _rows(TM, 2 * wide), _rows(TM, Q_RANK), _rows(TM, KV_RANK), _rows(TM, HEAD_PAD),
                   _full((1, Q_RANK)), _full((1, KV_RANK))],
        out_shape=[jax.ShapeDtypeStruct((s, wide), BF16), jax.ShapeDtypeStruct((s, 2 * wide), BF16),
                   jax.ShapeDtypeStruct((s, Q_RANK), F32), jax.ShapeDtypeStruct((s, KV_RANK), F32),
                   jax.ShapeDtypeStruct((s, HEAD_PAD), F32), jax.ShapeDtypeStruct((1, Q_RANK), F32),
                   jax.ShapeDtypeStruct((1, KV_RANK), F32)],
        compiler_params=_params(),
    )(dq, dk, dv, cq, ckv, qnw, kvnw, wuq, wkv, cosf, sinf)


def _inproj_bwd(dcq, dckv, dmisc_rope, dmisc_dt, dz, dxbc, h, dh1, nw, win):
    s = h.shape[0]

    def body(dcq_ref, dckv_ref, dma_ref, dmb_ref, dz_ref, dxbc_ref, h_ref, dh1_ref, nw_ref, w_ref, dproj_ref, dh0_ref, gnw_ref):
        dproj_ref[:, 0:768] = dcq_ref[...].astype(BF16)
        dproj_ref[:, 768:1024] = dckv_ref[...].astype(BF16)
        dproj_ref[:, 1024:1152] = (dma_ref[...] + dmb_ref[...]).astype(BF16)
        dproj_ref[:, 1152:1664] = dz_ref[...].astype(BF16)
        dproj_ref[:, 1664:2688] = dxbc_ref[...].astype(BF16)
        du = _dot_nt(dproj_ref[...], w_ref[...])
        dx, gnw = _rms_bwd(h_ref[...], nw_ref[...], du)
        _acc_rows(gnw_ref, gnw, pl.program_id(0) == 0)
        dh0_ref[...] = dh1_ref[...] + dx

    return pl.pallas_call(
        body, name="inproj_bwd", grid=(s // TM,),
        in_specs=[_rows(TM, Q_RANK), _rows(TM, KV_RANK), _rows(TM, HEAD_PAD), _rows(TM, HEAD_PAD), _rows(TM, SSD_INNER),
                  _rows(TM, CONV_DIM), _rows(TM, D_MODEL), _rows(TM, D_MODEL), _full((1, D_MODEL)), _resident((D_MODEL, IN_PAD))],
        out_specs=[_rows(TM, IN_PAD), _rows(TM, D_MODEL), _full((1, D_MODEL))],
        out_shape=[jax.ShapeDtypeStruct((s, IN_PAD), BF16), jax.ShapeDtypeStruct((s, D_MODEL), F32),
                   jax.ShapeDtypeStruct((1, D_MODEL), F32)],
        compiler_params=_params(),
    )(dcq, dckv, dmisc_rope, dmisc_dt, dz, dxbc, h, dh1, nw, win)


def _row_tile(rows, cols):
    cap = max(8, (1 << 18) // max(cols, 128))
    best = None
    for t in range(8, rows + 1, 8):
        if rows % t == 0 and t <= cap:
            best = t
    return best if best is not None else rows


def _adamw(w, g, m, v, name):
    rows, cols = w.shape
    tr = _row_tile(rows, cols)

    def body(w_ref, g_ref, m_ref, v_ref, d_ref, m2_ref, v2_ref):
        gg = g_ref[...]
        m2 = ADAM_B1 * m_ref[...] + (1.0 - ADAM_B1) * gg
        v2 = ADAM_B2 * v_ref[...] + (1.0 - ADAM_B2) * jnp.square(gg)
        m_hat = m2 / (1.0 - ADAM_B1 ** ADAM_STEP)
        v_hat = v2 / (1.0 - ADAM_B2 ** ADAM_STEP)
        d_ref[...] = -ADAM_LR * (m_hat / (jnp.sqrt(v_hat) + ADAM_EPS) + ADAM_WD * w_ref[...])
        m2_ref[...] = m2
        v2_ref[...] = v2

    spec = pl.BlockSpec((tr, cols), lambda i: (i, 0))
    return pl.pallas_call(
        body, name=name, grid=(rows // tr,),
        in_specs=[spec] * 4, out_specs=[spec] * 3,
        out_shape=[jax.ShapeDtypeStruct((rows, cols), F32)] * 3,
    )(w, g, m, v)


_MESH = pl.DeviceIdType.MESH
_ANY = pl.BlockSpec(memory_space=pl.ANY)


def _my_place():
    return lax.axis_index("x"), lax.axis_index("y"), lax.axis_index("c")


def _flip(place, k):
    x, y, c = place
    return (1 - x if k & 4 else x, 1 - y if k & 2 else y, 1 - c if k & 1 else c)


def _block_id(place):
    return 4 * place[0] + 2 * place[1] + place[2]


def _all_gather(shard):
    rows, lanes = shard.shape

    def body(x_ref, out_ref, send_sems, recv_sems, local_sem):
        me = _my_place()
        x, y, c = me
        sibling = (x, y, 1 - c)
        chips = [(1 - x, y), (x, 1 - y), (1 - x, 1 - y)]

        def block(place):
            return out_ref.at[_block_id(place)]

        def copy(k, place, to, src=None):
            return pltpu.make_async_remote_copy(
                src_ref=block(place) if src is None else src, dst_ref=block(place),
                send_sem=send_sems.at[k], recv_sem=recv_sems.at[k], device_id=to, device_id_type=_MESH)

        mine = pltpu.make_async_copy(x_ref, block(me), local_sem)
        mine.start()
        first = [copy(0, me, sibling, src=x_ref)]
        first += [copy(1 + j, me, (*chip, c), src=x_ref) for j, chip in enumerate(chips)]
        for cp in first:
            cp.start()
        passed = [copy(4 + j, (*chip, c), sibling) for j, chip in enumerate(chips)]
        for j, chip in enumerate(chips):
            copy(1 + j, (*chip, c), me).wait_recv()
            passed[j].start()
        copy(0, sibling, me).wait_recv()
        for j, chip in enumerate(chips):
            copy(4 + j, (*chip, 1 - c), me).wait_recv()
        for cp in first + passed:
            cp.wait_send()
        mine.wait()

    return pl.pallas_call(
        body, name="weight_all_gather",
        out_shape=jax.ShapeDtypeStruct((N_DEV, rows, lanes), shard.dtype),
        in_specs=[_ANY], out_specs=_ANY,
        scratch_shapes=[pltpu.SemaphoreType.DMA((7,)), pltpu.SemaphoreType.DMA((7,)), pltpu.SemaphoreType.DMA],
    )(shard)


def _exchange(blocks):
    def body(x_ref, out_ref, send_sems, recv_sems, local_sem):
        me = _my_place()
        my = _block_id(me)
        own = pltpu.make_async_copy(x_ref.at[my], out_ref.at[my], local_sem)
        own.start()
        copies = []
        for k in range(1, N_DEV):
            peer = _flip(me, k)
            cp = pltpu.make_async_remote_copy(
                src_ref=x_ref.at[_block_id(peer)], dst_ref=out_ref.at[my],
                send_sem=send_sems.at[k - 1], recv_sem=recv_sems.at[k - 1], device_id=peer, device_id_type=_MESH)
            cp.start()
            copies.append(cp)
        for cp in copies:
            cp.wait()
        own.wait()

    return pl.pallas_call(
        body, name="grad_exchange",
        out_shape=jax.ShapeDtypeStruct(blocks.shape, blocks.dtype),
        in_specs=[_ANY], out_specs=_ANY,
        scratch_shapes=[pltpu.SemaphoreType.DMA((7,)), pltpu.SemaphoreType.DMA((7,)), pltpu.SemaphoreType.DMA],
    )(blocks)


def _sum_slots(slots):
    _, rows, lanes = slots.shape
    tr = _row_tile(rows, 8 * lanes)

    def body(x_ref, o_ref):
        acc = x_ref[0].astype(F32)
        for i in range(1, N_DEV):
            acc = acc + x_ref[i].astype(F32)
        o_ref[...] = acc

    return pl.pallas_call(
        body, name="grad_sum", grid=(rows // tr,),
        in_specs=[pl.BlockSpec((N_DEV, tr, lanes), lambda i: (0, i, 0))],
        out_specs=pl.BlockSpec((tr, lanes), lambda i: (i, 0)),
        out_shape=jax.ShapeDtypeStruct((rows, lanes), F32),
    )(slots)


def _all_reduce_small(part):
    rows, lanes = part.shape
    vmem = pl.BlockSpec(memory_space=pltpu.VMEM)

    def body(x_ref, gath_ref, sum_ref, send_sems, recv_sems):
        me = _my_place()
        my = _block_id(me)
        gath_ref[my] = x_ref[...]
        copies = []
        for k in range(1, N_DEV):
            cp = pltpu.make_async_remote_copy(
                src_ref=x_ref, dst_ref=gath_ref.at[my], send_sem=send_sems.at[k - 1], recv_sem=recv_sems.at[k - 1],
                device_id=_flip(me, k), device_id_type=_MESH)
            cp.start()
            copies.append(cp)
        for cp in copies:
            cp.wait()
        acc = gath_ref[0]
        for i in range(1, N_DEV):
            acc = acc + gath_ref[i]
        sum_ref[...] = acc

    return pl.pallas_call(
        body, name="small_grad_all_reduce",
        out_shape=[jax.ShapeDtypeStruct((N_DEV, rows, lanes), F32), jax.ShapeDtypeStruct((rows, lanes), F32)],
        in_specs=[vmem], out_specs=[vmem, vmem],
        scratch_shapes=[pltpu.SemaphoreType.DMA((7,)), pltpu.SemaphoreType.DMA((7,))],
    )(part)[1]


_SHARDED = (("w_in", (D_MODEL, IN_PROJ // N_DEV)), ("w_uq", (Q_RANK // N_DEV, Q_RANK)), ("w_ukv", (KV_RANK, HEAD_PAD)),
            ("conv_w", (CONV_W, CONV_DIM // N_DEV)), ("w_out", (D_MODEL // N_DEV, D_MODEL)),
            ("w_up", (D_MODEL, D_FF // N_DEV)), ("w_down", (D_FF // N_DEV, D_MODEL)))
_SMALL = (("pre_mix_norm", D_MODEL), ("q_norm", Q_RANK), ("kv_norm", KV_RANK), ("conv_b", CONV_DIM), ("dt_bias", SSD_HEADS),
          ("a_log", SSD_HEADS), ("d_skip", SSD_HEADS), ("ssd_norm", SSD_INNER), ("post_mix_norm", D_MODEL),
          ("pre_mlp_norm", D_MODEL), ("post_mlp_norm", D_MODEL))
_WEIGHT_ORDER = ("pre_mix_norm", "w_in", "q_norm", "w_uq", "kv_norm", "w_ukv", "conv_w", "conv_b", "dt_bias", "a_log", "d_skip",
                 "ssd_norm", "w_out", "post_mix_norm", "pre_mlp_norm", "w_up", "w_down", "post_mlp_norm")
_FLAT_TILE = 16 * 128


def _pad_flat(flat):
    n = flat.shape[-1]
    padded = -(-n // _FLAT_TILE) * _FLAT_TILE
    return jnp.pad(flat, [(0, 0)] * (flat.ndim - 1) + [(0, padded - n)])


def _pack_shards(w):
    pieces = []
    for l in range(DEPTH):
        for name, _ in _SHARDED:
            a = w[name][l]
            a = lax.bitcast_convert_type(a, BF16) if name == "conv_w" else a.astype(BF16)
            pieces.append(a.reshape(-1))
    return _pad_flat(jnp.concatenate(pieces)).reshape(-1, 128)


def _unpack_gathered(g):
    flat = g.reshape(N_DEV, -1)
    out, off = [], 0
    for l in range(DEPTH):
        d = {}
        for name, shape in _SHARDED:
            n = math.prod(shape) * (2 if name == "conv_w" else 1)
            seg = flat[:, off:off + n]
            off += n
            if name == "conv_w":
                d[name] = lax.bitcast_convert_type(seg.reshape(N_DEV, *shape, 2), F32)
            else:
                d[name] = seg.reshape(N_DEV, *shape)
        out.append(d)
    return out


def _cols(stacked):
    return jnp.transpose(stacked, (1, 0, 2)).reshape(stacked.shape[1], -1)


def _expand_pairs(a, axis_len):
    eye = jnp.eye(2, dtype=a.dtype).reshape(1, 2, 2, 1, 1)
    return a[:, :, None] * eye


def _kernel_weights(sh):
    w_in = _cols(sh["w_in"])
    zeros = lambda n: jnp.zeros((D_MODEL, n), BF16)
    s1, s2, s3, s4, s5 = 768, 1024, 1056, 1568, 2592
    win = jnp.concatenate([w_in[:, :s2], zeros(MISC_ROPE), w_in[:, s2:s3], w_in[:, s5:], zeros(HEAD_PAD - MISC_DT - SSD_HEADS),
                           w_in[:, s3:s5]], axis=1)
    w_uq = sh["w_uq"].reshape(Q_RANK, MLA_HEADS, QK_NOPE + QK_ROPE)
    wuq = jnp.pad(w_uq, ((0, 0), (0, 0), (0, HEAD_PAD - QK_NOPE - QK_ROPE))).reshape(Q_RANK, -1)
    w_ukv = _cols(sh["w_ukv"]).reshape(KV_RANK, MLA_HEADS, QK_NOPE + V_DIM)
    wkn = jnp.pad(w_ukv[..., :QK_NOPE], ((0, 0), (0, 0), (0, HEAD_PAD - QK_NOPE))).reshape(KV_RANK, -1)
    wv = w_ukv[..., QK_NOPE:].reshape(KV_RANK, 4, 2, 1, V_DIM) * jnp.eye(2, dtype=BF16).reshape(1, 1, 2, 2, 1)
    wkv = jnp.concatenate([wkn, wv.reshape(KV_RANK, -1)], axis=1)
    w_out = sh["w_out"].reshape(D_MODEL, D_MODEL)
    watt = w_out[:SSD_INNER].reshape(4, 2, 1, V_DIM, D_MODEL) * jnp.eye(2, dtype=BF16).reshape(1, 2, 2, 1, 1)
    wout = jnp.concatenate([watt.reshape(MLA_HEADS * HEAD_PAD, D_MODEL), w_out[SSD_INNER:]], axis=0)
    return dict(win=win, wuq=wuq, wkv=wkv, wout=wout, wup=sh["w_up"], wdown=sh["w_down"], conv_w=_cols(sh["conv_w"]))


def _shard_grads(g):
    dwin = g["win"]
    s1, s2 = 768, 1024
    m0 = s2
    w_in = jnp.concatenate([dwin[:, :s2], dwin[:, m0 + MISC_ROPE:m0 + MISC_ROPE + QK_ROPE], dwin[:, 1152:2688],
                            dwin[:, m0 + MISC_DT:m0 + MISC_DT + SSD_HEADS]], axis=1)
    out = {"w_in": jnp.transpose(w_in.reshape(D_MODEL, N_DEV, -1), (1, 0, 2))}
    w_uq = g["wuq"].reshape(Q_RANK, MLA_HEADS, HEAD_PAD)[..., :QK_NOPE + QK_ROPE].reshape(Q_RANK, Q_RANK)
    out["w_uq"] = w_uq.reshape(N_DEV, Q_RANK // N_DEV, Q_RANK)
    wide = MLA_HEADS * HEAD_PAD
    kn = g["wkv"][:, :wide].reshape(KV_RANK, MLA_HEADS, HEAD_PAD)[..., :QK_NOPE]
    ve = g["wkv"][:, wide:].reshape(KV_RANK, 4, 2, 2, V_DIM)
    vv = jnp.stack([ve[:, :, 0, 0], ve[:, :, 1, 1]], axis=2).reshape(KV_RANK, MLA_HEADS, V_DIM)
    out["w_ukv"] = jnp.transpose(jnp.concatenate([kn, vv], axis=-1), (1, 0, 2))
    out["conv_w"] = jnp.transpose(g["conv_w"].reshape(CONV_W, N_DEV, -1), (1, 0, 2))
    ae = g["wout_att"].reshape(4, 2, 2, V_DIM, D_MODEL)
    att = jnp.stack([ae[:, 0, 0], ae[:, 1, 1]], axis=1).reshape(SSD_INNER, D_MODEL)
    out["w_out"] = jnp.concatenate([att, g["wout_ssd"]], axis=0).reshape(N_DEV, D_MODEL // N_DEV, D_MODEL)
    out["w_up"] = g["wup"]
    out["w_down"] = g["wdown"]
    return out


def _pack_grad_blocks(per_layer):
    pieces = [per_layer[l][name].reshape(N_DEV, -1) for l in range(DEPTH) for name, _ in _SHARDED]
    flat = _pad_flat(jnp.concatenate(pieces, axis=1).astype(BF16))
    return flat.reshape(N_DEV, -1, 128)


def _unpack_grad_shard(flat2d):
    flat = flat2d.reshape(-1)
    out, off = {name: [] for name, _ in _SHARDED}, 0
    for l in range(DEPTH):
        for name, shape in _SHARDED:
            n = math.prod(shape)
            out[name].append(flat[off:off + n].reshape(shape))
            off += n
    return {name: jnp.stack(v) for name, v in out.items()}


def _small_rows(n):
    return -(-n // 128)


def _pack_small(vals):
    rows = []
    for l in range(DEPTH):
        for name, n in _SMALL:
            r = _small_rows(n)
            rows.append(jnp.pad(vals[name][l].reshape(-1), (0, r * 128 - n)).reshape(r, 128))
    total = sum(r.shape[0] for r in rows)
    rows.append(jnp.zeros((-total % 8, 128), F32))
    return jnp.concatenate(rows, axis=0)


def _unpack_small(packed):
    out, off = {name: [] for name, _ in _SMALL}, 0
    for l in range(DEPTH):
        for name, n in _SMALL:
            r = _small_rows(n)
            out[name].append(packed[off:off + r].reshape(-1)[:n])
            off += r
    return {name: jnp.stack(v) for name, v in out.items()}


def _lane_rows(vec8):
    return jnp.repeat(vec8, SSD_P).reshape(1, SSD_INNER)


def _layer_fwd(h, kw, sm, l, cosf, sinf, consts):
    row = lambda name: sm[name][l].reshape(1, -1)
    t = {}
    t["h0"] = h
    t["ub"], t["cq"], t["ckv"], t["misc"], t["z"], t["xraw"] = _inproj_fwd(h, row("pre_mix_norm"), kw["win"])
    t["cqn"], t["ckvn"], t["q"], t["k"], t["v"] = _qkv_fwd(t["cq"], t["ckv"], t["misc"], row("q_norm"), row("kv_norm"),
                                                         kw["wuq"], kw["wkv"], cosf, sinf)
    t["oe"], t["lse"] = _attn_fwd(t["q"], t["k"], t["v"])
    t["dtb"] = _lane_rows(sm["dt_bias"][l])
    t["a_exp"] = _lane_rows(-jnp.exp(sm["a_log"][l]))
    t["d_exp"] = _lane_rows(sm["d_skip"][l])
    t["c"], t["prev"], t["ypre"], t["yssd"] = _ssd_fwd(t["xraw"], t["misc"], t["z"], kw["conv_w"], row("conv_b"), t["dtb"],
                                                     t["a_exp"], t["d_exp"], row("ssd_norm"), consts)
    t["mixed"], t["h1"] = _outproj_fwd(t["oe"], t["yssd"], kw["wout"], h, row("post_mix_norm"))
    t["mb"], t["d"], h2 = _mlp_fwd(t["h1"], row("pre_mlp_norm"), kw["wup"], kw["wdown"], row("post_mlp_norm"))
    return h2, t


def _layer_bwd(dh2, t, kw, sm, l, cosf, sinf, consts):
    row = lambda name: sm[name][l].reshape(1, -1)
    g, gs = {}, {}
    dh1, dab, rb, ddb, gs["post_mlp_norm"], gs["pre_mlp_norm"] = _mlp_bwd(
        dh2, t["d"], t["h1"], t["mb"], row("pre_mlp_norm"), kw["wup"], kw["wdown"], row("post_mlp_norm"))
    g["wup"] = _matmul_tn_stacked(t["mb"], dab, f"dw_up_{l}", a_stacked=False)
    g["wdown"] = _matmul_tn_stacked(rb, ddb, f"dw_down_{l}", a_stacked=True)
    dmixb, doe, dyssd, gs["post_mix_norm"] = _outproj_bwd(dh1, t["mixed"], row("post_mix_norm"), kw["wout"])
    g["wout_att"] = _matmul_tn(t["oe"], dmixb, f"dw_out_att_{l}")
    g["wout_ssd"] = _matmul_tn(t["yssd"], dmixb, f"dw_out_ssd_{l}")
    dz, dxraw, dmisc_dt, gs["ssd_norm"], gd, galog, gdtb, g["conv_w"], gs["conv_b"] = _ssd_bwd(
        dyssd, t["ypre"], t["z"], t["c"], t["xraw"], t["misc"], t["prev"], kw["conv_w"], t["dtb"], t["a_exp"], t["d_exp"],
        row("ssd_norm"), consts)
    gs["d_skip"] = jnp.sum(gd.reshape(SSD_HEADS, SSD_P), axis=1)
    gs["a_log"] = galog[0, MISC_DT:MISC_DT + SSD_HEADS]
    gs["dt_bias"] = gdtb[0, MISC_DT:MISC_DT + SSD_HEADS]
    dq, dk, dv = _attn_bwd(t["q"], t["k"], t["v"], doe, t["lse"], _attn_delta(t["oe"], doe))
    dqb, dkvb, dcq, dckv, dmisc_rope, gs["q_norm"], gs["kv_norm"] = _qkv_bwd(
        dq, dk, dv, t["cq"], t["ckv"], row("q_norm"), row("kv_norm"), kw["wuq"], kw["wkv"], cosf, sinf)
    g["wuq"] = _matmul_tn(t["cqn"], dqb, f"dw_uq_{l}")
    g["wkv"] = _matmul_tn(t["ckvn"], dkvb, f"dw_kv_{l}")
    dprojb, dh0, gs["pre_mix_norm"] = _inproj_bwd(dcq, dckv, dmisc_rope, dmisc_dt, dz, dxraw, t["h0"], dh1,
                                                  row("pre_mix_norm"), kw["win"])
    g["win"] = _matmul_tn(t["ub"], dprojb, f"dw_in_{l}")
    return dh0, g, {k: v.reshape(-1) for k, v in gs.items()}


def _local_step(x, positions, kws, sm, target):
    inv_freq = ROPE_THETA ** (-jnp.arange(0, QK_ROPE, 2, dtype=F32) / QK_ROPE)
    invf = jnp.zeros((HEAD_PAD,), F32).at[MISC_ROPE:MISC_ROPE + QK_ROPE].set(jnp.concatenate([inv_freq, inv_freq]))
    cosf, sinf = _rope_tables(positions.reshape(-1, 1), invf.reshape(1, HEAD_PAD))
    consts = _ssd_consts()
    h, saved = x, []
    for l in range(DEPTH):
        h, t = _layer_fwd(h, kws[l], sm, l, cosf, sinf, consts)
        saved.append(t)
    dh, loss = _loss_grad(h, target)
    grads, small = [None] * DEPTH, [None] * DEPTH
    for l in reversed(range(DEPTH)):
        dh, grads[l], small[l] = _layer_bwd(dh, saved[l], kws[l], sm, l, cosf, sinf, consts)
    return loss[0, 0], dh, grads, small


def kernel(x, positions, pre_mix_norm, w_in, q_norm, w_uq, kv_norm, w_ukv, conv_w, conv_b, dt_bias, a_log, d_skip, ssd_norm, w_out, post_mix_norm, pre_mlp_norm, w_up, w_down, post_mlp_norm, loss_target, m_pre_mix_norm, m_w_in, m_q_norm, m_w_uq, m_kv_norm, m_w_ukv, m_conv_w, m_conv_b, m_dt_bias, m_a_log, m_d_skip, m_ssd_norm, m_w_out, m_post_mix_norm, m_pre_mlp_norm, m_w_up, m_w_down, m_post_mlp_norm, v_pre_mix_norm, v_w_in, v_q_norm, v_w_uq, v_kv_norm, v_w_ukv, v_conv_w, v_conv_b, v_dt_bias, v_a_log, v_d_skip, v_ssd_norm, v_w_out, v_post_mix_norm, v_pre_mlp_norm, v_w_up, v_w_down, v_post_mlp_norm):
    w = dict(pre_mix_norm=pre_mix_norm, w_in=w_in, q_norm=q_norm, w_uq=w_uq, kv_norm=kv_norm, w_ukv=w_ukv, conv_w=conv_w,
             conv_b=conv_b, dt_bias=dt_bias, a_log=a_log, d_skip=d_skip, ssd_norm=ssd_norm, w_out=w_out,
             post_mix_norm=post_mix_norm, pre_mlp_norm=pre_mlp_norm, w_up=w_up, w_down=w_down, post_mlp_norm=post_mlp_norm)
    m = dict(pre_mix_norm=m_pre_mix_norm, w_in=m_w_in, q_norm=m_q_norm, w_uq=m_w_uq, kv_norm=m_kv_norm, w_ukv=m_w_ukv,
             conv_w=m_conv_w, conv_b=m_conv_b, dt_bias=m_dt_bias, a_log=m_a_log, d_skip=m_d_skip, ssd_norm=m_ssd_norm,
             w_out=m_w_out, post_mix_norm=m_post_mix_norm, pre_mlp_norm=m_pre_mlp_norm, w_up=m_w_up, w_down=m_w_down,
             post_mlp_norm=m_post_mlp_norm)
    v = dict(pre_mix_norm=v_pre_mix_norm, w_in=v_w_in, q_norm=v_q_norm, w_uq=v_w_uq, kv_norm=v_kv_norm, w_ukv=v_w_ukv,
             conv_w=v_conv_w, conv_b=v_conv_b, dt_bias=v_dt_bias, a_log=v_a_log, d_skip=v_d_skip, ssd_norm=v_ssd_norm,
             w_out=v_w_out, post_mix_norm=v_post_mix_norm, pre_mlp_norm=v_pre_mlp_norm, w_up=v_w_up, w_down=v_w_down,
             post_mlp_norm=v_post_mlp_norm)
    sm = {name: w[name] for name, _ in _SMALL}

    gathered = _all_gather(_pack_shards(w))
    kws = [_kernel_weights(sh) for sh in _unpack_gathered(gathered)]
    loss_part, dx, grads, small = _local_step(x[0], positions[0], kws, sm, loss_target[0])

    blocks = _pack_grad_blocks([_shard_grads(g) for g in grads])
    g_sharded = _unpack_grad_shard(_sum_slots(_exchange(blocks)))
    g_small = _unpack_small(_all_reduce_small(_pack_small({name: jnp.stack([small[l][name] for l in range(DEPTH)])
                                                           for name, _ in _SMALL})))
    loss = lax.psum(loss_part, ("x", "y", "c"))

    grad, delta, new_m, new_v = {}, {}, {}, {}
    for name, _ in _SHARDED:
        shape = w[name].shape
        flat = lambda a: a.reshape(-1, shape[-1])
        d_, m_, v_ = _adamw(flat(w[name]), flat(g_sharded[name]), flat(m[name]), flat(v[name]), f"adamw_{name}")
        grad[name] = g_sharded[name]
        delta[name], new_m[name], new_v[name] = d_.reshape(shape), m_.reshape(shape), v_.reshape(shape)
    pk = lambda d: _pack_small({name: d[name] for name, _ in _SMALL})
    d_, m_, v_ = _adamw(pk(w), pk(g_small), pk(m), pk(v), "adamw_small")
    for dst, packed in ((delta, d_), (new_m, m_), (new_v, v_)):
        dst.update(_unpack_small(packed))
    grad.update(g_small)

    outs = [loss, dx[None]]
    for d in (grad, delta, new_m, new_v):
        outs += [d[name] for name in _WEIGHT_ORDER]
    return tuple(outs)
```

```python
import jax
import jax.numpy as jnp
import numpy as np
from jax import lax
from jax.experimental import pallas as pl
from jax.experimental.pallas import tpu as pltpu

F32 = jnp.float32
BF16 = jnp.bfloat16
HI = lax.Precision.HIGHEST

D_MODEL = 1024
DEPTH = 2
N_DEV = 8
CHUNK = 64
EPS = 1e-6
MLA_HEADS = 8
QK_NOPE = 64
QK_ROPE = 32
V_DIM = 64
Q_RANK = 768
KV_RANK = 256
ROPE_THETA = 10000.0
SSD_HEADS = 8
SSD_P = 64
SSD_INNER = 512
SSD_GROUPS = 2
SSD_N = 128
CONV_W = 4
CONV_DIM = 1024
D_FF = 4096
IN_PROJ = 2600
HEAD_PAD = 128
IN_PAD = 2688
MISC_ROPE = 64
MISC_DT = 96
ATT_SCALE = (QK_NOPE + QK_ROPE) ** -0.5

ADAM_LR = 0.001
ADAM_B1 = 0.9
ADAM_B2 = 0.999
ADAM_EPS = 1e-08
ADAM_WD = 0.01
ADAM_STEP = 10

TM = 512
TQ = 256
ATT_T = 512
SSD_ROWS = 256
TK_DW = 2048
VMEM_LIMIT = 56 * 1024 * 1024

_NT = (((1,), (1,)), ((), ()))
_TN = (((0,), (0,)), ((), ()))


def _params(**kw):
    return pltpu.CompilerParams(vmem_limit_bytes=VMEM_LIMIT, **kw)


def _dot(a, b, precision=None):
    return jnp.dot(a, b, preferred_element_type=F32, precision=precision)


def _dot_nt(a, b, precision=None):
    return lax.dot_general(a, b, _NT, preferred_element_type=F32, precision=precision)


def _dot_tn(a, b, precision=None):
    return lax.dot_general(a, b, _TN, preferred_element_type=F32, precision=precision)


def _split3(x):
    hi = x.astype(BF16)
    r = x - hi.astype(F32)
    mid = r.astype(BF16)
    return hi, mid, (r - mid.astype(F32)).astype(BF16)


def _dot01(x, m01, dot=_dot, left=False):
    parts = [dot(m01, p) if left else dot(p, m01) for p in _split3(x)]
    return parts[0] + parts[1] + parts[2]


def _full(shape):
    n = len(shape)
    return pl.BlockSpec(shape, lambda *_: (0,) * n)


def _resident(shape):
    n = len(shape)
    return pl.BlockSpec(shape, lambda *_: (0,) * n, pipeline_mode=pl.Buffered(1))


def _rows(tm, width):
    return pl.BlockSpec((tm, width), lambda i: (i, 0))


def _rms_fwd(x, w):
    r = lax.rsqrt(jnp.mean(x * x, axis=-1, keepdims=True) + EPS)
    return (x * r) * w


def _rms_bwd(x, w, dy):
    r = lax.rsqrt(jnp.mean(x * x, axis=-1, keepdims=True) + EPS)
    xh = x * r
    dxn = dy * w
    dx = r * (dxn - xh * jnp.mean(dxn * xh, axis=-1, keepdims=True))
    return dx, dy * xh


def _acc_rows(ref, val, first):
    s = jnp.sum(val, axis=0, keepdims=True)

    @pl.when(first)
    def _():
        ref[...] = s

    @pl.when(jnp.logical_not(first))
    def _():
        ref[...] += s


def _rope(t, cosf, sinf, sign):
    lane = lax.broadcasted_iota(jnp.int32, t.shape, 1)
    rot = jnp.where(lane < MISC_ROPE + QK_ROPE // 2, -pltpu.roll(t, HEAD_PAD - QK_ROPE // 2, 1), pltpu.roll(t, QK_ROPE // 2, 1))
    return t * cosf + sign * (rot * sinf)


def _rope_tables(pos, invf):
    s = pos.shape[0]

    def body(pos_ref, invf_ref, cos_ref, sin_ref):
        ang = pos_ref[...].astype(F32) * invf_ref[...]
        cos_ref[...] = jnp.cos(ang)
        sin_ref[...] = jnp.sin(ang)

    return pl.pallas_call(
        body, name="rope_tables", grid=(s // TM,),
        in_specs=[_rows(TM, 1), _full((1, HEAD_PAD))],
        out_specs=[_rows(TM, HEAD_PAD), _rows(TM, HEAD_PAD)],
        out_shape=[jax.ShapeDtypeStruct((s, HEAD_PAD), F32)] * 2,
    )(pos, invf)


def _inproj_fwd(h, nw, win):
    s = h.shape[0]

    def body(h_ref, nw_ref, w_ref, ub_ref, cq_ref, ckv_ref, misc_ref, z_ref, xbc_ref):
        ub = _rms_fwd(h_ref[...], nw_ref[...]).astype(BF16)
        ub_ref[...] = ub
        proj = _dot(ub, w_ref[...])
        cq_ref[...] = proj[:, 0:768]
        ckv_ref[...] = proj[:, 768:1024]
        misc_ref[...] = proj[:, 1024:1152]
        z_ref[...] = proj[:, 1152:1664]
        xbc_ref[...] = proj[:, 1664:2688]

    widths = (768, 256, 128, 512, 1024)
    return pl.pallas_call(
        body, name="inproj_fwd", grid=(s // TM,),
        in_specs=[_rows(TM, D_MODEL), _full((1, D_MODEL)), _resident((D_MODEL, IN_PAD))],
        out_specs=[_rows(TM, D_MODEL)] + [_rows(TM, w) for w in widths],
        out_shape=[jax.ShapeDtypeStruct((s, D_MODEL), BF16)] + [jax.ShapeDtypeStruct((s, w), F32) for w in widths],
        compiler_params=_params(),
    )(h, nw, win)


def _qkv_fwd(cq, ckv, misc, qnw, kvnw, wuq, wkv, cosf, sinf):
    s = cq.shape[0]

    def body(cq_ref, ckv_ref, misc_ref, qnw_ref, kvnw_ref, wuq_ref, wkv_ref, cos_ref, sin_ref,
             cqn_ref, ckvn_ref, q_ref, k_ref, v_ref):
        cosf, sinf = cos_ref[...], sin_ref[...]
        cqn = _rms_fwd(cq_ref[...], qnw_ref[...]).astype(BF16)
        cqn_ref[...] = cqn
        q = _dot(cqn, wuq_ref[...])
        ckvn = _rms_fwd(ckv_ref[...], kvnw_ref[...]).astype(BF16)
        ckvn_ref[...] = ckvn
        kv = _dot(ckvn, wkv_ref[...])
        m = misc_ref[...]
        lane = lax.broadcasted_iota(jnp.int32, m.shape, 1)
        in_rope = jnp.logical_and(lane >= MISC_ROPE, lane < MISC_ROPE + QK_ROPE)
        kr = jnp.where(in_rope, _rope(m, cosf, sinf, 1.0), 0.0)
        for hd in range(MLA_HEADS):
            cols = slice(hd * HEAD_PAD, (hd + 1) * HEAD_PAD)
            q_ref[:, cols] = _rope(q[:, cols], cosf, sinf, 1.0).astype(BF16)
            k_ref[:, cols] = (kv[:, cols] + kr).astype(BF16)
        v_ref[...] = kv[:, MLA_HEADS * HEAD_PAD:].astype(BF16)

    wide = MLA_HEADS * HEAD_PAD
    return pl.pallas_call(
        body, name="qkv_fwd", grid=(s // TM,),
        in_specs=[_rows(TM, Q_RANK), _rows(TM, KV_RANK), _rows(TM, HEAD_PAD), _full((1, Q_RANK)), _full((1, KV_RANK)),
                  _resident((Q_RANK, wide)), _resident((KV_RANK, 2 * wide)), _rows(TM, HEAD_PAD), _rows(TM, HEAD_PAD)],
        out_specs=[_rows(TM, Q_RANK), _rows(TM, KV_RANK), _rows(TM, wide), _rows(TM, wide), _rows(TM, wide)],
        out_shape=[jax.ShapeDtypeStruct((s, Q_RANK), BF16), jax.ShapeDtypeStruct((s, KV_RANK), BF16)]
        + [jax.ShapeDtypeStruct((s, wide), BF16)] * 3,
        compiler_params=_params(),
    )(cq, ckv, misc, qnw, kvnw, wuq, wkv, cosf, sinf)


def _chunk_mask(t, keys_on_rows=False):
    row = lax.broadcasted_iota(jnp.int32, (t, t), 0) // CHUNK
    col = lax.broadcasted_iota(jnp.int32, (t, t), 1) // CHUNK
    return (row <= col) if keys_on_rows else (col <= row)


def _attn_fwd(q, k, v, gather=()):
    s = q.shape[0]
    t = ATT_T
    nq = s // t
    pair = 2 * HEAD_PAD
    ng = len(gather)

    def body(q_ref, k_ref, v_ref, *rest):
        g_in, (o_ref, lse_ref), g_out = rest[:ng], rest[ng:ng + 2], rest[ng + 2:2 * ng + 2]
        m_s, l_s, acc_s = rest[2 * ng + 2:2 * ng + 5]
        qi = pl.program_id(1)
        _hosted_comm("gather", g_in, g_out, rest[2 * ng + 5:],
                     jnp.logical_and(pl.program_id(0) == 0, qi == 0),
                     jnp.logical_and(pl.program_id(0) == MLA_HEADS // 2 - 1, qi == nq - 1))
        m_s[...] = jnp.full(m_s.shape, -jnp.inf, F32)
        l_s[...] = jnp.zeros(l_s.shape, F32)
        acc_s[...] = jnp.zeros(acc_s.shape, F32)

        def step(kb, masked):
            r0 = pl.multiple_of(kb * t, t)
            for hh in range(2):
                cols = slice(hh * HEAD_PAD, (hh + 1) * HEAD_PAD)
                sc = _dot_nt(q_ref[:, cols], k_ref[pl.ds(r0, t), cols]) * ATT_SCALE
                if masked:
                    sc = jnp.where(_chunk_mask(t), sc, -jnp.inf)
                m_old = m_s[hh]
                m_new = jnp.maximum(m_old, jnp.max(sc, axis=-1, keepdims=True))
                alpha = jnp.exp(m_old - m_new)
                p = jnp.exp(sc - jnp.tile(m_new, (1, t // HEAD_PAD)))
                l_s[hh] = alpha * l_s[hh] + jnp.sum(p, axis=-1, keepdims=True)
                acc_s[hh] = alpha * acc_s[hh] + _dot(p.astype(BF16), v_ref[pl.ds(r0, t), cols])
                m_s[hh] = m_new

        def loop(kb, c):
            step(kb, False)
            return c

        lax.fori_loop(0, qi, loop, 0)
        step(qi, True)
        for hh in range(2):
            cols = slice(hh * HEAD_PAD, (hh + 1) * HEAD_PAD)
            o_ref[:, cols] = (acc_s[hh] / l_s[hh]).astype(BF16)
            lse_ref[hh] = (m_s[hh] + jnp.log(l_s[hh])).T[0:8, :]

    outs = pl.pallas_call(
        body, name="attn_fwd_gather" if ng else "attn_fwd", grid=(MLA_HEADS // 2, nq),
        in_specs=[pl.BlockSpec((t, pair), lambda h, i: (i, h)),
                  pl.BlockSpec((s, pair), lambda h, i: (0, h)),
                  pl.BlockSpec((s, pair), lambda h, i: (0, h))] + [_ANY] * ng,
        out_specs=[pl.BlockSpec((t, pair), lambda h, i: (i, h)),
                   pl.BlockSpec((2, 8, t), lambda h, i: (h, 0, i))] + [_ANY] * ng,
        out_shape=[jax.ShapeDtypeStruct((s, MLA_HEADS * HEAD_PAD), BF16), jax.ShapeDtypeStruct((MLA_HEADS, 8, s), F32)]
        + _comm_out_shapes("gather", gather),
        scratch_shapes=[pltpu.VMEM((2, t, HEAD_PAD), F32), pltpu.VMEM((2, t, HEAD_PAD), F32), pltpu.VMEM((2, t, HEAD_PAD), F32)]
        + (_comm_scratch(ng) if ng else []),
        compiler_params=_params(),
    )(q, k, v, *gather)
    return outs[0], outs[1], list(outs[2:])


def _ssd_consts():
    emisc = np.zeros((HEAD_PAD, SSD_INNER), np.float32)
    for hd in range(SSD_HEADS):
        emisc[MISC_DT + hd, hd * SSD_P:(hd + 1) * SSD_P] = 1.0
    idx = np.arange(CHUNK)
    tri = (idx[:, None] >= idx[None, :]).astype(np.float32)
    return tuple(jnp.asarray(m, BF16) for m in (emisc, emisc.T.copy(), tri, tri.T.copy()))


def _ssd_chunk_common(cc, misc, emisc, tri, trit, dtb, a_exp):
    xa = cc * jax.nn.sigmoid(cc)
    dtr = _dot01(misc, emisc) + dtb
    dt = jax.nn.softplus(dtr)
    a = dt * a_exp
    acs = _dot01(a, tri, left=True)
    acs_t = _dot01(a, trit, dot=_dot_tn)
    alast = acs[CHUNK - 1:CHUNK, :]
    return xa, dtr, dt, acs, acs_t, alast


def _decay(acs, acs_t, hd):
    row = lax.broadcasted_iota(jnp.int32, (CHUNK, CHUNK), 0)
    col = lax.broadcasted_iota(jnp.int32, (CHUNK, CHUNK), 1)
    diff = acs[:, hd * SSD_P:hd * SSD_P + 1] - acs_t[hd * SSD_P:hd * SSD_P + 1, :]
    return jnp.exp(jnp.where(row >= col, diff, -jnp.inf))


def _half_mask(hh):
    lane = lax.broadcasted_iota(jnp.int32, (CHUNK, 2 * SSD_P), 1)
    return (lane >= SSD_P) if hh else (lane < SSD_P)


def _gate_norm(y, zz, nw):
    yz = y * (zz * jax.nn.sigmoid(zz))
    outs, rs = [], []
    half = SSD_INNER // SSD_GROUPS
    for g in range(SSD_GROUPS):
        yg = yz[:, g * half:(g + 1) * half]
        r = lax.rsqrt(jnp.mean(yg * yg, axis=-1, keepdims=True) + EPS)
        outs.append(yg * r)
        rs.append(r)
    return yz, jnp.concatenate(outs, axis=1), rs


def _ssd_fwd(xraw, misc, z, cw, cb, dtb, a_exp, d_exp, nw, consts):
    s = xraw.shape[0]
    nb = s // SSD_ROWS
    ncb = SSD_ROWS // CHUNK
    emisc, _, tri, trit = consts

    def body(x_ref, misc_ref, z_ref, cw_ref, cb_ref, dtb_ref, a_ref, d_ref, nw_ref, emisc_ref, tri_ref, trit_ref,
             c_ref, prev_ref, ypre_ref, yssd_ref, tail_s, state_s):
        i = pl.program_id(0)

        @pl.when(i == 0)
        def _():
            tail_s[...] = jnp.zeros(tail_s.shape, F32)
            state_s[...] = jnp.zeros(state_s.shape, F32)

        x = x_ref[...]
        xext = jnp.concatenate([tail_s[...], x], axis=0)
        acc = x * cw_ref[CONV_W - 1:CONV_W, :] + cb_ref[...]
        for j in range(1, CONV_W):
            acc = acc + pltpu.roll(xext, j, 0)[8:, :] * cw_ref[CONV_W - 1 - j:CONV_W - j, :]
        tail_s[...] = x[SSD_ROWS - 8:, :]
        c_ref[...] = acc

        def chunk(ci, carry):
            r0 = pl.multiple_of(ci * CHUNK, CHUNK)
            xa, _, dt, acs, acs_t, alast = _ssd_chunk_common(
                c_ref[pl.ds(r0, CHUNK), :], misc_ref[pl.ds(r0, CHUNK), :], emisc_ref[...], tri_ref[...], trit_ref[...],
                dtb_ref[...], a_ref[...])
            xs = xa[:, :SSD_INNER]
            xdt = xs * dt
            prev = state_s[...]
            prev_ref[ci] = prev
            wgt = (xdt * jnp.exp(alast - acs)).astype(BF16)
            e = jnp.exp(acs)
            ys, new_states = [], []
            for g in range(SSD_GROUPS):
                bm = xa[:, SSD_INNER + g * SSD_N:SSD_INNER + (g + 1) * SSD_N].astype(BF16)
                cm = xa[:, SSD_INNER + SSD_GROUPS * SSD_N + g * SSD_N:SSD_INNER + SSD_GROUPS * SSD_N + (g + 1) * SSD_N].astype(BF16)
                cb_g = _dot_nt(cm, bm)
                gl = slice(g * 256, (g + 1) * 256)
                new_states.append(_dot_tn(bm, wgt[:, gl]))
                yoff = _dot(cm, prev[:, gl].astype(BF16)) * e[:, gl]
                for jj in range(2):
                    pair = 2 * g + jj
                    pl_ = slice(pair * 128, (pair + 1) * 128)
                    xp = xdt[:, pl_]
                    yp = yoff[:, jj * 128:(jj + 1) * 128]
                    for hh in range(2):
                        sc = (cb_g * _decay(acs, acs_t, 2 * pair + hh)).astype(BF16)
                        yp = yp + _dot(sc, jnp.where(_half_mask(hh), xp, 0.0).astype(BF16))
                    ys.append(yp)
            y = jnp.concatenate(ys, axis=1) + d_ref[...] * xs
            state_s[...] = prev * jnp.exp(alast) + jnp.concatenate(new_states, axis=1)
            ypre_ref[pl.ds(r0, CHUNK), :] = y
            _, yn, _ = _gate_norm(y, z_ref[pl.ds(r0, CHUNK), :], None)
            yssd_ref[pl.ds(r0, CHUNK), :] = (yn * nw_ref[...]).astype(BF16)
            return carry

        lax.fori_loop(0, ncb, chunk, 0)

    return pl.pallas_call(
        body, name="ssd_fwd", grid=(nb,),
        in_specs=[_rows(SSD_ROWS, CONV_DIM), _rows(SSD_ROWS, HEAD_PAD), _rows(SSD_ROWS, SSD_INNER),
                  _full((CONV_W, CONV_DIM)), _full((1, CONV_DIM)), _full((1, SSD_INNER)), _full((1, SSD_INNER)),
                  _full((1, SSD_INNER)), _full((1, SSD_INNER)), _full((HEAD_PAD, SSD_INNER)), _full((CHUNK, CHUNK)),
                  _full((CHUNK, CHUNK))],
        out_specs=[_rows(SSD_ROWS, CONV_DIM), pl.BlockSpec((ncb, SSD_N, SSD_INNER), lambda i: (i, 0, 0)),
                   _rows(SSD_ROWS, SSD_INNER), _rows(SSD_ROWS, SSD_INNER)],
        out_shape=[jax.ShapeDtypeStruct((s, CONV_DIM), F32), jax.ShapeDtypeStruct((s // CHUNK, SSD_N, SSD_INNER), F32),
                   jax.ShapeDtypeStruct((s, SSD_INNER), F32), jax.ShapeDtypeStruct((s, SSD_INNER), BF16)],
        scratch_shapes=[pltpu.VMEM((8, CONV_DIM), F32), pltpu.VMEM((SSD_N, SSD_INNER), F32)],
        compiler_params=_params(),
    )(xraw, misc, z, cw, cb, dtb, a_exp, d_exp, nw, emisc, tri, trit)


def _outproj_fwd(oe, yssd, wout, h, nw):
    s = h.shape[0]
    wide = MLA_HEADS * HEAD_PAD

    def body(oe_ref, y_ref, w_ref, h_ref, nw_ref, mixed_ref, h1_ref):
        mixed = _dot(oe_ref[...], w_ref[0:wide, :]) + _dot(y_ref[...], w_ref[wide:, :])
        mixed_ref[...] = mixed
        h1_ref[...] = h_ref[...] + _rms_fwd(mixed, nw_ref[...])

    return pl.pallas_call(
        body, name="outproj_fwd", grid=(s // TM,),
        in_specs=[_rows(TM, wide), _rows(TM, SSD_INNER), _resident((wide + SSD_INNER, D_MODEL)), _rows(TM, D_MODEL),
                  _full((1, D_MODEL))],
        out_specs=[_rows(TM, D_MODEL), _rows(TM, D_MODEL)],
        out_shape=[jax.ShapeDtypeStruct((s, D_MODEL), F32)] * 2,
        compiler_params=_params(),
    )(oe, yssd, wout, h, nw)


def _mlp_fwd(h1, prew, wup, wdown, postw):
    s = h1.shape[0]
    fb = D_FF // N_DEV

    def body(h_ref, prew_ref, up_ref, down_ref, postw_ref, mb_ref, d_ref, h2_ref):
        hh = h_ref[...]
        mb = _rms_fwd(hh, prew_ref[...]).astype(BF16)
        mb_ref[...] = mb
        d = jnp.zeros((TM, D_MODEL), F32)
        for j in range(N_DEV):
            a = _dot(mb, up_ref[j])
            r = jnp.square(jnp.maximum(a, 0.0)).astype(BF16)
            d = d + _dot(r, down_ref[j])
        d_ref[...] = d
        h2_ref[...] = hh + _rms_fwd(d, postw_ref[...])

    return pl.pallas_call(
        body, name="mlp_fwd", grid=(s // TM,),
        in_specs=[_rows(TM, D_MODEL), _full((1, D_MODEL)), _resident((N_DEV, D_MODEL, fb)), _resident((N_DEV, fb, D_MODEL)),
                  _full((1, D_MODEL))],
        out_specs=[_rows(TM, D_MODEL)] * 3,
        out_shape=[jax.ShapeDtypeStruct((s, D_MODEL), BF16), jax.ShapeDtypeStruct((s, D_MODEL), F32),
                   jax.ShapeDtypeStruct((s, D_MODEL), F32)],
        compiler_params=_params(),
    )(h1, prew, wup, wdown, postw)


def _loss_grad(h, target):
    s = h.shape[0]

    def body(h_ref, t_ref, dh_ref, loss_ref):
        diff = h_ref[...] - t_ref[...]
        dh_ref[...] = diff * (1.0 / D_MODEL)
        part = 0.5 * jnp.sum(jnp.mean(diff * diff, axis=-1, keepdims=True), axis=0, keepdims=True)
        _acc_rows(loss_ref, part, pl.program_id(0) == 0)

    return pl.pallas_call(
        body, name="loss_grad", grid=(s // TM,),
        in_specs=[_rows(TM, D_MODEL)] * 2,
        out_specs=[_rows(TM, D_MODEL), _full((1, 1))],
        out_shape=[jax.ShapeDtypeStruct((s, D_MODEL), F32), jax.ShapeDtypeStruct((1, 1), F32)],
    )(h, target)


def _mlp_bwd(dh2, d, h1, mb, prew, wup, wdown, postw):
    s = dh2.shape[0]
    fb = D_FF // N_DEV
    tm = TM // 2

    def body(dh2_ref, d_ref, h1_ref, mb_ref, prew_ref, up_ref, down_ref, postw_ref,
             dh1_ref, da_ref, r_ref, dd_ref, gpost_ref, gpre_ref):
        first = pl.program_id(0) == 0
        dh2 = dh2_ref[...]
        dd, gpost = _rms_bwd(d_ref[...], postw_ref[...], dh2)
        _acc_rows(gpost_ref, gpost, first)
        ddb = dd.astype(BF16)
        dd_ref[...] = ddb
        mb = mb_ref[...]
        dm = jnp.zeros((tm, D_MODEL), F32)
        for j in range(N_DEV):
            a = jnp.maximum(_dot(mb, up_ref[j]), 0.0)
            r_ref[j] = jnp.square(a).astype(BF16)
            da = (_dot_nt(ddb, down_ref[j]) * (2.0 * a)).astype(BF16)
            da_ref[j] = da
            dm = dm + _dot_nt(da, up_ref[j])
        dx, gpre = _rms_bwd(h1_ref[...], prew_ref[...], dm)
        _acc_rows(gpre_ref, gpre, first)
        dh1_ref[...] = dh2 + dx

    stacked = pl.BlockSpec((N_DEV, tm, fb), lambda i: (0, i, 0))
    return pl.pallas_call(
        body, name="mlp_bwd", grid=(s // tm,),
        in_specs=[_rows(tm, D_MODEL)] * 4 + [_full((1, D_MODEL)), _resident((N_DEV, D_MODEL, fb)), _resident((N_DEV, fb, D_MODEL)),
                                              _full((1, D_MODEL))],
        out_specs=[_rows(tm, D_MODEL), stacked, stacked, _rows(tm, D_MODEL), _full((1, D_MODEL)), _full((1, D_MODEL))],
        out_shape=[jax.ShapeDtypeStruct((s, D_MODEL), F32), jax.ShapeDtypeStruct((N_DEV, s, fb), BF16),
                   jax.ShapeDtypeStruct((N_DEV, s, fb), BF16), jax.ShapeDtypeStruct((s, D_MODEL), BF16),
                   jax.ShapeDtypeStruct((1, D_MODEL), F32), jax.ShapeDtypeStruct((1, D_MODEL), F32)],
        compiler_params=_params(),
    )(dh2, d, h1, mb, prew, wup, wdown, postw)


def _matmul_tn(a, b, name, tk=TK_DW):
    s, m = a.shape
    n = b.shape[1]
    tn = n if n <= 1024 else (n // 2 if (n // 2) % 128 == 0 else n // 3)
    tk = min(tk, s)
    assert n % tn == 0 and tn % 128 == 0 and s % tk == 0

    def body(a_ref, b_ref, o_ref):
        part = _dot_tn(a_ref[...], b_ref[...])

        @pl.when(pl.program_id(1) == 0)
        def _():
            o_ref[...] = part

        @pl.when(pl.program_id(1) != 0)
        def _():
            o_ref[...] += part

    return pl.pallas_call(
        body, name=name, grid=(n // tn, s // tk),
        in_specs=[pl.BlockSpec((tk, m), lambda j, k: (k, 0)), pl.BlockSpec((tk, tn), lambda j, k: (k, j))],
        out_specs=pl.BlockSpec((m, tn), lambda j, k: (0, j)),
        out_shape=jax.ShapeDtypeStruct((m, n), F32),
        compiler_params=_params(),
    )(a, b)


def _matmul_tn_stacked(a, b, name, a_stacked, tk=TK_DW):
    tk = min(tk, a.shape[-2])
    if a_stacked:
        _, s, m = a.shape
        n = b.shape[1]
        in_specs = [pl.BlockSpec((1, tk, m), lambda j, k: (j, k, 0)), pl.BlockSpec((tk, n), lambda j, k: (k, 0))]
    else:
        s, m = a.shape
        n = b.shape[2]
        in_specs = [pl.BlockSpec((tk, m), lambda j, k: (k, 0)), pl.BlockSpec((1, tk, n), lambda j, k: (j, k, 0))]

    nk = s // tk

    def body(a_ref, b_ref, o_ref, acc_s):
        av = a_ref[0] if a_stacked else a_ref[...]
        bv = b_ref[...] if a_stacked else b_ref[0]
        part = _dot_tn(av, bv)
        k = pl.program_id(1)

        @pl.when(k == 0)
        def _():
            acc_s[...] = part

        @pl.when(jnp.logical_and(k != 0, k != nk - 1))
        def _():
            acc_s[...] += part

        @pl.when(k == nk - 1)
        def _():
            o_ref[0] = (part if nk == 1 else acc_s[...] + part).astype(BF16)

    return pl.pallas_call(
        body, name=name, grid=(N_DEV, nk),
        in_specs=in_specs,
        out_specs=pl.BlockSpec((1, m, n), lambda j, k: (j, 0, 0)),
        out_shape=jax.ShapeDtypeStruct((N_DEV, m, n), BF16),
        scratch_shapes=[pltpu.VMEM((m, n), F32)],
        compiler_params=_params(),
    )(a, b)


def _outproj_bwd(dh1, mixed, nw, wout):
    s = dh1.shape[0]
    wide = MLA_HEADS * HEAD_PAD

    def body(dh1_ref, mixed_ref, nw_ref, w_ref, dmix_ref, doe_ref, dy_ref, gnw_ref):
        dmix, gnw = _rms_bwd(mixed_ref[...], nw_ref[...], dh1_ref[...])
        _acc_rows(gnw_ref, gnw, pl.program_id(0) == 0)
        dmb = dmix.astype(BF16)
        dmix_ref[...] = dmb
        doe_ref[...] = _dot_nt(dmb, w_ref[0:wide, :]).astype(BF16)
        dy_ref[...] = _dot_nt(dmb, w_ref[wide:, :])

    return pl.pallas_call(
        body, name="outproj_bwd", grid=(s // TM,),
        in_specs=[_rows(TM, D_MODEL), _rows(TM, D_MODEL), _full((1, D_MODEL)), _resident((wide + SSD_INNER, D_MODEL))],
        out_specs=[_rows(TM, D_MODEL), _rows(TM, wide), _rows(TM, SSD_INNER), _full((1, D_MODEL))],
        out_shape=[jax.ShapeDtypeStruct((s, D_MODEL), BF16), jax.ShapeDtypeStruct((s, wide), BF16),
                   jax.ShapeDtypeStruct((s, SSD_INNER), F32), jax.ShapeDtypeStruct((1, D_MODEL), F32)],
        compiler_params=_params(),
    )(dh1, mixed, nw, wout)


def _attn_delta(o, do):
    s = o.shape[0]
    wide = MLA_HEADS * HEAD_PAD

    def body(o_ref, do_ref, d_ref):
        ones = jnp.ones((8, HEAD_PAD), BF16)
        for hd in range(MLA_HEADS):
            cols = slice(hd * HEAD_PAD, (hd + 1) * HEAD_PAD)
            prod = o_ref[:, cols].astype(F32) * do_ref[:, cols].astype(F32)
            d_ref[hd] = _dot01(prod, ones, dot=_dot_nt, left=True)

    return pl.pallas_call(
        body, name="attn_delta", grid=(s // TM,),
        in_specs=[_rows(TM, wide), _rows(TM, wide)],
        out_specs=pl.BlockSpec((MLA_HEADS, 8, TM), lambda i: (0, 0, i)),
        out_shape=jax.ShapeDtypeStruct((MLA_HEADS, 8, s), F32),
    )(o, do)


def _attn_bwd(q, k, v, do, lse, delta, exchange=()):
    s = q.shape[0]
    t = ATT_T
    nq = s // t
    pair = 2 * HEAD_PAD
    ne = len(exchange)

    def body(q_ref, k_ref, v_ref, do_ref, lse_ref, delta_ref, *rest):
        e_in, (dq_ref, dk_ref, dv_ref), e_out = rest[:ne], rest[ne:ne + 3], rest[ne + 3:2 * ne + 3]
        kb = pl.program_id(1)
        _hosted_comm("exchange", e_in, e_out, rest[2 * ne + 3:],
                     jnp.logical_and(pl.program_id(0) == 0, kb == 0),
                     jnp.logical_and(pl.program_id(0) == MLA_HEADS // 2 - 1, kb == nq - 1))

        @pl.when(kb == 0)
        def _():
            dq_ref[...] = jnp.zeros(dq_ref.shape, F32)

        dk_ref[...] = jnp.zeros(dk_ref.shape, F32)
        dv_ref[...] = jnp.zeros(dv_ref.shape, F32)

        def step(qb, masked):
            r0 = pl.multiple_of(qb * t, t)
            for hh in range(2):
                cols = slice(hh * HEAD_PAD, (hh + 1) * HEAD_PAD)
                kk = k_ref[:, cols]
                qq = q_ref[pl.ds(r0, t), cols]
                dd = do_ref[pl.ds(r0, t), cols]
                sc = _dot_nt(kk, qq) * ATT_SCALE
                if masked:
                    sc = jnp.where(_chunk_mask(t, keys_on_rows=True), sc, -jnp.inf)
                p = jnp.exp(sc - lse_ref[hh, 0:1, pl.ds(r0, t)])
                dv_ref[:, cols] += _dot(p.astype(BF16), dd)
                dp = _dot_nt(v_ref[:, cols], dd)
                ds = (p * (dp - delta_ref[hh, 0:1, pl.ds(r0, t)]) * ATT_SCALE).astype(BF16)
                dk_ref[:, cols] += _dot(ds, qq)
                dq_ref[pl.ds(r0, t), cols] += _dot_tn(ds, kk)

        def loop(qb, c):
            step(qb, False)
            return c

        step(kb, True)
        lax.fori_loop(kb + 1, nq, loop, 0)

    whole = pl.BlockSpec((s, pair), lambda h, i: (0, h))
    tile = pl.BlockSpec((t, pair), lambda h, i: (i, h))
    rowvec = pl.BlockSpec((2, 8, s), lambda h, i: (h, 0, 0))
    wide = MLA_HEADS * HEAD_PAD
    outs = pl.pallas_call(
        body, name="attn_bwd_exchange" if ne else "attn_bwd", grid=(MLA_HEADS // 2, nq),
        in_specs=[whole, tile, tile, whole, rowvec, rowvec] + [_ANY] * ne,
        out_specs=[whole, tile, tile] + [_ANY] * ne,
        out_shape=[jax.ShapeDtypeStruct((s, wide), F32)] * 3 + _comm_out_shapes("exchange", exchange),
        scratch_shapes=_comm_scratch(ne) if ne else [],
        compiler_params=_params(),
    )(q, k, v, do, lse, delta, *exchange)
    return outs[0], outs[1], outs[2], list(outs[3:])


def _ssd_bwd(dy, ypre, z, c, xraw, misc, prev, cw, dtb, a_exp, d_exp, nw, consts):
    s = dy.shape[0]
    nb = s // SSD_ROWS
    ncb = SSD_ROWS // CHUNK
    emisc, emisc_t, tri, trit = consts

    def body(dy_ref, ypre_ref, z_ref, c_ref, x_ref, xprev_ref, misc_ref, prev_ref, cw_ref, dtb_ref, a_ref, d_ref, nw_ref,
             emisc_ref, emisct_ref, tri_ref, trit_ref,
             dz_ref, dx_ref, dmisc_ref, gnw_ref, gd_ref, galog_ref, gdtb_ref, gcw_ref, gcb_ref,
             dst_s, dc_s, head_s):
        i = pl.program_id(0)
        first = i == 0

        @pl.when(first)
        def _():
            dst_s[...] = jnp.zeros(dst_s.shape, F32)
            head_s[...] = jnp.zeros(head_s.shape, F32)
            gnw_ref[...] = jnp.zeros(gnw_ref.shape, F32)
            gd_ref[...] = jnp.zeros(gd_ref.shape, F32)
            galog_ref[...] = jnp.zeros(galog_ref.shape, F32)
            gdtb_ref[...] = jnp.zeros(gdtb_ref.shape, F32)

        a_exp_v = a_ref[...]
        a8 = _dot01(a_exp_v, emisct_ref[...]) * (1.0 / SSD_P)

        def chunk(cr, carry):
            ci = ncb - 1 - cr
            r0 = pl.multiple_of(ci * CHUNK, CHUNK)
            cc = c_ref[pl.ds(r0, CHUNK), :]
            mm = misc_ref[pl.ds(r0, CHUNK), :]
            xa, dtr, dt, acs, acs_t, alast = _ssd_chunk_common(cc, mm, emisc_ref[...], tri_ref[...], trit_ref[...],
                                                              dtb_ref[...], a_exp_v)
            xs = xa[:, :SSD_INNER]
            xdt = xs * dt
            y = ypre_ref[pl.ds(r0, CHUNK), :]
            zz = z_ref[pl.ds(r0, CHUNK), :]
            yz, yn, rs = _gate_norm(y, zz, None)
            dyo = dy_ref[pl.ds(r0, CHUNK), :]
            gnw_ref[...] += jnp.sum(dyo * yn, axis=0, keepdims=True)
            dyn = dyo * nw_ref[...]
            half = SSD_INNER // SSD_GROUPS
            dyz_parts = []
            for g in range(SSD_GROUPS):
                gl = slice(g * half, (g + 1) * half)
                dyz_parts.append(rs[g] * (dyn[:, gl] - yn[:, gl] * jnp.mean(dyn[:, gl] * yn[:, gl], axis=-1, keepdims=True)))
            dyz = jnp.concatenate(dyz_parts, axis=1)
            sg = jax.nn.sigmoid(zz)
            dz_ref[pl.ds(r0, CHUNK), :] = dyz * y * (sg * (1.0 + zz * (1.0 - sg)))
            dyp = dyz * (zz * sg)
            dypb = dyp.astype(BF16)
            gd_ref[...] += jnp.sum(dyp * xs, axis=0, keepdims=True)
            prev = prev_ref[ci]
            dst = dst_s[...]
            cd = jnp.exp(alast)
            e = jnp.exp(acs)
            dsx = jnp.exp(alast - acs)
            wgt = (xdt * dsx).astype(BF16)
            dze = (dyp * e).astype(BF16)
            glast = jnp.sum(dst * prev, axis=0, keepdims=True) * cd
            dprev_parts, dxdt_parts, dxdt_state_parts, dbm, dcm, yoff_parts = [], [], [], [], [], []
            lane8 = lax.broadcasted_iota(jnp.int32, (CHUNK, HEAD_PAD), 1)
            diag8 = jnp.zeros((CHUNK, HEAD_PAD), F32)
            for g in range(SSD_GROUPS):
                gl = slice(g * 256, (g + 1) * 256)
                bm = xa[:, SSD_INNER + g * SSD_N:SSD_INNER + (g + 1) * SSD_N].astype(BF16)
                cm = xa[:, SSD_INNER + SSD_GROUPS * SSD_N + g * SSD_N:SSD_INNER + SSD_GROUPS * SSD_N + (g + 1) * SSD_N].astype(BF16)
                prev_g = prev[:, gl].astype(BF16)
                dst_g = dst[:, gl].astype(BF16)
                dcm_g = _dot_nt(dze[:, gl], prev_g)
                dprev_parts.append(_dot_tn(cm, dze[:, gl]))
                dxs_state = _dot(bm, dst_g) * dsx[:, gl]
                dbm_g = _dot_nt(wgt[:, gl], dst_g)
                cb_g = _dot_nt(cm, bm)
                dcb = jnp.zeros((CHUNK, CHUNK), F32)
                diag_parts = []
                for jj in range(2):
                    pair = 2 * g + jj
                    pl_ = slice(pair * 128, (pair + 1) * 128)
                    xp = xdt[:, pl_]
                    dyp_p = dypb[:, pl_]
                    dxp = jnp.zeros((CHUNK, 128), F32)
                    for hh in range(2):
                        hd = 2 * pair + hh
                        dec = _decay(acs, acs_t, hd)
                        xm = jnp.where(_half_mask(hh), xp, 0.0).astype(BF16)
                        dsc = _dot_nt(dyp_p, xm) * dec
                        dcb = dcb + dsc
                        sc = (cb_g * dec).astype(BF16)
                        dxp = dxp + jnp.where(_half_mask(hh), _dot_tn(sc, dyp_p), 0.0)
                        dm = dsc * cb_g
                        diag8 = diag8 + jnp.where(lane8 == MISC_DT + hd, jnp.sum(dm - dm.T, axis=1, keepdims=True), 0.0)
                    diag_parts.append(dxp)
                dcbb = dcb.astype(BF16)
                dcm.append(dcm_g + _dot(dcbb, bm))
                dbm.append(dbm_g + _dot_tn(dcbb, cm))
                dxdt_state_parts.append(dxs_state)
                dxdt_parts.append(jnp.concatenate(diag_parts, axis=1) + dxs_state)
                yoff_parts.append(_dot(cm, prev_g) * e[:, gl])
            dxdt = jnp.concatenate(dxdt_parts, axis=1)
            dxdt_state = jnp.concatenate(dxdt_state_parts, axis=1)
            dst_s[...] = dst * cd + jnp.concatenate(dprev_parts, axis=1)
            dacs = dyp * jnp.concatenate(yoff_parts, axis=1) - xdt * dxdt_state
            last = jnp.sum(xdt * dxdt_state, axis=0, keepdims=True) + glast
            row = lax.broadcasted_iota(jnp.int32, (CHUNK, SSD_INNER), 0)
            dacs = dacs + jnp.where(row == CHUNK - 1, last, 0.0)
            dacs8 = _dot01(dacs, emisct_ref[...]) + diag8
            da8 = _dot01(dacs8, trit_ref[...], left=True)
            ddt8 = da8 * a8 + _dot01(dxdt * xs, emisct_ref[...])
            dtr8 = mm + _dot01(dtb_ref[...], emisct_ref[...]) * (1.0 / SSD_P)
            dt8 = jax.nn.softplus(dtr8)
            lane = lax.broadcasted_iota(jnp.int32, (CHUNK, HEAD_PAD), 1)
            on_dt = jnp.logical_and(lane >= MISC_DT, lane < MISC_DT + SSD_HEADS)
            ddtr8 = jnp.where(on_dt, ddt8 * jax.nn.sigmoid(dtr8), 0.0)
            dmisc_ref[pl.ds(r0, CHUNK), :] = ddtr8
            gdtb_ref[...] += jnp.sum(ddtr8, axis=0, keepdims=True)
            galog_ref[...] += jnp.sum(jnp.where(on_dt, da8 * dt8, 0.0), axis=0, keepdims=True) * a8
            dxs = d_ref[...] * dyp + dxdt * dt
            dxa = jnp.concatenate([dxs] + dbm + dcm, axis=1)
            sc_ = jax.nn.sigmoid(cc)
            dc_s[pl.ds(r0, CHUNK), :] = dxa * (sc_ * (1.0 + cc * (1.0 - sc_)))
            return carry

        lax.fori_loop(0, ncb, chunk, 0)

        dc = dc_s[...]
        dcext = jnp.concatenate([dc, head_s[...]], axis=0)
        dx = dc * cw_ref[CONV_W - 1:CONV_W, :]
        for j in range(1, CONV_W):
            dx = dx + pltpu.roll(dcext, SSD_ROWS + 8 - j, 0)[:SSD_ROWS, :] * cw_ref[CONV_W - 1 - j:CONV_W - j, :]
        dx_ref[...] = dx
        head_s[...] = dc[:8, :]
        xprev = jnp.where(i == nb - 1, 0.0, xprev_ref[...])
        xext = jnp.concatenate([xprev, x_ref[...]], axis=0)
        rows = [jnp.sum(dc * pltpu.roll(xext, CONV_W - 1 - kk, 0)[8:, :], axis=0, keepdims=True) for kk in range(CONV_W)]
        gcw = jnp.concatenate(rows, axis=0)

        @pl.when(first)
        def _():
            gcw_ref[...] = gcw
            gcb_ref[...] = jnp.sum(dc, axis=0, keepdims=True)

        @pl.when(jnp.logical_not(first))
        def _():
            gcw_ref[...] += gcw
            gcb_ref[...] += jnp.sum(dc, axis=0, keepdims=True)

    def rev(width):
        return pl.BlockSpec((SSD_ROWS, width), lambda i: (nb - 1 - i, 0))

    per8 = SSD_ROWS // 8
    return pl.pallas_call(
        body, name="ssd_bwd", grid=(nb,),
        in_specs=[rev(SSD_INNER), rev(SSD_INNER), rev(SSD_INNER), rev(CONV_DIM), rev(CONV_DIM),
                  pl.BlockSpec((8, CONV_DIM), lambda i: (jnp.maximum((nb - 1 - i) * per8 - 1, 0), 0)),
                  rev(HEAD_PAD), pl.BlockSpec((ncb, SSD_N, SSD_INNER), lambda i: (nb - 1 - i, 0, 0)),
                  _full((CONV_W, CONV_DIM)), _full((1, SSD_INNER)), _full((1, SSD_INNER)), _full((1, SSD_INNER)),
                  _full((1, SSD_INNER)), _full((HEAD_PAD, SSD_INNER)), _full((SSD_INNER, HEAD_PAD)), _full((CHUNK, CHUNK)),
                  _full((CHUNK, CHUNK))],
        out_specs=[rev(SSD_INNER), rev(CONV_DIM), rev(HEAD_PAD), _full((1, SSD_INNER)), _full((1, SSD_INNER)),
                   _full((1, HEAD_PAD)), _full((1, HEAD_PAD)), _full((CONV_W, CONV_DIM)), _full((1, CONV_DIM))],
        out_shape=[jax.ShapeDtypeStruct((s, SSD_INNER), F32), jax.ShapeDtypeStruct((s, CONV_DIM), F32),
                   jax.ShapeDtypeStruct((s, HEAD_PAD), F32), jax.ShapeDtypeStruct((1, SSD_INNER), F32),
                   jax.ShapeDtypeStruct((1, SSD_INNER), F32), jax.ShapeDtypeStruct((1, HEAD_PAD), F32),
                   jax.ShapeDtypeStruct((1, HEAD_PAD), F32), jax.ShapeDtypeStruct((CONV_W, CONV_DIM), F32),
                   jax.ShapeDtypeStruct((1, CONV_DIM), F32)],
        scratch_shapes=[pltpu.VMEM((SSD_N, SSD_INNER), F32), pltpu.VMEM((SSD_ROWS, CONV_DIM), F32), pltpu.VMEM((8, CONV_DIM), F32)],
        compiler_params=_params(),
    )(dy, ypre, z, c, xraw, xraw, misc, prev, cw, dtb, a_exp, d_exp, nw, emisc, emisc_t, tri, trit)


def _qkv_bwd(dq, dk, dv, cq, ckv, qnw, kvnw, wuq, wkv, cosf, sinf):
    s = dq.shape[0]
    wide = MLA_HEADS * HEAD_PAD

    def body(dq_ref, dk_ref, dv_ref, cq_ref, ckv_ref, qnw_ref, kvnw_ref, wuq_ref, wkv_ref, cos_ref, sin_ref,
             dqb_ref, dkvb_ref, dcq_ref, dckv_ref, dmisc_ref, gq_ref, gkv_ref):
        first = pl.program_id(0) == 0
        cosf, sinf = cos_ref[...], sin_ref[...]
        dkr = jnp.zeros((TM, HEAD_PAD), F32)
        for hd in range(MLA_HEADS):
            cols = slice(hd * HEAD_PAD, (hd + 1) * HEAD_PAD)
            dqb_ref[:, cols] = _rope(dq_ref[:, cols], cosf, sinf, -1.0).astype(BF16)
            dkh = dk_ref[:, cols]
            dkvb_ref[:, cols] = dkh.astype(BF16)
            dkr = dkr + dkh
        dkvb_ref[:, wide:] = dv_ref[...].astype(BF16)
        lane = lax.broadcasted_iota(jnp.int32, dkr.shape, 1)
        in_rope = jnp.logical_and(lane >= MISC_ROPE, lane < MISC_ROPE + QK_ROPE)
        dmisc_ref[...] = jnp.where(in_rope, _rope(jnp.where(in_rope, dkr, 0.0), cosf, sinf, -1.0), 0.0)
        dcq, gq = _rms_bwd(cq_ref[...], qnw_ref[...], _dot_nt(dqb_ref[...], wuq_ref[...]))
        dcq_ref[...] = dcq
        _acc_rows(gq_ref, gq, first)
        dckv, gkv = _rms_bwd(ckv_ref[...], kvnw_ref[...], _dot_nt(dkvb_ref[...], wkv_ref[...]))
        dckv_ref[...] = dckv
        _acc_rows(gkv_ref, gkv, first)

    return pl.pallas_call(
        body, name="qkv_bwd", grid=(s // TM,),
        in_specs=[_rows(TM, wide)] * 3 + [_rows(TM, Q_RANK), _rows(TM, KV_RANK), _full((1, Q_RANK)), _full((1, KV_RANK)),
                                          _resident((Q_RANK, wide)), _resident((KV_RANK, 2 * wide)), _rows(TM, HEAD_PAD), _rows(TM, HEAD_PAD)],
        out_specs=[_rows(TM, wide), _rows(TM, 2 * wide), _rows(TM, Q_RANK), _rows(TM, KV_RANK), _rows(TM, HEAD_PAD),
                   _full((1, Q_RANK)), _full((1, KV_RANK))],
        out_shape=[jax.ShapeDtypeStruct((s, wide), BF16), jax.ShapeDtypeStruct((s, 2 * wide), BF16),
                   jax.ShapeDtypeStruct((s, Q_RANK), F32), jax.ShapeDtypeStruct((s, KV_RANK), F32),
                   jax.ShapeDtypeStruct((s, HEAD_PAD), F32), jax.ShapeDtypeStruct((1, Q_RANK), F32),
                   jax.ShapeDtypeStruct((1, KV_RANK), F32)],
        compiler_params=_params(),
    )(dq, dk, dv, cq, ckv, qnw, kvnw, wuq, wkv, cosf, sinf)


def _inproj_bwd(dcq, dckv, dmisc_rope, dmisc_dt, dz, dxbc, h, dh1, nw, win):
    s = h.shape[0]

    def body(dcq_ref, dckv_ref, dma_ref, dmb_ref, dz_ref, dxbc_ref, h_ref, dh1_ref, nw_ref, w_ref, dproj_ref, dh0_ref, gnw_ref):
        dproj_ref[:, 0:768] = dcq_ref[...].astype(BF16)
        dproj_ref[:, 768:1024] = dckv_ref[...].astype(BF16)
        dproj_ref[:, 1024:1152] = (dma_ref[...] + dmb_ref[...]).astype(BF16)
        dproj_ref[:, 1152:1664] = dz_ref[...].astype(BF16)
        dproj_ref[:, 1664:2688] = dxbc_ref[...].astype(BF16)
        du = _dot_nt(dproj_ref[...], w_ref[...])
        dx, gnw = _rms_bwd(h_ref[...], nw_ref[...], du)
        _acc_rows(gnw_ref, gnw, pl.program_id(0) == 0)
        dh0_ref[...] = dh1_ref[...] + dx

    return pl.pallas_call(
        body, name="inproj_bwd", grid=(s // TM,),
        in_specs=[_rows(TM, Q_RANK), _rows(TM, KV_RANK), _rows(TM, HEAD_PAD), _rows(TM, HEAD_PAD), _rows(TM, SSD_INNER),
                  _rows(TM, CONV_DIM), _rows(TM, D_MODEL), _rows(TM, D_MODEL), _full((1, D_MODEL)), _resident((D_MODEL, IN_PAD))],
        out_specs=[_rows(TM, IN_PAD), _rows(TM, D_MODEL), _full((1, D_MODEL))],
        out_shape=[jax.ShapeDtypeStruct((s, IN_PAD), BF16), jax.ShapeDtypeStruct((s, D_MODEL), F32),
                   jax.ShapeDtypeStruct((1, D_MODEL), F32)],
        compiler_params=_params(),
    )(dcq, dckv, dmisc_rope, dmisc_dt, dz, dxbc, h, dh1, nw, win)


def _row_tile(rows, cols):
    cap = max(8, (1 << 18) // max(cols, 128))
    best = None
    for t in range(8, rows + 1, 8):
        if rows % t == 0 and t <= cap:
            best = t
    return best if best is not None else rows


def _adamw(w, g, m, v, name):
    rows, cols = w.shape
    tr = _row_tile(rows, cols)

    def body(w_ref, g_ref, m_ref, v_ref, d_ref, m2_ref, v2_ref):
        gg = g_ref[...]
        m2 = ADAM_B1 * m_ref[...] + (1.0 - ADAM_B1) * gg
        v2 = ADAM_B2 * v_ref[...] + (1.0 - ADAM_B2) * jnp.square(gg)
        m_hat = m2 / (1.0 - ADAM_B1 ** ADAM_STEP)
        v_hat = v2 / (1.0 - ADAM_B2 ** ADAM_STEP)
        d_ref[...] = -ADAM_LR * (m_hat / (jnp.sqrt(v_hat) + ADAM_EPS) + ADAM_WD * w_ref[...])
        m2_ref[...] = m2
        v2_ref[...] = v2

    spec = pl.BlockSpec((tr, cols), lambda i: (i, 0))
    return pl.pallas_call(
        body, name=name, grid=(rows // tr,),
        in_specs=[spec] * 4, out_specs=[spec] * 3,
        out_shape=[jax.ShapeDtypeStruct((rows, cols), F32)] * 3,
    )(w, g, m, v)


def _sum_adamw(slots, w, m, v, name):
    _, rows, cols = w.shape
    tr = _row_tile(rows, cols)
    nb = rows // tr

    def body(s0_ref, s1_ref, w_ref, m_ref, v_ref, g_ref, d_ref, m2_ref, v2_ref):
        for l, ref in enumerate((s0_ref, s1_ref)):
            @pl.when(pl.program_id(0) == l)
            def _(ref=ref):
                acc = ref[0].astype(F32)
                for i in range(1, N_DEV):
                    acc = acc + ref[i].astype(F32)
                g_ref[...] = acc

        gg = g_ref[...]
        m2 = ADAM_B1 * m_ref[...] + (1.0 - ADAM_B1) * gg
        v2 = ADAM_B2 * v_ref[...] + (1.0 - ADAM_B2) * jnp.square(gg)
        m_hat = m2 / (1.0 - ADAM_B1 ** ADAM_STEP)
        v_hat = v2 / (1.0 - ADAM_B2 ** ADAM_STEP)
        d_ref[...] = -ADAM_LR * (m_hat / (jnp.sqrt(v_hat) + ADAM_EPS) + ADAM_WD * w_ref[...])
        m2_ref[...] = m2
        v2_ref[...] = v2

    slot_spec = lambda layer: pl.BlockSpec((N_DEV, tr, cols), lambda l, i: (0, jnp.where(l == layer, i, (nb - 1) * (1 - layer)), 0))
    spec = pl.BlockSpec((None, tr, cols), lambda l, i: (l, i, 0))
    return pl.pallas_call(
        body, name=name, grid=(DEPTH, nb),
        in_specs=[slot_spec(0), slot_spec(1), spec, spec, spec], out_specs=[spec] * 4,
        out_shape=[jax.ShapeDtypeStruct(w.shape, F32)] * 4,
        compiler_params=_params(),
    )(slots[0], slots[1], w, m, v)


_MESH = pl.DeviceIdType.MESH
_ANY = pl.BlockSpec(memory_space=pl.ANY)


def _my_place():
    return lax.axis_index("x"), lax.axis_index("y"), lax.axis_index("c")


def _flip(place, k):
    x, y, c = place
    return (1 - x if k & 4 else x, 1 - y if k & 2 else y, 1 - c if k & 1 else c)


def _block_id(place):
    return 4 * place[0] + 2 * place[1] + place[2]


def _peer_copies(kind, in_refs, out_refs, send_sems, recv_sems, local_sems):
    me = _my_place()
    my = _block_id(me)
    remote, local = [], []
    for a, (x_ref, out_ref) in enumerate(zip(in_refs, out_refs)):
        src_of = (lambda place, r=x_ref: r) if kind == "gather" else (lambda place, r=x_ref: r.at[_block_id(place)])
        local.append(pltpu.make_async_copy(src_of(me), out_ref.at[my], local_sems.at[a]))
        for k in range(1, N_DEV):
            peer = _flip(me, k)
            remote.append(pltpu.make_async_remote_copy(
                src_ref=src_of(peer), dst_ref=out_ref.at[my], send_sem=send_sems.at[a * 7 + k - 1],
                recv_sem=recv_sems.at[a * 7 + k - 1], device_id=peer, device_id_type=_MESH))
    return remote, local


def _comm_out_shapes(kind, arrays):
    return [jax.ShapeDtypeStruct((N_DEV, *a.shape) if kind == "gather" else a.shape, a.dtype) for a in arrays]


def _comm_scratch(n):
    return [pltpu.SemaphoreType.DMA((7 * n,)), pltpu.SemaphoreType.DMA((7 * n,)), pltpu.SemaphoreType.DMA((n,))]


def _hosted_comm(kind, in_refs, out_refs, sems, first, last):
    if not in_refs:
        return

    @pl.when(first)
    def _():
        remote, local = _peer_copies(kind, in_refs, out_refs, *sems)
        for cp in local + remote:
            cp.start()

    @pl.when(last)
    def _():
        remote, local = _peer_copies(kind, in_refs, out_refs, *sems)
        for cp in remote:
            cp.wait()
        for cp in local:
            cp.wait()


def _comm(kind, arrays, name):
    n = len(arrays)

    def body(*refs):
        remote, local = _peer_copies(kind, refs[:n], refs[n:2 * n], *refs[2 * n:])
        for cp in local + remote:
            cp.start()
        for cp in remote:
            cp.wait()
        for cp in local:
            cp.wait()

    return pl.pallas_call(
        body, name=name, out_shape=_comm_out_shapes(kind, arrays),
        in_specs=[_ANY] * n, out_specs=[_ANY] * n, scratch_shapes=_comm_scratch(n),
    )(*arrays)


def _all_reduce_small(part):
    rows, lanes = part.shape
    vmem = pl.BlockSpec(memory_space=pltpu.VMEM)

    def body(x_ref, gath_ref, sum_ref, send_sems, recv_sems):
        me = _my_place()
        my = _block_id(me)
        gath_ref[my] = x_ref[...]
        copies = []
        for k in range(1, N_DEV):
            cp = pltpu.make_async_remote_copy(
                src_ref=x_ref, dst_ref=gath_ref.at[my], send_sem=send_sems.at[k - 1], recv_sem=recv_sems.at[k - 1],
                device_id=_flip(me, k), device_id_type=_MESH)
            cp.start()
            copies.append(cp)
        for cp in copies:
            cp.wait()
        acc = gath_ref[0]
        for i in range(1, N_DEV):
            acc = acc + gath_ref[i]
        sum_ref[...] = acc

    return pl.pallas_call(
        body, name="small_grad_all_reduce",
        out_shape=[jax.ShapeDtypeStruct((N_DEV, rows, lanes), F32), jax.ShapeDtypeStruct((rows, lanes), F32)],
        in_specs=[vmem], out_specs=[vmem, vmem],
        scratch_shapes=[pltpu.SemaphoreType.DMA((7,)), pltpu.SemaphoreType.DMA((7,))],
    )(part)[1]


_SHARDED = (("w_in", (D_MODEL, IN_PROJ // N_DEV)), ("w_uq", (Q_RANK // N_DEV, Q_RANK)), ("w_ukv", (KV_RANK, HEAD_PAD)),
            ("conv_w", (CONV_W, CONV_DIM // N_DEV)), ("w_out", (D_MODEL // N_DEV, D_MODEL)),
            ("w_up", (D_MODEL, D_FF // N_DEV)), ("w_down", (D_FF // N_DEV, D_MODEL)))
_SMALL = (("pre_mix_norm", D_MODEL), ("q_norm", Q_RANK), ("kv_norm", KV_RANK), ("conv_b", CONV_DIM), ("dt_bias", SSD_HEADS),
          ("a_log", SSD_HEADS), ("d_skip", SSD_HEADS), ("ssd_norm", SSD_INNER), ("post_mix_norm", D_MODEL),
          ("pre_mlp_norm", D_MODEL), ("post_mlp_norm", D_MODEL))
_WEIGHT_ORDER = ("pre_mix_norm", "w_in", "q_norm", "w_uq", "kv_norm", "w_ukv", "conv_w", "conv_b", "dt_bias", "a_log", "d_skip",
                 "ssd_norm", "w_out", "post_mix_norm", "pre_mlp_norm", "w_up", "w_down", "post_mlp_norm")
_EARLY = ("w_in", "w_uq", "w_ukv", "conv_w")
_LATE = ("w_out", "w_up", "w_down")


def _wire_shard(name, a):
    return lax.bitcast_convert_type(a, BF16).reshape(CONV_W, -1) if name == "conv_w" else a.astype(BF16)


def _from_wire(name, g):
    return lax.bitcast_convert_type(g.reshape(N_DEV, CONV_W, -1, 2), F32) if name == "conv_w" else g


def _cols(stacked):
    return jnp.transpose(stacked, (1, 0, 2)).reshape(stacked.shape[1], -1)


def _early_weights(sh):
    w_in = _cols(sh["w_in"])
    zeros = lambda n: jnp.zeros((D_MODEL, n), BF16)
    s1, s2, s3, s4, s5 = 768, 1024, 1056, 1568, 2592
    win = jnp.concatenate([w_in[:, :s2], zeros(MISC_ROPE), w_in[:, s2:s3], w_in[:, s5:], zeros(HEAD_PAD - MISC_DT - SSD_HEADS),
                           w_in[:, s3:s5]], axis=1)
    w_uq = sh["w_uq"].reshape(Q_RANK, MLA_HEADS, QK_NOPE + QK_ROPE)
    wuq = jnp.pad(w_uq, ((0, 0), (0, 0), (0, HEAD_PAD - QK_NOPE - QK_ROPE))).reshape(Q_RANK, -1)
    w_ukv = _cols(sh["w_ukv"]).reshape(KV_RANK, MLA_HEADS, QK_NOPE + V_DIM)
    wkn = jnp.pad(w_ukv[..., :QK_NOPE], ((0, 0), (0, 0), (0, HEAD_PAD - QK_NOPE))).reshape(KV_RANK, -1)
    wv = w_ukv[..., QK_NOPE:].reshape(KV_RANK, 4, 2, 1, V_DIM) * jnp.eye(2, dtype=BF16).reshape(1, 1, 2, 2, 1)
    wkv = jnp.concatenate([wkn, wv.reshape(KV_RANK, -1)], axis=1)
    return dict(win=win, wuq=wuq, wkv=wkv, conv_w=_cols(sh["conv_w"]))


def _late_weights(sh):
    w_out = sh["w_out"].reshape(D_MODEL, D_MODEL)
    watt = w_out[:SSD_INNER].reshape(4, 2, 1, V_DIM, D_MODEL) * jnp.eye(2, dtype=BF16).reshape(1, 2, 2, 1, 1)
    wout = jnp.concatenate([watt.reshape(MLA_HEADS * HEAD_PAD, D_MODEL), w_out[SSD_INNER:]], axis=0)
    return dict(wout=wout, wup=sh["w_up"], wdown=sh["w_down"])


def _shard_grads(g):
    out = {}
    if "wup" in g:
        out["w_up"], out["w_down"] = g["wup"], g["wdown"]
        ae = g["wout_att"].reshape(4, 2, 2, V_DIM, D_MODEL)
        att = jnp.stack([ae[:, 0, 0], ae[:, 1, 1]], axis=1).reshape(SSD_INNER, D_MODEL)
        out["w_out"] = jnp.concatenate([att, g["wout_ssd"]], axis=0).astype(BF16).reshape(N_DEV, D_MODEL // N_DEV, D_MODEL)
    if "win" not in g:
        return out
    dwin = g["win"]
    s1, s2 = 768, 1024
    m0 = s2
    w_in = jnp.concatenate([dwin[:, :s2], dwin[:, m0 + MISC_ROPE:m0 + MISC_ROPE + QK_ROPE], dwin[:, 1152:2688],
                            dwin[:, m0 + MISC_DT:m0 + MISC_DT + SSD_HEADS]], axis=1)
    out["w_in"] = jnp.transpose(w_in.astype(BF16).reshape(D_MODEL, N_DEV, -1), (1, 0, 2))
    w_uq = g["wuq"].astype(BF16).reshape(Q_RANK, MLA_HEADS, HEAD_PAD)[..., :QK_NOPE + QK_ROPE].reshape(Q_RANK, Q_RANK)
    out["w_uq"] = w_uq.reshape(N_DEV, Q_RANK // N_DEV, Q_RANK)
    wide = MLA_HEADS * HEAD_PAD
    wkv = g["wkv"].astype(BF16)
    kn = wkv[:, :wide].reshape(KV_RANK, MLA_HEADS, HEAD_PAD)[..., :QK_NOPE]
    ve = wkv[:, wide:].reshape(KV_RANK, 4, 2, 2, V_DIM)
    vv = jnp.stack([ve[:, :, 0, 0], ve[:, :, 1, 1]], axis=2).reshape(KV_RANK, MLA_HEADS, V_DIM)
    out["w_ukv"] = jnp.transpose(jnp.concatenate([kn, vv], axis=-1), (1, 0, 2))
    out["conv_w"] = jnp.transpose(g["conv_w"].astype(BF16).reshape(CONV_W, N_DEV, -1), (1, 0, 2))
    return out


def _small_rows(n):
    return -(-n // 128)


def _pack_small(vals):
    rows = []
    for l in range(DEPTH):
        for name, n in _SMALL:
            r = _small_rows(n)
            rows.append(jnp.pad(vals[name][l].reshape(-1), (0, r * 128 - n)).reshape(r, 128))
    total = sum(r.shape[0] for r in rows)
    rows.append(jnp.zeros((-total % 8, 128), F32))
    return jnp.concatenate(rows, axis=0)


def _unpack_small(packed):
    out, off = {name: [] for name, _ in _SMALL}, 0
    for l in range(DEPTH):
        for name, n in _SMALL:
            r = _small_rows(n)
            out[name].append(packed[off:off + r].reshape(-1)[:n])
            off += r
    return {name: jnp.stack(v) for name, v in out.items()}


def _lane_rows(vec8):
    return jnp.repeat(vec8, SSD_P).reshape(1, SSD_INNER)


def _layer_fwd(h, kw, sm, l, cosf, sinf, consts, gather=(), after_gather=None):
    row = lambda name: sm[name][l].reshape(1, -1)
    t = {}
    t["h0"] = h
    t["ub"], t["cq"], t["ckv"], t["misc"], t["z"], t["xraw"] = _inproj_fwd(h, row("pre_mix_norm"), kw["win"])
    t["cqn"], t["ckvn"], t["q"], t["k"], t["v"] = _qkv_fwd(t["cq"], t["ckv"], t["misc"], row("q_norm"), row("kv_norm"),
                                                         kw["wuq"], kw["wkv"], cosf, sinf)
    t["oe"], t["lse"], gathered = _attn_fwd(t["q"], t["k"], t["v"], gather)
    if after_gather is not None:
        after_gather(gathered)
    t["dtb"] = _lane_rows(sm["dt_bias"][l])
    t["a_exp"] = _lane_rows(-jnp.exp(sm["a_log"][l]))
    t["d_exp"] = _lane_rows(sm["d_skip"][l])
    t["c"], t["prev"], t["ypre"], t["yssd"] = _ssd_fwd(t["xraw"], t["misc"], t["z"], kw["conv_w"], row("conv_b"), t["dtb"],
                                                     t["a_exp"], t["d_exp"], row("ssd_norm"), consts)
    t["mixed"], t["h1"] = _outproj_fwd(t["oe"], t["yssd"], kw["wout"], h, row("post_mix_norm"))
    t["mb"], t["d"], h2 = _mlp_fwd(t["h1"], row("pre_mlp_norm"), kw["wup"], kw["wdown"], row("post_mlp_norm"))
    return h2, t


def _layer_bwd(dh2, t, kw, sm, l, cosf, sinf, consts, exchange_of=None):
    row = lambda name: sm[name][l].reshape(1, -1)
    g, gs = {}, {}
    dh1, dab, rb, ddb, gs["post_mlp_norm"], gs["pre_mlp_norm"] = _mlp_bwd(
        dh2, t["d"], t["h1"], t["mb"], row("pre_mlp_norm"), kw["wup"], kw["wdown"], row("post_mlp_norm"))
    g["wup"] = _matmul_tn_stacked(t["mb"], dab, f"dw_up_{l}", a_stacked=False)
    g["wdown"] = _matmul_tn_stacked(rb, ddb, f"dw_down_{l}", a_stacked=True)
    dmixb, doe, dyssd, gs["post_mix_norm"] = _outproj_bwd(dh1, t["mixed"], row("post_mix_norm"), kw["wout"])
    g["wout_att"] = _matmul_tn(t["oe"], dmixb, f"dw_out_att_{l}")
    g["wout_ssd"] = _matmul_tn(t["yssd"], dmixb, f"dw_out_ssd_{l}")
    dz, dxraw, dmisc_dt, gs["ssd_norm"], gd, galog, gdtb, g["conv_w"], gs["conv_b"] = _ssd_bwd(
        dyssd, t["ypre"], t["z"], t["c"], t["xraw"], t["misc"], t["prev"], kw["conv_w"], t["dtb"], t["a_exp"], t["d_exp"],
        row("ssd_norm"), consts)
    gs["d_skip"] = jnp.sum(gd.reshape(SSD_HEADS, SSD_P), axis=1)
    gs["a_log"] = galog[0, MISC_DT:MISC_DT + SSD_HEADS]
    gs["dt_bias"] = gdtb[0, MISC_DT:MISC_DT + SSD_HEADS]
    dq, dk, dv, exchanged = _attn_bwd(t["q"], t["k"], t["v"], doe, t["lse"], _attn_delta(t["oe"], doe),
                                      exchange_of(g) if exchange_of is not None else ())
    dqb, dkvb, dcq, dckv, dmisc_rope, gs["q_norm"], gs["kv_norm"] = _qkv_bwd(
        dq, dk, dv, t["cq"], t["ckv"], row("q_norm"), row("kv_norm"), kw["wuq"], kw["wkv"], cosf, sinf)
    g["wuq"] = _matmul_tn(t["cqn"], dqb, f"dw_uq_{l}")
    g["wkv"] = _matmul_tn(t["ckvn"], dkvb, f"dw_kv_{l}")
    dprojb, dh0, gs["pre_mix_norm"] = _inproj_bwd(dcq, dckv, dmisc_rope, dmisc_dt, dz, dxraw, t["h0"], dh1,
                                                  row("pre_mix_norm"), kw["win"])
    g["win"] = _matmul_tn(t["ub"], dprojb, f"dw_in_{l}")
    return dh0, g, {k: v.reshape(-1) for k, v in gs.items()}, exchanged


def _local_step(x, positions, kws, sm, target, gather=(), after_gather=None, exchange_of=None):
    inv_freq = ROPE_THETA ** (-jnp.arange(0, QK_ROPE, 2, dtype=F32) / QK_ROPE)
    invf = jnp.zeros((HEAD_PAD,), F32).at[MISC_ROPE:MISC_ROPE + QK_ROPE].set(jnp.concatenate([inv_freq, inv_freq]))
    cosf, sinf = _rope_tables(positions.reshape(-1, 1), invf.reshape(1, HEAD_PAD))
    consts = _ssd_consts()
    h, saved = x, []
    for l in range(DEPTH):
        h, t = _layer_fwd(h, kws[l], sm, l, cosf, sinf, consts, *((gather, after_gather) if l == 0 else ()))
        saved.append(t)
    dh, loss = _loss_grad(h, target)
    grads, small, exchanged = [None] * DEPTH, [None] * DEPTH, []
    for l in reversed(range(DEPTH)):
        hook = (lambda g0: exchange_of(g0, grads[1])) if (l == 0 and exchange_of is not None) else None
        dh, grads[l], small[l], got = _layer_bwd(dh, saved[l], kws[l], sm, l, cosf, sinf, consts, hook)
        exchanged = got or exchanged
    return loss[0, 0], dh, grads, small, exchanged


def kernel(x, positions, pre_mix_norm, w_in, q_norm, w_uq, kv_norm, w_ukv, conv_w, conv_b, dt_bias, a_log, d_skip, ssd_norm, w_out, post_mix_norm, pre_mlp_norm, w_up, w_down, post_mlp_norm, loss_target, m_pre_mix_norm, m_w_in, m_q_norm, m_w_uq, m_kv_norm, m_w_ukv, m_conv_w, m_conv_b, m_dt_bias, m_a_log, m_d_skip, m_ssd_norm, m_w_out, m_post_mix_norm, m_pre_mlp_norm, m_w_up, m_w_down, m_post_mlp_norm, v_pre_mix_norm, v_w_in, v_q_norm, v_w_uq, v_kv_norm, v_w_ukv, v_conv_w, v_conv_b, v_dt_bias, v_a_log, v_d_skip, v_ssd_norm, v_w_out, v_post_mix_norm, v_pre_mlp_norm, v_w_up, v_w_down, v_post_mlp_norm):
    w = dict(pre_mix_norm=pre_mix_norm, w_in=w_in, q_norm=q_norm, w_uq=w_uq, kv_norm=kv_norm, w_ukv=w_ukv, conv_w=conv_w,
             conv_b=conv_b, dt_bias=dt_bias, a_log=a_log, d_skip=d_skip, ssd_norm=ssd_norm, w_out=w_out,
             post_mix_norm=post_mix_norm, pre_mlp_norm=pre_mlp_norm, w_up=w_up, w_down=w_down, post_mlp_norm=post_mlp_norm)
    m = dict(pre_mix_norm=m_pre_mix_norm, w_in=m_w_in, q_norm=m_q_norm, w_uq=m_w_uq, kv_norm=m_kv_norm, w_ukv=m_w_ukv,
             conv_w=m_conv_w, conv_b=m_conv_b, dt_bias=m_dt_bias, a_log=m_a_log, d_skip=m_d_skip, ssd_norm=m_ssd_norm,
             w_out=m_w_out, post_mix_norm=m_post_mix_norm, pre_mlp_norm=m_pre_mlp_norm, w_up=m_w_up, w_down=m_w_down,
             post_mlp_norm=m_post_mlp_norm)
    v = dict(pre_mix_norm=v_pre_mix_norm, w_in=v_w_in, q_norm=v_q_norm, w_uq=v_w_uq, kv_norm=v_kv_norm, w_ukv=v_w_ukv,
             conv_w=v_conv_w, conv_b=v_conv_b, dt_bias=v_dt_bias, a_log=v_a_log, d_skip=v_d_skip, ssd_norm=v_ssd_norm,
             w_out=v_w_out, post_mix_norm=v_post_mix_norm, pre_mlp_norm=v_pre_mlp_norm, w_up=v_w_up, w_down=v_w_down,
             post_mlp_norm=v_post_mlp_norm)
    sm = {name: w[name] for name, _ in _SMALL}

    wire = lambda name, l: _wire_shard(name, w[name][l])
    first = _comm("gather", [wire(name, 0) for name in _EARLY], "weight_gather_first")
    kws = [_early_weights({name: _from_wire(name, a) for name, a in zip(_EARLY, first)}), None]
    behind = [(name, 0) for name in _LATE] + [(name, 1) for name, _ in _SHARDED]

    def after_gather(gathered):
        got = {key: _from_wire(key[0], a) for key, a in zip(behind, gathered)}
        kws[0].update(_late_weights({name: got[name, 0] for name in _LATE}))
        kws[1] = {**_early_weights({name: got[name, 1] for name in _EARLY}),
                  **_late_weights({name: got[name, 1] for name in _LATE})}

    sent_behind = [(name, 1) for name, _ in _SHARDED] + [(name, 0) for name in _LATE]

    def exchange_of(g0, g1):
        blocks = {**{(name, 1): a for name, a in _shard_grads(g1).items()},
                  **{(name, 0): a for name, a in _shard_grads(g0).items()}}
        return [blocks[key] for key in sent_behind]

    loss_part, dx, grads, small, exchanged = _local_step(
        x[0], positions[0], kws, sm, loss_target[0], [wire(*key) for key in behind], after_gather, exchange_of)
    slots = dict(zip(sent_behind, exchanged))
    last = _shard_grads({k: grads[0][k] for k in ("win", "wuq", "wkv", "conv_w")})
    slots.update({(name, 0): a for name, a in zip(_EARLY, _comm("exchange", [last[name] for name in _EARLY], "grad_exchange_last"))})
    g_small = _unpack_small(_all_reduce_small(_pack_small({name: jnp.stack([small[l][name] for l in range(DEPTH)])
                                                           for name, _ in _SMALL})))
    loss = lax.psum(loss_part, ("x", "y", "c"))

    grad, delta, new_m, new_v = {}, {}, {}, {}
    for name, _ in _SHARDED:
        grad[name], delta[name], new_m[name], new_v[name] = _sum_adamw(
            [slots[name, 0], slots[name, 1]], w[name], m[name], v[name], f"sum_adamw_{name}")
    pk = lambda d: _pack_small({name: d[name] for name, _ in _SMALL})
    d_, m_, v_ = _adamw(pk(w), pk(g_small), pk(m), pk(v), "adamw_small")
    for dst, packed in ((delta, d_), (new_m, m_), (new_v, v_)):
        dst.update(_unpack_small(packed))
    grad.update(g_small)

    outs = [loss, dx[None]]
    for d in (grad, delta, new_m, new_v):
        outs += [d[name] for name in _WEIGHT_ORDER]
    return tuple(outs)
```

```python
import jax
import jax.numpy as jnp
import numpy as np
from jax import lax
from jax.experimental import pallas as pl
from jax.experimental.pallas import tpu as pltpu

F32 = jnp.float32
BF16 = jnp.bfloat16
HI = lax.Precision.HIGHEST

D_MODEL = 1024
DEPTH = 2
N_DEV = 8
CHUNK = 64
EPS = 1e-6
MLA_HEADS = 8
QK_NOPE = 64
QK_ROPE = 32
V_DIM = 64
Q_RANK = 768
KV_RANK = 256
ROPE_THETA = 10000.0
SSD_HEADS = 8
SSD_P = 64
SSD_INNER = 512
SSD_GROUPS = 2
SSD_N = 128
CONV_W = 4
CONV_DIM = 1024
D_FF = 4096
IN_PROJ = 2600
HEAD_PAD = 128
IN_PAD = 2688
MISC_ROPE = 64
MISC_DT = 96
ATT_SCALE = (QK_NOPE + QK_ROPE) ** -0.5
LOG2E = 1.4426950408889634
ATT_SCALE_LOG2 = ATT_SCALE * LOG2E

ADAM_LR = 0.001
ADAM_B1 = 0.9
ADAM_B2 = 0.999
ADAM_EPS = 1e-08
ADAM_WD = 0.01
ADAM_STEP = 10

TM = 512
TQ = 256
ATT_T = 512
SSD_ROWS = 256
TK_DW = 2048
VMEM_LIMIT = 56 * 1024 * 1024

_NT = (((1,), (1,)), ((), ()))
_TN = (((0,), (0,)), ((), ()))


def _params(**kw):
    return pltpu.CompilerParams(vmem_limit_bytes=VMEM_LIMIT, **kw)


def _dot(a, b, precision=None):
    return jnp.dot(a, b, preferred_element_type=F32, precision=precision)


def _dot_nt(a, b, precision=None):
    return lax.dot_general(a, b, _NT, preferred_element_type=F32, precision=precision)


def _dot_tn(a, b, precision=None):
    return lax.dot_general(a, b, _TN, preferred_element_type=F32, precision=precision)


def _split3(x):
    hi = x.astype(BF16)
    r = x - hi.astype(F32)
    mid = r.astype(BF16)
    return hi, mid, (r - mid.astype(F32)).astype(BF16)


def _dot01(x, m01, dot=_dot, left=False):
    parts = [dot(m01, p) if left else dot(p, m01) for p in _split3(x)]
    return parts[0] + parts[1] + parts[2]


def _full(shape):
    n = len(shape)
    return pl.BlockSpec(shape, lambda *_: (0,) * n)


def _resident(shape):
    n = len(shape)
    return pl.BlockSpec(shape, lambda *_: (0,) * n, pipeline_mode=pl.Buffered(1))


def _rows(tm, width):
    return pl.BlockSpec((tm, width), lambda i: (i, 0))


def _rms_fwd(x, w):
    r = lax.rsqrt(jnp.mean(x * x, axis=-1, keepdims=True) + EPS)
    return (x * r) * w


def _rms_bwd(x, w, dy):
    r = lax.rsqrt(jnp.mean(x * x, axis=-1, keepdims=True) + EPS)
    xh = x * r
    dxn = dy * w
    dx = r * (dxn - xh * jnp.mean(dxn * xh, axis=-1, keepdims=True))
    return dx, dy * xh


def _acc_rows(ref, val, first):
    s = jnp.sum(val, axis=0, keepdims=True)

    @pl.when(first)
    def _():
        ref[...] = s

    @pl.when(jnp.logical_not(first))
    def _():
        ref[...] += s


def _rope(t, cosf, sinf, sign):
    lane = lax.broadcasted_iota(jnp.int32, t.shape, 1)
    rot = jnp.where(lane < MISC_ROPE + QK_ROPE // 2, -pltpu.roll(t, HEAD_PAD - QK_ROPE // 2, 1), pltpu.roll(t, QK_ROPE // 2, 1))
    return t * cosf + sign * (rot * sinf)


def _rope_tables(pos, invf):
    s = pos.shape[0]

    def body(pos_ref, invf_ref, cos_ref, sin_ref):
        ang = pos_ref[...].astype(F32) * invf_ref[...]
        cos_ref[...] = jnp.cos(ang)
        sin_ref[...] = jnp.sin(ang)

    return pl.pallas_call(
        body, name="rope_tables", grid=(s // TM,),
        in_specs=[_rows(TM, 1), _full((1, HEAD_PAD))],
        out_specs=[_rows(TM, HEAD_PAD), _rows(TM, HEAD_PAD)],
        out_shape=[jax.ShapeDtypeStruct((s, HEAD_PAD), F32)] * 2,
    )(pos, invf)


def _inproj_fwd(h, nw, win):
    s = h.shape[0]

    def body(h_ref, nw_ref, w_ref, ub_ref, cq_ref, ckv_ref, misc_ref, z_ref, xbc_ref):
        ub = _rms_fwd(h_ref[...], nw_ref[...]).astype(BF16)
        ub_ref[...] = ub
        proj = _dot(ub, w_ref[...])
        cq_ref[...] = proj[:, 0:768]
        ckv_ref[...] = proj[:, 768:1024]
        misc_ref[...] = proj[:, 1024:1152]
        z_ref[...] = proj[:, 1152:1664]
        xbc_ref[...] = proj[:, 1664:2688]

    widths = (768, 256, 128, 512, 1024)
    return pl.pallas_call(
        body, name="inproj_fwd", grid=(s // TM,),
        in_specs=[_rows(TM, D_MODEL), _full((1, D_MODEL)), _resident((D_MODEL, IN_PAD))],
        out_specs=[_rows(TM, D_MODEL)] + [_rows(TM, w) for w in widths],
        out_shape=[jax.ShapeDtypeStruct((s, D_MODEL), BF16)] + [jax.ShapeDtypeStruct((s, w), F32) for w in widths],
        compiler_params=_params(),
    )(h, nw, win)


def _qkv_fwd(cq, ckv, misc, qnw, kvnw, wuq, wkv, cosf, sinf):
    s = cq.shape[0]

    def body(cq_ref, ckv_ref, misc_ref, qnw_ref, kvnw_ref, wuq_ref, wkv_ref, cos_ref, sin_ref,
             cqn_ref, ckvn_ref, q_ref, k_ref, v_ref):
        cosf, sinf = cos_ref[...], sin_ref[...]
        cqn = _rms_fwd(cq_ref[...], qnw_ref[...]).astype(BF16)
        cqn_ref[...] = cqn
        q = _dot(cqn, wuq_ref[...])
        ckvn = _rms_fwd(ckv_ref[...], kvnw_ref[...]).astype(BF16)
        ckvn_ref[...] = ckvn
        kv = _dot(ckvn, wkv_ref[...])
        m = misc_ref[...]
        lane = lax.broadcasted_iota(jnp.int32, m.shape, 1)
        in_rope = jnp.logical_and(lane >= MISC_ROPE, lane < MISC_ROPE + QK_ROPE)
        kr = jnp.where(in_rope, _rope(m, cosf, sinf, 1.0), 0.0)
        for hd in range(MLA_HEADS):
            cols = slice(hd * HEAD_PAD, (hd + 1) * HEAD_PAD)
            q_ref[:, cols] = _rope(q[:, cols], cosf, sinf, 1.0).astype(BF16)
            k_ref[:, cols] = (kv[:, cols] + kr).astype(BF16)
        v_ref[...] = kv[:, MLA_HEADS * HEAD_PAD:].astype(BF16)

    wide = MLA_HEADS * HEAD_PAD
    return pl.pallas_call(
        body, name="qkv_fwd", grid=(s // TM,),
        in_specs=[_rows(TM, Q_RANK), _rows(TM, KV_RANK), _rows(TM, HEAD_PAD), _full((1, Q_RANK)), _full((1, KV_RANK)),
                  _resident((Q_RANK, wide)), _resident((KV_RANK, 2 * wide)), _rows(TM, HEAD_PAD), _rows(TM, HEAD_PAD)],
        out_specs=[_rows(TM, Q_RANK), _rows(TM, KV_RANK), _rows(TM, wide), _rows(TM, wide), _rows(TM, wide)],
        out_shape=[jax.ShapeDtypeStruct((s, Q_RANK), BF16), jax.ShapeDtypeStruct((s, KV_RANK), BF16)]
        + [jax.ShapeDtypeStruct((s, wide), BF16)] * 3,
        compiler_params=_params(),
    )(cq, ckv, misc, qnw, kvnw, wuq, wkv, cosf, sinf)


def _chunk_mask(t, keys_on_rows=False):
    row = lax.broadcasted_iota(jnp.int32, (t, t), 0) // CHUNK
    col = lax.broadcasted_iota(jnp.int32, (t, t), 1) // CHUNK
    return (row <= col) if keys_on_rows else (col <= row)


def _attn_fwd(q, k, v, gather=()):
    s = q.shape[0]
    t = ATT_T
    nq = s // t
    pair = 2 * HEAD_PAD
    ng = len(gather)

    def body(q_ref, k_ref, v_ref, *rest):
        g_in, (o_ref, lse_ref), g_out = rest[:ng], rest[ng:ng + 2], rest[ng + 2:2 * ng + 2]
        m_s, l_s, acc_s = rest[2 * ng + 2:2 * ng + 5]
        qi = pl.program_id(1)
        _hosted_comm("gather", g_in, g_out, rest[2 * ng + 5:],
                     jnp.logical_and(pl.program_id(0) == 0, qi == 0),
                     jnp.logical_and(pl.program_id(0) == MLA_HEADS // 2 - 1, qi == nq - 1))
        m_s[...] = jnp.full(m_s.shape, -jnp.inf, F32)
        l_s[...] = jnp.zeros(l_s.shape, F32)
        acc_s[...] = jnp.zeros(acc_s.shape, F32)

        def step(kb, masked):
            r0 = pl.multiple_of(kb * t, t)
            for hh in range(2):
                cols = slice(hh * HEAD_PAD, (hh + 1) * HEAD_PAD)
                sc = _dot_nt(q_ref[:, cols], k_ref[pl.ds(r0, t), cols]) * ATT_SCALE_LOG2
                if masked:
                    sc = jnp.where(_chunk_mask(t), sc, -jnp.inf)
                m_old = m_s[hh]
                m_new = jnp.maximum(m_old, jnp.max(sc, axis=-1, keepdims=True))
                alpha = jnp.exp2(m_old - m_new)
                p = jnp.exp2(sc - jnp.tile(m_new, (1, t // HEAD_PAD)))
                l_s[hh] = alpha * l_s[hh] + jnp.sum(p, axis=-1, keepdims=True)
                acc_s[hh] = alpha * acc_s[hh] + _dot(p.astype(BF16), v_ref[pl.ds(r0, t), cols])
                m_s[hh] = m_new

        def loop(kb, c):
            step(kb, False)
            return c

        lax.fori_loop(0, qi, loop, 0)
        step(qi, True)
        for hh in range(2):
            cols = slice(hh * HEAD_PAD, (hh + 1) * HEAD_PAD)
            o_ref[:, cols] = (acc_s[hh] / l_s[hh]).astype(BF16)
            lse_ref[hh] = (m_s[hh] + jnp.log(l_s[hh]) * LOG2E).T[0:8, :]

    outs = pl.pallas_call(
        body, name="attn_fwd_gather" if ng else "attn_fwd", grid=(MLA_HEADS // 2, nq),
        in_specs=[pl.BlockSpec((t, pair), lambda h, i: (i, h)),
                  pl.BlockSpec((s, pair), lambda h, i: (0, h)),
                  pl.BlockSpec((s, pair), lambda h, i: (0, h))] + [_ANY] * ng,
        out_specs=[pl.BlockSpec((t, pair), lambda h, i: (i, h)),
                   pl.BlockSpec((2, 8, t), lambda h, i: (h, 0, i))] + [_ANY] * ng,
        out_shape=[jax.ShapeDtypeStruct((s, MLA_HEADS * HEAD_PAD), BF16), jax.ShapeDtypeStruct((MLA_HEADS, 8, s), F32)]
        + _comm_out_shapes("gather", gather),
        scratch_shapes=[pltpu.VMEM((2, t, HEAD_PAD), F32), pltpu.VMEM((2, t, HEAD_PAD), F32), pltpu.VMEM((2, t, HEAD_PAD), F32)]
        + (_comm_scratch(ng) if ng else []),
        compiler_params=_params(),
    )(q, k, v, *gather)
    return outs[0], outs[1], list(outs[2:])


def _ssd_consts():
    emisc = np.zeros((HEAD_PAD, SSD_INNER), np.float32)
    for hd in range(SSD_HEADS):
        emisc[MISC_DT + hd, hd * SSD_P:(hd + 1) * SSD_P] = 1.0
    idx = np.arange(CHUNK)
    tri = (idx[:, None] >= idx[None, :]).astype(np.float32)
    return tuple(jnp.asarray(m, BF16) for m in (emisc, emisc.T.copy(), tri, tri.T.copy()))


def _ssd_chunk_common(cc, misc, emisc, tri, trit, dtb, a_exp):
    xa = cc * jax.nn.sigmoid(cc)
    dtr = _dot01(misc, emisc) + dtb
    dt = jax.nn.softplus(dtr)
    a = dt * a_exp
    acs = _dot01(a, tri, left=True)
    acs_t = _dot01(a, trit, dot=_dot_tn)
    alast = acs[CHUNK - 1:CHUNK, :]
    return xa, dtr, dt, acs, acs_t, alast


def _decay(acs, acs_t, hd):
    row = lax.broadcasted_iota(jnp.int32, (CHUNK, CHUNK), 0)
    col = lax.broadcasted_iota(jnp.int32, (CHUNK, CHUNK), 1)
    diff = acs[:, hd * SSD_P:hd * SSD_P + 1] - acs_t[hd * SSD_P:hd * SSD_P + 1, :]
    return jnp.exp(jnp.where(row >= col, diff, -jnp.inf))


def _half_mask(hh):
    lane = lax.broadcasted_iota(jnp.int32, (CHUNK, 2 * SSD_P), 1)
    return (lane >= SSD_P) if hh else (lane < SSD_P)


def _gate_norm(y, zz, nw):
    yz = y * (zz * jax.nn.sigmoid(zz))
    outs, rs = [], []
    half = SSD_INNER // SSD_GROUPS
    for g in range(SSD_GROUPS):
        yg = yz[:, g * half:(g + 1) * half]
        r = lax.rsqrt(jnp.mean(yg * yg, axis=-1, keepdims=True) + EPS)
        outs.append(yg * r)
        rs.append(r)
    return yz, jnp.concatenate(outs, axis=1), rs


def _ssd_fwd(xraw, misc, z, cw, cb, dtb, a_exp, d_exp, nw, consts):
    s = xraw.shape[0]
    nb = s // SSD_ROWS
    ncb = SSD_ROWS // CHUNK
    emisc, _, tri, trit = consts

    def body(x_ref, misc_ref, z_ref, cw_ref, cb_ref, dtb_ref, a_ref, d_ref, nw_ref, emisc_ref, tri_ref, trit_ref,
             c_ref, prev_ref, ypre_ref, yssd_ref, tail_s, state_s):
        i = pl.program_id(0)

        @pl.when(i == 0)
        def _():
            tail_s[...] = jnp.zeros(tail_s.shape, F32)
            state_s[...] = jnp.zeros(state_s.shape, F32)

        x = x_ref[...]
        xext = jnp.concatenate([tail_s[...], x], axis=0)
        acc = x * cw_ref[CONV_W - 1:CONV_W, :] + cb_ref[...]
        for j in range(1, CONV_W):
            acc = acc + pltpu.roll(xext, j, 0)[8:, :] * cw_ref[CONV_W - 1 - j:CONV_W - j, :]
        tail_s[...] = x[SSD_ROWS - 8:, :]
        c_ref[...] = acc

        def chunk(ci, carry):
            r0 = pl.multiple_of(ci * CHUNK, CHUNK)
            xa, _, dt, acs, acs_t, alast = _ssd_chunk_common(
                c_ref[pl.ds(r0, CHUNK), :], misc_ref[pl.ds(r0, CHUNK), :], emisc_ref[...], tri_ref[...], trit_ref[...],
                dtb_ref[...], a_ref[...])
            xs = xa[:, :SSD_INNER]
            xdt = xs * dt
            prev = state_s[...]
            prev_ref[ci] = prev
            wgt = (xdt * jnp.exp(alast - acs)).astype(BF16)
            e = jnp.exp(acs)
            ys, new_states = [], []
            for g in range(SSD_GROUPS):
                bm = xa[:, SSD_INNER + g * SSD_N:SSD_INNER + (g + 1) * SSD_N].astype(BF16)
                cm = xa[:, SSD_INNER + SSD_GROUPS * SSD_N + g * SSD_N:SSD_INNER + SSD_GROUPS * SSD_N + (g + 1) * SSD_N].astype(BF16)
                cb_g = _dot_nt(cm, bm)
                gl = slice(g * 256, (g + 1) * 256)
                new_states.append(_dot_tn(bm, wgt[:, gl]))
                yoff = _dot(cm, prev[:, gl].astype(BF16)) * e[:, gl]
                for jj in range(2):
                    pair = 2 * g + jj
                    pl_ = slice(pair * 128, (pair + 1) * 128)
                    xp = xdt[:, pl_]
                    yp = yoff[:, jj * 128:(jj + 1) * 128]
                    for hh in range(2):
                        sc = (cb_g * _decay(acs, acs_t, 2 * pair + hh)).astype(BF16)
                        yp = yp + _dot(sc, jnp.where(_half_mask(hh), xp, 0.0).astype(BF16))
                    ys.append(yp)
            y = jnp.concatenate(ys, axis=1) + d_ref[...] * xs
            state_s[...] = prev * jnp.exp(alast) + jnp.concatenate(new_states, axis=1)
            ypre_ref[pl.ds(r0, CHUNK), :] = y
            _, yn, _ = _gate_norm(y, z_ref[pl.ds(r0, CHUNK), :], None)
            yssd_ref[pl.ds(r0, CHUNK), :] = (yn * nw_ref[...]).astype(BF16)
            return carry

        lax.fori_loop(0, ncb, chunk, 0, unroll=True)

    return pl.pallas_call(
        body, name="ssd_fwd", grid=(nb,),
        in_specs=[_rows(SSD_ROWS, CONV_DIM), _rows(SSD_ROWS, HEAD_PAD), _rows(SSD_ROWS, SSD_INNER),
                  _full((CONV_W, CONV_DIM)), _full((1, CONV_DIM)), _full((1, SSD_INNER)), _full((1, SSD_INNER)),
                  _full((1, SSD_INNER)), _full((1, SSD_INNER)), _full((HEAD_PAD, SSD_INNER)), _full((CHUNK, CHUNK)),
                  _full((CHUNK, CHUNK))],
        out_specs=[_rows(SSD_ROWS, CONV_DIM), pl.BlockSpec((ncb, SSD_N, SSD_INNER), lambda i: (i, 0, 0)),
                   _rows(SSD_ROWS, SSD_INNER), _rows(SSD_ROWS, SSD_INNER)],
        out_shape=[jax.ShapeDtypeStruct((s, CONV_DIM), F32), jax.ShapeDtypeStruct((s // CHUNK, SSD_N, SSD_INNER), F32),
                   jax.ShapeDtypeStruct((s, SSD_INNER), F32), jax.ShapeDtypeStruct((s, SSD_INNER), BF16)],
        scratch_shapes=[pltpu.VMEM((8, CONV_DIM), F32), pltpu.VMEM((SSD_N, SSD_INNER), F32)],
        compiler_params=_params(),
    )(xraw, misc, z, cw, cb, dtb, a_exp, d_exp, nw, emisc, tri, trit)


def _outproj_fwd(oe, yssd, wout, h, nw):
    s = h.shape[0]
    wide = MLA_HEADS * HEAD_PAD

    def body(oe_ref, y_ref, w_ref, h_ref, nw_ref, mixed_ref, h1_ref):
        mixed = _dot(oe_ref[...], w_ref[0:wide, :]) + _dot(y_ref[...], w_ref[wide:, :])
        mixed_ref[...] = mixed
        h1_ref[...] = h_ref[...] + _rms_fwd(mixed, nw_ref[...])

    return pl.pallas_call(
        body, name="outproj_fwd", grid=(s // TM,),
        in_specs=[_rows(TM, wide), _rows(TM, SSD_INNER), _resident((wide + SSD_INNER, D_MODEL)), _rows(TM, D_MODEL),
                  _full((1, D_MODEL))],
        out_specs=[_rows(TM, D_MODEL), _rows(TM, D_MODEL)],
        out_shape=[jax.ShapeDtypeStruct((s, D_MODEL), F32)] * 2,
        compiler_params=_params(),
    )(oe, yssd, wout, h, nw)


def _mlp_fwd(h1, prew, wup, wdown, postw):
    s = h1.shape[0]
    fb = D_FF // N_DEV

    def body(h_ref, prew_ref, up_ref, down_ref, postw_ref, mb_ref, d_ref, h2_ref):
        hh = h_ref[...]
        mb = _rms_fwd(hh, prew_ref[...]).astype(BF16)
        mb_ref[...] = mb
        d = jnp.zeros((TM, D_MODEL), F32)
        for j in range(N_DEV):
            a = _dot(mb, up_ref[j])
            r = jnp.square(jnp.maximum(a, 0.0)).astype(BF16)
            d = d + _dot(r, down_ref[j])
        d_ref[...] = d
        h2_ref[...] = hh + _rms_fwd(d, postw_ref[...])

    return pl.pallas_call(
        body, name="mlp_fwd", grid=(s // TM,),
        in_specs=[_rows(TM, D_MODEL), _full((1, D_MODEL)), _resident((N_DEV, D_MODEL, fb)), _resident((N_DEV, fb, D_MODEL)),
                  _full((1, D_MODEL))],
        out_specs=[_rows(TM, D_MODEL)] * 3,
        out_shape=[jax.ShapeDtypeStruct((s, D_MODEL), BF16), jax.ShapeDtypeStruct((s, D_MODEL), F32),
                   jax.ShapeDtypeStruct((s, D_MODEL), F32)],
        compiler_params=_params(),
    )(h1, prew, wup, wdown, postw)


def _loss_grad(h, target):
    s = h.shape[0]

    def body(h_ref, t_ref, dh_ref, loss_ref):
        diff = h_ref[...] - t_ref[...]
        dh_ref[...] = diff * (1.0 / D_MODEL)
        part = 0.5 * jnp.sum(jnp.mean(diff * diff, axis=-1, keepdims=True), axis=0, keepdims=True)
        _acc_rows(loss_ref, part, pl.program_id(0) == 0)

    return pl.pallas_call(
        body, name="loss_grad", grid=(s // TM,),
        in_specs=[_rows(TM, D_MODEL)] * 2,
        out_specs=[_rows(TM, D_MODEL), _full((1, 1))],
        out_shape=[jax.ShapeDtypeStruct((s, D_MODEL), F32), jax.ShapeDtypeStruct((1, 1), F32)],
    )(h, target)


def _mlp_bwd(dh2, d, h1, mb, prew, wup, wdown, postw):
    s = dh2.shape[0]
    fb = D_FF // N_DEV
    tm = TM // 2

    def body(dh2_ref, d_ref, h1_ref, mb_ref, prew_ref, up_ref, down_ref, postw_ref,
             dh1_ref, da_ref, r_ref, dd_ref, gpost_ref, gpre_ref):
        first = pl.program_id(0) == 0
        dh2 = dh2_ref[...]
        dd, gpost = _rms_bwd(d_ref[...], postw_ref[...], dh2)
        _acc_rows(gpost_ref, gpost, first)
        ddb = dd.astype(BF16)
        dd_ref[...] = ddb
        mb = mb_ref[...]
        dm = jnp.zeros((tm, D_MODEL), F32)
        for j in range(N_DEV):
            a = jnp.maximum(_dot(mb, up_ref[j]), 0.0)
            r_ref[j] = jnp.square(a).astype(BF16)
            da = (_dot_nt(ddb, down_ref[j]) * (2.0 * a)).astype(BF16)
            da_ref[j] = da
            dm = dm + _dot_nt(da, up_ref[j])
        dx, gpre = _rms_bwd(h1_ref[...], prew_ref[...], dm)
        _acc_rows(gpre_ref, gpre, first)
        dh1_ref[...] = dh2 + dx

    stacked = pl.BlockSpec((N_DEV, tm, fb), lambda i: (0, i, 0))
    return pl.pallas_call(
        body, name="mlp_bwd", grid=(s // tm,),
        in_specs=[_rows(tm, D_MODEL)] * 4 + [_full((1, D_MODEL)), _resident((N_DEV, D_MODEL, fb)), _resident((N_DEV, fb, D_MODEL)),
                                              _full((1, D_MODEL))],
        out_specs=[_rows(tm, D_MODEL), stacked, stacked, _rows(tm, D_MODEL), _full((1, D_MODEL)), _full((1, D_MODEL))],
        out_shape=[jax.ShapeDtypeStruct((s, D_MODEL), F32), jax.ShapeDtypeStruct((N_DEV, s, fb), BF16),
                   jax.ShapeDtypeStruct((N_DEV, s, fb), BF16), jax.ShapeDtypeStruct((s, D_MODEL), BF16),
                   jax.ShapeDtypeStruct((1, D_MODEL), F32), jax.ShapeDtypeStruct((1, D_MODEL), F32)],
        compiler_params=_params(),
    )(dh2, d, h1, mb, prew, wup, wdown, postw)


def _matmul_tn(a, b, name, tk=TK_DW):
    s, m = a.shape
    n = b.shape[1]
    tn = n if n <= 1024 else (n // 2 if (n // 2) % 128 == 0 else n // 3)
    tk = min(tk, s)
    assert n % tn == 0 and tn % 128 == 0 and s % tk == 0

    def body(a_ref, b_ref, o_ref):
        part = _dot_tn(a_ref[...], b_ref[...])

        @pl.when(pl.program_id(1) == 0)
        def _():
            o_ref[...] = part

        @pl.when(pl.program_id(1) != 0)
        def _():
            o_ref[...] += part

    return pl.pallas_call(
        body, name=name, grid=(n // tn, s // tk),
        in_specs=[pl.BlockSpec((tk, m), lambda j, k: (k, 0)), pl.BlockSpec((tk, tn), lambda j, k: (k, j))],
        out_specs=pl.BlockSpec((m, tn), lambda j, k: (0, j)),
        out_shape=jax.ShapeDtypeStruct((m, n), F32),
        compiler_params=_params(),
    )(a, b)


def _matmul_tn_stacked(a, b, name, a_stacked, tk=TK_DW):
    tk = min(tk, a.shape[-2])
    if a_stacked:
        _, s, m = a.shape
        n = b.shape[1]
        in_specs = [pl.BlockSpec((1, tk, m), lambda j, k: (j, k, 0)), pl.BlockSpec((tk, n), lambda j, k: (k, 0))]
    else:
        s, m = a.shape
        n = b.shape[2]
        in_specs = [pl.BlockSpec((tk, m), lambda j, k: (k, 0)), pl.BlockSpec((1, tk, n), lambda j, k: (j, k, 0))]

    nk = s // tk

    def body(a_ref, b_ref, o_ref, acc_s):
        av = a_ref[0] if a_stacked else a_ref[...]
        bv = b_ref[...] if a_stacked else b_ref[0]
        part = _dot_tn(av, bv)
        k = pl.program_id(1)

        @pl.when(k == 0)
        def _():
            acc_s[...] = part

        @pl.when(jnp.logical_and(k != 0, k != nk - 1))
        def _():
            acc_s[...] += part

        @pl.when(k == nk - 1)
        def _():
            o_ref[0] = (part if nk == 1 else acc_s[...] + part).astype(BF16)

    return pl.pallas_call(
        body, name=name, grid=(N_DEV, nk),
        in_specs=in_specs,
        out_specs=pl.BlockSpec((1, m, n), lambda j, k: (j, 0, 0)),
        out_shape=jax.ShapeDtypeStruct((N_DEV, m, n), BF16),
        scratch_shapes=[pltpu.VMEM((m, n), F32)],
        compiler_params=_params(),
    )(a, b)


def _outproj_bwd(dh1, mixed, nw, wout):
    s = dh1.shape[0]
    wide = MLA_HEADS * HEAD_PAD

    def body(dh1_ref, mixed_ref, nw_ref, w_ref, dmix_ref, doe_ref, dy_ref, gnw_ref):
        dmix, gnw = _rms_bwd(mixed_ref[...], nw_ref[...], dh1_ref[...])
        _acc_rows(gnw_ref, gnw, pl.program_id(0) == 0)
        dmb = dmix.astype(BF16)
        dmix_ref[...] = dmb
        doe_ref[...] = _dot_nt(dmb, w_ref[0:wide, :]).astype(BF16)
        dy_ref[...] = _dot_nt(dmb, w_ref[wide:, :])

    return pl.pallas_call(
        body, name="outproj_bwd", grid=(s // TM,),
        in_specs=[_rows(TM, D_MODEL), _rows(TM, D_MODEL), _full((1, D_MODEL)), _resident((wide + SSD_INNER, D_MODEL))],
        out_specs=[_rows(TM, D_MODEL), _rows(TM, wide), _rows(TM, SSD_INNER), _full((1, D_MODEL))],
        out_shape=[jax.ShapeDtypeStruct((s, D_MODEL), BF16), jax.ShapeDtypeStruct((s, wide), BF16),
                   jax.ShapeDtypeStruct((s, SSD_INNER), F32), jax.ShapeDtypeStruct((1, D_MODEL), F32)],
        compiler_params=_params(),
    )(dh1, mixed, nw, wout)


def _attn_delta(o, do):
    s = o.shape[0]
    wide = MLA_HEADS * HEAD_PAD

    def body(o_ref, do_ref, d_ref):
        ones = jnp.ones((8, HEAD_PAD), BF16)
        for hd in range(MLA_HEADS):
            cols = slice(hd * HEAD_PAD, (hd + 1) * HEAD_PAD)
            prod = o_ref[:, cols].astype(F32) * do_ref[:, cols].astype(F32)
            d_ref[hd] = _dot01(prod, ones, dot=_dot_nt, left=True)

    return pl.pallas_call(
        body, name="attn_delta", grid=(s // TM,),
        in_specs=[_rows(TM, wide), _rows(TM, wide)],
        out_specs=pl.BlockSpec((MLA_HEADS, 8, TM), lambda i: (0, 0, i)),
        out_shape=jax.ShapeDtypeStruct((MLA_HEADS, 8, s), F32),
    )(o, do)


def _attn_bwd(q, k, v, do, lse, delta, exchange=()):
    s = q.shape[0]
    t = ATT_T
    nq = s // t
    pair = 2 * HEAD_PAD
    ne = len(exchange)

    def body(q_ref, k_ref, v_ref, do_ref, lse_ref, delta_ref, *rest):
        e_in, (dq_ref, dk_ref, dv_ref), e_out = rest[:ne], rest[ne:ne + 3], rest[ne + 3:2 * ne + 3]
        kb = pl.program_id(1)
        _hosted_comm("exchange", e_in, e_out, rest[2 * ne + 3:],
                     jnp.logical_and(pl.program_id(0) == 0, kb == 0),
                     jnp.logical_and(pl.program_id(0) == MLA_HEADS // 2 - 1, kb == nq - 1))

        @pl.when(kb == 0)
        def _():
            dq_ref[...] = jnp.zeros(dq_ref.shape, F32)

        dk_ref[...] = jnp.zeros(dk_ref.shape, F32)
        dv_ref[...] = jnp.zeros(dv_ref.shape, F32)

        def step(qb, masked):
            r0 = pl.multiple_of(qb * t, t)
            for hh in range(2):
                cols = slice(hh * HEAD_PAD, (hh + 1) * HEAD_PAD)
                kk = k_ref[:, cols]
                qq = q_ref[pl.ds(r0, t), cols]
                dd = do_ref[pl.ds(r0, t), cols]
                sc = _dot_nt(kk, qq) * ATT_SCALE_LOG2
                if masked:
                    sc = jnp.where(_chunk_mask(t, keys_on_rows=True), sc, -jnp.inf)
                p = jnp.exp2(sc - lse_ref[hh, 0:1, pl.ds(r0, t)])
                dv_ref[:, cols] += _dot(p.astype(BF16), dd)
                dp = _dot_nt(v_ref[:, cols], dd)
                ds = (p * (dp - delta_ref[hh, 0:1, pl.ds(r0, t)]) * ATT_SCALE).astype(BF16)
                dk_ref[:, cols] += _dot(ds, qq)
                dq_ref[pl.ds(r0, t), cols] += _dot_tn(ds, kk)

        def loop(qb, c):
            step(qb, False)
            return c

        step(kb, True)
        lax.fori_loop(kb + 1, nq, loop, 0)

    whole = pl.BlockSpec((s, pair), lambda h, i: (0, h))
    tile = pl.BlockSpec((t, pair), lambda h, i: (i, h))
    rowvec = pl.BlockSpec((2, 8, s), lambda h, i: (h, 0, 0))
    wide = MLA_HEADS * HEAD_PAD
    outs = pl.pallas_call(
        body, name="attn_bwd_exchange" if ne else "attn_bwd", grid=(MLA_HEADS // 2, nq),
        in_specs=[whole, tile, tile, whole, rowvec, rowvec] + [_ANY] * ne,
        out_specs=[whole, tile, tile] + [_ANY] * ne,
        out_shape=[jax.ShapeDtypeStruct((s, wide), F32)] * 3 + _comm_out_shapes("exchange", exchange),
        scratch_shapes=_comm_scratch(ne) if ne else [],
        compiler_params=_params(),
    )(q, k, v, do, lse, delta, *exchange)
    return outs[0], outs[1], outs[2], list(outs[3:])


def _ssd_bwd(dy, ypre, z, c, xraw, misc, prev, cw, dtb, a_exp, d_exp, nw, consts):
    s = dy.shape[0]
    nb = s // SSD_ROWS
    ncb = SSD_ROWS // CHUNK
    emisc, emisc_t, tri, trit = consts

    def body(dy_ref, ypre_ref, z_ref, c_ref, x_ref, xprev_ref, misc_ref, prev_ref, cw_ref, dtb_ref, a_ref, d_ref, nw_ref,
             emisc_ref, emisct_ref, tri_ref, trit_ref,
             dz_ref, dx_ref, dmisc_ref, gnw_ref, gd_ref, galog_ref, gdtb_ref, gcw_ref, gcb_ref,
             dst_s, dc_s, head_s):
        i = pl.program_id(0)
        first = i == 0

        @pl.when(first)
        def _():
            dst_s[...] = jnp.zeros(dst_s.shape, F32)
            head_s[...] = jnp.zeros(head_s.shape, F32)
            gnw_ref[...] = jnp.zeros(gnw_ref.shape, F32)
            gd_ref[...] = jnp.zeros(gd_ref.shape, F32)
            galog_ref[...] = jnp.zeros(galog_ref.shape, F32)
            gdtb_ref[...] = jnp.zeros(gdtb_ref.shape, F32)

        a_exp_v = a_ref[...]
        a8 = _dot01(a_exp_v, emisct_ref[...]) * (1.0 / SSD_P)

        def chunk(cr, carry):
            ci = ncb - 1 - cr
            r0 = pl.multiple_of(ci * CHUNK, CHUNK)
            cc = c_ref[pl.ds(r0, CHUNK), :]
            mm = misc_ref[pl.ds(r0, CHUNK), :]
            xa, dtr, dt, acs, acs_t, alast = _ssd_chunk_common(cc, mm, emisc_ref[...], tri_ref[...], trit_ref[...],
                                                              dtb_ref[...], a_exp_v)
            xs = xa[:, :SSD_INNER]
            xdt = xs * dt
            y = ypre_ref[pl.ds(r0, CHUNK), :]
            zz = z_ref[pl.ds(r0, CHUNK), :]
            yz, yn, rs = _gate_norm(y, zz, None)
            dyo = dy_ref[pl.ds(r0, CHUNK), :]
            gnw_ref[...] += jnp.sum(dyo * yn, axis=0, keepdims=True)
            dyn = dyo * nw_ref[...]
            half = SSD_INNER // SSD_GROUPS
            dyz_parts = []
            for g in range(SSD_GROUPS):
                gl = slice(g * half, (g + 1) * half)
                dyz_parts.append(rs[g] * (dyn[:, gl] - yn[:, gl] * jnp.mean(dyn[:, gl] * yn[:, gl], axis=-1, keepdims=True)))
            dyz = jnp.concatenate(dyz_parts, axis=1)
            sg = jax.nn.sigmoid(zz)
            dz_ref[pl.ds(r0, CHUNK), :] = dyz * y * (sg * (1.0 + zz * (1.0 - sg)))
            dyp = dyz * (zz * sg)
            dypb = dyp.astype(BF16)
            gd_ref[...] += jnp.sum(dyp * xs, axis=0, keepdims=True)
            prev = prev_ref[ci]
            dst = dst_s[...]
            cd = jnp.exp(alast)
            e = jnp.exp(acs)
            dsx = jnp.exp(alast - acs)
            wgt = (xdt * dsx).astype(BF16)
            dze = (dyp * e).astype(BF16)
            glast = jnp.sum(dst * prev, axis=0, keepdims=True) * cd
            dprev_parts, dxdt_parts, dxdt_state_parts, dbm, dcm, yoff_parts = [], [], [], [], [], []
            lane8 = lax.broadcasted_iota(jnp.int32, (CHUNK, HEAD_PAD), 1)
            diag8 = jnp.zeros((CHUNK, HEAD_PAD), F32)
            for g in range(SSD_GROUPS):
                gl = slice(g * 256, (g + 1) * 256)
                bm = xa[:, SSD_INNER + g * SSD_N:SSD_INNER + (g + 1) * SSD_N].astype(BF16)
                cm = xa[:, SSD_INNER + SSD_GROUPS * SSD_N + g * SSD_N:SSD_INNER + SSD_GROUPS * SSD_N + (g + 1) * SSD_N].astype(BF16)
                prev_g = prev[:, gl].astype(BF16)
                dst_g = dst[:, gl].astype(BF16)
                dcm_g = _dot_nt(dze[:, gl], prev_g)
                dprev_parts.append(_dot_tn(cm, dze[:, gl]))
                dxs_state = _dot(bm, dst_g) * dsx[:, gl]
                dbm_g = _dot_nt(wgt[:, gl], dst_g)
                cb_g = _dot_nt(cm, bm)
                dcb = jnp.zeros((CHUNK, CHUNK), F32)
                diag_parts = []
                for jj in range(2):
                    pair = 2 * g + jj
                    pl_ = slice(pair * 128, (pair + 1) * 128)
                    xp = xdt[:, pl_]
                    dyp_p = dypb[:, pl_]
                    dxp = jnp.zeros((CHUNK, 128), F32)
                    for hh in range(2):
                        hd = 2 * pair + hh
                        dec = _decay(acs, acs_t, hd)
                        xm = jnp.where(_half_mask(hh), xp, 0.0).astype(BF16)
                        dsc = _dot_nt(dyp_p, xm) * dec
                        dcb = dcb + dsc
                        sc = (cb_g * dec).astype(BF16)
                        dxp = dxp + jnp.where(_half_mask(hh), _dot_tn(sc, dyp_p), 0.0)
                        dm = dsc * cb_g
                        diag8 = diag8 + jnp.where(lane8 == MISC_DT + hd, jnp.sum(dm - dm.T, axis=1, keepdims=True), 0.0)
                    diag_parts.append(dxp)
                dcbb = dcb.astype(BF16)
                dcm.append(dcm_g + _dot(dcbb, bm))
                dbm.append(dbm_g + _dot_tn(dcbb, cm))
                dxdt_state_parts.append(dxs_state)
                dxdt_parts.append(jnp.concatenate(diag_parts, axis=1) + dxs_state)
                yoff_parts.append(_dot(cm, prev_g) * e[:, gl])
            dxdt = jnp.concatenate(dxdt_parts, axis=1)
            dxdt_state = jnp.concatenate(dxdt_state_parts, axis=1)
            dst_s[...] = dst * cd + jnp.concatenate(dprev_parts, axis=1)
            dacs = dyp * jnp.concatenate(yoff_parts, axis=1) - xdt * dxdt_state
            last = jnp.sum(xdt * dxdt_state, axis=0, keepdims=True) + glast
            row = lax.broadcasted_iota(jnp.int32, (CHUNK, SSD_INNER), 0)
            dacs = dacs + jnp.where(row == CHUNK - 1, last, 0.0)
            dacs8 = _dot01(dacs, emisct_ref[...]) + diag8
            da8 = _dot01(dacs8, trit_ref[...], left=True)
            ddt8 = da8 * a8 + _dot01(dxdt * xs, emisct_ref[...])
            dtr8 = mm + _dot01(dtb_ref[...], emisct_ref[...]) * (1.0 / SSD_P)
            dt8 = jax.nn.softplus(dtr8)
            lane = lax.broadcasted_iota(jnp.int32, (CHUNK, HEAD_PAD), 1)
            on_dt = jnp.logical_and(lane >= MISC_DT, lane < MISC_DT + SSD_HEADS)
            ddtr8 = jnp.where(on_dt, ddt8 * jax.nn.sigmoid(dtr8), 0.0)
            dmisc_ref[pl.ds(r0, CHUNK), :] = ddtr8
            gdtb_ref[...] += jnp.sum(ddtr8, axis=0, keepdims=True)
            galog_ref[...] += jnp.sum(jnp.where(on_dt, da8 * dt8, 0.0), axis=0, keepdims=True) * a8
            dxs = d_ref[...] * dyp + dxdt * dt
            dxa = jnp.concatenate([dxs] + dbm + dcm, axis=1)
            sc_ = jax.nn.sigmoid(cc)
            dc_s[pl.ds(r0, CHUNK), :] = dxa * (sc_ * (1.0 + cc * (1.0 - sc_)))
            return carry

        lax.fori_loop(0, ncb, chunk, 0, unroll=True)

        dc = dc_s[...]
        dcext = jnp.concatenate([dc, head_s[...]], axis=0)
        dx = dc * cw_ref[CONV_W - 1:CONV_W, :]
        for j in range(1, CONV_W):
            dx = dx + pltpu.roll(dcext, SSD_ROWS + 8 - j, 0)[:SSD_ROWS, :] * cw_ref[CONV_W - 1 - j:CONV_W - j, :]
        dx_ref[...] = dx
        head_s[...] = dc[:8, :]
        xprev = jnp.where(i == nb - 1, 0.0, xprev_ref[...])
        xext = jnp.concatenate([xprev, x_ref[...]], axis=0)
        rows = [jnp.sum(dc * pltpu.roll(xext, CONV_W - 1 - kk, 0)[8:, :], axis=0, keepdims=True) for kk in range(CONV_W)]
        gcw = jnp.concatenate(rows, axis=0)

        @pl.when(first)
        def _():
            gcw_ref[...] = gcw
            gcb_ref[...] = jnp.sum(dc, axis=0, keepdims=True)

        @pl.when(jnp.logical_not(first))
        def _():
            gcw_ref[...] += gcw
            gcb_ref[...] += jnp.sum(dc, axis=0, keepdims=True)

    def rev(width):
        return pl.BlockSpec((SSD_ROWS, width), lambda i: (nb - 1 - i, 0))

    per8 = SSD_ROWS // 8
    return pl.pallas_call(
        body, name="ssd_bwd", grid=(nb,),
        in_specs=[rev(SSD_INNER), rev(SSD_INNER), rev(SSD_INNER), rev(CONV_DIM), rev(CONV_DIM),
                  pl.BlockSpec((8, CONV_DIM), lambda i: (jnp.maximum((nb - 1 - i) * per8 - 1, 0), 0)),
                  rev(HEAD_PAD), pl.BlockSpec((ncb, SSD_N, SSD_INNER), lambda i: (nb - 1 - i, 0, 0)),
                  _full((CONV_W, CONV_DIM)), _full((1, SSD_INNER)), _full((1, SSD_INNER)), _full((1, SSD_INNER)),
                  _full((1, SSD_INNER)), _full((HEAD_PAD, SSD_INNER)), _full((SSD_INNER, HEAD_PAD)), _full((CHUNK, CHUNK)),
                  _full((CHUNK, CHUNK))],
        out_specs=[rev(SSD_INNER), rev(CONV_DIM), rev(HEAD_PAD), _full((1, SSD_INNER)), _full((1, SSD_INNER)),
                   _full((1, HEAD_PAD)), _full((1, HEAD_PAD)), _full((CONV_W, CONV_DIM)), _full((1, CONV_DIM))],
        out_shape=[jax.ShapeDtypeStruct((s, SSD_INNER), F32), jax.ShapeDtypeStruct((s, CONV_DIM), F32),
                   jax.ShapeDtypeStruct((s, HEAD_PAD), F32), jax.ShapeDtypeStruct((1, SSD_INNER), F32),
                   jax.ShapeDtypeStruct((1, SSD_INNER), F32), jax.ShapeDtypeStruct((1, HEAD_PAD), F32),
                   jax.ShapeDtypeStruct((1, HEAD_PAD), F32), jax.ShapeDtypeStruct((CONV_W, CONV_DIM), F32),
                   jax.ShapeDtypeStruct((1, CONV_DIM), F32)],
        scratch_shapes=[pltpu.VMEM((SSD_N, SSD_INNER), F32), pltpu.VMEM((SSD_ROWS, CONV_DIM), F32), pltpu.VMEM((8, CONV_DIM), F32)],
        compiler_params=_params(),
    )(dy, ypre, z, c, xraw, xraw, misc, prev, cw, dtb, a_exp, d_exp, nw, emisc, emisc_t, tri, trit)


def _qkv_bwd(dq, dk, dv, cq, ckv, qnw, kvnw, wuq, wkv, cosf, sinf):
    s = dq.shape[0]
    wide = MLA_HEADS * HEAD_PAD

    def body(dq_ref, dk_ref, dv_ref, cq_ref, ckv_ref, qnw_ref, kvnw_ref, wuq_ref, wkv_ref, cos_ref, sin_ref,
             dqb_ref, dkvb_ref, dcq_ref, dckv_ref, dmisc_ref, gq_ref, gkv_ref):
        first = pl.program_id(0) == 0
        cosf, sinf = cos_ref[...], sin_ref[...]
        dkr = jnp.zeros((TM, HEAD_PAD), F32)
        for hd in range(MLA_HEADS):
            cols = slice(hd * HEAD_PAD, (hd + 1) * HEAD_PAD)
            dqb_ref[:, cols] = _rope(dq_ref[:, cols], cosf, sinf, -1.0).astype(BF16)
            dkh = dk_ref[:, cols]
            dkvb_ref[:, cols] = dkh.astype(BF16)
            dkr = dkr + dkh
        dkvb_ref[:, wide:] = dv_ref[...].astype(BF16)
        lane = lax.broadcasted_iota(jnp.int32, dkr.shape, 1)
        in_rope = jnp.logical_and(lane >= MISC_ROPE, lane < MISC_ROPE + QK_ROPE)
        dmisc_ref[...] = jnp.where(in_rope, _rope(jnp.where(in_rope, dkr, 0.0), cosf, sinf, -1.0), 0.0)
        dcq, gq = _rms_bwd(cq_ref[...], qnw_ref[...], _dot_nt(dqb_ref[...], wuq_ref[...]))
        dcq_ref[...] = dcq
        _acc_rows(gq_ref, gq, first)
        dckv, gkv = _rms_bwd(ckv_ref[...], kvnw_ref[...], _dot_nt(dkvb_ref[...], wkv_ref[...]))
        dckv_ref[...] = dckv
        _acc_rows(gkv_ref, gkv, first)

    return pl.pallas_call(
        body, name="qkv_bwd", grid=(s // TM,),
        in_specs=[_rows(TM, wide)] * 3 + [_rows(TM, Q_RANK), _rows(TM, KV_RANK), _full((1, Q_RANK)), _full((1, KV_RANK)),
                                          _resident((Q_RANK, wide)), _resident((KV_RANK, 2 * wide)), _rows(TM, HEAD_PAD), _rows(TM, HEAD_PAD)],
        out_specs=[_rows(TM, wide), _rows(TM, 2 * wide), _rows(TM, Q_RANK), _rows(TM, KV_RANK), _rows(TM, HEAD_PAD),
                   _full((1, Q_RANK)), _full((1, KV_RANK))],
        out_shape=[jax.ShapeDtypeStruct((s, wide), BF16), jax.ShapeDtypeStruct((s, 2 * wide), BF16),
                   jax.ShapeDtypeStruct((s, Q_RANK), F32), jax.ShapeDtypeStruct((s, KV_RANK), F32),
                   jax.ShapeDtypeStruct((s, HEAD_PAD), F32), jax.ShapeDtypeStruct((1, Q_RANK), F32),
                   jax.ShapeDtypeStruct((1, KV_RANK), F32)],
        compiler_params=_params(),
    )(dq, dk, dv, cq, ckv, qnw, kvnw, wuq, wkv, cosf, sinf)


def _inproj_bwd(dcq, dckv, dmisc_rope, dmisc_dt, dz, dxbc, h, dh1, nw, win):
    s = h.shape[0]

    def body(dcq_ref, dckv_ref, dma_ref, dmb_ref, dz_ref, dxbc_ref, h_ref, dh1_ref, nw_ref, w_ref, dproj_ref, dh0_ref, gnw_ref):
        dproj_ref[:, 0:768] = dcq_ref[...].astype(BF16)
        dproj_ref[:, 768:1024] = dckv_ref[...].astype(BF16)
        dproj_ref[:, 1024:1152] = (dma_ref[...] + dmb_ref[...]).astype(BF16)
        dproj_ref[:, 1152:1664] = dz_ref[...].astype(BF16)
        dproj_ref[:, 1664:2688] = dxbc_ref[...].astype(BF16)
        du = _dot_nt(dproj_ref[...], w_ref[...])
        dx, gnw = _rms_bwd(h_ref[...], nw_ref[...], du)
        _acc_rows(gnw_ref, gnw, pl.program_id(0) == 0)
        dh0_ref[...] = dh1_ref[...] + dx

    return pl.pallas_call(
        body, name="inproj_bwd", grid=(s // TM,),
        in_specs=[_rows(TM, Q_RANK), _rows(TM, KV_RANK), _rows(TM, HEAD_PAD), _rows(TM, HEAD_PAD), _rows(TM, SSD_INNER),
                  _rows(TM, CONV_DIM), _rows(TM, D_MODEL), _rows(TM, D_MODEL), _full((1, D_MODEL)), _resident((D_MODEL, IN_PAD))],
        out_specs=[_rows(TM, IN_PAD), _rows(TM, D_MODEL), _full((1, D_MODEL))],
        out_shape=[jax.ShapeDtypeStruct((s, IN_PAD), BF16), jax.ShapeDtypeStruct((s, D_MODEL), F32),
                   jax.ShapeDtypeStruct((1, D_MODEL), F32)],
        compiler_params=_params(),
    )(dcq, dckv, dmisc_rope, dmisc_dt, dz, dxbc, h, dh1, nw, win)


def _row_tile(rows, cols):
    cap = max(8, (1 << 18) // max(cols, 128))
    best = None
    for t in range(8, rows + 1, 8):
        if rows % t == 0 and t <= cap:
            best = t
    return best if best is not None else rows


def _adamw(w, g, m, v, name):
    rows, cols = w.shape
    tr = _row_tile(rows, cols)

    def body(w_ref, g_ref, m_ref, v_ref, d_ref, m2_ref, v2_ref):
        gg = g_ref[...]
        m2 = ADAM_B1 * m_ref[...] + (1.0 - ADAM_B1) * gg
        v2 = ADAM_B2 * v_ref[...] + (1.0 - ADAM_B2) * jnp.square(gg)
        m_hat = m2 / (1.0 - ADAM_B1 ** ADAM_STEP)
        v_hat = v2 / (1.0 - ADAM_B2 ** ADAM_STEP)
        d_ref[...] = -ADAM_LR * (m_hat / (jnp.sqrt(v_hat) + ADAM_EPS) + ADAM_WD * w_ref[...])
        m2_ref[...] = m2
        v2_ref[...] = v2

    spec = pl.BlockSpec((tr, cols), lambda i: (i, 0))
    return pl.pallas_call(
        body, name=name, grid=(rows // tr,),
        in_specs=[spec] * 4, out_specs=[spec] * 3,
        out_shape=[jax.ShapeDtypeStruct((rows, cols), F32)] * 3,
    )(w, g, m, v)


def _sum_adamw(slots, w, m, v, name):
    _, rows, cols = w.shape
    tr = _row_tile(rows, cols)
    nb = rows // tr

    def body(s0_ref, s1_ref, w_ref, m_ref, v_ref, g_ref, d_ref, m2_ref, v2_ref):
        for l, ref in enumerate((s0_ref, s1_ref)):
            @pl.when(pl.program_id(0) == l)
            def _(ref=ref):
                acc = ref[0].astype(F32)
                for i in range(1, N_DEV):
                    acc = acc + ref[i].astype(F32)
                g_ref[...] = acc

        gg = g_ref[...]
        m2 = ADAM_B1 * m_ref[...] + (1.0 - ADAM_B1) * gg
        v2 = ADAM_B2 * v_ref[...] + (1.0 - ADAM_B2) * jnp.square(gg)
        m_hat = m2 / (1.0 - ADAM_B1 ** ADAM_STEP)
        v_hat = v2 / (1.0 - ADAM_B2 ** ADAM_STEP)
        d_ref[...] = -ADAM_LR * (m_hat / (jnp.sqrt(v_hat) + ADAM_EPS) + ADAM_WD * w_ref[...])
        m2_ref[...] = m2
        v2_ref[...] = v2

    slot_spec = lambda layer: pl.BlockSpec((N_DEV, tr, cols), lambda l, i: (0, jnp.where(l == layer, i, (nb - 1) * (1 - layer)), 0))
    spec = pl.BlockSpec((None, tr, cols), lambda l, i: (l, i, 0))
    return pl.pallas_call(
        body, name=name, grid=(DEPTH, nb),
        in_specs=[slot_spec(0), slot_spec(1), spec, spec, spec], out_specs=[spec] * 4,
        out_shape=[jax.ShapeDtypeStruct(w.shape, F32)] * 4,
        compiler_params=_params(),
    )(slots[0], slots[1], w, m, v)


_MESH = pl.DeviceIdType.MESH
_ANY = pl.BlockSpec(memory_space=pl.ANY)


def _my_place():
    return lax.axis_index("x"), lax.axis_index("y"), lax.axis_index("c")


def _flip(place, k):
    x, y, c = place
    return (1 - x if k & 4 else x, 1 - y if k & 2 else y, 1 - c if k & 1 else c)


def _block_id(place):
    return 4 * place[0] + 2 * place[1] + place[2]


def _peer_copies(kind, in_refs, out_refs, send_sems, recv_sems, local_sems):
    me = _my_place()
    my = _block_id(me)
    remote, local = [], []
    for a, (x_ref, out_ref) in enumerate(zip(in_refs, out_refs)):
        src_of = (lambda place, r=x_ref: r) if kind == "gather" else (lambda place, r=x_ref: r.at[_block_id(place)])
        local.append(pltpu.make_async_copy(src_of(me), out_ref.at[my], local_sems.at[a]))
        for k in range(1, N_DEV):
            peer = _flip(me, k)
            remote.append(pltpu.make_async_remote_copy(
                src_ref=src_of(peer), dst_ref=out_ref.at[my], send_sem=send_sems.at[a * 7 + k - 1],
                recv_sem=recv_sems.at[a * 7 + k - 1], device_id=peer, device_id_type=_MESH))
    return remote, local


def _comm_out_shapes(kind, arrays):
    return [jax.ShapeDtypeStruct((N_DEV, *a.shape) if kind == "gather" else a.shape, a.dtype) for a in arrays]


def _comm_scratch(n):
    return [pltpu.SemaphoreType.DMA((7 * n,)), pltpu.SemaphoreType.DMA((7 * n,)), pltpu.SemaphoreType.DMA((n,))]


def _hosted_comm(kind, in_refs, out_refs, sems, first, last):
    if not in_refs:
        return

    @pl.when(first)
    def _():
        remote, local = _peer_copies(kind, in_refs, out_refs, *sems)
        for cp in local + remote:
            cp.start()

    @pl.when(last)
    def _():
        remote, local = _peer_copies(kind, in_refs, out_refs, *sems)
        for cp in remote:
            cp.wait()
        for cp in local:
            cp.wait()


def _comm(kind, arrays, name):
    n = len(arrays)

    def body(*refs):
        remote, local = _peer_copies(kind, refs[:n], refs[n:2 * n], *refs[2 * n:])
        for cp in local + remote:
            cp.start()
        for cp in remote:
            cp.wait()
        for cp in local:
            cp.wait()

    return pl.pallas_call(
        body, name=name, out_shape=_comm_out_shapes(kind, arrays),
        in_specs=[_ANY] * n, out_specs=[_ANY] * n, scratch_shapes=_comm_scratch(n),
    )(*arrays)


def _all_reduce_small(part):
    rows, lanes = part.shape
    vmem = pl.BlockSpec(memory_space=pltpu.VMEM)

    def body(x_ref, gath_ref, sum_ref, send_sems, recv_sems):
        me = _my_place()
        my = _block_id(me)
        gath_ref[my] = x_ref[...]
        copies = []
        for k in range(1, N_DEV):
            cp = pltpu.make_async_remote_copy(
                src_ref=x_ref, dst_ref=gath_ref.at[my], send_sem=send_sems.at[k - 1], recv_sem=recv_sems.at[k - 1],
                device_id=_flip(me, k), device_id_type=_MESH)
            cp.start()
            copies.append(cp)
        for cp in copies:
            cp.wait()
        acc = gath_ref[0]
        for i in range(1, N_DEV):
            acc = acc + gath_ref[i]
        sum_ref[...] = acc

    return pl.pallas_call(
        body, name="small_grad_all_reduce",
        out_shape=[jax.ShapeDtypeStruct((N_DEV, rows, lanes), F32), jax.ShapeDtypeStruct((rows, lanes), F32)],
        in_specs=[vmem], out_specs=[vmem, vmem],
        scratch_shapes=[pltpu.SemaphoreType.DMA((7,)), pltpu.SemaphoreType.DMA((7,))],
    )(part)[1]


_SHARDED = (("w_in", (D_MODEL, IN_PROJ // N_DEV)), ("w_uq", (Q_RANK // N_DEV, Q_RANK)), ("w_ukv", (KV_RANK, HEAD_PAD)),
            ("conv_w", (CONV_W, CONV_DIM // N_DEV)), ("w_out", (D_MODEL // N_DEV, D_MODEL)),
            ("w_up", (D_MODEL, D_FF // N_DEV)), ("w_down", (D_FF // N_DEV, D_MODEL)))
_SMALL = (("pre_mix_norm", D_MODEL), ("q_norm", Q_RANK), ("kv_norm", KV_RANK), ("conv_b", CONV_DIM), ("dt_bias", SSD_HEADS),
          ("a_log", SSD_HEADS), ("d_skip", SSD_HEADS), ("ssd_norm", SSD_INNER), ("post_mix_norm", D_MODEL),
          ("pre_mlp_norm", D_MODEL), ("post_mlp_norm", D_MODEL))
_WEIGHT_ORDER = ("pre_mix_norm", "w_in", "q_norm", "w_uq", "kv_norm", "w_ukv", "conv_w", "conv_b", "dt_bias", "a_log", "d_skip",
                 "ssd_norm", "w_out", "post_mix_norm", "pre_mlp_norm", "w_up", "w_down", "post_mlp_norm")
_EARLY = ("w_in", "w_uq", "w_ukv", "conv_w")
_LATE = ("w_out", "w_up", "w_down")


def _wire_shard(name, a):
    return lax.bitcast_convert_type(a, BF16).reshape(CONV_W, -1) if name == "conv_w" else a.astype(BF16)


def _from_wire(name, g):
    return lax.bitcast_convert_type(g.reshape(N_DEV, CONV_W, -1, 2), F32) if name == "conv_w" else g


def _cols(stacked):
    return jnp.transpose(stacked, (1, 0, 2)).reshape(stacked.shape[1], -1)


def _early_weights(sh):
    w_in = _cols(sh["w_in"])
    zeros = lambda n: jnp.zeros((D_MODEL, n), BF16)
    s1, s2, s3, s4, s5 = 768, 1024, 1056, 1568, 2592
    win = jnp.concatenate([w_in[:, :s2], zeros(MISC_ROPE), w_in[:, s2:s3], w_in[:, s5:], zeros(HEAD_PAD - MISC_DT - SSD_HEADS),
                           w_in[:, s3:s5]], axis=1)
    w_uq = sh["w_uq"].reshape(Q_RANK, MLA_HEADS, QK_NOPE + QK_ROPE)
    wuq = jnp.pad(w_uq, ((0, 0), (0, 0), (0, HEAD_PAD - QK_NOPE - QK_ROPE))).reshape(Q_RANK, -1)
    w_ukv = _cols(sh["w_ukv"]).reshape(KV_RANK, MLA_HEADS, QK_NOPE + V_DIM)
    wkn = jnp.pad(w_ukv[..., :QK_NOPE], ((0, 0), (0, 0), (0, HEAD_PAD - QK_NOPE))).reshape(KV_RANK, -1)
    wv = w_ukv[..., QK_NOPE:].reshape(KV_RANK, 4, 2, 1, V_DIM) * jnp.eye(2, dtype=BF16).reshape(1, 1, 2, 2, 1)
    wkv = jnp.concatenate([wkn, wv.reshape(KV_RANK, -1)], axis=1)
    return dict(win=win, wuq=wuq, wkv=wkv, conv_w=_cols(sh["conv_w"]))


def _late_weights(sh):
    w_out = sh["w_out"].reshape(D_MODEL, D_MODEL)
    watt = w_out[:SSD_INNER].reshape(4, 2, 1, V_DIM, D_MODEL) * jnp.eye(2, dtype=BF16).reshape(1, 2, 2, 1, 1)
    wout = jnp.concatenate([watt.reshape(MLA_HEADS * HEAD_PAD, D_MODEL), w_out[SSD_INNER:]], axis=0)
    return dict(wout=wout, wup=sh["w_up"], wdown=sh["w_down"])


def _shard_grads(g):
    out = {}
    if "wup" in g:
        out["w_up"], out["w_down"] = g["wup"], g["wdown"]
        ae = g["wout_att"].reshape(4, 2, 2, V_DIM, D_MODEL)
        att = jnp.stack([ae[:, 0, 0], ae[:, 1, 1]], axis=1).reshape(SSD_INNER, D_MODEL)
        out["w_out"] = jnp.concatenate([att, g["wout_ssd"]], axis=0).astype(BF16).reshape(N_DEV, D_MODEL // N_DEV, D_MODEL)
    if "win" not in g:
        return out
    dwin = g["win"]
    s1, s2 = 768, 1024
    m0 = s2
    w_in = jnp.concatenate([dwin[:, :s2], dwin[:, m0 + MISC_ROPE:m0 + MISC_ROPE + QK_ROPE], dwin[:, 1152:2688],
                            dwin[:, m0 + MISC_DT:m0 + MISC_DT + SSD_HEADS]], axis=1)
    out["w_in"] = jnp.transpose(w_in.astype(BF16).reshape(D_MODEL, N_DEV, -1), (1, 0, 2))
    w_uq = g["wuq"].astype(BF16).reshape(Q_RANK, MLA_HEADS, HEAD_PAD)[..., :QK_NOPE + QK_ROPE].reshape(Q_RANK, Q_RANK)
    out["w_uq"] = w_uq.reshape(N_DEV, Q_RANK // N_DEV, Q_RANK)
    wide = MLA_HEADS * HEAD_PAD
    wkv = g["wkv"].astype(BF16)
    kn = wkv[:, :wide].reshape(KV_RANK, MLA_HEADS, HEAD_PAD)[..., :QK_NOPE]
    ve = wkv[:, wide:].reshape(KV_RANK, 4, 2, 2, V_DIM)
    vv = jnp.stack([ve[:, :, 0, 0], ve[:, :, 1, 1]], axis=2).reshape(KV_RANK, MLA_HEADS, V_DIM)
    out["w_ukv"] = jnp.transpose(jnp.concatenate([kn, vv], axis=-1), (1, 0, 2))
    out["conv_w"] = jnp.transpose(g["conv_w"].astype(BF16).reshape(CONV_W, N_DEV, -1), (1, 0, 2))
    return out


def _small_rows(n):
    return -(-n // 1024) * 8


def _pack_small(vals):
    rows = []
    for l in range(DEPTH):
        for name, n in _SMALL:
            r = _small_rows(n)
            rows.append(jnp.pad(vals[name][l].reshape(-1), (0, r * 128 - n)).reshape(r, 128))
    return jnp.concatenate(rows, axis=0)


def _unpack_small(packed):
    out, off = {name: [] for name, _ in _SMALL}, 0
    for l in range(DEPTH):
        for name, n in _SMALL:
            r = _small_rows(n)
            out[name].append(packed[off:off + r].reshape(-1)[:n])
            off += r
    return {name: jnp.stack(v) for name, v in out.items()}


def _lane_rows(vec8):
    return jnp.repeat(vec8, SSD_P).reshape(1, SSD_INNER)


def _layer_fwd(h, kw, sm, l, cosf, sinf, consts, gather=(), after_gather=None):
    row = lambda name: sm[name][l].reshape(1, -1)
    t = {}
    t["h0"] = h
    t["ub"], t["cq"], t["ckv"], t["misc"], t["z"], t["xraw"] = _inproj_fwd(h, row("pre_mix_norm"), kw["win"])
    t["cqn"], t["ckvn"], t["q"], t["k"], t["v"] = _qkv_fwd(t["cq"], t["ckv"], t["misc"], row("q_norm"), row("kv_norm"),
                                                         kw["wuq"], kw["wkv"], cosf, sinf)
    t["oe"], t["lse"], gathered = _attn_fwd(t["q"], t["k"], t["v"], gather)
    if after_gather is not None:
        after_gather(gathered)
    t["dtb"] = _lane_rows(sm["dt_bias"][l])
    t["a_exp"] = _lane_rows(-jnp.exp(sm["a_log"][l]))
    t["d_exp"] = _lane_rows(sm["d_skip"][l])
    t["c"], t["prev"], t["ypre"], t["yssd"] = _ssd_fwd(t["xraw"], t["misc"], t["z"], kw["conv_w"], row("conv_b"), t["dtb"],
                                                     t["a_exp"], t["d_exp"], row("ssd_norm"), consts)
    t["mixed"], t["h1"] = _outproj_fwd(t["oe"], t["yssd"], kw["wout"], h, row("post_mix_norm"))
    t["mb"], t["d"], h2 = _mlp_fwd(t["h1"], row("pre_mlp_norm"), kw["wup"], kw["wdown"], row("post_mlp_norm"))
    return h2, t


def _layer_bwd(dh2, t, kw, sm, l, cosf, sinf, consts, exchange_of=None):
    row = lambda name: sm[name][l].reshape(1, -1)
    g, gs = {}, {}
    dh1, dab, rb, ddb, gs["post_mlp_norm"], gs["pre_mlp_norm"] = _mlp_bwd(
        dh2, t["d"], t["h1"], t["mb"], row("pre_mlp_norm"), kw["wup"], kw["wdown"], row("post_mlp_norm"))
    g["wup"] = _matmul_tn_stacked(t["mb"], dab, f"dw_up_{l}", a_stacked=False)
    g["wdown"] = _matmul_tn_stacked(rb, ddb, f"dw_down_{l}", a_stacked=True)
    dmixb, doe, dyssd, gs["post_mix_norm"] = _outproj_bwd(dh1, t["mixed"], row("post_mix_norm"), kw["wout"])
    g["wout_att"] = _matmul_tn(t["oe"], dmixb, f"dw_out_att_{l}")
    g["wout_ssd"] = _matmul_tn(t["yssd"], dmixb, f"dw_out_ssd_{l}")
    dz, dxraw, dmisc_dt, gs["ssd_norm"], gd, galog, gdtb, g["conv_w"], gs["conv_b"] = _ssd_bwd(
        dyssd, t["ypre"], t["z"], t["c"], t["xraw"], t["misc"], t["prev"], kw["conv_w"], t["dtb"], t["a_exp"], t["d_exp"],
        row("ssd_norm"), consts)
    gs["d_skip"] = jnp.sum(gd.reshape(SSD_HEADS, SSD_P), axis=1)
    gs["a_log"] = galog[0, MISC_DT:MISC_DT + SSD_HEADS]
    gs["dt_bias"] = gdtb[0, MISC_DT:MISC_DT + SSD_HEADS]
    dq, dk, dv, exchanged = _attn_bwd(t["q"], t["k"], t["v"], doe, t["lse"], _attn_delta(t["oe"], doe),
                                      exchange_of(g) if exchange_of is not None else ())
    dqb, dkvb, dcq, dckv, dmisc_rope, gs["q_norm"], gs["kv_norm"] = _qkv_bwd(
        dq, dk, dv, t["cq"], t["ckv"], row("q_norm"), row("kv_norm"), kw["wuq"], kw["wkv"], cosf, sinf)
    g["wuq"] = _matmul_tn(t["cqn"], dqb, f"dw_uq_{l}")
    g["wkv"] = _matmul_tn(t["ckvn"], dkvb, f"dw_kv_{l}")
    dprojb, dh0, gs["pre_mix_norm"] = _inproj_bwd(dcq, dckv, dmisc_rope, dmisc_dt, dz, dxraw, t["h0"], dh1,
                                                  row("pre_mix_norm"), kw["win"])
    g["win"] = _matmul_tn(t["ub"], dprojb, f"dw_in_{l}")
    return dh0, g, {k: v.reshape(-1) for k, v in gs.items()}, exchanged


def _local_step(x, positions, kws, sm, target, gather=(), after_gather=None, exchange_of=None):
    inv_freq = ROPE_THETA ** (-jnp.arange(0, QK_ROPE, 2, dtype=F32) / QK_ROPE)
    invf = jnp.zeros((HEAD_PAD,), F32).at[MISC_ROPE:MISC_ROPE + QK_ROPE].set(jnp.concatenate([inv_freq, inv_freq]))
    cosf, sinf = _rope_tables(positions.reshape(-1, 1), invf.reshape(1, HEAD_PAD))
    consts = _ssd_consts()
    h, saved = x, []
    for l in range(DEPTH):
        h, t = _layer_fwd(h, kws[l], sm, l, cosf, sinf, consts, *((gather, after_gather) if l == 0 else ()))
        saved.append(t)
    dh, loss = _loss_grad(h, target)
    grads, small, exchanged = [None] * DEPTH, [None] * DEPTH, []
    for l in reversed(range(DEPTH)):
        hook = (lambda g0: exchange_of(g0, grads[1])) if (l == 0 and exchange_of is not None) else None
        dh, grads[l], small[l], got = _layer_bwd(dh, saved[l], kws[l], sm, l, cosf, sinf, consts, hook)
        exchanged = got or exchanged
    return loss[0, 0], dh, grads, small, exchanged


def kernel(x, positions, pre_mix_norm, w_in, q_norm, w_uq, kv_norm, w_ukv, conv_w, conv_b, dt_bias, a_log, d_skip, ssd_norm, w_out, post_mix_norm, pre_mlp_norm, w_up, w_down, post_mlp_norm, loss_target, m_pre_mix_norm, m_w_in, m_q_norm, m_w_uq, m_kv_norm, m_w_ukv, m_conv_w, m_conv_b, m_dt_bias, m_a_log, m_d_skip, m_ssd_norm, m_w_out, m_post_mix_norm, m_pre_mlp_norm, m_w_up, m_w_down, m_post_mlp_norm, v_pre_mix_norm, v_w_in, v_q_norm, v_w_uq, v_kv_norm, v_w_ukv, v_conv_w, v_conv_b, v_dt_bias, v_a_log, v_d_skip, v_ssd_norm, v_w_out, v_post_mix_norm, v_pre_mlp_norm, v_w_up, v_w_down, v_post_mlp_norm):
    w = dict(pre_mix_norm=pre_mix_norm, w_in=w_in, q_norm=q_norm, w_uq=w_uq, kv_norm=kv_norm, w_ukv=w_ukv, conv_w=conv_w,
             conv_b=conv_b, dt_bias=dt_bias, a_log=a_log, d_skip=d_skip, ssd_norm=ssd_norm, w_out=w_out,
             post_mix_norm=post_mix_norm, pre_mlp_norm=pre_mlp_norm, w_up=w_up, w_down=w_down, post_mlp_norm=post_mlp_norm)
    m = dict(pre_mix_norm=m_pre_mix_norm, w_in=m_w_in, q_norm=m_q_norm, w_uq=m_w_uq, kv_norm=m_kv_norm, w_ukv=m_w_ukv,
             conv_w=m_conv_w, conv_b=m_conv_b, dt_bias=m_dt_bias, a_log=m_a_log, d_skip=m_d_skip, ssd_norm=m_ssd_norm,
             w_out=m_w_out, post_mix_norm=m_post_mix_norm, pre_mlp_norm=m_pre_mlp_norm, w_up=m_w_up, w_down=m_w_down,
             post_mlp_norm=m_post_mlp_norm)
    v = dict(pre_mix_norm=v_pre_mix_norm, w_in=v_w_in, q_norm=v_q_norm, w_uq=v_w_uq, kv_norm=v_kv_norm, w_ukv=v_w_ukv,
             conv_w=v_conv_w, conv_b=v_conv_b, dt_bias=v_dt_bias, a_log=v_a_log, d_skip=v_d_skip, ssd_norm=v_ssd_norm,
             w_out=v_w_out, post_mix_norm=v_post_mix_norm, pre_mlp_norm=v_pre_mlp_norm, w_up=v_w_up, w_down=v_w_down,
             post_mlp_norm=v_post_mlp_norm)
    sm = {name: w[name] for name, _ in _SMALL}

    wire = lambda name, l: _wire_shard(name, w[name][l])
    first = _comm("gather", [wire(name, 0) for name in _EARLY], "weight_gather_first")
    kws = [_early_weights({name: _from_wire(name, a) for name, a in zip(_EARLY, first)}), None]
    behind = [(name, 0) for name in _LATE] + [(name, 1) for name, _ in _SHARDED]

    def after_gather(gathered):
        got = {key: _from_wire(key[0], a) for key, a in zip(behind, gathered)}
        kws[0].update(_late_weights({name: got[name, 0] for name in _LATE}))
        kws[1] = {**_early_weights({name: got[name, 1] for name in _EARLY}),
                  **_late_weights({name: got[name, 1] for name in _LATE})}

    sent_behind = [(name, 1) for name, _ in _SHARDED] + [(name, 0) for name in _LATE]

    def exchange_of(g0, g1):
        blocks = {**{(name, 1): a for name, a in _shard_grads(g1).items()},
                  **{(name, 0): a for name, a in _shard_grads(g0).items()}}
        return [blocks[key] for key in sent_behind]

    loss_part, dx, grads, small, exchanged = _local_step(
        x[0], positions[0], kws, sm, loss_target[0], [wire(*key) for key in behind], after_gather, exchange_of)
    slots = dict(zip(sent_behind, exchanged))
    last = _shard_grads({k: grads[0][k] for k in ("win", "wuq", "wkv", "conv_w")})
    slots.update({(name, 0): a for name, a in zip(_EARLY, _comm("exchange", [last[name] for name in _EARLY], "grad_exchange_last"))})
    g_small = _unpack_small(_all_reduce_small(_pack_small({name: jnp.stack([small[l][name] for l in range(DEPTH)])
                                                           for name, _ in _SMALL})))
    loss = lax.psum(loss_part, ("x", "y", "c"))

    grad, delta, new_m, new_v = {}, {}, {}, {}
    for name, _ in _SHARDED:
        grad[name], delta[name], new_m[name], new_v[name] = _sum_adamw(
            [slots[name, 0], slots[name, 1]], w[name], m[name], v[name], f"sum_adamw_{name}")
    pk = lambda d: _pack_small({name: d[name] for name, _ in _SMALL})
    d_, m_, v_ = _adamw(pk(w), pk(g_small), pk(m), pk(v), "adamw_small")
    for dst, packed in ((delta, d_), (new_m, m_), (new_v, v_)):
        dst.update(_unpack_small(packed))
    grad.update(g_small)

    outs = [loss, dx[None]]
    for d in (grad, delta, new_m, new_v):
        outs += [d[name] for name in _WEIGHT_ORDER]
    return tuple(outs)
```

```python
import jax
import jax.numpy as jnp
import numpy as np
from jax import lax
from jax.experimental import pallas as pl
from jax.experimental.pallas import tpu as pltpu

F32 = jnp.float32
BF16 = jnp.bfloat16
HI = lax.Precision.HIGHEST

D_MODEL = 1024
DEPTH = 2
N_DEV = 8
CHUNK = 64
EPS = 1e-6
MLA_HEADS = 8
QK_NOPE = 64
QK_ROPE = 32
V_DIM = 64
Q_RANK = 768
KV_RANK = 256
ROPE_THETA = 10000.0
SSD_HEADS = 8
SSD_P = 64
SSD_INNER = 512
SSD_GROUPS = 2
SSD_N = 128
CONV_W = 4
CONV_DIM = 1024
D_FF = 4096
IN_PROJ = 2600
HEAD_PAD = 128
IN_PAD = 2688
MISC_ROPE = 64
MISC_DT = 96
ATT_SCALE = (QK_NOPE + QK_ROPE) ** -0.5
LOG2E = 1.4426950408889634
ATT_SCALE_LOG2 = ATT_SCALE * LOG2E

ADAM_LR = 0.001
ADAM_B1 = 0.9
ADAM_B2 = 0.999
ADAM_EPS = 1e-08
ADAM_WD = 0.01
ADAM_STEP = 10

TM = 512
TQ = 256
ATT_T = 512
SSD_ROWS = 256
TK_DW = 2048
VMEM_LIMIT = 56 * 1024 * 1024

_NT = (((1,), (1,)), ((), ()))
_TN = (((0,), (0,)), ((), ()))


def _params(**kw):
    return pltpu.CompilerParams(vmem_limit_bytes=VMEM_LIMIT, **kw)


def _dot(a, b, precision=None):
    return jnp.dot(a, b, preferred_element_type=F32, precision=precision)


def _dot_nt(a, b, precision=None):
    return lax.dot_general(a, b, _NT, preferred_element_type=F32, precision=precision)


def _dot_tn(a, b, precision=None):
    return lax.dot_general(a, b, _TN, preferred_element_type=F32, precision=precision)


def _split3(x):
    hi = x.astype(BF16)
    r = x - hi.astype(F32)
    mid = r.astype(BF16)
    return hi, mid, (r - mid.astype(F32)).astype(BF16)


def _dot01(x, m01, dot=_dot, left=False):
    parts = [dot(m01, p) if left else dot(p, m01) for p in _split3(x)]
    return parts[0] + parts[1] + parts[2]


def _full(shape):
    n = len(shape)
    return pl.BlockSpec(shape, lambda *_: (0,) * n)


def _resident(shape):
    n = len(shape)
    return pl.BlockSpec(shape, lambda *_: (0,) * n, pipeline_mode=pl.Buffered(1))


def _rows(tm, width):
    return pl.BlockSpec((tm, width), lambda i: (i, 0))


def _rms_fwd(x, w):
    r = lax.rsqrt(jnp.mean(x * x, axis=-1, keepdims=True) + EPS)
    return (x * r) * w


def _rms_bwd(x, w, dy):
    r = lax.rsqrt(jnp.mean(x * x, axis=-1, keepdims=True) + EPS)
    xh = x * r
    dxn = dy * w
    dx = r * (dxn - xh * jnp.mean(dxn * xh, axis=-1, keepdims=True))
    return dx, dy * xh


def _acc_rows(ref, val, first):
    s = jnp.sum(val, axis=0, keepdims=True)

    @pl.when(first)
    def _():
        ref[...] = s

    @pl.when(jnp.logical_not(first))
    def _():
        ref[...] += s


def _rope(t, cosf, sinf, sign):
    lane = lax.broadcasted_iota(jnp.int32, t.shape, 1)
    rot = jnp.where(lane < MISC_ROPE + QK_ROPE // 2, -pltpu.roll(t, HEAD_PAD - QK_ROPE // 2, 1), pltpu.roll(t, QK_ROPE // 2, 1))
    return t * cosf + sign * (rot * sinf)


def _rope_tables(pos, invf):
    s = pos.shape[0]

    def body(pos_ref, invf_ref, cos_ref, sin_ref):
        ang = pos_ref[...].astype(F32) * invf_ref[...]
        cos_ref[...] = jnp.cos(ang)
        sin_ref[...] = jnp.sin(ang)

    return pl.pallas_call(
        body, name="rope_tables", grid=(s // TM,),
        in_specs=[_rows(TM, 1), _full((1, HEAD_PAD))],
        out_specs=[_rows(TM, HEAD_PAD), _rows(TM, HEAD_PAD)],
        out_shape=[jax.ShapeDtypeStruct((s, HEAD_PAD), F32)] * 2,
    )(pos, invf)


def _inproj_fwd(h, nw, win):
    s = h.shape[0]

    def body(h_ref, nw_ref, w_ref, ub_ref, cq_ref, ckv_ref, misc_ref, z_ref, xbc_ref):
        ub = _rms_fwd(h_ref[...], nw_ref[...]).astype(BF16)
        ub_ref[...] = ub
        proj = _dot(ub, w_ref[...])
        cq_ref[...] = proj[:, 0:768]
        ckv_ref[...] = proj[:, 768:1024]
        misc_ref[...] = proj[:, 1024:1152]
        z_ref[...] = proj[:, 1152:1664]
        xbc_ref[...] = proj[:, 1664:2688]

    widths = (768, 256, 128, 512, 1024)
    return pl.pallas_call(
        body, name="inproj_fwd", grid=(s // TM,),
        in_specs=[_rows(TM, D_MODEL), _full((1, D_MODEL)), _resident((D_MODEL, IN_PAD))],
        out_specs=[_rows(TM, D_MODEL)] + [_rows(TM, w) for w in widths],
        out_shape=[jax.ShapeDtypeStruct((s, D_MODEL), BF16)] + [jax.ShapeDtypeStruct((s, w), F32) for w in widths],
        compiler_params=_params(),
    )(h, nw, win)


def _qkv_fwd(cq, ckv, misc, qnw, kvnw, wuq, wkv, cosf, sinf):
    s = cq.shape[0]

    def body(cq_ref, ckv_ref, misc_ref, qnw_ref, kvnw_ref, wuq_ref, wkv_ref, cos_ref, sin_ref,
             cqn_ref, ckvn_ref, q_ref, k_ref, v_ref):
        cosf, sinf = cos_ref[...], sin_ref[...]
        cqn = _rms_fwd(cq_ref[...], qnw_ref[...]).astype(BF16)
        cqn_ref[...] = cqn
        q = _dot(cqn, wuq_ref[...])
        ckvn = _rms_fwd(ckv_ref[...], kvnw_ref[...]).astype(BF16)
        ckvn_ref[...] = ckvn
        kv = _dot(ckvn, wkv_ref[...])
        m = misc_ref[...]
        lane = lax.broadcasted_iota(jnp.int32, m.shape, 1)
        in_rope = jnp.logical_and(lane >= MISC_ROPE, lane < MISC_ROPE + QK_ROPE)
        kr = jnp.where(in_rope, _rope(m, cosf, sinf, 1.0), 0.0)
        for hd in range(MLA_HEADS):
            cols = slice(hd * HEAD_PAD, (hd + 1) * HEAD_PAD)
            q_ref[:, cols] = _rope(q[:, cols], cosf, sinf, 1.0).astype(BF16)
            k_ref[:, cols] = (kv[:, cols] + kr).astype(BF16)
        vv = kv[:, MLA_HEADS * HEAD_PAD:]
        vlane = lax.broadcasted_iota(jnp.int32, vv.shape, 1)
        ones_at = jnp.where((vlane // HEAD_PAD) % 2 == 0, V_DIM, 0)
        v_ref[...] = jnp.where(vlane % HEAD_PAD == ones_at, 1.0, vv).astype(BF16)

    wide = MLA_HEADS * HEAD_PAD
    return pl.pallas_call(
        body, name="qkv_fwd", grid=(s // TM,),
        in_specs=[_rows(TM, Q_RANK), _rows(TM, KV_RANK), _rows(TM, HEAD_PAD), _full((1, Q_RANK)), _full((1, KV_RANK)),
                  _resident((Q_RANK, wide)), _resident((KV_RANK, 2 * wide)), _rows(TM, HEAD_PAD), _rows(TM, HEAD_PAD)],
        out_specs=[_rows(TM, Q_RANK), _rows(TM, KV_RANK), _rows(TM, wide), _rows(TM, wide), _rows(TM, wide)],
        out_shape=[jax.ShapeDtypeStruct((s, Q_RANK), BF16), jax.ShapeDtypeStruct((s, KV_RANK), BF16)]
        + [jax.ShapeDtypeStruct((s, wide), BF16)] * 3,
        compiler_params=_params(),
    )(cq, ckv, misc, qnw, kvnw, wuq, wkv, cosf, sinf)


def _chunk_mask(t, keys_on_rows=False):
    row = lax.broadcasted_iota(jnp.int32, (t, t), 0) // CHUNK
    col = lax.broadcasted_iota(jnp.int32, (t, t), 1) // CHUNK
    return (row <= col) if keys_on_rows else (col <= row)


def _attn_fwd(q, k, v, gather=()):
    s = q.shape[0]
    t = ATT_T
    nq = s // t
    pair = 2 * HEAD_PAD
    ng = len(gather)

    def body(q_ref, k_ref, v_ref, *rest):
        g_in, (o_ref, lse_ref), g_out = rest[:ng], rest[ng:ng + 2], rest[ng + 2:2 * ng + 2]
        m_s, acc_s = rest[2 * ng + 2:2 * ng + 4]
        qi = pl.program_id(1)
        _hosted_comm("gather", g_in, g_out, rest[2 * ng + 4:],
                     jnp.logical_and(pl.program_id(0) == 0, qi == 0),
                     jnp.logical_and(pl.program_id(0) == MLA_HEADS // 2 - 1, qi == nq - 1))
        m_s[...] = jnp.full(m_s.shape, -jnp.inf, F32)
        acc_s[...] = jnp.zeros(acc_s.shape, F32)

        def step(kb, masked):
            r0 = pl.multiple_of(kb * t, t)
            for hh in range(2):
                cols = slice(hh * HEAD_PAD, (hh + 1) * HEAD_PAD)
                sc = _dot_nt(q_ref[:, cols], k_ref[pl.ds(r0, t), cols]) * ATT_SCALE_LOG2
                if masked:
                    sc = jnp.where(_chunk_mask(t), sc, -jnp.inf)
                m_old = m_s[hh]
                m_new = jnp.maximum(m_old, jnp.max(sc, axis=-1, keepdims=True))
                alpha = jnp.exp2(m_old - m_new)
                p = jnp.exp2(sc - jnp.tile(m_new, (1, t // HEAD_PAD)))
                acc_s[hh] = alpha * acc_s[hh] + _dot(p.astype(BF16), v_ref[pl.ds(r0, t), cols])
                m_s[hh] = m_new

        def loop(kb, c):
            step(kb, False)
            return c

        lax.fori_loop(0, qi, loop, 0)
        step(qi, True)
        for hh in range(2):
            cols = slice(hh * HEAD_PAD, (hh + 1) * HEAD_PAD)
            acc = acc_s[hh]
            ones_at = V_DIM * (1 - hh)
            l = jnp.broadcast_to(acc[:, ones_at:ones_at + 1], acc.shape)
            o_ref[:, cols] = (acc / l).astype(BF16)
            lse_ref[hh] = (m_s[hh] + jnp.log(l) * LOG2E).T[0:8, :]

    outs = pl.pallas_call(
        body, name="attn_fwd_gather" if ng else "attn_fwd", grid=(MLA_HEADS // 2, nq),
        in_specs=[pl.BlockSpec((t, pair), lambda h, i: (i, h)),
                  pl.BlockSpec((s, pair), lambda h, i: (0, h)),
                  pl.BlockSpec((s, pair), lambda h, i: (0, h))] + [_ANY] * ng,
        out_specs=[pl.BlockSpec((t, pair), lambda h, i: (i, h)),
                   pl.BlockSpec((2, 8, t), lambda h, i: (h, 0, i))] + [_ANY] * ng,
        out_shape=[jax.ShapeDtypeStruct((s, MLA_HEADS * HEAD_PAD), BF16), jax.ShapeDtypeStruct((MLA_HEADS, 8, s), F32)]
        + _comm_out_shapes("gather", gather),
        scratch_shapes=[pltpu.VMEM((2, t, HEAD_PAD), F32), pltpu.VMEM((2, t, HEAD_PAD), F32)]
        + (_comm_scratch(ng) if ng else []),
        compiler_params=_params(),
    )(q, k, v, *gather)
    return outs[0], outs[1], list(outs[2:])


def _ssd_consts():
    emisc = np.zeros((HEAD_PAD, SSD_INNER), np.float32)
    for hd in range(SSD_HEADS):
        emisc[MISC_DT + hd, hd * SSD_P:(hd + 1) * SSD_P] = 1.0
    idx = np.arange(CHUNK)
    tri = (idx[:, None] >= idx[None, :]).astype(np.float32)
    return tuple(jnp.asarray(m, BF16) for m in (emisc, emisc.T.copy(), tri, tri.T.copy()))


def _ssd_chunk_common(cc, misc, emisc, tri, trit, dtb, a_exp):
    xa = cc * jax.nn.sigmoid(cc)
    dtr = _dot01(misc, emisc) + dtb
    dt = jax.nn.softplus(dtr)
    a = dt * a_exp
    acs = _dot01(a, tri, left=True)
    acs_t = _dot01(a, trit, dot=_dot_tn)
    alast = acs[CHUNK - 1:CHUNK, :]
    return xa, dtr, dt, acs, acs_t, alast


def _decay(acs, acs_t, hd):
    row = lax.broadcasted_iota(jnp.int32, (CHUNK, CHUNK), 0)
    col = lax.broadcasted_iota(jnp.int32, (CHUNK, CHUNK), 1)
    diff = acs[:, hd * SSD_P:hd * SSD_P + 1] - acs_t[hd * SSD_P:hd * SSD_P + 1, :]
    return jnp.exp(jnp.where(row >= col, diff, -jnp.inf))


def _half_mask(hh):
    lane = lax.broadcasted_iota(jnp.int32, (CHUNK, 2 * SSD_P), 1)
    return (lane >= SSD_P) if hh else (lane < SSD_P)


def _gate_norm(y, zz, nw):
    yz = y * (zz * jax.nn.sigmoid(zz))
    outs, rs = [], []
    half = SSD_INNER // SSD_GROUPS
    for g in range(SSD_GROUPS):
        yg = yz[:, g * half:(g + 1) * half]
        r = lax.rsqrt(jnp.mean(yg * yg, axis=-1, keepdims=True) + EPS)
        outs.append(yg * r)
        rs.append(r)
    return yz, jnp.concatenate(outs, axis=1), rs


def _ssd_fwd(xraw, misc, z, cw, cb, dtb, a_exp, d_exp, nw, consts):
    s = xraw.shape[0]
    nb = s // SSD_ROWS
    ncb = SSD_ROWS // CHUNK
    emisc, _, tri, trit = consts

    def body(x_ref, misc_ref, z_ref, cw_ref, cb_ref, dtb_ref, a_ref, d_ref, nw_ref, emisc_ref, tri_ref, trit_ref,
             c_ref, prev_ref, ypre_ref, yssd_ref, tail_s, state_s):
        i = pl.program_id(0)

        @pl.when(i == 0)
        def _():
            tail_s[...] = jnp.zeros(tail_s.shape, F32)
            state_s[...] = jnp.zeros(state_s.shape, F32)

        x = x_ref[...]
        xext = jnp.concatenate([tail_s[...], x], axis=0)
        acc = x * cw_ref[CONV_W - 1:CONV_W, :] + cb_ref[...]
        for j in range(1, CONV_W):
            acc = acc + pltpu.roll(xext, j, 0)[8:, :] * cw_ref[CONV_W - 1 - j:CONV_W - j, :]
        tail_s[...] = x[SSD_ROWS - 8:, :]
        c_ref[...] = acc

        def chunk(ci, carry):
            r0 = pl.multiple_of(ci * CHUNK, CHUNK)
            xa, _, dt, acs, acs_t, alast = _ssd_chunk_common(
                c_ref[pl.ds(r0, CHUNK), :], misc_ref[pl.ds(r0, CHUNK), :], emisc_ref[...], tri_ref[...], trit_ref[...],
                dtb_ref[...], a_ref[...])
            xs = xa[:, :SSD_INNER]
            xdt = xs * dt
            prev = state_s[...]
            prev_ref[ci] = prev
            wgt = (xdt * jnp.exp(alast - acs)).astype(BF16)
            e = jnp.exp(acs)
            ys, new_states = [], []
            for g in range(SSD_GROUPS):
                bm = xa[:, SSD_INNER + g * SSD_N:SSD_INNER + (g + 1) * SSD_N].astype(BF16)
                cm = xa[:, SSD_INNER + SSD_GROUPS * SSD_N + g * SSD_N:SSD_INNER + SSD_GROUPS * SSD_N + (g + 1) * SSD_N].astype(BF16)
                cb_g = _dot_nt(cm, bm)
                gl = slice(g * 256, (g + 1) * 256)
                new_states.append(_dot_tn(bm, wgt[:, gl]))
                yoff = _dot(cm, prev[:, gl].astype(BF16)) * e[:, gl]
                for jj in range(2):
                    pair = 2 * g + jj
                    pl_ = slice(pair * 128, (pair + 1) * 128)
                    xp = xdt[:, pl_]
                    yp = yoff[:, jj * 128:(jj + 1) * 128]
                    for hh in range(2):
                        sc = (cb_g * _decay(acs, acs_t, 2 * pair + hh)).astype(BF16)
                        yp = yp + _dot(sc, jnp.where(_half_mask(hh), xp, 0.0).astype(BF16))
                    ys.append(yp)
            y = jnp.concatenate(ys, axis=1) + d_ref[...] * xs
            state_s[...] = prev * jnp.exp(alast) + jnp.concatenate(new_states, axis=1)
            ypre_ref[pl.ds(r0, CHUNK), :] = y
            _, yn, _ = _gate_norm(y, z_ref[pl.ds(r0, CHUNK), :], None)
            yssd_ref[pl.ds(r0, CHUNK), :] = (yn * nw_ref[...]).astype(BF16)
            return carry

        lax.fori_loop(0, ncb, chunk, 0, unroll=True)

    return pl.pallas_call(
        body, name="ssd_fwd", grid=(nb,),
        in_specs=[_rows(SSD_ROWS, CONV_DIM), _rows(SSD_ROWS, HEAD_PAD), _rows(SSD_ROWS, SSD_INNER),
                  _full((CONV_W, CONV_DIM)), _full((1, CONV_DIM)), _full((1, SSD_INNER)), _full((1, SSD_INNER)),
                  _full((1, SSD_INNER)), _full((1, SSD_INNER)), _full((HEAD_PAD, SSD_INNER)), _full((CHUNK, CHUNK)),
                  _full((CHUNK, CHUNK))],
        out_specs=[_rows(SSD_ROWS, CONV_DIM), pl.BlockSpec((ncb, SSD_N, SSD_INNER), lambda i: (i, 0, 0)),
                   _rows(SSD_ROWS, SSD_INNER), _rows(SSD_ROWS, SSD_INNER)],
        out_shape=[jax.ShapeDtypeStruct((s, CONV_DIM), F32), jax.ShapeDtypeStruct((s // CHUNK, SSD_N, SSD_INNER), F32),
                   jax.ShapeDtypeStruct((s, SSD_INNER), F32), jax.ShapeDtypeStruct((s, SSD_INNER), BF16)],
        scratch_shapes=[pltpu.VMEM((8, CONV_DIM), F32), pltpu.VMEM((SSD_N, SSD_INNER), F32)],
        compiler_params=_params(),
    )(xraw, misc, z, cw, cb, dtb, a_exp, d_exp, nw, emisc, tri, trit)


def _outproj_fwd(oe, yssd, wout, h, nw):
    s = h.shape[0]
    wide = MLA_HEADS * HEAD_PAD

    def body(oe_ref, y_ref, w_ref, h_ref, nw_ref, mixed_ref, h1_ref):
        mixed = _dot(oe_ref[...], w_ref[0:wide, :]) + _dot(y_ref[...], w_ref[wide:, :])
        mixed_ref[...] = mixed
        h1_ref[...] = h_ref[...] + _rms_fwd(mixed, nw_ref[...])

    return pl.pallas_call(
        body, name="outproj_fwd", grid=(s // TM,),
        in_specs=[_rows(TM, wide), _rows(TM, SSD_INNER), _resident((wide + SSD_INNER, D_MODEL)), _rows(TM, D_MODEL),
                  _full((1, D_MODEL))],
        out_specs=[_rows(TM, D_MODEL), _rows(TM, D_MODEL)],
        out_shape=[jax.ShapeDtypeStruct((s, D_MODEL), F32)] * 2,
        compiler_params=_params(),
    )(oe, yssd, wout, h, nw)


def _mlp_fwd(h1, prew, wup, wdown, postw):
    s = h1.shape[0]
    fb = D_FF // N_DEV

    def body(h_ref, prew_ref, up_ref, down_ref, postw_ref, mb_ref, d_ref, h2_ref):
        hh = h_ref[...]
        mb = _rms_fwd(hh, prew_ref[...]).astype(BF16)
        mb_ref[...] = mb
        d = jnp.zeros((TM, D_MODEL), F32)
        for j in range(N_DEV):
            a = _dot(mb, up_ref[j])
            r = jnp.square(jnp.maximum(a, 0.0)).astype(BF16)
            d = d + _dot(r, down_ref[j])
        d_ref[...] = d
        h2_ref[...] = hh + _rms_fwd(d, postw_ref[...])

    return pl.pallas_call(
        body, name="mlp_fwd", grid=(s // TM,),
        in_specs=[_rows(TM, D_MODEL), _full((1, D_MODEL)), _resident((N_DEV, D_MODEL, fb)), _resident((N_DEV, fb, D_MODEL)),
                  _full((1, D_MODEL))],
        out_specs=[_rows(TM, D_MODEL)] * 3,
        out_shape=[jax.ShapeDtypeStruct((s, D_MODEL), BF16), jax.ShapeDtypeStruct((s, D_MODEL), F32),
                   jax.ShapeDtypeStruct((s, D_MODEL), F32)],
        compiler_params=_params(),
    )(h1, prew, wup, wdown, postw)


def _loss_grad(h, target):
    s = h.shape[0]

    def body(h_ref, t_ref, dh_ref, loss_ref):
        diff = h_ref[...] - t_ref[...]
        dh_ref[...] = diff * (1.0 / D_MODEL)
        part = 0.5 * jnp.sum(jnp.mean(diff * diff, axis=-1, keepdims=True), axis=0, keepdims=True)
        _acc_rows(loss_ref, part, pl.program_id(0) == 0)

    return pl.pallas_call(
        body, name="loss_grad", grid=(s // TM,),
        in_specs=[_rows(TM, D_MODEL)] * 2,
        out_specs=[_rows(TM, D_MODEL), _full((1, 1))],
        out_shape=[jax.ShapeDtypeStruct((s, D_MODEL), F32), jax.ShapeDtypeStruct((1, 1), F32)],
    )(h, target)


def _mlp_bwd(dh2, d, h1, mb, prew, wup, wdown, postw):
    s = dh2.shape[0]
    fb = D_FF // N_DEV
    tm = TM // 2

    def body(dh2_ref, d_ref, h1_ref, mb_ref, prew_ref, up_ref, down_ref, postw_ref,
             dh1_ref, da_ref, r_ref, dd_ref, gpost_ref, gpre_ref):
        first = pl.program_id(0) == 0
        dh2 = dh2_ref[...]
        dd, gpost = _rms_bwd(d_ref[...], postw_ref[...], dh2)
        _acc_rows(gpost_ref, gpost, first)
        ddb = dd.astype(BF16)
        dd_ref[...] = ddb
        mb = mb_ref[...]
        dm = jnp.zeros((tm, D_MODEL), F32)
        for j in range(N_DEV):
            a = jnp.maximum(_dot(mb, up_ref[j]), 0.0)
            r_ref[j] = jnp.square(a).astype(BF16)
            da = (_dot_nt(ddb, down_ref[j]) * (2.0 * a)).astype(BF16)
            da_ref[j] = da
            dm = dm + _dot_nt(da, up_ref[j])
        dx, gpre = _rms_bwd(h1_ref[...], prew_ref[...], dm)
        _acc_rows(gpre_ref, gpre, first)
        dh1_ref[...] = dh2 + dx

    stacked = pl.BlockSpec((N_DEV, tm, fb), lambda i: (0, i, 0))
    return pl.pallas_call(
        body, name="mlp_bwd", grid=(s // tm,),
        in_specs=[_rows(tm, D_MODEL)] * 4 + [_full((1, D_MODEL)), _resident((N_DEV, D_MODEL, fb)), _resident((N_DEV, fb, D_MODEL)),
                                              _full((1, D_MODEL))],
        out_specs=[_rows(tm, D_MODEL), stacked, stacked, _rows(tm, D_MODEL), _full((1, D_MODEL)), _full((1, D_MODEL))],
        out_shape=[jax.ShapeDtypeStruct((s, D_MODEL), F32), jax.ShapeDtypeStruct((N_DEV, s, fb), BF16),
                   jax.ShapeDtypeStruct((N_DEV, s, fb), BF16), jax.ShapeDtypeStruct((s, D_MODEL), BF16),
                   jax.ShapeDtypeStruct((1, D_MODEL), F32), jax.ShapeDtypeStruct((1, D_MODEL), F32)],
        compiler_params=_params(),
    )(dh2, d, h1, mb, prew, wup, wdown, postw)


def _matmul_tn(a, b, name, tk=TK_DW):
    s, m = a.shape
    n = b.shape[1]
    tn = n if n <= 1024 else (n // 2 if (n // 2) % 128 == 0 else n // 3)
    tk = min(tk, s)
    assert n % tn == 0 and tn % 128 == 0 and s % tk == 0

    def body(a_ref, b_ref, o_ref):
        part = _dot_tn(a_ref[...], b_ref[...])

        @pl.when(pl.program_id(1) == 0)
        def _():
            o_ref[...] = part

        @pl.when(pl.program_id(1) != 0)
        def _():
            o_ref[...] += part

    return pl.pallas_call(
        body, name=name, grid=(n // tn, s // tk),
        in_specs=[pl.BlockSpec((tk, m), lambda j, k: (k, 0)), pl.BlockSpec((tk, tn), lambda j, k: (k, j))],
        out_specs=pl.BlockSpec((m, tn), lambda j, k: (0, j)),
        out_shape=jax.ShapeDtypeStruct((m, n), F32),
        compiler_params=_params(),
    )(a, b)


def _matmul_tn_stacked(a, b, name, a_stacked, tk=TK_DW):
    tk = min(tk, a.shape[-2])
    if a_stacked:
        _, s, m = a.shape
        n = b.shape[1]
        in_specs = [pl.BlockSpec((1, tk, m), lambda j, k: (j, k, 0)), pl.BlockSpec((tk, n), lambda j, k: (k, 0))]
    else:
        s, m = a.shape
        n = b.shape[2]
        in_specs = [pl.BlockSpec((tk, m), lambda j, k: (k, 0)), pl.BlockSpec((1, tk, n), lambda j, k: (j, k, 0))]

    nk = s // tk

    def body(a_ref, b_ref, o_ref, acc_s):
        av = a_ref[0] if a_stacked else a_ref[...]
        bv = b_ref[...] if a_stacked else b_ref[0]
        part = _dot_tn(av, bv)
        k = pl.program_id(1)

        @pl.when(k == 0)
        def _():
            acc_s[...] = part

        @pl.when(jnp.logical_and(k != 0, k != nk - 1))
        def _():
            acc_s[...] += part

        @pl.when(k == nk - 1)
        def _():
            o_ref[0] = (part if nk == 1 else acc_s[...] + part).astype(BF16)

    return pl.pallas_call(
        body, name=name, grid=(N_DEV, nk),
        in_specs=in_specs,
        out_specs=pl.BlockSpec((1, m, n), lambda j, k: (j, 0, 0)),
        out_shape=jax.ShapeDtypeStruct((N_DEV, m, n), BF16),
        scratch_shapes=[pltpu.VMEM((m, n), F32)],
        compiler_params=_params(),
    )(a, b)


def _outproj_bwd(dh1, mixed, nw, wout):
    s = dh1.shape[0]
    wide = MLA_HEADS * HEAD_PAD

    def body(dh1_ref, mixed_ref, nw_ref, w_ref, dmix_ref, doe_ref, dy_ref, gnw_ref):
        dmix, gnw = _rms_bwd(mixed_ref[...], nw_ref[...], dh1_ref[...])
        _acc_rows(gnw_ref, gnw, pl.program_id(0) == 0)
        dmb = dmix.astype(BF16)
        dmix_ref[...] = dmb
        doe_ref[...] = _dot_nt(dmb, w_ref[0:wide, :]).astype(BF16)
        dy_ref[...] = _dot_nt(dmb, w_ref[wide:, :])

    return pl.pallas_call(
        body, name="outproj_bwd", grid=(s // TM,),
        in_specs=[_rows(TM, D_MODEL), _rows(TM, D_MODEL), _full((1, D_MODEL)), _resident((wide + SSD_INNER, D_MODEL))],
        out_specs=[_rows(TM, D_MODEL), _rows(TM, wide), _rows(TM, SSD_INNER), _full((1, D_MODEL))],
        out_shape=[jax.ShapeDtypeStruct((s, D_MODEL), BF16), jax.ShapeDtypeStruct((s, wide), BF16),
                   jax.ShapeDtypeStruct((s, SSD_INNER), F32), jax.ShapeDtypeStruct((1, D_MODEL), F32)],
        compiler_params=_params(),
    )(dh1, mixed, nw, wout)


def _attn_delta(o, do):
    s = o.shape[0]
    wide = MLA_HEADS * HEAD_PAD

    def body(o_ref, do_ref, d_ref):
        ones = jnp.ones((8, HEAD_PAD), BF16)
        for hd in range(MLA_HEADS):
            cols = slice(hd * HEAD_PAD, (hd + 1) * HEAD_PAD)
            prod = o_ref[:, cols].astype(F32) * do_ref[:, cols].astype(F32)
            d_ref[hd] = _dot01(prod, ones, dot=_dot_nt, left=True)

    return pl.pallas_call(
        body, name="attn_delta", grid=(s // TM,),
        in_specs=[_rows(TM, wide), _rows(TM, wide)],
        out_specs=pl.BlockSpec((MLA_HEADS, 8, TM), lambda i: (0, 0, i)),
        out_shape=jax.ShapeDtypeStruct((MLA_HEADS, 8, s), F32),
    )(o, do)


def _attn_bwd(q, k, v, do, lse, delta, exchange=()):
    s = q.shape[0]
    t = ATT_T
    nq = s // t
    pair = 2 * HEAD_PAD
    ne = len(exchange)

    def body(q_ref, k_ref, v_ref, do_ref, lse_ref, delta_ref, *rest):
        e_in, (dq_ref, dk_ref, dv_ref), e_out = rest[:ne], rest[ne:ne + 3], rest[ne + 3:2 * ne + 3]
        kb = pl.program_id(1)
        _hosted_comm("exchange", e_in, e_out, rest[2 * ne + 3:],
                     jnp.logical_and(pl.program_id(0) == 0, kb == 0),
                     jnp.logical_and(pl.program_id(0) == MLA_HEADS // 2 - 1, kb == nq - 1))

        @pl.when(kb == 0)
        def _():
            dq_ref[...] = jnp.zeros(dq_ref.shape, F32)

        dk_ref[...] = jnp.zeros(dk_ref.shape, F32)
        dv_ref[...] = jnp.zeros(dv_ref.shape, F32)

        def step(qb, masked):
            r0 = pl.multiple_of(qb * t, t)
            for hh in range(2):
                cols = slice(hh * HEAD_PAD, (hh + 1) * HEAD_PAD)
                kk = k_ref[:, cols]
                qq = q_ref[pl.ds(r0, t), cols]
                dd = do_ref[pl.ds(r0, t), cols]
                sc = _dot_nt(kk, qq) * ATT_SCALE_LOG2
                if masked:
                    sc = jnp.where(_chunk_mask(t, keys_on_rows=True), sc, -jnp.inf)
                p = jnp.exp2(sc - lse_ref[hh, 0:1, pl.ds(r0, t)])
                dv_ref[:, cols] += _dot(p.astype(BF16), dd)
                dp = _dot_nt(v_ref[:, cols], dd)
                ds = (p * (dp - delta_ref[hh, 0:1, pl.ds(r0, t)]) * ATT_SCALE).astype(BF16)
                dk_ref[:, cols] += _dot(ds, qq)
                dq_ref[pl.ds(r0, t), cols] += _dot_tn(ds, kk)

        def loop(qb, c):
            step(qb, False)
            return c

        step(kb, True)
        lax.fori_loop(kb + 1, nq, loop, 0)

    whole = pl.BlockSpec((s, pair), lambda h, i: (0, h))
    tile = pl.BlockSpec((t, pair), lambda h, i: (i, h))
    rowvec = pl.BlockSpec((2, 8, s), lambda h, i: (h, 0, 0))
    wide = MLA_HEADS * HEAD_PAD
    outs = pl.pallas_call(
        body, name="attn_bwd_exchange" if ne else "attn_bwd", grid=(MLA_HEADS // 2, nq),
        in_specs=[whole, tile, tile, whole, rowvec, rowvec] + [_ANY] * ne,
        out_specs=[whole, tile, tile] + [_ANY] * ne,
        out_shape=[jax.ShapeDtypeStruct((s, wide), F32)] * 3 + _comm_out_shapes("exchange", exchange),
        scratch_shapes=_comm_scratch(ne) if ne else [],
        compiler_params=_params(),
    )(q, k, v, do, lse, delta, *exchange)
    return outs[0], outs[1], outs[2], list(outs[3:])


def _ssd_bwd(dy, ypre, z, c, xraw, misc, prev, cw, dtb, a_exp, d_exp, nw, consts):
    s = dy.shape[0]
    nb = s // SSD_ROWS
    ncb = SSD_ROWS // CHUNK
    emisc, emisc_t, tri, trit = consts

    def body(dy_ref, ypre_ref, z_ref, c_ref, x_ref, xprev_ref, misc_ref, prev_ref, cw_ref, dtb_ref, a_ref, d_ref, nw_ref,
             emisc_ref, emisct_ref, tri_ref, trit_ref,
             dz_ref, dx_ref, dmisc_ref, gnw_ref, gd_ref, galog_ref, gdtb_ref, gcw_ref, gcb_ref,
             dst_s, dc_s, head_s):
        i = pl.program_id(0)
        first = i == 0

        @pl.when(first)
        def _():
            dst_s[...] = jnp.zeros(dst_s.shape, F32)
            head_s[...] = jnp.zeros(head_s.shape, F32)
            gnw_ref[...] = jnp.zeros(gnw_ref.shape, F32)
            gd_ref[...] = jnp.zeros(gd_ref.shape, F32)
            galog_ref[...] = jnp.zeros(galog_ref.shape, F32)
            gdtb_ref[...] = jnp.zeros(gdtb_ref.shape, F32)

        a_exp_v = a_ref[...]
        a8 = _dot01(a_exp_v, emisct_ref[...]) * (1.0 / SSD_P)

        def chunk(cr, carry):
            ci = ncb - 1 - cr
            r0 = pl.multiple_of(ci * CHUNK, CHUNK)
            cc = c_ref[pl.ds(r0, CHUNK), :]
            mm = misc_ref[pl.ds(r0, CHUNK), :]
            xa, dtr, dt, acs, acs_t, alast = _ssd_chunk_common(cc, mm, emisc_ref[...], tri_ref[...], trit_ref[...],
                                                              dtb_ref[...], a_exp_v)
            xs = xa[:, :SSD_INNER]
            xdt = xs * dt
            y = ypre_ref[pl.ds(r0, CHUNK), :]
            zz = z_ref[pl.ds(r0, CHUNK), :]
            yz, yn, rs = _gate_norm(y, zz, None)
            dyo = dy_ref[pl.ds(r0, CHUNK), :]
            gnw_ref[...] += jnp.sum(dyo * yn, axis=0, keepdims=True)
            dyn = dyo * nw_ref[...]
            half = SSD_INNER // SSD_GROUPS
            dyz_parts = []
            for g in range(SSD_GROUPS):
                gl = slice(g * half, (g + 1) * half)
                dyz_parts.append(rs[g] * (dyn[:, gl] - yn[:, gl] * jnp.mean(dyn[:, gl] * yn[:, gl], axis=-1, keepdims=True)))
            dyz = jnp.concatenate(dyz_parts, axis=1)
            sg = jax.nn.sigmoid(zz)
            dz_ref[pl.ds(r0, CHUNK), :] = dyz * y * (sg * (1.0 + zz * (1.0 - sg)))
            dyp = dyz * (zz * sg)
            dypb = dyp.astype(BF16)
            gd_ref[...] += jnp.sum(dyp * xs, axis=0, keepdims=True)
            prev = prev_ref[ci]
            dst = dst_s[...]
            cd = jnp.exp(alast)
            e = jnp.exp(acs)
            dsx = jnp.exp(alast - acs)
            wgt = (xdt * dsx).astype(BF16)
            dze = (dyp * e).astype(BF16)
            glast = jnp.sum(dst * prev, axis=0, keepdims=True) * cd
            dprev_parts, dxdt_parts, dxdt_state_parts, dbm, dcm, yoff_parts = [], [], [], [], [], []
            lane8 = lax.broadcasted_iota(jnp.int32, (CHUNK, HEAD_PAD), 1)
            diag8 = jnp.zeros((CHUNK, HEAD_PAD), F32)
            for g in range(SSD_GROUPS):
                gl = slice(g * 256, (g + 1) * 256)
                bm = xa[:, SSD_INNER + g * SSD_N:SSD_INNER + (g + 1) * SSD_N].astype(BF16)
                cm = xa[:, SSD_INNER + SSD_GROUPS * SSD_N + g * SSD_N:SSD_INNER + SSD_GROUPS * SSD_N + (g + 1) * SSD_N].astype(BF16)
                prev_g = prev[:, gl].astype(BF16)
                dst_g = dst[:, gl].astype(BF16)
                dcm_g = _dot_nt(dze[:, gl], prev_g)
                dprev_parts.append(_dot_tn(cm, dze[:, gl]))
                dxs_state = _dot(bm, dst_g) * dsx[:, gl]
                dbm_g = _dot_nt(wgt[:, gl], dst_g)
                cb_g = _dot_nt(cm, bm)
                dcb = jnp.zeros((CHUNK, CHUNK), F32)
                diag_parts = []
                for jj in range(2):
                    pair = 2 * g + jj
                    pl_ = slice(pair * 128, (pair + 1) * 128)
                    xp = xdt[:, pl_]
                    dyp_p = dypb[:, pl_]
                    dxp = jnp.zeros((CHUNK, 128), F32)
                    for hh in range(2):
                        hd = 2 * pair + hh
                        dec = _decay(acs, acs_t, hd)
                        xm = jnp.where(_half_mask(hh), xp, 0.0).astype(BF16)
                        dsc = _dot_nt(dyp_p, xm) * dec
                        dcb = dcb + dsc
                        sc = (cb_g * dec).astype(BF16)
                        dxp = dxp + jnp.where(_half_mask(hh), _dot_tn(sc, dyp_p), 0.0)
                        dm = dsc * cb_g
                        diag8 = diag8 + jnp.where(lane8 == MISC_DT + hd, jnp.sum(dm - dm.T, axis=1, keepdims=True), 0.0)
                    diag_parts.append(dxp)
                dcbb = dcb.astype(BF16)
                dcm.append(dcm_g + _dot(dcbb, bm))
                dbm.append(dbm_g + _dot_tn(dcbb, cm))
                dxdt_state_parts.append(dxs_state)
                dxdt_parts.append(jnp.concatenate(diag_parts, axis=1) + dxs_state)
                yoff_parts.append(_dot(cm, prev_g) * e[:, gl])
            dxdt = jnp.concatenate(dxdt_parts, axis=1)
            dxdt_state = jnp.concatenate(dxdt_state_parts, axis=1)
            dst_s[...] = dst * cd + jnp.concatenate(dprev_parts, axis=1)
            dacs = dyp * jnp.concatenate(yoff_parts, axis=1) - xdt * dxdt_state
            last = jnp.sum(xdt * dxdt_state, axis=0, keepdims=True) + glast
            row = lax.broadcasted_iota(jnp.int32, (CHUNK, SSD_INNER), 0)
            dacs = dacs + jnp.where(row == CHUNK - 1, last, 0.0)
            dacs8 = _dot01(dacs, emisct_ref[...]) + diag8
            da8 = _dot01(dacs8, trit_ref[...], left=True)
            ddt8 = da8 * a8 + _dot01(dxdt * xs, emisct_ref[...])
            dtr8 = mm + _dot01(dtb_ref[...], emisct_ref[...]) * (1.0 / SSD_P)
            dt8 = jax.nn.softplus(dtr8)
            lane = lax.broadcasted_iota(jnp.int32, (CHUNK, HEAD_PAD), 1)
            on_dt = jnp.logical_and(lane >= MISC_DT, lane < MISC_DT + SSD_HEADS)
            ddtr8 = jnp.where(on_dt, ddt8 * jax.nn.sigmoid(dtr8), 0.0)
            dmisc_ref[pl.ds(r0, CHUNK), :] = ddtr8
            gdtb_ref[...] += jnp.sum(ddtr8, axis=0, keepdims=True)
            galog_ref[...] += jnp.sum(jnp.where(on_dt, da8 * dt8, 0.0), axis=0, keepdims=True) * a8
            dxs = d_ref[...] * dyp + dxdt * dt
            dxa = jnp.concatenate([dxs] + dbm + dcm, axis=1)
            sc_ = jax.nn.sigmoid(cc)
            dc_s[pl.ds(r0, CHUNK), :] = dxa * (sc_ * (1.0 + cc * (1.0 - sc_)))
            return carry

        lax.fori_loop(0, ncb, chunk, 0, unroll=True)

        dc = dc_s[...]
        dcext = jnp.concatenate([dc, head_s[...]], axis=0)
        dx = dc * cw_ref[CONV_W - 1:CONV_W, :]
        for j in range(1, CONV_W):
            dx = dx + pltpu.roll(dcext, SSD_ROWS + 8 - j, 0)[:SSD_ROWS, :] * cw_ref[CONV_W - 1 - j:CONV_W - j, :]
        dx_ref[...] = dx
        head_s[...] = dc[:8, :]
        xprev = jnp.where(i == nb - 1, 0.0, xprev_ref[...])
        xext = jnp.concatenate([xprev, x_ref[...]], axis=0)
        rows = [jnp.sum(dc * pltpu.roll(xext, CONV_W - 1 - kk, 0)[8:, :], axis=0, keepdims=True) for kk in range(CONV_W)]
        gcw = jnp.concatenate(rows, axis=0)

        @pl.when(first)
        def _():
            gcw_ref[...] = gcw
            gcb_ref[...] = jnp.sum(dc, axis=0, keepdims=True)

        @pl.when(jnp.logical_not(first))
        def _():
            gcw_ref[...] += gcw
            gcb_ref[...] += jnp.sum(dc, axis=0, keepdims=True)

    def rev(width):
        return pl.BlockSpec((SSD_ROWS, width), lambda i: (nb - 1 - i, 0))

    per8 = SSD_ROWS // 8
    return pl.pallas_call(
        body, name="ssd_bwd", grid=(nb,),
        in_specs=[rev(SSD_INNER), rev(SSD_INNER), rev(SSD_INNER), rev(CONV_DIM), rev(CONV_DIM),
                  pl.BlockSpec((8, CONV_DIM), lambda i: (jnp.maximum((nb - 1 - i) * per8 - 1, 0), 0)),
                  rev(HEAD_PAD), pl.BlockSpec((ncb, SSD_N, SSD_INNER), lambda i: (nb - 1 - i, 0, 0)),
                  _full((CONV_W, CONV_DIM)), _full((1, SSD_INNER)), _full((1, SSD_INNER)), _full((1, SSD_INNER)),
                  _full((1, SSD_INNER)), _full((HEAD_PAD, SSD_INNER)), _full((SSD_INNER, HEAD_PAD)), _full((CHUNK, CHUNK)),
                  _full((CHUNK, CHUNK))],
        out_specs=[rev(SSD_INNER), rev(CONV_DIM), rev(HEAD_PAD), _full((1, SSD_INNER)), _full((1, SSD_INNER)),
                   _full((1, HEAD_PAD)), _full((1, HEAD_PAD)), _full((CONV_W, CONV_DIM)), _full((1, CONV_DIM))],
        out_shape=[jax.ShapeDtypeStruct((s, SSD_INNER), F32), jax.ShapeDtypeStruct((s, CONV_DIM), F32),
                   jax.ShapeDtypeStruct((s, HEAD_PAD), F32), jax.ShapeDtypeStruct((1, SSD_INNER), F32),
                   jax.ShapeDtypeStruct((1, SSD_INNER), F32), jax.ShapeDtypeStruct((1, HEAD_PAD), F32),
                   jax.ShapeDtypeStruct((1, HEAD_PAD), F32), jax.ShapeDtypeStruct((CONV_W, CONV_DIM), F32),
                   jax.ShapeDtypeStruct((1, CONV_DIM), F32)],
        scratch_shapes=[pltpu.VMEM((SSD_N, SSD_INNER), F32), pltpu.VMEM((SSD_ROWS, CONV_DIM), F32), pltpu.VMEM((8, CONV_DIM), F32)],
        compiler_params=_params(),
    )(dy, ypre, z, c, xraw, xraw, misc, prev, cw, dtb, a_exp, d_exp, nw, emisc, emisc_t, tri, trit)


def _qkv_bwd(dq, dk, dv, cq, ckv, qnw, kvnw, wuq, wkv, cosf, sinf):
    s = dq.shape[0]
    wide = MLA_HEADS * HEAD_PAD

    def body(dq_ref, dk_ref, dv_ref, cq_ref, ckv_ref, qnw_ref, kvnw_ref, wuq_ref, wkv_ref, cos_ref, sin_ref,
             dqb_ref, dkvb_ref, dcq_ref, dckv_ref, dmisc_ref, gq_ref, gkv_ref):
        first = pl.program_id(0) == 0
        cosf, sinf = cos_ref[...], sin_ref[...]
        dkr = jnp.zeros((TM, HEAD_PAD), F32)
        for hd in range(MLA_HEADS):
            cols = slice(hd * HEAD_PAD, (hd + 1) * HEAD_PAD)
            dqb_ref[:, cols] = _rope(dq_ref[:, cols], cosf, sinf, -1.0).astype(BF16)
            dkh = dk_ref[:, cols]
            dkvb_ref[:, cols] = dkh.astype(BF16)
            dkr = dkr + dkh
        dkvb_ref[:, wide:] = dv_ref[...].astype(BF16)
        lane = lax.broadcasted_iota(jnp.int32, dkr.shape, 1)
        in_rope = jnp.logical_and(lane >= MISC_ROPE, lane < MISC_ROPE + QK_ROPE)
        dmisc_ref[...] = jnp.where(in_rope, _rope(jnp.where(in_rope, dkr, 0.0), cosf, sinf, -1.0), 0.0)
        dcq, gq = _rms_bwd(cq_ref[...], qnw_ref[...], _dot_nt(dqb_ref[...], wuq_ref[...]))
        dcq_ref[...] = dcq
        _acc_rows(gq_ref, gq, first)
        dckv, gkv = _rms_bwd(ckv_ref[...], kvnw_ref[...], _dot_nt(dkvb_ref[...], wkv_ref[...]))
        dckv_ref[...] = dckv
        _acc_rows(gkv_ref, gkv, first)

    return pl.pallas_call(
        body, name="qkv_bwd", grid=(s // TM,),
        in_specs=[_rows(TM, wide)] * 3 + [_rows(TM, Q_RANK), _rows(TM, KV_RANK), _full((1, Q_RANK)), _full((1, KV_RANK)),
                                          _resident((Q_RANK, wide)), _resident((KV_RANK, 2 * wide)), _rows(TM, HEAD_PAD), _rows(TM, HEAD_PAD)],
        out_specs=[_rows(TM, wide), _rows(TM, 2 * wide), _rows(TM, Q_RANK), _rows(TM, KV_RANK), _rows(TM, HEAD_PAD),
                   _full((1, Q_RANK)), _full((1, KV_RANK))],
        out_shape=[jax.ShapeDtypeStruct((s, wide), BF16), jax.ShapeDtypeStruct((s, 2 * wide), BF16),
                   jax.ShapeDtypeStruct((s, Q_RANK), F32), jax.ShapeDtypeStruct((s, KV_RANK), F32),
                   jax.ShapeDtypeStruct((s, HEAD_PAD), F32), jax.ShapeDtypeStruct((1, Q_RANK), F32),
                   jax.ShapeDtypeStruct((1, KV_RANK), F32)],
        compiler_params=_params(),
    )(dq, dk, dv, cq, ckv, qnw, kvnw, wuq, wkv, cosf, sinf)


def _inproj_bwd(dcq, dckv, dmisc_rope, dmisc_dt, dz, dxbc, h, dh1, nw, win):
    s = h.shape[0]

    def body(dcq_ref, dckv_ref, dma_ref, dmb_ref, dz_ref, dxbc_ref, h_ref, dh1_ref, nw_ref, w_ref, dproj_ref, dh0_ref, gnw_ref):
        dproj_ref[:, 0:768] = dcq_ref[...].astype(BF16)
        dproj_ref[:, 768:1024] = dckv_ref[...].astype(BF16)
        dproj_ref[:, 1024:1152] = (dma_ref[...] + dmb_ref[...]).astype(BF16)
        dproj_ref[:, 1152:1664] = dz_ref[...].astype(BF16)
        dproj_ref[:, 1664:2688] = dxbc_ref[...].astype(BF16)
        du = _dot_nt(dproj_ref[...], w_ref[...])
        dx, gnw = _rms_bwd(h_ref[...], nw_ref[...], du)
        _acc_rows(gnw_ref, gnw, pl.program_id(0) == 0)
        dh0_ref[...] = dh1_ref[...] + dx

    return pl.pallas_call(
        body, name="inproj_bwd", grid=(s // TM,),
        in_specs=[_rows(TM, Q_RANK), _rows(TM, KV_RANK), _rows(TM, HEAD_PAD), _rows(TM, HEAD_PAD), _rows(TM, SSD_INNER),
                  _rows(TM, CONV_DIM), _rows(TM, D_MODEL), _rows(TM, D_MODEL), _full((1, D_MODEL)), _resident((D_MODEL, IN_PAD))],
        out_specs=[_rows(TM, IN_PAD), _rows(TM, D_MODEL), _full((1, D_MODEL))],
        out_shape=[jax.ShapeDtypeStruct((s, IN_PAD), BF16), jax.ShapeDtypeStruct((s, D_MODEL), F32),
                   jax.ShapeDtypeStruct((1, D_MODEL), F32)],
        compiler_params=_params(),
    )(dcq, dckv, dmisc_rope, dmisc_dt, dz, dxbc, h, dh1, nw, win)


def _row_tile(rows, cols):
    cap = max(8, (1 << 18) // max(cols, 128))
    best = None
    for t in range(8, rows + 1, 8):
        if rows % t == 0 and t <= cap:
            best = t
    return best if best is not None else rows


def _adamw(w, g, m, v, name):
    rows, cols = w.shape
    tr = _row_tile(rows, cols)

    def body(w_ref, g_ref, m_ref, v_ref, d_ref, m2_ref, v2_ref):
        gg = g_ref[...]
        m2 = ADAM_B1 * m_ref[...] + (1.0 - ADAM_B1) * gg
        v2 = ADAM_B2 * v_ref[...] + (1.0 - ADAM_B2) * jnp.square(gg)
        m_hat = m2 / (1.0 - ADAM_B1 ** ADAM_STEP)
        v_hat = v2 / (1.0 - ADAM_B2 ** ADAM_STEP)
        d_ref[...] = -ADAM_LR * (m_hat / (jnp.sqrt(v_hat) + ADAM_EPS) + ADAM_WD * w_ref[...])
        m2_ref[...] = m2
        v2_ref[...] = v2

    spec = pl.BlockSpec((tr, cols), lambda i: (i, 0))
    return pl.pallas_call(
        body, name=name, grid=(rows // tr,),
        in_specs=[spec] * 4, out_specs=[spec] * 3,
        out_shape=[jax.ShapeDtypeStruct((rows, cols), F32)] * 3,
    )(w, g, m, v)


def _sum_adamw(slots, w, m, v, name):
    _, rows, cols = w.shape
    tr = _row_tile(rows, cols)
    nb = rows // tr

    def body(s0_ref, s1_ref, w_ref, m_ref, v_ref, g_ref, d_ref, m2_ref, v2_ref):
        for l, ref in enumerate((s0_ref, s1_ref)):
            @pl.when(pl.program_id(0) == l)
            def _(ref=ref):
                acc = ref[0].astype(F32)
                for i in range(1, N_DEV):
                    acc = acc + ref[i].astype(F32)
                g_ref[...] = acc

        gg = g_ref[...]
        m2 = ADAM_B1 * m_ref[...] + (1.0 - ADAM_B1) * gg
        v2 = ADAM_B2 * v_ref[...] + (1.0 - ADAM_B2) * jnp.square(gg)
        m_hat = m2 / (1.0 - ADAM_B1 ** ADAM_STEP)
        v_hat = v2 / (1.0 - ADAM_B2 ** ADAM_STEP)
        d_ref[...] = -ADAM_LR * (m_hat / (jnp.sqrt(v_hat) + ADAM_EPS) + ADAM_WD * w_ref[...])
        m2_ref[...] = m2
        v2_ref[...] = v2

    slot_spec = lambda layer: pl.BlockSpec((N_DEV, tr, cols), lambda l, i: (0, jnp.where(l == layer, i, (nb - 1) * (1 - layer)), 0))
    spec = pl.BlockSpec((None, tr, cols), lambda l, i: (l, i, 0))
    return pl.pallas_call(
        body, name=name, grid=(DEPTH, nb),
        in_specs=[slot_spec(0), slot_spec(1), spec, spec, spec], out_specs=[spec] * 4,
        out_shape=[jax.ShapeDtypeStruct(w.shape, F32)] * 4,
        compiler_params=_params(),
    )(slots[0], slots[1], w, m, v)


_MESH = pl.DeviceIdType.MESH
_ANY = pl.BlockSpec(memory_space=pl.ANY)


def _my_place():
    return lax.axis_index("x"), lax.axis_index("y"), lax.axis_index("c")


def _flip(place, k):
    x, y, c = place
    return (1 - x if k & 4 else x, 1 - y if k & 2 else y, 1 - c if k & 1 else c)


def _block_id(place):
    return 4 * place[0] + 2 * place[1] + place[2]


def _peer_copies(kind, in_refs, out_refs, send_sems, recv_sems, local_sems):
    me = _my_place()
    my = _block_id(me)
    remote, local = [], []
    for a, (x_ref, out_ref) in enumerate(zip(in_refs, out_refs)):
        src_of = (lambda place, r=x_ref: r) if kind == "gather" else (lambda place, r=x_ref: r.at[_block_id(place)])
        local.append(pltpu.make_async_copy(src_of(me), out_ref.at[my], local_sems.at[a]))
        for k in range(1, N_DEV):
            peer = _flip(me, k)
            remote.append(pltpu.make_async_remote_copy(
                src_ref=src_of(peer), dst_ref=out_ref.at[my], send_sem=send_sems.at[a * 7 + k - 1],
                recv_sem=recv_sems.at[a * 7 + k - 1], device_id=peer, device_id_type=_MESH))
    return remote, local


def _comm_out_shapes(kind, arrays):
    return [jax.ShapeDtypeStruct((N_DEV, *a.shape) if kind == "gather" else a.shape, a.dtype) for a in arrays]


def _comm_scratch(n):
    return [pltpu.SemaphoreType.DMA((7 * n,)), pltpu.SemaphoreType.DMA((7 * n,)), pltpu.SemaphoreType.DMA((n,))]


def _hosted_comm(kind, in_refs, out_refs, sems, first, last):
    if not in_refs:
        return

    @pl.when(first)
    def _():
        remote, local = _peer_copies(kind, in_refs, out_refs, *sems)
        for cp in local + remote:
            cp.start()

    @pl.when(last)
    def _():
        remote, local = _peer_copies(kind, in_refs, out_refs, *sems)
        for cp in remote:
            cp.wait()
        for cp in local:
            cp.wait()


def _comm(kind, arrays, name):
    n = len(arrays)

    def body(*refs):
        remote, local = _peer_copies(kind, refs[:n], refs[n:2 * n], *refs[2 * n:])
        for cp in local + remote:
            cp.start()
        for cp in remote:
            cp.wait()
        for cp in local:
            cp.wait()

    return pl.pallas_call(
        body, name=name, out_shape=_comm_out_shapes(kind, arrays),
        in_specs=[_ANY] * n, out_specs=[_ANY] * n, scratch_shapes=_comm_scratch(n),
    )(*arrays)


def _all_reduce_small(part):
    rows, lanes = part.shape
    vmem = pl.BlockSpec(memory_space=pltpu.VMEM)

    def body(x_ref, gath_ref, sum_ref, send_sems, recv_sems):
        me = _my_place()
        my = _block_id(me)
        gath_ref[my] = x_ref[...]
        copies = []
        for k in range(1, N_DEV):
            cp = pltpu.make_async_remote_copy(
                src_ref=x_ref, dst_ref=gath_ref.at[my], send_sem=send_sems.at[k - 1], recv_sem=recv_sems.at[k - 1],
                device_id=_flip(me, k), device_id_type=_MESH)
            cp.start()
            copies.append(cp)
        for cp in copies:
            cp.wait()
        acc = gath_ref[0]
        for i in range(1, N_DEV):
            acc = acc + gath_ref[i]
        sum_ref[...] = acc

    return pl.pallas_call(
        body, name="small_grad_all_reduce",
        out_shape=[jax.ShapeDtypeStruct((N_DEV, rows, lanes), F32), jax.ShapeDtypeStruct((rows, lanes), F32)],
        in_specs=[vmem], out_specs=[vmem, vmem],
        scratch_shapes=[pltpu.SemaphoreType.DMA((7,)), pltpu.SemaphoreType.DMA((7,))],
    )(part)[1]


_SHARDED = (("w_in", (D_MODEL, IN_PROJ // N_DEV)), ("w_uq", (Q_RANK // N_DEV, Q_RANK)), ("w_ukv", (KV_RANK, HEAD_PAD)),
            ("conv_w", (CONV_W, CONV_DIM // N_DEV)), ("w_out", (D_MODEL // N_DEV, D_MODEL)),
            ("w_up", (D_MODEL, D_FF // N_DEV)), ("w_down", (D_FF // N_DEV, D_MODEL)))
_SMALL = (("pre_mix_norm", D_MODEL), ("q_norm", Q_RANK), ("kv_norm", KV_RANK), ("conv_b", CONV_DIM), ("dt_bias", SSD_HEADS),
          ("a_log", SSD_HEADS), ("d_skip", SSD_HEADS), ("ssd_norm", SSD_INNER), ("post_mix_norm", D_MODEL),
          ("pre_mlp_norm", D_MODEL), ("post_mlp_norm", D_MODEL))
_WEIGHT_ORDER = ("pre_mix_norm", "w_in", "q_norm", "w_uq", "kv_norm", "w_ukv", "conv_w", "conv_b", "dt_bias", "a_log", "d_skip",
                 "ssd_norm", "w_out", "post_mix_norm", "pre_mlp_norm", "w_up", "w_down", "post_mlp_norm")
_EARLY = ("w_in", "w_uq", "w_ukv", "conv_w")
_LATE = ("w_out", "w_up", "w_down")


def _wire_shard(name, a):
    return lax.bitcast_convert_type(a, BF16).reshape(CONV_W, -1) if name == "conv_w" else a.astype(BF16)


def _from_wire(name, g):
    return lax.bitcast_convert_type(g.reshape(N_DEV, CONV_W, -1, 2), F32) if name == "conv_w" else g


def _cols(stacked):
    return jnp.transpose(stacked, (1, 0, 2)).reshape(stacked.shape[1], -1)


def _early_weights(sh):
    w_in = _cols(sh["w_in"])
    zeros = lambda n: jnp.zeros((D_MODEL, n), BF16)
    s1, s2, s3, s4, s5 = 768, 1024, 1056, 1568, 2592
    win = jnp.concatenate([w_in[:, :s2], zeros(MISC_ROPE), w_in[:, s2:s3], w_in[:, s5:], zeros(HEAD_PAD - MISC_DT - SSD_HEADS),
                           w_in[:, s3:s5]], axis=1)
    w_uq = sh["w_uq"].reshape(Q_RANK, MLA_HEADS, QK_NOPE + QK_ROPE)
    wuq = jnp.pad(w_uq, ((0, 0), (0, 0), (0, HEAD_PAD - QK_NOPE - QK_ROPE))).reshape(Q_RANK, -1)
    w_ukv = _cols(sh["w_ukv"]).reshape(KV_RANK, MLA_HEADS, QK_NOPE + V_DIM)
    wkn = jnp.pad(w_ukv[..., :QK_NOPE], ((0, 0), (0, 0), (0, HEAD_PAD - QK_NOPE))).reshape(KV_RANK, -1)
    wv = w_ukv[..., QK_NOPE:].reshape(KV_RANK, 4, 2, 1, V_DIM) * jnp.eye(2, dtype=BF16).reshape(1, 1, 2, 2, 1)
    wkv = jnp.concatenate([wkn, wv.reshape(KV_RANK, -1)], axis=1)
    return dict(win=win, wuq=wuq, wkv=wkv, conv_w=_cols(sh["conv_w"]))


def _late_weights(sh):
    w_out = sh["w_out"].reshape(D_MODEL, D_MODEL)
    watt = w_out[:SSD_INNER].reshape(4, 2, 1, V_DIM, D_MODEL) * jnp.eye(2, dtype=BF16).reshape(1, 2, 2, 1, 1)
    wout = jnp.concatenate([watt.reshape(MLA_HEADS * HEAD_PAD, D_MODEL), w_out[SSD_INNER:]], axis=0)
    return dict(wout=wout, wup=sh["w_up"], wdown=sh["w_down"])


def _shard_grads(g):
    out = {}
    if "wup" in g:
        out["w_up"], out["w_down"] = g["wup"], g["wdown"]
        ae = g["wout_att"].reshape(4, 2, 2, V_DIM, D_MODEL)
        att = jnp.stack([ae[:, 0, 0], ae[:, 1, 1]], axis=1).reshape(SSD_INNER, D_MODEL)
        out["w_out"] = jnp.concatenate([att, g["wout_ssd"]], axis=0).astype(BF16).reshape(N_DEV, D_MODEL // N_DEV, D_MODEL)
    if "win" not in g:
        return out
    dwin = g["win"]
    s1, s2 = 768, 1024
    m0 = s2
    w_in = jnp.concatenate([dwin[:, :s2], dwin[:, m0 + MISC_ROPE:m0 + MISC_ROPE + QK_ROPE], dwin[:, 1152:2688],
                            dwin[:, m0 + MISC_DT:m0 + MISC_DT + SSD_HEADS]], axis=1)
    out["w_in"] = jnp.transpose(w_in.astype(BF16).reshape(D_MODEL, N_DEV, -1), (1, 0, 2))
    w_uq = g["wuq"].astype(BF16).reshape(Q_RANK, MLA_HEADS, HEAD_PAD)[..., :QK_NOPE + QK_ROPE].reshape(Q_RANK, Q_RANK)
    out["w_uq"] = w_uq.reshape(N_DEV, Q_RANK // N_DEV, Q_RANK)
    wide = MLA_HEADS * HEAD_PAD
    wkv = g["wkv"].astype(BF16)
    kn = wkv[:, :wide].reshape(KV_RANK, MLA_HEADS, HEAD_PAD)[..., :QK_NOPE]
    ve = wkv[:, wide:].reshape(KV_RANK, 4, 2, 2, V_DIM)
    vv = jnp.stack([ve[:, :, 0, 0], ve[:, :, 1, 1]], axis=2).reshape(KV_RANK, MLA_HEADS, V_DIM)
    out["w_ukv"] = jnp.transpose(jnp.concatenate([kn, vv], axis=-1), (1, 0, 2))
    out["conv_w"] = jnp.transpose(g["conv_w"].astype(BF16).reshape(CONV_W, N_DEV, -1), (1, 0, 2))
    return out


def _small_rows(n):
    return -(-n // 1024) * 8


def _pack_small(vals):
    rows = []
    for l in range(DEPTH):
        for name, n in _SMALL:
            r = _small_rows(n)
            rows.append(jnp.pad(vals[name][l].reshape(-1), (0, r * 128 - n)).reshape(r, 128))
    return jnp.concatenate(rows, axis=0)


def _unpack_small(packed):
    out, off = {name: [] for name, _ in _SMALL}, 0
    for l in range(DEPTH):
        for name, n in _SMALL:
            r = _small_rows(n)
            out[name].append(packed[off:off + r].reshape(-1)[:n])
            off += r
    return {name: jnp.stack(v) for name, v in out.items()}


def _lane_rows(vec8):
    return jnp.repeat(vec8, SSD_P).reshape(1, SSD_INNER)


def _layer_fwd(h, kw, sm, l, cosf, sinf, consts, gather=(), after_gather=None):
    row = lambda name: sm[name][l].reshape(1, -1)
    t = {}
    t["h0"] = h
    t["ub"], t["cq"], t["ckv"], t["misc"], t["z"], t["xraw"] = _inproj_fwd(h, row("pre_mix_norm"), kw["win"])
    t["cqn"], t["ckvn"], t["q"], t["k"], t["v"] = _qkv_fwd(t["cq"], t["ckv"], t["misc"], row("q_norm"), row("kv_norm"),
                                                         kw["wuq"], kw["wkv"], cosf, sinf)
    t["oe"], t["lse"], gathered = _attn_fwd(t["q"], t["k"], t["v"], gather)
    if after_gather is not None:
        after_gather(gathered)
    t["dtb"] = _lane_rows(sm["dt_bias"][l])
    t["a_exp"] = _lane_rows(-jnp.exp(sm["a_log"][l]))
    t["d_exp"] = _lane_rows(sm["d_skip"][l])
    t["c"], t["prev"], t["ypre"], t["yssd"] = _ssd_fwd(t["xraw"], t["misc"], t["z"], kw["conv_w"], row("conv_b"), t["dtb"],
                                                     t["a_exp"], t["d_exp"], row("ssd_norm"), consts)
    t["mixed"], t["h1"] = _outproj_fwd(t["oe"], t["yssd"], kw["wout"], h, row("post_mix_norm"))
    t["mb"], t["d"], h2 = _mlp_fwd(t["h1"], row("pre_mlp_norm"), kw["wup"], kw["wdown"], row("post_mlp_norm"))
    return h2, t


def _layer_bwd(dh2, t, kw, sm, l, cosf, sinf, consts, exchange_of=None):
    row = lambda name: sm[name][l].reshape(1, -1)
    g, gs = {}, {}
    dh1, dab, rb, ddb, gs["post_mlp_norm"], gs["pre_mlp_norm"] = _mlp_bwd(
        dh2, t["d"], t["h1"], t["mb"], row("pre_mlp_norm"), kw["wup"], kw["wdown"], row("post_mlp_norm"))
    g["wup"] = _matmul_tn_stacked(t["mb"], dab, f"dw_up_{l}", a_stacked=False)
    g["wdown"] = _matmul_tn_stacked(rb, ddb, f"dw_down_{l}", a_stacked=True)
    dmixb, doe, dyssd, gs["post_mix_norm"] = _outproj_bwd(dh1, t["mixed"], row("post_mix_norm"), kw["wout"])
    g["wout_att"] = _matmul_tn(t["oe"], dmixb, f"dw_out_att_{l}")
    g["wout_ssd"] = _matmul_tn(t["yssd"], dmixb, f"dw_out_ssd_{l}")
    dz, dxraw, dmisc_dt, gs["ssd_norm"], gd, galog, gdtb, g["conv_w"], gs["conv_b"] = _ssd_bwd(
        dyssd, t["ypre"], t["z"], t["c"], t["xraw"], t["misc"], t["prev"], kw["conv_w"], t["dtb"], t["a_exp"], t["d_exp"],
        row("ssd_norm"), consts)
    gs["d_skip"] = jnp.sum(gd.reshape(SSD_HEADS, SSD_P), axis=1)
    gs["a_log"] = galog[0, MISC_DT:MISC_DT + SSD_HEADS]
    gs["dt_bias"] = gdtb[0, MISC_DT:MISC_DT + SSD_HEADS]
    dq, dk, dv, exchanged = _attn_bwd(t["q"], t["k"], t["v"], doe, t["lse"], _attn_delta(t["oe"], doe),
                                      exchange_of(g) if exchange_of is not None else ())
    dqb, dkvb, dcq, dckv, dmisc_rope, gs["q_norm"], gs["kv_norm"] = _qkv_bwd(
        dq, dk, dv, t["cq"], t["ckv"], row("q_norm"), row("kv_norm"), kw["wuq"], kw["wkv"], cosf, sinf)
    g["wuq"] = _matmul_tn(t["cqn"], dqb, f"dw_uq_{l}")
    g["wkv"] = _matmul_tn(t["ckvn"], dkvb, f"dw_kv_{l}")
    dprojb, dh0, gs["pre_mix_norm"] = _inproj_bwd(dcq, dckv, dmisc_rope, dmisc_dt, dz, dxraw, t["h0"], dh1,
                                                  row("pre_mix_norm"), kw["win"])
    g["win"] = _matmul_tn(t["ub"], dprojb, f"dw_in_{l}")
    return dh0, g, {k: v.reshape(-1) for k, v in gs.items()}, exchanged


def _local_step(x, positions, kws, sm, target, gather=(), after_gather=None, exchange_of=None):
    inv_freq = ROPE_THETA ** (-jnp.arange(0, QK_ROPE, 2, dtype=F32) / QK_ROPE)
    invf = jnp.zeros((HEAD_PAD,), F32).at[MISC_ROPE:MISC_ROPE + QK_ROPE].set(jnp.concatenate([inv_freq, inv_freq]))
    cosf, sinf = _rope_tables(positions.reshape(-1, 1), invf.reshape(1, HEAD_PAD))
    consts = _ssd_consts()
    h, saved = x, []
    for l in range(DEPTH):
        h, t = _layer_fwd(h, kws[l], sm, l, cosf, sinf, consts, *((gather, after_gather) if l == 0 else ()))
        saved.append(t)
    dh, loss = _loss_grad(h, target)
    grads, small, exchanged = [None] * DEPTH, [None] * DEPTH, []
    for l in reversed(range(DEPTH)):
        hook = (lambda g0: exchange_of(g0, grads[1])) if (l == 0 and exchange_of is not None) else None
        dh, grads[l], small[l], got = _layer_bwd(dh, saved[l], kws[l], sm, l, cosf, sinf, consts, hook)
        exchanged = got or exchanged
    return loss[0, 0], dh, grads, small, exchanged


def kernel(x, positions, pre_mix_norm, w_in, q_norm, w_uq, kv_norm, w_ukv, conv_w, conv_b, dt_bias, a_log, d_skip, ssd_norm, w_out, post_mix_norm, pre_mlp_norm, w_up, w_down, post_mlp_norm, loss_target, m_pre_mix_norm, m_w_in, m_q_norm, m_w_uq, m_kv_norm, m_w_ukv, m_conv_w, m_conv_b, m_dt_bias, m_a_log, m_d_skip, m_ssd_norm, m_w_out, m_post_mix_norm, m_pre_mlp_norm, m_w_up, m_w_down, m_post_mlp_norm, v_pre_mix_norm, v_w_in, v_q_norm, v_w_uq, v_kv_norm, v_w_ukv, v_conv_w, v_conv_b, v_dt_bias, v_a_log, v_d_skip, v_ssd_norm, v_w_out, v_post_mix_norm, v_pre_mlp_norm, v_w_up, v_w_down, v_post_mlp_norm):
    w = dict(pre_mix_norm=pre_mix_norm, w_in=w_in, q_norm=q_norm, w_uq=w_uq, kv_norm=kv_norm, w_ukv=w_ukv, conv_w=conv_w,
             conv_b=conv_b, dt_bias=dt_bias, a_log=a_log, d_skip=d_skip, ssd_norm=ssd_norm, w_out=w_out,
             post_mix_norm=post_mix_norm, pre_mlp_norm=pre_mlp_norm, w_up=w_up, w_down=w_down, post_mlp_norm=post_mlp_norm)
    m = dict(pre_mix_norm=m_pre_mix_norm, w_in=m_w_in, q_norm=m_q_norm, w_uq=m_w_uq, kv_norm=m_kv_norm, w_ukv=m_w_ukv,
             conv_w=m_conv_w, conv_b=m_conv_b, dt_bias=m_dt_bias, a_log=m_a_log, d_skip=m_d_skip, ssd_norm=m_ssd_norm,
             w_out=m_w_out, post_mix_norm=m_post_mix_norm, pre_mlp_norm=m_pre_mlp_norm, w_up=m_w_up, w_down=m_w_down,
             post_mlp_norm=m_post_mlp_norm)
    v = dict(pre_mix_norm=v_pre_mix_norm, w_in=v_w_in, q_norm=v_q_norm, w_uq=v_w_uq, kv_norm=v_kv_norm, w_ukv=v_w_ukv,
             conv_w=v_conv_w, conv_b=v_conv_b, dt_bias=v_dt_bias, a_log=v_a_log, d_skip=v_d_skip, ssd_norm=v_ssd_norm,
             w_out=v_w_out, post_mix_norm=v_post_mix_norm, pre_mlp_norm=v_pre_mlp_norm, w_up=v_w_up, w_down=v_w_down,
             post_mlp_norm=v_post_mlp_norm)
    sm = {name: w[name] for name, _ in _SMALL}

    wire = lambda name, l: _wire_shard(name, w[name][l])
    first = _comm("gather", [wire(name, 0) for name in _EARLY], "weight_gather_first")
    kws = [_early_weights({name: _from_wire(name, a) for name, a in zip(_EARLY, first)}), None]
    behind = [(name, 0) for name in _LATE] + [(name, 1) for name, _ in _SHARDED]

    def after_gather(gathered):
        got = {key: _from_wire(key[0], a) for key, a in zip(behind, gathered)}
        kws[0].update(_late_weights({name: got[name, 0] for name in _LATE}))
        kws[1] = {**_early_weights({name: got[name, 1] for name in _EARLY}),
                  **_late_weights({name: got[name, 1] for name in _LATE})}

    sent_behind = [(name, 1) for name, _ in _SHARDED] + [(name, 0) for name in _LATE]

    def exchange_of(g0, g1):
        blocks = {**{(name, 1): a for name, a in _shard_grads(g1).items()},
                  **{(name, 0): a for name, a in _shard_grads(g0).items()}}
        return [blocks[key] for key in sent_behind]

    loss_part, dx, grads, small, exchanged = _local_step(
        x[0], positions[0], kws, sm, loss_target[0], [wire(*key) for key in behind], after_gather, exchange_of)
    slots = dict(zip(sent_behind, exchanged))
    last = _shard_grads({k: grads[0][k] for k in ("win", "wuq", "wkv", "conv_w")})
    slots.update({(name, 0): a for name, a in zip(_EARLY, _comm("exchange", [last[name] for name in _EARLY], "grad_exchange_last"))})
    g_small = _unpack_small(_all_reduce_small(_pack_small({name: jnp.stack([small[l][name] for l in range(DEPTH)])
                                                           for name, _ in _SMALL})))
    loss = lax.psum(loss_part, ("x", "y", "c"))

    grad, delta, new_m, new_v = {}, {}, {}, {}
    for name, _ in _SHARDED:
        grad[name], delta[name], new_m[name], new_v[name] = _sum_adamw(
            [slots[name, 0], slots[name, 1]], w[name], m[name], v[name], f"sum_adamw_{name}")
    pk = lambda d: _pack_small({name: d[name] for name, _ in _SMALL})
    d_, m_, v_ = _adamw(pk(w), pk(g_small), pk(m), pk(v), "adamw_small")
    for dst, packed in ((delta, d_), (new_m, m_), (new_v, v_)):
        dst.update(_unpack_small(packed))
    grad.update(g_small)

    outs = [loss, dx[None]]
    for d in (grad, delta, new_m, new_v):
        outs += [d[name] for name in _WEIGHT_ORDER]
    return tuple(outs)
```

```python
import jax
import jax.numpy as jnp
import numpy as np
from jax import lax
from jax.experimental import pallas as pl
from jax.experimental.pallas import tpu as pltpu

F32 = jnp.float32
BF16 = jnp.bfloat16
HI = lax.Precision.HIGHEST

D_MODEL = 1024
DEPTH = 2
N_DEV = 8
CHUNK = 64
EPS = 1e-6
MLA_HEADS = 8
QK_NOPE = 64
QK_ROPE = 32
V_DIM = 64
Q_RANK = 768
KV_RANK = 256
ROPE_THETA = 10000.0
SSD_HEADS = 8
SSD_P = 64
SSD_INNER = 512
SSD_GROUPS = 2
SSD_N = 128
CONV_W = 4
CONV_DIM = 1024
D_FF = 4096
IN_PROJ = 2600
HEAD_PAD = 128
IN_PAD = 2688
MISC_ROPE = 64
MISC_DT = 96
ATT_SCALE = (QK_NOPE + QK_ROPE) ** -0.5
LOG2E = 1.4426950408889634
ATT_SCALE_LOG2 = ATT_SCALE * LOG2E

ADAM_LR = 0.001
ADAM_B1 = 0.9
ADAM_B2 = 0.999
ADAM_EPS = 1e-08
ADAM_WD = 0.01
ADAM_STEP = 10

TM = 512
TQ = 256
ATT_T = 512
ATT_G = 4
SSD_ROWS = 256
TK_DW = 2048
VMEM_LIMIT = 56 * 1024 * 1024

_NT = (((1,), (1,)), ((), ()))
_TN = (((0,), (0,)), ((), ()))


def _params(**kw):
    return pltpu.CompilerParams(vmem_limit_bytes=VMEM_LIMIT, **kw)


def _dot(a, b, precision=None):
    return jnp.dot(a, b, preferred_element_type=F32, precision=precision)


def _dot_nt(a, b, precision=None):
    return lax.dot_general(a, b, _NT, preferred_element_type=F32, precision=precision)


def _dot_tn(a, b, precision=None):
    return lax.dot_general(a, b, _TN, preferred_element_type=F32, precision=precision)


def _split3(x):
    hi = x.astype(BF16)
    r = x - hi.astype(F32)
    mid = r.astype(BF16)
    return hi, mid, (r - mid.astype(F32)).astype(BF16)


def _dot01(x, m01, dot=_dot, left=False):
    parts = [dot(m01, p) if left else dot(p, m01) for p in _split3(x)]
    return parts[0] + parts[1] + parts[2]


def _full(shape):
    n = len(shape)
    return pl.BlockSpec(shape, lambda *_: (0,) * n)


def _resident(shape):
    n = len(shape)
    return pl.BlockSpec(shape, lambda *_: (0,) * n, pipeline_mode=pl.Buffered(1))


def _rows(tm, width):
    return pl.BlockSpec((tm, width), lambda i: (i, 0))


def _rms_fwd(x, w):
    r = lax.rsqrt(jnp.mean(x * x, axis=-1, keepdims=True) + EPS)
    return (x * r) * w


def _rms_bwd(x, w, dy):
    r = lax.rsqrt(jnp.mean(x * x, axis=-1, keepdims=True) + EPS)
    xh = x * r
    dxn = dy * w
    dx = r * (dxn - xh * jnp.mean(dxn * xh, axis=-1, keepdims=True))
    return dx, dy * xh


def _acc_rows(ref, val, first):
    s = jnp.sum(val, axis=0, keepdims=True)

    @pl.when(first)
    def _():
        ref[...] = s

    @pl.when(jnp.logical_not(first))
    def _():
        ref[...] += s


def _rope(t, cosf, sinf, sign):
    lane = lax.broadcasted_iota(jnp.int32, t.shape, 1)
    rot = jnp.where(lane < MISC_ROPE + QK_ROPE // 2, -pltpu.roll(t, HEAD_PAD - QK_ROPE // 2, 1), pltpu.roll(t, QK_ROPE // 2, 1))
    return t * cosf + sign * (rot * sinf)


def _rope_tables(pos, invf):
    s = pos.shape[0]

    def body(pos_ref, invf_ref, cos_ref, sin_ref):
        ang = pos_ref[...].astype(F32) * invf_ref[...]
        cos_ref[...] = jnp.cos(ang)
        sin_ref[...] = jnp.sin(ang)

    return pl.pallas_call(
        body, name="rope_tables", grid=(s // TM,),
        in_specs=[_rows(TM, 1), _full((1, HEAD_PAD))],
        out_specs=[_rows(TM, HEAD_PAD), _rows(TM, HEAD_PAD)],
        out_shape=[jax.ShapeDtypeStruct((s, HEAD_PAD), F32)] * 2,
    )(pos, invf)


def _inproj_fwd(h, nw, win):
    s = h.shape[0]

    def body(h_ref, nw_ref, w_ref, ub_ref, cq_ref, ckv_ref, misc_ref, z_ref, xbc_ref):
        ub = _rms_fwd(h_ref[...], nw_ref[...]).astype(BF16)
        ub_ref[...] = ub
        proj = _dot(ub, w_ref[...])
        cq_ref[...] = proj[:, 0:768]
        ckv_ref[...] = proj[:, 768:1024]
        misc_ref[...] = proj[:, 1024:1152]
        z_ref[...] = proj[:, 1152:1664]
        xbc_ref[...] = proj[:, 1664:2688]

    widths = (768, 256, 128, 512, 1024)
    return pl.pallas_call(
        body, name="inproj_fwd", grid=(s // TM,),
        in_specs=[_rows(TM, D_MODEL), _full((1, D_MODEL)), _resident((D_MODEL, IN_PAD))],
        out_specs=[_rows(TM, D_MODEL)] + [_rows(TM, w) for w in widths],
        out_shape=[jax.ShapeDtypeStruct((s, D_MODEL), BF16)] + [jax.ShapeDtypeStruct((s, w), F32) for w in widths],
        compiler_params=_params(),
    )(h, nw, win)


def _qkv_fwd(cq, ckv, misc, qnw, kvnw, wuq, wkv, cosf, sinf):
    s = cq.shape[0]

    def body(cq_ref, ckv_ref, misc_ref, qnw_ref, kvnw_ref, wuq_ref, wkv_ref, cos_ref, sin_ref,
             cqn_ref, ckvn_ref, q_ref, k_ref, v_ref):
        cosf, sinf = cos_ref[...], sin_ref[...]
        cqn = _rms_fwd(cq_ref[...], qnw_ref[...]).astype(BF16)
        cqn_ref[...] = cqn
        q = _dot(cqn, wuq_ref[...])
        ckvn = _rms_fwd(ckv_ref[...], kvnw_ref[...]).astype(BF16)
        ckvn_ref[...] = ckvn
        kv = _dot(ckvn, wkv_ref[...])
        m = misc_ref[...]
        lane = lax.broadcasted_iota(jnp.int32, m.shape, 1)
        in_rope = jnp.logical_and(lane >= MISC_ROPE, lane < MISC_ROPE + QK_ROPE)
        kr = jnp.where(in_rope, _rope(m, cosf, sinf, 1.0), 0.0)
        for hd in range(MLA_HEADS):
            cols = slice(hd * HEAD_PAD, (hd + 1) * HEAD_PAD)
            q_ref[:, cols] = _rope(q[:, cols], cosf, sinf, 1.0).astype(BF16)
            k_ref[:, cols] = (kv[:, cols] + kr).astype(BF16)
        vv = kv[:, MLA_HEADS * HEAD_PAD:]
        vlane = lax.broadcasted_iota(jnp.int32, vv.shape, 1)
        ones_at = jnp.where((vlane // HEAD_PAD) % 2 == 0, V_DIM, 0)
        v_ref[...] = jnp.where(vlane % HEAD_PAD == ones_at, 1.0, vv).astype(BF16)

    wide = MLA_HEADS * HEAD_PAD
    return pl.pallas_call(
        body, name="qkv_fwd", grid=(s // TM,),
        in_specs=[_rows(TM, Q_RANK), _rows(TM, KV_RANK), _rows(TM, HEAD_PAD), _full((1, Q_RANK)), _full((1, KV_RANK)),
                  _resident((Q_RANK, wide)), _resident((KV_RANK, 2 * wide)), _rows(TM, HEAD_PAD), _rows(TM, HEAD_PAD)],
        out_specs=[_rows(TM, Q_RANK), _rows(TM, KV_RANK), _rows(TM, wide), _rows(TM, wide), _rows(TM, wide)],
        out_shape=[jax.ShapeDtypeStruct((s, Q_RANK), BF16), jax.ShapeDtypeStruct((s, KV_RANK), BF16)]
        + [jax.ShapeDtypeStruct((s, wide), BF16)] * 3,
        compiler_params=_params(),
    )(cq, ckv, misc, qnw, kvnw, wuq, wkv, cosf, sinf)


def _chunk_mask(t, keys_on_rows=False):
    row = lax.broadcasted_iota(jnp.int32, (t, t), 0) // CHUNK
    col = lax.broadcasted_iota(jnp.int32, (t, t), 1) // CHUNK
    return (row <= col) if keys_on_rows else (col <= row)


def _attn_fwd(q, k, v, gather=()):
    s = q.shape[0]
    t = ATT_T
    nq = s // t
    pair = ATT_G * HEAD_PAD
    ng = len(gather)

    def body(q_ref, k_ref, v_ref, *rest):
        g_in, (o_ref, lse_ref), g_out = rest[:ng], rest[ng:ng + 2], rest[ng + 2:2 * ng + 2]
        m_s, acc_s = rest[2 * ng + 2:2 * ng + 4]
        qi = pl.program_id(1)
        _hosted_comm("gather", g_in, g_out, rest[2 * ng + 4:],
                     jnp.logical_and(pl.program_id(0) == 0, qi == 0),
                     jnp.logical_and(pl.program_id(0) == MLA_HEADS // ATT_G - 1, qi == nq - 1))
        m_s[...] = jnp.full(m_s.shape, -jnp.inf, F32)
        acc_s[...] = jnp.zeros(acc_s.shape, F32)

        def step(kb, masked):
            r0 = pl.multiple_of(kb * t, t)
            for hh in range(ATT_G):
                cols = slice(hh * HEAD_PAD, (hh + 1) * HEAD_PAD)
                sc = _dot_nt(q_ref[:, cols], k_ref[pl.ds(r0, t), cols]) * ATT_SCALE_LOG2
                if masked:
                    sc = jnp.where(_chunk_mask(t), sc, -jnp.inf)
                m_old = m_s[hh]
                m_new = jnp.maximum(m_old, jnp.max(sc, axis=-1, keepdims=True))
                alpha = jnp.exp2(m_old - m_new)
                p = jnp.exp2(sc - jnp.tile(m_new, (1, t // HEAD_PAD)))
                acc_s[hh] = alpha * acc_s[hh] + _dot(p.astype(BF16), v_ref[pl.ds(r0, t), cols])
                m_s[hh] = m_new

        def loop(kb, c):
            step(kb, False)
            return c

        lax.fori_loop(0, qi, loop, 0)
        step(qi, True)
        for hh in range(ATT_G):
            cols = slice(hh * HEAD_PAD, (hh + 1) * HEAD_PAD)
            acc = acc_s[hh]
            ones_at = V_DIM * (1 - hh % 2)
            l = jnp.broadcast_to(acc[:, ones_at:ones_at + 1], acc.shape)
            o_ref[:, cols] = (acc / l).astype(BF16)
            lse_ref[hh] = (m_s[hh] + jnp.log(l) * LOG2E).T[0:8, :]

    outs = pl.pallas_call(
        body, name="attn_fwd_gather" if ng else "attn_fwd", grid=(MLA_HEADS // ATT_G, nq),
        in_specs=[pl.BlockSpec((t, pair), lambda h, i: (i, h)),
                  pl.BlockSpec((s, pair), lambda h, i: (0, h)),
                  pl.BlockSpec((s, pair), lambda h, i: (0, h))] + [_ANY] * ng,
        out_specs=[pl.BlockSpec((t, pair), lambda h, i: (i, h)),
                   pl.BlockSpec((ATT_G, 8, t), lambda h, i: (h, 0, i))] + [_ANY] * ng,
        out_shape=[jax.ShapeDtypeStruct((s, MLA_HEADS * HEAD_PAD), BF16), jax.ShapeDtypeStruct((MLA_HEADS, 8, s), F32)]
        + _comm_out_shapes("gather", gather),
        scratch_shapes=[pltpu.VMEM((ATT_G, t, HEAD_PAD), F32), pltpu.VMEM((ATT_G, t, HEAD_PAD), F32)]
        + (_comm_scratch(ng) if ng else []),
        compiler_params=_params(),
    )(q, k, v, *gather)
    return outs[0], outs[1], list(outs[2:])


def _ssd_consts():
    emisc = np.zeros((HEAD_PAD, SSD_INNER), np.float32)
    for hd in range(SSD_HEADS):
        emisc[MISC_DT + hd, hd * SSD_P:(hd + 1) * SSD_P] = 1.0
    idx = np.arange(CHUNK)
    tri = (idx[:, None] >= idx[None, :]).astype(np.float32)
    return tuple(jnp.asarray(m, BF16) for m in (emisc, emisc.T.copy(), tri, tri.T.copy()))


def _ssd_chunk_common(cc, misc, emisc, tri, trit, dtb, a_exp):
    xa = cc * jax.nn.sigmoid(cc)
    dtr = _dot01(misc, emisc) + dtb
    dt = jax.nn.softplus(dtr)
    a = dt * a_exp
    acs = _dot01(a, tri, left=True)
    acs_t = _dot01(a, trit, dot=_dot_tn)
    alast = acs[CHUNK - 1:CHUNK, :]
    return xa, dtr, dt, acs, acs_t, alast


def _decay(acs, acs_t, hd):
    row = lax.broadcasted_iota(jnp.int32, (CHUNK, CHUNK), 0)
    col = lax.broadcasted_iota(jnp.int32, (CHUNK, CHUNK), 1)
    diff = acs[:, hd * SSD_P:hd * SSD_P + 1] - acs_t[hd * SSD_P:hd * SSD_P + 1, :]
    return jnp.exp(jnp.where(row >= col, diff, -jnp.inf))


def _half_mask(hh):
    lane = lax.broadcasted_iota(jnp.int32, (CHUNK, 2 * SSD_P), 1)
    return (lane >= SSD_P) if hh else (lane < SSD_P)


def _gate_norm(y, zz, nw):
    yz = y * (zz * jax.nn.sigmoid(zz))
    outs, rs = [], []
    half = SSD_INNER // SSD_GROUPS
    for g in range(SSD_GROUPS):
        yg = yz[:, g * half:(g + 1) * half]
        r = lax.rsqrt(jnp.mean(yg * yg, axis=-1, keepdims=True) + EPS)
        outs.append(yg * r)
        rs.append(r)
    return yz, jnp.concatenate(outs, axis=1), rs


def _ssd_fwd(xraw, misc, z, cw, cb, dtb, a_exp, d_exp, nw, consts):
    s = xraw.shape[0]
    nb = s // SSD_ROWS
    ncb = SSD_ROWS // CHUNK
    emisc, _, tri, trit = consts

    def body(x_ref, misc_ref, z_ref, cw_ref, cb_ref, dtb_ref, a_ref, d_ref, nw_ref, emisc_ref, tri_ref, trit_ref,
             c_ref, prev_ref, ypre_ref, yssd_ref, tail_s, state_s):
        i = pl.program_id(0)

        @pl.when(i == 0)
        def _():
            tail_s[...] = jnp.zeros(tail_s.shape, F32)
            state_s[...] = jnp.zeros(state_s.shape, F32)

        x = x_ref[...]
        xext = jnp.concatenate([tail_s[...], x], axis=0)
        acc = x * cw_ref[CONV_W - 1:CONV_W, :] + cb_ref[...]
        for j in range(1, CONV_W):
            acc = acc + pltpu.roll(xext, j, 0)[8:, :] * cw_ref[CONV_W - 1 - j:CONV_W - j, :]
        tail_s[...] = x[SSD_ROWS - 8:, :]
        c_ref[...] = acc

        def chunk(ci, carry):
            r0 = pl.multiple_of(ci * CHUNK, CHUNK)
            xa, _, dt, acs, acs_t, alast = _ssd_chunk_common(
                c_ref[pl.ds(r0, CHUNK), :], misc_ref[pl.ds(r0, CHUNK), :], emisc_ref[...], tri_ref[...], trit_ref[...],
                dtb_ref[...], a_ref[...])
            xs = xa[:, :SSD_INNER]
            xdt = xs * dt
            prev = state_s[...]
            prev_ref[ci] = prev
            wgt = (xdt * jnp.exp(alast - acs)).astype(BF16)
            e = jnp.exp(acs)
            ys, new_states = [], []
            for g in range(SSD_GROUPS):
                bm = xa[:, SSD_INNER + g * SSD_N:SSD_INNER + (g + 1) * SSD_N].astype(BF16)
                cm = xa[:, SSD_INNER + SSD_GROUPS * SSD_N + g * SSD_N:SSD_INNER + SSD_GROUPS * SSD_N + (g + 1) * SSD_N].astype(BF16)
                cb_g = _dot_nt(cm, bm)
                gl = slice(g * 256, (g + 1) * 256)
                new_states.append(_dot_tn(bm, wgt[:, gl]))
                yoff = _dot(cm, prev[:, gl].astype(BF16)) * e[:, gl]
                for jj in range(2):
                    pair = 2 * g + jj
                    pl_ = slice(pair * 128, (pair + 1) * 128)
                    xp = xdt[:, pl_]
                    yp = yoff[:, jj * 128:(jj + 1) * 128]
                    for hh in range(2):
                        sc = (cb_g * _decay(acs, acs_t, 2 * pair + hh)).astype(BF16)
                        yp = yp + _dot(sc, jnp.where(_half_mask(hh), xp, 0.0).astype(BF16))
                    ys.append(yp)
            y = jnp.concatenate(ys, axis=1) + d_ref[...] * xs
            state_s[...] = prev * jnp.exp(alast) + jnp.concatenate(new_states, axis=1)
            ypre_ref[pl.ds(r0, CHUNK), :] = y
            _, yn, _ = _gate_norm(y, z_ref[pl.ds(r0, CHUNK), :], None)
            yssd_ref[pl.ds(r0, CHUNK), :] = (yn * nw_ref[...]).astype(BF16)
            return carry

        lax.fori_loop(0, ncb, chunk, 0, unroll=True)

    return pl.pallas_call(
        body, name="ssd_fwd", grid=(nb,),
        in_specs=[_rows(SSD_ROWS, CONV_DIM), _rows(SSD_ROWS, HEAD_PAD), _rows(SSD_ROWS, SSD_INNER),
                  _full((CONV_W, CONV_DIM)), _full((1, CONV_DIM)), _full((1, SSD_INNER)), _full((1, SSD_INNER)),
                  _full((1, SSD_INNER)), _full((1, SSD_INNER)), _full((HEAD_PAD, SSD_INNER)), _full((CHUNK, CHUNK)),
                  _full((CHUNK, CHUNK))],
        out_specs=[_rows(SSD_ROWS, CONV_DIM), pl.BlockSpec((ncb, SSD_N, SSD_INNER), lambda i: (i, 0, 0)),
                   _rows(SSD_ROWS, SSD_INNER), _rows(SSD_ROWS, SSD_INNER)],
        out_shape=[jax.ShapeDtypeStruct((s, CONV_DIM), F32), jax.ShapeDtypeStruct((s // CHUNK, SSD_N, SSD_INNER), F32),
                   jax.ShapeDtypeStruct((s, SSD_INNER), F32), jax.ShapeDtypeStruct((s, SSD_INNER), BF16)],
        scratch_shapes=[pltpu.VMEM((8, CONV_DIM), F32), pltpu.VMEM((SSD_N, SSD_INNER), F32)],
        compiler_params=_params(),
    )(xraw, misc, z, cw, cb, dtb, a_exp, d_exp, nw, emisc, tri, trit)


def _outproj_fwd(oe, yssd, wout, h, nw):
    s = h.shape[0]
    wide = MLA_HEADS * HEAD_PAD

    def body(oe_ref, y_ref, w_ref, h_ref, nw_ref, mixed_ref, h1_ref):
        mixed = _dot(oe_ref[...], w_ref[0:wide, :]) + _dot(y_ref[...], w_ref[wide:, :])
        mixed_ref[...] = mixed
        h1_ref[...] = h_ref[...] + _rms_fwd(mixed, nw_ref[...])

    return pl.pallas_call(
        body, name="outproj_fwd", grid=(s // TM,),
        in_specs=[_rows(TM, wide), _rows(TM, SSD_INNER), _resident((wide + SSD_INNER, D_MODEL)), _rows(TM, D_MODEL),
                  _full((1, D_MODEL))],
        out_specs=[_rows(TM, D_MODEL), _rows(TM, D_MODEL)],
        out_shape=[jax.ShapeDtypeStruct((s, D_MODEL), F32)] * 2,
        compiler_params=_params(),
    )(oe, yssd, wout, h, nw)


def _mlp_fwd(h1, prew, wup, wdown, postw, target=None):
    s = h1.shape[0]
    fb = D_FF // N_DEV
    last = target is not None

    def body(h_ref, prew_ref, up_ref, down_ref, postw_ref, *rest):
        hh = h_ref[...]
        mb = _rms_fwd(hh, prew_ref[...]).astype(BF16)
        rest[-3 - last][...] = mb
        d = jnp.zeros((TM, D_MODEL), F32)
        for j in range(N_DEV):
            a = _dot(mb, up_ref[j])
            r = jnp.square(jnp.maximum(a, 0.0)).astype(BF16)
            d = d + _dot(r, down_ref[j])
        rest[-2 - last][...] = d
        h2 = hh + _rms_fwd(d, postw_ref[...])
        if last:
            diff = h2 - rest[0][...]
            rest[-2][...] = diff * (1.0 / D_MODEL)
            part = 0.5 * jnp.sum(jnp.mean(diff * diff, axis=-1, keepdims=True), axis=0, keepdims=True)
            _acc_rows(rest[-1], part, pl.program_id(0) == 0)
        else:
            rest[-1][...] = h2

    return pl.pallas_call(
        body, name="mlp_fwd_loss" if last else "mlp_fwd", grid=(s // TM,),
        in_specs=[_rows(TM, D_MODEL), _full((1, D_MODEL)), _resident((N_DEV, D_MODEL, fb)), _resident((N_DEV, fb, D_MODEL)),
                  _full((1, D_MODEL))] + ([_rows(TM, D_MODEL)] if last else []),
        out_specs=[_rows(TM, D_MODEL)] * 3 + ([_full((1, 1))] if last else []),
        out_shape=[jax.ShapeDtypeStruct((s, D_MODEL), BF16), jax.ShapeDtypeStruct((s, D_MODEL), F32),
                   jax.ShapeDtypeStruct((s, D_MODEL), F32)] + ([jax.ShapeDtypeStruct((1, 1), F32)] if last else []),
        compiler_params=_params(),
    )(h1, prew, wup, wdown, postw, *([target] if last else []))


def _mlp_bwd(dh2, d, h1, mb, prew, wup, wdown, postw):
    s = dh2.shape[0]
    fb = D_FF // N_DEV
    tm = TM // 2

    def body(dh2_ref, d_ref, h1_ref, mb_ref, prew_ref, up_ref, down_ref, postw_ref,
             dh1_ref, da_ref, r_ref, dd_ref, gpost_ref, gpre_ref):
        first = pl.program_id(0) == 0
        dh2 = dh2_ref[...]
        dd, gpost = _rms_bwd(d_ref[...], postw_ref[...], dh2)
        _acc_rows(gpost_ref, gpost, first)
        ddb = dd.astype(BF16)
        dd_ref[...] = ddb
        mb = mb_ref[...]
        dm = jnp.zeros((tm, D_MODEL), F32)
        for j in range(N_DEV):
            a = jnp.maximum(_dot(mb, up_ref[j]), 0.0)
            r_ref[j] = jnp.square(a).astype(BF16)
            da = (_dot_nt(ddb, down_ref[j]) * (2.0 * a)).astype(BF16)
            da_ref[j] = da
            dm = dm + _dot_nt(da, up_ref[j])
        dx, gpre = _rms_bwd(h1_ref[...], prew_ref[...], dm)
        _acc_rows(gpre_ref, gpre, first)
        dh1_ref[...] = dh2 + dx

    stacked = pl.BlockSpec((N_DEV, tm, fb), lambda i: (0, i, 0))
    return pl.pallas_call(
        body, name="mlp_bwd", grid=(s // tm,),
        in_specs=[_rows(tm, D_MODEL)] * 4 + [_full((1, D_MODEL)), _resident((N_DEV, D_MODEL, fb)), _resident((N_DEV, fb, D_MODEL)),
                                              _full((1, D_MODEL))],
        out_specs=[_rows(tm, D_MODEL), stacked, stacked, _rows(tm, D_MODEL), _full((1, D_MODEL)), _full((1, D_MODEL))],
        out_shape=[jax.ShapeDtypeStruct((s, D_MODEL), F32), jax.ShapeDtypeStruct((N_DEV, s, fb), BF16),
                   jax.ShapeDtypeStruct((N_DEV, s, fb), BF16), jax.ShapeDtypeStruct((s, D_MODEL), BF16),
                   jax.ShapeDtypeStruct((1, D_MODEL), F32), jax.ShapeDtypeStruct((1, D_MODEL), F32)],
        compiler_params=_params(),
    )(dh2, d, h1, mb, prew, wup, wdown, postw)


def _matmul_tn(a, b, name, tk=TK_DW):
    s, m = a.shape
    n = b.shape[1]
    tn = n if n <= 1024 else (n // 2 if (n // 2) % 128 == 0 else n // 3)
    tk = min(tk, s)
    assert n % tn == 0 and tn % 128 == 0 and s % tk == 0

    def body(a_ref, b_ref, o_ref):
        part = _dot_tn(a_ref[...], b_ref[...])

        @pl.when(pl.program_id(1) == 0)
        def _():
            o_ref[...] = part

        @pl.when(pl.program_id(1) != 0)
        def _():
            o_ref[...] += part

    return pl.pallas_call(
        body, name=name, grid=(n // tn, s // tk),
        in_specs=[pl.BlockSpec((tk, m), lambda j, k: (k, 0)), pl.BlockSpec((tk, tn), lambda j, k: (k, j))],
        out_specs=pl.BlockSpec((m, tn), lambda j, k: (0, j)),
        out_shape=jax.ShapeDtypeStruct((m, n), F32),
        compiler_params=_params(),
    )(a, b)


def _matmul_tn_stacked(a, b, name, a_stacked, tk=TK_DW):
    tk = min(tk, a.shape[-2])
    if a_stacked:
        _, s, m = a.shape
        n = b.shape[1]
        in_specs = [pl.BlockSpec((1, tk, m), lambda j, k: (j, k, 0)), pl.BlockSpec((tk, n), lambda j, k: (k, 0))]
    else:
        s, m = a.shape
        n = b.shape[2]
        in_specs = [pl.BlockSpec((tk, m), lambda j, k: (k, 0)), pl.BlockSpec((1, tk, n), lambda j, k: (j, k, 0))]

    nk = s // tk

    def body(a_ref, b_ref, o_ref, acc_s):
        av = a_ref[0] if a_stacked else a_ref[...]
        bv = b_ref[...] if a_stacked else b_ref[0]
        part = _dot_tn(av, bv)
        k = pl.program_id(1)

        @pl.when(k == 0)
        def _():
            acc_s[...] = part

        @pl.when(jnp.logical_and(k != 0, k != nk - 1))
        def _():
            acc_s[...] += part

        @pl.when(k == nk - 1)
        def _():
            o_ref[0] = (part if nk == 1 else acc_s[...] + part).astype(BF16)

    return pl.pallas_call(
        body, name=name, grid=(N_DEV, nk),
        in_specs=in_specs,
        out_specs=pl.BlockSpec((1, m, n), lambda j, k: (j, 0, 0)),
        out_shape=jax.ShapeDtypeStruct((N_DEV, m, n), BF16),
        scratch_shapes=[pltpu.VMEM((m, n), F32)],
        compiler_params=_params(),
    )(a, b)


def _outproj_bwd(dh1, mixed, nw, wout, oe):
    s = dh1.shape[0]
    wide = MLA_HEADS * HEAD_PAD

    def body(dh1_ref, mixed_ref, nw_ref, w_ref, oe_ref, dmix_ref, doe_ref, dy_ref, gnw_ref, delta_ref):
        dmix, gnw = _rms_bwd(mixed_ref[...], nw_ref[...], dh1_ref[...])
        _acc_rows(gnw_ref, gnw, pl.program_id(0) == 0)
        dmb = dmix.astype(BF16)
        dmix_ref[...] = dmb
        doe_ref[...] = _dot_nt(dmb, w_ref[0:wide, :]).astype(BF16)
        dy_ref[...] = _dot_nt(dmb, w_ref[wide:, :])
        ones = jnp.ones((8, HEAD_PAD), BF16)
        for hd in range(MLA_HEADS):
            cols = slice(hd * HEAD_PAD, (hd + 1) * HEAD_PAD)
            prod = oe_ref[:, cols].astype(F32) * doe_ref[:, cols].astype(F32)
            delta_ref[hd] = _dot01(prod, ones, dot=_dot_nt, left=True)

    return pl.pallas_call(
        body, name="outproj_bwd", grid=(s // TM,),
        in_specs=[_rows(TM, D_MODEL), _rows(TM, D_MODEL), _full((1, D_MODEL)), _resident((wide + SSD_INNER, D_MODEL)),
                  _rows(TM, wide)],
        out_specs=[_rows(TM, D_MODEL), _rows(TM, wide), _rows(TM, SSD_INNER), _full((1, D_MODEL)),
                   pl.BlockSpec((MLA_HEADS, 8, TM), lambda i: (0, 0, i))],
        out_shape=[jax.ShapeDtypeStruct((s, D_MODEL), BF16), jax.ShapeDtypeStruct((s, wide), BF16),
                   jax.ShapeDtypeStruct((s, SSD_INNER), F32), jax.ShapeDtypeStruct((1, D_MODEL), F32),
                   jax.ShapeDtypeStruct((MLA_HEADS, 8, s), F32)],
        compiler_params=_params(),
    )(dh1, mixed, nw, wout, oe)


def _attn_bwd(q, k, v, do, lse, delta, exchange=()):
    s = q.shape[0]
    t = ATT_T
    nq = s // t
    pair = 2 * HEAD_PAD
    ne = len(exchange)

    def body(q_ref, k_ref, v_ref, do_ref, lse_ref, delta_ref, *rest):
        e_in, (dq_ref, dk_ref, dv_ref), e_out = rest[:ne], rest[ne:ne + 3], rest[ne + 3:2 * ne + 3]
        kb = pl.program_id(1)
        _hosted_comm("exchange", e_in, e_out, rest[2 * ne + 3:],
                     jnp.logical_and(pl.program_id(0) == 0, kb == 0),
                     jnp.logical_and(pl.program_id(0) == MLA_HEADS // 2 - 1, kb == nq - 1))

        @pl.when(kb == 0)
        def _():
            dq_ref[...] = jnp.zeros(dq_ref.shape, F32)

        dk_ref[...] = jnp.zeros(dk_ref.shape, F32)
        dv_ref[...] = jnp.zeros(dv_ref.shape, F32)

        def step(qb, masked):
            r0 = pl.multiple_of(qb * t, t)
            for hh in range(2):
                cols = slice(hh * HEAD_PAD, (hh + 1) * HEAD_PAD)
                kk = k_ref[:, cols]
                qq = q_ref[pl.ds(r0, t), cols]
                dd = do_ref[pl.ds(r0, t), cols]
                sc = _dot_nt(kk, qq) * ATT_SCALE_LOG2
                if masked:
                    sc = jnp.where(_chunk_mask(t, keys_on_rows=True), sc, -jnp.inf)
                p = jnp.exp2(sc - lse_ref[hh, 0:1, pl.ds(r0, t)])
                dv_ref[:, cols] += _dot(p.astype(BF16), dd)
                dp = _dot_nt(v_ref[:, cols], dd)
                ds = (p * (dp - delta_ref[hh, 0:1, pl.ds(r0, t)]) * ATT_SCALE).astype(BF16)
                dk_ref[:, cols] += _dot(ds, qq)
                dq_ref[pl.ds(r0, t), cols] += _dot_tn(ds, kk)

        def loop(qb, c):
            step(qb, False)
            return c

        step(kb, True)
        lax.fori_loop(kb + 1, nq, loop, 0)

    whole = pl.BlockSpec((s, pair), lambda h, i: (0, h))
    tile = pl.BlockSpec((t, pair), lambda h, i: (i, h))
    rowvec = pl.BlockSpec((2, 8, s), lambda h, i: (h, 0, 0))
    wide = MLA_HEADS * HEAD_PAD
    outs = pl.pallas_call(
        body, name="attn_bwd_exchange" if ne else "attn_bwd", grid=(MLA_HEADS // 2, nq),
        in_specs=[whole, tile, tile, whole, rowvec, rowvec] + [_ANY] * ne,
        out_specs=[whole, tile, tile] + [_ANY] * ne,
        out_shape=[jax.ShapeDtypeStruct((s, wide), F32)] * 3 + _comm_out_shapes("exchange", exchange),
        scratch_shapes=_comm_scratch(ne) if ne else [],
        compiler_params=_params(),
    )(q, k, v, do, lse, delta, *exchange)
    return outs[0], outs[1], outs[2], list(outs[3:])


def _ssd_bwd(dy, ypre, z, c, xraw, misc, prev, cw, dtb, a_exp, d_exp, nw, consts):
    s = dy.shape[0]
    nb = s // SSD_ROWS
    ncb = SSD_ROWS // CHUNK
    emisc, emisc_t, tri, trit = consts

    def body(dy_ref, ypre_ref, z_ref, c_ref, x_ref, xprev_ref, misc_ref, prev_ref, cw_ref, dtb_ref, a_ref, d_ref, nw_ref,
             emisc_ref, emisct_ref, tri_ref, trit_ref,
             dz_ref, dx_ref, dmisc_ref, gnw_ref, gd_ref, galog_ref, gdtb_ref, gcw_ref, gcb_ref,
             dst_s, dc_s, head_s):
        i = pl.program_id(0)
        first = i == 0

        @pl.when(first)
        def _():
            dst_s[...] = jnp.zeros(dst_s.shape, F32)
            head_s[...] = jnp.zeros(head_s.shape, F32)
            gnw_ref[...] = jnp.zeros(gnw_ref.shape, F32)
            gd_ref[...] = jnp.zeros(gd_ref.shape, F32)
            galog_ref[...] = jnp.zeros(galog_ref.shape, F32)
            gdtb_ref[...] = jnp.zeros(gdtb_ref.shape, F32)

        a_exp_v = a_ref[...]
        a8 = _dot01(a_exp_v, emisct_ref[...]) * (1.0 / SSD_P)

        def chunk(cr, carry):
            ci = ncb - 1 - cr
            r0 = pl.multiple_of(ci * CHUNK, CHUNK)
            cc = c_ref[pl.ds(r0, CHUNK), :]
            mm = misc_ref[pl.ds(r0, CHUNK), :]
            xa, dtr, dt, acs, acs_t, alast = _ssd_chunk_common(cc, mm, emisc_ref[...], tri_ref[...], trit_ref[...],
                                                              dtb_ref[...], a_exp_v)
            xs = xa[:, :SSD_INNER]
            xdt = xs * dt
            y = ypre_ref[pl.ds(r0, CHUNK), :]
            zz = z_ref[pl.ds(r0, CHUNK), :]
            yz, yn, rs = _gate_norm(y, zz, None)
            dyo = dy_ref[pl.ds(r0, CHUNK), :]
            gnw_ref[...] += jnp.sum(dyo * yn, axis=0, keepdims=True)
            dyn = dyo * nw_ref[...]
            half = SSD_INNER // SSD_GROUPS
            dyz_parts = []
            for g in range(SSD_GROUPS):
                gl = slice(g * half, (g + 1) * half)
                dyz_parts.append(rs[g] * (dyn[:, gl] - yn[:, gl] * jnp.mean(dyn[:, gl] * yn[:, gl], axis=-1, keepdims=True)))
            dyz = jnp.concatenate(dyz_parts, axis=1)
            sg = jax.nn.sigmoid(zz)
            dz_ref[pl.ds(r0, CHUNK), :] = dyz * y * (sg * (1.0 + zz * (1.0 - sg)))
            dyp = dyz * (zz * sg)
            dypb = dyp.astype(BF16)
            gd_ref[...] += jnp.sum(dyp * xs, axis=0, keepdims=True)
            prev = prev_ref[ci]
            dst = dst_s[...]
            cd = jnp.exp(alast)
            e = jnp.exp(acs)
            dsx = jnp.exp(alast - acs)
            wgt = (xdt * dsx).astype(BF16)
            dze = (dyp * e).astype(BF16)
            glast = jnp.sum(dst * prev, axis=0, keepdims=True) * cd
            dprev_parts, dxdt_parts, dxdt_state_parts, dbm, dcm, yoff_parts = [], [], [], [], [], []
            lane8 = lax.broadcasted_iota(jnp.int32, (CHUNK, HEAD_PAD), 1)
            diag8 = jnp.zeros((CHUNK, HEAD_PAD), F32)
            for g in range(SSD_GROUPS):
                gl = slice(g * 256, (g + 1) * 256)
                bm = xa[:, SSD_INNER + g * SSD_N:SSD_INNER + (g + 1) * SSD_N].astype(BF16)
                cm = xa[:, SSD_INNER + SSD_GROUPS * SSD_N + g * SSD_N:SSD_INNER + SSD_GROUPS * SSD_N + (g + 1) * SSD_N].astype(BF16)
                prev_g = prev[:, gl].astype(BF16)
                dst_g = dst[:, gl].astype(BF16)
                dcm_g = _dot_nt(dze[:, gl], prev_g)
                dprev_parts.append(_dot_tn(cm, dze[:, gl]))
                dxs_state = _dot(bm, dst_g) * dsx[:, gl]
                dbm_g = _dot_nt(wgt[:, gl], dst_g)
                cb_g = _dot_nt(cm, bm)
                dcb = jnp.zeros((CHUNK, CHUNK), F32)
                diag_parts = []
                for jj in range(2):
                    pair = 2 * g + jj
                    pl_ = slice(pair * 128, (pair + 1) * 128)
                    xp = xdt[:, pl_]
                    dyp_p = dypb[:, pl_]
                    dxp = jnp.zeros((CHUNK, 128), F32)
                    for hh in range(2):
                        hd = 2 * pair + hh
                        dec = _decay(acs, acs_t, hd)
                        xm = jnp.where(_half_mask(hh), xp, 0.0).astype(BF16)
                        dsc = _dot_nt(dyp_p, xm) * dec
                        dcb = dcb + dsc
                        sc = (cb_g * dec).astype(BF16)
                        dxp = dxp + jnp.where(_half_mask(hh), _dot_tn(sc, dyp_p), 0.0)
                        dm = dsc * cb_g
                        diag8 = diag8 + jnp.where(lane8 == MISC_DT + hd, jnp.sum(dm - dm.T, axis=1, keepdims=True), 0.0)
                    diag_parts.append(dxp)
                dcbb = dcb.astype(BF16)
                dcm.append(dcm_g + _dot(dcbb, bm))
                dbm.append(dbm_g + _dot_tn(dcbb, cm))
                dxdt_state_parts.append(dxs_state)
                dxdt_parts.append(jnp.concatenate(diag_parts, axis=1) + dxs_state)
                yoff_parts.append(_dot(cm, prev_g) * e[:, gl])
            dxdt = jnp.concatenate(dxdt_parts, axis=1)
            dxdt_state = jnp.concatenate(dxdt_state_parts, axis=1)
            dst_s[...] = dst * cd + jnp.concatenate(dprev_parts, axis=1)
            dacs = dyp * jnp.concatenate(yoff_parts, axis=1) - xdt * dxdt_state
            last = jnp.sum(xdt * dxdt_state, axis=0, keepdims=True) + glast
            row = lax.broadcasted_iota(jnp.int32, (CHUNK, SSD_INNER), 0)
            dacs = dacs + jnp.where(row == CHUNK - 1, last, 0.0)
            dacs8 = _dot01(dacs, emisct_ref[...]) + diag8
            da8 = _dot01(dacs8, trit_ref[...], left=True)
            ddt8 = da8 * a8 + _dot01(dxdt * xs, emisct_ref[...])
            dtr8 = mm + _dot01(dtb_ref[...], emisct_ref[...]) * (1.0 / SSD_P)
            dt8 = jax.nn.softplus(dtr8)
            lane = lax.broadcasted_iota(jnp.int32, (CHUNK, HEAD_PAD), 1)
            on_dt = jnp.logical_and(lane >= MISC_DT, lane < MISC_DT + SSD_HEADS)
            ddtr8 = jnp.where(on_dt, ddt8 * jax.nn.sigmoid(dtr8), 0.0)
            dmisc_ref[pl.ds(r0, CHUNK), :] = ddtr8
            gdtb_ref[...] += jnp.sum(ddtr8, axis=0, keepdims=True)
            galog_ref[...] += jnp.sum(jnp.where(on_dt, da8 * dt8, 0.0), axis=0, keepdims=True) * a8
            dxs = d_ref[...] * dyp + dxdt * dt
            dxa = jnp.concatenate([dxs] + dbm + dcm, axis=1)
            sc_ = jax.nn.sigmoid(cc)
            dc_s[pl.ds(r0, CHUNK), :] = dxa * (sc_ * (1.0 + cc * (1.0 - sc_)))
            return carry

        lax.fori_loop(0, ncb, chunk, 0, unroll=True)

        dc = dc_s[...]
        dcext = jnp.concatenate([dc, head_s[...]], axis=0)
        dx = dc * cw_ref[CONV_W - 1:CONV_W, :]
        for j in range(1, CONV_W):
            dx = dx + pltpu.roll(dcext, SSD_ROWS + 8 - j, 0)[:SSD_ROWS, :] * cw_ref[CONV_W - 1 - j:CONV_W - j, :]
        dx_ref[...] = dx
        head_s[...] = dc[:8, :]
        xprev = jnp.where(i == nb - 1, 0.0, xprev_ref[...])
        xext = jnp.concatenate([xprev, x_ref[...]], axis=0)
        rows = [jnp.sum(dc * pltpu.roll(xext, CONV_W - 1 - kk, 0)[8:, :], axis=0, keepdims=True) for kk in range(CONV_W)]
        gcw = jnp.concatenate(rows, axis=0)

        @pl.when(first)
        def _():
            gcw_ref[...] = gcw
            gcb_ref[...] = jnp.sum(dc, axis=0, keepdims=True)

        @pl.when(jnp.logical_not(first))
        def _():
            gcw_ref[...] += gcw
            gcb_ref[...] += jnp.sum(dc, axis=0, keepdims=True)

    def rev(width):
        return pl.BlockSpec((SSD_ROWS, width), lambda i: (nb - 1 - i, 0))

    per8 = SSD_ROWS // 8
    return pl.pallas_call(
        body, name="ssd_bwd", grid=(nb,),
        in_specs=[rev(SSD_INNER), rev(SSD_INNER), rev(SSD_INNER), rev(CONV_DIM), rev(CONV_DIM),
                  pl.BlockSpec((8, CONV_DIM), lambda i: (jnp.maximum((nb - 1 - i) * per8 - 1, 0), 0)),
                  rev(HEAD_PAD), pl.BlockSpec((ncb, SSD_N, SSD_INNER), lambda i: (nb - 1 - i, 0, 0)),
                  _full((CONV_W, CONV_DIM)), _full((1, SSD_INNER)), _full((1, SSD_INNER)), _full((1, SSD_INNER)),
                  _full((1, SSD_INNER)), _full((HEAD_PAD, SSD_INNER)), _full((SSD_INNER, HEAD_PAD)), _full((CHUNK, CHUNK)),
                  _full((CHUNK, CHUNK))],
        out_specs=[rev(SSD_INNER), rev(CONV_DIM), rev(HEAD_PAD), _full((1, SSD_INNER)), _full((1, SSD_INNER)),
                   _full((1, HEAD_PAD)), _full((1, HEAD_PAD)), _full((CONV_W, CONV_DIM)), _full((1, CONV_DIM))],
        out_shape=[jax.ShapeDtypeStruct((s, SSD_INNER), F32), jax.ShapeDtypeStruct((s, CONV_DIM), F32),
                   jax.ShapeDtypeStruct((s, HEAD_PAD), F32), jax.ShapeDtypeStruct((1, SSD_INNER), F32),
                   jax.ShapeDtypeStruct((1, SSD_INNER), F32), jax.ShapeDtypeStruct((1, HEAD_PAD), F32),
                   jax.ShapeDtypeStruct((1, HEAD_PAD), F32), jax.ShapeDtypeStruct((CONV_W, CONV_DIM), F32),
                   jax.ShapeDtypeStruct((1, CONV_DIM), F32)],
        scratch_shapes=[pltpu.VMEM((SSD_N, SSD_INNER), F32), pltpu.VMEM((SSD_ROWS, CONV_DIM), F32), pltpu.VMEM((8, CONV_DIM), F32)],
        compiler_params=_params(),
    )(dy, ypre, z, c, xraw, xraw, misc, prev, cw, dtb, a_exp, d_exp, nw, emisc, emisc_t, tri, trit)


def _qkv_bwd(dq, dk, dv, cq, ckv, qnw, kvnw, wuq, wkv, cosf, sinf):
    s = dq.shape[0]
    wide = MLA_HEADS * HEAD_PAD

    def body(dq_ref, dk_ref, dv_ref, cq_ref, ckv_ref, qnw_ref, kvnw_ref, wuq_ref, wkv_ref, cos_ref, sin_ref,
             dqb_ref, dkvb_ref, dcq_ref, dckv_ref, dmisc_ref, gq_ref, gkv_ref):
        first = pl.program_id(0) == 0
        cosf, sinf = cos_ref[...], sin_ref[...]
        dkr = jnp.zeros((TM, HEAD_PAD), F32)
        for hd in range(MLA_HEADS):
            cols = slice(hd * HEAD_PAD, (hd + 1) * HEAD_PAD)
            dqb_ref[:, cols] = _rope(dq_ref[:, cols], cosf, sinf, -1.0).astype(BF16)
            dkh = dk_ref[:, cols]
            dkvb_ref[:, cols] = dkh.astype(BF16)
            dkr = dkr + dkh
        dkvb_ref[:, wide:] = dv_ref[...].astype(BF16)
        lane = lax.broadcasted_iota(jnp.int32, dkr.shape, 1)
        in_rope = jnp.logical_and(lane >= MISC_ROPE, lane < MISC_ROPE + QK_ROPE)
        dmisc_ref[...] = jnp.where(in_rope, _rope(jnp.where(in_rope, dkr, 0.0), cosf, sinf, -1.0), 0.0)
        dcq, gq = _rms_bwd(cq_ref[...], qnw_ref[...], _dot_nt(dqb_ref[...], wuq_ref[...]))
        dcq_ref[...] = dcq
        _acc_rows(gq_ref, gq, first)
        dckv, gkv = _rms_bwd(ckv_ref[...], kvnw_ref[...], _dot_nt(dkvb_ref[...], wkv_ref[...]))
        dckv_ref[...] = dckv
        _acc_rows(gkv_ref, gkv, first)

    return pl.pallas_call(
        body, name="qkv_bwd", grid=(s // TM,),
        in_specs=[_rows(TM, wide)] * 3 + [_rows(TM, Q_RANK), _rows(TM, KV_RANK), _full((1, Q_RANK)), _full((1, KV_RANK)),
                                          _resident((Q_RANK, wide)), _resident((KV_RANK, 2 * wide)), _rows(TM, HEAD_PAD), _rows(TM, HEAD_PAD)],
        out_specs=[_rows(TM, wide), _rows(TM, 2 * wide), _rows(TM, Q_RANK), _rows(TM, KV_RANK), _rows(TM, HEAD_PAD),
                   _full((1, Q_RANK)), _full((1, KV_RANK))],
        out_shape=[jax.ShapeDtypeStruct((s, wide), BF16), jax.ShapeDtypeStruct((s, 2 * wide), BF16),
                   jax.ShapeDtypeStruct((s, Q_RANK), F32), jax.ShapeDtypeStruct((s, KV_RANK), F32),
                   jax.ShapeDtypeStruct((s, HEAD_PAD), F32), jax.ShapeDtypeStruct((1, Q_RANK), F32),
                   jax.ShapeDtypeStruct((1, KV_RANK), F32)],
        compiler_params=_params(),
    )(dq, dk, dv, cq, ckv, qnw, kvnw, wuq, wkv, cosf, sinf)


def _inproj_bwd(dcq, dckv, dmisc_rope, dmisc_dt, dz, dxbc, h, dh1, nw, win):
    s = h.shape[0]

    def body(dcq_ref, dckv_ref, dma_ref, dmb_ref, dz_ref, dxbc_ref, h_ref, dh1_ref, nw_ref, w_ref, dproj_ref, dh0_ref, gnw_ref):
        dproj_ref[:, 0:768] = dcq_ref[...].astype(BF16)
        dproj_ref[:, 768:1024] = dckv_ref[...].astype(BF16)
        dproj_ref[:, 1024:1152] = (dma_ref[...] + dmb_ref[...]).astype(BF16)
        dproj_ref[:, 1152:1664] = dz_ref[...].astype(BF16)
        dproj_ref[:, 1664:2688] = dxbc_ref[...].astype(BF16)
        du = _dot_nt(dproj_ref[...], w_ref[...])
        dx, gnw = _rms_bwd(h_ref[...], nw_ref[...], du)
        _acc_rows(gnw_ref, gnw, pl.program_id(0) == 0)
        dh0_ref[...] = dh1_ref[...] + dx

    return pl.pallas_call(
        body, name="inproj_bwd", grid=(s // TM,),
        in_specs=[_rows(TM, Q_RANK), _rows(TM, KV_RANK), _rows(TM, HEAD_PAD), _rows(TM, HEAD_PAD), _rows(TM, SSD_INNER),
                  _rows(TM, CONV_DIM), _rows(TM, D_MODEL), _rows(TM, D_MODEL), _full((1, D_MODEL)), _resident((D_MODEL, IN_PAD))],
        out_specs=[_rows(TM, IN_PAD), _rows(TM, D_MODEL), _full((1, D_MODEL))],
        out_shape=[jax.ShapeDtypeStruct((s, IN_PAD), BF16), jax.ShapeDtypeStruct((s, D_MODEL), F32),
                   jax.ShapeDtypeStruct((1, D_MODEL), F32)],
        compiler_params=_params(),
    )(dcq, dckv, dmisc_rope, dmisc_dt, dz, dxbc, h, dh1, nw, win)


def _row_tile(rows, cols):
    cap = max(8, (1 << 18) // max(cols, 128))
    best = None
    for t in range(8, rows + 1, 8):
        if rows % t == 0 and t <= cap:
            best = t
    return best if best is not None else rows


def _adamw(w, g, m, v, name):
    rows, cols = w.shape
    tr = _row_tile(rows, cols)

    def body(w_ref, g_ref, m_ref, v_ref, d_ref, m2_ref, v2_ref):
        gg = g_ref[...]
        m2 = ADAM_B1 * m_ref[...] + (1.0 - ADAM_B1) * gg
        v2 = ADAM_B2 * v_ref[...] + (1.0 - ADAM_B2) * jnp.square(gg)
        m_hat = m2 / (1.0 - ADAM_B1 ** ADAM_STEP)
        v_hat = v2 / (1.0 - ADAM_B2 ** ADAM_STEP)
        d_ref[...] = -ADAM_LR * (m_hat / (jnp.sqrt(v_hat) + ADAM_EPS) + ADAM_WD * w_ref[...])
        m2_ref[...] = m2
        v2_ref[...] = v2

    spec = pl.BlockSpec((tr, cols), lambda i: (i, 0))
    return pl.pallas_call(
        body, name=name, grid=(rows // tr,),
        in_specs=[spec] * 4, out_specs=[spec] * 3,
        out_shape=[jax.ShapeDtypeStruct((rows, cols), F32)] * 3,
    )(w, g, m, v)


def _sum_adamw(slots, w, m, v, name):
    _, rows, cols = w.shape
    tr = _row_tile(rows, cols)
    nb = rows // tr

    def body(s0_ref, s1_ref, w_ref, m_ref, v_ref, g_ref, d_ref, m2_ref, v2_ref):
        for l, ref in enumerate((s0_ref, s1_ref)):
            @pl.when(pl.program_id(0) == l)
            def _(ref=ref):
                acc = ref[0].astype(F32)
                for i in range(1, N_DEV):
                    acc = acc + ref[i].astype(F32)
                g_ref[...] = acc

        gg = g_ref[...]
        m2 = ADAM_B1 * m_ref[...] + (1.0 - ADAM_B1) * gg
        v2 = ADAM_B2 * v_ref[...] + (1.0 - ADAM_B2) * jnp.square(gg)
        m_hat = m2 / (1.0 - ADAM_B1 ** ADAM_STEP)
        v_hat = v2 / (1.0 - ADAM_B2 ** ADAM_STEP)
        d_ref[...] = -ADAM_LR * (m_hat / (jnp.sqrt(v_hat) + ADAM_EPS) + ADAM_WD * w_ref[...])
        m2_ref[...] = m2
        v2_ref[...] = v2

    slot_spec = lambda layer: pl.BlockSpec((N_DEV, tr, cols), lambda l, i: (0, jnp.where(l == layer, i, (nb - 1) * (1 - layer)), 0))
    spec = pl.BlockSpec((None, tr, cols), lambda l, i: (l, i, 0))
    return pl.pallas_call(
        body, name=name, grid=(DEPTH, nb),
        in_specs=[slot_spec(0), slot_spec(1), spec, spec, spec], out_specs=[spec] * 4,
        out_shape=[jax.ShapeDtypeStruct(w.shape, F32)] * 4,
        compiler_params=_params(),
    )(slots[0], slots[1], w, m, v)


_MESH = pl.DeviceIdType.MESH
_ANY = pl.BlockSpec(memory_space=pl.ANY)


def _my_place():
    return lax.axis_index("x"), lax.axis_index("y"), lax.axis_index("c")


def _flip(place, k):
    x, y, c = place
    return (1 - x if k & 4 else x, 1 - y if k & 2 else y, 1 - c if k & 1 else c)


def _block_id(place):
    return 4 * place[0] + 2 * place[1] + place[2]


def _peer_copies(kind, in_refs, out_refs, send_sems, recv_sems, local_sems):
    me = _my_place()
    my = _block_id(me)
    remote, local = [], []
    for a, (x_ref, out_ref) in enumerate(zip(in_refs, out_refs)):
        src_of = (lambda place, r=x_ref: r) if kind == "gather" else (lambda place, r=x_ref: r.at[_block_id(place)])
        local.append(pltpu.make_async_copy(src_of(me), out_ref.at[my], local_sems.at[a]))
        for k in range(1, N_DEV):
            peer = _flip(me, k)
            remote.append(pltpu.make_async_remote_copy(
                src_ref=src_of(peer), dst_ref=out_ref.at[my], send_sem=send_sems.at[a * 7 + k - 1],
                recv_sem=recv_sems.at[a * 7 + k - 1], device_id=peer, device_id_type=_MESH))
    return remote, local


def _comm_out_shapes(kind, arrays):
    return [jax.ShapeDtypeStruct((N_DEV, *a.shape) if kind == "gather" else a.shape, a.dtype) for a in arrays]


def _comm_scratch(n):
    return [pltpu.SemaphoreType.DMA((7 * n,)), pltpu.SemaphoreType.DMA((7 * n,)), pltpu.SemaphoreType.DMA((n,))]


def _hosted_comm(kind, in_refs, out_refs, sems, first, last):
    if not in_refs:
        return

    @pl.when(first)
    def _():
        remote, local = _peer_copies(kind, in_refs, out_refs, *sems)
        for cp in local + remote:
            cp.start()

    @pl.when(last)
    def _():
        remote, local = _peer_copies(kind, in_refs, out_refs, *sems)
        for cp in remote:
            cp.wait()
        for cp in local:
            cp.wait()


def _gather_two_level(arrays, name):
    n = len(arrays)

    def body(*refs):
        in_refs, out_refs = refs[:n], refs[n:2 * n]
        send_sems, recv_sems, local_sems = refs[2 * n:]
        me = _my_place()
        x, y, c = me
        sibling = (x, y, 1 - c)
        chips = [(1 - x, y), (x, 1 - y), (1 - x, 1 - y)]

        def copy(a, k, place, to, src=None):
            block = out_refs[a].at[_block_id(place)]
            return pltpu.make_async_remote_copy(
                src_ref=block if src is None else src, dst_ref=block, send_sem=send_sems.at[7 * a + k],
                recv_sem=recv_sems.at[7 * a + k], device_id=to, device_id_type=_MESH)

        mine = [pltpu.make_async_copy(in_refs[a], out_refs[a].at[_block_id(me)], local_sems.at[a]) for a in range(n)]
        first = [copy(a, 0, me, sibling, src=in_refs[a]) for a in range(n)]
        first += [copy(a, 1 + j, me, (*chip, c), src=in_refs[a]) for a in range(n) for j, chip in enumerate(chips)]
        for cp in mine + first:
            cp.start()
        passed = []
        for a in range(n):
            for j, chip in enumerate(chips):
                copy(a, 1 + j, (*chip, c), me).wait_recv()
                passed.append(copy(a, 4 + j, (*chip, c), sibling))
                passed[-1].start()
        for a in range(n):
            copy(a, 0, sibling, me).wait_recv()
            for j, chip in enumerate(chips):
                copy(a, 4 + j, (*chip, 1 - c), me).wait_recv()
        for cp in first + passed:
            cp.wait_send()
        for cp in mine:
            cp.wait()

    return pl.pallas_call(
        body, name=name, out_shape=_comm_out_shapes("gather", arrays),
        in_specs=[_ANY] * n, out_specs=[_ANY] * n, scratch_shapes=_comm_scratch(n),
    )(*arrays)


def _comm(kind, arrays, name):
    n = len(arrays)

    def body(*refs):
        remote, local = _peer_copies(kind, refs[:n], refs[n:2 * n], *refs[2 * n:])
        for cp in local + remote:
            cp.start()
        for cp in remote:
            cp.wait()
        for cp in local:
            cp.wait()

    return pl.pallas_call(
        body, name=name, out_shape=_comm_out_shapes(kind, arrays),
        in_specs=[_ANY] * n, out_specs=[_ANY] * n, scratch_shapes=_comm_scratch(n),
    )(*arrays)


def _all_reduce_small(part):
    rows, lanes = part.shape
    vmem = pl.BlockSpec(memory_space=pltpu.VMEM)

    def body(x_ref, gath_ref, sum_ref, send_sems, recv_sems):
        me = _my_place()
        my = _block_id(me)
        gath_ref[my] = x_ref[...]
        copies = []
        for k in range(1, N_DEV):
            cp = pltpu.make_async_remote_copy(
                src_ref=x_ref, dst_ref=gath_ref.at[my], send_sem=send_sems.at[k - 1], recv_sem=recv_sems.at[k - 1],
                device_id=_flip(me, k), device_id_type=_MESH)
            cp.start()
            copies.append(cp)
        for cp in copies:
            cp.wait()
        acc = gath_ref[0]
        for i in range(1, N_DEV):
            acc = acc + gath_ref[i]
        sum_ref[...] = acc

    return pl.pallas_call(
        body, name="small_grad_all_reduce",
        out_shape=[jax.ShapeDtypeStruct((N_DEV, rows, lanes), F32), jax.ShapeDtypeStruct((rows, lanes), F32)],
        in_specs=[vmem], out_specs=[vmem, vmem],
        scratch_shapes=[pltpu.SemaphoreType.DMA((7,)), pltpu.SemaphoreType.DMA((7,))],
    )(part)[1]


_SHARDED = (("w_in", (D_MODEL, IN_PROJ // N_DEV)), ("w_uq", (Q_RANK // N_DEV, Q_RANK)), ("w_ukv", (KV_RANK, HEAD_PAD)),
            ("conv_w", (CONV_W, CONV_DIM // N_DEV)), ("w_out", (D_MODEL // N_DEV, D_MODEL)),
            ("w_up", (D_MODEL, D_FF // N_DEV)), ("w_down", (D_FF // N_DEV, D_MODEL)))
_SMALL = (("pre_mix_norm", D_MODEL), ("q_norm", Q_RANK), ("kv_norm", KV_RANK), ("conv_b", CONV_DIM), ("dt_bias", SSD_HEADS),
          ("a_log", SSD_HEADS), ("d_skip", SSD_HEADS), ("ssd_norm", SSD_INNER), ("post_mix_norm", D_MODEL),
          ("pre_mlp_norm", D_MODEL), ("post_mlp_norm", D_MODEL))
_WEIGHT_ORDER = ("pre_mix_norm", "w_in", "q_norm", "w_uq", "kv_norm", "w_ukv", "conv_w", "conv_b", "dt_bias", "a_log", "d_skip",
                 "ssd_norm", "w_out", "post_mix_norm", "pre_mlp_norm", "w_up", "w_down", "post_mlp_norm")
_EARLY = ("w_in", "w_uq", "w_ukv", "conv_w")
_LATE = ("w_out", "w_up", "w_down")


def _wire_shard(name, a):
    return lax.bitcast_convert_type(a, BF16).reshape(CONV_W, -1) if name == "conv_w" else a.astype(BF16)


def _from_wire(name, g):
    return lax.bitcast_convert_type(g.reshape(N_DEV, CONV_W, -1, 2), F32) if name == "conv_w" else g


def _cols(stacked):
    return jnp.transpose(stacked, (1, 0, 2)).reshape(stacked.shape[1], -1)


def _early_weights(sh):
    w_in = _cols(sh["w_in"])
    zeros = lambda n: jnp.zeros((D_MODEL, n), BF16)
    s1, s2, s3, s4, s5 = 768, 1024, 1056, 1568, 2592
    win = jnp.concatenate([w_in[:, :s2], zeros(MISC_ROPE), w_in[:, s2:s3], w_in[:, s5:], zeros(HEAD_PAD - MISC_DT - SSD_HEADS),
                           w_in[:, s3:s5]], axis=1)
    w_uq = sh["w_uq"].reshape(Q_RANK, MLA_HEADS, QK_NOPE + QK_ROPE)
    wuq = jnp.pad(w_uq, ((0, 0), (0, 0), (0, HEAD_PAD - QK_NOPE - QK_ROPE))).reshape(Q_RANK, -1)
    w_ukv = _cols(sh["w_ukv"]).reshape(KV_RANK, MLA_HEADS, QK_NOPE + V_DIM)
    wkn = jnp.pad(w_ukv[..., :QK_NOPE], ((0, 0), (0, 0), (0, HEAD_PAD - QK_NOPE))).reshape(KV_RANK, -1)
    wv = w_ukv[..., QK_NOPE:].reshape(KV_RANK, 4, 2, 1, V_DIM) * jnp.eye(2, dtype=BF16).reshape(1, 1, 2, 2, 1)
    wkv = jnp.concatenate([wkn, wv.reshape(KV_RANK, -1)], axis=1)
    return dict(win=win, wuq=wuq, wkv=wkv, conv_w=_cols(sh["conv_w"]))


def _late_weights(sh):
    w_out = sh["w_out"].reshape(D_MODEL, D_MODEL)
    watt = w_out[:SSD_INNER].reshape(4, 2, 1, V_DIM, D_MODEL) * jnp.eye(2, dtype=BF16).reshape(1, 2, 2, 1, 1)
    wout = jnp.concatenate([watt.reshape(MLA_HEADS * HEAD_PAD, D_MODEL), w_out[SSD_INNER:]], axis=0)
    return dict(wout=wout, wup=sh["w_up"], wdown=sh["w_down"])


def _shard_grads(g):
    out = {}
    if "wup" in g:
        out["w_up"], out["w_down"] = g["wup"], g["wdown"]
        ae = g["wout_att"].reshape(4, 2, 2, V_DIM, D_MODEL)
        att = jnp.stack([ae[:, 0, 0], ae[:, 1, 1]], axis=1).reshape(SSD_INNER, D_MODEL)
        out["w_out"] = jnp.concatenate([att, g["wout_ssd"]], axis=0).astype(BF16).reshape(N_DEV, D_MODEL // N_DEV, D_MODEL)
    if "win" not in g:
        return out
    dwin = g["win"]
    s1, s2 = 768, 1024
    m0 = s2
    w_in = jnp.concatenate([dwin[:, :s2], dwin[:, m0 + MISC_ROPE:m0 + MISC_ROPE + QK_ROPE], dwin[:, 1152:2688],
                            dwin[:, m0 + MISC_DT:m0 + MISC_DT + SSD_HEADS]], axis=1)
    out["w_in"] = jnp.transpose(w_in.astype(BF16).reshape(D_MODEL, N_DEV, -1), (1, 0, 2))
    w_uq = g["wuq"].astype(BF16).reshape(Q_RANK, MLA_HEADS, HEAD_PAD)[..., :QK_NOPE + QK_ROPE].reshape(Q_RANK, Q_RANK)
    out["w_uq"] = w_uq.reshape(N_DEV, Q_RANK // N_DEV, Q_RANK)
    wide = MLA_HEADS * HEAD_PAD
    wkv = g["wkv"].astype(BF16)
    kn = wkv[:, :wide].reshape(KV_RANK, MLA_HEADS, HEAD_PAD)[..., :QK_NOPE]
    ve = wkv[:, wide:].reshape(KV_RANK, 4, 2, 2, V_DIM)
    vv = jnp.stack([ve[:, :, 0, 0], ve[:, :, 1, 1]], axis=2).reshape(KV_RANK, MLA_HEADS, V_DIM)
    out["w_ukv"] = jnp.transpose(jnp.concatenate([kn, vv], axis=-1), (1, 0, 2))
    out["conv_w"] = jnp.transpose(g["conv_w"].astype(BF16).reshape(CONV_W, N_DEV, -1), (1, 0, 2))
    return out


def _small_rows(n):
    return -(-n // 1024) * 8


def _pack_small(vals):
    rows = []
    for l in range(DEPTH):
        for name, n in _SMALL:
            r = _small_rows(n)
            rows.append(jnp.pad(vals[name][l].reshape(-1), (0, r * 128 - n)).reshape(r, 128))
    return jnp.concatenate(rows, axis=0)


def _unpack_small(packed):
    out, off = {name: [] for name, _ in _SMALL}, 0
    for l in range(DEPTH):
        for name, n in _SMALL:
            r = _small_rows(n)
            out[name].append(packed[off:off + r].reshape(-1)[:n])
            off += r
    return {name: jnp.stack(v) for name, v in out.items()}


def _lane_rows(vec8):
    return jnp.repeat(vec8, SSD_P).reshape(1, SSD_INNER)


def _layer_fwd(h, kw, sm, l, cosf, sinf, consts, gather=(), after_gather=None, target=None):
    row = lambda name: sm[name][l].reshape(1, -1)
    t = {}
    t["h0"] = h
    t["ub"], t["cq"], t["ckv"], t["misc"], t["z"], t["xraw"] = _inproj_fwd(h, row("pre_mix_norm"), kw["win"])
    t["cqn"], t["ckvn"], t["q"], t["k"], t["v"] = _qkv_fwd(t["cq"], t["ckv"], t["misc"], row("q_norm"), row("kv_norm"),
                                                         kw["wuq"], kw["wkv"], cosf, sinf)
    t["oe"], t["lse"], gathered = _attn_fwd(t["q"], t["k"], t["v"], gather)
    if after_gather is not None:
        after_gather(gathered)
    t["dtb"] = _lane_rows(sm["dt_bias"][l])
    t["a_exp"] = _lane_rows(-jnp.exp(sm["a_log"][l]))
    t["d_exp"] = _lane_rows(sm["d_skip"][l])
    t["c"], t["prev"], t["ypre"], t["yssd"] = _ssd_fwd(t["xraw"], t["misc"], t["z"], kw["conv_w"], row("conv_b"), t["dtb"],
                                                     t["a_exp"], t["d_exp"], row("ssd_norm"), consts)
    t["mixed"], t["h1"] = _outproj_fwd(t["oe"], t["yssd"], kw["wout"], h, row("post_mix_norm"))
    t["mb"], t["d"], *out = _mlp_fwd(t["h1"], row("pre_mlp_norm"), kw["wup"], kw["wdown"], row("post_mlp_norm"), target)
    return out, t


def _layer_bwd(dh2, t, kw, sm, l, cosf, sinf, consts, exchange_of=None):
    row = lambda name: sm[name][l].reshape(1, -1)
    g, gs = {}, {}
    dh1, dab, rb, ddb, gs["post_mlp_norm"], gs["pre_mlp_norm"] = _mlp_bwd(
        dh2, t["d"], t["h1"], t["mb"], row("pre_mlp_norm"), kw["wup"], kw["wdown"], row("post_mlp_norm"))
    g["wup"] = _matmul_tn_stacked(t["mb"], dab, f"dw_up_{l}", a_stacked=False)
    g["wdown"] = _matmul_tn_stacked(rb, ddb, f"dw_down_{l}", a_stacked=True)
    dmixb, doe, dyssd, gs["post_mix_norm"], delta = _outproj_bwd(dh1, t["mixed"], row("post_mix_norm"), kw["wout"], t["oe"])
    g["wout_att"] = _matmul_tn(t["oe"], dmixb, f"dw_out_att_{l}")
    g["wout_ssd"] = _matmul_tn(t["yssd"], dmixb, f"dw_out_ssd_{l}")
    dz, dxraw, dmisc_dt, gs["ssd_norm"], gd, galog, gdtb, g["conv_w"], gs["conv_b"] = _ssd_bwd(
        dyssd, t["ypre"], t["z"], t["c"], t["xraw"], t["misc"], t["prev"], kw["conv_w"], t["dtb"], t["a_exp"], t["d_exp"],
        row("ssd_norm"), consts)
    gs["d_skip"] = jnp.sum(gd.reshape(SSD_HEADS, SSD_P), axis=1)
    gs["a_log"] = galog[0, MISC_DT:MISC_DT + SSD_HEADS]
    gs["dt_bias"] = gdtb[0, MISC_DT:MISC_DT + SSD_HEADS]
    dq, dk, dv, exchanged = _attn_bwd(t["q"], t["k"], t["v"], doe, t["lse"], delta,
                                      exchange_of(g) if exchange_of is not None else ())
    dqb, dkvb, dcq, dckv, dmisc_rope, gs["q_norm"], gs["kv_norm"] = _qkv_bwd(
        dq, dk, dv, t["cq"], t["ckv"], row("q_norm"), row("kv_norm"), kw["wuq"], kw["wkv"], cosf, sinf)
    g["wuq"] = _matmul_tn(t["cqn"], dqb, f"dw_uq_{l}")
    g["wkv"] = _matmul_tn(t["ckvn"], dkvb, f"dw_kv_{l}")
    dprojb, dh0, gs["pre_mix_norm"] = _inproj_bwd(dcq, dckv, dmisc_rope, dmisc_dt, dz, dxraw, t["h0"], dh1,
                                                  row("pre_mix_norm"), kw["win"])
    g["win"] = _matmul_tn(t["ub"], dprojb, f"dw_in_{l}")
    return dh0, g, {k: v.reshape(-1) for k, v in gs.items()}, exchanged


def _local_step(x, positions, kws, sm, target, gather=(), after_gather=None, exchange_of=None):
    inv_freq = ROPE_THETA ** (-jnp.arange(0, QK_ROPE, 2, dtype=F32) / QK_ROPE)
    invf = jnp.zeros((HEAD_PAD,), F32).at[MISC_ROPE:MISC_ROPE + QK_ROPE].set(jnp.concatenate([inv_freq, inv_freq]))
    cosf, sinf = _rope_tables(positions.reshape(-1, 1), invf.reshape(1, HEAD_PAD))
    consts = _ssd_consts()
    (h,), t0 = _layer_fwd(x, kws[0], sm, 0, cosf, sinf, consts, gather, after_gather)
    (dh, loss), t1 = _layer_fwd(h, kws[1], sm, 1, cosf, sinf, consts, target=target)
    saved = [t0, t1]
    grads, small, exchanged = [None] * DEPTH, [None] * DEPTH, []
    for l in reversed(range(DEPTH)):
        hook = (lambda g0: exchange_of(g0, grads[1])) if (l == 0 and exchange_of is not None) else None
        dh, grads[l], small[l], got = _layer_bwd(dh, saved[l], kws[l], sm, l, cosf, sinf, consts, hook)
        exchanged = got or exchanged
    return loss[0, 0], dh, grads, small, exchanged


def kernel(x, positions, pre_mix_norm, w_in, q_norm, w_uq, kv_norm, w_ukv, conv_w, conv_b, dt_bias, a_log, d_skip, ssd_norm, w_out, post_mix_norm, pre_mlp_norm, w_up, w_down, post_mlp_norm, loss_target, m_pre_mix_norm, m_w_in, m_q_norm, m_w_uq, m_kv_norm, m_w_ukv, m_conv_w, m_conv_b, m_dt_bias, m_a_log, m_d_skip, m_ssd_norm, m_w_out, m_post_mix_norm, m_pre_mlp_norm, m_w_up, m_w_down, m_post_mlp_norm, v_pre_mix_norm, v_w_in, v_q_norm, v_w_uq, v_kv_norm, v_w_ukv, v_conv_w, v_conv_b, v_dt_bias, v_a_log, v_d_skip, v_ssd_norm, v_w_out, v_post_mix_norm, v_pre_mlp_norm, v_w_up, v_w_down, v_post_mlp_norm):
    w = dict(pre_mix_norm=pre_mix_norm, w_in=w_in, q_norm=q_norm, w_uq=w_uq, kv_norm=kv_norm, w_ukv=w_ukv, conv_w=conv_w,
             conv_b=conv_b, dt_bias=dt_bias, a_log=a_log, d_skip=d_skip, ssd_norm=ssd_norm, w_out=w_out,
             post_mix_norm=post_mix_norm, pre_mlp_norm=pre_mlp_norm, w_up=w_up, w_down=w_down, post_mlp_norm=post_mlp_norm)
    m = dict(pre_mix_norm=m_pre_mix_norm, w_in=m_w_in, q_norm=m_q_norm, w_uq=m_w_uq, kv_norm=m_kv_norm, w_ukv=m_w_ukv,
             conv_w=m_conv_w, conv_b=m_conv_b, dt_bias=m_dt_bias, a_log=m_a_log, d_skip=m_d_skip, ssd_norm=m_ssd_norm,
             w_out=m_w_out, post_mix_norm=m_post_mix_norm, pre_mlp_norm=m_pre_mlp_norm, w_up=m_w_up, w_down=m_w_down,
             post_mlp_norm=m_post_mlp_norm)
    v = dict(pre_mix_norm=v_pre_mix_norm, w_in=v_w_in, q_norm=v_q_norm, w_uq=v_w_uq, kv_norm=v_kv_norm, w_ukv=v_w_ukv,
             conv_w=v_conv_w, conv_b=v_conv_b, dt_bias=v_dt_bias, a_log=v_a_log, d_skip=v_d_skip, ssd_norm=v_ssd_norm,
             w_out=v_w_out, post_mix_norm=v_post_mix_norm, pre_mlp_norm=v_pre_mlp_norm, w_up=v_w_up, w_down=v_w_down,
             post_mlp_norm=v_post_mlp_norm)
    sm = {name: w[name] for name, _ in _SMALL}

    wire = lambda name, l: _wire_shard(name, w[name][l])
    first = _gather_two_level([wire(name, 0) for name in _EARLY], "weight_gather_first")
    kws = [_early_weights({name: _from_wire(name, a) for name, a in zip(_EARLY, first)}), None]
    behind = [(name, 0) for name in _LATE] + [(name, 1) for name, _ in _SHARDED]

    def after_gather(gathered):
        got = {key: _from_wire(key[0], a) for key, a in zip(behind, gathered)}
        kws[0].update(_late_weights({name: got[name, 0] for name in _LATE}))
        kws[1] = {**_early_weights({name: got[name, 1] for name in _EARLY}),
                  **_late_weights({name: got[name, 1] for name in _LATE})}

    sent_behind = [(name, 1) for name, _ in _SHARDED] + [(name, 0) for name in _LATE]

    def exchange_of(g0, g1):
        blocks = {**{(name, 1): a for name, a in _shard_grads(g1).items()},
                  **{(name, 0): a for name, a in _shard_grads(g0).items()}}
        return [blocks[key] for key in sent_behind]

    loss_part, dx, grads, small, exchanged = _local_step(
        x[0], positions[0], kws, sm, loss_target[0], [wire(*key) for key in behind], after_gather, exchange_of)
    slots = dict(zip(sent_behind, exchanged))
    last = _shard_grads({k: grads[0][k] for k in ("win", "wuq", "wkv", "conv_w")})
    slots.update({(name, 0): a for name, a in zip(_EARLY, _comm("exchange", [last[name] for name in _EARLY], "grad_exchange_last"))})
    g_small = _unpack_small(_all_reduce_small(_pack_small({name: jnp.stack([small[l][name] for l in range(DEPTH)])
                                                           for name, _ in _SMALL})))
    loss = lax.psum(loss_part, ("x", "y", "c"))

    grad, delta, new_m, new_v = {}, {}, {}, {}
    for name, _ in _SHARDED:
        grad[name], delta[name], new_m[name], new_v[name] = _sum_adamw(
            [slots[name, 0], slots[name, 1]], w[name], m[name], v[name], f"sum_adamw_{name}")
    pk = lambda d: _pack_small({name: d[name] for name, _ in _SMALL})
    d_, m_, v_ = _adamw(pk(w), pk(g_small), pk(m), pk(v), "adamw_small")
    for dst, packed in ((delta, d_), (new_m, m_), (new_v, v_)):
        dst.update(_unpack_small(packed))
    grad.update(g_small)

    outs = [loss, dx[None]]
    for d in (grad, delta, new_m, new_v):
        outs += [d[name] for name in _WEIGHT_ORDER]
    return tuple(outs)
```

```python
import jax
import jax.numpy as jnp
import numpy as np
from jax import lax
from jax.experimental import pallas as pl
from jax.experimental.pallas import tpu as pltpu

F32 = jnp.float32
BF16 = jnp.bfloat16
HI = lax.Precision.HIGHEST

D_MODEL = 1024
DEPTH = 2
N_DEV = 8
CHUNK = 64
EPS = 1e-6
MLA_HEADS = 8
QK_NOPE = 64
QK_ROPE = 32
V_DIM = 64
Q_RANK = 768
KV_RANK = 256
ROPE_THETA = 10000.0
SSD_HEADS = 8
SSD_P = 64
SSD_INNER = 512
SSD_GROUPS = 2
SSD_N = 128
CONV_W = 4
CONV_DIM = 1024
D_FF = 4096
IN_PROJ = 2600
HEAD_PAD = 128
IN_PAD = 2688
MISC_ROPE = 64
MISC_DT = 96
ATT_SCALE = (QK_NOPE + QK_ROPE) ** -0.5
LOG2E = 1.4426950408889634
ATT_SCALE_LOG2 = ATT_SCALE * LOG2E

ADAM_LR = 0.001
ADAM_B1 = 0.9
ADAM_B2 = 0.999
ADAM_EPS = 1e-08
ADAM_WD = 0.01
ADAM_STEP = 10

TM = 512
TQ = 256
ATT_T = 512
ATT_G = 4
SSD_ROWS = 256
TK_DW = 4096
VMEM_LIMIT = 56 * 1024 * 1024

_NT = (((1,), (1,)), ((), ()))
_TN = (((0,), (0,)), ((), ()))


def _params(**kw):
    return pltpu.CompilerParams(vmem_limit_bytes=VMEM_LIMIT, **kw)


def _dot(a, b, precision=None):
    return jnp.dot(a, b, preferred_element_type=F32, precision=precision)


def _dot_nt(a, b, precision=None):
    return lax.dot_general(a, b, _NT, preferred_element_type=F32, precision=precision)


def _dot_tn(a, b, precision=None):
    return lax.dot_general(a, b, _TN, preferred_element_type=F32, precision=precision)


def _split3(x):
    hi = x.astype(BF16)
    r = x - hi.astype(F32)
    mid = r.astype(BF16)
    return hi, mid, (r - mid.astype(F32)).astype(BF16)


def _dot01(x, m01, dot=_dot, left=False):
    parts = [dot(m01, p) if left else dot(p, m01) for p in _split3(x)]
    return parts[0] + parts[1] + parts[2]


def _full(shape):
    n = len(shape)
    return pl.BlockSpec(shape, lambda *_: (0,) * n)


def _resident(shape):
    n = len(shape)
    return pl.BlockSpec(shape, lambda *_: (0,) * n, pipeline_mode=pl.Buffered(1))


def _rows(tm, width):
    return pl.BlockSpec((tm, width), lambda i: (i, 0))


def _rms_fwd(x, w):
    r = lax.rsqrt(jnp.mean(x * x, axis=-1, keepdims=True) + EPS)
    return (x * r) * w


def _rms_bwd(x, w, dy):
    r = lax.rsqrt(jnp.mean(x * x, axis=-1, keepdims=True) + EPS)
    xh = x * r
    dxn = dy * w
    dx = r * (dxn - xh * jnp.mean(dxn * xh, axis=-1, keepdims=True))
    return dx, dy * xh


def _acc_rows(ref, val, first):
    s = jnp.sum(val, axis=0, keepdims=True)

    @pl.when(first)
    def _():
        ref[...] = s

    @pl.when(jnp.logical_not(first))
    def _():
        ref[...] += s


def _rope(t, cosf, sinf, sign):
    lane = lax.broadcasted_iota(jnp.int32, t.shape, 1)
    rot = jnp.where(lane < MISC_ROPE + QK_ROPE // 2, -pltpu.roll(t, HEAD_PAD - QK_ROPE // 2, 1), pltpu.roll(t, QK_ROPE // 2, 1))
    return t * cosf + sign * (rot * sinf)


def _rope_tables(pos, invf):
    s = pos.shape[0]

    def body(pos_ref, invf_ref, cos_ref, sin_ref):
        ang = pos_ref[...].astype(F32) * invf_ref[...]
        cos_ref[...] = jnp.cos(ang)
        sin_ref[...] = jnp.sin(ang)

    return pl.pallas_call(
        body, name="rope_tables", grid=(s // TM,),
        in_specs=[_rows(TM, 1), _full((1, HEAD_PAD))],
        out_specs=[_rows(TM, HEAD_PAD), _rows(TM, HEAD_PAD)],
        out_shape=[jax.ShapeDtypeStruct((s, HEAD_PAD), F32)] * 2,
    )(pos, invf)


def _inproj_fwd(h, nw, win):
    s = h.shape[0]

    def body(h_ref, nw_ref, w_ref, ub_ref, cq_ref, ckv_ref, misc_ref, z_ref, xbc_ref):
        ub = _rms_fwd(h_ref[...], nw_ref[...]).astype(BF16)
        ub_ref[...] = ub
        proj = _dot(ub, w_ref[...])
        cq_ref[...] = proj[:, 0:768]
        ckv_ref[...] = proj[:, 768:1024]
        misc_ref[...] = proj[:, 1024:1152]
        z_ref[...] = proj[:, 1152:1664]
        xbc_ref[...] = proj[:, 1664:2688]

    widths = (768, 256, 128, 512, 1024)
    return pl.pallas_call(
        body, name="inproj_fwd", grid=(s // TM,),
        in_specs=[_rows(TM, D_MODEL), _full((1, D_MODEL)), _resident((D_MODEL, IN_PAD))],
        out_specs=[_rows(TM, D_MODEL)] + [_rows(TM, w) for w in widths],
        out_shape=[jax.ShapeDtypeStruct((s, D_MODEL), BF16)] + [jax.ShapeDtypeStruct((s, w), F32) for w in widths],
        compiler_params=_params(),
    )(h, nw, win)


def _qkv_fwd(cq, ckv, misc, qnw, kvnw, wuq, wkv, cosf, sinf):
    s = cq.shape[0]

    def body(cq_ref, ckv_ref, misc_ref, qnw_ref, kvnw_ref, wuq_ref, wkv_ref, cos_ref, sin_ref,
             cqn_ref, ckvn_ref, q_ref, k_ref, v_ref):
        cosf, sinf = cos_ref[...], sin_ref[...]
        cqn = _rms_fwd(cq_ref[...], qnw_ref[...]).astype(BF16)
        cqn_ref[...] = cqn
        q = _dot(cqn, wuq_ref[...])
        ckvn = _rms_fwd(ckv_ref[...], kvnw_ref[...]).astype(BF16)
        ckvn_ref[...] = ckvn
        kv = _dot(ckvn, wkv_ref[...])
        m = misc_ref[...]
        lane = lax.broadcasted_iota(jnp.int32, m.shape, 1)
        in_rope = jnp.logical_and(lane >= MISC_ROPE, lane < MISC_ROPE + QK_ROPE)
        kr = jnp.where(in_rope, _rope(m, cosf, sinf, 1.0), 0.0)
        for hd in range(MLA_HEADS):
            cols = slice(hd * HEAD_PAD, (hd + 1) * HEAD_PAD)
            q_ref[:, cols] = _rope(q[:, cols], cosf, sinf, 1.0).astype(BF16)
            k_ref[:, cols] = (kv[:, cols] + kr).astype(BF16)
        vv = kv[:, MLA_HEADS * HEAD_PAD:]
        vlane = lax.broadcasted_iota(jnp.int32, vv.shape, 1)
        ones_at = jnp.where((vlane // HEAD_PAD) % 2 == 0, V_DIM, 0)
        v_ref[...] = jnp.where(vlane % HEAD_PAD == ones_at, 1.0, vv).astype(BF16)

    wide = MLA_HEADS * HEAD_PAD
    return pl.pallas_call(
        body, name="qkv_fwd", grid=(s // TM,),
        in_specs=[_rows(TM, Q_RANK), _rows(TM, KV_RANK), _rows(TM, HEAD_PAD), _full((1, Q_RANK)), _full((1, KV_RANK)),
                  _resident((Q_RANK, wide)), _resident((KV_RANK, 2 * wide)), _rows(TM, HEAD_PAD), _rows(TM, HEAD_PAD)],
        out_specs=[_rows(TM, Q_RANK), _rows(TM, KV_RANK), _rows(TM, wide), _rows(TM, wide), _rows(TM, wide)],
        out_shape=[jax.ShapeDtypeStruct((s, Q_RANK), BF16), jax.ShapeDtypeStruct((s, KV_RANK), BF16)]
        + [jax.ShapeDtypeStruct((s, wide), BF16)] * 3,
        compiler_params=_params(),
    )(cq, ckv, misc, qnw, kvnw, wuq, wkv, cosf, sinf)


def _chunk_mask(t, keys_on_rows=False):
    row = lax.broadcasted_iota(jnp.int32, (t, t), 0) // CHUNK
    col = lax.broadcasted_iota(jnp.int32, (t, t), 1) // CHUNK
    return (row <= col) if keys_on_rows else (col <= row)


def _attn_fwd(q, k, v, gather=()):
    s = q.shape[0]
    t = ATT_T
    nq = s // t
    pair = ATT_G * HEAD_PAD
    ng = len(gather)

    def body(q_ref, k_ref, v_ref, *rest):
        g_in, (o_ref, lse_ref), g_out = rest[:ng], rest[ng:ng + 2], rest[ng + 2:2 * ng + 2]
        m_s, acc_s = rest[2 * ng + 2:2 * ng + 4]
        qi = pl.program_id(1)
        _hosted_comm("gather", g_in, g_out, rest[2 * ng + 4:],
                     jnp.logical_and(pl.program_id(0) == 0, qi == 0),
                     jnp.logical_and(pl.program_id(0) == MLA_HEADS // ATT_G - 1, qi == nq - 1))
        m_s[...] = jnp.full(m_s.shape, -jnp.inf, F32)
        acc_s[...] = jnp.zeros(acc_s.shape, F32)

        def step(kb, masked):
            r0 = pl.multiple_of(kb * t, t)

            def scores(hh):
                cols = slice(hh * HEAD_PAD, (hh + 1) * HEAD_PAD)
                return _dot_nt(q_ref[:, cols], k_ref[pl.ds(r0, t), cols])

            def soft(hh, raw):
                sc = raw * ATT_SCALE_LOG2
                if masked:
                    sc = jnp.where(_chunk_mask(t), sc, -jnp.inf)
                m_old = m_s[hh]
                m_new = jnp.maximum(m_old, jnp.max(sc, axis=-1, keepdims=True))
                alpha = jnp.exp2(m_old - m_new)
                p = jnp.exp2(sc - jnp.tile(m_new, (1, t // HEAD_PAD)))
                m_s[hh] = m_new
                return alpha, p.astype(BF16)

            def update(hh, alpha, p):
                cols = slice(hh * HEAD_PAD, (hh + 1) * HEAD_PAD)
                acc_s[hh] = alpha * acc_s[hh] + _dot(p, v_ref[pl.ds(r0, t), cols])

            raw, ap = [None] * ATT_G, [None] * ATT_G
            raw[0] = scores(0)
            for hh in range(ATT_G):
                if hh + 1 < ATT_G:
                    raw[hh + 1] = scores(hh + 1)
                ap[hh] = soft(hh, raw[hh])
                if hh >= 1:
                    update(hh - 1, *ap[hh - 1])
            update(ATT_G - 1, *ap[ATT_G - 1])

        def loop(kb, c):
            step(kb, False)
            return c

        lax.fori_loop(0, qi, loop, 0)
        step(qi, True)
        for hh in range(ATT_G):
            cols = slice(hh * HEAD_PAD, (hh + 1) * HEAD_PAD)
            acc = acc_s[hh]
            ones_at = V_DIM * (1 - hh % 2)
            l = jnp.broadcast_to(acc[:, ones_at:ones_at + 1], acc.shape)
            o_ref[:, cols] = (acc / l).astype(BF16)
            lse_ref[hh] = (m_s[hh] + jnp.log(l) * LOG2E).T[0:8, :]

    outs = pl.pallas_call(
        body, name="attn_fwd_gather" if ng else "attn_fwd", grid=(MLA_HEADS // ATT_G, nq),
        in_specs=[pl.BlockSpec((t, pair), lambda h, i: (i, h)),
                  pl.BlockSpec((s, pair), lambda h, i: (0, h)),
                  pl.BlockSpec((s, pair), lambda h, i: (0, h))] + [_ANY] * ng,
        out_specs=[pl.BlockSpec((t, pair), lambda h, i: (i, h)),
                   pl.BlockSpec((ATT_G, 8, t), lambda h, i: (h, 0, i))] + [_ANY] * ng,
        out_shape=[jax.ShapeDtypeStruct((s, MLA_HEADS * HEAD_PAD), BF16), jax.ShapeDtypeStruct((MLA_HEADS, 8, s), F32)]
        + _comm_out_shapes("gather", gather),
        scratch_shapes=[pltpu.VMEM((ATT_G, t, HEAD_PAD), F32), pltpu.VMEM((ATT_G, t, HEAD_PAD), F32)]
        + (_comm_scratch(ng) if ng else []),
        compiler_params=_params(),
    )(q, k, v, *gather)
    return outs[0], outs[1], list(outs[2:])


def _interleave(stages):
    live = list(stages)
    while live:
        still = []
        for g in live:
            try:
                next(g)
                still.append(g)
            except StopIteration:
                pass
        live = still


def _ssd_consts():
    emisc = np.zeros((HEAD_PAD, SSD_INNER), np.float32)
    for hd in range(SSD_HEADS):
        emisc[MISC_DT + hd, hd * SSD_P:(hd + 1) * SSD_P] = 1.0
    idx = np.arange(CHUNK)
    tri = (idx[:, None] >= idx[None, :]).astype(np.float32)
    return tuple(jnp.asarray(m, BF16) for m in (emisc, emisc.T.copy(), tri, tri.T.copy()))


def _ssd_chunk_common(cc, misc, emisc, tri, trit, dtb, a_exp):
    xa = cc * jax.nn.sigmoid(cc)
    dtr = _dot01(misc, emisc) + dtb
    dt = jax.nn.softplus(dtr)
    a = dt * a_exp
    acs = _dot01(a, tri, left=True)
    acs_t = _dot01(a, trit, dot=_dot_tn)
    alast = acs[CHUNK - 1:CHUNK, :]
    return xa, dtr, dt, acs, acs_t, alast


def _decay(acs, acs_t, hd):
    row = lax.broadcasted_iota(jnp.int32, (CHUNK, CHUNK), 0)
    col = lax.broadcasted_iota(jnp.int32, (CHUNK, CHUNK), 1)
    diff = acs[:, hd * SSD_P:hd * SSD_P + 1] - acs_t[hd * SSD_P:hd * SSD_P + 1, :]
    return jnp.exp(jnp.where(row >= col, diff, -jnp.inf))


def _half_mask(hh):
    lane = lax.broadcasted_iota(jnp.int32, (CHUNK, 2 * SSD_P), 1)
    return (lane >= SSD_P) if hh else (lane < SSD_P)


def _gate_norm(y, zz, nw):
    yz = y * (zz * jax.nn.sigmoid(zz))
    outs, rs = [], []
    half = SSD_INNER // SSD_GROUPS
    for g in range(SSD_GROUPS):
        yg = yz[:, g * half:(g + 1) * half]
        r = lax.rsqrt(jnp.mean(yg * yg, axis=-1, keepdims=True) + EPS)
        outs.append(yg * r)
        rs.append(r)
    return yz, jnp.concatenate(outs, axis=1), rs


def _ssd_fwd(xraw, misc, z, cw, cb, dtb, a_exp, d_exp, nw, consts):
    s = xraw.shape[0]
    nb = s // SSD_ROWS
    ncb = SSD_ROWS // CHUNK
    emisc, _, tri, trit = consts

    def body(x_ref, misc_ref, z_ref, cw_ref, cb_ref, dtb_ref, a_ref, d_ref, nw_ref, emisc_ref, tri_ref, trit_ref,
             c_ref, prev_ref, ypre_ref, yssd_ref, tail_s, state_s):
        i = pl.program_id(0)

        @pl.when(i == 0)
        def _():
            tail_s[...] = jnp.zeros(tail_s.shape, F32)
            state_s[...] = jnp.zeros(state_s.shape, F32)

        x = x_ref[...]
        xext = jnp.concatenate([tail_s[...], x], axis=0)
        acc = x * cw_ref[CONV_W - 1:CONV_W, :] + cb_ref[...]
        for j in range(1, CONV_W):
            acc = acc + pltpu.roll(xext, j, 0)[8:, :] * cw_ref[CONV_W - 1 - j:CONV_W - j, :]
        tail_s[...] = x[SSD_ROWS - 8:, :]
        c_ref[...] = acc

        def chunk(ci):
            r0 = ci * CHUNK
            xa, _, dt, acs, acs_t, alast = _ssd_chunk_common(
                c_ref[pl.ds(r0, CHUNK), :], misc_ref[pl.ds(r0, CHUNK), :], emisc_ref[...], tri_ref[...], trit_ref[...],
                dtb_ref[...], a_ref[...])
            yield
            xs = xa[:, :SSD_INNER]
            xdt = xs * dt
            wgt = (xdt * jnp.exp(alast - acs)).astype(BF16)
            e = jnp.exp(acs)
            ys, new_states, cms = [], [], []
            for g in range(SSD_GROUPS):
                bm = xa[:, SSD_INNER + g * SSD_N:SSD_INNER + (g + 1) * SSD_N].astype(BF16)
                cm = xa[:, SSD_INNER + SSD_GROUPS * SSD_N + g * SSD_N:SSD_INNER + SSD_GROUPS * SSD_N + (g + 1) * SSD_N].astype(BF16)
                cms.append(cm)
                cb_g = _dot_nt(cm, bm)
                gl = slice(g * 256, (g + 1) * 256)
                new_states.append(_dot_tn(bm, wgt[:, gl]))
                for jj in range(2):
                    pair = 2 * g + jj
                    xp = xdt[:, pair * 128:(pair + 1) * 128]
                    yp = None
                    for hh in range(2):
                        sc = (cb_g * _decay(acs, acs_t, 2 * pair + hh)).astype(BF16)
                        term = _dot(sc, jnp.where(_half_mask(hh), xp, 0.0).astype(BF16))
                        yp = term if yp is None else yp + term
                    ys.append(yp)
                yield
            prev = state_s[...]
            prev_ref[ci] = prev
            yoff = jnp.concatenate([_dot(cms[g], prev[:, g * 256:(g + 1) * 256].astype(BF16)) for g in range(SSD_GROUPS)],
                                   axis=1) * e
            state_s[...] = prev * jnp.exp(alast) + jnp.concatenate(new_states, axis=1)
            yield
            y = jnp.concatenate(ys, axis=1) + yoff + d_ref[...] * xs
            ypre_ref[pl.ds(r0, CHUNK), :] = y
            _, yn, _ = _gate_norm(y, z_ref[pl.ds(r0, CHUNK), :], None)
            yssd_ref[pl.ds(r0, CHUNK), :] = (yn * nw_ref[...]).astype(BF16)

        _interleave([chunk(ci) for ci in range(ncb)])

    return pl.pallas_call(
        body, name="ssd_fwd", grid=(nb,),
        in_specs=[_rows(SSD_ROWS, CONV_DIM), _rows(SSD_ROWS, HEAD_PAD), _rows(SSD_ROWS, SSD_INNER),
                  _full((CONV_W, CONV_DIM)), _full((1, CONV_DIM)), _full((1, SSD_INNER)), _full((1, SSD_INNER)),
                  _full((1, SSD_INNER)), _full((1, SSD_INNER)), _full((HEAD_PAD, SSD_INNER)), _full((CHUNK, CHUNK)),
                  _full((CHUNK, CHUNK))],
        out_specs=[_rows(SSD_ROWS, CONV_DIM), pl.BlockSpec((ncb, SSD_N, SSD_INNER), lambda i: (i, 0, 0)),
                   _rows(SSD_ROWS, SSD_INNER), _rows(SSD_ROWS, SSD_INNER)],
        out_shape=[jax.ShapeDtypeStruct((s, CONV_DIM), F32), jax.ShapeDtypeStruct((s // CHUNK, SSD_N, SSD_INNER), F32),
                   jax.ShapeDtypeStruct((s, SSD_INNER), F32), jax.ShapeDtypeStruct((s, SSD_INNER), BF16)],
        scratch_shapes=[pltpu.VMEM((8, CONV_DIM), F32), pltpu.VMEM((SSD_N, SSD_INNER), F32)],
        compiler_params=_params(),
    )(xraw, misc, z, cw, cb, dtb, a_exp, d_exp, nw, emisc, tri, trit)


def _outproj_fwd(oe, yssd, wout, h, nw):
    s = h.shape[0]
    wide = MLA_HEADS * HEAD_PAD

    def body(oe_ref, y_ref, w_ref, h_ref, nw_ref, mixed_ref, h1_ref):
        mixed = _dot(oe_ref[...], w_ref[0:wide, :]) + _dot(y_ref[...], w_ref[wide:, :])
        mixed_ref[...] = mixed
        h1_ref[...] = h_ref[...] + _rms_fwd(mixed, nw_ref[...])

    return pl.pallas_call(
        body, name="outproj_fwd", grid=(s // TM,),
        in_specs=[_rows(TM, wide), _rows(TM, SSD_INNER), _resident((wide + SSD_INNER, D_MODEL)), _rows(TM, D_MODEL),
                  _full((1, D_MODEL))],
        out_specs=[_rows(TM, D_MODEL), _rows(TM, D_MODEL)],
        out_shape=[jax.ShapeDtypeStruct((s, D_MODEL), F32)] * 2,
        compiler_params=_params(),
    )(oe, yssd, wout, h, nw)


def _mlp_fwd(h1, prew, wup, wdown, postw, target=None):
    s = h1.shape[0]
    fb = D_FF // N_DEV
    last = target is not None

    def body(h_ref, prew_ref, up_ref, down_ref, postw_ref, *rest):
        hh = h_ref[...]
        mb = _rms_fwd(hh, prew_ref[...]).astype(BF16)
        rest[-3 - last][...] = mb
        d = jnp.zeros((TM, D_MODEL), F32)
        for j in range(N_DEV):
            a = _dot(mb, up_ref[j])
            r = jnp.square(jnp.maximum(a, 0.0)).astype(BF16)
            d = d + _dot(r, down_ref[j])
        rest[-2 - last][...] = d
        h2 = hh + _rms_fwd(d, postw_ref[...])
        if last:
            diff = h2 - rest[0][...]
            rest[-2][...] = diff * (1.0 / D_MODEL)
            part = 0.5 * jnp.sum(jnp.mean(diff * diff, axis=-1, keepdims=True), axis=0, keepdims=True)
            _acc_rows(rest[-1], part, pl.program_id(0) == 0)
        else:
            rest[-1][...] = h2

    return pl.pallas_call(
        body, name="mlp_fwd_loss" if last else "mlp_fwd", grid=(s // TM,),
        in_specs=[_rows(TM, D_MODEL), _full((1, D_MODEL)), _resident((N_DEV, D_MODEL, fb)), _resident((N_DEV, fb, D_MODEL)),
                  _full((1, D_MODEL))] + ([_rows(TM, D_MODEL)] if last else []),
        out_specs=[_rows(TM, D_MODEL)] * 3 + ([_full((1, 1))] if last else []),
        out_shape=[jax.ShapeDtypeStruct((s, D_MODEL), BF16), jax.ShapeDtypeStruct((s, D_MODEL), F32),
                   jax.ShapeDtypeStruct((s, D_MODEL), F32)] + ([jax.ShapeDtypeStruct((1, 1), F32)] if last else []),
        compiler_params=_params(),
    )(h1, prew, wup, wdown, postw, *([target] if last else []))


def _mlp_bwd(dh2, d, h1, mb, prew, wup, wdown, postw):
    s = dh2.shape[0]
    fb = D_FF // N_DEV
    tm = TM // 2

    def body(dh2_ref, d_ref, h1_ref, mb_ref, prew_ref, up_ref, down_ref, postw_ref,
             dh1_ref, da_ref, r_ref, dd_ref, gpost_ref, gpre_ref):
        first = pl.program_id(0) == 0
        dh2 = dh2_ref[...]
        dd, gpost = _rms_bwd(d_ref[...], postw_ref[...], dh2)
        _acc_rows(gpost_ref, gpost, first)
        ddb = dd.astype(BF16)
        dd_ref[...] = ddb
        mb = mb_ref[...]
        dm = jnp.zeros((tm, D_MODEL), F32)
        for j in range(N_DEV):
            a = jnp.maximum(_dot(mb, up_ref[j]), 0.0)
            r_ref[j] = jnp.square(a).astype(BF16)
            da = (_dot_nt(ddb, down_ref[j]) * (2.0 * a)).astype(BF16)
            da_ref[j] = da
            dm = dm + _dot_nt(da, up_ref[j])
        dx, gpre = _rms_bwd(h1_ref[...], prew_ref[...], dm)
        _acc_rows(gpre_ref, gpre, first)
        dh1_ref[...] = dh2 + dx

    stacked = pl.BlockSpec((N_DEV, tm, fb), lambda i: (0, i, 0))
    return pl.pallas_call(
        body, name="mlp_bwd", grid=(s // tm,),
        in_specs=[_rows(tm, D_MODEL)] * 4 + [_full((1, D_MODEL)), _resident((N_DEV, D_MODEL, fb)), _resident((N_DEV, fb, D_MODEL)),
                                              _full((1, D_MODEL))],
        out_specs=[_rows(tm, D_MODEL), stacked, stacked, _rows(tm, D_MODEL), _full((1, D_MODEL)), _full((1, D_MODEL))],
        out_shape=[jax.ShapeDtypeStruct((s, D_MODEL), F32), jax.ShapeDtypeStruct((N_DEV, s, fb), BF16),
                   jax.ShapeDtypeStruct((N_DEV, s, fb), BF16), jax.ShapeDtypeStruct((s, D_MODEL), BF16),
                   jax.ShapeDtypeStruct((1, D_MODEL), F32), jax.ShapeDtypeStruct((1, D_MODEL), F32)],
        compiler_params=_params(),
    )(dh2, d, h1, mb, prew, wup, wdown, postw)


def _matmul_tn(a, b, name, tk=TK_DW):
    s, m = a.shape
    n = b.shape[1]
    tn = n if n <= 1024 else (n // 2 if (n // 2) % 128 == 0 else n // 3)
    tk = min(tk, s)
    assert n % tn == 0 and tn % 128 == 0 and s % tk == 0

    def body(a_ref, b_ref, o_ref):
        part = _dot_tn(a_ref[...], b_ref[...])

        @pl.when(pl.program_id(1) == 0)
        def _():
            o_ref[...] = part

        @pl.when(pl.program_id(1) != 0)
        def _():
            o_ref[...] += part

    return pl.pallas_call(
        body, name=name, grid=(n // tn, s // tk),
        in_specs=[pl.BlockSpec((tk, m), lambda j, k: (k, 0)), pl.BlockSpec((tk, tn), lambda j, k: (k, j))],
        out_specs=pl.BlockSpec((m, tn), lambda j, k: (0, j)),
        out_shape=jax.ShapeDtypeStruct((m, n), F32),
        compiler_params=_params(),
    )(a, b)


def _matmul_tn_stacked(a, b, name, a_stacked, tk=TK_DW):
    tk = min(tk, a.shape[-2])
    if a_stacked:
        _, s, m = a.shape
        n = b.shape[1]
        in_specs = [pl.BlockSpec((1, tk, m), lambda j, k: (j, k, 0)), pl.BlockSpec((tk, n), lambda j, k: (k, 0))]
    else:
        s, m = a.shape
        n = b.shape[2]
        in_specs = [pl.BlockSpec((tk, m), lambda j, k: (k, 0)), pl.BlockSpec((1, tk, n), lambda j, k: (j, k, 0))]

    nk = s // tk

    def body(a_ref, b_ref, o_ref, acc_s):
        av = a_ref[0] if a_stacked else a_ref[...]
        bv = b_ref[...] if a_stacked else b_ref[0]
        part = _dot_tn(av, bv)
        k = pl.program_id(1)

        @pl.when(k == 0)
        def _():
            acc_s[...] = part

        @pl.when(jnp.logical_and(k != 0, k != nk - 1))
        def _():
            acc_s[...] += part

        @pl.when(k == nk - 1)
        def _():
            o_ref[0] = (part if nk == 1 else acc_s[...] + part).astype(BF16)

    return pl.pallas_call(
        body, name=name, grid=(N_DEV, nk),
        in_specs=in_specs,
        out_specs=pl.BlockSpec((1, m, n), lambda j, k: (j, 0, 0)),
        out_shape=jax.ShapeDtypeStruct((N_DEV, m, n), BF16),
        scratch_shapes=[pltpu.VMEM((m, n), F32)],
        compiler_params=_params(),
    )(a, b)


def _outproj_bwd(dh1, mixed, nw, wout, oe):
    s = dh1.shape[0]
    wide = MLA_HEADS * HEAD_PAD

    def body(dh1_ref, mixed_ref, nw_ref, w_ref, oe_ref, dmix_ref, doe_ref, dy_ref, gnw_ref, delta_ref):
        dmix, gnw = _rms_bwd(mixed_ref[...], nw_ref[...], dh1_ref[...])
        _acc_rows(gnw_ref, gnw, pl.program_id(0) == 0)
        dmb = dmix.astype(BF16)
        dmix_ref[...] = dmb
        doe_ref[...] = _dot_nt(dmb, w_ref[0:wide, :]).astype(BF16)
        dy_ref[...] = _dot_nt(dmb, w_ref[wide:, :])
        ones = jnp.ones((8, HEAD_PAD), BF16)
        for hd in range(MLA_HEADS):
            cols = slice(hd * HEAD_PAD, (hd + 1) * HEAD_PAD)
            prod = oe_ref[:, cols].astype(F32) * doe_ref[:, cols].astype(F32)
            delta_ref[hd] = _dot01(prod, ones, dot=_dot_nt, left=True)

    return pl.pallas_call(
        body, name="outproj_bwd", grid=(s // TM,),
        in_specs=[_rows(TM, D_MODEL), _rows(TM, D_MODEL), _full((1, D_MODEL)), _resident((wide + SSD_INNER, D_MODEL)),
                  _rows(TM, wide)],
        out_specs=[_rows(TM, D_MODEL), _rows(TM, wide), _rows(TM, SSD_INNER), _full((1, D_MODEL)),
                   pl.BlockSpec((MLA_HEADS, 8, TM), lambda i: (0, 0, i))],
        out_shape=[jax.ShapeDtypeStruct((s, D_MODEL), BF16), jax.ShapeDtypeStruct((s, wide), BF16),
                   jax.ShapeDtypeStruct((s, SSD_INNER), F32), jax.ShapeDtypeStruct((1, D_MODEL), F32),
                   jax.ShapeDtypeStruct((MLA_HEADS, 8, s), F32)],
        compiler_params=_params(),
    )(dh1, mixed, nw, wout, oe)


def _attn_bwd(q, k, v, do, lse, delta, exchange=()):
    s = q.shape[0]
    t = ATT_T
    nq = s // t
    pair = 2 * HEAD_PAD
    ne = len(exchange)

    def body(q_ref, k_ref, v_ref, do_ref, lse_ref, delta_ref, *rest):
        e_in, (dq_ref, dk_ref, dv_ref), e_out = rest[:ne], rest[ne:ne + 3], rest[ne + 3:2 * ne + 3]
        kb = pl.program_id(1)
        _hosted_comm("exchange", e_in, e_out, rest[2 * ne + 3:],
                     jnp.logical_and(pl.program_id(0) == 0, kb == 0),
                     jnp.logical_and(pl.program_id(0) == MLA_HEADS // 2 - 1, kb == nq - 1))

        @pl.when(kb == 0)
        def _():
            dq_ref[...] = jnp.zeros(dq_ref.shape, F32)

        dk_ref[...] = jnp.zeros(dk_ref.shape, F32)
        dv_ref[...] = jnp.zeros(dv_ref.shape, F32)

        def step(qb, masked):
            r0 = pl.multiple_of(qb * t, t)
            for hh in range(2):
                cols = slice(hh * HEAD_PAD, (hh + 1) * HEAD_PAD)
                kk = k_ref[:, cols]
                qq = q_ref[pl.ds(r0, t), cols]
                dd = do_ref[pl.ds(r0, t), cols]
                sc = _dot_nt(kk, qq) * ATT_SCALE_LOG2
                if masked:
                    sc = jnp.where(_chunk_mask(t, keys_on_rows=True), sc, -jnp.inf)
                p = jnp.exp2(sc - lse_ref[hh, 0:1, pl.ds(r0, t)])
                dv_ref[:, cols] += _dot(p.astype(BF16), dd)
                dp = _dot_nt(v_ref[:, cols], dd)
                ds = (p * (dp - delta_ref[hh, 0:1, pl.ds(r0, t)]) * ATT_SCALE).astype(BF16)
                dk_ref[:, cols] += _dot(ds, qq)
                dq_ref[pl.ds(r0, t), cols] += _dot_tn(ds, kk)

        def loop(qb, c):
            step(qb, False)
            return c

        step(kb, True)
        lax.fori_loop(kb + 1, nq, loop, 0)

    whole = pl.BlockSpec((s, pair), lambda h, i: (0, h))
    tile = pl.BlockSpec((t, pair), lambda h, i: (i, h))
    rowvec = pl.BlockSpec((2, 8, s), lambda h, i: (h, 0, 0))
    wide = MLA_HEADS * HEAD_PAD
    outs = pl.pallas_call(
        body, name="attn_bwd_exchange" if ne else "attn_bwd", grid=(MLA_HEADS // 2, nq),
        in_specs=[whole, tile, tile, whole, rowvec, rowvec] + [_ANY] * ne,
        out_specs=[whole, tile, tile] + [_ANY] * ne,
        out_shape=[jax.ShapeDtypeStruct((s, wide), F32)] * 3 + _comm_out_shapes("exchange", exchange),
        scratch_shapes=_comm_scratch(ne) if ne else [],
        compiler_params=_params(),
    )(q, k, v, do, lse, delta, *exchange)
    return outs[0], outs[1], outs[2], list(outs[3:])


def _ssd_bwd(dy, ypre, z, c, xraw, misc, prev, cw, dtb, a_exp, d_exp, nw, consts):
    s = dy.shape[0]
    nb = s // SSD_ROWS
    ncb = SSD_ROWS // CHUNK
    emisc, emisc_t, tri, trit = consts

    def body(dy_ref, ypre_ref, z_ref, c_ref, x_ref, xprev_ref, misc_ref, prev_ref, cw_ref, dtb_ref, a_ref, d_ref, nw_ref,
             emisc_ref, emisct_ref, tri_ref, trit_ref,
             dz_ref, dx_ref, dmisc_ref, gnw_ref, gd_ref, galog_ref, gdtb_ref, gcw_ref, gcb_ref,
             dst_s, dc_s, head_s):
        i = pl.program_id(0)
        first = i == 0

        @pl.when(first)
        def _():
            dst_s[...] = jnp.zeros(dst_s.shape, F32)
            head_s[...] = jnp.zeros(head_s.shape, F32)
            gnw_ref[...] = jnp.zeros(gnw_ref.shape, F32)
            gd_ref[...] = jnp.zeros(gd_ref.shape, F32)
            galog_ref[...] = jnp.zeros(galog_ref.shape, F32)
            gdtb_ref[...] = jnp.zeros(gdtb_ref.shape, F32)

        a_exp_v = a_ref[...]
        a8 = _dot01(a_exp_v, emisct_ref[...]) * (1.0 / SSD_P)

        def chunk(ci):
            r0 = ci * CHUNK
            cc = c_ref[pl.ds(r0, CHUNK), :]
            mm = misc_ref[pl.ds(r0, CHUNK), :]
            xa, dtr, dt, acs, acs_t, alast = _ssd_chunk_common(cc, mm, emisc_ref[...], tri_ref[...], trit_ref[...],
                                                              dtb_ref[...], a_exp_v)
            yield
            xs = xa[:, :SSD_INNER]
            xdt = xs * dt
            y = ypre_ref[pl.ds(r0, CHUNK), :]
            zz = z_ref[pl.ds(r0, CHUNK), :]
            yz, yn, rs = _gate_norm(y, zz, None)
            dyo = dy_ref[pl.ds(r0, CHUNK), :]
            gnw_ref[...] += jnp.sum(dyo * yn, axis=0, keepdims=True)
            dyn = dyo * nw_ref[...]
            half = SSD_INNER // SSD_GROUPS
            dyz_parts = []
            for g in range(SSD_GROUPS):
                gl = slice(g * half, (g + 1) * half)
                dyz_parts.append(rs[g] * (dyn[:, gl] - yn[:, gl] * jnp.mean(dyn[:, gl] * yn[:, gl], axis=-1, keepdims=True)))
            dyz = jnp.concatenate(dyz_parts, axis=1)
            sg = jax.nn.sigmoid(zz)
            dz_ref[pl.ds(r0, CHUNK), :] = dyz * y * (sg * (1.0 + zz * (1.0 - sg)))
            dyp = dyz * (zz * sg)
            dypb = dyp.astype(BF16)
            gd_ref[...] += jnp.sum(dyp * xs, axis=0, keepdims=True)
            yield
            prev = prev_ref[ci]
            cd = jnp.exp(alast)
            e = jnp.exp(acs)
            dsx = jnp.exp(alast - acs)
            wgt = (xdt * dsx).astype(BF16)
            dze = (dyp * e).astype(BF16)
            dprev_parts, diag_all, dbm, dcm, yoff_parts, bms = [], [], [], [], [], []
            lane8 = lax.broadcasted_iota(jnp.int32, (CHUNK, HEAD_PAD), 1)
            diag8 = jnp.zeros((CHUNK, HEAD_PAD), F32)
            for g in range(SSD_GROUPS):
                gl = slice(g * 256, (g + 1) * 256)
                bm = xa[:, SSD_INNER + g * SSD_N:SSD_INNER + (g + 1) * SSD_N].astype(BF16)
                cm = xa[:, SSD_INNER + SSD_GROUPS * SSD_N + g * SSD_N:SSD_INNER + SSD_GROUPS * SSD_N + (g + 1) * SSD_N].astype(BF16)
                bms.append(bm)
                prev_g = prev[:, gl].astype(BF16)
                dcm_g = _dot_nt(dze[:, gl], prev_g)
                dprev_parts.append(_dot_tn(cm, dze[:, gl]))
                cb_g = _dot_nt(cm, bm)
                dcb = jnp.zeros((CHUNK, CHUNK), F32)
                diag_parts = []
                for jj in range(2):
                    pair = 2 * g + jj
                    pl_ = slice(pair * 128, (pair + 1) * 128)
                    xp = xdt[:, pl_]
                    dyp_p = dypb[:, pl_]
                    dxp = jnp.zeros((CHUNK, 128), F32)
                    for hh in range(2):
                        hd = 2 * pair + hh
                        dec = _decay(acs, acs_t, hd)
                        xm = jnp.where(_half_mask(hh), xp, 0.0).astype(BF16)
                        dsc = _dot_nt(dyp_p, xm) * dec
                        dcb = dcb + dsc
                        sc = (cb_g * dec).astype(BF16)
                        dxp = dxp + jnp.where(_half_mask(hh), _dot_tn(sc, dyp_p), 0.0)
                        dm = dsc * cb_g
                        diag8 = diag8 + jnp.where(lane8 == MISC_DT + hd, jnp.sum(dm - dm.T, axis=1, keepdims=True), 0.0)
                    diag_parts.append(dxp)
                dcbb = dcb.astype(BF16)
                dcm.append(dcm_g + _dot(dcbb, bm))
                dbm.append(_dot_tn(dcbb, cm))
                diag_all.append(jnp.concatenate(diag_parts, axis=1))
                yoff_parts.append(_dot(cm, prev_g) * e[:, gl])
                yield
            dst = dst_s[...]
            glast = jnp.sum(dst * prev, axis=0, keepdims=True) * cd
            dxdt_state_parts = []
            for g in range(SSD_GROUPS):
                gl = slice(g * 256, (g + 1) * 256)
                dst_g = dst[:, gl].astype(BF16)
                dxdt_state_parts.append(_dot(bms[g], dst_g) * dsx[:, gl])
                dbm[g] = dbm[g] + _dot_nt(wgt[:, gl], dst_g)
            dst_s[...] = dst * cd + jnp.concatenate(dprev_parts, axis=1)
            yield
            dxdt_state = jnp.concatenate(dxdt_state_parts, axis=1)
            dxdt = jnp.concatenate(diag_all, axis=1) + dxdt_state
            dacs = dyp * jnp.concatenate(yoff_parts, axis=1) - xdt * dxdt_state
            last = jnp.sum(xdt * dxdt_state, axis=0, keepdims=True) + glast
            row = lax.broadcasted_iota(jnp.int32, (CHUNK, SSD_INNER), 0)
            dacs = dacs + jnp.where(row == CHUNK - 1, last, 0.0)
            dacs8 = _dot01(dacs, emisct_ref[...]) + diag8
            da8 = _dot01(dacs8, trit_ref[...], left=True)
            ddt8 = da8 * a8 + _dot01(dxdt * xs, emisct_ref[...])
            yield
            dtr8 = mm + _dot01(dtb_ref[...], emisct_ref[...]) * (1.0 / SSD_P)
            dt8 = jax.nn.softplus(dtr8)
            lane = lax.broadcasted_iota(jnp.int32, (CHUNK, HEAD_PAD), 1)
            on_dt = jnp.logical_and(lane >= MISC_DT, lane < MISC_DT + SSD_HEADS)
            ddtr8 = jnp.where(on_dt, ddt8 * jax.nn.sigmoid(dtr8), 0.0)
            dmisc_ref[pl.ds(r0, CHUNK), :] = ddtr8
            gdtb_ref[...] += jnp.sum(ddtr8, axis=0, keepdims=True)
            galog_ref[...] += jnp.sum(jnp.where(on_dt, da8 * dt8, 0.0), axis=0, keepdims=True) * a8
            dxs = d_ref[...] * dyp + dxdt * dt
            dxa = jnp.concatenate([dxs] + dbm + dcm, axis=1)
            sc_ = jax.nn.sigmoid(cc)
            dc_s[pl.ds(r0, CHUNK), :] = dxa * (sc_ * (1.0 + cc * (1.0 - sc_)))

        _interleave([chunk(ci) for ci in reversed(range(ncb))])

        dc = dc_s[...]
        dcext = jnp.concatenate([dc, head_s[...]], axis=0)
        dx = dc * cw_ref[CONV_W - 1:CONV_W, :]
        for j in range(1, CONV_W):
            dx = dx + pltpu.roll(dcext, SSD_ROWS + 8 - j, 0)[:SSD_ROWS, :] * cw_ref[CONV_W - 1 - j:CONV_W - j, :]
        dx_ref[...] = dx
        head_s[...] = dc[:8, :]
        xprev = jnp.where(i == nb - 1, 0.0, xprev_ref[...])
        xext = jnp.concatenate([xprev, x_ref[...]], axis=0)
        rows = [jnp.sum(dc * pltpu.roll(xext, CONV_W - 1 - kk, 0)[8:, :], axis=0, keepdims=True) for kk in range(CONV_W)]
        gcw = jnp.concatenate(rows, axis=0)

        @pl.when(first)
        def _():
            gcw_ref[...] = gcw
            gcb_ref[...] = jnp.sum(dc, axis=0, keepdims=True)

        @pl.when(jnp.logical_not(first))
        def _():
            gcw_ref[...] += gcw
            gcb_ref[...] += jnp.sum(dc, axis=0, keepdims=True)

    def rev(width):
        return pl.BlockSpec((SSD_ROWS, width), lambda i: (nb - 1 - i, 0))

    per8 = SSD_ROWS // 8
    return pl.pallas_call(
        body, name="ssd_bwd", grid=(nb,),
        in_specs=[rev(SSD_INNER), rev(SSD_INNER), rev(SSD_INNER), rev(CONV_DIM), rev(CONV_DIM),
                  pl.BlockSpec((8, CONV_DIM), lambda i: (jnp.maximum((nb - 1 - i) * per8 - 1, 0), 0)),
                  rev(HEAD_PAD), pl.BlockSpec((ncb, SSD_N, SSD_INNER), lambda i: (nb - 1 - i, 0, 0)),
                  _full((CONV_W, CONV_DIM)), _full((1, SSD_INNER)), _full((1, SSD_INNER)), _full((1, SSD_INNER)),
                  _full((1, SSD_INNER)), _full((HEAD_PAD, SSD_INNER)), _full((SSD_INNER, HEAD_PAD)), _full((CHUNK, CHUNK)),
                  _full((CHUNK, CHUNK))],
        out_specs=[rev(SSD_INNER), rev(CONV_DIM), rev(HEAD_PAD), _full((1, SSD_INNER)), _full((1, SSD_INNER)),
                   _full((1, HEAD_PAD)), _full((1, HEAD_PAD)), _full((CONV_W, CONV_DIM)), _full((1, CONV_DIM))],
        out_shape=[jax.ShapeDtypeStruct((s, SSD_INNER), F32), jax.ShapeDtypeStruct((s, CONV_DIM), F32),
                   jax.ShapeDtypeStruct((s, HEAD_PAD), F32), jax.ShapeDtypeStruct((1, SSD_INNER), F32),
                   jax.ShapeDtypeStruct((1, SSD_INNER), F32), jax.ShapeDtypeStruct((1, HEAD_PAD), F32),
                   jax.ShapeDtypeStruct((1, HEAD_PAD), F32), jax.ShapeDtypeStruct((CONV_W, CONV_DIM), F32),
                   jax.ShapeDtypeStruct((1, CONV_DIM), F32)],
        scratch_shapes=[pltpu.VMEM((SSD_N, SSD_INNER), F32), pltpu.VMEM((SSD_ROWS, CONV_DIM), F32), pltpu.VMEM((8, CONV_DIM), F32)],
        compiler_params=_params(),
    )(dy, ypre, z, c, xraw, xraw, misc, prev, cw, dtb, a_exp, d_exp, nw, emisc, emisc_t, tri, trit)


def _qkv_bwd(dq, dk, dv, cq, ckv, qnw, kvnw, wuq, wkv, cosf, sinf):
    s = dq.shape[0]
    wide = MLA_HEADS * HEAD_PAD

    def body(dq_ref, dk_ref, dv_ref, cq_ref, ckv_ref, qnw_ref, kvnw_ref, wuq_ref, wkv_ref, cos_ref, sin_ref,
             dqb_ref, dkvb_ref, dcq_ref, dckv_ref, dmisc_ref, gq_ref, gkv_ref):
        first = pl.program_id(0) == 0
        cosf, sinf = cos_ref[...], sin_ref[...]
        dkr = jnp.zeros((TM, HEAD_PAD), F32)
        for hd in range(MLA_HEADS):
            cols = slice(hd * HEAD_PAD, (hd + 1) * HEAD_PAD)
            dqb_ref[:, cols] = _rope(dq_ref[:, cols], cosf, sinf, -1.0).astype(BF16)
            dkh = dk_ref[:, cols]
            dkvb_ref[:, cols] = dkh.astype(BF16)
            dkr = dkr + dkh
        dkvb_ref[:, wide:] = dv_ref[...].astype(BF16)
        lane = lax.broadcasted_iota(jnp.int32, dkr.shape, 1)
        in_rope = jnp.logical_and(lane >= MISC_ROPE, lane < MISC_ROPE + QK_ROPE)
        dmisc_ref[...] = jnp.where(in_rope, _rope(jnp.where(in_rope, dkr, 0.0), cosf, sinf, -1.0), 0.0)
        dcq, gq = _rms_bwd(cq_ref[...], qnw_ref[...], _dot_nt(dqb_ref[...], wuq_ref[...]))
        dcq_ref[...] = dcq
        _acc_rows(gq_ref, gq, first)
        dckv, gkv = _rms_bwd(ckv_ref[...], kvnw_ref[...], _dot_nt(dkvb_ref[...], wkv_ref[...]))
        dckv_ref[...] = dckv
        _acc_rows(gkv_ref, gkv, first)

    return pl.pallas_call(
        body, name="qkv_bwd", grid=(s // TM,),
        in_specs=[_rows(TM, wide)] * 3 + [_rows(TM, Q_RANK), _rows(TM, KV_RANK), _full((1, Q_RANK)), _full((1, KV_RANK)),
                                          _resident((Q_RANK, wide)), _resident((KV_RANK, 2 * wide)), _rows(TM, HEAD_PAD), _rows(TM, HEAD_PAD)],
        out_specs=[_rows(TM, wide), _rows(TM, 2 * wide), _rows(TM, Q_RANK), _rows(TM, KV_RANK), _rows(TM, HEAD_PAD),
                   _full((1, Q_RANK)), _full((1, KV_RANK))],
        out_shape=[jax.ShapeDtypeStruct((s, wide), BF16), jax.ShapeDtypeStruct((s, 2 * wide), BF16),
                   jax.ShapeDtypeStruct((s, Q_RANK), F32), jax.ShapeDtypeStruct((s, KV_RANK), F32),
                   jax.ShapeDtypeStruct((s, HEAD_PAD), F32), jax.ShapeDtypeStruct((1, Q_RANK), F32),
                   jax.ShapeDtypeStruct((1, KV_RANK), F32)],
        compiler_params=_params(),
    )(dq, dk, dv, cq, ckv, qnw, kvnw, wuq, wkv, cosf, sinf)


def _inproj_bwd(dcq, dckv, dmisc_rope, dmisc_dt, dz, dxbc, h, dh1, nw, win):
    s = h.shape[0]

    def body(dcq_ref, dckv_ref, dma_ref, dmb_ref, dz_ref, dxbc_ref, h_ref, dh1_ref, nw_ref, w_ref, dproj_ref, dh0_ref, gnw_ref):
        dproj_ref[:, 0:768] = dcq_ref[...].astype(BF16)
        dproj_ref[:, 768:1024] = dckv_ref[...].astype(BF16)
        dproj_ref[:, 1024:1152] = (dma_ref[...] + dmb_ref[...]).astype(BF16)
        dproj_ref[:, 1152:1664] = dz_ref[...].astype(BF16)
        dproj_ref[:, 1664:2688] = dxbc_ref[...].astype(BF16)
        du = _dot_nt(dproj_ref[...], w_ref[...])
        dx, gnw = _rms_bwd(h_ref[...], nw_ref[...], du)
        _acc_rows(gnw_ref, gnw, pl.program_id(0) == 0)
        dh0_ref[...] = dh1_ref[...] + dx

    return pl.pallas_call(
        body, name="inproj_bwd", grid=(s // TM,),
        in_specs=[_rows(TM, Q_RANK), _rows(TM, KV_RANK), _rows(TM, HEAD_PAD), _rows(TM, HEAD_PAD), _rows(TM, SSD_INNER),
                  _rows(TM, CONV_DIM), _rows(TM, D_MODEL), _rows(TM, D_MODEL), _full((1, D_MODEL)), _resident((D_MODEL, IN_PAD))],
        out_specs=[_rows(TM, IN_PAD), _rows(TM, D_MODEL), _full((1, D_MODEL))],
        out_shape=[jax.ShapeDtypeStruct((s, IN_PAD), BF16), jax.ShapeDtypeStruct((s, D_MODEL), F32),
                   jax.ShapeDtypeStruct((1, D_MODEL), F32)],
        compiler_params=_params(),
    )(dcq, dckv, dmisc_rope, dmisc_dt, dz, dxbc, h, dh1, nw, win)


def _row_tile(rows, cols):
    cap = max(8, (1 << 18) // max(cols, 128))
    best = None
    for t in range(8, rows + 1, 8):
        if rows % t == 0 and t <= cap:
            best = t
    return best if best is not None else rows


def _adamw(w, g, m, v, name):
    rows, cols = w.shape
    tr = _row_tile(rows, cols)

    def body(w_ref, g_ref, m_ref, v_ref, d_ref, m2_ref, v2_ref):
        gg = g_ref[...]
        m2 = ADAM_B1 * m_ref[...] + (1.0 - ADAM_B1) * gg
        v2 = ADAM_B2 * v_ref[...] + (1.0 - ADAM_B2) * jnp.square(gg)
        m_hat = m2 / (1.0 - ADAM_B1 ** ADAM_STEP)
        v_hat = v2 / (1.0 - ADAM_B2 ** ADAM_STEP)
        d_ref[...] = -ADAM_LR * (m_hat / (jnp.sqrt(v_hat) + ADAM_EPS) + ADAM_WD * w_ref[...])
        m2_ref[...] = m2
        v2_ref[...] = v2

    spec = pl.BlockSpec((tr, cols), lambda i: (i, 0))
    return pl.pallas_call(
        body, name=name, grid=(rows // tr,),
        in_specs=[spec] * 4, out_specs=[spec] * 3,
        out_shape=[jax.ShapeDtypeStruct((rows, cols), F32)] * 3,
    )(w, g, m, v)


def _sum_adamw(slots, w, m, v, name):
    _, rows, cols = w.shape
    tr = _row_tile(rows, cols)
    nb = rows // tr

    def body(s0_ref, s1_ref, w_ref, m_ref, v_ref, g_ref, d_ref, m2_ref, v2_ref):
        for l, ref in enumerate((s0_ref, s1_ref)):
            @pl.when(pl.program_id(0) == l)
            def _(ref=ref):
                acc = ref[0].astype(F32)
                for i in range(1, N_DEV):
                    acc = acc + ref[i].astype(F32)
                g_ref[...] = acc

        gg = g_ref[...]
        m2 = ADAM_B1 * m_ref[...] + (1.0 - ADAM_B1) * gg
        v2 = ADAM_B2 * v_ref[...] + (1.0 - ADAM_B2) * jnp.square(gg)
        m_hat = m2 / (1.0 - ADAM_B1 ** ADAM_STEP)
        v_hat = v2 / (1.0 - ADAM_B2 ** ADAM_STEP)
        d_ref[...] = -ADAM_LR * (m_hat / (jnp.sqrt(v_hat) + ADAM_EPS) + ADAM_WD * w_ref[...])
        m2_ref[...] = m2
        v2_ref[...] = v2

    slot_spec = lambda layer: pl.BlockSpec((N_DEV, tr, cols), lambda l, i: (0, jnp.where(l == layer, i, (nb - 1) * (1 - layer)), 0))
    spec = pl.BlockSpec((None, tr, cols), lambda l, i: (l, i, 0))
    return pl.pallas_call(
        body, name=name, grid=(DEPTH, nb),
        in_specs=[slot_spec(0), slot_spec(1), spec, spec, spec], out_specs=[spec] * 4,
        out_shape=[jax.ShapeDtypeStruct(w.shape, F32)] * 4,
        compiler_params=_params(),
    )(slots[0], slots[1], w, m, v)


_MESH = pl.DeviceIdType.MESH
_ANY = pl.BlockSpec(memory_space=pl.ANY)


def _my_place():
    return lax.axis_index("x"), lax.axis_index("y"), lax.axis_index("c")


def _flip(place, k):
    x, y, c = place
    return (1 - x if k & 4 else x, 1 - y if k & 2 else y, 1 - c if k & 1 else c)


def _block_id(place):
    return 4 * place[0] + 2 * place[1] + place[2]


def _peer_copies(kind, in_refs, out_refs, send_sems, recv_sems, local_sems):
    me = _my_place()
    my = _block_id(me)
    remote, local = [], []
    for a, (x_ref, out_ref) in enumerate(zip(in_refs, out_refs)):
        src_of = (lambda place, r=x_ref: r) if kind == "gather" else (lambda place, r=x_ref: r.at[_block_id(place)])
        local.append(pltpu.make_async_copy(src_of(me), out_ref.at[my], local_sems.at[a]))
        for k in range(1, N_DEV):
            peer = _flip(me, k)
            remote.append(pltpu.make_async_remote_copy(
                src_ref=src_of(peer), dst_ref=out_ref.at[my], send_sem=send_sems.at[a * 7 + k - 1],
                recv_sem=recv_sems.at[a * 7 + k - 1], device_id=peer, device_id_type=_MESH))
    return remote, local


def _comm_out_shapes(kind, arrays):
    return [jax.ShapeDtypeStruct((N_DEV, *a.shape) if kind == "gather" else a.shape, a.dtype) for a in arrays]


def _comm_scratch(n):
    return [pltpu.SemaphoreType.DMA((7 * n,)), pltpu.SemaphoreType.DMA((7 * n,)), pltpu.SemaphoreType.DMA((n,))]


def _hosted_comm(kind, in_refs, out_refs, sems, first, last):
    if not in_refs:
        return

    @pl.when(first)
    def _():
        remote, local = _peer_copies(kind, in_refs, out_refs, *sems)
        for cp in local + remote:
            cp.start()

    @pl.when(last)
    def _():
        remote, local = _peer_copies(kind, in_refs, out_refs, *sems)
        for cp in remote:
            cp.wait()
        for cp in local:
            cp.wait()


def _gather_two_level(arrays, name):
    n = len(arrays)

    def body(*refs):
        in_refs, out_refs = refs[:n], refs[n:2 * n]
        send_sems, recv_sems, local_sems = refs[2 * n:]
        me = _my_place()
        x, y, c = me
        sibling = (x, y, 1 - c)
        chips = [(1 - x, y), (x, 1 - y), (1 - x, 1 - y)]

        def copy(a, k, place, to, src=None):
            block = out_refs[a].at[_block_id(place)]
            return pltpu.make_async_remote_copy(
                src_ref=block if src is None else src, dst_ref=block, send_sem=send_sems.at[7 * a + k],
                recv_sem=recv_sems.at[7 * a + k], device_id=to, device_id_type=_MESH)

        mine = [pltpu.make_async_copy(in_refs[a], out_refs[a].at[_block_id(me)], local_sems.at[a]) for a in range(n)]
        first = [copy(a, 0, me, sibling, src=in_refs[a]) for a in range(n)]
        first += [copy(a, 1 + j, me, (*chip, c), src=in_refs[a]) for a in range(n) for j, chip in enumerate(chips)]
        for cp in mine + first:
            cp.start()
        passed = []
        for a in range(n):
            for j, chip in enumerate(chips):
                copy(a, 1 + j, (*chip, c), me).wait_recv()
                passed.append(copy(a, 4 + j, (*chip, c), sibling))
                passed[-1].start()
        for a in range(n):
            copy(a, 0, sibling, me).wait_recv()
            for j, chip in enumerate(chips):
                copy(a, 4 + j, (*chip, 1 - c), me).wait_recv()
        for cp in first + passed:
            cp.wait_send()
        for cp in mine:
            cp.wait()

    return pl.pallas_call(
        body, name=name, out_shape=_comm_out_shapes("gather", arrays),
        in_specs=[_ANY] * n, out_specs=[_ANY] * n, scratch_shapes=_comm_scratch(n),
    )(*arrays)


def _comm(kind, arrays, name):
    n = len(arrays)

    def body(*refs):
        remote, local = _peer_copies(kind, refs[:n], refs[n:2 * n], *refs[2 * n:])
        for cp in local + remote:
            cp.start()
        for cp in remote:
            cp.wait()
        for cp in local:
            cp.wait()

    return pl.pallas_call(
        body, name=name, out_shape=_comm_out_shapes(kind, arrays),
        in_specs=[_ANY] * n, out_specs=[_ANY] * n, scratch_shapes=_comm_scratch(n),
    )(*arrays)


def _all_reduce_small(part):
    rows, lanes = part.shape
    vmem = pl.BlockSpec(memory_space=pltpu.VMEM)

    def body(x_ref, gath_ref, sum_ref, send_sems, recv_sems):
        me = _my_place()
        my = _block_id(me)
        gath_ref[my] = x_ref[...]
        copies = []
        for k in range(1, N_DEV):
            cp = pltpu.make_async_remote_copy(
                src_ref=x_ref, dst_ref=gath_ref.at[my], send_sem=send_sems.at[k - 1], recv_sem=recv_sems.at[k - 1],
                device_id=_flip(me, k), device_id_type=_MESH)
            cp.start()
            copies.append(cp)
        for cp in copies:
            cp.wait()
        acc = gath_ref[0]
        for i in range(1, N_DEV):
            acc = acc + gath_ref[i]
        sum_ref[...] = acc

    return pl.pallas_call(
        body, name="small_grad_all_reduce",
        out_shape=[jax.ShapeDtypeStruct((N_DEV, rows, lanes), F32), jax.ShapeDtypeStruct((rows, lanes), F32)],
        in_specs=[vmem], out_specs=[vmem, vmem],
        scratch_shapes=[pltpu.SemaphoreType.DMA((7,)), pltpu.SemaphoreType.DMA((7,))],
    )(part)[1]


_SHARDED = (("w_in", (D_MODEL, IN_PROJ // N_DEV)), ("w_uq", (Q_RANK // N_DEV, Q_RANK)), ("w_ukv", (KV_RANK, HEAD_PAD)),
            ("conv_w", (CONV_W, CONV_DIM // N_DEV)), ("w_out", (D_MODEL // N_DEV, D_MODEL)),
            ("w_up", (D_MODEL, D_FF // N_DEV)), ("w_down", (D_FF // N_DEV, D_MODEL)))
_SMALL = (("pre_mix_norm", D_MODEL), ("q_norm", Q_RANK), ("kv_norm", KV_RANK), ("conv_b", CONV_DIM), ("dt_bias", SSD_HEADS),
          ("a_log", SSD_HEADS), ("d_skip", SSD_HEADS), ("ssd_norm", SSD_INNER), ("post_mix_norm", D_MODEL),
          ("pre_mlp_norm", D_MODEL), ("post_mlp_norm", D_MODEL))
_WEIGHT_ORDER = ("pre_mix_norm", "w_in", "q_norm", "w_uq", "kv_norm", "w_ukv", "conv_w", "conv_b", "dt_bias", "a_log", "d_skip",
                 "ssd_norm", "w_out", "post_mix_norm", "pre_mlp_norm", "w_up", "w_down", "post_mlp_norm")
_EARLY = ("w_in", "w_uq", "w_ukv", "conv_w")
_LATE = ("w_out", "w_up", "w_down")


def _wire_shard(name, a):
    return lax.bitcast_convert_type(a, BF16).reshape(CONV_W, -1) if name == "conv_w" else a.astype(BF16)


def _from_wire(name, g):
    return lax.bitcast_convert_type(g.reshape(N_DEV, CONV_W, -1, 2), F32) if name == "conv_w" else g


def _cols(stacked):
    return jnp.transpose(stacked, (1, 0, 2)).reshape(stacked.shape[1], -1)


def _early_weights(sh):
    w_in = _cols(sh["w_in"])
    zeros = lambda n: jnp.zeros((D_MODEL, n), BF16)
    s1, s2, s3, s4, s5 = 768, 1024, 1056, 1568, 2592
    win = jnp.concatenate([w_in[:, :s2], zeros(MISC_ROPE), w_in[:, s2:s3], w_in[:, s5:], zeros(HEAD_PAD - MISC_DT - SSD_HEADS),
                           w_in[:, s3:s5]], axis=1)
    w_uq = sh["w_uq"].reshape(Q_RANK, MLA_HEADS, QK_NOPE + QK_ROPE)
    wuq = jnp.pad(w_uq, ((0, 0), (0, 0), (0, HEAD_PAD - QK_NOPE - QK_ROPE))).reshape(Q_RANK, -1)
    w_ukv = _cols(sh["w_ukv"]).reshape(KV_RANK, MLA_HEADS, QK_NOPE + V_DIM)
    wkn = jnp.pad(w_ukv[..., :QK_NOPE], ((0, 0), (0, 0), (0, HEAD_PAD - QK_NOPE))).reshape(KV_RANK, -1)
    wv = w_ukv[..., QK_NOPE:].reshape(KV_RANK, 4, 2, 1, V_DIM) * jnp.eye(2, dtype=BF16).reshape(1, 1, 2, 2, 1)
    wkv = jnp.concatenate([wkn, wv.reshape(KV_RANK, -1)], axis=1)
    return dict(win=win, wuq=wuq, wkv=wkv, conv_w=_cols(sh["conv_w"]))


def _late_weights(sh):
    w_out = sh["w_out"].reshape(D_MODEL, D_MODEL)
    watt = w_out[:SSD_INNER].reshape(4, 2, 1, V_DIM, D_MODEL) * jnp.eye(2, dtype=BF16).reshape(1, 2, 2, 1, 1)
    wout = jnp.concatenate([watt.reshape(MLA_HEADS * HEAD_PAD, D_MODEL), w_out[SSD_INNER:]], axis=0)
    return dict(wout=wout, wup=sh["w_up"], wdown=sh["w_down"])


def _shard_grads(g):
    out = {}
    if "wup" in g:
        out["w_up"], out["w_down"] = g["wup"], g["wdown"]
        ae = g["wout_att"].reshape(4, 2, 2, V_DIM, D_MODEL)
        att = jnp.stack([ae[:, 0, 0], ae[:, 1, 1]], axis=1).reshape(SSD_INNER, D_MODEL)
        out["w_out"] = jnp.concatenate([att, g["wout_ssd"]], axis=0).astype(BF16).reshape(N_DEV, D_MODEL // N_DEV, D_MODEL)
    if "win" not in g:
        return out
    dwin = g["win"]
    s1, s2 = 768, 1024
    m0 = s2
    w_in = jnp.concatenate([dwin[:, :s2], dwin[:, m0 + MISC_ROPE:m0 + MISC_ROPE + QK_ROPE], dwin[:, 1152:2688],
                            dwin[:, m0 + MISC_DT:m0 + MISC_DT + SSD_HEADS]], axis=1)
    out["w_in"] = jnp.transpose(w_in.astype(BF16).reshape(D_MODEL, N_DEV, -1), (1, 0, 2))
    w_uq = g["wuq"].astype(BF16).reshape(Q_RANK, MLA_HEADS, HEAD_PAD)[..., :QK_NOPE + QK_ROPE].reshape(Q_RANK, Q_RANK)
    out["w_uq"] = w_uq.reshape(N_DEV, Q_RANK // N_DEV, Q_RANK)
    wide = MLA_HEADS * HEAD_PAD
    wkv = g["wkv"].astype(BF16)
    kn = wkv[:, :wide].reshape(KV_RANK, MLA_HEADS, HEAD_PAD)[..., :QK_NOPE]
    ve = wkv[:, wide:].reshape(KV_RANK, 4, 2, 2, V_DIM)
    vv = jnp.stack([ve[:, :, 0, 0], ve[:, :, 1, 1]], axis=2).reshape(KV_RANK, MLA_HEADS, V_DIM)
    out["w_ukv"] = jnp.transpose(jnp.concatenate([kn, vv], axis=-1), (1, 0, 2))
    out["conv_w"] = jnp.transpose(g["conv_w"].astype(BF16).reshape(CONV_W, N_DEV, -1), (1, 0, 2))
    return out


def _small_rows(n):
    return -(-n // 1024) * 8


def _pack_small(vals):
    rows = []
    for l in range(DEPTH):
        for name, n in _SMALL:
            r = _small_rows(n)
            rows.append(jnp.pad(vals[name][l].reshape(-1), (0, r * 128 - n)).reshape(r, 128))
    return jnp.concatenate(rows, axis=0)


def _unpack_small(packed):
    out, off = {name: [] for name, _ in _SMALL}, 0
    for l in range(DEPTH):
        for name, n in _SMALL:
            r = _small_rows(n)
            out[name].append(packed[off:off + r].reshape(-1)[:n])
            off += r
    return {name: jnp.stack(v) for name, v in out.items()}


def _lane_rows(vec8):
    return jnp.repeat(vec8, SSD_P).reshape(1, SSD_INNER)


def _layer_fwd(h, kw, sm, l, cosf, sinf, consts, gather=(), after_gather=None, target=None):
    row = lambda name: sm[name][l].reshape(1, -1)
    t = {}
    t["h0"] = h
    t["ub"], t["cq"], t["ckv"], t["misc"], t["z"], t["xraw"] = _inproj_fwd(h, row("pre_mix_norm"), kw["win"])
    t["cqn"], t["ckvn"], t["q"], t["k"], t["v"] = _qkv_fwd(t["cq"], t["ckv"], t["misc"], row("q_norm"), row("kv_norm"),
                                                         kw["wuq"], kw["wkv"], cosf, sinf)
    t["oe"], t["lse"], gathered = _attn_fwd(t["q"], t["k"], t["v"], gather)
    if after_gather is not None:
        after_gather(gathered)
    t["dtb"] = _lane_rows(sm["dt_bias"][l])
    t["a_exp"] = _lane_rows(-jnp.exp(sm["a_log"][l]))
    t["d_exp"] = _lane_rows(sm["d_skip"][l])
    t["c"], t["prev"], t["ypre"], t["yssd"] = _ssd_fwd(t["xraw"], t["misc"], t["z"], kw["conv_w"], row("conv_b"), t["dtb"],
                                                     t["a_exp"], t["d_exp"], row("ssd_norm"), consts)
    t["mixed"], t["h1"] = _outproj_fwd(t["oe"], t["yssd"], kw["wout"], h, row("post_mix_norm"))
    t["mb"], t["d"], *out = _mlp_fwd(t["h1"], row("pre_mlp_norm"), kw["wup"], kw["wdown"], row("post_mlp_norm"), target)
    return out, t


def _layer_bwd(dh2, t, kw, sm, l, cosf, sinf, consts, exchange_of=None):
    row = lambda name: sm[name][l].reshape(1, -1)
    g, gs = {}, {}
    dh1, dab, rb, ddb, gs["post_mlp_norm"], gs["pre_mlp_norm"] = _mlp_bwd(
        dh2, t["d"], t["h1"], t["mb"], row("pre_mlp_norm"), kw["wup"], kw["wdown"], row("post_mlp_norm"))
    g["wup"] = _matmul_tn_stacked(t["mb"], dab, f"dw_up_{l}", a_stacked=False)
    g["wdown"] = _matmul_tn_stacked(rb, ddb, f"dw_down_{l}", a_stacked=True)
    dmixb, doe, dyssd, gs["post_mix_norm"], delta = _outproj_bwd(dh1, t["mixed"], row("post_mix_norm"), kw["wout"], t["oe"])
    g["wout_att"] = _matmul_tn(t["oe"], dmixb, f"dw_out_att_{l}")
    g["wout_ssd"] = _matmul_tn(t["yssd"], dmixb, f"dw_out_ssd_{l}")
    dz, dxraw, dmisc_dt, gs["ssd_norm"], gd, galog, gdtb, g["conv_w"], gs["conv_b"] = _ssd_bwd(
        dyssd, t["ypre"], t["z"], t["c"], t["xraw"], t["misc"], t["prev"], kw["conv_w"], t["dtb"], t["a_exp"], t["d_exp"],
        row("ssd_norm"), consts)
    gs["d_skip"] = jnp.sum(gd.reshape(SSD_HEADS, SSD_P), axis=1)
    gs["a_log"] = galog[0, MISC_DT:MISC_DT + SSD_HEADS]
    gs["dt_bias"] = gdtb[0, MISC_DT:MISC_DT + SSD_HEADS]
    dq, dk, dv, exchanged = _attn_bwd(t["q"], t["k"], t["v"], doe, t["lse"], delta,
                                      exchange_of(g) if exchange_of is not None else ())
    dqb, dkvb, dcq, dckv, dmisc_rope, gs["q_norm"], gs["kv_norm"] = _qkv_bwd(
        dq, dk, dv, t["cq"], t["ckv"], row("q_norm"), row("kv_norm"), kw["wuq"], kw["wkv"], cosf, sinf)
    g["wuq"] = _matmul_tn(t["cqn"], dqb, f"dw_uq_{l}")
    g["wkv"] = _matmul_tn(t["ckvn"], dkvb, f"dw_kv_{l}")
    dprojb, dh0, gs["pre_mix_norm"] = _inproj_bwd(dcq, dckv, dmisc_rope, dmisc_dt, dz, dxraw, t["h0"], dh1,
                                                  row("pre_mix_norm"), kw["win"])
    g["win"] = _matmul_tn(t["ub"], dprojb, f"dw_in_{l}")
    return dh0, g, {k: v.reshape(-1) for k, v in gs.items()}, exchanged


def _local_step(x, positions, kws, sm, target, gather=(), after_gather=None, exchange_of=None):
    inv_freq = ROPE_THETA ** (-jnp.arange(0, QK_ROPE, 2, dtype=F32) / QK_ROPE)
    invf = jnp.zeros((HEAD_PAD,), F32).at[MISC_ROPE:MISC_ROPE + QK_ROPE].set(jnp.concatenate([inv_freq, inv_freq]))
    cosf, sinf = _rope_tables(positions.reshape(-1, 1), invf.reshape(1, HEAD_PAD))
    consts = _ssd_consts()
    (h,), t0 = _layer_fwd(x, kws[0], sm, 0, cosf, sinf, consts, gather, after_gather)
    (dh, loss), t1 = _layer_fwd(h, kws[1], sm, 1, cosf, sinf, consts, target=target)
    saved = [t0, t1]
    grads, small, exchanged = [None] * DEPTH, [None] * DEPTH, []
    for l in reversed(range(DEPTH)):
        hook = (lambda g0: exchange_of(g0, grads[1])) if (l == 0 and exchange_of is not None) else None
        dh, grads[l], small[l], got = _layer_bwd(dh, saved[l], kws[l], sm, l, cosf, sinf, consts, hook)
        exchanged = got or exchanged
    return loss[0, 0], dh, grads, small, exchanged


def kernel(x, positions, pre_mix_norm, w_in, q_norm, w_uq, kv_norm, w_ukv, conv_w, conv_b, dt_bias, a_log, d_skip, ssd_norm, w_out, post_mix_norm, pre_mlp_norm, w_up, w_down, post_mlp_norm, loss_target, m_pre_mix_norm, m_w_in, m_q_norm, m_w_uq, m_kv_norm, m_w_ukv, m_conv_w, m_conv_b, m_dt_bias, m_a_log, m_d_skip, m_ssd_norm, m_w_out, m_post_mix_norm, m_pre_mlp_norm, m_w_up, m_w_down, m_post_mlp_norm, v_pre_mix_norm, v_w_in, v_q_norm, v_w_uq, v_kv_norm, v_w_ukv, v_conv_w, v_conv_b, v_dt_bias, v_a_log, v_d_skip, v_ssd_norm, v_w_out, v_post_mix_norm, v_pre_mlp_norm, v_w_up, v_w_down, v_post_mlp_norm):
    w = dict(pre_mix_norm=pre_mix_norm, w_in=w_in, q_norm=q_norm, w_uq=w_uq, kv_norm=kv_norm, w_ukv=w_ukv, conv_w=conv_w,
             conv_b=conv_b, dt_bias=dt_bias, a_log=a_log, d_skip=d_skip, ssd_norm=ssd_norm, w_out=w_out,
             post_mix_norm=post_mix_norm, pre_mlp_norm=pre_mlp_norm, w_up=w_up, w_down=w_down, post_mlp_norm=post_mlp_norm)
    m = dict(pre_mix_norm=m_pre_mix_norm, w_in=m_w_in, q_norm=m_q_norm, w_uq=m_w_uq, kv_norm=m_kv_norm, w_ukv=m_w_ukv,
             conv_w=m_conv_w, conv_b=m_conv_b, dt_bias=m_dt_bias, a_log=m_a_log, d_skip=m_d_skip, ssd_norm=m_ssd_norm,
             w_out=m_w_out, post_mix_norm=m_post_mix_norm, pre_mlp_norm=m_pre_mlp_norm, w_up=m_w_up, w_down=m_w_down,
             post_mlp_norm=m_post_mlp_norm)
    v = dict(pre_mix_norm=v_pre_mix_norm, w_in=v_w_in, q_norm=v_q_norm, w_uq=v_w_uq, kv_norm=v_kv_norm, w_ukv=v_w_ukv,
             conv_w=v_conv_w, conv_b=v_conv_b, dt_bias=v_dt_bias, a_log=v_a_log, d_skip=v_d_skip, ssd_norm=v_ssd_norm,
             w_out=v_w_out, post_mix_norm=v_post_mix_norm, pre_mlp_norm=v_pre_mlp_norm, w_up=v_w_up, w_down=v_w_down,
             post_mlp_norm=v_post_mlp_norm)
    sm = {name: w[name] for name, _ in _SMALL}

    wire = lambda name, l: _wire_shard(name, w[name][l])
    first = _gather_two_level([wire(name, 0) for name in _EARLY], "weight_gather_first")
    kws = [_early_weights({name: _from_wire(name, a) for name, a in zip(_EARLY, first)}), None]
    behind = [(name, 0) for name in _LATE] + [(name, 1) for name, _ in _SHARDED]

    def after_gather(gathered):
        got = {key: _from_wire(key[0], a) for key, a in zip(behind, gathered)}
        kws[0].update(_late_weights({name: got[name, 0] for name in _LATE}))
        kws[1] = {**_early_weights({name: got[name, 1] for name in _EARLY}),
                  **_late_weights({name: got[name, 1] for name in _LATE})}

    sent_behind = [(name, 1) for name, _ in _SHARDED] + [(name, 0) for name in _LATE]

    def exchange_of(g0, g1):
        blocks = {**{(name, 1): a for name, a in _shard_grads(g1).items()},
                  **{(name, 0): a for name, a in _shard_grads(g0).items()}}
        return [blocks[key] for key in sent_behind]

    loss_part, dx, grads, small, exchanged = _local_step(
        x[0], positions[0], kws, sm, loss_target[0], [wire(*key) for key in behind], after_gather, exchange_of)
    slots = dict(zip(sent_behind, exchanged))
    last = _shard_grads({k: grads[0][k] for k in ("win", "wuq", "wkv", "conv_w")})
    slots.update({(name, 0): a for name, a in zip(_EARLY, _comm("exchange", [last[name] for name in _EARLY], "grad_exchange_last"))})
    g_small = _unpack_small(_all_reduce_small(_pack_small({name: jnp.stack([small[l][name] for l in range(DEPTH)])
                                                           for name, _ in _SMALL})))
    loss = lax.psum(loss_part, ("x", "y", "c"))

    grad, delta, new_m, new_v = {}, {}, {}, {}
    for name, _ in _SHARDED:
        grad[name], delta[name], new_m[name], new_v[name] = _sum_adamw(
            [slots[name, 0], slots[name, 1]], w[name], m[name], v[name], f"sum_adamw_{name}")
    pk = lambda d: _pack_small({name: d[name] for name, _ in _SMALL})
    d_, m_, v_ = _adamw(pk(w), pk(g_small), pk(m), pk(v), "adamw_small")
    for dst, packed in ((delta, d_), (new_m, m_), (new_v, v_)):
        dst.update(_unpack_small(packed))
    grad.update(g_small)

    outs = [loss, dx[None]]
    for d in (grad, delta, new_m, new_v):
        outs += [d[name] for name in _WEIGHT_ORDER]
    return tuple(outs)
```

```python
import jax
import jax.numpy as jnp
import numpy as np
from jax import lax
from jax.experimental import pallas as pl
from jax.experimental.pallas import tpu as pltpu

F32 = jnp.float32
BF16 = jnp.bfloat16
HI = lax.Precision.HIGHEST

D_MODEL = 1024
DEPTH = 2
N_DEV = 8
CHUNK = 64
EPS = 1e-6
MLA_HEADS = 8
QK_NOPE = 64
QK_ROPE = 32
V_DIM = 64
Q_RANK = 768
KV_RANK = 256
ROPE_THETA = 10000.0
SSD_HEADS = 8
SSD_P = 64
SSD_INNER = 512
SSD_GROUPS = 2
SSD_N = 128
CONV_W = 4
CONV_DIM = 1024
D_FF = 4096
IN_PROJ = 2600
HEAD_PAD = 128
IN_PAD = 2688
MISC_ROPE = 64
MISC_DT = 96
ATT_SCALE = (QK_NOPE + QK_ROPE) ** -0.5
LOG2E = 1.4426950408889634
ATT_SCALE_LOG2 = ATT_SCALE * LOG2E

ADAM_LR = 0.001
ADAM_B1 = 0.9
ADAM_B2 = 0.999
ADAM_EPS = 1e-08
ADAM_WD = 0.01
ADAM_STEP = 10

TM = 512
TQ = 256
ATT_T = 512
ATT_G = 4
SSD_ROWS = 256
TK_DW = 4096
VMEM_LIMIT = 56 * 1024 * 1024

_NT = (((1,), (1,)), ((), ()))
_TN = (((0,), (0,)), ((), ()))


def _params(**kw):
    return pltpu.CompilerParams(vmem_limit_bytes=VMEM_LIMIT, **kw)


def _dot(a, b, precision=None):
    return jnp.dot(a, b, preferred_element_type=F32, precision=precision)


def _dot_nt(a, b, precision=None):
    return lax.dot_general(a, b, _NT, preferred_element_type=F32, precision=precision)


def _dot_tn(a, b, precision=None):
    return lax.dot_general(a, b, _TN, preferred_element_type=F32, precision=precision)


def _split3(x):
    hi = x.astype(BF16)
    r = x - hi.astype(F32)
    mid = r.astype(BF16)
    return hi, mid, (r - mid.astype(F32)).astype(BF16)


def _dot01(x, m01, dot=_dot, left=False):
    parts = [dot(m01, p) if left else dot(p, m01) for p in _split3(x)]
    return parts[0] + parts[1] + parts[2]


def _full(shape):
    n = len(shape)
    return pl.BlockSpec(shape, lambda *_: (0,) * n)


def _resident(shape):
    n = len(shape)
    return pl.BlockSpec(shape, lambda *_: (0,) * n, pipeline_mode=pl.Buffered(1))


def _rows(tm, width):
    return pl.BlockSpec((tm, width), lambda i: (i, 0))


def _rms_fwd(x, w):
    r = lax.rsqrt(jnp.mean(x * x, axis=-1, keepdims=True) + EPS)
    return (x * r) * w


def _rms_bwd(x, w, dy):
    r = lax.rsqrt(jnp.mean(x * x, axis=-1, keepdims=True) + EPS)
    xh = x * r
    dxn = dy * w
    dx = r * (dxn - xh * jnp.mean(dxn * xh, axis=-1, keepdims=True))
    return dx, dy * xh


def _acc_rows(ref, val, first):
    s = jnp.sum(val, axis=0, keepdims=True)

    @pl.when(first)
    def _():
        ref[...] = s

    @pl.when(jnp.logical_not(first))
    def _():
        ref[...] += s


def _rope(t, cosf, sinf, sign):
    lane = lax.broadcasted_iota(jnp.int32, t.shape, 1)
    rot = jnp.where(lane < MISC_ROPE + QK_ROPE // 2, -pltpu.roll(t, HEAD_PAD - QK_ROPE // 2, 1), pltpu.roll(t, QK_ROPE // 2, 1))
    return t * cosf + sign * (rot * sinf)


def _rope_tables(pos, invf):
    s = pos.shape[0]

    def body(pos_ref, invf_ref, cos_ref, sin_ref):
        ang = pos_ref[...].astype(F32) * invf_ref[...]
        cos_ref[...] = jnp.cos(ang)
        sin_ref[...] = jnp.sin(ang)

    return pl.pallas_call(
        body, name="rope_tables", grid=(s // TM,),
        in_specs=[_rows(TM, 1), _full((1, HEAD_PAD))],
        out_specs=[_rows(TM, HEAD_PAD), _rows(TM, HEAD_PAD)],
        out_shape=[jax.ShapeDtypeStruct((s, HEAD_PAD), F32)] * 2,
    )(pos, invf)


def _inproj_fwd(h, nw, win):
    s = h.shape[0]

    def body(h_ref, nw_ref, w_ref, ub_ref, cq_ref, ckv_ref, misc_ref, z_ref, xbc_ref):
        ub = _rms_fwd(h_ref[...], nw_ref[...]).astype(BF16)
        ub_ref[...] = ub
        proj = _dot(ub, w_ref[...])
        cq_ref[...] = proj[:, 0:768]
        ckv_ref[...] = proj[:, 768:1024]
        misc_ref[...] = proj[:, 1024:1152]
        z_ref[...] = proj[:, 1152:1664]
        xbc_ref[...] = proj[:, 1664:2688]

    widths = (768, 256, 128, 512, 1024)
    return pl.pallas_call(
        body, name="inproj_fwd", grid=(s // TM,),
        in_specs=[_rows(TM, D_MODEL), _full((1, D_MODEL)), _resident((D_MODEL, IN_PAD))],
        out_specs=[_rows(TM, D_MODEL)] + [_rows(TM, w) for w in widths],
        out_shape=[jax.ShapeDtypeStruct((s, D_MODEL), BF16)] + [jax.ShapeDtypeStruct((s, w), F32) for w in widths],
        compiler_params=_params(),
    )(h, nw, win)


def _qkv_fwd(cq, ckv, misc, qnw, kvnw, wuq, wkv, cosf, sinf):
    s = cq.shape[0]

    def body(cq_ref, ckv_ref, misc_ref, qnw_ref, kvnw_ref, wuq_ref, wkv_ref, cos_ref, sin_ref,
             cqn_ref, ckvn_ref, q_ref, k_ref, v_ref):
        cosf, sinf = cos_ref[...], sin_ref[...]
        cqn = _rms_fwd(cq_ref[...], qnw_ref[...]).astype(BF16)
        cqn_ref[...] = cqn
        q = _dot(cqn, wuq_ref[...])
        ckvn = _rms_fwd(ckv_ref[...], kvnw_ref[...]).astype(BF16)
        ckvn_ref[...] = ckvn
        kv = _dot(ckvn, wkv_ref[...])
        m = misc_ref[...]
        lane = lax.broadcasted_iota(jnp.int32, m.shape, 1)
        in_rope = jnp.logical_and(lane >= MISC_ROPE, lane < MISC_ROPE + QK_ROPE)
        kr = jnp.where(in_rope, _rope(m, cosf, sinf, 1.0), 0.0)
        for hd in range(MLA_HEADS):
            cols = slice(hd * HEAD_PAD, (hd + 1) * HEAD_PAD)
            q_ref[:, cols] = _rope(q[:, cols], cosf, sinf, 1.0).astype(BF16)
            k_ref[:, cols] = (kv[:, cols] + kr).astype(BF16)
        vv = kv[:, MLA_HEADS * HEAD_PAD:]
        vlane = lax.broadcasted_iota(jnp.int32, vv.shape, 1)
        ones_at = jnp.where((vlane // HEAD_PAD) % 2 == 0, V_DIM, 0)
        v_ref[...] = jnp.where(vlane % HEAD_PAD == ones_at, 1.0, vv).astype(BF16)

    wide = MLA_HEADS * HEAD_PAD
    return pl.pallas_call(
        body, name="qkv_fwd", grid=(s // TM,),
        in_specs=[_rows(TM, Q_RANK), _rows(TM, KV_RANK), _rows(TM, HEAD_PAD), _full((1, Q_RANK)), _full((1, KV_RANK)),
                  _resident((Q_RANK, wide)), _resident((KV_RANK, 2 * wide)), _rows(TM, HEAD_PAD), _rows(TM, HEAD_PAD)],
        out_specs=[_rows(TM, Q_RANK), _rows(TM, KV_RANK), _rows(TM, wide), _rows(TM, wide), _rows(TM, wide)],
        out_shape=[jax.ShapeDtypeStruct((s, Q_RANK), BF16), jax.ShapeDtypeStruct((s, KV_RANK), BF16)]
        + [jax.ShapeDtypeStruct((s, wide), BF16)] * 3,
        compiler_params=_params(),
    )(cq, ckv, misc, qnw, kvnw, wuq, wkv, cosf, sinf)


def _chunk_mask(t, keys_on_rows=False):
    row = lax.broadcasted_iota(jnp.int32, (t, t), 0) // CHUNK
    col = lax.broadcasted_iota(jnp.int32, (t, t), 1) // CHUNK
    return (row <= col) if keys_on_rows else (col <= row)


def _attn_fwd(q, k, v, gather=()):
    s = q.shape[0]
    t = ATT_T
    nq = s // t
    pair = ATT_G * HEAD_PAD
    ng = len(gather)

    def body(q_ref, k_ref, v_ref, *rest):
        g_in, (o_ref, lse_ref), g_out = rest[:ng], rest[ng:ng + 2], rest[ng + 2:2 * ng + 2]
        m_s, acc_s = rest[2 * ng + 2:2 * ng + 4]
        qi = pl.program_id(1)
        group, groups = pl.program_id(0), MLA_HEADS // ATT_G
        _hosted_gather(g_in, g_out, rest[2 * ng + 4:],
                       jnp.logical_and(group == 0, qi == 0),
                       jnp.logical_and(group == groups // 2, qi == nq // 2),
                       jnp.logical_and(group == groups - 1, qi == nq - 1))
        m_s[...] = jnp.full(m_s.shape, -jnp.inf, F32)
        acc_s[...] = jnp.zeros(acc_s.shape, F32)

        def step(kb, masked):
            r0 = pl.multiple_of(kb * t, t)

            def scores(hh):
                cols = slice(hh * HEAD_PAD, (hh + 1) * HEAD_PAD)
                return _dot_nt(q_ref[:, cols], k_ref[pl.ds(r0, t), cols])

            def soft(hh, raw):
                sc = raw * ATT_SCALE_LOG2
                if masked:
                    sc = jnp.where(_chunk_mask(t), sc, -jnp.inf)
                m_old = m_s[hh]
                m_new = jnp.maximum(m_old, jnp.max(sc, axis=-1, keepdims=True))
                alpha = jnp.exp2(m_old - m_new)
                p = jnp.exp2(sc - jnp.tile(m_new, (1, t // HEAD_PAD)))
                m_s[hh] = m_new
                return alpha, p.astype(BF16)

            def update(hh, alpha, p):
                cols = slice(hh * HEAD_PAD, (hh + 1) * HEAD_PAD)
                acc_s[hh] = alpha * acc_s[hh] + _dot(p, v_ref[pl.ds(r0, t), cols])

            raw, ap = [None] * ATT_G, [None] * ATT_G
            raw[0] = scores(0)
            for hh in range(ATT_G):
                if hh + 1 < ATT_G:
                    raw[hh + 1] = scores(hh + 1)
                ap[hh] = soft(hh, raw[hh])
                if hh >= 1:
                    update(hh - 1, *ap[hh - 1])
            update(ATT_G - 1, *ap[ATT_G - 1])

        def loop(kb, c):
            step(kb, False)
            return c

        lax.fori_loop(0, qi, loop, 0)
        step(qi, True)
        for hh in range(ATT_G):
            cols = slice(hh * HEAD_PAD, (hh + 1) * HEAD_PAD)
            acc = acc_s[hh]
            ones_at = V_DIM * (1 - hh % 2)
            l = jnp.broadcast_to(acc[:, ones_at:ones_at + 1], acc.shape)
            o_ref[:, cols] = (acc / l).astype(BF16)
            lse_ref[hh] = (m_s[hh] + jnp.log(l) * LOG2E).T[0:8, :]

    outs = pl.pallas_call(
        body, name="attn_fwd_gather" if ng else "attn_fwd", grid=(MLA_HEADS // ATT_G, nq),
        in_specs=[pl.BlockSpec((t, pair), lambda h, i: (i, h)),
                  pl.BlockSpec((s, pair), lambda h, i: (0, h)),
                  pl.BlockSpec((s, pair), lambda h, i: (0, h))] + [_ANY] * ng,
        out_specs=[pl.BlockSpec((t, pair), lambda h, i: (i, h)),
                   pl.BlockSpec((ATT_G, 8, t), lambda h, i: (h, 0, i))] + [_ANY] * ng,
        out_shape=[jax.ShapeDtypeStruct((s, MLA_HEADS * HEAD_PAD), BF16), jax.ShapeDtypeStruct((MLA_HEADS, 8, s), F32)]
        + _comm_out_shapes("gather", gather),
        scratch_shapes=[pltpu.VMEM((ATT_G, t, HEAD_PAD), F32), pltpu.VMEM((ATT_G, t, HEAD_PAD), F32)]
        + (_comm_scratch(ng) if ng else []),
        compiler_params=_params(),
    )(q, k, v, *gather)
    return outs[0], outs[1], list(outs[2:])


def _interleave(stages):
    live = list(stages)
    while live:
        still = []
        for g in live:
            try:
                next(g)
                still.append(g)
            except StopIteration:
                pass
        live = still


def _ssd_consts():
    emisc = np.zeros((HEAD_PAD, SSD_INNER), np.float32)
    for hd in range(SSD_HEADS):
        emisc[MISC_DT + hd, hd * SSD_P:(hd + 1) * SSD_P] = 1.0
    idx = np.arange(CHUNK)
    tri = (idx[:, None] >= idx[None, :]).astype(np.float32)
    return tuple(jnp.asarray(m, BF16) for m in (emisc, emisc.T.copy(), tri, tri.T.copy()))


def _ssd_chunk_common(cc, misc, emisc, tri, trit, dtb, a_exp):
    xa = cc * jax.nn.sigmoid(cc)
    dtr = _dot01(misc, emisc) + dtb
    dt = jax.nn.softplus(dtr)
    a = dt * a_exp
    acs = _dot01(a, tri, left=True)
    acs_t = _dot01(a, trit, dot=_dot_tn)
    alast = acs[CHUNK - 1:CHUNK, :]
    return xa, dtr, dt, acs, acs_t, alast


def _decay(acs, acs_t, hd):
    row = lax.broadcasted_iota(jnp.int32, (CHUNK, CHUNK), 0)
    col = lax.broadcasted_iota(jnp.int32, (CHUNK, CHUNK), 1)
    diff = acs[:, hd * SSD_P:hd * SSD_P + 1] - acs_t[hd * SSD_P:hd * SSD_P + 1, :]
    return jnp.exp(jnp.where(row >= col, diff, -jnp.inf))


def _half_mask(hh):
    lane = lax.broadcasted_iota(jnp.int32, (CHUNK, 2 * SSD_P), 1)
    return (lane >= SSD_P) if hh else (lane < SSD_P)


def _gate_norm(y, zz, nw):
    yz = y * (zz * jax.nn.sigmoid(zz))
    outs, rs = [], []
    half = SSD_INNER // SSD_GROUPS
    for g in range(SSD_GROUPS):
        yg = yz[:, g * half:(g + 1) * half]
        r = lax.rsqrt(jnp.mean(yg * yg, axis=-1, keepdims=True) + EPS)
        outs.append(yg * r)
        rs.append(r)
    return yz, jnp.concatenate(outs, axis=1), rs


def _ssd_fwd(xraw, misc, z, cw, cb, dtb, a_exp, d_exp, nw, consts):
    s = xraw.shape[0]
    nb = s // SSD_ROWS
    ncb = SSD_ROWS // CHUNK
    emisc, _, tri, trit = consts

    def body(x_ref, misc_ref, z_ref, cw_ref, cb_ref, dtb_ref, a_ref, d_ref, nw_ref, emisc_ref, tri_ref, trit_ref,
             c_ref, prev_ref, ypre_ref, yssd_ref, tail_s, state_s):
        i = pl.program_id(0)

        @pl.when(i == 0)
        def _():
            tail_s[...] = jnp.zeros(tail_s.shape, F32)
            state_s[...] = jnp.zeros(state_s.shape, F32)

        x = x_ref[...]
        xext = jnp.concatenate([tail_s[...], x], axis=0)
        acc = x * cw_ref[CONV_W - 1:CONV_W, :] + cb_ref[...]
        for j in range(1, CONV_W):
            acc = acc + pltpu.roll(xext, j, 0)[8:, :] * cw_ref[CONV_W - 1 - j:CONV_W - j, :]
        tail_s[...] = x[SSD_ROWS - 8:, :]
        c_ref[...] = acc

        def chunk(ci):
            r0 = ci * CHUNK
            xa, _, dt, acs, acs_t, alast = _ssd_chunk_common(
                c_ref[pl.ds(r0, CHUNK), :], misc_ref[pl.ds(r0, CHUNK), :], emisc_ref[...], tri_ref[...], trit_ref[...],
                dtb_ref[...], a_ref[...])
            yield
            xs = xa[:, :SSD_INNER]
            xdt = xs * dt
            wgt = (xdt * jnp.exp(alast - acs)).astype(BF16)
            e = jnp.exp(acs)
            ys, new_states, cms = [], [], []
            for g in range(SSD_GROUPS):
                bm = xa[:, SSD_INNER + g * SSD_N:SSD_INNER + (g + 1) * SSD_N].astype(BF16)
                cm = xa[:, SSD_INNER + SSD_GROUPS * SSD_N + g * SSD_N:SSD_INNER + SSD_GROUPS * SSD_N + (g + 1) * SSD_N].astype(BF16)
                cms.append(cm)
                cb_g = _dot_nt(cm, bm)
                gl = slice(g * 256, (g + 1) * 256)
                new_states.append(_dot_tn(bm, wgt[:, gl]))
                for jj in range(2):
                    pair = 2 * g + jj
                    xp = xdt[:, pair * 128:(pair + 1) * 128]
                    yp = None
                    for hh in range(2):
                        sc = (cb_g * _decay(acs, acs_t, 2 * pair + hh)).astype(BF16)
                        term = _dot(sc, jnp.where(_half_mask(hh), xp, 0.0).astype(BF16))
                        yp = term if yp is None else yp + term
                    ys.append(yp)
                yield
            prev = state_s[...]
            prev_ref[ci] = prev
            yoff = jnp.concatenate([_dot(cms[g], prev[:, g * 256:(g + 1) * 256].astype(BF16)) for g in range(SSD_GROUPS)],
                                   axis=1) * e
            state_s[...] = prev * jnp.exp(alast) + jnp.concatenate(new_states, axis=1)
            yield
            y = jnp.concatenate(ys, axis=1) + yoff + d_ref[...] * xs
            ypre_ref[pl.ds(r0, CHUNK), :] = y
            _, yn, _ = _gate_norm(y, z_ref[pl.ds(r0, CHUNK), :], None)
            yssd_ref[pl.ds(r0, CHUNK), :] = (yn * nw_ref[...]).astype(BF16)

        _interleave([chunk(ci) for ci in range(ncb)])

    return pl.pallas_call(
        body, name="ssd_fwd", grid=(nb,),
        in_specs=[_rows(SSD_ROWS, CONV_DIM), _rows(SSD_ROWS, HEAD_PAD), _rows(SSD_ROWS, SSD_INNER),
                  _full((CONV_W, CONV_DIM)), _full((1, CONV_DIM)), _full((1, SSD_INNER)), _full((1, SSD_INNER)),
                  _full((1, SSD_INNER)), _full((1, SSD_INNER)), _full((HEAD_PAD, SSD_INNER)), _full((CHUNK, CHUNK)),
                  _full((CHUNK, CHUNK))],
        out_specs=[_rows(SSD_ROWS, CONV_DIM), pl.BlockSpec((ncb, SSD_N, SSD_INNER), lambda i: (i, 0, 0)),
                   _rows(SSD_ROWS, SSD_INNER), _rows(SSD_ROWS, SSD_INNER)],
        out_shape=[jax.ShapeDtypeStruct((s, CONV_DIM), F32), jax.ShapeDtypeStruct((s // CHUNK, SSD_N, SSD_INNER), F32),
                   jax.ShapeDtypeStruct((s, SSD_INNER), F32), jax.ShapeDtypeStruct((s, SSD_INNER), BF16)],
        scratch_shapes=[pltpu.VMEM((8, CONV_DIM), F32), pltpu.VMEM((SSD_N, SSD_INNER), F32)],
        compiler_params=_params(),
    )(xraw, misc, z, cw, cb, dtb, a_exp, d_exp, nw, emisc, tri, trit)


def _outproj_fwd(oe, yssd, wout, h, nw):
    s = h.shape[0]
    wide = MLA_HEADS * HEAD_PAD

    def body(oe_ref, y_ref, w_ref, h_ref, nw_ref, mixed_ref, h1_ref):
        mixed = _dot(oe_ref[...], w_ref[0:wide, :]) + _dot(y_ref[...], w_ref[wide:, :])
        mixed_ref[...] = mixed
        h1_ref[...] = h_ref[...] + _rms_fwd(mixed, nw_ref[...])

    return pl.pallas_call(
        body, name="outproj_fwd", grid=(s // TM,),
        in_specs=[_rows(TM, wide), _rows(TM, SSD_INNER), _resident((wide + SSD_INNER, D_MODEL)), _rows(TM, D_MODEL),
                  _full((1, D_MODEL))],
        out_specs=[_rows(TM, D_MODEL), _rows(TM, D_MODEL)],
        out_shape=[jax.ShapeDtypeStruct((s, D_MODEL), F32)] * 2,
        compiler_params=_params(),
    )(oe, yssd, wout, h, nw)


def _mlp_fwd(h1, prew, wup, wdown, postw, target=None):
    s = h1.shape[0]
    fb = D_FF // N_DEV
    last = target is not None

    def body(h_ref, prew_ref, up_ref, down_ref, postw_ref, *rest):
        hh = h_ref[...]
        mb = _rms_fwd(hh, prew_ref[...]).astype(BF16)
        rest[-3 - last][...] = mb
        d = jnp.zeros((TM, D_MODEL), F32)
        for j in range(N_DEV):
            a = _dot(mb, up_ref[j])
            r = jnp.square(jnp.maximum(a, 0.0)).astype(BF16)
            d = d + _dot(r, down_ref[j])
        rest[-2 - last][...] = d
        h2 = hh + _rms_fwd(d, postw_ref[...])
        if last:
            diff = h2 - rest[0][...]
            rest[-2][...] = diff * (1.0 / D_MODEL)
            part = 0.5 * jnp.sum(jnp.mean(diff * diff, axis=-1, keepdims=True), axis=0, keepdims=True)
            _acc_rows(rest[-1], part, pl.program_id(0) == 0)
        else:
            rest[-1][...] = h2

    return pl.pallas_call(
        body, name="mlp_fwd_loss" if last else "mlp_fwd", grid=(s // TM,),
        in_specs=[_rows(TM, D_MODEL), _full((1, D_MODEL)), _resident((N_DEV, D_MODEL, fb)), _resident((N_DEV, fb, D_MODEL)),
                  _full((1, D_MODEL))] + ([_rows(TM, D_MODEL)] if last else []),
        out_specs=[_rows(TM, D_MODEL)] * 3 + ([_full((1, 1))] if last else []),
        out_shape=[jax.ShapeDtypeStruct((s, D_MODEL), BF16), jax.ShapeDtypeStruct((s, D_MODEL), F32),
                   jax.ShapeDtypeStruct((s, D_MODEL), F32)] + ([jax.ShapeDtypeStruct((1, 1), F32)] if last else []),
        compiler_params=_params(),
    )(h1, prew, wup, wdown, postw, *([target] if last else []))


def _mlp_bwd(dh2, d, h1, mb, prew, wup, wdown, postw):
    s = dh2.shape[0]
    fb = D_FF // N_DEV
    tm = TM // 2

    def body(dh2_ref, d_ref, h1_ref, mb_ref, prew_ref, up_ref, down_ref, postw_ref,
             dh1_ref, da_ref, r_ref, dd_ref, gpost_ref, gpre_ref):
        first = pl.program_id(0) == 0
        dh2 = dh2_ref[...]
        dd, gpost = _rms_bwd(d_ref[...], postw_ref[...], dh2)
        _acc_rows(gpost_ref, gpost, first)
        ddb = dd.astype(BF16)
        dd_ref[...] = ddb
        mb = mb_ref[...]
        def products(j):
            return _dot(mb, up_ref[j]), _dot_nt(ddb, down_ref[j])

        def pointwise(j, a, dr):
            a = jnp.maximum(a, 0.0)
            r_ref[j] = jnp.square(a).astype(BF16)
            da = (dr * (2.0 * a)).astype(BF16)
            da_ref[j] = da
            return da

        dm = jnp.zeros((tm, D_MODEL), F32)
        nxt, da_prev = products(0), None
        for j in range(N_DEV):
            cur = nxt
            if j + 1 < N_DEV:
                nxt = products(j + 1)
            da = pointwise(j, *cur)
            if da_prev is not None:
                dm = dm + _dot_nt(da_prev, up_ref[j - 1])
            da_prev = da
        dm = dm + _dot_nt(da_prev, up_ref[N_DEV - 1])
        dx, gpre = _rms_bwd(h1_ref[...], prew_ref[...], dm)
        _acc_rows(gpre_ref, gpre, first)
        dh1_ref[...] = dh2 + dx

    stacked = pl.BlockSpec((N_DEV, tm, fb), lambda i: (0, i, 0))
    return pl.pallas_call(
        body, name="mlp_bwd", grid=(s // tm,),
        in_specs=[_rows(tm, D_MODEL)] * 4 + [_full((1, D_MODEL)), _resident((N_DEV, D_MODEL, fb)), _resident((N_DEV, fb, D_MODEL)),
                                              _full((1, D_MODEL))],
        out_specs=[_rows(tm, D_MODEL), stacked, stacked, _rows(tm, D_MODEL), _full((1, D_MODEL)), _full((1, D_MODEL))],
        out_shape=[jax.ShapeDtypeStruct((s, D_MODEL), F32), jax.ShapeDtypeStruct((N_DEV, s, fb), BF16),
                   jax.ShapeDtypeStruct((N_DEV, s, fb), BF16), jax.ShapeDtypeStruct((s, D_MODEL), BF16),
                   jax.ShapeDtypeStruct((1, D_MODEL), F32), jax.ShapeDtypeStruct((1, D_MODEL), F32)],
        compiler_params=_params(),
    )(dh2, d, h1, mb, prew, wup, wdown, postw)


def _matmul_tn(a, b, name, tk=TK_DW):
    s, m = a.shape
    n = b.shape[1]
    tn = n if n <= 1024 else (n // 2 if (n // 2) % 128 == 0 else n // 3)
    tk = min(tk, s)
    assert n % tn == 0 and tn % 128 == 0 and s % tk == 0

    def body(a_ref, b_ref, o_ref):
        part = _dot_tn(a_ref[...], b_ref[...])

        @pl.when(pl.program_id(1) == 0)
        def _():
            o_ref[...] = part

        @pl.when(pl.program_id(1) != 0)
        def _():
            o_ref[...] += part

    return pl.pallas_call(
        body, name=name, grid=(n // tn, s // tk),
        in_specs=[pl.BlockSpec((tk, m), lambda j, k: (k, 0)), pl.BlockSpec((tk, tn), lambda j, k: (k, j))],
        out_specs=pl.BlockSpec((m, tn), lambda j, k: (0, j)),
        out_shape=jax.ShapeDtypeStruct((m, n), F32),
        compiler_params=_params(),
    )(a, b)


def _matmul_tn_stacked(a, b, name, a_stacked, tk=TK_DW):
    tk = min(tk, a.shape[-2])
    if a_stacked:
        _, s, m = a.shape
        n = b.shape[1]
        in_specs = [pl.BlockSpec((1, tk, m), lambda j, k: (j, k, 0)), pl.BlockSpec((tk, n), lambda j, k: (k, 0))]
    else:
        s, m = a.shape
        n = b.shape[2]
        in_specs = [pl.BlockSpec((tk, m), lambda j, k: (k, 0)), pl.BlockSpec((1, tk, n), lambda j, k: (j, k, 0))]

    nk = s // tk

    def body(a_ref, b_ref, o_ref, acc_s):
        av = a_ref[0] if a_stacked else a_ref[...]
        bv = b_ref[...] if a_stacked else b_ref[0]
        part = _dot_tn(av, bv)
        k = pl.program_id(1)

        @pl.when(k == 0)
        def _():
            acc_s[...] = part

        @pl.when(jnp.logical_and(k != 0, k != nk - 1))
        def _():
            acc_s[...] += part

        @pl.when(k == nk - 1)
        def _():
            o_ref[0] = (part if nk == 1 else acc_s[...] + part).astype(BF16)

    return pl.pallas_call(
        body, name=name, grid=(N_DEV, nk),
        in_specs=in_specs,
        out_specs=pl.BlockSpec((1, m, n), lambda j, k: (j, 0, 0)),
        out_shape=jax.ShapeDtypeStruct((N_DEV, m, n), BF16),
        scratch_shapes=[pltpu.VMEM((m, n), F32)],
        compiler_params=_params(),
    )(a, b)


def _outproj_bwd(dh1, mixed, nw, wout, oe):
    s = dh1.shape[0]
    wide = MLA_HEADS * HEAD_PAD

    def body(dh1_ref, mixed_ref, nw_ref, w_ref, oe_ref, dmix_ref, doe_ref, dy_ref, gnw_ref, delta_ref):
        dmix, gnw = _rms_bwd(mixed_ref[...], nw_ref[...], dh1_ref[...])
        _acc_rows(gnw_ref, gnw, pl.program_id(0) == 0)
        dmb = dmix.astype(BF16)
        dmix_ref[...] = dmb
        doe_ref[...] = _dot_nt(dmb, w_ref[0:wide, :]).astype(BF16)
        dy_ref[...] = _dot_nt(dmb, w_ref[wide:, :])
        ones = jnp.ones((8, HEAD_PAD), BF16)
        for hd in range(MLA_HEADS):
            cols = slice(hd * HEAD_PAD, (hd + 1) * HEAD_PAD)
            prod = oe_ref[:, cols].astype(F32) * doe_ref[:, cols].astype(F32)
            delta_ref[hd] = _dot01(prod, ones, dot=_dot_nt, left=True)

    return pl.pallas_call(
        body, name="outproj_bwd", grid=(s // TM,),
        in_specs=[_rows(TM, D_MODEL), _rows(TM, D_MODEL), _full((1, D_MODEL)), _resident((wide + SSD_INNER, D_MODEL)),
                  _rows(TM, wide)],
        out_specs=[_rows(TM, D_MODEL), _rows(TM, wide), _rows(TM, SSD_INNER), _full((1, D_MODEL)),
                   pl.BlockSpec((MLA_HEADS, 8, TM), lambda i: (0, 0, i))],
        out_shape=[jax.ShapeDtypeStruct((s, D_MODEL), BF16), jax.ShapeDtypeStruct((s, wide), BF16),
                   jax.ShapeDtypeStruct((s, SSD_INNER), F32), jax.ShapeDtypeStruct((1, D_MODEL), F32),
                   jax.ShapeDtypeStruct((MLA_HEADS, 8, s), F32)],
        compiler_params=_params(),
    )(dh1, mixed, nw, wout, oe)


def _attn_bwd(q, k, v, do, lse, delta, exchange=()):
    s = q.shape[0]
    t = ATT_T
    nq = s // t
    pair = 2 * HEAD_PAD
    ne = len(exchange)

    def body(q_ref, k_ref, v_ref, do_ref, lse_ref, delta_ref, *rest):
        e_in, (dq_ref, dk_ref, dv_ref), e_out = rest[:ne], rest[ne:ne + 3], rest[ne + 3:2 * ne + 3]
        kb = pl.program_id(1)
        _hosted_comm("exchange", e_in, e_out, rest[2 * ne + 3:],
                     jnp.logical_and(pl.program_id(0) == 0, kb == 0),
                     jnp.logical_and(pl.program_id(0) == MLA_HEADS // 2 - 1, kb == nq - 1))

        @pl.when(kb == 0)
        def _():
            dq_ref[...] = jnp.zeros(dq_ref.shape, F32)

        dk_ref[...] = jnp.zeros(dk_ref.shape, F32)
        dv_ref[...] = jnp.zeros(dv_ref.shape, F32)

        def step(qb, masked):
            r0 = pl.multiple_of(qb * t, t)
            for hh in range(2):
                cols = slice(hh * HEAD_PAD, (hh + 1) * HEAD_PAD)
                kk = k_ref[:, cols]
                qq = q_ref[pl.ds(r0, t), cols]
                dd = do_ref[pl.ds(r0, t), cols]
                sc = _dot_nt(kk, qq) * ATT_SCALE_LOG2
                if masked:
                    sc = jnp.where(_chunk_mask(t, keys_on_rows=True), sc, -jnp.inf)
                p = jnp.exp2(sc - lse_ref[hh, 0:1, pl.ds(r0, t)])
                dv_ref[:, cols] += _dot(p.astype(BF16), dd)
                dp = _dot_nt(v_ref[:, cols], dd)
                ds = (p * (dp - delta_ref[hh, 0:1, pl.ds(r0, t)]) * ATT_SCALE).astype(BF16)
                dk_ref[:, cols] += _dot(ds, qq)
                dq_ref[pl.ds(r0, t), cols] += _dot_tn(ds, kk)

        def loop(qb, c):
            step(qb, False)
            return c

        step(kb, True)
        lax.fori_loop(kb + 1, nq, loop, 0)

    whole = pl.BlockSpec((s, pair), lambda h, i: (0, h))
    tile = pl.BlockSpec((t, pair), lambda h, i: (i, h))
    rowvec = pl.BlockSpec((2, 8, s), lambda h, i: (h, 0, 0))
    wide = MLA_HEADS * HEAD_PAD
    outs = pl.pallas_call(
        body, name="attn_bwd_exchange" if ne else "attn_bwd", grid=(MLA_HEADS // 2, nq),
        in_specs=[whole, tile, tile, whole, rowvec, rowvec] + [_ANY] * ne,
        out_specs=[whole, tile, tile] + [_ANY] * ne,
        out_shape=[jax.ShapeDtypeStruct((s, wide), F32)] * 3 + _comm_out_shapes("exchange", exchange),
        scratch_shapes=_comm_scratch(ne) if ne else [],
        compiler_params=_params(),
    )(q, k, v, do, lse, delta, *exchange)
    return outs[0], outs[1], outs[2], list(outs[3:])


def _ssd_bwd(dy, ypre, z, c, xraw, misc, prev, cw, dtb, a_exp, d_exp, nw, consts):
    s = dy.shape[0]
    nb = s // SSD_ROWS
    ncb = SSD_ROWS // CHUNK
    emisc, emisc_t, tri, trit = consts

    def body(dy_ref, ypre_ref, z_ref, c_ref, x_ref, xprev_ref, misc_ref, prev_ref, cw_ref, dtb_ref, a_ref, d_ref, nw_ref,
             emisc_ref, emisct_ref, tri_ref, trit_ref,
             dz_ref, dx_ref, dmisc_ref, gnw_ref, gd_ref, galog_ref, gdtb_ref, gcw_ref, gcb_ref,
             dst_s, dc_s, head_s):
        i = pl.program_id(0)
        first = i == 0

        @pl.when(first)
        def _():
            dst_s[...] = jnp.zeros(dst_s.shape, F32)
            head_s[...] = jnp.zeros(head_s.shape, F32)
            gnw_ref[...] = jnp.zeros(gnw_ref.shape, F32)
            gd_ref[...] = jnp.zeros(gd_ref.shape, F32)
            galog_ref[...] = jnp.zeros(galog_ref.shape, F32)
            gdtb_ref[...] = jnp.zeros(gdtb_ref.shape, F32)

        a_exp_v = a_ref[...]
        a8 = _dot01(a_exp_v, emisct_ref[...]) * (1.0 / SSD_P)

        def chunk(ci):
            r0 = ci * CHUNK
            cc = c_ref[pl.ds(r0, CHUNK), :]
            mm = misc_ref[pl.ds(r0, CHUNK), :]
            xa, dtr, dt, acs, acs_t, alast = _ssd_chunk_common(cc, mm, emisc_ref[...], tri_ref[...], trit_ref[...],
                                                              dtb_ref[...], a_exp_v)
            yield
            xs = xa[:, :SSD_INNER]
            xdt = xs * dt
            y = ypre_ref[pl.ds(r0, CHUNK), :]
            zz = z_ref[pl.ds(r0, CHUNK), :]
            yz, yn, rs = _gate_norm(y, zz, None)
            dyo = dy_ref[pl.ds(r0, CHUNK), :]
            gnw_ref[...] += jnp.sum(dyo * yn, axis=0, keepdims=True)
            dyn = dyo * nw_ref[...]
            half = SSD_INNER // SSD_GROUPS
            dyz_parts = []
            for g in range(SSD_GROUPS):
                gl = slice(g * half, (g + 1) * half)
                dyz_parts.append(rs[g] * (dyn[:, gl] - yn[:, gl] * jnp.mean(dyn[:, gl] * yn[:, gl], axis=-1, keepdims=True)))
            dyz = jnp.concatenate(dyz_parts, axis=1)
            sg = jax.nn.sigmoid(zz)
            dz_ref[pl.ds(r0, CHUNK), :] = dyz * y * (sg * (1.0 + zz * (1.0 - sg)))
            dyp = dyz * (zz * sg)
            dypb = dyp.astype(BF16)
            gd_ref[...] += jnp.sum(dyp * xs, axis=0, keepdims=True)
            yield
            prev = prev_ref[ci]
            cd = jnp.exp(alast)
            e = jnp.exp(acs)
            dsx = jnp.exp(alast - acs)
            wgt = (xdt * dsx).astype(BF16)
            dze = (dyp * e).astype(BF16)
            dprev_parts, diag_all, dbm, dcm, yoff_parts, bms = [], [], [], [], [], []
            lane8 = lax.broadcasted_iota(jnp.int32, (CHUNK, HEAD_PAD), 1)
            diag8 = jnp.zeros((CHUNK, HEAD_PAD), F32)
            for g in range(SSD_GROUPS):
                gl = slice(g * 256, (g + 1) * 256)
                bm = xa[:, SSD_INNER + g * SSD_N:SSD_INNER + (g + 1) * SSD_N].astype(BF16)
                cm = xa[:, SSD_INNER + SSD_GROUPS * SSD_N + g * SSD_N:SSD_INNER + SSD_GROUPS * SSD_N + (g + 1) * SSD_N].astype(BF16)
                bms.append(bm)
                prev_g = prev[:, gl].astype(BF16)
                dcm_g = _dot_nt(dze[:, gl], prev_g)
                dprev_parts.append(_dot_tn(cm, dze[:, gl]))
                cb_g = _dot_nt(cm, bm)
                dcb = jnp.zeros((CHUNK, CHUNK), F32)
                diag_parts = []
                for jj in range(2):
                    pair = 2 * g + jj
                    pl_ = slice(pair * 128, (pair + 1) * 128)
                    xp = xdt[:, pl_]
                    dyp_p = dypb[:, pl_]
                    dxp = jnp.zeros((CHUNK, 128), F32)
                    for hh in range(2):
                        hd = 2 * pair + hh
                        dec = _decay(acs, acs_t, hd)
                        xm = jnp.where(_half_mask(hh), xp, 0.0).astype(BF16)
                        dsc = _dot_nt(dyp_p, xm) * dec
                        dcb = dcb + dsc
                        sc = (cb_g * dec).astype(BF16)
                        dxp = dxp + jnp.where(_half_mask(hh), _dot_tn(sc, dyp_p), 0.0)
                        dm = dsc * cb_g
                        diag8 = diag8 + jnp.where(lane8 == MISC_DT + hd, jnp.sum(dm - dm.T, axis=1, keepdims=True), 0.0)
                    diag_parts.append(dxp)
                dcbb = dcb.astype(BF16)
                dcm.append(dcm_g + _dot(dcbb, bm))
                dbm.append(_dot_tn(dcbb, cm))
                diag_all.append(jnp.concatenate(diag_parts, axis=1))
                yoff_parts.append(_dot(cm, prev_g) * e[:, gl])
                yield
            dst = dst_s[...]
            glast = jnp.sum(dst * prev, axis=0, keepdims=True) * cd
            dxdt_state_parts = []
            for g in range(SSD_GROUPS):
                gl = slice(g * 256, (g + 1) * 256)
                dst_g = dst[:, gl].astype(BF16)
                dxdt_state_parts.append(_dot(bms[g], dst_g) * dsx[:, gl])
                dbm[g] = dbm[g] + _dot_nt(wgt[:, gl], dst_g)
            dst_s[...] = dst * cd + jnp.concatenate(dprev_parts, axis=1)
            yield
            dxdt_state = jnp.concatenate(dxdt_state_parts, axis=1)
            dxdt = jnp.concatenate(diag_all, axis=1) + dxdt_state
            dacs = dyp * jnp.concatenate(yoff_parts, axis=1) - xdt * dxdt_state
            last = jnp.sum(xdt * dxdt_state, axis=0, keepdims=True) + glast
            row = lax.broadcasted_iota(jnp.int32, (CHUNK, SSD_INNER), 0)
            dacs = dacs + jnp.where(row == CHUNK - 1, last, 0.0)
            dacs8 = _dot01(dacs, emisct_ref[...]) + diag8
            da8 = _dot01(dacs8, trit_ref[...], left=True)
            ddt8 = da8 * a8 + _dot01(dxdt * xs, emisct_ref[...])
            yield
            dtr8 = mm + _dot01(dtb_ref[...], emisct_ref[...]) * (1.0 / SSD_P)
            dt8 = jax.nn.softplus(dtr8)
            lane = lax.broadcasted_iota(jnp.int32, (CHUNK, HEAD_PAD), 1)
            on_dt = jnp.logical_and(lane >= MISC_DT, lane < MISC_DT + SSD_HEADS)
            ddtr8 = jnp.where(on_dt, ddt8 * jax.nn.sigmoid(dtr8), 0.0)
            dmisc_ref[pl.ds(r0, CHUNK), :] = ddtr8
            gdtb_ref[...] += jnp.sum(ddtr8, axis=0, keepdims=True)
            galog_ref[...] += jnp.sum(jnp.where(on_dt, da8 * dt8, 0.0), axis=0, keepdims=True) * a8
            dxs = d_ref[...] * dyp + dxdt * dt
            dxa = jnp.concatenate([dxs] + dbm + dcm, axis=1)
            sc_ = jax.nn.sigmoid(cc)
            dc_s[pl.ds(r0, CHUNK), :] = dxa * (sc_ * (1.0 + cc * (1.0 - sc_)))

        _interleave([chunk(ci) for ci in reversed(range(ncb))])

        dc = dc_s[...]
        dcext = jnp.concatenate([dc, head_s[...]], axis=0)
        dx = dc * cw_ref[CONV_W - 1:CONV_W, :]
        for j in range(1, CONV_W):
            dx = dx + pltpu.roll(dcext, SSD_ROWS + 8 - j, 0)[:SSD_ROWS, :] * cw_ref[CONV_W - 1 - j:CONV_W - j, :]
        dx_ref[...] = dx
        head_s[...] = dc[:8, :]
        xprev = jnp.where(i == nb - 1, 0.0, xprev_ref[...])
        xext = jnp.concatenate([xprev, x_ref[...]], axis=0)
        rows = [jnp.sum(dc * pltpu.roll(xext, CONV_W - 1 - kk, 0)[8:, :], axis=0, keepdims=True) for kk in range(CONV_W)]
        gcw = jnp.concatenate(rows, axis=0)

        @pl.when(first)
        def _():
            gcw_ref[...] = gcw
            gcb_ref[...] = jnp.sum(dc, axis=0, keepdims=True)

        @pl.when(jnp.logical_not(first))
        def _():
            gcw_ref[...] += gcw
            gcb_ref[...] += jnp.sum(dc, axis=0, keepdims=True)

    def rev(width):
        return pl.BlockSpec((SSD_ROWS, width), lambda i: (nb - 1 - i, 0))

    per8 = SSD_ROWS // 8
    return pl.pallas_call(
        body, name="ssd_bwd", grid=(nb,),
        in_specs=[rev(SSD_INNER), rev(SSD_INNER), rev(SSD_INNER), rev(CONV_DIM), rev(CONV_DIM),
                  pl.BlockSpec((8, CONV_DIM), lambda i: (jnp.maximum((nb - 1 - i) * per8 - 1, 0), 0)),
                  rev(HEAD_PAD), pl.BlockSpec((ncb, SSD_N, SSD_INNER), lambda i: (nb - 1 - i, 0, 0)),
                  _full((CONV_W, CONV_DIM)), _full((1, SSD_INNER)), _full((1, SSD_INNER)), _full((1, SSD_INNER)),
                  _full((1, SSD_INNER)), _full((HEAD_PAD, SSD_INNER)), _full((SSD_INNER, HEAD_PAD)), _full((CHUNK, CHUNK)),
                  _full((CHUNK, CHUNK))],
        out_specs=[rev(SSD_INNER), rev(CONV_DIM), rev(HEAD_PAD), _full((1, SSD_INNER)), _full((1, SSD_INNER)),
                   _full((1, HEAD_PAD)), _full((1, HEAD_PAD)), _full((CONV_W, CONV_DIM)), _full((1, CONV_DIM))],
        out_shape=[jax.ShapeDtypeStruct((s, SSD_INNER), F32), jax.ShapeDtypeStruct((s, CONV_DIM), F32),
                   jax.ShapeDtypeStruct((s, HEAD_PAD), F32), jax.ShapeDtypeStruct((1, SSD_INNER), F32),
                   jax.ShapeDtypeStruct((1, SSD_INNER), F32), jax.ShapeDtypeStruct((1, HEAD_PAD), F32),
                   jax.ShapeDtypeStruct((1, HEAD_PAD), F32), jax.ShapeDtypeStruct((CONV_W, CONV_DIM), F32),
                   jax.ShapeDtypeStruct((1, CONV_DIM), F32)],
        scratch_shapes=[pltpu.VMEM((SSD_N, SSD_INNER), F32), pltpu.VMEM((SSD_ROWS, CONV_DIM), F32), pltpu.VMEM((8, CONV_DIM), F32)],
        compiler_params=_params(),
    )(dy, ypre, z, c, xraw, xraw, misc, prev, cw, dtb, a_exp, d_exp, nw, emisc, emisc_t, tri, trit)


def _qkv_bwd(dq, dk, dv, cq, ckv, qnw, kvnw, wuq, wkv, cosf, sinf):
    s = dq.shape[0]
    wide = MLA_HEADS * HEAD_PAD

    def body(dq_ref, dk_ref, dv_ref, cq_ref, ckv_ref, qnw_ref, kvnw_ref, wuq_ref, wkv_ref, cos_ref, sin_ref,
             dqb_ref, dkvb_ref, dcq_ref, dckv_ref, dmisc_ref, gq_ref, gkv_ref):
        first = pl.program_id(0) == 0
        cosf, sinf = cos_ref[...], sin_ref[...]
        dkr = jnp.zeros((TM, HEAD_PAD), F32)
        for hd in range(MLA_HEADS):
            cols = slice(hd * HEAD_PAD, (hd + 1) * HEAD_PAD)
            dqb_ref[:, cols] = _rope(dq_ref[:, cols], cosf, sinf, -1.0).astype(BF16)
            dkh = dk_ref[:, cols]
            dkvb_ref[:, cols] = dkh.astype(BF16)
            dkr = dkr + dkh
        dkvb_ref[:, wide:] = dv_ref[...].astype(BF16)
        lane = lax.broadcasted_iota(jnp.int32, dkr.shape, 1)
        in_rope = jnp.logical_and(lane >= MISC_ROPE, lane < MISC_ROPE + QK_ROPE)
        dmisc_ref[...] = jnp.where(in_rope, _rope(jnp.where(in_rope, dkr, 0.0), cosf, sinf, -1.0), 0.0)
        dcq, gq = _rms_bwd(cq_ref[...], qnw_ref[...], _dot_nt(dqb_ref[...], wuq_ref[...]))
        dcq_ref[...] = dcq
        _acc_rows(gq_ref, gq, first)
        dckv, gkv = _rms_bwd(ckv_ref[...], kvnw_ref[...], _dot_nt(dkvb_ref[...], wkv_ref[...]))
        dckv_ref[...] = dckv
        _acc_rows(gkv_ref, gkv, first)

    return pl.pallas_call(
        body, name="qkv_bwd", grid=(s // TM,),
        in_specs=[_rows(TM, wide)] * 3 + [_rows(TM, Q_RANK), _rows(TM, KV_RANK), _full((1, Q_RANK)), _full((1, KV_RANK)),
                                          _resident((Q_RANK, wide)), _resident((KV_RANK, 2 * wide)), _rows(TM, HEAD_PAD), _rows(TM, HEAD_PAD)],
        out_specs=[_rows(TM, wide), _rows(TM, 2 * wide), _rows(TM, Q_RANK), _rows(TM, KV_RANK), _rows(TM, HEAD_PAD),
                   _full((1, Q_RANK)), _full((1, KV_RANK))],
        out_shape=[jax.ShapeDtypeStruct((s, wide), BF16), jax.ShapeDtypeStruct((s, 2 * wide), BF16),
                   jax.ShapeDtypeStruct((s, Q_RANK), F32), jax.ShapeDtypeStruct((s, KV_RANK), F32),
                   jax.ShapeDtypeStruct((s, HEAD_PAD), F32), jax.ShapeDtypeStruct((1, Q_RANK), F32),
                   jax.ShapeDtypeStruct((1, KV_RANK), F32)],
        compiler_params=_params(),
    )(dq, dk, dv, cq, ckv, qnw, kvnw, wuq, wkv, cosf, sinf)


def _inproj_bwd(dcq, dckv, dmisc_rope, dmisc_dt, dz, dxbc, h, dh1, nw, win):
    s = h.shape[0]

    def body(dcq_ref, dckv_ref, dma_ref, dmb_ref, dz_ref, dxbc_ref, h_ref, dh1_ref, nw_ref, w_ref, dproj_ref, dh0_ref, gnw_ref):
        dproj_ref[:, 0:768] = dcq_ref[...].astype(BF16)
        dproj_ref[:, 768:1024] = dckv_ref[...].astype(BF16)
        dproj_ref[:, 1024:1152] = (dma_ref[...] + dmb_ref[...]).astype(BF16)
        dproj_ref[:, 1152:1664] = dz_ref[...].astype(BF16)
        dproj_ref[:, 1664:2688] = dxbc_ref[...].astype(BF16)
        du = _dot_nt(dproj_ref[...], w_ref[...])
        dx, gnw = _rms_bwd(h_ref[...], nw_ref[...], du)
        _acc_rows(gnw_ref, gnw, pl.program_id(0) == 0)
        dh0_ref[...] = dh1_ref[...] + dx

    return pl.pallas_call(
        body, name="inproj_bwd", grid=(s // TM,),
        in_specs=[_rows(TM, Q_RANK), _rows(TM, KV_RANK), _rows(TM, HEAD_PAD), _rows(TM, HEAD_PAD), _rows(TM, SSD_INNER),
                  _rows(TM, CONV_DIM), _rows(TM, D_MODEL), _rows(TM, D_MODEL), _full((1, D_MODEL)), _resident((D_MODEL, IN_PAD))],
        out_specs=[_rows(TM, IN_PAD), _rows(TM, D_MODEL), _full((1, D_MODEL))],
        out_shape=[jax.ShapeDtypeStruct((s, IN_PAD), BF16), jax.ShapeDtypeStruct((s, D_MODEL), F32),
                   jax.ShapeDtypeStruct((1, D_MODEL), F32)],
        compiler_params=_params(),
    )(dcq, dckv, dmisc_rope, dmisc_dt, dz, dxbc, h, dh1, nw, win)


def _row_tile(rows, cols):
    cap = max(8, (1 << 18) // max(cols, 128))
    best = None
    for t in range(8, rows + 1, 8):
        if rows % t == 0 and t <= cap:
            best = t
    return best if best is not None else rows


def _adamw(w, g, m, v, name):
    rows, cols = w.shape
    tr = _row_tile(rows, cols)

    def body(w_ref, g_ref, m_ref, v_ref, d_ref, m2_ref, v2_ref):
        gg = g_ref[...]
        m2 = ADAM_B1 * m_ref[...] + (1.0 - ADAM_B1) * gg
        v2 = ADAM_B2 * v_ref[...] + (1.0 - ADAM_B2) * jnp.square(gg)
        m_hat = m2 / (1.0 - ADAM_B1 ** ADAM_STEP)
        v_hat = v2 / (1.0 - ADAM_B2 ** ADAM_STEP)
        d_ref[...] = -ADAM_LR * (m_hat / (jnp.sqrt(v_hat) + ADAM_EPS) + ADAM_WD * w_ref[...])
        m2_ref[...] = m2
        v2_ref[...] = v2

    spec = pl.BlockSpec((tr, cols), lambda i: (i, 0))
    return pl.pallas_call(
        body, name=name, grid=(rows // tr,),
        in_specs=[spec] * 4, out_specs=[spec] * 3,
        out_shape=[jax.ShapeDtypeStruct((rows, cols), F32)] * 3,
    )(w, g, m, v)


def _sum_adamw(slots, w, m, v, name):
    _, rows, cols = w.shape
    tr = _row_tile(rows, cols)
    nb = rows // tr

    def body(s0_ref, s1_ref, w_ref, m_ref, v_ref, g_ref, d_ref, m2_ref, v2_ref):
        for l, ref in enumerate((s0_ref, s1_ref)):
            @pl.when(pl.program_id(0) == l)
            def _(ref=ref):
                acc = ref[0].astype(F32)
                for i in range(1, N_DEV):
                    acc = acc + ref[i].astype(F32)
                g_ref[...] = acc

        gg = g_ref[...]
        m2 = ADAM_B1 * m_ref[...] + (1.0 - ADAM_B1) * gg
        v2 = ADAM_B2 * v_ref[...] + (1.0 - ADAM_B2) * jnp.square(gg)
        m_hat = m2 / (1.0 - ADAM_B1 ** ADAM_STEP)
        v_hat = v2 / (1.0 - ADAM_B2 ** ADAM_STEP)
        d_ref[...] = -ADAM_LR * (m_hat / (jnp.sqrt(v_hat) + ADAM_EPS) + ADAM_WD * w_ref[...])
        m2_ref[...] = m2
        v2_ref[...] = v2

    slot_spec = lambda layer: pl.BlockSpec((N_DEV, tr, cols), lambda l, i: (0, jnp.where(l == layer, i, (nb - 1) * (1 - layer)), 0))
    spec = pl.BlockSpec((None, tr, cols), lambda l, i: (l, i, 0))
    return pl.pallas_call(
        body, name=name, grid=(DEPTH, nb),
        in_specs=[slot_spec(0), slot_spec(1), spec, spec, spec], out_specs=[spec] * 4,
        out_shape=[jax.ShapeDtypeStruct(w.shape, F32)] * 4,
        compiler_params=_params(),
    )(slots[0], slots[1], w, m, v)


_MESH = pl.DeviceIdType.MESH
_ANY = pl.BlockSpec(memory_space=pl.ANY)


def _my_place():
    return lax.axis_index("x"), lax.axis_index("y"), lax.axis_index("c")


def _flip(place, k):
    x, y, c = place
    return (1 - x if k & 4 else x, 1 - y if k & 2 else y, 1 - c if k & 1 else c)


def _block_id(place):
    return 4 * place[0] + 2 * place[1] + place[2]


def _peer_copies(kind, in_refs, out_refs, send_sems, recv_sems, local_sems):
    me = _my_place()
    my = _block_id(me)
    remote, local = [], []
    for a, (x_ref, out_ref) in enumerate(zip(in_refs, out_refs)):
        src_of = (lambda place, r=x_ref: r) if kind == "gather" else (lambda place, r=x_ref: r.at[_block_id(place)])
        local.append(pltpu.make_async_copy(src_of(me), out_ref.at[my], local_sems.at[a]))
        for k in range(1, N_DEV):
            peer = _flip(me, k)
            remote.append(pltpu.make_async_remote_copy(
                src_ref=src_of(peer), dst_ref=out_ref.at[my], send_sem=send_sems.at[a * 7 + k - 1],
                recv_sem=recv_sems.at[a * 7 + k - 1], device_id=peer, device_id_type=_MESH))
    return remote, local


def _comm_out_shapes(kind, arrays):
    return [jax.ShapeDtypeStruct((N_DEV, *a.shape) if kind == "gather" else a.shape, a.dtype) for a in arrays]


def _comm_scratch(n):
    return [pltpu.SemaphoreType.DMA((7 * n,)), pltpu.SemaphoreType.DMA((7 * n,)), pltpu.SemaphoreType.DMA((n,))]


def _hosted_comm(kind, in_refs, out_refs, sems, first, last):
    if not in_refs:
        return

    @pl.when(first)
    def _():
        remote, local = _peer_copies(kind, in_refs, out_refs, *sems)
        for cp in local + remote:
            cp.start()

    @pl.when(last)
    def _():
        remote, local = _peer_copies(kind, in_refs, out_refs, *sems)
        for cp in remote:
            cp.wait()
        for cp in local:
            cp.wait()


def _two_level_gather_steps(in_refs, out_refs, send_sems, recv_sems, local_sems):
    n = len(in_refs)
    me = _my_place()
    x, y, c = me
    sibling = (x, y, 1 - c)
    chips = [(1 - x, y), (x, 1 - y), (1 - x, 1 - y)]

    def copy(a, k, place, to, src=None):
        block = out_refs[a].at[_block_id(place)]
        return pltpu.make_async_remote_copy(
            src_ref=block if src is None else src, dst_ref=block, send_sem=send_sems.at[7 * a + k],
            recv_sem=recv_sems.at[7 * a + k], device_id=to, device_id_type=_MESH)

    mine = [pltpu.make_async_copy(in_refs[a], out_refs[a].at[_block_id(me)], local_sems.at[a]) for a in range(n)]
    first = [copy(a, 0, me, sibling, src=in_refs[a]) for a in range(n)]
    first += [copy(a, 1 + j, me, (*chip, c), src=in_refs[a]) for a in range(n) for j, chip in enumerate(chips)]
    passed = [copy(a, 4 + j, (*chip, c), sibling) for a in range(n) for j, chip in enumerate(chips)]

    def send():
        for cp in mine + first:
            cp.start()

    def forward():
        for a in range(n):
            for j, chip in enumerate(chips):
                copy(a, 1 + j, (*chip, c), me).wait_recv()
                passed[3 * a + j].start()

    def finish():
        for a in range(n):
            copy(a, 0, sibling, me).wait_recv()
            for j, chip in enumerate(chips):
                copy(a, 4 + j, (*chip, 1 - c), me).wait_recv()
        for cp in first + passed:
            cp.wait_send()
        for cp in mine:
            cp.wait()

    return send, forward, finish


def _gather_two_level(arrays, name):
    n = len(arrays)

    def body(*refs):
        for step in _two_level_gather_steps(refs[:n], refs[n:2 * n], *refs[2 * n:]):
            step()

    return pl.pallas_call(
        body, name=name, out_shape=_comm_out_shapes("gather", arrays),
        in_specs=[_ANY] * n, out_specs=[_ANY] * n, scratch_shapes=_comm_scratch(n),
    )(*arrays)


def _hosted_gather(in_refs, out_refs, sems, first, middle, last):
    if not in_refs:
        return
    for when, index in ((first, 0), (middle, 1), (last, 2)):
        @pl.when(when)
        def _(index=index):
            _two_level_gather_steps(in_refs, out_refs, *sems)[index]()


def _comm(kind, arrays, name):
    n = len(arrays)

    def body(*refs):
        remote, local = _peer_copies(kind, refs[:n], refs[n:2 * n], *refs[2 * n:])
        for cp in local + remote:
            cp.start()
        for cp in remote:
            cp.wait()
        for cp in local:
            cp.wait()

    return pl.pallas_call(
        body, name=name, out_shape=_comm_out_shapes(kind, arrays),
        in_specs=[_ANY] * n, out_specs=[_ANY] * n, scratch_shapes=_comm_scratch(n),
    )(*arrays)


def _all_reduce_small(part):
    rows, lanes = part.shape
    vmem = pl.BlockSpec(memory_space=pltpu.VMEM)

    def body(x_ref, gath_ref, sum_ref, send_sems, recv_sems):
        me = _my_place()
        my = _block_id(me)
        gath_ref[my] = x_ref[...]
        copies = []
        for k in range(1, N_DEV):
            cp = pltpu.make_async_remote_copy(
                src_ref=x_ref, dst_ref=gath_ref.at[my], send_sem=send_sems.at[k - 1], recv_sem=recv_sems.at[k - 1],
                device_id=_flip(me, k), device_id_type=_MESH)
            cp.start()
            copies.append(cp)
        for cp in copies:
            cp.wait()
        acc = gath_ref[0]
        for i in range(1, N_DEV):
            acc = acc + gath_ref[i]
        sum_ref[...] = acc

    return pl.pallas_call(
        body, name="small_grad_all_reduce",
        out_shape=[jax.ShapeDtypeStruct((N_DEV, rows, lanes), F32), jax.ShapeDtypeStruct((rows, lanes), F32)],
        in_specs=[vmem], out_specs=[vmem, vmem],
        scratch_shapes=[pltpu.SemaphoreType.DMA((7,)), pltpu.SemaphoreType.DMA((7,))],
    )(part)[1]


_SHARDED = (("w_in", (D_MODEL, IN_PROJ // N_DEV)), ("w_uq", (Q_RANK // N_DEV, Q_RANK)), ("w_ukv", (KV_RANK, HEAD_PAD)),
            ("conv_w", (CONV_W, CONV_DIM // N_DEV)), ("w_out", (D_MODEL // N_DEV, D_MODEL)),
            ("w_up", (D_MODEL, D_FF // N_DEV)), ("w_down", (D_FF // N_DEV, D_MODEL)))
_SMALL = (("pre_mix_norm", D_MODEL), ("q_norm", Q_RANK), ("kv_norm", KV_RANK), ("conv_b", CONV_DIM), ("dt_bias", SSD_HEADS),
          ("a_log", SSD_HEADS), ("d_skip", SSD_HEADS), ("ssd_norm", SSD_INNER), ("post_mix_norm", D_MODEL),
          ("pre_mlp_norm", D_MODEL), ("post_mlp_norm", D_MODEL))
_WEIGHT_ORDER = ("pre_mix_norm", "w_in", "q_norm", "w_uq", "kv_norm", "w_ukv", "conv_w", "conv_b", "dt_bias", "a_log", "d_skip",
                 "ssd_norm", "w_out", "post_mix_norm", "pre_mlp_norm", "w_up", "w_down", "post_mlp_norm")
_EARLY = ("w_in", "w_uq", "w_ukv", "conv_w")
_LATE = ("w_out", "w_up", "w_down")


def _wire_shard(name, a):
    return lax.bitcast_convert_type(a, BF16).reshape(CONV_W, -1) if name == "conv_w" else a.astype(BF16)


def _from_wire(name, g):
    return lax.bitcast_convert_type(g.reshape(N_DEV, CONV_W, -1, 2), F32) if name == "conv_w" else g


def _cols(stacked):
    return jnp.transpose(stacked, (1, 0, 2)).reshape(stacked.shape[1], -1)


def _early_weights(sh):
    w_in = _cols(sh["w_in"])
    zeros = lambda n: jnp.zeros((D_MODEL, n), BF16)
    s1, s2, s3, s4, s5 = 768, 1024, 1056, 1568, 2592
    win = jnp.concatenate([w_in[:, :s2], zeros(MISC_ROPE), w_in[:, s2:s3], w_in[:, s5:], zeros(HEAD_PAD - MISC_DT - SSD_HEADS),
                           w_in[:, s3:s5]], axis=1)
    w_uq = sh["w_uq"].reshape(Q_RANK, MLA_HEADS, QK_NOPE + QK_ROPE)
    wuq = jnp.pad(w_uq, ((0, 0), (0, 0), (0, HEAD_PAD - QK_NOPE - QK_ROPE))).reshape(Q_RANK, -1)
    w_ukv = _cols(sh["w_ukv"]).reshape(KV_RANK, MLA_HEADS, QK_NOPE + V_DIM)
    wkn = jnp.pad(w_ukv[..., :QK_NOPE], ((0, 0), (0, 0), (0, HEAD_PAD - QK_NOPE))).reshape(KV_RANK, -1)
    wv = w_ukv[..., QK_NOPE:].reshape(KV_RANK, 4, 2, 1, V_DIM) * jnp.eye(2, dtype=BF16).reshape(1, 1, 2, 2, 1)
    wkv = jnp.concatenate([wkn, wv.reshape(KV_RANK, -1)], axis=1)
    return dict(win=win, wuq=wuq, wkv=wkv, conv_w=_cols(sh["conv_w"]))


def _late_weights(sh):
    w_out = sh["w_out"].reshape(D_MODEL, D_MODEL)
    watt = w_out[:SSD_INNER].reshape(4, 2, 1, V_DIM, D_MODEL) * jnp.eye(2, dtype=BF16).reshape(1, 2, 2, 1, 1)
    wout = jnp.concatenate([watt.reshape(MLA_HEADS * HEAD_PAD, D_MODEL), w_out[SSD_INNER:]], axis=0)
    return dict(wout=wout, wup=sh["w_up"], wdown=sh["w_down"])


def _shard_grads(g):
    out = {}
    if "wup" in g:
        out["w_up"], out["w_down"] = g["wup"], g["wdown"]
        ae = g["wout_att"].reshape(4, 2, 2, V_DIM, D_MODEL)
        att = jnp.stack([ae[:, 0, 0], ae[:, 1, 1]], axis=1).reshape(SSD_INNER, D_MODEL)
        out["w_out"] = jnp.concatenate([att, g["wout_ssd"]], axis=0).astype(BF16).reshape(N_DEV, D_MODEL // N_DEV, D_MODEL)
    if "win" not in g:
        return out
    dwin = g["win"]
    s1, s2 = 768, 1024
    m0 = s2
    w_in = jnp.concatenate([dwin[:, :s2], dwin[:, m0 + MISC_ROPE:m0 + MISC_ROPE + QK_ROPE], dwin[:, 1152:2688],
                            dwin[:, m0 + MISC_DT:m0 + MISC_DT + SSD_HEADS]], axis=1)
    out["w_in"] = jnp.transpose(w_in.astype(BF16).reshape(D_MODEL, N_DEV, -1), (1, 0, 2))
    w_uq = g["wuq"].astype(BF16).reshape(Q_RANK, MLA_HEADS, HEAD_PAD)[..., :QK_NOPE + QK_ROPE].reshape(Q_RANK, Q_RANK)
    out["w_uq"] = w_uq.reshape(N_DEV, Q_RANK // N_DEV, Q_RANK)
    wide = MLA_HEADS * HEAD_PAD
    wkv = g["wkv"].astype(BF16)
    kn = wkv[:, :wide].reshape(KV_RANK, MLA_HEADS, HEAD_PAD)[..., :QK_NOPE]
    ve = wkv[:, wide:].reshape(KV_RANK, 4, 2, 2, V_DIM)
    vv = jnp.stack([ve[:, :, 0, 0], ve[:, :, 1, 1]], axis=2).reshape(KV_RANK, MLA_HEADS, V_DIM)
    out["w_ukv"] = jnp.transpose(jnp.concatenate([kn, vv], axis=-1), (1, 0, 2))
    out["conv_w"] = jnp.transpose(g["conv_w"].astype(BF16).reshape(CONV_W, N_DEV, -1), (1, 0, 2))
    return out


def _small_rows(n):
    return -(-n // 1024) * 8


def _pack_small(vals):
    rows = []
    for l in range(DEPTH):
        for name, n in _SMALL:
            r = _small_rows(n)
            rows.append(jnp.pad(vals[name][l].reshape(-1), (0, r * 128 - n)).reshape(r, 128))
    return jnp.concatenate(rows, axis=0)


def _unpack_small(packed):
    out, off = {name: [] for name, _ in _SMALL}, 0
    for l in range(DEPTH):
        for name, n in _SMALL:
            r = _small_rows(n)
            out[name].append(packed[off:off + r].reshape(-1)[:n])
            off += r
    return {name: jnp.stack(v) for name, v in out.items()}


def _lane_rows(vec8):
    return jnp.repeat(vec8, SSD_P).reshape(1, SSD_INNER)


def _layer_fwd(h, kw, sm, l, cosf, sinf, consts, gather=(), after_gather=None, target=None):
    row = lambda name: sm[name][l].reshape(1, -1)
    t = {}
    t["h0"] = h
    t["ub"], t["cq"], t["ckv"], t["misc"], t["z"], t["xraw"] = _inproj_fwd(h, row("pre_mix_norm"), kw["win"])
    t["cqn"], t["ckvn"], t["q"], t["k"], t["v"] = _qkv_fwd(t["cq"], t["ckv"], t["misc"], row("q_norm"), row("kv_norm"),
                                                         kw["wuq"], kw["wkv"], cosf, sinf)
    t["oe"], t["lse"], gathered = _attn_fwd(t["q"], t["k"], t["v"], gather)
    if after_gather is not None:
        after_gather(gathered)
    t["dtb"] = _lane_rows(sm["dt_bias"][l])
    t["a_exp"] = _lane_rows(-jnp.exp(sm["a_log"][l]))
    t["d_exp"] = _lane_rows(sm["d_skip"][l])
    t["c"], t["prev"], t["ypre"], t["yssd"] = _ssd_fwd(t["xraw"], t["misc"], t["z"], kw["conv_w"], row("conv_b"), t["dtb"],
                                                     t["a_exp"], t["d_exp"], row("ssd_norm"), consts)
    t["mixed"], t["h1"] = _outproj_fwd(t["oe"], t["yssd"], kw["wout"], h, row("post_mix_norm"))
    t["mb"], t["d"], *out = _mlp_fwd(t["h1"], row("pre_mlp_norm"), kw["wup"], kw["wdown"], row("post_mlp_norm"), target)
    return out, t


def _layer_bwd(dh2, t, kw, sm, l, cosf, sinf, consts, exchange_of=None):
    row = lambda name: sm[name][l].reshape(1, -1)
    g, gs = {}, {}
    dh1, dab, rb, ddb, gs["post_mlp_norm"], gs["pre_mlp_norm"] = _mlp_bwd(
        dh2, t["d"], t["h1"], t["mb"], row("pre_mlp_norm"), kw["wup"], kw["wdown"], row("post_mlp_norm"))
    g["wup"] = _matmul_tn_stacked(t["mb"], dab, f"dw_up_{l}", a_stacked=False)
    g["wdown"] = _matmul_tn_stacked(rb, ddb, f"dw_down_{l}", a_stacked=True)
    dmixb, doe, dyssd, gs["post_mix_norm"], delta = _outproj_bwd(dh1, t["mixed"], row("post_mix_norm"), kw["wout"], t["oe"])
    g["wout_att"] = _matmul_tn(t["oe"], dmixb, f"dw_out_att_{l}")
    g["wout_ssd"] = _matmul_tn(t["yssd"], dmixb, f"dw_out_ssd_{l}")
    dz, dxraw, dmisc_dt, gs["ssd_norm"], gd, galog, gdtb, g["conv_w"], gs["conv_b"] = _ssd_bwd(
        dyssd, t["ypre"], t["z"], t["c"], t["xraw"], t["misc"], t["prev"], kw["conv_w"], t["dtb"], t["a_exp"], t["d_exp"],
        row("ssd_norm"), consts)
    gs["d_skip"] = jnp.sum(gd.reshape(SSD_HEADS, SSD_P), axis=1)
    gs["a_log"] = galog[0, MISC_DT:MISC_DT + SSD_HEADS]
    gs["dt_bias"] = gdtb[0, MISC_DT:MISC_DT + SSD_HEADS]
    dq, dk, dv, exchanged = _attn_bwd(t["q"], t["k"], t["v"], doe, t["lse"], delta,
                                      exchange_of(g) if exchange_of is not None else ())
    dqb, dkvb, dcq, dckv, dmisc_rope, gs["q_norm"], gs["kv_norm"] = _qkv_bwd(
        dq, dk, dv, t["cq"], t["ckv"], row("q_norm"), row("kv_norm"), kw["wuq"], kw["wkv"], cosf, sinf)
    g["wuq"] = _matmul_tn(t["cqn"], dqb, f"dw_uq_{l}")
    g["wkv"] = _matmul_tn(t["ckvn"], dkvb, f"dw_kv_{l}")
    dprojb, dh0, gs["pre_mix_norm"] = _inproj_bwd(dcq, dckv, dmisc_rope, dmisc_dt, dz, dxraw, t["h0"], dh1,
                                                  row("pre_mix_norm"), kw["win"])
    g["win"] = _matmul_tn(t["ub"], dprojb, f"dw_in_{l}")
    return dh0, g, {k: v.reshape(-1) for k, v in gs.items()}, exchanged


def _local_step(x, positions, kws, sm, target, gather=(), after_gather=None, exchange_of=None):
    inv_freq = ROPE_THETA ** (-jnp.arange(0, QK_ROPE, 2, dtype=F32) / QK_ROPE)
    invf = jnp.zeros((HEAD_PAD,), F32).at[MISC_ROPE:MISC_ROPE + QK_ROPE].set(jnp.concatenate([inv_freq, inv_freq]))
    cosf, sinf = _rope_tables(positions.reshape(-1, 1), invf.reshape(1, HEAD_PAD))
    consts = _ssd_consts()
    (h,), t0 = _layer_fwd(x, kws[0], sm, 0, cosf, sinf, consts, gather, after_gather)
    (dh, loss), t1 = _layer_fwd(h, kws[1], sm, 1, cosf, sinf, consts, target=target)
    saved = [t0, t1]
    grads, small, exchanged = [None] * DEPTH, [None] * DEPTH, []
    for l in reversed(range(DEPTH)):
        hook = (lambda g0: exchange_of(g0, grads[1])) if (l == 0 and exchange_of is not None) else None
        dh, grads[l], small[l], got = _layer_bwd(dh, saved[l], kws[l], sm, l, cosf, sinf, consts, hook)
        exchanged = got or exchanged
    return loss[0, 0], dh, grads, small, exchanged


def kernel(x, positions, pre_mix_norm, w_in, q_norm, w_uq, kv_norm, w_ukv, conv_w, conv_b, dt_bias, a_log, d_skip, ssd_norm, w_out, post_mix_norm, pre_mlp_norm, w_up, w_down, post_mlp_norm, loss_target, m_pre_mix_norm, m_w_in, m_q_norm, m_w_uq, m_kv_norm, m_w_ukv, m_conv_w, m_conv_b, m_dt_bias, m_a_log, m_d_skip, m_ssd_norm, m_w_out, m_post_mix_norm, m_pre_mlp_norm, m_w_up, m_w_down, m_post_mlp_norm, v_pre_mix_norm, v_w_in, v_q_norm, v_w_uq, v_kv_norm, v_w_ukv, v_conv_w, v_conv_b, v_dt_bias, v_a_log, v_d_skip, v_ssd_norm, v_w_out, v_post_mix_norm, v_pre_mlp_norm, v_w_up, v_w_down, v_post_mlp_norm):
    w = dict(pre_mix_norm=pre_mix_norm, w_in=w_in, q_norm=q_norm, w_uq=w_uq, kv_norm=kv_norm, w_ukv=w_ukv, conv_w=conv_w,
             conv_b=conv_b, dt_bias=dt_bias, a_log=a_log, d_skip=d_skip, ssd_norm=ssd_norm, w_out=w_out,
             post_mix_norm=post_mix_norm, pre_mlp_norm=pre_mlp_norm, w_up=w_up, w_down=w_down, post_mlp_norm=post_mlp_norm)
    m = dict(pre_mix_norm=m_pre_mix_norm, w_in=m_w_in, q_norm=m_q_norm, w_uq=m_w_uq, kv_norm=m_kv_norm, w_ukv=m_w_ukv,
             conv_w=m_conv_w, conv_b=m_conv_b, dt_bias=m_dt_bias, a_log=m_a_log, d_skip=m_d_skip, ssd_norm=m_ssd_norm,
             w_out=m_w_out, post_mix_norm=m_post_mix_norm, pre_mlp_norm=m_pre_mlp_norm, w_up=m_w_up, w_down=m_w_down,
             post_mlp_norm=m_post_mlp_norm)
    v = dict(pre_mix_norm=v_pre_mix_norm, w_in=v_w_in, q_norm=v_q_norm, w_uq=v_w_uq, kv_norm=v_kv_norm, w_ukv=v_w_ukv,
             conv_w=v_conv_w, conv_b=v_conv_b, dt_bias=v_dt_bias, a_log=v_a_log, d_skip=v_d_skip, ssd_norm=v_ssd_norm,
             w_out=v_w_out, post_mix_norm=v_post_mix_norm, pre_mlp_norm=v_pre_mlp_norm, w_up=v_w_up, w_down=v_w_down,
             post_mlp_norm=v_post_mlp_norm)
    sm = {name: w[name] for name, _ in _SMALL}

    wire = lambda name, l: _wire_shard(name, w[name][l])
    first = _gather_two_level([wire(name, 0) for name in _EARLY], "weight_gather_first")
    kws = [_early_weights({name: _from_wire(name, a) for name, a in zip(_EARLY, first)}), None]
    behind = [(name, 0) for name in _LATE] + [(name, 1) for name, _ in _SHARDED]

    def after_gather(gathered):
        got = {key: _from_wire(key[0], a) for key, a in zip(behind, gathered)}
        kws[0].update(_late_weights({name: got[name, 0] for name in _LATE}))
        kws[1] = {**_early_weights({name: got[name, 1] for name in _EARLY}),
                  **_late_weights({name: got[name, 1] for name in _LATE})}

    sent_behind = [(name, 1) for name, _ in _SHARDED] + [(name, 0) for name in _LATE]

    def exchange_of(g0, g1):
        blocks = {**{(name, 1): a for name, a in _shard_grads(g1).items()},
                  **{(name, 0): a for name, a in _shard_grads(g0).items()}}
        return [blocks[key] for key in sent_behind]

    loss_part, dx, grads, small, exchanged = _local_step(
        x[0], positions[0], kws, sm, loss_target[0], [wire(*key) for key in behind], after_gather, exchange_of)
    slots = dict(zip(sent_behind, exchanged))
    last = _shard_grads({k: grads[0][k] for k in ("win", "wuq", "wkv", "conv_w")})
    slots.update({(name, 0): a for name, a in zip(_EARLY, _comm("exchange", [last[name] for name in _EARLY], "grad_exchange_last"))})
    g_small = _unpack_small(_all_reduce_small(_pack_small({name: jnp.stack([small[l][name] for l in range(DEPTH)])
                                                           for name, _ in _SMALL})))
    loss = lax.psum(loss_part, ("x", "y", "c"))

    grad, delta, new_m, new_v = {}, {}, {}, {}
    for name, _ in _SHARDED:
        grad[name], delta[name], new_m[name], new_v[name] = _sum_adamw(
            [slots[name, 0], slots[name, 1]], w[name], m[name], v[name], f"sum_adamw_{name}")
    pk = lambda d: _pack_small({name: d[name] for name, _ in _SMALL})
    d_, m_, v_ = _adamw(pk(w), pk(g_small), pk(m), pk(v), "adamw_small")
    for dst, packed in ((delta, d_), (new_m, m_), (new_v, v_)):
        dst.update(_unpack_small(packed))
    grad.update(g_small)

    outs = [loss, dx[None]]
    for d in (grad, delta, new_m, new_v):
        outs += [d[name] for name in _WEIGHT_ORDER]
    return tuple(outs)
```

```python
import jax
import jax.numpy as jnp
import numpy as np
from jax import lax
from jax.experimental import pallas as pl
from jax.experimental.pallas import tpu as pltpu

F32 = jnp.float32
BF16 = jnp.bfloat16
HI = lax.Precision.HIGHEST

D_MODEL = 1024
DEPTH = 2
N_DEV = 8
CHUNK = 64
EPS = 1e-6
MLA_HEADS = 8
QK_NOPE = 64
QK_ROPE = 32
V_DIM = 64
Q_RANK = 768
KV_RANK = 256
ROPE_THETA = 10000.0
SSD_HEADS = 8
SSD_P = 64
SSD_INNER = 512
SSD_GROUPS = 2
SSD_N = 128
CONV_W = 4
CONV_DIM = 1024
D_FF = 4096
IN_PROJ = 2600
HEAD_PAD = 128
IN_PAD = 2688
MISC_ROPE = 64
MISC_DT = 96
ATT_SCALE = (QK_NOPE + QK_ROPE) ** -0.5
LOG2E = 1.4426950408889634
ATT_SCALE_LOG2 = ATT_SCALE * LOG2E

ADAM_LR = 0.001
ADAM_B1 = 0.9
ADAM_B2 = 0.999
ADAM_EPS = 1e-08
ADAM_WD = 0.01
ADAM_STEP = 10

TM = 512
TQ = 256
ATT_T = 512
ATT_G = 4
SSD_ROWS = 256
TK_DW = 4096
VMEM_LIMIT = 56 * 1024 * 1024

_NT = (((1,), (1,)), ((), ()))
_TN = (((0,), (0,)), ((), ()))


def _params(**kw):
    return pltpu.CompilerParams(vmem_limit_bytes=VMEM_LIMIT, **kw)


def _dot(a, b, precision=None):
    return jnp.dot(a, b, preferred_element_type=F32, precision=precision)


def _dot_nt(a, b, precision=None):
    return lax.dot_general(a, b, _NT, preferred_element_type=F32, precision=precision)


def _dot_tn(a, b, precision=None):
    return lax.dot_general(a, b, _TN, preferred_element_type=F32, precision=precision)


def _split3(x):
    hi = x.astype(BF16)
    r = x - hi.astype(F32)
    mid = r.astype(BF16)
    return hi, mid, (r - mid.astype(F32)).astype(BF16)


def _dot01(x, m01, dot=_dot, left=False):
    parts = [dot(m01, p) if left else dot(p, m01) for p in _split3(x)]
    return parts[0] + parts[1] + parts[2]


def _full(shape):
    n = len(shape)
    return pl.BlockSpec(shape, lambda *_: (0,) * n)


def _resident(shape):
    n = len(shape)
    return pl.BlockSpec(shape, lambda *_: (0,) * n, pipeline_mode=pl.Buffered(1))


def _rows(tm, width):
    return pl.BlockSpec((tm, width), lambda i: (i, 0))


def _rms_fwd(x, w):
    r = lax.rsqrt(jnp.mean(x * x, axis=-1, keepdims=True) + EPS)
    return (x * r) * w


def _rms_bwd(x, w, dy):
    r = lax.rsqrt(jnp.mean(x * x, axis=-1, keepdims=True) + EPS)
    xh = x * r
    dxn = dy * w
    dx = r * (dxn - xh * jnp.mean(dxn * xh, axis=-1, keepdims=True))
    return dx, dy * xh


def _acc_rows(ref, val, first):
    s = jnp.sum(val, axis=0, keepdims=True)

    @pl.when(first)
    def _():
        ref[...] = s

    @pl.when(jnp.logical_not(first))
    def _():
        ref[...] += s


def _rope(t, cosf, sinf, sign):
    lane = lax.broadcasted_iota(jnp.int32, t.shape, 1)
    rot = jnp.where(lane < MISC_ROPE + QK_ROPE // 2, -pltpu.roll(t, HEAD_PAD - QK_ROPE // 2, 1), pltpu.roll(t, QK_ROPE // 2, 1))
    return t * cosf + sign * (rot * sinf)


def _rope_tables(pos, invf):
    s = pos.shape[0]

    def body(pos_ref, invf_ref, cos_ref, sin_ref):
        ang = pos_ref[...].astype(F32) * invf_ref[...]
        cos_ref[...] = jnp.cos(ang)
        sin_ref[...] = jnp.sin(ang)

    return pl.pallas_call(
        body, name="rope_tables", grid=(s // TM,),
        in_specs=[_rows(TM, 1), _full((1, HEAD_PAD))],
        out_specs=[_rows(TM, HEAD_PAD), _rows(TM, HEAD_PAD)],
        out_shape=[jax.ShapeDtypeStruct((s, HEAD_PAD), F32)] * 2,
    )(pos, invf)


def _inproj_fwd(h, nw, win):
    s = h.shape[0]

    def body(h_ref, nw_ref, w_ref, ub_ref, cq_ref, ckv_ref, misc_ref, z_ref, xbc_ref):
        ub = _rms_fwd(h_ref[...], nw_ref[...]).astype(BF16)
        ub_ref[...] = ub
        proj = _dot(ub, w_ref[...])
        cq_ref[...] = proj[:, 0:768]
        ckv_ref[...] = proj[:, 768:1024]
        misc_ref[...] = proj[:, 1024:1152]
        z_ref[...] = proj[:, 1152:1664]
        xbc_ref[...] = proj[:, 1664:2688]

    widths = (768, 256, 128, 512, 1024)
    return pl.pallas_call(
        body, name="inproj_fwd", grid=(s // TM,),
        in_specs=[_rows(TM, D_MODEL), _full((1, D_MODEL)), _resident((D_MODEL, IN_PAD))],
        out_specs=[_rows(TM, D_MODEL)] + [_rows(TM, w) for w in widths],
        out_shape=[jax.ShapeDtypeStruct((s, D_MODEL), BF16)] + [jax.ShapeDtypeStruct((s, w), F32) for w in widths],
        compiler_params=_params(),
    )(h, nw, win)


def _qkv_fwd(cq, ckv, misc, qnw, kvnw, wuq, wkv, cosf, sinf):
    s = cq.shape[0]

    def body(cq_ref, ckv_ref, misc_ref, qnw_ref, kvnw_ref, wuq_ref, wkv_ref, cos_ref, sin_ref,
             cqn_ref, ckvn_ref, q_ref, k_ref, v_ref):
        cosf, sinf = cos_ref[...], sin_ref[...]
        cqn = _rms_fwd(cq_ref[...], qnw_ref[...]).astype(BF16)
        cqn_ref[...] = cqn
        q = _dot(cqn, wuq_ref[...])
        ckvn = _rms_fwd(ckv_ref[...], kvnw_ref[...]).astype(BF16)
        ckvn_ref[...] = ckvn
        kv = _dot(ckvn, wkv_ref[...])
        m = misc_ref[...]
        lane = lax.broadcasted_iota(jnp.int32, m.shape, 1)
        in_rope = jnp.logical_and(lane >= MISC_ROPE, lane < MISC_ROPE + QK_ROPE)
        kr = jnp.where(in_rope, _rope(m, cosf, sinf, 1.0), 0.0)
        for hd in range(MLA_HEADS):
            cols = slice(hd * HEAD_PAD, (hd + 1) * HEAD_PAD)
            q_ref[:, cols] = _rope(q[:, cols], cosf, sinf, 1.0).astype(BF16)
            k_ref[:, cols] = (kv[:, cols] + kr).astype(BF16)
        vv = kv[:, MLA_HEADS * HEAD_PAD:]
        vlane = lax.broadcasted_iota(jnp.int32, vv.shape, 1)
        ones_at = jnp.where((vlane // HEAD_PAD) % 2 == 0, V_DIM, 0)
        v_ref[...] = jnp.where(vlane % HEAD_PAD == ones_at, 1.0, vv).astype(BF16)

    wide = MLA_HEADS * HEAD_PAD
    return pl.pallas_call(
        body, name="qkv_fwd", grid=(s // TM,),
        in_specs=[_rows(TM, Q_RANK), _rows(TM, KV_RANK), _rows(TM, HEAD_PAD), _full((1, Q_RANK)), _full((1, KV_RANK)),
                  _resident((Q_RANK, wide)), _resident((KV_RANK, 2 * wide)), _rows(TM, HEAD_PAD), _rows(TM, HEAD_PAD)],
        out_specs=[_rows(TM, Q_RANK), _rows(TM, KV_RANK), _rows(TM, wide), _rows(TM, wide), _rows(TM, wide)],
        out_shape=[jax.ShapeDtypeStruct((s, Q_RANK), BF16), jax.ShapeDtypeStruct((s, KV_RANK), BF16)]
        + [jax.ShapeDtypeStruct((s, wide), BF16)] * 3,
        compiler_params=_params(),
    )(cq, ckv, misc, qnw, kvnw, wuq, wkv, cosf, sinf)


def _chunk_bias(t, keys_on_rows=False):
    row = lax.broadcasted_iota(jnp.int32, (t, 1), 0) // CHUNK
    col = lax.broadcasted_iota(jnp.int32, (1, t), 1) // CHUNK
    return jnp.where((row <= col) if keys_on_rows else (col <= row), 0.0, -jnp.inf).astype(F32)


def _attn_fwd(q, k, v, gather=()):
    s = q.shape[0]
    t = ATT_T
    nq = s // t
    pair = ATT_G * HEAD_PAD
    ng = len(gather)

    def body(q_ref, k_ref, v_ref, *rest):
        g_in, (o_ref, lse_ref), g_out = rest[:ng], rest[ng:ng + 2], rest[ng + 2:2 * ng + 2]
        m_s, acc_s, bias_s = rest[2 * ng + 2:2 * ng + 5]
        qi = pl.program_id(1)
        group, groups = pl.program_id(0), MLA_HEADS // ATT_G

        @pl.when(jnp.logical_and(group == 0, qi == 0))
        def _():
            bias_s[...] = _chunk_bias(t)

        _hosted_gather(g_in, g_out, rest[2 * ng + 5:],
                       jnp.logical_and(group == 0, qi == 0),
                       jnp.logical_and(group == groups // 2, qi == nq // 2),
                       jnp.logical_and(group == groups - 1, qi == nq - 1))
        m_s[...] = jnp.full(m_s.shape, -jnp.inf, F32)
        acc_s[...] = jnp.zeros(acc_s.shape, F32)

        def step(kb, masked):
            r0 = pl.multiple_of(kb * t, t)

            def scores(hh):
                cols = slice(hh * HEAD_PAD, (hh + 1) * HEAD_PAD)
                return _dot_nt(q_ref[:, cols], k_ref[pl.ds(r0, t), cols])

            def soft(hh, raw):
                sc = raw * ATT_SCALE_LOG2
                if masked:
                    sc = sc + bias_s[...]
                m_old = m_s[hh]
                m_new = jnp.maximum(m_old, jnp.max(sc, axis=-1, keepdims=True))
                alpha = jnp.exp2(m_old - m_new)
                p = jnp.exp2(sc - jnp.tile(m_new, (1, t // HEAD_PAD)))
                m_s[hh] = m_new
                return alpha, p.astype(BF16)

            def update(hh, alpha, p):
                cols = slice(hh * HEAD_PAD, (hh + 1) * HEAD_PAD)
                acc_s[hh] = alpha * acc_s[hh] + _dot(p, v_ref[pl.ds(r0, t), cols])

            raw, ap = [None] * ATT_G, [None] * ATT_G
            raw[0] = scores(0)
            for hh in range(ATT_G):
                if hh + 1 < ATT_G:
                    raw[hh + 1] = scores(hh + 1)
                ap[hh] = soft(hh, raw[hh])
                if hh >= 1:
                    update(hh - 1, *ap[hh - 1])
            update(ATT_G - 1, *ap[ATT_G - 1])

        def loop(kb, c):
            step(kb, False)
            return c

        lax.fori_loop(0, qi, loop, 0)
        step(qi, True)
        for hh in range(ATT_G):
            cols = slice(hh * HEAD_PAD, (hh + 1) * HEAD_PAD)
            acc = acc_s[hh]
            ones_at = V_DIM * (1 - hh % 2)
            l = jnp.broadcast_to(acc[:, ones_at:ones_at + 1], acc.shape)
            o_ref[:, cols] = (acc / l).astype(BF16)
            lse_ref[hh] = (m_s[hh] + jnp.log(l) * LOG2E).T[0:8, :]

    outs = pl.pallas_call(
        body, name="attn_fwd_gather" if ng else "attn_fwd", grid=(MLA_HEADS // ATT_G, nq),
        in_specs=[pl.BlockSpec((t, pair), lambda h, i: (i, h)),
                  pl.BlockSpec((s, pair), lambda h, i: (0, h)),
                  pl.BlockSpec((s, pair), lambda h, i: (0, h))] + [_ANY] * ng,
        out_specs=[pl.BlockSpec((t, pair), lambda h, i: (i, h)),
                   pl.BlockSpec((ATT_G, 8, t), lambda h, i: (h, 0, i))] + [_ANY] * ng,
        out_shape=[jax.ShapeDtypeStruct((s, MLA_HEADS * HEAD_PAD), BF16), jax.ShapeDtypeStruct((MLA_HEADS, 8, s), F32)]
        + _comm_out_shapes("gather", gather),
        scratch_shapes=[pltpu.VMEM((ATT_G, t, HEAD_PAD), F32), pltpu.VMEM((ATT_G, t, HEAD_PAD), F32), pltpu.VMEM((t, t), F32)]
        + (_comm_scratch(ng) if ng else []),
        compiler_params=_params(),
    )(q, k, v, *gather)
    return outs[0], outs[1], list(outs[2:])


def _interleave(stages):
    live = list(stages)
    while live:
        still = []
        for g in live:
            try:
                next(g)
                still.append(g)
            except StopIteration:
                pass
        live = still


def _ssd_consts():
    emisc = np.zeros((HEAD_PAD, SSD_INNER), np.float32)
    for hd in range(SSD_HEADS):
        emisc[MISC_DT + hd, hd * SSD_P:(hd + 1) * SSD_P] = 1.0
    idx = np.arange(CHUNK)
    tri = (idx[:, None] >= idx[None, :]).astype(np.float32)
    return tuple(jnp.asarray(m, BF16) for m in (emisc, emisc.T.copy(), tri, tri.T.copy()))


def _ssd_chunk_common(cc, misc, emisc, tri, trit, dtb, a_exp):
    xa = cc * jax.nn.sigmoid(cc)
    dtr = _dot01(misc, emisc) + dtb
    dt = jax.nn.softplus(dtr)
    a = dt * a_exp
    acs = _dot01(a, tri, left=True)
    acs_t = _dot01(a, trit, dot=_dot_tn)
    alast = acs[CHUNK - 1:CHUNK, :]
    return xa, dtr, dt, acs, acs_t, alast


def _decay(acs, acs_t, hd):
    row = lax.broadcasted_iota(jnp.int32, (CHUNK, CHUNK), 0)
    col = lax.broadcasted_iota(jnp.int32, (CHUNK, CHUNK), 1)
    diff = acs[:, hd * SSD_P:hd * SSD_P + 1] - acs_t[hd * SSD_P:hd * SSD_P + 1, :]
    return jnp.exp(jnp.where(row >= col, diff, -jnp.inf))


def _half_mask(hh):
    lane = lax.broadcasted_iota(jnp.int32, (CHUNK, 2 * SSD_P), 1)
    return (lane >= SSD_P) if hh else (lane < SSD_P)


def _gate_norm(y, zz, nw):
    yz = y * (zz * jax.nn.sigmoid(zz))
    outs, rs = [], []
    half = SSD_INNER // SSD_GROUPS
    for g in range(SSD_GROUPS):
        yg = yz[:, g * half:(g + 1) * half]
        r = lax.rsqrt(jnp.mean(yg * yg, axis=-1, keepdims=True) + EPS)
        outs.append(yg * r)
        rs.append(r)
    return yz, jnp.concatenate(outs, axis=1), rs


def _ssd_fwd(xraw, misc, z, cw, cb, dtb, a_exp, d_exp, nw, consts):
    s = xraw.shape[0]
    nb = s // SSD_ROWS
    ncb = SSD_ROWS // CHUNK
    emisc, _, tri, trit = consts

    def body(x_ref, misc_ref, z_ref, cw_ref, cb_ref, dtb_ref, a_ref, d_ref, nw_ref, emisc_ref, tri_ref, trit_ref,
             c_ref, prev_ref, ypre_ref, yssd_ref, tail_s, state_s):
        i = pl.program_id(0)

        @pl.when(i == 0)
        def _():
            tail_s[...] = jnp.zeros(tail_s.shape, F32)
            state_s[...] = jnp.zeros(state_s.shape, F32)

        x = x_ref[...]
        xext = jnp.concatenate([tail_s[...], x], axis=0)
        acc = x * cw_ref[CONV_W - 1:CONV_W, :] + cb_ref[...]
        for j in range(1, CONV_W):
            acc = acc + pltpu.roll(xext, j, 0)[8:, :] * cw_ref[CONV_W - 1 - j:CONV_W - j, :]
        tail_s[...] = x[SSD_ROWS - 8:, :]
        c_ref[...] = acc

        def chunk(ci):
            r0 = ci * CHUNK
            xa, _, dt, acs, acs_t, alast = _ssd_chunk_common(
                c_ref[pl.ds(r0, CHUNK), :], misc_ref[pl.ds(r0, CHUNK), :], emisc_ref[...], tri_ref[...], trit_ref[...],
                dtb_ref[...], a_ref[...])
            yield
            xs = xa[:, :SSD_INNER]
            xdt = xs * dt
            wgt = (xdt * jnp.exp(alast - acs)).astype(BF16)
            e = jnp.exp(acs)
            ys, new_states, cms = [], [], []
            for g in range(SSD_GROUPS):
                bm = xa[:, SSD_INNER + g * SSD_N:SSD_INNER + (g + 1) * SSD_N].astype(BF16)
                cm = xa[:, SSD_INNER + SSD_GROUPS * SSD_N + g * SSD_N:SSD_INNER + SSD_GROUPS * SSD_N + (g + 1) * SSD_N].astype(BF16)
                cms.append(cm)
                cb_g = _dot_nt(cm, bm)
                gl = slice(g * 256, (g + 1) * 256)
                new_states.append(_dot_tn(bm, wgt[:, gl]))
                for jj in range(2):
                    pair = 2 * g + jj
                    xp = xdt[:, pair * 128:(pair + 1) * 128]
                    yp = None
                    for hh in range(2):
                        sc = (cb_g * _decay(acs, acs_t, 2 * pair + hh)).astype(BF16)
                        term = _dot(sc, jnp.where(_half_mask(hh), xp, 0.0).astype(BF16))
                        yp = term if yp is None else yp + term
                    ys.append(yp)
                yield
            prev = state_s[...]
            prev_ref[ci] = prev
            yoff = jnp.concatenate([_dot(cms[g], prev[:, g * 256:(g + 1) * 256].astype(BF16)) for g in range(SSD_GROUPS)],
                                   axis=1) * e
            state_s[...] = prev * jnp.exp(alast) + jnp.concatenate(new_states, axis=1)
            yield
            y = jnp.concatenate(ys, axis=1) + yoff + d_ref[...] * xs
            ypre_ref[pl.ds(r0, CHUNK), :] = y
            _, yn, _ = _gate_norm(y, z_ref[pl.ds(r0, CHUNK), :], None)
            yssd_ref[pl.ds(r0, CHUNK), :] = (yn * nw_ref[...]).astype(BF16)

        _interleave([chunk(ci) for ci in range(ncb)])

    return pl.pallas_call(
        body, name="ssd_fwd", grid=(nb,),
        in_specs=[_rows(SSD_ROWS, CONV_DIM), _rows(SSD_ROWS, HEAD_PAD), _rows(SSD_ROWS, SSD_INNER),
                  _full((CONV_W, CONV_DIM)), _full((1, CONV_DIM)), _full((1, SSD_INNER)), _full((1, SSD_INNER)),
                  _full((1, SSD_INNER)), _full((1, SSD_INNER)), _full((HEAD_PAD, SSD_INNER)), _full((CHUNK, CHUNK)),
                  _full((CHUNK, CHUNK))],
        out_specs=[_rows(SSD_ROWS, CONV_DIM), pl.BlockSpec((ncb, SSD_N, SSD_INNER), lambda i: (i, 0, 0)),
                   _rows(SSD_ROWS, SSD_INNER), _rows(SSD_ROWS, SSD_INNER)],
        out_shape=[jax.ShapeDtypeStruct((s, CONV_DIM), F32), jax.ShapeDtypeStruct((s // CHUNK, SSD_N, SSD_INNER), F32),
                   jax.ShapeDtypeStruct((s, SSD_INNER), F32), jax.ShapeDtypeStruct((s, SSD_INNER), BF16)],
        scratch_shapes=[pltpu.VMEM((8, CONV_DIM), F32), pltpu.VMEM((SSD_N, SSD_INNER), F32)],
        compiler_params=_params(),
    )(xraw, misc, z, cw, cb, dtb, a_exp, d_exp, nw, emisc, tri, trit)


def _outproj_fwd(oe, yssd, wout, h, nw):
    s = h.shape[0]
    wide = MLA_HEADS * HEAD_PAD

    def body(oe_ref, y_ref, w_ref, h_ref, nw_ref, mixed_ref, h1_ref):
        mixed = _dot(oe_ref[...], w_ref[0:wide, :]) + _dot(y_ref[...], w_ref[wide:, :])
        mixed_ref[...] = mixed
        h1_ref[...] = h_ref[...] + _rms_fwd(mixed, nw_ref[...])

    return pl.pallas_call(
        body, name="outproj_fwd", grid=(s // TM,),
        in_specs=[_rows(TM, wide), _rows(TM, SSD_INNER), _resident((wide + SSD_INNER, D_MODEL)), _rows(TM, D_MODEL),
                  _full((1, D_MODEL))],
        out_specs=[_rows(TM, D_MODEL), _rows(TM, D_MODEL)],
        out_shape=[jax.ShapeDtypeStruct((s, D_MODEL), F32)] * 2,
        compiler_params=_params(),
    )(oe, yssd, wout, h, nw)


def _mlp_fwd(h1, prew, wup, wdown, postw, target=None):
    s = h1.shape[0]
    fb = D_FF // N_DEV
    last = target is not None

    def body(h_ref, prew_ref, up_ref, down_ref, postw_ref, *rest):
        hh = h_ref[...]
        mb = _rms_fwd(hh, prew_ref[...]).astype(BF16)
        rest[-3 - last][...] = mb
        d = jnp.zeros((TM, D_MODEL), F32)
        for j in range(N_DEV):
            a = _dot(mb, up_ref[j])
            r = jnp.square(jnp.maximum(a, 0.0)).astype(BF16)
            d = d + _dot(r, down_ref[j])
        rest[-2 - last][...] = d
        h2 = hh + _rms_fwd(d, postw_ref[...])
        if last:
            diff = h2 - rest[0][...]
            rest[-2][...] = diff * (1.0 / D_MODEL)
            part = 0.5 * jnp.sum(jnp.mean(diff * diff, axis=-1, keepdims=True), axis=0, keepdims=True)
            _acc_rows(rest[-1], part, pl.program_id(0) == 0)
        else:
            rest[-1][...] = h2

    return pl.pallas_call(
        body, name="mlp_fwd_loss" if last else "mlp_fwd", grid=(s // TM,),
        in_specs=[_rows(TM, D_MODEL), _full((1, D_MODEL)), _resident((N_DEV, D_MODEL, fb)), _resident((N_DEV, fb, D_MODEL)),
                  _full((1, D_MODEL))] + ([_rows(TM, D_MODEL)] if last else []),
        out_specs=[_rows(TM, D_MODEL)] * 3 + ([_full((1, 1))] if last else []),
        out_shape=[jax.ShapeDtypeStruct((s, D_MODEL), BF16), jax.ShapeDtypeStruct((s, D_MODEL), F32),
                   jax.ShapeDtypeStruct((s, D_MODEL), F32)] + ([jax.ShapeDtypeStruct((1, 1), F32)] if last else []),
        compiler_params=_params(),
    )(h1, prew, wup, wdown, postw, *([target] if last else []))


def _mlp_bwd(dh2, d, h1, mb, prew, wup, wdown, postw):
    s = dh2.shape[0]
    fb = D_FF // N_DEV
    tm = TM // 2

    def body(dh2_ref, d_ref, h1_ref, mb_ref, prew_ref, up_ref, down_ref, postw_ref,
             dh1_ref, da_ref, r_ref, dd_ref, gpost_ref, gpre_ref):
        first = pl.program_id(0) == 0
        dh2 = dh2_ref[...]
        dd, gpost = _rms_bwd(d_ref[...], postw_ref[...], dh2)
        _acc_rows(gpost_ref, gpost, first)
        ddb = dd.astype(BF16)
        dd_ref[...] = ddb
        mb = mb_ref[...]
        def products(j):
            return _dot(mb, up_ref[j]), _dot_nt(ddb, down_ref[j])

        def pointwise(j, a, dr):
            a = jnp.maximum(a, 0.0)
            r_ref[j] = jnp.square(a).astype(BF16)
            da = (dr * (2.0 * a)).astype(BF16)
            da_ref[j] = da
            return da

        dm = jnp.zeros((tm, D_MODEL), F32)
        nxt, da_prev = products(0), None
        for j in range(N_DEV):
            cur = nxt
            if j + 1 < N_DEV:
                nxt = products(j + 1)
            da = pointwise(j, *cur)
            if da_prev is not None:
                dm = dm + _dot_nt(da_prev, up_ref[j - 1])
            da_prev = da
        dm = dm + _dot_nt(da_prev, up_ref[N_DEV - 1])
        dx, gpre = _rms_bwd(h1_ref[...], prew_ref[...], dm)
        _acc_rows(gpre_ref, gpre, first)
        dh1_ref[...] = dh2 + dx

    stacked = pl.BlockSpec((N_DEV, tm, fb), lambda i: (0, i, 0))
    return pl.pallas_call(
        body, name="mlp_bwd", grid=(s // tm,),
        in_specs=[_rows(tm, D_MODEL)] * 4 + [_full((1, D_MODEL)), _resident((N_DEV, D_MODEL, fb)), _resident((N_DEV, fb, D_MODEL)),
                                              _full((1, D_MODEL))],
        out_specs=[_rows(tm, D_MODEL), stacked, stacked, _rows(tm, D_MODEL), _full((1, D_MODEL)), _full((1, D_MODEL))],
        out_shape=[jax.ShapeDtypeStruct((s, D_MODEL), F32), jax.ShapeDtypeStruct((N_DEV, s, fb), BF16),
                   jax.ShapeDtypeStruct((N_DEV, s, fb), BF16), jax.ShapeDtypeStruct((s, D_MODEL), BF16),
                   jax.ShapeDtypeStruct((1, D_MODEL), F32), jax.ShapeDtypeStruct((1, D_MODEL), F32)],
        compiler_params=_params(),
    )(dh2, d, h1, mb, prew, wup, wdown, postw)


def _matmul_tn(a, b, name, tk=TK_DW):
    s, m = a.shape
    n = b.shape[1]
    tn = n if n <= 1024 else (n // 2 if (n // 2) % 128 == 0 else n // 3)
    tk = min(tk, s)
    assert n % tn == 0 and tn % 128 == 0 and s % tk == 0

    def body(a_ref, b_ref, o_ref):
        part = _dot_tn(a_ref[...], b_ref[...])

        @pl.when(pl.program_id(1) == 0)
        def _():
            o_ref[...] = part

        @pl.when(pl.program_id(1) != 0)
        def _():
            o_ref[...] += part

    return pl.pallas_call(
        body, name=name, grid=(n // tn, s // tk),
        in_specs=[pl.BlockSpec((tk, m), lambda j, k: (k, 0)), pl.BlockSpec((tk, tn), lambda j, k: (k, j))],
        out_specs=pl.BlockSpec((m, tn), lambda j, k: (0, j)),
        out_shape=jax.ShapeDtypeStruct((m, n), F32),
        compiler_params=_params(),
    )(a, b)


def _matmul_tn_stacked(a, b, name, a_stacked, tk=TK_DW):
    tk = min(tk, a.shape[-2])
    if a_stacked:
        _, s, m = a.shape
        n = b.shape[1]
        in_specs = [pl.BlockSpec((1, tk, m), lambda j, k: (j, k, 0)), pl.BlockSpec((tk, n), lambda j, k: (k, 0))]
    else:
        s, m = a.shape
        n = b.shape[2]
        in_specs = [pl.BlockSpec((tk, m), lambda j, k: (k, 0)), pl.BlockSpec((1, tk, n), lambda j, k: (j, k, 0))]

    nk = s // tk

    def body(a_ref, b_ref, o_ref, acc_s):
        av = a_ref[0] if a_stacked else a_ref[...]
        bv = b_ref[...] if a_stacked else b_ref[0]
        part = _dot_tn(av, bv)
        k = pl.program_id(1)

        @pl.when(k == 0)
        def _():
            acc_s[...] = part

        @pl.when(jnp.logical_and(k != 0, k != nk - 1))
        def _():
            acc_s[...] += part

        @pl.when(k == nk - 1)
        def _():
            o_ref[0] = (part if nk == 1 else acc_s[...] + part).astype(BF16)

    return pl.pallas_call(
        body, name=name, grid=(N_DEV, nk),
        in_specs=in_specs,
        out_specs=pl.BlockSpec((1, m, n), lambda j, k: (j, 0, 0)),
        out_shape=jax.ShapeDtypeStruct((N_DEV, m, n), BF16),
        scratch_shapes=[pltpu.VMEM((m, n), F32)],
        compiler_params=_params(),
    )(a, b)


def _outproj_bwd(dh1, mixed, nw, wout, oe):
    s = dh1.shape[0]
    wide = MLA_HEADS * HEAD_PAD

    def body(dh1_ref, mixed_ref, nw_ref, w_ref, oe_ref, dmix_ref, doe_ref, dy_ref, gnw_ref, delta_ref):
        dmix, gnw = _rms_bwd(mixed_ref[...], nw_ref[...], dh1_ref[...])
        _acc_rows(gnw_ref, gnw, pl.program_id(0) == 0)
        dmb = dmix.astype(BF16)
        dmix_ref[...] = dmb
        doe_ref[...] = _dot_nt(dmb, w_ref[0:wide, :]).astype(BF16)
        dy_ref[...] = _dot_nt(dmb, w_ref[wide:, :])
        ones = jnp.ones((8, HEAD_PAD), BF16)
        for hd in range(MLA_HEADS):
            cols = slice(hd * HEAD_PAD, (hd + 1) * HEAD_PAD)
            prod = oe_ref[:, cols].astype(F32) * doe_ref[:, cols].astype(F32)
            delta_ref[hd] = _dot01(prod, ones, dot=_dot_nt, left=True)

    return pl.pallas_call(
        body, name="outproj_bwd", grid=(s // TM,),
        in_specs=[_rows(TM, D_MODEL), _rows(TM, D_MODEL), _full((1, D_MODEL)), _resident((wide + SSD_INNER, D_MODEL)),
                  _rows(TM, wide)],
        out_specs=[_rows(TM, D_MODEL), _rows(TM, wide), _rows(TM, SSD_INNER), _full((1, D_MODEL)),
                   pl.BlockSpec((MLA_HEADS, 8, TM), lambda i: (0, 0, i))],
        out_shape=[jax.ShapeDtypeStruct((s, D_MODEL), BF16), jax.ShapeDtypeStruct((s, wide), BF16),
                   jax.ShapeDtypeStruct((s, SSD_INNER), F32), jax.ShapeDtypeStruct((1, D_MODEL), F32),
                   jax.ShapeDtypeStruct((MLA_HEADS, 8, s), F32)],
        compiler_params=_params(),
    )(dh1, mixed, nw, wout, oe)


def _attn_bwd(q, k, v, do, lse, delta, exchange=()):
    s = q.shape[0]
    t = ATT_T
    nq = s // t
    pair = 2 * HEAD_PAD
    ne = len(exchange)

    def body(q_ref, k_ref, v_ref, do_ref, lse_ref, delta_ref, *rest):
        e_in, (dq_ref, dk_ref, dv_ref), e_out = rest[:ne], rest[ne:ne + 3], rest[ne + 3:2 * ne + 3]
        dk_s, dv_s, bias_s = rest[2 * ne + 3:2 * ne + 6]
        kb = pl.program_id(1)
        _hosted_comm("exchange", e_in, e_out, rest[2 * ne + 6:],
                     jnp.logical_and(pl.program_id(0) == 0, kb == 0),
                     jnp.logical_and(pl.program_id(0) == MLA_HEADS // 2 - 1, kb == nq - 1))

        @pl.when(jnp.logical_and(pl.program_id(0) == 0, kb == 0))
        def _():
            bias_s[...] = _chunk_bias(t, keys_on_rows=True)

        @pl.when(kb == 0)
        def _():
            dq_ref[...] = jnp.zeros(dq_ref.shape, F32)

        def step(qb, diagonal):
            r0 = pl.multiple_of(qb * t, t)
            for hh in range(2):
                cols = slice(hh * HEAD_PAD, (hh + 1) * HEAD_PAD)
                kk = k_ref[:, cols]
                qq = q_ref[pl.ds(r0, t), cols]
                dd = do_ref[pl.ds(r0, t), cols]
                sc = _dot_nt(kk, qq) * ATT_SCALE_LOG2
                if diagonal:
                    sc = sc + bias_s[...]
                p = jnp.exp2(sc - lse_ref[hh, 0:1, pl.ds(r0, t)])
                dv = _dot(p.astype(BF16), dd)
                dp = _dot_nt(v_ref[:, cols], dd)
                ds = (p * (dp - delta_ref[hh, 0:1, pl.ds(r0, t)]) * ATT_SCALE).astype(BF16)
                dk = _dot(ds, qq)
                if diagonal:
                    dv_s[:, cols] = dv
                    dk_s[:, cols] = dk
                else:
                    dv_s[:, cols] += dv
                    dk_s[:, cols] += dk
                dq_ref[pl.ds(r0, t), cols] += _dot_tn(ds, kk)

        def loop(qb, c):
            step(qb, False)
            return c

        step(kb, True)
        lax.fori_loop(kb + 1, nq, loop, 0)
        dk_ref[...] = dk_s[...].astype(BF16)
        dv_ref[...] = dv_s[...].astype(BF16)

    whole = pl.BlockSpec((s, pair), lambda h, i: (0, h))
    tile = pl.BlockSpec((t, pair), lambda h, i: (i, h))
    rowvec = pl.BlockSpec((2, 8, s), lambda h, i: (h, 0, 0))
    wide = MLA_HEADS * HEAD_PAD
    outs = pl.pallas_call(
        body, name="attn_bwd_exchange" if ne else "attn_bwd", grid=(MLA_HEADS // 2, nq),
        in_specs=[whole, tile, tile, whole, rowvec, rowvec] + [_ANY] * ne,
        out_specs=[whole, tile, tile] + [_ANY] * ne,
        out_shape=[jax.ShapeDtypeStruct((s, wide), F32)] + [jax.ShapeDtypeStruct((s, wide), BF16)] * 2
        + _comm_out_shapes("exchange", exchange),
        scratch_shapes=[pltpu.VMEM((t, pair), F32), pltpu.VMEM((t, pair), F32), pltpu.VMEM((t, t), F32)]
        + (_comm_scratch(ne) if ne else []),
        compiler_params=_params(),
    )(q, k, v, do, lse, delta, *exchange)
    return outs[0], outs[1], outs[2], list(outs[3:])


def _ssd_bwd(dy, ypre, z, c, xraw, misc, prev, cw, dtb, a_exp, d_exp, nw, consts):
    s = dy.shape[0]
    nb = s // SSD_ROWS
    ncb = SSD_ROWS // CHUNK
    emisc, emisc_t, tri, trit = consts

    def body(dy_ref, ypre_ref, z_ref, c_ref, x_ref, xprev_ref, misc_ref, prev_ref, cw_ref, dtb_ref, a_ref, d_ref, nw_ref,
             emisc_ref, emisct_ref, tri_ref, trit_ref,
             dz_ref, dx_ref, dmisc_ref, gnw_ref, gd_ref, galog_ref, gdtb_ref, gcw_ref, gcb_ref,
             dst_s, dc_s, head_s):
        i = pl.program_id(0)
        first = i == 0

        @pl.when(first)
        def _():
            dst_s[...] = jnp.zeros(dst_s.shape, F32)
            head_s[...] = jnp.zeros(head_s.shape, F32)
            gnw_ref[...] = jnp.zeros(gnw_ref.shape, F32)
            gd_ref[...] = jnp.zeros(gd_ref.shape, F32)
            galog_ref[...] = jnp.zeros(galog_ref.shape, F32)
            gdtb_ref[...] = jnp.zeros(gdtb_ref.shape, F32)

        a_exp_v = a_ref[...]
        a8 = _dot01(a_exp_v, emisct_ref[...]) * (1.0 / SSD_P)

        def chunk(ci):
            r0 = ci * CHUNK
            cc = c_ref[pl.ds(r0, CHUNK), :]
            mm = misc_ref[pl.ds(r0, CHUNK), :]
            xa, dtr, dt, acs, acs_t, alast = _ssd_chunk_common(cc, mm, emisc_ref[...], tri_ref[...], trit_ref[...],
                                                              dtb_ref[...], a_exp_v)
            yield
            xs = xa[:, :SSD_INNER]
            xdt = xs * dt
            y = ypre_ref[pl.ds(r0, CHUNK), :]
            zz = z_ref[pl.ds(r0, CHUNK), :]
            yz, yn, rs = _gate_norm(y, zz, None)
            dyo = dy_ref[pl.ds(r0, CHUNK), :]
            gnw_ref[...] += jnp.sum(dyo * yn, axis=0, keepdims=True)
            dyn = dyo * nw_ref[...]
            half = SSD_INNER // SSD_GROUPS
            dyz_parts = []
            for g in range(SSD_GROUPS):
                gl = slice(g * half, (g + 1) * half)
                dyz_parts.append(rs[g] * (dyn[:, gl] - yn[:, gl] * jnp.mean(dyn[:, gl] * yn[:, gl], axis=-1, keepdims=True)))
            dyz = jnp.concatenate(dyz_parts, axis=1)
            sg = jax.nn.sigmoid(zz)
            dz_ref[pl.ds(r0, CHUNK), :] = dyz * y * (sg * (1.0 + zz * (1.0 - sg)))
            dyp = dyz * (zz * sg)
            dypb = dyp.astype(BF16)
            gd_ref[...] += jnp.sum(dyp * xs, axis=0, keepdims=True)
            yield
            prev = prev_ref[ci]
            cd = jnp.exp(alast)
            e = jnp.exp(acs)
            dsx = jnp.exp(alast - acs)
            wgt = (xdt * dsx).astype(BF16)
            dze = (dyp * e).astype(BF16)
            dprev_parts, diag_all, dbm, dcm, yoff_parts, bms = [], [], [], [], [], []
            lane8 = lax.broadcasted_iota(jnp.int32, (CHUNK, HEAD_PAD), 1)
            diag8 = jnp.zeros((CHUNK, HEAD_PAD), F32)
            for g in range(SSD_GROUPS):
                gl = slice(g * 256, (g + 1) * 256)
                bm = xa[:, SSD_INNER + g * SSD_N:SSD_INNER + (g + 1) * SSD_N].astype(BF16)
                cm = xa[:, SSD_INNER + SSD_GROUPS * SSD_N + g * SSD_N:SSD_INNER + SSD_GROUPS * SSD_N + (g + 1) * SSD_N].astype(BF16)
                bms.append(bm)
                prev_g = prev[:, gl].astype(BF16)
                dcm_g = _dot_nt(dze[:, gl], prev_g)
                dprev_parts.append(_dot_tn(cm, dze[:, gl]))
                cb_g = _dot_nt(cm, bm)
                dcb = jnp.zeros((CHUNK, CHUNK), F32)
                diag_parts = []
                for jj in range(2):
                    pair = 2 * g + jj
                    pl_ = slice(pair * 128, (pair + 1) * 128)
                    xp = xdt[:, pl_]
                    dyp_p = dypb[:, pl_]
                    dxp = jnp.zeros((CHUNK, 128), F32)
                    for hh in range(2):
                        hd = 2 * pair + hh
                        dec = _decay(acs, acs_t, hd)
                        xm = jnp.where(_half_mask(hh), xp, 0.0).astype(BF16)
                        dsc = _dot_nt(dyp_p, xm) * dec
                        dcb = dcb + dsc
                        sc = (cb_g * dec).astype(BF16)
                        dxp = dxp + jnp.where(_half_mask(hh), _dot_tn(sc, dyp_p), 0.0)
                        dm = dsc * cb_g
                        diag8 = diag8 + jnp.where(lane8 == MISC_DT + hd, jnp.sum(dm - dm.T, axis=1, keepdims=True), 0.0)
                    diag_parts.append(dxp)
                dcbb = dcb.astype(BF16)
                dcm.append(dcm_g + _dot(dcbb, bm))
                dbm.append(_dot_tn(dcbb, cm))
                diag_all.append(jnp.concatenate(diag_parts, axis=1))
                yoff_parts.append(_dot(cm, prev_g) * e[:, gl])
                yield
            dst = dst_s[...]
            glast = jnp.sum(dst * prev, axis=0, keepdims=True) * cd
            dxdt_state_parts = []
            for g in range(SSD_GROUPS):
                gl = slice(g * 256, (g + 1) * 256)
                dst_g = dst[:, gl].astype(BF16)
                dxdt_state_parts.append(_dot(bms[g], dst_g) * dsx[:, gl])
                dbm[g] = dbm[g] + _dot_nt(wgt[:, gl], dst_g)
            dst_s[...] = dst * cd + jnp.concatenate(dprev_parts, axis=1)
            yield
            dxdt_state = jnp.concatenate(dxdt_state_parts, axis=1)
            dxdt = jnp.concatenate(diag_all, axis=1) + dxdt_state
            dacs = dyp * jnp.concatenate(yoff_parts, axis=1) - xdt * dxdt_state
            last = jnp.sum(xdt * dxdt_state, axis=0, keepdims=True) + glast
            row = lax.broadcasted_iota(jnp.int32, (CHUNK, SSD_INNER), 0)
            dacs = dacs + jnp.where(row == CHUNK - 1, last, 0.0)
            dacs8 = _dot01(dacs, emisct_ref[...]) + diag8
            da8 = _dot01(dacs8, trit_ref[...], left=True)
            ddt8 = da8 * a8 + _dot01(dxdt * xs, emisct_ref[...])
            yield
            dtr8 = mm + _dot01(dtb_ref[...], emisct_ref[...]) * (1.0 / SSD_P)
            dt8 = jax.nn.softplus(dtr8)
            lane = lax.broadcasted_iota(jnp.int32, (CHUNK, HEAD_PAD), 1)
            on_dt = jnp.logical_and(lane >= MISC_DT, lane < MISC_DT + SSD_HEADS)
            ddtr8 = jnp.where(on_dt, ddt8 * jax.nn.sigmoid(dtr8), 0.0)
            dmisc_ref[pl.ds(r0, CHUNK), :] = ddtr8
            gdtb_ref[...] += jnp.sum(ddtr8, axis=0, keepdims=True)
            galog_ref[...] += jnp.sum(jnp.where(on_dt, da8 * dt8, 0.0), axis=0, keepdims=True) * a8
            dxs = d_ref[...] * dyp + dxdt * dt
            dxa = jnp.concatenate([dxs] + dbm + dcm, axis=1)
            sc_ = jax.nn.sigmoid(cc)
            dc_s[pl.ds(r0, CHUNK), :] = dxa * (sc_ * (1.0 + cc * (1.0 - sc_)))

        _interleave([chunk(ci) for ci in reversed(range(ncb))])

        dc = dc_s[...]
        dcext = jnp.concatenate([dc, head_s[...]], axis=0)
        dx = dc * cw_ref[CONV_W - 1:CONV_W, :]
        for j in range(1, CONV_W):
            dx = dx + pltpu.roll(dcext, SSD_ROWS + 8 - j, 0)[:SSD_ROWS, :] * cw_ref[CONV_W - 1 - j:CONV_W - j, :]
        dx_ref[...] = dx
        head_s[...] = dc[:8, :]
        xprev = jnp.where(i == nb - 1, 0.0, xprev_ref[...])
        xext = jnp.concatenate([xprev, x_ref[...]], axis=0)
        rows = [jnp.sum(dc * pltpu.roll(xext, CONV_W - 1 - kk, 0)[8:, :], axis=0, keepdims=True) for kk in range(CONV_W)]
        gcw = jnp.concatenate(rows, axis=0)

        @pl.when(first)
        def _():
            gcw_ref[...] = gcw
            gcb_ref[...] = jnp.sum(dc, axis=0, keepdims=True)

        @pl.when(jnp.logical_not(first))
        def _():
            gcw_ref[...] += gcw
            gcb_ref[...] += jnp.sum(dc, axis=0, keepdims=True)

    def rev(width):
        return pl.BlockSpec((SSD_ROWS, width), lambda i: (nb - 1 - i, 0))

    per8 = SSD_ROWS // 8
    return pl.pallas_call(
        body, name="ssd_bwd", grid=(nb,),
        in_specs=[rev(SSD_INNER), rev(SSD_INNER), rev(SSD_INNER), rev(CONV_DIM), rev(CONV_DIM),
                  pl.BlockSpec((8, CONV_DIM), lambda i: (jnp.maximum((nb - 1 - i) * per8 - 1, 0), 0)),
                  rev(HEAD_PAD), pl.BlockSpec((ncb, SSD_N, SSD_INNER), lambda i: (nb - 1 - i, 0, 0)),
                  _full((CONV_W, CONV_DIM)), _full((1, SSD_INNER)), _full((1, SSD_INNER)), _full((1, SSD_INNER)),
                  _full((1, SSD_INNER)), _full((HEAD_PAD, SSD_INNER)), _full((SSD_INNER, HEAD_PAD)), _full((CHUNK, CHUNK)),
                  _full((CHUNK, CHUNK))],
        out_specs=[rev(SSD_INNER), rev(CONV_DIM), rev(HEAD_PAD), _full((1, SSD_INNER)), _full((1, SSD_INNER)),
                   _full((1, HEAD_PAD)), _full((1, HEAD_PAD)), _full((CONV_W, CONV_DIM)), _full((1, CONV_DIM))],
        out_shape=[jax.ShapeDtypeStruct((s, SSD_INNER), F32), jax.ShapeDtypeStruct((s, CONV_DIM), F32),
                   jax.ShapeDtypeStruct((s, HEAD_PAD), F32), jax.ShapeDtypeStruct((1, SSD_INNER), F32),
                   jax.ShapeDtypeStruct((1, SSD_INNER), F32), jax.ShapeDtypeStruct((1, HEAD_PAD), F32),
                   jax.ShapeDtypeStruct((1, HEAD_PAD), F32), jax.ShapeDtypeStruct((CONV_W, CONV_DIM), F32),
                   jax.ShapeDtypeStruct((1, CONV_DIM), F32)],
        scratch_shapes=[pltpu.VMEM((SSD_N, SSD_INNER), F32), pltpu.VMEM((SSD_ROWS, CONV_DIM), F32), pltpu.VMEM((8, CONV_DIM), F32)],
        compiler_params=_params(),
    )(dy, ypre, z, c, xraw, xraw, misc, prev, cw, dtb, a_exp, d_exp, nw, emisc, emisc_t, tri, trit)


def _qkv_bwd(dq, dk, dv, cq, ckv, qnw, kvnw, wuq, wkv, cosf, sinf):
    s = dq.shape[0]
    wide = MLA_HEADS * HEAD_PAD

    def body(dq_ref, dk_ref, dv_ref, cq_ref, ckv_ref, qnw_ref, kvnw_ref, wuq_ref, wkv_ref, cos_ref, sin_ref,
             dqb_ref, dkvb_ref, dcq_ref, dckv_ref, dmisc_ref, gq_ref, gkv_ref):
        first = pl.program_id(0) == 0
        cosf, sinf = cos_ref[...], sin_ref[...]
        dkr = jnp.zeros((TM, HEAD_PAD), F32)
        for hd in range(MLA_HEADS):
            cols = slice(hd * HEAD_PAD, (hd + 1) * HEAD_PAD)
            dqb_ref[:, cols] = _rope(dq_ref[:, cols], cosf, sinf, -1.0).astype(BF16)
            dkh = dk_ref[:, cols]
            dkvb_ref[:, cols] = dkh.astype(BF16)
            dkr = dkr + dkh
        dkvb_ref[:, wide:] = dv_ref[...].astype(BF16)
        lane = lax.broadcasted_iota(jnp.int32, dkr.shape, 1)
        in_rope = jnp.logical_and(lane >= MISC_ROPE, lane < MISC_ROPE + QK_ROPE)
        dmisc_ref[...] = jnp.where(in_rope, _rope(jnp.where(in_rope, dkr, 0.0), cosf, sinf, -1.0), 0.0)
        dcq, gq = _rms_bwd(cq_ref[...], qnw_ref[...], _dot_nt(dqb_ref[...], wuq_ref[...]))
        dcq_ref[...] = dcq
        _acc_rows(gq_ref, gq, first)
        dckv, gkv = _rms_bwd(ckv_ref[...], kvnw_ref[...], _dot_nt(dkvb_ref[...], wkv_ref[...]))
        dckv_ref[...] = dckv
        _acc_rows(gkv_ref, gkv, first)

    return pl.pallas_call(
        body, name="qkv_bwd", grid=(s // TM,),
        in_specs=[_rows(TM, wide)] * 3 + [_rows(TM, Q_RANK), _rows(TM, KV_RANK), _full((1, Q_RANK)), _full((1, KV_RANK)),
                                          _resident((Q_RANK, wide)), _resident((KV_RANK, 2 * wide)), _rows(TM, HEAD_PAD), _rows(TM, HEAD_PAD)],
        out_specs=[_rows(TM, wide), _rows(TM, 2 * wide), _rows(TM, Q_RANK), _rows(TM, KV_RANK), _rows(TM, HEAD_PAD),
                   _full((1, Q_RANK)), _full((1, KV_RANK))],
        out_shape=[jax.ShapeDtypeStruct((s, wide), BF16), jax.ShapeDtypeStruct((s, 2 * wide), BF16),
                   jax.ShapeDtypeStruct((s, Q_RANK), F32), jax.ShapeDtypeStruct((s, KV_RANK), F32),
                   jax.ShapeDtypeStruct((s, HEAD_PAD), F32), jax.ShapeDtypeStruct((1, Q_RANK), F32),
                   jax.ShapeDtypeStruct((1, KV_RANK), F32)],
        compiler_params=_params(),
    )(dq, dk, dv, cq, ckv, qnw, kvnw, wuq, wkv, cosf, sinf)


def _inproj_bwd(dcq, dckv, dmisc_rope, dmisc_dt, dz, dxbc, h, dh1, nw, win):
    s = h.shape[0]

    def body(dcq_ref, dckv_ref, dma_ref, dmb_ref, dz_ref, dxbc_ref, h_ref, dh1_ref, nw_ref, w_ref, dproj_ref, dh0_ref, gnw_ref):
        dproj_ref[:, 0:768] = dcq_ref[...].astype(BF16)
        dproj_ref[:, 768:1024] = dckv_ref[...].astype(BF16)
        dproj_ref[:, 1024:1152] = (dma_ref[...] + dmb_ref[...]).astype(BF16)
        dproj_ref[:, 1152:1664] = dz_ref[...].astype(BF16)
        dproj_ref[:, 1664:2688] = dxbc_ref[...].astype(BF16)
        du = _dot_nt(dproj_ref[...], w_ref[...])
        dx, gnw = _rms_bwd(h_ref[...], nw_ref[...], du)
        _acc_rows(gnw_ref, gnw, pl.program_id(0) == 0)
        dh0_ref[...] = dh1_ref[...] + dx

    return pl.pallas_call(
        body, name="inproj_bwd", grid=(s // TM,),
        in_specs=[_rows(TM, Q_RANK), _rows(TM, KV_RANK), _rows(TM, HEAD_PAD), _rows(TM, HEAD_PAD), _rows(TM, SSD_INNER),
                  _rows(TM, CONV_DIM), _rows(TM, D_MODEL), _rows(TM, D_MODEL), _full((1, D_MODEL)), _resident((D_MODEL, IN_PAD))],
        out_specs=[_rows(TM, IN_PAD), _rows(TM, D_MODEL), _full((1, D_MODEL))],
        out_shape=[jax.ShapeDtypeStruct((s, IN_PAD), BF16), jax.ShapeDtypeStruct((s, D_MODEL), F32),
                   jax.ShapeDtypeStruct((1, D_MODEL), F32)],
        compiler_params=_params(),
    )(dcq, dckv, dmisc_rope, dmisc_dt, dz, dxbc, h, dh1, nw, win)


def _row_tile(rows, cols):
    cap = max(8, (1 << 18) // max(cols, 128))
    best = None
    for t in range(8, rows + 1, 8):
        if rows % t == 0 and t <= cap:
            best = t
    return best if best is not None else rows


def _adamw(w, g, m, v, name):
    rows, cols = w.shape
    tr = _row_tile(rows, cols)

    def body(w_ref, g_ref, m_ref, v_ref, d_ref, m2_ref, v2_ref):
        gg = g_ref[...]
        m2 = ADAM_B1 * m_ref[...] + (1.0 - ADAM_B1) * gg
        v2 = ADAM_B2 * v_ref[...] + (1.0 - ADAM_B2) * jnp.square(gg)
        m_hat = m2 / (1.0 - ADAM_B1 ** ADAM_STEP)
        v_hat = v2 / (1.0 - ADAM_B2 ** ADAM_STEP)
        d_ref[...] = -ADAM_LR * (m_hat / (jnp.sqrt(v_hat) + ADAM_EPS) + ADAM_WD * w_ref[...])
        m2_ref[...] = m2
        v2_ref[...] = v2

    spec = pl.BlockSpec((tr, cols), lambda i: (i, 0))
    return pl.pallas_call(
        body, name=name, grid=(rows // tr,),
        in_specs=[spec] * 4, out_specs=[spec] * 3,
        out_shape=[jax.ShapeDtypeStruct((rows, cols), F32)] * 3,
    )(w, g, m, v)


def _sum_adamw(slots, w, m, v, name):
    _, rows, cols = w.shape
    tr = _row_tile(rows, cols)
    nb = rows // tr

    def body(s0_ref, s1_ref, w_ref, m_ref, v_ref, g_ref, d_ref, m2_ref, v2_ref):
        for l, ref in enumerate((s0_ref, s1_ref)):
            @pl.when(pl.program_id(0) == l)
            def _(ref=ref):
                acc = ref[0].astype(F32)
                for i in range(1, N_DEV):
                    acc = acc + ref[i].astype(F32)
                g_ref[...] = acc

        gg = g_ref[...]
        m2 = ADAM_B1 * m_ref[...] + (1.0 - ADAM_B1) * gg
        v2 = ADAM_B2 * v_ref[...] + (1.0 - ADAM_B2) * jnp.square(gg)
        m_hat = m2 / (1.0 - ADAM_B1 ** ADAM_STEP)
        v_hat = v2 / (1.0 - ADAM_B2 ** ADAM_STEP)
        d_ref[...] = -ADAM_LR * (m_hat / (jnp.sqrt(v_hat) + ADAM_EPS) + ADAM_WD * w_ref[...])
        m2_ref[...] = m2
        v2_ref[...] = v2

    slot_spec = lambda layer: pl.BlockSpec((N_DEV, tr, cols), lambda l, i: (0, jnp.where(l == layer, i, (nb - 1) * (1 - layer)), 0))
    spec = pl.BlockSpec((None, tr, cols), lambda l, i: (l, i, 0))
    return pl.pallas_call(
        body, name=name, grid=(DEPTH, nb),
        in_specs=[slot_spec(0), slot_spec(1), spec, spec, spec], out_specs=[spec] * 4,
        out_shape=[jax.ShapeDtypeStruct(w.shape, F32)] * 4,
        compiler_params=_params(),
    )(slots[0], slots[1], w, m, v)


_MESH = pl.DeviceIdType.MESH
_ANY = pl.BlockSpec(memory_space=pl.ANY)


def _my_place():
    return lax.axis_index("x"), lax.axis_index("y"), lax.axis_index("c")


def _flip(place, k):
    x, y, c = place
    return (1 - x if k & 4 else x, 1 - y if k & 2 else y, 1 - c if k & 1 else c)


def _block_id(place):
    return 4 * place[0] + 2 * place[1] + place[2]


def _peer_copies(kind, in_refs, out_refs, send_sems, recv_sems, local_sems):
    me = _my_place()
    my = _block_id(me)
    remote, local = [], []
    for a, (x_ref, out_ref) in enumerate(zip(in_refs, out_refs)):
        src_of = (lambda place, r=x_ref: r) if kind == "gather" else (lambda place, r=x_ref: r.at[_block_id(place)])
        local.append(pltpu.make_async_copy(src_of(me), out_ref.at[my], local_sems.at[a]))
        for k in range(1, N_DEV):
            peer = _flip(me, k)
            remote.append(pltpu.make_async_remote_copy(
                src_ref=src_of(peer), dst_ref=out_ref.at[my], send_sem=send_sems.at[a * 7 + k - 1],
                recv_sem=recv_sems.at[a * 7 + k - 1], device_id=peer, device_id_type=_MESH))
    return remote, local


def _comm_out_shapes(kind, arrays):
    return [jax.ShapeDtypeStruct((N_DEV, *a.shape) if kind == "gather" else a.shape, a.dtype) for a in arrays]


def _comm_scratch(n):
    return [pltpu.SemaphoreType.DMA((7 * n,)), pltpu.SemaphoreType.DMA((7 * n,)), pltpu.SemaphoreType.DMA((n,))]


def _hosted_comm(kind, in_refs, out_refs, sems, first, last):
    if not in_refs:
        return

    @pl.when(first)
    def _():
        remote, local = _peer_copies(kind, in_refs, out_refs, *sems)
        for cp in local + remote:
            cp.start()

    @pl.when(last)
    def _():
        remote, local = _peer_copies(kind, in_refs, out_refs, *sems)
        for cp in remote:
            cp.wait()
        for cp in local:
            cp.wait()


def _two_level_gather_steps(in_refs, out_refs, send_sems, recv_sems, local_sems):
    n = len(in_refs)
    me = _my_place()
    x, y, c = me
    sibling = (x, y, 1 - c)
    chips = [(1 - x, y), (x, 1 - y), (1 - x, 1 - y)]

    def copy(a, k, place, to, src=None):
        block = out_refs[a].at[_block_id(place)]
        return pltpu.make_async_remote_copy(
            src_ref=block if src is None else src, dst_ref=block, send_sem=send_sems.at[7 * a + k],
            recv_sem=recv_sems.at[7 * a + k], device_id=to, device_id_type=_MESH)

    mine = [pltpu.make_async_copy(in_refs[a], out_refs[a].at[_block_id(me)], local_sems.at[a]) for a in range(n)]
    first = [copy(a, 0, me, sibling, src=in_refs[a]) for a in range(n)]
    first += [copy(a, 1 + j, me, (*chip, c), src=in_refs[a]) for a in range(n) for j, chip in enumerate(chips)]
    passed = [copy(a, 4 + j, (*chip, c), sibling) for a in range(n) for j, chip in enumerate(chips)]

    def send():
        for cp in mine + first:
            cp.start()

    def forward():
        for a in range(n):
            for j, chip in enumerate(chips):
                copy(a, 1 + j, (*chip, c), me).wait_recv()
                passed[3 * a + j].start()

    def finish():
        for a in range(n):
            copy(a, 0, sibling, me).wait_recv()
            for j, chip in enumerate(chips):
                copy(a, 4 + j, (*chip, 1 - c), me).wait_recv()
        for cp in first + passed:
            cp.wait_send()
        for cp in mine:
            cp.wait()

    return send, forward, finish


def _gather_two_level(arrays, name):
    n = len(arrays)

    def body(*refs):
        for step in _two_level_gather_steps(refs[:n], refs[n:2 * n], *refs[2 * n:]):
            step()

    return pl.pallas_call(
        body, name=name, out_shape=_comm_out_shapes("gather", arrays),
        in_specs=[_ANY] * n, out_specs=[_ANY] * n, scratch_shapes=_comm_scratch(n),
    )(*arrays)


def _hosted_gather(in_refs, out_refs, sems, first, middle, last):
    if not in_refs:
        return
    for when, index in ((first, 0), (middle, 1), (last, 2)):
        @pl.when(when)
        def _(index=index):
            _two_level_gather_steps(in_refs, out_refs, *sems)[index]()


def _comm(kind, arrays, name):
    n = len(arrays)

    def body(*refs):
        remote, local = _peer_copies(kind, refs[:n], refs[n:2 * n], *refs[2 * n:])
        for cp in local + remote:
            cp.start()
        for cp in remote:
            cp.wait()
        for cp in local:
            cp.wait()

    return pl.pallas_call(
        body, name=name, out_shape=_comm_out_shapes(kind, arrays),
        in_specs=[_ANY] * n, out_specs=[_ANY] * n, scratch_shapes=_comm_scratch(n),
    )(*arrays)


def _all_reduce_small(part):
    rows, lanes = part.shape
    vmem = pl.BlockSpec(memory_space=pltpu.VMEM)

    def body(x_ref, gath_ref, sum_ref, send_sems, recv_sems):
        me = _my_place()
        my = _block_id(me)
        gath_ref[my] = x_ref[...]
        copies = []
        for k in range(1, N_DEV):
            cp = pltpu.make_async_remote_copy(
                src_ref=x_ref, dst_ref=gath_ref.at[my], send_sem=send_sems.at[k - 1], recv_sem=recv_sems.at[k - 1],
                device_id=_flip(me, k), device_id_type=_MESH)
            cp.start()
            copies.append(cp)
        for cp in copies:
            cp.wait()
        acc = gath_ref[0]
        for i in range(1, N_DEV):
            acc = acc + gath_ref[i]
        sum_ref[...] = acc

    return pl.pallas_call(
        body, name="small_grad_all_reduce",
        out_shape=[jax.ShapeDtypeStruct((N_DEV, rows, lanes), F32), jax.ShapeDtypeStruct((rows, lanes), F32)],
        in_specs=[vmem], out_specs=[vmem, vmem],
        scratch_shapes=[pltpu.SemaphoreType.DMA((7,)), pltpu.SemaphoreType.DMA((7,))],
    )(part)[1]


_SHARDED = (("w_in", (D_MODEL, IN_PROJ // N_DEV)), ("w_uq", (Q_RANK // N_DEV, Q_RANK)), ("w_ukv", (KV_RANK, HEAD_PAD)),
            ("conv_w", (CONV_W, CONV_DIM // N_DEV)), ("w_out", (D_MODEL // N_DEV, D_MODEL)),
            ("w_up", (D_MODEL, D_FF // N_DEV)), ("w_down", (D_FF // N_DEV, D_MODEL)))
_SMALL = (("pre_mix_norm", D_MODEL), ("q_norm", Q_RANK), ("kv_norm", KV_RANK), ("conv_b", CONV_DIM), ("dt_bias", SSD_HEADS),
          ("a_log", SSD_HEADS), ("d_skip", SSD_HEADS), ("ssd_norm", SSD_INNER), ("post_mix_norm", D_MODEL),
          ("pre_mlp_norm", D_MODEL), ("post_mlp_norm", D_MODEL))
_WEIGHT_ORDER = ("pre_mix_norm", "w_in", "q_norm", "w_uq", "kv_norm", "w_ukv", "conv_w", "conv_b", "dt_bias", "a_log", "d_skip",
                 "ssd_norm", "w_out", "post_mix_norm", "pre_mlp_norm", "w_up", "w_down", "post_mlp_norm")
_EARLY = ("w_in", "w_uq", "w_ukv", "conv_w")
_LATE = ("w_out", "w_up", "w_down")


def _wire_shard(name, a):
    return lax.bitcast_convert_type(a, BF16).reshape(CONV_W, -1) if name == "conv_w" else a.astype(BF16)


def _from_wire(name, g):
    return lax.bitcast_convert_type(g.reshape(N_DEV, CONV_W, -1, 2), F32) if name == "conv_w" else g


def _cols(stacked):
    return jnp.transpose(stacked, (1, 0, 2)).reshape(stacked.shape[1], -1)


def _early_weights(sh):
    w_in = _cols(sh["w_in"])
    zeros = lambda n: jnp.zeros((D_MODEL, n), BF16)
    s1, s2, s3, s4, s5 = 768, 1024, 1056, 1568, 2592
    win = jnp.concatenate([w_in[:, :s2], zeros(MISC_ROPE), w_in[:, s2:s3], w_in[:, s5:], zeros(HEAD_PAD - MISC_DT - SSD_HEADS),
                           w_in[:, s3:s5]], axis=1)
    w_uq = sh["w_uq"].reshape(Q_RANK, MLA_HEADS, QK_NOPE + QK_ROPE)
    wuq = jnp.pad(w_uq, ((0, 0), (0, 0), (0, HEAD_PAD - QK_NOPE - QK_ROPE))).reshape(Q_RANK, -1)
    w_ukv = _cols(sh["w_ukv"]).reshape(KV_RANK, MLA_HEADS, QK_NOPE + V_DIM)
    wkn = jnp.pad(w_ukv[..., :QK_NOPE], ((0, 0), (0, 0), (0, HEAD_PAD - QK_NOPE))).reshape(KV_RANK, -1)
    wv = w_ukv[..., QK_NOPE:].reshape(KV_RANK, 4, 2, 1, V_DIM) * jnp.eye(2, dtype=BF16).reshape(1, 1, 2, 2, 1)
    wkv = jnp.concatenate([wkn, wv.reshape(KV_RANK, -1)], axis=1)
    return dict(win=win, wuq=wuq, wkv=wkv, conv_w=_cols(sh["conv_w"]))


def _late_weights(sh):
    w_out = sh["w_out"].reshape(D_MODEL, D_MODEL)
    watt = w_out[:SSD_INNER].reshape(4, 2, 1, V_DIM, D_MODEL) * jnp.eye(2, dtype=BF16).reshape(1, 2, 2, 1, 1)
    wout = jnp.concatenate([watt.reshape(MLA_HEADS * HEAD_PAD, D_MODEL), w_out[SSD_INNER:]], axis=0)
    return dict(wout=wout, wup=sh["w_up"], wdown=sh["w_down"])


def _shard_grads(g):
    out = {}
    if "wup" in g:
        out["w_up"], out["w_down"] = g["wup"], g["wdown"]
        ae = g["wout_att"].reshape(4, 2, 2, V_DIM, D_MODEL)
        att = jnp.stack([ae[:, 0, 0], ae[:, 1, 1]], axis=1).reshape(SSD_INNER, D_MODEL)
        out["w_out"] = jnp.concatenate([att, g["wout_ssd"]], axis=0).astype(BF16).reshape(N_DEV, D_MODEL // N_DEV, D_MODEL)
    if "win" not in g:
        return out
    dwin = g["win"]
    s1, s2 = 768, 1024
    m0 = s2
    w_in = jnp.concatenate([dwin[:, :s2], dwin[:, m0 + MISC_ROPE:m0 + MISC_ROPE + QK_ROPE], dwin[:, 1152:2688],
                            dwin[:, m0 + MISC_DT:m0 + MISC_DT + SSD_HEADS]], axis=1)
    out["w_in"] = jnp.transpose(w_in.astype(BF16).reshape(D_MODEL, N_DEV, -1), (1, 0, 2))
    w_uq = g["wuq"].astype(BF16).reshape(Q_RANK, MLA_HEADS, HEAD_PAD)[..., :QK_NOPE + QK_ROPE].reshape(Q_RANK, Q_RANK)
    out["w_uq"] = w_uq.reshape(N_DEV, Q_RANK // N_DEV, Q_RANK)
    wide = MLA_HEADS * HEAD_PAD
    wkv = g["wkv"].astype(BF16)
    kn = wkv[:, :wide].reshape(KV_RANK, MLA_HEADS, HEAD_PAD)[..., :QK_NOPE]
    ve = wkv[:, wide:].reshape(KV_RANK, 4, 2, 2, V_DIM)
    vv = jnp.stack([ve[:, :, 0, 0], ve[:, :, 1, 1]], axis=2).reshape(KV_RANK, MLA_HEADS, V_DIM)
    out["w_ukv"] = jnp.transpose(jnp.concatenate([kn, vv], axis=-1), (1, 0, 2))
    out["conv_w"] = jnp.transpose(g["conv_w"].astype(BF16).reshape(CONV_W, N_DEV, -1), (1, 0, 2))
    return out


def _small_rows(n):
    return -(-n // 1024) * 8


def _pack_small(vals):
    rows = []
    for l in range(DEPTH):
        for name, n in _SMALL:
            r = _small_rows(n)
            rows.append(jnp.pad(vals[name][l].reshape(-1), (0, r * 128 - n)).reshape(r, 128))
    return jnp.concatenate(rows, axis=0)


def _unpack_small(packed):
    out, off = {name: [] for name, _ in _SMALL}, 0
    for l in range(DEPTH):
        for name, n in _SMALL:
            r = _small_rows(n)
            out[name].append(packed[off:off + r].reshape(-1)[:n])
            off += r
    return {name: jnp.stack(v) for name, v in out.items()}


def _lane_rows(vec8):
    return jnp.repeat(vec8, SSD_P).reshape(1, SSD_INNER)


def _layer_fwd(h, kw, sm, l, cosf, sinf, consts, gather=(), after_gather=None, target=None):
    row = lambda name: sm[name][l].reshape(1, -1)
    t = {}
    t["h0"] = h
    t["ub"], t["cq"], t["ckv"], t["misc"], t["z"], t["xraw"] = _inproj_fwd(h, row("pre_mix_norm"), kw["win"])
    t["cqn"], t["ckvn"], t["q"], t["k"], t["v"] = _qkv_fwd(t["cq"], t["ckv"], t["misc"], row("q_norm"), row("kv_norm"),
                                                         kw["wuq"], kw["wkv"], cosf, sinf)
    t["oe"], t["lse"], gathered = _attn_fwd(t["q"], t["k"], t["v"], gather)
    if after_gather is not None:
        after_gather(gathered)
    t["dtb"] = _lane_rows(sm["dt_bias"][l])
    t["a_exp"] = _lane_rows(-jnp.exp(sm["a_log"][l]))
    t["d_exp"] = _lane_rows(sm["d_skip"][l])
    t["c"], t["prev"], t["ypre"], t["yssd"] = _ssd_fwd(t["xraw"], t["misc"], t["z"], kw["conv_w"], row("conv_b"), t["dtb"],
                                                     t["a_exp"], t["d_exp"], row("ssd_norm"), consts)
    t["mixed"], t["h1"] = _outproj_fwd(t["oe"], t["yssd"], kw["wout"], h, row("post_mix_norm"))
    t["mb"], t["d"], *out = _mlp_fwd(t["h1"], row("pre_mlp_norm"), kw["wup"], kw["wdown"], row("post_mlp_norm"), target)
    return out, t


def _layer_bwd(dh2, t, kw, sm, l, cosf, sinf, consts, exchange_of=None):
    row = lambda name: sm[name][l].reshape(1, -1)
    g, gs = {}, {}
    dh1, dab, rb, ddb, gs["post_mlp_norm"], gs["pre_mlp_norm"] = _mlp_bwd(
        dh2, t["d"], t["h1"], t["mb"], row("pre_mlp_norm"), kw["wup"], kw["wdown"], row("post_mlp_norm"))
    g["wup"] = _matmul_tn_stacked(t["mb"], dab, f"dw_up_{l}", a_stacked=False)
    g["wdown"] = _matmul_tn_stacked(rb, ddb, f"dw_down_{l}", a_stacked=True)
    dmixb, doe, dyssd, gs["post_mix_norm"], delta = _outproj_bwd(dh1, t["mixed"], row("post_mix_norm"), kw["wout"], t["oe"])
    g["wout_att"] = _matmul_tn(t["oe"], dmixb, f"dw_out_att_{l}")
    g["wout_ssd"] = _matmul_tn(t["yssd"], dmixb, f"dw_out_ssd_{l}")
    dz, dxraw, dmisc_dt, gs["ssd_norm"], gd, galog, gdtb, g["conv_w"], gs["conv_b"] = _ssd_bwd(
        dyssd, t["ypre"], t["z"], t["c"], t["xraw"], t["misc"], t["prev"], kw["conv_w"], t["dtb"], t["a_exp"], t["d_exp"],
        row("ssd_norm"), consts)
    gs["d_skip"] = jnp.sum(gd.reshape(SSD_HEADS, SSD_P), axis=1)
    gs["a_log"] = galog[0, MISC_DT:MISC_DT + SSD_HEADS]
    gs["dt_bias"] = gdtb[0, MISC_DT:MISC_DT + SSD_HEADS]
    dq, dk, dv, exchanged = _attn_bwd(t["q"], t["k"], t["v"], doe, t["lse"], delta,
                                      exchange_of(g) if exchange_of is not None else ())
    dqb, dkvb, dcq, dckv, dmisc_rope, gs["q_norm"], gs["kv_norm"] = _qkv_bwd(
        dq, dk, dv, t["cq"], t["ckv"], row("q_norm"), row("kv_norm"), kw["wuq"], kw["wkv"], cosf, sinf)
    g["wuq"] = _matmul_tn(t["cqn"], dqb, f"dw_uq_{l}")
    g["wkv"] = _matmul_tn(t["ckvn"], dkvb, f"dw_kv_{l}")
    dprojb, dh0, gs["pre_mix_norm"] = _inproj_bwd(dcq, dckv, dmisc_rope, dmisc_dt, dz, dxraw, t["h0"], dh1,
                                                  row("pre_mix_norm"), kw["win"])
    g["win"] = _matmul_tn(t["ub"], dprojb, f"dw_in_{l}")
    return dh0, g, {k: v.reshape(-1) for k, v in gs.items()}, exchanged


def _local_step(x, positions, kws, sm, target, gather=(), after_gather=None, exchange_of=None):
    inv_freq = ROPE_THETA ** (-jnp.arange(0, QK_ROPE, 2, dtype=F32) / QK_ROPE)
    invf = jnp.zeros((HEAD_PAD,), F32).at[MISC_ROPE:MISC_ROPE + QK_ROPE].set(jnp.concatenate([inv_freq, inv_freq]))
    cosf, sinf = _rope_tables(positions.reshape(-1, 1), invf.reshape(1, HEAD_PAD))
    consts = _ssd_consts()
    (h,), t0 = _layer_fwd(x, kws[0], sm, 0, cosf, sinf, consts, gather, after_gather)
    (dh, loss), t1 = _layer_fwd(h, kws[1], sm, 1, cosf, sinf, consts, target=target)
    saved = [t0, t1]
    grads, small, exchanged = [None] * DEPTH, [None] * DEPTH, []
    for l in reversed(range(DEPTH)):
        hook = (lambda g0: exchange_of(g0, grads[1])) if (l == 0 and exchange_of is not None) else None
        dh, grads[l], small[l], got = _layer_bwd(dh, saved[l], kws[l], sm, l, cosf, sinf, consts, hook)
        exchanged = got or exchanged
    return loss[0, 0], dh, grads, small, exchanged


def kernel(x, positions, pre_mix_norm, w_in, q_norm, w_uq, kv_norm, w_ukv, conv_w, conv_b, dt_bias, a_log, d_skip, ssd_norm, w_out, post_mix_norm, pre_mlp_norm, w_up, w_down, post_mlp_norm, loss_target, m_pre_mix_norm, m_w_in, m_q_norm, m_w_uq, m_kv_norm, m_w_ukv, m_conv_w, m_conv_b, m_dt_bias, m_a_log, m_d_skip, m_ssd_norm, m_w_out, m_post_mix_norm, m_pre_mlp_norm, m_w_up, m_w_down, m_post_mlp_norm, v_pre_mix_norm, v_w_in, v_q_norm, v_w_uq, v_kv_norm, v_w_ukv, v_conv_w, v_conv_b, v_dt_bias, v_a_log, v_d_skip, v_ssd_norm, v_w_out, v_post_mix_norm, v_pre_mlp_norm, v_w_up, v_w_down, v_post_mlp_norm):
    w = dict(pre_mix_norm=pre_mix_norm, w_in=w_in, q_norm=q_norm, w_uq=w_uq, kv_norm=kv_norm, w_ukv=w_ukv, conv_w=conv_w,
             conv_b=conv_b, dt_bias=dt_bias, a_log=a_log, d_skip=d_skip, ssd_norm=ssd_norm, w_out=w_out,
             post_mix_norm=post_mix_norm, pre_mlp_norm=pre_mlp_norm, w_up=w_up, w_down=w_down, post_mlp_norm=post_mlp_norm)
    m = dict(pre_mix_norm=m_pre_mix_norm, w_in=m_w_in, q_norm=m_q_norm, w_uq=m_w_uq, kv_norm=m_kv_norm, w_ukv=m_w_ukv,
             conv_w=m_conv_w, conv_b=m_conv_b, dt_bias=m_dt_bias, a_log=m_a_log, d_skip=m_d_skip, ssd_norm=m_ssd_norm,
             w_out=m_w_out, post_mix_norm=m_post_mix_norm, pre_mlp_norm=m_pre_mlp_norm, w_up=m_w_up, w_down=m_w_down,
             post_mlp_norm=m_post_mlp_norm)
    v = dict(pre_mix_norm=v_pre_mix_norm, w_in=v_w_in, q_norm=v_q_norm, w_uq=v_w_uq, kv_norm=v_kv_norm, w_ukv=v_w_ukv,
             conv_w=v_conv_w, conv_b=v_conv_b, dt_bias=v_dt_bias, a_log=v_a_log, d_skip=v_d_skip, ssd_norm=v_ssd_norm,
             w_out=v_w_out, post_mix_norm=v_post_mix_norm, pre_mlp_norm=v_pre_mlp_norm, w_up=v_w_up, w_down=v_w_down,
             post_mlp_norm=v_post_mlp_norm)
    sm = {name: w[name] for name, _ in _SMALL}

    wire = lambda name, l: _wire_shard(name, w[name][l])
    first = _gather_two_level([wire(name, 0) for name in _EARLY], "weight_gather_first")
    kws = [_early_weights({name: _from_wire(name, a) for name, a in zip(_EARLY, first)}), None]
    behind = [(name, 0) for name in _LATE] + [(name, 1) for name, _ in _SHARDED]

    def after_gather(gathered):
        got = {key: _from_wire(key[0], a) for key, a in zip(behind, gathered)}
        kws[0].update(_late_weights({name: got[name, 0] for name in _LATE}))
        kws[1] = {**_early_weights({name: got[name, 1] for name in _EARLY}),
                  **_late_weights({name: got[name, 1] for name in _LATE})}

    sent_behind = [(name, 1) for name, _ in _SHARDED] + [(name, 0) for name in _LATE]

    def exchange_of(g0, g1):
        blocks = {**{(name, 1): a for name, a in _shard_grads(g1).items()},
                  **{(name, 0): a for name, a in _shard_grads(g0).items()}}
        return [blocks[key] for key in sent_behind]

    loss_part, dx, grads, small, exchanged = _local_step(
        x[0], positions[0], kws, sm, loss_target[0], [wire(*key) for key in behind], after_gather, exchange_of)
    slots = dict(zip(sent_behind, exchanged))
    last = _shard_grads({k: grads[0][k] for k in ("win", "wuq", "wkv", "conv_w")})
    slots.update({(name, 0): a for name, a in zip(_EARLY, _comm("exchange", [last[name] for name in _EARLY], "grad_exchange_last"))})
    g_small = _unpack_small(_all_reduce_small(_pack_small({name: jnp.stack([small[l][name] for l in range(DEPTH)])
                                                           for name, _ in _SMALL})))
    loss = lax.psum(loss_part, ("x", "y", "c"))

    grad, delta, new_m, new_v = {}, {}, {}, {}
    for name, _ in _SHARDED:
        grad[name], delta[name], new_m[name], new_v[name] = _sum_adamw(
            [slots[name, 0], slots[name, 1]], w[name], m[name], v[name], f"sum_adamw_{name}")
    pk = lambda d: _pack_small({name: d[name] for name, _ in _SMALL})
    d_, m_, v_ = _adamw(pk(w), pk(g_small), pk(m), pk(v), "adamw_small")
    for dst, packed in ((delta, d_), (new_m, m_), (new_v, v_)):
        dst.update(_unpack_small(packed))
    grad.update(g_small)

    outs = [loss, dx[None]]
    for d in (grad, delta, new_m, new_v):
        outs += [d[name] for name in _WEIGHT_ORDER]
    return tuple(outs)
```

```python
import jax
import jax.numpy as jnp
import numpy as np
from jax import lax
from jax.experimental import pallas as pl
from jax.experimental.pallas import tpu as pltpu

F32 = jnp.float32
BF16 = jnp.bfloat16
HI = lax.Precision.HIGHEST

D_MODEL = 1024
DEPTH = 2
N_DEV = 8
CHUNK = 64
EPS = 1e-6
MLA_HEADS = 8
QK_NOPE = 64
QK_ROPE = 32
V_DIM = 64
Q_RANK = 768
KV_RANK = 256
ROPE_THETA = 10000.0
SSD_HEADS = 8
SSD_P = 64
SSD_INNER = 512
SSD_GROUPS = 2
SSD_N = 128
CONV_W = 4
CONV_DIM = 1024
D_FF = 4096
IN_PROJ = 2600
HEAD_PAD = 128
IN_PAD = 2688
MISC_ROPE = 64
MISC_DT = 96
ATT_SCALE = (QK_NOPE + QK_ROPE) ** -0.5
LOG2E = 1.4426950408889634
ATT_SCALE_LOG2 = ATT_SCALE * LOG2E

ADAM_LR = 0.001
ADAM_B1 = 0.9
ADAM_B2 = 0.999
ADAM_EPS = 1e-08
ADAM_WD = 0.01
ADAM_STEP = 10

TM = 512
TQ = 256
ATT_T = 512
ATT_G = 8
SSD_ROWS = 256
TK_DW = 4096
VMEM_LIMIT = 56 * 1024 * 1024

_NT = (((1,), (1,)), ((), ()))
_TN = (((0,), (0,)), ((), ()))


def _params(**kw):
    return pltpu.CompilerParams(vmem_limit_bytes=VMEM_LIMIT, **kw)


def _dot(a, b, precision=None):
    return jnp.dot(a, b, preferred_element_type=F32, precision=precision)


def _dot_nt(a, b, precision=None):
    return lax.dot_general(a, b, _NT, preferred_element_type=F32, precision=precision)


def _dot_tn(a, b, precision=None):
    return lax.dot_general(a, b, _TN, preferred_element_type=F32, precision=precision)


def _split3(x):
    hi = x.astype(BF16)
    r = x - hi.astype(F32)
    mid = r.astype(BF16)
    return hi, mid, (r - mid.astype(F32)).astype(BF16)


def _dot01(x, m01, dot=_dot, left=False):
    parts = [dot(m01, p) if left else dot(p, m01) for p in _split3(x)]
    return parts[0] + parts[1] + parts[2]


def _full(shape):
    n = len(shape)
    return pl.BlockSpec(shape, lambda *_: (0,) * n)


def _resident(shape):
    n = len(shape)
    return pl.BlockSpec(shape, lambda *_: (0,) * n, pipeline_mode=pl.Buffered(1))


def _rows(tm, width):
    return pl.BlockSpec((tm, width), lambda i: (i, 0))


def _rms_fwd(x, w):
    r = lax.rsqrt(jnp.mean(x * x, axis=-1, keepdims=True) + EPS)
    return (x * r) * w


def _rms_bwd(x, w, dy):
    r = lax.rsqrt(jnp.mean(x * x, axis=-1, keepdims=True) + EPS)
    xh = x * r
    dxn = dy * w
    dx = r * (dxn - xh * jnp.mean(dxn * xh, axis=-1, keepdims=True))
    return dx, dy * xh


def _acc_rows(ref, val, first):
    s = jnp.sum(val, axis=0, keepdims=True)

    @pl.when(first)
    def _():
        ref[...] = s

    @pl.when(jnp.logical_not(first))
    def _():
        ref[...] += s


def _rope(t, cosf, sinf, sign):
    lane = lax.broadcasted_iota(jnp.int32, t.shape, 1)
    rot = jnp.where(lane < MISC_ROPE + QK_ROPE // 2, -pltpu.roll(t, HEAD_PAD - QK_ROPE // 2, 1), pltpu.roll(t, QK_ROPE // 2, 1))
    return t * cosf + sign * (rot * sinf)


def _rope_tables(pos, invf):
    s = pos.shape[0]

    def body(pos_ref, invf_ref, cos_ref, sin_ref):
        ang = pos_ref[...].astype(F32) * invf_ref[...]
        cos_ref[...] = jnp.cos(ang)
        sin_ref[...] = jnp.sin(ang)

    return pl.pallas_call(
        body, name="rope_tables", grid=(s // TM,),
        in_specs=[_rows(TM, 1), _full((1, HEAD_PAD))],
        out_specs=[_rows(TM, HEAD_PAD), _rows(TM, HEAD_PAD)],
        out_shape=[jax.ShapeDtypeStruct((s, HEAD_PAD), F32)] * 2,
    )(pos, invf)


def _inproj_fwd(h, nw, win):
    s = h.shape[0]

    def body(h_ref, nw_ref, w_ref, ub_ref, cq_ref, ckv_ref, misc_ref, z_ref, xbc_ref):
        ub = _rms_fwd(h_ref[...], nw_ref[...]).astype(BF16)
        ub_ref[...] = ub
        proj = _dot(ub, w_ref[...])
        cq_ref[...] = proj[:, 0:768]
        ckv_ref[...] = proj[:, 768:1024]
        misc_ref[...] = proj[:, 1024:1152]
        z_ref[...] = proj[:, 1152:1664]
        xbc_ref[...] = proj[:, 1664:2688]

    widths = (768, 256, 128, 512, 1024)
    return pl.pallas_call(
        body, name="inproj_fwd", grid=(s // TM,),
        in_specs=[_rows(TM, D_MODEL), _full((1, D_MODEL)), _resident((D_MODEL, IN_PAD))],
        out_specs=[_rows(TM, D_MODEL)] + [_rows(TM, w) for w in widths],
        out_shape=[jax.ShapeDtypeStruct((s, D_MODEL), BF16)] + [jax.ShapeDtypeStruct((s, w), F32) for w in widths],
        compiler_params=_params(),
    )(h, nw, win)


def _qkv_fwd(cq, ckv, misc, qnw, kvnw, wuq, wkv, cosf, sinf):
    s = cq.shape[0]

    def body(cq_ref, ckv_ref, misc_ref, qnw_ref, kvnw_ref, wuq_ref, wkv_ref, cos_ref, sin_ref,
             cqn_ref, ckvn_ref, q_ref, k_ref, v_ref):
        cosf, sinf = cos_ref[...], sin_ref[...]
        cqn = _rms_fwd(cq_ref[...], qnw_ref[...]).astype(BF16)
        cqn_ref[...] = cqn
        q = _dot(cqn, wuq_ref[...])
        ckvn = _rms_fwd(ckv_ref[...], kvnw_ref[...]).astype(BF16)
        ckvn_ref[...] = ckvn
        kv = _dot(ckvn, wkv_ref[...])
        m = misc_ref[...]
        lane = lax.broadcasted_iota(jnp.int32, m.shape, 1)
        in_rope = jnp.logical_and(lane >= MISC_ROPE, lane < MISC_ROPE + QK_ROPE)
        kr = jnp.where(in_rope, _rope(m, cosf, sinf, 1.0), 0.0)
        for hd in range(MLA_HEADS):
            cols = slice(hd * HEAD_PAD, (hd + 1) * HEAD_PAD)
            q_ref[:, cols] = _rope(q[:, cols], cosf, sinf, 1.0).astype(BF16)
            k_ref[:, cols] = (kv[:, cols] + kr).astype(BF16)
        vv = kv[:, MLA_HEADS * HEAD_PAD:]
        vlane = lax.broadcasted_iota(jnp.int32, vv.shape, 1)
        ones_at = jnp.where((vlane // HEAD_PAD) % 2 == 0, V_DIM, 0)
        v_ref[...] = jnp.where(vlane % HEAD_PAD == ones_at, 1.0, vv).astype(BF16)

    wide = MLA_HEADS * HEAD_PAD
    return pl.pallas_call(
        body, name="qkv_fwd", grid=(s // TM,),
        in_specs=[_rows(TM, Q_RANK), _rows(TM, KV_RANK), _rows(TM, HEAD_PAD), _full((1, Q_RANK)), _full((1, KV_RANK)),
                  _resident((Q_RANK, wide)), _resident((KV_RANK, 2 * wide)), _rows(TM, HEAD_PAD), _rows(TM, HEAD_PAD)],
        out_specs=[_rows(TM, Q_RANK), _rows(TM, KV_RANK), _rows(TM, wide), _rows(TM, wide), _rows(TM, wide)],
        out_shape=[jax.ShapeDtypeStruct((s, Q_RANK), BF16), jax.ShapeDtypeStruct((s, KV_RANK), BF16)]
        + [jax.ShapeDtypeStruct((s, wide), BF16)] * 3,
        compiler_params=_params(),
    )(cq, ckv, misc, qnw, kvnw, wuq, wkv, cosf, sinf)


def _chunk_bias(t, keys_on_rows=False):
    row = lax.broadcasted_iota(jnp.int32, (t, 1), 0) // CHUNK
    col = lax.broadcasted_iota(jnp.int32, (1, t), 1) // CHUNK
    return jnp.where((row <= col) if keys_on_rows else (col <= row), 0.0, -jnp.inf).astype(F32)


def _attn_fwd(q, k, v, gather=()):
    s = q.shape[0]
    t = ATT_T
    nq = s // t
    pair = ATT_G * HEAD_PAD
    ng = len(gather)

    def body(q_ref, k_ref, v_ref, *rest):
        g_in, (o_ref, lse_ref), g_out = rest[:ng], rest[ng:ng + 2], rest[ng + 2:2 * ng + 2]
        m_s, acc_s, bias_s = rest[2 * ng + 2:2 * ng + 5]
        qi = pl.program_id(1)
        group, groups = pl.program_id(0), MLA_HEADS // ATT_G

        @pl.when(jnp.logical_and(group == 0, qi == 0))
        def _():
            bias_s[...] = _chunk_bias(t)

        _hosted_gather(g_in, g_out, rest[2 * ng + 5:],
                       jnp.logical_and(group == 0, qi == 0),
                       jnp.logical_and(group == groups - 1, qi == min(3 * nq // 4 + 1, nq - 1)),
                       jnp.logical_and(group == groups - 1, qi == nq - 1))
        m_s[...] = jnp.full(m_s.shape, -jnp.inf, F32)
        acc_s[...] = jnp.zeros(acc_s.shape, F32)

        def step(kb, masked):
            r0 = pl.multiple_of(kb * t, t)

            def scores(hh):
                cols = slice(hh * HEAD_PAD, (hh + 1) * HEAD_PAD)
                return _dot_nt(q_ref[:, cols], k_ref[pl.ds(r0, t), cols])

            def soft(hh, raw):
                sc = raw * ATT_SCALE_LOG2
                if masked:
                    sc = sc + bias_s[...]
                m_old = m_s[hh]
                m_new = jnp.maximum(m_old, jnp.max(sc, axis=-1, keepdims=True))
                alpha = jnp.exp2(m_old - m_new)
                p = jnp.exp2(sc - jnp.tile(m_new, (1, t // HEAD_PAD)))
                m_s[hh] = m_new
                return alpha, p.astype(BF16)

            def update(hh, alpha, p):
                cols = slice(hh * HEAD_PAD, (hh + 1) * HEAD_PAD)
                acc_s[hh] = alpha * acc_s[hh] + _dot(p, v_ref[pl.ds(r0, t), cols])

            raw, ap = [None] * ATT_G, [None] * ATT_G
            raw[0] = scores(0)
            for hh in range(ATT_G):
                if hh + 1 < ATT_G:
                    raw[hh + 1] = scores(hh + 1)
                ap[hh] = soft(hh, raw[hh])
                if hh >= 1:
                    update(hh - 1, *ap[hh - 1])
            update(ATT_G - 1, *ap[ATT_G - 1])

        def loop(kb, c):
            step(kb, False)
            return c

        lax.fori_loop(0, qi, loop, 0)
        step(qi, True)
        for hh in range(ATT_G):
            cols = slice(hh * HEAD_PAD, (hh + 1) * HEAD_PAD)
            acc = acc_s[hh]
            ones_at = V_DIM * (1 - hh % 2)
            l = jnp.broadcast_to(acc[:, ones_at:ones_at + 1], acc.shape)
            o_ref[:, cols] = (acc / l).astype(BF16)
            lse_ref[hh] = (m_s[hh] + jnp.log(l) * LOG2E).T[0:8, :]

    outs = pl.pallas_call(
        body, name="attn_fwd_gather" if ng else "attn_fwd", grid=(MLA_HEADS // ATT_G, nq),
        in_specs=[pl.BlockSpec((t, pair), lambda h, i: (i, h)),
                  pl.BlockSpec((s, pair), lambda h, i: (0, h), pipeline_mode=pl.Buffered(1)),
                  pl.BlockSpec((s, pair), lambda h, i: (0, h), pipeline_mode=pl.Buffered(1))] + [_ANY] * ng,
        out_specs=[pl.BlockSpec((t, pair), lambda h, i: (i, h)),
                   pl.BlockSpec((ATT_G, 8, t), lambda h, i: (h, 0, i))] + [_ANY] * ng,
        out_shape=[jax.ShapeDtypeStruct((s, MLA_HEADS * HEAD_PAD), BF16), jax.ShapeDtypeStruct((MLA_HEADS, 8, s), F32)]
        + _comm_out_shapes("gather", gather),
        scratch_shapes=[pltpu.VMEM((ATT_G, t, HEAD_PAD), F32), pltpu.VMEM((ATT_G, t, HEAD_PAD), F32), pltpu.VMEM((t, t), F32)]
        + (_comm_scratch(ng) if ng else []),
        compiler_params=_params(),
    )(q, k, v, *gather)
    return outs[0], outs[1], list(outs[2:])


def _interleave(stages):
    live = list(stages)
    while live:
        still = []
        for g in live:
            try:
                next(g)
                still.append(g)
            except StopIteration:
                pass
        live = still


def _ssd_consts():
    emisc = np.zeros((HEAD_PAD, SSD_INNER), np.float32)
    for hd in range(SSD_HEADS):
        emisc[MISC_DT + hd, hd * SSD_P:(hd + 1) * SSD_P] = 1.0
    idx = np.arange(CHUNK)
    tri = (idx[:, None] >= idx[None, :]).astype(np.float32)
    return tuple(jnp.asarray(m, BF16) for m in (emisc, emisc.T.copy(), tri, tri.T.copy()))


def _ssd_chunk_common(cc, misc, emisc, tri, trit, dtb, a_exp):
    sig = jax.nn.sigmoid(cc)
    xa = cc * sig
    dt = jax.nn.softplus(_dot01(misc, emisc) + dtb)
    a = dt * a_exp
    acs = _dot01(a, tri, left=True)
    acs_t = _dot01(a, trit, dot=_dot_tn)
    alast = acs[CHUNK - 1:CHUNK, :]
    return xa, sig, dt, acs, acs_t, alast


def _decay(acs, acs_t, hd):
    row = lax.broadcasted_iota(jnp.int32, (CHUNK, CHUNK), 0)
    col = lax.broadcasted_iota(jnp.int32, (CHUNK, CHUNK), 1)
    diff = acs[:, hd * SSD_P:hd * SSD_P + 1] - acs_t[hd * SSD_P:hd * SSD_P + 1, :]
    return jnp.exp(jnp.where(row >= col, diff, -jnp.inf))


def _half_mask(hh):
    lane = lax.broadcasted_iota(jnp.int32, (CHUNK, 2 * SSD_P), 1)
    return (lane >= SSD_P) if hh else (lane < SSD_P)


def _gate_norm(y, zz):
    sg = jax.nn.sigmoid(zz)
    yz = y * (zz * sg)
    outs, rs = [], []
    half = SSD_INNER // SSD_GROUPS
    for g in range(SSD_GROUPS):
        yg = yz[:, g * half:(g + 1) * half]
        r = lax.rsqrt(jnp.mean(yg * yg, axis=-1, keepdims=True) + EPS)
        outs.append(yg * r)
        rs.append(r)
    return sg, jnp.concatenate(outs, axis=1), rs


def _ssd_fwd(xraw, misc, z, cw, cb, dtb, a_exp, d_exp, nw, consts):
    s = xraw.shape[0]
    nb = s // SSD_ROWS
    ncb = SSD_ROWS // CHUNK
    emisc, _, tri, trit = consts

    def body(x_ref, misc_ref, z_ref, cw_ref, cb_ref, dtb_ref, a_ref, d_ref, nw_ref, emisc_ref, tri_ref, trit_ref,
             c_ref, prev_ref, ypre_ref, yssd_ref, tail_s, state_s):
        i = pl.program_id(0)

        @pl.when(i == 0)
        def _():
            tail_s[...] = jnp.zeros(tail_s.shape, F32)
            state_s[...] = jnp.zeros(state_s.shape, F32)

        x = x_ref[...]
        xext = jnp.concatenate([tail_s[...], x], axis=0)
        acc = x * cw_ref[CONV_W - 1:CONV_W, :] + cb_ref[...]
        for j in range(1, CONV_W):
            acc = acc + pltpu.roll(xext, j, 0)[8:, :] * cw_ref[CONV_W - 1 - j:CONV_W - j, :]
        tail_s[...] = x[SSD_ROWS - 8:, :]
        c_ref[...] = acc

        def chunk(ci):
            r0 = ci * CHUNK
            xa, _, dt, acs, acs_t, alast = _ssd_chunk_common(
                c_ref[pl.ds(r0, CHUNK), :], misc_ref[pl.ds(r0, CHUNK), :], emisc_ref[...], tri_ref[...], trit_ref[...],
                dtb_ref[...], a_ref[...])
            yield
            xs = xa[:, :SSD_INNER]
            xdt = xs * dt
            wgt = (xdt * jnp.exp(alast - acs)).astype(BF16)
            e = jnp.exp(acs)
            ys, new_states, cms = [], [], []
            for g in range(SSD_GROUPS):
                bm = xa[:, SSD_INNER + g * SSD_N:SSD_INNER + (g + 1) * SSD_N].astype(BF16)
                cm = xa[:, SSD_INNER + SSD_GROUPS * SSD_N + g * SSD_N:SSD_INNER + SSD_GROUPS * SSD_N + (g + 1) * SSD_N].astype(BF16)
                cms.append(cm)
                cb_g = _dot_nt(cm, bm)
                gl = slice(g * 256, (g + 1) * 256)
                new_states.append(_dot_tn(bm, wgt[:, gl]))
                for jj in range(2):
                    pair = 2 * g + jj
                    xp = xdt[:, pair * 128:(pair + 1) * 128]
                    yp = None
                    for hh in range(2):
                        sc = (cb_g * _decay(acs, acs_t, 2 * pair + hh)).astype(BF16)
                        term = _dot(sc, jnp.where(_half_mask(hh), xp, 0.0).astype(BF16))
                        yp = term if yp is None else yp + term
                    ys.append(yp)
                yield
            prev = state_s[...]
            prev_ref[ci] = prev
            yoff = jnp.concatenate([_dot(cms[g], prev[:, g * 256:(g + 1) * 256].astype(BF16)) for g in range(SSD_GROUPS)],
                                   axis=1) * e
            state_s[...] = prev * jnp.exp(alast) + jnp.concatenate(new_states, axis=1)
            yield
            y = jnp.concatenate(ys, axis=1) + yoff + d_ref[...] * xs
            ypre_ref[pl.ds(r0, CHUNK), :] = y
            _, yn, _ = _gate_norm(y, z_ref[pl.ds(r0, CHUNK), :])
            yssd_ref[pl.ds(r0, CHUNK), :] = (yn * nw_ref[...]).astype(BF16)

        _interleave([chunk(ci) for ci in range(ncb)])

    return pl.pallas_call(
        body, name="ssd_fwd", grid=(nb,),
        in_specs=[_rows(SSD_ROWS, CONV_DIM), _rows(SSD_ROWS, HEAD_PAD), _rows(SSD_ROWS, SSD_INNER),
                  _full((CONV_W, CONV_DIM)), _full((1, CONV_DIM)), _full((1, SSD_INNER)), _full((1, SSD_INNER)),
                  _full((1, SSD_INNER)), _full((1, SSD_INNER)), _full((HEAD_PAD, SSD_INNER)), _full((CHUNK, CHUNK)),
                  _full((CHUNK, CHUNK))],
        out_specs=[_rows(SSD_ROWS, CONV_DIM), pl.BlockSpec((ncb, SSD_N, SSD_INNER), lambda i: (i, 0, 0)),
                   _rows(SSD_ROWS, SSD_INNER), _rows(SSD_ROWS, SSD_INNER)],
        out_shape=[jax.ShapeDtypeStruct((s, CONV_DIM), F32), jax.ShapeDtypeStruct((s // CHUNK, SSD_N, SSD_INNER), F32),
                   jax.ShapeDtypeStruct((s, SSD_INNER), F32), jax.ShapeDtypeStruct((s, SSD_INNER), BF16)],
        scratch_shapes=[pltpu.VMEM((8, CONV_DIM), F32), pltpu.VMEM((SSD_N, SSD_INNER), F32)],
        compiler_params=_params(),
    )(xraw, misc, z, cw, cb, dtb, a_exp, d_exp, nw, emisc, tri, trit)


def _outproj_fwd(oe, yssd, wout, h, nw):
    s = h.shape[0]
    wide = MLA_HEADS * HEAD_PAD

    def body(oe_ref, y_ref, w_ref, h_ref, nw_ref, mixed_ref, h1_ref):
        mixed = _dot(oe_ref[...], w_ref[0:wide, :]) + _dot(y_ref[...], w_ref[wide:, :])
        mixed_ref[...] = mixed
        h1_ref[...] = h_ref[...] + _rms_fwd(mixed, nw_ref[...])

    return pl.pallas_call(
        body, name="outproj_fwd", grid=(s // TM,),
        in_specs=[_rows(TM, wide), _rows(TM, SSD_INNER), _resident((wide + SSD_INNER, D_MODEL)), _rows(TM, D_MODEL),
                  _full((1, D_MODEL))],
        out_specs=[_rows(TM, D_MODEL), _rows(TM, D_MODEL)],
        out_shape=[jax.ShapeDtypeStruct((s, D_MODEL), F32)] * 2,
        compiler_params=_params(),
    )(oe, yssd, wout, h, nw)


def _mlp_fwd(h1, prew, wup, wdown, postw, target=None):
    s = h1.shape[0]
    fb = D_FF // N_DEV
    last = target is not None

    def body(h_ref, prew_ref, up_ref, down_ref, postw_ref, *rest):
        hh = h_ref[...]
        mb = _rms_fwd(hh, prew_ref[...]).astype(BF16)
        rest[-3 - last][...] = mb
        d = jnp.zeros((TM, D_MODEL), F32)
        for j in range(N_DEV):
            a = _dot(mb, up_ref[j])
            r = jnp.square(jnp.maximum(a, 0.0)).astype(BF16)
            d = d + _dot(r, down_ref[j])
        rest[-2 - last][...] = d
        h2 = hh + _rms_fwd(d, postw_ref[...])
        if last:
            diff = h2 - rest[0][...]
            rest[-2][...] = diff * (1.0 / D_MODEL)
            part = 0.5 * jnp.sum(jnp.mean(diff * diff, axis=-1, keepdims=True), axis=0, keepdims=True)
            _acc_rows(rest[-1], part, pl.program_id(0) == 0)
        else:
            rest[-1][...] = h2

    return pl.pallas_call(
        body, name="mlp_fwd_loss" if last else "mlp_fwd", grid=(s // TM,),
        in_specs=[_rows(TM, D_MODEL), _full((1, D_MODEL)), _resident((N_DEV, D_MODEL, fb)), _resident((N_DEV, fb, D_MODEL)),
                  _full((1, D_MODEL))] + ([_rows(TM, D_MODEL)] if last else []),
        out_specs=[_rows(TM, D_MODEL)] * 3 + ([_full((1, 1))] if last else []),
        out_shape=[jax.ShapeDtypeStruct((s, D_MODEL), BF16), jax.ShapeDtypeStruct((s, D_MODEL), F32),
                   jax.ShapeDtypeStruct((s, D_MODEL), F32)] + ([jax.ShapeDtypeStruct((1, 1), F32)] if last else []),
        compiler_params=_params(),
    )(h1, prew, wup, wdown, postw, *([target] if last else []))


def _mlp_bwd(dh2, d, h1, mb, prew, wup, wdown, postw):
    s = dh2.shape[0]
    fb = D_FF // N_DEV
    tm = TM // 2

    def body(dh2_ref, d_ref, h1_ref, mb_ref, prew_ref, up_ref, down_ref, postw_ref,
             dh1_ref, da_ref, r_ref, dd_ref, gpost_ref, gpre_ref):
        first = pl.program_id(0) == 0
        dh2 = dh2_ref[...]
        dd, gpost = _rms_bwd(d_ref[...], postw_ref[...], dh2)
        _acc_rows(gpost_ref, gpost, first)
        ddb = dd.astype(BF16)
        dd_ref[...] = ddb
        mb = mb_ref[...]
        def products(j):
            return _dot(mb, up_ref[j]), _dot_nt(ddb, down_ref[j])

        def pointwise(j, a, dr):
            a = jnp.maximum(a, 0.0)
            r_ref[j] = jnp.square(a).astype(BF16)
            da = (dr * (2.0 * a)).astype(BF16)
            da_ref[j] = da
            return da

        dm = jnp.zeros((tm, D_MODEL), F32)
        nxt, da_prev = products(0), None
        for j in range(N_DEV):
            cur = nxt
            if j + 1 < N_DEV:
                nxt = products(j + 1)
            da = pointwise(j, *cur)
            if da_prev is not None:
                dm = dm + _dot_nt(da_prev, up_ref[j - 1])
            da_prev = da
        dm = dm + _dot_nt(da_prev, up_ref[N_DEV - 1])
        dx, gpre = _rms_bwd(h1_ref[...], prew_ref[...], dm)
        _acc_rows(gpre_ref, gpre, first)
        dh1_ref[...] = dh2 + dx

    stacked = pl.BlockSpec((N_DEV, tm, fb), lambda i: (0, i, 0))
    return pl.pallas_call(
        body, name="mlp_bwd", grid=(s // tm,),
        in_specs=[_rows(tm, D_MODEL)] * 4 + [_full((1, D_MODEL)), _resident((N_DEV, D_MODEL, fb)), _resident((N_DEV, fb, D_MODEL)),
                                              _full((1, D_MODEL))],
        out_specs=[_rows(tm, D_MODEL), stacked, stacked, _rows(tm, D_MODEL), _full((1, D_MODEL)), _full((1, D_MODEL))],
        out_shape=[jax.ShapeDtypeStruct((s, D_MODEL), F32), jax.ShapeDtypeStruct((N_DEV, s, fb), BF16),
                   jax.ShapeDtypeStruct((N_DEV, s, fb), BF16), jax.ShapeDtypeStruct((s, D_MODEL), BF16),
                   jax.ShapeDtypeStruct((1, D_MODEL), F32), jax.ShapeDtypeStruct((1, D_MODEL), F32)],
        compiler_params=_params(),
    )(dh2, d, h1, mb, prew, wup, wdown, postw)


def _matmul_tn(a, b, name, tk=TK_DW):
    s, m = a.shape
    n = b.shape[1]
    tn = n if n <= 1024 else (n // 2 if (n // 2) % 128 == 0 else n // 3)
    tk = min(tk, s)
    assert n % tn == 0 and tn % 128 == 0 and s % tk == 0

    def body(a_ref, b_ref, o_ref):
        part = _dot_tn(a_ref[...], b_ref[...])

        @pl.when(pl.program_id(1) == 0)
        def _():
            o_ref[...] = part

        @pl.when(pl.program_id(1) != 0)
        def _():
            o_ref[...] += part

    return pl.pallas_call(
        body, name=name, grid=(n // tn, s // tk),
        in_specs=[pl.BlockSpec((tk, m), lambda j, k: (k, 0)), pl.BlockSpec((tk, tn), lambda j, k: (k, j))],
        out_specs=pl.BlockSpec((m, tn), lambda j, k: (0, j)),
        out_shape=jax.ShapeDtypeStruct((m, n), F32),
        compiler_params=_params(),
    )(a, b)


def _matmul_tn_stacked(a, b, name, a_stacked, tk=TK_DW):
    tk = min(tk, a.shape[-2])
    if a_stacked:
        _, s, m = a.shape
        n = b.shape[1]
        in_specs = [pl.BlockSpec((1, tk, m), lambda j, k: (j, k, 0)), pl.BlockSpec((tk, n), lambda j, k: (k, 0))]
    else:
        s, m = a.shape
        n = b.shape[2]
        in_specs = [pl.BlockSpec((tk, m), lambda j, k: (k, 0)), pl.BlockSpec((1, tk, n), lambda j, k: (j, k, 0))]

    nk = s // tk

    def body(a_ref, b_ref, o_ref, acc_s):
        av = a_ref[0] if a_stacked else a_ref[...]
        bv = b_ref[...] if a_stacked else b_ref[0]
        part = _dot_tn(av, bv)
        k = pl.program_id(1)

        @pl.when(k == 0)
        def _():
            acc_s[...] = part

        @pl.when(jnp.logical_and(k != 0, k != nk - 1))
        def _():
            acc_s[...] += part

        @pl.when(k == nk - 1)
        def _():
            o_ref[0] = (part if nk == 1 else acc_s[...] + part).astype(BF16)

    return pl.pallas_call(
        body, name=name, grid=(N_DEV, nk),
        in_specs=in_specs,
        out_specs=pl.BlockSpec((1, m, n), lambda j, k: (j, 0, 0)),
        out_shape=jax.ShapeDtypeStruct((N_DEV, m, n), BF16),
        scratch_shapes=[pltpu.VMEM((m, n), F32)],
        compiler_params=_params(),
    )(a, b)


def _outproj_bwd(dh1, mixed, nw, wout, oe):
    s = dh1.shape[0]
    wide = MLA_HEADS * HEAD_PAD

    def body(dh1_ref, mixed_ref, nw_ref, w_ref, oe_ref, dmix_ref, doe_ref, dy_ref, gnw_ref, delta_ref):
        dmix, gnw = _rms_bwd(mixed_ref[...], nw_ref[...], dh1_ref[...])
        _acc_rows(gnw_ref, gnw, pl.program_id(0) == 0)
        dmb = dmix.astype(BF16)
        dmix_ref[...] = dmb
        doe_ref[...] = _dot_nt(dmb, w_ref[0:wide, :]).astype(BF16)
        dy_ref[...] = _dot_nt(dmb, w_ref[wide:, :])
        ones = jnp.ones((8, HEAD_PAD), BF16)
        for hd in range(MLA_HEADS):
            cols = slice(hd * HEAD_PAD, (hd + 1) * HEAD_PAD)
            prod = oe_ref[:, cols].astype(F32) * doe_ref[:, cols].astype(F32)
            delta_ref[hd] = _dot01(prod, ones, dot=_dot_nt, left=True)

    return pl.pallas_call(
        body, name="outproj_bwd", grid=(s // TM,),
        in_specs=[_rows(TM, D_MODEL), _rows(TM, D_MODEL), _full((1, D_MODEL)), _resident((wide + SSD_INNER, D_MODEL)),
                  _rows(TM, wide)],
        out_specs=[_rows(TM, D_MODEL), _rows(TM, wide), _rows(TM, SSD_INNER), _full((1, D_MODEL)),
                   pl.BlockSpec((MLA_HEADS, 8, TM), lambda i: (0, 0, i))],
        out_shape=[jax.ShapeDtypeStruct((s, D_MODEL), BF16), jax.ShapeDtypeStruct((s, wide), BF16),
                   jax.ShapeDtypeStruct((s, SSD_INNER), F32), jax.ShapeDtypeStruct((1, D_MODEL), F32),
                   jax.ShapeDtypeStruct((MLA_HEADS, 8, s), F32)],
        compiler_params=_params(),
    )(dh1, mixed, nw, wout, oe)


def _attn_bwd(q, k, v, do, lse, delta, exchange=()):
    s = q.shape[0]
    t = ATT_T
    nq = s // t
    pair = 2 * HEAD_PAD
    ne = len(exchange)

    def body(q_ref, k_ref, v_ref, do_ref, lse_ref, delta_ref, *rest):
        e_in, (dq_ref, dk_ref, dv_ref), e_out = rest[:ne], rest[ne:ne + 3], rest[ne + 3:2 * ne + 3]
        dk_s, dv_s, bias_s = rest[2 * ne + 3:2 * ne + 6]
        kb = pl.program_id(1)
        _hosted_comm("exchange", e_in, e_out, rest[2 * ne + 6:],
                     jnp.logical_and(pl.program_id(0) == 0, kb == 0),
                     jnp.logical_and(pl.program_id(0) == MLA_HEADS // 2 - 1, kb == nq - 1))

        @pl.when(jnp.logical_and(pl.program_id(0) == 0, kb == 0))
        def _():
            bias_s[...] = _chunk_bias(t, keys_on_rows=True)

        @pl.when(kb == 0)
        def _():
            dq_ref[...] = jnp.zeros(dq_ref.shape, F32)

        def step(qb, diagonal):
            r0 = pl.multiple_of(qb * t, t)
            for hh in range(2):
                cols = slice(hh * HEAD_PAD, (hh + 1) * HEAD_PAD)
                kk = k_ref[:, cols]
                qq = q_ref[pl.ds(r0, t), cols]
                dd = do_ref[pl.ds(r0, t), cols]
                sc = _dot_nt(kk, qq) * ATT_SCALE_LOG2
                if diagonal:
                    sc = sc + bias_s[...]
                p = jnp.exp2(sc - lse_ref[hh, 0:1, pl.ds(r0, t)])
                dv = _dot(p.astype(BF16), dd)
                dp = _dot_nt(v_ref[:, cols], dd)
                ds = (p * (dp - delta_ref[hh, 0:1, pl.ds(r0, t)]) * ATT_SCALE).astype(BF16)
                dk = _dot(ds, qq)
                if diagonal:
                    dv_s[:, cols] = dv
                    dk_s[:, cols] = dk
                else:
                    dv_s[:, cols] += dv
                    dk_s[:, cols] += dk
                dq_ref[pl.ds(r0, t), cols] += _dot_tn(ds, kk)

        def loop(qb, c):
            step(qb, False)
            return c

        step(kb, True)
        lax.fori_loop(kb + 1, nq, loop, 0)
        dk_ref[...] = dk_s[...].astype(BF16)
        dv_ref[...] = dv_s[...].astype(BF16)

    whole = pl.BlockSpec((s, pair), lambda h, i: (0, h))
    tile = pl.BlockSpec((t, pair), lambda h, i: (i, h))
    rowvec = pl.BlockSpec((2, 8, s), lambda h, i: (h, 0, 0))
    wide = MLA_HEADS * HEAD_PAD
    outs = pl.pallas_call(
        body, name="attn_bwd_exchange" if ne else "attn_bwd", grid=(MLA_HEADS // 2, nq),
        in_specs=[whole, tile, tile, whole, rowvec, rowvec] + [_ANY] * ne,
        out_specs=[whole, tile, tile] + [_ANY] * ne,
        out_shape=[jax.ShapeDtypeStruct((s, wide), F32)] + [jax.ShapeDtypeStruct((s, wide), BF16)] * 2
        + _comm_out_shapes("exchange", exchange),
        scratch_shapes=[pltpu.VMEM((t, pair), F32), pltpu.VMEM((t, pair), F32), pltpu.VMEM((t, t), F32)]
        + (_comm_scratch(ne) if ne else []),
        compiler_params=_params(),
    )(q, k, v, do, lse, delta, *exchange)
    return outs[0], outs[1], outs[2], list(outs[3:])


def _ssd_bwd(dy, ypre, z, c, xraw, misc, prev, cw, dtb, a_exp, d_exp, nw, consts):
    s = dy.shape[0]
    nb = s // SSD_ROWS
    ncb = SSD_ROWS // CHUNK
    emisc, emisc_t, tri, trit = consts

    def body(dy_ref, ypre_ref, z_ref, c_ref, x_ref, misc_ref, prev_ref, cw_ref, dtb_ref, a_ref, d_ref, nw_ref,
             emisc_ref, emisct_ref, tri_ref, trit_ref,
             dz_ref, dx_ref, dmisc_ref, gnw_ref, gd_ref, galog_ref, gdtb_ref, gcw_ref, gcb_ref,
             dst_s, dc_s, head_s):
        i = pl.program_id(0)
        first = i == 0

        @pl.when(first)
        def _():
            dst_s[...] = jnp.zeros(dst_s.shape, F32)
            head_s[...] = jnp.zeros(head_s.shape, F32)
            gnw_ref[...] = jnp.zeros(gnw_ref.shape, F32)
            gd_ref[...] = jnp.zeros(gd_ref.shape, F32)
            galog_ref[...] = jnp.zeros(galog_ref.shape, F32)
            gdtb_ref[...] = jnp.zeros(gdtb_ref.shape, F32)

        a_exp_v = a_ref[...]
        a8 = _dot01(a_exp_v, emisct_ref[...]) * (1.0 / SSD_P)

        def chunk(ci):
            r0 = ci * CHUNK
            cc = c_ref[pl.ds(r0, CHUNK), :]
            mm = misc_ref[pl.ds(r0, CHUNK), :]
            xa, sig_c, dt, acs, acs_t, alast = _ssd_chunk_common(cc, mm, emisc_ref[...], tri_ref[...], trit_ref[...],
                                                              dtb_ref[...], a_exp_v)
            yield
            xs = xa[:, :SSD_INNER]
            xdt = xs * dt
            y = ypre_ref[pl.ds(r0, CHUNK), :]
            zz = z_ref[pl.ds(r0, CHUNK), :]
            sg, yn, rs = _gate_norm(y, zz)
            dyo = dy_ref[pl.ds(r0, CHUNK), :]
            gnw_ref[...] += jnp.sum(dyo * yn, axis=0, keepdims=True)
            dyn = dyo * nw_ref[...]
            half = SSD_INNER // SSD_GROUPS
            dyz_parts = []
            for g in range(SSD_GROUPS):
                gl = slice(g * half, (g + 1) * half)
                dyz_parts.append(rs[g] * (dyn[:, gl] - yn[:, gl] * jnp.mean(dyn[:, gl] * yn[:, gl], axis=-1, keepdims=True)))
            dyz = jnp.concatenate(dyz_parts, axis=1)
            dz_ref[pl.ds(r0, CHUNK), :] = dyz * y * (sg * (1.0 + zz * (1.0 - sg)))
            dyp = dyz * (zz * sg)
            dypb = dyp.astype(BF16)
            gd_ref[...] += jnp.sum(dyp * xs, axis=0, keepdims=True)
            yield
            prev = prev_ref[ci]
            cd = jnp.exp(alast)
            e = jnp.exp(acs)
            dsx = jnp.exp(alast - acs)
            wgt = (xdt * dsx).astype(BF16)
            dze = (dyp * e).astype(BF16)
            dprev_parts, diag_all, dbm, dcm, yoff_parts, bms = [], [], [], [], [], []
            lane8 = lax.broadcasted_iota(jnp.int32, (CHUNK, HEAD_PAD), 1)
            diag8 = jnp.zeros((CHUNK, HEAD_PAD), F32)
            for g in range(SSD_GROUPS):
                gl = slice(g * 256, (g + 1) * 256)
                bm = xa[:, SSD_INNER + g * SSD_N:SSD_INNER + (g + 1) * SSD_N].astype(BF16)
                cm = xa[:, SSD_INNER + SSD_GROUPS * SSD_N + g * SSD_N:SSD_INNER + SSD_GROUPS * SSD_N + (g + 1) * SSD_N].astype(BF16)
                bms.append(bm)
                prev_g = prev[:, gl].astype(BF16)
                dcm_g = _dot_nt(dze[:, gl], prev_g)
                dprev_parts.append(_dot_tn(cm, dze[:, gl]))
                cb_g = _dot_nt(cm, bm)
                dcb = jnp.zeros((CHUNK, CHUNK), F32)
                diag_parts = []
                for jj in range(2):
                    pair = 2 * g + jj
                    pl_ = slice(pair * 128, (pair + 1) * 128)
                    xp = xdt[:, pl_]
                    dyp_p = dypb[:, pl_]
                    dxp = jnp.zeros((CHUNK, 128), F32)
                    for hh in range(2):
                        hd = 2 * pair + hh
                        dec = _decay(acs, acs_t, hd)
                        xm = jnp.where(_half_mask(hh), xp, 0.0).astype(BF16)
                        dsc = _dot_nt(dyp_p, xm) * dec
                        dcb = dcb + dsc
                        sc = (cb_g * dec).astype(BF16)
                        dxp = dxp + jnp.where(_half_mask(hh), _dot_tn(sc, dyp_p), 0.0)
                        dm = dsc * cb_g
                        diag8 = diag8 + jnp.where(lane8 == MISC_DT + hd, jnp.sum(dm - dm.T, axis=1, keepdims=True), 0.0)
                    diag_parts.append(dxp)
                dcbb = dcb.astype(BF16)
                dcm.append(dcm_g + _dot(dcbb, bm))
                dbm.append(_dot_tn(dcbb, cm))
                diag_all.append(jnp.concatenate(diag_parts, axis=1))
                yoff_parts.append(_dot(cm, prev_g) * e[:, gl])
                yield
            dst = dst_s[...]
            glast = jnp.sum(dst * prev, axis=0, keepdims=True) * cd
            dxdt_state_parts = []
            for g in range(SSD_GROUPS):
                gl = slice(g * 256, (g + 1) * 256)
                dst_g = dst[:, gl].astype(BF16)
                dxdt_state_parts.append(_dot(bms[g], dst_g) * dsx[:, gl])
                dbm[g] = dbm[g] + _dot_nt(wgt[:, gl], dst_g)
            dst_s[...] = dst * cd + jnp.concatenate(dprev_parts, axis=1)
            yield
            dxdt_state = jnp.concatenate(dxdt_state_parts, axis=1)
            dxdt = jnp.concatenate(diag_all, axis=1) + dxdt_state
            dacs = dyp * jnp.concatenate(yoff_parts, axis=1) - xdt * dxdt_state
            last = jnp.sum(xdt * dxdt_state, axis=0, keepdims=True) + glast
            row = lax.broadcasted_iota(jnp.int32, (CHUNK, SSD_INNER), 0)
            dacs = dacs + jnp.where(row == CHUNK - 1, last, 0.0)
            dacs8 = _dot01(dacs, emisct_ref[...]) + diag8
            da8 = _dot01(dacs8, trit_ref[...], left=True)
            ddt8 = da8 * a8 + _dot01(dxdt * xs, emisct_ref[...])
            yield
            dtr8 = mm + _dot01(dtb_ref[...], emisct_ref[...]) * (1.0 / SSD_P)
            dt8 = jax.nn.softplus(dtr8)
            lane = lax.broadcasted_iota(jnp.int32, (CHUNK, HEAD_PAD), 1)
            on_dt = jnp.logical_and(lane >= MISC_DT, lane < MISC_DT + SSD_HEADS)
            ddtr8 = jnp.where(on_dt, ddt8 * jax.nn.sigmoid(dtr8), 0.0)
            dmisc_ref[pl.ds(r0, CHUNK), :] = ddtr8
            gdtb_ref[...] += jnp.sum(ddtr8, axis=0, keepdims=True)
            galog_ref[...] += jnp.sum(jnp.where(on_dt, da8 * dt8, 0.0), axis=0, keepdims=True) * a8
            dxs = d_ref[...] * dyp + dxdt * dt
            dxa = jnp.concatenate([dxs] + dbm + dcm, axis=1)
            dc_s[pl.ds(r0, CHUNK), :] = dxa * (sig_c * (1.0 + cc * (1.0 - sig_c)))

        _interleave([chunk(ci) for ci in reversed(range(ncb))])

        dc = dc_s[...]
        x = x_ref[...]
        dcext = jnp.concatenate([dc, head_s[...]], axis=0)
        dx = dc * cw_ref[CONV_W - 1:CONV_W, :]
        rows = [jnp.sum(dc * x, axis=0, keepdims=True)]
        for j in range(1, CONV_W):
            ahead = pltpu.roll(dcext, SSD_ROWS + 8 - j, 0)[:SSD_ROWS, :]
            dx = dx + ahead * cw_ref[CONV_W - 1 - j:CONV_W - j, :]
            rows.insert(0, jnp.sum(ahead * x, axis=0, keepdims=True))
        dx_ref[...] = dx
        head_s[...] = dc[:8, :]
        gcw = jnp.concatenate(rows, axis=0)

        @pl.when(first)
        def _():
            gcw_ref[...] = gcw
            gcb_ref[...] = jnp.sum(dc, axis=0, keepdims=True)

        @pl.when(jnp.logical_not(first))
        def _():
            gcw_ref[...] += gcw
            gcb_ref[...] += jnp.sum(dc, axis=0, keepdims=True)

    def rev(width):
        return pl.BlockSpec((SSD_ROWS, width), lambda i: (nb - 1 - i, 0))

    return pl.pallas_call(
        body, name="ssd_bwd", grid=(nb,),
        in_specs=[rev(SSD_INNER), rev(SSD_INNER), rev(SSD_INNER), rev(CONV_DIM), rev(CONV_DIM),
                  rev(HEAD_PAD), pl.BlockSpec((ncb, SSD_N, SSD_INNER), lambda i: (nb - 1 - i, 0, 0)),
                  _full((CONV_W, CONV_DIM)), _full((1, SSD_INNER)), _full((1, SSD_INNER)), _full((1, SSD_INNER)),
                  _full((1, SSD_INNER)), _full((HEAD_PAD, SSD_INNER)), _full((SSD_INNER, HEAD_PAD)), _full((CHUNK, CHUNK)),
                  _full((CHUNK, CHUNK))],
        out_specs=[rev(SSD_INNER), rev(CONV_DIM), rev(HEAD_PAD), _full((1, SSD_INNER)), _full((1, SSD_INNER)),
                   _full((1, HEAD_PAD)), _full((1, HEAD_PAD)), _full((CONV_W, CONV_DIM)), _full((1, CONV_DIM))],
        out_shape=[jax.ShapeDtypeStruct((s, SSD_INNER), F32), jax.ShapeDtypeStruct((s, CONV_DIM), F32),
                   jax.ShapeDtypeStruct((s, HEAD_PAD), F32), jax.ShapeDtypeStruct((1, SSD_INNER), F32),
                   jax.ShapeDtypeStruct((1, SSD_INNER), F32), jax.ShapeDtypeStruct((1, HEAD_PAD), F32),
                   jax.ShapeDtypeStruct((1, HEAD_PAD), F32), jax.ShapeDtypeStruct((CONV_W, CONV_DIM), F32),
                   jax.ShapeDtypeStruct((1, CONV_DIM), F32)],
        scratch_shapes=[pltpu.VMEM((SSD_N, SSD_INNER), F32), pltpu.VMEM((SSD_ROWS, CONV_DIM), F32), pltpu.VMEM((8, CONV_DIM), F32)],
        compiler_params=_params(),
    )(dy, ypre, z, c, xraw, misc, prev, cw, dtb, a_exp, d_exp, nw, emisc, emisc_t, tri, trit)


def _qkv_bwd(dq, dk, dv, cq, ckv, qnw, kvnw, wuq, wkv, cosf, sinf):
    s = dq.shape[0]
    wide = MLA_HEADS * HEAD_PAD

    def body(dq_ref, dk_ref, dv_ref, cq_ref, ckv_ref, qnw_ref, kvnw_ref, wuq_ref, wkv_ref, cos_ref, sin_ref,
             dqb_ref, dkvb_ref, dcq_ref, dckv_ref, dmisc_ref, gq_ref, gkv_ref):
        first = pl.program_id(0) == 0
        cosf, sinf = cos_ref[...], sin_ref[...]
        dkr = jnp.zeros((TM, HEAD_PAD), F32)
        for hd in range(MLA_HEADS):
            cols = slice(hd * HEAD_PAD, (hd + 1) * HEAD_PAD)
            dqb_ref[:, cols] = _rope(dq_ref[:, cols], cosf, sinf, -1.0).astype(BF16)
            dkh = dk_ref[:, cols]
            dkvb_ref[:, cols] = dkh.astype(BF16)
            dkr = dkr + dkh
        dkvb_ref[:, wide:] = dv_ref[...].astype(BF16)
        lane = lax.broadcasted_iota(jnp.int32, dkr.shape, 1)
        in_rope = jnp.logical_and(lane >= MISC_ROPE, lane < MISC_ROPE + QK_ROPE)
        dmisc_ref[...] = jnp.where(in_rope, _rope(jnp.where(in_rope, dkr, 0.0), cosf, sinf, -1.0), 0.0)
        dcq, gq = _rms_bwd(cq_ref[...], qnw_ref[...], _dot_nt(dqb_ref[...], wuq_ref[...]))
        dcq_ref[...] = dcq
        _acc_rows(gq_ref, gq, first)
        dckv, gkv = _rms_bwd(ckv_ref[...], kvnw_ref[...], _dot_nt(dkvb_ref[...], wkv_ref[...]))
        dckv_ref[...] = dckv
        _acc_rows(gkv_ref, gkv, first)

    return pl.pallas_call(
        body, name="qkv_bwd", grid=(s // TM,),
        in_specs=[_rows(TM, wide)] * 3 + [_rows(TM, Q_RANK), _rows(TM, KV_RANK), _full((1, Q_RANK)), _full((1, KV_RANK)),
                                          _resident((Q_RANK, wide)), _resident((KV_RANK, 2 * wide)), _rows(TM, HEAD_PAD), _rows(TM, HEAD_PAD)],
        out_specs=[_rows(TM, wide), _rows(TM, 2 * wide), _rows(TM, Q_RANK), _rows(TM, KV_RANK), _rows(TM, HEAD_PAD),
                   _full((1, Q_RANK)), _full((1, KV_RANK))],
        out_shape=[jax.ShapeDtypeStruct((s, wide), BF16), jax.ShapeDtypeStruct((s, 2 * wide), BF16),
                   jax.ShapeDtypeStruct((s, Q_RANK), F32), jax.ShapeDtypeStruct((s, KV_RANK), F32),
                   jax.ShapeDtypeStruct((s, HEAD_PAD), F32), jax.ShapeDtypeStruct((1, Q_RANK), F32),
                   jax.ShapeDtypeStruct((1, KV_RANK), F32)],
        compiler_params=_params(),
    )(dq, dk, dv, cq, ckv, qnw, kvnw, wuq, wkv, cosf, sinf)


def _inproj_bwd(dcq, dckv, dmisc_rope, dmisc_dt, dz, dxbc, h, dh1, nw, win):
    s = h.shape[0]

    def body(dcq_ref, dckv_ref, dma_ref, dmb_ref, dz_ref, dxbc_ref, h_ref, dh1_ref, nw_ref, w_ref, dproj_ref, dh0_ref, gnw_ref):
        dproj_ref[:, 0:768] = dcq_ref[...].astype(BF16)
        dproj_ref[:, 768:1024] = dckv_ref[...].astype(BF16)
        dproj_ref[:, 1024:1152] = (dma_ref[...] + dmb_ref[...]).astype(BF16)
        dproj_ref[:, 1152:1664] = dz_ref[...].astype(BF16)
        dproj_ref[:, 1664:2688] = dxbc_ref[...].astype(BF16)
        du = _dot_nt(dproj_ref[...], w_ref[...])
        dx, gnw = _rms_bwd(h_ref[...], nw_ref[...], du)
        _acc_rows(gnw_ref, gnw, pl.program_id(0) == 0)
        dh0_ref[...] = dh1_ref[...] + dx

    return pl.pallas_call(
        body, name="inproj_bwd", grid=(s // TM,),
        in_specs=[_rows(TM, Q_RANK), _rows(TM, KV_RANK), _rows(TM, HEAD_PAD), _rows(TM, HEAD_PAD), _rows(TM, SSD_INNER),
                  _rows(TM, CONV_DIM), _rows(TM, D_MODEL), _rows(TM, D_MODEL), _full((1, D_MODEL)), _resident((D_MODEL, IN_PAD))],
        out_specs=[_rows(TM, IN_PAD), _rows(TM, D_MODEL), _full((1, D_MODEL))],
        out_shape=[jax.ShapeDtypeStruct((s, IN_PAD), BF16), jax.ShapeDtypeStruct((s, D_MODEL), F32),
                   jax.ShapeDtypeStruct((1, D_MODEL), F32)],
        compiler_params=_params(),
    )(dcq, dckv, dmisc_rope, dmisc_dt, dz, dxbc, h, dh1, nw, win)


def _row_tile(rows, cols):
    cap = max(8, (1 << 18) // max(cols, 128))
    best = None
    for t in range(8, rows + 1, 8):
        if rows % t == 0 and t <= cap:
            best = t
    return best if best is not None else rows


def _adamw(w, g, m, v, name):
    rows, cols = w.shape
    tr = _row_tile(rows, cols)

    def body(w_ref, g_ref, m_ref, v_ref, d_ref, m2_ref, v2_ref):
        gg = g_ref[...]
        m2 = ADAM_B1 * m_ref[...] + (1.0 - ADAM_B1) * gg
        v2 = ADAM_B2 * v_ref[...] + (1.0 - ADAM_B2) * jnp.square(gg)
        m_hat = m2 / (1.0 - ADAM_B1 ** ADAM_STEP)
        v_hat = v2 / (1.0 - ADAM_B2 ** ADAM_STEP)
        d_ref[...] = -ADAM_LR * (m_hat / (jnp.sqrt(v_hat) + ADAM_EPS) + ADAM_WD * w_ref[...])
        m2_ref[...] = m2
        v2_ref[...] = v2

    spec = pl.BlockSpec((tr, cols), lambda i: (i, 0))
    return pl.pallas_call(
        body, name=name, grid=(rows // tr,),
        in_specs=[spec] * 4, out_specs=[spec] * 3,
        out_shape=[jax.ShapeDtypeStruct((rows, cols), F32)] * 3,
    )(w, g, m, v)


def _sum_adamw(slots, w, m, v, name):
    _, rows, cols = w.shape
    tr = _row_tile(rows, cols)
    nb = rows // tr

    def body(s0_ref, s1_ref, w_ref, m_ref, v_ref, g_ref, d_ref, m2_ref, v2_ref):
        for l, ref in enumerate((s0_ref, s1_ref)):
            @pl.when(pl.program_id(0) == l)
            def _(ref=ref):
                acc = ref[0].astype(F32)
                for i in range(1, N_DEV):
                    acc = acc + ref[i].astype(F32)
                g_ref[...] = acc

        gg = g_ref[...]
        m2 = ADAM_B1 * m_ref[...] + (1.0 - ADAM_B1) * gg
        v2 = ADAM_B2 * v_ref[...] + (1.0 - ADAM_B2) * jnp.square(gg)
        m_hat = m2 / (1.0 - ADAM_B1 ** ADAM_STEP)
        v_hat = v2 / (1.0 - ADAM_B2 ** ADAM_STEP)
        d_ref[...] = -ADAM_LR * (m_hat / (jnp.sqrt(v_hat) + ADAM_EPS) + ADAM_WD * w_ref[...])
        m2_ref[...] = m2
        v2_ref[...] = v2

    slot_spec = lambda layer: pl.BlockSpec((N_DEV, tr, cols), lambda l, i: (0, jnp.where(l == layer, i, (nb - 1) * (1 - layer)), 0))
    spec = pl.BlockSpec((None, tr, cols), lambda l, i: (l, i, 0))
    return pl.pallas_call(
        body, name=name, grid=(DEPTH, nb),
        in_specs=[slot_spec(0), slot_spec(1), spec, spec, spec], out_specs=[spec] * 4,
        out_shape=[jax.ShapeDtypeStruct(w.shape, F32)] * 4,
        compiler_params=_params(),
    )(slots[0], slots[1], w, m, v)


_MESH = pl.DeviceIdType.MESH
_ANY = pl.BlockSpec(memory_space=pl.ANY)


def _my_place():
    return lax.axis_index("x"), lax.axis_index("y"), lax.axis_index("c")


def _flip(place, k):
    x, y, c = place
    return (1 - x if k & 4 else x, 1 - y if k & 2 else y, 1 - c if k & 1 else c)


def _block_id(place):
    return 4 * place[0] + 2 * place[1] + place[2]


def _peer_copies(kind, in_refs, out_refs, send_sems, recv_sems, local_sems):
    me = _my_place()
    my = _block_id(me)
    remote, local = [], []
    for a, (x_ref, out_ref) in enumerate(zip(in_refs, out_refs)):
        src_of = (lambda place, r=x_ref: r) if kind == "gather" else (lambda place, r=x_ref: r.at[_block_id(place)])
        local.append(pltpu.make_async_copy(src_of(me), out_ref.at[my], local_sems.at[a]))
        for k in range(1, N_DEV):
            peer = _flip(me, k)
            remote.append(pltpu.make_async_remote_copy(
                src_ref=src_of(peer), dst_ref=out_ref.at[my], send_sem=send_sems.at[a * 7 + k - 1],
                recv_sem=recv_sems.at[a * 7 + k - 1], device_id=peer, device_id_type=_MESH))
    return remote, local


def _comm_out_shapes(kind, arrays):
    return [jax.ShapeDtypeStruct((N_DEV, *a.shape) if kind == "gather" else a.shape, a.dtype) for a in arrays]


def _comm_scratch(n):
    return [pltpu.SemaphoreType.DMA((7 * n,)), pltpu.SemaphoreType.DMA((7 * n,)), pltpu.SemaphoreType.DMA((n,))]


def _hosted_comm(kind, in_refs, out_refs, sems, first, last):
    if not in_refs:
        return

    @pl.when(first)
    def _():
        remote, local = _peer_copies(kind, in_refs, out_refs, *sems)
        for cp in local + remote:
            cp.start()

    @pl.when(last)
    def _():
        remote, local = _peer_copies(kind, in_refs, out_refs, *sems)
        for cp in remote:
            cp.wait()
        for cp in local:
            cp.wait()


def _two_level_gather_steps(in_refs, out_refs, send_sems, recv_sems, local_sems):
    n = len(in_refs)
    me = _my_place()
    x, y, c = me
    sibling = (x, y, 1 - c)
    chips = [(1 - x, y), (x, 1 - y), (1 - x, 1 - y)]

    def copy(a, k, place, to, src=None):
        block = out_refs[a].at[_block_id(place)]
        return pltpu.make_async_remote_copy(
            src_ref=block if src is None else src, dst_ref=block, send_sem=send_sems.at[7 * a + k],
            recv_sem=recv_sems.at[7 * a + k], device_id=to, device_id_type=_MESH)

    mine = [pltpu.make_async_copy(in_refs[a], out_refs[a].at[_block_id(me)], local_sems.at[a]) for a in range(n)]
    first = [copy(a, 0, me, sibling, src=in_refs[a]) for a in range(n)]
    first += [copy(a, 1 + j, me, (*chip, c), src=in_refs[a]) for a in range(n) for j, chip in enumerate(chips)]
    passed = [copy(a, 4 + j, (*chip, c), sibling) for a in range(n) for j, chip in enumerate(chips)]

    def send():
        for cp in mine + first:
            cp.start()

    def forward():
        for a in range(n):
            for j, chip in enumerate(chips):
                copy(a, 1 + j, (*chip, c), me).wait_recv()
                passed[3 * a + j].start()

    def finish():
        for a in range(n):
            copy(a, 0, sibling, me).wait_recv()
            for j, chip in enumerate(chips):
                copy(a, 4 + j, (*chip, 1 - c), me).wait_recv()
        for cp in first + passed:
            cp.wait_send()
        for cp in mine:
            cp.wait()

    return send, forward, finish


def _gather_two_level(arrays, name):
    n = len(arrays)

    def body(*refs):
        for step in _two_level_gather_steps(refs[:n], refs[n:2 * n], *refs[2 * n:]):
            step()

    return pl.pallas_call(
        body, name=name, out_shape=_comm_out_shapes("gather", arrays),
        in_specs=[_ANY] * n, out_specs=[_ANY] * n, scratch_shapes=_comm_scratch(n),
    )(*arrays)


def _hosted_gather(in_refs, out_refs, sems, first, middle, last):
    if not in_refs:
        return
    for when, index in ((first, 0), (middle, 1), (last, 2)):
        @pl.when(when)
        def _(index=index):
            _two_level_gather_steps(in_refs, out_refs, *sems)[index]()


def _comm(kind, arrays, name):
    n = len(arrays)

    def body(*refs):
        remote, local = _peer_copies(kind, refs[:n], refs[n:2 * n], *refs[2 * n:])
        for cp in local + remote:
            cp.start()
        for cp in remote:
            cp.wait()
        for cp in local:
            cp.wait()

    return pl.pallas_call(
        body, name=name, out_shape=_comm_out_shapes(kind, arrays),
        in_specs=[_ANY] * n, out_specs=[_ANY] * n, scratch_shapes=_comm_scratch(n),
    )(*arrays)


def _all_reduce_small(part):
    rows, lanes = part.shape
    vmem = pl.BlockSpec(memory_space=pltpu.VMEM)

    def body(x_ref, gath_ref, sum_ref, send_sems, recv_sems):
        me = _my_place()
        my = _block_id(me)
        gath_ref[my] = x_ref[...]
        copies = []
        for k in range(1, N_DEV):
            cp = pltpu.make_async_remote_copy(
                src_ref=x_ref, dst_ref=gath_ref.at[my], send_sem=send_sems.at[k - 1], recv_sem=recv_sems.at[k - 1],
                device_id=_flip(me, k), device_id_type=_MESH)
            cp.start()
            copies.append(cp)
        for cp in copies:
            cp.wait()
        acc = gath_ref[0]
        for i in range(1, N_DEV):
            acc = acc + gath_ref[i]
        sum_ref[...] = acc

    return pl.pallas_call(
        body, name="small_grad_all_reduce",
        out_shape=[jax.ShapeDtypeStruct((N_DEV, rows, lanes), F32), jax.ShapeDtypeStruct((rows, lanes), F32)],
        in_specs=[vmem], out_specs=[vmem, vmem],
        scratch_shapes=[pltpu.SemaphoreType.DMA((7,)), pltpu.SemaphoreType.DMA((7,))],
    )(part)[1]


_SHARDED = (("w_in", (D_MODEL, IN_PROJ // N_DEV)), ("w_uq", (Q_RANK // N_DEV, Q_RANK)), ("w_ukv", (KV_RANK, HEAD_PAD)),
            ("conv_w", (CONV_W, CONV_DIM // N_DEV)), ("w_out", (D_MODEL // N_DEV, D_MODEL)),
            ("w_up", (D_MODEL, D_FF // N_DEV)), ("w_down", (D_FF // N_DEV, D_MODEL)))
_SMALL = (("pre_mix_norm", D_MODEL), ("q_norm", Q_RANK), ("kv_norm", KV_RANK), ("conv_b", CONV_DIM), ("dt_bias", SSD_HEADS),
          ("a_log", SSD_HEADS), ("d_skip", SSD_HEADS), ("ssd_norm", SSD_INNER), ("post_mix_norm", D_MODEL),
          ("pre_mlp_norm", D_MODEL), ("post_mlp_norm", D_MODEL))
_WEIGHT_ORDER = ("pre_mix_norm", "w_in", "q_norm", "w_uq", "kv_norm", "w_ukv", "conv_w", "conv_b", "dt_bias", "a_log", "d_skip",
                 "ssd_norm", "w_out", "post_mix_norm", "pre_mlp_norm", "w_up", "w_down", "post_mlp_norm")
_EARLY = ("w_in", "w_uq", "w_ukv", "conv_w")
_LATE = ("w_out", "w_up", "w_down")


def _wire_shard(name, a):
    return lax.bitcast_convert_type(a, BF16).reshape(CONV_W, -1) if name == "conv_w" else a.astype(BF16)


def _from_wire(name, g):
    return lax.bitcast_convert_type(g.reshape(N_DEV, CONV_W, -1, 2), F32) if name == "conv_w" else g


def _cols(stacked):
    return jnp.transpose(stacked, (1, 0, 2)).reshape(stacked.shape[1], -1)


def _early_weights(sh):
    w_in = _cols(sh["w_in"])
    zeros = lambda n: jnp.zeros((D_MODEL, n), BF16)
    s1, s2, s3, s4, s5 = 768, 1024, 1056, 1568, 2592
    win = jnp.concatenate([w_in[:, :s2], zeros(MISC_ROPE), w_in[:, s2:s3], w_in[:, s5:], zeros(HEAD_PAD - MISC_DT - SSD_HEADS),
                           w_in[:, s3:s5]], axis=1)
    w_uq = sh["w_uq"].reshape(Q_RANK, MLA_HEADS, QK_NOPE + QK_ROPE)
    wuq = jnp.pad(w_uq, ((0, 0), (0, 0), (0, HEAD_PAD - QK_NOPE - QK_ROPE))).reshape(Q_RANK, -1)
    w_ukv = _cols(sh["w_ukv"]).reshape(KV_RANK, MLA_HEADS, QK_NOPE + V_DIM)
    wkn = jnp.pad(w_ukv[..., :QK_NOPE], ((0, 0), (0, 0), (0, HEAD_PAD - QK_NOPE))).reshape(KV_RANK, -1)
    wv = w_ukv[..., QK_NOPE:].reshape(KV_RANK, 4, 2, 1, V_DIM) * jnp.eye(2, dtype=BF16).reshape(1, 1, 2, 2, 1)
    wkv = jnp.concatenate([wkn, wv.reshape(KV_RANK, -1)], axis=1)
    return dict(win=win, wuq=wuq, wkv=wkv, conv_w=_cols(sh["conv_w"]))


def _late_weights(sh):
    w_out = sh["w_out"].reshape(D_MODEL, D_MODEL)
    watt = w_out[:SSD_INNER].reshape(4, 2, 1, V_DIM, D_MODEL) * jnp.eye(2, dtype=BF16).reshape(1, 2, 2, 1, 1)
    wout = jnp.concatenate([watt.reshape(MLA_HEADS * HEAD_PAD, D_MODEL), w_out[SSD_INNER:]], axis=0)
    return dict(wout=wout, wup=sh["w_up"], wdown=sh["w_down"])


def _shard_grads(g):
    out = {}
    if "wup" in g:
        out["w_up"], out["w_down"] = g["wup"], g["wdown"]
        ae = g["wout_att"].reshape(4, 2, 2, V_DIM, D_MODEL)
        att = jnp.stack([ae[:, 0, 0], ae[:, 1, 1]], axis=1).reshape(SSD_INNER, D_MODEL)
        out["w_out"] = jnp.concatenate([att, g["wout_ssd"]], axis=0).astype(BF16).reshape(N_DEV, D_MODEL // N_DEV, D_MODEL)
    if "win" not in g:
        return out
    dwin = g["win"]
    s1, s2 = 768, 1024
    m0 = s2
    w_in = jnp.concatenate([dwin[:, :s2], dwin[:, m0 + MISC_ROPE:m0 + MISC_ROPE + QK_ROPE], dwin[:, 1152:2688],
                            dwin[:, m0 + MISC_DT:m0 + MISC_DT + SSD_HEADS]], axis=1)
    out["w_in"] = jnp.transpose(w_in.astype(BF16).reshape(D_MODEL, N_DEV, -1), (1, 0, 2))
    w_uq = g["wuq"].astype(BF16).reshape(Q_RANK, MLA_HEADS, HEAD_PAD)[..., :QK_NOPE + QK_ROPE].reshape(Q_RANK, Q_RANK)
    out["w_uq"] = w_uq.reshape(N_DEV, Q_RANK // N_DEV, Q_RANK)
    wide = MLA_HEADS * HEAD_PAD
    wkv = g["wkv"].astype(BF16)
    kn = wkv[:, :wide].reshape(KV_RANK, MLA_HEADS, HEAD_PAD)[..., :QK_NOPE]
    ve = wkv[:, wide:].reshape(KV_RANK, 4, 2, 2, V_DIM)
    vv = jnp.stack([ve[:, :, 0, 0], ve[:, :, 1, 1]], axis=2).reshape(KV_RANK, MLA_HEADS, V_DIM)
    out["w_ukv"] = jnp.transpose(jnp.concatenate([kn, vv], axis=-1), (1, 0, 2))
    out["conv_w"] = jnp.transpose(g["conv_w"].astype(BF16).reshape(CONV_W, N_DEV, -1), (1, 0, 2))
    return out


def _small_rows(n):
    return -(-n // 1024) * 8


def _pack_small(vals):
    rows = []
    for l in range(DEPTH):
        for name, n in _SMALL:
            r = _small_rows(n)
            rows.append(jnp.pad(vals[name][l].reshape(-1), (0, r * 128 - n)).reshape(r, 128))
    return jnp.concatenate(rows, axis=0)


def _unpack_small(packed):
    out, off = {name: [] for name, _ in _SMALL}, 0
    for l in range(DEPTH):
        for name, n in _SMALL:
            r = _small_rows(n)
            out[name].append(packed[off:off + r].reshape(-1)[:n])
            off += r
    return {name: jnp.stack(v) for name, v in out.items()}


def _lane_rows(vec8):
    return jnp.repeat(vec8, SSD_P).reshape(1, SSD_INNER)


def _layer_fwd(h, kw, sm, l, cosf, sinf, consts, gather=(), after_gather=None, target=None):
    row = lambda name: sm[name][l].reshape(1, -1)
    t = {}
    t["h0"] = h
    t["ub"], t["cq"], t["ckv"], t["misc"], t["z"], t["xraw"] = _inproj_fwd(h, row("pre_mix_norm"), kw["win"])
    t["cqn"], t["ckvn"], t["q"], t["k"], t["v"] = _qkv_fwd(t["cq"], t["ckv"], t["misc"], row("q_norm"), row("kv_norm"),
                                                         kw["wuq"], kw["wkv"], cosf, sinf)
    t["oe"], t["lse"], gathered = _attn_fwd(t["q"], t["k"], t["v"], gather)
    if after_gather is not None:
        after_gather(gathered)
    t["dtb"] = _lane_rows(sm["dt_bias"][l])
    t["a_exp"] = _lane_rows(-jnp.exp(sm["a_log"][l]))
    t["d_exp"] = _lane_rows(sm["d_skip"][l])
    t["c"], t["prev"], t["ypre"], t["yssd"] = _ssd_fwd(t["xraw"], t["misc"], t["z"], kw["conv_w"], row("conv_b"), t["dtb"],
                                                     t["a_exp"], t["d_exp"], row("ssd_norm"), consts)
    t["mixed"], t["h1"] = _outproj_fwd(t["oe"], t["yssd"], kw["wout"], h, row("post_mix_norm"))
    t["mb"], t["d"], *out = _mlp_fwd(t["h1"], row("pre_mlp_norm"), kw["wup"], kw["wdown"], row("post_mlp_norm"), target)
    return out, t


def _layer_bwd(dh2, t, kw, sm, l, cosf, sinf, consts, exchange_of=None):
    row = lambda name: sm[name][l].reshape(1, -1)
    g, gs = {}, {}
    dh1, dab, rb, ddb, gs["post_mlp_norm"], gs["pre_mlp_norm"] = _mlp_bwd(
        dh2, t["d"], t["h1"], t["mb"], row("pre_mlp_norm"), kw["wup"], kw["wdown"], row("post_mlp_norm"))
    g["wup"] = _matmul_tn_stacked(t["mb"], dab, f"dw_up_{l}", a_stacked=False)
    g["wdown"] = _matmul_tn_stacked(rb, ddb, f"dw_down_{l}", a_stacked=True)
    dmixb, doe, dyssd, gs["post_mix_norm"], delta = _outproj_bwd(dh1, t["mixed"], row("post_mix_norm"), kw["wout"], t["oe"])
    g["wout_att"] = _matmul_tn(t["oe"], dmixb, f"dw_out_att_{l}")
    g["wout_ssd"] = _matmul_tn(t["yssd"], dmixb, f"dw_out_ssd_{l}")
    dz, dxraw, dmisc_dt, gs["ssd_norm"], gd, galog, gdtb, g["conv_w"], gs["conv_b"] = _ssd_bwd(
        dyssd, t["ypre"], t["z"], t["c"], t["xraw"], t["misc"], t["prev"], kw["conv_w"], t["dtb"], t["a_exp"], t["d_exp"],
        row("ssd_norm"), consts)
    gs["d_skip"] = jnp.sum(gd.reshape(SSD_HEADS, SSD_P), axis=1)
    gs["a_log"] = galog[0, MISC_DT:MISC_DT + SSD_HEADS]
    gs["dt_bias"] = gdtb[0, MISC_DT:MISC_DT + SSD_HEADS]
    dq, dk, dv, exchanged = _attn_bwd(t["q"], t["k"], t["v"], doe, t["lse"], delta,
                                      exchange_of(g) if exchange_of is not None else ())
    dqb, dkvb, dcq, dckv, dmisc_rope, gs["q_norm"], gs["kv_norm"] = _qkv_bwd(
        dq, dk, dv, t["cq"], t["ckv"], row("q_norm"), row("kv_norm"), kw["wuq"], kw["wkv"], cosf, sinf)
    g["wuq"] = _matmul_tn(t["cqn"], dqb, f"dw_uq_{l}")
    g["wkv"] = _matmul_tn(t["ckvn"], dkvb, f"dw_kv_{l}")
    dprojb, dh0, gs["pre_mix_norm"] = _inproj_bwd(dcq, dckv, dmisc_rope, dmisc_dt, dz, dxraw, t["h0"], dh1,
                                                  row("pre_mix_norm"), kw["win"])
    g["win"] = _matmul_tn(t["ub"], dprojb, f"dw_in_{l}")
    return dh0, g, {k: v.reshape(-1) for k, v in gs.items()}, exchanged


def _local_step(x, positions, kws, sm, target, gather=(), after_gather=None, exchange_of=None):
    inv_freq = ROPE_THETA ** (-jnp.arange(0, QK_ROPE, 2, dtype=F32) / QK_ROPE)
    invf = jnp.zeros((HEAD_PAD,), F32).at[MISC_ROPE:MISC_ROPE + QK_ROPE].set(jnp.concatenate([inv_freq, inv_freq]))
    cosf, sinf = _rope_tables(positions.reshape(-1, 1), invf.reshape(1, HEAD_PAD))
    consts = _ssd_consts()
    (h,), t0 = _layer_fwd(x, kws[0], sm, 0, cosf, sinf, consts, gather, after_gather)
    (dh, loss), t1 = _layer_fwd(h, kws[1], sm, 1, cosf, sinf, consts, target=target)
    saved = [t0, t1]
    grads, small, exchanged = [None] * DEPTH, [None] * DEPTH, []
    for l in reversed(range(DEPTH)):
        hook = (lambda g0: exchange_of(g0, grads[1])) if (l == 0 and exchange_of is not None) else None
        dh, grads[l], small[l], got = _layer_bwd(dh, saved[l], kws[l], sm, l, cosf, sinf, consts, hook)
        exchanged = got or exchanged
    return loss[0, 0], dh, grads, small, exchanged


def kernel(x, positions, pre_mix_norm, w_in, q_norm, w_uq, kv_norm, w_ukv, conv_w, conv_b, dt_bias, a_log, d_skip, ssd_norm, w_out, post_mix_norm, pre_mlp_norm, w_up, w_down, post_mlp_norm, loss_target, m_pre_mix_norm, m_w_in, m_q_norm, m_w_uq, m_kv_norm, m_w_ukv, m_conv_w, m_conv_b, m_dt_bias, m_a_log, m_d_skip, m_ssd_norm, m_w_out, m_post_mix_norm, m_pre_mlp_norm, m_w_up, m_w_down, m_post_mlp_norm, v_pre_mix_norm, v_w_in, v_q_norm, v_w_uq, v_kv_norm, v_w_ukv, v_conv_w, v_conv_b, v_dt_bias, v_a_log, v_d_skip, v_ssd_norm, v_w_out, v_post_mix_norm, v_pre_mlp_norm, v_w_up, v_w_down, v_post_mlp_norm):
    w = dict(pre_mix_norm=pre_mix_norm, w_in=w_in, q_norm=q_norm, w_uq=w_uq, kv_norm=kv_norm, w_ukv=w_ukv, conv_w=conv_w,
             conv_b=conv_b, dt_bias=dt_bias, a_log=a_log, d_skip=d_skip, ssd_norm=ssd_norm, w_out=w_out,
             post_mix_norm=post_mix_norm, pre_mlp_norm=pre_mlp_norm, w_up=w_up, w_down=w_down, post_mlp_norm=post_mlp_norm)
    m = dict(pre_mix_norm=m_pre_mix_norm, w_in=m_w_in, q_norm=m_q_norm, w_uq=m_w_uq, kv_norm=m_kv_norm, w_ukv=m_w_ukv,
             conv_w=m_conv_w, conv_b=m_conv_b, dt_bias=m_dt_bias, a_log=m_a_log, d_skip=m_d_skip, ssd_norm=m_ssd_norm,
             w_out=m_w_out, post_mix_norm=m_post_mix_norm, pre_mlp_norm=m_pre_mlp_norm, w_up=m_w_up, w_down=m_w_down,
             post_mlp_norm=m_post_mlp_norm)
    v = dict(pre_mix_norm=v_pre_mix_norm, w_in=v_w_in, q_norm=v_q_norm, w_uq=v_w_uq, kv_norm=v_kv_norm, w_ukv=v_w_ukv,
             conv_w=v_conv_w, conv_b=v_conv_b, dt_bias=v_dt_bias, a_log=v_a_log, d_skip=v_d_skip, ssd_norm=v_ssd_norm,
             w_out=v_w_out, post_mix_norm=v_post_mix_norm, pre_mlp_norm=v_pre_mlp_norm, w_up=v_w_up, w_down=v_w_down,
             post_mlp_norm=v_post_mlp_norm)
    sm = {name: w[name] for name, _ in _SMALL}

    wire = lambda name, l: _wire_shard(name, w[name][l])
    first = _gather_two_level([wire(name, 0) for name in _EARLY], "weight_gather_first")
    kws = [_early_weights({name: _from_wire(name, a) for name, a in zip(_EARLY, first)}), None]
    behind = [(name, 0) for name in _LATE] + [(name, 1) for name, _ in _SHARDED]

    def after_gather(gathered):
        got = {key: _from_wire(key[0], a) for key, a in zip(behind, gathered)}
        kws[0].update(_late_weights({name: got[name, 0] for name in _LATE}))
        kws[1] = {**_early_weights({name: got[name, 1] for name in _EARLY}),
                  **_late_weights({name: got[name, 1] for name in _LATE})}

    sent_behind = [(name, 1) for name, _ in _SHARDED] + [(name, 0) for name in _LATE]

    def exchange_of(g0, g1):
        blocks = {**{(name, 1): a for name, a in _shard_grads(g1).items()},
                  **{(name, 0): a for name, a in _shard_grads(g0).items()}}
        return [blocks[key] for key in sent_behind]

    loss_part, dx, grads, small, exchanged = _local_step(
        x[0], positions[0], kws, sm, loss_target[0], [wire(*key) for key in behind], after_gather, exchange_of)
    slots = dict(zip(sent_behind, exchanged))
    last = _shard_grads({k: grads[0][k] for k in ("win", "wuq", "wkv", "conv_w")})
    slots.update({(name, 0): a for name, a in zip(_EARLY, _comm("exchange", [last[name] for name in _EARLY], "grad_exchange_last"))})
    g_small = _unpack_small(_all_reduce_small(_pack_small({name: jnp.stack([small[l][name] for l in range(DEPTH)])
                                                           for name, _ in _SMALL})))
    loss = lax.psum(loss_part, ("x", "y", "c"))

    grad, delta, new_m, new_v = {}, {}, {}, {}
    for name, _ in _SHARDED:
        grad[name], delta[name], new_m[name], new_v[name] = _sum_adamw(
            [slots[name, 0], slots[name, 1]], w[name], m[name], v[name], f"sum_adamw_{name}")
    pk = lambda d: _pack_small({name: d[name] for name, _ in _SMALL})
    d_, m_, v_ = _adamw(pk(w), pk(g_small), pk(m), pk(v), "adamw_small")
    for dst, packed in ((delta, d_), (new_m, m_), (new_v, v_)):
        dst.update(_unpack_small(packed))
    grad.update(g_small)

    outs = [loss, dx[None]]
    for d in (grad, delta, new_m, new_v):
        outs += [d[name] for name in _WEIGHT_ORDER]
    return tuple(outs)
```

```python
import jax
import jax.numpy as jnp
import numpy as np
from jax import lax
from jax.experimental import pallas as pl
from jax.experimental.pallas import tpu as pltpu

F32 = jnp.float32
BF16 = jnp.bfloat16
HI = lax.Precision.HIGHEST

D_MODEL = 1024
DEPTH = 2
N_DEV = 8
CHUNK = 64
EPS = 1e-6
MLA_HEADS = 8
QK_NOPE = 64
QK_ROPE = 32
V_DIM = 64
Q_RANK = 768
KV_RANK = 256
ROPE_THETA = 10000.0
SSD_HEADS = 8
SSD_P = 64
SSD_INNER = 512
SSD_GROUPS = 2
SSD_N = 128
CONV_W = 4
CONV_DIM = 1024
D_FF = 4096
IN_PROJ = 2600
HEAD_PAD = 128
IN_PAD = 2688
MISC_ROPE = 64
MISC_DT = 96
ATT_SCALE = (QK_NOPE + QK_ROPE) ** -0.5
LOG2E = 1.4426950408889634
ATT_SCALE_LOG2 = ATT_SCALE * LOG2E

ADAM_LR = 0.001
ADAM_B1 = 0.9
ADAM_B2 = 0.999
ADAM_EPS = 1e-08
ADAM_WD = 0.01
ADAM_STEP = 10

TM = 512
ATT_T = 512
ATT_G = 8
SSD_ROWS = 256
TK_DW = 4096
VMEM_LIMIT = 56 * 1024 * 1024

_NT = (((1,), (1,)), ((), ()))
_TN = (((0,), (0,)), ((), ()))


def _params(**kw):
    return pltpu.CompilerParams(vmem_limit_bytes=VMEM_LIMIT, **kw)


def _dot(a, b, precision=None):
    return jnp.dot(a, b, preferred_element_type=F32, precision=precision)


def _dot_nt(a, b, precision=None):
    return lax.dot_general(a, b, _NT, preferred_element_type=F32, precision=precision)


def _dot_tn(a, b, precision=None):
    return lax.dot_general(a, b, _TN, preferred_element_type=F32, precision=precision)


def _split3(x):
    hi = x.astype(BF16)
    r = x - hi.astype(F32)
    mid = r.astype(BF16)
    return hi, mid, (r - mid.astype(F32)).astype(BF16)


def _dot01(x, m01, dot=_dot, left=False):
    parts = [dot(m01, p) if left else dot(p, m01) for p in _split3(x)]
    return parts[0] + parts[1] + parts[2]


def _full(shape):
    n = len(shape)
    return pl.BlockSpec(shape, lambda *_: (0,) * n)


def _resident(shape):
    n = len(shape)
    return pl.BlockSpec(shape, lambda *_: (0,) * n, pipeline_mode=pl.Buffered(1))


def _rows(tm, width):
    return pl.BlockSpec((tm, width), lambda i: (i, 0))


def _rms_fwd(x, w):
    r = lax.rsqrt(jnp.mean(x * x, axis=-1, keepdims=True) + EPS)
    return (x * r) * w


def _rms_bwd(x, w, dy):
    r = lax.rsqrt(jnp.mean(x * x, axis=-1, keepdims=True) + EPS)
    xh = x * r
    dxn = dy * w
    dx = r * (dxn - xh * jnp.mean(dxn * xh, axis=-1, keepdims=True))
    return dx, dy * xh


def _acc_rows(ref, val, first):
    s = jnp.sum(val, axis=0, keepdims=True)

    @pl.when(first)
    def _():
        ref[...] = s

    @pl.when(jnp.logical_not(first))
    def _():
        ref[...] += s


def _rope(t, cosf, sinf, sign):
    lane = lax.broadcasted_iota(jnp.int32, t.shape, 1)
    rot = jnp.where(lane < MISC_ROPE + QK_ROPE // 2, -pltpu.roll(t, HEAD_PAD - QK_ROPE // 2, 1), pltpu.roll(t, QK_ROPE // 2, 1))
    return t * cosf + sign * (rot * sinf)


def _rope_tables(pos, invf):
    s = pos.shape[0]

    def body(pos_ref, invf_ref, cos_ref, sin_ref):
        ang = pos_ref[...].astype(F32) * invf_ref[...]
        cos_ref[...] = jnp.cos(ang)
        sin_ref[...] = jnp.sin(ang)

    return pl.pallas_call(
        body, name="rope_tables", grid=(s // TM,),
        in_specs=[_rows(TM, 1), _full((1, HEAD_PAD))],
        out_specs=[_rows(TM, HEAD_PAD), _rows(TM, HEAD_PAD)],
        out_shape=[jax.ShapeDtypeStruct((s, HEAD_PAD), F32)] * 2,
    )(pos, invf)


def _inproj_fwd(h, nw, win):
    s = h.shape[0]

    def body(h_ref, nw_ref, w_ref, ub_ref, cq_ref, ckv_ref, misc_ref, z_ref, xbc_ref):
        ub = _rms_fwd(h_ref[...], nw_ref[...]).astype(BF16)
        ub_ref[...] = ub
        proj = _dot(ub, w_ref[...])
        cq_ref[...] = proj[:, 0:768]
        ckv_ref[...] = proj[:, 768:1024]
        misc_ref[...] = proj[:, 1024:1152]
        z_ref[...] = proj[:, 1152:1664]
        xbc_ref[...] = proj[:, 1664:2688]

    widths = (768, 256, 128, 512, 1024)
    return pl.pallas_call(
        body, name="inproj_fwd", grid=(s // TM,),
        in_specs=[_rows(TM, D_MODEL), _full((1, D_MODEL)), _resident((D_MODEL, IN_PAD))],
        out_specs=[_rows(TM, D_MODEL)] + [_rows(TM, w) for w in widths],
        out_shape=[jax.ShapeDtypeStruct((s, D_MODEL), BF16)] + [jax.ShapeDtypeStruct((s, w), F32) for w in widths],
        compiler_params=_params(),
    )(h, nw, win)


def _qkv_fwd(cq, ckv, misc, qnw, kvnw, wuq, wkv, cosf, sinf):
    s = cq.shape[0]

    def body(cq_ref, ckv_ref, misc_ref, qnw_ref, kvnw_ref, wuq_ref, wkv_ref, cos_ref, sin_ref,
             cqn_ref, ckvn_ref, q_ref, k_ref, v_ref):
        cosf, sinf = cos_ref[...], sin_ref[...]
        cqn = _rms_fwd(cq_ref[...], qnw_ref[...]).astype(BF16)
        cqn_ref[...] = cqn
        q = _dot(cqn, wuq_ref[...])
        ckvn = _rms_fwd(ckv_ref[...], kvnw_ref[...]).astype(BF16)
        ckvn_ref[...] = ckvn
        kv = _dot(ckvn, wkv_ref[...])
        m = misc_ref[...]
        lane = lax.broadcasted_iota(jnp.int32, m.shape, 1)
        in_rope = jnp.logical_and(lane >= MISC_ROPE, lane < MISC_ROPE + QK_ROPE)
        kr = jnp.where(in_rope, _rope(m, cosf, sinf, 1.0), 0.0)
        for hd in range(MLA_HEADS):
            cols = slice(hd * HEAD_PAD, (hd + 1) * HEAD_PAD)
            q_ref[:, cols] = _rope(q[:, cols], cosf, sinf, 1.0).astype(BF16)
            k_ref[:, cols] = (kv[:, cols] + kr).astype(BF16)
        vv = kv[:, MLA_HEADS * HEAD_PAD:]
        vlane = lax.broadcasted_iota(jnp.int32, vv.shape, 1)
        ones_at = jnp.where((vlane // HEAD_PAD) % 2 == 0, V_DIM, 0)
        v_ref[...] = jnp.where(vlane % HEAD_PAD == ones_at, 1.0, vv).astype(BF16)

    wide = MLA_HEADS * HEAD_PAD
    return pl.pallas_call(
        body, name="qkv_fwd", grid=(s // TM,),
        in_specs=[_rows(TM, Q_RANK), _rows(TM, KV_RANK), _rows(TM, HEAD_PAD), _full((1, Q_RANK)), _full((1, KV_RANK)),
                  _resident((Q_RANK, wide)), _resident((KV_RANK, 2 * wide)), _rows(TM, HEAD_PAD), _rows(TM, HEAD_PAD)],
        out_specs=[_rows(TM, Q_RANK), _rows(TM, KV_RANK), _rows(TM, wide), _rows(TM, wide), _rows(TM, wide)],
        out_shape=[jax.ShapeDtypeStruct((s, Q_RANK), BF16), jax.ShapeDtypeStruct((s, KV_RANK), BF16)]
        + [jax.ShapeDtypeStruct((s, wide), BF16)] * 3,
        compiler_params=_params(),
    )(cq, ckv, misc, qnw, kvnw, wuq, wkv, cosf, sinf)


def _chunk_bias(t, keys_on_rows=False):
    row = lax.broadcasted_iota(jnp.int32, (t, 1), 0) // CHUNK
    col = lax.broadcasted_iota(jnp.int32, (1, t), 1) // CHUNK
    return jnp.where((row <= col) if keys_on_rows else (col <= row), 0.0, -jnp.inf).astype(F32)


def _attn_fwd(q, k, v, gather=()):
    s = q.shape[0]
    t = ATT_T
    nq = s // t
    pair = ATT_G * HEAD_PAD
    ng = len(gather)

    def body(q_ref, k_ref, v_ref, *rest):
        g_in, (o_ref, lse_ref), g_out = rest[:ng], rest[ng:ng + 2], rest[ng + 2:2 * ng + 2]
        m_s, acc_s, bias_s = rest[2 * ng + 2:2 * ng + 5]
        qi = pl.program_id(1)
        group, groups = pl.program_id(0), MLA_HEADS // ATT_G

        @pl.when(jnp.logical_and(group == 0, qi == 0))
        def _():
            bias_s[...] = _chunk_bias(t)

        _hosted_gather(g_in, g_out, rest[2 * ng + 5:],
                       jnp.logical_and(group == 0, qi == 0),
                       jnp.logical_and(group == groups - 1, qi == min(3 * nq // 4 + 1, nq - 1)),
                       jnp.logical_and(group == groups - 1, qi == nq - 1))
        m_s[...] = jnp.full(m_s.shape, -jnp.inf, F32)
        acc_s[...] = jnp.zeros(acc_s.shape, F32)

        def step(kb, masked):
            r0 = pl.multiple_of(kb * t, t)

            def scores(hh):
                cols = slice(hh * HEAD_PAD, (hh + 1) * HEAD_PAD)
                return _dot_nt(q_ref[:, cols], k_ref[pl.ds(r0, t), cols])

            def soft(hh, raw):
                sc = raw * ATT_SCALE_LOG2
                if masked:
                    sc = sc + bias_s[...]
                m_old = m_s[hh]
                m_new = jnp.maximum(m_old, jnp.max(sc, axis=-1, keepdims=True))
                alpha = jnp.exp2(m_old - m_new)
                p = jnp.exp2(sc - jnp.tile(m_new, (1, t // HEAD_PAD)))
                m_s[hh] = m_new
                return alpha, p.astype(BF16)

            def update(hh, alpha, p):
                cols = slice(hh * HEAD_PAD, (hh + 1) * HEAD_PAD)
                acc_s[hh] = alpha * acc_s[hh] + _dot(p, v_ref[pl.ds(r0, t), cols])

            raw, ap = [None] * ATT_G, [None] * ATT_G
            raw[0] = scores(0)
            for hh in range(ATT_G):
                if hh + 1 < ATT_G:
                    raw[hh + 1] = scores(hh + 1)
                ap[hh] = soft(hh, raw[hh])
                if hh >= 1:
                    update(hh - 1, *ap[hh - 1])
            update(ATT_G - 1, *ap[ATT_G - 1])

        def loop(kb, c):
            step(kb, False)
            return c

        lax.fori_loop(0, qi, loop, 0)
        step(qi, True)
        for hh in range(ATT_G):
            cols = slice(hh * HEAD_PAD, (hh + 1) * HEAD_PAD)
            acc = acc_s[hh]
            ones_at = V_DIM * (1 - hh % 2)
            l = jnp.broadcast_to(acc[:, ones_at:ones_at + 1], acc.shape)
            o_ref[:, cols] = (acc / l).astype(BF16)
            lse_ref[hh] = (m_s[hh] + jnp.log(l) * LOG2E).T[0:8, :]

    outs = pl.pallas_call(
        body, name="attn_fwd_gather" if ng else "attn_fwd", grid=(MLA_HEADS // ATT_G, nq),
        in_specs=[pl.BlockSpec((t, pair), lambda h, i: (i, h)),
                  pl.BlockSpec((s, pair), lambda h, i: (0, h), pipeline_mode=pl.Buffered(1)),
                  pl.BlockSpec((s, pair), lambda h, i: (0, h), pipeline_mode=pl.Buffered(1))] + [_ANY] * ng,
        out_specs=[pl.BlockSpec((t, pair), lambda h, i: (i, h)),
                   pl.BlockSpec((ATT_G, 8, t), lambda h, i: (h, 0, i))] + [_ANY] * ng,
        out_shape=[jax.ShapeDtypeStruct((s, MLA_HEADS * HEAD_PAD), BF16), jax.ShapeDtypeStruct((MLA_HEADS, 8, s), F32)]
        + _comm_out_shapes("gather", gather),
        scratch_shapes=[pltpu.VMEM((ATT_G, t, HEAD_PAD), F32), pltpu.VMEM((ATT_G, t, HEAD_PAD), F32), pltpu.VMEM((t, t), F32)]
        + (_comm_scratch(ng) if ng else []),
        compiler_params=_params(),
    )(q, k, v, *gather)
    return outs[0], outs[1], list(outs[2:])


def _interleave(stages):
    live = list(stages)
    while live:
        still = []
        for g in live:
            try:
                next(g)
                still.append(g)
            except StopIteration:
                pass
        live = still


def _ssd_consts():
    emisc = np.zeros((HEAD_PAD, SSD_INNER), np.float32)
    for hd in range(SSD_HEADS):
        emisc[MISC_DT + hd, hd * SSD_P:(hd + 1) * SSD_P] = 1.0
    idx = np.arange(CHUNK)
    tri = (idx[:, None] >= idx[None, :]).astype(np.float32)
    return tuple(jnp.asarray(m, BF16) for m in (emisc, emisc.T.copy(), tri, tri.T.copy()))


def _ssd_chunk_common(cc, misc, emisc, tri, trit, dtb, a_exp):
    sig = jax.nn.sigmoid(cc)
    xa = cc * sig
    dt = jax.nn.softplus(_dot01(misc, emisc) + dtb)
    a = dt * a_exp
    acs = _dot01(a, tri, left=True)
    acs_t = _dot01(a, trit, dot=_dot_tn)
    alast = acs[CHUNK - 1:CHUNK, :]
    return xa, sig, dt, acs, acs_t, alast


def _decay(acs, acs_t, hd):
    row = lax.broadcasted_iota(jnp.int32, (CHUNK, CHUNK), 0)
    col = lax.broadcasted_iota(jnp.int32, (CHUNK, CHUNK), 1)
    diff = acs[:, hd * SSD_P:hd * SSD_P + 1] - acs_t[hd * SSD_P:hd * SSD_P + 1, :]
    return jnp.exp(jnp.where(row >= col, diff, -jnp.inf))


def _half_mask(hh):
    lane = lax.broadcasted_iota(jnp.int32, (CHUNK, 2 * SSD_P), 1)
    return (lane >= SSD_P) if hh else (lane < SSD_P)


def _gate_norm(y, zz):
    sg = jax.nn.sigmoid(zz)
    yz = y * (zz * sg)
    outs, rs = [], []
    half = SSD_INNER // SSD_GROUPS
    for g in range(SSD_GROUPS):
        yg = yz[:, g * half:(g + 1) * half]
        r = lax.rsqrt(jnp.mean(yg * yg, axis=-1, keepdims=True) + EPS)
        outs.append(yg * r)
        rs.append(r)
    return sg, jnp.concatenate(outs, axis=1), rs


def _ssd_fwd(xraw, misc, z, cw, cb, dtb, a_exp, d_exp, nw, consts):
    s = xraw.shape[0]
    nb = s // SSD_ROWS
    ncb = SSD_ROWS // CHUNK
    emisc, _, tri, trit = consts

    def body(x_ref, misc_ref, z_ref, cw_ref, cb_ref, dtb_ref, a_ref, d_ref, nw_ref, emisc_ref, tri_ref, trit_ref,
             c_ref, prev_ref, ypre_ref, yssd_ref, tail_s, state_s):
        i = pl.program_id(0)

        @pl.when(i == 0)
        def _():
            tail_s[...] = jnp.zeros(tail_s.shape, F32)
            state_s[...] = jnp.zeros(state_s.shape, F32)

        x = x_ref[...]
        xext = jnp.concatenate([tail_s[...], x], axis=0)
        acc = x * cw_ref[CONV_W - 1:CONV_W, :] + cb_ref[...]
        for j in range(1, CONV_W):
            acc = acc + pltpu.roll(xext, j, 0)[8:, :] * cw_ref[CONV_W - 1 - j:CONV_W - j, :]
        tail_s[...] = x[SSD_ROWS - 8:, :]
        c_ref[...] = acc

        def chunk(ci):
            r0 = ci * CHUNK
            xa, _, dt, acs, acs_t, alast = _ssd_chunk_common(
                c_ref[pl.ds(r0, CHUNK), :], misc_ref[pl.ds(r0, CHUNK), :], emisc_ref[...], tri_ref[...], trit_ref[...],
                dtb_ref[...], a_ref[...])
            yield
            xs = xa[:, :SSD_INNER]
            xdt = xs * dt
            wgt = (xdt * jnp.exp(alast - acs)).astype(BF16)
            e = jnp.exp(acs)
            ys, new_states, cms = [], [], []
            for g in range(SSD_GROUPS):
                bm = xa[:, SSD_INNER + g * SSD_N:SSD_INNER + (g + 1) * SSD_N].astype(BF16)
                cm = xa[:, SSD_INNER + SSD_GROUPS * SSD_N + g * SSD_N:SSD_INNER + SSD_GROUPS * SSD_N + (g + 1) * SSD_N].astype(BF16)
                cms.append(cm)
                cb_g = _dot_nt(cm, bm)
                gl = slice(g * 256, (g + 1) * 256)
                new_states.append(_dot_tn(bm, wgt[:, gl]))
                for jj in range(2):
                    pair = 2 * g + jj
                    xp = xdt[:, pair * 128:(pair + 1) * 128]
                    yp = None
                    for hh in range(2):
                        sc = (cb_g * _decay(acs, acs_t, 2 * pair + hh)).astype(BF16)
                        term = _dot(sc, jnp.where(_half_mask(hh), xp, 0.0).astype(BF16))
                        yp = term if yp is None else yp + term
                    ys.append(yp)
                yield
            prev = state_s[...]
            prev_ref[ci] = prev
            yoff = jnp.concatenate([_dot(cms[g], prev[:, g * 256:(g + 1) * 256].astype(BF16)) for g in range(SSD_GROUPS)],
                                   axis=1) * e
            state_s[...] = prev * jnp.exp(alast) + jnp.concatenate(new_states, axis=1)
            yield
            y = jnp.concatenate(ys, axis=1) + yoff + d_ref[...] * xs
            ypre_ref[pl.ds(r0, CHUNK), :] = y
            _, yn, _ = _gate_norm(y, z_ref[pl.ds(r0, CHUNK), :])
            yssd_ref[pl.ds(r0, CHUNK), :] = (yn * nw_ref[...]).astype(BF16)

        _interleave([chunk(ci) for ci in range(ncb)])

    return pl.pallas_call(
        body, name="ssd_fwd", grid=(nb,),
        in_specs=[_rows(SSD_ROWS, CONV_DIM), _rows(SSD_ROWS, HEAD_PAD), _rows(SSD_ROWS, SSD_INNER),
                  _full((CONV_W, CONV_DIM)), _full((1, CONV_DIM)), _full((1, SSD_INNER)), _full((1, SSD_INNER)),
                  _full((1, SSD_INNER)), _full((1, SSD_INNER)), _full((HEAD_PAD, SSD_INNER)), _full((CHUNK, CHUNK)),
                  _full((CHUNK, CHUNK))],
        out_specs=[_rows(SSD_ROWS, CONV_DIM), pl.BlockSpec((ncb, SSD_N, SSD_INNER), lambda i: (i, 0, 0)),
                   _rows(SSD_ROWS, SSD_INNER), _rows(SSD_ROWS, SSD_INNER)],
        out_shape=[jax.ShapeDtypeStruct((s, CONV_DIM), F32), jax.ShapeDtypeStruct((s // CHUNK, SSD_N, SSD_INNER), F32),
                   jax.ShapeDtypeStruct((s, SSD_INNER), F32), jax.ShapeDtypeStruct((s, SSD_INNER), BF16)],
        scratch_shapes=[pltpu.VMEM((8, CONV_DIM), F32), pltpu.VMEM((SSD_N, SSD_INNER), F32)],
        compiler_params=_params(),
    )(xraw, misc, z, cw, cb, dtb, a_exp, d_exp, nw, emisc, tri, trit)


def _outproj_fwd(oe, yssd, wout, h, nw):
    s = h.shape[0]
    wide = MLA_HEADS * HEAD_PAD

    def body(oe_ref, y_ref, w_ref, h_ref, nw_ref, mixed_ref, h1_ref):
        mixed = _dot(oe_ref[...], w_ref[0:wide, :]) + _dot(y_ref[...], w_ref[wide:, :])
        mixed_ref[...] = mixed
        h1_ref[...] = h_ref[...] + _rms_fwd(mixed, nw_ref[...])

    return pl.pallas_call(
        body, name="outproj_fwd", grid=(s // TM,),
        in_specs=[_rows(TM, wide), _rows(TM, SSD_INNER), _resident((wide + SSD_INNER, D_MODEL)), _rows(TM, D_MODEL),
                  _full((1, D_MODEL))],
        out_specs=[_rows(TM, D_MODEL), _rows(TM, D_MODEL)],
        out_shape=[jax.ShapeDtypeStruct((s, D_MODEL), F32)] * 2,
        compiler_params=_params(),
    )(oe, yssd, wout, h, nw)


def _mlp_fwd(h1, prew, wup, wdown, postw, target=None):
    s = h1.shape[0]
    fb = D_FF // N_DEV
    last = target is not None

    def body(h_ref, prew_ref, up_ref, down_ref, postw_ref, *rest):
        target_ref, (mb_ref, ab_ref, d_ref, out_ref) = (rest[0] if last else None), rest[last:last + 4]
        hh = h_ref[...]
        mb = _rms_fwd(hh, prew_ref[...]).astype(BF16)
        mb_ref[...] = mb
        d = jnp.zeros((TM, D_MODEL), F32)
        for j in range(N_DEV):
            a = jnp.maximum(_dot(mb, up_ref[j]), 0.0)
            ab_ref[j] = a.astype(BF16)
            d = d + _dot(jnp.square(a).astype(BF16), down_ref[j])
        d_ref[...] = d
        h2 = hh + _rms_fwd(d, postw_ref[...])
        if last:
            diff = h2 - target_ref[...]
            out_ref[...] = diff * (1.0 / D_MODEL)
            part = 0.5 * jnp.sum(jnp.mean(diff * diff, axis=-1, keepdims=True), axis=0, keepdims=True)
            _acc_rows(rest[-1], part, pl.program_id(0) == 0)
        else:
            out_ref[...] = h2

    stacked = pl.BlockSpec((N_DEV, TM, fb), lambda i: (0, i, 0))
    return pl.pallas_call(
        body, name="mlp_fwd_loss" if last else "mlp_fwd", grid=(s // TM,),
        in_specs=[_rows(TM, D_MODEL), _full((1, D_MODEL)), _resident((N_DEV, D_MODEL, fb)), _resident((N_DEV, fb, D_MODEL)),
                  _full((1, D_MODEL))] + ([_rows(TM, D_MODEL)] if last else []),
        out_specs=[_rows(TM, D_MODEL), stacked, _rows(TM, D_MODEL), _rows(TM, D_MODEL)] + ([_full((1, 1))] if last else []),
        out_shape=[jax.ShapeDtypeStruct((s, D_MODEL), BF16), jax.ShapeDtypeStruct((N_DEV, s, fb), BF16),
                   jax.ShapeDtypeStruct((s, D_MODEL), F32), jax.ShapeDtypeStruct((s, D_MODEL), F32)]
        + ([jax.ShapeDtypeStruct((1, 1), F32)] if last else []),
        compiler_params=_params(),
    )(h1, prew, wup, wdown, postw, *([target] if last else []))


def _mlp_bwd(dh2, d, h1, ab, prew, wup, wdown, postw):
    s = dh2.shape[0]
    fb = D_FF // N_DEV
    tm = TM // 2

    def body(dh2_ref, d_ref, h1_ref, ab_ref, prew_ref, up_ref, down_ref, postw_ref,
             dh1_ref, da_ref, dd_ref, gpost_ref, gpre_ref):
        first = pl.program_id(0) == 0
        dh2 = dh2_ref[...]
        dd, gpost = _rms_bwd(d_ref[...], postw_ref[...], dh2)
        _acc_rows(gpost_ref, gpost, first)
        ddb = dd.astype(BF16)
        dd_ref[...] = ddb

        def d_relu_squared(j):
            return _dot_nt(ddb, down_ref[j])

        def pointwise(j, dr):
            da = (dr * (2.0 * ab_ref[j].astype(F32))).astype(BF16)
            da_ref[j] = da
            return da

        dm = jnp.zeros((tm, D_MODEL), F32)
        nxt, da_prev = d_relu_squared(0), None
        for j in range(N_DEV):
            cur = nxt
            if j + 1 < N_DEV:
                nxt = d_relu_squared(j + 1)
            da = pointwise(j, cur)
            if da_prev is not None:
                dm = dm + _dot_nt(da_prev, up_ref[j - 1])
            da_prev = da
        dm = dm + _dot_nt(da_prev, up_ref[N_DEV - 1])
        dx, gpre = _rms_bwd(h1_ref[...], prew_ref[...], dm)
        _acc_rows(gpre_ref, gpre, first)
        dh1_ref[...] = dh2 + dx

    stacked = pl.BlockSpec((N_DEV, tm, fb), lambda i: (0, i, 0))
    return pl.pallas_call(
        body, name="mlp_bwd", grid=(s // tm,),
        in_specs=[_rows(tm, D_MODEL)] * 3 + [stacked, _full((1, D_MODEL)), _resident((N_DEV, D_MODEL, fb)),
                                              _resident((N_DEV, fb, D_MODEL)), _full((1, D_MODEL))],
        out_specs=[_rows(tm, D_MODEL), stacked, _rows(tm, D_MODEL), _full((1, D_MODEL)), _full((1, D_MODEL))],
        out_shape=[jax.ShapeDtypeStruct((s, D_MODEL), F32), jax.ShapeDtypeStruct((N_DEV, s, fb), BF16),
                   jax.ShapeDtypeStruct((s, D_MODEL), BF16), jax.ShapeDtypeStruct((1, D_MODEL), F32),
                   jax.ShapeDtypeStruct((1, D_MODEL), F32)],
        compiler_params=_params(),
    )(dh2, d, h1, ab, prew, wup, wdown, postw)


def _matmul_tn(a, b, name, tk=TK_DW):
    s, m = a.shape
    n = b.shape[1]
    tn = n if n <= 1024 else (n // 2 if (n // 2) % 128 == 0 else n // 3)
    tk = min(tk, s)
    assert n % tn == 0 and tn % 128 == 0 and s % tk == 0

    def body(a_ref, b_ref, o_ref):
        part = _dot_tn(a_ref[...], b_ref[...])

        @pl.when(pl.program_id(1) == 0)
        def _():
            o_ref[...] = part

        @pl.when(pl.program_id(1) != 0)
        def _():
            o_ref[...] += part

    return pl.pallas_call(
        body, name=name, grid=(n // tn, s // tk),
        in_specs=[pl.BlockSpec((tk, m), lambda j, k: (k, 0)), pl.BlockSpec((tk, tn), lambda j, k: (k, j))],
        out_specs=pl.BlockSpec((m, tn), lambda j, k: (0, j)),
        out_shape=jax.ShapeDtypeStruct((m, n), F32),
        compiler_params=_params(),
    )(a, b)


def _matmul_tn_stacked(a, b, name, a_stacked, square_a=False, tk=TK_DW):
    tk = min(tk, a.shape[-2])
    if a_stacked:
        _, s, m = a.shape
        n = b.shape[1]
        in_specs = [pl.BlockSpec((1, tk, m), lambda j, k: (j, k, 0)), pl.BlockSpec((tk, n), lambda j, k: (k, 0))]
    else:
        s, m = a.shape
        n = b.shape[2]
        in_specs = [pl.BlockSpec((tk, m), lambda j, k: (k, 0)), pl.BlockSpec((1, tk, n), lambda j, k: (j, k, 0))]

    nk = s // tk

    def body(a_ref, b_ref, o_ref, acc_s):
        av = a_ref[0] if a_stacked else a_ref[...]
        bv = b_ref[...] if a_stacked else b_ref[0]
        if square_a:
            av = jnp.square(av.astype(F32)).astype(BF16)
        part = _dot_tn(av, bv)
        k = pl.program_id(1)

        @pl.when(k == 0)
        def _():
            acc_s[...] = part

        @pl.when(jnp.logical_and(k != 0, k != nk - 1))
        def _():
            acc_s[...] += part

        @pl.when(k == nk - 1)
        def _():
            o_ref[0] = (part if nk == 1 else acc_s[...] + part).astype(BF16)

    return pl.pallas_call(
        body, name=name, grid=(N_DEV, nk),
        in_specs=in_specs,
        out_specs=pl.BlockSpec((1, m, n), lambda j, k: (j, 0, 0)),
        out_shape=jax.ShapeDtypeStruct((N_DEV, m, n), BF16),
        scratch_shapes=[pltpu.VMEM((m, n), F32)],
        compiler_params=_params(),
    )(a, b)


def _outproj_bwd(dh1, mixed, nw, wout, oe):
    s = dh1.shape[0]
    wide = MLA_HEADS * HEAD_PAD

    def body(dh1_ref, mixed_ref, nw_ref, w_ref, oe_ref, dmix_ref, doe_ref, dy_ref, gnw_ref, delta_ref):
        dmix, gnw = _rms_bwd(mixed_ref[...], nw_ref[...], dh1_ref[...])
        _acc_rows(gnw_ref, gnw, pl.program_id(0) == 0)
        dmb = dmix.astype(BF16)
        dmix_ref[...] = dmb
        doe_ref[...] = _dot_nt(dmb, w_ref[0:wide, :]).astype(BF16)
        dy_ref[...] = _dot_nt(dmb, w_ref[wide:, :])
        ones = jnp.ones((8, HEAD_PAD), BF16)
        for hd in range(MLA_HEADS):
            cols = slice(hd * HEAD_PAD, (hd + 1) * HEAD_PAD)
            prod = oe_ref[:, cols].astype(F32) * doe_ref[:, cols].astype(F32)
            delta_ref[hd] = _dot01(prod, ones, dot=_dot_nt, left=True)

    return pl.pallas_call(
        body, name="outproj_bwd", grid=(s // TM,),
        in_specs=[_rows(TM, D_MODEL), _rows(TM, D_MODEL), _full((1, D_MODEL)), _resident((wide + SSD_INNER, D_MODEL)),
                  _rows(TM, wide)],
        out_specs=[_rows(TM, D_MODEL), _rows(TM, wide), _rows(TM, SSD_INNER), _full((1, D_MODEL)),
                   pl.BlockSpec((MLA_HEADS, 8, TM), lambda i: (0, 0, i))],
        out_shape=[jax.ShapeDtypeStruct((s, D_MODEL), BF16), jax.ShapeDtypeStruct((s, wide), BF16),
                   jax.ShapeDtypeStruct((s, SSD_INNER), F32), jax.ShapeDtypeStruct((1, D_MODEL), F32),
                   jax.ShapeDtypeStruct((MLA_HEADS, 8, s), F32)],
        compiler_params=_params(),
    )(dh1, mixed, nw, wout, oe)


def _attn_bwd(q, k, v, do, lse, delta, exchange=()):
    s = q.shape[0]
    t = ATT_T
    nq = s // t
    pair = 2 * HEAD_PAD
    ne = len(exchange)

    def body(q_ref, k_ref, v_ref, do_ref, lse_ref, delta_ref, *rest):
        e_in, (dq_ref, dk_ref, dv_ref), e_out = rest[:ne], rest[ne:ne + 3], rest[ne + 3:2 * ne + 3]
        dk_s, dv_s, bias_s = rest[2 * ne + 3:2 * ne + 6]
        kb = pl.program_id(1)
        _hosted_comm("exchange", e_in, e_out, rest[2 * ne + 6:],
                     jnp.logical_and(pl.program_id(0) == 0, kb == 0),
                     jnp.logical_and(pl.program_id(0) == MLA_HEADS // 2 - 1, kb == nq - 1))

        @pl.when(jnp.logical_and(pl.program_id(0) == 0, kb == 0))
        def _():
            bias_s[...] = _chunk_bias(t, keys_on_rows=True)

        @pl.when(kb == 0)
        def _():
            dq_ref[...] = jnp.zeros(dq_ref.shape, F32)

        def step(qb, diagonal):
            r0 = pl.multiple_of(qb * t, t)
            for hh in range(2):
                cols = slice(hh * HEAD_PAD, (hh + 1) * HEAD_PAD)
                kk = k_ref[:, cols]
                qq = q_ref[pl.ds(r0, t), cols]
                dd = do_ref[pl.ds(r0, t), cols]
                sc = _dot_nt(kk, qq) * ATT_SCALE_LOG2
                if diagonal:
                    sc = sc + bias_s[...]
                p = jnp.exp2(sc - lse_ref[hh, 0:1, pl.ds(r0, t)])
                dv = _dot(p.astype(BF16), dd)
                dp = _dot_nt(v_ref[:, cols], dd)
                ds = (p * (dp - delta_ref[hh, 0:1, pl.ds(r0, t)]) * ATT_SCALE).astype(BF16)
                dk = _dot(ds, qq)
                if diagonal:
                    dv_s[:, cols] = dv
                    dk_s[:, cols] = dk
                else:
                    dv_s[:, cols] += dv
                    dk_s[:, cols] += dk
                dq_ref[pl.ds(r0, t), cols] += _dot_tn(ds, kk)

        def loop(qb, c):
            step(qb, False)
            return c

        step(kb, True)
        lax.fori_loop(kb + 1, nq, loop, 0)
        dk_ref[...] = dk_s[...].astype(BF16)
        dv_ref[...] = dv_s[...].astype(BF16)

    whole = pl.BlockSpec((s, pair), lambda h, i: (0, h))
    tile = pl.BlockSpec((t, pair), lambda h, i: (i, h))
    rowvec = pl.BlockSpec((2, 8, s), lambda h, i: (h, 0, 0))
    wide = MLA_HEADS * HEAD_PAD
    outs = pl.pallas_call(
        body, name="attn_bwd_exchange" if ne else "attn_bwd", grid=(MLA_HEADS // 2, nq),
        in_specs=[whole, tile, tile, whole, rowvec, rowvec] + [_ANY] * ne,
        out_specs=[whole, tile, tile] + [_ANY] * ne,
        out_shape=[jax.ShapeDtypeStruct((s, wide), F32)] + [jax.ShapeDtypeStruct((s, wide), BF16)] * 2
        + _comm_out_shapes("exchange", exchange),
        scratch_shapes=[pltpu.VMEM((t, pair), F32), pltpu.VMEM((t, pair), F32), pltpu.VMEM((t, t), F32)]
        + (_comm_scratch(ne) if ne else []),
        compiler_params=_params(),
    )(q, k, v, do, lse, delta, *exchange)
    return outs[0], outs[1], outs[2], list(outs[3:])


def _ssd_bwd(dy, ypre, z, c, xraw, misc, prev, cw, dtb, a_exp, d_exp, nw, consts):
    s = dy.shape[0]
    nb = s // SSD_ROWS
    ncb = SSD_ROWS // CHUNK
    emisc, emisc_t, tri, trit = consts

    def body(dy_ref, ypre_ref, z_ref, c_ref, x_ref, misc_ref, prev_ref, cw_ref, dtb_ref, a_ref, d_ref, nw_ref,
             emisc_ref, emisct_ref, tri_ref, trit_ref,
             dz_ref, dx_ref, dmisc_ref, gnw_ref, gd_ref, galog_ref, gdtb_ref, gcw_ref, gcb_ref,
             dst_s, dc_s, head_s):
        i = pl.program_id(0)
        first = i == 0

        @pl.when(first)
        def _():
            dst_s[...] = jnp.zeros(dst_s.shape, F32)
            head_s[...] = jnp.zeros(head_s.shape, F32)
            gnw_ref[...] = jnp.zeros(gnw_ref.shape, F32)
            gd_ref[...] = jnp.zeros(gd_ref.shape, F32)
            galog_ref[...] = jnp.zeros(galog_ref.shape, F32)
            gdtb_ref[...] = jnp.zeros(gdtb_ref.shape, F32)

        a_exp_v = a_ref[...]
        a8 = _dot01(a_exp_v, emisct_ref[...]) * (1.0 / SSD_P)

        def chunk(ci):
            r0 = ci * CHUNK
            cc = c_ref[pl.ds(r0, CHUNK), :]
            mm = misc_ref[pl.ds(r0, CHUNK), :]
            xa, sig_c, dt, acs, acs_t, alast = _ssd_chunk_common(cc, mm, emisc_ref[...], tri_ref[...], trit_ref[...],
                                                              dtb_ref[...], a_exp_v)
            yield
            xs = xa[:, :SSD_INNER]
            xdt = xs * dt
            y = ypre_ref[pl.ds(r0, CHUNK), :]
            zz = z_ref[pl.ds(r0, CHUNK), :]
            sg, yn, rs = _gate_norm(y, zz)
            dyo = dy_ref[pl.ds(r0, CHUNK), :]
            gnw_ref[...] += jnp.sum(dyo * yn, axis=0, keepdims=True)
            dyn = dyo * nw_ref[...]
            half = SSD_INNER // SSD_GROUPS
            dyz_parts = []
            for g in range(SSD_GROUPS):
                gl = slice(g * half, (g + 1) * half)
                dyz_parts.append(rs[g] * (dyn[:, gl] - yn[:, gl] * jnp.mean(dyn[:, gl] * yn[:, gl], axis=-1, keepdims=True)))
            dyz = jnp.concatenate(dyz_parts, axis=1)
            dz_ref[pl.ds(r0, CHUNK), :] = dyz * y * (sg * (1.0 + zz * (1.0 - sg)))
            dyp = dyz * (zz * sg)
            dypb = dyp.astype(BF16)
            gd_ref[...] += jnp.sum(dyp * xs, axis=0, keepdims=True)
            yield
            prev = prev_ref[ci]
            cd = jnp.exp(alast)
            e = jnp.exp(acs)
            dsx = jnp.exp(alast - acs)
            wgt = (xdt * dsx).astype(BF16)
            dze = (dyp * e).astype(BF16)
            dprev_parts, diag_all, dbm, dcm, yoff_parts, bms = [], [], [], [], [], []
            lane8 = lax.broadcasted_iota(jnp.int32, (CHUNK, HEAD_PAD), 1)
            diag8 = jnp.zeros((CHUNK, HEAD_PAD), F32)
            for g in range(SSD_GROUPS):
                gl = slice(g * 256, (g + 1) * 256)
                bm = xa[:, SSD_INNER + g * SSD_N:SSD_INNER + (g + 1) * SSD_N].astype(BF16)
                cm = xa[:, SSD_INNER + SSD_GROUPS * SSD_N + g * SSD_N:SSD_INNER + SSD_GROUPS * SSD_N + (g + 1) * SSD_N].astype(BF16)
                bms.append(bm)
                prev_g = prev[:, gl].astype(BF16)
                dcm_g = _dot_nt(dze[:, gl], prev_g)
                dprev_parts.append(_dot_tn(cm, dze[:, gl]))
                cb_g = _dot_nt(cm, bm)
                dcb = jnp.zeros((CHUNK, CHUNK), F32)
                diag_parts = []
                for jj in range(2):
                    pair = 2 * g + jj
                    pl_ = slice(pair * 128, (pair + 1) * 128)
                    xp = xdt[:, pl_]
                    dyp_p = dypb[:, pl_]
                    dxp = jnp.zeros((CHUNK, 128), F32)
                    for hh in range(2):
                        hd = 2 * pair + hh
                        dec = _decay(acs, acs_t, hd)
                        xm = jnp.where(_half_mask(hh), xp, 0.0).astype(BF16)
                        dsc = _dot_nt(dyp_p, xm) * dec
                        dcb = dcb + dsc
                        sc = (cb_g * dec).astype(BF16)
                        dxp = dxp + jnp.where(_half_mask(hh), _dot_tn(sc, dyp_p), 0.0)
                        dm = dsc * cb_g
                        diag8 = diag8 + jnp.where(lane8 == MISC_DT + hd, jnp.sum(dm - dm.T, axis=1, keepdims=True), 0.0)
                    diag_parts.append(dxp)
                dcbb = dcb.astype(BF16)
                dcm.append(dcm_g + _dot(dcbb, bm))
                dbm.append(_dot_tn(dcbb, cm))
                diag_all.append(jnp.concatenate(diag_parts, axis=1))
                yoff_parts.append(_dot(cm, prev_g) * e[:, gl])
                yield
            dst = dst_s[...]
            glast = jnp.sum(dst * prev, axis=0, keepdims=True) * cd
            dxdt_state_parts = []
            for g in range(SSD_GROUPS):
                gl = slice(g * 256, (g + 1) * 256)
                dst_g = dst[:, gl].astype(BF16)
                dxdt_state_parts.append(_dot(bms[g], dst_g) * dsx[:, gl])
                dbm[g] = dbm[g] + _dot_nt(wgt[:, gl], dst_g)
            dst_s[...] = dst * cd + jnp.concatenate(dprev_parts, axis=1)
            yield
            dxdt_state = jnp.concatenate(dxdt_state_parts, axis=1)
            dxdt = jnp.concatenate(diag_all, axis=1) + dxdt_state
            dacs = dyp * jnp.concatenate(yoff_parts, axis=1) - xdt * dxdt_state
            last = jnp.sum(xdt * dxdt_state, axis=0, keepdims=True) + glast
            row = lax.broadcasted_iota(jnp.int32, (CHUNK, SSD_INNER), 0)
            dacs = dacs + jnp.where(row == CHUNK - 1, last, 0.0)
            dacs8 = _dot01(dacs, emisct_ref[...]) + diag8
            da8 = _dot01(dacs8, trit_ref[...], left=True)
            ddt8 = da8 * a8 + _dot01(dxdt * xs, emisct_ref[...])
            yield
            dtr8 = mm + _dot01(dtb_ref[...], emisct_ref[...]) * (1.0 / SSD_P)
            dt8 = jax.nn.softplus(dtr8)
            lane = lax.broadcasted_iota(jnp.int32, (CHUNK, HEAD_PAD), 1)
            on_dt = jnp.logical_and(lane >= MISC_DT, lane < MISC_DT + SSD_HEADS)
            ddtr8 = jnp.where(on_dt, ddt8 * jax.nn.sigmoid(dtr8), 0.0)
            dmisc_ref[pl.ds(r0, CHUNK), :] = ddtr8
            gdtb_ref[...] += jnp.sum(ddtr8, axis=0, keepdims=True)
            galog_ref[...] += jnp.sum(jnp.where(on_dt, da8 * dt8, 0.0), axis=0, keepdims=True) * a8
            dxs = d_ref[...] * dyp + dxdt * dt
            dxa = jnp.concatenate([dxs] + dbm + dcm, axis=1)
            dc_s[pl.ds(r0, CHUNK), :] = dxa * (sig_c * (1.0 + cc * (1.0 - sig_c)))

        _interleave([chunk(ci) for ci in reversed(range(ncb))])

        dc = dc_s[...]
        x = x_ref[...]
        dcext = jnp.concatenate([dc, head_s[...]], axis=0)
        dx = dc * cw_ref[CONV_W - 1:CONV_W, :]
        rows = [jnp.sum(dc * x, axis=0, keepdims=True)]
        for j in range(1, CONV_W):
            ahead = pltpu.roll(dcext, SSD_ROWS + 8 - j, 0)[:SSD_ROWS, :]
            dx = dx + ahead * cw_ref[CONV_W - 1 - j:CONV_W - j, :]
            rows.insert(0, jnp.sum(ahead * x, axis=0, keepdims=True))
        dx_ref[...] = dx
        head_s[...] = dc[:8, :]
        gcw = jnp.concatenate(rows, axis=0)

        @pl.when(first)
        def _():
            gcw_ref[...] = gcw
            gcb_ref[...] = jnp.sum(dc, axis=0, keepdims=True)

        @pl.when(jnp.logical_not(first))
        def _():
            gcw_ref[...] += gcw
            gcb_ref[...] += jnp.sum(dc, axis=0, keepdims=True)

    def rev(width):
        return pl.BlockSpec((SSD_ROWS, width), lambda i: (nb - 1 - i, 0))

    return pl.pallas_call(
        body, name="ssd_bwd", grid=(nb,),
        in_specs=[rev(SSD_INNER), rev(SSD_INNER), rev(SSD_INNER), rev(CONV_DIM), rev(CONV_DIM),
                  rev(HEAD_PAD), pl.BlockSpec((ncb, SSD_N, SSD_INNER), lambda i: (nb - 1 - i, 0, 0)),
                  _full((CONV_W, CONV_DIM)), _full((1, SSD_INNER)), _full((1, SSD_INNER)), _full((1, SSD_INNER)),
                  _full((1, SSD_INNER)), _full((HEAD_PAD, SSD_INNER)), _full((SSD_INNER, HEAD_PAD)), _full((CHUNK, CHUNK)),
                  _full((CHUNK, CHUNK))],
        out_specs=[rev(SSD_INNER), rev(CONV_DIM), rev(HEAD_PAD), _full((1, SSD_INNER)), _full((1, SSD_INNER)),
                   _full((1, HEAD_PAD)), _full((1, HEAD_PAD)), _full((CONV_W, CONV_DIM)), _full((1, CONV_DIM))],
        out_shape=[jax.ShapeDtypeStruct((s, SSD_INNER), F32), jax.ShapeDtypeStruct((s, CONV_DIM), F32),
                   jax.ShapeDtypeStruct((s, HEAD_PAD), F32), jax.ShapeDtypeStruct((1, SSD_INNER), F32),
                   jax.ShapeDtypeStruct((1, SSD_INNER), F32), jax.ShapeDtypeStruct((1, HEAD_PAD), F32),
                   jax.ShapeDtypeStruct((1, HEAD_PAD), F32), jax.ShapeDtypeStruct((CONV_W, CONV_DIM), F32),
                   jax.ShapeDtypeStruct((1, CONV_DIM), F32)],
        scratch_shapes=[pltpu.VMEM((SSD_N, SSD_INNER), F32), pltpu.VMEM((SSD_ROWS, CONV_DIM), F32), pltpu.VMEM((8, CONV_DIM), F32)],
        compiler_params=_params(),
    )(dy, ypre, z, c, xraw, misc, prev, cw, dtb, a_exp, d_exp, nw, emisc, emisc_t, tri, trit)


def _qkv_bwd(dq, dk, dv, cq, ckv, qnw, kvnw, wuq, wkv, cosf, sinf):
    s = dq.shape[0]
    wide = MLA_HEADS * HEAD_PAD

    def body(dq_ref, dk_ref, dv_ref, cq_ref, ckv_ref, qnw_ref, kvnw_ref, wuq_ref, wkv_ref, cos_ref, sin_ref,
             dqb_ref, dkvb_ref, dcq_ref, dckv_ref, dmisc_ref, gq_ref, gkv_ref):
        first = pl.program_id(0) == 0
        cosf, sinf = cos_ref[...], sin_ref[...]
        dkr = jnp.zeros((TM, HEAD_PAD), F32)
        for hd in range(MLA_HEADS):
            cols = slice(hd * HEAD_PAD, (hd + 1) * HEAD_PAD)
            dqb_ref[:, cols] = _rope(dq_ref[:, cols], cosf, sinf, -1.0).astype(BF16)
            dkh = dk_ref[:, cols]
            dkvb_ref[:, cols] = dkh.astype(BF16)
            dkr = dkr + dkh
        dkvb_ref[:, wide:] = dv_ref[...].astype(BF16)
        lane = lax.broadcasted_iota(jnp.int32, dkr.shape, 1)
        in_rope = jnp.logical_and(lane >= MISC_ROPE, lane < MISC_ROPE + QK_ROPE)
        dmisc_ref[...] = jnp.where(in_rope, _rope(jnp.where(in_rope, dkr, 0.0), cosf, sinf, -1.0), 0.0)
        dcq, gq = _rms_bwd(cq_ref[...], qnw_ref[...], _dot_nt(dqb_ref[...], wuq_ref[...]))
        dcq_ref[...] = dcq
        _acc_rows(gq_ref, gq, first)
        dckv, gkv = _rms_bwd(ckv_ref[...], kvnw_ref[...], _dot_nt(dkvb_ref[...], wkv_ref[...]))
        dckv_ref[...] = dckv
        _acc_rows(gkv_ref, gkv, first)

    return pl.pallas_call(
        body, name="qkv_bwd", grid=(s // TM,),
        in_specs=[_rows(TM, wide)] * 3 + [_rows(TM, Q_RANK), _rows(TM, KV_RANK), _full((1, Q_RANK)), _full((1, KV_RANK)),
                                          _resident((Q_RANK, wide)), _resident((KV_RANK, 2 * wide)), _rows(TM, HEAD_PAD), _rows(TM, HEAD_PAD)],
        out_specs=[_rows(TM, wide), _rows(TM, 2 * wide), _rows(TM, Q_RANK), _rows(TM, KV_RANK), _rows(TM, HEAD_PAD),
                   _full((1, Q_RANK)), _full((1, KV_RANK))],
        out_shape=[jax.ShapeDtypeStruct((s, wide), BF16), jax.ShapeDtypeStruct((s, 2 * wide), BF16),
                   jax.ShapeDtypeStruct((s, Q_RANK), F32), jax.ShapeDtypeStruct((s, KV_RANK), F32),
                   jax.ShapeDtypeStruct((s, HEAD_PAD), F32), jax.ShapeDtypeStruct((1, Q_RANK), F32),
                   jax.ShapeDtypeStruct((1, KV_RANK), F32)],
        compiler_params=_params(),
    )(dq, dk, dv, cq, ckv, qnw, kvnw, wuq, wkv, cosf, sinf)


def _inproj_bwd(dcq, dckv, dmisc_rope, dmisc_dt, dz, dxbc, h, dh1, nw, win):
    s = h.shape[0]

    def body(dcq_ref, dckv_ref, dma_ref, dmb_ref, dz_ref, dxbc_ref, h_ref, dh1_ref, nw_ref, w_ref, dproj_ref, dh0_ref, gnw_ref):
        dproj_ref[:, 0:768] = dcq_ref[...].astype(BF16)
        dproj_ref[:, 768:1024] = dckv_ref[...].astype(BF16)
        dproj_ref[:, 1024:1152] = (dma_ref[...] + dmb_ref[...]).astype(BF16)
        dproj_ref[:, 1152:1664] = dz_ref[...].astype(BF16)
        dproj_ref[:, 1664:2688] = dxbc_ref[...].astype(BF16)
        du = _dot_nt(dproj_ref[...], w_ref[...])
        dx, gnw = _rms_bwd(h_ref[...], nw_ref[...], du)
        _acc_rows(gnw_ref, gnw, pl.program_id(0) == 0)
        dh0_ref[...] = dh1_ref[...] + dx

    return pl.pallas_call(
        body, name="inproj_bwd", grid=(s // TM,),
        in_specs=[_rows(TM, Q_RANK), _rows(TM, KV_RANK), _rows(TM, HEAD_PAD), _rows(TM, HEAD_PAD), _rows(TM, SSD_INNER),
                  _rows(TM, CONV_DIM), _rows(TM, D_MODEL), _rows(TM, D_MODEL), _full((1, D_MODEL)), _resident((D_MODEL, IN_PAD))],
        out_specs=[_rows(TM, IN_PAD), _rows(TM, D_MODEL), _full((1, D_MODEL))],
        out_shape=[jax.ShapeDtypeStruct((s, IN_PAD), BF16), jax.ShapeDtypeStruct((s, D_MODEL), F32),
                   jax.ShapeDtypeStruct((1, D_MODEL), F32)],
        compiler_params=_params(),
    )(dcq, dckv, dmisc_rope, dmisc_dt, dz, dxbc, h, dh1, nw, win)


def _row_tile(rows, cols):
    cap = max(8, (1 << 18) // max(cols, 128))
    best = None
    for t in range(8, rows + 1, 8):
        if rows % t == 0 and t <= cap:
            best = t
    return best if best is not None else rows


def _adamw(w, g, m, v, name):
    rows, cols = w.shape
    tr = _row_tile(rows, cols)

    def body(w_ref, g_ref, m_ref, v_ref, d_ref, m2_ref, v2_ref):
        gg = g_ref[...]
        m2 = ADAM_B1 * m_ref[...] + (1.0 - ADAM_B1) * gg
        v2 = ADAM_B2 * v_ref[...] + (1.0 - ADAM_B2) * jnp.square(gg)
        m_hat = m2 / (1.0 - ADAM_B1 ** ADAM_STEP)
        v_hat = v2 / (1.0 - ADAM_B2 ** ADAM_STEP)
        d_ref[...] = -ADAM_LR * (m_hat / (jnp.sqrt(v_hat) + ADAM_EPS) + ADAM_WD * w_ref[...])
        m2_ref[...] = m2
        v2_ref[...] = v2

    spec = pl.BlockSpec((tr, cols), lambda i: (i, 0))
    return pl.pallas_call(
        body, name=name, grid=(rows // tr,),
        in_specs=[spec] * 4, out_specs=[spec] * 3,
        out_shape=[jax.ShapeDtypeStruct((rows, cols), F32)] * 3,
    )(w, g, m, v)


def _sum_adamw(slots, w, m, v, name):
    _, rows, cols = w.shape
    tr = _row_tile(rows, cols)
    nb = rows // tr

    def body(s0_ref, s1_ref, w_ref, m_ref, v_ref, g_ref, d_ref, m2_ref, v2_ref):
        for l, ref in enumerate((s0_ref, s1_ref)):
            @pl.when(pl.program_id(0) == l)
            def _(ref=ref):
                acc = ref[0].astype(F32)
                for i in range(1, N_DEV):
                    acc = acc + ref[i].astype(F32)
                g_ref[...] = acc

        gg = g_ref[...]
        m2 = ADAM_B1 * m_ref[...] + (1.0 - ADAM_B1) * gg
        v2 = ADAM_B2 * v_ref[...] + (1.0 - ADAM_B2) * jnp.square(gg)
        m_hat = m2 / (1.0 - ADAM_B1 ** ADAM_STEP)
        v_hat = v2 / (1.0 - ADAM_B2 ** ADAM_STEP)
        d_ref[...] = -ADAM_LR * (m_hat / (jnp.sqrt(v_hat) + ADAM_EPS) + ADAM_WD * w_ref[...])
        m2_ref[...] = m2
        v2_ref[...] = v2

    slot_spec = lambda layer: pl.BlockSpec((N_DEV, tr, cols), lambda l, i: (0, jnp.where(l == layer, i, (nb - 1) * (1 - layer)), 0))
    spec = pl.BlockSpec((None, tr, cols), lambda l, i: (l, i, 0))
    return pl.pallas_call(
        body, name=name, grid=(DEPTH, nb),
        in_specs=[slot_spec(0), slot_spec(1), spec, spec, spec], out_specs=[spec] * 4,
        out_shape=[jax.ShapeDtypeStruct(w.shape, F32)] * 4,
        compiler_params=_params(),
    )(slots[0], slots[1], w, m, v)


_MESH = pl.DeviceIdType.MESH
_ANY = pl.BlockSpec(memory_space=pl.ANY)


def _my_place():
    return lax.axis_index("x"), lax.axis_index("y"), lax.axis_index("c")


def _flip(place, k):
    x, y, c = place
    return (1 - x if k & 4 else x, 1 - y if k & 2 else y, 1 - c if k & 1 else c)


def _block_id(place):
    return 4 * place[0] + 2 * place[1] + place[2]


def _peer_copies(kind, in_refs, out_refs, send_sems, recv_sems, local_sems):
    me = _my_place()
    my = _block_id(me)
    remote, local = [], []
    for a, (x_ref, out_ref) in enumerate(zip(in_refs, out_refs)):
        src_of = (lambda place, r=x_ref: r) if kind == "gather" else (lambda place, r=x_ref: r.at[_block_id(place)])
        local.append(pltpu.make_async_copy(src_of(me), out_ref.at[my], local_sems.at[a]))
        for k in range(1, N_DEV):
            peer = _flip(me, k)
            remote.append(pltpu.make_async_remote_copy(
                src_ref=src_of(peer), dst_ref=out_ref.at[my], send_sem=send_sems.at[a * 7 + k - 1],
                recv_sem=recv_sems.at[a * 7 + k - 1], device_id=peer, device_id_type=_MESH))
    return remote, local


def _comm_out_shapes(kind, arrays):
    return [jax.ShapeDtypeStruct((N_DEV, *a.shape) if kind == "gather" else a.shape, a.dtype) for a in arrays]


def _comm_scratch(n):
    return [pltpu.SemaphoreType.DMA((7 * n,)), pltpu.SemaphoreType.DMA((7 * n,)), pltpu.SemaphoreType.DMA((n,))]


def _hosted_comm(kind, in_refs, out_refs, sems, first, last):
    if not in_refs:
        return

    @pl.when(first)
    def _():
        remote, local = _peer_copies(kind, in_refs, out_refs, *sems)
        for cp in local + remote:
            cp.start()

    @pl.when(last)
    def _():
        remote, local = _peer_copies(kind, in_refs, out_refs, *sems)
        for cp in remote:
            cp.wait()
        for cp in local:
            cp.wait()


def _two_level_gather_steps(in_refs, out_refs, send_sems, recv_sems, local_sems):
    n = len(in_refs)
    me = _my_place()
    x, y, c = me
    sibling = (x, y, 1 - c)
    chips = [(1 - x, y), (x, 1 - y), (1 - x, 1 - y)]

    def copy(a, k, place, to, src=None):
        block = out_refs[a].at[_block_id(place)]
        return pltpu.make_async_remote_copy(
            src_ref=block if src is None else src, dst_ref=block, send_sem=send_sems.at[7 * a + k],
            recv_sem=recv_sems.at[7 * a + k], device_id=to, device_id_type=_MESH)

    mine = [pltpu.make_async_copy(in_refs[a], out_refs[a].at[_block_id(me)], local_sems.at[a]) for a in range(n)]
    first = [copy(a, 0, me, sibling, src=in_refs[a]) for a in range(n)]
    first += [copy(a, 1 + j, me, (*chip, c), src=in_refs[a]) for a in range(n) for j, chip in enumerate(chips)]
    passed = [copy(a, 4 + j, (*chip, c), sibling) for a in range(n) for j, chip in enumerate(chips)]

    def send():
        for cp in mine + first:
            cp.start()

    def forward():
        for a in range(n):
            for j, chip in enumerate(chips):
                copy(a, 1 + j, (*chip, c), me).wait_recv()
                passed[3 * a + j].start()

    def finish():
        for a in range(n):
            copy(a, 0, sibling, me).wait_recv()
            for j, chip in enumerate(chips):
                copy(a, 4 + j, (*chip, 1 - c), me).wait_recv()
        for cp in first + passed:
            cp.wait_send()
        for cp in mine:
            cp.wait()

    return send, forward, finish


def _gather_two_level(arrays, name):
    n = len(arrays)

    def body(*refs):
        for step in _two_level_gather_steps(refs[:n], refs[n:2 * n], *refs[2 * n:]):
            step()

    return pl.pallas_call(
        body, name=name, out_shape=_comm_out_shapes("gather", arrays),
        in_specs=[_ANY] * n, out_specs=[_ANY] * n, scratch_shapes=_comm_scratch(n),
    )(*arrays)


def _hosted_gather(in_refs, out_refs, sems, first, middle, last):
    if not in_refs:
        return
    for when, index in ((first, 0), (middle, 1), (last, 2)):
        @pl.when(when)
        def _(index=index):
            _two_level_gather_steps(in_refs, out_refs, *sems)[index]()


def _comm(kind, arrays, name):
    n = len(arrays)

    def body(*refs):
        remote, local = _peer_copies(kind, refs[:n], refs[n:2 * n], *refs[2 * n:])
        for cp in local + remote:
            cp.start()
        for cp in remote:
            cp.wait()
        for cp in local:
            cp.wait()

    return pl.pallas_call(
        body, name=name, out_shape=_comm_out_shapes(kind, arrays),
        in_specs=[_ANY] * n, out_specs=[_ANY] * n, scratch_shapes=_comm_scratch(n),
    )(*arrays)


def _all_reduce_small(part):
    rows, lanes = part.shape
    vmem = pl.BlockSpec(memory_space=pltpu.VMEM)

    def body(x_ref, gath_ref, sum_ref, send_sems, recv_sems):
        me = _my_place()
        my = _block_id(me)
        gath_ref[my] = x_ref[...]
        copies = []
        for k in range(1, N_DEV):
            cp = pltpu.make_async_remote_copy(
                src_ref=x_ref, dst_ref=gath_ref.at[my], send_sem=send_sems.at[k - 1], recv_sem=recv_sems.at[k - 1],
                device_id=_flip(me, k), device_id_type=_MESH)
            cp.start()
            copies.append(cp)
        for cp in copies:
            cp.wait()
        acc = gath_ref[0]
        for i in range(1, N_DEV):
            acc = acc + gath_ref[i]
        sum_ref[...] = acc

    return pl.pallas_call(
        body, name="small_grad_all_reduce",
        out_shape=[jax.ShapeDtypeStruct((N_DEV, rows, lanes), F32), jax.ShapeDtypeStruct((rows, lanes), F32)],
        in_specs=[vmem], out_specs=[vmem, vmem],
        scratch_shapes=[pltpu.SemaphoreType.DMA((7,)), pltpu.SemaphoreType.DMA((7,))],
    )(part)[1]


_SHARDED = (("w_in", (D_MODEL, IN_PROJ // N_DEV)), ("w_uq", (Q_RANK // N_DEV, Q_RANK)), ("w_ukv", (KV_RANK, HEAD_PAD)),
            ("conv_w", (CONV_W, CONV_DIM // N_DEV)), ("w_out", (D_MODEL // N_DEV, D_MODEL)),
            ("w_up", (D_MODEL, D_FF // N_DEV)), ("w_down", (D_FF // N_DEV, D_MODEL)))
_SMALL = (("pre_mix_norm", D_MODEL), ("q_norm", Q_RANK), ("kv_norm", KV_RANK), ("conv_b", CONV_DIM), ("dt_bias", SSD_HEADS),
          ("a_log", SSD_HEADS), ("d_skip", SSD_HEADS), ("ssd_norm", SSD_INNER), ("post_mix_norm", D_MODEL),
          ("pre_mlp_norm", D_MODEL), ("post_mlp_norm", D_MODEL))
_WEIGHT_ORDER = ("pre_mix_norm", "w_in", "q_norm", "w_uq", "kv_norm", "w_ukv", "conv_w", "conv_b", "dt_bias", "a_log", "d_skip",
                 "ssd_norm", "w_out", "post_mix_norm", "pre_mlp_norm", "w_up", "w_down", "post_mlp_norm")
_EARLY = ("w_in", "w_uq", "w_ukv", "conv_w")
_LATE = ("w_out", "w_up", "w_down")


def _wire_shard(name, a):
    return lax.bitcast_convert_type(a, BF16).reshape(CONV_W, -1) if name == "conv_w" else a.astype(BF16)


def _from_wire(name, g):
    return lax.bitcast_convert_type(g.reshape(N_DEV, CONV_W, -1, 2), F32) if name == "conv_w" else g


def _cols(stacked):
    return jnp.transpose(stacked, (1, 0, 2)).reshape(stacked.shape[1], -1)


def _early_weights(sh):
    w_in = _cols(sh["w_in"])
    zeros = lambda n: jnp.zeros((D_MODEL, n), BF16)
    s1, s2, s3, s4, s5 = 768, 1024, 1056, 1568, 2592
    win = jnp.concatenate([w_in[:, :s2], zeros(MISC_ROPE), w_in[:, s2:s3], w_in[:, s5:], zeros(HEAD_PAD - MISC_DT - SSD_HEADS),
                           w_in[:, s3:s5]], axis=1)
    w_uq = sh["w_uq"].reshape(Q_RANK, MLA_HEADS, QK_NOPE + QK_ROPE)
    wuq = jnp.pad(w_uq, ((0, 0), (0, 0), (0, HEAD_PAD - QK_NOPE - QK_ROPE))).reshape(Q_RANK, -1)
    w_ukv = _cols(sh["w_ukv"]).reshape(KV_RANK, MLA_HEADS, QK_NOPE + V_DIM)
    wkn = jnp.pad(w_ukv[..., :QK_NOPE], ((0, 0), (0, 0), (0, HEAD_PAD - QK_NOPE))).reshape(KV_RANK, -1)
    wv = w_ukv[..., QK_NOPE:].reshape(KV_RANK, 4, 2, 1, V_DIM) * jnp.eye(2, dtype=BF16).reshape(1, 1, 2, 2, 1)
    wkv = jnp.concatenate([wkn, wv.reshape(KV_RANK, -1)], axis=1)
    return dict(win=win, wuq=wuq, wkv=wkv, conv_w=_cols(sh["conv_w"]))


def _late_weights(sh):
    w_out = sh["w_out"].reshape(D_MODEL, D_MODEL)
    watt = w_out[:SSD_INNER].reshape(4, 2, 1, V_DIM, D_MODEL) * jnp.eye(2, dtype=BF16).reshape(1, 2, 2, 1, 1)
    wout = jnp.concatenate([watt.reshape(MLA_HEADS * HEAD_PAD, D_MODEL), w_out[SSD_INNER:]], axis=0)
    return dict(wout=wout, wup=sh["w_up"], wdown=sh["w_down"])


def _shard_grads(g):
    out = {}
    if "wup" in g:
        out["w_up"], out["w_down"] = g["wup"], g["wdown"]
        ae = g["wout_att"].reshape(4, 2, 2, V_DIM, D_MODEL)
        att = jnp.stack([ae[:, 0, 0], ae[:, 1, 1]], axis=1).reshape(SSD_INNER, D_MODEL)
        out["w_out"] = jnp.concatenate([att, g["wout_ssd"]], axis=0).astype(BF16).reshape(N_DEV, D_MODEL // N_DEV, D_MODEL)
    if "win" not in g:
        return out
    dwin = g["win"]
    s2 = Q_RANK + KV_RANK
    m0 = s2
    w_in = jnp.concatenate([dwin[:, :s2], dwin[:, m0 + MISC_ROPE:m0 + MISC_ROPE + QK_ROPE], dwin[:, 1152:2688],
                            dwin[:, m0 + MISC_DT:m0 + MISC_DT + SSD_HEADS]], axis=1)
    out["w_in"] = jnp.transpose(w_in.astype(BF16).reshape(D_MODEL, N_DEV, -1), (1, 0, 2))
    w_uq = g["wuq"].astype(BF16).reshape(Q_RANK, MLA_HEADS, HEAD_PAD)[..., :QK_NOPE + QK_ROPE].reshape(Q_RANK, Q_RANK)
    out["w_uq"] = w_uq.reshape(N_DEV, Q_RANK // N_DEV, Q_RANK)
    wide = MLA_HEADS * HEAD_PAD
    wkv = g["wkv"].astype(BF16)
    kn = wkv[:, :wide].reshape(KV_RANK, MLA_HEADS, HEAD_PAD)[..., :QK_NOPE]
    ve = wkv[:, wide:].reshape(KV_RANK, 4, 2, 2, V_DIM)
    vv = jnp.stack([ve[:, :, 0, 0], ve[:, :, 1, 1]], axis=2).reshape(KV_RANK, MLA_HEADS, V_DIM)
    out["w_ukv"] = jnp.transpose(jnp.concatenate([kn, vv], axis=-1), (1, 0, 2))
    out["conv_w"] = jnp.transpose(g["conv_w"].astype(BF16).reshape(CONV_W, N_DEV, -1), (1, 0, 2))
    return out


def _small_rows(n):
    return -(-n // 1024) * 8


def _pack_small(vals):
    rows = []
    for l in range(DEPTH):
        for name, n in _SMALL:
            r = _small_rows(n)
            rows.append(jnp.pad(vals[name][l].reshape(-1), (0, r * 128 - n)).reshape(r, 128))
    return jnp.concatenate(rows, axis=0)


def _unpack_small(packed):
    out, off = {name: [] for name, _ in _SMALL}, 0
    for l in range(DEPTH):
        for name, n in _SMALL:
            r = _small_rows(n)
            out[name].append(packed[off:off + r].reshape(-1)[:n])
            off += r
    return {name: jnp.stack(v) for name, v in out.items()}


def _lane_rows(vec8):
    return jnp.repeat(vec8, SSD_P).reshape(1, SSD_INNER)


def _layer_fwd(h, kw, sm, l, cosf, sinf, consts, gather=(), after_gather=None, target=None):
    row = lambda name: sm[name][l].reshape(1, -1)
    t = {}
    t["h0"] = h
    t["ub"], t["cq"], t["ckv"], t["misc"], t["z"], t["xraw"] = _inproj_fwd(h, row("pre_mix_norm"), kw["win"])
    t["cqn"], t["ckvn"], t["q"], t["k"], t["v"] = _qkv_fwd(t["cq"], t["ckv"], t["misc"], row("q_norm"), row("kv_norm"),
                                                         kw["wuq"], kw["wkv"], cosf, sinf)
    t["oe"], t["lse"], gathered = _attn_fwd(t["q"], t["k"], t["v"], gather)
    if after_gather is not None:
        after_gather(gathered)
    t["dtb"] = _lane_rows(sm["dt_bias"][l])
    t["a_exp"] = _lane_rows(-jnp.exp(sm["a_log"][l]))
    t["d_exp"] = _lane_rows(sm["d_skip"][l])
    t["c"], t["prev"], t["ypre"], t["yssd"] = _ssd_fwd(t["xraw"], t["misc"], t["z"], kw["conv_w"], row("conv_b"), t["dtb"],
                                                     t["a_exp"], t["d_exp"], row("ssd_norm"), consts)
    t["mixed"], t["h1"] = _outproj_fwd(t["oe"], t["yssd"], kw["wout"], h, row("post_mix_norm"))
    t["mb"], t["ab"], t["d"], *out = _mlp_fwd(t["h1"], row("pre_mlp_norm"), kw["wup"], kw["wdown"], row("post_mlp_norm"), target)
    return out, t


def _layer_bwd(dh2, t, kw, sm, l, cosf, sinf, consts, exchange_of=None):
    row = lambda name: sm[name][l].reshape(1, -1)
    g, gs = {}, {}
    dh1, dab, ddb, gs["post_mlp_norm"], gs["pre_mlp_norm"] = _mlp_bwd(
        dh2, t["d"], t["h1"], t["ab"], row("pre_mlp_norm"), kw["wup"], kw["wdown"], row("post_mlp_norm"))
    g["wup"] = _matmul_tn_stacked(t["mb"], dab, f"dw_up_{l}", a_stacked=False)
    g["wdown"] = _matmul_tn_stacked(t["ab"], ddb, f"dw_down_{l}", a_stacked=True, square_a=True)
    dmixb, doe, dyssd, gs["post_mix_norm"], delta = _outproj_bwd(dh1, t["mixed"], row("post_mix_norm"), kw["wout"], t["oe"])
    g["wout_att"] = _matmul_tn(t["oe"], dmixb, f"dw_out_att_{l}")
    g["wout_ssd"] = _matmul_tn(t["yssd"], dmixb, f"dw_out_ssd_{l}")
    dz, dxraw, dmisc_dt, gs["ssd_norm"], gd, galog, gdtb, g["conv_w"], gs["conv_b"] = _ssd_bwd(
        dyssd, t["ypre"], t["z"], t["c"], t["xraw"], t["misc"], t["prev"], kw["conv_w"], t["dtb"], t["a_exp"], t["d_exp"],
        row("ssd_norm"), consts)
    gs["d_skip"] = jnp.sum(gd.reshape(SSD_HEADS, SSD_P), axis=1)
    gs["a_log"] = galog[0, MISC_DT:MISC_DT + SSD_HEADS]
    gs["dt_bias"] = gdtb[0, MISC_DT:MISC_DT + SSD_HEADS]
    dq, dk, dv, exchanged = _attn_bwd(t["q"], t["k"], t["v"], doe, t["lse"], delta,
                                      exchange_of(g) if exchange_of is not None else ())
    dqb, dkvb, dcq, dckv, dmisc_rope, gs["q_norm"], gs["kv_norm"] = _qkv_bwd(
        dq, dk, dv, t["cq"], t["ckv"], row("q_norm"), row("kv_norm"), kw["wuq"], kw["wkv"], cosf, sinf)
    g["wuq"] = _matmul_tn(t["cqn"], dqb, f"dw_uq_{l}")
    g["wkv"] = _matmul_tn(t["ckvn"], dkvb, f"dw_kv_{l}")
    dprojb, dh0, gs["pre_mix_norm"] = _inproj_bwd(dcq, dckv, dmisc_rope, dmisc_dt, dz, dxraw, t["h0"], dh1,
                                                  row("pre_mix_norm"), kw["win"])
    g["win"] = _matmul_tn(t["ub"], dprojb, f"dw_in_{l}")
    return dh0, g, {k: v.reshape(-1) for k, v in gs.items()}, exchanged


def _local_step(x, positions, kws, sm, target, gather=(), after_gather=None, exchange_of=None):
    inv_freq = ROPE_THETA ** (-jnp.arange(0, QK_ROPE, 2, dtype=F32) / QK_ROPE)
    invf = jnp.zeros((HEAD_PAD,), F32).at[MISC_ROPE:MISC_ROPE + QK_ROPE].set(jnp.concatenate([inv_freq, inv_freq]))
    cosf, sinf = _rope_tables(positions.reshape(-1, 1), invf.reshape(1, HEAD_PAD))
    consts = _ssd_consts()
    (h,), t0 = _layer_fwd(x, kws[0], sm, 0, cosf, sinf, consts, gather, after_gather)
    (dh, loss), t1 = _layer_fwd(h, kws[1], sm, 1, cosf, sinf, consts, target=target)
    saved = [t0, t1]
    grads, small, exchanged = [None] * DEPTH, [None] * DEPTH, []
    for l in reversed(range(DEPTH)):
        hook = (lambda g0: exchange_of(g0, grads[1])) if (l == 0 and exchange_of is not None) else None
        dh, grads[l], small[l], got = _layer_bwd(dh, saved[l], kws[l], sm, l, cosf, sinf, consts, hook)
        exchanged = got or exchanged
    return loss[0, 0], dh, grads, small, exchanged


def kernel(x, positions, pre_mix_norm, w_in, q_norm, w_uq, kv_norm, w_ukv, conv_w, conv_b, dt_bias, a_log, d_skip, ssd_norm, w_out, post_mix_norm, pre_mlp_norm, w_up, w_down, post_mlp_norm, loss_target, m_pre_mix_norm, m_w_in, m_q_norm, m_w_uq, m_kv_norm, m_w_ukv, m_conv_w, m_conv_b, m_dt_bias, m_a_log, m_d_skip, m_ssd_norm, m_w_out, m_post_mix_norm, m_pre_mlp_norm, m_w_up, m_w_down, m_post_mlp_norm, v_pre_mix_norm, v_w_in, v_q_norm, v_w_uq, v_kv_norm, v_w_ukv, v_conv_w, v_conv_b, v_dt_bias, v_a_log, v_d_skip, v_ssd_norm, v_w_out, v_post_mix_norm, v_pre_mlp_norm, v_w_up, v_w_down, v_post_mlp_norm):
    w = dict(pre_mix_norm=pre_mix_norm, w_in=w_in, q_norm=q_norm, w_uq=w_uq, kv_norm=kv_norm, w_ukv=w_ukv, conv_w=conv_w,
             conv_b=conv_b, dt_bias=dt_bias, a_log=a_log, d_skip=d_skip, ssd_norm=ssd_norm, w_out=w_out,
             post_mix_norm=post_mix_norm, pre_mlp_norm=pre_mlp_norm, w_up=w_up, w_down=w_down, post_mlp_norm=post_mlp_norm)
    m = dict(pre_mix_norm=m_pre_mix_norm, w_in=m_w_in, q_norm=m_q_norm, w_uq=m_w_uq, kv_norm=m_kv_norm, w_ukv=m_w_ukv,
             conv_w=m_conv_w, conv_b=m_conv_b, dt_bias=m_dt_bias, a_log=m_a_log, d_skip=m_d_skip, ssd_norm=m_ssd_norm,
             w_out=m_w_out, post_mix_norm=m_post_mix_norm, pre_mlp_norm=m_pre_mlp_norm, w_up=m_w_up, w_down=m_w_down,
             post_mlp_norm=m_post_mlp_norm)
    v = dict(pre_mix_norm=v_pre_mix_norm, w_in=v_w_in, q_norm=v_q_norm, w_uq=v_w_uq, kv_norm=v_kv_norm, w_ukv=v_w_ukv,
             conv_w=v_conv_w, conv_b=v_conv_b, dt_bias=v_dt_bias, a_log=v_a_log, d_skip=v_d_skip, ssd_norm=v_ssd_norm,
             w_out=v_w_out, post_mix_norm=v_post_mix_norm, pre_mlp_norm=v_pre_mlp_norm, w_up=v_w_up, w_down=v_w_down,
             post_mlp_norm=v_post_mlp_norm)
    sm = {name: w[name] for name, _ in _SMALL}

    wire = lambda name, l: _wire_shard(name, w[name][l])
    first = _gather_two_level([wire(name, 0) for name in _EARLY], "weight_gather_first")
    kws = [_early_weights({name: _from_wire(name, a) for name, a in zip(_EARLY, first)}), None]
    behind = [(name, 0) for name in _LATE] + [(name, 1) for name, _ in _SHARDED]

    def after_gather(gathered):
        got = {key: _from_wire(key[0], a) for key, a in zip(behind, gathered)}
        kws[0].update(_late_weights({name: got[name, 0] for name in _LATE}))
        kws[1] = {**_early_weights({name: got[name, 1] for name in _EARLY}),
                  **_late_weights({name: got[name, 1] for name in _LATE})}

    sent_behind = [(name, 1) for name, _ in _SHARDED] + [(name, 0) for name in _LATE]

    def exchange_of(g0, g1):
        blocks = {**{(name, 1): a for name, a in _shard_grads(g1).items()},
                  **{(name, 0): a for name, a in _shard_grads(g0).items()}}
        return [blocks[key] for key in sent_behind]

    loss_part, dx, grads, small, exchanged = _local_step(
        x[0], positions[0], kws, sm, loss_target[0], [wire(*key) for key in behind], after_gather, exchange_of)
    slots = dict(zip(sent_behind, exchanged))
    last = _shard_grads({k: grads[0][k] for k in ("win", "wuq", "wkv", "conv_w")})
    slots.update({(name, 0): a for name, a in zip(_EARLY, _comm("exchange", [last[name] for name in _EARLY], "grad_exchange_last"))})
    g_small = _unpack_small(_all_reduce_small(_pack_small({name: jnp.stack([small[l][name] for l in range(DEPTH)])
                                                           for name, _ in _SMALL})))
    loss = lax.psum(loss_part, ("x", "y", "c"))

    grad, delta, new_m, new_v = {}, {}, {}, {}
    for name, _ in _SHARDED:
        grad[name], delta[name], new_m[name], new_v[name] = _sum_adamw(
            [slots[name, 0], slots[name, 1]], w[name], m[name], v[name], f"sum_adamw_{name}")
    pk = lambda d: _pack_small({name: d[name] for name, _ in _SMALL})
    d_, m_, v_ = _adamw(pk(w), pk(g_small), pk(m), pk(v), "adamw_small")
    for dst, packed in ((delta, d_), (new_m, m_), (new_v, v_)):
        dst.update(_unpack_small(packed))
    grad.update(g_small)

    outs = [loss, dx[None]]
    for d in (grad, delta, new_m, new_v):
        outs += [d[name] for name in _WEIGHT_ORDER]
    return tuple(outs)
```

```python
import jax
import jax.numpy as jnp
import numpy as np
from jax import lax
from jax.experimental import pallas as pl
from jax.experimental.pallas import tpu as pltpu

F32 = jnp.float32
BF16 = jnp.bfloat16
HI = lax.Precision.HIGHEST

D_MODEL = 1024
DEPTH = 2
N_DEV = 8
CHUNK = 64
EPS = 1e-6
MLA_HEADS = 8
QK_NOPE = 64
QK_ROPE = 32
V_DIM = 64
Q_RANK = 768
KV_RANK = 256
ROPE_THETA = 10000.0
SSD_HEADS = 8
SSD_P = 64
SSD_INNER = 512
SSD_GROUPS = 2
SSD_N = 128
CONV_W = 4
CONV_DIM = 1024
D_FF = 4096
IN_PROJ = 2600
HEAD_PAD = 128
IN_PAD = 2688
MISC_ROPE = 64
MISC_DT = 96
ATT_SCALE = (QK_NOPE + QK_ROPE) ** -0.5
LOG2E = 1.4426950408889634
ATT_SCALE_LOG2 = ATT_SCALE * LOG2E

ADAM_LR = 0.001
ADAM_B1 = 0.9
ADAM_B2 = 0.999
ADAM_EPS = 1e-08
ADAM_WD = 0.01
ADAM_STEP = 10

TM = 512
ATT_T = 512
ATT_G = 8
SSD_ROWS = 256
TK_DW = 4096
VMEM_LIMIT = 56 * 1024 * 1024

_NT = (((1,), (1,)), ((), ()))
_TN = (((0,), (0,)), ((), ()))


def _params(**kw):
    return pltpu.CompilerParams(vmem_limit_bytes=VMEM_LIMIT, **kw)


def _dot(a, b, precision=None):
    return jnp.dot(a, b, preferred_element_type=F32, precision=precision)


def _dot_nt(a, b, precision=None):
    return lax.dot_general(a, b, _NT, preferred_element_type=F32, precision=precision)


def _dot_tn(a, b, precision=None):
    return lax.dot_general(a, b, _TN, preferred_element_type=F32, precision=precision)


def _split3(x):
    hi = x.astype(BF16)
    r = x - hi.astype(F32)
    mid = r.astype(BF16)
    return hi, mid, (r - mid.astype(F32)).astype(BF16)


def _dot01(x, m01, dot=_dot, left=False):
    parts = [dot(m01, p) if left else dot(p, m01) for p in _split3(x)]
    return parts[0] + parts[1] + parts[2]


def _full(shape):
    n = len(shape)
    return pl.BlockSpec(shape, lambda *_: (0,) * n)


def _resident(shape):
    n = len(shape)
    return pl.BlockSpec(shape, lambda *_: (0,) * n, pipeline_mode=pl.Buffered(1))


def _rows(tm, width):
    return pl.BlockSpec((tm, width), lambda i: (i, 0))


def _rms_fwd(x, w):
    r = lax.rsqrt(jnp.mean(x * x, axis=-1, keepdims=True) + EPS)
    return (x * r) * w


def _rms_bwd(x, w, dy):
    r = lax.rsqrt(jnp.mean(x * x, axis=-1, keepdims=True) + EPS)
    xh = x * r
    dxn = dy * w
    dx = r * (dxn - xh * jnp.mean(dxn * xh, axis=-1, keepdims=True))
    return dx, dy * xh


def _acc_rows(ref, val, first):
    s = jnp.sum(val, axis=0, keepdims=True)

    @pl.when(first)
    def _():
        ref[...] = s

    @pl.when(jnp.logical_not(first))
    def _():
        ref[...] += s


def _rope(t, cosf, sinf, sign):
    lane = lax.broadcasted_iota(jnp.int32, t.shape, 1)
    rot = jnp.where(lane < MISC_ROPE + QK_ROPE // 2, -pltpu.roll(t, HEAD_PAD - QK_ROPE // 2, 1), pltpu.roll(t, QK_ROPE // 2, 1))
    return t * cosf + sign * (rot * sinf)


def _rope_tables(pos, invf):
    s = pos.shape[0]

    def body(pos_ref, invf_ref, cos_ref, sin_ref):
        ang = pos_ref[...].astype(F32) * invf_ref[...]
        cos_ref[...] = jnp.cos(ang)
        sin_ref[...] = jnp.sin(ang)

    return pl.pallas_call(
        body, name="rope_tables", grid=(s // TM,),
        in_specs=[_rows(TM, 1), _full((1, HEAD_PAD))],
        out_specs=[_rows(TM, HEAD_PAD), _rows(TM, HEAD_PAD)],
        out_shape=[jax.ShapeDtypeStruct((s, HEAD_PAD), F32)] * 2,
    )(pos, invf)


def _inproj_fwd(h, nw, win):
    s = h.shape[0]

    def body(h_ref, nw_ref, w_ref, ub_ref, cq_ref, ckv_ref, misc_ref, z_ref, xbc_ref):
        ub = _rms_fwd(h_ref[...], nw_ref[...]).astype(BF16)
        ub_ref[...] = ub
        proj = _dot(ub, w_ref[...])
        cq_ref[...] = proj[:, 0:768]
        ckv_ref[...] = proj[:, 768:1024]
        misc_ref[...] = proj[:, 1024:1152]
        z_ref[...] = proj[:, 1152:1664]
        xbc_ref[...] = proj[:, 1664:2688]

    widths = (768, 256, 128, 512, 1024)
    return pl.pallas_call(
        body, name="inproj_fwd", grid=(s // TM,),
        in_specs=[_rows(TM, D_MODEL), _full((1, D_MODEL)), _resident((D_MODEL, IN_PAD))],
        out_specs=[_rows(TM, D_MODEL)] + [_rows(TM, w) for w in widths],
        out_shape=[jax.ShapeDtypeStruct((s, D_MODEL), BF16)] + [jax.ShapeDtypeStruct((s, w), F32) for w in widths],
        compiler_params=_params(),
    )(h, nw, win)


def _qkv_fwd(cq, ckv, misc, qnw, kvnw, wuq, wkv, cosf, sinf):
    s = cq.shape[0]

    def body(cq_ref, ckv_ref, misc_ref, qnw_ref, kvnw_ref, wuq_ref, wkv_ref, cos_ref, sin_ref,
             cqn_ref, ckvn_ref, q_ref, k_ref, v_ref):
        cosf, sinf = cos_ref[...], sin_ref[...]
        cqn = _rms_fwd(cq_ref[...], qnw_ref[...]).astype(BF16)
        cqn_ref[...] = cqn
        q = _dot(cqn, wuq_ref[...])
        ckvn = _rms_fwd(ckv_ref[...], kvnw_ref[...]).astype(BF16)
        ckvn_ref[...] = ckvn
        kv = _dot(ckvn, wkv_ref[...])
        m = misc_ref[...]
        lane = lax.broadcasted_iota(jnp.int32, m.shape, 1)
        in_rope = jnp.logical_and(lane >= MISC_ROPE, lane < MISC_ROPE + QK_ROPE)
        kr = jnp.where(in_rope, _rope(m, cosf, sinf, 1.0), 0.0)
        for hd in range(MLA_HEADS):
            cols = slice(hd * HEAD_PAD, (hd + 1) * HEAD_PAD)
            q_ref[:, cols] = _rope(q[:, cols], cosf, sinf, 1.0).astype(BF16)
            k_ref[:, cols] = (kv[:, cols] + kr).astype(BF16)
        vv = kv[:, MLA_HEADS * HEAD_PAD:]
        vlane = lax.broadcasted_iota(jnp.int32, vv.shape, 1)
        ones_at = jnp.where((vlane // HEAD_PAD) % 2 == 0, V_DIM, 0)
        v_ref[...] = jnp.where(vlane % HEAD_PAD == ones_at, 1.0, vv).astype(BF16)

    wide = MLA_HEADS * HEAD_PAD
    return pl.pallas_call(
        body, name="qkv_fwd", grid=(s // TM,),
        in_specs=[_rows(TM, Q_RANK), _rows(TM, KV_RANK), _rows(TM, HEAD_PAD), _full((1, Q_RANK)), _full((1, KV_RANK)),
                  _resident((Q_RANK, wide)), _resident((KV_RANK, 2 * wide)), _rows(TM, HEAD_PAD), _rows(TM, HEAD_PAD)],
        out_specs=[_rows(TM, Q_RANK), _rows(TM, KV_RANK), _rows(TM, wide), _rows(TM, wide), _rows(TM, wide)],
        out_shape=[jax.ShapeDtypeStruct((s, Q_RANK), BF16), jax.ShapeDtypeStruct((s, KV_RANK), BF16)]
        + [jax.ShapeDtypeStruct((s, wide), BF16)] * 3,
        compiler_params=_params(),
    )(cq, ckv, misc, qnw, kvnw, wuq, wkv, cosf, sinf)


def _chunk_bias(t, keys_on_rows=False):
    row = lax.broadcasted_iota(jnp.int32, (t, 1), 0) // CHUNK
    col = lax.broadcasted_iota(jnp.int32, (1, t), 1) // CHUNK
    return jnp.where((row <= col) if keys_on_rows else (col <= row), 0.0, -jnp.inf).astype(F32)


def _attn_fwd(q, k, v, gather=()):
    s = q.shape[0]
    t = ATT_T
    nq = s // t
    pair = ATT_G * HEAD_PAD
    ng = len(gather)

    def body(q_ref, k_ref, v_ref, *rest):
        g_in, (o_ref, lse_ref), g_out = rest[:ng], rest[ng:ng + 2], rest[ng + 2:2 * ng + 2]
        m_s, acc_s, bias_s = rest[2 * ng + 2:2 * ng + 5]
        qi = pl.program_id(1)
        group, groups = pl.program_id(0), MLA_HEADS // ATT_G

        @pl.when(jnp.logical_and(group == 0, qi == 0))
        def _():
            bias_s[...] = _chunk_bias(t)

        _hosted_gather(g_in, g_out, rest[2 * ng + 5:],
                       jnp.logical_and(group == 0, qi == 0),
                       jnp.logical_and(group == groups - 1, qi == min(3 * nq // 4 + 1, nq - 1)),
                       jnp.logical_and(group == groups - 1, qi == nq - 1))
        m_s[...] = jnp.full(m_s.shape, -jnp.inf, F32)
        acc_s[...] = jnp.zeros(acc_s.shape, F32)

        def step(kb, masked):
            r0 = pl.multiple_of(kb * t, t)

            def scores(hh):
                cols = slice(hh * HEAD_PAD, (hh + 1) * HEAD_PAD)
                return _dot_nt(q_ref[:, cols], k_ref[pl.ds(r0, t), cols])

            def soft(hh, raw):
                sc = raw * ATT_SCALE_LOG2
                if masked:
                    sc = sc + bias_s[...]
                m_old = m_s[hh]
                m_new = jnp.maximum(m_old, jnp.max(sc, axis=-1, keepdims=True))
                alpha = jnp.exp2(m_old - m_new)
                p = jnp.exp2(sc - jnp.tile(m_new, (1, t // HEAD_PAD)))
                m_s[hh] = m_new
                return alpha, p.astype(BF16)

            def update(hh, alpha, p):
                cols = slice(hh * HEAD_PAD, (hh + 1) * HEAD_PAD)
                acc_s[hh] = alpha * acc_s[hh] + _dot(p, v_ref[pl.ds(r0, t), cols])

            raw, ap = [None] * ATT_G, [None] * ATT_G
            raw[0] = scores(0)
            for hh in range(ATT_G):
                if hh + 1 < ATT_G:
                    raw[hh + 1] = scores(hh + 1)
                ap[hh] = soft(hh, raw[hh])
                if hh >= 1:
                    update(hh - 1, *ap[hh - 1])
            update(ATT_G - 1, *ap[ATT_G - 1])

        def loop(kb, c):
            step(kb, False)
            return c

        lax.fori_loop(0, qi, loop, 0)
        step(qi, True)
        for hh in range(ATT_G):
            cols = slice(hh * HEAD_PAD, (hh + 1) * HEAD_PAD)
            acc = acc_s[hh]
            ones_at = V_DIM * (1 - hh % 2)
            l = jnp.broadcast_to(acc[:, ones_at:ones_at + 1], acc.shape)
            o_ref[:, cols] = (acc / l).astype(BF16)
            lse_ref[hh] = (m_s[hh] + jnp.log(l) * LOG2E).T[0:8, :]

    outs = pl.pallas_call(
        body, name="attn_fwd_gather" if ng else "attn_fwd", grid=(MLA_HEADS // ATT_G, nq),
        in_specs=[pl.BlockSpec((t, pair), lambda h, i: (i, h)),
                  pl.BlockSpec((s, pair), lambda h, i: (0, h), pipeline_mode=pl.Buffered(1)),
                  pl.BlockSpec((s, pair), lambda h, i: (0, h), pipeline_mode=pl.Buffered(1))] + [_ANY] * ng,
        out_specs=[pl.BlockSpec((t, pair), lambda h, i: (i, h)),
                   pl.BlockSpec((ATT_G, 8, t), lambda h, i: (h, 0, i))] + [_ANY] * ng,
        out_shape=[jax.ShapeDtypeStruct((s, MLA_HEADS * HEAD_PAD), BF16), jax.ShapeDtypeStruct((MLA_HEADS, 8, s), F32)]
        + _comm_out_shapes("gather", gather),
        scratch_shapes=[pltpu.VMEM((ATT_G, t, HEAD_PAD), F32), pltpu.VMEM((ATT_G, t, HEAD_PAD), F32), pltpu.VMEM((t, t), F32)]
        + (_comm_scratch(ng) if ng else []),
        compiler_params=_params(),
    )(q, k, v, *gather)
    return outs[0], outs[1], list(outs[2:])


def _interleave(stages):
    live = list(stages)
    while live:
        still = []
        for g in live:
            try:
                next(g)
                still.append(g)
            except StopIteration:
                pass
        live = still


def _ssd_consts():
    emisc = np.zeros((HEAD_PAD, SSD_INNER), np.float32)
    for hd in range(SSD_HEADS):
        emisc[MISC_DT + hd, hd * SSD_P:(hd + 1) * SSD_P] = 1.0
    idx = np.arange(CHUNK)
    tri = (idx[:, None] >= idx[None, :]).astype(np.float32)
    return tuple(jnp.asarray(m, BF16) for m in (emisc, emisc.T.copy(), tri, tri.T.copy()))


def _ssd_chunk_common(cc, misc, emisc, tri, trit, dtb, a_exp):
    sig = jax.nn.sigmoid(cc)
    xa = cc * sig
    dt = jax.nn.softplus(_dot01(misc, emisc) + dtb)
    a = dt * a_exp
    acs = _dot01(a, tri, left=True)
    acs_t = _dot01(a, trit, dot=_dot_tn)
    alast = acs[CHUNK - 1:CHUNK, :]
    return xa, sig, dt, acs, acs_t, alast


def _decay(acs, acs_t, hd):
    row = lax.broadcasted_iota(jnp.int32, (CHUNK, CHUNK), 0)
    col = lax.broadcasted_iota(jnp.int32, (CHUNK, CHUNK), 1)
    diff = acs[:, hd * SSD_P:hd * SSD_P + 1] - acs_t[hd * SSD_P:hd * SSD_P + 1, :]
    return jnp.exp(jnp.where(row >= col, diff, -jnp.inf))


def _half_mask(hh):
    lane = lax.broadcasted_iota(jnp.int32, (CHUNK, 2 * SSD_P), 1)
    return (lane >= SSD_P) if hh else (lane < SSD_P)


def _gate_norm(y, zz):
    sg = jax.nn.sigmoid(zz)
    yz = y * (zz * sg)
    outs, rs = [], []
    half = SSD_INNER // SSD_GROUPS
    for g in range(SSD_GROUPS):
        yg = yz[:, g * half:(g + 1) * half]
        r = lax.rsqrt(jnp.mean(yg * yg, axis=-1, keepdims=True) + EPS)
        outs.append(yg * r)
        rs.append(r)
    return sg, jnp.concatenate(outs, axis=1), rs


def _ssd_fwd(xraw, misc, z, cw, cb, dtb, a_exp, d_exp, nw, consts):
    s = xraw.shape[0]
    nb = s // SSD_ROWS
    ncb = SSD_ROWS // CHUNK
    emisc, _, tri, trit = consts

    def body(x_ref, misc_ref, z_ref, cw_ref, cb_ref, dtb_ref, a_ref, d_ref, nw_ref, emisc_ref, tri_ref, trit_ref,
             c_ref, prev_ref, ypre_ref, yssd_ref, tail_s, state_s):
        i = pl.program_id(0)

        @pl.when(i == 0)
        def _():
            tail_s[...] = jnp.zeros(tail_s.shape, F32)
            state_s[...] = jnp.zeros(state_s.shape, F32)

        x = x_ref[...]
        xext = jnp.concatenate([tail_s[...], x], axis=0)
        acc = x * cw_ref[CONV_W - 1:CONV_W, :] + cb_ref[...]
        for j in range(1, CONV_W):
            acc = acc + pltpu.roll(xext, j, 0)[8:, :] * cw_ref[CONV_W - 1 - j:CONV_W - j, :]
        tail_s[...] = x[SSD_ROWS - 8:, :]
        c_ref[...] = acc

        def chunk(ci):
            r0 = ci * CHUNK
            xa, _, dt, acs, acs_t, alast = _ssd_chunk_common(
                c_ref[pl.ds(r0, CHUNK), :], misc_ref[pl.ds(r0, CHUNK), :], emisc_ref[...], tri_ref[...], trit_ref[...],
                dtb_ref[...], a_ref[...])
            yield
            xs = xa[:, :SSD_INNER]
            xdt = xs * dt
            wgt = (xdt * jnp.exp(alast - acs)).astype(BF16)
            e = jnp.exp(acs)
            ys, new_states, cms = [], [], []
            for g in range(SSD_GROUPS):
                bm = xa[:, SSD_INNER + g * SSD_N:SSD_INNER + (g + 1) * SSD_N].astype(BF16)
                cm = xa[:, SSD_INNER + SSD_GROUPS * SSD_N + g * SSD_N:SSD_INNER + SSD_GROUPS * SSD_N + (g + 1) * SSD_N].astype(BF16)
                cms.append(cm)
                cb_g = _dot_nt(cm, bm)
                gl = slice(g * 256, (g + 1) * 256)
                new_states.append(_dot_tn(bm, wgt[:, gl]))
                for jj in range(2):
                    pair = 2 * g + jj
                    xp = xdt[:, pair * 128:(pair + 1) * 128]
                    yp = None
                    for hh in range(2):
                        sc = (cb_g * _decay(acs, acs_t, 2 * pair + hh)).astype(BF16)
                        term = _dot(sc, jnp.where(_half_mask(hh), xp, 0.0).astype(BF16))
                        yp = term if yp is None else yp + term
                    ys.append(yp)
                yield
            prev = state_s[...]
            prev_ref[ci] = prev
            yoff = jnp.concatenate([_dot(cms[g], prev[:, g * 256:(g + 1) * 256].astype(BF16)) for g in range(SSD_GROUPS)],
                                   axis=1) * e
            state_s[...] = prev * jnp.exp(alast) + jnp.concatenate(new_states, axis=1)
            yield
            y = jnp.concatenate(ys, axis=1) + yoff + d_ref[...] * xs
            ypre_ref[pl.ds(r0, CHUNK), :] = y
            _, yn, _ = _gate_norm(y, z_ref[pl.ds(r0, CHUNK), :])
            yssd_ref[pl.ds(r0, CHUNK), :] = (yn * nw_ref[...]).astype(BF16)

        _interleave([chunk(ci) for ci in range(ncb)])

    return pl.pallas_call(
        body, name="ssd_fwd", grid=(nb,),
        in_specs=[_rows(SSD_ROWS, CONV_DIM), _rows(SSD_ROWS, HEAD_PAD), _rows(SSD_ROWS, SSD_INNER),
                  _full((CONV_W, CONV_DIM)), _full((1, CONV_DIM)), _full((1, SSD_INNER)), _full((1, SSD_INNER)),
                  _full((1, SSD_INNER)), _full((1, SSD_INNER)), _full((HEAD_PAD, SSD_INNER)), _full((CHUNK, CHUNK)),
                  _full((CHUNK, CHUNK))],
        out_specs=[_rows(SSD_ROWS, CONV_DIM), pl.BlockSpec((ncb, SSD_N, SSD_INNER), lambda i: (i, 0, 0)),
                   _rows(SSD_ROWS, SSD_INNER), _rows(SSD_ROWS, SSD_INNER)],
        out_shape=[jax.ShapeDtypeStruct((s, CONV_DIM), F32), jax.ShapeDtypeStruct((s // CHUNK, SSD_N, SSD_INNER), F32),
                   jax.ShapeDtypeStruct((s, SSD_INNER), F32), jax.ShapeDtypeStruct((s, SSD_INNER), BF16)],
        scratch_shapes=[pltpu.VMEM((8, CONV_DIM), F32), pltpu.VMEM((SSD_N, SSD_INNER), F32)],
        compiler_params=_params(),
    )(xraw, misc, z, cw, cb, dtb, a_exp, d_exp, nw, emisc, tri, trit)


def _outproj_fwd(oe, yssd, wout, h, nw):
    s = h.shape[0]
    wide = MLA_HEADS * HEAD_PAD

    def body(oe_ref, y_ref, w_ref, h_ref, nw_ref, mixed_ref, h1_ref):
        mixed = _dot(oe_ref[...], w_ref[0:wide, :]) + _dot(y_ref[...], w_ref[wide:, :])
        mixed_ref[...] = mixed
        h1_ref[...] = h_ref[...] + _rms_fwd(mixed, nw_ref[...])

    return pl.pallas_call(
        body, name="outproj_fwd", grid=(s // TM,),
        in_specs=[_rows(TM, wide), _rows(TM, SSD_INNER), _resident((wide + SSD_INNER, D_MODEL)), _rows(TM, D_MODEL),
                  _full((1, D_MODEL))],
        out_specs=[_rows(TM, D_MODEL), _rows(TM, D_MODEL)],
        out_shape=[jax.ShapeDtypeStruct((s, D_MODEL), F32)] * 2,
        compiler_params=_params(),
    )(oe, yssd, wout, h, nw)


def _mlp_fwd(h1, prew, wup, wdown, postw, target=None):
    s = h1.shape[0]
    fb = D_FF // N_DEV
    last = target is not None

    def body(h_ref, prew_ref, up_ref, down_ref, postw_ref, *rest):
        target_ref, (mb_ref, ab_ref, d_ref, out_ref) = (rest[0] if last else None), rest[last:last + 4]
        hh = h_ref[...]
        mb = _rms_fwd(hh, prew_ref[...]).astype(BF16)
        mb_ref[...] = mb
        d = jnp.zeros((TM, D_MODEL), F32)
        for j in range(N_DEV):
            a = jnp.maximum(_dot(mb, up_ref[j]), 0.0)
            ab_ref[j] = a.astype(BF16)
            d = d + _dot(jnp.square(a).astype(BF16), down_ref[j])
        d_ref[...] = d
        h2 = hh + _rms_fwd(d, postw_ref[...])
        if last:
            diff = h2 - target_ref[...]
            out_ref[...] = diff * (1.0 / D_MODEL)
            part = 0.5 * jnp.sum(jnp.mean(diff * diff, axis=-1, keepdims=True), axis=0, keepdims=True)
            _acc_rows(rest[-1], part, pl.program_id(0) == 0)
        else:
            out_ref[...] = h2

    stacked = pl.BlockSpec((N_DEV, TM, fb), lambda i: (0, i, 0))
    return pl.pallas_call(
        body, name="mlp_fwd_loss" if last else "mlp_fwd", grid=(s // TM,),
        in_specs=[_rows(TM, D_MODEL), _full((1, D_MODEL)), _resident((N_DEV, D_MODEL, fb)), _resident((N_DEV, fb, D_MODEL)),
                  _full((1, D_MODEL))] + ([_rows(TM, D_MODEL)] if last else []),
        out_specs=[_rows(TM, D_MODEL), stacked, _rows(TM, D_MODEL), _rows(TM, D_MODEL)] + ([_full((1, 1))] if last else []),
        out_shape=[jax.ShapeDtypeStruct((s, D_MODEL), BF16), jax.ShapeDtypeStruct((N_DEV, s, fb), BF16),
                   jax.ShapeDtypeStruct((s, D_MODEL), F32), jax.ShapeDtypeStruct((s, D_MODEL), F32)]
        + ([jax.ShapeDtypeStruct((1, 1), F32)] if last else []),
        compiler_params=_params(),
    )(h1, prew, wup, wdown, postw, *([target] if last else []))


def _mlp_bwd(dh2, d, h1, ab, prew, wup, wdown, postw):
    s = dh2.shape[0]
    fb = D_FF // N_DEV
    tm = TM // 2

    def body(dh2_ref, d_ref, h1_ref, ab_ref, prew_ref, up_ref, down_ref, postw_ref,
             dh1_ref, da_ref, dd_ref, gpost_ref, gpre_ref):
        first = pl.program_id(0) == 0
        dh2 = dh2_ref[...]
        dd, gpost = _rms_bwd(d_ref[...], postw_ref[...], dh2)
        _acc_rows(gpost_ref, gpost, first)
        ddb = dd.astype(BF16)
        dd_ref[...] = ddb

        def d_relu_squared(j):
            return _dot_nt(ddb, down_ref[j])

        def pointwise(j, dr):
            da = (dr * (2.0 * ab_ref[j].astype(F32))).astype(BF16)
            da_ref[j] = da
            return da

        dm = jnp.zeros((tm, D_MODEL), F32)
        nxt, da_prev = d_relu_squared(0), None
        for j in range(N_DEV):
            cur = nxt
            if j + 1 < N_DEV:
                nxt = d_relu_squared(j + 1)
            da = pointwise(j, cur)
            if da_prev is not None:
                dm = dm + _dot_nt(da_prev, up_ref[j - 1])
            da_prev = da
        dm = dm + _dot_nt(da_prev, up_ref[N_DEV - 1])
        dx, gpre = _rms_bwd(h1_ref[...], prew_ref[...], dm)
        _acc_rows(gpre_ref, gpre, first)
        dh1_ref[...] = dh2 + dx

    stacked = pl.BlockSpec((N_DEV, tm, fb), lambda i: (0, i, 0))
    return pl.pallas_call(
        body, name="mlp_bwd", grid=(s // tm,),
        in_specs=[_rows(tm, D_MODEL)] * 3 + [stacked, _full((1, D_MODEL)), _resident((N_DEV, D_MODEL, fb)),
                                              _resident((N_DEV, fb, D_MODEL)), _full((1, D_MODEL))],
        out_specs=[_rows(tm, D_MODEL), stacked, _rows(tm, D_MODEL), _full((1, D_MODEL)), _full((1, D_MODEL))],
        out_shape=[jax.ShapeDtypeStruct((s, D_MODEL), F32), jax.ShapeDtypeStruct((N_DEV, s, fb), BF16),
                   jax.ShapeDtypeStruct((s, D_MODEL), BF16), jax.ShapeDtypeStruct((1, D_MODEL), F32),
                   jax.ShapeDtypeStruct((1, D_MODEL), F32)],
        compiler_params=_params(),
    )(dh2, d, h1, ab, prew, wup, wdown, postw)


def _matmul_tn(a, b, name, tk=TK_DW):
    s, m = a.shape
    n = b.shape[1]
    tn = n if n <= 1024 else (n // 2 if (n // 2) % 128 == 0 else n // 3)
    tk = min(tk, s)
    assert n % tn == 0 and tn % 128 == 0 and s % tk == 0

    def body(a_ref, b_ref, o_ref):
        part = _dot_tn(a_ref[...], b_ref[...])

        @pl.when(pl.program_id(1) == 0)
        def _():
            o_ref[...] = part

        @pl.when(pl.program_id(1) != 0)
        def _():
            o_ref[...] += part

    return pl.pallas_call(
        body, name=name, grid=(n // tn, s // tk),
        in_specs=[pl.BlockSpec((tk, m), lambda j, k: (k, 0)), pl.BlockSpec((tk, tn), lambda j, k: (k, j))],
        out_specs=pl.BlockSpec((m, tn), lambda j, k: (0, j)),
        out_shape=jax.ShapeDtypeStruct((m, n), F32),
        compiler_params=_params(),
    )(a, b)


def _matmul_tn_stacked(a, b, name, a_stacked, square_a=False, tk=TK_DW):
    tk = min(tk, a.shape[-2])
    if a_stacked:
        _, s, m = a.shape
        n = b.shape[1]
        in_specs = [pl.BlockSpec((1, tk, m), lambda j, k: (j, k, 0)), pl.BlockSpec((tk, n), lambda j, k: (k, 0))]
    else:
        s, m = a.shape
        n = b.shape[2]
        in_specs = [pl.BlockSpec((tk, m), lambda j, k: (k, 0)), pl.BlockSpec((1, tk, n), lambda j, k: (j, k, 0))]

    nk = s // tk

    def body(a_ref, b_ref, o_ref, acc_s):
        av = a_ref[0] if a_stacked else a_ref[...]
        bv = b_ref[...] if a_stacked else b_ref[0]
        if square_a:
            av = jnp.square(av.astype(F32)).astype(BF16)
        part = _dot_tn(av, bv)
        k = pl.program_id(1)

        @pl.when(k == 0)
        def _():
            acc_s[...] = part

        @pl.when(jnp.logical_and(k != 0, k != nk - 1))
        def _():
            acc_s[...] += part

        @pl.when(k == nk - 1)
        def _():
            o_ref[0] = (part if nk == 1 else acc_s[...] + part).astype(BF16)

    return pl.pallas_call(
        body, name=name, grid=(N_DEV, nk),
        in_specs=in_specs,
        out_specs=pl.BlockSpec((1, m, n), lambda j, k: (j, 0, 0)),
        out_shape=jax.ShapeDtypeStruct((N_DEV, m, n), BF16),
        scratch_shapes=[pltpu.VMEM((m, n), F32)],
        compiler_params=_params(),
    )(a, b)


def _outproj_bwd(dh1, mixed, nw, wout, oe):
    s = dh1.shape[0]
    wide = MLA_HEADS * HEAD_PAD

    def body(dh1_ref, mixed_ref, nw_ref, w_ref, oe_ref, dmix_ref, doe_ref, dy_ref, gnw_ref, delta_ref):
        dmix, gnw = _rms_bwd(mixed_ref[...], nw_ref[...], dh1_ref[...])
        _acc_rows(gnw_ref, gnw, pl.program_id(0) == 0)
        dmb = dmix.astype(BF16)
        dmix_ref[...] = dmb
        doe_ref[...] = _dot_nt(dmb, w_ref[0:wide, :]).astype(BF16)
        dy_ref[...] = _dot_nt(dmb, w_ref[wide:, :])
        ones = jnp.ones((8, HEAD_PAD), BF16)
        for hd in range(MLA_HEADS):
            cols = slice(hd * HEAD_PAD, (hd + 1) * HEAD_PAD)
            prod = oe_ref[:, cols].astype(F32) * doe_ref[:, cols].astype(F32)
            delta_ref[hd] = _dot01(prod, ones, dot=_dot_nt, left=True)

    return pl.pallas_call(
        body, name="outproj_bwd", grid=(s // TM,),
        in_specs=[_rows(TM, D_MODEL), _rows(TM, D_MODEL), _full((1, D_MODEL)), _resident((wide + SSD_INNER, D_MODEL)),
                  _rows(TM, wide)],
        out_specs=[_rows(TM, D_MODEL), _rows(TM, wide), _rows(TM, SSD_INNER), _full((1, D_MODEL)),
                   pl.BlockSpec((MLA_HEADS, 8, TM), lambda i: (0, 0, i))],
        out_shape=[jax.ShapeDtypeStruct((s, D_MODEL), BF16), jax.ShapeDtypeStruct((s, wide), BF16),
                   jax.ShapeDtypeStruct((s, SSD_INNER), F32), jax.ShapeDtypeStruct((1, D_MODEL), F32),
                   jax.ShapeDtypeStruct((MLA_HEADS, 8, s), F32)],
        compiler_params=_params(),
    )(dh1, mixed, nw, wout, oe)


def _attn_bwd(q, k, v, do, lse, delta, exchange=()):
    s = q.shape[0]
    t = ATT_T
    nq = s // t
    pair = 2 * HEAD_PAD
    ne = len(exchange)

    def body(q_ref, k_ref, v_ref, do_ref, lse_ref, delta_ref, *rest):
        e_in, (dq_ref, dk_ref, dv_ref), e_out = rest[:ne], rest[ne:ne + 3], rest[ne + 3:2 * ne + 3]
        dk_s, dv_s, bias_s = rest[2 * ne + 3:2 * ne + 6]
        kb = pl.program_id(1)
        _hosted_comm("exchange", e_in, e_out, rest[2 * ne + 6:],
                     jnp.logical_and(pl.program_id(0) == 0, kb == 0),
                     jnp.logical_and(pl.program_id(0) == MLA_HEADS // 2 - 1, kb == nq - 1))

        @pl.when(jnp.logical_and(pl.program_id(0) == 0, kb == 0))
        def _():
            bias_s[...] = _chunk_bias(t, keys_on_rows=True)

        @pl.when(kb == 0)
        def _():
            dq_ref[...] = jnp.zeros(dq_ref.shape, F32)

        def step(qb, diagonal):
            r0 = pl.multiple_of(qb * t, t)
            for hh in range(2):
                cols = slice(hh * HEAD_PAD, (hh + 1) * HEAD_PAD)
                kk = k_ref[:, cols]
                qq = q_ref[pl.ds(r0, t), cols]
                dd = do_ref[pl.ds(r0, t), cols]
                sc = _dot_nt(kk, qq) * ATT_SCALE_LOG2
                if diagonal:
                    sc = sc + bias_s[...]
                p = jnp.exp2(sc - lse_ref[hh, 0:1, pl.ds(r0, t)])
                dv = _dot(p.astype(BF16), dd)
                dp = _dot_nt(v_ref[:, cols], dd)
                ds = (p * (dp - delta_ref[hh, 0:1, pl.ds(r0, t)]) * ATT_SCALE).astype(BF16)
                dk = _dot(ds, qq)
                if diagonal:
                    dv_s[:, cols] = dv
                    dk_s[:, cols] = dk
                else:
                    dv_s[:, cols] += dv
                    dk_s[:, cols] += dk
                dq_ref[pl.ds(r0, t), cols] += _dot_tn(ds, kk)

        def loop(i, c):
            step(kb + 1 + 2 * i, False)
            step(kb + 2 + 2 * i, False)
            return c

        step(kb, True)
        later_tiles = nq - 1 - kb
        lax.fori_loop(0, later_tiles // 2, loop, 0)

        @pl.when(later_tiles % 2 == 1)
        def _():
            step(nq - 1, False)

        dk_ref[...] = dk_s[...].astype(BF16)
        dv_ref[...] = dv_s[...].astype(BF16)

    whole = pl.BlockSpec((s, pair), lambda h, i: (0, h))
    tile = pl.BlockSpec((t, pair), lambda h, i: (i, h))
    rowvec = pl.BlockSpec((2, 8, s), lambda h, i: (h, 0, 0))
    wide = MLA_HEADS * HEAD_PAD
    outs = pl.pallas_call(
        body, name="attn_bwd_exchange" if ne else "attn_bwd", grid=(MLA_HEADS // 2, nq),
        in_specs=[whole, tile, tile, whole, rowvec, rowvec] + [_ANY] * ne,
        out_specs=[whole, tile, tile] + [_ANY] * ne,
        out_shape=[jax.ShapeDtypeStruct((s, wide), F32)] + [jax.ShapeDtypeStruct((s, wide), BF16)] * 2
        + _comm_out_shapes("exchange", exchange),
        scratch_shapes=[pltpu.VMEM((t, pair), F32), pltpu.VMEM((t, pair), F32), pltpu.VMEM((t, t), F32)]
        + (_comm_scratch(ne) if ne else []),
        compiler_params=_params(),
    )(q, k, v, do, lse, delta, *exchange)
    return outs[0], outs[1], outs[2], list(outs[3:])


def _ssd_bwd(dy, ypre, z, c, xraw, misc, prev, cw, dtb, a_exp, d_exp, nw, consts):
    s = dy.shape[0]
    nb = s // SSD_ROWS
    ncb = SSD_ROWS // CHUNK
    emisc, emisc_t, tri, trit = consts

    def body(dy_ref, ypre_ref, z_ref, c_ref, x_ref, misc_ref, prev_ref, cw_ref, dtb_ref, a_ref, d_ref, nw_ref,
             emisc_ref, emisct_ref, tri_ref, trit_ref,
             dz_ref, dx_ref, dmisc_ref, gnw_ref, gd_ref, galog_ref, gdtb_ref, gcw_ref, gcb_ref,
             dst_s, dc_s, head_s):
        i = pl.program_id(0)
        first = i == 0

        @pl.when(first)
        def _():
            dst_s[...] = jnp.zeros(dst_s.shape, F32)
            head_s[...] = jnp.zeros(head_s.shape, F32)
            gnw_ref[...] = jnp.zeros(gnw_ref.shape, F32)
            gd_ref[...] = jnp.zeros(gd_ref.shape, F32)
            galog_ref[...] = jnp.zeros(galog_ref.shape, F32)
            gdtb_ref[...] = jnp.zeros(gdtb_ref.shape, F32)

        a_exp_v = a_ref[...]
        a8 = _dot01(a_exp_v, emisct_ref[...]) * (1.0 / SSD_P)

        def chunk(ci):
            r0 = ci * CHUNK
            cc = c_ref[pl.ds(r0, CHUNK), :]
            mm = misc_ref[pl.ds(r0, CHUNK), :]
            xa, sig_c, dt, acs, acs_t, alast = _ssd_chunk_common(cc, mm, emisc_ref[...], tri_ref[...], trit_ref[...],
                                                              dtb_ref[...], a_exp_v)
            yield
            xs = xa[:, :SSD_INNER]
            xdt = xs * dt
            y = ypre_ref[pl.ds(r0, CHUNK), :]
            zz = z_ref[pl.ds(r0, CHUNK), :]
            sg, yn, rs = _gate_norm(y, zz)
            dyo = dy_ref[pl.ds(r0, CHUNK), :]
            gnw_ref[...] += jnp.sum(dyo * yn, axis=0, keepdims=True)
            dyn = dyo * nw_ref[...]
            half = SSD_INNER // SSD_GROUPS
            dyz_parts = []
            for g in range(SSD_GROUPS):
                gl = slice(g * half, (g + 1) * half)
                dyz_parts.append(rs[g] * (dyn[:, gl] - yn[:, gl] * jnp.mean(dyn[:, gl] * yn[:, gl], axis=-1, keepdims=True)))
            dyz = jnp.concatenate(dyz_parts, axis=1)
            dz_ref[pl.ds(r0, CHUNK), :] = dyz * y * (sg * (1.0 + zz * (1.0 - sg)))
            dyp = dyz * (zz * sg)
            dypb = dyp.astype(BF16)
            gd_ref[...] += jnp.sum(dyp * xs, axis=0, keepdims=True)
            yield
            prev = prev_ref[ci]
            cd = jnp.exp(alast)
            e = jnp.exp(acs)
            dsx = jnp.exp(alast - acs)
            wgt = (xdt * dsx).astype(BF16)
            dze = (dyp * e).astype(BF16)
            dprev_parts, diag_all, dbm, dcm, yoff_parts, bms = [], [], [], [], [], []
            lane8 = lax.broadcasted_iota(jnp.int32, (CHUNK, HEAD_PAD), 1)
            diag8 = jnp.zeros((CHUNK, HEAD_PAD), F32)
            for g in range(SSD_GROUPS):
                gl = slice(g * 256, (g + 1) * 256)
                bm = xa[:, SSD_INNER + g * SSD_N:SSD_INNER + (g + 1) * SSD_N].astype(BF16)
                cm = xa[:, SSD_INNER + SSD_GROUPS * SSD_N + g * SSD_N:SSD_INNER + SSD_GROUPS * SSD_N + (g + 1) * SSD_N].astype(BF16)
                bms.append(bm)
                prev_g = prev[:, gl].astype(BF16)
                dcm_g = _dot_nt(dze[:, gl], prev_g)
                dprev_parts.append(_dot_tn(cm, dze[:, gl]))
                cb_g = _dot_nt(cm, bm)
                dcb = jnp.zeros((CHUNK, CHUNK), F32)
                diag_parts = []
                for jj in range(2):
                    pair = 2 * g + jj
                    pl_ = slice(pair * 128, (pair + 1) * 128)
                    xp = xdt[:, pl_]
                    dyp_p = dypb[:, pl_]
                    dxp = jnp.zeros((CHUNK, 128), F32)
                    for hh in range(2):
                        hd = 2 * pair + hh
                        dec = _decay(acs, acs_t, hd)
                        xm = jnp.where(_half_mask(hh), xp, 0.0).astype(BF16)
                        dsc = _dot_nt(dyp_p, xm) * dec
                        dcb = dcb + dsc
                        sc = (cb_g * dec).astype(BF16)
                        dxp = dxp + jnp.where(_half_mask(hh), _dot_tn(sc, dyp_p), 0.0)
                        dm = dsc * cb_g
                        diag8 = diag8 + jnp.where(lane8 == MISC_DT + hd, jnp.sum(dm - dm.T, axis=1, keepdims=True), 0.0)
                    diag_parts.append(dxp)
                dcbb = dcb.astype(BF16)
                dcm.append(dcm_g + _dot(dcbb, bm))
                dbm.append(_dot_tn(dcbb, cm))
                diag_all.append(jnp.concatenate(diag_parts, axis=1))
                yoff_parts.append(_dot(cm, prev_g) * e[:, gl])
                yield
            dst = dst_s[...]
            glast = jnp.sum(dst * prev, axis=0, keepdims=True) * cd
            dxdt_state_parts = []
            for g in range(SSD_GROUPS):
                gl = slice(g * 256, (g + 1) * 256)
                dst_g = dst[:, gl].astype(BF16)
                dxdt_state_parts.append(_dot(bms[g], dst_g) * dsx[:, gl])
                dbm[g] = dbm[g] + _dot_nt(wgt[:, gl], dst_g)
            dst_s[...] = dst * cd + jnp.concatenate(dprev_parts, axis=1)
            yield
            dxdt_state = jnp.concatenate(dxdt_state_parts, axis=1)
            dxdt = jnp.concatenate(diag_all, axis=1) + dxdt_state
            dacs = dyp * jnp.concatenate(yoff_parts, axis=1) - xdt * dxdt_state
            last = jnp.sum(xdt * dxdt_state, axis=0, keepdims=True) + glast
            row = lax.broadcasted_iota(jnp.int32, (CHUNK, SSD_INNER), 0)
            dacs = dacs + jnp.where(row == CHUNK - 1, last, 0.0)
            dacs8 = _dot01(dacs, emisct_ref[...]) + diag8
            da8 = _dot01(dacs8, trit_ref[...], left=True)
            ddt8 = da8 * a8 + _dot01(dxdt * xs, emisct_ref[...])
            yield
            dtr8 = mm + _dot01(dtb_ref[...], emisct_ref[...]) * (1.0 / SSD_P)
            dt8 = jax.nn.softplus(dtr8)
            lane = lax.broadcasted_iota(jnp.int32, (CHUNK, HEAD_PAD), 1)
            on_dt = jnp.logical_and(lane >= MISC_DT, lane < MISC_DT + SSD_HEADS)
            ddtr8 = jnp.where(on_dt, ddt8 * jax.nn.sigmoid(dtr8), 0.0)
            dmisc_ref[pl.ds(r0, CHUNK), :] = ddtr8
            gdtb_ref[...] += jnp.sum(ddtr8, axis=0, keepdims=True)
            galog_ref[...] += jnp.sum(jnp.where(on_dt, da8 * dt8, 0.0), axis=0, keepdims=True) * a8
            dxs = d_ref[...] * dyp + dxdt * dt
            dxa = jnp.concatenate([dxs] + dbm + dcm, axis=1)
            dc_s[pl.ds(r0, CHUNK), :] = dxa * (sig_c * (1.0 + cc * (1.0 - sig_c)))

        _interleave([chunk(ci) for ci in reversed(range(ncb))])

        dc = dc_s[...]
        x = x_ref[...]
        dcext = jnp.concatenate([dc, head_s[...]], axis=0)
        dx = dc * cw_ref[CONV_W - 1:CONV_W, :]
        rows = [jnp.sum(dc * x, axis=0, keepdims=True)]
        for j in range(1, CONV_W):
            ahead = pltpu.roll(dcext, SSD_ROWS + 8 - j, 0)[:SSD_ROWS, :]
            dx = dx + ahead * cw_ref[CONV_W - 1 - j:CONV_W - j, :]
            rows.insert(0, jnp.sum(ahead * x, axis=0, keepdims=True))
        dx_ref[...] = dx
        head_s[...] = dc[:8, :]
        gcw = jnp.concatenate(rows, axis=0)

        @pl.when(first)
        def _():
            gcw_ref[...] = gcw
            gcb_ref[...] = jnp.sum(dc, axis=0, keepdims=True)

        @pl.when(jnp.logical_not(first))
        def _():
            gcw_ref[...] += gcw
            gcb_ref[...] += jnp.sum(dc, axis=0, keepdims=True)

    def rev(width):
        return pl.BlockSpec((SSD_ROWS, width), lambda i: (nb - 1 - i, 0))

    return pl.pallas_call(
        body, name="ssd_bwd", grid=(nb,),
        in_specs=[rev(SSD_INNER), rev(SSD_INNER), rev(SSD_INNER), rev(CONV_DIM), rev(CONV_DIM),
                  rev(HEAD_PAD), pl.BlockSpec((ncb, SSD_N, SSD_INNER), lambda i: (nb - 1 - i, 0, 0)),
                  _full((CONV_W, CONV_DIM)), _full((1, SSD_INNER)), _full((1, SSD_INNER)), _full((1, SSD_INNER)),
                  _full((1, SSD_INNER)), _full((HEAD_PAD, SSD_INNER)), _full((SSD_INNER, HEAD_PAD)), _full((CHUNK, CHUNK)),
                  _full((CHUNK, CHUNK))],
        out_specs=[rev(SSD_INNER), rev(CONV_DIM), rev(HEAD_PAD), _full((1, SSD_INNER)), _full((1, SSD_INNER)),
                   _full((1, HEAD_PAD)), _full((1, HEAD_PAD)), _full((CONV_W, CONV_DIM)), _full((1, CONV_DIM))],
        out_shape=[jax.ShapeDtypeStruct((s, SSD_INNER), F32), jax.ShapeDtypeStruct((s, CONV_DIM), F32),
                   jax.ShapeDtypeStruct((s, HEAD_PAD), F32), jax.ShapeDtypeStruct((1, SSD_INNER), F32),
                   jax.ShapeDtypeStruct((1, SSD_INNER), F32), jax.ShapeDtypeStruct((1, HEAD_PAD), F32),
                   jax.ShapeDtypeStruct((1, HEAD_PAD), F32), jax.ShapeDtypeStruct((CONV_W, CONV_DIM), F32),
                   jax.ShapeDtypeStruct((1, CONV_DIM), F32)],
        scratch_shapes=[pltpu.VMEM((SSD_N, SSD_INNER), F32), pltpu.VMEM((SSD_ROWS, CONV_DIM), F32), pltpu.VMEM((8, CONV_DIM), F32)],
        compiler_params=_params(),
    )(dy, ypre, z, c, xraw, misc, prev, cw, dtb, a_exp, d_exp, nw, emisc, emisc_t, tri, trit)


def _qkv_bwd(dq, dk, dv, cq, ckv, qnw, kvnw, wuq, wkv, cosf, sinf):
    s = dq.shape[0]
    wide = MLA_HEADS * HEAD_PAD

    def body(dq_ref, dk_ref, dv_ref, cq_ref, ckv_ref, qnw_ref, kvnw_ref, wuq_ref, wkv_ref, cos_ref, sin_ref,
             dqb_ref, dkvb_ref, dcq_ref, dckv_ref, dmisc_ref, gq_ref, gkv_ref):
        first = pl.program_id(0) == 0
        cosf, sinf = cos_ref[...], sin_ref[...]
        dkr = jnp.zeros((TM, HEAD_PAD), F32)
        for hd in range(MLA_HEADS):
            cols = slice(hd * HEAD_PAD, (hd + 1) * HEAD_PAD)
            dqb_ref[:, cols] = _rope(dq_ref[:, cols], cosf, sinf, -1.0).astype(BF16)
            dkh = dk_ref[:, cols]
            dkvb_ref[:, cols] = dkh.astype(BF16)
            dkr = dkr + dkh
        dkvb_ref[:, wide:] = dv_ref[...].astype(BF16)
        lane = lax.broadcasted_iota(jnp.int32, dkr.shape, 1)
        in_rope = jnp.logical_and(lane >= MISC_ROPE, lane < MISC_ROPE + QK_ROPE)
        dmisc_ref[...] = jnp.where(in_rope, _rope(jnp.where(in_rope, dkr, 0.0), cosf, sinf, -1.0), 0.0)
        dcq, gq = _rms_bwd(cq_ref[...], qnw_ref[...], _dot_nt(dqb_ref[...], wuq_ref[...]))
        dcq_ref[...] = dcq
        _acc_rows(gq_ref, gq, first)
        dckv, gkv = _rms_bwd(ckv_ref[...], kvnw_ref[...], _dot_nt(dkvb_ref[...], wkv_ref[...]))
        dckv_ref[...] = dckv
        _acc_rows(gkv_ref, gkv, first)

    return pl.pallas_call(
        body, name="qkv_bwd", grid=(s // TM,),
        in_specs=[_rows(TM, wide)] * 3 + [_rows(TM, Q_RANK), _rows(TM, KV_RANK), _full((1, Q_RANK)), _full((1, KV_RANK)),
                                          _resident((Q_RANK, wide)), _resident((KV_RANK, 2 * wide)), _rows(TM, HEAD_PAD), _rows(TM, HEAD_PAD)],
        out_specs=[_rows(TM, wide), _rows(TM, 2 * wide), _rows(TM, Q_RANK), _rows(TM, KV_RANK), _rows(TM, HEAD_PAD),
                   _full((1, Q_RANK)), _full((1, KV_RANK))],
        out_shape=[jax.ShapeDtypeStruct((s, wide), BF16), jax.ShapeDtypeStruct((s, 2 * wide), BF16),
                   jax.ShapeDtypeStruct((s, Q_RANK), F32), jax.ShapeDtypeStruct((s, KV_RANK), F32),
                   jax.ShapeDtypeStruct((s, HEAD_PAD), F32), jax.ShapeDtypeStruct((1, Q_RANK), F32),
                   jax.ShapeDtypeStruct((1, KV_RANK), F32)],
        compiler_params=_params(),
    )(dq, dk, dv, cq, ckv, qnw, kvnw, wuq, wkv, cosf, sinf)


def _inproj_bwd(dcq, dckv, dmisc_rope, dmisc_dt, dz, dxbc, h, dh1, nw, win):
    s = h.shape[0]

    def body(dcq_ref, dckv_ref, dma_ref, dmb_ref, dz_ref, dxbc_ref, h_ref, dh1_ref, nw_ref, w_ref, dproj_ref, dh0_ref, gnw_ref):
        dproj_ref[:, 0:768] = dcq_ref[...].astype(BF16)
        dproj_ref[:, 768:1024] = dckv_ref[...].astype(BF16)
        dproj_ref[:, 1024:1152] = (dma_ref[...] + dmb_ref[...]).astype(BF16)
        dproj_ref[:, 1152:1664] = dz_ref[...].astype(BF16)
        dproj_ref[:, 1664:2688] = dxbc_ref[...].astype(BF16)
        du = _dot_nt(dproj_ref[...], w_ref[...])
        dx, gnw = _rms_bwd(h_ref[...], nw_ref[...], du)
        _acc_rows(gnw_ref, gnw, pl.program_id(0) == 0)
        dh0_ref[...] = dh1_ref[...] + dx

    return pl.pallas_call(
        body, name="inproj_bwd", grid=(s // TM,),
        in_specs=[_rows(TM, Q_RANK), _rows(TM, KV_RANK), _rows(TM, HEAD_PAD), _rows(TM, HEAD_PAD), _rows(TM, SSD_INNER),
                  _rows(TM, CONV_DIM), _rows(TM, D_MODEL), _rows(TM, D_MODEL), _full((1, D_MODEL)), _resident((D_MODEL, IN_PAD))],
        out_specs=[_rows(TM, IN_PAD), _rows(TM, D_MODEL), _full((1, D_MODEL))],
        out_shape=[jax.ShapeDtypeStruct((s, IN_PAD), BF16), jax.ShapeDtypeStruct((s, D_MODEL), F32),
                   jax.ShapeDtypeStruct((1, D_MODEL), F32)],
        compiler_params=_params(),
    )(dcq, dckv, dmisc_rope, dmisc_dt, dz, dxbc, h, dh1, nw, win)


def _row_tile(rows, cols):
    cap = max(8, (1 << 18) // max(cols, 128))
    best = None
    for t in range(8, rows + 1, 8):
        if rows % t == 0 and t <= cap:
            best = t
    return best if best is not None else rows


def _adamw(w, g, m, v, name):
    rows, cols = w.shape
    tr = _row_tile(rows, cols)

    def body(w_ref, g_ref, m_ref, v_ref, d_ref, m2_ref, v2_ref):
        gg = g_ref[...]
        m2 = ADAM_B1 * m_ref[...] + (1.0 - ADAM_B1) * gg
        v2 = ADAM_B2 * v_ref[...] + (1.0 - ADAM_B2) * jnp.square(gg)
        m_hat = m2 / (1.0 - ADAM_B1 ** ADAM_STEP)
        v_hat = v2 / (1.0 - ADAM_B2 ** ADAM_STEP)
        d_ref[...] = -ADAM_LR * (m_hat / (jnp.sqrt(v_hat) + ADAM_EPS) + ADAM_WD * w_ref[...])
        m2_ref[...] = m2
        v2_ref[...] = v2

    spec = pl.BlockSpec((tr, cols), lambda i: (i, 0))
    return pl.pallas_call(
        body, name=name, grid=(rows // tr,),
        in_specs=[spec] * 4, out_specs=[spec] * 3,
        out_shape=[jax.ShapeDtypeStruct((rows, cols), F32)] * 3,
    )(w, g, m, v)


def _sum_adamw(slots, w, m, v, name):
    _, rows, cols = w.shape
    tr = _row_tile(rows, cols)
    nb = rows // tr

    def body(s0_ref, s1_ref, w_ref, m_ref, v_ref, g_ref, d_ref, m2_ref, v2_ref):
        for l, ref in enumerate((s0_ref, s1_ref)):
            @pl.when(pl.program_id(0) == l)
            def _(ref=ref):
                acc = ref[0].astype(F32)
                for i in range(1, N_DEV):
                    acc = acc + ref[i].astype(F32)
                g_ref[...] = acc

        gg = g_ref[...]
        m2 = ADAM_B1 * m_ref[...] + (1.0 - ADAM_B1) * gg
        v2 = ADAM_B2 * v_ref[...] + (1.0 - ADAM_B2) * jnp.square(gg)
        m_hat = m2 / (1.0 - ADAM_B1 ** ADAM_STEP)
        v_hat = v2 / (1.0 - ADAM_B2 ** ADAM_STEP)
        d_ref[...] = -ADAM_LR * (m_hat / (jnp.sqrt(v_hat) + ADAM_EPS) + ADAM_WD * w_ref[...])
        m2_ref[...] = m2
        v2_ref[...] = v2

    slot_spec = lambda layer: pl.BlockSpec((N_DEV, tr, cols), lambda l, i: (0, jnp.where(l == layer, i, (nb - 1) * (1 - layer)), 0))
    spec = pl.BlockSpec((None, tr, cols), lambda l, i: (l, i, 0))
    return pl.pallas_call(
        body, name=name, grid=(DEPTH, nb),
        in_specs=[slot_spec(0), slot_spec(1), spec, spec, spec], out_specs=[spec] * 4,
        out_shape=[jax.ShapeDtypeStruct(w.shape, F32)] * 4,
        compiler_params=_params(),
    )(slots[0], slots[1], w, m, v)


_MESH = pl.DeviceIdType.MESH
_ANY = pl.BlockSpec(memory_space=pl.ANY)


def _my_place():
    return lax.axis_index("x"), lax.axis_index("y"), lax.axis_index("c")


def _flip(place, k):
    x, y, c = place
    return (1 - x if k & 4 else x, 1 - y if k & 2 else y, 1 - c if k & 1 else c)


def _block_id(place):
    return 4 * place[0] + 2 * place[1] + place[2]


def _peer_copies(kind, in_refs, out_refs, send_sems, recv_sems, local_sems):
    me = _my_place()
    my = _block_id(me)
    remote, local = [], []
    for a, (x_ref, out_ref) in enumerate(zip(in_refs, out_refs)):
        src_of = (lambda place, r=x_ref: r) if kind == "gather" else (lambda place, r=x_ref: r.at[_block_id(place)])
        local.append(pltpu.make_async_copy(src_of(me), out_ref.at[my], local_sems.at[a]))
        for k in range(1, N_DEV):
            peer = _flip(me, k)
            remote.append(pltpu.make_async_remote_copy(
                src_ref=src_of(peer), dst_ref=out_ref.at[my], send_sem=send_sems.at[a * 7 + k - 1],
                recv_sem=recv_sems.at[a * 7 + k - 1], device_id=peer, device_id_type=_MESH))
    return remote, local


def _comm_out_shapes(kind, arrays):
    return [jax.ShapeDtypeStruct((N_DEV, *a.shape) if kind == "gather" else a.shape, a.dtype) for a in arrays]


def _comm_scratch(n):
    return [pltpu.SemaphoreType.DMA((7 * n,)), pltpu.SemaphoreType.DMA((7 * n,)), pltpu.SemaphoreType.DMA((n,))]


def _hosted_comm(kind, in_refs, out_refs, sems, first, last):
    if not in_refs:
        return

    @pl.when(first)
    def _():
        remote, local = _peer_copies(kind, in_refs, out_refs, *sems)
        for cp in local + remote:
            cp.start()

    @pl.when(last)
    def _():
        remote, local = _peer_copies(kind, in_refs, out_refs, *sems)
        for cp in remote:
            cp.wait()
        for cp in local:
            cp.wait()


def _two_level_gather_steps(in_refs, out_refs, send_sems, recv_sems, local_sems):
    n = len(in_refs)
    me = _my_place()
    x, y, c = me
    sibling = (x, y, 1 - c)
    chips = [(1 - x, y), (x, 1 - y), (1 - x, 1 - y)]

    def copy(a, k, place, to, src=None):
        block = out_refs[a].at[_block_id(place)]
        return pltpu.make_async_remote_copy(
            src_ref=block if src is None else src, dst_ref=block, send_sem=send_sems.at[7 * a + k],
            recv_sem=recv_sems.at[7 * a + k], device_id=to, device_id_type=_MESH)

    mine = [pltpu.make_async_copy(in_refs[a], out_refs[a].at[_block_id(me)], local_sems.at[a]) for a in range(n)]
    first = [copy(a, 0, me, sibling, src=in_refs[a]) for a in range(n)]
    first += [copy(a, 1 + j, me, (*chip, c), src=in_refs[a]) for a in range(n) for j, chip in enumerate(chips)]
    passed = [copy(a, 4 + j, (*chip, c), sibling) for a in range(n) for j, chip in enumerate(chips)]

    def send():
        for cp in mine + first:
            cp.start()

    def forward():
        for a in range(n):
            for j, chip in enumerate(chips):
                copy(a, 1 + j, (*chip, c), me).wait_recv()
                passed[3 * a + j].start()

    def finish():
        for a in range(n):
            copy(a, 0, sibling, me).wait_recv()
            for j, chip in enumerate(chips):
                copy(a, 4 + j, (*chip, 1 - c), me).wait_recv()
        for cp in first + passed:
            cp.wait_send()
        for cp in mine:
            cp.wait()

    return send, forward, finish


def _gather_two_level(arrays, name):
    n = len(arrays)

    def body(*refs):
        for step in _two_level_gather_steps(refs[:n], refs[n:2 * n], *refs[2 * n:]):
            step()

    return pl.pallas_call(
        body, name=name, out_shape=_comm_out_shapes("gather", arrays),
        in_specs=[_ANY] * n, out_specs=[_ANY] * n, scratch_shapes=_comm_scratch(n),
    )(*arrays)


def _hosted_gather(in_refs, out_refs, sems, first, middle, last):
    if not in_refs:
        return
    for when, index in ((first, 0), (middle, 1), (last, 2)):
        @pl.when(when)
        def _(index=index):
            _two_level_gather_steps(in_refs, out_refs, *sems)[index]()


def _comm(kind, arrays, name):
    n = len(arrays)

    def body(*refs):
        remote, local = _peer_copies(kind, refs[:n], refs[n:2 * n], *refs[2 * n:])
        for cp in local + remote:
            cp.start()
        for cp in remote:
            cp.wait()
        for cp in local:
            cp.wait()

    return pl.pallas_call(
        body, name=name, out_shape=_comm_out_shapes(kind, arrays),
        in_specs=[_ANY] * n, out_specs=[_ANY] * n, scratch_shapes=_comm_scratch(n),
    )(*arrays)


def _all_reduce_small(part):
    rows, lanes = part.shape
    vmem = pl.BlockSpec(memory_space=pltpu.VMEM)

    def body(x_ref, gath_ref, sum_ref, send_sems, recv_sems):
        me = _my_place()
        my = _block_id(me)
        gath_ref[my] = x_ref[...]
        copies = []
        for k in range(1, N_DEV):
            cp = pltpu.make_async_remote_copy(
                src_ref=x_ref, dst_ref=gath_ref.at[my], send_sem=send_sems.at[k - 1], recv_sem=recv_sems.at[k - 1],
                device_id=_flip(me, k), device_id_type=_MESH)
            cp.start()
            copies.append(cp)
        for cp in copies:
            cp.wait()
        acc = gath_ref[0]
        for i in range(1, N_DEV):
            acc = acc + gath_ref[i]
        sum_ref[...] = acc

    return pl.pallas_call(
        body, name="small_grad_all_reduce",
        out_shape=[jax.ShapeDtypeStruct((N_DEV, rows, lanes), F32), jax.ShapeDtypeStruct((rows, lanes), F32)],
        in_specs=[vmem], out_specs=[vmem, vmem],
        scratch_shapes=[pltpu.SemaphoreType.DMA((7,)), pltpu.SemaphoreType.DMA((7,))],
    )(part)[1]


_SHARDED = (("w_in", (D_MODEL, IN_PROJ // N_DEV)), ("w_uq", (Q_RANK // N_DEV, Q_RANK)), ("w_ukv", (KV_RANK, HEAD_PAD)),
            ("conv_w", (CONV_W, CONV_DIM // N_DEV)), ("w_out", (D_MODEL // N_DEV, D_MODEL)),
            ("w_up", (D_MODEL, D_FF // N_DEV)), ("w_down", (D_FF // N_DEV, D_MODEL)))
_SMALL = (("pre_mix_norm", D_MODEL), ("q_norm", Q_RANK), ("kv_norm", KV_RANK), ("conv_b", CONV_DIM), ("dt_bias", SSD_HEADS),
          ("a_log", SSD_HEADS), ("d_skip", SSD_HEADS), ("ssd_norm", SSD_INNER), ("post_mix_norm", D_MODEL),
          ("pre_mlp_norm", D_MODEL), ("post_mlp_norm", D_MODEL))
_WEIGHT_ORDER = ("pre_mix_norm", "w_in", "q_norm", "w_uq", "kv_norm", "w_ukv", "conv_w", "conv_b", "dt_bias", "a_log", "d_skip",
                 "ssd_norm", "w_out", "post_mix_norm", "pre_mlp_norm", "w_up", "w_down", "post_mlp_norm")
_EARLY = ("w_in", "w_uq", "w_ukv", "conv_w")
_LATE = ("w_out", "w_up", "w_down")


def _wire_shard(name, a):
    return lax.bitcast_convert_type(a, BF16).reshape(CONV_W, -1) if name == "conv_w" else a.astype(BF16)


def _from_wire(name, g):
    return lax.bitcast_convert_type(g.reshape(N_DEV, CONV_W, -1, 2), F32) if name == "conv_w" else g


def _cols(stacked):
    return jnp.transpose(stacked, (1, 0, 2)).reshape(stacked.shape[1], -1)


def _early_weights(sh):
    w_in = _cols(sh["w_in"])
    zeros = lambda n: jnp.zeros((D_MODEL, n), BF16)
    s1, s2, s3, s4, s5 = 768, 1024, 1056, 1568, 2592
    win = jnp.concatenate([w_in[:, :s2], zeros(MISC_ROPE), w_in[:, s2:s3], w_in[:, s5:], zeros(HEAD_PAD - MISC_DT - SSD_HEADS),
                           w_in[:, s3:s5]], axis=1)
    w_uq = sh["w_uq"].reshape(Q_RANK, MLA_HEADS, QK_NOPE + QK_ROPE)
    wuq = jnp.pad(w_uq, ((0, 0), (0, 0), (0, HEAD_PAD - QK_NOPE - QK_ROPE))).reshape(Q_RANK, -1)
    w_ukv = _cols(sh["w_ukv"]).reshape(KV_RANK, MLA_HEADS, QK_NOPE + V_DIM)
    wkn = jnp.pad(w_ukv[..., :QK_NOPE], ((0, 0), (0, 0), (0, HEAD_PAD - QK_NOPE))).reshape(KV_RANK, -1)
    wv = w_ukv[..., QK_NOPE:].reshape(KV_RANK, 4, 2, 1, V_DIM) * jnp.eye(2, dtype=BF16).reshape(1, 1, 2, 2, 1)
    wkv = jnp.concatenate([wkn, wv.reshape(KV_RANK, -1)], axis=1)
    return dict(win=win, wuq=wuq, wkv=wkv, conv_w=_cols(sh["conv_w"]))


def _late_weights(sh):
    w_out = sh["w_out"].reshape(D_MODEL, D_MODEL)
    watt = w_out[:SSD_INNER].reshape(4, 2, 1, V_DIM, D_MODEL) * jnp.eye(2, dtype=BF16).reshape(1, 2, 2, 1, 1)
    wout = jnp.concatenate([watt.reshape(MLA_HEADS * HEAD_PAD, D_MODEL), w_out[SSD_INNER:]], axis=0)
    return dict(wout=wout, wup=sh["w_up"], wdown=sh["w_down"])


def _shard_grads(g):
    out = {}
    if "wup" in g:
        out["w_up"], out["w_down"] = g["wup"], g["wdown"]
        ae = g["wout_att"].reshape(4, 2, 2, V_DIM, D_MODEL)
        att = jnp.stack([ae[:, 0, 0], ae[:, 1, 1]], axis=1).reshape(SSD_INNER, D_MODEL)
        out["w_out"] = jnp.concatenate([att, g["wout_ssd"]], axis=0).astype(BF16).reshape(N_DEV, D_MODEL // N_DEV, D_MODEL)
    if "win" not in g:
        return out
    dwin = g["win"]
    s2 = Q_RANK + KV_RANK
    m0 = s2
    w_in = jnp.concatenate([dwin[:, :s2], dwin[:, m0 + MISC_ROPE:m0 + MISC_ROPE + QK_ROPE], dwin[:, 1152:2688],
                            dwin[:, m0 + MISC_DT:m0 + MISC_DT + SSD_HEADS]], axis=1)
    out["w_in"] = jnp.transpose(w_in.astype(BF16).reshape(D_MODEL, N_DEV, -1), (1, 0, 2))
    w_uq = g["wuq"].astype(BF16).reshape(Q_RANK, MLA_HEADS, HEAD_PAD)[..., :QK_NOPE + QK_ROPE].reshape(Q_RANK, Q_RANK)
    out["w_uq"] = w_uq.reshape(N_DEV, Q_RANK // N_DEV, Q_RANK)
    wide = MLA_HEADS * HEAD_PAD
    wkv = g["wkv"].astype(BF16)
    kn = wkv[:, :wide].reshape(KV_RANK, MLA_HEADS, HEAD_PAD)[..., :QK_NOPE]
    ve = wkv[:, wide:].reshape(KV_RANK, 4, 2, 2, V_DIM)
    vv = jnp.stack([ve[:, :, 0, 0], ve[:, :, 1, 1]], axis=2).reshape(KV_RANK, MLA_HEADS, V_DIM)
    out["w_ukv"] = jnp.transpose(jnp.concatenate([kn, vv], axis=-1), (1, 0, 2))
    out["conv_w"] = jnp.transpose(g["conv_w"].astype(BF16).reshape(CONV_W, N_DEV, -1), (1, 0, 2))
    return out


def _small_rows(n):
    return -(-n // 1024) * 8


def _pack_small(vals):
    rows = []
    for l in range(DEPTH):
        for name, n in _SMALL:
            r = _small_rows(n)
            rows.append(jnp.pad(vals[name][l].reshape(-1), (0, r * 128 - n)).reshape(r, 128))
    return jnp.concatenate(rows, axis=0)


def _unpack_small(packed):
    out, off = {name: [] for name, _ in _SMALL}, 0
    for l in range(DEPTH):
        for name, n in _SMALL:
            r = _small_rows(n)
            out[name].append(packed[off:off + r].reshape(-1)[:n])
            off += r
    return {name: jnp.stack(v) for name, v in out.items()}


def _lane_rows(vec8):
    return jnp.repeat(vec8, SSD_P).reshape(1, SSD_INNER)


def _layer_fwd(h, kw, sm, l, cosf, sinf, consts, gather=(), after_gather=None, target=None):
    row = lambda name: sm[name][l].reshape(1, -1)
    t = {}
    t["h0"] = h
    t["ub"], t["cq"], t["ckv"], t["misc"], t["z"], t["xraw"] = _inproj_fwd(h, row("pre_mix_norm"), kw["win"])
    t["cqn"], t["ckvn"], t["q"], t["k"], t["v"] = _qkv_fwd(t["cq"], t["ckv"], t["misc"], row("q_norm"), row("kv_norm"),
                                                         kw["wuq"], kw["wkv"], cosf, sinf)
    t["oe"], t["lse"], gathered = _attn_fwd(t["q"], t["k"], t["v"], gather)
    if after_gather is not None:
        after_gather(gathered)
    t["dtb"] = _lane_rows(sm["dt_bias"][l])
    t["a_exp"] = _lane_rows(-jnp.exp(sm["a_log"][l]))
    t["d_exp"] = _lane_rows(sm["d_skip"][l])
    t["c"], t["prev"], t["ypre"], t["yssd"] = _ssd_fwd(t["xraw"], t["misc"], t["z"], kw["conv_w"], row("conv_b"), t["dtb"],
                                                     t["a_exp"], t["d_exp"], row("ssd_norm"), consts)
    t["mixed"], t["h1"] = _outproj_fwd(t["oe"], t["yssd"], kw["wout"], h, row("post_mix_norm"))
    t["mb"], t["ab"], t["d"], *out = _mlp_fwd(t["h1"], row("pre_mlp_norm"), kw["wup"], kw["wdown"], row("post_mlp_norm"), target)
    return out, t


def _layer_bwd(dh2, t, kw, sm, l, cosf, sinf, consts, exchange_of=None):
    row = lambda name: sm[name][l].reshape(1, -1)
    g, gs = {}, {}
    dh1, dab, ddb, gs["post_mlp_norm"], gs["pre_mlp_norm"] = _mlp_bwd(
        dh2, t["d"], t["h1"], t["ab"], row("pre_mlp_norm"), kw["wup"], kw["wdown"], row("post_mlp_norm"))
    g["wup"] = _matmul_tn_stacked(t["mb"], dab, f"dw_up_{l}", a_stacked=False)
    g["wdown"] = _matmul_tn_stacked(t["ab"], ddb, f"dw_down_{l}", a_stacked=True, square_a=True)
    dmixb, doe, dyssd, gs["post_mix_norm"], delta = _outproj_bwd(dh1, t["mixed"], row("post_mix_norm"), kw["wout"], t["oe"])
    g["wout_att"] = _matmul_tn(t["oe"], dmixb, f"dw_out_att_{l}")
    g["wout_ssd"] = _matmul_tn(t["yssd"], dmixb, f"dw_out_ssd_{l}")
    dz, dxraw, dmisc_dt, gs["ssd_norm"], gd, galog, gdtb, g["conv_w"], gs["conv_b"] = _ssd_bwd(
        dyssd, t["ypre"], t["z"], t["c"], t["xraw"], t["misc"], t["prev"], kw["conv_w"], t["dtb"], t["a_exp"], t["d_exp"],
        row("ssd_norm"), consts)
    gs["d_skip"] = jnp.sum(gd.reshape(SSD_HEADS, SSD_P), axis=1)
    gs["a_log"] = galog[0, MISC_DT:MISC_DT + SSD_HEADS]
    gs["dt_bias"] = gdtb[0, MISC_DT:MISC_DT + SSD_HEADS]
    dq, dk, dv, exchanged = _attn_bwd(t["q"], t["k"], t["v"], doe, t["lse"], delta,
                                      exchange_of(g) if exchange_of is not None else ())
    dqb, dkvb, dcq, dckv, dmisc_rope, gs["q_norm"], gs["kv_norm"] = _qkv_bwd(
        dq, dk, dv, t["cq"], t["ckv"], row("q_norm"), row("kv_norm"), kw["wuq"], kw["wkv"], cosf, sinf)
    g["wuq"] = _matmul_tn(t["cqn"], dqb, f"dw_uq_{l}")
    g["wkv"] = _matmul_tn(t["ckvn"], dkvb, f"dw_kv_{l}")
    dprojb, dh0, gs["pre_mix_norm"] = _inproj_bwd(dcq, dckv, dmisc_rope, dmisc_dt, dz, dxraw, t["h0"], dh1,
                                                  row("pre_mix_norm"), kw["win"])
    g["win"] = _matmul_tn(t["ub"], dprojb, f"dw_in_{l}")
    return dh0, g, {k: v.reshape(-1) for k, v in gs.items()}, exchanged


def _local_step(x, positions, kws, sm, target, gather=(), after_gather=None, exchange_of=None):
    inv_freq = ROPE_THETA ** (-jnp.arange(0, QK_ROPE, 2, dtype=F32) / QK_ROPE)
    invf = jnp.zeros((HEAD_PAD,), F32).at[MISC_ROPE:MISC_ROPE + QK_ROPE].set(jnp.concatenate([inv_freq, inv_freq]))
    cosf, sinf = _rope_tables(positions.reshape(-1, 1), invf.reshape(1, HEAD_PAD))
    consts = _ssd_consts()
    (h,), t0 = _layer_fwd(x, kws[0], sm, 0, cosf, sinf, consts, gather, after_gather)
    (dh, loss), t1 = _layer_fwd(h, kws[1], sm, 1, cosf, sinf, consts, target=target)
    saved = [t0, t1]
    grads, small, exchanged = [None] * DEPTH, [None] * DEPTH, []
    for l in reversed(range(DEPTH)):
        hook = (lambda g0: exchange_of(g0, grads[1])) if (l == 0 and exchange_of is not None) else None
        dh, grads[l], small[l], got = _layer_bwd(dh, saved[l], kws[l], sm, l, cosf, sinf, consts, hook)
        exchanged = got or exchanged
    return loss[0, 0], dh, grads, small, exchanged


def kernel(x, positions, pre_mix_norm, w_in, q_norm, w_uq, kv_norm, w_ukv, conv_w, conv_b, dt_bias, a_log, d_skip, ssd_norm, w_out, post_mix_norm, pre_mlp_norm, w_up, w_down, post_mlp_norm, loss_target, m_pre_mix_norm, m_w_in, m_q_norm, m_w_uq, m_kv_norm, m_w_ukv, m_conv_w, m_conv_b, m_dt_bias, m_a_log, m_d_skip, m_ssd_norm, m_w_out, m_post_mix_norm, m_pre_mlp_norm, m_w_up, m_w_down, m_post_mlp_norm, v_pre_mix_norm, v_w_in, v_q_norm, v_w_uq, v_kv_norm, v_w_ukv, v_conv_w, v_conv_b, v_dt_bias, v_a_log, v_d_skip, v_ssd_norm, v_w_out, v_post_mix_norm, v_pre_mlp_norm, v_w_up, v_w_down, v_post_mlp_norm):
    w = dict(pre_mix_norm=pre_mix_norm, w_in=w_in, q_norm=q_norm, w_uq=w_uq, kv_norm=kv_norm, w_ukv=w_ukv, conv_w=conv_w,
             conv_b=conv_b, dt_bias=dt_bias, a_log=a_log, d_skip=d_skip, ssd_norm=ssd_norm, w_out=w_out,
             post_mix_norm=post_mix_norm, pre_mlp_norm=pre_mlp_norm, w_up=w_up, w_down=w_down, post_mlp_norm=post_mlp_norm)
    m = dict(pre_mix_norm=m_pre_mix_norm, w_in=m_w_in, q_norm=m_q_norm, w_uq=m_w_uq, kv_norm=m_kv_norm, w_ukv=m_w_ukv,
             conv_w=m_conv_w, conv_b=m_conv_b, dt_bias=m_dt_bias, a_log=m_a_log, d_skip=m_d_skip, ssd_norm=m_ssd_norm,
             w_out=m_w_out, post_mix_norm=m_post_mix_norm, pre_mlp_norm=m_pre_mlp_norm, w_up=m_w_up, w_down=m_w_down,
             post_mlp_norm=m_post_mlp_norm)
    v = dict(pre_mix_norm=v_pre_mix_norm, w_in=v_w_in, q_norm=v_q_norm, w_uq=v_w_uq, kv_norm=v_kv_norm, w_ukv=v_w_ukv,
             conv_w=v_conv_w, conv_b=v_conv_b, dt_bias=v_dt_bias, a_log=v_a_log, d_skip=v_d_skip, ssd_norm=v_ssd_norm,
             w_out=v_w_out, post_mix_norm=v_post_mix_norm, pre_mlp_norm=v_pre_mlp_norm, w_up=v_w_up, w_down=v_w_down,
             post_mlp_norm=v_post_mlp_norm)
    sm = {name: w[name] for name, _ in _SMALL}

    wire = lambda name, l: _wire_shard(name, w[name][l])
    first = _gather_two_level([wire(name, 0) for name in _EARLY], "weight_gather_first")
    kws = [_early_weights({name: _from_wire(name, a) for name, a in zip(_EARLY, first)}), None]
    behind = [(name, 0) for name in _LATE] + [(name, 1) for name, _ in _SHARDED]

    def after_gather(gathered):
        got = {key: _from_wire(key[0], a) for key, a in zip(behind, gathered)}
        kws[0].update(_late_weights({name: got[name, 0] for name in _LATE}))
        kws[1] = {**_early_weights({name: got[name, 1] for name in _EARLY}),
                  **_late_weights({name: got[name, 1] for name in _LATE})}

    sent_behind = [(name, 1) for name, _ in _SHARDED] + [(name, 0) for name in _LATE]

    def exchange_of(g0, g1):
        blocks = {**{(name, 1): a for name, a in _shard_grads(g1).items()},
                  **{(name, 0): a for name, a in _shard_grads(g0).items()}}
        return [blocks[key] for key in sent_behind]

    loss_part, dx, grads, small, exchanged = _local_step(
        x[0], positions[0], kws, sm, loss_target[0], [wire(*key) for key in behind], after_gather, exchange_of)
    slots = dict(zip(sent_behind, exchanged))
    last = _shard_grads({k: grads[0][k] for k in ("win", "wuq", "wkv", "conv_w")})
    slots.update({(name, 0): a for name, a in zip(_EARLY, _comm("exchange", [last[name] for name in _EARLY], "grad_exchange_last"))})
    g_small = _unpack_small(_all_reduce_small(_pack_small({name: jnp.stack([small[l][name] for l in range(DEPTH)])
                                                           for name, _ in _SMALL})))
    loss = lax.psum(loss_part, ("x", "y", "c"))

    grad, delta, new_m, new_v = {}, {}, {}, {}
    for name, _ in _SHARDED:
        grad[name], delta[name], new_m[name], new_v[name] = _sum_adamw(
            [slots[name, 0], slots[name, 1]], w[name], m[name], v[name], f"sum_adamw_{name}")
    pk = lambda d: _pack_small({name: d[name] for name, _ in _SMALL})
    d_, m_, v_ = _adamw(pk(w), pk(g_small), pk(m), pk(v), "adamw_small")
    for dst, packed in ((delta, d_), (new_m, m_), (new_v, v_)):
        dst.update(_unpack_small(packed))
    grad.update(g_small)

    outs = [loss, dx[None]]
    for d in (grad, delta, new_m, new_v):
        outs += [d[name] for name in _WEIGHT_ORDER]
    return tuple(outs)
```

```python
import jax
import jax.numpy as jnp
import numpy as np
from jax import lax
from jax.experimental import pallas as pl
from jax.experimental.pallas import tpu as pltpu

F32 = jnp.float32
BF16 = jnp.bfloat16
HI = lax.Precision.HIGHEST

D_MODEL = 1024
DEPTH = 2
N_DEV = 8
CHUNK = 64
EPS = 1e-6
MLA_HEADS = 8
QK_NOPE = 64
QK_ROPE = 32
V_DIM = 64
Q_RANK = 768
KV_RANK = 256
ROPE_THETA = 10000.0
SSD_HEADS = 8
SSD_P = 64
SSD_INNER = 512
SSD_GROUPS = 2
SSD_N = 128
CONV_W = 4
CONV_DIM = 1024
D_FF = 4096
IN_PROJ = 2600
HEAD_PAD = 128
IN_PAD = 2688
MISC_ROPE = 64
MISC_DT = 96
ATT_SCALE = (QK_NOPE + QK_ROPE) ** -0.5
LOG2E = 1.4426950408889634
ATT_SCALE_LOG2 = ATT_SCALE * LOG2E

ADAM_LR = 0.001
ADAM_B1 = 0.9
ADAM_B2 = 0.999
ADAM_EPS = 1e-08
ADAM_WD = 0.01
ADAM_STEP = 10

TM = 512
ATT_T = 512
ATT_G = 8
ATT_UNROLL = 4
SSD_ROWS = 256
TK_DW = 4096
VMEM_LIMIT = 56 * 1024 * 1024

_NT = (((1,), (1,)), ((), ()))
_TN = (((0,), (0,)), ((), ()))


def _params(**kw):
    return pltpu.CompilerParams(vmem_limit_bytes=VMEM_LIMIT, **kw)


def _dot(a, b, precision=None):
    return jnp.dot(a, b, preferred_element_type=F32, precision=precision)


def _dot_nt(a, b, precision=None):
    return lax.dot_general(a, b, _NT, preferred_element_type=F32, precision=precision)


def _dot_tn(a, b, precision=None):
    return lax.dot_general(a, b, _TN, preferred_element_type=F32, precision=precision)


def _split3(x):
    hi = x.astype(BF16)
    r = x - hi.astype(F32)
    mid = r.astype(BF16)
    return hi, mid, (r - mid.astype(F32)).astype(BF16)


def _dot01(x, m01, dot=_dot, left=False):
    parts = [dot(m01, p) if left else dot(p, m01) for p in _split3(x)]
    return parts[0] + parts[1] + parts[2]


def _full(shape):
    n = len(shape)
    return pl.BlockSpec(shape, lambda *_: (0,) * n)


def _resident(shape):
    n = len(shape)
    return pl.BlockSpec(shape, lambda *_: (0,) * n, pipeline_mode=pl.Buffered(1))


def _rows(tm, width):
    return pl.BlockSpec((tm, width), lambda i: (i, 0))


def _rms_fwd(x, w):
    r = lax.rsqrt(jnp.mean(x * x, axis=-1, keepdims=True) + EPS)
    return (x * r) * w


def _rms_bwd(x, w, dy):
    r = lax.rsqrt(jnp.mean(x * x, axis=-1, keepdims=True) + EPS)
    xh = x * r
    dxn = dy * w
    dx = r * (dxn - xh * jnp.mean(dxn * xh, axis=-1, keepdims=True))
    return dx, dy * xh


def _acc_rows(ref, val, first):
    s = jnp.sum(val, axis=0, keepdims=True)

    @pl.when(first)
    def _():
        ref[...] = s

    @pl.when(jnp.logical_not(first))
    def _():
        ref[...] += s


def _rope(t, cosf, sinf, sign):
    lane = lax.broadcasted_iota(jnp.int32, t.shape, 1)
    rot = jnp.where(lane < MISC_ROPE + QK_ROPE // 2, -pltpu.roll(t, HEAD_PAD - QK_ROPE // 2, 1), pltpu.roll(t, QK_ROPE // 2, 1))
    return t * cosf + sign * (rot * sinf)


def _rope_tables(pos, invf):
    s = pos.shape[0]

    def body(pos_ref, invf_ref, cos_ref, sin_ref):
        ang = pos_ref[...].astype(F32) * invf_ref[...]
        cos_ref[...] = jnp.cos(ang)
        sin_ref[...] = jnp.sin(ang)

    return pl.pallas_call(
        body, name="rope_tables", grid=(s // TM,),
        in_specs=[_rows(TM, 1), _full((1, HEAD_PAD))],
        out_specs=[_rows(TM, HEAD_PAD), _rows(TM, HEAD_PAD)],
        out_shape=[jax.ShapeDtypeStruct((s, HEAD_PAD), F32)] * 2,
    )(pos, invf)


def _inproj_fwd(h, nw, win):
    s = h.shape[0]

    def body(h_ref, nw_ref, w_ref, ub_ref, cq_ref, ckv_ref, misc_ref, z_ref, xbc_ref):
        ub = _rms_fwd(h_ref[...], nw_ref[...]).astype(BF16)
        ub_ref[...] = ub
        proj = _dot(ub, w_ref[...])
        cq_ref[...] = proj[:, 0:768]
        ckv_ref[...] = proj[:, 768:1024]
        misc_ref[...] = proj[:, 1024:1152]
        z_ref[...] = proj[:, 1152:1664]
        xbc_ref[...] = proj[:, 1664:2688]

    widths = (768, 256, 128, 512, 1024)
    return pl.pallas_call(
        body, name="inproj_fwd", grid=(s // TM,),
        in_specs=[_rows(TM, D_MODEL), _full((1, D_MODEL)), _resident((D_MODEL, IN_PAD))],
        out_specs=[_rows(TM, D_MODEL)] + [_rows(TM, w) for w in widths],
        out_shape=[jax.ShapeDtypeStruct((s, D_MODEL), BF16)] + [jax.ShapeDtypeStruct((s, w), F32) for w in widths],
        compiler_params=_params(),
    )(h, nw, win)


def _qkv_fwd(cq, ckv, misc, qnw, kvnw, wuq, wkv, cosf, sinf):
    s = cq.shape[0]

    def body(cq_ref, ckv_ref, misc_ref, qnw_ref, kvnw_ref, wuq_ref, wkv_ref, cos_ref, sin_ref,
             cqn_ref, ckvn_ref, q_ref, k_ref, v_ref):
        cosf, sinf = cos_ref[...], sin_ref[...]
        cqn = _rms_fwd(cq_ref[...], qnw_ref[...]).astype(BF16)
        cqn_ref[...] = cqn
        q = _dot(cqn, wuq_ref[...])
        ckvn = _rms_fwd(ckv_ref[...], kvnw_ref[...]).astype(BF16)
        ckvn_ref[...] = ckvn
        kv = _dot(ckvn, wkv_ref[...])
        m = misc_ref[...]
        lane = lax.broadcasted_iota(jnp.int32, m.shape, 1)
        in_rope = jnp.logical_and(lane >= MISC_ROPE, lane < MISC_ROPE + QK_ROPE)
        kr = jnp.where(in_rope, _rope(m, cosf, sinf, 1.0), 0.0)
        for hd in range(MLA_HEADS):
            cols = slice(hd * HEAD_PAD, (hd + 1) * HEAD_PAD)
            q_ref[:, cols] = _rope(q[:, cols], cosf, sinf, 1.0).astype(BF16)
            k_ref[:, cols] = (kv[:, cols] + kr).astype(BF16)
        vv = kv[:, MLA_HEADS * HEAD_PAD:]
        vlane = lax.broadcasted_iota(jnp.int32, vv.shape, 1)
        ones_at = jnp.where((vlane // HEAD_PAD) % 2 == 0, V_DIM, 0)
        v_ref[...] = jnp.where(vlane % HEAD_PAD == ones_at, 1.0, vv).astype(BF16)

    wide = MLA_HEADS * HEAD_PAD
    return pl.pallas_call(
        body, name="qkv_fwd", grid=(s // TM,),
        in_specs=[_rows(TM, Q_RANK), _rows(TM, KV_RANK), _rows(TM, HEAD_PAD), _full((1, Q_RANK)), _full((1, KV_RANK)),
                  _resident((Q_RANK, wide)), _resident((KV_RANK, 2 * wide)), _rows(TM, HEAD_PAD), _rows(TM, HEAD_PAD)],
        out_specs=[_rows(TM, Q_RANK), _rows(TM, KV_RANK), _rows(TM, wide), _rows(TM, wide), _rows(TM, wide)],
        out_shape=[jax.ShapeDtypeStruct((s, Q_RANK), BF16), jax.ShapeDtypeStruct((s, KV_RANK), BF16)]
        + [jax.ShapeDtypeStruct((s, wide), BF16)] * 3,
        compiler_params=_params(),
    )(cq, ckv, misc, qnw, kvnw, wuq, wkv, cosf, sinf)


def _chunk_bias(t, keys_on_rows=False):
    row = lax.broadcasted_iota(jnp.int32, (t, 1), 0) // CHUNK
    col = lax.broadcasted_iota(jnp.int32, (1, t), 1) // CHUNK
    return jnp.where((row <= col) if keys_on_rows else (col <= row), 0.0, -jnp.inf).astype(F32)


def _attn_fwd(q, k, v, gather=()):
    s = q.shape[0]
    t = ATT_T
    nq = s // t
    pair = ATT_G * HEAD_PAD
    ng = len(gather)

    def body(q_ref, k_ref, v_ref, *rest):
        g_in, (o_ref, lse_ref), g_out = rest[:ng], rest[ng:ng + 2], rest[ng + 2:2 * ng + 2]
        m_s, acc_s, bias_s = rest[2 * ng + 2:2 * ng + 5]
        qi = pl.program_id(1)
        group, groups = pl.program_id(0), MLA_HEADS // ATT_G

        @pl.when(jnp.logical_and(group == 0, qi == 0))
        def _():
            bias_s[...] = _chunk_bias(t)

        _hosted_gather(g_in, g_out, rest[2 * ng + 5:],
                       jnp.logical_and(group == 0, qi == 0),
                       jnp.logical_and(group == groups - 1, qi == min(3 * nq // 4 + 1, nq - 1)),
                       jnp.logical_and(group == groups - 1, qi == nq - 1))
        m_s[...] = jnp.full(m_s.shape, -jnp.inf, F32)
        acc_s[...] = jnp.zeros(acc_s.shape, F32)

        def step(kb, masked):
            r0 = pl.multiple_of(kb * t, t)

            def scores(hh):
                cols = slice(hh * HEAD_PAD, (hh + 1) * HEAD_PAD)
                return _dot_nt(q_ref[:, cols], k_ref[pl.ds(r0, t), cols])

            def soft(hh, raw):
                sc = raw * ATT_SCALE_LOG2
                if masked:
                    sc = sc + bias_s[...]
                m_old = m_s[hh]
                m_new = jnp.maximum(m_old, jnp.max(sc, axis=-1, keepdims=True))
                alpha = jnp.exp2(m_old - m_new)
                p = jnp.exp2(sc - jnp.tile(m_new, (1, t // HEAD_PAD)))
                m_s[hh] = m_new
                return alpha, p.astype(BF16)

            def update(hh, alpha, p):
                cols = slice(hh * HEAD_PAD, (hh + 1) * HEAD_PAD)
                acc_s[hh] = alpha * acc_s[hh] + _dot(p, v_ref[pl.ds(r0, t), cols])

            raw, ap = [None] * ATT_G, [None] * ATT_G
            raw[0] = scores(0)
            for hh in range(ATT_G):
                if hh + 1 < ATT_G:
                    raw[hh + 1] = scores(hh + 1)
                ap[hh] = soft(hh, raw[hh])
                if hh >= 1:
                    update(hh - 1, *ap[hh - 1])
            update(ATT_G - 1, *ap[ATT_G - 1])

        def loop(i, c):
            step(2 * i, False)
            step(2 * i + 1, False)
            return c

        lax.fori_loop(0, qi // 2, loop, 0)

        @pl.when(qi % 2 == 1)
        def _():
            step(qi - 1, False)

        step(qi, True)
        for hh in range(ATT_G):
            cols = slice(hh * HEAD_PAD, (hh + 1) * HEAD_PAD)
            acc = acc_s[hh]
            ones_at = V_DIM * (1 - hh % 2)
            l = jnp.broadcast_to(acc[:, ones_at:ones_at + 1], acc.shape)
            o_ref[:, cols] = (acc / l).astype(BF16)
            lse_ref[hh] = (m_s[hh] + jnp.log(l) * LOG2E).T[0:8, :]

    outs = pl.pallas_call(
        body, name="attn_fwd_gather" if ng else "attn_fwd", grid=(MLA_HEADS // ATT_G, nq),
        in_specs=[pl.BlockSpec((t, pair), lambda h, i: (i, h)),
                  pl.BlockSpec((s, pair), lambda h, i: (0, h), pipeline_mode=pl.Buffered(1)),
                  pl.BlockSpec((s, pair), lambda h, i: (0, h), pipeline_mode=pl.Buffered(1))] + [_ANY] * ng,
        out_specs=[pl.BlockSpec((t, pair), lambda h, i: (i, h)),
                   pl.BlockSpec((ATT_G, 8, t), lambda h, i: (h, 0, i))] + [_ANY] * ng,
        out_shape=[jax.ShapeDtypeStruct((s, MLA_HEADS * HEAD_PAD), BF16), jax.ShapeDtypeStruct((MLA_HEADS, 8, s), F32)]
        + _comm_out_shapes("gather", gather),
        scratch_shapes=[pltpu.VMEM((ATT_G, t, HEAD_PAD), F32), pltpu.VMEM((ATT_G, t, HEAD_PAD), F32), pltpu.VMEM((t, t), F32)]
        + (_comm_scratch(ng) if ng else []),
        compiler_params=_params(),
    )(q, k, v, *gather)
    return outs[0], outs[1], list(outs[2:])


def _interleave(stages):
    live = list(stages)
    while live:
        still = []
        for g in live:
            try:
                next(g)
                still.append(g)
            except StopIteration:
                pass
        live = still


def _ssd_consts():
    emisc = np.zeros((HEAD_PAD, SSD_INNER), np.float32)
    for hd in range(SSD_HEADS):
        emisc[MISC_DT + hd, hd * SSD_P:(hd + 1) * SSD_P] = 1.0
    idx = np.arange(CHUNK)
    tri = (idx[:, None] >= idx[None, :]).astype(np.float32)
    return tuple(jnp.asarray(m, BF16) for m in (emisc, emisc.T.copy(), tri, tri.T.copy()))


def _ssd_chunk_common(cc, misc, emisc, tri, trit, dtb, a_exp):
    sig = jax.nn.sigmoid(cc)
    xa = cc * sig
    dt = jax.nn.softplus(_dot01(misc, emisc) + dtb)
    a = dt * a_exp
    acs = _dot01(a, tri, left=True)
    acs_t = _dot01(a, trit, dot=_dot_tn)
    alast = acs[CHUNK - 1:CHUNK, :]
    return xa, sig, dt, acs, acs_t, alast


def _decay(acs, acs_t, hd):
    row = lax.broadcasted_iota(jnp.int32, (CHUNK, CHUNK), 0)
    col = lax.broadcasted_iota(jnp.int32, (CHUNK, CHUNK), 1)
    diff = acs[:, hd * SSD_P:hd * SSD_P + 1] - acs_t[hd * SSD_P:hd * SSD_P + 1, :]
    return jnp.exp(jnp.where(row >= col, diff, -jnp.inf))


def _half_mask(hh):
    lane = lax.broadcasted_iota(jnp.int32, (CHUNK, 2 * SSD_P), 1)
    return (lane >= SSD_P) if hh else (lane < SSD_P)


def _gate_norm(y, zz):
    sg = jax.nn.sigmoid(zz)
    yz = y * (zz * sg)
    outs, rs = [], []
    half = SSD_INNER // SSD_GROUPS
    for g in range(SSD_GROUPS):
        yg = yz[:, g * half:(g + 1) * half]
        r = lax.rsqrt(jnp.mean(yg * yg, axis=-1, keepdims=True) + EPS)
        outs.append(yg * r)
        rs.append(r)
    return sg, jnp.concatenate(outs, axis=1), rs


def _ssd_fwd(xraw, misc, z, cw, cb, dtb, a_exp, d_exp, nw, consts):
    s = xraw.shape[0]
    nb = s // SSD_ROWS
    ncb = SSD_ROWS // CHUNK
    emisc, _, tri, trit = consts

    def body(x_ref, misc_ref, z_ref, cw_ref, cb_ref, dtb_ref, a_ref, d_ref, nw_ref, emisc_ref, tri_ref, trit_ref,
             c_ref, prev_ref, ypre_ref, yssd_ref, tail_s, state_s):
        i = pl.program_id(0)

        @pl.when(i == 0)
        def _():
            tail_s[...] = jnp.zeros(tail_s.shape, F32)
            state_s[...] = jnp.zeros(state_s.shape, F32)

        x = x_ref[...]
        xext = jnp.concatenate([tail_s[...], x], axis=0)
        acc = x * cw_ref[CONV_W - 1:CONV_W, :] + cb_ref[...]
        for j in range(1, CONV_W):
            acc = acc + pltpu.roll(xext, j, 0)[8:, :] * cw_ref[CONV_W - 1 - j:CONV_W - j, :]
        tail_s[...] = x[SSD_ROWS - 8:, :]
        c_ref[...] = acc

        def chunk(ci):
            r0 = ci * CHUNK
            xa, _, dt, acs, acs_t, alast = _ssd_chunk_common(
                c_ref[pl.ds(r0, CHUNK), :], misc_ref[pl.ds(r0, CHUNK), :], emisc_ref[...], tri_ref[...], trit_ref[...],
                dtb_ref[...], a_ref[...])
            yield
            xs = xa[:, :SSD_INNER]
            xdt = xs * dt
            wgt = (xdt * jnp.exp(alast - acs)).astype(BF16)
            e = jnp.exp(acs)
            ys, new_states, cms = [], [], []
            for g in range(SSD_GROUPS):
                bm = xa[:, SSD_INNER + g * SSD_N:SSD_INNER + (g + 1) * SSD_N].astype(BF16)
                cm = xa[:, SSD_INNER + SSD_GROUPS * SSD_N + g * SSD_N:SSD_INNER + SSD_GROUPS * SSD_N + (g + 1) * SSD_N].astype(BF16)
                cms.append(cm)
                cb_g = _dot_nt(cm, bm)
                gl = slice(g * 256, (g + 1) * 256)
                new_states.append(_dot_tn(bm, wgt[:, gl]))
                for jj in range(2):
                    pair = 2 * g + jj
                    xp = xdt[:, pair * 128:(pair + 1) * 128]
                    yp = None
                    for hh in range(2):
                        sc = (cb_g * _decay(acs, acs_t, 2 * pair + hh)).astype(BF16)
                        term = _dot(sc, jnp.where(_half_mask(hh), xp, 0.0).astype(BF16))
                        yp = term if yp is None else yp + term
                    ys.append(yp)
                yield
            prev = state_s[...]
            prev_ref[ci] = prev
            yoff = jnp.concatenate([_dot(cms[g], prev[:, g * 256:(g + 1) * 256].astype(BF16)) for g in range(SSD_GROUPS)],
                                   axis=1) * e
            state_s[...] = prev * jnp.exp(alast) + jnp.concatenate(new_states, axis=1)
            yield
            y = jnp.concatenate(ys, axis=1) + yoff + d_ref[...] * xs
            ypre_ref[pl.ds(r0, CHUNK), :] = y
            _, yn, _ = _gate_norm(y, z_ref[pl.ds(r0, CHUNK), :])
            yssd_ref[pl.ds(r0, CHUNK), :] = (yn * nw_ref[...]).astype(BF16)

        _interleave([chunk(ci) for ci in range(ncb)])

    return pl.pallas_call(
        body, name="ssd_fwd", grid=(nb,),
        in_specs=[_rows(SSD_ROWS, CONV_DIM), _rows(SSD_ROWS, HEAD_PAD), _rows(SSD_ROWS, SSD_INNER),
                  _full((CONV_W, CONV_DIM)), _full((1, CONV_DIM)), _full((1, SSD_INNER)), _full((1, SSD_INNER)),
                  _full((1, SSD_INNER)), _full((1, SSD_INNER)), _full((HEAD_PAD, SSD_INNER)), _full((CHUNK, CHUNK)),
                  _full((CHUNK, CHUNK))],
        out_specs=[_rows(SSD_ROWS, CONV_DIM), pl.BlockSpec((ncb, SSD_N, SSD_INNER), lambda i: (i, 0, 0)),
                   _rows(SSD_ROWS, SSD_INNER), _rows(SSD_ROWS, SSD_INNER)],
        out_shape=[jax.ShapeDtypeStruct((s, CONV_DIM), F32), jax.ShapeDtypeStruct((s // CHUNK, SSD_N, SSD_INNER), F32),
                   jax.ShapeDtypeStruct((s, SSD_INNER), F32), jax.ShapeDtypeStruct((s, SSD_INNER), BF16)],
        scratch_shapes=[pltpu.VMEM((8, CONV_DIM), F32), pltpu.VMEM((SSD_N, SSD_INNER), F32)],
        compiler_params=_params(),
    )(xraw, misc, z, cw, cb, dtb, a_exp, d_exp, nw, emisc, tri, trit)


def _outproj_fwd(oe, yssd, wout, h, nw):
    s = h.shape[0]
    wide = MLA_HEADS * HEAD_PAD

    def body(oe_ref, y_ref, w_ref, h_ref, nw_ref, mixed_ref, h1_ref):
        mixed = _dot(oe_ref[...], w_ref[0:wide, :]) + _dot(y_ref[...], w_ref[wide:, :])
        mixed_ref[...] = mixed
        h1_ref[...] = h_ref[...] + _rms_fwd(mixed, nw_ref[...])

    return pl.pallas_call(
        body, name="outproj_fwd", grid=(s // TM,),
        in_specs=[_rows(TM, wide), _rows(TM, SSD_INNER), _resident((wide + SSD_INNER, D_MODEL)), _rows(TM, D_MODEL),
                  _full((1, D_MODEL))],
        out_specs=[_rows(TM, D_MODEL), _rows(TM, D_MODEL)],
        out_shape=[jax.ShapeDtypeStruct((s, D_MODEL), F32)] * 2,
        compiler_params=_params(),
    )(oe, yssd, wout, h, nw)


def _mlp_fwd(h1, prew, wup, wdown, postw, target=None):
    s = h1.shape[0]
    fb = D_FF // N_DEV
    last = target is not None

    def body(h_ref, prew_ref, up_ref, down_ref, postw_ref, *rest):
        target_ref, (mb_ref, ab_ref, d_ref, out_ref) = (rest[0] if last else None), rest[last:last + 4]
        hh = h_ref[...]
        mb = _rms_fwd(hh, prew_ref[...]).astype(BF16)
        mb_ref[...] = mb
        d = jnp.zeros((TM, D_MODEL), F32)
        for j in range(N_DEV):
            a = jnp.maximum(_dot(mb, up_ref[j]), 0.0)
            ab_ref[j] = a.astype(BF16)
            d = d + _dot(jnp.square(a).astype(BF16), down_ref[j])
        d_ref[...] = d
        h2 = hh + _rms_fwd(d, postw_ref[...])
        if last:
            diff = h2 - target_ref[...]
            out_ref[...] = diff * (1.0 / D_MODEL)
            part = 0.5 * jnp.sum(jnp.mean(diff * diff, axis=-1, keepdims=True), axis=0, keepdims=True)
            _acc_rows(rest[-1], part, pl.program_id(0) == 0)
        else:
            out_ref[...] = h2

    stacked = pl.BlockSpec((N_DEV, TM, fb), lambda i: (0, i, 0))
    return pl.pallas_call(
        body, name="mlp_fwd_loss" if last else "mlp_fwd", grid=(s // TM,),
        in_specs=[_rows(TM, D_MODEL), _full((1, D_MODEL)), _resident((N_DEV, D_MODEL, fb)), _resident((N_DEV, fb, D_MODEL)),
                  _full((1, D_MODEL))] + ([_rows(TM, D_MODEL)] if last else []),
        out_specs=[_rows(TM, D_MODEL), stacked, _rows(TM, D_MODEL), _rows(TM, D_MODEL)] + ([_full((1, 1))] if last else []),
        out_shape=[jax.ShapeDtypeStruct((s, D_MODEL), BF16), jax.ShapeDtypeStruct((N_DEV, s, fb), BF16),
                   jax.ShapeDtypeStruct((s, D_MODEL), F32), jax.ShapeDtypeStruct((s, D_MODEL), F32)]
        + ([jax.ShapeDtypeStruct((1, 1), F32)] if last else []),
        compiler_params=_params(),
    )(h1, prew, wup, wdown, postw, *([target] if last else []))


def _mlp_bwd(dh2, d, h1, ab, prew, wup, wdown, postw):
    s = dh2.shape[0]
    fb = D_FF // N_DEV
    tm = TM // 2

    def body(dh2_ref, d_ref, h1_ref, ab_ref, prew_ref, up_ref, down_ref, postw_ref,
             dh1_ref, da_ref, dd_ref, gpost_ref, gpre_ref):
        first = pl.program_id(0) == 0
        dh2 = dh2_ref[...]
        dd, gpost = _rms_bwd(d_ref[...], postw_ref[...], dh2)
        _acc_rows(gpost_ref, gpost, first)
        ddb = dd.astype(BF16)
        dd_ref[...] = ddb

        def d_relu_squared(j):
            return _dot_nt(ddb, down_ref[j])

        def pointwise(j, dr):
            da = (dr * (2.0 * ab_ref[j].astype(F32))).astype(BF16)
            da_ref[j] = da
            return da

        dm = jnp.zeros((tm, D_MODEL), F32)
        nxt, da_prev = d_relu_squared(0), None
        for j in range(N_DEV):
            cur = nxt
            if j + 1 < N_DEV:
                nxt = d_relu_squared(j + 1)
            da = pointwise(j, cur)
            if da_prev is not None:
                dm = dm + _dot_nt(da_prev, up_ref[j - 1])
            da_prev = da
        dm = dm + _dot_nt(da_prev, up_ref[N_DEV - 1])
        dx, gpre = _rms_bwd(h1_ref[...], prew_ref[...], dm)
        _acc_rows(gpre_ref, gpre, first)
        dh1_ref[...] = dh2 + dx

    stacked = pl.BlockSpec((N_DEV, tm, fb), lambda i: (0, i, 0))
    return pl.pallas_call(
        body, name="mlp_bwd", grid=(s // tm,),
        in_specs=[_rows(tm, D_MODEL)] * 3 + [stacked, _full((1, D_MODEL)), _resident((N_DEV, D_MODEL, fb)),
                                              _resident((N_DEV, fb, D_MODEL)), _full((1, D_MODEL))],
        out_specs=[_rows(tm, D_MODEL), stacked, _rows(tm, D_MODEL), _full((1, D_MODEL)), _full((1, D_MODEL))],
        out_shape=[jax.ShapeDtypeStruct((s, D_MODEL), F32), jax.ShapeDtypeStruct((N_DEV, s, fb), BF16),
                   jax.ShapeDtypeStruct((s, D_MODEL), BF16), jax.ShapeDtypeStruct((1, D_MODEL), F32),
                   jax.ShapeDtypeStruct((1, D_MODEL), F32)],
        compiler_params=_params(),
    )(dh2, d, h1, ab, prew, wup, wdown, postw)


def _matmul_tn(a, b, name, tk=TK_DW):
    s, m = a.shape
    n = b.shape[1]
    tn = n if n <= 1024 else (n // 2 if (n // 2) % 128 == 0 else n // 3)
    tk = min(tk, s)
    assert n % tn == 0 and tn % 128 == 0 and s % tk == 0

    def body(a_ref, b_ref, o_ref):
        part = _dot_tn(a_ref[...], b_ref[...])

        @pl.when(pl.program_id(1) == 0)
        def _():
            o_ref[...] = part

        @pl.when(pl.program_id(1) != 0)
        def _():
            o_ref[...] += part

    return pl.pallas_call(
        body, name=name, grid=(n // tn, s // tk),
        in_specs=[pl.BlockSpec((tk, m), lambda j, k: (k, 0)), pl.BlockSpec((tk, tn), lambda j, k: (k, j))],
        out_specs=pl.BlockSpec((m, tn), lambda j, k: (0, j)),
        out_shape=jax.ShapeDtypeStruct((m, n), F32),
        compiler_params=_params(),
    )(a, b)


def _matmul_tn_stacked(a, b, name, a_stacked, square_a=False, tk=TK_DW):
    tk = min(tk, a.shape[-2])
    if a_stacked:
        _, s, m = a.shape
        n = b.shape[1]
        in_specs = [pl.BlockSpec((1, tk, m), lambda j, k: (j, k, 0)), pl.BlockSpec((tk, n), lambda j, k: (k, 0))]
    else:
        s, m = a.shape
        n = b.shape[2]
        in_specs = [pl.BlockSpec((tk, m), lambda j, k: (k, 0)), pl.BlockSpec((1, tk, n), lambda j, k: (j, k, 0))]

    nk = s // tk

    def body(a_ref, b_ref, o_ref, acc_s):
        av = a_ref[0] if a_stacked else a_ref[...]
        bv = b_ref[...] if a_stacked else b_ref[0]
        if square_a:
            av = jnp.square(av.astype(F32)).astype(BF16)
        part = _dot_tn(av, bv)
        k = pl.program_id(1)

        @pl.when(k == 0)
        def _():
            acc_s[...] = part

        @pl.when(jnp.logical_and(k != 0, k != nk - 1))
        def _():
            acc_s[...] += part

        @pl.when(k == nk - 1)
        def _():
            o_ref[0] = (part if nk == 1 else acc_s[...] + part).astype(BF16)

    return pl.pallas_call(
        body, name=name, grid=(N_DEV, nk),
        in_specs=in_specs,
        out_specs=pl.BlockSpec((1, m, n), lambda j, k: (j, 0, 0)),
        out_shape=jax.ShapeDtypeStruct((N_DEV, m, n), BF16),
        scratch_shapes=[pltpu.VMEM((m, n), F32)],
        compiler_params=_params(),
    )(a, b)


def _outproj_bwd(dh1, mixed, nw, wout, oe):
    s = dh1.shape[0]
    wide = MLA_HEADS * HEAD_PAD

    def body(dh1_ref, mixed_ref, nw_ref, w_ref, oe_ref, dmix_ref, doe_ref, dy_ref, gnw_ref, delta_ref):
        dmix, gnw = _rms_bwd(mixed_ref[...], nw_ref[...], dh1_ref[...])
        _acc_rows(gnw_ref, gnw, pl.program_id(0) == 0)
        dmb = dmix.astype(BF16)
        dmix_ref[...] = dmb
        doe_ref[...] = _dot_nt(dmb, w_ref[0:wide, :]).astype(BF16)
        dy_ref[...] = _dot_nt(dmb, w_ref[wide:, :])
        ones = jnp.ones((8, HEAD_PAD), BF16)
        for hd in range(MLA_HEADS):
            cols = slice(hd * HEAD_PAD, (hd + 1) * HEAD_PAD)
            prod = oe_ref[:, cols].astype(F32) * doe_ref[:, cols].astype(F32)
            delta_ref[hd] = _dot01(prod, ones, dot=_dot_nt, left=True)

    return pl.pallas_call(
        body, name="outproj_bwd", grid=(s // TM,),
        in_specs=[_rows(TM, D_MODEL), _rows(TM, D_MODEL), _full((1, D_MODEL)), _resident((wide + SSD_INNER, D_MODEL)),
                  _rows(TM, wide)],
        out_specs=[_rows(TM, D_MODEL), _rows(TM, wide), _rows(TM, SSD_INNER), _full((1, D_MODEL)),
                   pl.BlockSpec((MLA_HEADS, 8, TM), lambda i: (0, 0, i))],
        out_shape=[jax.ShapeDtypeStruct((s, D_MODEL), BF16), jax.ShapeDtypeStruct((s, wide), BF16),
                   jax.ShapeDtypeStruct((s, SSD_INNER), F32), jax.ShapeDtypeStruct((1, D_MODEL), F32),
                   jax.ShapeDtypeStruct((MLA_HEADS, 8, s), F32)],
        compiler_params=_params(),
    )(dh1, mixed, nw, wout, oe)


def _attn_bwd(q, k, v, do, lse, delta, exchange=()):
    s = q.shape[0]
    t = ATT_T
    nq = s // t
    pair = 2 * HEAD_PAD
    ne = len(exchange)

    def body(q_ref, k_ref, v_ref, do_ref, lse_ref, delta_ref, *rest):
        e_in, (dq_ref, dk_ref, dv_ref), e_out = rest[:ne], rest[ne:ne + 3], rest[ne + 3:2 * ne + 3]
        dk_s, dv_s, bias_s = rest[2 * ne + 3:2 * ne + 6]
        kb = pl.program_id(1)
        _hosted_comm("exchange", e_in, e_out, rest[2 * ne + 6:],
                     jnp.logical_and(pl.program_id(0) == 0, kb == 0),
                     jnp.logical_and(pl.program_id(0) == MLA_HEADS // 2 - 1, kb == nq - 1))

        @pl.when(jnp.logical_and(pl.program_id(0) == 0, kb == 0))
        def _():
            bias_s[...] = _chunk_bias(t, keys_on_rows=True)

        @pl.when(kb == 0)
        def _():
            dq_ref[...] = jnp.zeros(dq_ref.shape, F32)

        def step(qb, diagonal):
            r0 = pl.multiple_of(qb * t, t)
            for hh in range(2):
                cols = slice(hh * HEAD_PAD, (hh + 1) * HEAD_PAD)
                kk = k_ref[:, cols]
                qq = q_ref[pl.ds(r0, t), cols]
                dd = do_ref[pl.ds(r0, t), cols]
                sc = _dot_nt(kk, qq) * ATT_SCALE_LOG2
                if diagonal:
                    sc = sc + bias_s[...]
                p = jnp.exp2(sc - lse_ref[hh, 0:1, pl.ds(r0, t)])
                dv = _dot(p.astype(BF16), dd)
                dp = _dot_nt(v_ref[:, cols], dd)
                ds = (p * (dp - delta_ref[hh, 0:1, pl.ds(r0, t)]) * ATT_SCALE).astype(BF16)
                dk = _dot(ds, qq)
                if diagonal:
                    dv_s[:, cols] = dv
                    dk_s[:, cols] = dk
                else:
                    dv_s[:, cols] += dv
                    dk_s[:, cols] += dk
                dq_ref[pl.ds(r0, t), cols] += _dot_tn(ds, kk)

        def loop(i, c):
            for u in range(ATT_UNROLL):
                step(kb + 1 + u + ATT_UNROLL * i, False)
            return c

        step(kb, True)
        later_tiles = nq - 1 - kb
        lax.fori_loop(0, later_tiles // ATT_UNROLL, loop, 0)
        left = later_tiles % ATT_UNROLL
        for u in range(ATT_UNROLL - 1):
            @pl.when(left > u)
            def _(u=u):
                step(nq - left + u, False)

        dk_ref[...] = dk_s[...].astype(BF16)
        dv_ref[...] = dv_s[...].astype(BF16)

    whole = pl.BlockSpec((s, pair), lambda h, i: (0, h))
    tile = pl.BlockSpec((t, pair), lambda h, i: (i, h))
    rowvec = pl.BlockSpec((2, 8, s), lambda h, i: (h, 0, 0))
    wide = MLA_HEADS * HEAD_PAD
    outs = pl.pallas_call(
        body, name="attn_bwd_exchange" if ne else "attn_bwd", grid=(MLA_HEADS // 2, nq),
        in_specs=[whole, tile, tile, whole, rowvec, rowvec] + [_ANY] * ne,
        out_specs=[whole, tile, tile] + [_ANY] * ne,
        out_shape=[jax.ShapeDtypeStruct((s, wide), F32)] + [jax.ShapeDtypeStruct((s, wide), BF16)] * 2
        + _comm_out_shapes("exchange", exchange),
        scratch_shapes=[pltpu.VMEM((t, pair), F32), pltpu.VMEM((t, pair), F32), pltpu.VMEM((t, t), F32)]
        + (_comm_scratch(ne) if ne else []),
        compiler_params=_params(),
    )(q, k, v, do, lse, delta, *exchange)
    return outs[0], outs[1], outs[2], list(outs[3:])


def _ssd_bwd(dy, ypre, z, c, xraw, misc, prev, cw, dtb, a_exp, d_exp, nw, consts):
    s = dy.shape[0]
    nb = s // SSD_ROWS
    ncb = SSD_ROWS // CHUNK
    emisc, emisc_t, tri, trit = consts

    def body(dy_ref, ypre_ref, z_ref, c_ref, x_ref, misc_ref, prev_ref, cw_ref, dtb_ref, a_ref, d_ref, nw_ref,
             emisc_ref, emisct_ref, tri_ref, trit_ref,
             dz_ref, dx_ref, dmisc_ref, gnw_ref, gd_ref, galog_ref, gdtb_ref, gcw_ref, gcb_ref,
             dst_s, dc_s, head_s):
        i = pl.program_id(0)
        first = i == 0

        @pl.when(first)
        def _():
            dst_s[...] = jnp.zeros(dst_s.shape, F32)
            head_s[...] = jnp.zeros(head_s.shape, F32)
            gnw_ref[...] = jnp.zeros(gnw_ref.shape, F32)
            gd_ref[...] = jnp.zeros(gd_ref.shape, F32)
            galog_ref[...] = jnp.zeros(galog_ref.shape, F32)
            gdtb_ref[...] = jnp.zeros(gdtb_ref.shape, F32)

        a_exp_v = a_ref[...]
        a8 = _dot01(a_exp_v, emisct_ref[...]) * (1.0 / SSD_P)

        def chunk(ci):
            r0 = ci * CHUNK
            cc = c_ref[pl.ds(r0, CHUNK), :]
            mm = misc_ref[pl.ds(r0, CHUNK), :]
            xa, sig_c, dt, acs, acs_t, alast = _ssd_chunk_common(cc, mm, emisc_ref[...], tri_ref[...], trit_ref[...],
                                                              dtb_ref[...], a_exp_v)
            yield
            xs = xa[:, :SSD_INNER]
            xdt = xs * dt
            y = ypre_ref[pl.ds(r0, CHUNK), :]
            zz = z_ref[pl.ds(r0, CHUNK), :]
            sg, yn, rs = _gate_norm(y, zz)
            dyo = dy_ref[pl.ds(r0, CHUNK), :]
            gnw_ref[...] += jnp.sum(dyo * yn, axis=0, keepdims=True)
            dyn = dyo * nw_ref[...]
            half = SSD_INNER // SSD_GROUPS
            dyz_parts = []
            for g in range(SSD_GROUPS):
                gl = slice(g * half, (g + 1) * half)
                dyz_parts.append(rs[g] * (dyn[:, gl] - yn[:, gl] * jnp.mean(dyn[:, gl] * yn[:, gl], axis=-1, keepdims=True)))
            dyz = jnp.concatenate(dyz_parts, axis=1)
            dz_ref[pl.ds(r0, CHUNK), :] = dyz * y * (sg * (1.0 + zz * (1.0 - sg)))
            dyp = dyz * (zz * sg)
            dypb = dyp.astype(BF16)
            gd_ref[...] += jnp.sum(dyp * xs, axis=0, keepdims=True)
            yield
            prev = prev_ref[ci]
            cd = jnp.exp(alast)
            e = jnp.exp(acs)
            dsx = jnp.exp(alast - acs)
            wgt = (xdt * dsx).astype(BF16)
            dze = (dyp * e).astype(BF16)
            dprev_parts, diag_all, dbm, dcm, yoff_parts, bms = [], [], [], [], [], []
            lane8 = lax.broadcasted_iota(jnp.int32, (CHUNK, HEAD_PAD), 1)
            diag8 = jnp.zeros((CHUNK, HEAD_PAD), F32)
            for g in range(SSD_GROUPS):
                gl = slice(g * 256, (g + 1) * 256)
                bm = xa[:, SSD_INNER + g * SSD_N:SSD_INNER + (g + 1) * SSD_N].astype(BF16)
                cm = xa[:, SSD_INNER + SSD_GROUPS * SSD_N + g * SSD_N:SSD_INNER + SSD_GROUPS * SSD_N + (g + 1) * SSD_N].astype(BF16)
                bms.append(bm)
                prev_g = prev[:, gl].astype(BF16)
                dcm_g = _dot_nt(dze[:, gl], prev_g)
                dprev_parts.append(_dot_tn(cm, dze[:, gl]))
                cb_g = _dot_nt(cm, bm)
                dcb = jnp.zeros((CHUNK, CHUNK), F32)
                diag_parts = []
                for jj in range(2):
                    pair = 2 * g + jj
                    pl_ = slice(pair * 128, (pair + 1) * 128)
                    xp = xdt[:, pl_]
                    dyp_p = dypb[:, pl_]
                    dxp = jnp.zeros((CHUNK, 128), F32)
                    for hh in range(2):
                        hd = 2 * pair + hh
                        dec = _decay(acs, acs_t, hd)
                        xm = jnp.where(_half_mask(hh), xp, 0.0).astype(BF16)
                        dsc = _dot_nt(dyp_p, xm) * dec
                        dcb = dcb + dsc
                        sc = (cb_g * dec).astype(BF16)
                        dxp = dxp + jnp.where(_half_mask(hh), _dot_tn(sc, dyp_p), 0.0)
                        dm = dsc * cb_g
                        diag8 = diag8 + jnp.where(lane8 == MISC_DT + hd, jnp.sum(dm - dm.T, axis=1, keepdims=True), 0.0)
                    diag_parts.append(dxp)
                dcbb = dcb.astype(BF16)
                dcm.append(dcm_g + _dot(dcbb, bm))
                dbm.append(_dot_tn(dcbb, cm))
                diag_all.append(jnp.concatenate(diag_parts, axis=1))
                yoff_parts.append(_dot(cm, prev_g) * e[:, gl])
                yield
            dst = dst_s[...]
            glast = jnp.sum(dst * prev, axis=0, keepdims=True) * cd
            dxdt_state_parts = []
            for g in range(SSD_GROUPS):
                gl = slice(g * 256, (g + 1) * 256)
                dst_g = dst[:, gl].astype(BF16)
                dxdt_state_parts.append(_dot(bms[g], dst_g) * dsx[:, gl])
                dbm[g] = dbm[g] + _dot_nt(wgt[:, gl], dst_g)
            dst_s[...] = dst * cd + jnp.concatenate(dprev_parts, axis=1)
            yield
            dxdt_state = jnp.concatenate(dxdt_state_parts, axis=1)
            dxdt = jnp.concatenate(diag_all, axis=1) + dxdt_state
            dacs = dyp * jnp.concatenate(yoff_parts, axis=1) - xdt * dxdt_state
            last = jnp.sum(xdt * dxdt_state, axis=0, keepdims=True) + glast
            row = lax.broadcasted_iota(jnp.int32, (CHUNK, SSD_INNER), 0)
            dacs = dacs + jnp.where(row == CHUNK - 1, last, 0.0)
            dacs8 = _dot01(dacs, emisct_ref[...]) + diag8
            da8 = _dot01(dacs8, trit_ref[...], left=True)
            ddt8 = da8 * a8 + _dot01(dxdt * xs, emisct_ref[...])
            yield
            dtr8 = mm + _dot01(dtb_ref[...], emisct_ref[...]) * (1.0 / SSD_P)
            dt8 = jax.nn.softplus(dtr8)
            lane = lax.broadcasted_iota(jnp.int32, (CHUNK, HEAD_PAD), 1)
            on_dt = jnp.logical_and(lane >= MISC_DT, lane < MISC_DT + SSD_HEADS)
            ddtr8 = jnp.where(on_dt, ddt8 * jax.nn.sigmoid(dtr8), 0.0)
            dmisc_ref[pl.ds(r0, CHUNK), :] = ddtr8
            gdtb_ref[...] += jnp.sum(ddtr8, axis=0, keepdims=True)
            galog_ref[...] += jnp.sum(jnp.where(on_dt, da8 * dt8, 0.0), axis=0, keepdims=True) * a8
            dxs = d_ref[...] * dyp + dxdt * dt
            dxa = jnp.concatenate([dxs] + dbm + dcm, axis=1)
            dc_s[pl.ds(r0, CHUNK), :] = dxa * (sig_c * (1.0 + cc * (1.0 - sig_c)))

        _interleave([chunk(ci) for ci in reversed(range(ncb))])

        dc = dc_s[...]
        x = x_ref[...]
        dcext = jnp.concatenate([dc, head_s[...]], axis=0)
        dx = dc * cw_ref[CONV_W - 1:CONV_W, :]
        rows = [jnp.sum(dc * x, axis=0, keepdims=True)]
        for j in range(1, CONV_W):
            ahead = pltpu.roll(dcext, SSD_ROWS + 8 - j, 0)[:SSD_ROWS, :]
            dx = dx + ahead * cw_ref[CONV_W - 1 - j:CONV_W - j, :]
            rows.insert(0, jnp.sum(ahead * x, axis=0, keepdims=True))
        dx_ref[...] = dx
        head_s[...] = dc[:8, :]
        gcw = jnp.concatenate(rows, axis=0)

        @pl.when(first)
        def _():
            gcw_ref[...] = gcw
            gcb_ref[...] = jnp.sum(dc, axis=0, keepdims=True)

        @pl.when(jnp.logical_not(first))
        def _():
            gcw_ref[...] += gcw
            gcb_ref[...] += jnp.sum(dc, axis=0, keepdims=True)

    def rev(width):
        return pl.BlockSpec((SSD_ROWS, width), lambda i: (nb - 1 - i, 0))

    return pl.pallas_call(
        body, name="ssd_bwd", grid=(nb,),
        in_specs=[rev(SSD_INNER), rev(SSD_INNER), rev(SSD_INNER), rev(CONV_DIM), rev(CONV_DIM),
                  rev(HEAD_PAD), pl.BlockSpec((ncb, SSD_N, SSD_INNER), lambda i: (nb - 1 - i, 0, 0)),
                  _full((CONV_W, CONV_DIM)), _full((1, SSD_INNER)), _full((1, SSD_INNER)), _full((1, SSD_INNER)),
                  _full((1, SSD_INNER)), _full((HEAD_PAD, SSD_INNER)), _full((SSD_INNER, HEAD_PAD)), _full((CHUNK, CHUNK)),
                  _full((CHUNK, CHUNK))],
        out_specs=[rev(SSD_INNER), rev(CONV_DIM), rev(HEAD_PAD), _full((1, SSD_INNER)), _full((1, SSD_INNER)),
                   _full((1, HEAD_PAD)), _full((1, HEAD_PAD)), _full((CONV_W, CONV_DIM)), _full((1, CONV_DIM))],
        out_shape=[jax.ShapeDtypeStruct((s, SSD_INNER), F32), jax.ShapeDtypeStruct((s, CONV_DIM), F32),
                   jax.ShapeDtypeStruct((s, HEAD_PAD), F32), jax.ShapeDtypeStruct((1, SSD_INNER), F32),
                   jax.ShapeDtypeStruct((1, SSD_INNER), F32), jax.ShapeDtypeStruct((1, HEAD_PAD), F32),
                   jax.ShapeDtypeStruct((1, HEAD_PAD), F32), jax.ShapeDtypeStruct((CONV_W, CONV_DIM), F32),
                   jax.ShapeDtypeStruct((1, CONV_DIM), F32)],
        scratch_shapes=[pltpu.VMEM((SSD_N, SSD_INNER), F32), pltpu.VMEM((SSD_ROWS, CONV_DIM), F32), pltpu.VMEM((8, CONV_DIM), F32)],
        compiler_params=_params(),
    )(dy, ypre, z, c, xraw, misc, prev, cw, dtb, a_exp, d_exp, nw, emisc, emisc_t, tri, trit)


def _qkv_bwd(dq, dk, dv, cq, ckv, qnw, kvnw, wuq, wkv, cosf, sinf):
    s = dq.shape[0]
    wide = MLA_HEADS * HEAD_PAD

    def body(dq_ref, dk_ref, dv_ref, cq_ref, ckv_ref, qnw_ref, kvnw_ref, wuq_ref, wkv_ref, cos_ref, sin_ref,
             dqb_ref, dkvb_ref, dcq_ref, dckv_ref, dmisc_ref, gq_ref, gkv_ref):
        first = pl.program_id(0) == 0
        cosf, sinf = cos_ref[...], sin_ref[...]
        dkr = jnp.zeros((TM, HEAD_PAD), F32)
        for hd in range(MLA_HEADS):
            cols = slice(hd * HEAD_PAD, (hd + 1) * HEAD_PAD)
            dqb_ref[:, cols] = _rope(dq_ref[:, cols], cosf, sinf, -1.0).astype(BF16)
            dkh = dk_ref[:, cols]
            dkvb_ref[:, cols] = dkh.astype(BF16)
            dkr = dkr + dkh
        dkvb_ref[:, wide:] = dv_ref[...].astype(BF16)
        lane = lax.broadcasted_iota(jnp.int32, dkr.shape, 1)
        in_rope = jnp.logical_and(lane >= MISC_ROPE, lane < MISC_ROPE + QK_ROPE)
        dmisc_ref[...] = jnp.where(in_rope, _rope(jnp.where(in_rope, dkr, 0.0), cosf, sinf, -1.0), 0.0)
        dcq, gq = _rms_bwd(cq_ref[...], qnw_ref[...], _dot_nt(dqb_ref[...], wuq_ref[...]))
        dcq_ref[...] = dcq
        _acc_rows(gq_ref, gq, first)
        dckv, gkv = _rms_bwd(ckv_ref[...], kvnw_ref[...], _dot_nt(dkvb_ref[...], wkv_ref[...]))
        dckv_ref[...] = dckv
        _acc_rows(gkv_ref, gkv, first)

    return pl.pallas_call(
        body, name="qkv_bwd", grid=(s // TM,),
        in_specs=[_rows(TM, wide)] * 3 + [_rows(TM, Q_RANK), _rows(TM, KV_RANK), _full((1, Q_RANK)), _full((1, KV_RANK)),
                                          _resident((Q_RANK, wide)), _resident((KV_RANK, 2 * wide)), _rows(TM, HEAD_PAD), _rows(TM, HEAD_PAD)],
        out_specs=[_rows(TM, wide), _rows(TM, 2 * wide), _rows(TM, Q_RANK), _rows(TM, KV_RANK), _rows(TM, HEAD_PAD),
                   _full((1, Q_RANK)), _full((1, KV_RANK))],
        out_shape=[jax.ShapeDtypeStruct((s, wide), BF16), jax.ShapeDtypeStruct((s, 2 * wide), BF16),
                   jax.ShapeDtypeStruct((s, Q_RANK), F32), jax.ShapeDtypeStruct((s, KV_RANK), F32),
                   jax.ShapeDtypeStruct((s, HEAD_PAD), F32), jax.ShapeDtypeStruct((1, Q_RANK), F32),
                   jax.ShapeDtypeStruct((1, KV_RANK), F32)],
        compiler_params=_params(),
    )(dq, dk, dv, cq, ckv, qnw, kvnw, wuq, wkv, cosf, sinf)


def _inproj_bwd(dcq, dckv, dmisc_rope, dmisc_dt, dz, dxbc, h, dh1, nw, win):
    s = h.shape[0]

    def body(dcq_ref, dckv_ref, dma_ref, dmb_ref, dz_ref, dxbc_ref, h_ref, dh1_ref, nw_ref, w_ref, dproj_ref, dh0_ref, gnw_ref):
        dproj_ref[:, 0:768] = dcq_ref[...].astype(BF16)
        dproj_ref[:, 768:1024] = dckv_ref[...].astype(BF16)
        dproj_ref[:, 1024:1152] = (dma_ref[...] + dmb_ref[...]).astype(BF16)
        dproj_ref[:, 1152:1664] = dz_ref[...].astype(BF16)
        dproj_ref[:, 1664:2688] = dxbc_ref[...].astype(BF16)
        du = _dot_nt(dproj_ref[...], w_ref[...])
        dx, gnw = _rms_bwd(h_ref[...], nw_ref[...], du)
        _acc_rows(gnw_ref, gnw, pl.program_id(0) == 0)
        dh0_ref[...] = dh1_ref[...] + dx

    return pl.pallas_call(
        body, name="inproj_bwd", grid=(s // TM,),
        in_specs=[_rows(TM, Q_RANK), _rows(TM, KV_RANK), _rows(TM, HEAD_PAD), _rows(TM, HEAD_PAD), _rows(TM, SSD_INNER),
                  _rows(TM, CONV_DIM), _rows(TM, D_MODEL), _rows(TM, D_MODEL), _full((1, D_MODEL)), _resident((D_MODEL, IN_PAD))],
        out_specs=[_rows(TM, IN_PAD), _rows(TM, D_MODEL), _full((1, D_MODEL))],
        out_shape=[jax.ShapeDtypeStruct((s, IN_PAD), BF16), jax.ShapeDtypeStruct((s, D_MODEL), F32),
                   jax.ShapeDtypeStruct((1, D_MODEL), F32)],
        compiler_params=_params(),
    )(dcq, dckv, dmisc_rope, dmisc_dt, dz, dxbc, h, dh1, nw, win)


def _row_tile(rows, cols):
    cap = max(8, (1 << 18) // max(cols, 128))
    best = None
    for t in range(8, rows + 1, 8):
        if rows % t == 0 and t <= cap:
            best = t
    return best if best is not None else rows


def _adamw(w, g, m, v, name):
    rows, cols = w.shape
    tr = _row_tile(rows, cols)

    def body(w_ref, g_ref, m_ref, v_ref, d_ref, m2_ref, v2_ref):
        gg = g_ref[...]
        m2 = ADAM_B1 * m_ref[...] + (1.0 - ADAM_B1) * gg
        v2 = ADAM_B2 * v_ref[...] + (1.0 - ADAM_B2) * jnp.square(gg)
        m_hat = m2 / (1.0 - ADAM_B1 ** ADAM_STEP)
        v_hat = v2 / (1.0 - ADAM_B2 ** ADAM_STEP)
        d_ref[...] = -ADAM_LR * (m_hat / (jnp.sqrt(v_hat) + ADAM_EPS) + ADAM_WD * w_ref[...])
        m2_ref[...] = m2
        v2_ref[...] = v2

    spec = pl.BlockSpec((tr, cols), lambda i: (i, 0))
    return pl.pallas_call(
        body, name=name, grid=(rows // tr,),
        in_specs=[spec] * 4, out_specs=[spec] * 3,
        out_shape=[jax.ShapeDtypeStruct((rows, cols), F32)] * 3,
    )(w, g, m, v)


def _sum_adamw(slots, w, m, v, name):
    _, rows, cols = w.shape
    tr = _row_tile(rows, cols)
    nb = rows // tr

    def body(s0_ref, s1_ref, w_ref, m_ref, v_ref, g_ref, d_ref, m2_ref, v2_ref):
        for l, ref in enumerate((s0_ref, s1_ref)):
            @pl.when(pl.program_id(0) == l)
            def _(ref=ref):
                acc = ref[0].astype(F32)
                for i in range(1, N_DEV):
                    acc = acc + ref[i].astype(F32)
                g_ref[...] = acc

        gg = g_ref[...]
        m2 = ADAM_B1 * m_ref[...] + (1.0 - ADAM_B1) * gg
        v2 = ADAM_B2 * v_ref[...] + (1.0 - ADAM_B2) * jnp.square(gg)
        m_hat = m2 / (1.0 - ADAM_B1 ** ADAM_STEP)
        v_hat = v2 / (1.0 - ADAM_B2 ** ADAM_STEP)
        d_ref[...] = -ADAM_LR * (m_hat / (jnp.sqrt(v_hat) + ADAM_EPS) + ADAM_WD * w_ref[...])
        m2_ref[...] = m2
        v2_ref[...] = v2

    slot_spec = lambda layer: pl.BlockSpec((N_DEV, tr, cols), lambda l, i: (0, jnp.where(l == layer, i, (nb - 1) * (1 - layer)), 0))
    spec = pl.BlockSpec((None, tr, cols), lambda l, i: (l, i, 0))
    return pl.pallas_call(
        body, name=name, grid=(DEPTH, nb),
        in_specs=[slot_spec(0), slot_spec(1), spec, spec, spec], out_specs=[spec] * 4,
        out_shape=[jax.ShapeDtypeStruct(w.shape, F32)] * 4,
        compiler_params=_params(),
    )(slots[0], slots[1], w, m, v)


_MESH = pl.DeviceIdType.MESH
_ANY = pl.BlockSpec(memory_space=pl.ANY)


def _my_place():
    return lax.axis_index("x"), lax.axis_index("y"), lax.axis_index("c")


def _flip(place, k):
    x, y, c = place
    return (1 - x if k & 4 else x, 1 - y if k & 2 else y, 1 - c if k & 1 else c)


def _block_id(place):
    return 4 * place[0] + 2 * place[1] + place[2]


def _peer_copies(kind, in_refs, out_refs, send_sems, recv_sems, local_sems):
    me = _my_place()
    my = _block_id(me)
    remote, local = [], []
    for a, (x_ref, out_ref) in enumerate(zip(in_refs, out_refs)):
        src_of = (lambda place, r=x_ref: r) if kind == "gather" else (lambda place, r=x_ref: r.at[_block_id(place)])
        local.append(pltpu.make_async_copy(src_of(me), out_ref.at[my], local_sems.at[a]))
        for k in range(1, N_DEV):
            peer = _flip(me, k)
            remote.append(pltpu.make_async_remote_copy(
                src_ref=src_of(peer), dst_ref=out_ref.at[my], send_sem=send_sems.at[a * 7 + k - 1],
                recv_sem=recv_sems.at[a * 7 + k - 1], device_id=peer, device_id_type=_MESH))
    return remote, local


def _comm_out_shapes(kind, arrays):
    return [jax.ShapeDtypeStruct((N_DEV, *a.shape) if kind == "gather" else a.shape, a.dtype) for a in arrays]


def _comm_scratch(n):
    return [pltpu.SemaphoreType.DMA((7 * n,)), pltpu.SemaphoreType.DMA((7 * n,)), pltpu.SemaphoreType.DMA((n,))]


def _hosted_comm(kind, in_refs, out_refs, sems, first, last):
    if not in_refs:
        return

    @pl.when(first)
    def _():
        remote, local = _peer_copies(kind, in_refs, out_refs, *sems)
        for cp in local + remote:
            cp.start()

    @pl.when(last)
    def _():
        remote, local = _peer_copies(kind, in_refs, out_refs, *sems)
        for cp in remote:
            cp.wait()
        for cp in local:
            cp.wait()


def _two_level_gather_steps(in_refs, out_refs, send_sems, recv_sems, local_sems):
    n = len(in_refs)
    me = _my_place()
    x, y, c = me
    sibling = (x, y, 1 - c)
    chips = [(1 - x, y), (x, 1 - y), (1 - x, 1 - y)]

    def copy(a, k, place, to, src=None):
        block = out_refs[a].at[_block_id(place)]
        return pltpu.make_async_remote_copy(
            src_ref=block if src is None else src, dst_ref=block, send_sem=send_sems.at[7 * a + k],
            recv_sem=recv_sems.at[7 * a + k], device_id=to, device_id_type=_MESH)

    mine = [pltpu.make_async_copy(in_refs[a], out_refs[a].at[_block_id(me)], local_sems.at[a]) for a in range(n)]
    first = [copy(a, 0, me, sibling, src=in_refs[a]) for a in range(n)]
    first += [copy(a, 1 + j, me, (*chip, c), src=in_refs[a]) for a in range(n) for j, chip in enumerate(chips)]
    passed = [copy(a, 4 + j, (*chip, c), sibling) for a in range(n) for j, chip in enumerate(chips)]

    def send():
        for cp in mine + first:
            cp.start()

    def forward():
        for a in range(n):
            for j, chip in enumerate(chips):
                copy(a, 1 + j, (*chip, c), me).wait_recv()
                passed[3 * a + j].start()

    def finish():
        for a in range(n):
            copy(a, 0, sibling, me).wait_recv()
            for j, chip in enumerate(chips):
                copy(a, 4 + j, (*chip, 1 - c), me).wait_recv()
        for cp in first + passed:
            cp.wait_send()
        for cp in mine:
            cp.wait()

    return send, forward, finish


def _gather_two_level(arrays, name):
    n = len(arrays)

    def body(*refs):
        for step in _two_level_gather_steps(refs[:n], refs[n:2 * n], *refs[2 * n:]):
            step()

    return pl.pallas_call(
        body, name=name, out_shape=_comm_out_shapes("gather", arrays),
        in_specs=[_ANY] * n, out_specs=[_ANY] * n, scratch_shapes=_comm_scratch(n),
    )(*arrays)


def _hosted_gather(in_refs, out_refs, sems, first, middle, last):
    if not in_refs:
        return
    for when, index in ((first, 0), (middle, 1), (last, 2)):
        @pl.when(when)
        def _(index=index):
            _two_level_gather_steps(in_refs, out_refs, *sems)[index]()


def _comm(kind, arrays, name):
    n = len(arrays)

    def body(*refs):
        remote, local = _peer_copies(kind, refs[:n], refs[n:2 * n], *refs[2 * n:])
        for cp in local + remote:
            cp.start()
        for cp in remote:
            cp.wait()
        for cp in local:
            cp.wait()

    return pl.pallas_call(
        body, name=name, out_shape=_comm_out_shapes(kind, arrays),
        in_specs=[_ANY] * n, out_specs=[_ANY] * n, scratch_shapes=_comm_scratch(n),
    )(*arrays)


def _all_reduce_small(part):
    rows, lanes = part.shape
    vmem = pl.BlockSpec(memory_space=pltpu.VMEM)

    def body(x_ref, gath_ref, sum_ref, send_sems, recv_sems):
        me = _my_place()
        my = _block_id(me)
        gath_ref[my] = x_ref[...]
        copies = []
        for k in range(1, N_DEV):
            cp = pltpu.make_async_remote_copy(
                src_ref=x_ref, dst_ref=gath_ref.at[my], send_sem=send_sems.at[k - 1], recv_sem=recv_sems.at[k - 1],
                device_id=_flip(me, k), device_id_type=_MESH)
            cp.start()
            copies.append(cp)
        for cp in copies:
            cp.wait()
        acc = gath_ref[0]
        for i in range(1, N_DEV):
            acc = acc + gath_ref[i]
        sum_ref[...] = acc

    return pl.pallas_call(
        body, name="small_grad_all_reduce",
        out_shape=[jax.ShapeDtypeStruct((N_DEV, rows, lanes), F32), jax.ShapeDtypeStruct((rows, lanes), F32)],
        in_specs=[vmem], out_specs=[vmem, vmem],
        scratch_shapes=[pltpu.SemaphoreType.DMA((7,)), pltpu.SemaphoreType.DMA((7,))],
    )(part)[1]


_SHARDED = (("w_in", (D_MODEL, IN_PROJ // N_DEV)), ("w_uq", (Q_RANK // N_DEV, Q_RANK)), ("w_ukv", (KV_RANK, HEAD_PAD)),
            ("conv_w", (CONV_W, CONV_DIM // N_DEV)), ("w_out", (D_MODEL // N_DEV, D_MODEL)),
            ("w_up", (D_MODEL, D_FF // N_DEV)), ("w_down", (D_FF // N_DEV, D_MODEL)))
_SMALL = (("pre_mix_norm", D_MODEL), ("q_norm", Q_RANK), ("kv_norm", KV_RANK), ("conv_b", CONV_DIM), ("dt_bias", SSD_HEADS),
          ("a_log", SSD_HEADS), ("d_skip", SSD_HEADS), ("ssd_norm", SSD_INNER), ("post_mix_norm", D_MODEL),
          ("pre_mlp_norm", D_MODEL), ("post_mlp_norm", D_MODEL))
_WEIGHT_ORDER = ("pre_mix_norm", "w_in", "q_norm", "w_uq", "kv_norm", "w_ukv", "conv_w", "conv_b", "dt_bias", "a_log", "d_skip",
                 "ssd_norm", "w_out", "post_mix_norm", "pre_mlp_norm", "w_up", "w_down", "post_mlp_norm")
_EARLY = ("w_in", "w_uq", "w_ukv", "conv_w")
_LATE = ("w_out", "w_up", "w_down")


def _wire_shard(name, a):
    return lax.bitcast_convert_type(a, BF16).reshape(CONV_W, -1) if name == "conv_w" else a.astype(BF16)


def _from_wire(name, g):
    return lax.bitcast_convert_type(g.reshape(N_DEV, CONV_W, -1, 2), F32) if name == "conv_w" else g


def _cols(stacked):
    return jnp.transpose(stacked, (1, 0, 2)).reshape(stacked.shape[1], -1)


def _early_weights(sh):
    w_in = _cols(sh["w_in"])
    zeros = lambda n: jnp.zeros((D_MODEL, n), BF16)
    s1, s2, s3, s4, s5 = 768, 1024, 1056, 1568, 2592
    win = jnp.concatenate([w_in[:, :s2], zeros(MISC_ROPE), w_in[:, s2:s3], w_in[:, s5:], zeros(HEAD_PAD - MISC_DT - SSD_HEADS),
                           w_in[:, s3:s5]], axis=1)
    w_uq = sh["w_uq"].reshape(Q_RANK, MLA_HEADS, QK_NOPE + QK_ROPE)
    wuq = jnp.pad(w_uq, ((0, 0), (0, 0), (0, HEAD_PAD - QK_NOPE - QK_ROPE))).reshape(Q_RANK, -1)
    w_ukv = _cols(sh["w_ukv"]).reshape(KV_RANK, MLA_HEADS, QK_NOPE + V_DIM)
    wkn = jnp.pad(w_ukv[..., :QK_NOPE], ((0, 0), (0, 0), (0, HEAD_PAD - QK_NOPE))).reshape(KV_RANK, -1)
    wv = w_ukv[..., QK_NOPE:].reshape(KV_RANK, 4, 2, 1, V_DIM) * jnp.eye(2, dtype=BF16).reshape(1, 1, 2, 2, 1)
    wkv = jnp.concatenate([wkn, wv.reshape(KV_RANK, -1)], axis=1)
    return dict(win=win, wuq=wuq, wkv=wkv, conv_w=_cols(sh["conv_w"]))


def _late_weights(sh):
    w_out = sh["w_out"].reshape(D_MODEL, D_MODEL)
    watt = w_out[:SSD_INNER].reshape(4, 2, 1, V_DIM, D_MODEL) * jnp.eye(2, dtype=BF16).reshape(1, 2, 2, 1, 1)
    wout = jnp.concatenate([watt.reshape(MLA_HEADS * HEAD_PAD, D_MODEL), w_out[SSD_INNER:]], axis=0)
    return dict(wout=wout, wup=sh["w_up"], wdown=sh["w_down"])


def _shard_grads(g):
    out = {}
    if "wup" in g:
        out["w_up"], out["w_down"] = g["wup"], g["wdown"]
        ae = g["wout_att"].reshape(4, 2, 2, V_DIM, D_MODEL)
        att = jnp.stack([ae[:, 0, 0], ae[:, 1, 1]], axis=1).reshape(SSD_INNER, D_MODEL)
        out["w_out"] = jnp.concatenate([att, g["wout_ssd"]], axis=0).astype(BF16).reshape(N_DEV, D_MODEL // N_DEV, D_MODEL)
    if "win" not in g:
        return out
    dwin = g["win"]
    s2 = Q_RANK + KV_RANK
    m0 = s2
    w_in = jnp.concatenate([dwin[:, :s2], dwin[:, m0 + MISC_ROPE:m0 + MISC_ROPE + QK_ROPE], dwin[:, 1152:2688],
                            dwin[:, m0 + MISC_DT:m0 + MISC_DT + SSD_HEADS]], axis=1)
    out["w_in"] = jnp.transpose(w_in.astype(BF16).reshape(D_MODEL, N_DEV, -1), (1, 0, 2))
    w_uq = g["wuq"].astype(BF16).reshape(Q_RANK, MLA_HEADS, HEAD_PAD)[..., :QK_NOPE + QK_ROPE].reshape(Q_RANK, Q_RANK)
    out["w_uq"] = w_uq.reshape(N_DEV, Q_RANK // N_DEV, Q_RANK)
    wide = MLA_HEADS * HEAD_PAD
    wkv = g["wkv"].astype(BF16)
    kn = wkv[:, :wide].reshape(KV_RANK, MLA_HEADS, HEAD_PAD)[..., :QK_NOPE]
    ve = wkv[:, wide:].reshape(KV_RANK, 4, 2, 2, V_DIM)
    vv = jnp.stack([ve[:, :, 0, 0], ve[:, :, 1, 1]], axis=2).reshape(KV_RANK, MLA_HEADS, V_DIM)
    out["w_ukv"] = jnp.transpose(jnp.concatenate([kn, vv], axis=-1), (1, 0, 2))
    out["conv_w"] = jnp.transpose(g["conv_w"].astype(BF16).reshape(CONV_W, N_DEV, -1), (1, 0, 2))
    return out


def _small_rows(n):
    return -(-n // 1024) * 8


def _pack_small(vals):
    rows = []
    for l in range(DEPTH):
        for name, n in _SMALL:
            r = _small_rows(n)
            rows.append(jnp.pad(vals[name][l].reshape(-1), (0, r * 128 - n)).reshape(r, 128))
    return jnp.concatenate(rows, axis=0)


def _unpack_small(packed):
    out, off = {name: [] for name, _ in _SMALL}, 0
    for l in range(DEPTH):
        for name, n in _SMALL:
            r = _small_rows(n)
            out[name].append(packed[off:off + r].reshape(-1)[:n])
            off += r
    return {name: jnp.stack(v) for name, v in out.items()}


def _lane_rows(vec8):
    return jnp.repeat(vec8, SSD_P).reshape(1, SSD_INNER)


def _layer_fwd(h, kw, sm, l, cosf, sinf, consts, gather=(), after_gather=None, target=None):
    row = lambda name: sm[name][l].reshape(1, -1)
    t = {}
    t["h0"] = h
    t["ub"], t["cq"], t["ckv"], t["misc"], t["z"], t["xraw"] = _inproj_fwd(h, row("pre_mix_norm"), kw["win"])
    t["cqn"], t["ckvn"], t["q"], t["k"], t["v"] = _qkv_fwd(t["cq"], t["ckv"], t["misc"], row("q_norm"), row("kv_norm"),
                                                         kw["wuq"], kw["wkv"], cosf, sinf)
    t["oe"], t["lse"], gathered = _attn_fwd(t["q"], t["k"], t["v"], gather)
    if after_gather is not None:
        after_gather(gathered)
    t["dtb"] = _lane_rows(sm["dt_bias"][l])
    t["a_exp"] = _lane_rows(-jnp.exp(sm["a_log"][l]))
    t["d_exp"] = _lane_rows(sm["d_skip"][l])
    t["c"], t["prev"], t["ypre"], t["yssd"] = _ssd_fwd(t["xraw"], t["misc"], t["z"], kw["conv_w"], row("conv_b"), t["dtb"],
                                                     t["a_exp"], t["d_exp"], row("ssd_norm"), consts)
    t["mixed"], t["h1"] = _outproj_fwd(t["oe"], t["yssd"], kw["wout"], h, row("post_mix_norm"))
    t["mb"], t["ab"], t["d"], *out = _mlp_fwd(t["h1"], row("pre_mlp_norm"), kw["wup"], kw["wdown"], row("post_mlp_norm"), target)
    return out, t


def _layer_bwd(dh2, t, kw, sm, l, cosf, sinf, consts, exchange_of=None):
    row = lambda name: sm[name][l].reshape(1, -1)
    g, gs = {}, {}
    dh1, dab, ddb, gs["post_mlp_norm"], gs["pre_mlp_norm"] = _mlp_bwd(
        dh2, t["d"], t["h1"], t["ab"], row("pre_mlp_norm"), kw["wup"], kw["wdown"], row("post_mlp_norm"))
    g["wup"] = _matmul_tn_stacked(t["mb"], dab, f"dw_up_{l}", a_stacked=False)
    g["wdown"] = _matmul_tn_stacked(t["ab"], ddb, f"dw_down_{l}", a_stacked=True, square_a=True)
    dmixb, doe, dyssd, gs["post_mix_norm"], delta = _outproj_bwd(dh1, t["mixed"], row("post_mix_norm"), kw["wout"], t["oe"])
    g["wout_att"] = _matmul_tn(t["oe"], dmixb, f"dw_out_att_{l}")
    g["wout_ssd"] = _matmul_tn(t["yssd"], dmixb, f"dw_out_ssd_{l}")
    dz, dxraw, dmisc_dt, gs["ssd_norm"], gd, galog, gdtb, g["conv_w"], gs["conv_b"] = _ssd_bwd(
        dyssd, t["ypre"], t["z"], t["c"], t["xraw"], t["misc"], t["prev"], kw["conv_w"], t["dtb"], t["a_exp"], t["d_exp"],
        row("ssd_norm"), consts)
    gs["d_skip"] = jnp.sum(gd.reshape(SSD_HEADS, SSD_P), axis=1)
    gs["a_log"] = galog[0, MISC_DT:MISC_DT + SSD_HEADS]
    gs["dt_bias"] = gdtb[0, MISC_DT:MISC_DT + SSD_HEADS]
    dq, dk, dv, exchanged = _attn_bwd(t["q"], t["k"], t["v"], doe, t["lse"], delta,
                                      exchange_of(g) if exchange_of is not None else ())
    dqb, dkvb, dcq, dckv, dmisc_rope, gs["q_norm"], gs["kv_norm"] = _qkv_bwd(
        dq, dk, dv, t["cq"], t["ckv"], row("q_norm"), row("kv_norm"), kw["wuq"], kw["wkv"], cosf, sinf)
    g["wuq"] = _matmul_tn(t["cqn"], dqb, f"dw_uq_{l}")
    g["wkv"] = _matmul_tn(t["ckvn"], dkvb, f"dw_kv_{l}")
    dprojb, dh0, gs["pre_mix_norm"] = _inproj_bwd(dcq, dckv, dmisc_rope, dmisc_dt, dz, dxraw, t["h0"], dh1,
                                                  row("pre_mix_norm"), kw["win"])
    g["win"] = _matmul_tn(t["ub"], dprojb, f"dw_in_{l}")
    return dh0, g, {k: v.reshape(-1) for k, v in gs.items()}, exchanged


def _local_step(x, positions, kws, sm, target, gather=(), after_gather=None, exchange_of=None):
    inv_freq = ROPE_THETA ** (-jnp.arange(0, QK_ROPE, 2, dtype=F32) / QK_ROPE)
    invf = jnp.zeros((HEAD_PAD,), F32).at[MISC_ROPE:MISC_ROPE + QK_ROPE].set(jnp.concatenate([inv_freq, inv_freq]))
    cosf, sinf = _rope_tables(positions.reshape(-1, 1), invf.reshape(1, HEAD_PAD))
    consts = _ssd_consts()
    (h,), t0 = _layer_fwd(x, kws[0], sm, 0, cosf, sinf, consts, gather, after_gather)
    (dh, loss), t1 = _layer_fwd(h, kws[1], sm, 1, cosf, sinf, consts, target=target)
    saved = [t0, t1]
    grads, small, exchanged = [None] * DEPTH, [None] * DEPTH, []
    for l in reversed(range(DEPTH)):
        hook = (lambda g0: exchange_of(g0, grads[1])) if (l == 0 and exchange_of is not None) else None
        dh, grads[l], small[l], got = _layer_bwd(dh, saved[l], kws[l], sm, l, cosf, sinf, consts, hook)
        exchanged = got or exchanged
    return loss[0, 0], dh, grads, small, exchanged


def kernel(x, positions, pre_mix_norm, w_in, q_norm, w_uq, kv_norm, w_ukv, conv_w, conv_b, dt_bias, a_log, d_skip, ssd_norm, w_out, post_mix_norm, pre_mlp_norm, w_up, w_down, post_mlp_norm, loss_target, m_pre_mix_norm, m_w_in, m_q_norm, m_w_uq, m_kv_norm, m_w_ukv, m_conv_w, m_conv_b, m_dt_bias, m_a_log, m_d_skip, m_ssd_norm, m_w_out, m_post_mix_norm, m_pre_mlp_norm, m_w_up, m_w_down, m_post_mlp_norm, v_pre_mix_norm, v_w_in, v_q_norm, v_w_uq, v_kv_norm, v_w_ukv, v_conv_w, v_conv_b, v_dt_bias, v_a_log, v_d_skip, v_ssd_norm, v_w_out, v_post_mix_norm, v_pre_mlp_norm, v_w_up, v_w_down, v_post_mlp_norm):
    w = dict(pre_mix_norm=pre_mix_norm, w_in=w_in, q_norm=q_norm, w_uq=w_uq, kv_norm=kv_norm, w_ukv=w_ukv, conv_w=conv_w,
             conv_b=conv_b, dt_bias=dt_bias, a_log=a_log, d_skip=d_skip, ssd_norm=ssd_norm, w_out=w_out,
             post_mix_norm=post_mix_norm, pre_mlp_norm=pre_mlp_norm, w_up=w_up, w_down=w_down, post_mlp_norm=post_mlp_norm)
    m = dict(pre_mix_norm=m_pre_mix_norm, w_in=m_w_in, q_norm=m_q_norm, w_uq=m_w_uq, kv_norm=m_kv_norm, w_ukv=m_w_ukv,
             conv_w=m_conv_w, conv_b=m_conv_b, dt_bias=m_dt_bias, a_log=m_a_log, d_skip=m_d_skip, ssd_norm=m_ssd_norm,
             w_out=m_w_out, post_mix_norm=m_post_mix_norm, pre_mlp_norm=m_pre_mlp_norm, w_up=m_w_up, w_down=m_w_down,
             post_mlp_norm=m_post_mlp_norm)
    v = dict(pre_mix_norm=v_pre_mix_norm, w_in=v_w_in, q_norm=v_q_norm, w_uq=v_w_uq, kv_norm=v_kv_norm, w_ukv=v_w_ukv,
             conv_w=v_conv_w, conv_b=v_conv_b, dt_bias=v_dt_bias, a_log=v_a_log, d_skip=v_d_skip, ssd_norm=v_ssd_norm,
             w_out=v_w_out, post_mix_norm=v_post_mix_norm, pre_mlp_norm=v_pre_mlp_norm, w_up=v_w_up, w_down=v_w_down,
             post_mlp_norm=v_post_mlp_norm)
    sm = {name: w[name] for name, _ in _SMALL}

    wire = lambda name, l: _wire_shard(name, w[name][l])
    first = _gather_two_level([wire(name, 0) for name in _EARLY], "weight_gather_first")
    kws = [_early_weights({name: _from_wire(name, a) for name, a in zip(_EARLY, first)}), None]
    behind = [(name, 0) for name in _LATE] + [(name, 1) for name, _ in _SHARDED]

    def after_gather(gathered):
        got = {key: _from_wire(key[0], a) for key, a in zip(behind, gathered)}
        kws[0].update(_late_weights({name: got[name, 0] for name in _LATE}))
        kws[1] = {**_early_weights({name: got[name, 1] for name in _EARLY}),
                  **_late_weights({name: got[name, 1] for name in _LATE})}

    sent_behind = [(name, 1) for name, _ in _SHARDED] + [(name, 0) for name in _LATE]

    def exchange_of(g0, g1):
        blocks = {**{(name, 1): a for name, a in _shard_grads(g1).items()},
                  **{(name, 0): a for name, a in _shard_grads(g0).items()}}
        return [blocks[key] for key in sent_behind]

    loss_part, dx, grads, small, exchanged = _local_step(
        x[0], positions[0], kws, sm, loss_target[0], [wire(*key) for key in behind], after_gather, exchange_of)
    slots = dict(zip(sent_behind, exchanged))
    last = _shard_grads({k: grads[0][k] for k in ("win", "wuq", "wkv", "conv_w")})
    slots.update({(name, 0): a for name, a in zip(_EARLY, _comm("exchange", [last[name] for name in _EARLY], "grad_exchange_last"))})
    g_small = _unpack_small(_all_reduce_small(_pack_small({name: jnp.stack([small[l][name] for l in range(DEPTH)])
                                                           for name, _ in _SMALL})))
    loss = lax.psum(loss_part, ("x", "y", "c"))

    grad, delta, new_m, new_v = {}, {}, {}, {}
    for name, _ in _SHARDED:
        grad[name], delta[name], new_m[name], new_v[name] = _sum_adamw(
            [slots[name, 0], slots[name, 1]], w[name], m[name], v[name], f"sum_adamw_{name}")
    pk = lambda d: _pack_small({name: d[name] for name, _ in _SMALL})
    d_, m_, v_ = _adamw(pk(w), pk(g_small), pk(m), pk(v), "adamw_small")
    for dst, packed in ((delta, d_), (new_m, m_), (new_v, v_)):
        dst.update(_unpack_small(packed))
    grad.update(g_small)

    outs = [loss, dx[None]]
    for d in (grad, delta, new_m, new_v):
        outs += [d[name] for name in _WEIGHT_ORDER]
    return tuple(outs)
```

```python
import jax
import jax.numpy as jnp
import numpy as np
from jax import lax
from jax.experimental import pallas as pl
from jax.experimental.pallas import tpu as pltpu

F32 = jnp.float32
BF16 = jnp.bfloat16
HI = lax.Precision.HIGHEST

D_MODEL = 1024
DEPTH = 2
N_DEV = 8
CHUNK = 64
EPS = 1e-6
MLA_HEADS = 8
QK_NOPE = 64
QK_ROPE = 32
V_DIM = 64
Q_RANK = 768
KV_RANK = 256
ROPE_THETA = 10000.0
SSD_HEADS = 8
SSD_P = 64
SSD_INNER = 512
SSD_GROUPS = 2
SSD_N = 128
CONV_W = 4
CONV_DIM = 1024
D_FF = 4096
IN_PROJ = 2600
HEAD_PAD = 128
IN_PAD = 2688
MISC_ROPE = 64
MISC_DT = 96
ATT_SCALE = (QK_NOPE + QK_ROPE) ** -0.5
LOG2E = 1.4426950408889634
ATT_SCALE_LOG2 = ATT_SCALE * LOG2E

ADAM_LR = 0.001
ADAM_B1 = 0.9
ADAM_B2 = 0.999
ADAM_EPS = 1e-08
ADAM_WD = 0.01
ADAM_STEP = 10

TM = 512
ATT_T = 512
ATT_G = 8
ATT_UNROLL = 4
SSD_ROWS = 512
TK_DW = 4096
VMEM_LIMIT = 56 * 1024 * 1024

_NT = (((1,), (1,)), ((), ()))
_TN = (((0,), (0,)), ((), ()))


def _params(**kw):
    return pltpu.CompilerParams(vmem_limit_bytes=VMEM_LIMIT, **kw)


def _dot(a, b, precision=None):
    return jnp.dot(a, b, preferred_element_type=F32, precision=precision)


def _dot_nt(a, b, precision=None):
    return lax.dot_general(a, b, _NT, preferred_element_type=F32, precision=precision)


def _dot_tn(a, b, precision=None):
    return lax.dot_general(a, b, _TN, preferred_element_type=F32, precision=precision)


def _split3(x):
    hi = x.astype(BF16)
    r = x - hi.astype(F32)
    mid = r.astype(BF16)
    return hi, mid, (r - mid.astype(F32)).astype(BF16)


def _dot01(x, m01, dot=_dot, left=False):
    parts = [dot(m01, p) if left else dot(p, m01) for p in _split3(x)]
    return parts[0] + parts[1] + parts[2]


def _full(shape):
    n = len(shape)
    return pl.BlockSpec(shape, lambda *_: (0,) * n)


def _resident(shape):
    n = len(shape)
    return pl.BlockSpec(shape, lambda *_: (0,) * n, pipeline_mode=pl.Buffered(1))


def _rows(tm, width):
    return pl.BlockSpec((tm, width), lambda i: (i, 0))


def _rms_fwd(x, w):
    r = lax.rsqrt(jnp.mean(x * x, axis=-1, keepdims=True) + EPS)
    return (x * r) * w


def _rms_bwd(x, w, dy):
    r = lax.rsqrt(jnp.mean(x * x, axis=-1, keepdims=True) + EPS)
    xh = x * r
    dxn = dy * w
    dx = r * (dxn - xh * jnp.mean(dxn * xh, axis=-1, keepdims=True))
    return dx, dy * xh


def _acc_rows(ref, val, first):
    s = jnp.sum(val, axis=0, keepdims=True)

    @pl.when(first)
    def _():
        ref[...] = s

    @pl.when(jnp.logical_not(first))
    def _():
        ref[...] += s


def _rope(t, cosf, sinf, sign):
    lane = lax.broadcasted_iota(jnp.int32, t.shape, 1)
    rot = jnp.where(lane < MISC_ROPE + QK_ROPE // 2, -pltpu.roll(t, HEAD_PAD - QK_ROPE // 2, 1), pltpu.roll(t, QK_ROPE // 2, 1))
    return t * cosf + sign * (rot * sinf)


def _rope_tables(pos, invf):
    s = pos.shape[0]

    def body(pos_ref, invf_ref, cos_ref, sin_ref):
        ang = pos_ref[...].astype(F32) * invf_ref[...]
        cos_ref[...] = jnp.cos(ang)
        sin_ref[...] = jnp.sin(ang)

    return pl.pallas_call(
        body, name="rope_tables", grid=(s // TM,),
        in_specs=[_rows(TM, 1), _full((1, HEAD_PAD))],
        out_specs=[_rows(TM, HEAD_PAD), _rows(TM, HEAD_PAD)],
        out_shape=[jax.ShapeDtypeStruct((s, HEAD_PAD), F32)] * 2,
    )(pos, invf)


def _inproj_fwd(h, nw, win):
    s = h.shape[0]

    def body(h_ref, nw_ref, w_ref, ub_ref, cq_ref, ckv_ref, misc_ref, z_ref, xbc_ref):
        ub = _rms_fwd(h_ref[...], nw_ref[...]).astype(BF16)
        ub_ref[...] = ub
        proj = _dot(ub, w_ref[...])
        cq_ref[...] = proj[:, 0:768]
        ckv_ref[...] = proj[:, 768:1024]
        misc_ref[...] = proj[:, 1024:1152]
        z_ref[...] = proj[:, 1152:1664]
        xbc_ref[...] = proj[:, 1664:2688]

    widths = (768, 256, 128, 512, 1024)
    return pl.pallas_call(
        body, name="inproj_fwd", grid=(s // TM,),
        in_specs=[_rows(TM, D_MODEL), _full((1, D_MODEL)), _resident((D_MODEL, IN_PAD))],
        out_specs=[_rows(TM, D_MODEL)] + [_rows(TM, w) for w in widths],
        out_shape=[jax.ShapeDtypeStruct((s, D_MODEL), BF16)] + [jax.ShapeDtypeStruct((s, w), F32) for w in widths],
        compiler_params=_params(),
    )(h, nw, win)


def _qkv_fwd(cq, ckv, misc, qnw, kvnw, wuq, wkv, cosf, sinf):
    s = cq.shape[0]

    def body(cq_ref, ckv_ref, misc_ref, qnw_ref, kvnw_ref, wuq_ref, wkv_ref, cos_ref, sin_ref,
             cqn_ref, ckvn_ref, q_ref, k_ref, v_ref):
        cosf, sinf = cos_ref[...], sin_ref[...]
        cqn = _rms_fwd(cq_ref[...], qnw_ref[...]).astype(BF16)
        cqn_ref[...] = cqn
        q = _dot(cqn, wuq_ref[...])
        ckvn = _rms_fwd(ckv_ref[...], kvnw_ref[...]).astype(BF16)
        ckvn_ref[...] = ckvn
        kv = _dot(ckvn, wkv_ref[...])
        m = misc_ref[...]
        lane = lax.broadcasted_iota(jnp.int32, m.shape, 1)
        in_rope = jnp.logical_and(lane >= MISC_ROPE, lane < MISC_ROPE + QK_ROPE)
        kr = jnp.where(in_rope, _rope(m, cosf, sinf, 1.0), 0.0)
        for hd in range(MLA_HEADS):
            cols = slice(hd * HEAD_PAD, (hd + 1) * HEAD_PAD)
            q_ref[:, cols] = _rope(q[:, cols], cosf, sinf, 1.0).astype(BF16)
            k_ref[:, cols] = (kv[:, cols] + kr).astype(BF16)
        vv = kv[:, MLA_HEADS * HEAD_PAD:]
        vlane = lax.broadcasted_iota(jnp.int32, vv.shape, 1)
        ones_at = jnp.where((vlane // HEAD_PAD) % 2 == 0, V_DIM, 0)
        v_ref[...] = jnp.where(vlane % HEAD_PAD == ones_at, 1.0, vv).astype(BF16)

    wide = MLA_HEADS * HEAD_PAD
    return pl.pallas_call(
        body, name="qkv_fwd", grid=(s // TM,),
        in_specs=[_rows(TM, Q_RANK), _rows(TM, KV_RANK), _rows(TM, HEAD_PAD), _full((1, Q_RANK)), _full((1, KV_RANK)),
                  _resident((Q_RANK, wide)), _resident((KV_RANK, 2 * wide)), _rows(TM, HEAD_PAD), _rows(TM, HEAD_PAD)],
        out_specs=[_rows(TM, Q_RANK), _rows(TM, KV_RANK), _rows(TM, wide), _rows(TM, wide), _rows(TM, wide)],
        out_shape=[jax.ShapeDtypeStruct((s, Q_RANK), BF16), jax.ShapeDtypeStruct((s, KV_RANK), BF16)]
        + [jax.ShapeDtypeStruct((s, wide), BF16)] * 3,
        compiler_params=_params(),
    )(cq, ckv, misc, qnw, kvnw, wuq, wkv, cosf, sinf)


def _chunk_bias(t, keys_on_rows=False):
    row = lax.broadcasted_iota(jnp.int32, (t, 1), 0) // CHUNK
    col = lax.broadcasted_iota(jnp.int32, (1, t), 1) // CHUNK
    return jnp.where((row <= col) if keys_on_rows else (col <= row), 0.0, -jnp.inf).astype(F32)


def _attn_fwd(q, k, v, gather=()):
    s = q.shape[0]
    t = ATT_T
    nq = s // t
    pair = ATT_G * HEAD_PAD
    ng = len(gather)

    def body(q_ref, k_ref, v_ref, *rest):
        g_in, (o_ref, lse_ref), g_out = rest[:ng], rest[ng:ng + 2], rest[ng + 2:2 * ng + 2]
        m_s, acc_s, bias_s = rest[2 * ng + 2:2 * ng + 5]
        qi = pl.program_id(1)
        group, groups = pl.program_id(0), MLA_HEADS // ATT_G

        @pl.when(jnp.logical_and(group == 0, qi == 0))
        def _():
            bias_s[...] = _chunk_bias(t)

        _hosted_gather(g_in, g_out, rest[2 * ng + 5:],
                       jnp.logical_and(group == 0, qi == 0),
                       jnp.logical_and(group == groups - 1, qi == min(3 * nq // 4 + 1, nq - 1)),
                       jnp.logical_and(group == groups - 1, qi == nq - 1))
        m_s[...] = jnp.full(m_s.shape, -jnp.inf, F32)
        acc_s[...] = jnp.zeros(acc_s.shape, F32)

        def step(kb, masked):
            r0 = pl.multiple_of(kb * t, t)

            def scores(hh):
                cols = slice(hh * HEAD_PAD, (hh + 1) * HEAD_PAD)
                return _dot_nt(q_ref[:, cols], k_ref[pl.ds(r0, t), cols])

            def soft(hh, raw):
                sc = raw * ATT_SCALE_LOG2
                if masked:
                    sc = sc + bias_s[...]
                m_old = m_s[hh]
                m_new = jnp.maximum(m_old, jnp.max(sc, axis=-1, keepdims=True))
                alpha = jnp.exp2(m_old - m_new)
                p = jnp.exp2(sc - jnp.tile(m_new, (1, t // HEAD_PAD)))
                m_s[hh] = m_new
                return alpha, p.astype(BF16)

            def update(hh, alpha, p):
                cols = slice(hh * HEAD_PAD, (hh + 1) * HEAD_PAD)
                acc_s[hh] = alpha * acc_s[hh] + _dot(p, v_ref[pl.ds(r0, t), cols])

            raw, ap = [None] * ATT_G, [None] * ATT_G
            raw[0] = scores(0)
            for hh in range(ATT_G):
                if hh + 1 < ATT_G:
                    raw[hh + 1] = scores(hh + 1)
                ap[hh] = soft(hh, raw[hh])
                if hh >= 1:
                    update(hh - 1, *ap[hh - 1])
            update(ATT_G - 1, *ap[ATT_G - 1])

        def loop(i, c):
            step(2 * i, False)
            step(2 * i + 1, False)
            return c

        lax.fori_loop(0, qi // 2, loop, 0)

        @pl.when(qi % 2 == 1)
        def _():
            step(qi - 1, False)

        step(qi, True)
        for hh in range(ATT_G):
            cols = slice(hh * HEAD_PAD, (hh + 1) * HEAD_PAD)
            acc = acc_s[hh]
            ones_at = V_DIM * (1 - hh % 2)
            l = jnp.broadcast_to(acc[:, ones_at:ones_at + 1], acc.shape)
            o_ref[:, cols] = (acc / l).astype(BF16)
            lse_ref[hh] = (m_s[hh] + jnp.log(l) * LOG2E).T[0:8, :]

    outs = pl.pallas_call(
        body, name="attn_fwd_gather" if ng else "attn_fwd", grid=(MLA_HEADS // ATT_G, nq),
        in_specs=[pl.BlockSpec((t, pair), lambda h, i: (i, h)),
                  pl.BlockSpec((s, pair), lambda h, i: (0, h), pipeline_mode=pl.Buffered(1)),
                  pl.BlockSpec((s, pair), lambda h, i: (0, h), pipeline_mode=pl.Buffered(1))] + [_ANY] * ng,
        out_specs=[pl.BlockSpec((t, pair), lambda h, i: (i, h)),
                   pl.BlockSpec((ATT_G, 8, t), lambda h, i: (h, 0, i))] + [_ANY] * ng,
        out_shape=[jax.ShapeDtypeStruct((s, MLA_HEADS * HEAD_PAD), BF16), jax.ShapeDtypeStruct((MLA_HEADS, 8, s), F32)]
        + _comm_out_shapes("gather", gather),
        scratch_shapes=[pltpu.VMEM((ATT_G, t, HEAD_PAD), F32), pltpu.VMEM((ATT_G, t, HEAD_PAD), F32), pltpu.VMEM((t, t), F32)]
        + (_comm_scratch(ng) if ng else []),
        compiler_params=_params(),
    )(q, k, v, *gather)
    return outs[0], outs[1], list(outs[2:])


def _interleave(stages):
    live = list(stages)
    while live:
        still = []
        for g in live:
            try:
                next(g)
                still.append(g)
            except StopIteration:
                pass
        live = still


def _ssd_consts():
    emisc = np.zeros((HEAD_PAD, SSD_INNER), np.float32)
    for hd in range(SSD_HEADS):
        emisc[MISC_DT + hd, hd * SSD_P:(hd + 1) * SSD_P] = 1.0
    idx = np.arange(CHUNK)
    tri = (idx[:, None] >= idx[None, :]).astype(np.float32)
    return tuple(jnp.asarray(m, BF16) for m in (emisc, emisc.T.copy(), tri, tri.T.copy()))


def _ssd_chunk_common(cc, misc, emisc, tri, trit, dtb, a_exp):
    sig = jax.nn.sigmoid(cc)
    xa = cc * sig
    dt = jax.nn.softplus(_dot01(misc, emisc) + dtb)
    a = dt * a_exp
    acs = _dot01(a, tri, left=True)
    acs_t = _dot01(a, trit, dot=_dot_tn)
    alast = acs[CHUNK - 1:CHUNK, :]
    return xa, sig, dt, acs, acs_t, alast


def _decay(acs, acs_t, hd):
    row = lax.broadcasted_iota(jnp.int32, (CHUNK, CHUNK), 0)
    col = lax.broadcasted_iota(jnp.int32, (CHUNK, CHUNK), 1)
    diff = acs[:, hd * SSD_P:hd * SSD_P + 1] - acs_t[hd * SSD_P:hd * SSD_P + 1, :]
    return jnp.exp(jnp.where(row >= col, diff, -jnp.inf))


def _half_mask(hh):
    lane = lax.broadcasted_iota(jnp.int32, (CHUNK, 2 * SSD_P), 1)
    return (lane >= SSD_P) if hh else (lane < SSD_P)


def _gate_norm(y, zz):
    sg = jax.nn.sigmoid(zz)
    yz = y * (zz * sg)
    outs, rs = [], []
    half = SSD_INNER // SSD_GROUPS
    for g in range(SSD_GROUPS):
        yg = yz[:, g * half:(g + 1) * half]
        r = lax.rsqrt(jnp.mean(yg * yg, axis=-1, keepdims=True) + EPS)
        outs.append(yg * r)
        rs.append(r)
    return sg, jnp.concatenate(outs, axis=1), rs


def _ssd_fwd(xraw, misc, z, cw, cb, dtb, a_exp, d_exp, nw, consts):
    s = xraw.shape[0]
    nb = s // SSD_ROWS
    ncb = SSD_ROWS // CHUNK
    emisc, _, tri, trit = consts

    def body(x_ref, misc_ref, z_ref, cw_ref, cb_ref, dtb_ref, a_ref, d_ref, nw_ref, emisc_ref, tri_ref, trit_ref,
             c_ref, prev_ref, ypre_ref, yssd_ref, tail_s, state_s):
        i = pl.program_id(0)

        @pl.when(i == 0)
        def _():
            tail_s[...] = jnp.zeros(tail_s.shape, F32)
            state_s[...] = jnp.zeros(state_s.shape, F32)

        x = x_ref[...]
        xext = jnp.concatenate([tail_s[...], x], axis=0)
        acc = x * cw_ref[CONV_W - 1:CONV_W, :] + cb_ref[...]
        for j in range(1, CONV_W):
            acc = acc + pltpu.roll(xext, j, 0)[8:, :] * cw_ref[CONV_W - 1 - j:CONV_W - j, :]
        tail_s[...] = x[SSD_ROWS - 8:, :]
        c_ref[...] = acc

        def chunk(ci):
            r0 = ci * CHUNK
            xa, _, dt, acs, acs_t, alast = _ssd_chunk_common(
                c_ref[pl.ds(r0, CHUNK), :], misc_ref[pl.ds(r0, CHUNK), :], emisc_ref[...], tri_ref[...], trit_ref[...],
                dtb_ref[...], a_ref[...])
            yield
            xs = xa[:, :SSD_INNER]
            xdt = xs * dt
            wgt = (xdt * jnp.exp(alast - acs)).astype(BF16)
            e = jnp.exp(acs)
            ys, new_states, cms = [], [], []
            for g in range(SSD_GROUPS):
                bm = xa[:, SSD_INNER + g * SSD_N:SSD_INNER + (g + 1) * SSD_N].astype(BF16)
                cm = xa[:, SSD_INNER + SSD_GROUPS * SSD_N + g * SSD_N:SSD_INNER + SSD_GROUPS * SSD_N + (g + 1) * SSD_N].astype(BF16)
                cms.append(cm)
                cb_g = _dot_nt(cm, bm)
                gl = slice(g * 256, (g + 1) * 256)
                new_states.append(_dot_tn(bm, wgt[:, gl]))
                for jj in range(2):
                    pair = 2 * g + jj
                    xp = xdt[:, pair * 128:(pair + 1) * 128]
                    yp = None
                    for hh in range(2):
                        sc = (cb_g * _decay(acs, acs_t, 2 * pair + hh)).astype(BF16)
                        term = _dot(sc, jnp.where(_half_mask(hh), xp, 0.0).astype(BF16))
                        yp = term if yp is None else yp + term
                    ys.append(yp)
                yield
            prev = state_s[...]
            prev_ref[ci] = prev
            yoff = jnp.concatenate([_dot(cms[g], prev[:, g * 256:(g + 1) * 256].astype(BF16)) for g in range(SSD_GROUPS)],
                                   axis=1) * e
            state_s[...] = prev * jnp.exp(alast) + jnp.concatenate(new_states, axis=1)
            yield
            y = jnp.concatenate(ys, axis=1) + yoff + d_ref[...] * xs
            ypre_ref[pl.ds(r0, CHUNK), :] = y
            _, yn, _ = _gate_norm(y, z_ref[pl.ds(r0, CHUNK), :])
            yssd_ref[pl.ds(r0, CHUNK), :] = (yn * nw_ref[...]).astype(BF16)

        _interleave([chunk(ci) for ci in range(ncb)])

    return pl.pallas_call(
        body, name="ssd_fwd", grid=(nb,),
        in_specs=[_rows(SSD_ROWS, CONV_DIM), _rows(SSD_ROWS, HEAD_PAD), _rows(SSD_ROWS, SSD_INNER),
                  _full((CONV_W, CONV_DIM)), _full((1, CONV_DIM)), _full((1, SSD_INNER)), _full((1, SSD_INNER)),
                  _full((1, SSD_INNER)), _full((1, SSD_INNER)), _full((HEAD_PAD, SSD_INNER)), _full((CHUNK, CHUNK)),
                  _full((CHUNK, CHUNK))],
        out_specs=[_rows(SSD_ROWS, CONV_DIM), pl.BlockSpec((ncb, SSD_N, SSD_INNER), lambda i: (i, 0, 0)),
                   _rows(SSD_ROWS, SSD_INNER), _rows(SSD_ROWS, SSD_INNER)],
        out_shape=[jax.ShapeDtypeStruct((s, CONV_DIM), F32), jax.ShapeDtypeStruct((s // CHUNK, SSD_N, SSD_INNER), F32),
                   jax.ShapeDtypeStruct((s, SSD_INNER), F32), jax.ShapeDtypeStruct((s, SSD_INNER), BF16)],
        scratch_shapes=[pltpu.VMEM((8, CONV_DIM), F32), pltpu.VMEM((SSD_N, SSD_INNER), F32)],
        compiler_params=_params(),
    )(xraw, misc, z, cw, cb, dtb, a_exp, d_exp, nw, emisc, tri, trit)


def _outproj_fwd(oe, yssd, wout, h, nw):
    s = h.shape[0]
    wide = MLA_HEADS * HEAD_PAD

    def body(oe_ref, y_ref, w_ref, h_ref, nw_ref, mixed_ref, h1_ref):
        mixed = _dot(oe_ref[...], w_ref[0:wide, :]) + _dot(y_ref[...], w_ref[wide:, :])
        mixed_ref[...] = mixed
        h1_ref[...] = h_ref[...] + _rms_fwd(mixed, nw_ref[...])

    return pl.pallas_call(
        body, name="outproj_fwd", grid=(s // TM,),
        in_specs=[_rows(TM, wide), _rows(TM, SSD_INNER), _resident((wide + SSD_INNER, D_MODEL)), _rows(TM, D_MODEL),
                  _full((1, D_MODEL))],
        out_specs=[_rows(TM, D_MODEL), _rows(TM, D_MODEL)],
        out_shape=[jax.ShapeDtypeStruct((s, D_MODEL), F32)] * 2,
        compiler_params=_params(),
    )(oe, yssd, wout, h, nw)


def _mlp_fwd(h1, prew, wup, wdown, postw, target=None):
    s = h1.shape[0]
    fb = D_FF // N_DEV
    last = target is not None

    def body(h_ref, prew_ref, up_ref, down_ref, postw_ref, *rest):
        target_ref, (mb_ref, ab_ref, d_ref, out_ref) = (rest[0] if last else None), rest[last:last + 4]
        hh = h_ref[...]
        mb = _rms_fwd(hh, prew_ref[...]).astype(BF16)
        mb_ref[...] = mb
        d = jnp.zeros((TM, D_MODEL), F32)
        for j in range(N_DEV):
            a = jnp.maximum(_dot(mb, up_ref[j]), 0.0)
            ab_ref[j] = a.astype(BF16)
            d = d + _dot(jnp.square(a).astype(BF16), down_ref[j])
        d_ref[...] = d
        h2 = hh + _rms_fwd(d, postw_ref[...])
        if last:
            diff = h2 - target_ref[...]
            out_ref[...] = diff * (1.0 / D_MODEL)
            part = 0.5 * jnp.sum(jnp.mean(diff * diff, axis=-1, keepdims=True), axis=0, keepdims=True)
            _acc_rows(rest[-1], part, pl.program_id(0) == 0)
        else:
            out_ref[...] = h2

    stacked = pl.BlockSpec((N_DEV, TM, fb), lambda i: (0, i, 0))
    return pl.pallas_call(
        body, name="mlp_fwd_loss" if last else "mlp_fwd", grid=(s // TM,),
        in_specs=[_rows(TM, D_MODEL), _full((1, D_MODEL)), _resident((N_DEV, D_MODEL, fb)), _resident((N_DEV, fb, D_MODEL)),
                  _full((1, D_MODEL))] + ([_rows(TM, D_MODEL)] if last else []),
        out_specs=[_rows(TM, D_MODEL), stacked, _rows(TM, D_MODEL), _rows(TM, D_MODEL)] + ([_full((1, 1))] if last else []),
        out_shape=[jax.ShapeDtypeStruct((s, D_MODEL), BF16), jax.ShapeDtypeStruct((N_DEV, s, fb), BF16),
                   jax.ShapeDtypeStruct((s, D_MODEL), F32), jax.ShapeDtypeStruct((s, D_MODEL), F32)]
        + ([jax.ShapeDtypeStruct((1, 1), F32)] if last else []),
        compiler_params=_params(),
    )(h1, prew, wup, wdown, postw, *([target] if last else []))


def _mlp_bwd(dh2, d, h1, ab, prew, wup, wdown, postw):
    s = dh2.shape[0]
    fb = D_FF // N_DEV
    tm = TM // 2

    def body(dh2_ref, d_ref, h1_ref, ab_ref, prew_ref, up_ref, down_ref, postw_ref,
             dh1_ref, da_ref, dd_ref, gpost_ref, gpre_ref):
        first = pl.program_id(0) == 0
        dh2 = dh2_ref[...]
        dd, gpost = _rms_bwd(d_ref[...], postw_ref[...], dh2)
        _acc_rows(gpost_ref, gpost, first)
        ddb = dd.astype(BF16)
        dd_ref[...] = ddb

        def d_relu_squared(j):
            return _dot_nt(ddb, down_ref[j])

        def pointwise(j, dr):
            da = (dr * (2.0 * ab_ref[j].astype(F32))).astype(BF16)
            da_ref[j] = da
            return da

        dm = jnp.zeros((tm, D_MODEL), F32)
        nxt, da_prev = d_relu_squared(0), None
        for j in range(N_DEV):
            cur = nxt
            if j + 1 < N_DEV:
                nxt = d_relu_squared(j + 1)
            da = pointwise(j, cur)
            if da_prev is not None:
                dm = dm + _dot_nt(da_prev, up_ref[j - 1])
            da_prev = da
        dm = dm + _dot_nt(da_prev, up_ref[N_DEV - 1])
        dx, gpre = _rms_bwd(h1_ref[...], prew_ref[...], dm)
        _acc_rows(gpre_ref, gpre, first)
        dh1_ref[...] = dh2 + dx

    stacked = pl.BlockSpec((N_DEV, tm, fb), lambda i: (0, i, 0))
    return pl.pallas_call(
        body, name="mlp_bwd", grid=(s // tm,),
        in_specs=[_rows(tm, D_MODEL)] * 3 + [stacked, _full((1, D_MODEL)), _resident((N_DEV, D_MODEL, fb)),
                                              _resident((N_DEV, fb, D_MODEL)), _full((1, D_MODEL))],
        out_specs=[_rows(tm, D_MODEL), stacked, _rows(tm, D_MODEL), _full((1, D_MODEL)), _full((1, D_MODEL))],
        out_shape=[jax.ShapeDtypeStruct((s, D_MODEL), F32), jax.ShapeDtypeStruct((N_DEV, s, fb), BF16),
                   jax.ShapeDtypeStruct((s, D_MODEL), BF16), jax.ShapeDtypeStruct((1, D_MODEL), F32),
                   jax.ShapeDtypeStruct((1, D_MODEL), F32)],
        compiler_params=_params(),
    )(dh2, d, h1, ab, prew, wup, wdown, postw)


def _matmul_tn(a, b, name, tk=TK_DW):
    s, m = a.shape
    n = b.shape[1]
    tn = n if n <= 1024 else (n // 2 if (n // 2) % 128 == 0 else n // 3)
    tk = min(tk, s)
    assert n % tn == 0 and tn % 128 == 0 and s % tk == 0

    def body(a_ref, b_ref, o_ref):
        part = _dot_tn(a_ref[...], b_ref[...])

        @pl.when(pl.program_id(1) == 0)
        def _():
            o_ref[...] = part

        @pl.when(pl.program_id(1) != 0)
        def _():
            o_ref[...] += part

    return pl.pallas_call(
        body, name=name, grid=(n // tn, s // tk),
        in_specs=[pl.BlockSpec((tk, m), lambda j, k: (k, 0)), pl.BlockSpec((tk, tn), lambda j, k: (k, j))],
        out_specs=pl.BlockSpec((m, tn), lambda j, k: (0, j)),
        out_shape=jax.ShapeDtypeStruct((m, n), F32),
        compiler_params=_params(),
    )(a, b)


def _matmul_tn_stacked(a, b, name, a_stacked, square_a=False, tk=TK_DW):
    tk = min(tk, a.shape[-2])
    if a_stacked:
        _, s, m = a.shape
        n = b.shape[1]
        in_specs = [pl.BlockSpec((1, tk, m), lambda j, k: (j, k, 0)), pl.BlockSpec((tk, n), lambda j, k: (k, 0))]
    else:
        s, m = a.shape
        n = b.shape[2]
        in_specs = [pl.BlockSpec((tk, m), lambda j, k: (k, 0)), pl.BlockSpec((1, tk, n), lambda j, k: (j, k, 0))]

    nk = s // tk

    def body(a_ref, b_ref, o_ref, acc_s):
        av = a_ref[0] if a_stacked else a_ref[...]
        bv = b_ref[...] if a_stacked else b_ref[0]
        if square_a:
            av = jnp.square(av.astype(F32)).astype(BF16)
        part = _dot_tn(av, bv)
        k = pl.program_id(1)

        @pl.when(k == 0)
        def _():
            acc_s[...] = part

        @pl.when(jnp.logical_and(k != 0, k != nk - 1))
        def _():
            acc_s[...] += part

        @pl.when(k == nk - 1)
        def _():
            o_ref[0] = (part if nk == 1 else acc_s[...] + part).astype(BF16)

    return pl.pallas_call(
        body, name=name, grid=(N_DEV, nk),
        in_specs=in_specs,
        out_specs=pl.BlockSpec((1, m, n), lambda j, k: (j, 0, 0)),
        out_shape=jax.ShapeDtypeStruct((N_DEV, m, n), BF16),
        scratch_shapes=[pltpu.VMEM((m, n), F32)],
        compiler_params=_params(),
    )(a, b)


def _outproj_bwd(dh1, mixed, nw, wout, oe):
    s = dh1.shape[0]
    wide = MLA_HEADS * HEAD_PAD

    def body(dh1_ref, mixed_ref, nw_ref, w_ref, oe_ref, dmix_ref, doe_ref, dy_ref, gnw_ref, delta_ref):
        dmix, gnw = _rms_bwd(mixed_ref[...], nw_ref[...], dh1_ref[...])
        _acc_rows(gnw_ref, gnw, pl.program_id(0) == 0)
        dmb = dmix.astype(BF16)
        dmix_ref[...] = dmb
        doe_ref[...] = _dot_nt(dmb, w_ref[0:wide, :]).astype(BF16)
        dy_ref[...] = _dot_nt(dmb, w_ref[wide:, :])
        ones = jnp.ones((8, HEAD_PAD), BF16)
        for hd in range(MLA_HEADS):
            cols = slice(hd * HEAD_PAD, (hd + 1) * HEAD_PAD)
            prod = oe_ref[:, cols].astype(F32) * doe_ref[:, cols].astype(F32)
            delta_ref[hd] = _dot01(prod, ones, dot=_dot_nt, left=True)

    return pl.pallas_call(
        body, name="outproj_bwd", grid=(s // TM,),
        in_specs=[_rows(TM, D_MODEL), _rows(TM, D_MODEL), _full((1, D_MODEL)), _resident((wide + SSD_INNER, D_MODEL)),
                  _rows(TM, wide)],
        out_specs=[_rows(TM, D_MODEL), _rows(TM, wide), _rows(TM, SSD_INNER), _full((1, D_MODEL)),
                   pl.BlockSpec((MLA_HEADS, 8, TM), lambda i: (0, 0, i))],
        out_shape=[jax.ShapeDtypeStruct((s, D_MODEL), BF16), jax.ShapeDtypeStruct((s, wide), BF16),
                   jax.ShapeDtypeStruct((s, SSD_INNER), F32), jax.ShapeDtypeStruct((1, D_MODEL), F32),
                   jax.ShapeDtypeStruct((MLA_HEADS, 8, s), F32)],
        compiler_params=_params(),
    )(dh1, mixed, nw, wout, oe)


def _attn_bwd(q, k, v, do, lse, delta, exchange=()):
    s = q.shape[0]
    t = ATT_T
    nq = s // t
    pair = 2 * HEAD_PAD
    ne = len(exchange)

    def body(q_ref, k_ref, v_ref, do_ref, lse_ref, delta_ref, *rest):
        e_in, (dq_ref, dk_ref, dv_ref), e_out = rest[:ne], rest[ne:ne + 3], rest[ne + 3:2 * ne + 3]
        dk_s, dv_s, bias_s = rest[2 * ne + 3:2 * ne + 6]
        kb = pl.program_id(1)
        _hosted_comm("exchange", e_in, e_out, rest[2 * ne + 6:],
                     jnp.logical_and(pl.program_id(0) == 0, kb == 0),
                     jnp.logical_and(pl.program_id(0) == MLA_HEADS // 2 - 1, kb == nq - 1))

        @pl.when(jnp.logical_and(pl.program_id(0) == 0, kb == 0))
        def _():
            bias_s[...] = _chunk_bias(t, keys_on_rows=True)

        @pl.when(kb == 0)
        def _():
            dq_ref[...] = jnp.zeros(dq_ref.shape, F32)

        def step(qb, diagonal):
            r0 = pl.multiple_of(qb * t, t)
            for hh in range(2):
                cols = slice(hh * HEAD_PAD, (hh + 1) * HEAD_PAD)
                kk = k_ref[:, cols]
                qq = q_ref[pl.ds(r0, t), cols]
                dd = do_ref[pl.ds(r0, t), cols]
                sc = _dot_nt(kk, qq) * ATT_SCALE_LOG2
                if diagonal:
                    sc = sc + bias_s[...]
                p = jnp.exp2(sc - lse_ref[hh, 0:1, pl.ds(r0, t)])
                dv = _dot(p.astype(BF16), dd)
                dp = _dot_nt(v_ref[:, cols], dd)
                ds = (p * (dp - delta_ref[hh, 0:1, pl.ds(r0, t)]) * ATT_SCALE).astype(BF16)
                dk = _dot(ds, qq)
                if diagonal:
                    dv_s[:, cols] = dv
                    dk_s[:, cols] = dk
                else:
                    dv_s[:, cols] += dv
                    dk_s[:, cols] += dk
                dq_ref[pl.ds(r0, t), cols] += _dot_tn(ds, kk)

        def loop(i, c):
            for u in range(ATT_UNROLL):
                step(kb + 1 + u + ATT_UNROLL * i, False)
            return c

        step(kb, True)
        later_tiles = nq - 1 - kb
        lax.fori_loop(0, later_tiles // ATT_UNROLL, loop, 0)
        left = later_tiles % ATT_UNROLL
        for u in range(ATT_UNROLL - 1):
            @pl.when(left > u)
            def _(u=u):
                step(nq - left + u, False)

        dk_ref[...] = dk_s[...].astype(BF16)
        dv_ref[...] = dv_s[...].astype(BF16)

    whole = pl.BlockSpec((s, pair), lambda h, i: (0, h))
    tile = pl.BlockSpec((t, pair), lambda h, i: (i, h))
    rowvec = pl.BlockSpec((2, 8, s), lambda h, i: (h, 0, 0))
    wide = MLA_HEADS * HEAD_PAD
    outs = pl.pallas_call(
        body, name="attn_bwd_exchange" if ne else "attn_bwd", grid=(MLA_HEADS // 2, nq),
        in_specs=[whole, tile, tile, whole, rowvec, rowvec] + [_ANY] * ne,
        out_specs=[whole, tile, tile] + [_ANY] * ne,
        out_shape=[jax.ShapeDtypeStruct((s, wide), F32)] + [jax.ShapeDtypeStruct((s, wide), BF16)] * 2
        + _comm_out_shapes("exchange", exchange),
        scratch_shapes=[pltpu.VMEM((t, pair), F32), pltpu.VMEM((t, pair), F32), pltpu.VMEM((t, t), F32)]
        + (_comm_scratch(ne) if ne else []),
        compiler_params=_params(),
    )(q, k, v, do, lse, delta, *exchange)
    return outs[0], outs[1], outs[2], list(outs[3:])


def _ssd_bwd(dy, ypre, z, c, xraw, misc, prev, cw, dtb, a_exp, d_exp, nw, consts):
    s = dy.shape[0]
    nb = s // SSD_ROWS
    ncb = SSD_ROWS // CHUNK
    emisc, emisc_t, tri, trit = consts

    def body(dy_ref, ypre_ref, z_ref, c_ref, x_ref, misc_ref, prev_ref, cw_ref, dtb_ref, a_ref, d_ref, nw_ref,
             emisc_ref, emisct_ref, tri_ref, trit_ref,
             dz_ref, dx_ref, dmisc_ref, gnw_ref, gd_ref, galog_ref, gdtb_ref, gcw_ref, gcb_ref,
             dst_s, dc_s, head_s):
        i = pl.program_id(0)
        first = i == 0

        @pl.when(first)
        def _():
            dst_s[...] = jnp.zeros(dst_s.shape, F32)
            head_s[...] = jnp.zeros(head_s.shape, F32)
            gnw_ref[...] = jnp.zeros(gnw_ref.shape, F32)
            gd_ref[...] = jnp.zeros(gd_ref.shape, F32)
            galog_ref[...] = jnp.zeros(galog_ref.shape, F32)
            gdtb_ref[...] = jnp.zeros(gdtb_ref.shape, F32)

        a_exp_v = a_ref[...]
        a8 = _dot01(a_exp_v, emisct_ref[...]) * (1.0 / SSD_P)

        def chunk(ci):
            r0 = ci * CHUNK
            cc = c_ref[pl.ds(r0, CHUNK), :]
            mm = misc_ref[pl.ds(r0, CHUNK), :]
            xa, sig_c, dt, acs, acs_t, alast = _ssd_chunk_common(cc, mm, emisc_ref[...], tri_ref[...], trit_ref[...],
                                                              dtb_ref[...], a_exp_v)
            yield
            xs = xa[:, :SSD_INNER]
            xdt = xs * dt
            y = ypre_ref[pl.ds(r0, CHUNK), :]
            zz = z_ref[pl.ds(r0, CHUNK), :]
            sg, yn, rs = _gate_norm(y, zz)
            dyo = dy_ref[pl.ds(r0, CHUNK), :]
            gnw_ref[...] += jnp.sum(dyo * yn, axis=0, keepdims=True)
            dyn = dyo * nw_ref[...]
            half = SSD_INNER // SSD_GROUPS
            dyz_parts = []
            for g in range(SSD_GROUPS):
                gl = slice(g * half, (g + 1) * half)
                dyz_parts.append(rs[g] * (dyn[:, gl] - yn[:, gl] * jnp.mean(dyn[:, gl] * yn[:, gl], axis=-1, keepdims=True)))
            dyz = jnp.concatenate(dyz_parts, axis=1)
            dz_ref[pl.ds(r0, CHUNK), :] = dyz * y * (sg * (1.0 + zz * (1.0 - sg)))
            dyp = dyz * (zz * sg)
            dypb = dyp.astype(BF16)
            gd_ref[...] += jnp.sum(dyp * xs, axis=0, keepdims=True)
            yield
            prev = prev_ref[ci]
            cd = jnp.exp(alast)
            e = jnp.exp(acs)
            dsx = jnp.exp(alast - acs)
            wgt = (xdt * dsx).astype(BF16)
            dze = (dyp * e).astype(BF16)
            dprev_parts, diag_all, dbm, dcm, yoff_parts, bms = [], [], [], [], [], []
            lane8 = lax.broadcasted_iota(jnp.int32, (CHUNK, HEAD_PAD), 1)
            diag8 = jnp.zeros((CHUNK, HEAD_PAD), F32)
            for g in range(SSD_GROUPS):
                gl = slice(g * 256, (g + 1) * 256)
                bm = xa[:, SSD_INNER + g * SSD_N:SSD_INNER + (g + 1) * SSD_N].astype(BF16)
                cm = xa[:, SSD_INNER + SSD_GROUPS * SSD_N + g * SSD_N:SSD_INNER + SSD_GROUPS * SSD_N + (g + 1) * SSD_N].astype(BF16)
                bms.append(bm)
                prev_g = prev[:, gl].astype(BF16)
                dcm_g = _dot_nt(dze[:, gl], prev_g)
                dprev_parts.append(_dot_tn(cm, dze[:, gl]))
                cb_g = _dot_nt(cm, bm)
                dcb = jnp.zeros((CHUNK, CHUNK), F32)
                diag_parts = []
                for jj in range(2):
                    pair = 2 * g + jj
                    pl_ = slice(pair * 128, (pair + 1) * 128)
                    xp = xdt[:, pl_]
                    dyp_p = dypb[:, pl_]
                    dxp = jnp.zeros((CHUNK, 128), F32)
                    for hh in range(2):
                        hd = 2 * pair + hh
                        dec = _decay(acs, acs_t, hd)
                        xm = jnp.where(_half_mask(hh), xp, 0.0).astype(BF16)
                        dsc = _dot_nt(dyp_p, xm) * dec
                        dcb = dcb + dsc
                        sc = (cb_g * dec).astype(BF16)
                        dxp = dxp + jnp.where(_half_mask(hh), _dot_tn(sc, dyp_p), 0.0)
                        dm = dsc * cb_g
                        diag8 = diag8 + jnp.where(lane8 == MISC_DT + hd, jnp.sum(dm - dm.T, axis=1, keepdims=True), 0.0)
                    diag_parts.append(dxp)
                dcbb = dcb.astype(BF16)
                dcm.append(dcm_g + _dot(dcbb, bm))
                dbm.append(_dot_tn(dcbb, cm))
                diag_all.append(jnp.concatenate(diag_parts, axis=1))
                yoff_parts.append(_dot(cm, prev_g) * e[:, gl])
                yield
            dst = dst_s[...]
            glast = jnp.sum(dst * prev, axis=0, keepdims=True) * cd
            dxdt_state_parts = []
            for g in range(SSD_GROUPS):
                gl = slice(g * 256, (g + 1) * 256)
                dst_g = dst[:, gl].astype(BF16)
                dxdt_state_parts.append(_dot(bms[g], dst_g) * dsx[:, gl])
                dbm[g] = dbm[g] + _dot_nt(wgt[:, gl], dst_g)
            dst_s[...] = dst * cd + jnp.concatenate(dprev_parts, axis=1)
            yield
            dxdt_state = jnp.concatenate(dxdt_state_parts, axis=1)
            dxdt = jnp.concatenate(diag_all, axis=1) + dxdt_state
            dacs = dyp * jnp.concatenate(yoff_parts, axis=1) - xdt * dxdt_state
            last = jnp.sum(xdt * dxdt_state, axis=0, keepdims=True) + glast
            row = lax.broadcasted_iota(jnp.int32, (CHUNK, SSD_INNER), 0)
            dacs = dacs + jnp.where(row == CHUNK - 1, last, 0.0)
            dacs8 = _dot01(dacs, emisct_ref[...]) + diag8
            da8 = _dot01(dacs8, trit_ref[...], left=True)
            ddt8 = da8 * a8 + _dot01(dxdt * xs, emisct_ref[...])
            yield
            dtr8 = mm + _dot01(dtb_ref[...], emisct_ref[...]) * (1.0 / SSD_P)
            dt8 = jax.nn.softplus(dtr8)
            lane = lax.broadcasted_iota(jnp.int32, (CHUNK, HEAD_PAD), 1)
            on_dt = jnp.logical_and(lane >= MISC_DT, lane < MISC_DT + SSD_HEADS)
            ddtr8 = jnp.where(on_dt, ddt8 * jax.nn.sigmoid(dtr8), 0.0)
            dmisc_ref[pl.ds(r0, CHUNK), :] = ddtr8
            gdtb_ref[...] += jnp.sum(ddtr8, axis=0, keepdims=True)
            galog_ref[...] += jnp.sum(jnp.where(on_dt, da8 * dt8, 0.0), axis=0, keepdims=True) * a8
            dxs = d_ref[...] * dyp + dxdt * dt
            dxa = jnp.concatenate([dxs] + dbm + dcm, axis=1)
            dc_s[pl.ds(r0, CHUNK), :] = dxa * (sig_c * (1.0 + cc * (1.0 - sig_c)))

        _interleave([chunk(ci) for ci in reversed(range(ncb))])

        dc = dc_s[...]
        x = x_ref[...]
        dcext = jnp.concatenate([dc, head_s[...]], axis=0)
        dx = dc * cw_ref[CONV_W - 1:CONV_W, :]
        rows = [jnp.sum(dc * x, axis=0, keepdims=True)]
        for j in range(1, CONV_W):
            ahead = pltpu.roll(dcext, SSD_ROWS + 8 - j, 0)[:SSD_ROWS, :]
            dx = dx + ahead * cw_ref[CONV_W - 1 - j:CONV_W - j, :]
            rows.insert(0, jnp.sum(ahead * x, axis=0, keepdims=True))
        dx_ref[...] = dx
        head_s[...] = dc[:8, :]
        gcw = jnp.concatenate(rows, axis=0)

        @pl.when(first)
        def _():
            gcw_ref[...] = gcw
            gcb_ref[...] = jnp.sum(dc, axis=0, keepdims=True)

        @pl.when(jnp.logical_not(first))
        def _():
            gcw_ref[...] += gcw
            gcb_ref[...] += jnp.sum(dc, axis=0, keepdims=True)

    def rev(width):
        return pl.BlockSpec((SSD_ROWS, width), lambda i: (nb - 1 - i, 0))

    return pl.pallas_call(
        body, name="ssd_bwd", grid=(nb,),
        in_specs=[rev(SSD_INNER), rev(SSD_INNER), rev(SSD_INNER), rev(CONV_DIM), rev(CONV_DIM),
                  rev(HEAD_PAD), pl.BlockSpec((ncb, SSD_N, SSD_INNER), lambda i: (nb - 1 - i, 0, 0)),
                  _full((CONV_W, CONV_DIM)), _full((1, SSD_INNER)), _full((1, SSD_INNER)), _full((1, SSD_INNER)),
                  _full((1, SSD_INNER)), _full((HEAD_PAD, SSD_INNER)), _full((SSD_INNER, HEAD_PAD)), _full((CHUNK, CHUNK)),
                  _full((CHUNK, CHUNK))],
        out_specs=[rev(SSD_INNER), rev(CONV_DIM), rev(HEAD_PAD), _full((1, SSD_INNER)), _full((1, SSD_INNER)),
                   _full((1, HEAD_PAD)), _full((1, HEAD_PAD)), _full((CONV_W, CONV_DIM)), _full((1, CONV_DIM))],
        out_shape=[jax.ShapeDtypeStruct((s, SSD_INNER), F32), jax.ShapeDtypeStruct((s, CONV_DIM), F32),
                   jax.ShapeDtypeStruct((s, HEAD_PAD), F32), jax.ShapeDtypeStruct((1, SSD_INNER), F32),
                   jax.ShapeDtypeStruct((1, SSD_INNER), F32), jax.ShapeDtypeStruct((1, HEAD_PAD), F32),
                   jax.ShapeDtypeStruct((1, HEAD_PAD), F32), jax.ShapeDtypeStruct((CONV_W, CONV_DIM), F32),
                   jax.ShapeDtypeStruct((1, CONV_DIM), F32)],
        scratch_shapes=[pltpu.VMEM((SSD_N, SSD_INNER), F32), pltpu.VMEM((SSD_ROWS, CONV_DIM), F32), pltpu.VMEM((8, CONV_DIM), F32)],
        compiler_params=_params(),
    )(dy, ypre, z, c, xraw, misc, prev, cw, dtb, a_exp, d_exp, nw, emisc, emisc_t, tri, trit)


def _qkv_bwd(dq, dk, dv, cq, ckv, qnw, kvnw, wuq, wkv, cosf, sinf):
    s = dq.shape[0]
    wide = MLA_HEADS * HEAD_PAD

    def body(dq_ref, dk_ref, dv_ref, cq_ref, ckv_ref, qnw_ref, kvnw_ref, wuq_ref, wkv_ref, cos_ref, sin_ref,
             dqb_ref, dkvb_ref, dcq_ref, dckv_ref, dmisc_ref, gq_ref, gkv_ref):
        first = pl.program_id(0) == 0
        cosf, sinf = cos_ref[...], sin_ref[...]
        dkr = jnp.zeros((TM, HEAD_PAD), F32)
        for hd in range(MLA_HEADS):
            cols = slice(hd * HEAD_PAD, (hd + 1) * HEAD_PAD)
            dqb_ref[:, cols] = _rope(dq_ref[:, cols], cosf, sinf, -1.0).astype(BF16)
            dkh = dk_ref[:, cols]
            dkvb_ref[:, cols] = dkh.astype(BF16)
            dkr = dkr + dkh
        dkvb_ref[:, wide:] = dv_ref[...].astype(BF16)
        lane = lax.broadcasted_iota(jnp.int32, dkr.shape, 1)
        in_rope = jnp.logical_and(lane >= MISC_ROPE, lane < MISC_ROPE + QK_ROPE)
        dmisc_ref[...] = jnp.where(in_rope, _rope(jnp.where(in_rope, dkr, 0.0), cosf, sinf, -1.0), 0.0)
        dcq, gq = _rms_bwd(cq_ref[...], qnw_ref[...], _dot_nt(dqb_ref[...], wuq_ref[...]))
        dcq_ref[...] = dcq
        _acc_rows(gq_ref, gq, first)
        dckv, gkv = _rms_bwd(ckv_ref[...], kvnw_ref[...], _dot_nt(dkvb_ref[...], wkv_ref[...]))
        dckv_ref[...] = dckv
        _acc_rows(gkv_ref, gkv, first)

    return pl.pallas_call(
        body, name="qkv_bwd", grid=(s // TM,),
        in_specs=[_rows(TM, wide)] * 3 + [_rows(TM, Q_RANK), _rows(TM, KV_RANK), _full((1, Q_RANK)), _full((1, KV_RANK)),
                                          _resident((Q_RANK, wide)), _resident((KV_RANK, 2 * wide)), _rows(TM, HEAD_PAD), _rows(TM, HEAD_PAD)],
        out_specs=[_rows(TM, wide), _rows(TM, 2 * wide), _rows(TM, Q_RANK), _rows(TM, KV_RANK), _rows(TM, HEAD_PAD),
                   _full((1, Q_RANK)), _full((1, KV_RANK))],
        out_shape=[jax.ShapeDtypeStruct((s, wide), BF16), jax.ShapeDtypeStruct((s, 2 * wide), BF16),
                   jax.ShapeDtypeStruct((s, Q_RANK), F32), jax.ShapeDtypeStruct((s, KV_RANK), F32),
                   jax.ShapeDtypeStruct((s, HEAD_PAD), F32), jax.ShapeDtypeStruct((1, Q_RANK), F32),
                   jax.ShapeDtypeStruct((1, KV_RANK), F32)],
        compiler_params=_params(),
    )(dq, dk, dv, cq, ckv, qnw, kvnw, wuq, wkv, cosf, sinf)


def _inproj_bwd(dcq, dckv, dmisc_rope, dmisc_dt, dz, dxbc, h, dh1, nw, win):
    s = h.shape[0]

    def body(dcq_ref, dckv_ref, dma_ref, dmb_ref, dz_ref, dxbc_ref, h_ref, dh1_ref, nw_ref, w_ref, dproj_ref, dh0_ref, gnw_ref):
        dproj_ref[:, 0:768] = dcq_ref[...].astype(BF16)
        dproj_ref[:, 768:1024] = dckv_ref[...].astype(BF16)
        dproj_ref[:, 1024:1152] = (dma_ref[...] + dmb_ref[...]).astype(BF16)
        dproj_ref[:, 1152:1664] = dz_ref[...].astype(BF16)
        dproj_ref[:, 1664:2688] = dxbc_ref[...].astype(BF16)
        du = _dot_nt(dproj_ref[...], w_ref[...])
        dx, gnw = _rms_bwd(h_ref[...], nw_ref[...], du)
        _acc_rows(gnw_ref, gnw, pl.program_id(0) == 0)
        dh0_ref[...] = dh1_ref[...] + dx

    return pl.pallas_call(
        body, name="inproj_bwd", grid=(s // TM,),
        in_specs=[_rows(TM, Q_RANK), _rows(TM, KV_RANK), _rows(TM, HEAD_PAD), _rows(TM, HEAD_PAD), _rows(TM, SSD_INNER),
                  _rows(TM, CONV_DIM), _rows(TM, D_MODEL), _rows(TM, D_MODEL), _full((1, D_MODEL)), _resident((D_MODEL, IN_PAD))],
        out_specs=[_rows(TM, IN_PAD), _rows(TM, D_MODEL), _full((1, D_MODEL))],
        out_shape=[jax.ShapeDtypeStruct((s, IN_PAD), BF16), jax.ShapeDtypeStruct((s, D_MODEL), F32),
                   jax.ShapeDtypeStruct((1, D_MODEL), F32)],
        compiler_params=_params(),
    )(dcq, dckv, dmisc_rope, dmisc_dt, dz, dxbc, h, dh1, nw, win)


def _row_tile(rows, cols):
    cap = max(8, (1 << 18) // max(cols, 128))
    best = None
    for t in range(8, rows + 1, 8):
        if rows % t == 0 and t <= cap:
            best = t
    return best if best is not None else rows


def _adamw(w, g, m, v, name):
    rows, cols = w.shape
    tr = _row_tile(rows, cols)

    def body(w_ref, g_ref, m_ref, v_ref, d_ref, m2_ref, v2_ref):
        gg = g_ref[...]
        m2 = ADAM_B1 * m_ref[...] + (1.0 - ADAM_B1) * gg
        v2 = ADAM_B2 * v_ref[...] + (1.0 - ADAM_B2) * jnp.square(gg)
        m_hat = m2 / (1.0 - ADAM_B1 ** ADAM_STEP)
        v_hat = v2 / (1.0 - ADAM_B2 ** ADAM_STEP)
        d_ref[...] = -ADAM_LR * (m_hat / (jnp.sqrt(v_hat) + ADAM_EPS) + ADAM_WD * w_ref[...])
        m2_ref[...] = m2
        v2_ref[...] = v2

    spec = pl.BlockSpec((tr, cols), lambda i: (i, 0))
    return pl.pallas_call(
        body, name=name, grid=(rows // tr,),
        in_specs=[spec] * 4, out_specs=[spec] * 3,
        out_shape=[jax.ShapeDtypeStruct((rows, cols), F32)] * 3,
    )(w, g, m, v)


def _sum_adamw(slots, w, m, v, name):
    _, rows, cols = w.shape
    tr = _row_tile(rows, cols)
    nb = rows // tr

    def body(s0_ref, s1_ref, w_ref, m_ref, v_ref, g_ref, d_ref, m2_ref, v2_ref):
        for l, ref in enumerate((s0_ref, s1_ref)):
            @pl.when(pl.program_id(0) == l)
            def _(ref=ref):
                acc = ref[0].astype(F32)
                for i in range(1, N_DEV):
                    acc = acc + ref[i].astype(F32)
                g_ref[...] = acc

        gg = g_ref[...]
        m2 = ADAM_B1 * m_ref[...] + (1.0 - ADAM_B1) * gg
        v2 = ADAM_B2 * v_ref[...] + (1.0 - ADAM_B2) * jnp.square(gg)
        m_hat = m2 / (1.0 - ADAM_B1 ** ADAM_STEP)
        v_hat = v2 / (1.0 - ADAM_B2 ** ADAM_STEP)
        d_ref[...] = -ADAM_LR * (m_hat / (jnp.sqrt(v_hat) + ADAM_EPS) + ADAM_WD * w_ref[...])
        m2_ref[...] = m2
        v2_ref[...] = v2

    slot_spec = lambda layer: pl.BlockSpec((N_DEV, tr, cols), lambda l, i: (0, jnp.where(l == layer, i, (nb - 1) * (1 - layer)), 0))
    spec = pl.BlockSpec((None, tr, cols), lambda l, i: (l, i, 0))
    return pl.pallas_call(
        body, name=name, grid=(DEPTH, nb),
        in_specs=[slot_spec(0), slot_spec(1), spec, spec, spec], out_specs=[spec] * 4,
        out_shape=[jax.ShapeDtypeStruct(w.shape, F32)] * 4,
        compiler_params=_params(),
    )(slots[0], slots[1], w, m, v)


_MESH = pl.DeviceIdType.MESH
_ANY = pl.BlockSpec(memory_space=pl.ANY)


def _my_place():
    return lax.axis_index("x"), lax.axis_index("y"), lax.axis_index("c")


def _flip(place, k):
    x, y, c = place
    return (1 - x if k & 4 else x, 1 - y if k & 2 else y, 1 - c if k & 1 else c)


def _block_id(place):
    return 4 * place[0] + 2 * place[1] + place[2]


def _peer_copies(kind, in_refs, out_refs, send_sems, recv_sems, local_sems):
    me = _my_place()
    my = _block_id(me)
    remote, local = [], []
    for a, (x_ref, out_ref) in enumerate(zip(in_refs, out_refs)):
        src_of = (lambda place, r=x_ref: r) if kind == "gather" else (lambda place, r=x_ref: r.at[_block_id(place)])
        local.append(pltpu.make_async_copy(src_of(me), out_ref.at[my], local_sems.at[a]))
        for k in range(1, N_DEV):
            peer = _flip(me, k)
            remote.append(pltpu.make_async_remote_copy(
                src_ref=src_of(peer), dst_ref=out_ref.at[my], send_sem=send_sems.at[a * 7 + k - 1],
                recv_sem=recv_sems.at[a * 7 + k - 1], device_id=peer, device_id_type=_MESH))
    return remote, local


def _comm_out_shapes(kind, arrays):
    return [jax.ShapeDtypeStruct((N_DEV, *a.shape) if kind == "gather" else a.shape, a.dtype) for a in arrays]


def _comm_scratch(n):
    return [pltpu.SemaphoreType.DMA((7 * n,)), pltpu.SemaphoreType.DMA((7 * n,)), pltpu.SemaphoreType.DMA((n,))]


def _hosted_comm(kind, in_refs, out_refs, sems, first, last):
    if not in_refs:
        return

    @pl.when(first)
    def _():
        remote, local = _peer_copies(kind, in_refs, out_refs, *sems)
        for cp in local + remote:
            cp.start()

    @pl.when(last)
    def _():
        remote, local = _peer_copies(kind, in_refs, out_refs, *sems)
        for cp in remote:
            cp.wait()
        for cp in local:
            cp.wait()


def _two_level_gather_steps(in_refs, out_refs, send_sems, recv_sems, local_sems):
    n = len(in_refs)
    me = _my_place()
    x, y, c = me
    sibling = (x, y, 1 - c)
    chips = [(1 - x, y), (x, 1 - y), (1 - x, 1 - y)]

    def copy(a, k, place, to, src=None):
        block = out_refs[a].at[_block_id(place)]
        return pltpu.make_async_remote_copy(
            src_ref=block if src is None else src, dst_ref=block, send_sem=send_sems.at[7 * a + k],
            recv_sem=recv_sems.at[7 * a + k], device_id=to, device_id_type=_MESH)

    mine = [pltpu.make_async_copy(in_refs[a], out_refs[a].at[_block_id(me)], local_sems.at[a]) for a in range(n)]
    first = [copy(a, 0, me, sibling, src=in_refs[a]) for a in range(n)]
    first += [copy(a, 1 + j, me, (*chip, c), src=in_refs[a]) for a in range(n) for j, chip in enumerate(chips)]
    passed = [copy(a, 4 + j, (*chip, c), sibling) for a in range(n) for j, chip in enumerate(chips)]

    def send():
        for cp in mine + first:
            cp.start()

    def forward():
        for a in range(n):
            for j, chip in enumerate(chips):
                copy(a, 1 + j, (*chip, c), me).wait_recv()
                passed[3 * a + j].start()

    def finish():
        for a in range(n):
            copy(a, 0, sibling, me).wait_recv()
            for j, chip in enumerate(chips):
                copy(a, 4 + j, (*chip, 1 - c), me).wait_recv()
        for cp in first + passed:
            cp.wait_send()
        for cp in mine:
            cp.wait()

    return send, forward, finish


def _gather_two_level(arrays, name):
    n = len(arrays)

    def body(*refs):
        for step in _two_level_gather_steps(refs[:n], refs[n:2 * n], *refs[2 * n:]):
            step()

    return pl.pallas_call(
        body, name=name, out_shape=_comm_out_shapes("gather", arrays),
        in_specs=[_ANY] * n, out_specs=[_ANY] * n, scratch_shapes=_comm_scratch(n),
    )(*arrays)


def _hosted_gather(in_refs, out_refs, sems, first, middle, last):
    if not in_refs:
        return
    for when, index in ((first, 0), (middle, 1), (last, 2)):
        @pl.when(when)
        def _(index=index):
            _two_level_gather_steps(in_refs, out_refs, *sems)[index]()


def _comm(kind, arrays, name):
    n = len(arrays)

    def body(*refs):
        remote, local = _peer_copies(kind, refs[:n], refs[n:2 * n], *refs[2 * n:])
        for cp in local + remote:
            cp.start()
        for cp in remote:
            cp.wait()
        for cp in local:
            cp.wait()

    return pl.pallas_call(
        body, name=name, out_shape=_comm_out_shapes(kind, arrays),
        in_specs=[_ANY] * n, out_specs=[_ANY] * n, scratch_shapes=_comm_scratch(n),
    )(*arrays)


def _all_reduce_small(part):
    rows, lanes = part.shape
    vmem = pl.BlockSpec(memory_space=pltpu.VMEM)

    def body(x_ref, gath_ref, sum_ref, send_sems, recv_sems):
        me = _my_place()
        my = _block_id(me)
        gath_ref[my] = x_ref[...]
        copies = []
        for k in range(1, N_DEV):
            cp = pltpu.make_async_remote_copy(
                src_ref=x_ref, dst_ref=gath_ref.at[my], send_sem=send_sems.at[k - 1], recv_sem=recv_sems.at[k - 1],
                device_id=_flip(me, k), device_id_type=_MESH)
            cp.start()
            copies.append(cp)
        for cp in copies:
            cp.wait()
        acc = gath_ref[0]
        for i in range(1, N_DEV):
            acc = acc + gath_ref[i]
        sum_ref[...] = acc

    return pl.pallas_call(
        body, name="small_grad_all_reduce",
        out_shape=[jax.ShapeDtypeStruct((N_DEV, rows, lanes), F32), jax.ShapeDtypeStruct((rows, lanes), F32)],
        in_specs=[vmem], out_specs=[vmem, vmem],
        scratch_shapes=[pltpu.SemaphoreType.DMA((7,)), pltpu.SemaphoreType.DMA((7,))],
    )(part)[1]


_SHARDED = (("w_in", (D_MODEL, IN_PROJ // N_DEV)), ("w_uq", (Q_RANK // N_DEV, Q_RANK)), ("w_ukv", (KV_RANK, HEAD_PAD)),
            ("conv_w", (CONV_W, CONV_DIM // N_DEV)), ("w_out", (D_MODEL // N_DEV, D_MODEL)),
            ("w_up", (D_MODEL, D_FF // N_DEV)), ("w_down", (D_FF // N_DEV, D_MODEL)))
_SMALL = (("pre_mix_norm", D_MODEL), ("q_norm", Q_RANK), ("kv_norm", KV_RANK), ("conv_b", CONV_DIM), ("dt_bias", SSD_HEADS),
          ("a_log", SSD_HEADS), ("d_skip", SSD_HEADS), ("ssd_norm", SSD_INNER), ("post_mix_norm", D_MODEL),
          ("pre_mlp_norm", D_MODEL), ("post_mlp_norm", D_MODEL))
_WEIGHT_ORDER = ("pre_mix_norm", "w_in", "q_norm", "w_uq", "kv_norm", "w_ukv", "conv_w", "conv_b", "dt_bias", "a_log", "d_skip",
                 "ssd_norm", "w_out", "post_mix_norm", "pre_mlp_norm", "w_up", "w_down", "post_mlp_norm")
_EARLY = ("w_in", "w_uq", "w_ukv", "conv_w")
_LATE = ("w_out", "w_up", "w_down")


def _wire_shard(name, a):
    return lax.bitcast_convert_type(a, BF16).reshape(CONV_W, -1) if name == "conv_w" else a.astype(BF16)


def _from_wire(name, g):
    return lax.bitcast_convert_type(g.reshape(N_DEV, CONV_W, -1, 2), F32) if name == "conv_w" else g


def _cols(stacked):
    return jnp.transpose(stacked, (1, 0, 2)).reshape(stacked.shape[1], -1)


def _early_weights(sh):
    w_in = _cols(sh["w_in"])
    zeros = lambda n: jnp.zeros((D_MODEL, n), BF16)
    s1, s2, s3, s4, s5 = 768, 1024, 1056, 1568, 2592
    win = jnp.concatenate([w_in[:, :s2], zeros(MISC_ROPE), w_in[:, s2:s3], w_in[:, s5:], zeros(HEAD_PAD - MISC_DT - SSD_HEADS),
                           w_in[:, s3:s5]], axis=1)
    w_uq = sh["w_uq"].reshape(Q_RANK, MLA_HEADS, QK_NOPE + QK_ROPE)
    wuq = jnp.pad(w_uq, ((0, 0), (0, 0), (0, HEAD_PAD - QK_NOPE - QK_ROPE))).reshape(Q_RANK, -1)
    w_ukv = _cols(sh["w_ukv"]).reshape(KV_RANK, MLA_HEADS, QK_NOPE + V_DIM)
    wkn = jnp.pad(w_ukv[..., :QK_NOPE], ((0, 0), (0, 0), (0, HEAD_PAD - QK_NOPE))).reshape(KV_RANK, -1)
    wv = w_ukv[..., QK_NOPE:].reshape(KV_RANK, 4, 2, 1, V_DIM) * jnp.eye(2, dtype=BF16).reshape(1, 1, 2, 2, 1)
    wkv = jnp.concatenate([wkn, wv.reshape(KV_RANK, -1)], axis=1)
    return dict(win=win, wuq=wuq, wkv=wkv, conv_w=_cols(sh["conv_w"]))


def _late_weights(sh):
    w_out = sh["w_out"].reshape(D_MODEL, D_MODEL)
    watt = w_out[:SSD_INNER].reshape(4, 2, 1, V_DIM, D_MODEL) * jnp.eye(2, dtype=BF16).reshape(1, 2, 2, 1, 1)
    wout = jnp.concatenate([watt.reshape(MLA_HEADS * HEAD_PAD, D_MODEL), w_out[SSD_INNER:]], axis=0)
    return dict(wout=wout, wup=sh["w_up"], wdown=sh["w_down"])


def _shard_grads(g):
    out = {}
    if "wup" in g:
        out["w_up"], out["w_down"] = g["wup"], g["wdown"]
        ae = g["wout_att"].reshape(4, 2, 2, V_DIM, D_MODEL)
        att = jnp.stack([ae[:, 0, 0], ae[:, 1, 1]], axis=1).reshape(SSD_INNER, D_MODEL)
        out["w_out"] = jnp.concatenate([att, g["wout_ssd"]], axis=0).astype(BF16).reshape(N_DEV, D_MODEL // N_DEV, D_MODEL)
    if "win" not in g:
        return out
    dwin = g["win"]
    s2 = Q_RANK + KV_RANK
    m0 = s2
    w_in = jnp.concatenate([dwin[:, :s2], dwin[:, m0 + MISC_ROPE:m0 + MISC_ROPE + QK_ROPE], dwin[:, 1152:2688],
                            dwin[:, m0 + MISC_DT:m0 + MISC_DT + SSD_HEADS]], axis=1)
    out["w_in"] = jnp.transpose(w_in.astype(BF16).reshape(D_MODEL, N_DEV, -1), (1, 0, 2))
    w_uq = g["wuq"].astype(BF16).reshape(Q_RANK, MLA_HEADS, HEAD_PAD)[..., :QK_NOPE + QK_ROPE].reshape(Q_RANK, Q_RANK)
    out["w_uq"] = w_uq.reshape(N_DEV, Q_RANK // N_DEV, Q_RANK)
    wide = MLA_HEADS * HEAD_PAD
    wkv = g["wkv"].astype(BF16)
    kn = wkv[:, :wide].reshape(KV_RANK, MLA_HEADS, HEAD_PAD)[..., :QK_NOPE]
    ve = wkv[:, wide:].reshape(KV_RANK, 4, 2, 2, V_DIM)
    vv = jnp.stack([ve[:, :, 0, 0], ve[:, :, 1, 1]], axis=2).reshape(KV_RANK, MLA_HEADS, V_DIM)
    out["w_ukv"] = jnp.transpose(jnp.concatenate([kn, vv], axis=-1), (1, 0, 2))
    out["conv_w"] = jnp.transpose(g["conv_w"].astype(BF16).reshape(CONV_W, N_DEV, -1), (1, 0, 2))
    return out


def _small_rows(n):
    return -(-n // 1024) * 8


def _pack_small(vals):
    rows = []
    for l in range(DEPTH):
        for name, n in _SMALL:
            r = _small_rows(n)
            rows.append(jnp.pad(vals[name][l].reshape(-1), (0, r * 128 - n)).reshape(r, 128))
    return jnp.concatenate(rows, axis=0)


def _unpack_small(packed):
    out, off = {name: [] for name, _ in _SMALL}, 0
    for l in range(DEPTH):
        for name, n in _SMALL:
            r = _small_rows(n)
            out[name].append(packed[off:off + r].reshape(-1)[:n])
            off += r
    return {name: jnp.stack(v) for name, v in out.items()}


def _lane_rows(vec8):
    return jnp.repeat(vec8, SSD_P).reshape(1, SSD_INNER)


def _layer_fwd(h, kw, sm, l, cosf, sinf, consts, gather=(), after_gather=None, target=None):
    row = lambda name: sm[name][l].reshape(1, -1)
    t = {}
    t["h0"] = h
    t["ub"], t["cq"], t["ckv"], t["misc"], t["z"], t["xraw"] = _inproj_fwd(h, row("pre_mix_norm"), kw["win"])
    t["cqn"], t["ckvn"], t["q"], t["k"], t["v"] = _qkv_fwd(t["cq"], t["ckv"], t["misc"], row("q_norm"), row("kv_norm"),
                                                         kw["wuq"], kw["wkv"], cosf, sinf)
    t["oe"], t["lse"], gathered = _attn_fwd(t["q"], t["k"], t["v"], gather)
    if after_gather is not None:
        after_gather(gathered)
    t["dtb"] = _lane_rows(sm["dt_bias"][l])
    t["a_exp"] = _lane_rows(-jnp.exp(sm["a_log"][l]))
    t["d_exp"] = _lane_rows(sm["d_skip"][l])
    t["c"], t["prev"], t["ypre"], t["yssd"] = _ssd_fwd(t["xraw"], t["misc"], t["z"], kw["conv_w"], row("conv_b"), t["dtb"],
                                                     t["a_exp"], t["d_exp"], row("ssd_norm"), consts)
    t["mixed"], t["h1"] = _outproj_fwd(t["oe"], t["yssd"], kw["wout"], h, row("post_mix_norm"))
    t["mb"], t["ab"], t["d"], *out = _mlp_fwd(t["h1"], row("pre_mlp_norm"), kw["wup"], kw["wdown"], row("post_mlp_norm"), target)
    return out, t


def _layer_bwd(dh2, t, kw, sm, l, cosf, sinf, consts, exchange_of=None):
    row = lambda name: sm[name][l].reshape(1, -1)
    g, gs = {}, {}
    dh1, dab, ddb, gs["post_mlp_norm"], gs["pre_mlp_norm"] = _mlp_bwd(
        dh2, t["d"], t["h1"], t["ab"], row("pre_mlp_norm"), kw["wup"], kw["wdown"], row("post_mlp_norm"))
    g["wup"] = _matmul_tn_stacked(t["mb"], dab, f"dw_up_{l}", a_stacked=False)
    g["wdown"] = _matmul_tn_stacked(t["ab"], ddb, f"dw_down_{l}", a_stacked=True, square_a=True)
    dmixb, doe, dyssd, gs["post_mix_norm"], delta = _outproj_bwd(dh1, t["mixed"], row("post_mix_norm"), kw["wout"], t["oe"])
    g["wout_att"] = _matmul_tn(t["oe"], dmixb, f"dw_out_att_{l}")
    g["wout_ssd"] = _matmul_tn(t["yssd"], dmixb, f"dw_out_ssd_{l}")
    dz, dxraw, dmisc_dt, gs["ssd_norm"], gd, galog, gdtb, g["conv_w"], gs["conv_b"] = _ssd_bwd(
        dyssd, t["ypre"], t["z"], t["c"], t["xraw"], t["misc"], t["prev"], kw["conv_w"], t["dtb"], t["a_exp"], t["d_exp"],
        row("ssd_norm"), consts)
    gs["d_skip"] = jnp.sum(gd.reshape(SSD_HEADS, SSD_P), axis=1)
    gs["a_log"] = galog[0, MISC_DT:MISC_DT + SSD_HEADS]
    gs["dt_bias"] = gdtb[0, MISC_DT:MISC_DT + SSD_HEADS]
    dq, dk, dv, exchanged = _attn_bwd(t["q"], t["k"], t["v"], doe, t["lse"], delta,
                                      exchange_of(g) if exchange_of is not None else ())
    dqb, dkvb, dcq, dckv, dmisc_rope, gs["q_norm"], gs["kv_norm"] = _qkv_bwd(
        dq, dk, dv, t["cq"], t["ckv"], row("q_norm"), row("kv_norm"), kw["wuq"], kw["wkv"], cosf, sinf)
    g["wuq"] = _matmul_tn(t["cqn"], dqb, f"dw_uq_{l}")
    g["wkv"] = _matmul_tn(t["ckvn"], dkvb, f"dw_kv_{l}")
    dprojb, dh0, gs["pre_mix_norm"] = _inproj_bwd(dcq, dckv, dmisc_rope, dmisc_dt, dz, dxraw, t["h0"], dh1,
                                                  row("pre_mix_norm"), kw["win"])
    g["win"] = _matmul_tn(t["ub"], dprojb, f"dw_in_{l}")
    return dh0, g, {k: v.reshape(-1) for k, v in gs.items()}, exchanged


def _local_step(x, positions, kws, sm, target, gather=(), after_gather=None, exchange_of=None):
    inv_freq = ROPE_THETA ** (-jnp.arange(0, QK_ROPE, 2, dtype=F32) / QK_ROPE)
    invf = jnp.zeros((HEAD_PAD,), F32).at[MISC_ROPE:MISC_ROPE + QK_ROPE].set(jnp.concatenate([inv_freq, inv_freq]))
    cosf, sinf = _rope_tables(positions.reshape(-1, 1), invf.reshape(1, HEAD_PAD))
    consts = _ssd_consts()
    (h,), t0 = _layer_fwd(x, kws[0], sm, 0, cosf, sinf, consts, gather, after_gather)
    (dh, loss), t1 = _layer_fwd(h, kws[1], sm, 1, cosf, sinf, consts, target=target)
    saved = [t0, t1]
    grads, small, exchanged = [None] * DEPTH, [None] * DEPTH, []
    for l in reversed(range(DEPTH)):
        hook = (lambda g0: exchange_of(g0, grads[1])) if (l == 0 and exchange_of is not None) else None
        dh, grads[l], small[l], got = _layer_bwd(dh, saved[l], kws[l], sm, l, cosf, sinf, consts, hook)
        exchanged = got or exchanged
    return loss[0, 0], dh, grads, small, exchanged


def kernel(x, positions, pre_mix_norm, w_in, q_norm, w_uq, kv_norm, w_ukv, conv_w, conv_b, dt_bias, a_log, d_skip, ssd_norm, w_out, post_mix_norm, pre_mlp_norm, w_up, w_down, post_mlp_norm, loss_target, m_pre_mix_norm, m_w_in, m_q_norm, m_w_uq, m_kv_norm, m_w_ukv, m_conv_w, m_conv_b, m_dt_bias, m_a_log, m_d_skip, m_ssd_norm, m_w_out, m_post_mix_norm, m_pre_mlp_norm, m_w_up, m_w_down, m_post_mlp_norm, v_pre_mix_norm, v_w_in, v_q_norm, v_w_uq, v_kv_norm, v_w_ukv, v_conv_w, v_conv_b, v_dt_bias, v_a_log, v_d_skip, v_ssd_norm, v_w_out, v_post_mix_norm, v_pre_mlp_norm, v_w_up, v_w_down, v_post_mlp_norm):
    w = dict(pre_mix_norm=pre_mix_norm, w_in=w_in, q_norm=q_norm, w_uq=w_uq, kv_norm=kv_norm, w_ukv=w_ukv, conv_w=conv_w,
             conv_b=conv_b, dt_bias=dt_bias, a_log=a_log, d_skip=d_skip, ssd_norm=ssd_norm, w_out=w_out,
             post_mix_norm=post_mix_norm, pre_mlp_norm=pre_mlp_norm, w_up=w_up, w_down=w_down, post_mlp_norm=post_mlp_norm)
    m = dict(pre_mix_norm=m_pre_mix_norm, w_in=m_w_in, q_norm=m_q_norm, w_uq=m_w_uq, kv_norm=m_kv_norm, w_ukv=m_w_ukv,
             conv_w=m_conv_w, conv_b=m_conv_b, dt_bias=m_dt_bias, a_log=m_a_log, d_skip=m_d_skip, ssd_norm=m_ssd_norm,
             w_out=m_w_out, post_mix_norm=m_post_mix_norm, pre_mlp_norm=m_pre_mlp_norm, w_up=m_w_up, w_down=m_w_down,
             post_mlp_norm=m_post_mlp_norm)
    v = dict(pre_mix_norm=v_pre_mix_norm, w_in=v_w_in, q_norm=v_q_norm, w_uq=v_w_uq, kv_norm=v_kv_norm, w_ukv=v_w_ukv,
             conv_w=v_conv_w, conv_b=v_conv_b, dt_bias=v_dt_bias, a_log=v_a_log, d_skip=v_d_skip, ssd_norm=v_ssd_norm,
             w_out=v_w_out, post_mix_norm=v_post_mix_norm, pre_mlp_norm=v_pre_mlp_norm, w_up=v_w_up, w_down=v_w_down,
             post_mlp_norm=v_post_mlp_norm)
    sm = {name: w[name] for name, _ in _SMALL}

    wire = lambda name, l: _wire_shard(name, w[name][l])
    first = _gather_two_level([wire(name, 0) for name in _EARLY], "weight_gather_first")
    kws = [_early_weights({name: _from_wire(name, a) for name, a in zip(_EARLY, first)}), None]
    behind = [(name, 0) for name in _LATE] + [(name, 1) for name, _ in _SHARDED]

    def after_gather(gathered):
        got = {key: _from_wire(key[0], a) for key, a in zip(behind, gathered)}
        kws[0].update(_late_weights({name: got[name, 0] for name in _LATE}))
        kws[1] = {**_early_weights({name: got[name, 1] for name in _EARLY}),
                  **_late_weights({name: got[name, 1] for name in _LATE})}

    sent_behind = [(name, 1) for name, _ in _SHARDED] + [(name, 0) for name in _LATE]

    def exchange_of(g0, g1):
        blocks = {**{(name, 1): a for name, a in _shard_grads(g1).items()},
                  **{(name, 0): a for name, a in _shard_grads(g0).items()}}
        return [blocks[key] for key in sent_behind]

    loss_part, dx, grads, small, exchanged = _local_step(
        x[0], positions[0], kws, sm, loss_target[0], [wire(*key) for key in behind], after_gather, exchange_of)
    slots = dict(zip(sent_behind, exchanged))
    last = _shard_grads({k: grads[0][k] for k in ("win", "wuq", "wkv", "conv_w")})
    slots.update({(name, 0): a for name, a in zip(_EARLY, _comm("exchange", [last[name] for name in _EARLY], "grad_exchange_last"))})
    g_small = _unpack_small(_all_reduce_small(_pack_small({name: jnp.stack([small[l][name] for l in range(DEPTH)])
                                                           for name, _ in _SMALL})))
    loss = lax.psum(loss_part, ("x", "y", "c"))

    grad, delta, new_m, new_v = {}, {}, {}, {}
    for name, _ in _SHARDED:
        grad[name], delta[name], new_m[name], new_v[name] = _sum_adamw(
            [slots[name, 0], slots[name, 1]], w[name], m[name], v[name], f"sum_adamw_{name}")
    pk = lambda d: _pack_small({name: d[name] for name, _ in _SMALL})
    d_, m_, v_ = _adamw(pk(w), pk(g_small), pk(m), pk(v), "adamw_small")
    for dst, packed in ((delta, d_), (new_m, m_), (new_v, v_)):
        dst.update(_unpack_small(packed))
    grad.update(g_small)

    outs = [loss, dx[None]]
    for d in (grad, delta, new_m, new_v):
        outs += [d[name] for name in _WEIGHT_ORDER]
    return tuple(outs)
```

```python
import jax
import jax.numpy as jnp
import numpy as np
from jax import lax
from jax.experimental import pallas as pl
from jax.experimental.pallas import tpu as pltpu

F32 = jnp.float32
BF16 = jnp.bfloat16
HI = lax.Precision.HIGHEST

D_MODEL = 1024
DEPTH = 2
N_DEV = 8
CHUNK = 64
EPS = 1e-6
MLA_HEADS = 8
QK_NOPE = 64
QK_ROPE = 32
V_DIM = 64
Q_RANK = 768
KV_RANK = 256
ROPE_THETA = 10000.0
SSD_HEADS = 8
SSD_P = 64
SSD_INNER = 512
SSD_GROUPS = 2
SSD_N = 128
CONV_W = 4
CONV_DIM = 1024
D_FF = 4096
IN_PROJ = 2600
HEAD_PAD = 128
IN_PAD = 2688
MISC_ROPE = 64
MISC_DT = 96
ATT_SCALE = (QK_NOPE + QK_ROPE) ** -0.5
LOG2E = 1.4426950408889634
ATT_SCALE_LOG2 = ATT_SCALE * LOG2E

ADAM_LR = 0.001
ADAM_B1 = 0.9
ADAM_B2 = 0.999
ADAM_EPS = 1e-08
ADAM_WD = 0.01
ADAM_STEP = 10

TM = 512
ATT_T = 512
ATT_G = 8
ATT_UNROLL = 4
SSD_ROWS = 512
TK_DW = 4096
VMEM_LIMIT = 56 * 1024 * 1024

_NT = (((1,), (1,)), ((), ()))
_TN = (((0,), (0,)), ((), ()))


def _params(**kw):
    return pltpu.CompilerParams(vmem_limit_bytes=VMEM_LIMIT, **kw)


def _dot(a, b, precision=None):
    return jnp.dot(a, b, preferred_element_type=F32, precision=precision)


def _dot_nt(a, b, precision=None):
    return lax.dot_general(a, b, _NT, preferred_element_type=F32, precision=precision)


def _dot_tn(a, b, precision=None):
    return lax.dot_general(a, b, _TN, preferred_element_type=F32, precision=precision)


def _split3(x):
    hi = x.astype(BF16)
    r = x - hi.astype(F32)
    mid = r.astype(BF16)
    return hi, mid, (r - mid.astype(F32)).astype(BF16)


def _dot01(x, m01, dot=_dot, left=False):
    parts = [dot(m01, p) if left else dot(p, m01) for p in _split3(x)]
    return parts[0] + parts[1] + parts[2]


def _full(shape):
    n = len(shape)
    return pl.BlockSpec(shape, lambda *_: (0,) * n)


def _resident(shape):
    n = len(shape)
    return pl.BlockSpec(shape, lambda *_: (0,) * n, pipeline_mode=pl.Buffered(1))


def _rows(tm, width):
    return pl.BlockSpec((tm, width), lambda i: (i, 0))


def _rms_fwd(x, w):
    r = lax.rsqrt(jnp.mean(x * x, axis=-1, keepdims=True) + EPS)
    return (x * r) * w


def _rms_bwd(x, w, dy):
    r = lax.rsqrt(jnp.mean(x * x, axis=-1, keepdims=True) + EPS)
    xh = x * r
    dxn = dy * w
    dx = r * (dxn - xh * jnp.mean(dxn * xh, axis=-1, keepdims=True))
    return dx, dy * xh


def _acc_rows(ref, val, first):
    s = jnp.sum(val, axis=0, keepdims=True)

    @pl.when(first)
    def _():
        ref[...] = s

    @pl.when(jnp.logical_not(first))
    def _():
        ref[...] += s


def _rope(t, cosf, sinf, sign):
    lane = lax.broadcasted_iota(jnp.int32, t.shape, 1)
    rot = jnp.where(lane < MISC_ROPE + QK_ROPE // 2, -pltpu.roll(t, HEAD_PAD - QK_ROPE // 2, 1), pltpu.roll(t, QK_ROPE // 2, 1))
    return t * cosf + sign * (rot * sinf)


def _rope_tables(pos, invf):
    s = pos.shape[0]

    def body(pos_ref, invf_ref, cos_ref, sin_ref):
        ang = pos_ref[...].astype(F32) * invf_ref[...]
        cos_ref[...] = jnp.cos(ang)
        sin_ref[...] = jnp.sin(ang)

    return pl.pallas_call(
        body, name="rope_tables", grid=(s // TM,),
        in_specs=[_rows(TM, 1), _full((1, HEAD_PAD))],
        out_specs=[_rows(TM, HEAD_PAD), _rows(TM, HEAD_PAD)],
        out_shape=[jax.ShapeDtypeStruct((s, HEAD_PAD), F32)] * 2,
    )(pos, invf)


def _inproj_fwd(h, nw, win):
    s = h.shape[0]

    def body(h_ref, nw_ref, w_ref, ub_ref, cq_ref, ckv_ref, misc_ref, z_ref, xbc_ref):
        ub = _rms_fwd(h_ref[...], nw_ref[...]).astype(BF16)
        ub_ref[...] = ub
        proj = _dot(ub, w_ref[...])
        cq_ref[...] = proj[:, 0:768]
        ckv_ref[...] = proj[:, 768:1024]
        misc_ref[...] = proj[:, 1024:1152]
        z_ref[...] = proj[:, 1152:1664]
        xbc_ref[...] = proj[:, 1664:2688]

    widths = (768, 256, 128, 512, 1024)
    return pl.pallas_call(
        body, name="inproj_fwd", grid=(s // TM,),
        in_specs=[_rows(TM, D_MODEL), _full((1, D_MODEL)), _resident((D_MODEL, IN_PAD))],
        out_specs=[_rows(TM, D_MODEL)] + [_rows(TM, w) for w in widths],
        out_shape=[jax.ShapeDtypeStruct((s, D_MODEL), BF16)] + [jax.ShapeDtypeStruct((s, w), F32) for w in widths],
        compiler_params=_params(),
    )(h, nw, win)


def _qkv_fwd(cq, ckv, misc, qnw, kvnw, wuq, wkv, cosf, sinf):
    s = cq.shape[0]

    def body(cq_ref, ckv_ref, misc_ref, qnw_ref, kvnw_ref, wuq_ref, wkv_ref, cos_ref, sin_ref,
             cqn_ref, ckvn_ref, q_ref, k_ref, v_ref):
        cosf, sinf = cos_ref[...], sin_ref[...]
        cqn = _rms_fwd(cq_ref[...], qnw_ref[...]).astype(BF16)
        cqn_ref[...] = cqn
        q = _dot(cqn, wuq_ref[...])
        ckvn = _rms_fwd(ckv_ref[...], kvnw_ref[...]).astype(BF16)
        ckvn_ref[...] = ckvn
        kv = _dot(ckvn, wkv_ref[...])
        m = misc_ref[...]
        lane = lax.broadcasted_iota(jnp.int32, m.shape, 1)
        in_rope = jnp.logical_and(lane >= MISC_ROPE, lane < MISC_ROPE + QK_ROPE)
        kr = jnp.where(in_rope, _rope(m, cosf, sinf, 1.0), 0.0)
        for hd in range(MLA_HEADS):
            cols = slice(hd * HEAD_PAD, (hd + 1) * HEAD_PAD)
            q_ref[:, cols] = _rope(q[:, cols], cosf, sinf, 1.0).astype(BF16)
            k_ref[:, cols] = (kv[:, cols] + kr).astype(BF16)
        vv = kv[:, MLA_HEADS * HEAD_PAD:]
        vlane = lax.broadcasted_iota(jnp.int32, vv.shape, 1)
        ones_at = jnp.where((vlane // HEAD_PAD) % 2 == 0, V_DIM, 0)
        v_ref[...] = jnp.where(vlane % HEAD_PAD == ones_at, 1.0, vv).astype(BF16)

    wide = MLA_HEADS * HEAD_PAD
    return pl.pallas_call(
        body, name="qkv_fwd", grid=(s // TM,),
        in_specs=[_rows(TM, Q_RANK), _rows(TM, KV_RANK), _rows(TM, HEAD_PAD), _full((1, Q_RANK)), _full((1, KV_RANK)),
                  _resident((Q_RANK, wide)), _resident((KV_RANK, 2 * wide)), _rows(TM, HEAD_PAD), _rows(TM, HEAD_PAD)],
        out_specs=[_rows(TM, Q_RANK), _rows(TM, KV_RANK), _rows(TM, wide), _rows(TM, wide), _rows(TM, wide)],
        out_shape=[jax.ShapeDtypeStruct((s, Q_RANK), BF16), jax.ShapeDtypeStruct((s, KV_RANK), BF16)]
        + [jax.ShapeDtypeStruct((s, wide), BF16)] * 3,
        compiler_params=_params(),
    )(cq, ckv, misc, qnw, kvnw, wuq, wkv, cosf, sinf)


def _chunk_bias(t, keys_on_rows=False):
    row = lax.broadcasted_iota(jnp.int32, (t, 1), 0) // CHUNK
    col = lax.broadcasted_iota(jnp.int32, (1, t), 1) // CHUNK
    return jnp.where((row <= col) if keys_on_rows else (col <= row), 0.0, -jnp.inf).astype(F32)


def _attn_fwd(q, k, v, gather=()):
    s = q.shape[0]
    t = ATT_T
    nq = s // t
    pair = ATT_G * HEAD_PAD
    ng = len(gather)

    def body(q_ref, k_ref, v_ref, *rest):
        g_in, (o_ref, lse_ref), g_out = rest[:ng], rest[ng:ng + 2], rest[ng + 2:2 * ng + 2]
        m_s, acc_s, bias_s = rest[2 * ng + 2:2 * ng + 5]
        qi = pl.program_id(1)
        group, groups = pl.program_id(0), MLA_HEADS // ATT_G

        @pl.when(jnp.logical_and(group == 0, qi == 0))
        def _():
            bias_s[...] = _chunk_bias(t)

        _hosted_gather(g_in, g_out, rest[2 * ng + 5:],
                       jnp.logical_and(group == 0, qi == 0),
                       jnp.logical_and(group == groups - 1, qi == min(3 * nq // 4 + 1, nq - 1)),
                       jnp.logical_and(group == groups - 1, qi == nq - 1))
        m_s[...] = jnp.full(m_s.shape, -jnp.inf, F32)
        acc_s[...] = jnp.zeros(acc_s.shape, F32)

        def step(kb, masked):
            r0 = pl.multiple_of(kb * t, t)

            def scores(hh):
                cols = slice(hh * HEAD_PAD, (hh + 1) * HEAD_PAD)
                return _dot_nt(q_ref[:, cols], k_ref[pl.ds(r0, t), cols])

            def soft(hh, raw):
                sc = raw * ATT_SCALE_LOG2
                if masked:
                    sc = sc + bias_s[...]
                m_old = m_s[hh]
                m_new = jnp.maximum(m_old, jnp.max(sc, axis=-1, keepdims=True))
                alpha = jnp.exp2(m_old - m_new)
                p = jnp.exp2(sc - jnp.tile(m_new, (1, t // HEAD_PAD)))
                m_s[hh] = m_new
                return alpha, p.astype(BF16)

            def update(hh, alpha, p):
                cols = slice(hh * HEAD_PAD, (hh + 1) * HEAD_PAD)
                acc_s[hh] = alpha * acc_s[hh] + _dot(p, v_ref[pl.ds(r0, t), cols])

            raw, ap = [None] * ATT_G, [None] * ATT_G
            raw[0] = scores(0)
            for hh in range(ATT_G):
                if hh + 1 < ATT_G:
                    raw[hh + 1] = scores(hh + 1)
                ap[hh] = soft(hh, raw[hh])
                if hh >= 1:
                    update(hh - 1, *ap[hh - 1])
            update(ATT_G - 1, *ap[ATT_G - 1])

        def loop(i, c):
            step(2 * i, False)
            step(2 * i + 1, False)
            return c

        lax.fori_loop(0, qi // 2, loop, 0)

        @pl.when(qi % 2 == 1)
        def _():
            step(qi - 1, False)

        step(qi, True)
        for hh in range(ATT_G):
            cols = slice(hh * HEAD_PAD, (hh + 1) * HEAD_PAD)
            acc = acc_s[hh]
            ones_at = V_DIM * (1 - hh % 2)
            l = jnp.broadcast_to(acc[:, ones_at:ones_at + 1], acc.shape)
            o_ref[:, cols] = (acc / l).astype(BF16)
            lse_ref[hh] = (m_s[hh] + jnp.log(l) * LOG2E).T[0:8, :]

    outs = pl.pallas_call(
        body, name="attn_fwd_gather" if ng else "attn_fwd", grid=(MLA_HEADS // ATT_G, nq),
        in_specs=[pl.BlockSpec((t, pair), lambda h, i: (i, h)),
                  pl.BlockSpec((s, pair), lambda h, i: (0, h), pipeline_mode=pl.Buffered(1)),
                  pl.BlockSpec((s, pair), lambda h, i: (0, h), pipeline_mode=pl.Buffered(1))] + [_ANY] * ng,
        out_specs=[pl.BlockSpec((t, pair), lambda h, i: (i, h)),
                   pl.BlockSpec((ATT_G, 8, t), lambda h, i: (h, 0, i))] + [_ANY] * ng,
        out_shape=[jax.ShapeDtypeStruct((s, MLA_HEADS * HEAD_PAD), BF16), jax.ShapeDtypeStruct((MLA_HEADS, 8, s), F32)]
        + _comm_out_shapes("gather", gather),
        scratch_shapes=[pltpu.VMEM((ATT_G, t, HEAD_PAD), F32), pltpu.VMEM((ATT_G, t, HEAD_PAD), F32), pltpu.VMEM((t, t), F32)]
        + (_comm_scratch(ng) if ng else []),
        compiler_params=_params(),
    )(q, k, v, *gather)
    return outs[0], outs[1], list(outs[2:])


def _interleave(stages):
    live = list(stages)
    while live:
        still = []
        for g in live:
            try:
                next(g)
                still.append(g)
            except StopIteration:
                pass
        live = still


def _ssd_consts():
    emisc = np.zeros((HEAD_PAD, SSD_INNER), np.float32)
    for hd in range(SSD_HEADS):
        emisc[MISC_DT + hd, hd * SSD_P:(hd + 1) * SSD_P] = 1.0
    idx = np.arange(CHUNK)
    tri = (idx[:, None] >= idx[None, :]).astype(np.float32)
    return tuple(jnp.asarray(m, BF16) for m in (emisc, emisc.T.copy(), tri, tri.T.copy()))


def _ssd_chunk_common(cc, misc, emisc, tri, trit, dtb, a_exp):
    sig = jax.nn.sigmoid(cc)
    xa = cc * sig
    dt = jax.nn.softplus(_dot01(misc, emisc) + dtb)
    a = dt * a_exp
    acs = _dot01(a, tri, left=True)
    acs_t = _dot01(a, trit, dot=_dot_tn)
    alast = acs[CHUNK - 1:CHUNK, :]
    return xa, sig, dt, acs, acs_t, alast


def _decay(acs, acs_t, hd):
    row = lax.broadcasted_iota(jnp.int32, (CHUNK, CHUNK), 0)
    col = lax.broadcasted_iota(jnp.int32, (CHUNK, CHUNK), 1)
    diff = acs[:, hd * SSD_P:hd * SSD_P + 1] - acs_t[hd * SSD_P:hd * SSD_P + 1, :]
    return jnp.exp(jnp.where(row >= col, diff, -jnp.inf))


def _half_mask(hh):
    lane = lax.broadcasted_iota(jnp.int32, (CHUNK, 2 * SSD_P), 1)
    return (lane >= SSD_P) if hh else (lane < SSD_P)


def _gate_norm(y, zz):
    sg = jax.nn.sigmoid(zz)
    yz = y * (zz * sg)
    outs, rs = [], []
    half = SSD_INNER // SSD_GROUPS
    for g in range(SSD_GROUPS):
        yg = yz[:, g * half:(g + 1) * half]
        r = lax.rsqrt(jnp.mean(yg * yg, axis=-1, keepdims=True) + EPS)
        outs.append(yg * r)
        rs.append(r)
    return sg, jnp.concatenate(outs, axis=1), rs


def _ssd_fwd(xraw, misc, z, cw, cb, dtb, a_exp, d_exp, nw, consts):
    s = xraw.shape[0]
    nb = s // SSD_ROWS
    ncb = SSD_ROWS // CHUNK
    emisc, _, tri, trit = consts

    def body(x_ref, misc_ref, z_ref, cw_ref, cb_ref, dtb_ref, a_ref, d_ref, nw_ref, emisc_ref, tri_ref, trit_ref,
             c_ref, prev_ref, ypre_ref, yssd_ref, tail_s, state_s):
        i = pl.program_id(0)

        @pl.when(i == 0)
        def _():
            tail_s[...] = jnp.zeros(tail_s.shape, F32)
            state_s[...] = jnp.zeros(state_s.shape, F32)

        x = x_ref[...]
        xext = jnp.concatenate([tail_s[...], x], axis=0)
        acc = x * cw_ref[CONV_W - 1:CONV_W, :] + cb_ref[...]
        for j in range(1, CONV_W):
            acc = acc + pltpu.roll(xext, j, 0)[8:, :] * cw_ref[CONV_W - 1 - j:CONV_W - j, :]
        tail_s[...] = x[SSD_ROWS - 8:, :]
        c_ref[...] = acc

        def chunk(ci):
            r0 = ci * CHUNK
            xa, _, dt, acs, acs_t, alast = _ssd_chunk_common(
                c_ref[pl.ds(r0, CHUNK), :], misc_ref[pl.ds(r0, CHUNK), :], emisc_ref[...], tri_ref[...], trit_ref[...],
                dtb_ref[...], a_ref[...])
            yield
            xs = xa[:, :SSD_INNER]
            xdt = xs * dt
            wgt = (xdt * jnp.exp(alast - acs)).astype(BF16)
            e = jnp.exp(acs)
            ys, new_states, cms = [], [], []
            for g in range(SSD_GROUPS):
                bm = xa[:, SSD_INNER + g * SSD_N:SSD_INNER + (g + 1) * SSD_N].astype(BF16)
                cm = xa[:, SSD_INNER + SSD_GROUPS * SSD_N + g * SSD_N:SSD_INNER + SSD_GROUPS * SSD_N + (g + 1) * SSD_N].astype(BF16)
                cms.append(cm)
                cb_g = _dot_nt(cm, bm)
                gl = slice(g * 256, (g + 1) * 256)
                new_states.append(_dot_tn(bm, wgt[:, gl]))
                for jj in range(2):
                    pair = 2 * g + jj
                    xp = xdt[:, pair * 128:(pair + 1) * 128]
                    yp = None
                    for hh in range(2):
                        sc = (cb_g * _decay(acs, acs_t, 2 * pair + hh)).astype(BF16)
                        term = _dot(sc, jnp.where(_half_mask(hh), xp, 0.0).astype(BF16))
                        yp = term if yp is None else yp + term
                    ys.append(yp)
                yield
            prev = state_s[...]
            prev_ref[ci] = prev
            yoff = jnp.concatenate([_dot(cms[g], prev[:, g * 256:(g + 1) * 256].astype(BF16)) for g in range(SSD_GROUPS)],
                                   axis=1) * e
            state_s[...] = prev * jnp.exp(alast) + jnp.concatenate(new_states, axis=1)
            yield
            y = jnp.concatenate(ys, axis=1) + yoff + d_ref[...] * xs
            ypre_ref[pl.ds(r0, CHUNK), :] = y
            _, yn, _ = _gate_norm(y, z_ref[pl.ds(r0, CHUNK), :])
            yssd_ref[pl.ds(r0, CHUNK), :] = (yn * nw_ref[...]).astype(BF16)

        _interleave([chunk(ci) for ci in range(ncb)])

    return pl.pallas_call(
        body, name="ssd_fwd", grid=(nb,),
        in_specs=[_rows(SSD_ROWS, CONV_DIM), _rows(SSD_ROWS, HEAD_PAD), _rows(SSD_ROWS, SSD_INNER),
                  _full((CONV_W, CONV_DIM)), _full((1, CONV_DIM)), _full((1, SSD_INNER)), _full((1, SSD_INNER)),
                  _full((1, SSD_INNER)), _full((1, SSD_INNER)), _full((HEAD_PAD, SSD_INNER)), _full((CHUNK, CHUNK)),
                  _full((CHUNK, CHUNK))],
        out_specs=[_rows(SSD_ROWS, CONV_DIM), pl.BlockSpec((ncb, SSD_N, SSD_INNER), lambda i: (i, 0, 0)),
                   _rows(SSD_ROWS, SSD_INNER), _rows(SSD_ROWS, SSD_INNER)],
        out_shape=[jax.ShapeDtypeStruct((s, CONV_DIM), F32), jax.ShapeDtypeStruct((s // CHUNK, SSD_N, SSD_INNER), F32),
                   jax.ShapeDtypeStruct((s, SSD_INNER), F32), jax.ShapeDtypeStruct((s, SSD_INNER), BF16)],
        scratch_shapes=[pltpu.VMEM((8, CONV_DIM), F32), pltpu.VMEM((SSD_N, SSD_INNER), F32)],
        compiler_params=_params(),
    )(xraw, misc, z, cw, cb, dtb, a_exp, d_exp, nw, emisc, tri, trit)


def _outproj_fwd(oe, yssd, wout, h, nw):
    s = h.shape[0]
    wide = MLA_HEADS * HEAD_PAD

    def body(oe_ref, y_ref, w_ref, h_ref, nw_ref, mixed_ref, h1_ref):
        mixed = _dot(oe_ref[...], w_ref[0:wide, :]) + _dot(y_ref[...], w_ref[wide:, :])
        mixed_ref[...] = mixed
        h1_ref[...] = h_ref[...] + _rms_fwd(mixed, nw_ref[...])

    return pl.pallas_call(
        body, name="outproj_fwd", grid=(s // TM,),
        in_specs=[_rows(TM, wide), _rows(TM, SSD_INNER), _resident((wide + SSD_INNER, D_MODEL)), _rows(TM, D_MODEL),
                  _full((1, D_MODEL))],
        out_specs=[_rows(TM, D_MODEL), _rows(TM, D_MODEL)],
        out_shape=[jax.ShapeDtypeStruct((s, D_MODEL), F32)] * 2,
        compiler_params=_params(),
    )(oe, yssd, wout, h, nw)


def _mlp_fwd(h1, prew, wup, wdown, postw, target=None):
    s = h1.shape[0]
    fb = D_FF // N_DEV
    last = target is not None

    def body(h_ref, prew_ref, up_ref, down_ref, postw_ref, *rest):
        target_ref, (mb_ref, ab_ref, d_ref, out_ref) = (rest[0] if last else None), rest[last:last + 4]
        hh = h_ref[...]
        mb = _rms_fwd(hh, prew_ref[...]).astype(BF16)
        mb_ref[...] = mb
        d = jnp.zeros((TM, D_MODEL), F32)
        for j in range(N_DEV):
            a = jnp.maximum(_dot(mb, up_ref[j]), 0.0)
            ab_ref[j] = a.astype(BF16)
            d = d + _dot(jnp.square(a).astype(BF16), down_ref[j])
        d_ref[...] = d
        h2 = hh + _rms_fwd(d, postw_ref[...])
        if last:
            diff = h2 - target_ref[...]
            out_ref[...] = diff * (1.0 / D_MODEL)
            part = 0.5 * jnp.sum(jnp.mean(diff * diff, axis=-1, keepdims=True), axis=0, keepdims=True)
            _acc_rows(rest[-1], part, pl.program_id(0) == 0)
        else:
            out_ref[...] = h2

    stacked = pl.BlockSpec((N_DEV, TM, fb), lambda i: (0, i, 0))
    return pl.pallas_call(
        body, name="mlp_fwd_loss" if last else "mlp_fwd", grid=(s // TM,),
        in_specs=[_rows(TM, D_MODEL), _full((1, D_MODEL)), _resident((N_DEV, D_MODEL, fb)), _resident((N_DEV, fb, D_MODEL)),
                  _full((1, D_MODEL))] + ([_rows(TM, D_MODEL)] if last else []),
        out_specs=[_rows(TM, D_MODEL), stacked, _rows(TM, D_MODEL), _rows(TM, D_MODEL)] + ([_full((1, 1))] if last else []),
        out_shape=[jax.ShapeDtypeStruct((s, D_MODEL), BF16), jax.ShapeDtypeStruct((N_DEV, s, fb), BF16),
                   jax.ShapeDtypeStruct((s, D_MODEL), F32), jax.ShapeDtypeStruct((s, D_MODEL), F32)]
        + ([jax.ShapeDtypeStruct((1, 1), F32)] if last else []),
        compiler_params=_params(),
    )(h1, prew, wup, wdown, postw, *([target] if last else []))


def _mlp_bwd(dh2, d, h1, ab, prew, wup, wdown, postw):
    s = dh2.shape[0]
    fb = D_FF // N_DEV
    tm = TM // 2

    def body(dh2_ref, d_ref, h1_ref, ab_ref, prew_ref, up_ref, down_ref, postw_ref,
             dh1_ref, da_ref, dd_ref, gpost_ref, gpre_ref):
        first = pl.program_id(0) == 0
        dh2 = dh2_ref[...]
        dd, gpost = _rms_bwd(d_ref[...], postw_ref[...], dh2)
        _acc_rows(gpost_ref, gpost, first)
        ddb = dd.astype(BF16)
        dd_ref[...] = ddb

        def d_relu_squared(j):
            return _dot_nt(ddb, down_ref[j])

        def pointwise(j, dr):
            da = (dr * (2.0 * ab_ref[j].astype(F32))).astype(BF16)
            da_ref[j] = da
            return da

        dm = jnp.zeros((tm, D_MODEL), F32)
        nxt, da_prev = d_relu_squared(0), None
        for j in range(N_DEV):
            cur = nxt
            if j + 1 < N_DEV:
                nxt = d_relu_squared(j + 1)
            da = pointwise(j, cur)
            if da_prev is not None:
                dm = dm + _dot_nt(da_prev, up_ref[j - 1])
            da_prev = da
        dm = dm + _dot_nt(da_prev, up_ref[N_DEV - 1])
        dx, gpre = _rms_bwd(h1_ref[...], prew_ref[...], dm)
        _acc_rows(gpre_ref, gpre, first)
        dh1_ref[...] = dh2 + dx

    stacked = pl.BlockSpec((N_DEV, tm, fb), lambda i: (0, i, 0))
    return pl.pallas_call(
        body, name="mlp_bwd", grid=(s // tm,),
        in_specs=[_rows(tm, D_MODEL)] * 3 + [stacked, _full((1, D_MODEL)), _resident((N_DEV, D_MODEL, fb)),
                                              _resident((N_DEV, fb, D_MODEL)), _full((1, D_MODEL))],
        out_specs=[_rows(tm, D_MODEL), stacked, _rows(tm, D_MODEL), _full((1, D_MODEL)), _full((1, D_MODEL))],
        out_shape=[jax.ShapeDtypeStruct((s, D_MODEL), F32), jax.ShapeDtypeStruct((N_DEV, s, fb), BF16),
                   jax.ShapeDtypeStruct((s, D_MODEL), BF16), jax.ShapeDtypeStruct((1, D_MODEL), F32),
                   jax.ShapeDtypeStruct((1, D_MODEL), F32)],
        compiler_params=_params(),
    )(dh2, d, h1, ab, prew, wup, wdown, postw)


def _matmul_tn(a, b, name, tk=TK_DW):
    s, m = a.shape
    n = b.shape[1]
    tn = n if n <= 1024 else (n // 2 if (n // 2) % 128 == 0 else n // 3)
    tk = min(tk, s)
    assert n % tn == 0 and tn % 128 == 0 and s % tk == 0

    def body(a_ref, b_ref, o_ref):
        part = _dot_tn(a_ref[...], b_ref[...])

        @pl.when(pl.program_id(1) == 0)
        def _():
            o_ref[...] = part

        @pl.when(pl.program_id(1) != 0)
        def _():
            o_ref[...] += part

    return pl.pallas_call(
        body, name=name, grid=(n // tn, s // tk),
        in_specs=[pl.BlockSpec((tk, m), lambda j, k: (k, 0)), pl.BlockSpec((tk, tn), lambda j, k: (k, j))],
        out_specs=pl.BlockSpec((m, tn), lambda j, k: (0, j)),
        out_shape=jax.ShapeDtypeStruct((m, n), F32),
        compiler_params=_params(),
    )(a, b)


def _matmul_tn_stacked(a, b, name, a_stacked, square_a=False, tk=TK_DW):
    tk = min(tk, a.shape[-2])
    if a_stacked:
        _, s, m = a.shape
        n = b.shape[1]
        in_specs = [pl.BlockSpec((1, tk, m), lambda j, k: (j, k, 0)), pl.BlockSpec((tk, n), lambda j, k: (k, 0))]
    else:
        s, m = a.shape
        n = b.shape[2]
        in_specs = [pl.BlockSpec((tk, m), lambda j, k: (k, 0)), pl.BlockSpec((1, tk, n), lambda j, k: (j, k, 0))]

    nk = s // tk

    def body(a_ref, b_ref, o_ref, acc_s):
        av = a_ref[0] if a_stacked else a_ref[...]
        bv = b_ref[...] if a_stacked else b_ref[0]
        if square_a:
            av = jnp.square(av.astype(F32)).astype(BF16)
        part = _dot_tn(av, bv)
        k = pl.program_id(1)

        @pl.when(k == 0)
        def _():
            acc_s[...] = part

        @pl.when(jnp.logical_and(k != 0, k != nk - 1))
        def _():
            acc_s[...] += part

        @pl.when(k == nk - 1)
        def _():
            o_ref[0] = (part if nk == 1 else acc_s[...] + part).astype(BF16)

    return pl.pallas_call(
        body, name=name, grid=(N_DEV, nk),
        in_specs=in_specs,
        out_specs=pl.BlockSpec((1, m, n), lambda j, k: (j, 0, 0)),
        out_shape=jax.ShapeDtypeStruct((N_DEV, m, n), BF16),
        scratch_shapes=[pltpu.VMEM((m, n), F32)],
        compiler_params=_params(),
    )(a, b)


def _outproj_bwd(dh1, mixed, nw, wout, oe):
    s = dh1.shape[0]
    wide = MLA_HEADS * HEAD_PAD

    def body(dh1_ref, mixed_ref, nw_ref, w_ref, oe_ref, dmix_ref, doe_ref, dy_ref, gnw_ref, delta_ref):
        dmix, gnw = _rms_bwd(mixed_ref[...], nw_ref[...], dh1_ref[...])
        _acc_rows(gnw_ref, gnw, pl.program_id(0) == 0)
        dmb = dmix.astype(BF16)
        dmix_ref[...] = dmb
        doe_ref[...] = _dot_nt(dmb, w_ref[0:wide, :]).astype(BF16)
        dy_ref[...] = _dot_nt(dmb, w_ref[wide:, :])
        ones = jnp.ones((8, HEAD_PAD), BF16)
        for hd in range(MLA_HEADS):
            cols = slice(hd * HEAD_PAD, (hd + 1) * HEAD_PAD)
            prod = oe_ref[:, cols].astype(F32) * doe_ref[:, cols].astype(F32)
            delta_ref[hd] = _dot01(prod, ones, dot=_dot_nt, left=True)

    return pl.pallas_call(
        body, name="outproj_bwd", grid=(s // TM,),
        in_specs=[_rows(TM, D_MODEL), _rows(TM, D_MODEL), _full((1, D_MODEL)), _resident((wide + SSD_INNER, D_MODEL)),
                  _rows(TM, wide)],
        out_specs=[_rows(TM, D_MODEL), _rows(TM, wide), _rows(TM, SSD_INNER), _full((1, D_MODEL)),
                   pl.BlockSpec((MLA_HEADS, 8, TM), lambda i: (0, 0, i))],
        out_shape=[jax.ShapeDtypeStruct((s, D_MODEL), BF16), jax.ShapeDtypeStruct((s, wide), BF16),
                   jax.ShapeDtypeStruct((s, SSD_INNER), F32), jax.ShapeDtypeStruct((1, D_MODEL), F32),
                   jax.ShapeDtypeStruct((MLA_HEADS, 8, s), F32)],
        compiler_params=_params(),
    )(dh1, mixed, nw, wout, oe)


def _attn_bwd(q, k, v, do, lse, delta, exchange=()):
    s = q.shape[0]
    t = ATT_T
    nq = s // t
    pair = 2 * HEAD_PAD
    ne = len(exchange)

    def body(q_ref, k_ref, v_ref, do_ref, lse_ref, delta_ref, *rest):
        e_in, (dq_ref, dk_ref, dv_ref), e_out = rest[:ne], rest[ne:ne + 3], rest[ne + 3:2 * ne + 3]
        dk_s, dv_s, bias_s = rest[2 * ne + 3:2 * ne + 6]
        kb = pl.program_id(1)
        _hosted_comm("exchange", e_in, e_out, rest[2 * ne + 6:],
                     jnp.logical_and(pl.program_id(0) == 0, kb == 0),
                     jnp.logical_and(pl.program_id(0) == MLA_HEADS // 2 - 1, kb == nq - 1))

        @pl.when(jnp.logical_and(pl.program_id(0) == 0, kb == 0))
        def _():
            bias_s[...] = _chunk_bias(t, keys_on_rows=True)

        @pl.when(kb == 0)
        def _():
            dq_ref[...] = jnp.zeros(dq_ref.shape, F32)

        def step(qb, diagonal):
            r0 = pl.multiple_of(qb * t, t)
            for hh in range(2):
                cols = slice(hh * HEAD_PAD, (hh + 1) * HEAD_PAD)
                kk = k_ref[:, cols]
                qq = q_ref[pl.ds(r0, t), cols]
                dd = do_ref[pl.ds(r0, t), cols]
                sc = _dot_nt(kk, qq) * ATT_SCALE_LOG2
                if diagonal:
                    sc = sc + bias_s[...]
                p = jnp.exp2(sc - lse_ref[hh, 0:1, pl.ds(r0, t)])
                dv = _dot(p.astype(BF16), dd)
                dp = _dot_nt(v_ref[:, cols], dd)
                ds = (p * (dp - delta_ref[hh, 0:1, pl.ds(r0, t)]) * ATT_SCALE).astype(BF16)
                dk = _dot(ds, qq)
                if diagonal:
                    dv_s[:, cols] = dv
                    dk_s[:, cols] = dk
                else:
                    dv_s[:, cols] += dv
                    dk_s[:, cols] += dk
                dq_ref[pl.ds(r0, t), cols] += _dot_tn(ds, kk)

        def loop(i, c):
            for u in range(ATT_UNROLL):
                step(kb + 1 + u + ATT_UNROLL * i, False)
            return c

        step(kb, True)
        later_tiles = nq - 1 - kb
        lax.fori_loop(0, later_tiles // ATT_UNROLL, loop, 0)
        left = later_tiles % ATT_UNROLL
        for u in range(ATT_UNROLL - 1):
            @pl.when(left > u)
            def _(u=u):
                step(nq - left + u, False)

        dk_ref[...] = dk_s[...].astype(BF16)
        dv_ref[...] = dv_s[...].astype(BF16)

    whole = pl.BlockSpec((s, pair), lambda h, i: (0, h))
    tile = pl.BlockSpec((t, pair), lambda h, i: (i, h))
    rowvec = pl.BlockSpec((2, 8, s), lambda h, i: (h, 0, 0))
    wide = MLA_HEADS * HEAD_PAD
    outs = pl.pallas_call(
        body, name="attn_bwd_exchange" if ne else "attn_bwd", grid=(MLA_HEADS // 2, nq),
        in_specs=[whole, tile, tile, whole, rowvec, rowvec] + [_ANY] * ne,
        out_specs=[whole, tile, tile] + [_ANY] * ne,
        out_shape=[jax.ShapeDtypeStruct((s, wide), F32)] + [jax.ShapeDtypeStruct((s, wide), BF16)] * 2
        + _comm_out_shapes("exchange", exchange),
        scratch_shapes=[pltpu.VMEM((t, pair), F32), pltpu.VMEM((t, pair), F32), pltpu.VMEM((t, t), F32)]
        + (_comm_scratch(ne) if ne else []),
        compiler_params=_params(),
    )(q, k, v, do, lse, delta, *exchange)
    return outs[0], outs[1], outs[2], list(outs[3:])


def _ssd_bwd(dy, ypre, z, c, xraw, misc, prev, cw, dtb, a_exp, d_exp, nw, consts):
    s = dy.shape[0]
    nb = s // SSD_ROWS
    ncb = SSD_ROWS // CHUNK
    emisc, emisc_t, tri, trit = consts

    def body(dy_ref, ypre_ref, z_ref, c_ref, x_ref, misc_ref, prev_ref, cw_ref, dtb_ref, a_ref, d_ref, nw_ref,
             emisc_ref, emisct_ref, tri_ref, trit_ref,
             dz_ref, dx_ref, dmisc_ref, gnw_ref, gd_ref, galog_ref, gdtb_ref, gcw_ref, gcb_ref,
             dst_s, dc_s, head_s):
        i = pl.program_id(0)
        first = i == 0

        @pl.when(first)
        def _():
            dst_s[...] = jnp.zeros(dst_s.shape, F32)
            head_s[...] = jnp.zeros(head_s.shape, F32)
            gnw_ref[...] = jnp.zeros(gnw_ref.shape, F32)
            gd_ref[...] = jnp.zeros(gd_ref.shape, F32)
            galog_ref[...] = jnp.zeros(galog_ref.shape, F32)
            gdtb_ref[...] = jnp.zeros(gdtb_ref.shape, F32)

        a_exp_v = a_ref[...]
        a8 = _dot01(a_exp_v, emisct_ref[...]) * (1.0 / SSD_P)

        def chunk(ci):
            r0 = ci * CHUNK
            cc = c_ref[pl.ds(r0, CHUNK), :]
            mm = misc_ref[pl.ds(r0, CHUNK), :]
            xa, sig_c, dt, acs, acs_t, alast = _ssd_chunk_common(cc, mm, emisc_ref[...], tri_ref[...], trit_ref[...],
                                                              dtb_ref[...], a_exp_v)
            yield
            xs = xa[:, :SSD_INNER]
            xdt = xs * dt
            y = ypre_ref[pl.ds(r0, CHUNK), :]
            zz = z_ref[pl.ds(r0, CHUNK), :]
            sg, yn, rs = _gate_norm(y, zz)
            dyo = dy_ref[pl.ds(r0, CHUNK), :]
            gnw_ref[...] += jnp.sum(dyo * yn, axis=0, keepdims=True)
            dyn = dyo * nw_ref[...]
            half = SSD_INNER // SSD_GROUPS
            dyz_parts = []
            for g in range(SSD_GROUPS):
                gl = slice(g * half, (g + 1) * half)
                dyz_parts.append(rs[g] * (dyn[:, gl] - yn[:, gl] * jnp.mean(dyn[:, gl] * yn[:, gl], axis=-1, keepdims=True)))
            dyz = jnp.concatenate(dyz_parts, axis=1)
            dz_ref[pl.ds(r0, CHUNK), :] = dyz * y * (sg * (1.0 + zz * (1.0 - sg)))
            dyp = dyz * (zz * sg)
            dypb = dyp.astype(BF16)
            gd_ref[...] += jnp.sum(dyp * xs, axis=0, keepdims=True)
            yield
            prev = prev_ref[ci]
            cd = jnp.exp(alast)
            e = jnp.exp(acs)
            dsx = jnp.exp(alast - acs)
            wgt = (xdt * dsx).astype(BF16)
            dze = (dyp * e).astype(BF16)
            dprev_parts, diag_all, dbm, dcm, yoff_parts, bms = [], [], [], [], [], []
            lane8 = lax.broadcasted_iota(jnp.int32, (CHUNK, HEAD_PAD), 1)
            diag8 = jnp.zeros((CHUNK, HEAD_PAD), F32)
            for g in range(SSD_GROUPS):
                gl = slice(g * 256, (g + 1) * 256)
                bm = xa[:, SSD_INNER + g * SSD_N:SSD_INNER + (g + 1) * SSD_N].astype(BF16)
                cm = xa[:, SSD_INNER + SSD_GROUPS * SSD_N + g * SSD_N:SSD_INNER + SSD_GROUPS * SSD_N + (g + 1) * SSD_N].astype(BF16)
                bms.append(bm)
                prev_g = prev[:, gl].astype(BF16)
                dcm_g = _dot_nt(dze[:, gl], prev_g)
                dprev_parts.append(_dot_tn(cm, dze[:, gl]))
                cb_g = _dot_nt(cm, bm)
                dcb = jnp.zeros((CHUNK, CHUNK), F32)
                diag_parts = []
                for jj in range(2):
                    pair = 2 * g + jj
                    pl_ = slice(pair * 128, (pair + 1) * 128)
                    xp = xdt[:, pl_]
                    dyp_p = dypb[:, pl_]
                    dxp = jnp.zeros((CHUNK, 128), F32)
                    for hh in range(2):
                        hd = 2 * pair + hh
                        dec = _decay(acs, acs_t, hd)
                        xm = jnp.where(_half_mask(hh), xp, 0.0).astype(BF16)
                        dsc = _dot_nt(dyp_p, xm) * dec
                        dcb = dcb + dsc
                        sc = (cb_g * dec).astype(BF16)
                        dxp = dxp + jnp.where(_half_mask(hh), _dot_tn(sc, dyp_p), 0.0)
                        dm = dsc * cb_g
                        diag8 = diag8 + jnp.where(lane8 == MISC_DT + hd, jnp.sum(dm - dm.T, axis=1, keepdims=True), 0.0)
                    diag_parts.append(dxp)
                dcbb = dcb.astype(BF16)
                dcm.append(dcm_g + _dot(dcbb, bm))
                dbm.append(_dot_tn(dcbb, cm))
                diag_all.append(jnp.concatenate(diag_parts, axis=1))
                yoff_parts.append(_dot(cm, prev_g) * e[:, gl])
                yield
            dst = dst_s[...]
            glast = jnp.sum(dst * prev, axis=0, keepdims=True) * cd
            dxdt_state_parts = []
            for g in range(SSD_GROUPS):
                gl = slice(g * 256, (g + 1) * 256)
                dst_g = dst[:, gl].astype(BF16)
                dxdt_state_parts.append(_dot(bms[g], dst_g) * dsx[:, gl])
                dbm[g] = dbm[g] + _dot_nt(wgt[:, gl], dst_g)
            dst_s[...] = dst * cd + jnp.concatenate(dprev_parts, axis=1)
            yield
            dxdt_state = jnp.concatenate(dxdt_state_parts, axis=1)
            dxdt = jnp.concatenate(diag_all, axis=1) + dxdt_state
            dacs = dyp * jnp.concatenate(yoff_parts, axis=1) - xdt * dxdt_state
            last = jnp.sum(xdt * dxdt_state, axis=0, keepdims=True) + glast
            row = lax.broadcasted_iota(jnp.int32, (CHUNK, SSD_INNER), 0)
            dacs = dacs + jnp.where(row == CHUNK - 1, last, 0.0)
            dacs8 = _dot01(dacs, emisct_ref[...]) + diag8
            da8 = _dot01(dacs8, trit_ref[...], left=True)
            ddt8 = da8 * a8 + _dot01(dxdt * xs, emisct_ref[...])
            yield
            dtr8 = mm + _dot01(dtb_ref[...], emisct_ref[...]) * (1.0 / SSD_P)
            dt8 = jax.nn.softplus(dtr8)
            lane = lax.broadcasted_iota(jnp.int32, (CHUNK, HEAD_PAD), 1)
            on_dt = jnp.logical_and(lane >= MISC_DT, lane < MISC_DT + SSD_HEADS)
            ddtr8 = jnp.where(on_dt, ddt8 * jax.nn.sigmoid(dtr8), 0.0)
            dmisc_ref[pl.ds(r0, CHUNK), :] = ddtr8
            gdtb_ref[...] += jnp.sum(ddtr8, axis=0, keepdims=True)
            galog_ref[...] += jnp.sum(jnp.where(on_dt, da8 * dt8, 0.0), axis=0, keepdims=True) * a8
            dxs = d_ref[...] * dyp + dxdt * dt
            dxa = jnp.concatenate([dxs] + dbm + dcm, axis=1)
            dc_s[pl.ds(r0, CHUNK), :] = dxa * (sig_c * (1.0 + cc * (1.0 - sig_c)))

        _interleave([chunk(ci) for ci in reversed(range(ncb))])

        dc = dc_s[...]
        x = x_ref[...]
        dcext = jnp.concatenate([dc, head_s[...]], axis=0)
        dx = dc * cw_ref[CONV_W - 1:CONV_W, :]
        rows = [jnp.sum(dc * x, axis=0, keepdims=True)]
        for j in range(1, CONV_W):
            ahead = pltpu.roll(dcext, SSD_ROWS + 8 - j, 0)[:SSD_ROWS, :]
            dx = dx + ahead * cw_ref[CONV_W - 1 - j:CONV_W - j, :]
            rows.insert(0, jnp.sum(ahead * x, axis=0, keepdims=True))
        dx_ref[...] = dx
        head_s[...] = dc[:8, :]
        gcw = jnp.concatenate(rows, axis=0)

        @pl.when(first)
        def _():
            gcw_ref[...] = gcw
            gcb_ref[...] = jnp.sum(dc, axis=0, keepdims=True)

        @pl.when(jnp.logical_not(first))
        def _():
            gcw_ref[...] += gcw
            gcb_ref[...] += jnp.sum(dc, axis=0, keepdims=True)

    def rev(width):
        return pl.BlockSpec((SSD_ROWS, width), lambda i: (nb - 1 - i, 0))

    return pl.pallas_call(
        body, name="ssd_bwd", grid=(nb,),
        in_specs=[rev(SSD_INNER), rev(SSD_INNER), rev(SSD_INNER), rev(CONV_DIM), rev(CONV_DIM),
                  rev(HEAD_PAD), pl.BlockSpec((ncb, SSD_N, SSD_INNER), lambda i: (nb - 1 - i, 0, 0)),
                  _full((CONV_W, CONV_DIM)), _full((1, SSD_INNER)), _full((1, SSD_INNER)), _full((1, SSD_INNER)),
                  _full((1, SSD_INNER)), _full((HEAD_PAD, SSD_INNER)), _full((SSD_INNER, HEAD_PAD)), _full((CHUNK, CHUNK)),
                  _full((CHUNK, CHUNK))],
        out_specs=[rev(SSD_INNER), rev(CONV_DIM), rev(HEAD_PAD), _full((1, SSD_INNER)), _full((1, SSD_INNER)),
                   _full((1, HEAD_PAD)), _full((1, HEAD_PAD)), _full((CONV_W, CONV_DIM)), _full((1, CONV_DIM))],
        out_shape=[jax.ShapeDtypeStruct((s, SSD_INNER), F32), jax.ShapeDtypeStruct((s, CONV_DIM), F32),
                   jax.ShapeDtypeStruct((s, HEAD_PAD), F32), jax.ShapeDtypeStruct((1, SSD_INNER), F32),
                   jax.ShapeDtypeStruct((1, SSD_INNER), F32), jax.ShapeDtypeStruct((1, HEAD_PAD), F32),
                   jax.ShapeDtypeStruct((1, HEAD_PAD), F32), jax.ShapeDtypeStruct((CONV_W, CONV_DIM), F32),
                   jax.ShapeDtypeStruct((1, CONV_DIM), F32)],
        scratch_shapes=[pltpu.VMEM((SSD_N, SSD_INNER), F32), pltpu.VMEM((SSD_ROWS, CONV_DIM), F32), pltpu.VMEM((8, CONV_DIM), F32)],
        compiler_params=_params(),
    )(dy, ypre, z, c, xraw, misc, prev, cw, dtb, a_exp, d_exp, nw, emisc, emisc_t, tri, trit)


def _qkv_bwd(dq, dk, dv, cq, ckv, qnw, kvnw, wuq, wkv, cosf, sinf):
    s = dq.shape[0]
    wide = MLA_HEADS * HEAD_PAD

    def body(dq_ref, dk_ref, dv_ref, cq_ref, ckv_ref, qnw_ref, kvnw_ref, wuq_ref, wkv_ref, cos_ref, sin_ref,
             dqb_ref, dkvb_ref, dcq_ref, dckv_ref, dmisc_ref, gq_ref, gkv_ref):
        first = pl.program_id(0) == 0
        cosf, sinf = cos_ref[...], sin_ref[...]
        dkr = jnp.zeros((TM, HEAD_PAD), F32)
        for hd in range(MLA_HEADS):
            cols = slice(hd * HEAD_PAD, (hd + 1) * HEAD_PAD)
            dqb_ref[:, cols] = _rope(dq_ref[:, cols], cosf, sinf, -1.0).astype(BF16)
            dkh = dk_ref[:, cols]
            dkvb_ref[:, cols] = dkh.astype(BF16)
            dkr = dkr + dkh
        dkvb_ref[:, wide:] = dv_ref[...].astype(BF16)
        lane = lax.broadcasted_iota(jnp.int32, dkr.shape, 1)
        in_rope = jnp.logical_and(lane >= MISC_ROPE, lane < MISC_ROPE + QK_ROPE)
        dmisc_ref[...] = jnp.where(in_rope, _rope(jnp.where(in_rope, dkr, 0.0), cosf, sinf, -1.0), 0.0)
        dcq, gq = _rms_bwd(cq_ref[...], qnw_ref[...], _dot_nt(dqb_ref[...], wuq_ref[...]))
        dcq_ref[...] = dcq
        _acc_rows(gq_ref, gq, first)
        dckv, gkv = _rms_bwd(ckv_ref[...], kvnw_ref[...], _dot_nt(dkvb_ref[...], wkv_ref[...]))
        dckv_ref[...] = dckv
        _acc_rows(gkv_ref, gkv, first)

    return pl.pallas_call(
        body, name="qkv_bwd", grid=(s // TM,),
        in_specs=[_rows(TM, wide)] * 3 + [_rows(TM, Q_RANK), _rows(TM, KV_RANK), _full((1, Q_RANK)), _full((1, KV_RANK)),
                                          _resident((Q_RANK, wide)), _resident((KV_RANK, 2 * wide)), _rows(TM, HEAD_PAD), _rows(TM, HEAD_PAD)],
        out_specs=[_rows(TM, wide), _rows(TM, 2 * wide), _rows(TM, Q_RANK), _rows(TM, KV_RANK), _rows(TM, HEAD_PAD),
                   _full((1, Q_RANK)), _full((1, KV_RANK))],
        out_shape=[jax.ShapeDtypeStruct((s, wide), BF16), jax.ShapeDtypeStruct((s, 2 * wide), BF16),
                   jax.ShapeDtypeStruct((s, Q_RANK), F32), jax.ShapeDtypeStruct((s, KV_RANK), F32),
                   jax.ShapeDtypeStruct((s, HEAD_PAD), F32), jax.ShapeDtypeStruct((1, Q_RANK), F32),
                   jax.ShapeDtypeStruct((1, KV_RANK), F32)],
        compiler_params=_params(),
    )(dq, dk, dv, cq, ckv, qnw, kvnw, wuq, wkv, cosf, sinf)


def _inproj_bwd(dcq, dckv, dmisc_rope, dmisc_dt, dz, dxbc, h, dh1, nw, win):
    s = h.shape[0]

    def body(dcq_ref, dckv_ref, dma_ref, dmb_ref, dz_ref, dxbc_ref, h_ref, dh1_ref, nw_ref, w_ref, dproj_ref, dh0_ref, gnw_ref):
        dproj_ref[:, 0:768] = dcq_ref[...].astype(BF16)
        dproj_ref[:, 768:1024] = dckv_ref[...].astype(BF16)
        dproj_ref[:, 1024:1152] = (dma_ref[...] + dmb_ref[...]).astype(BF16)
        dproj_ref[:, 1152:1664] = dz_ref[...].astype(BF16)
        dproj_ref[:, 1664:2688] = dxbc_ref[...].astype(BF16)
        du = _dot_nt(dproj_ref[...], w_ref[...])
        dx, gnw = _rms_bwd(h_ref[...], nw_ref[...], du)
        _acc_rows(gnw_ref, gnw, pl.program_id(0) == 0)
        dh0_ref[...] = dh1_ref[...] + dx

    return pl.pallas_call(
        body, name="inproj_bwd", grid=(s // TM,),
        in_specs=[_rows(TM, Q_RANK), _rows(TM, KV_RANK), _rows(TM, HEAD_PAD), _rows(TM, HEAD_PAD), _rows(TM, SSD_INNER),
                  _rows(TM, CONV_DIM), _rows(TM, D_MODEL), _rows(TM, D_MODEL), _full((1, D_MODEL)), _resident((D_MODEL, IN_PAD))],
        out_specs=[_rows(TM, IN_PAD), _rows(TM, D_MODEL), _full((1, D_MODEL))],
        out_shape=[jax.ShapeDtypeStruct((s, IN_PAD), BF16), jax.ShapeDtypeStruct((s, D_MODEL), F32),
                   jax.ShapeDtypeStruct((1, D_MODEL), F32)],
        compiler_params=_params(),
    )(dcq, dckv, dmisc_rope, dmisc_dt, dz, dxbc, h, dh1, nw, win)


def _row_tile(rows, cols):
    cap = max(8, (1 << 18) // max(cols, 128))
    best = None
    for t in range(8, rows + 1, 8):
        if rows % t == 0 and t <= cap:
            best = t
    return best if best is not None else rows


def _adamw(w, g, m, v, name):
    rows, cols = w.shape
    tr = _row_tile(rows, cols)

    def body(w_ref, g_ref, m_ref, v_ref, d_ref, m2_ref, v2_ref):
        gg = g_ref[...]
        m2 = ADAM_B1 * m_ref[...] + (1.0 - ADAM_B1) * gg
        v2 = ADAM_B2 * v_ref[...] + (1.0 - ADAM_B2) * jnp.square(gg)
        m_hat = m2 / (1.0 - ADAM_B1 ** ADAM_STEP)
        v_hat = v2 / (1.0 - ADAM_B2 ** ADAM_STEP)
        d_ref[...] = -ADAM_LR * (m_hat / (jnp.sqrt(v_hat) + ADAM_EPS) + ADAM_WD * w_ref[...])
        m2_ref[...] = m2
        v2_ref[...] = v2

    spec = pl.BlockSpec((tr, cols), lambda i: (i, 0))
    return pl.pallas_call(
        body, name=name, grid=(rows // tr,),
        in_specs=[spec] * 4, out_specs=[spec] * 3,
        out_shape=[jax.ShapeDtypeStruct((rows, cols), F32)] * 3,
    )(w, g, m, v)


def _sum_adamw(slots, w, m, v, name):
    _, rows, cols = w.shape
    tr = _row_tile(rows, cols)
    nb = rows // tr

    def body(s0_ref, s1_ref, w_ref, m_ref, v_ref, g_ref, d_ref, m2_ref, v2_ref):
        for l, ref in enumerate((s0_ref, s1_ref)):
            @pl.when(pl.program_id(0) == l)
            def _(ref=ref):
                acc = ref[0].astype(F32)
                for i in range(1, N_DEV):
                    acc = acc + ref[i].astype(F32)
                g_ref[...] = acc

        gg = g_ref[...]
        m2 = ADAM_B1 * m_ref[...] + (1.0 - ADAM_B1) * gg
        v2 = ADAM_B2 * v_ref[...] + (1.0 - ADAM_B2) * jnp.square(gg)
        m_hat = m2 / (1.0 - ADAM_B1 ** ADAM_STEP)
        v_hat = v2 / (1.0 - ADAM_B2 ** ADAM_STEP)
        d_ref[...] = -ADAM_LR * (m_hat / (jnp.sqrt(v_hat) + ADAM_EPS) + ADAM_WD * w_ref[...])
        m2_ref[...] = m2
        v2_ref[...] = v2

    slot_spec = lambda layer: pl.BlockSpec((N_DEV, tr, cols), lambda l, i: (0, jnp.where(l == layer, i, (nb - 1) * (1 - layer)), 0))
    spec = pl.BlockSpec((None, tr, cols), lambda l, i: (l, i, 0))
    return pl.pallas_call(
        body, name=name, grid=(DEPTH, nb),
        in_specs=[slot_spec(0), slot_spec(1), spec, spec, spec], out_specs=[spec] * 4,
        out_shape=[jax.ShapeDtypeStruct(w.shape, F32)] * 4,
        compiler_params=_params(),
    )(slots[0], slots[1], w, m, v)


_MESH = pl.DeviceIdType.MESH
_ANY = pl.BlockSpec(memory_space=pl.ANY)


def _my_place():
    return lax.axis_index("x"), lax.axis_index("y"), lax.axis_index("c")


def _flip(place, k):
    x, y, c = place
    return (1 - x if k & 4 else x, 1 - y if k & 2 else y, 1 - c if k & 1 else c)


def _block_id(place):
    return 4 * place[0] + 2 * place[1] + place[2]


def _peer_copies(kind, in_refs, out_refs, send_sems, recv_sems, local_sems):
    me = _my_place()
    my = _block_id(me)
    remote, local = [], []
    for a, (x_ref, out_ref) in enumerate(zip(in_refs, out_refs)):
        src_of = (lambda place, r=x_ref: r) if kind == "gather" else (lambda place, r=x_ref: r.at[_block_id(place)])
        local.append(pltpu.make_async_copy(src_of(me), out_ref.at[my], local_sems.at[a]))
        for k in range(1, N_DEV):
            peer = _flip(me, k)
            remote.append(pltpu.make_async_remote_copy(
                src_ref=src_of(peer), dst_ref=out_ref.at[my], send_sem=send_sems.at[a * 7 + k - 1],
                recv_sem=recv_sems.at[a * 7 + k - 1], device_id=peer, device_id_type=_MESH))
    return remote, local


def _comm_out_shapes(kind, arrays):
    return [jax.ShapeDtypeStruct((N_DEV, *a.shape) if kind == "gather" else a.shape, a.dtype) for a in arrays]


def _comm_scratch(n):
    return [pltpu.SemaphoreType.DMA((7 * n,)), pltpu.SemaphoreType.DMA((7 * n,)), pltpu.SemaphoreType.DMA((n,))]


def _hosted_comm(kind, in_refs, out_refs, sems, first, last):
    if not in_refs:
        return

    @pl.when(first)
    def _():
        remote, local = _peer_copies(kind, in_refs, out_refs, *sems)
        for cp in local + remote:
            cp.start()

    @pl.when(last)
    def _():
        remote, local = _peer_copies(kind, in_refs, out_refs, *sems)
        for cp in remote:
            cp.wait()
        for cp in local:
            cp.wait()


def _two_level_gather_steps(in_refs, out_refs, send_sems, recv_sems, local_sems):
    n = len(in_refs)
    me = _my_place()
    x, y, c = me
    sibling = (x, y, 1 - c)
    chips = [(1 - x, y), (x, 1 - y), (1 - x, 1 - y)]

    def copy(a, k, place, to, src=None):
        block = out_refs[a].at[_block_id(place)]
        return pltpu.make_async_remote_copy(
            src_ref=block if src is None else src, dst_ref=block, send_sem=send_sems.at[7 * a + k],
            recv_sem=recv_sems.at[7 * a + k], device_id=to, device_id_type=_MESH)

    mine = [pltpu.make_async_copy(in_refs[a], out_refs[a].at[_block_id(me)], local_sems.at[a]) for a in range(n)]
    first = [copy(a, 0, me, sibling, src=in_refs[a]) for a in range(n)]
    first += [copy(a, 1 + j, me, (*chip, c), src=in_refs[a]) for a in range(n) for j, chip in enumerate(chips)]
    passed = [copy(a, 4 + j, (*chip, c), sibling) for a in range(n) for j, chip in enumerate(chips)]

    def send():
        for cp in mine + first:
            cp.start()

    def forward():
        for a in range(n):
            for j, chip in enumerate(chips):
                copy(a, 1 + j, (*chip, c), me).wait_recv()
                passed[3 * a + j].start()

    def finish():
        for a in range(n):
            copy(a, 0, sibling, me).wait_recv()
            for j, chip in enumerate(chips):
                copy(a, 4 + j, (*chip, 1 - c), me).wait_recv()
        for cp in first + passed:
            cp.wait_send()
        for cp in mine:
            cp.wait()

    return send, forward, finish


def _gather_two_level(arrays, name):
    n = len(arrays)

    def body(*refs):
        for step in _two_level_gather_steps(refs[:n], refs[n:2 * n], *refs[2 * n:]):
            step()

    return pl.pallas_call(
        body, name=name, out_shape=_comm_out_shapes("gather", arrays),
        in_specs=[_ANY] * n, out_specs=[_ANY] * n, scratch_shapes=_comm_scratch(n),
    )(*arrays)


def _hosted_gather(in_refs, out_refs, sems, first, middle, last):
    if not in_refs:
        return
    for when, index in ((first, 0), (middle, 1), (last, 2)):
        @pl.when(when)
        def _(index=index):
            _two_level_gather_steps(in_refs, out_refs, *sems)[index]()


def _comm(kind, arrays, name):
    n = len(arrays)

    def body(*refs):
        remote, local = _peer_copies(kind, refs[:n], refs[n:2 * n], *refs[2 * n:])
        for cp in local + remote:
            cp.start()
        for cp in remote:
            cp.wait()
        for cp in local:
            cp.wait()

    return pl.pallas_call(
        body, name=name, out_shape=_comm_out_shapes(kind, arrays),
        in_specs=[_ANY] * n, out_specs=[_ANY] * n, scratch_shapes=_comm_scratch(n),
    )(*arrays)


def _all_reduce_small(part):
    rows, lanes = part.shape
    vmem = pl.BlockSpec(memory_space=pltpu.VMEM)

    def body(x_ref, gath_ref, sum_ref, send_sems, recv_sems):
        me = _my_place()
        my = _block_id(me)
        gath_ref[my] = x_ref[...]
        copies = []
        for k in range(1, N_DEV):
            cp = pltpu.make_async_remote_copy(
                src_ref=x_ref, dst_ref=gath_ref.at[my], send_sem=send_sems.at[k - 1], recv_sem=recv_sems.at[k - 1],
                device_id=_flip(me, k), device_id_type=_MESH)
            cp.start()
            copies.append(cp)
        for cp in copies:
            cp.wait()
        acc = gath_ref[0]
        for i in range(1, N_DEV):
            acc = acc + gath_ref[i]
        sum_ref[...] = acc

    return pl.pallas_call(
        body, name="small_grad_all_reduce",
        out_shape=[jax.ShapeDtypeStruct((N_DEV, rows, lanes), F32), jax.ShapeDtypeStruct((rows, lanes), F32)],
        in_specs=[vmem], out_specs=[vmem, vmem],
        scratch_shapes=[pltpu.SemaphoreType.DMA((7,)), pltpu.SemaphoreType.DMA((7,))],
    )(part)[1]


_SHARDED = (("w_in", (D_MODEL, IN_PROJ // N_DEV)), ("w_uq", (Q_RANK // N_DEV, Q_RANK)), ("w_ukv", (KV_RANK, HEAD_PAD)),
            ("conv_w", (CONV_W, CONV_DIM // N_DEV)), ("w_out", (D_MODEL // N_DEV, D_MODEL)),
            ("w_up", (D_MODEL, D_FF // N_DEV)), ("w_down", (D_FF // N_DEV, D_MODEL)))
_SMALL = (("pre_mix_norm", D_MODEL), ("q_norm", Q_RANK), ("kv_norm", KV_RANK), ("conv_b", CONV_DIM), ("dt_bias", SSD_HEADS),
          ("a_log", SSD_HEADS), ("d_skip", SSD_HEADS), ("ssd_norm", SSD_INNER), ("post_mix_norm", D_MODEL),
          ("pre_mlp_norm", D_MODEL), ("post_mlp_norm", D_MODEL))
_WEIGHT_ORDER = ("pre_mix_norm", "w_in", "q_norm", "w_uq", "kv_norm", "w_ukv", "conv_w", "conv_b", "dt_bias", "a_log", "d_skip",
                 "ssd_norm", "w_out", "post_mix_norm", "pre_mlp_norm", "w_up", "w_down", "post_mlp_norm")
_EARLY = ("w_in", "w_uq", "w_ukv", "conv_w")
_LATE = ("w_out", "w_up", "w_down")


def _wire_shard(name, a):
    return lax.bitcast_convert_type(a, BF16).reshape(CONV_W, -1) if name == "conv_w" else a.astype(BF16)


def _from_wire(name, g):
    return lax.bitcast_convert_type(g.reshape(N_DEV, CONV_W, -1, 2), F32) if name == "conv_w" else g


def _cols(stacked):
    return jnp.transpose(stacked, (1, 0, 2)).reshape(stacked.shape[1], -1)


def _win_segments():
    s2, s3, s5 = Q_RANK + KV_RANK, Q_RANK + KV_RANK + QK_ROPE, IN_PROJ - SSD_HEADS
    return [(0, s2), (None, MISC_ROPE), (s2, s3), (s5, IN_PROJ), (None, HEAD_PAD - MISC_DT - SSD_HEADS), (s3, s5)]


def _win_from_shards(stacked):
    per = IN_PROJ // N_DEV
    parts = []
    for start, stop in _win_segments():
        if start is None:
            parts.append(jnp.zeros((D_MODEL, stop), stacked.dtype))
            continue
        while start < stop:
            j, a = divmod(start, per)
            b = min(per, a + stop - start)
            parts.append(stacked[j, :, a:b])
            start += b - a
    return jnp.concatenate(parts, axis=1)


def _win_grad_shards(dwin):
    per = IN_PROJ // N_DEV
    runs, at = [], 0
    for start, stop in _win_segments():
        if start is not None:
            runs.append((start, stop, at))
        at += stop if start is None else stop - start
    blocks = []
    for j in range(N_DEV):
        lo, hi = j * per, (j + 1) * per
        parts = [dwin[:, p + max(lo, a) - a:p + min(hi, b) - a] for a, b, p in sorted(runs) if max(lo, a) < min(hi, b)]
        blocks.append(jnp.concatenate(parts, axis=1))
    return jnp.stack(blocks)


def _early_weights(sh):
    win = _win_from_shards(sh["w_in"])
    w_uq = sh["w_uq"].reshape(Q_RANK, MLA_HEADS, QK_NOPE + QK_ROPE)
    wuq = jnp.pad(w_uq, ((0, 0), (0, 0), (0, HEAD_PAD - QK_NOPE - QK_ROPE))).reshape(Q_RANK, -1)
    w_ukv = _cols(sh["w_ukv"]).reshape(KV_RANK, MLA_HEADS, QK_NOPE + V_DIM)
    wkn = jnp.pad(w_ukv[..., :QK_NOPE], ((0, 0), (0, 0), (0, HEAD_PAD - QK_NOPE))).reshape(KV_RANK, -1)
    wv = w_ukv[..., QK_NOPE:].reshape(KV_RANK, 4, 2, 1, V_DIM) * jnp.eye(2, dtype=BF16).reshape(1, 1, 2, 2, 1)
    wkv = jnp.concatenate([wkn, wv.reshape(KV_RANK, -1)], axis=1)
    return dict(win=win, wuq=wuq, wkv=wkv, conv_w=_cols(sh["conv_w"]))


def _late_weights(sh):
    w_out = sh["w_out"].reshape(D_MODEL, D_MODEL)
    watt = w_out[:SSD_INNER].reshape(4, 2, 1, V_DIM, D_MODEL) * jnp.eye(2, dtype=BF16).reshape(1, 2, 2, 1, 1)
    wout = jnp.concatenate([watt.reshape(MLA_HEADS * HEAD_PAD, D_MODEL), w_out[SSD_INNER:]], axis=0)
    return dict(wout=wout, wup=sh["w_up"], wdown=sh["w_down"])


def _shard_grads(g):
    out = {}
    if "wup" in g:
        out["w_up"], out["w_down"] = g["wup"], g["wdown"]
        ae = g["wout_att"].reshape(4, 2, 2, V_DIM, D_MODEL)
        att = jnp.stack([ae[:, 0, 0], ae[:, 1, 1]], axis=1).reshape(SSD_INNER, D_MODEL)
        out["w_out"] = jnp.concatenate([att, g["wout_ssd"]], axis=0).astype(BF16).reshape(N_DEV, D_MODEL // N_DEV, D_MODEL)
    if "win" not in g:
        return out
    out["w_in"] = _win_grad_shards(g["win"].astype(BF16))
    w_uq = g["wuq"].astype(BF16).reshape(Q_RANK, MLA_HEADS, HEAD_PAD)[..., :QK_NOPE + QK_ROPE].reshape(Q_RANK, Q_RANK)
    out["w_uq"] = w_uq.reshape(N_DEV, Q_RANK // N_DEV, Q_RANK)
    wide = MLA_HEADS * HEAD_PAD
    wkv = g["wkv"].astype(BF16)
    kn = wkv[:, :wide].reshape(KV_RANK, MLA_HEADS, HEAD_PAD)[..., :QK_NOPE]
    ve = wkv[:, wide:].reshape(KV_RANK, 4, 2, 2, V_DIM)
    vv = jnp.stack([ve[:, :, 0, 0], ve[:, :, 1, 1]], axis=2).reshape(KV_RANK, MLA_HEADS, V_DIM)
    out["w_ukv"] = jnp.transpose(jnp.concatenate([kn, vv], axis=-1), (1, 0, 2))
    out["conv_w"] = jnp.transpose(g["conv_w"].astype(BF16).reshape(CONV_W, N_DEV, -1), (1, 0, 2))
    return out


def _small_rows(n):
    return -(-n // 1024) * 8


def _pack_small(vals):
    rows = []
    for l in range(DEPTH):
        for name, n in _SMALL:
            r = _small_rows(n)
            rows.append(jnp.pad(vals[name][l].reshape(-1), (0, r * 128 - n)).reshape(r, 128))
    return jnp.concatenate(rows, axis=0)


def _unpack_small(packed):
    out, off = {name: [] for name, _ in _SMALL}, 0
    for l in range(DEPTH):
        for name, n in _SMALL:
            r = _small_rows(n)
            out[name].append(packed[off:off + r].reshape(-1)[:n])
            off += r
    return {name: jnp.stack(v) for name, v in out.items()}


def _lane_rows(vec8):
    return jnp.repeat(vec8, SSD_P).reshape(1, SSD_INNER)


def _layer_fwd(h, kw, sm, l, cosf, sinf, consts, gather=(), after_gather=None, target=None):
    row = lambda name: sm[name][l].reshape(1, -1)
    t = {}
    t["h0"] = h
    t["ub"], t["cq"], t["ckv"], t["misc"], t["z"], t["xraw"] = _inproj_fwd(h, row("pre_mix_norm"), kw["win"])
    t["cqn"], t["ckvn"], t["q"], t["k"], t["v"] = _qkv_fwd(t["cq"], t["ckv"], t["misc"], row("q_norm"), row("kv_norm"),
                                                         kw["wuq"], kw["wkv"], cosf, sinf)
    t["oe"], t["lse"], gathered = _attn_fwd(t["q"], t["k"], t["v"], gather)
    if after_gather is not None:
        after_gather(gathered)
    t["dtb"] = _lane_rows(sm["dt_bias"][l])
    t["a_exp"] = _lane_rows(-jnp.exp(sm["a_log"][l]))
    t["d_exp"] = _lane_rows(sm["d_skip"][l])
    t["c"], t["prev"], t["ypre"], t["yssd"] = _ssd_fwd(t["xraw"], t["misc"], t["z"], kw["conv_w"], row("conv_b"), t["dtb"],
                                                     t["a_exp"], t["d_exp"], row("ssd_norm"), consts)
    t["mixed"], t["h1"] = _outproj_fwd(t["oe"], t["yssd"], kw["wout"], h, row("post_mix_norm"))
    t["mb"], t["ab"], t["d"], *out = _mlp_fwd(t["h1"], row("pre_mlp_norm"), kw["wup"], kw["wdown"], row("post_mlp_norm"), target)
    return out, t


def _layer_bwd(dh2, t, kw, sm, l, cosf, sinf, consts, exchange_of=None):
    row = lambda name: sm[name][l].reshape(1, -1)
    g, gs = {}, {}
    dh1, dab, ddb, gs["post_mlp_norm"], gs["pre_mlp_norm"] = _mlp_bwd(
        dh2, t["d"], t["h1"], t["ab"], row("pre_mlp_norm"), kw["wup"], kw["wdown"], row("post_mlp_norm"))
    g["wup"] = _matmul_tn_stacked(t["mb"], dab, f"dw_up_{l}", a_stacked=False)
    g["wdown"] = _matmul_tn_stacked(t["ab"], ddb, f"dw_down_{l}", a_stacked=True, square_a=True)
    dmixb, doe, dyssd, gs["post_mix_norm"], delta = _outproj_bwd(dh1, t["mixed"], row("post_mix_norm"), kw["wout"], t["oe"])
    g["wout_att"] = _matmul_tn(t["oe"], dmixb, f"dw_out_att_{l}")
    g["wout_ssd"] = _matmul_tn(t["yssd"], dmixb, f"dw_out_ssd_{l}")
    dz, dxraw, dmisc_dt, gs["ssd_norm"], gd, galog, gdtb, g["conv_w"], gs["conv_b"] = _ssd_bwd(
        dyssd, t["ypre"], t["z"], t["c"], t["xraw"], t["misc"], t["prev"], kw["conv_w"], t["dtb"], t["a_exp"], t["d_exp"],
        row("ssd_norm"), consts)
    gs["d_skip"] = jnp.sum(gd.reshape(SSD_HEADS, SSD_P), axis=1)
    gs["a_log"] = galog[0, MISC_DT:MISC_DT + SSD_HEADS]
    gs["dt_bias"] = gdtb[0, MISC_DT:MISC_DT + SSD_HEADS]
    dq, dk, dv, exchanged = _attn_bwd(t["q"], t["k"], t["v"], doe, t["lse"], delta,
                                      exchange_of(g) if exchange_of is not None else ())
    dqb, dkvb, dcq, dckv, dmisc_rope, gs["q_norm"], gs["kv_norm"] = _qkv_bwd(
        dq, dk, dv, t["cq"], t["ckv"], row("q_norm"), row("kv_norm"), kw["wuq"], kw["wkv"], cosf, sinf)
    g["wuq"] = _matmul_tn(t["cqn"], dqb, f"dw_uq_{l}")
    g["wkv"] = _matmul_tn(t["ckvn"], dkvb, f"dw_kv_{l}")
    dprojb, dh0, gs["pre_mix_norm"] = _inproj_bwd(dcq, dckv, dmisc_rope, dmisc_dt, dz, dxraw, t["h0"], dh1,
                                                  row("pre_mix_norm"), kw["win"])
    g["win"] = _matmul_tn(t["ub"], dprojb, f"dw_in_{l}")
    return dh0, g, {k: v.reshape(-1) for k, v in gs.items()}, exchanged


def _local_step(x, positions, kws, sm, target, gather=(), after_gather=None, exchange_of=None):
    inv_freq = ROPE_THETA ** (-jnp.arange(0, QK_ROPE, 2, dtype=F32) / QK_ROPE)
    invf = jnp.zeros((HEAD_PAD,), F32).at[MISC_ROPE:MISC_ROPE + QK_ROPE].set(jnp.concatenate([inv_freq, inv_freq]))
    cosf, sinf = _rope_tables(positions.reshape(-1, 1), invf.reshape(1, HEAD_PAD))
    consts = _ssd_consts()
    (h,), t0 = _layer_fwd(x, kws[0], sm, 0, cosf, sinf, consts, gather, after_gather)
    (dh, loss), t1 = _layer_fwd(h, kws[1], sm, 1, cosf, sinf, consts, target=target)
    saved = [t0, t1]
    grads, small, exchanged = [None] * DEPTH, [None] * DEPTH, []
    for l in reversed(range(DEPTH)):
        hook = (lambda g0: exchange_of(g0, grads[1])) if (l == 0 and exchange_of is not None) else None
        dh, grads[l], small[l], got = _layer_bwd(dh, saved[l], kws[l], sm, l, cosf, sinf, consts, hook)
        exchanged = got or exchanged
    return loss[0, 0], dh, grads, small, exchanged


def kernel(x, positions, pre_mix_norm, w_in, q_norm, w_uq, kv_norm, w_ukv, conv_w, conv_b, dt_bias, a_log, d_skip, ssd_norm, w_out, post_mix_norm, pre_mlp_norm, w_up, w_down, post_mlp_norm, loss_target, m_pre_mix_norm, m_w_in, m_q_norm, m_w_uq, m_kv_norm, m_w_ukv, m_conv_w, m_conv_b, m_dt_bias, m_a_log, m_d_skip, m_ssd_norm, m_w_out, m_post_mix_norm, m_pre_mlp_norm, m_w_up, m_w_down, m_post_mlp_norm, v_pre_mix_norm, v_w_in, v_q_norm, v_w_uq, v_kv_norm, v_w_ukv, v_conv_w, v_conv_b, v_dt_bias, v_a_log, v_d_skip, v_ssd_norm, v_w_out, v_post_mix_norm, v_pre_mlp_norm, v_w_up, v_w_down, v_post_mlp_norm):
    w = dict(pre_mix_norm=pre_mix_norm, w_in=w_in, q_norm=q_norm, w_uq=w_uq, kv_norm=kv_norm, w_ukv=w_ukv, conv_w=conv_w,
             conv_b=conv_b, dt_bias=dt_bias, a_log=a_log, d_skip=d_skip, ssd_norm=ssd_norm, w_out=w_out,
             post_mix_norm=post_mix_norm, pre_mlp_norm=pre_mlp_norm, w_up=w_up, w_down=w_down, post_mlp_norm=post_mlp_norm)
    m = dict(pre_mix_norm=m_pre_mix_norm, w_in=m_w_in, q_norm=m_q_norm, w_uq=m_w_uq, kv_norm=m_kv_norm, w_ukv=m_w_ukv,
             conv_w=m_conv_w, conv_b=m_conv_b, dt_bias=m_dt_bias, a_log=m_a_log, d_skip=m_d_skip, ssd_norm=m_ssd_norm,
             w_out=m_w_out, post_mix_norm=m_post_mix_norm, pre_mlp_norm=m_pre_mlp_norm, w_up=m_w_up, w_down=m_w_down,
             post_mlp_norm=m_post_mlp_norm)
    v = dict(pre_mix_norm=v_pre_mix_norm, w_in=v_w_in, q_norm=v_q_norm, w_uq=v_w_uq, kv_norm=v_kv_norm, w_ukv=v_w_ukv,
             conv_w=v_conv_w, conv_b=v_conv_b, dt_bias=v_dt_bias, a_log=v_a_log, d_skip=v_d_skip, ssd_norm=v_ssd_norm,
             w_out=v_w_out, post_mix_norm=v_post_mix_norm, pre_mlp_norm=v_pre_mlp_norm, w_up=v_w_up, w_down=v_w_down,
             post_mlp_norm=v_post_mlp_norm)
    sm = {name: w[name] for name, _ in _SMALL}

    wire = lambda name, l: _wire_shard(name, w[name][l])
    first = _gather_two_level([wire(name, 0) for name in _EARLY], "weight_gather_first")
    kws = [_early_weights({name: _from_wire(name, a) for name, a in zip(_EARLY, first)}), None]
    behind = [(name, 0) for name in _LATE] + [(name, 1) for name, _ in _SHARDED]

    def after_gather(gathered):
        got = {key: _from_wire(key[0], a) for key, a in zip(behind, gathered)}
        kws[0].update(_late_weights({name: got[name, 0] for name in _LATE}))
        kws[1] = {**_early_weights({name: got[name, 1] for name in _EARLY}),
                  **_late_weights({name: got[name, 1] for name in _LATE})}

    sent_behind = [(name, 1) for name, _ in _SHARDED] + [(name, 0) for name in _LATE]

    def exchange_of(g0, g1):
        blocks = {**{(name, 1): a for name, a in _shard_grads(g1).items()},
                  **{(name, 0): a for name, a in _shard_grads(g0).items()}}
        return [blocks[key] for key in sent_behind]

    loss_part, dx, grads, small, exchanged = _local_step(
        x[0], positions[0], kws, sm, loss_target[0], [wire(*key) for key in behind], after_gather, exchange_of)
    slots = dict(zip(sent_behind, exchanged))
    last = _shard_grads({k: grads[0][k] for k in ("win", "wuq", "wkv", "conv_w")})
    slots.update({(name, 0): a for name, a in zip(_EARLY, _comm("exchange", [last[name] for name in _EARLY], "grad_exchange_last"))})
    g_small = _unpack_small(_all_reduce_small(_pack_small({name: jnp.stack([small[l][name] for l in range(DEPTH)])
                                                           for name, _ in _SMALL})))
    loss = lax.psum(loss_part, ("x", "y", "c"))

    grad, delta, new_m, new_v = {}, {}, {}, {}
    for name, _ in _SHARDED:
        grad[name], delta[name], new_m[name], new_v[name] = _sum_adamw(
            [slots[name, 0], slots[name, 1]], w[name], m[name], v[name], f"sum_adamw_{name}")
    pk = lambda d: _pack_small({name: d[name] for name, _ in _SMALL})
    d_, m_, v_ = _adamw(pk(w), pk(g_small), pk(m), pk(v), "adamw_small")
    for dst, packed in ((delta, d_), (new_m, m_), (new_v, v_)):
        dst.update(_unpack_small(packed))
    grad.update(g_small)

    outs = [loss, dx[None]]
    for d in (grad, delta, new_m, new_v):
        outs += [d[name] for name in _WEIGHT_ORDER]
    return tuple(outs)
```

```python
import jax
import jax.numpy as jnp
import numpy as np
from jax import lax
from jax.experimental import pallas as pl
from jax.experimental.pallas import tpu as pltpu

F32 = jnp.float32
BF16 = jnp.bfloat16

D_MODEL = 1024
DEPTH = 2
N_DEV = 8
CHUNK = 64
EPS = 1e-6
MLA_HEADS = 8
QK_NOPE = 64
QK_ROPE = 32
V_DIM = 64
Q_RANK = 768
KV_RANK = 256
ROPE_THETA = 10000.0
SSD_HEADS = 8
SSD_P = 64
SSD_INNER = 512
SSD_GROUPS = 2
SSD_N = 128
CONV_W = 4
CONV_DIM = 1024
D_FF = 4096
IN_PROJ = 2600
HEAD_PAD = 128
IN_PAD = 2688
MISC_ROPE = 64
MISC_DT = 96
ATT_SCALE = (QK_NOPE + QK_ROPE) ** -0.5
LOG2E = 1.4426950408889634
ATT_SCALE_LOG2 = ATT_SCALE * LOG2E

ADAM_LR = 0.001
ADAM_B1 = 0.9
ADAM_B2 = 0.999
ADAM_EPS = 1e-08
ADAM_WD = 0.01
ADAM_STEP = 10

TM = 512
ATT_T = 512
ATT_G = 8
ATT_UNROLL = 4
SSD_ROWS = 512
TK_DW = 4096
V7X_VMEM_BYTES = 64 * 1024 * 1024
VMEM_LIMIT = V7X_VMEM_BYTES - 8 * 1024 * 1024

_NT = (((1,), (1,)), ((), ()))
_TN = (((0,), (0,)), ((), ()))


def _params(**kw):
    return pltpu.CompilerParams(vmem_limit_bytes=VMEM_LIMIT, **kw)


def _dot(a, b, precision=None):
    return jnp.dot(a, b, preferred_element_type=F32, precision=precision)


def _dot_nt(a, b, precision=None):
    return lax.dot_general(a, b, _NT, preferred_element_type=F32, precision=precision)


def _dot_tn(a, b, precision=None):
    return lax.dot_general(a, b, _TN, preferred_element_type=F32, precision=precision)


def _split3(x):
    hi = x.astype(BF16)
    r = x - hi.astype(F32)
    mid = r.astype(BF16)
    return hi, mid, (r - mid.astype(F32)).astype(BF16)


def _dot01(x, m01, dot=_dot, left=False):
    parts = [dot(m01, p) if left else dot(p, m01) for p in _split3(x)]
    return parts[0] + parts[1] + parts[2]


def _full(shape):
    n = len(shape)
    return pl.BlockSpec(shape, lambda *_: (0,) * n)


def _resident(shape):
    n = len(shape)
    return pl.BlockSpec(shape, lambda *_: (0,) * n, pipeline_mode=pl.Buffered(1))


def _rows(tm, width):
    return pl.BlockSpec((tm, width), lambda i: (i, 0))


def _rms_fwd(x, w):
    r = lax.rsqrt(jnp.mean(x * x, axis=-1, keepdims=True) + EPS)
    return (x * r) * w


def _rms_bwd(x, w, dy):
    r = lax.rsqrt(jnp.mean(x * x, axis=-1, keepdims=True) + EPS)
    xh = x * r
    dxn = dy * w
    dx = r * (dxn - xh * jnp.mean(dxn * xh, axis=-1, keepdims=True))
    return dx, dy * xh


def _acc_rows(ref, val, first):
    s = jnp.sum(val, axis=0, keepdims=True)

    @pl.when(first)
    def _():
        ref[...] = s

    @pl.when(jnp.logical_not(first))
    def _():
        ref[...] += s


def _rope(t, cosf, sinf, sign):
    lane = lax.broadcasted_iota(jnp.int32, t.shape, 1)
    rot = jnp.where(lane < MISC_ROPE + QK_ROPE // 2, -pltpu.roll(t, HEAD_PAD - QK_ROPE // 2, 1), pltpu.roll(t, QK_ROPE // 2, 1))
    return t * cosf + sign * (rot * sinf)


def _rope_tables(pos, invf):
    s = pos.shape[0]

    def body(pos_ref, invf_ref, cos_ref, sin_ref):
        ang = pos_ref[...].astype(F32) * invf_ref[...]
        cos_ref[...] = jnp.cos(ang)
        sin_ref[...] = jnp.sin(ang)

    return pl.pallas_call(
        body, name="rope_tables", grid=(s // TM,),
        in_specs=[_rows(TM, 1), _full((1, HEAD_PAD))],
        out_specs=[_rows(TM, HEAD_PAD), _rows(TM, HEAD_PAD)],
        out_shape=[jax.ShapeDtypeStruct((s, HEAD_PAD), F32)] * 2,
    )(pos, invf)


def _inproj_fwd(h, nw, win):
    s = h.shape[0]

    def body(h_ref, nw_ref, w_ref, ub_ref, cq_ref, ckv_ref, misc_ref, z_ref, xbc_ref):
        ub = _rms_fwd(h_ref[...], nw_ref[...]).astype(BF16)
        ub_ref[...] = ub
        proj = _dot(ub, w_ref[...])
        cq_ref[...] = proj[:, 0:768]
        ckv_ref[...] = proj[:, 768:1024]
        misc_ref[...] = proj[:, 1024:1152]
        z_ref[...] = proj[:, 1152:1664]
        xbc_ref[...] = proj[:, 1664:2688]

    widths = (768, 256, 128, 512, 1024)
    return pl.pallas_call(
        body, name="inproj_fwd", grid=(s // TM,),
        in_specs=[_rows(TM, D_MODEL), _full((1, D_MODEL)), _resident((D_MODEL, IN_PAD))],
        out_specs=[_rows(TM, D_MODEL)] + [_rows(TM, w) for w in widths],
        out_shape=[jax.ShapeDtypeStruct((s, D_MODEL), BF16)] + [jax.ShapeDtypeStruct((s, w), F32) for w in widths],
        compiler_params=_params(),
    )(h, nw, win)


def _qkv_fwd(cq, ckv, misc, qnw, kvnw, wuq, wkv, cosf, sinf):
    s = cq.shape[0]

    def body(cq_ref, ckv_ref, misc_ref, qnw_ref, kvnw_ref, wuq_ref, wkv_ref, cos_ref, sin_ref,
             cqn_ref, ckvn_ref, q_ref, k_ref, v_ref):
        cosf, sinf = cos_ref[...], sin_ref[...]
        cqn = _rms_fwd(cq_ref[...], qnw_ref[...]).astype(BF16)
        cqn_ref[...] = cqn
        q = _dot(cqn, wuq_ref[...])
        ckvn = _rms_fwd(ckv_ref[...], kvnw_ref[...]).astype(BF16)
        ckvn_ref[...] = ckvn
        kv = _dot(ckvn, wkv_ref[...])
        m = misc_ref[...]
        lane = lax.broadcasted_iota(jnp.int32, m.shape, 1)
        in_rope = jnp.logical_and(lane >= MISC_ROPE, lane < MISC_ROPE + QK_ROPE)
        kr = jnp.where(in_rope, _rope(m, cosf, sinf, 1.0), 0.0)
        for hd in range(MLA_HEADS):
            cols = slice(hd * HEAD_PAD, (hd + 1) * HEAD_PAD)
            q_ref[:, cols] = _rope(q[:, cols], cosf, sinf, 1.0).astype(BF16)
            k_ref[:, cols] = (kv[:, cols] + kr).astype(BF16)
        vv = kv[:, MLA_HEADS * HEAD_PAD:]
        vlane = lax.broadcasted_iota(jnp.int32, vv.shape, 1)
        ones_at = jnp.where((vlane // HEAD_PAD) % 2 == 0, V_DIM, 0)
        v_ref[...] = jnp.where(vlane % HEAD_PAD == ones_at, 1.0, vv).astype(BF16)

    wide = MLA_HEADS * HEAD_PAD
    return pl.pallas_call(
        body, name="qkv_fwd", grid=(s // TM,),
        in_specs=[_rows(TM, Q_RANK), _rows(TM, KV_RANK), _rows(TM, HEAD_PAD), _full((1, Q_RANK)), _full((1, KV_RANK)),
                  _resident((Q_RANK, wide)), _resident((KV_RANK, 2 * wide)), _rows(TM, HEAD_PAD), _rows(TM, HEAD_PAD)],
        out_specs=[_rows(TM, Q_RANK), _rows(TM, KV_RANK), _rows(TM, wide), _rows(TM, wide), _rows(TM, wide)],
        out_shape=[jax.ShapeDtypeStruct((s, Q_RANK), BF16), jax.ShapeDtypeStruct((s, KV_RANK), BF16)]
        + [jax.ShapeDtypeStruct((s, wide), BF16)] * 3,
        compiler_params=_params(),
    )(cq, ckv, misc, qnw, kvnw, wuq, wkv, cosf, sinf)


def _chunk_bias(t, keys_on_rows=False):
    row = lax.broadcasted_iota(jnp.int32, (t, 1), 0) // CHUNK
    col = lax.broadcasted_iota(jnp.int32, (1, t), 1) // CHUNK
    return jnp.where((row <= col) if keys_on_rows else (col <= row), 0.0, -jnp.inf).astype(F32)


def _attn_fwd(q, k, v, gather=()):
    s = q.shape[0]
    t = ATT_T
    nq = s // t
    pair = ATT_G * HEAD_PAD
    ng = len(gather)

    def body(q_ref, k_ref, v_ref, *rest):
        g_in, (o_ref, lse_ref), g_out = rest[:ng], rest[ng:ng + 2], rest[ng + 2:2 * ng + 2]
        m_s, acc_s, bias_s = rest[2 * ng + 2:2 * ng + 5]
        qi = pl.program_id(1)
        group, groups = pl.program_id(0), MLA_HEADS // ATT_G

        @pl.when(jnp.logical_and(group == 0, qi == 0))
        def _():
            bias_s[...] = _chunk_bias(t)

        _hosted_gather(g_in, g_out, rest[2 * ng + 5:],
                       jnp.logical_and(group == 0, qi == 0),
                       jnp.logical_and(group == groups - 1, qi == min(3 * nq // 4 + 1, nq - 1)),
                       jnp.logical_and(group == groups - 1, qi == nq - 1))
        m_s[...] = jnp.full(m_s.shape, -jnp.inf, F32)
        acc_s[...] = jnp.zeros(acc_s.shape, F32)

        def step(kb, masked):
            r0 = pl.multiple_of(kb * t, t)

            def scores(hh):
                cols = slice(hh * HEAD_PAD, (hh + 1) * HEAD_PAD)
                return _dot_nt(q_ref[:, cols], k_ref[pl.ds(r0, t), cols])

            def soft(hh, raw):
                sc = raw * ATT_SCALE_LOG2
                if masked:
                    sc = sc + bias_s[...]
                m_old = m_s[hh]
                m_new = jnp.maximum(m_old, jnp.max(sc, axis=-1, keepdims=True))
                alpha = jnp.exp2(m_old - m_new)
                p = jnp.exp2(sc - jnp.tile(m_new, (1, t // HEAD_PAD)))
                m_s[hh] = m_new
                return alpha, p.astype(BF16)

            def update(hh, alpha, p):
                cols = slice(hh * HEAD_PAD, (hh + 1) * HEAD_PAD)
                acc_s[hh] = alpha * acc_s[hh] + _dot(p, v_ref[pl.ds(r0, t), cols])

            raw, ap = [None] * ATT_G, [None] * ATT_G
            raw[0] = scores(0)
            for hh in range(ATT_G):
                if hh + 1 < ATT_G:
                    raw[hh + 1] = scores(hh + 1)
                ap[hh] = soft(hh, raw[hh])
                if hh >= 1:
                    update(hh - 1, *ap[hh - 1])
            update(ATT_G - 1, *ap[ATT_G - 1])

        def loop(i, c):
            step(2 * i, False)
            step(2 * i + 1, False)
            return c

        lax.fori_loop(0, qi // 2, loop, 0)

        @pl.when(qi % 2 == 1)
        def _():
            step(qi - 1, False)

        step(qi, True)
        for hh in range(ATT_G):
            cols = slice(hh * HEAD_PAD, (hh + 1) * HEAD_PAD)
            acc = acc_s[hh]
            ones_at = V_DIM * (1 - hh % 2)
            l = jnp.broadcast_to(acc[:, ones_at:ones_at + 1], acc.shape)
            o_ref[:, cols] = (acc / l).astype(BF16)
            lse_ref[hh] = (m_s[hh] + jnp.log(l) * LOG2E).T[0:8, :]

    outs = pl.pallas_call(
        body, name="attn_fwd_gather" if ng else "attn_fwd", grid=(MLA_HEADS // ATT_G, nq),
        in_specs=[pl.BlockSpec((t, pair), lambda h, i: (i, h)),
                  pl.BlockSpec((s, pair), lambda h, i: (0, h), pipeline_mode=pl.Buffered(1)),
                  pl.BlockSpec((s, pair), lambda h, i: (0, h), pipeline_mode=pl.Buffered(1))] + [_ANY] * ng,
        out_specs=[pl.BlockSpec((t, pair), lambda h, i: (i, h)),
                   pl.BlockSpec((ATT_G, 8, t), lambda h, i: (h, 0, i))] + [_ANY] * ng,
        out_shape=[jax.ShapeDtypeStruct((s, MLA_HEADS * HEAD_PAD), BF16), jax.ShapeDtypeStruct((MLA_HEADS, 8, s), F32)]
        + _comm_out_shapes("gather", gather),
        scratch_shapes=[pltpu.VMEM((ATT_G, t, HEAD_PAD), F32), pltpu.VMEM((ATT_G, t, HEAD_PAD), F32), pltpu.VMEM((t, t), F32)]
        + (_comm_scratch(ng) if ng else []),
        compiler_params=_params(),
    )(q, k, v, *gather)
    return outs[0], outs[1], list(outs[2:])


def _interleave(stages):
    live = list(stages)
    while live:
        still = []
        for g in live:
            try:
                next(g)
                still.append(g)
            except StopIteration:
                pass
        live = still


def _ssd_consts():
    emisc = np.zeros((HEAD_PAD, SSD_INNER), np.float32)
    for hd in range(SSD_HEADS):
        emisc[MISC_DT + hd, hd * SSD_P:(hd + 1) * SSD_P] = 1.0
    idx = np.arange(CHUNK)
    tri = (idx[:, None] >= idx[None, :]).astype(np.float32)
    return tuple(jnp.asarray(m, BF16) for m in (emisc, emisc.T.copy(), tri, tri.T.copy()))


def _ssd_chunk_common(cc, misc, emisc, tri, trit, dtb, a_exp):
    sig = jax.nn.sigmoid(cc)
    xa = cc * sig
    dt = jax.nn.softplus(_dot01(misc, emisc) + dtb)
    a = dt * a_exp
    acs = _dot01(a, tri, left=True)
    acs_t = _dot01(a, trit, dot=_dot_tn)
    alast = acs[CHUNK - 1:CHUNK, :]
    return xa, sig, dt, acs, acs_t, alast


def _decay(acs, acs_t, hd):
    row = lax.broadcasted_iota(jnp.int32, (CHUNK, CHUNK), 0)
    col = lax.broadcasted_iota(jnp.int32, (CHUNK, CHUNK), 1)
    diff = acs[:, hd * SSD_P:hd * SSD_P + 1] - acs_t[hd * SSD_P:hd * SSD_P + 1, :]
    return jnp.exp(jnp.where(row >= col, diff, -jnp.inf))


def _half_mask(hh):
    lane = lax.broadcasted_iota(jnp.int32, (CHUNK, 2 * SSD_P), 1)
    return (lane >= SSD_P) if hh else (lane < SSD_P)


def _gate_norm(y, zz):
    sg = jax.nn.sigmoid(zz)
    yz = y * (zz * sg)
    outs, rs = [], []
    half = SSD_INNER // SSD_GROUPS
    for g in range(SSD_GROUPS):
        yg = yz[:, g * half:(g + 1) * half]
        r = lax.rsqrt(jnp.mean(yg * yg, axis=-1, keepdims=True) + EPS)
        outs.append(yg * r)
        rs.append(r)
    return sg, jnp.concatenate(outs, axis=1), rs


def _ssd_fwd(xraw, misc, z, cw, cb, dtb, a_exp, d_exp, nw, consts):
    s = xraw.shape[0]
    nb = s // SSD_ROWS
    ncb = SSD_ROWS // CHUNK
    emisc, _, tri, trit = consts

    def body(x_ref, misc_ref, z_ref, cw_ref, cb_ref, dtb_ref, a_ref, d_ref, nw_ref, emisc_ref, tri_ref, trit_ref,
             c_ref, prev_ref, ypre_ref, yssd_ref, tail_s, state_s):
        i = pl.program_id(0)

        @pl.when(i == 0)
        def _():
            tail_s[...] = jnp.zeros(tail_s.shape, F32)
            state_s[...] = jnp.zeros(state_s.shape, F32)

        x = x_ref[...]
        xext = jnp.concatenate([tail_s[...], x], axis=0)
        acc = x * cw_ref[CONV_W - 1:CONV_W, :] + cb_ref[...]
        for j in range(1, CONV_W):
            acc = acc + pltpu.roll(xext, j, 0)[8:, :] * cw_ref[CONV_W - 1 - j:CONV_W - j, :]
        tail_s[...] = x[SSD_ROWS - 8:, :]
        c_ref[...] = acc

        def chunk(ci):
            r0 = ci * CHUNK
            xa, _, dt, acs, acs_t, alast = _ssd_chunk_common(
                c_ref[pl.ds(r0, CHUNK), :], misc_ref[pl.ds(r0, CHUNK), :], emisc_ref[...], tri_ref[...], trit_ref[...],
                dtb_ref[...], a_ref[...])
            yield
            xs = xa[:, :SSD_INNER]
            xdt = xs * dt
            wgt = (xdt * jnp.exp(alast - acs)).astype(BF16)
            e = jnp.exp(acs)
            ys, new_states, cms = [], [], []
            for g in range(SSD_GROUPS):
                bm = xa[:, SSD_INNER + g * SSD_N:SSD_INNER + (g + 1) * SSD_N].astype(BF16)
                cm = xa[:, SSD_INNER + SSD_GROUPS * SSD_N + g * SSD_N:SSD_INNER + SSD_GROUPS * SSD_N + (g + 1) * SSD_N].astype(BF16)
                cms.append(cm)
                cb_g = _dot_nt(cm, bm)
                gl = slice(g * 256, (g + 1) * 256)
                new_states.append(_dot_tn(bm, wgt[:, gl]))
                for jj in range(2):
                    pair = 2 * g + jj
                    xp = xdt[:, pair * 128:(pair + 1) * 128]
                    yp = None
                    for hh in range(2):
                        sc = (cb_g * _decay(acs, acs_t, 2 * pair + hh)).astype(BF16)
                        term = _dot(sc, jnp.where(_half_mask(hh), xp, 0.0).astype(BF16))
                        yp = term if yp is None else yp + term
                    ys.append(yp)
                yield
            prev = state_s[...]
            prev_ref[ci] = prev
            yoff = jnp.concatenate([_dot(cms[g], prev[:, g * 256:(g + 1) * 256].astype(BF16)) for g in range(SSD_GROUPS)],
                                   axis=1) * e
            state_s[...] = prev * jnp.exp(alast) + jnp.concatenate(new_states, axis=1)
            yield
            y = jnp.concatenate(ys, axis=1) + yoff + d_ref[...] * xs
            ypre_ref[pl.ds(r0, CHUNK), :] = y
            _, yn, _ = _gate_norm(y, z_ref[pl.ds(r0, CHUNK), :])
            yssd_ref[pl.ds(r0, CHUNK), :] = (yn * nw_ref[...]).astype(BF16)

        _interleave([chunk(ci) for ci in range(ncb)])

    return pl.pallas_call(
        body, name="ssd_fwd", grid=(nb,),
        in_specs=[_rows(SSD_ROWS, CONV_DIM), _rows(SSD_ROWS, HEAD_PAD), _rows(SSD_ROWS, SSD_INNER),
                  _full((CONV_W, CONV_DIM)), _full((1, CONV_DIM)), _full((1, SSD_INNER)), _full((1, SSD_INNER)),
                  _full((1, SSD_INNER)), _full((1, SSD_INNER)), _full((HEAD_PAD, SSD_INNER)), _full((CHUNK, CHUNK)),
                  _full((CHUNK, CHUNK))],
        out_specs=[_rows(SSD_ROWS, CONV_DIM), pl.BlockSpec((ncb, SSD_N, SSD_INNER), lambda i: (i, 0, 0)),
                   _rows(SSD_ROWS, SSD_INNER), _rows(SSD_ROWS, SSD_INNER)],
        out_shape=[jax.ShapeDtypeStruct((s, CONV_DIM), F32), jax.ShapeDtypeStruct((s // CHUNK, SSD_N, SSD_INNER), F32),
                   jax.ShapeDtypeStruct((s, SSD_INNER), F32), jax.ShapeDtypeStruct((s, SSD_INNER), BF16)],
        scratch_shapes=[pltpu.VMEM((8, CONV_DIM), F32), pltpu.VMEM((SSD_N, SSD_INNER), F32)],
        compiler_params=_params(),
    )(xraw, misc, z, cw, cb, dtb, a_exp, d_exp, nw, emisc, tri, trit)


def _outproj_fwd(oe, yssd, wout, h, nw):
    s = h.shape[0]
    wide = MLA_HEADS * HEAD_PAD

    def body(oe_ref, y_ref, w_ref, h_ref, nw_ref, mixed_ref, h1_ref):
        mixed = _dot(oe_ref[...], w_ref[0:wide, :]) + _dot(y_ref[...], w_ref[wide:, :])
        mixed_ref[...] = mixed
        h1_ref[...] = h_ref[...] + _rms_fwd(mixed, nw_ref[...])

    return pl.pallas_call(
        body, name="outproj_fwd", grid=(s // TM,),
        in_specs=[_rows(TM, wide), _rows(TM, SSD_INNER), _resident((wide + SSD_INNER, D_MODEL)), _rows(TM, D_MODEL),
                  _full((1, D_MODEL))],
        out_specs=[_rows(TM, D_MODEL), _rows(TM, D_MODEL)],
        out_shape=[jax.ShapeDtypeStruct((s, D_MODEL), F32)] * 2,
        compiler_params=_params(),
    )(oe, yssd, wout, h, nw)


def _mlp_fwd(h1, prew, wup, wdown, postw, target=None):
    s = h1.shape[0]
    fb = D_FF // N_DEV
    last = target is not None

    def body(h_ref, prew_ref, up_ref, down_ref, postw_ref, *rest):
        target_ref, (mb_ref, ab_ref, d_ref, out_ref) = (rest[0] if last else None), rest[last:last + 4]
        hh = h_ref[...]
        mb = _rms_fwd(hh, prew_ref[...]).astype(BF16)
        mb_ref[...] = mb
        d = jnp.zeros((TM, D_MODEL), F32)
        for j in range(N_DEV):
            a = jnp.maximum(_dot(mb, up_ref[j]), 0.0)
            ab_ref[j] = a.astype(BF16)
            d = d + _dot(jnp.square(a).astype(BF16), down_ref[j])
        d_ref[...] = d
        h2 = hh + _rms_fwd(d, postw_ref[...])
        if last:
            diff = h2 - target_ref[...]
            out_ref[...] = diff * (1.0 / D_MODEL)
            part = 0.5 * jnp.sum(jnp.mean(diff * diff, axis=-1, keepdims=True), axis=0, keepdims=True)
            _acc_rows(rest[-1], part, pl.program_id(0) == 0)
        else:
            out_ref[...] = h2

    stacked = pl.BlockSpec((N_DEV, TM, fb), lambda i: (0, i, 0))
    return pl.pallas_call(
        body, name="mlp_fwd_loss" if last else "mlp_fwd", grid=(s // TM,),
        in_specs=[_rows(TM, D_MODEL), _full((1, D_MODEL)), _resident((N_DEV, D_MODEL, fb)), _resident((N_DEV, fb, D_MODEL)),
                  _full((1, D_MODEL))] + ([_rows(TM, D_MODEL)] if last else []),
        out_specs=[_rows(TM, D_MODEL), stacked, _rows(TM, D_MODEL), _rows(TM, D_MODEL)] + ([_full((1, 1))] if last else []),
        out_shape=[jax.ShapeDtypeStruct((s, D_MODEL), BF16), jax.ShapeDtypeStruct((N_DEV, s, fb), BF16),
                   jax.ShapeDtypeStruct((s, D_MODEL), F32), jax.ShapeDtypeStruct((s, D_MODEL), F32)]
        + ([jax.ShapeDtypeStruct((1, 1), F32)] if last else []),
        compiler_params=_params(),
    )(h1, prew, wup, wdown, postw, *([target] if last else []))


def _mlp_bwd(dh2, d, h1, ab, prew, wup, wdown, postw):
    s = dh2.shape[0]
    fb = D_FF // N_DEV
    tm = TM // 2

    def body(dh2_ref, d_ref, h1_ref, ab_ref, prew_ref, up_ref, down_ref, postw_ref,
             dh1_ref, da_ref, dd_ref, gpost_ref, gpre_ref):
        first = pl.program_id(0) == 0
        dh2 = dh2_ref[...]
        dd, gpost = _rms_bwd(d_ref[...], postw_ref[...], dh2)
        _acc_rows(gpost_ref, gpost, first)
        ddb = dd.astype(BF16)
        dd_ref[...] = ddb

        def d_relu_squared(j):
            return _dot_nt(ddb, down_ref[j])

        def pointwise(j, dr):
            da = (dr * (2.0 * ab_ref[j].astype(F32))).astype(BF16)
            da_ref[j] = da
            return da

        dm = jnp.zeros((tm, D_MODEL), F32)
        nxt, da_prev = d_relu_squared(0), None
        for j in range(N_DEV):
            cur = nxt
            if j + 1 < N_DEV:
                nxt = d_relu_squared(j + 1)
            da = pointwise(j, cur)
            if da_prev is not None:
                dm = dm + _dot_nt(da_prev, up_ref[j - 1])
            da_prev = da
        dm = dm + _dot_nt(da_prev, up_ref[N_DEV - 1])
        dx, gpre = _rms_bwd(h1_ref[...], prew_ref[...], dm)
        _acc_rows(gpre_ref, gpre, first)
        dh1_ref[...] = dh2 + dx

    stacked = pl.BlockSpec((N_DEV, tm, fb), lambda i: (0, i, 0))
    return pl.pallas_call(
        body, name="mlp_bwd", grid=(s // tm,),
        in_specs=[_rows(tm, D_MODEL)] * 3 + [stacked, _full((1, D_MODEL)), _resident((N_DEV, D_MODEL, fb)),
                                              _resident((N_DEV, fb, D_MODEL)), _full((1, D_MODEL))],
        out_specs=[_rows(tm, D_MODEL), stacked, _rows(tm, D_MODEL), _full((1, D_MODEL)), _full((1, D_MODEL))],
        out_shape=[jax.ShapeDtypeStruct((s, D_MODEL), F32), jax.ShapeDtypeStruct((N_DEV, s, fb), BF16),
                   jax.ShapeDtypeStruct((s, D_MODEL), BF16), jax.ShapeDtypeStruct((1, D_MODEL), F32),
                   jax.ShapeDtypeStruct((1, D_MODEL), F32)],
        compiler_params=_params(),
    )(dh2, d, h1, ab, prew, wup, wdown, postw)


def _matmul_tn(a, b, name, tk=TK_DW):
    s, m = a.shape
    n = b.shape[1]
    tn = n if n <= 1024 else (n // 2 if (n // 2) % 128 == 0 else n // 3)
    tk = min(tk, s)
    assert n % tn == 0 and tn % 128 == 0 and s % tk == 0

    def body(a_ref, b_ref, o_ref):
        part = _dot_tn(a_ref[...], b_ref[...])

        @pl.when(pl.program_id(1) == 0)
        def _():
            o_ref[...] = part

        @pl.when(pl.program_id(1) != 0)
        def _():
            o_ref[...] += part

    return pl.pallas_call(
        body, name=name, grid=(n // tn, s // tk),
        in_specs=[pl.BlockSpec((tk, m), lambda j, k: (k, 0)), pl.BlockSpec((tk, tn), lambda j, k: (k, j))],
        out_specs=pl.BlockSpec((m, tn), lambda j, k: (0, j)),
        out_shape=jax.ShapeDtypeStruct((m, n), F32),
        compiler_params=_params(),
    )(a, b)


def _matmul_tn_stacked(a, b, name, a_stacked, square_a=False, tk=TK_DW):
    tk = min(tk, a.shape[-2])
    if a_stacked:
        _, s, m = a.shape
        n = b.shape[1]
        in_specs = [pl.BlockSpec((1, tk, m), lambda j, k: (j, k, 0)), pl.BlockSpec((tk, n), lambda j, k: (k, 0))]
    else:
        s, m = a.shape
        n = b.shape[2]
        in_specs = [pl.BlockSpec((tk, m), lambda j, k: (k, 0)), pl.BlockSpec((1, tk, n), lambda j, k: (j, k, 0))]

    nk = s // tk

    def body(a_ref, b_ref, o_ref, acc_s):
        av = a_ref[0] if a_stacked else a_ref[...]
        bv = b_ref[...] if a_stacked else b_ref[0]
        if square_a:
            av = jnp.square(av.astype(F32)).astype(BF16)
        part = _dot_tn(av, bv)
        k = pl.program_id(1)

        @pl.when(k == 0)
        def _():
            acc_s[...] = part

        @pl.when(jnp.logical_and(k != 0, k != nk - 1))
        def _():
            acc_s[...] += part

        @pl.when(k == nk - 1)
        def _():
            o_ref[0] = (part if nk == 1 else acc_s[...] + part).astype(BF16)

    return pl.pallas_call(
        body, name=name, grid=(N_DEV, nk),
        in_specs=in_specs,
        out_specs=pl.BlockSpec((1, m, n), lambda j, k: (j, 0, 0)),
        out_shape=jax.ShapeDtypeStruct((N_DEV, m, n), BF16),
        scratch_shapes=[pltpu.VMEM((m, n), F32)],
        compiler_params=_params(),
    )(a, b)


def _outproj_bwd(dh1, mixed, nw, wout, oe):
    s = dh1.shape[0]
    wide = MLA_HEADS * HEAD_PAD

    def body(dh1_ref, mixed_ref, nw_ref, w_ref, oe_ref, dmix_ref, doe_ref, dy_ref, gnw_ref, delta_ref):
        dmix, gnw = _rms_bwd(mixed_ref[...], nw_ref[...], dh1_ref[...])
        _acc_rows(gnw_ref, gnw, pl.program_id(0) == 0)
        dmb = dmix.astype(BF16)
        dmix_ref[...] = dmb
        doe_ref[...] = _dot_nt(dmb, w_ref[0:wide, :]).astype(BF16)
        dy_ref[...] = _dot_nt(dmb, w_ref[wide:, :])
        ones = jnp.ones((8, HEAD_PAD), BF16)
        for hd in range(MLA_HEADS):
            cols = slice(hd * HEAD_PAD, (hd + 1) * HEAD_PAD)
            prod = oe_ref[:, cols].astype(F32) * doe_ref[:, cols].astype(F32)
            delta_ref[hd] = _dot01(prod, ones, dot=_dot_nt, left=True)

    return pl.pallas_call(
        body, name="outproj_bwd", grid=(s // TM,),
        in_specs=[_rows(TM, D_MODEL), _rows(TM, D_MODEL), _full((1, D_MODEL)), _resident((wide + SSD_INNER, D_MODEL)),
                  _rows(TM, wide)],
        out_specs=[_rows(TM, D_MODEL), _rows(TM, wide), _rows(TM, SSD_INNER), _full((1, D_MODEL)),
                   pl.BlockSpec((MLA_HEADS, 8, TM), lambda i: (0, 0, i))],
        out_shape=[jax.ShapeDtypeStruct((s, D_MODEL), BF16), jax.ShapeDtypeStruct((s, wide), BF16),
                   jax.ShapeDtypeStruct((s, SSD_INNER), F32), jax.ShapeDtypeStruct((1, D_MODEL), F32),
                   jax.ShapeDtypeStruct((MLA_HEADS, 8, s), F32)],
        compiler_params=_params(),
    )(dh1, mixed, nw, wout, oe)


def _attn_bwd(q, k, v, do, lse, delta, exchange=()):
    s = q.shape[0]
    t = ATT_T
    nq = s // t
    pair = 2 * HEAD_PAD
    ne = len(exchange)

    def body(q_ref, k_ref, v_ref, do_ref, lse_ref, delta_ref, *rest):
        e_in, (dq_ref, dk_ref, dv_ref), e_out = rest[:ne], rest[ne:ne + 3], rest[ne + 3:2 * ne + 3]
        dk_s, dv_s, bias_s = rest[2 * ne + 3:2 * ne + 6]
        kb = pl.program_id(1)
        _hosted_comm("exchange", e_in, e_out, rest[2 * ne + 6:],
                     jnp.logical_and(pl.program_id(0) == 0, kb == 0),
                     jnp.logical_and(pl.program_id(0) == MLA_HEADS // 2 - 1, kb == nq - 1))

        @pl.when(jnp.logical_and(pl.program_id(0) == 0, kb == 0))
        def _():
            bias_s[...] = _chunk_bias(t, keys_on_rows=True)

        @pl.when(kb == 0)
        def _():
            dq_ref[...] = jnp.zeros(dq_ref.shape, F32)

        def step(qb, diagonal):
            r0 = pl.multiple_of(qb * t, t)
            for hh in range(2):
                cols = slice(hh * HEAD_PAD, (hh + 1) * HEAD_PAD)
                kk = k_ref[:, cols]
                qq = q_ref[pl.ds(r0, t), cols]
                dd = do_ref[pl.ds(r0, t), cols]
                sc = _dot_nt(kk, qq) * ATT_SCALE_LOG2
                if diagonal:
                    sc = sc + bias_s[...]
                p = jnp.exp2(sc - lse_ref[hh, 0:1, pl.ds(r0, t)])
                dv = _dot(p.astype(BF16), dd)
                dp = _dot_nt(v_ref[:, cols], dd)
                ds = (p * (dp - delta_ref[hh, 0:1, pl.ds(r0, t)]) * ATT_SCALE).astype(BF16)
                dk = _dot(ds, qq)
                if diagonal:
                    dv_s[:, cols] = dv
                    dk_s[:, cols] = dk
                else:
                    dv_s[:, cols] += dv
                    dk_s[:, cols] += dk
                dq_ref[pl.ds(r0, t), cols] += _dot_tn(ds, kk)

        def loop(i, c):
            for u in range(ATT_UNROLL):
                step(kb + 1 + u + ATT_UNROLL * i, False)
            return c

        step(kb, True)
        later_tiles = nq - 1 - kb
        lax.fori_loop(0, later_tiles // ATT_UNROLL, loop, 0)
        left = later_tiles % ATT_UNROLL
        for u in range(ATT_UNROLL - 1):
            @pl.when(left > u)
            def _(u=u):
                step(nq - left + u, False)

        dk_ref[...] = dk_s[...].astype(BF16)
        dv_ref[...] = dv_s[...].astype(BF16)

    whole = pl.BlockSpec((s, pair), lambda h, i: (0, h))
    tile = pl.BlockSpec((t, pair), lambda h, i: (i, h))
    rowvec = pl.BlockSpec((2, 8, s), lambda h, i: (h, 0, 0))
    wide = MLA_HEADS * HEAD_PAD
    outs = pl.pallas_call(
        body, name="attn_bwd_exchange" if ne else "attn_bwd", grid=(MLA_HEADS // 2, nq),
        in_specs=[whole, tile, tile, whole, rowvec, rowvec] + [_ANY] * ne,
        out_specs=[whole, tile, tile] + [_ANY] * ne,
        out_shape=[jax.ShapeDtypeStruct((s, wide), F32)] + [jax.ShapeDtypeStruct((s, wide), BF16)] * 2
        + _comm_out_shapes("exchange", exchange),
        scratch_shapes=[pltpu.VMEM((t, pair), F32), pltpu.VMEM((t, pair), F32), pltpu.VMEM((t, t), F32)]
        + (_comm_scratch(ne) if ne else []),
        compiler_params=_params(),
    )(q, k, v, do, lse, delta, *exchange)
    return outs[0], outs[1], outs[2], list(outs[3:])


def _ssd_bwd(dy, ypre, z, c, xraw, misc, prev, cw, dtb, a_exp, d_exp, nw, consts):
    s = dy.shape[0]
    nb = s // SSD_ROWS
    ncb = SSD_ROWS // CHUNK
    emisc, emisc_t, tri, trit = consts

    def body(dy_ref, ypre_ref, z_ref, c_ref, x_ref, misc_ref, prev_ref, cw_ref, dtb_ref, a_ref, d_ref, nw_ref,
             emisc_ref, emisct_ref, tri_ref, trit_ref,
             dz_ref, dx_ref, dmisc_ref, gnw_ref, gd_ref, galog_ref, gdtb_ref, gcw_ref, gcb_ref,
             dst_s, dc_s, head_s):
        i = pl.program_id(0)
        first = i == 0

        @pl.when(first)
        def _():
            dst_s[...] = jnp.zeros(dst_s.shape, F32)
            head_s[...] = jnp.zeros(head_s.shape, F32)
            gnw_ref[...] = jnp.zeros(gnw_ref.shape, F32)
            gd_ref[...] = jnp.zeros(gd_ref.shape, F32)
            galog_ref[...] = jnp.zeros(galog_ref.shape, F32)
            gdtb_ref[...] = jnp.zeros(gdtb_ref.shape, F32)

        a_exp_v = a_ref[...]
        a8 = _dot01(a_exp_v, emisct_ref[...]) * (1.0 / SSD_P)

        def chunk(ci):
            r0 = ci * CHUNK
            cc = c_ref[pl.ds(r0, CHUNK), :]
            mm = misc_ref[pl.ds(r0, CHUNK), :]
            xa, sig_c, dt, acs, acs_t, alast = _ssd_chunk_common(cc, mm, emisc_ref[...], tri_ref[...], trit_ref[...],
                                                              dtb_ref[...], a_exp_v)
            yield
            xs = xa[:, :SSD_INNER]
            xdt = xs * dt
            y = ypre_ref[pl.ds(r0, CHUNK), :]
            zz = z_ref[pl.ds(r0, CHUNK), :]
            sg, yn, rs = _gate_norm(y, zz)
            dyo = dy_ref[pl.ds(r0, CHUNK), :]
            gnw_ref[...] += jnp.sum(dyo * yn, axis=0, keepdims=True)
            dyn = dyo * nw_ref[...]
            half = SSD_INNER // SSD_GROUPS
            dyz_parts = []
            for g in range(SSD_GROUPS):
                gl = slice(g * half, (g + 1) * half)
                dyz_parts.append(rs[g] * (dyn[:, gl] - yn[:, gl] * jnp.mean(dyn[:, gl] * yn[:, gl], axis=-1, keepdims=True)))
            dyz = jnp.concatenate(dyz_parts, axis=1)
            dz_ref[pl.ds(r0, CHUNK), :] = dyz * y * (sg * (1.0 + zz * (1.0 - sg)))
            dyp = dyz * (zz * sg)
            dypb = dyp.astype(BF16)
            gd_ref[...] += jnp.sum(dyp * xs, axis=0, keepdims=True)
            yield
            prev = prev_ref[ci]
            cd = jnp.exp(alast)
            e = jnp.exp(acs)
            dsx = jnp.exp(alast - acs)
            wgt = (xdt * dsx).astype(BF16)
            dze = (dyp * e).astype(BF16)
            dprev_parts, diag_all, dbm, dcm, yoff_parts, bms = [], [], [], [], [], []
            lane8 = lax.broadcasted_iota(jnp.int32, (CHUNK, HEAD_PAD), 1)
            diag8 = jnp.zeros((CHUNK, HEAD_PAD), F32)
            for g in range(SSD_GROUPS):
                gl = slice(g * 256, (g + 1) * 256)
                bm = xa[:, SSD_INNER + g * SSD_N:SSD_INNER + (g + 1) * SSD_N].astype(BF16)
                cm = xa[:, SSD_INNER + SSD_GROUPS * SSD_N + g * SSD_N:SSD_INNER + SSD_GROUPS * SSD_N + (g + 1) * SSD_N].astype(BF16)
                bms.append(bm)
                prev_g = prev[:, gl].astype(BF16)
                dcm_g = _dot_nt(dze[:, gl], prev_g)
                dprev_parts.append(_dot_tn(cm, dze[:, gl]))
                cb_g = _dot_nt(cm, bm)
                dcb = jnp.zeros((CHUNK, CHUNK), F32)
                diag_parts = []
                for jj in range(2):
                    pair = 2 * g + jj
                    pl_ = slice(pair * 128, (pair + 1) * 128)
                    xp = xdt[:, pl_]
                    dyp_p = dypb[:, pl_]
                    dxp = jnp.zeros((CHUNK, 128), F32)
                    for hh in range(2):
                        hd = 2 * pair + hh
                        dec = _decay(acs, acs_t, hd)
                        xm = jnp.where(_half_mask(hh), xp, 0.0).astype(BF16)
                        dsc = _dot_nt(dyp_p, xm) * dec
                        dcb = dcb + dsc
                        sc = (cb_g * dec).astype(BF16)
                        dxp = dxp + jnp.where(_half_mask(hh), _dot_tn(sc, dyp_p), 0.0)
                        dm = dsc * cb_g
                        diag8 = diag8 + jnp.where(lane8 == MISC_DT + hd, jnp.sum(dm - dm.T, axis=1, keepdims=True), 0.0)
                    diag_parts.append(dxp)
                dcbb = dcb.astype(BF16)
                dcm.append(dcm_g + _dot(dcbb, bm))
                dbm.append(_dot_tn(dcbb, cm))
                diag_all.append(jnp.concatenate(diag_parts, axis=1))
                yoff_parts.append(_dot(cm, prev_g) * e[:, gl])
                yield
            dst = dst_s[...]
            glast = jnp.sum(dst * prev, axis=0, keepdims=True) * cd
            dxdt_state_parts = []
            for g in range(SSD_GROUPS):
                gl = slice(g * 256, (g + 1) * 256)
                dst_g = dst[:, gl].astype(BF16)
                dxdt_state_parts.append(_dot(bms[g], dst_g) * dsx[:, gl])
                dbm[g] = dbm[g] + _dot_nt(wgt[:, gl], dst_g)
            dst_s[...] = dst * cd + jnp.concatenate(dprev_parts, axis=1)
            yield
            dxdt_state = jnp.concatenate(dxdt_state_parts, axis=1)
            dxdt = jnp.concatenate(diag_all, axis=1) + dxdt_state
            dacs = dyp * jnp.concatenate(yoff_parts, axis=1) - xdt * dxdt_state
            last = jnp.sum(xdt * dxdt_state, axis=0, keepdims=True) + glast
            row = lax.broadcasted_iota(jnp.int32, (CHUNK, SSD_INNER), 0)
            dacs = dacs + jnp.where(row == CHUNK - 1, last, 0.0)
            dacs8 = _dot01(dacs, emisct_ref[...]) + diag8
            da8 = _dot01(dacs8, trit_ref[...], left=True)
            ddt8 = da8 * a8 + _dot01(dxdt * xs, emisct_ref[...])
            yield
            dtr8 = mm + _dot01(dtb_ref[...], emisct_ref[...]) * (1.0 / SSD_P)
            dt8 = jax.nn.softplus(dtr8)
            lane = lax.broadcasted_iota(jnp.int32, (CHUNK, HEAD_PAD), 1)
            on_dt = jnp.logical_and(lane >= MISC_DT, lane < MISC_DT + SSD_HEADS)
            ddtr8 = jnp.where(on_dt, ddt8 * jax.nn.sigmoid(dtr8), 0.0)
            dmisc_ref[pl.ds(r0, CHUNK), :] = ddtr8
            gdtb_ref[...] += jnp.sum(ddtr8, axis=0, keepdims=True)
            galog_ref[...] += jnp.sum(jnp.where(on_dt, da8 * dt8, 0.0), axis=0, keepdims=True) * a8
            dxs = d_ref[...] * dyp + dxdt * dt
            dxa = jnp.concatenate([dxs] + dbm + dcm, axis=1)
            dc_s[pl.ds(r0, CHUNK), :] = dxa * (sig_c * (1.0 + cc * (1.0 - sig_c)))

        _interleave([chunk(ci) for ci in reversed(range(ncb))])

        dc = dc_s[...]
        x = x_ref[...]
        dcext = jnp.concatenate([dc, head_s[...]], axis=0)
        dx = dc * cw_ref[CONV_W - 1:CONV_W, :]
        rows = [jnp.sum(dc * x, axis=0, keepdims=True)]
        for j in range(1, CONV_W):
            ahead = pltpu.roll(dcext, SSD_ROWS + 8 - j, 0)[:SSD_ROWS, :]
            dx = dx + ahead * cw_ref[CONV_W - 1 - j:CONV_W - j, :]
            rows.insert(0, jnp.sum(ahead * x, axis=0, keepdims=True))
        dx_ref[...] = dx
        head_s[...] = dc[:8, :]
        gcw = jnp.concatenate(rows, axis=0)

        @pl.when(first)
        def _():
            gcw_ref[...] = gcw
            gcb_ref[...] = jnp.sum(dc, axis=0, keepdims=True)

        @pl.when(jnp.logical_not(first))
        def _():
            gcw_ref[...] += gcw
            gcb_ref[...] += jnp.sum(dc, axis=0, keepdims=True)

    def rev(width):
        return pl.BlockSpec((SSD_ROWS, width), lambda i: (nb - 1 - i, 0))

    return pl.pallas_call(
        body, name="ssd_bwd", grid=(nb,),
        in_specs=[rev(SSD_INNER), rev(SSD_INNER), rev(SSD_INNER), rev(CONV_DIM), rev(CONV_DIM),
                  rev(HEAD_PAD), pl.BlockSpec((ncb, SSD_N, SSD_INNER), lambda i: (nb - 1 - i, 0, 0)),
                  _full((CONV_W, CONV_DIM)), _full((1, SSD_INNER)), _full((1, SSD_INNER)), _full((1, SSD_INNER)),
                  _full((1, SSD_INNER)), _full((HEAD_PAD, SSD_INNER)), _full((SSD_INNER, HEAD_PAD)), _full((CHUNK, CHUNK)),
                  _full((CHUNK, CHUNK))],
        out_specs=[rev(SSD_INNER), rev(CONV_DIM), rev(HEAD_PAD), _full((1, SSD_INNER)), _full((1, SSD_INNER)),
                   _full((1, HEAD_PAD)), _full((1, HEAD_PAD)), _full((CONV_W, CONV_DIM)), _full((1, CONV_DIM))],
        out_shape=[jax.ShapeDtypeStruct((s, SSD_INNER), F32), jax.ShapeDtypeStruct((s, CONV_DIM), F32),
                   jax.ShapeDtypeStruct((s, HEAD_PAD), F32), jax.ShapeDtypeStruct((1, SSD_INNER), F32),
                   jax.ShapeDtypeStruct((1, SSD_INNER), F32), jax.ShapeDtypeStruct((1, HEAD_PAD), F32),
                   jax.ShapeDtypeStruct((1, HEAD_PAD), F32), jax.ShapeDtypeStruct((CONV_W, CONV_DIM), F32),
                   jax.ShapeDtypeStruct((1, CONV_DIM), F32)],
        scratch_shapes=[pltpu.VMEM((SSD_N, SSD_INNER), F32), pltpu.VMEM((SSD_ROWS, CONV_DIM), F32), pltpu.VMEM((8, CONV_DIM), F32)],
        compiler_params=_params(),
    )(dy, ypre, z, c, xraw, misc, prev, cw, dtb, a_exp, d_exp, nw, emisc, emisc_t, tri, trit)


def _qkv_inproj_bwd(dq, dk, dv, cq, ckv, dmisc_dt, dz, dxbc, h, dh1, qnw, kvnw, nw, wuq, wkv, win, cosf, sinf):
    s = dq.shape[0]
    wide = MLA_HEADS * HEAD_PAD
    tm = TM

    def body(dq_ref, dk_ref, dv_ref, cq_ref, ckv_ref, dmdt_ref, dz_ref, dxbc_ref, h_ref, dh1_ref, qnw_ref, kvnw_ref, nw_ref,
             wuq_ref, wkv_ref, win_ref, cos_ref, sin_ref, dqb_ref, dkvb_ref, dproj_ref, dh0_ref, gq_ref, gkv_ref, gnw_ref):
        first = pl.program_id(0) == 0
        cosf, sinf = cos_ref[...], sin_ref[...]
        dkr = jnp.zeros((tm, HEAD_PAD), F32)
        for hd in range(MLA_HEADS):
            cols = slice(hd * HEAD_PAD, (hd + 1) * HEAD_PAD)
            dqb_ref[:, cols] = _rope(dq_ref[:, cols], cosf, sinf, -1.0).astype(BF16)
            dkh = dk_ref[:, cols]
            dkvb_ref[:, cols] = dkh
            dkr = dkr + dkh
        dkvb_ref[:, wide:] = dv_ref[...]
        lane = lax.broadcasted_iota(jnp.int32, dkr.shape, 1)
        in_rope = jnp.logical_and(lane >= MISC_ROPE, lane < MISC_ROPE + QK_ROPE)
        dmisc_rope = jnp.where(in_rope, _rope(jnp.where(in_rope, dkr, 0.0), cosf, sinf, -1.0), 0.0)
        dcq, gq = _rms_bwd(cq_ref[...], qnw_ref[...], _dot_nt(dqb_ref[...], wuq_ref[...]))
        _acc_rows(gq_ref, gq, first)
        dckv, gkv = _rms_bwd(ckv_ref[...], kvnw_ref[...], _dot_nt(dkvb_ref[...], wkv_ref[...]))
        _acc_rows(gkv_ref, gkv, first)
        dproj_ref[:, 0:768] = dcq.astype(BF16)
        dproj_ref[:, 768:1024] = dckv.astype(BF16)
        dproj_ref[:, 1024:1152] = (dmisc_rope + dmdt_ref[...]).astype(BF16)
        dproj_ref[:, 1152:1664] = dz_ref[...].astype(BF16)
        dproj_ref[:, 1664:2688] = dxbc_ref[...].astype(BF16)
        dx, gnw = _rms_bwd(h_ref[...], nw_ref[...], _dot_nt(dproj_ref[...], win_ref[...]))
        _acc_rows(gnw_ref, gnw, first)
        dh0_ref[...] = dh1_ref[...] + dx

    return pl.pallas_call(
        body, name="qkv_inproj_bwd", grid=(s // tm,),
        in_specs=[_rows(tm, wide)] * 3 + [_rows(tm, Q_RANK), _rows(tm, KV_RANK), _rows(tm, HEAD_PAD), _rows(tm, SSD_INNER),
                                          _rows(tm, CONV_DIM), _rows(tm, D_MODEL), _rows(tm, D_MODEL),
                                          _full((1, Q_RANK)), _full((1, KV_RANK)), _full((1, D_MODEL)),
                                          _resident((Q_RANK, wide)), _resident((KV_RANK, 2 * wide)), _resident((D_MODEL, IN_PAD)),
                                          _rows(tm, HEAD_PAD), _rows(tm, HEAD_PAD)],
        out_specs=[_rows(tm, wide), _rows(tm, 2 * wide), _rows(tm, IN_PAD), _rows(tm, D_MODEL),
                   _full((1, Q_RANK)), _full((1, KV_RANK)), _full((1, D_MODEL))],
        out_shape=[jax.ShapeDtypeStruct((s, wide), BF16), jax.ShapeDtypeStruct((s, 2 * wide), BF16),
                   jax.ShapeDtypeStruct((s, IN_PAD), BF16), jax.ShapeDtypeStruct((s, D_MODEL), F32),
                   jax.ShapeDtypeStruct((1, Q_RANK), F32), jax.ShapeDtypeStruct((1, KV_RANK), F32),
                   jax.ShapeDtypeStruct((1, D_MODEL), F32)],
        compiler_params=_params(),
    )(dq, dk, dv, cq, ckv, dmisc_dt, dz, dxbc, h, dh1, qnw, kvnw, nw, wuq, wkv, win, cosf, sinf)


def _row_tile(rows, cols):
    cap = max(8, (1 << 18) // max(cols, 128))
    best = None
    for t in range(8, rows + 1, 8):
        if rows % t == 0 and t <= cap:
            best = t
    return best if best is not None else rows


def _adamw(w, g, m, v, name):
    rows, cols = w.shape
    tr = _row_tile(rows, cols)

    def body(w_ref, g_ref, m_ref, v_ref, d_ref, m2_ref, v2_ref):
        gg = g_ref[...]
        m2 = ADAM_B1 * m_ref[...] + (1.0 - ADAM_B1) * gg
        v2 = ADAM_B2 * v_ref[...] + (1.0 - ADAM_B2) * jnp.square(gg)
        m_hat = m2 / (1.0 - ADAM_B1 ** ADAM_STEP)
        v_hat = v2 / (1.0 - ADAM_B2 ** ADAM_STEP)
        d_ref[...] = -ADAM_LR * (m_hat / (jnp.sqrt(v_hat) + ADAM_EPS) + ADAM_WD * w_ref[...])
        m2_ref[...] = m2
        v2_ref[...] = v2

    spec = pl.BlockSpec((tr, cols), lambda i: (i, 0))
    return pl.pallas_call(
        body, name=name, grid=(rows // tr,),
        in_specs=[spec] * 4, out_specs=[spec] * 3,
        out_shape=[jax.ShapeDtypeStruct((rows, cols), F32)] * 3,
    )(w, g, m, v)


def _sum_adamw(slots, w, m, v, name):
    _, rows, cols = w.shape
    tr = _row_tile(rows, cols)
    nb = rows // tr

    def body(s0_ref, s1_ref, w_ref, m_ref, v_ref, g_ref, d_ref, m2_ref, v2_ref):
        for l, ref in enumerate((s0_ref, s1_ref)):
            @pl.when(pl.program_id(0) == l)
            def _(ref=ref):
                acc = ref[0].astype(F32)
                for i in range(1, N_DEV):
                    acc = acc + ref[i].astype(F32)
                g_ref[...] = acc

        gg = g_ref[...]
        m2 = ADAM_B1 * m_ref[...] + (1.0 - ADAM_B1) * gg
        v2 = ADAM_B2 * v_ref[...] + (1.0 - ADAM_B2) * jnp.square(gg)
        m_hat = m2 / (1.0 - ADAM_B1 ** ADAM_STEP)
        v_hat = v2 / (1.0 - ADAM_B2 ** ADAM_STEP)
        d_ref[...] = -ADAM_LR * (m_hat / (jnp.sqrt(v_hat) + ADAM_EPS) + ADAM_WD * w_ref[...])
        m2_ref[...] = m2
        v2_ref[...] = v2

    slot_spec = lambda layer: pl.BlockSpec((N_DEV, tr, cols), lambda l, i: (0, jnp.where(l == layer, i, (nb - 1) * (1 - layer)), 0))
    spec = pl.BlockSpec((None, tr, cols), lambda l, i: (l, i, 0))
    return pl.pallas_call(
        body, name=name, grid=(DEPTH, nb),
        in_specs=[slot_spec(0), slot_spec(1), spec, spec, spec], out_specs=[spec] * 4,
        out_shape=[jax.ShapeDtypeStruct(w.shape, F32)] * 4,
        compiler_params=_params(),
    )(slots[0], slots[1], w, m, v)


_MESH = pl.DeviceIdType.MESH
_ANY = pl.BlockSpec(memory_space=pl.ANY)


def _my_place():
    return lax.axis_index("x"), lax.axis_index("y"), lax.axis_index("c")


def _flip(place, k):
    x, y, c = place
    return (1 - x if k & 4 else x, 1 - y if k & 2 else y, 1 - c if k & 1 else c)


def _block_id(place):
    return 4 * place[0] + 2 * place[1] + place[2]


def _peer_copies(kind, in_refs, out_refs, send_sems, recv_sems, local_sems):
    me = _my_place()
    my = _block_id(me)
    remote, local = [], []
    for a, (x_ref, out_ref) in enumerate(zip(in_refs, out_refs)):
        src_of = (lambda place, r=x_ref: r) if kind == "gather" else (lambda place, r=x_ref: r.at[_block_id(place)])
        local.append(pltpu.make_async_copy(src_of(me), out_ref.at[my], local_sems.at[a]))
        for k in range(1, N_DEV):
            peer = _flip(me, k)
            remote.append(pltpu.make_async_remote_copy(
                src_ref=src_of(peer), dst_ref=out_ref.at[my], send_sem=send_sems.at[a * 7 + k - 1],
                recv_sem=recv_sems.at[a * 7 + k - 1], device_id=peer, device_id_type=_MESH))
    return remote, local


def _comm_out_shapes(kind, arrays):
    return [jax.ShapeDtypeStruct((N_DEV, *a.shape) if kind == "gather" else a.shape, a.dtype) for a in arrays]


def _comm_scratch(n):
    return [pltpu.SemaphoreType.DMA((7 * n,)), pltpu.SemaphoreType.DMA((7 * n,)), pltpu.SemaphoreType.DMA((n,))]


def _hosted_comm(kind, in_refs, out_refs, sems, first, last):
    if not in_refs:
        return

    @pl.when(first)
    def _():
        remote, local = _peer_copies(kind, in_refs, out_refs, *sems)
        for cp in local + remote:
            cp.start()

    @pl.when(last)
    def _():
        remote, local = _peer_copies(kind, in_refs, out_refs, *sems)
        for cp in remote:
            cp.wait()
        for cp in local:
            cp.wait()


def _two_level_gather_steps(in_refs, out_refs, send_sems, recv_sems, local_sems):
    n = len(in_refs)
    me = _my_place()
    x, y, c = me
    sibling = (x, y, 1 - c)
    chips = [(1 - x, y), (x, 1 - y), (1 - x, 1 - y)]

    def copy(a, k, place, to, src=None):
        block = out_refs[a].at[_block_id(place)]
        return pltpu.make_async_remote_copy(
            src_ref=block if src is None else src, dst_ref=block, send_sem=send_sems.at[7 * a + k],
            recv_sem=recv_sems.at[7 * a + k], device_id=to, device_id_type=_MESH)

    mine = [pltpu.make_async_copy(in_refs[a], out_refs[a].at[_block_id(me)], local_sems.at[a]) for a in range(n)]
    first = [copy(a, 0, me, sibling, src=in_refs[a]) for a in range(n)]
    first += [copy(a, 1 + j, me, (*chip, c), src=in_refs[a]) for a in range(n) for j, chip in enumerate(chips)]
    passed = [copy(a, 4 + j, (*chip, c), sibling) for a in range(n) for j, chip in enumerate(chips)]

    def send():
        for cp in mine + first:
            cp.start()

    def forward():
        for a in range(n):
            for j, chip in enumerate(chips):
                copy(a, 1 + j, (*chip, c), me).wait_recv()
                passed[3 * a + j].start()

    def finish():
        for a in range(n):
            copy(a, 0, sibling, me).wait_recv()
            for j, chip in enumerate(chips):
                copy(a, 4 + j, (*chip, 1 - c), me).wait_recv()
        for cp in first + passed:
            cp.wait_send()
        for cp in mine:
            cp.wait()

    return send, forward, finish


def _gather_two_level(arrays, name):
    n = len(arrays)

    def body(*refs):
        for step in _two_level_gather_steps(refs[:n], refs[n:2 * n], *refs[2 * n:]):
            step()

    return pl.pallas_call(
        body, name=name, out_shape=_comm_out_shapes("gather", arrays),
        in_specs=[_ANY] * n, out_specs=[_ANY] * n, scratch_shapes=_comm_scratch(n),
    )(*arrays)


def _hosted_gather(in_refs, out_refs, sems, first, middle, last):
    if not in_refs:
        return
    for when, index in ((first, 0), (middle, 1), (last, 2)):
        @pl.when(when)
        def _(index=index):
            _two_level_gather_steps(in_refs, out_refs, *sems)[index]()


def _comm(kind, arrays, name):
    n = len(arrays)

    def body(*refs):
        remote, local = _peer_copies(kind, refs[:n], refs[n:2 * n], *refs[2 * n:])
        for cp in local + remote:
            cp.start()
        for cp in remote:
            cp.wait()
        for cp in local:
            cp.wait()

    return pl.pallas_call(
        body, name=name, out_shape=_comm_out_shapes(kind, arrays),
        in_specs=[_ANY] * n, out_specs=[_ANY] * n, scratch_shapes=_comm_scratch(n),
    )(*arrays)


def _all_reduce_small(part):
    rows, lanes = part.shape
    vmem = pl.BlockSpec(memory_space=pltpu.VMEM)

    def body(x_ref, gath_ref, sum_ref, send_sems, recv_sems):
        me = _my_place()
        my = _block_id(me)
        gath_ref[my] = x_ref[...]
        copies = []
        for k in range(1, N_DEV):
            cp = pltpu.make_async_remote_copy(
                src_ref=x_ref, dst_ref=gath_ref.at[my], send_sem=send_sems.at[k - 1], recv_sem=recv_sems.at[k - 1],
                device_id=_flip(me, k), device_id_type=_MESH)
            cp.start()
            copies.append(cp)
        for cp in copies:
            cp.wait()
        acc = gath_ref[0]
        for i in range(1, N_DEV):
            acc = acc + gath_ref[i]
        sum_ref[...] = acc

    return pl.pallas_call(
        body, name="small_grad_all_reduce",
        out_shape=[jax.ShapeDtypeStruct((N_DEV, rows, lanes), F32), jax.ShapeDtypeStruct((rows, lanes), F32)],
        in_specs=[vmem], out_specs=[vmem, vmem],
        scratch_shapes=[pltpu.SemaphoreType.DMA((7,)), pltpu.SemaphoreType.DMA((7,))],
    )(part)[1]


_SHARDED = (("w_in", (D_MODEL, IN_PROJ // N_DEV)), ("w_uq", (Q_RANK // N_DEV, Q_RANK)), ("w_ukv", (KV_RANK, HEAD_PAD)),
            ("conv_w", (CONV_W, CONV_DIM // N_DEV)), ("w_out", (D_MODEL // N_DEV, D_MODEL)),
            ("w_up", (D_MODEL, D_FF // N_DEV)), ("w_down", (D_FF // N_DEV, D_MODEL)))
_SMALL = (("pre_mix_norm", D_MODEL), ("q_norm", Q_RANK), ("kv_norm", KV_RANK), ("conv_b", CONV_DIM), ("dt_bias", SSD_HEADS),
          ("a_log", SSD_HEADS), ("d_skip", SSD_HEADS), ("ssd_norm", SSD_INNER), ("post_mix_norm", D_MODEL),
          ("pre_mlp_norm", D_MODEL), ("post_mlp_norm", D_MODEL))
_WEIGHT_ORDER = ("pre_mix_norm", "w_in", "q_norm", "w_uq", "kv_norm", "w_ukv", "conv_w", "conv_b", "dt_bias", "a_log", "d_skip",
                 "ssd_norm", "w_out", "post_mix_norm", "pre_mlp_norm", "w_up", "w_down", "post_mlp_norm")
_EARLY = ("w_in", "w_uq", "w_ukv", "conv_w")
_LATE = ("w_out", "w_up", "w_down")


def _wire_shard(name, a):
    return lax.bitcast_convert_type(a, BF16).reshape(CONV_W, -1) if name == "conv_w" else a.astype(BF16)


def _from_wire(name, g):
    return lax.bitcast_convert_type(g.reshape(N_DEV, CONV_W, -1, 2), F32) if name == "conv_w" else g


def _cols(stacked):
    return jnp.transpose(stacked, (1, 0, 2)).reshape(stacked.shape[1], -1)


def _win_segments():
    s2, s3, s5 = Q_RANK + KV_RANK, Q_RANK + KV_RANK + QK_ROPE, IN_PROJ - SSD_HEADS
    return [(0, s2), (None, MISC_ROPE), (s2, s3), (s5, IN_PROJ), (None, HEAD_PAD - MISC_DT - SSD_HEADS), (s3, s5)]


def _win_from_shards(stacked):
    per = IN_PROJ // N_DEV
    parts = []
    for start, stop in _win_segments():
        if start is None:
            parts.append(jnp.zeros((D_MODEL, stop), stacked.dtype))
            continue
        while start < stop:
            j, a = divmod(start, per)
            b = min(per, a + stop - start)
            parts.append(stacked[j, :, a:b])
            start += b - a
    return jnp.concatenate(parts, axis=1)


def _win_grad_shards(dwin):
    per = IN_PROJ // N_DEV
    runs, at = [], 0
    for start, stop in _win_segments():
        if start is not None:
            runs.append((start, stop, at))
        at += stop if start is None else stop - start
    blocks = []
    for j in range(N_DEV):
        lo, hi = j * per, (j + 1) * per
        parts = [dwin[:, p + max(lo, a) - a:p + min(hi, b) - a] for a, b, p in sorted(runs) if max(lo, a) < min(hi, b)]
        blocks.append(jnp.concatenate(parts, axis=1))
    return jnp.stack(blocks)


def _early_weights(sh):
    win = _win_from_shards(sh["w_in"])
    w_uq = sh["w_uq"].reshape(Q_RANK, MLA_HEADS, QK_NOPE + QK_ROPE)
    wuq = jnp.pad(w_uq, ((0, 0), (0, 0), (0, HEAD_PAD - QK_NOPE - QK_ROPE))).reshape(Q_RANK, -1)
    w_ukv = _cols(sh["w_ukv"]).reshape(KV_RANK, MLA_HEADS, QK_NOPE + V_DIM)
    wkn = jnp.pad(w_ukv[..., :QK_NOPE], ((0, 0), (0, 0), (0, HEAD_PAD - QK_NOPE))).reshape(KV_RANK, -1)
    wv = w_ukv[..., QK_NOPE:].reshape(KV_RANK, 4, 2, 1, V_DIM) * jnp.eye(2, dtype=BF16).reshape(1, 1, 2, 2, 1)
    wkv = jnp.concatenate([wkn, wv.reshape(KV_RANK, -1)], axis=1)
    return dict(win=win, wuq=wuq, wkv=wkv, conv_w=_cols(sh["conv_w"]))


def _late_weights(sh):
    w_out = sh["w_out"].reshape(D_MODEL, D_MODEL)
    watt = w_out[:SSD_INNER].reshape(4, 2, 1, V_DIM, D_MODEL) * jnp.eye(2, dtype=BF16).reshape(1, 2, 2, 1, 1)
    wout = jnp.concatenate([watt.reshape(MLA_HEADS * HEAD_PAD, D_MODEL), w_out[SSD_INNER:]], axis=0)
    return dict(wout=wout, wup=sh["w_up"], wdown=sh["w_down"])


def _shard_grads(g):
    out = {}
    if "wup" in g:
        out["w_up"], out["w_down"] = g["wup"], g["wdown"]
        ae = g["wout_att"].reshape(4, 2, 2, V_DIM, D_MODEL)
        att = jnp.stack([ae[:, 0, 0], ae[:, 1, 1]], axis=1).reshape(SSD_INNER, D_MODEL)
        out["w_out"] = jnp.concatenate([att, g["wout_ssd"]], axis=0).astype(BF16).reshape(N_DEV, D_MODEL // N_DEV, D_MODEL)
    if "win" not in g:
        return out
    out["w_in"] = _win_grad_shards(g["win"].astype(BF16))
    w_uq = g["wuq"].astype(BF16).reshape(Q_RANK, MLA_HEADS, HEAD_PAD)[..., :QK_NOPE + QK_ROPE].reshape(Q_RANK, Q_RANK)
    out["w_uq"] = w_uq.reshape(N_DEV, Q_RANK // N_DEV, Q_RANK)
    wide = MLA_HEADS * HEAD_PAD
    wkv = g["wkv"].astype(BF16)
    kn = wkv[:, :wide].reshape(KV_RANK, MLA_HEADS, HEAD_PAD)[..., :QK_NOPE]
    ve = wkv[:, wide:].reshape(KV_RANK, 4, 2, 2, V_DIM)
    vv = jnp.stack([ve[:, :, 0, 0], ve[:, :, 1, 1]], axis=2).reshape(KV_RANK, MLA_HEADS, V_DIM)
    out["w_ukv"] = jnp.transpose(jnp.concatenate([kn, vv], axis=-1), (1, 0, 2))
    out["conv_w"] = jnp.transpose(g["conv_w"].astype(BF16).reshape(CONV_W, N_DEV, -1), (1, 0, 2))
    return out


def _small_rows(n):
    return -(-n // 1024) * 8


def _pack_small(vals):
    rows = []
    for l in range(DEPTH):
        for name, n in _SMALL:
            r = _small_rows(n)
            rows.append(jnp.pad(vals[name][l].reshape(-1), (0, r * 128 - n)).reshape(r, 128))
    return jnp.concatenate(rows, axis=0)


def _unpack_small(packed):
    out, off = {name: [] for name, _ in _SMALL}, 0
    for l in range(DEPTH):
        for name, n in _SMALL:
            r = _small_rows(n)
            out[name].append(packed[off:off + r].reshape(-1)[:n])
            off += r
    return {name: jnp.stack(v) for name, v in out.items()}


def _lane_rows(vec8):
    return jnp.repeat(vec8, SSD_P).reshape(1, SSD_INNER)


def _layer_fwd(h, kw, sm, l, cosf, sinf, consts, gather=(), after_gather=None, target=None):
    row = lambda name: sm[name][l].reshape(1, -1)
    t = {}
    t["h0"] = h
    t["ub"], t["cq"], t["ckv"], t["misc"], t["z"], t["xraw"] = _inproj_fwd(h, row("pre_mix_norm"), kw["win"])
    t["cqn"], t["ckvn"], t["q"], t["k"], t["v"] = _qkv_fwd(t["cq"], t["ckv"], t["misc"], row("q_norm"), row("kv_norm"),
                                                         kw["wuq"], kw["wkv"], cosf, sinf)
    t["oe"], t["lse"], gathered = _attn_fwd(t["q"], t["k"], t["v"], gather)
    if after_gather is not None:
        after_gather(gathered)
    t["dtb"] = _lane_rows(sm["dt_bias"][l])
    t["a_exp"] = _lane_rows(-jnp.exp(sm["a_log"][l]))
    t["d_exp"] = _lane_rows(sm["d_skip"][l])
    t["c"], t["prev"], t["ypre"], t["yssd"] = _ssd_fwd(t["xraw"], t["misc"], t["z"], kw["conv_w"], row("conv_b"), t["dtb"],
                                                     t["a_exp"], t["d_exp"], row("ssd_norm"), consts)
    t["mixed"], t["h1"] = _outproj_fwd(t["oe"], t["yssd"], kw["wout"], h, row("post_mix_norm"))
    t["mb"], t["ab"], t["d"], *out = _mlp_fwd(t["h1"], row("pre_mlp_norm"), kw["wup"], kw["wdown"], row("post_mlp_norm"), target)
    return out, t


def _layer_bwd(dh2, t, kw, sm, l, cosf, sinf, consts, exchange_of=None):
    row = lambda name: sm[name][l].reshape(1, -1)
    g, gs = {}, {}
    dh1, dab, ddb, gs["post_mlp_norm"], gs["pre_mlp_norm"] = _mlp_bwd(
        dh2, t["d"], t["h1"], t["ab"], row("pre_mlp_norm"), kw["wup"], kw["wdown"], row("post_mlp_norm"))
    g["wup"] = _matmul_tn_stacked(t["mb"], dab, f"dw_up_{l}", a_stacked=False)
    g["wdown"] = _matmul_tn_stacked(t["ab"], ddb, f"dw_down_{l}", a_stacked=True, square_a=True)
    dmixb, doe, dyssd, gs["post_mix_norm"], delta = _outproj_bwd(dh1, t["mixed"], row("post_mix_norm"), kw["wout"], t["oe"])
    g["wout_att"] = _matmul_tn(t["oe"], dmixb, f"dw_out_att_{l}")
    g["wout_ssd"] = _matmul_tn(t["yssd"], dmixb, f"dw_out_ssd_{l}")
    dz, dxraw, dmisc_dt, gs["ssd_norm"], gd, galog, gdtb, g["conv_w"], gs["conv_b"] = _ssd_bwd(
        dyssd, t["ypre"], t["z"], t["c"], t["xraw"], t["misc"], t["prev"], kw["conv_w"], t["dtb"], t["a_exp"], t["d_exp"],
        row("ssd_norm"), consts)
    gs["d_skip"] = jnp.sum(gd.reshape(SSD_HEADS, SSD_P), axis=1)
    gs["a_log"] = galog[0, MISC_DT:MISC_DT + SSD_HEADS]
    gs["dt_bias"] = gdtb[0, MISC_DT:MISC_DT + SSD_HEADS]
    dq, dk, dv, exchanged = _attn_bwd(t["q"], t["k"], t["v"], doe, t["lse"], delta,
                                      exchange_of(g) if exchange_of is not None else ())
    dqb, dkvb, dprojb, dh0, gs["q_norm"], gs["kv_norm"], gs["pre_mix_norm"] = _qkv_inproj_bwd(
        dq, dk, dv, t["cq"], t["ckv"], dmisc_dt, dz, dxraw, t["h0"], dh1, row("q_norm"), row("kv_norm"),
        row("pre_mix_norm"), kw["wuq"], kw["wkv"], kw["win"], cosf, sinf)
    g["wuq"] = _matmul_tn(t["cqn"], dqb, f"dw_uq_{l}")
    g["wkv"] = _matmul_tn(t["ckvn"], dkvb, f"dw_kv_{l}")
    g["win"] = _matmul_tn(t["ub"], dprojb, f"dw_in_{l}")
    return dh0, g, {k: v.reshape(-1) for k, v in gs.items()}, exchanged


def _local_step(x, positions, kws, sm, target, gather=(), after_gather=None, exchange_of=None):
    inv_freq = ROPE_THETA ** (-jnp.arange(0, QK_ROPE, 2, dtype=F32) / QK_ROPE)
    invf = jnp.zeros((HEAD_PAD,), F32).at[MISC_ROPE:MISC_ROPE + QK_ROPE].set(jnp.concatenate([inv_freq, inv_freq]))
    cosf, sinf = _rope_tables(positions.reshape(-1, 1), invf.reshape(1, HEAD_PAD))
    consts = _ssd_consts()
    (h,), t0 = _layer_fwd(x, kws[0], sm, 0, cosf, sinf, consts, gather, after_gather)
    (dh, loss), t1 = _layer_fwd(h, kws[1], sm, 1, cosf, sinf, consts, target=target)
    saved = [t0, t1]
    grads, small, exchanged = [None] * DEPTH, [None] * DEPTH, []
    for l in reversed(range(DEPTH)):
        hook = (lambda g0: exchange_of(g0, grads[1])) if (l == 0 and exchange_of is not None) else None
        dh, grads[l], small[l], got = _layer_bwd(dh, saved[l], kws[l], sm, l, cosf, sinf, consts, hook)
        exchanged = got or exchanged
    return loss[0, 0], dh, grads, small, exchanged


def kernel(x, positions, pre_mix_norm, w_in, q_norm, w_uq, kv_norm, w_ukv, conv_w, conv_b, dt_bias, a_log, d_skip, ssd_norm, w_out, post_mix_norm, pre_mlp_norm, w_up, w_down, post_mlp_norm, loss_target, m_pre_mix_norm, m_w_in, m_q_norm, m_w_uq, m_kv_norm, m_w_ukv, m_conv_w, m_conv_b, m_dt_bias, m_a_log, m_d_skip, m_ssd_norm, m_w_out, m_post_mix_norm, m_pre_mlp_norm, m_w_up, m_w_down, m_post_mlp_norm, v_pre_mix_norm, v_w_in, v_q_norm, v_w_uq, v_kv_norm, v_w_ukv, v_conv_w, v_conv_b, v_dt_bias, v_a_log, v_d_skip, v_ssd_norm, v_w_out, v_post_mix_norm, v_pre_mlp_norm, v_w_up, v_w_down, v_post_mlp_norm):
    w = dict(pre_mix_norm=pre_mix_norm, w_in=w_in, q_norm=q_norm, w_uq=w_uq, kv_norm=kv_norm, w_ukv=w_ukv, conv_w=conv_w,
             conv_b=conv_b, dt_bias=dt_bias, a_log=a_log, d_skip=d_skip, ssd_norm=ssd_norm, w_out=w_out,
             post_mix_norm=post_mix_norm, pre_mlp_norm=pre_mlp_norm, w_up=w_up, w_down=w_down, post_mlp_norm=post_mlp_norm)
    m = dict(pre_mix_norm=m_pre_mix_norm, w_in=m_w_in, q_norm=m_q_norm, w_uq=m_w_uq, kv_norm=m_kv_norm, w_ukv=m_w_ukv,
             conv_w=m_conv_w, conv_b=m_conv_b, dt_bias=m_dt_bias, a_log=m_a_log, d_skip=m_d_skip, ssd_norm=m_ssd_norm,
             w_out=m_w_out, post_mix_norm=m_post_mix_norm, pre_mlp_norm=m_pre_mlp_norm, w_up=m_w_up, w_down=m_w_down,
             post_mlp_norm=m_post_mlp_norm)
    v = dict(pre_mix_norm=v_pre_mix_norm, w_in=v_w_in, q_norm=v_q_norm, w_uq=v_w_uq, kv_norm=v_kv_norm, w_ukv=v_w_ukv,
             conv_w=v_conv_w, conv_b=v_conv_b, dt_bias=v_dt_bias, a_log=v_a_log, d_skip=v_d_skip, ssd_norm=v_ssd_norm,
             w_out=v_w_out, post_mix_norm=v_post_mix_norm, pre_mlp_norm=v_pre_mlp_norm, w_up=v_w_up, w_down=v_w_down,
             post_mlp_norm=v_post_mlp_norm)
    sm = {name: w[name] for name, _ in _SMALL}

    wire = lambda name, l: _wire_shard(name, w[name][l])
    first = _gather_two_level([wire(name, 0) for name in _EARLY], "weight_gather_first")
    kws = [_early_weights({name: _from_wire(name, a) for name, a in zip(_EARLY, first)}), None]
    behind = [(name, 0) for name in _LATE] + [(name, 1) for name, _ in _SHARDED]

    def after_gather(gathered):
        got = {key: _from_wire(key[0], a) for key, a in zip(behind, gathered)}
        kws[0].update(_late_weights({name: got[name, 0] for name in _LATE}))
        kws[1] = {**_early_weights({name: got[name, 1] for name in _EARLY}),
                  **_late_weights({name: got[name, 1] for name in _LATE})}

    sent_behind = [(name, 1) for name, _ in _SHARDED] + [(name, 0) for name in _LATE]

    def exchange_of(g0, g1):
        blocks = {**{(name, 1): a for name, a in _shard_grads(g1).items()},
                  **{(name, 0): a for name, a in _shard_grads(g0).items()}}
        return [blocks[key] for key in sent_behind]

    loss_part, dx, grads, small, exchanged = _local_step(
        x[0], positions[0], kws, sm, loss_target[0], [wire(*key) for key in behind], after_gather, exchange_of)
    slots = dict(zip(sent_behind, exchanged))
    last = _shard_grads({k: grads[0][k] for k in ("win", "wuq", "wkv", "conv_w")})
    slots.update({(name, 0): a for name, a in zip(_EARLY, _comm("exchange", [last[name] for name in _EARLY], "grad_exchange_last"))})
    g_small = _unpack_small(_all_reduce_small(_pack_small({name: jnp.stack([small[l][name] for l in range(DEPTH)])
                                                           for name, _ in _SMALL})))
    loss = lax.psum(loss_part, ("x", "y", "c"))

    grad, delta, new_m, new_v = {}, {}, {}, {}
    for name, _ in _SHARDED:
        grad[name], delta[name], new_m[name], new_v[name] = _sum_adamw(
            [slots[name, 0], slots[name, 1]], w[name], m[name], v[name], f"sum_adamw_{name}")
    pk = lambda d: _pack_small({name: d[name] for name, _ in _SMALL})
    d_, m_, v_ = _adamw(pk(w), pk(g_small), pk(m), pk(v), "adamw_small")
    for dst, packed in ((delta, d_), (new_m, m_), (new_v, v_)):
        dst.update(_unpack_small(packed))
    grad.update(g_small)

    outs = [loss, dx[None]]
    for d in (grad, delta, new_m, new_v):
        outs += [d[name] for name in _WEIGHT_ORDER]
    return tuple(outs)
```

```python
import jax
import jax.numpy as jnp
import numpy as np
from jax import lax
from jax.experimental import pallas as pl
from jax.experimental.pallas import tpu as pltpu

F32 = jnp.float32
BF16 = jnp.bfloat16

D_MODEL = 1024
DEPTH = 2
N_DEV = 8
CHUNK = 64
EPS = 1e-6
MLA_HEADS = 8
QK_NOPE = 64
QK_ROPE = 32
V_DIM = 64
Q_RANK = 768
KV_RANK = 256
ROPE_THETA = 10000.0
SSD_HEADS = 8
SSD_P = 64
SSD_INNER = 512
SSD_GROUPS = 2
SSD_N = 128
CONV_W = 4
CONV_DIM = 1024
D_FF = 4096
IN_PROJ = 2600
HEAD_PAD = 128
IN_PAD = 2688
MISC_ROPE = 64
MISC_DT = 96
ATT_SCALE = (QK_NOPE + QK_ROPE) ** -0.5
LOG2E = 1.4426950408889634
ATT_SCALE_LOG2 = ATT_SCALE * LOG2E

ADAM_LR = 0.001
ADAM_B1 = 0.9
ADAM_B2 = 0.999
ADAM_EPS = 1e-08
ADAM_WD = 0.01
ADAM_STEP = 10

TM = 512
ATT_T = 512
ATT_G = 8
ATT_UNROLL = 4
SSD_ROWS = 512
TK_DW = 4096
V7X_VMEM_BYTES = 64 * 1024 * 1024
VMEM_LIMIT = V7X_VMEM_BYTES - 8 * 1024 * 1024

_NT = (((1,), (1,)), ((), ()))
_TN = (((0,), (0,)), ((), ()))


def _params(**kw):
    return pltpu.CompilerParams(vmem_limit_bytes=VMEM_LIMIT, **kw)


def _dot(a, b, precision=None):
    return jnp.dot(a, b, preferred_element_type=F32, precision=precision)


def _dot_nt(a, b, precision=None):
    return lax.dot_general(a, b, _NT, preferred_element_type=F32, precision=precision)


def _dot_tn(a, b, precision=None):
    return lax.dot_general(a, b, _TN, preferred_element_type=F32, precision=precision)


def _split3(x):
    hi = x.astype(BF16)
    r = x - hi.astype(F32)
    mid = r.astype(BF16)
    return hi, mid, (r - mid.astype(F32)).astype(BF16)


def _dot01(x, m01, dot=_dot, left=False):
    parts = [dot(m01, p) if left else dot(p, m01) for p in _split3(x)]
    return parts[0] + parts[1] + parts[2]


def _full(shape):
    n = len(shape)
    return pl.BlockSpec(shape, lambda *_: (0,) * n)


def _resident(shape):
    n = len(shape)
    return pl.BlockSpec(shape, lambda *_: (0,) * n, pipeline_mode=pl.Buffered(1))


def _rows(tm, width):
    return pl.BlockSpec((tm, width), lambda i: (i, 0))


def _rms_fwd(x, w):
    r = lax.rsqrt(jnp.mean(x * x, axis=-1, keepdims=True) + EPS)
    return (x * r) * w


def _rms_bwd(x, w, dy):
    r = lax.rsqrt(jnp.mean(x * x, axis=-1, keepdims=True) + EPS)
    xh = x * r
    dxn = dy * w
    dx = r * (dxn - xh * jnp.mean(dxn * xh, axis=-1, keepdims=True))
    return dx, dy * xh


def _acc_rows(ref, val, first):
    s = jnp.sum(val, axis=0, keepdims=True)

    @pl.when(first)
    def _():
        ref[...] = s

    @pl.when(jnp.logical_not(first))
    def _():
        ref[...] += s


def _rope(t, cosf, sinf, sign):
    lane = lax.broadcasted_iota(jnp.int32, t.shape, 1)
    rot = jnp.where(lane < MISC_ROPE + QK_ROPE // 2, -pltpu.roll(t, HEAD_PAD - QK_ROPE // 2, 1), pltpu.roll(t, QK_ROPE // 2, 1))
    return t * cosf + sign * (rot * sinf)


def _rope_tables(pos, invf):
    s = pos.shape[0]

    def body(pos_ref, invf_ref, cos_ref, sin_ref):
        ang = pos_ref[...].astype(F32) * invf_ref[...]
        cos_ref[...] = jnp.cos(ang)
        sin_ref[...] = jnp.sin(ang)

    return pl.pallas_call(
        body, name="rope_tables", grid=(s // TM,),
        in_specs=[_rows(TM, 1), _full((1, HEAD_PAD))],
        out_specs=[_rows(TM, HEAD_PAD), _rows(TM, HEAD_PAD)],
        out_shape=[jax.ShapeDtypeStruct((s, HEAD_PAD), F32)] * 2,
    )(pos, invf)


def _inproj_qkv_fwd(h, nw, win, qnw, kvnw, wuq, wkv, cosf, sinf):
    s = h.shape[0]

    def body(h_ref, nw_ref, w_ref, qnw_ref, kvnw_ref, wuq_ref, wkv_ref, cos_ref, sin_ref,
             ub_ref, cq_ref, ckv_ref, misc_ref, z_ref, xbc_ref, cqn_ref, ckvn_ref, q_ref, k_ref, v_ref):
        ub = _rms_fwd(h_ref[...], nw_ref[...]).astype(BF16)
        ub_ref[...] = ub
        proj = _dot(ub, w_ref[...])
        cq, ckv, m = proj[:, 0:768], proj[:, 768:1024], proj[:, 1024:1152]
        cq_ref[...] = cq
        ckv_ref[...] = ckv
        misc_ref[...] = m
        z_ref[...] = proj[:, 1152:1664]
        xbc_ref[...] = proj[:, 1664:2688]
        cosf, sinf = cos_ref[...], sin_ref[...]
        cqn = _rms_fwd(cq, qnw_ref[...]).astype(BF16)
        cqn_ref[...] = cqn
        q = _dot(cqn, wuq_ref[...])
        ckvn = _rms_fwd(ckv, kvnw_ref[...]).astype(BF16)
        ckvn_ref[...] = ckvn
        kv = _dot(ckvn, wkv_ref[...])
        lane = lax.broadcasted_iota(jnp.int32, m.shape, 1)
        in_rope = jnp.logical_and(lane >= MISC_ROPE, lane < MISC_ROPE + QK_ROPE)
        kr = jnp.where(in_rope, _rope(m, cosf, sinf, 1.0), 0.0)
        for hd in range(MLA_HEADS):
            cols = slice(hd * HEAD_PAD, (hd + 1) * HEAD_PAD)
            q_ref[:, cols] = _rope(q[:, cols], cosf, sinf, 1.0).astype(BF16)
            k_ref[:, cols] = (kv[:, cols] + kr).astype(BF16)
        vv = kv[:, MLA_HEADS * HEAD_PAD:]
        vlane = lax.broadcasted_iota(jnp.int32, vv.shape, 1)
        ones_at = jnp.where((vlane // HEAD_PAD) % 2 == 0, V_DIM, 0)
        v_ref[...] = jnp.where(vlane % HEAD_PAD == ones_at, 1.0, vv).astype(BF16)

    wide = MLA_HEADS * HEAD_PAD
    widths = (Q_RANK, KV_RANK, HEAD_PAD, SSD_INNER, CONV_DIM)
    return pl.pallas_call(
        body, name="inproj_qkv_fwd", grid=(s // TM,),
        in_specs=[_rows(TM, D_MODEL), _full((1, D_MODEL)), _resident((D_MODEL, IN_PAD)), _full((1, Q_RANK)), _full((1, KV_RANK)),
                  _resident((Q_RANK, wide)), _resident((KV_RANK, 2 * wide)), _rows(TM, HEAD_PAD), _rows(TM, HEAD_PAD)],
        out_specs=[_rows(TM, D_MODEL)] + [_rows(TM, w) for w in widths]
        + [_rows(TM, Q_RANK), _rows(TM, KV_RANK), _rows(TM, wide), _rows(TM, wide), _rows(TM, wide)],
        out_shape=[jax.ShapeDtypeStruct((s, D_MODEL), BF16)] + [jax.ShapeDtypeStruct((s, w), F32) for w in widths]
        + [jax.ShapeDtypeStruct((s, Q_RANK), BF16), jax.ShapeDtypeStruct((s, KV_RANK), BF16)]
        + [jax.ShapeDtypeStruct((s, wide), BF16)] * 3,
        compiler_params=_params(),
    )(h, nw, win, qnw, kvnw, wuq, wkv, cosf, sinf)


def _chunk_bias(t, keys_on_rows=False):
    row = lax.broadcasted_iota(jnp.int32, (t, 1), 0) // CHUNK
    col = lax.broadcasted_iota(jnp.int32, (1, t), 1) // CHUNK
    return jnp.where((row <= col) if keys_on_rows else (col <= row), 0.0, -jnp.inf).astype(F32)


def _attn_fwd(q, k, v, gather=()):
    s = q.shape[0]
    t = ATT_T
    nq = s // t
    pair = ATT_G * HEAD_PAD
    ng = len(gather)

    def body(q_ref, k_ref, v_ref, *rest):
        g_in, (o_ref, lse_ref), g_out = rest[:ng], rest[ng:ng + 2], rest[ng + 2:2 * ng + 2]
        m_s, acc_s, bias_s = rest[2 * ng + 2:2 * ng + 5]
        qi = pl.program_id(1)
        group, groups = pl.program_id(0), MLA_HEADS // ATT_G

        @pl.when(jnp.logical_and(group == 0, qi == 0))
        def _():
            bias_s[...] = _chunk_bias(t)

        _hosted_gather(g_in, g_out, rest[2 * ng + 5:],
                       jnp.logical_and(group == 0, qi == 0),
                       jnp.logical_and(group == groups - 1, qi == min(3 * nq // 4 + 1, nq - 1)),
                       jnp.logical_and(group == groups - 1, qi == nq - 1))
        m_s[...] = jnp.full(m_s.shape, -jnp.inf, F32)
        acc_s[...] = jnp.zeros(acc_s.shape, F32)

        def step(kb, masked):
            r0 = pl.multiple_of(kb * t, t)

            def scores(hh):
                cols = slice(hh * HEAD_PAD, (hh + 1) * HEAD_PAD)
                return _dot_nt(q_ref[:, cols], k_ref[pl.ds(r0, t), cols])

            def soft(hh, raw):
                sc = raw * ATT_SCALE_LOG2
                if masked:
                    sc = sc + bias_s[...]
                m_old = m_s[hh]
                m_new = jnp.maximum(m_old, jnp.max(sc, axis=-1, keepdims=True))
                alpha = jnp.exp2(m_old - m_new)
                p = jnp.exp2(sc - jnp.tile(m_new, (1, t // HEAD_PAD)))
                m_s[hh] = m_new
                return alpha, p.astype(BF16)

            def update(hh, alpha, p):
                cols = slice(hh * HEAD_PAD, (hh + 1) * HEAD_PAD)
                acc_s[hh] = alpha * acc_s[hh] + _dot(p, v_ref[pl.ds(r0, t), cols])

            raw, ap = [None] * ATT_G, [None] * ATT_G
            raw[0] = scores(0)
            for hh in range(ATT_G):
                if hh + 1 < ATT_G:
                    raw[hh + 1] = scores(hh + 1)
                ap[hh] = soft(hh, raw[hh])
                if hh >= 1:
                    update(hh - 1, *ap[hh - 1])
            update(ATT_G - 1, *ap[ATT_G - 1])

        def loop(i, c):
            step(2 * i, False)
            step(2 * i + 1, False)
            return c

        lax.fori_loop(0, qi // 2, loop, 0)

        @pl.when(qi % 2 == 1)
        def _():
            step(qi - 1, False)

        step(qi, True)
        for hh in range(ATT_G):
            cols = slice(hh * HEAD_PAD, (hh + 1) * HEAD_PAD)
            acc = acc_s[hh]
            ones_at = V_DIM * (1 - hh % 2)
            l = jnp.broadcast_to(acc[:, ones_at:ones_at + 1], acc.shape)
            o_ref[:, cols] = (acc / l).astype(BF16)
            lse_ref[hh] = (m_s[hh] + jnp.log(l) * LOG2E).T[0:8, :]

    outs = pl.pallas_call(
        body, name="attn_fwd_gather" if ng else "attn_fwd", grid=(MLA_HEADS // ATT_G, nq),
        in_specs=[pl.BlockSpec((t, pair), lambda h, i: (i, h)),
                  pl.BlockSpec((s, pair), lambda h, i: (0, h), pipeline_mode=pl.Buffered(1)),
                  pl.BlockSpec((s, pair), lambda h, i: (0, h), pipeline_mode=pl.Buffered(1))] + [_ANY] * ng,
        out_specs=[pl.BlockSpec((t, pair), lambda h, i: (i, h)),
                   pl.BlockSpec((ATT_G, 8, t), lambda h, i: (h, 0, i))] + [_ANY] * ng,
        out_shape=[jax.ShapeDtypeStruct((s, MLA_HEADS * HEAD_PAD), BF16), jax.ShapeDtypeStruct((MLA_HEADS, 8, s), F32)]
        + _comm_out_shapes("gather", gather),
        scratch_shapes=[pltpu.VMEM((ATT_G, t, HEAD_PAD), F32), pltpu.VMEM((ATT_G, t, HEAD_PAD), F32), pltpu.VMEM((t, t), F32)]
        + (_comm_scratch(ng) if ng else []),
        compiler_params=_params(),
    )(q, k, v, *gather)
    return outs[0], outs[1], list(outs[2:])


def _interleave(stages):
    live = list(stages)
    while live:
        still = []
        for g in live:
            try:
                next(g)
                still.append(g)
            except StopIteration:
                pass
        live = still


def _ssd_consts():
    emisc = np.zeros((HEAD_PAD, SSD_INNER), np.float32)
    for hd in range(SSD_HEADS):
        emisc[MISC_DT + hd, hd * SSD_P:(hd + 1) * SSD_P] = 1.0
    idx = np.arange(CHUNK)
    tri = (idx[:, None] >= idx[None, :]).astype(np.float32)
    return tuple(jnp.asarray(m, BF16) for m in (emisc, emisc.T.copy(), tri, tri.T.copy()))


def _ssd_chunk_common(cc, misc, emisc, tri, trit, dtb, a_exp):
    sig = jax.nn.sigmoid(cc)
    xa = cc * sig
    dt = jax.nn.softplus(_dot01(misc, emisc) + dtb)
    a = dt * a_exp
    acs = _dot01(a, tri, left=True)
    acs_t = _dot01(a, trit, dot=_dot_tn)
    alast = acs[CHUNK - 1:CHUNK, :]
    return xa, sig, dt, acs, acs_t, alast


def _decay(acs, acs_t, hd):
    row = lax.broadcasted_iota(jnp.int32, (CHUNK, CHUNK), 0)
    col = lax.broadcasted_iota(jnp.int32, (CHUNK, CHUNK), 1)
    diff = acs[:, hd * SSD_P:hd * SSD_P + 1] - acs_t[hd * SSD_P:hd * SSD_P + 1, :]
    return jnp.exp(jnp.where(row >= col, diff, -jnp.inf))


def _half_mask(hh):
    lane = lax.broadcasted_iota(jnp.int32, (CHUNK, 2 * SSD_P), 1)
    return (lane >= SSD_P) if hh else (lane < SSD_P)


def _gate_norm(y, zz):
    sg = jax.nn.sigmoid(zz)
    yz = y * (zz * sg)
    outs, rs = [], []
    half = SSD_INNER // SSD_GROUPS
    for g in range(SSD_GROUPS):
        yg = yz[:, g * half:(g + 1) * half]
        r = lax.rsqrt(jnp.mean(yg * yg, axis=-1, keepdims=True) + EPS)
        outs.append(yg * r)
        rs.append(r)
    return sg, jnp.concatenate(outs, axis=1), rs


def _ssd_fwd(xraw, misc, z, cw, cb, dtb, a_exp, d_exp, nw, consts):
    s = xraw.shape[0]
    nb = s // SSD_ROWS
    ncb = SSD_ROWS // CHUNK
    emisc, _, tri, trit = consts

    def body(x_ref, misc_ref, z_ref, cw_ref, cb_ref, dtb_ref, a_ref, d_ref, nw_ref, emisc_ref, tri_ref, trit_ref,
             c_ref, prev_ref, ypre_ref, yssd_ref, tail_s, state_s):
        i = pl.program_id(0)

        @pl.when(i == 0)
        def _():
            tail_s[...] = jnp.zeros(tail_s.shape, F32)
            state_s[...] = jnp.zeros(state_s.shape, F32)

        x = x_ref[...]
        xext = jnp.concatenate([tail_s[...], x], axis=0)
        acc = x * cw_ref[CONV_W - 1:CONV_W, :] + cb_ref[...]
        for j in range(1, CONV_W):
            acc = acc + pltpu.roll(xext, j, 0)[8:, :] * cw_ref[CONV_W - 1 - j:CONV_W - j, :]
        tail_s[...] = x[SSD_ROWS - 8:, :]
        c_ref[...] = acc

        def chunk(ci):
            r0 = ci * CHUNK
            xa, _, dt, acs, acs_t, alast = _ssd_chunk_common(
                c_ref[pl.ds(r0, CHUNK), :], misc_ref[pl.ds(r0, CHUNK), :], emisc_ref[...], tri_ref[...], trit_ref[...],
                dtb_ref[...], a_ref[...])
            yield
            xs = xa[:, :SSD_INNER]
            xdt = xs * dt
            wgt = (xdt * jnp.exp(alast - acs)).astype(BF16)
            e = jnp.exp(acs)
            ys, new_states, cms = [], [], []
            for g in range(SSD_GROUPS):
                bm = xa[:, SSD_INNER + g * SSD_N:SSD_INNER + (g + 1) * SSD_N].astype(BF16)
                cm = xa[:, SSD_INNER + SSD_GROUPS * SSD_N + g * SSD_N:SSD_INNER + SSD_GROUPS * SSD_N + (g + 1) * SSD_N].astype(BF16)
                cms.append(cm)
                cb_g = _dot_nt(cm, bm)
                gl = slice(g * 256, (g + 1) * 256)
                new_states.append(_dot_tn(bm, wgt[:, gl]))
                for jj in range(2):
                    pair = 2 * g + jj
                    xp = xdt[:, pair * 128:(pair + 1) * 128]
                    yp = None
                    for hh in range(2):
                        sc = (cb_g * _decay(acs, acs_t, 2 * pair + hh)).astype(BF16)
                        term = _dot(sc, jnp.where(_half_mask(hh), xp, 0.0).astype(BF16))
                        yp = term if yp is None else yp + term
                    ys.append(yp)
                yield
            prev = state_s[...]
            prev_ref[ci] = prev
            yoff = jnp.concatenate([_dot(cms[g], prev[:, g * 256:(g + 1) * 256].astype(BF16)) for g in range(SSD_GROUPS)],
                                   axis=1) * e
            state_s[...] = prev * jnp.exp(alast) + jnp.concatenate(new_states, axis=1)
            yield
            y = jnp.concatenate(ys, axis=1) + yoff + d_ref[...] * xs
            ypre_ref[pl.ds(r0, CHUNK), :] = y
            _, yn, _ = _gate_norm(y, z_ref[pl.ds(r0, CHUNK), :])
            yssd_ref[pl.ds(r0, CHUNK), :] = (yn * nw_ref[...]).astype(BF16)

        _interleave([chunk(ci) for ci in range(ncb)])

    return pl.pallas_call(
        body, name="ssd_fwd", grid=(nb,),
        in_specs=[_rows(SSD_ROWS, CONV_DIM), _rows(SSD_ROWS, HEAD_PAD), _rows(SSD_ROWS, SSD_INNER),
                  _full((CONV_W, CONV_DIM)), _full((1, CONV_DIM)), _full((1, SSD_INNER)), _full((1, SSD_INNER)),
                  _full((1, SSD_INNER)), _full((1, SSD_INNER)), _full((HEAD_PAD, SSD_INNER)), _full((CHUNK, CHUNK)),
                  _full((CHUNK, CHUNK))],
        out_specs=[_rows(SSD_ROWS, CONV_DIM), pl.BlockSpec((ncb, SSD_N, SSD_INNER), lambda i: (i, 0, 0)),
                   _rows(SSD_ROWS, SSD_INNER), _rows(SSD_ROWS, SSD_INNER)],
        out_shape=[jax.ShapeDtypeStruct((s, CONV_DIM), F32), jax.ShapeDtypeStruct((s // CHUNK, SSD_N, SSD_INNER), F32),
                   jax.ShapeDtypeStruct((s, SSD_INNER), F32), jax.ShapeDtypeStruct((s, SSD_INNER), BF16)],
        scratch_shapes=[pltpu.VMEM((8, CONV_DIM), F32), pltpu.VMEM((SSD_N, SSD_INNER), F32)],
        compiler_params=_params(),
    )(xraw, misc, z, cw, cb, dtb, a_exp, d_exp, nw, emisc, tri, trit)


def _outproj_fwd(oe, yssd, wout, h, nw):
    s = h.shape[0]
    wide = MLA_HEADS * HEAD_PAD

    def body(oe_ref, y_ref, w_ref, h_ref, nw_ref, mixed_ref, h1_ref):
        mixed = _dot(oe_ref[...], w_ref[0:wide, :]) + _dot(y_ref[...], w_ref[wide:, :])
        mixed_ref[...] = mixed
        h1_ref[...] = h_ref[...] + _rms_fwd(mixed, nw_ref[...])

    return pl.pallas_call(
        body, name="outproj_fwd", grid=(s // TM,),
        in_specs=[_rows(TM, wide), _rows(TM, SSD_INNER), _resident((wide + SSD_INNER, D_MODEL)), _rows(TM, D_MODEL),
                  _full((1, D_MODEL))],
        out_specs=[_rows(TM, D_MODEL), _rows(TM, D_MODEL)],
        out_shape=[jax.ShapeDtypeStruct((s, D_MODEL), F32)] * 2,
        compiler_params=_params(),
    )(oe, yssd, wout, h, nw)


def _mlp_fwd(h1, prew, wup, wdown, postw, target=None):
    s = h1.shape[0]
    fb = D_FF // N_DEV
    last = target is not None

    def body(h_ref, prew_ref, up_ref, down_ref, postw_ref, *rest):
        target_ref, (mb_ref, ab_ref, d_ref, out_ref) = (rest[0] if last else None), rest[last:last + 4]
        hh = h_ref[...]
        mb = _rms_fwd(hh, prew_ref[...]).astype(BF16)
        mb_ref[...] = mb
        d = jnp.zeros((TM, D_MODEL), F32)
        for j in range(N_DEV):
            a = jnp.maximum(_dot(mb, up_ref[j]), 0.0)
            ab_ref[j] = a.astype(BF16)
            d = d + _dot(jnp.square(a).astype(BF16), down_ref[j])
        d_ref[...] = d
        h2 = hh + _rms_fwd(d, postw_ref[...])
        if last:
            diff = h2 - target_ref[...]
            out_ref[...] = diff * (1.0 / D_MODEL)
            part = 0.5 * jnp.sum(jnp.mean(diff * diff, axis=-1, keepdims=True), axis=0, keepdims=True)
            _acc_rows(rest[-1], part, pl.program_id(0) == 0)
        else:
            out_ref[...] = h2

    stacked = pl.BlockSpec((N_DEV, TM, fb), lambda i: (0, i, 0))
    return pl.pallas_call(
        body, name="mlp_fwd_loss" if last else "mlp_fwd", grid=(s // TM,),
        in_specs=[_rows(TM, D_MODEL), _full((1, D_MODEL)), _resident((N_DEV, D_MODEL, fb)), _resident((N_DEV, fb, D_MODEL)),
                  _full((1, D_MODEL))] + ([_rows(TM, D_MODEL)] if last else []),
        out_specs=[_rows(TM, D_MODEL), stacked, _rows(TM, D_MODEL), _rows(TM, D_MODEL)] + ([_full((1, 1))] if last else []),
        out_shape=[jax.ShapeDtypeStruct((s, D_MODEL), BF16), jax.ShapeDtypeStruct((N_DEV, s, fb), BF16),
                   jax.ShapeDtypeStruct((s, D_MODEL), F32), jax.ShapeDtypeStruct((s, D_MODEL), F32)]
        + ([jax.ShapeDtypeStruct((1, 1), F32)] if last else []),
        compiler_params=_params(),
    )(h1, prew, wup, wdown, postw, *([target] if last else []))


def _mlp_bwd(dh2, d, h1, ab, prew, wup, wdown, postw):
    s = dh2.shape[0]
    fb = D_FF // N_DEV
    tm = TM // 2

    def body(dh2_ref, d_ref, h1_ref, ab_ref, prew_ref, up_ref, down_ref, postw_ref,
             dh1_ref, da_ref, dd_ref, gpost_ref, gpre_ref):
        first = pl.program_id(0) == 0
        dh2 = dh2_ref[...]
        dd, gpost = _rms_bwd(d_ref[...], postw_ref[...], dh2)
        _acc_rows(gpost_ref, gpost, first)
        ddb = dd.astype(BF16)
        dd_ref[...] = ddb

        def d_relu_squared(j):
            return _dot_nt(ddb, down_ref[j])

        def pointwise(j, dr):
            da = (dr * (2.0 * ab_ref[j].astype(F32))).astype(BF16)
            da_ref[j] = da
            return da

        dm = jnp.zeros((tm, D_MODEL), F32)
        nxt, da_prev = d_relu_squared(0), None
        for j in range(N_DEV):
            cur = nxt
            if j + 1 < N_DEV:
                nxt = d_relu_squared(j + 1)
            da = pointwise(j, cur)
            if da_prev is not None:
                dm = dm + _dot_nt(da_prev, up_ref[j - 1])
            da_prev = da
        dm = dm + _dot_nt(da_prev, up_ref[N_DEV - 1])
        dx, gpre = _rms_bwd(h1_ref[...], prew_ref[...], dm)
        _acc_rows(gpre_ref, gpre, first)
        dh1_ref[...] = dh2 + dx

    stacked = pl.BlockSpec((N_DEV, tm, fb), lambda i: (0, i, 0))
    return pl.pallas_call(
        body, name="mlp_bwd", grid=(s // tm,),
        in_specs=[_rows(tm, D_MODEL)] * 3 + [stacked, _full((1, D_MODEL)), _resident((N_DEV, D_MODEL, fb)),
                                              _resident((N_DEV, fb, D_MODEL)), _full((1, D_MODEL))],
        out_specs=[_rows(tm, D_MODEL), stacked, _rows(tm, D_MODEL), _full((1, D_MODEL)), _full((1, D_MODEL))],
        out_shape=[jax.ShapeDtypeStruct((s, D_MODEL), F32), jax.ShapeDtypeStruct((N_DEV, s, fb), BF16),
                   jax.ShapeDtypeStruct((s, D_MODEL), BF16), jax.ShapeDtypeStruct((1, D_MODEL), F32),
                   jax.ShapeDtypeStruct((1, D_MODEL), F32)],
        compiler_params=_params(),
    )(dh2, d, h1, ab, prew, wup, wdown, postw)


def _matmul_tn(a, b, name, tk=TK_DW):
    s, m = a.shape
    n = b.shape[1]
    tn = n if n <= 1024 else (n // 2 if (n // 2) % 128 == 0 else n // 3)
    tk = min(tk, s)
    assert n % tn == 0 and tn % 128 == 0 and s % tk == 0

    def body(a_ref, b_ref, o_ref):
        part = _dot_tn(a_ref[...], b_ref[...])

        @pl.when(pl.program_id(1) == 0)
        def _():
            o_ref[...] = part

        @pl.when(pl.program_id(1) != 0)
        def _():
            o_ref[...] += part

    return pl.pallas_call(
        body, name=name, grid=(n // tn, s // tk),
        in_specs=[pl.BlockSpec((tk, m), lambda j, k: (k, 0)), pl.BlockSpec((tk, tn), lambda j, k: (k, j))],
        out_specs=pl.BlockSpec((m, tn), lambda j, k: (0, j)),
        out_shape=jax.ShapeDtypeStruct((m, n), F32),
        compiler_params=_params(),
    )(a, b)


def _matmul_tn_stacked(a, b, name, a_stacked, square_a=False, tk=TK_DW):
    tk = min(tk, a.shape[-2])
    if a_stacked:
        _, s, m = a.shape
        n = b.shape[1]
        in_specs = [pl.BlockSpec((1, tk, m), lambda j, k: (j, k, 0)), pl.BlockSpec((tk, n), lambda j, k: (k, 0))]
    else:
        s, m = a.shape
        n = b.shape[2]
        in_specs = [pl.BlockSpec((tk, m), lambda j, k: (k, 0)), pl.BlockSpec((1, tk, n), lambda j, k: (j, k, 0))]

    nk = s // tk

    def body(a_ref, b_ref, o_ref, acc_s):
        av = a_ref[0] if a_stacked else a_ref[...]
        bv = b_ref[...] if a_stacked else b_ref[0]
        if square_a:
            av = jnp.square(av.astype(F32)).astype(BF16)
        part = _dot_tn(av, bv)
        k = pl.program_id(1)

        @pl.when(k == 0)
        def _():
            acc_s[...] = part

        @pl.when(jnp.logical_and(k != 0, k != nk - 1))
        def _():
            acc_s[...] += part

        @pl.when(k == nk - 1)
        def _():
            o_ref[0] = (part if nk == 1 else acc_s[...] + part).astype(BF16)

    return pl.pallas_call(
        body, name=name, grid=(N_DEV, nk),
        in_specs=in_specs,
        out_specs=pl.BlockSpec((1, m, n), lambda j, k: (j, 0, 0)),
        out_shape=jax.ShapeDtypeStruct((N_DEV, m, n), BF16),
        scratch_shapes=[pltpu.VMEM((m, n), F32)],
        compiler_params=_params(),
    )(a, b)


def _outproj_bwd(dh1, mixed, nw, wout, oe):
    s = dh1.shape[0]
    wide = MLA_HEADS * HEAD_PAD

    def body(dh1_ref, mixed_ref, nw_ref, w_ref, oe_ref, dmix_ref, doe_ref, dy_ref, gnw_ref, delta_ref):
        dmix, gnw = _rms_bwd(mixed_ref[...], nw_ref[...], dh1_ref[...])
        _acc_rows(gnw_ref, gnw, pl.program_id(0) == 0)
        dmb = dmix.astype(BF16)
        dmix_ref[...] = dmb
        doe_ref[...] = _dot_nt(dmb, w_ref[0:wide, :]).astype(BF16)
        dy_ref[...] = _dot_nt(dmb, w_ref[wide:, :])
        ones = jnp.ones((8, HEAD_PAD), BF16)
        for hd in range(MLA_HEADS):
            cols = slice(hd * HEAD_PAD, (hd + 1) * HEAD_PAD)
            prod = oe_ref[:, cols].astype(F32) * doe_ref[:, cols].astype(F32)
            delta_ref[hd] = _dot01(prod, ones, dot=_dot_nt, left=True)

    return pl.pallas_call(
        body, name="outproj_bwd", grid=(s // TM,),
        in_specs=[_rows(TM, D_MODEL), _rows(TM, D_MODEL), _full((1, D_MODEL)), _resident((wide + SSD_INNER, D_MODEL)),
                  _rows(TM, wide)],
        out_specs=[_rows(TM, D_MODEL), _rows(TM, wide), _rows(TM, SSD_INNER), _full((1, D_MODEL)),
                   pl.BlockSpec((MLA_HEADS, 8, TM), lambda i: (0, 0, i))],
        out_shape=[jax.ShapeDtypeStruct((s, D_MODEL), BF16), jax.ShapeDtypeStruct((s, wide), BF16),
                   jax.ShapeDtypeStruct((s, SSD_INNER), F32), jax.ShapeDtypeStruct((1, D_MODEL), F32),
                   jax.ShapeDtypeStruct((MLA_HEADS, 8, s), F32)],
        compiler_params=_params(),
    )(dh1, mixed, nw, wout, oe)


def _attn_bwd(q, k, v, do, lse, delta, exchange=()):
    s = q.shape[0]
    t = ATT_T
    nq = s // t
    pair = 2 * HEAD_PAD
    ne = len(exchange)

    def body(q_ref, k_ref, v_ref, do_ref, lse_ref, delta_ref, *rest):
        e_in, (dq_ref, dk_ref, dv_ref), e_out = rest[:ne], rest[ne:ne + 3], rest[ne + 3:2 * ne + 3]
        dk_s, dv_s, bias_s = rest[2 * ne + 3:2 * ne + 6]
        kb = pl.program_id(1)
        _hosted_comm("exchange", e_in, e_out, rest[2 * ne + 6:],
                     jnp.logical_and(pl.program_id(0) == 0, kb == 0),
                     jnp.logical_and(pl.program_id(0) == MLA_HEADS // 2 - 1, kb == nq - 1))

        @pl.when(jnp.logical_and(pl.program_id(0) == 0, kb == 0))
        def _():
            bias_s[...] = _chunk_bias(t, keys_on_rows=True)

        @pl.when(kb == 0)
        def _():
            dq_ref[...] = jnp.zeros(dq_ref.shape, F32)

        def step(qb, diagonal):
            r0 = pl.multiple_of(qb * t, t)
            for hh in range(2):
                cols = slice(hh * HEAD_PAD, (hh + 1) * HEAD_PAD)
                kk = k_ref[:, cols]
                qq = q_ref[pl.ds(r0, t), cols]
                dd = do_ref[pl.ds(r0, t), cols]
                sc = _dot_nt(kk, qq) * ATT_SCALE_LOG2
                if diagonal:
                    sc = sc + bias_s[...]
                p = jnp.exp2(sc - lse_ref[hh, 0:1, pl.ds(r0, t)])
                dv = _dot(p.astype(BF16), dd)
                dp = _dot_nt(v_ref[:, cols], dd)
                ds = (p * (dp - delta_ref[hh, 0:1, pl.ds(r0, t)]) * ATT_SCALE).astype(BF16)
                dk = _dot(ds, qq)
                if diagonal:
                    dv_s[:, cols] = dv
                    dk_s[:, cols] = dk
                else:
                    dv_s[:, cols] += dv
                    dk_s[:, cols] += dk
                dq_ref[pl.ds(r0, t), cols] += _dot_tn(ds, kk)

        def loop(i, c):
            for u in range(ATT_UNROLL):
                step(kb + 1 + u + ATT_UNROLL * i, False)
            return c

        step(kb, True)
        later_tiles = nq - 1 - kb
        lax.fori_loop(0, later_tiles // ATT_UNROLL, loop, 0)
        left = later_tiles % ATT_UNROLL
        for u in range(ATT_UNROLL - 1):
            @pl.when(left > u)
            def _(u=u):
                step(nq - left + u, False)

        dk_ref[...] = dk_s[...].astype(BF16)
        dv_ref[...] = dv_s[...].astype(BF16)

    whole = pl.BlockSpec((s, pair), lambda h, i: (0, h))
    tile = pl.BlockSpec((t, pair), lambda h, i: (i, h))
    rowvec = pl.BlockSpec((2, 8, s), lambda h, i: (h, 0, 0))
    wide = MLA_HEADS * HEAD_PAD
    outs = pl.pallas_call(
        body, name="attn_bwd_exchange" if ne else "attn_bwd", grid=(MLA_HEADS // 2, nq),
        in_specs=[whole, tile, tile, whole, rowvec, rowvec] + [_ANY] * ne,
        out_specs=[whole, tile, tile] + [_ANY] * ne,
        out_shape=[jax.ShapeDtypeStruct((s, wide), F32)] + [jax.ShapeDtypeStruct((s, wide), BF16)] * 2
        + _comm_out_shapes("exchange", exchange),
        scratch_shapes=[pltpu.VMEM((t, pair), F32), pltpu.VMEM((t, pair), F32), pltpu.VMEM((t, t), F32)]
        + (_comm_scratch(ne) if ne else []),
        compiler_params=_params(),
    )(q, k, v, do, lse, delta, *exchange)
    return outs[0], outs[1], outs[2], list(outs[3:])


def _ssd_bwd(dy, ypre, z, c, xraw, misc, prev, cw, dtb, a_exp, d_exp, nw, consts):
    s = dy.shape[0]
    nb = s // SSD_ROWS
    ncb = SSD_ROWS // CHUNK
    emisc, emisc_t, tri, trit = consts

    def body(dy_ref, ypre_ref, z_ref, c_ref, x_ref, misc_ref, prev_ref, cw_ref, dtb_ref, a_ref, d_ref, nw_ref,
             emisc_ref, emisct_ref, tri_ref, trit_ref,
             dz_ref, dx_ref, dmisc_ref, gnw_ref, gd_ref, galog_ref, gdtb_ref, gcw_ref, gcb_ref,
             dst_s, dc_s, head_s):
        i = pl.program_id(0)
        first = i == 0

        @pl.when(first)
        def _():
            dst_s[...] = jnp.zeros(dst_s.shape, F32)
            head_s[...] = jnp.zeros(head_s.shape, F32)
            gnw_ref[...] = jnp.zeros(gnw_ref.shape, F32)
            gd_ref[...] = jnp.zeros(gd_ref.shape, F32)
            galog_ref[...] = jnp.zeros(galog_ref.shape, F32)
            gdtb_ref[...] = jnp.zeros(gdtb_ref.shape, F32)

        a_exp_v = a_ref[...]
        a8 = _dot01(a_exp_v, emisct_ref[...]) * (1.0 / SSD_P)

        def chunk(ci):
            r0 = ci * CHUNK
            cc = c_ref[pl.ds(r0, CHUNK), :]
            mm = misc_ref[pl.ds(r0, CHUNK), :]
            xa, sig_c, dt, acs, acs_t, alast = _ssd_chunk_common(cc, mm, emisc_ref[...], tri_ref[...], trit_ref[...],
                                                              dtb_ref[...], a_exp_v)
            yield
            xs = xa[:, :SSD_INNER]
            xdt = xs * dt
            y = ypre_ref[pl.ds(r0, CHUNK), :]
            zz = z_ref[pl.ds(r0, CHUNK), :]
            sg, yn, rs = _gate_norm(y, zz)
            dyo = dy_ref[pl.ds(r0, CHUNK), :]
            gnw_ref[...] += jnp.sum(dyo * yn, axis=0, keepdims=True)
            dyn = dyo * nw_ref[...]
            half = SSD_INNER // SSD_GROUPS
            dyz_parts = []
            for g in range(SSD_GROUPS):
                gl = slice(g * half, (g + 1) * half)
                dyz_parts.append(rs[g] * (dyn[:, gl] - yn[:, gl] * jnp.mean(dyn[:, gl] * yn[:, gl], axis=-1, keepdims=True)))
            dyz = jnp.concatenate(dyz_parts, axis=1)
            dz_ref[pl.ds(r0, CHUNK), :] = dyz * y * (sg * (1.0 + zz * (1.0 - sg)))
            dyp = dyz * (zz * sg)
            dypb = dyp.astype(BF16)
            gd_ref[...] += jnp.sum(dyp * xs, axis=0, keepdims=True)
            yield
            prev = prev_ref[ci]
            cd = jnp.exp(alast)
            e = jnp.exp(acs)
            dsx = jnp.exp(alast - acs)
            wgt = (xdt * dsx).astype(BF16)
            dze = (dyp * e).astype(BF16)
            dprev_parts, diag_all, dbm, dcm, yoff_parts, bms = [], [], [], [], [], []
            lane8 = lax.broadcasted_iota(jnp.int32, (CHUNK, HEAD_PAD), 1)
            diag8 = jnp.zeros((CHUNK, HEAD_PAD), F32)
            for g in range(SSD_GROUPS):
                gl = slice(g * 256, (g + 1) * 256)
                bm = xa[:, SSD_INNER + g * SSD_N:SSD_INNER + (g + 1) * SSD_N].astype(BF16)
                cm = xa[:, SSD_INNER + SSD_GROUPS * SSD_N + g * SSD_N:SSD_INNER + SSD_GROUPS * SSD_N + (g + 1) * SSD_N].astype(BF16)
                bms.append(bm)
                prev_g = prev[:, gl].astype(BF16)
                dcm_g = _dot_nt(dze[:, gl], prev_g)
                dprev_parts.append(_dot_tn(cm, dze[:, gl]))
                cb_g = _dot_nt(cm, bm)
                dcb = jnp.zeros((CHUNK, CHUNK), F32)
                diag_parts = []
                for jj in range(2):
                    pair = 2 * g + jj
                    pl_ = slice(pair * 128, (pair + 1) * 128)
                    xp = xdt[:, pl_]
                    dyp_p = dypb[:, pl_]
                    dxp = jnp.zeros((CHUNK, 128), F32)
                    for hh in range(2):
                        hd = 2 * pair + hh
                        dec = _decay(acs, acs_t, hd)
                        xm = jnp.where(_half_mask(hh), xp, 0.0).astype(BF16)
                        dsc = _dot_nt(dyp_p, xm) * dec
                        dcb = dcb + dsc
                        sc = (cb_g * dec).astype(BF16)
                        dxp = dxp + jnp.where(_half_mask(hh), _dot_tn(sc, dyp_p), 0.0)
                        dm = dsc * cb_g
                        diag8 = diag8 + jnp.where(lane8 == MISC_DT + hd, jnp.sum(dm - dm.T, axis=1, keepdims=True), 0.0)
                    diag_parts.append(dxp)
                dcbb = dcb.astype(BF16)
                dcm.append(dcm_g + _dot(dcbb, bm))
                dbm.append(_dot_tn(dcbb, cm))
                diag_all.append(jnp.concatenate(diag_parts, axis=1))
                yoff_parts.append(_dot(cm, prev_g) * e[:, gl])
                yield
            dst = dst_s[...]
            glast = jnp.sum(dst * prev, axis=0, keepdims=True) * cd
            dxdt_state_parts = []
            for g in range(SSD_GROUPS):
                gl = slice(g * 256, (g + 1) * 256)
                dst_g = dst[:, gl].astype(BF16)
                dxdt_state_parts.append(_dot(bms[g], dst_g) * dsx[:, gl])
                dbm[g] = dbm[g] + _dot_nt(wgt[:, gl], dst_g)
            dst_s[...] = dst * cd + jnp.concatenate(dprev_parts, axis=1)
            yield
            dxdt_state = jnp.concatenate(dxdt_state_parts, axis=1)
            dxdt = jnp.concatenate(diag_all, axis=1) + dxdt_state
            dacs = dyp * jnp.concatenate(yoff_parts, axis=1) - xdt * dxdt_state
            last = jnp.sum(xdt * dxdt_state, axis=0, keepdims=True) + glast
            row = lax.broadcasted_iota(jnp.int32, (CHUNK, SSD_INNER), 0)
            dacs = dacs + jnp.where(row == CHUNK - 1, last, 0.0)
            dacs8 = _dot01(dacs, emisct_ref[...]) + diag8
            da8 = _dot01(dacs8, trit_ref[...], left=True)
            ddt8 = da8 * a8 + _dot01(dxdt * xs, emisct_ref[...])
            yield
            dtr8 = mm + _dot01(dtb_ref[...], emisct_ref[...]) * (1.0 / SSD_P)
            dt8 = jax.nn.softplus(dtr8)
            lane = lax.broadcasted_iota(jnp.int32, (CHUNK, HEAD_PAD), 1)
            on_dt = jnp.logical_and(lane >= MISC_DT, lane < MISC_DT + SSD_HEADS)
            ddtr8 = jnp.where(on_dt, ddt8 * jax.nn.sigmoid(dtr8), 0.0)
            dmisc_ref[pl.ds(r0, CHUNK), :] = ddtr8
            gdtb_ref[...] += jnp.sum(ddtr8, axis=0, keepdims=True)
            galog_ref[...] += jnp.sum(jnp.where(on_dt, da8 * dt8, 0.0), axis=0, keepdims=True) * a8
            dxs = d_ref[...] * dyp + dxdt * dt
            dxa = jnp.concatenate([dxs] + dbm + dcm, axis=1)
            dc_s[pl.ds(r0, CHUNK), :] = dxa * (sig_c * (1.0 + cc * (1.0 - sig_c)))

        _interleave([chunk(ci) for ci in reversed(range(ncb))])

        dc = dc_s[...]
        x = x_ref[...]
        dcext = jnp.concatenate([dc, head_s[...]], axis=0)
        dx = dc * cw_ref[CONV_W - 1:CONV_W, :]
        rows = [jnp.sum(dc * x, axis=0, keepdims=True)]
        for j in range(1, CONV_W):
            ahead = pltpu.roll(dcext, SSD_ROWS + 8 - j, 0)[:SSD_ROWS, :]
            dx = dx + ahead * cw_ref[CONV_W - 1 - j:CONV_W - j, :]
            rows.insert(0, jnp.sum(ahead * x, axis=0, keepdims=True))
        dx_ref[...] = dx
        head_s[...] = dc[:8, :]
        gcw = jnp.concatenate(rows, axis=0)

        @pl.when(first)
        def _():
            gcw_ref[...] = gcw
            gcb_ref[...] = jnp.sum(dc, axis=0, keepdims=True)

        @pl.when(jnp.logical_not(first))
        def _():
            gcw_ref[...] += gcw
            gcb_ref[...] += jnp.sum(dc, axis=0, keepdims=True)

    def rev(width):
        return pl.BlockSpec((SSD_ROWS, width), lambda i: (nb - 1 - i, 0))

    return pl.pallas_call(
        body, name="ssd_bwd", grid=(nb,),
        in_specs=[rev(SSD_INNER), rev(SSD_INNER), rev(SSD_INNER), rev(CONV_DIM), rev(CONV_DIM),
                  rev(HEAD_PAD), pl.BlockSpec((ncb, SSD_N, SSD_INNER), lambda i: (nb - 1 - i, 0, 0)),
                  _full((CONV_W, CONV_DIM)), _full((1, SSD_INNER)), _full((1, SSD_INNER)), _full((1, SSD_INNER)),
                  _full((1, SSD_INNER)), _full((HEAD_PAD, SSD_INNER)), _full((SSD_INNER, HEAD_PAD)), _full((CHUNK, CHUNK)),
                  _full((CHUNK, CHUNK))],
        out_specs=[rev(SSD_INNER), rev(CONV_DIM), rev(HEAD_PAD), _full((1, SSD_INNER)), _full((1, SSD_INNER)),
                   _full((1, HEAD_PAD)), _full((1, HEAD_PAD)), _full((CONV_W, CONV_DIM)), _full((1, CONV_DIM))],
        out_shape=[jax.ShapeDtypeStruct((s, SSD_INNER), F32), jax.ShapeDtypeStruct((s, CONV_DIM), F32),
                   jax.ShapeDtypeStruct((s, HEAD_PAD), F32), jax.ShapeDtypeStruct((1, SSD_INNER), F32),
                   jax.ShapeDtypeStruct((1, SSD_INNER), F32), jax.ShapeDtypeStruct((1, HEAD_PAD), F32),
                   jax.ShapeDtypeStruct((1, HEAD_PAD), F32), jax.ShapeDtypeStruct((CONV_W, CONV_DIM), F32),
                   jax.ShapeDtypeStruct((1, CONV_DIM), F32)],
        scratch_shapes=[pltpu.VMEM((SSD_N, SSD_INNER), F32), pltpu.VMEM((SSD_ROWS, CONV_DIM), F32), pltpu.VMEM((8, CONV_DIM), F32)],
        compiler_params=_params(),
    )(dy, ypre, z, c, xraw, misc, prev, cw, dtb, a_exp, d_exp, nw, emisc, emisc_t, tri, trit)


def _qkv_inproj_bwd(dq, dk, dv, cq, ckv, dmisc_dt, dz, dxbc, h, dh1, qnw, kvnw, nw, wuq, wkv, win, cosf, sinf):
    s = dq.shape[0]
    wide = MLA_HEADS * HEAD_PAD
    tm = TM

    def body(dq_ref, dk_ref, dv_ref, cq_ref, ckv_ref, dmdt_ref, dz_ref, dxbc_ref, h_ref, dh1_ref, qnw_ref, kvnw_ref, nw_ref,
             wuq_ref, wkv_ref, win_ref, cos_ref, sin_ref, dqb_ref, dkvb_ref, dproj_ref, dh0_ref, gq_ref, gkv_ref, gnw_ref):
        first = pl.program_id(0) == 0
        cosf, sinf = cos_ref[...], sin_ref[...]
        dkr = jnp.zeros((tm, HEAD_PAD), F32)
        for hd in range(MLA_HEADS):
            cols = slice(hd * HEAD_PAD, (hd + 1) * HEAD_PAD)
            dqb_ref[:, cols] = _rope(dq_ref[:, cols], cosf, sinf, -1.0).astype(BF16)
            dkh = dk_ref[:, cols]
            dkvb_ref[:, cols] = dkh
            dkr = dkr + dkh
        dkvb_ref[:, wide:] = dv_ref[...]
        lane = lax.broadcasted_iota(jnp.int32, dkr.shape, 1)
        in_rope = jnp.logical_and(lane >= MISC_ROPE, lane < MISC_ROPE + QK_ROPE)
        dmisc_rope = jnp.where(in_rope, _rope(jnp.where(in_rope, dkr, 0.0), cosf, sinf, -1.0), 0.0)
        dcq, gq = _rms_bwd(cq_ref[...], qnw_ref[...], _dot_nt(dqb_ref[...], wuq_ref[...]))
        _acc_rows(gq_ref, gq, first)
        dckv, gkv = _rms_bwd(ckv_ref[...], kvnw_ref[...], _dot_nt(dkvb_ref[...], wkv_ref[...]))
        _acc_rows(gkv_ref, gkv, first)
        dproj_ref[:, 0:768] = dcq.astype(BF16)
        dproj_ref[:, 768:1024] = dckv.astype(BF16)
        dproj_ref[:, 1024:1152] = (dmisc_rope + dmdt_ref[...]).astype(BF16)
        dproj_ref[:, 1152:1664] = dz_ref[...].astype(BF16)
        dproj_ref[:, 1664:2688] = dxbc_ref[...].astype(BF16)
        dx, gnw = _rms_bwd(h_ref[...], nw_ref[...], _dot_nt(dproj_ref[...], win_ref[...]))
        _acc_rows(gnw_ref, gnw, first)
        dh0_ref[...] = dh1_ref[...] + dx

    return pl.pallas_call(
        body, name="qkv_inproj_bwd", grid=(s // tm,),
        in_specs=[_rows(tm, wide)] * 3 + [_rows(tm, Q_RANK), _rows(tm, KV_RANK), _rows(tm, HEAD_PAD), _rows(tm, SSD_INNER),
                                          _rows(tm, CONV_DIM), _rows(tm, D_MODEL), _rows(tm, D_MODEL),
                                          _full((1, Q_RANK)), _full((1, KV_RANK)), _full((1, D_MODEL)),
                                          _resident((Q_RANK, wide)), _resident((KV_RANK, 2 * wide)), _resident((D_MODEL, IN_PAD)),
                                          _rows(tm, HEAD_PAD), _rows(tm, HEAD_PAD)],
        out_specs=[_rows(tm, wide), _rows(tm, 2 * wide), _rows(tm, IN_PAD), _rows(tm, D_MODEL),
                   _full((1, Q_RANK)), _full((1, KV_RANK)), _full((1, D_MODEL))],
        out_shape=[jax.ShapeDtypeStruct((s, wide), BF16), jax.ShapeDtypeStruct((s, 2 * wide), BF16),
                   jax.ShapeDtypeStruct((s, IN_PAD), BF16), jax.ShapeDtypeStruct((s, D_MODEL), F32),
                   jax.ShapeDtypeStruct((1, Q_RANK), F32), jax.ShapeDtypeStruct((1, KV_RANK), F32),
                   jax.ShapeDtypeStruct((1, D_MODEL), F32)],
        compiler_params=_params(),
    )(dq, dk, dv, cq, ckv, dmisc_dt, dz, dxbc, h, dh1, qnw, kvnw, nw, wuq, wkv, win, cosf, sinf)


def _row_tile(rows, cols):
    cap = max(8, (1 << 18) // max(cols, 128))
    best = None
    for t in range(8, rows + 1, 8):
        if rows % t == 0 and t <= cap:
            best = t
    return best if best is not None else rows


def _adamw(w, g, m, v, name):
    rows, cols = w.shape
    tr = _row_tile(rows, cols)

    def body(w_ref, g_ref, m_ref, v_ref, d_ref, m2_ref, v2_ref):
        gg = g_ref[...]
        m2 = ADAM_B1 * m_ref[...] + (1.0 - ADAM_B1) * gg
        v2 = ADAM_B2 * v_ref[...] + (1.0 - ADAM_B2) * jnp.square(gg)
        m_hat = m2 / (1.0 - ADAM_B1 ** ADAM_STEP)
        v_hat = v2 / (1.0 - ADAM_B2 ** ADAM_STEP)
        d_ref[...] = -ADAM_LR * (m_hat / (jnp.sqrt(v_hat) + ADAM_EPS) + ADAM_WD * w_ref[...])
        m2_ref[...] = m2
        v2_ref[...] = v2

    spec = pl.BlockSpec((tr, cols), lambda i: (i, 0))
    return pl.pallas_call(
        body, name=name, grid=(rows // tr,),
        in_specs=[spec] * 4, out_specs=[spec] * 3,
        out_shape=[jax.ShapeDtypeStruct((rows, cols), F32)] * 3,
    )(w, g, m, v)


def _sum_adamw(slots, w, m, v, name):
    _, rows, cols = w.shape
    tr = _row_tile(rows, cols)
    nb = rows // tr

    def body(s0_ref, s1_ref, w_ref, m_ref, v_ref, g_ref, d_ref, m2_ref, v2_ref):
        for l, ref in enumerate((s0_ref, s1_ref)):
            @pl.when(pl.program_id(0) == l)
            def _(ref=ref):
                acc = ref[0].astype(F32)
                for i in range(1, N_DEV):
                    acc = acc + ref[i].astype(F32)
                g_ref[...] = acc

        gg = g_ref[...]
        m2 = ADAM_B1 * m_ref[...] + (1.0 - ADAM_B1) * gg
        v2 = ADAM_B2 * v_ref[...] + (1.0 - ADAM_B2) * jnp.square(gg)
        m_hat = m2 / (1.0 - ADAM_B1 ** ADAM_STEP)
        v_hat = v2 / (1.0 - ADAM_B2 ** ADAM_STEP)
        d_ref[...] = -ADAM_LR * (m_hat / (jnp.sqrt(v_hat) + ADAM_EPS) + ADAM_WD * w_ref[...])
        m2_ref[...] = m2
        v2_ref[...] = v2

    slot_spec = lambda layer: pl.BlockSpec((N_DEV, tr, cols), lambda l, i: (0, jnp.where(l == layer, i, (nb - 1) * (1 - layer)), 0))
    spec = pl.BlockSpec((None, tr, cols), lambda l, i: (l, i, 0))
    return pl.pallas_call(
        body, name=name, grid=(DEPTH, nb),
        in_specs=[slot_spec(0), slot_spec(1), spec, spec, spec], out_specs=[spec] * 4,
        out_shape=[jax.ShapeDtypeStruct(w.shape, F32)] * 4,
        compiler_params=_params(),
    )(slots[0], slots[1], w, m, v)


_MESH = pl.DeviceIdType.MESH
_ANY = pl.BlockSpec(memory_space=pl.ANY)


def _my_place():
    return lax.axis_index("x"), lax.axis_index("y"), lax.axis_index("c")


def _flip(place, k):
    x, y, c = place
    return (1 - x if k & 4 else x, 1 - y if k & 2 else y, 1 - c if k & 1 else c)


def _block_id(place):
    return 4 * place[0] + 2 * place[1] + place[2]


def _peer_copies(kind, in_refs, out_refs, send_sems, recv_sems, local_sems):
    me = _my_place()
    my = _block_id(me)
    remote, local = [], []
    for a, (x_ref, out_ref) in enumerate(zip(in_refs, out_refs)):
        src_of = (lambda place, r=x_ref: r) if kind == "gather" else (lambda place, r=x_ref: r.at[_block_id(place)])
        local.append(pltpu.make_async_copy(src_of(me), out_ref.at[my], local_sems.at[a]))
        for k in range(1, N_DEV):
            peer = _flip(me, k)
            remote.append(pltpu.make_async_remote_copy(
                src_ref=src_of(peer), dst_ref=out_ref.at[my], send_sem=send_sems.at[a * 7 + k - 1],
                recv_sem=recv_sems.at[a * 7 + k - 1], device_id=peer, device_id_type=_MESH))
    return remote, local


def _comm_out_shapes(kind, arrays):
    return [jax.ShapeDtypeStruct((N_DEV, *a.shape) if kind == "gather" else a.shape, a.dtype) for a in arrays]


def _comm_scratch(n):
    return [pltpu.SemaphoreType.DMA((7 * n,)), pltpu.SemaphoreType.DMA((7 * n,)), pltpu.SemaphoreType.DMA((n,))]


def _hosted_comm(kind, in_refs, out_refs, sems, first, last):
    if not in_refs:
        return

    @pl.when(first)
    def _():
        remote, local = _peer_copies(kind, in_refs, out_refs, *sems)
        for cp in local + remote:
            cp.start()

    @pl.when(last)
    def _():
        remote, local = _peer_copies(kind, in_refs, out_refs, *sems)
        for cp in remote:
            cp.wait()
        for cp in local:
            cp.wait()


def _two_level_gather_steps(in_refs, out_refs, send_sems, recv_sems, local_sems):
    n = len(in_refs)
    me = _my_place()
    x, y, c = me
    sibling = (x, y, 1 - c)
    chips = [(1 - x, y), (x, 1 - y), (1 - x, 1 - y)]

    def copy(a, k, place, to, src=None):
        block = out_refs[a].at[_block_id(place)]
        return pltpu.make_async_remote_copy(
            src_ref=block if src is None else src, dst_ref=block, send_sem=send_sems.at[7 * a + k],
            recv_sem=recv_sems.at[7 * a + k], device_id=to, device_id_type=_MESH)

    mine = [pltpu.make_async_copy(in_refs[a], out_refs[a].at[_block_id(me)], local_sems.at[a]) for a in range(n)]
    first = [copy(a, 0, me, sibling, src=in_refs[a]) for a in range(n)]
    first += [copy(a, 1 + j, me, (*chip, c), src=in_refs[a]) for a in range(n) for j, chip in enumerate(chips)]
    passed = [copy(a, 4 + j, (*chip, c), sibling) for a in range(n) for j, chip in enumerate(chips)]

    def send():
        for cp in mine + first:
            cp.start()

    def forward():
        for a in range(n):
            for j, chip in enumerate(chips):
                copy(a, 1 + j, (*chip, c), me).wait_recv()
                passed[3 * a + j].start()

    def finish():
        for a in range(n):
            copy(a, 0, sibling, me).wait_recv()
            for j, chip in enumerate(chips):
                copy(a, 4 + j, (*chip, 1 - c), me).wait_recv()
        for cp in first + passed:
            cp.wait_send()
        for cp in mine:
            cp.wait()

    return send, forward, finish


def _gather_two_level(arrays, name):
    n = len(arrays)

    def body(*refs):
        for step in _two_level_gather_steps(refs[:n], refs[n:2 * n], *refs[2 * n:]):
            step()

    return pl.pallas_call(
        body, name=name, out_shape=_comm_out_shapes("gather", arrays),
        in_specs=[_ANY] * n, out_specs=[_ANY] * n, scratch_shapes=_comm_scratch(n),
    )(*arrays)


def _hosted_gather(in_refs, out_refs, sems, first, middle, last):
    if not in_refs:
        return
    for when, index in ((first, 0), (middle, 1), (last, 2)):
        @pl.when(when)
        def _(index=index):
            _two_level_gather_steps(in_refs, out_refs, *sems)[index]()


def _comm(kind, arrays, name):
    n = len(arrays)

    def body(*refs):
        remote, local = _peer_copies(kind, refs[:n], refs[n:2 * n], *refs[2 * n:])
        for cp in local + remote:
            cp.start()
        for cp in remote:
            cp.wait()
        for cp in local:
            cp.wait()

    return pl.pallas_call(
        body, name=name, out_shape=_comm_out_shapes(kind, arrays),
        in_specs=[_ANY] * n, out_specs=[_ANY] * n, scratch_shapes=_comm_scratch(n),
    )(*arrays)


def _all_reduce_small(part):
    rows, lanes = part.shape
    vmem = pl.BlockSpec(memory_space=pltpu.VMEM)

    def body(x_ref, gath_ref, sum_ref, send_sems, recv_sems):
        me = _my_place()
        my = _block_id(me)
        gath_ref[my] = x_ref[...]
        copies = []
        for k in range(1, N_DEV):
            cp = pltpu.make_async_remote_copy(
                src_ref=x_ref, dst_ref=gath_ref.at[my], send_sem=send_sems.at[k - 1], recv_sem=recv_sems.at[k - 1],
                device_id=_flip(me, k), device_id_type=_MESH)
            cp.start()
            copies.append(cp)
        for cp in copies:
            cp.wait()
        acc = gath_ref[0]
        for i in range(1, N_DEV):
            acc = acc + gath_ref[i]
        sum_ref[...] = acc

    return pl.pallas_call(
        body, name="small_grad_all_reduce",
        out_shape=[jax.ShapeDtypeStruct((N_DEV, rows, lanes), F32), jax.ShapeDtypeStruct((rows, lanes), F32)],
        in_specs=[vmem], out_specs=[vmem, vmem],
        scratch_shapes=[pltpu.SemaphoreType.DMA((7,)), pltpu.SemaphoreType.DMA((7,))],
    )(part)[1]


_SHARDED = (("w_in", (D_MODEL, IN_PROJ // N_DEV)), ("w_uq", (Q_RANK // N_DEV, Q_RANK)), ("w_ukv", (KV_RANK, HEAD_PAD)),
            ("conv_w", (CONV_W, CONV_DIM // N_DEV)), ("w_out", (D_MODEL // N_DEV, D_MODEL)),
            ("w_up", (D_MODEL, D_FF // N_DEV)), ("w_down", (D_FF // N_DEV, D_MODEL)))
_SMALL = (("pre_mix_norm", D_MODEL), ("q_norm", Q_RANK), ("kv_norm", KV_RANK), ("conv_b", CONV_DIM), ("dt_bias", SSD_HEADS),
          ("a_log", SSD_HEADS), ("d_skip", SSD_HEADS), ("ssd_norm", SSD_INNER), ("post_mix_norm", D_MODEL),
          ("pre_mlp_norm", D_MODEL), ("post_mlp_norm", D_MODEL))
_WEIGHT_ORDER = ("pre_mix_norm", "w_in", "q_norm", "w_uq", "kv_norm", "w_ukv", "conv_w", "conv_b", "dt_bias", "a_log", "d_skip",
                 "ssd_norm", "w_out", "post_mix_norm", "pre_mlp_norm", "w_up", "w_down", "post_mlp_norm")
_EARLY = ("w_in", "w_uq", "w_ukv", "conv_w")
_LATE = ("w_out", "w_up", "w_down")


def _wire_shard(name, a):
    return lax.bitcast_convert_type(a, BF16).reshape(CONV_W, -1) if name == "conv_w" else a.astype(BF16)


def _from_wire(name, g):
    return lax.bitcast_convert_type(g.reshape(N_DEV, CONV_W, -1, 2), F32) if name == "conv_w" else g


def _cols(stacked):
    return jnp.transpose(stacked, (1, 0, 2)).reshape(stacked.shape[1], -1)


def _win_segments():
    s2, s3, s5 = Q_RANK + KV_RANK, Q_RANK + KV_RANK + QK_ROPE, IN_PROJ - SSD_HEADS
    return [(0, s2), (None, MISC_ROPE), (s2, s3), (s5, IN_PROJ), (None, HEAD_PAD - MISC_DT - SSD_HEADS), (s3, s5)]


def _win_from_shards(stacked):
    per = IN_PROJ // N_DEV
    parts = []
    for start, stop in _win_segments():
        if start is None:
            parts.append(jnp.zeros((D_MODEL, stop), stacked.dtype))
            continue
        while start < stop:
            j, a = divmod(start, per)
            b = min(per, a + stop - start)
            parts.append(stacked[j, :, a:b])
            start += b - a
    return jnp.concatenate(parts, axis=1)


def _win_grad_shards(dwin):
    per = IN_PROJ // N_DEV
    runs, at = [], 0
    for start, stop in _win_segments():
        if start is not None:
            runs.append((start, stop, at))
        at += stop if start is None else stop - start
    blocks = []
    for j in range(N_DEV):
        lo, hi = j * per, (j + 1) * per
        parts = [dwin[:, p + max(lo, a) - a:p + min(hi, b) - a] for a, b, p in sorted(runs) if max(lo, a) < min(hi, b)]
        blocks.append(jnp.concatenate(parts, axis=1))
    return jnp.stack(blocks)


def _early_weights(sh):
    win = _win_from_shards(sh["w_in"])
    w_uq = sh["w_uq"].reshape(Q_RANK, MLA_HEADS, QK_NOPE + QK_ROPE)
    wuq = jnp.pad(w_uq, ((0, 0), (0, 0), (0, HEAD_PAD - QK_NOPE - QK_ROPE))).reshape(Q_RANK, -1)
    w_ukv = _cols(sh["w_ukv"]).reshape(KV_RANK, MLA_HEADS, QK_NOPE + V_DIM)
    wkn = jnp.pad(w_ukv[..., :QK_NOPE], ((0, 0), (0, 0), (0, HEAD_PAD - QK_NOPE))).reshape(KV_RANK, -1)
    wv = w_ukv[..., QK_NOPE:].reshape(KV_RANK, 4, 2, 1, V_DIM) * jnp.eye(2, dtype=BF16).reshape(1, 1, 2, 2, 1)
    wkv = jnp.concatenate([wkn, wv.reshape(KV_RANK, -1)], axis=1)
    return dict(win=win, wuq=wuq, wkv=wkv, conv_w=_cols(sh["conv_w"]))


def _late_weights(sh):
    w_out = sh["w_out"].reshape(D_MODEL, D_MODEL)
    watt = w_out[:SSD_INNER].reshape(4, 2, 1, V_DIM, D_MODEL) * jnp.eye(2, dtype=BF16).reshape(1, 2, 2, 1, 1)
    wout = jnp.concatenate([watt.reshape(MLA_HEADS * HEAD_PAD, D_MODEL), w_out[SSD_INNER:]], axis=0)
    return dict(wout=wout, wup=sh["w_up"], wdown=sh["w_down"])


def _shard_grads(g):
    out = {}
    if "wup" in g:
        out["w_up"], out["w_down"] = g["wup"], g["wdown"]
        ae = g["wout_att"].reshape(4, 2, 2, V_DIM, D_MODEL)
        att = jnp.stack([ae[:, 0, 0], ae[:, 1, 1]], axis=1).reshape(SSD_INNER, D_MODEL)
        out["w_out"] = jnp.concatenate([att, g["wout_ssd"]], axis=0).astype(BF16).reshape(N_DEV, D_MODEL // N_DEV, D_MODEL)
    if "win" not in g:
        return out
    out["w_in"] = _win_grad_shards(g["win"].astype(BF16))
    w_uq = g["wuq"].astype(BF16).reshape(Q_RANK, MLA_HEADS, HEAD_PAD)[..., :QK_NOPE + QK_ROPE].reshape(Q_RANK, Q_RANK)
    out["w_uq"] = w_uq.reshape(N_DEV, Q_RANK // N_DEV, Q_RANK)
    wide = MLA_HEADS * HEAD_PAD
    wkv = g["wkv"].astype(BF16)
    kn = wkv[:, :wide].reshape(KV_RANK, MLA_HEADS, HEAD_PAD)[..., :QK_NOPE]
    ve = wkv[:, wide:].reshape(KV_RANK, 4, 2, 2, V_DIM)
    vv = jnp.stack([ve[:, :, 0, 0], ve[:, :, 1, 1]], axis=2).reshape(KV_RANK, MLA_HEADS, V_DIM)
    out["w_ukv"] = jnp.transpose(jnp.concatenate([kn, vv], axis=-1), (1, 0, 2))
    out["conv_w"] = jnp.transpose(g["conv_w"].astype(BF16).reshape(CONV_W, N_DEV, -1), (1, 0, 2))
    return out


def _small_rows(n):
    return -(-n // 1024) * 8


def _pack_small(vals):
    rows = []
    for l in range(DEPTH):
        for name, n in _SMALL:
            r = _small_rows(n)
            rows.append(jnp.pad(vals[name][l].reshape(-1), (0, r * 128 - n)).reshape(r, 128))
    return jnp.concatenate(rows, axis=0)


def _unpack_small(packed):
    out, off = {name: [] for name, _ in _SMALL}, 0
    for l in range(DEPTH):
        for name, n in _SMALL:
            r = _small_rows(n)
            out[name].append(packed[off:off + r].reshape(-1)[:n])
            off += r
    return {name: jnp.stack(v) for name, v in out.items()}


def _lane_rows(vec8):
    return jnp.repeat(vec8, SSD_P).reshape(1, SSD_INNER)


def _layer_fwd(h, kw, sm, l, cosf, sinf, consts, gather=(), after_gather=None, target=None):
    row = lambda name: sm[name][l].reshape(1, -1)
    t = {}
    t["h0"] = h
    (t["ub"], t["cq"], t["ckv"], t["misc"], t["z"], t["xraw"], t["cqn"], t["ckvn"], t["q"], t["k"], t["v"]) = _inproj_qkv_fwd(
        h, row("pre_mix_norm"), kw["win"], row("q_norm"), row("kv_norm"), kw["wuq"], kw["wkv"], cosf, sinf)
    t["oe"], t["lse"], gathered = _attn_fwd(t["q"], t["k"], t["v"], gather)
    if after_gather is not None:
        after_gather(gathered)
    t["dtb"] = _lane_rows(sm["dt_bias"][l])
    t["a_exp"] = _lane_rows(-jnp.exp(sm["a_log"][l]))
    t["d_exp"] = _lane_rows(sm["d_skip"][l])
    t["c"], t["prev"], t["ypre"], t["yssd"] = _ssd_fwd(t["xraw"], t["misc"], t["z"], kw["conv_w"], row("conv_b"), t["dtb"],
                                                     t["a_exp"], t["d_exp"], row("ssd_norm"), consts)
    t["mixed"], t["h1"] = _outproj_fwd(t["oe"], t["yssd"], kw["wout"], h, row("post_mix_norm"))
    t["mb"], t["ab"], t["d"], *out = _mlp_fwd(t["h1"], row("pre_mlp_norm"), kw["wup"], kw["wdown"], row("post_mlp_norm"), target)
    return out, t


def _layer_bwd(dh2, t, kw, sm, l, cosf, sinf, consts, exchange_of=None):
    row = lambda name: sm[name][l].reshape(1, -1)
    g, gs = {}, {}
    dh1, dab, ddb, gs["post_mlp_norm"], gs["pre_mlp_norm"] = _mlp_bwd(
        dh2, t["d"], t["h1"], t["ab"], row("pre_mlp_norm"), kw["wup"], kw["wdown"], row("post_mlp_norm"))
    g["wup"] = _matmul_tn_stacked(t["mb"], dab, f"dw_up_{l}", a_stacked=False)
    g["wdown"] = _matmul_tn_stacked(t["ab"], ddb, f"dw_down_{l}", a_stacked=True, square_a=True)
    dmixb, doe, dyssd, gs["post_mix_norm"], delta = _outproj_bwd(dh1, t["mixed"], row("post_mix_norm"), kw["wout"], t["oe"])
    g["wout_att"] = _matmul_tn(t["oe"], dmixb, f"dw_out_att_{l}")
    g["wout_ssd"] = _matmul_tn(t["yssd"], dmixb, f"dw_out_ssd_{l}")
    dz, dxraw, dmisc_dt, gs["ssd_norm"], gd, galog, gdtb, g["conv_w"], gs["conv_b"] = _ssd_bwd(
        dyssd, t["ypre"], t["z"], t["c"], t["xraw"], t["misc"], t["prev"], kw["conv_w"], t["dtb"], t["a_exp"], t["d_exp"],
        row("ssd_norm"), consts)
    gs["d_skip"] = jnp.sum(gd.reshape(SSD_HEADS, SSD_P), axis=1)
    gs["a_log"] = galog[0, MISC_DT:MISC_DT + SSD_HEADS]
    gs["dt_bias"] = gdtb[0, MISC_DT:MISC_DT + SSD_HEADS]
    dq, dk, dv, exchanged = _attn_bwd(t["q"], t["k"], t["v"], doe, t["lse"], delta,
                                      exchange_of(g) if exchange_of is not None else ())
    dqb, dkvb, dprojb, dh0, gs["q_norm"], gs["kv_norm"], gs["pre_mix_norm"] = _qkv_inproj_bwd(
        dq, dk, dv, t["cq"], t["ckv"], dmisc_dt, dz, dxraw, t["h0"], dh1, row("q_norm"), row("kv_norm"),
        row("pre_mix_norm"), kw["wuq"], kw["wkv"], kw["win"], cosf, sinf)
    g["wuq"] = _matmul_tn(t["cqn"], dqb, f"dw_uq_{l}")
    g["wkv"] = _matmul_tn(t["ckvn"], dkvb, f"dw_kv_{l}")
    g["win"] = _matmul_tn(t["ub"], dprojb, f"dw_in_{l}")
    return dh0, g, {k: v.reshape(-1) for k, v in gs.items()}, exchanged


def _local_step(x, positions, kws, sm, target, gather=(), after_gather=None, exchange_of=None):
    inv_freq = ROPE_THETA ** (-jnp.arange(0, QK_ROPE, 2, dtype=F32) / QK_ROPE)
    invf = jnp.zeros((HEAD_PAD,), F32).at[MISC_ROPE:MISC_ROPE + QK_ROPE].set(jnp.concatenate([inv_freq, inv_freq]))
    cosf, sinf = _rope_tables(positions.reshape(-1, 1), invf.reshape(1, HEAD_PAD))
    consts = _ssd_consts()
    (h,), t0 = _layer_fwd(x, kws[0], sm, 0, cosf, sinf, consts, gather, after_gather)
    (dh, loss), t1 = _layer_fwd(h, kws[1], sm, 1, cosf, sinf, consts, target=target)
    saved = [t0, t1]
    grads, small, exchanged = [None] * DEPTH, [None] * DEPTH, []
    for l in reversed(range(DEPTH)):
        hook = (lambda g0: exchange_of(g0, grads[1])) if (l == 0 and exchange_of is not None) else None
        dh, grads[l], small[l], got = _layer_bwd(dh, saved[l], kws[l], sm, l, cosf, sinf, consts, hook)
        exchanged = got or exchanged
    return loss[0, 0], dh, grads, small, exchanged


def kernel(x, positions, pre_mix_norm, w_in, q_norm, w_uq, kv_norm, w_ukv, conv_w, conv_b, dt_bias, a_log, d_skip, ssd_norm, w_out, post_mix_norm, pre_mlp_norm, w_up, w_down, post_mlp_norm, loss_target, m_pre_mix_norm, m_w_in, m_q_norm, m_w_uq, m_kv_norm, m_w_ukv, m_conv_w, m_conv_b, m_dt_bias, m_a_log, m_d_skip, m_ssd_norm, m_w_out, m_post_mix_norm, m_pre_mlp_norm, m_w_up, m_w_down, m_post_mlp_norm, v_pre_mix_norm, v_w_in, v_q_norm, v_w_uq, v_kv_norm, v_w_ukv, v_conv_w, v_conv_b, v_dt_bias, v_a_log, v_d_skip, v_ssd_norm, v_w_out, v_post_mix_norm, v_pre_mlp_norm, v_w_up, v_w_down, v_post_mlp_norm):
    w = dict(pre_mix_norm=pre_mix_norm, w_in=w_in, q_norm=q_norm, w_uq=w_uq, kv_norm=kv_norm, w_ukv=w_ukv, conv_w=conv_w,
             conv_b=conv_b, dt_bias=dt_bias, a_log=a_log, d_skip=d_skip, ssd_norm=ssd_norm, w_out=w_out,
             post_mix_norm=post_mix_norm, pre_mlp_norm=pre_mlp_norm, w_up=w_up, w_down=w_down, post_mlp_norm=post_mlp_norm)
    m = dict(pre_mix_norm=m_pre_mix_norm, w_in=m_w_in, q_norm=m_q_norm, w_uq=m_w_uq, kv_norm=m_kv_norm, w_ukv=m_w_ukv,
             conv_w=m_conv_w, conv_b=m_conv_b, dt_bias=m_dt_bias, a_log=m_a_log, d_skip=m_d_skip, ssd_norm=m_ssd_norm,
             w_out=m_w_out, post_mix_norm=m_post_mix_norm, pre_mlp_norm=m_pre_mlp_norm, w_up=m_w_up, w_down=m_w_down,
             post_mlp_norm=m_post_mlp_norm)
    v = dict(pre_mix_norm=v_pre_mix_norm, w_in=v_w_in, q_norm=v_q_norm, w_uq=v_w_uq, kv_norm=v_kv_norm, w_ukv=v_w_ukv,
             conv_w=v_conv_w, conv_b=v_conv_b, dt_bias=v_dt_bias, a_log=v_a_log, d_skip=v_d_skip, ssd_norm=v_ssd_norm,
             w_out=v_w_out, post_mix_norm=v_post_mix_norm, pre_mlp_norm=v_pre_mlp_norm, w_up=v_w_up, w_down=v_w_down,
             post_mlp_norm=v_post_mlp_norm)
    sm = {name: w[name] for name, _ in _SMALL}

    wire = lambda name, l: _wire_shard(name, w[name][l])
    first = _gather_two_level([wire(name, 0) for name in _EARLY], "weight_gather_first")
    kws = [_early_weights({name: _from_wire(name, a) for name, a in zip(_EARLY, first)}), None]
    behind = [(name, 0) for name in _LATE] + [(name, 1) for name, _ in _SHARDED]

    def after_gather(gathered):
        got = {key: _from_wire(key[0], a) for key, a in zip(behind, gathered)}
        kws[0].update(_late_weights({name: got[name, 0] for name in _LATE}))
        kws[1] = {**_early_weights({name: got[name, 1] for name in _EARLY}),
                  **_late_weights({name: got[name, 1] for name in _LATE})}

    sent_behind = [(name, 1) for name, _ in _SHARDED] + [(name, 0) for name in _LATE]

    def exchange_of(g0, g1):
        blocks = {**{(name, 1): a for name, a in _shard_grads(g1).items()},
                  **{(name, 0): a for name, a in _shard_grads(g0).items()}}
        return [blocks[key] for key in sent_behind]

    loss_part, dx, grads, small, exchanged = _local_step(
        x[0], positions[0], kws, sm, loss_target[0], [wire(*key) for key in behind], after_gather, exchange_of)
    slots = dict(zip(sent_behind, exchanged))
    last = _shard_grads({k: grads[0][k] for k in ("win", "wuq", "wkv", "conv_w")})
    slots.update({(name, 0): a for name, a in zip(_EARLY, _comm("exchange", [last[name] for name in _EARLY], "grad_exchange_last"))})
    g_small = _unpack_small(_all_reduce_small(_pack_small({name: jnp.stack([small[l][name] for l in range(DEPTH)])
                                                           for name, _ in _SMALL})))
    loss = lax.psum(loss_part, ("x", "y", "c"))

    grad, delta, new_m, new_v = {}, {}, {}, {}
    for name, _ in _SHARDED:
        grad[name], delta[name], new_m[name], new_v[name] = _sum_adamw(
            [slots[name, 0], slots[name, 1]], w[name], m[name], v[name], f"sum_adamw_{name}")
    pk = lambda d: _pack_small({name: d[name] for name, _ in _SMALL})
    d_, m_, v_ = _adamw(pk(w), pk(g_small), pk(m), pk(v), "adamw_small")
    for dst, packed in ((delta, d_), (new_m, m_), (new_v, v_)):
        dst.update(_unpack_small(packed))
    grad.update(g_small)

    outs = [loss, dx[None]]
    for d in (grad, delta, new_m, new_v):
        outs += [d[name] for name in _WEIGHT_ORDER]
    return tuple(outs)
```

```python
import jax
import jax.numpy as jnp
import numpy as np
from jax import lax
from jax.experimental import pallas as pl
from jax.experimental.pallas import tpu as pltpu

F32 = jnp.float32
BF16 = jnp.bfloat16

D_MODEL = 1024
DEPTH = 2
N_DEV = 8
CHUNK = 64
EPS = 1e-6
MLA_HEADS = 8
QK_NOPE = 64
QK_ROPE = 32
V_DIM = 64
Q_RANK = 768
KV_RANK = 256
ROPE_THETA = 10000.0
SSD_HEADS = 8
SSD_P = 64
SSD_INNER = 512
SSD_GROUPS = 2
SSD_N = 128
CONV_W = 4
CONV_DIM = 1024
D_FF = 4096
IN_PROJ = 2600
HEAD_PAD = 128
IN_PAD = 2688
MISC_ROPE = 64
MISC_DT = 96
ATT_SCALE = (QK_NOPE + QK_ROPE) ** -0.5
LOG2E = 1.4426950408889634
ATT_SCALE_LOG2 = ATT_SCALE * LOG2E

ADAM_LR = 0.001
ADAM_B1 = 0.9
ADAM_B2 = 0.999
ADAM_EPS = 1e-08
ADAM_WD = 0.01
ADAM_STEP = 10

TM = 512
ATT_T = 512
ATT_G = 8
ATT_UNROLL = 4
SSD_ROWS = 512
TK_DW = 4096
V7X_VMEM_BYTES = 64 * 1024 * 1024
VMEM_LIMIT = V7X_VMEM_BYTES - 8 * 1024 * 1024

_NT = (((1,), (1,)), ((), ()))
_TN = (((0,), (0,)), ((), ()))


def _params(**kw):
    return pltpu.CompilerParams(vmem_limit_bytes=VMEM_LIMIT, **kw)


def _dot(a, b, precision=None):
    return jnp.dot(a, b, preferred_element_type=F32, precision=precision)


def _dot_nt(a, b, precision=None):
    return lax.dot_general(a, b, _NT, preferred_element_type=F32, precision=precision)


def _dot_tn(a, b, precision=None):
    return lax.dot_general(a, b, _TN, preferred_element_type=F32, precision=precision)


def _split3(x):
    hi = x.astype(BF16)
    r = x - hi.astype(F32)
    mid = r.astype(BF16)
    return hi, mid, (r - mid.astype(F32)).astype(BF16)


def _dot01(x, m01, dot=_dot, left=False):
    parts = [dot(m01, p) if left else dot(p, m01) for p in _split3(x)]
    return parts[0] + parts[1] + parts[2]


def _full(shape):
    n = len(shape)
    return pl.BlockSpec(shape, lambda *_: (0,) * n)


def _resident(shape):
    n = len(shape)
    return pl.BlockSpec(shape, lambda *_: (0,) * n, pipeline_mode=pl.Buffered(1))


def _rows(tm, width):
    return pl.BlockSpec((tm, width), lambda i: (i, 0))


def _rms_fwd(x, w):
    r = lax.rsqrt(jnp.mean(x * x, axis=-1, keepdims=True) + EPS)
    return (x * r) * w


def _rms_bwd(x, w, dy):
    r = lax.rsqrt(jnp.mean(x * x, axis=-1, keepdims=True) + EPS)
    xh = x * r
    dxn = dy * w
    dx = r * (dxn - xh * jnp.mean(dxn * xh, axis=-1, keepdims=True))
    return dx, dy * xh


def _acc_rows(ref, val, first):
    s = jnp.sum(val, axis=0, keepdims=True)

    @pl.when(first)
    def _():
        ref[...] = s

    @pl.when(jnp.logical_not(first))
    def _():
        ref[...] += s


def _rope(t, cosf, sinf, sign):
    lane = lax.broadcasted_iota(jnp.int32, t.shape, 1)
    rot = jnp.where(lane < MISC_ROPE + QK_ROPE // 2, -pltpu.roll(t, HEAD_PAD - QK_ROPE // 2, 1), pltpu.roll(t, QK_ROPE // 2, 1))
    return t * cosf + sign * (rot * sinf)


def _rope_tables(pos, invf):
    s = pos.shape[0]

    def body(pos_ref, invf_ref, cos_ref, sin_ref):
        ang = pos_ref[...].astype(F32) * invf_ref[...]
        cos_ref[...] = jnp.cos(ang)
        sin_ref[...] = jnp.sin(ang)

    return pl.pallas_call(
        body, name="rope_tables", grid=(s // TM,),
        in_specs=[_rows(TM, 1), _full((1, HEAD_PAD))],
        out_specs=[_rows(TM, HEAD_PAD), _rows(TM, HEAD_PAD)],
        out_shape=[jax.ShapeDtypeStruct((s, HEAD_PAD), F32)] * 2,
    )(pos, invf)


def _inproj_qkv_fwd(h, nw, win, qnw, kvnw, wuq, wkv, cosf, sinf):
    s = h.shape[0]

    def body(h_ref, nw_ref, w_ref, qnw_ref, kvnw_ref, wuq_ref, wkv_ref, cos_ref, sin_ref,
             ub_ref, cq_ref, ckv_ref, misc_ref, z_ref, xbc_ref, cqn_ref, ckvn_ref, q_ref, k_ref, v_ref):
        ub = _rms_fwd(h_ref[...], nw_ref[...]).astype(BF16)
        ub_ref[...] = ub
        proj = _dot(ub, w_ref[...])
        cq, ckv, m = proj[:, 0:768], proj[:, 768:1024], proj[:, 1024:1152]
        cq_ref[...] = cq
        ckv_ref[...] = ckv
        misc_ref[...] = m
        z_ref[...] = proj[:, 1152:1664]
        xbc_ref[...] = proj[:, 1664:2688]
        cosf, sinf = cos_ref[...], sin_ref[...]
        cqn = _rms_fwd(cq, qnw_ref[...]).astype(BF16)
        cqn_ref[...] = cqn
        q = _dot(cqn, wuq_ref[...])
        ckvn = _rms_fwd(ckv, kvnw_ref[...]).astype(BF16)
        ckvn_ref[...] = ckvn
        kv = _dot(ckvn, wkv_ref[...])
        lane = lax.broadcasted_iota(jnp.int32, m.shape, 1)
        in_rope = jnp.logical_and(lane >= MISC_ROPE, lane < MISC_ROPE + QK_ROPE)
        kr = jnp.where(in_rope, _rope(m, cosf, sinf, 1.0), 0.0)
        for hd in range(MLA_HEADS):
            cols = slice(hd * HEAD_PAD, (hd + 1) * HEAD_PAD)
            q_ref[:, cols] = _rope(q[:, cols], cosf, sinf, 1.0).astype(BF16)
            k_ref[:, cols] = (kv[:, cols] + kr).astype(BF16)
        vv = kv[:, MLA_HEADS * HEAD_PAD:]
        vlane = lax.broadcasted_iota(jnp.int32, vv.shape, 1)
        ones_at = jnp.where((vlane // HEAD_PAD) % 2 == 0, V_DIM, 0)
        v_ref[...] = jnp.where(vlane % HEAD_PAD == ones_at, 1.0, vv).astype(BF16)

    wide = MLA_HEADS * HEAD_PAD
    widths = (Q_RANK, KV_RANK, HEAD_PAD, SSD_INNER, CONV_DIM)
    return pl.pallas_call(
        body, name="inproj_qkv_fwd", grid=(s // TM,),
        in_specs=[_rows(TM, D_MODEL), _full((1, D_MODEL)), _resident((D_MODEL, IN_PAD)), _full((1, Q_RANK)), _full((1, KV_RANK)),
                  _resident((Q_RANK, wide)), _resident((KV_RANK, 2 * wide)), _rows(TM, HEAD_PAD), _rows(TM, HEAD_PAD)],
        out_specs=[_rows(TM, D_MODEL)] + [_rows(TM, w) for w in widths]
        + [_rows(TM, Q_RANK), _rows(TM, KV_RANK), _rows(TM, wide), _rows(TM, wide), _rows(TM, wide)],
        out_shape=[jax.ShapeDtypeStruct((s, D_MODEL), BF16)] + [jax.ShapeDtypeStruct((s, w), F32) for w in widths]
        + [jax.ShapeDtypeStruct((s, Q_RANK), BF16), jax.ShapeDtypeStruct((s, KV_RANK), BF16)]
        + [jax.ShapeDtypeStruct((s, wide), BF16)] * 3,
        compiler_params=_params(),
    )(h, nw, win, qnw, kvnw, wuq, wkv, cosf, sinf)


def _chunk_bias(t, keys_on_rows=False):
    row = lax.broadcasted_iota(jnp.int32, (t, 1), 0) // CHUNK
    col = lax.broadcasted_iota(jnp.int32, (1, t), 1) // CHUNK
    return jnp.where((row <= col) if keys_on_rows else (col <= row), 0.0, -jnp.inf).astype(F32)


def _attn_fwd(q, k, v, gather=()):
    s = q.shape[0]
    t = ATT_T
    nq = s // t
    pair = ATT_G * HEAD_PAD
    ng = len(gather)

    def body(q_ref, k_ref, v_ref, *rest):
        g_in, (o_ref, lse_ref), g_out = rest[:ng], rest[ng:ng + 2], rest[ng + 2:2 * ng + 2]
        m_s, acc_s, bias_s = rest[2 * ng + 2:2 * ng + 5]
        qi = pl.program_id(1)
        group, groups = pl.program_id(0), MLA_HEADS // ATT_G

        @pl.when(jnp.logical_and(group == 0, qi == 0))
        def _():
            bias_s[...] = _chunk_bias(t)

        _hosted_gather(g_in, g_out, rest[2 * ng + 5:],
                       jnp.logical_and(group == 0, qi == 0),
                       jnp.logical_and(group == groups - 1, qi == min(3 * nq // 4 + 1, nq - 1)),
                       jnp.logical_and(group == groups - 1, qi == nq - 1))
        m_s[...] = jnp.full(m_s.shape, -jnp.inf, F32)
        acc_s[...] = jnp.zeros(acc_s.shape, F32)

        def step(kb, masked):
            r0 = pl.multiple_of(kb * t, t)

            def scores(hh):
                cols = slice(hh * HEAD_PAD, (hh + 1) * HEAD_PAD)
                return _dot_nt(q_ref[:, cols], k_ref[pl.ds(r0, t), cols])

            def soft(hh, raw):
                sc = raw * ATT_SCALE_LOG2
                if masked:
                    sc = sc + bias_s[...]
                m_old = m_s[hh]
                m_new = jnp.maximum(m_old, jnp.max(sc, axis=-1, keepdims=True))
                alpha = jnp.exp2(m_old - m_new)
                p = jnp.exp2(sc - jnp.tile(m_new, (1, t // HEAD_PAD)))
                m_s[hh] = m_new
                return alpha, p.astype(BF16)

            def update(hh, alpha, p):
                cols = slice(hh * HEAD_PAD, (hh + 1) * HEAD_PAD)
                acc_s[hh] = alpha * acc_s[hh] + _dot(p, v_ref[pl.ds(r0, t), cols])

            raw, ap = [None] * ATT_G, [None] * ATT_G
            raw[0] = scores(0)
            for hh in range(ATT_G):
                if hh + 1 < ATT_G:
                    raw[hh + 1] = scores(hh + 1)
                ap[hh] = soft(hh, raw[hh])
                if hh >= 1:
                    update(hh - 1, *ap[hh - 1])
            update(ATT_G - 1, *ap[ATT_G - 1])

        def loop(i, c):
            step(2 * i, False)
            step(2 * i + 1, False)
            return c

        lax.fori_loop(0, qi // 2, loop, 0)

        @pl.when(qi % 2 == 1)
        def _():
            step(qi - 1, False)

        step(qi, True)
        for hh in range(ATT_G):
            cols = slice(hh * HEAD_PAD, (hh + 1) * HEAD_PAD)
            acc = acc_s[hh]
            ones_at = V_DIM * (1 - hh % 2)
            l = jnp.broadcast_to(acc[:, ones_at:ones_at + 1], acc.shape)
            o_ref[:, cols] = (acc / l).astype(BF16)
            lse_ref[hh] = (m_s[hh] + jnp.log(l) * LOG2E).T[0:8, :]

    outs = pl.pallas_call(
        body, name="attn_fwd_gather" if ng else "attn_fwd", grid=(MLA_HEADS // ATT_G, nq),
        in_specs=[pl.BlockSpec((t, pair), lambda h, i: (i, h)),
                  pl.BlockSpec((s, pair), lambda h, i: (0, h), pipeline_mode=pl.Buffered(1)),
                  pl.BlockSpec((s, pair), lambda h, i: (0, h), pipeline_mode=pl.Buffered(1))] + [_ANY] * ng,
        out_specs=[pl.BlockSpec((t, pair), lambda h, i: (i, h)),
                   pl.BlockSpec((ATT_G, 8, t), lambda h, i: (h, 0, i))] + [_ANY] * ng,
        out_shape=[jax.ShapeDtypeStruct((s, MLA_HEADS * HEAD_PAD), BF16), jax.ShapeDtypeStruct((MLA_HEADS, 8, s), F32)]
        + _comm_out_shapes("gather", gather),
        scratch_shapes=[pltpu.VMEM((ATT_G, t, HEAD_PAD), F32), pltpu.VMEM((ATT_G, t, HEAD_PAD), F32), pltpu.VMEM((t, t), F32)]
        + (_comm_scratch(ng) if ng else []),
        compiler_params=_params(),
    )(q, k, v, *gather)
    return outs[0], outs[1], list(outs[2:])


def _interleave(stages):
    live = list(stages)
    while live:
        still = []
        for g in live:
            try:
                next(g)
                still.append(g)
            except StopIteration:
                pass
        live = still


def _ssd_consts():
    emisc = np.zeros((HEAD_PAD, SSD_INNER), np.float32)
    for hd in range(SSD_HEADS):
        emisc[MISC_DT + hd, hd * SSD_P:(hd + 1) * SSD_P] = 1.0
    idx = np.arange(CHUNK)
    tri = (idx[:, None] >= idx[None, :]).astype(np.float32)
    return tuple(jnp.asarray(m, BF16) for m in (emisc, emisc.T.copy(), tri, tri.T.copy()))


def _ssd_chunk_common(cc, misc, emisc, tri, trit, dtb, a_exp):
    sig = jax.nn.sigmoid(cc)
    xa = cc * sig
    dt = jax.nn.softplus(_dot01(misc, emisc) + dtb)
    a = dt * a_exp
    acs = _dot01(a, tri, left=True)
    acs_t = _dot01(a, trit, dot=_dot_tn)
    alast = acs[CHUNK - 1:CHUNK, :]
    return xa, sig, dt, acs, acs_t, alast


def _decay(acs, acs_t, hd):
    row = lax.broadcasted_iota(jnp.int32, (CHUNK, CHUNK), 0)
    col = lax.broadcasted_iota(jnp.int32, (CHUNK, CHUNK), 1)
    diff = acs[:, hd * SSD_P:hd * SSD_P + 1] - acs_t[hd * SSD_P:hd * SSD_P + 1, :]
    return jnp.exp(jnp.where(row >= col, diff, -jnp.inf))


def _half_mask(hh):
    lane = lax.broadcasted_iota(jnp.int32, (CHUNK, 2 * SSD_P), 1)
    return (lane >= SSD_P) if hh else (lane < SSD_P)


def _gate_norm(y, zz):
    sg = jax.nn.sigmoid(zz)
    yz = y * (zz * sg)
    outs, rs = [], []
    half = SSD_INNER // SSD_GROUPS
    for g in range(SSD_GROUPS):
        yg = yz[:, g * half:(g + 1) * half]
        r = lax.rsqrt(jnp.mean(yg * yg, axis=-1, keepdims=True) + EPS)
        outs.append(yg * r)
        rs.append(r)
    return sg, jnp.concatenate(outs, axis=1), rs


def _ssd_outproj_fwd(xraw, misc, z, cw, cb, dtb, a_exp, d_exp, nw, consts, oe, h, wout, post_w):
    s = xraw.shape[0]
    nb = s // SSD_ROWS
    ncb = SSD_ROWS // CHUNK
    emisc, _, tri, trit = consts
    wide = MLA_HEADS * HEAD_PAD

    def body(x_ref, misc_ref, z_ref, cw_ref, cb_ref, dtb_ref, a_ref, d_ref, nw_ref, emisc_ref, tri_ref, trit_ref,
             oe_ref, h_ref, wout_ref, postw_ref, c_ref, prev_ref, ypre_ref, yssd_ref, mixed_ref, h1_ref, tail_s, state_s):
        i = pl.program_id(0)

        @pl.when(i == 0)
        def _():
            tail_s[...] = jnp.zeros(tail_s.shape, F32)
            state_s[...] = jnp.zeros(state_s.shape, F32)

        mixed_att = _dot(oe_ref[...], wout_ref[0:wide, :])
        x = x_ref[...]
        xext = jnp.concatenate([tail_s[...], x], axis=0)
        acc = x * cw_ref[CONV_W - 1:CONV_W, :] + cb_ref[...]
        for j in range(1, CONV_W):
            acc = acc + pltpu.roll(xext, j, 0)[8:, :] * cw_ref[CONV_W - 1 - j:CONV_W - j, :]
        tail_s[...] = x[SSD_ROWS - 8:, :]
        c_ref[...] = acc

        def chunk(ci):
            r0 = ci * CHUNK
            xa, _, dt, acs, acs_t, alast = _ssd_chunk_common(
                c_ref[pl.ds(r0, CHUNK), :], misc_ref[pl.ds(r0, CHUNK), :], emisc_ref[...], tri_ref[...], trit_ref[...],
                dtb_ref[...], a_ref[...])
            yield
            xs = xa[:, :SSD_INNER]
            xdt = xs * dt
            wgt = (xdt * jnp.exp(alast - acs)).astype(BF16)
            e = jnp.exp(acs)
            ys, new_states, cms = [], [], []
            for g in range(SSD_GROUPS):
                bm = xa[:, SSD_INNER + g * SSD_N:SSD_INNER + (g + 1) * SSD_N].astype(BF16)
                cm = xa[:, SSD_INNER + SSD_GROUPS * SSD_N + g * SSD_N:SSD_INNER + SSD_GROUPS * SSD_N + (g + 1) * SSD_N].astype(BF16)
                cms.append(cm)
                cb_g = _dot_nt(cm, bm)
                gl = slice(g * 256, (g + 1) * 256)
                new_states.append(_dot_tn(bm, wgt[:, gl]))
                for jj in range(2):
                    pair = 2 * g + jj
                    xp = xdt[:, pair * 128:(pair + 1) * 128]
                    yp = None
                    for hh in range(2):
                        sc = (cb_g * _decay(acs, acs_t, 2 * pair + hh)).astype(BF16)
                        term = _dot(sc, jnp.where(_half_mask(hh), xp, 0.0).astype(BF16))
                        yp = term if yp is None else yp + term
                    ys.append(yp)
                yield
            prev = state_s[...]
            prev_ref[ci] = prev
            yoff = jnp.concatenate([_dot(cms[g], prev[:, g * 256:(g + 1) * 256].astype(BF16)) for g in range(SSD_GROUPS)],
                                   axis=1) * e
            state_s[...] = prev * jnp.exp(alast) + jnp.concatenate(new_states, axis=1)
            yield
            y = jnp.concatenate(ys, axis=1) + yoff + d_ref[...] * xs
            ypre_ref[pl.ds(r0, CHUNK), :] = y
            _, yn, _ = _gate_norm(y, z_ref[pl.ds(r0, CHUNK), :])
            yssd_ref[pl.ds(r0, CHUNK), :] = (yn * nw_ref[...]).astype(BF16)

        _interleave([chunk(ci) for ci in range(ncb)])
        mixed = mixed_att + _dot(yssd_ref[...], wout_ref[wide:, :])
        mixed_ref[...] = mixed
        h1_ref[...] = h_ref[...] + _rms_fwd(mixed, postw_ref[...])

    return pl.pallas_call(
        body, name="ssd_outproj_fwd", grid=(nb,),
        in_specs=[_rows(SSD_ROWS, CONV_DIM), _rows(SSD_ROWS, HEAD_PAD), _rows(SSD_ROWS, SSD_INNER),
                  _full((CONV_W, CONV_DIM)), _full((1, CONV_DIM)), _full((1, SSD_INNER)), _full((1, SSD_INNER)),
                  _full((1, SSD_INNER)), _full((1, SSD_INNER)), _full((HEAD_PAD, SSD_INNER)), _full((CHUNK, CHUNK)),
                  _full((CHUNK, CHUNK)), _rows(SSD_ROWS, wide), _rows(SSD_ROWS, D_MODEL),
                  _resident((wide + SSD_INNER, D_MODEL)), _full((1, D_MODEL))],
        out_specs=[_rows(SSD_ROWS, CONV_DIM), pl.BlockSpec((ncb, SSD_N, SSD_INNER), lambda i: (i, 0, 0)),
                   _rows(SSD_ROWS, SSD_INNER), _rows(SSD_ROWS, SSD_INNER), _rows(SSD_ROWS, D_MODEL), _rows(SSD_ROWS, D_MODEL)],
        out_shape=[jax.ShapeDtypeStruct((s, CONV_DIM), F32), jax.ShapeDtypeStruct((s // CHUNK, SSD_N, SSD_INNER), F32),
                   jax.ShapeDtypeStruct((s, SSD_INNER), F32), jax.ShapeDtypeStruct((s, SSD_INNER), BF16),
                   jax.ShapeDtypeStruct((s, D_MODEL), F32), jax.ShapeDtypeStruct((s, D_MODEL), F32)],
        scratch_shapes=[pltpu.VMEM((8, CONV_DIM), F32), pltpu.VMEM((SSD_N, SSD_INNER), F32)],
        compiler_params=_params(),
    )(xraw, misc, z, cw, cb, dtb, a_exp, d_exp, nw, emisc, tri, trit, oe, h, wout, post_w)


def _mlp_fwd(h1, prew, wup, wdown, postw, target=None):
    s = h1.shape[0]
    fb = D_FF // N_DEV
    last = target is not None

    def body(h_ref, prew_ref, up_ref, down_ref, postw_ref, *rest):
        target_ref, (mb_ref, ab_ref, d_ref, out_ref) = (rest[0] if last else None), rest[last:last + 4]
        hh = h_ref[...]
        mb = _rms_fwd(hh, prew_ref[...]).astype(BF16)
        mb_ref[...] = mb
        d = jnp.zeros((TM, D_MODEL), F32)
        for j in range(N_DEV):
            a = jnp.maximum(_dot(mb, up_ref[j]), 0.0)
            ab_ref[j] = a.astype(BF16)
            d = d + _dot(jnp.square(a).astype(BF16), down_ref[j])
        d_ref[...] = d
        h2 = hh + _rms_fwd(d, postw_ref[...])
        if last:
            diff = h2 - target_ref[...]
            out_ref[...] = diff * (1.0 / D_MODEL)
            part = 0.5 * jnp.sum(jnp.mean(diff * diff, axis=-1, keepdims=True), axis=0, keepdims=True)
            _acc_rows(rest[-1], part, pl.program_id(0) == 0)
        else:
            out_ref[...] = h2

    stacked = pl.BlockSpec((N_DEV, TM, fb), lambda i: (0, i, 0))
    return pl.pallas_call(
        body, name="mlp_fwd_loss" if last else "mlp_fwd", grid=(s // TM,),
        in_specs=[_rows(TM, D_MODEL), _full((1, D_MODEL)), _resident((N_DEV, D_MODEL, fb)), _resident((N_DEV, fb, D_MODEL)),
                  _full((1, D_MODEL))] + ([_rows(TM, D_MODEL)] if last else []),
        out_specs=[_rows(TM, D_MODEL), stacked, _rows(TM, D_MODEL), _rows(TM, D_MODEL)] + ([_full((1, 1))] if last else []),
        out_shape=[jax.ShapeDtypeStruct((s, D_MODEL), BF16), jax.ShapeDtypeStruct((N_DEV, s, fb), BF16),
                   jax.ShapeDtypeStruct((s, D_MODEL), F32), jax.ShapeDtypeStruct((s, D_MODEL), F32)]
        + ([jax.ShapeDtypeStruct((1, 1), F32)] if last else []),
        compiler_params=_params(),
    )(h1, prew, wup, wdown, postw, *([target] if last else []))


def _mlp_bwd(dh2, d, h1, ab, prew, wup, wdown, postw):
    s = dh2.shape[0]
    fb = D_FF // N_DEV
    tm = TM // 2

    def body(dh2_ref, d_ref, h1_ref, ab_ref, prew_ref, up_ref, down_ref, postw_ref,
             dh1_ref, da_ref, dd_ref, gpost_ref, gpre_ref):
        first = pl.program_id(0) == 0
        dh2 = dh2_ref[...]
        dd, gpost = _rms_bwd(d_ref[...], postw_ref[...], dh2)
        _acc_rows(gpost_ref, gpost, first)
        ddb = dd.astype(BF16)
        dd_ref[...] = ddb

        def d_relu_squared(j):
            return _dot_nt(ddb, down_ref[j])

        def pointwise(j, dr):
            da = (dr * (2.0 * ab_ref[j].astype(F32))).astype(BF16)
            da_ref[j] = da
            return da

        dm = jnp.zeros((tm, D_MODEL), F32)
        nxt, da_prev = d_relu_squared(0), None
        for j in range(N_DEV):
            cur = nxt
            if j + 1 < N_DEV:
                nxt = d_relu_squared(j + 1)
            da = pointwise(j, cur)
            if da_prev is not None:
                dm = dm + _dot_nt(da_prev, up_ref[j - 1])
            da_prev = da
        dm = dm + _dot_nt(da_prev, up_ref[N_DEV - 1])
        dx, gpre = _rms_bwd(h1_ref[...], prew_ref[...], dm)
        _acc_rows(gpre_ref, gpre, first)
        dh1_ref[...] = dh2 + dx

    stacked = pl.BlockSpec((N_DEV, tm, fb), lambda i: (0, i, 0))
    return pl.pallas_call(
        body, name="mlp_bwd", grid=(s // tm,),
        in_specs=[_rows(tm, D_MODEL)] * 3 + [stacked, _full((1, D_MODEL)), _resident((N_DEV, D_MODEL, fb)),
                                              _resident((N_DEV, fb, D_MODEL)), _full((1, D_MODEL))],
        out_specs=[_rows(tm, D_MODEL), stacked, _rows(tm, D_MODEL), _full((1, D_MODEL)), _full((1, D_MODEL))],
        out_shape=[jax.ShapeDtypeStruct((s, D_MODEL), F32), jax.ShapeDtypeStruct((N_DEV, s, fb), BF16),
                   jax.ShapeDtypeStruct((s, D_MODEL), BF16), jax.ShapeDtypeStruct((1, D_MODEL), F32),
                   jax.ShapeDtypeStruct((1, D_MODEL), F32)],
        compiler_params=_params(),
    )(dh2, d, h1, ab, prew, wup, wdown, postw)


def _matmul_tn(a, b, name, tk=TK_DW):
    s, m = a.shape
    n = b.shape[1]
    tn = n if n <= 1024 else (n // 2 if (n // 2) % 128 == 0 else n // 3)
    tk = min(tk, s)
    assert n % tn == 0 and tn % 128 == 0 and s % tk == 0

    def body(a_ref, b_ref, o_ref):
        part = _dot_tn(a_ref[...], b_ref[...])

        @pl.when(pl.program_id(1) == 0)
        def _():
            o_ref[...] = part

        @pl.when(pl.program_id(1) != 0)
        def _():
            o_ref[...] += part

    return pl.pallas_call(
        body, name=name, grid=(n // tn, s // tk),
        in_specs=[pl.BlockSpec((tk, m), lambda j, k: (k, 0)), pl.BlockSpec((tk, tn), lambda j, k: (k, j))],
        out_specs=pl.BlockSpec((m, tn), lambda j, k: (0, j)),
        out_shape=jax.ShapeDtypeStruct((m, n), F32),
        compiler_params=_params(),
    )(a, b)


def _matmul_tn_stacked(a, b, name, a_stacked, square_a=False, tk=TK_DW):
    tk = min(tk, a.shape[-2])
    if a_stacked:
        _, s, m = a.shape
        n = b.shape[1]
        in_specs = [pl.BlockSpec((1, tk, m), lambda j, k: (j, k, 0)), pl.BlockSpec((tk, n), lambda j, k: (k, 0))]
    else:
        s, m = a.shape
        n = b.shape[2]
        in_specs = [pl.BlockSpec((tk, m), lambda j, k: (k, 0)), pl.BlockSpec((1, tk, n), lambda j, k: (j, k, 0))]

    nk = s // tk

    def body(a_ref, b_ref, o_ref, acc_s):
        av = a_ref[0] if a_stacked else a_ref[...]
        bv = b_ref[...] if a_stacked else b_ref[0]
        if square_a:
            av = jnp.square(av.astype(F32)).astype(BF16)
        part = _dot_tn(av, bv)
        k = pl.program_id(1)

        @pl.when(k == 0)
        def _():
            acc_s[...] = part

        @pl.when(jnp.logical_and(k != 0, k != nk - 1))
        def _():
            acc_s[...] += part

        @pl.when(k == nk - 1)
        def _():
            o_ref[0] = (part if nk == 1 else acc_s[...] + part).astype(BF16)

    return pl.pallas_call(
        body, name=name, grid=(N_DEV, nk),
        in_specs=in_specs,
        out_specs=pl.BlockSpec((1, m, n), lambda j, k: (j, 0, 0)),
        out_shape=jax.ShapeDtypeStruct((N_DEV, m, n), BF16),
        scratch_shapes=[pltpu.VMEM((m, n), F32)],
        compiler_params=_params(),
    )(a, b)


def _outproj_bwd(dh1, mixed, nw, wout, oe):
    s = dh1.shape[0]
    wide = MLA_HEADS * HEAD_PAD

    def body(dh1_ref, mixed_ref, nw_ref, w_ref, oe_ref, dmix_ref, doe_ref, dy_ref, gnw_ref, delta_ref):
        dmix, gnw = _rms_bwd(mixed_ref[...], nw_ref[...], dh1_ref[...])
        _acc_rows(gnw_ref, gnw, pl.program_id(0) == 0)
        dmb = dmix.astype(BF16)
        dmix_ref[...] = dmb
        doe_ref[...] = _dot_nt(dmb, w_ref[0:wide, :]).astype(BF16)
        dy_ref[...] = _dot_nt(dmb, w_ref[wide:, :])
        ones = jnp.ones((8, HEAD_PAD), BF16)
        for hd in range(MLA_HEADS):
            cols = slice(hd * HEAD_PAD, (hd + 1) * HEAD_PAD)
            prod = oe_ref[:, cols].astype(F32) * doe_ref[:, cols].astype(F32)
            delta_ref[hd] = _dot01(prod, ones, dot=_dot_nt, left=True)

    return pl.pallas_call(
        body, name="outproj_bwd", grid=(s // TM,),
        in_specs=[_rows(TM, D_MODEL), _rows(TM, D_MODEL), _full((1, D_MODEL)), _resident((wide + SSD_INNER, D_MODEL)),
                  _rows(TM, wide)],
        out_specs=[_rows(TM, D_MODEL), _rows(TM, wide), _rows(TM, SSD_INNER), _full((1, D_MODEL)),
                   pl.BlockSpec((MLA_HEADS, 8, TM), lambda i: (0, 0, i))],
        out_shape=[jax.ShapeDtypeStruct((s, D_MODEL), BF16), jax.ShapeDtypeStruct((s, wide), BF16),
                   jax.ShapeDtypeStruct((s, SSD_INNER), F32), jax.ShapeDtypeStruct((1, D_MODEL), F32),
                   jax.ShapeDtypeStruct((MLA_HEADS, 8, s), F32)],
        compiler_params=_params(),
    )(dh1, mixed, nw, wout, oe)


def _attn_bwd(q, k, v, do, lse, delta, exchange=()):
    s = q.shape[0]
    t = ATT_T
    nq = s // t
    pair = 2 * HEAD_PAD
    ne = len(exchange)

    def body(q_ref, k_ref, v_ref, do_ref, lse_ref, delta_ref, *rest):
        e_in, (dq_ref, dk_ref, dv_ref), e_out = rest[:ne], rest[ne:ne + 3], rest[ne + 3:2 * ne + 3]
        dk_s, dv_s, bias_s = rest[2 * ne + 3:2 * ne + 6]
        kb = pl.program_id(1)
        _hosted_comm("exchange", e_in, e_out, rest[2 * ne + 6:],
                     jnp.logical_and(pl.program_id(0) == 0, kb == 0),
                     jnp.logical_and(pl.program_id(0) == MLA_HEADS // 2 - 1, kb == nq - 1))

        @pl.when(jnp.logical_and(pl.program_id(0) == 0, kb == 0))
        def _():
            bias_s[...] = _chunk_bias(t, keys_on_rows=True)

        @pl.when(kb == 0)
        def _():
            dq_ref[...] = jnp.zeros(dq_ref.shape, F32)

        def step(qb, diagonal):
            r0 = pl.multiple_of(qb * t, t)
            for hh in range(2):
                cols = slice(hh * HEAD_PAD, (hh + 1) * HEAD_PAD)
                kk = k_ref[:, cols]
                qq = q_ref[pl.ds(r0, t), cols]
                dd = do_ref[pl.ds(r0, t), cols]
                sc = _dot_nt(kk, qq) * ATT_SCALE_LOG2
                if diagonal:
                    sc = sc + bias_s[...]
                p = jnp.exp2(sc - lse_ref[hh, 0:1, pl.ds(r0, t)])
                dv = _dot(p.astype(BF16), dd)
                dp = _dot_nt(v_ref[:, cols], dd)
                ds = (p * (dp - delta_ref[hh, 0:1, pl.ds(r0, t)]) * ATT_SCALE).astype(BF16)
                dk = _dot(ds, qq)
                if diagonal:
                    dv_s[:, cols] = dv
                    dk_s[:, cols] = dk
                else:
                    dv_s[:, cols] += dv
                    dk_s[:, cols] += dk
                dq_ref[pl.ds(r0, t), cols] += _dot_tn(ds, kk)

        def loop(i, c):
            for u in range(ATT_UNROLL):
                step(kb + 1 + u + ATT_UNROLL * i, False)
            return c

        step(kb, True)
        later_tiles = nq - 1 - kb
        lax.fori_loop(0, later_tiles // ATT_UNROLL, loop, 0)
        left = later_tiles % ATT_UNROLL
        for u in range(ATT_UNROLL - 1):
            @pl.when(left > u)
            def _(u=u):
                step(nq - left + u, False)

        dk_ref[...] = dk_s[...].astype(BF16)
        dv_ref[...] = dv_s[...].astype(BF16)

    whole = pl.BlockSpec((s, pair), lambda h, i: (0, h))
    tile = pl.BlockSpec((t, pair), lambda h, i: (i, h))
    rowvec = pl.BlockSpec((2, 8, s), lambda h, i: (h, 0, 0))
    wide = MLA_HEADS * HEAD_PAD
    outs = pl.pallas_call(
        body, name="attn_bwd_exchange" if ne else "attn_bwd", grid=(MLA_HEADS // 2, nq),
        in_specs=[whole, tile, tile, whole, rowvec, rowvec] + [_ANY] * ne,
        out_specs=[whole, tile, tile] + [_ANY] * ne,
        out_shape=[jax.ShapeDtypeStruct((s, wide), F32)] + [jax.ShapeDtypeStruct((s, wide), BF16)] * 2
        + _comm_out_shapes("exchange", exchange),
        scratch_shapes=[pltpu.VMEM((t, pair), F32), pltpu.VMEM((t, pair), F32), pltpu.VMEM((t, t), F32)]
        + (_comm_scratch(ne) if ne else []),
        compiler_params=_params(),
    )(q, k, v, do, lse, delta, *exchange)
    return outs[0], outs[1], outs[2], list(outs[3:])


def _ssd_bwd(dy, ypre, z, c, xraw, misc, prev, cw, dtb, a_exp, d_exp, nw, consts):
    s = dy.shape[0]
    nb = s // SSD_ROWS
    ncb = SSD_ROWS // CHUNK
    emisc, emisc_t, tri, trit = consts

    def body(dy_ref, ypre_ref, z_ref, c_ref, x_ref, misc_ref, prev_ref, cw_ref, dtb_ref, a_ref, d_ref, nw_ref,
             emisc_ref, emisct_ref, tri_ref, trit_ref,
             dz_ref, dx_ref, dmisc_ref, gnw_ref, gd_ref, galog_ref, gdtb_ref, gcw_ref, gcb_ref,
             dst_s, dc_s, head_s):
        i = pl.program_id(0)
        first = i == 0

        @pl.when(first)
        def _():
            dst_s[...] = jnp.zeros(dst_s.shape, F32)
            head_s[...] = jnp.zeros(head_s.shape, F32)
            gnw_ref[...] = jnp.zeros(gnw_ref.shape, F32)
            gd_ref[...] = jnp.zeros(gd_ref.shape, F32)
            galog_ref[...] = jnp.zeros(galog_ref.shape, F32)
            gdtb_ref[...] = jnp.zeros(gdtb_ref.shape, F32)

        a_exp_v = a_ref[...]
        a8 = _dot01(a_exp_v, emisct_ref[...]) * (1.0 / SSD_P)

        def chunk(ci):
            r0 = ci * CHUNK
            cc = c_ref[pl.ds(r0, CHUNK), :]
            mm = misc_ref[pl.ds(r0, CHUNK), :]
            xa, sig_c, dt, acs, acs_t, alast = _ssd_chunk_common(cc, mm, emisc_ref[...], tri_ref[...], trit_ref[...],
                                                              dtb_ref[...], a_exp_v)
            yield
            xs = xa[:, :SSD_INNER]
            xdt = xs * dt
            y = ypre_ref[pl.ds(r0, CHUNK), :]
            zz = z_ref[pl.ds(r0, CHUNK), :]
            sg, yn, rs = _gate_norm(y, zz)
            dyo = dy_ref[pl.ds(r0, CHUNK), :]
            gnw_ref[...] += jnp.sum(dyo * yn, axis=0, keepdims=True)
            dyn = dyo * nw_ref[...]
            half = SSD_INNER // SSD_GROUPS
            dyz_parts = []
            for g in range(SSD_GROUPS):
                gl = slice(g * half, (g + 1) * half)
                dyz_parts.append(rs[g] * (dyn[:, gl] - yn[:, gl] * jnp.mean(dyn[:, gl] * yn[:, gl], axis=-1, keepdims=True)))
            dyz = jnp.concatenate(dyz_parts, axis=1)
            dz_ref[pl.ds(r0, CHUNK), :] = dyz * y * (sg * (1.0 + zz * (1.0 - sg)))
            dyp = dyz * (zz * sg)
            dypb = dyp.astype(BF16)
            gd_ref[...] += jnp.sum(dyp * xs, axis=0, keepdims=True)
            yield
            prev = prev_ref[ci]
            cd = jnp.exp(alast)
            e = jnp.exp(acs)
            dsx = jnp.exp(alast - acs)
            wgt = (xdt * dsx).astype(BF16)
            dze = (dyp * e).astype(BF16)
            dprev_parts, diag_all, dbm, dcm, yoff_parts, bms = [], [], [], [], [], []
            lane8 = lax.broadcasted_iota(jnp.int32, (CHUNK, HEAD_PAD), 1)
            diag8 = jnp.zeros((CHUNK, HEAD_PAD), F32)
            for g in range(SSD_GROUPS):
                gl = slice(g * 256, (g + 1) * 256)
                bm = xa[:, SSD_INNER + g * SSD_N:SSD_INNER + (g + 1) * SSD_N].astype(BF16)
                cm = xa[:, SSD_INNER + SSD_GROUPS * SSD_N + g * SSD_N:SSD_INNER + SSD_GROUPS * SSD_N + (g + 1) * SSD_N].astype(BF16)
                bms.append(bm)
                prev_g = prev[:, gl].astype(BF16)
                dcm_g = _dot_nt(dze[:, gl], prev_g)
                dprev_parts.append(_dot_tn(cm, dze[:, gl]))
                cb_g = _dot_nt(cm, bm)
                dcb = jnp.zeros((CHUNK, CHUNK), F32)
                diag_parts = []
                for jj in range(2):
                    pair = 2 * g + jj
                    pl_ = slice(pair * 128, (pair + 1) * 128)
                    xp = xdt[:, pl_]
                    dyp_p = dypb[:, pl_]
                    dxp = jnp.zeros((CHUNK, 128), F32)
                    for hh in range(2):
                        hd = 2 * pair + hh
                        dec = _decay(acs, acs_t, hd)
                        xm = jnp.where(_half_mask(hh), xp, 0.0).astype(BF16)
                        dsc = _dot_nt(dyp_p, xm) * dec
                        dcb = dcb + dsc
                        sc = (cb_g * dec).astype(BF16)
                        dxp = dxp + jnp.where(_half_mask(hh), _dot_tn(sc, dyp_p), 0.0)
                        dm = dsc * cb_g
                        diag8 = diag8 + jnp.where(lane8 == MISC_DT + hd, jnp.sum(dm - dm.T, axis=1, keepdims=True), 0.0)
                    diag_parts.append(dxp)
                dcbb = dcb.astype(BF16)
                dcm.append(dcm_g + _dot(dcbb, bm))
                dbm.append(_dot_tn(dcbb, cm))
                diag_all.append(jnp.concatenate(diag_parts, axis=1))
                yoff_parts.append(_dot(cm, prev_g) * e[:, gl])
                yield
            dst = dst_s[...]
            glast = jnp.sum(dst * prev, axis=0, keepdims=True) * cd
            dxdt_state_parts = []
            for g in range(SSD_GROUPS):
                gl = slice(g * 256, (g + 1) * 256)
                dst_g = dst[:, gl].astype(BF16)
                dxdt_state_parts.append(_dot(bms[g], dst_g) * dsx[:, gl])
                dbm[g] = dbm[g] + _dot_nt(wgt[:, gl], dst_g)
            dst_s[...] = dst * cd + jnp.concatenate(dprev_parts, axis=1)
            yield
            dxdt_state = jnp.concatenate(dxdt_state_parts, axis=1)
            dxdt = jnp.concatenate(diag_all, axis=1) + dxdt_state
            dacs = dyp * jnp.concatenate(yoff_parts, axis=1) - xdt * dxdt_state
            last = jnp.sum(xdt * dxdt_state, axis=0, keepdims=True) + glast
            row = lax.broadcasted_iota(jnp.int32, (CHUNK, SSD_INNER), 0)
            dacs = dacs + jnp.where(row == CHUNK - 1, last, 0.0)
            dacs8 = _dot01(dacs, emisct_ref[...]) + diag8
            da8 = _dot01(dacs8, trit_ref[...], left=True)
            ddt8 = da8 * a8 + _dot01(dxdt * xs, emisct_ref[...])
            yield
            dtr8 = mm + _dot01(dtb_ref[...], emisct_ref[...]) * (1.0 / SSD_P)
            dt8 = jax.nn.softplus(dtr8)
            lane = lax.broadcasted_iota(jnp.int32, (CHUNK, HEAD_PAD), 1)
            on_dt = jnp.logical_and(lane >= MISC_DT, lane < MISC_DT + SSD_HEADS)
            ddtr8 = jnp.where(on_dt, ddt8 * jax.nn.sigmoid(dtr8), 0.0)
            dmisc_ref[pl.ds(r0, CHUNK), :] = ddtr8
            gdtb_ref[...] += jnp.sum(ddtr8, axis=0, keepdims=True)
            galog_ref[...] += jnp.sum(jnp.where(on_dt, da8 * dt8, 0.0), axis=0, keepdims=True) * a8
            dxs = d_ref[...] * dyp + dxdt * dt
            dxa = jnp.concatenate([dxs] + dbm + dcm, axis=1)
            dc_s[pl.ds(r0, CHUNK), :] = dxa * (sig_c * (1.0 + cc * (1.0 - sig_c)))

        _interleave([chunk(ci) for ci in reversed(range(ncb))])

        dc = dc_s[...]
        x = x_ref[...]
        dcext = jnp.concatenate([dc, head_s[...]], axis=0)
        dx = dc * cw_ref[CONV_W - 1:CONV_W, :]
        rows = [jnp.sum(dc * x, axis=0, keepdims=True)]
        for j in range(1, CONV_W):
            ahead = pltpu.roll(dcext, SSD_ROWS + 8 - j, 0)[:SSD_ROWS, :]
            dx = dx + ahead * cw_ref[CONV_W - 1 - j:CONV_W - j, :]
            rows.insert(0, jnp.sum(ahead * x, axis=0, keepdims=True))
        dx_ref[...] = dx
        head_s[...] = dc[:8, :]
        gcw = jnp.concatenate(rows, axis=0)

        @pl.when(first)
        def _():
            gcw_ref[...] = gcw
            gcb_ref[...] = jnp.sum(dc, axis=0, keepdims=True)

        @pl.when(jnp.logical_not(first))
        def _():
            gcw_ref[...] += gcw
            gcb_ref[...] += jnp.sum(dc, axis=0, keepdims=True)

    def rev(width):
        return pl.BlockSpec((SSD_ROWS, width), lambda i: (nb - 1 - i, 0))

    return pl.pallas_call(
        body, name="ssd_bwd", grid=(nb,),
        in_specs=[rev(SSD_INNER), rev(SSD_INNER), rev(SSD_INNER), rev(CONV_DIM), rev(CONV_DIM),
                  rev(HEAD_PAD), pl.BlockSpec((ncb, SSD_N, SSD_INNER), lambda i: (nb - 1 - i, 0, 0)),
                  _full((CONV_W, CONV_DIM)), _full((1, SSD_INNER)), _full((1, SSD_INNER)), _full((1, SSD_INNER)),
                  _full((1, SSD_INNER)), _full((HEAD_PAD, SSD_INNER)), _full((SSD_INNER, HEAD_PAD)), _full((CHUNK, CHUNK)),
                  _full((CHUNK, CHUNK))],
        out_specs=[rev(SSD_INNER), rev(CONV_DIM), rev(HEAD_PAD), _full((1, SSD_INNER)), _full((1, SSD_INNER)),
                   _full((1, HEAD_PAD)), _full((1, HEAD_PAD)), _full((CONV_W, CONV_DIM)), _full((1, CONV_DIM))],
        out_shape=[jax.ShapeDtypeStruct((s, SSD_INNER), F32), jax.ShapeDtypeStruct((s, CONV_DIM), F32),
                   jax.ShapeDtypeStruct((s, HEAD_PAD), F32), jax.ShapeDtypeStruct((1, SSD_INNER), F32),
                   jax.ShapeDtypeStruct((1, SSD_INNER), F32), jax.ShapeDtypeStruct((1, HEAD_PAD), F32),
                   jax.ShapeDtypeStruct((1, HEAD_PAD), F32), jax.ShapeDtypeStruct((CONV_W, CONV_DIM), F32),
                   jax.ShapeDtypeStruct((1, CONV_DIM), F32)],
        scratch_shapes=[pltpu.VMEM((SSD_N, SSD_INNER), F32), pltpu.VMEM((SSD_ROWS, CONV_DIM), F32), pltpu.VMEM((8, CONV_DIM), F32)],
        compiler_params=_params(),
    )(dy, ypre, z, c, xraw, misc, prev, cw, dtb, a_exp, d_exp, nw, emisc, emisc_t, tri, trit)


def _qkv_inproj_bwd(dq, dk, dv, cq, ckv, dmisc_dt, dz, dxbc, h, dh1, qnw, kvnw, nw, wuq, wkv, win, cosf, sinf):
    s = dq.shape[0]
    wide = MLA_HEADS * HEAD_PAD
    tm = TM

    def body(dq_ref, dk_ref, dv_ref, cq_ref, ckv_ref, dmdt_ref, dz_ref, dxbc_ref, h_ref, dh1_ref, qnw_ref, kvnw_ref, nw_ref,
             wuq_ref, wkv_ref, win_ref, cos_ref, sin_ref, dqb_ref, dkvb_ref, dproj_ref, dh0_ref, gq_ref, gkv_ref, gnw_ref):
        first = pl.program_id(0) == 0
        cosf, sinf = cos_ref[...], sin_ref[...]
        dkr = jnp.zeros((tm, HEAD_PAD), F32)
        for hd in range(MLA_HEADS):
            cols = slice(hd * HEAD_PAD, (hd + 1) * HEAD_PAD)
            dqb_ref[:, cols] = _rope(dq_ref[:, cols], cosf, sinf, -1.0).astype(BF16)
            dkh = dk_ref[:, cols]
            dkvb_ref[:, cols] = dkh
            dkr = dkr + dkh
        dkvb_ref[:, wide:] = dv_ref[...]
        lane = lax.broadcasted_iota(jnp.int32, dkr.shape, 1)
        in_rope = jnp.logical_and(lane >= MISC_ROPE, lane < MISC_ROPE + QK_ROPE)
        dmisc_rope = jnp.where(in_rope, _rope(jnp.where(in_rope, dkr, 0.0), cosf, sinf, -1.0), 0.0)
        dcq, gq = _rms_bwd(cq_ref[...], qnw_ref[...], _dot_nt(dqb_ref[...], wuq_ref[...]))
        _acc_rows(gq_ref, gq, first)
        dckv, gkv = _rms_bwd(ckv_ref[...], kvnw_ref[...], _dot_nt(dkvb_ref[...], wkv_ref[...]))
        _acc_rows(gkv_ref, gkv, first)
        dproj_ref[:, 0:768] = dcq.astype(BF16)
        dproj_ref[:, 768:1024] = dckv.astype(BF16)
        dproj_ref[:, 1024:1152] = (dmisc_rope + dmdt_ref[...]).astype(BF16)
        dproj_ref[:, 1152:1664] = dz_ref[...].astype(BF16)
        dproj_ref[:, 1664:2688] = dxbc_ref[...].astype(BF16)
        dx, gnw = _rms_bwd(h_ref[...], nw_ref[...], _dot_nt(dproj_ref[...], win_ref[...]))
        _acc_rows(gnw_ref, gnw, first)
        dh0_ref[...] = dh1_ref[...] + dx

    return pl.pallas_call(
        body, name="qkv_inproj_bwd", grid=(s // tm,),
        in_specs=[_rows(tm, wide)] * 3 + [_rows(tm, Q_RANK), _rows(tm, KV_RANK), _rows(tm, HEAD_PAD), _rows(tm, SSD_INNER),
                                          _rows(tm, CONV_DIM), _rows(tm, D_MODEL), _rows(tm, D_MODEL),
                                          _full((1, Q_RANK)), _full((1, KV_RANK)), _full((1, D_MODEL)),
                                          _resident((Q_RANK, wide)), _resident((KV_RANK, 2 * wide)), _resident((D_MODEL, IN_PAD)),
                                          _rows(tm, HEAD_PAD), _rows(tm, HEAD_PAD)],
        out_specs=[_rows(tm, wide), _rows(tm, 2 * wide), _rows(tm, IN_PAD), _rows(tm, D_MODEL),
                   _full((1, Q_RANK)), _full((1, KV_RANK)), _full((1, D_MODEL))],
        out_shape=[jax.ShapeDtypeStruct((s, wide), BF16), jax.ShapeDtypeStruct((s, 2 * wide), BF16),
                   jax.ShapeDtypeStruct((s, IN_PAD), BF16), jax.ShapeDtypeStruct((s, D_MODEL), F32),
                   jax.ShapeDtypeStruct((1, Q_RANK), F32), jax.ShapeDtypeStruct((1, KV_RANK), F32),
                   jax.ShapeDtypeStruct((1, D_MODEL), F32)],
        compiler_params=_params(),
    )(dq, dk, dv, cq, ckv, dmisc_dt, dz, dxbc, h, dh1, qnw, kvnw, nw, wuq, wkv, win, cosf, sinf)


def _row_tile(rows, cols):
    cap = max(8, (1 << 18) // max(cols, 128))
    best = None
    for t in range(8, rows + 1, 8):
        if rows % t == 0 and t <= cap:
            best = t
    return best if best is not None else rows


def _adamw(w, g, m, v, name):
    rows, cols = w.shape
    tr = _row_tile(rows, cols)

    def body(w_ref, g_ref, m_ref, v_ref, d_ref, m2_ref, v2_ref):
        gg = g_ref[...]
        m2 = ADAM_B1 * m_ref[...] + (1.0 - ADAM_B1) * gg
        v2 = ADAM_B2 * v_ref[...] + (1.0 - ADAM_B2) * jnp.square(gg)
        m_hat = m2 / (1.0 - ADAM_B1 ** ADAM_STEP)
        v_hat = v2 / (1.0 - ADAM_B2 ** ADAM_STEP)
        d_ref[...] = -ADAM_LR * (m_hat / (jnp.sqrt(v_hat) + ADAM_EPS) + ADAM_WD * w_ref[...])
        m2_ref[...] = m2
        v2_ref[...] = v2

    spec = pl.BlockSpec((tr, cols), lambda i: (i, 0))
    return pl.pallas_call(
        body, name=name, grid=(rows // tr,),
        in_specs=[spec] * 4, out_specs=[spec] * 3,
        out_shape=[jax.ShapeDtypeStruct((rows, cols), F32)] * 3,
    )(w, g, m, v)


def _sum_adamw(slots, w, m, v, name):
    _, rows, cols = w.shape
    tr = _row_tile(rows, cols)
    nb = rows // tr

    def body(s0_ref, s1_ref, w_ref, m_ref, v_ref, g_ref, d_ref, m2_ref, v2_ref):
        for l, ref in enumerate((s0_ref, s1_ref)):
            @pl.when(pl.program_id(0) == l)
            def _(ref=ref):
                acc = ref[0].astype(F32)
                for i in range(1, N_DEV):
                    acc = acc + ref[i].astype(F32)
                g_ref[...] = acc

        gg = g_ref[...]
        m2 = ADAM_B1 * m_ref[...] + (1.0 - ADAM_B1) * gg
        v2 = ADAM_B2 * v_ref[...] + (1.0 - ADAM_B2) * jnp.square(gg)
        m_hat = m2 / (1.0 - ADAM_B1 ** ADAM_STEP)
        v_hat = v2 / (1.0 - ADAM_B2 ** ADAM_STEP)
        d_ref[...] = -ADAM_LR * (m_hat / (jnp.sqrt(v_hat) + ADAM_EPS) + ADAM_WD * w_ref[...])
        m2_ref[...] = m2
        v2_ref[...] = v2

    slot_spec = lambda layer: pl.BlockSpec((N_DEV, tr, cols), lambda l, i: (0, jnp.where(l == layer, i, (nb - 1) * (1 - layer)), 0))
    spec = pl.BlockSpec((None, tr, cols), lambda l, i: (l, i, 0))
    return pl.pallas_call(
        body, name=name, grid=(DEPTH, nb),
        in_specs=[slot_spec(0), slot_spec(1), spec, spec, spec], out_specs=[spec] * 4,
        out_shape=[jax.ShapeDtypeStruct(w.shape, F32)] * 4,
        compiler_params=_params(),
    )(slots[0], slots[1], w, m, v)


_MESH = pl.DeviceIdType.MESH
_ANY = pl.BlockSpec(memory_space=pl.ANY)


def _my_place():
    return lax.axis_index("x"), lax.axis_index("y"), lax.axis_index("c")


def _flip(place, k):
    x, y, c = place
    return (1 - x if k & 4 else x, 1 - y if k & 2 else y, 1 - c if k & 1 else c)


def _block_id(place):
    return 4 * place[0] + 2 * place[1] + place[2]


def _peer_copies(kind, in_refs, out_refs, send_sems, recv_sems, local_sems):
    me = _my_place()
    my = _block_id(me)
    remote, local = [], []
    for a, (x_ref, out_ref) in enumerate(zip(in_refs, out_refs)):
        src_of = (lambda place, r=x_ref: r) if kind == "gather" else (lambda place, r=x_ref: r.at[_block_id(place)])
        local.append(pltpu.make_async_copy(src_of(me), out_ref.at[my], local_sems.at[a]))
        for k in range(1, N_DEV):
            peer = _flip(me, k)
            remote.append(pltpu.make_async_remote_copy(
                src_ref=src_of(peer), dst_ref=out_ref.at[my], send_sem=send_sems.at[a * 7 + k - 1],
                recv_sem=recv_sems.at[a * 7 + k - 1], device_id=peer, device_id_type=_MESH))
    return remote, local


def _comm_out_shapes(kind, arrays):
    return [jax.ShapeDtypeStruct((N_DEV, *a.shape) if kind == "gather" else a.shape, a.dtype) for a in arrays]


def _comm_scratch(n):
    return [pltpu.SemaphoreType.DMA((7 * n,)), pltpu.SemaphoreType.DMA((7 * n,)), pltpu.SemaphoreType.DMA((n,))]


def _hosted_comm(kind, in_refs, out_refs, sems, first, last):
    if not in_refs:
        return

    @pl.when(first)
    def _():
        remote, local = _peer_copies(kind, in_refs, out_refs, *sems)
        for cp in local + remote:
            cp.start()

    @pl.when(last)
    def _():
        remote, local = _peer_copies(kind, in_refs, out_refs, *sems)
        for cp in remote:
            cp.wait()
        for cp in local:
            cp.wait()


def _two_level_gather_steps(in_refs, out_refs, send_sems, recv_sems, local_sems):
    n = len(in_refs)
    me = _my_place()
    x, y, c = me
    sibling = (x, y, 1 - c)
    chips = [(1 - x, y), (x, 1 - y), (1 - x, 1 - y)]

    def copy(a, k, place, to, src=None):
        block = out_refs[a].at[_block_id(place)]
        return pltpu.make_async_remote_copy(
            src_ref=block if src is None else src, dst_ref=block, send_sem=send_sems.at[7 * a + k],
            recv_sem=recv_sems.at[7 * a + k], device_id=to, device_id_type=_MESH)

    mine = [pltpu.make_async_copy(in_refs[a], out_refs[a].at[_block_id(me)], local_sems.at[a]) for a in range(n)]
    first = [copy(a, 0, me, sibling, src=in_refs[a]) for a in range(n)]
    first += [copy(a, 1 + j, me, (*chip, c), src=in_refs[a]) for a in range(n) for j, chip in enumerate(chips)]
    passed = [copy(a, 4 + j, (*chip, c), sibling) for a in range(n) for j, chip in enumerate(chips)]

    def send():
        for cp in mine + first:
            cp.start()

    def forward():
        for a in range(n):
            for j, chip in enumerate(chips):
                copy(a, 1 + j, (*chip, c), me).wait_recv()
                passed[3 * a + j].start()

    def finish():
        for a in range(n):
            copy(a, 0, sibling, me).wait_recv()
            for j, chip in enumerate(chips):
                copy(a, 4 + j, (*chip, 1 - c), me).wait_recv()
        for cp in first + passed:
            cp.wait_send()
        for cp in mine:
            cp.wait()

    return send, forward, finish


def _gather_two_level(arrays, name):
    n = len(arrays)

    def body(*refs):
        for step in _two_level_gather_steps(refs[:n], refs[n:2 * n], *refs[2 * n:]):
            step()

    return pl.pallas_call(
        body, name=name, out_shape=_comm_out_shapes("gather", arrays),
        in_specs=[_ANY] * n, out_specs=[_ANY] * n, scratch_shapes=_comm_scratch(n),
    )(*arrays)


def _hosted_gather(in_refs, out_refs, sems, first, middle, last):
    if not in_refs:
        return
    for when, index in ((first, 0), (middle, 1), (last, 2)):
        @pl.when(when)
        def _(index=index):
            _two_level_gather_steps(in_refs, out_refs, *sems)[index]()


def _comm(kind, arrays, name):
    n = len(arrays)

    def body(*refs):
        remote, local = _peer_copies(kind, refs[:n], refs[n:2 * n], *refs[2 * n:])
        for cp in local + remote:
            cp.start()
        for cp in remote:
            cp.wait()
        for cp in local:
            cp.wait()

    return pl.pallas_call(
        body, name=name, out_shape=_comm_out_shapes(kind, arrays),
        in_specs=[_ANY] * n, out_specs=[_ANY] * n, scratch_shapes=_comm_scratch(n),
    )(*arrays)


def _all_reduce_small(part):
    rows, lanes = part.shape
    vmem = pl.BlockSpec(memory_space=pltpu.VMEM)

    def body(x_ref, gath_ref, sum_ref, send_sems, recv_sems):
        me = _my_place()
        my = _block_id(me)
        gath_ref[my] = x_ref[...]
        copies = []
        for k in range(1, N_DEV):
            cp = pltpu.make_async_remote_copy(
                src_ref=x_ref, dst_ref=gath_ref.at[my], send_sem=send_sems.at[k - 1], recv_sem=recv_sems.at[k - 1],
                device_id=_flip(me, k), device_id_type=_MESH)
            cp.start()
            copies.append(cp)
        for cp in copies:
            cp.wait()
        acc = gath_ref[0]
        for i in range(1, N_DEV):
            acc = acc + gath_ref[i]
        sum_ref[...] = acc

    return pl.pallas_call(
        body, name="small_grad_all_reduce",
        out_shape=[jax.ShapeDtypeStruct((N_DEV, rows, lanes), F32), jax.ShapeDtypeStruct((rows, lanes), F32)],
        in_specs=[vmem], out_specs=[vmem, vmem],
        scratch_shapes=[pltpu.SemaphoreType.DMA((7,)), pltpu.SemaphoreType.DMA((7,))],
    )(part)[1]


_SHARDED = (("w_in", (D_MODEL, IN_PROJ // N_DEV)), ("w_uq", (Q_RANK // N_DEV, Q_RANK)), ("w_ukv", (KV_RANK, HEAD_PAD)),
            ("conv_w", (CONV_W, CONV_DIM // N_DEV)), ("w_out", (D_MODEL // N_DEV, D_MODEL)),
            ("w_up", (D_MODEL, D_FF // N_DEV)), ("w_down", (D_FF // N_DEV, D_MODEL)))
_SMALL = (("pre_mix_norm", D_MODEL), ("q_norm", Q_RANK), ("kv_norm", KV_RANK), ("conv_b", CONV_DIM), ("dt_bias", SSD_HEADS),
          ("a_log", SSD_HEADS), ("d_skip", SSD_HEADS), ("ssd_norm", SSD_INNER), ("post_mix_norm", D_MODEL),
          ("pre_mlp_norm", D_MODEL), ("post_mlp_norm", D_MODEL))
_WEIGHT_ORDER = ("pre_mix_norm", "w_in", "q_norm", "w_uq", "kv_norm", "w_ukv", "conv_w", "conv_b", "dt_bias", "a_log", "d_skip",
                 "ssd_norm", "w_out", "post_mix_norm", "pre_mlp_norm", "w_up", "w_down", "post_mlp_norm")
_EARLY = ("w_in", "w_uq", "w_ukv", "conv_w")
_LATE = ("w_out", "w_up", "w_down")


def _wire_shard(name, a):
    return lax.bitcast_convert_type(a, BF16).reshape(CONV_W, -1) if name == "conv_w" else a.astype(BF16)


def _from_wire(name, g):
    return lax.bitcast_convert_type(g.reshape(N_DEV, CONV_W, -1, 2), F32) if name == "conv_w" else g


def _cols(stacked):
    return jnp.transpose(stacked, (1, 0, 2)).reshape(stacked.shape[1], -1)


def _win_segments():
    s2, s3, s5 = Q_RANK + KV_RANK, Q_RANK + KV_RANK + QK_ROPE, IN_PROJ - SSD_HEADS
    return [(0, s2), (None, MISC_ROPE), (s2, s3), (s5, IN_PROJ), (None, HEAD_PAD - MISC_DT - SSD_HEADS), (s3, s5)]


def _win_from_shards(stacked):
    per = IN_PROJ // N_DEV
    parts = []
    for start, stop in _win_segments():
        if start is None:
            parts.append(jnp.zeros((D_MODEL, stop), stacked.dtype))
            continue
        while start < stop:
            j, a = divmod(start, per)
            b = min(per, a + stop - start)
            parts.append(stacked[j, :, a:b])
            start += b - a
    return jnp.concatenate(parts, axis=1)


def _win_grad_shards(dwin):
    per = IN_PROJ // N_DEV
    runs, at = [], 0
    for start, stop in _win_segments():
        if start is not None:
            runs.append((start, stop, at))
        at += stop if start is None else stop - start
    blocks = []
    for j in range(N_DEV):
        lo, hi = j * per, (j + 1) * per
        parts = [dwin[:, p + max(lo, a) - a:p + min(hi, b) - a] for a, b, p in sorted(runs) if max(lo, a) < min(hi, b)]
        blocks.append(jnp.concatenate(parts, axis=1))
    return jnp.stack(blocks)


def _early_weights(sh):
    win = _win_from_shards(sh["w_in"])
    w_uq = sh["w_uq"].reshape(Q_RANK, MLA_HEADS, QK_NOPE + QK_ROPE)
    wuq = jnp.pad(w_uq, ((0, 0), (0, 0), (0, HEAD_PAD - QK_NOPE - QK_ROPE))).reshape(Q_RANK, -1)
    w_ukv = _cols(sh["w_ukv"]).reshape(KV_RANK, MLA_HEADS, QK_NOPE + V_DIM)
    wkn = jnp.pad(w_ukv[..., :QK_NOPE], ((0, 0), (0, 0), (0, HEAD_PAD - QK_NOPE))).reshape(KV_RANK, -1)
    wv = w_ukv[..., QK_NOPE:].reshape(KV_RANK, 4, 2, 1, V_DIM) * jnp.eye(2, dtype=BF16).reshape(1, 1, 2, 2, 1)
    wkv = jnp.concatenate([wkn, wv.reshape(KV_RANK, -1)], axis=1)
    return dict(win=win, wuq=wuq, wkv=wkv, conv_w=_cols(sh["conv_w"]))


def _late_weights(sh):
    w_out = sh["w_out"].reshape(D_MODEL, D_MODEL)
    watt = w_out[:SSD_INNER].reshape(4, 2, 1, V_DIM, D_MODEL) * jnp.eye(2, dtype=BF16).reshape(1, 2, 2, 1, 1)
    wout = jnp.concatenate([watt.reshape(MLA_HEADS * HEAD_PAD, D_MODEL), w_out[SSD_INNER:]], axis=0)
    return dict(wout=wout, wup=sh["w_up"], wdown=sh["w_down"])


def _shard_grads(g):
    out = {}
    if "wup" in g:
        out["w_up"], out["w_down"] = g["wup"], g["wdown"]
        ae = g["wout_att"].reshape(4, 2, 2, V_DIM, D_MODEL)
        att = jnp.stack([ae[:, 0, 0], ae[:, 1, 1]], axis=1).reshape(SSD_INNER, D_MODEL)
        out["w_out"] = jnp.concatenate([att, g["wout_ssd"]], axis=0).astype(BF16).reshape(N_DEV, D_MODEL // N_DEV, D_MODEL)
    if "win" not in g:
        return out
    out["w_in"] = _win_grad_shards(g["win"].astype(BF16))
    w_uq = g["wuq"].astype(BF16).reshape(Q_RANK, MLA_HEADS, HEAD_PAD)[..., :QK_NOPE + QK_ROPE].reshape(Q_RANK, Q_RANK)
    out["w_uq"] = w_uq.reshape(N_DEV, Q_RANK // N_DEV, Q_RANK)
    wide = MLA_HEADS * HEAD_PAD
    wkv = g["wkv"].astype(BF16)
    kn = wkv[:, :wide].reshape(KV_RANK, MLA_HEADS, HEAD_PAD)[..., :QK_NOPE]
    ve = wkv[:, wide:].reshape(KV_RANK, 4, 2, 2, V_DIM)
    vv = jnp.stack([ve[:, :, 0, 0], ve[:, :, 1, 1]], axis=2).reshape(KV_RANK, MLA_HEADS, V_DIM)
    out["w_ukv"] = jnp.transpose(jnp.concatenate([kn, vv], axis=-1), (1, 0, 2))
    out["conv_w"] = jnp.transpose(g["conv_w"].astype(BF16).reshape(CONV_W, N_DEV, -1), (1, 0, 2))
    return out


def _small_rows(n):
    return -(-n // 1024) * 8


def _pack_small(vals):
    rows = []
    for l in range(DEPTH):
        for name, n in _SMALL:
            r = _small_rows(n)
            rows.append(jnp.pad(vals[name][l].reshape(-1), (0, r * 128 - n)).reshape(r, 128))
    return jnp.concatenate(rows, axis=0)


def _unpack_small(packed):
    out, off = {name: [] for name, _ in _SMALL}, 0
    for l in range(DEPTH):
        for name, n in _SMALL:
            r = _small_rows(n)
            out[name].append(packed[off:off + r].reshape(-1)[:n])
            off += r
    return {name: jnp.stack(v) for name, v in out.items()}


def _lane_rows(vec8):
    return jnp.repeat(vec8, SSD_P).reshape(1, SSD_INNER)


def _layer_fwd(h, kw, sm, l, cosf, sinf, consts, gather=(), after_gather=None, target=None):
    row = lambda name: sm[name][l].reshape(1, -1)
    t = {}
    t["h0"] = h
    (t["ub"], t["cq"], t["ckv"], t["misc"], t["z"], t["xraw"], t["cqn"], t["ckvn"], t["q"], t["k"], t["v"]) = _inproj_qkv_fwd(
        h, row("pre_mix_norm"), kw["win"], row("q_norm"), row("kv_norm"), kw["wuq"], kw["wkv"], cosf, sinf)
    t["oe"], t["lse"], gathered = _attn_fwd(t["q"], t["k"], t["v"], gather)
    if after_gather is not None:
        after_gather(gathered)
    t["dtb"] = _lane_rows(sm["dt_bias"][l])
    t["a_exp"] = _lane_rows(-jnp.exp(sm["a_log"][l]))
    t["d_exp"] = _lane_rows(sm["d_skip"][l])
    t["c"], t["prev"], t["ypre"], t["yssd"], t["mixed"], t["h1"] = _ssd_outproj_fwd(
        t["xraw"], t["misc"], t["z"], kw["conv_w"], row("conv_b"), t["dtb"], t["a_exp"], t["d_exp"], row("ssd_norm"), consts,
        t["oe"], h, kw["wout"], row("post_mix_norm"))
    t["mb"], t["ab"], t["d"], *out = _mlp_fwd(t["h1"], row("pre_mlp_norm"), kw["wup"], kw["wdown"], row("post_mlp_norm"), target)
    return out, t


def _layer_bwd(dh2, t, kw, sm, l, cosf, sinf, consts, exchange_of=None):
    row = lambda name: sm[name][l].reshape(1, -1)
    g, gs = {}, {}
    dh1, dab, ddb, gs["post_mlp_norm"], gs["pre_mlp_norm"] = _mlp_bwd(
        dh2, t["d"], t["h1"], t["ab"], row("pre_mlp_norm"), kw["wup"], kw["wdown"], row("post_mlp_norm"))
    g["wup"] = _matmul_tn_stacked(t["mb"], dab, f"dw_up_{l}", a_stacked=False)
    g["wdown"] = _matmul_tn_stacked(t["ab"], ddb, f"dw_down_{l}", a_stacked=True, square_a=True)
    dmixb, doe, dyssd, gs["post_mix_norm"], delta = _outproj_bwd(dh1, t["mixed"], row("post_mix_norm"), kw["wout"], t["oe"])
    g["wout_att"] = _matmul_tn(t["oe"], dmixb, f"dw_out_att_{l}")
    g["wout_ssd"] = _matmul_tn(t["yssd"], dmixb, f"dw_out_ssd_{l}")
    dz, dxraw, dmisc_dt, gs["ssd_norm"], gd, galog, gdtb, g["conv_w"], gs["conv_b"] = _ssd_bwd(
        dyssd, t["ypre"], t["z"], t["c"], t["xraw"], t["misc"], t["prev"], kw["conv_w"], t["dtb"], t["a_exp"], t["d_exp"],
        row("ssd_norm"), consts)
    gs["d_skip"] = jnp.sum(gd.reshape(SSD_HEADS, SSD_P), axis=1)
    gs["a_log"] = galog[0, MISC_DT:MISC_DT + SSD_HEADS]
    gs["dt_bias"] = gdtb[0, MISC_DT:MISC_DT + SSD_HEADS]
    dq, dk, dv, exchanged = _attn_bwd(t["q"], t["k"], t["v"], doe, t["lse"], delta,
                                      exchange_of(g) if exchange_of is not None else ())
    dqb, dkvb, dprojb, dh0, gs["q_norm"], gs["kv_norm"], gs["pre_mix_norm"] = _qkv_inproj_bwd(
        dq, dk, dv, t["cq"], t["ckv"], dmisc_dt, dz, dxraw, t["h0"], dh1, row("q_norm"), row("kv_norm"),
        row("pre_mix_norm"), kw["wuq"], kw["wkv"], kw["win"], cosf, sinf)
    g["wuq"] = _matmul_tn(t["cqn"], dqb, f"dw_uq_{l}")
    g["wkv"] = _matmul_tn(t["ckvn"], dkvb, f"dw_kv_{l}")
    g["win"] = _matmul_tn(t["ub"], dprojb, f"dw_in_{l}")
    return dh0, g, {k: v.reshape(-1) for k, v in gs.items()}, exchanged


def _local_step(x, positions, kws, sm, target, gather=(), after_gather=None, exchange_of=None):
    inv_freq = ROPE_THETA ** (-jnp.arange(0, QK_ROPE, 2, dtype=F32) / QK_ROPE)
    invf = jnp.zeros((HEAD_PAD,), F32).at[MISC_ROPE:MISC_ROPE + QK_ROPE].set(jnp.concatenate([inv_freq, inv_freq]))
    cosf, sinf = _rope_tables(positions.reshape(-1, 1), invf.reshape(1, HEAD_PAD))
    consts = _ssd_consts()
    (h,), t0 = _layer_fwd(x, kws[0], sm, 0, cosf, sinf, consts, gather, after_gather)
    (dh, loss), t1 = _layer_fwd(h, kws[1], sm, 1, cosf, sinf, consts, target=target)
    saved = [t0, t1]
    grads, small, exchanged = [None] * DEPTH, [None] * DEPTH, []
    for l in reversed(range(DEPTH)):
        hook = (lambda g0: exchange_of(g0, grads[1])) if (l == 0 and exchange_of is not None) else None
        dh, grads[l], small[l], got = _layer_bwd(dh, saved[l], kws[l], sm, l, cosf, sinf, consts, hook)
        exchanged = got or exchanged
    return loss[0, 0], dh, grads, small, exchanged


def kernel(x, positions, pre_mix_norm, w_in, q_norm, w_uq, kv_norm, w_ukv, conv_w, conv_b, dt_bias, a_log, d_skip, ssd_norm, w_out, post_mix_norm, pre_mlp_norm, w_up, w_down, post_mlp_norm, loss_target, m_pre_mix_norm, m_w_in, m_q_norm, m_w_uq, m_kv_norm, m_w_ukv, m_conv_w, m_conv_b, m_dt_bias, m_a_log, m_d_skip, m_ssd_norm, m_w_out, m_post_mix_norm, m_pre_mlp_norm, m_w_up, m_w_down, m_post_mlp_norm, v_pre_mix_norm, v_w_in, v_q_norm, v_w_uq, v_kv_norm, v_w_ukv, v_conv_w, v_conv_b, v_dt_bias, v_a_log, v_d_skip, v_ssd_norm, v_w_out, v_post_mix_norm, v_pre_mlp_norm, v_w_up, v_w_down, v_post_mlp_norm):
    w = dict(pre_mix_norm=pre_mix_norm, w_in=w_in, q_norm=q_norm, w_uq=w_uq, kv_norm=kv_norm, w_ukv=w_ukv, conv_w=conv_w,
             conv_b=conv_b, dt_bias=dt_bias, a_log=a_log, d_skip=d_skip, ssd_norm=ssd_norm, w_out=w_out,
             post_mix_norm=post_mix_norm, pre_mlp_norm=pre_mlp_norm, w_up=w_up, w_down=w_down, post_mlp_norm=post_mlp_norm)
    m = dict(pre_mix_norm=m_pre_mix_norm, w_in=m_w_in, q_norm=m_q_norm, w_uq=m_w_uq, kv_norm=m_kv_norm, w_ukv=m_w_ukv,
             conv_w=m_conv_w, conv_b=m_conv_b, dt_bias=m_dt_bias, a_log=m_a_log, d_skip=m_d_skip, ssd_norm=m_ssd_norm,
             w_out=m_w_out, post_mix_norm=m_post_mix_norm, pre_mlp_norm=m_pre_mlp_norm, w_up=m_w_up, w_down=m_w_down,
             post_mlp_norm=m_post_mlp_norm)
    v = dict(pre_mix_norm=v_pre_mix_norm, w_in=v_w_in, q_norm=v_q_norm, w_uq=v_w_uq, kv_norm=v_kv_norm, w_ukv=v_w_ukv,
             conv_w=v_conv_w, conv_b=v_conv_b, dt_bias=v_dt_bias, a_log=v_a_log, d_skip=v_d_skip, ssd_norm=v_ssd_norm,
             w_out=v_w_out, post_mix_norm=v_post_mix_norm, pre_mlp_norm=v_pre_mlp_norm, w_up=v_w_up, w_down=v_w_down,
             post_mlp_norm=v_post_mlp_norm)
    sm = {name: w[name] for name, _ in _SMALL}

    wire = lambda name, l: _wire_shard(name, w[name][l])
    first = _gather_two_level([wire(name, 0) for name in _EARLY], "weight_gather_first")
    kws = [_early_weights({name: _from_wire(name, a) for name, a in zip(_EARLY, first)}), None]
    behind = [(name, 0) for name in _LATE] + [(name, 1) for name, _ in _SHARDED]

    def after_gather(gathered):
        got = {key: _from_wire(key[0], a) for key, a in zip(behind, gathered)}
        kws[0].update(_late_weights({name: got[name, 0] for name in _LATE}))
        kws[1] = {**_early_weights({name: got[name, 1] for name in _EARLY}),
                  **_late_weights({name: got[name, 1] for name in _LATE})}

    sent_behind = [(name, 1) for name, _ in _SHARDED] + [(name, 0) for name in _LATE]

    def exchange_of(g0, g1):
        blocks = {**{(name, 1): a for name, a in _shard_grads(g1).items()},
                  **{(name, 0): a for name, a in _shard_grads(g0).items()}}
        return [blocks[key] for key in sent_behind]

    loss_part, dx, grads, small, exchanged = _local_step(
        x[0], positions[0], kws, sm, loss_target[0], [wire(*key) for key in behind], after_gather, exchange_of)
    slots = dict(zip(sent_behind, exchanged))
    last = _shard_grads({k: grads[0][k] for k in ("win", "wuq", "wkv", "conv_w")})
    slots.update({(name, 0): a for name, a in zip(_EARLY, _comm("exchange", [last[name] for name in _EARLY], "grad_exchange_last"))})
    g_small = _unpack_small(_all_reduce_small(_pack_small({name: jnp.stack([small[l][name] for l in range(DEPTH)])
                                                           for name, _ in _SMALL})))
    loss = lax.psum(loss_part, ("x", "y", "c"))

    grad, delta, new_m, new_v = {}, {}, {}, {}
    for name, _ in _SHARDED:
        grad[name], delta[name], new_m[name], new_v[name] = _sum_adamw(
            [slots[name, 0], slots[name, 1]], w[name], m[name], v[name], f"sum_adamw_{name}")
    pk = lambda d: _pack_small({name: d[name] for name, _ in _SMALL})
    d_, m_, v_ = _adamw(pk(w), pk(g_small), pk(m), pk(v), "adamw_small")
    for dst, packed in ((delta, d_), (new_m, m_), (new_v, v_)):
        dst.update(_unpack_small(packed))
    grad.update(g_small)

    outs = [loss, dx[None]]
    for d in (grad, delta, new_m, new_v):
        outs += [d[name] for name in _WEIGHT_ORDER]
    return tuple(outs)
```

```python
import jax
import jax.numpy as jnp
import numpy as np
from jax import lax
from jax.experimental import pallas as pl
from jax.experimental.pallas import tpu as pltpu

F32 = jnp.float32
BF16 = jnp.bfloat16

D_MODEL = 1024
DEPTH = 2
N_DEV = 8
CHUNK = 64
EPS = 1e-6
MLA_HEADS = 8
QK_NOPE = 64
QK_ROPE = 32
V_DIM = 64
Q_RANK = 768
KV_RANK = 256
ROPE_THETA = 10000.0
SSD_HEADS = 8
SSD_P = 64
SSD_INNER = 512
SSD_GROUPS = 2
SSD_N = 128
CONV_W = 4
CONV_DIM = 1024
D_FF = 4096
IN_PROJ = 2600
HEAD_PAD = 128
IN_PAD = 2688
MISC_ROPE = 64
MISC_DT = 96
ATT_SCALE = (QK_NOPE + QK_ROPE) ** -0.5
LOG2E = 1.4426950408889634
ATT_SCALE_LOG2 = ATT_SCALE * LOG2E

ADAM_LR = 0.001
ADAM_B1 = 0.9
ADAM_B2 = 0.999
ADAM_EPS = 1e-08
ADAM_WD = 0.01
ADAM_STEP = 10

TM = 512
ATT_T = 512
ATT_G = 8
ATT_UNROLL = 4
SSD_ROWS = 512
TK_DW = 4096
V7X_VMEM_BYTES = 64 * 1024 * 1024
VMEM_LIMIT = V7X_VMEM_BYTES - 8 * 1024 * 1024

_NT = (((1,), (1,)), ((), ()))
_TN = (((0,), (0,)), ((), ()))


def _params(vmem_limit=VMEM_LIMIT):
    return pltpu.CompilerParams(vmem_limit_bytes=vmem_limit)


def _dot(a, b, precision=None):
    return jnp.dot(a, b, preferred_element_type=F32, precision=precision)


def _dot_nt(a, b, precision=None):
    return lax.dot_general(a, b, _NT, preferred_element_type=F32, precision=precision)


def _dot_tn(a, b, precision=None):
    return lax.dot_general(a, b, _TN, preferred_element_type=F32, precision=precision)


def _split3(x):
    hi = x.astype(BF16)
    r = x - hi.astype(F32)
    mid = r.astype(BF16)
    return hi, mid, (r - mid.astype(F32)).astype(BF16)


def _dot01(x, m01, dot=_dot, left=False):
    parts = [dot(m01, p) if left else dot(p, m01) for p in _split3(x)]
    return parts[0] + parts[1] + parts[2]


def _full(shape):
    n = len(shape)
    return pl.BlockSpec(shape, lambda *_: (0,) * n)


def _resident(shape):
    n = len(shape)
    return pl.BlockSpec(shape, lambda *_: (0,) * n, pipeline_mode=pl.Buffered(1))


def _rows(tm, width):
    return pl.BlockSpec((tm, width), lambda i: (i, 0))


def _rms_fwd(x, w):
    r = lax.rsqrt(jnp.mean(x * x, axis=-1, keepdims=True) + EPS)
    return (x * r) * w


def _rms_bwd(x, w, dy):
    r = lax.rsqrt(jnp.mean(x * x, axis=-1, keepdims=True) + EPS)
    xh = x * r
    dxn = dy * w
    dx = r * (dxn - xh * jnp.mean(dxn * xh, axis=-1, keepdims=True))
    return dx, dy * xh


def _acc_rows(ref, val, first):
    s = jnp.sum(val, axis=0, keepdims=True)

    @pl.when(first)
    def _():
        ref[...] = s

    @pl.when(jnp.logical_not(first))
    def _():
        ref[...] += s


def _rope(t, cosf, sinf, sign):
    lane = lax.broadcasted_iota(jnp.int32, t.shape, 1)
    rot = jnp.where(lane < MISC_ROPE + QK_ROPE // 2, -pltpu.roll(t, HEAD_PAD - QK_ROPE // 2, 1), pltpu.roll(t, QK_ROPE // 2, 1))
    return t * cosf + sign * (rot * sinf)


def _rope_tables(pos, invf):
    s = pos.shape[0]

    def body(pos_ref, invf_ref, cos_ref, sin_ref):
        ang = pos_ref[...].astype(F32) * invf_ref[...]
        cos_ref[...] = jnp.cos(ang)
        sin_ref[...] = jnp.sin(ang)

    return pl.pallas_call(
        body, name="rope_tables", grid=(s // TM,),
        in_specs=[_rows(TM, 1), _full((1, HEAD_PAD))],
        out_specs=[_rows(TM, HEAD_PAD), _rows(TM, HEAD_PAD)],
        out_shape=[jax.ShapeDtypeStruct((s, HEAD_PAD), F32)] * 2,
    )(pos, invf)


def _inproj_qkv_fwd(h, nw, win, qnw, kvnw, wuq, wkv, cosf, sinf):
    s = h.shape[0]

    def body(h_ref, nw_ref, w_ref, qnw_ref, kvnw_ref, wuq_ref, wkv_ref, cos_ref, sin_ref,
             ub_ref, cq_ref, ckv_ref, misc_ref, z_ref, xbc_ref, cqn_ref, ckvn_ref, q_ref, k_ref, v_ref):
        ub = _rms_fwd(h_ref[...], nw_ref[...]).astype(BF16)
        ub_ref[...] = ub
        proj = _dot(ub, w_ref[...])
        cq, ckv, m = proj[:, 0:768], proj[:, 768:1024], proj[:, 1024:1152]
        cq_ref[...] = cq
        ckv_ref[...] = ckv
        misc_ref[...] = m
        z_ref[...] = proj[:, 1152:1664]
        xbc_ref[...] = proj[:, 1664:2688]
        cosf, sinf = cos_ref[...], sin_ref[...]
        cqn = _rms_fwd(cq, qnw_ref[...]).astype(BF16)
        cqn_ref[...] = cqn
        q = _dot(cqn, wuq_ref[...])
        ckvn = _rms_fwd(ckv, kvnw_ref[...]).astype(BF16)
        ckvn_ref[...] = ckvn
        kv = _dot(ckvn, wkv_ref[...])
        lane = lax.broadcasted_iota(jnp.int32, m.shape, 1)
        in_rope = jnp.logical_and(lane >= MISC_ROPE, lane < MISC_ROPE + QK_ROPE)
        kr = jnp.where(in_rope, _rope(m, cosf, sinf, 1.0), 0.0)
        for hd in range(MLA_HEADS):
            cols = slice(hd * HEAD_PAD, (hd + 1) * HEAD_PAD)
            q_ref[:, cols] = _rope(q[:, cols], cosf, sinf, 1.0).astype(BF16)
            k_ref[:, cols] = (kv[:, cols] + kr).astype(BF16)
        vv = kv[:, MLA_HEADS * HEAD_PAD:]
        vlane = lax.broadcasted_iota(jnp.int32, vv.shape, 1)
        ones_at = jnp.where((vlane // HEAD_PAD) % 2 == 0, V_DIM, 0)
        v_ref[...] = jnp.where(vlane % HEAD_PAD == ones_at, 1.0, vv).astype(BF16)

    wide = MLA_HEADS * HEAD_PAD
    widths = (Q_RANK, KV_RANK, HEAD_PAD, SSD_INNER, CONV_DIM)
    return pl.pallas_call(
        body, name="inproj_qkv_fwd", grid=(s // TM,),
        in_specs=[_rows(TM, D_MODEL), _full((1, D_MODEL)), _resident((D_MODEL, IN_PAD)), _full((1, Q_RANK)), _full((1, KV_RANK)),
                  _resident((Q_RANK, wide)), _resident((KV_RANK, 2 * wide)), _rows(TM, HEAD_PAD), _rows(TM, HEAD_PAD)],
        out_specs=[_rows(TM, D_MODEL)] + [_rows(TM, w) for w in widths]
        + [_rows(TM, Q_RANK), _rows(TM, KV_RANK), _rows(TM, wide), _rows(TM, wide), _rows(TM, wide)],
        out_shape=[jax.ShapeDtypeStruct((s, D_MODEL), BF16)] + [jax.ShapeDtypeStruct((s, w), F32) for w in widths]
        + [jax.ShapeDtypeStruct((s, Q_RANK), BF16), jax.ShapeDtypeStruct((s, KV_RANK), BF16)]
        + [jax.ShapeDtypeStruct((s, wide), BF16)] * 3,
        compiler_params=_params(),
    )(h, nw, win, qnw, kvnw, wuq, wkv, cosf, sinf)


def _chunk_bias(t, keys_on_rows=False):
    row = lax.broadcasted_iota(jnp.int32, (t, 1), 0) // CHUNK
    col = lax.broadcasted_iota(jnp.int32, (1, t), 1) // CHUNK
    return jnp.where((row <= col) if keys_on_rows else (col <= row), 0.0, -jnp.inf).astype(F32)


def _attn_fwd(q, k, v, gather=()):
    s = q.shape[0]
    t = ATT_T
    nq = s // t
    pair = ATT_G * HEAD_PAD
    ng = len(gather)

    def body(q_ref, k_ref, v_ref, *rest):
        g_in, (o_ref, lse_ref), g_out = rest[:ng], rest[ng:ng + 2], rest[ng + 2:2 * ng + 2]
        m_s, acc_s, bias_s = rest[2 * ng + 2:2 * ng + 5]
        qi = pl.program_id(1)
        group, groups = pl.program_id(0), MLA_HEADS // ATT_G

        @pl.when(jnp.logical_and(group == 0, qi == 0))
        def _():
            bias_s[...] = _chunk_bias(t)

        _hosted_gather(g_in, g_out, rest[2 * ng + 5:],
                       jnp.logical_and(group == 0, qi == 0),
                       jnp.logical_and(group == groups - 1, qi == min(3 * nq // 4 + 1, nq - 1)),
                       jnp.logical_and(group == groups - 1, qi == nq - 1))
        m_s[...] = jnp.full(m_s.shape, -jnp.inf, F32)
        acc_s[...] = jnp.zeros(acc_s.shape, F32)

        def step(kb, masked):
            r0 = pl.multiple_of(kb * t, t)

            def scores(hh):
                cols = slice(hh * HEAD_PAD, (hh + 1) * HEAD_PAD)
                return _dot_nt(q_ref[:, cols], k_ref[pl.ds(r0, t), cols])

            def soft(hh, raw):
                sc = raw * ATT_SCALE_LOG2
                if masked:
                    sc = sc + bias_s[...]
                m_old = m_s[hh]
                m_new = jnp.maximum(m_old, jnp.max(sc, axis=-1, keepdims=True))
                alpha = jnp.exp2(m_old - m_new)
                p = jnp.exp2(sc - jnp.tile(m_new, (1, t // HEAD_PAD)))
                m_s[hh] = m_new
                return alpha, p.astype(BF16)

            def update(hh, alpha, p):
                cols = slice(hh * HEAD_PAD, (hh + 1) * HEAD_PAD)
                acc_s[hh] = alpha * acc_s[hh] + _dot(p, v_ref[pl.ds(r0, t), cols])

            raw, ap = [None] * ATT_G, [None] * ATT_G
            raw[0] = scores(0)
            for hh in range(ATT_G):
                if hh + 1 < ATT_G:
                    raw[hh + 1] = scores(hh + 1)
                ap[hh] = soft(hh, raw[hh])
                if hh >= 1:
                    update(hh - 1, *ap[hh - 1])
            update(ATT_G - 1, *ap[ATT_G - 1])

        def loop(i, c):
            step(2 * i, False)
            step(2 * i + 1, False)
            return c

        lax.fori_loop(0, qi // 2, loop, 0)

        @pl.when(qi % 2 == 1)
        def _():
            step(qi - 1, False)

        step(qi, True)
        for hh in range(ATT_G):
            cols = slice(hh * HEAD_PAD, (hh + 1) * HEAD_PAD)
            acc = acc_s[hh]
            ones_at = V_DIM * (1 - hh % 2)
            l = jnp.broadcast_to(acc[:, ones_at:ones_at + 1], acc.shape)
            o_ref[:, cols] = (acc / l).astype(BF16)
            lse_ref[hh] = (m_s[hh] + jnp.log(l) * LOG2E).T[0:8, :]

    outs = pl.pallas_call(
        body, name="attn_fwd_gather" if ng else "attn_fwd", grid=(MLA_HEADS // ATT_G, nq),
        in_specs=[pl.BlockSpec((t, pair), lambda h, i: (i, h)),
                  pl.BlockSpec((s, pair), lambda h, i: (0, h), pipeline_mode=pl.Buffered(1)),
                  pl.BlockSpec((s, pair), lambda h, i: (0, h), pipeline_mode=pl.Buffered(1))] + [_ANY] * ng,
        out_specs=[pl.BlockSpec((t, pair), lambda h, i: (i, h)),
                   pl.BlockSpec((ATT_G, 8, t), lambda h, i: (h, 0, i))] + [_ANY] * ng,
        out_shape=[jax.ShapeDtypeStruct((s, MLA_HEADS * HEAD_PAD), BF16), jax.ShapeDtypeStruct((MLA_HEADS, 8, s), F32)]
        + _comm_out_shapes("gather", gather),
        scratch_shapes=[pltpu.VMEM((ATT_G, t, HEAD_PAD), F32), pltpu.VMEM((ATT_G, t, HEAD_PAD), F32), pltpu.VMEM((t, t), F32)]
        + (_comm_scratch(ng) if ng else []),
        compiler_params=_params(),
    )(q, k, v, *gather)
    return outs[0], outs[1], list(outs[2:])


def _interleave(stages):
    live = list(stages)
    while live:
        still = []
        for g in live:
            try:
                next(g)
                still.append(g)
            except StopIteration:
                pass
        live = still


def _ssd_consts():
    emisc = np.zeros((HEAD_PAD, SSD_INNER), np.float32)
    for hd in range(SSD_HEADS):
        emisc[MISC_DT + hd, hd * SSD_P:(hd + 1) * SSD_P] = 1.0
    idx = np.arange(CHUNK)
    tri = (idx[:, None] >= idx[None, :]).astype(np.float32)
    return tuple(jnp.asarray(m, BF16) for m in (emisc, emisc.T.copy(), tri, tri.T.copy()))


def _ssd_chunk_common(cc, misc, emisc, tri, trit, dtb, a_exp):
    sig = jax.nn.sigmoid(cc)
    xa = cc * sig
    dt = jax.nn.softplus(_dot01(misc, emisc) + dtb)
    a = dt * a_exp
    acs = _dot01(a, tri, left=True)
    acs_t = _dot01(a, trit, dot=_dot_tn)
    alast = acs[CHUNK - 1:CHUNK, :]
    return xa, sig, dt, acs, acs_t, alast


def _decay(acs, acs_t, hd):
    row = lax.broadcasted_iota(jnp.int32, (CHUNK, CHUNK), 0)
    col = lax.broadcasted_iota(jnp.int32, (CHUNK, CHUNK), 1)
    diff = acs[:, hd * SSD_P:hd * SSD_P + 1] - acs_t[hd * SSD_P:hd * SSD_P + 1, :]
    return jnp.exp(jnp.where(row >= col, diff, -jnp.inf))


def _half_mask(hh):
    lane = lax.broadcasted_iota(jnp.int32, (CHUNK, 2 * SSD_P), 1)
    return (lane >= SSD_P) if hh else (lane < SSD_P)


def _gate_norm(y, zz):
    sg = jax.nn.sigmoid(zz)
    yz = y * (zz * sg)
    outs, rs = [], []
    half = SSD_INNER // SSD_GROUPS
    for g in range(SSD_GROUPS):
        yg = yz[:, g * half:(g + 1) * half]
        r = lax.rsqrt(jnp.mean(yg * yg, axis=-1, keepdims=True) + EPS)
        outs.append(yg * r)
        rs.append(r)
    return sg, jnp.concatenate(outs, axis=1), rs


def _ssd_outproj_fwd(xraw, misc, z, cw, cb, dtb, a_exp, d_exp, nw, consts, oe, h, wout, post_w):
    s = xraw.shape[0]
    nb = s // SSD_ROWS
    ncb = SSD_ROWS // CHUNK
    emisc, _, tri, trit = consts
    wide = MLA_HEADS * HEAD_PAD

    def body(x_ref, misc_ref, z_ref, cw_ref, cb_ref, dtb_ref, a_ref, d_ref, nw_ref, emisc_ref, tri_ref, trit_ref,
             oe_ref, h_ref, wout_ref, postw_ref, c_ref, prev_ref, ypre_ref, yssd_ref, mixed_ref, h1_ref, tail_s, state_s):
        i = pl.program_id(0)

        @pl.when(i == 0)
        def _():
            tail_s[...] = jnp.zeros(tail_s.shape, F32)
            state_s[...] = jnp.zeros(state_s.shape, F32)

        mixed_att = _dot(oe_ref[...], wout_ref[0:wide, :])
        x = x_ref[...]
        xext = jnp.concatenate([tail_s[...], x], axis=0)
        acc = x * cw_ref[CONV_W - 1:CONV_W, :] + cb_ref[...]
        for j in range(1, CONV_W):
            acc = acc + pltpu.roll(xext, j, 0)[8:, :] * cw_ref[CONV_W - 1 - j:CONV_W - j, :]
        tail_s[...] = x[SSD_ROWS - 8:, :]
        c_ref[...] = acc

        def chunk(ci):
            r0 = ci * CHUNK
            xa, _, dt, acs, acs_t, alast = _ssd_chunk_common(
                c_ref[pl.ds(r0, CHUNK), :], misc_ref[pl.ds(r0, CHUNK), :], emisc_ref[...], tri_ref[...], trit_ref[...],
                dtb_ref[...], a_ref[...])
            yield
            xs = xa[:, :SSD_INNER]
            xdt = xs * dt
            wgt = (xdt * jnp.exp(alast - acs)).astype(BF16)
            e = jnp.exp(acs)
            ys, new_states, cms = [], [], []
            for g in range(SSD_GROUPS):
                bm = xa[:, SSD_INNER + g * SSD_N:SSD_INNER + (g + 1) * SSD_N].astype(BF16)
                cm = xa[:, SSD_INNER + SSD_GROUPS * SSD_N + g * SSD_N:SSD_INNER + SSD_GROUPS * SSD_N + (g + 1) * SSD_N].astype(BF16)
                cms.append(cm)
                cb_g = _dot_nt(cm, bm)
                gl = slice(g * 256, (g + 1) * 256)
                new_states.append(_dot_tn(bm, wgt[:, gl]))
                for jj in range(2):
                    pair = 2 * g + jj
                    xp = xdt[:, pair * 128:(pair + 1) * 128]
                    yp = None
                    for hh in range(2):
                        sc = (cb_g * _decay(acs, acs_t, 2 * pair + hh)).astype(BF16)
                        term = _dot(sc, jnp.where(_half_mask(hh), xp, 0.0).astype(BF16))
                        yp = term if yp is None else yp + term
                    ys.append(yp)
                yield
            prev = state_s[...]
            prev_ref[ci] = prev
            yoff = jnp.concatenate([_dot(cms[g], prev[:, g * 256:(g + 1) * 256].astype(BF16)) for g in range(SSD_GROUPS)],
                                   axis=1) * e
            state_s[...] = prev * jnp.exp(alast) + jnp.concatenate(new_states, axis=1)
            yield
            y = jnp.concatenate(ys, axis=1) + yoff + d_ref[...] * xs
            ypre_ref[pl.ds(r0, CHUNK), :] = y
            _, yn, _ = _gate_norm(y, z_ref[pl.ds(r0, CHUNK), :])
            yssd_ref[pl.ds(r0, CHUNK), :] = (yn * nw_ref[...]).astype(BF16)

        _interleave([chunk(ci) for ci in range(ncb)])
        mixed = mixed_att + _dot(yssd_ref[...], wout_ref[wide:, :])
        mixed_ref[...] = mixed
        h1_ref[...] = h_ref[...] + _rms_fwd(mixed, postw_ref[...])

    return pl.pallas_call(
        body, name="ssd_outproj_fwd", grid=(nb,),
        in_specs=[_rows(SSD_ROWS, CONV_DIM), _rows(SSD_ROWS, HEAD_PAD), _rows(SSD_ROWS, SSD_INNER),
                  _full((CONV_W, CONV_DIM)), _full((1, CONV_DIM)), _full((1, SSD_INNER)), _full((1, SSD_INNER)),
                  _full((1, SSD_INNER)), _full((1, SSD_INNER)), _full((HEAD_PAD, SSD_INNER)), _full((CHUNK, CHUNK)),
                  _full((CHUNK, CHUNK)), _rows(SSD_ROWS, wide), _rows(SSD_ROWS, D_MODEL),
                  _resident((wide + SSD_INNER, D_MODEL)), _full((1, D_MODEL))],
        out_specs=[_rows(SSD_ROWS, CONV_DIM), pl.BlockSpec((ncb, SSD_N, SSD_INNER), lambda i: (i, 0, 0)),
                   _rows(SSD_ROWS, SSD_INNER), _rows(SSD_ROWS, SSD_INNER), _rows(SSD_ROWS, D_MODEL), _rows(SSD_ROWS, D_MODEL)],
        out_shape=[jax.ShapeDtypeStruct((s, CONV_DIM), F32), jax.ShapeDtypeStruct((s // CHUNK, SSD_N, SSD_INNER), F32),
                   jax.ShapeDtypeStruct((s, SSD_INNER), F32), jax.ShapeDtypeStruct((s, SSD_INNER), BF16),
                   jax.ShapeDtypeStruct((s, D_MODEL), F32), jax.ShapeDtypeStruct((s, D_MODEL), F32)],
        scratch_shapes=[pltpu.VMEM((8, CONV_DIM), F32), pltpu.VMEM((SSD_N, SSD_INNER), F32)],
        compiler_params=_params(),
    )(xraw, misc, z, cw, cb, dtb, a_exp, d_exp, nw, emisc, tri, trit, oe, h, wout, post_w)


def _mlp_fwd(h1, prew, wup, wdown, postw, target=None):
    s = h1.shape[0]
    fb = D_FF // N_DEV
    last = target is not None

    def body(h_ref, prew_ref, up_ref, down_ref, postw_ref, *rest):
        target_ref, (mb_ref, ab_ref, d_ref, out_ref) = (rest[0] if last else None), rest[last:last + 4]
        hh = h_ref[...]
        mb = _rms_fwd(hh, prew_ref[...]).astype(BF16)
        mb_ref[...] = mb
        d = jnp.zeros((TM, D_MODEL), F32)
        for j in range(N_DEV):
            a = jnp.maximum(_dot(mb, up_ref[j]), 0.0)
            ab_ref[j] = a.astype(BF16)
            d = d + _dot(jnp.square(a).astype(BF16), down_ref[j])
        d_ref[...] = d
        h2 = hh + _rms_fwd(d, postw_ref[...])
        if last:
            diff = h2 - target_ref[...]
            out_ref[...] = diff * (1.0 / D_MODEL)
            part = 0.5 * jnp.sum(jnp.mean(diff * diff, axis=-1, keepdims=True), axis=0, keepdims=True)
            _acc_rows(rest[-1], part, pl.program_id(0) == 0)
        else:
            out_ref[...] = h2

    stacked = pl.BlockSpec((N_DEV, TM, fb), lambda i: (0, i, 0))
    return pl.pallas_call(
        body, name="mlp_fwd_loss" if last else "mlp_fwd", grid=(s // TM,),
        in_specs=[_rows(TM, D_MODEL), _full((1, D_MODEL)), _resident((N_DEV, D_MODEL, fb)), _resident((N_DEV, fb, D_MODEL)),
                  _full((1, D_MODEL))] + ([_rows(TM, D_MODEL)] if last else []),
        out_specs=[_rows(TM, D_MODEL), stacked, _rows(TM, D_MODEL), _rows(TM, D_MODEL)] + ([_full((1, 1))] if last else []),
        out_shape=[jax.ShapeDtypeStruct((s, D_MODEL), BF16), jax.ShapeDtypeStruct((N_DEV, s, fb), BF16),
                   jax.ShapeDtypeStruct((s, D_MODEL), F32), jax.ShapeDtypeStruct((s, D_MODEL), F32)]
        + ([jax.ShapeDtypeStruct((1, 1), F32)] if last else []),
        compiler_params=_params(),
    )(h1, prew, wup, wdown, postw, *([target] if last else []))


def _mlp_bwd(dh2, d, h1, ab, prew, wup, wdown, postw):
    s = dh2.shape[0]
    fb = D_FF // N_DEV
    tm = TM // 2

    def body(dh2_ref, d_ref, h1_ref, ab_ref, prew_ref, up_ref, down_ref, postw_ref,
             dh1_ref, da_ref, dd_ref, gpost_ref, gpre_ref):
        first = pl.program_id(0) == 0
        dh2 = dh2_ref[...]
        dd, gpost = _rms_bwd(d_ref[...], postw_ref[...], dh2)
        _acc_rows(gpost_ref, gpost, first)
        ddb = dd.astype(BF16)
        dd_ref[...] = ddb

        def d_relu_squared(j):
            return _dot_nt(ddb, down_ref[j])

        def pointwise(j, dr):
            da = (dr * (2.0 * ab_ref[j].astype(F32))).astype(BF16)
            da_ref[j] = da
            return da

        dm = jnp.zeros((tm, D_MODEL), F32)
        nxt, da_prev = d_relu_squared(0), None
        for j in range(N_DEV):
            cur = nxt
            if j + 1 < N_DEV:
                nxt = d_relu_squared(j + 1)
            da = pointwise(j, cur)
            if da_prev is not None:
                dm = dm + _dot_nt(da_prev, up_ref[j - 1])
            da_prev = da
        dm = dm + _dot_nt(da_prev, up_ref[N_DEV - 1])
        dx, gpre = _rms_bwd(h1_ref[...], prew_ref[...], dm)
        _acc_rows(gpre_ref, gpre, first)
        dh1_ref[...] = dh2 + dx

    stacked = pl.BlockSpec((N_DEV, tm, fb), lambda i: (0, i, 0))
    return pl.pallas_call(
        body, name="mlp_bwd", grid=(s // tm,),
        in_specs=[_rows(tm, D_MODEL)] * 3 + [stacked, _full((1, D_MODEL)), _resident((N_DEV, D_MODEL, fb)),
                                              _resident((N_DEV, fb, D_MODEL)), _full((1, D_MODEL))],
        out_specs=[_rows(tm, D_MODEL), stacked, _rows(tm, D_MODEL), _full((1, D_MODEL)), _full((1, D_MODEL))],
        out_shape=[jax.ShapeDtypeStruct((s, D_MODEL), F32), jax.ShapeDtypeStruct((N_DEV, s, fb), BF16),
                   jax.ShapeDtypeStruct((s, D_MODEL), BF16), jax.ShapeDtypeStruct((1, D_MODEL), F32),
                   jax.ShapeDtypeStruct((1, D_MODEL), F32)],
        compiler_params=_params(),
    )(dh2, d, h1, ab, prew, wup, wdown, postw)


def _matmul_tn(a, b, name, tk=TK_DW):
    s, m = a.shape
    n = b.shape[1]
    tn = n if n <= 1024 else (n // 2 if (n // 2) % 128 == 0 else n // 3)
    tk = min(tk, s)
    assert n % tn == 0 and tn % 128 == 0 and s % tk == 0

    def body(a_ref, b_ref, o_ref):
        part = _dot_tn(a_ref[...], b_ref[...])

        @pl.when(pl.program_id(1) == 0)
        def _():
            o_ref[...] = part

        @pl.when(pl.program_id(1) != 0)
        def _():
            o_ref[...] += part

    return pl.pallas_call(
        body, name=name, grid=(n // tn, s // tk),
        in_specs=[pl.BlockSpec((tk, m), lambda j, k: (k, 0)), pl.BlockSpec((tk, tn), lambda j, k: (k, j))],
        out_specs=pl.BlockSpec((m, tn), lambda j, k: (0, j)),
        out_shape=jax.ShapeDtypeStruct((m, n), F32),
        compiler_params=_params(),
    )(a, b)


def _matmul_tn_stacked(a, b, name, a_stacked, square_a=False, tk=TK_DW):
    tk = min(tk, a.shape[-2])
    if a_stacked:
        _, s, m = a.shape
        n = b.shape[1]
        in_specs = [pl.BlockSpec((1, tk, m), lambda j, k: (j, k, 0)), pl.BlockSpec((tk, n), lambda j, k: (k, 0))]
    else:
        s, m = a.shape
        n = b.shape[2]
        in_specs = [pl.BlockSpec((tk, m), lambda j, k: (k, 0)), pl.BlockSpec((1, tk, n), lambda j, k: (j, k, 0))]

    nk = s // tk

    def body(a_ref, b_ref, o_ref, acc_s):
        av = a_ref[0] if a_stacked else a_ref[...]
        bv = b_ref[...] if a_stacked else b_ref[0]
        if square_a:
            av = jnp.square(av.astype(F32)).astype(BF16)
        part = _dot_tn(av, bv)
        k = pl.program_id(1)

        @pl.when(k == 0)
        def _():
            acc_s[...] = part

        @pl.when(jnp.logical_and(k != 0, k != nk - 1))
        def _():
            acc_s[...] += part

        @pl.when(k == nk - 1)
        def _():
            o_ref[0] = (part if nk == 1 else acc_s[...] + part).astype(BF16)

    return pl.pallas_call(
        body, name=name, grid=(N_DEV, nk),
        in_specs=in_specs,
        out_specs=pl.BlockSpec((1, m, n), lambda j, k: (j, 0, 0)),
        out_shape=jax.ShapeDtypeStruct((N_DEV, m, n), BF16),
        scratch_shapes=[pltpu.VMEM((m, n), F32)],
        compiler_params=_params(),
    )(a, b)


def _attn_bwd(q, k, v, do, lse, delta, exchange=()):
    s = q.shape[0]
    t = ATT_T
    nq = s // t
    pair = 2 * HEAD_PAD
    ne = len(exchange)

    def body(q_ref, k_ref, v_ref, do_ref, lse_ref, delta_ref, *rest):
        e_in, (dq_ref, dk_ref, dv_ref), e_out = rest[:ne], rest[ne:ne + 3], rest[ne + 3:2 * ne + 3]
        dk_s, dv_s, bias_s = rest[2 * ne + 3:2 * ne + 6]
        kb = pl.program_id(1)
        _hosted_comm("exchange", e_in, e_out, rest[2 * ne + 6:],
                     jnp.logical_and(pl.program_id(0) == 0, kb == 0),
                     jnp.logical_and(pl.program_id(0) == MLA_HEADS // 2 - 1, kb == nq - 1))

        @pl.when(jnp.logical_and(pl.program_id(0) == 0, kb == 0))
        def _():
            bias_s[...] = _chunk_bias(t, keys_on_rows=True)

        @pl.when(kb == 0)
        def _():
            dq_ref[...] = jnp.zeros(dq_ref.shape, F32)

        def step(qb, diagonal):
            r0 = pl.multiple_of(qb * t, t)
            for hh in range(2):
                cols = slice(hh * HEAD_PAD, (hh + 1) * HEAD_PAD)
                kk = k_ref[:, cols]
                qq = q_ref[pl.ds(r0, t), cols]
                dd = do_ref[pl.ds(r0, t), cols]
                sc = _dot_nt(kk, qq) * ATT_SCALE_LOG2
                if diagonal:
                    sc = sc + bias_s[...]
                p = jnp.exp2(sc - lse_ref[hh, 0:1, pl.ds(r0, t)])
                dv = _dot(p.astype(BF16), dd)
                dp = _dot_nt(v_ref[:, cols], dd)
                ds = (p * (dp - delta_ref[hh, 0:1, pl.ds(r0, t)]) * ATT_SCALE).astype(BF16)
                dk = _dot(ds, qq)
                if diagonal:
                    dv_s[:, cols] = dv
                    dk_s[:, cols] = dk
                else:
                    dv_s[:, cols] += dv
                    dk_s[:, cols] += dk
                dq_ref[pl.ds(r0, t), cols] += _dot_tn(ds, kk)

        def loop(i, c):
            for u in range(ATT_UNROLL):
                step(kb + 1 + u + ATT_UNROLL * i, False)
            return c

        step(kb, True)
        later_tiles = nq - 1 - kb
        lax.fori_loop(0, later_tiles // ATT_UNROLL, loop, 0)
        left = later_tiles % ATT_UNROLL
        for u in range(ATT_UNROLL - 1):
            @pl.when(left > u)
            def _(u=u):
                step(nq - left + u, False)

        dk_ref[...] = dk_s[...].astype(BF16)
        dv_ref[...] = dv_s[...].astype(BF16)

    whole = pl.BlockSpec((s, pair), lambda h, i: (0, h))
    tile = pl.BlockSpec((t, pair), lambda h, i: (i, h))
    rowvec = pl.BlockSpec((2, 8, s), lambda h, i: (h, 0, 0))
    wide = MLA_HEADS * HEAD_PAD
    outs = pl.pallas_call(
        body, name="attn_bwd_exchange" if ne else "attn_bwd", grid=(MLA_HEADS // 2, nq),
        in_specs=[whole, tile, tile, whole, rowvec, rowvec] + [_ANY] * ne,
        out_specs=[whole, tile, tile] + [_ANY] * ne,
        out_shape=[jax.ShapeDtypeStruct((s, wide), F32)] + [jax.ShapeDtypeStruct((s, wide), BF16)] * 2
        + _comm_out_shapes("exchange", exchange),
        scratch_shapes=[pltpu.VMEM((t, pair), F32), pltpu.VMEM((t, pair), F32), pltpu.VMEM((t, t), F32)]
        + (_comm_scratch(ne) if ne else []),
        compiler_params=_params(),
    )(q, k, v, do, lse, delta, *exchange)
    return outs[0], outs[1], outs[2], list(outs[3:])


def _outproj_ssd_bwd(dh1, mixed, oe, wout, post_w, ypre, z, c, xraw, misc, prev, cw, dtb, a_exp, d_exp, nw, consts):
    s = dh1.shape[0]
    nb = s // SSD_ROWS
    ncb = SSD_ROWS // CHUNK
    emisc, emisc_t, tri, trit = consts
    wide = MLA_HEADS * HEAD_PAD

    def body(dh1_ref, mixed_ref, oe_ref, wout_ref, postw_ref, ypre_ref, z_ref, c_ref, x_ref, misc_ref, prev_ref, cw_ref,
             dtb_ref, a_ref, d_ref, nw_ref, emisc_ref, emisct_ref, tri_ref, trit_ref,
             dmix_ref, doe_ref, delta_ref, dz_ref, dx_ref, dmisc_ref, gpost_ref, gnw_ref, gd_ref, galog_ref, gdtb_ref,
             gcw_ref, gcb_ref, dst_s, dc_s, head_s):
        i = pl.program_id(0)
        first = i == 0

        @pl.when(first)
        def _():
            dst_s[...] = jnp.zeros(dst_s.shape, F32)
            head_s[...] = jnp.zeros(head_s.shape, F32)
            gnw_ref[...] = jnp.zeros(gnw_ref.shape, F32)
            gd_ref[...] = jnp.zeros(gd_ref.shape, F32)
            galog_ref[...] = jnp.zeros(galog_ref.shape, F32)
            gdtb_ref[...] = jnp.zeros(gdtb_ref.shape, F32)

        dmix, gpost = _rms_bwd(mixed_ref[...], postw_ref[...], dh1_ref[...])
        _acc_rows(gpost_ref, gpost, first)
        dmb = dmix.astype(BF16)
        dmix_ref[...] = dmb
        doe_ref[...] = _dot_nt(dmb, wout_ref[0:wide, :]).astype(BF16)
        dc_s[:, 0:SSD_INNER] = _dot_nt(dmb, wout_ref[wide:, :])
        ones = jnp.ones((8, HEAD_PAD), BF16)
        for hd in range(MLA_HEADS):
            cols = slice(hd * HEAD_PAD, (hd + 1) * HEAD_PAD)
            prod = oe_ref[:, cols].astype(F32) * doe_ref[:, cols].astype(F32)
            delta_ref[hd] = _dot01(prod, ones, dot=_dot_nt, left=True)

        a_exp_v = a_ref[...]
        a8 = _dot01(a_exp_v, emisct_ref[...]) * (1.0 / SSD_P)

        def chunk(ci):
            r0 = ci * CHUNK
            cc = c_ref[pl.ds(r0, CHUNK), :]
            mm = misc_ref[pl.ds(r0, CHUNK), :]
            xa, sig_c, dt, acs, acs_t, alast = _ssd_chunk_common(cc, mm, emisc_ref[...], tri_ref[...], trit_ref[...],
                                                              dtb_ref[...], a_exp_v)
            yield
            xs = xa[:, :SSD_INNER]
            xdt = xs * dt
            y = ypre_ref[pl.ds(r0, CHUNK), :]
            zz = z_ref[pl.ds(r0, CHUNK), :]
            sg, yn, rs = _gate_norm(y, zz)
            dyo = dc_s[pl.ds(r0, CHUNK), 0:SSD_INNER]
            gnw_ref[...] += jnp.sum(dyo * yn, axis=0, keepdims=True)
            dyn = dyo * nw_ref[...]
            half = SSD_INNER // SSD_GROUPS
            dyz_parts = []
            for g in range(SSD_GROUPS):
                gl = slice(g * half, (g + 1) * half)
                dyz_parts.append(rs[g] * (dyn[:, gl] - yn[:, gl] * jnp.mean(dyn[:, gl] * yn[:, gl], axis=-1, keepdims=True)))
            dyz = jnp.concatenate(dyz_parts, axis=1)
            dz_ref[pl.ds(r0, CHUNK), :] = dyz * y * (sg * (1.0 + zz * (1.0 - sg)))
            dyp = dyz * (zz * sg)
            dypb = dyp.astype(BF16)
            gd_ref[...] += jnp.sum(dyp * xs, axis=0, keepdims=True)
            yield
            prev = prev_ref[ci]
            cd = jnp.exp(alast)
            e = jnp.exp(acs)
            dsx = jnp.exp(alast - acs)
            wgt = (xdt * dsx).astype(BF16)
            dze = (dyp * e).astype(BF16)
            dprev_parts, diag_all, dbm, dcm, yoff_parts, bms = [], [], [], [], [], []
            lane8 = lax.broadcasted_iota(jnp.int32, (CHUNK, HEAD_PAD), 1)
            diag8 = jnp.zeros((CHUNK, HEAD_PAD), F32)
            for g in range(SSD_GROUPS):
                gl = slice(g * 256, (g + 1) * 256)
                bm = xa[:, SSD_INNER + g * SSD_N:SSD_INNER + (g + 1) * SSD_N].astype(BF16)
                cm = xa[:, SSD_INNER + SSD_GROUPS * SSD_N + g * SSD_N:SSD_INNER + SSD_GROUPS * SSD_N + (g + 1) * SSD_N].astype(BF16)
                bms.append(bm)
                prev_g = prev[:, gl].astype(BF16)
                dcm_g = _dot_nt(dze[:, gl], prev_g)
                dprev_parts.append(_dot_tn(cm, dze[:, gl]))
                cb_g = _dot_nt(cm, bm)
                dcb = jnp.zeros((CHUNK, CHUNK), F32)
                diag_parts = []
                for jj in range(2):
                    pair = 2 * g + jj
                    pl_ = slice(pair * 128, (pair + 1) * 128)
                    xp = xdt[:, pl_]
                    dyp_p = dypb[:, pl_]
                    dxp = jnp.zeros((CHUNK, 128), F32)
                    for hh in range(2):
                        hd = 2 * pair + hh
                        dec = _decay(acs, acs_t, hd)
                        xm = jnp.where(_half_mask(hh), xp, 0.0).astype(BF16)
                        dsc = _dot_nt(dyp_p, xm) * dec
                        dcb = dcb + dsc
                        sc = (cb_g * dec).astype(BF16)
                        dxp = dxp + jnp.where(_half_mask(hh), _dot_tn(sc, dyp_p), 0.0)
                        dm = dsc * cb_g
                        diag8 = diag8 + jnp.where(lane8 == MISC_DT + hd, jnp.sum(dm - dm.T, axis=1, keepdims=True), 0.0)
                    diag_parts.append(dxp)
                dcbb = dcb.astype(BF16)
                dcm.append(dcm_g + _dot(dcbb, bm))
                dbm.append(_dot_tn(dcbb, cm))
                diag_all.append(jnp.concatenate(diag_parts, axis=1))
                yoff_parts.append(_dot(cm, prev_g) * e[:, gl])
                yield
            dst = dst_s[...]
            glast = jnp.sum(dst * prev, axis=0, keepdims=True) * cd
            dxdt_state_parts = []
            for g in range(SSD_GROUPS):
                gl = slice(g * 256, (g + 1) * 256)
                dst_g = dst[:, gl].astype(BF16)
                dxdt_state_parts.append(_dot(bms[g], dst_g) * dsx[:, gl])
                dbm[g] = dbm[g] + _dot_nt(wgt[:, gl], dst_g)
            dst_s[...] = dst * cd + jnp.concatenate(dprev_parts, axis=1)
            yield
            dxdt_state = jnp.concatenate(dxdt_state_parts, axis=1)
            dxdt = jnp.concatenate(diag_all, axis=1) + dxdt_state
            dacs = dyp * jnp.concatenate(yoff_parts, axis=1) - xdt * dxdt_state
            last = jnp.sum(xdt * dxdt_state, axis=0, keepdims=True) + glast
            row = lax.broadcasted_iota(jnp.int32, (CHUNK, SSD_INNER), 0)
            dacs = dacs + jnp.where(row == CHUNK - 1, last, 0.0)
            dacs8 = _dot01(dacs, emisct_ref[...]) + diag8
            da8 = _dot01(dacs8, trit_ref[...], left=True)
            ddt8 = da8 * a8 + _dot01(dxdt * xs, emisct_ref[...])
            yield
            dtr8 = mm + _dot01(dtb_ref[...], emisct_ref[...]) * (1.0 / SSD_P)
            dt8 = jax.nn.softplus(dtr8)
            lane = lax.broadcasted_iota(jnp.int32, (CHUNK, HEAD_PAD), 1)
            on_dt = jnp.logical_and(lane >= MISC_DT, lane < MISC_DT + SSD_HEADS)
            ddtr8 = jnp.where(on_dt, ddt8 * jax.nn.sigmoid(dtr8), 0.0)
            dmisc_ref[pl.ds(r0, CHUNK), :] = ddtr8
            gdtb_ref[...] += jnp.sum(ddtr8, axis=0, keepdims=True)
            galog_ref[...] += jnp.sum(jnp.where(on_dt, da8 * dt8, 0.0), axis=0, keepdims=True) * a8
            dxs = d_ref[...] * dyp + dxdt * dt
            dxa = jnp.concatenate([dxs] + dbm + dcm, axis=1)
            dc_s[pl.ds(r0, CHUNK), :] = dxa * (sig_c * (1.0 + cc * (1.0 - sig_c)))

        _interleave([chunk(ci) for ci in reversed(range(ncb))])

        dc = dc_s[...]
        x = x_ref[...]
        dcext = jnp.concatenate([dc, head_s[...]], axis=0)
        dx = dc * cw_ref[CONV_W - 1:CONV_W, :]
        rows = [jnp.sum(dc * x, axis=0, keepdims=True)]
        for j in range(1, CONV_W):
            ahead = pltpu.roll(dcext, SSD_ROWS + 8 - j, 0)[:SSD_ROWS, :]
            dx = dx + ahead * cw_ref[CONV_W - 1 - j:CONV_W - j, :]
            rows.insert(0, jnp.sum(ahead * x, axis=0, keepdims=True))
        dx_ref[...] = dx
        head_s[...] = dc[:8, :]
        gcw = jnp.concatenate(rows, axis=0)

        @pl.when(first)
        def _():
            gcw_ref[...] = gcw
            gcb_ref[...] = jnp.sum(dc, axis=0, keepdims=True)

        @pl.when(jnp.logical_not(first))
        def _():
            gcw_ref[...] += gcw
            gcb_ref[...] += jnp.sum(dc, axis=0, keepdims=True)

    def rev(width):
        return pl.BlockSpec((SSD_ROWS, width), lambda i: (nb - 1 - i, 0))

    return pl.pallas_call(
        body, name="outproj_ssd_bwd", grid=(nb,),
        in_specs=[rev(D_MODEL), rev(D_MODEL), rev(wide), _resident((wide + SSD_INNER, D_MODEL)), _full((1, D_MODEL)),
                  rev(SSD_INNER), rev(SSD_INNER), rev(CONV_DIM), rev(CONV_DIM),
                  rev(HEAD_PAD), pl.BlockSpec((ncb, SSD_N, SSD_INNER), lambda i: (nb - 1 - i, 0, 0)),
                  _full((CONV_W, CONV_DIM)), _full((1, SSD_INNER)), _full((1, SSD_INNER)), _full((1, SSD_INNER)),
                  _full((1, SSD_INNER)), _full((HEAD_PAD, SSD_INNER)), _full((SSD_INNER, HEAD_PAD)), _full((CHUNK, CHUNK)),
                  _full((CHUNK, CHUNK))],
        out_specs=[rev(D_MODEL), rev(wide), pl.BlockSpec((MLA_HEADS, 8, SSD_ROWS), lambda i: (0, 0, nb - 1 - i)),
                   rev(SSD_INNER), rev(CONV_DIM), rev(HEAD_PAD), _full((1, D_MODEL)), _full((1, SSD_INNER)),
                   _full((1, SSD_INNER)), _full((1, HEAD_PAD)), _full((1, HEAD_PAD)), _full((CONV_W, CONV_DIM)),
                   _full((1, CONV_DIM))],
        out_shape=[jax.ShapeDtypeStruct((s, D_MODEL), BF16), jax.ShapeDtypeStruct((s, wide), BF16),
                   jax.ShapeDtypeStruct((MLA_HEADS, 8, s), F32),
                   jax.ShapeDtypeStruct((s, SSD_INNER), F32), jax.ShapeDtypeStruct((s, CONV_DIM), F32),
                   jax.ShapeDtypeStruct((s, HEAD_PAD), F32), jax.ShapeDtypeStruct((1, D_MODEL), F32),
                   jax.ShapeDtypeStruct((1, SSD_INNER), F32), jax.ShapeDtypeStruct((1, SSD_INNER), F32),
                   jax.ShapeDtypeStruct((1, HEAD_PAD), F32), jax.ShapeDtypeStruct((1, HEAD_PAD), F32),
                   jax.ShapeDtypeStruct((CONV_W, CONV_DIM), F32), jax.ShapeDtypeStruct((1, CONV_DIM), F32)],
        scratch_shapes=[pltpu.VMEM((SSD_N, SSD_INNER), F32), pltpu.VMEM((SSD_ROWS, CONV_DIM), F32), pltpu.VMEM((8, CONV_DIM), F32)],
        compiler_params=_params(VMEM_LIMIT + 4 * 1024 * 1024),
    )(dh1, mixed, oe, wout, post_w, ypre, z, c, xraw, misc, prev, cw, dtb, a_exp, d_exp, nw, emisc, emisc_t, tri, trit)


def _qkv_inproj_bwd(dq, dk, dv, cq, ckv, dmisc_dt, dz, dxbc, h, dh1, qnw, kvnw, nw, wuq, wkv, win, cosf, sinf):
    s = dq.shape[0]
    wide = MLA_HEADS * HEAD_PAD
    tm = TM

    def body(dq_ref, dk_ref, dv_ref, cq_ref, ckv_ref, dmdt_ref, dz_ref, dxbc_ref, h_ref, dh1_ref, qnw_ref, kvnw_ref, nw_ref,
             wuq_ref, wkv_ref, win_ref, cos_ref, sin_ref, dqb_ref, dkvb_ref, dproj_ref, dh0_ref, gq_ref, gkv_ref, gnw_ref):
        first = pl.program_id(0) == 0
        cosf, sinf = cos_ref[...], sin_ref[...]
        dkr = jnp.zeros((tm, HEAD_PAD), F32)
        for hd in range(MLA_HEADS):
            cols = slice(hd * HEAD_PAD, (hd + 1) * HEAD_PAD)
            dqb_ref[:, cols] = _rope(dq_ref[:, cols], cosf, sinf, -1.0).astype(BF16)
            dkh = dk_ref[:, cols]
            dkvb_ref[:, cols] = dkh
            dkr = dkr + dkh
        dkvb_ref[:, wide:] = dv_ref[...]
        lane = lax.broadcasted_iota(jnp.int32, dkr.shape, 1)
        in_rope = jnp.logical_and(lane >= MISC_ROPE, lane < MISC_ROPE + QK_ROPE)
        dmisc_rope = jnp.where(in_rope, _rope(jnp.where(in_rope, dkr, 0.0), cosf, sinf, -1.0), 0.0)
        dcq, gq = _rms_bwd(cq_ref[...], qnw_ref[...], _dot_nt(dqb_ref[...], wuq_ref[...]))
        _acc_rows(gq_ref, gq, first)
        dckv, gkv = _rms_bwd(ckv_ref[...], kvnw_ref[...], _dot_nt(dkvb_ref[...], wkv_ref[...]))
        _acc_rows(gkv_ref, gkv, first)
        dproj_ref[:, 0:768] = dcq.astype(BF16)
        dproj_ref[:, 768:1024] = dckv.astype(BF16)
        dproj_ref[:, 1024:1152] = (dmisc_rope + dmdt_ref[...]).astype(BF16)
        dproj_ref[:, 1152:1664] = dz_ref[...].astype(BF16)
        dproj_ref[:, 1664:2688] = dxbc_ref[...].astype(BF16)
        dx, gnw = _rms_bwd(h_ref[...], nw_ref[...], _dot_nt(dproj_ref[...], win_ref[...]))
        _acc_rows(gnw_ref, gnw, first)
        dh0_ref[...] = dh1_ref[...] + dx

    return pl.pallas_call(
        body, name="qkv_inproj_bwd", grid=(s // tm,),
        in_specs=[_rows(tm, wide)] * 3 + [_rows(tm, Q_RANK), _rows(tm, KV_RANK), _rows(tm, HEAD_PAD), _rows(tm, SSD_INNER),
                                          _rows(tm, CONV_DIM), _rows(tm, D_MODEL), _rows(tm, D_MODEL),
                                          _full((1, Q_RANK)), _full((1, KV_RANK)), _full((1, D_MODEL)),
                                          _resident((Q_RANK, wide)), _resident((KV_RANK, 2 * wide)), _resident((D_MODEL, IN_PAD)),
                                          _rows(tm, HEAD_PAD), _rows(tm, HEAD_PAD)],
        out_specs=[_rows(tm, wide), _rows(tm, 2 * wide), _rows(tm, IN_PAD), _rows(tm, D_MODEL),
                   _full((1, Q_RANK)), _full((1, KV_RANK)), _full((1, D_MODEL))],
        out_shape=[jax.ShapeDtypeStruct((s, wide), BF16), jax.ShapeDtypeStruct((s, 2 * wide), BF16),
                   jax.ShapeDtypeStruct((s, IN_PAD), BF16), jax.ShapeDtypeStruct((s, D_MODEL), F32),
                   jax.ShapeDtypeStruct((1, Q_RANK), F32), jax.ShapeDtypeStruct((1, KV_RANK), F32),
                   jax.ShapeDtypeStruct((1, D_MODEL), F32)],
        compiler_params=_params(),
    )(dq, dk, dv, cq, ckv, dmisc_dt, dz, dxbc, h, dh1, qnw, kvnw, nw, wuq, wkv, win, cosf, sinf)


def _row_tile(rows, cols):
    cap = max(8, (1 << 18) // max(cols, 128))
    best = None
    for t in range(8, rows + 1, 8):
        if rows % t == 0 and t <= cap:
            best = t
    return best if best is not None else rows


def _adamw(w, g, m, v, name):
    rows, cols = w.shape
    tr = _row_tile(rows, cols)

    def body(w_ref, g_ref, m_ref, v_ref, d_ref, m2_ref, v2_ref):
        gg = g_ref[...]
        m2 = ADAM_B1 * m_ref[...] + (1.0 - ADAM_B1) * gg
        v2 = ADAM_B2 * v_ref[...] + (1.0 - ADAM_B2) * jnp.square(gg)
        m_hat = m2 / (1.0 - ADAM_B1 ** ADAM_STEP)
        v_hat = v2 / (1.0 - ADAM_B2 ** ADAM_STEP)
        d_ref[...] = -ADAM_LR * (m_hat / (jnp.sqrt(v_hat) + ADAM_EPS) + ADAM_WD * w_ref[...])
        m2_ref[...] = m2
        v2_ref[...] = v2

    spec = pl.BlockSpec((tr, cols), lambda i: (i, 0))
    return pl.pallas_call(
        body, name=name, grid=(rows // tr,),
        in_specs=[spec] * 4, out_specs=[spec] * 3,
        out_shape=[jax.ShapeDtypeStruct((rows, cols), F32)] * 3,
    )(w, g, m, v)


def _sum_adamw(slots, w, m, v, name):
    _, rows, cols = w.shape
    tr = _row_tile(rows, cols)
    nb = rows // tr

    def body(s0_ref, s1_ref, w_ref, m_ref, v_ref, g_ref, d_ref, m2_ref, v2_ref):
        for l, ref in enumerate((s0_ref, s1_ref)):
            @pl.when(pl.program_id(0) == l)
            def _(ref=ref):
                acc = ref[0].astype(F32)
                for i in range(1, N_DEV):
                    acc = acc + ref[i].astype(F32)
                g_ref[...] = acc

        gg = g_ref[...]
        m2 = ADAM_B1 * m_ref[...] + (1.0 - ADAM_B1) * gg
        v2 = ADAM_B2 * v_ref[...] + (1.0 - ADAM_B2) * jnp.square(gg)
        m_hat = m2 / (1.0 - ADAM_B1 ** ADAM_STEP)
        v_hat = v2 / (1.0 - ADAM_B2 ** ADAM_STEP)
        d_ref[...] = -ADAM_LR * (m_hat / (jnp.sqrt(v_hat) + ADAM_EPS) + ADAM_WD * w_ref[...])
        m2_ref[...] = m2
        v2_ref[...] = v2

    slot_spec = lambda layer: pl.BlockSpec((N_DEV, tr, cols), lambda l, i: (0, jnp.where(l == layer, i, (nb - 1) * (1 - layer)), 0))
    spec = pl.BlockSpec((None, tr, cols), lambda l, i: (l, i, 0))
    return pl.pallas_call(
        body, name=name, grid=(DEPTH, nb),
        in_specs=[slot_spec(0), slot_spec(1), spec, spec, spec], out_specs=[spec] * 4,
        out_shape=[jax.ShapeDtypeStruct(w.shape, F32)] * 4,
        compiler_params=_params(),
    )(slots[0], slots[1], w, m, v)


_MESH = pl.DeviceIdType.MESH
_ANY = pl.BlockSpec(memory_space=pl.ANY)


def _my_place():
    return lax.axis_index("x"), lax.axis_index("y"), lax.axis_index("c")


def _flip(place, k):
    x, y, c = place
    return (1 - x if k & 4 else x, 1 - y if k & 2 else y, 1 - c if k & 1 else c)


def _block_id(place):
    return 4 * place[0] + 2 * place[1] + place[2]


def _peer_copies(kind, in_refs, out_refs, send_sems, recv_sems, local_sems):
    me = _my_place()
    my = _block_id(me)
    remote, local = [], []
    for a, (x_ref, out_ref) in enumerate(zip(in_refs, out_refs)):
        src_of = (lambda place, r=x_ref: r) if kind == "gather" else (lambda place, r=x_ref: r.at[_block_id(place)])
        local.append(pltpu.make_async_copy(src_of(me), out_ref.at[my], local_sems.at[a]))
        for k in range(1, N_DEV):
            peer = _flip(me, k)
            remote.append(pltpu.make_async_remote_copy(
                src_ref=src_of(peer), dst_ref=out_ref.at[my], send_sem=send_sems.at[a * 7 + k - 1],
                recv_sem=recv_sems.at[a * 7 + k - 1], device_id=peer, device_id_type=_MESH))
    return remote, local


def _comm_out_shapes(kind, arrays):
    return [jax.ShapeDtypeStruct((N_DEV, *a.shape) if kind == "gather" else a.shape, a.dtype) for a in arrays]


def _comm_scratch(n):
    return [pltpu.SemaphoreType.DMA((7 * n,)), pltpu.SemaphoreType.DMA((7 * n,)), pltpu.SemaphoreType.DMA((n,))]


def _hosted_comm(kind, in_refs, out_refs, sems, first, last):
    if not in_refs:
        return

    @pl.when(first)
    def _():
        remote, local = _peer_copies(kind, in_refs, out_refs, *sems)
        for cp in local + remote:
            cp.start()

    @pl.when(last)
    def _():
        remote, local = _peer_copies(kind, in_refs, out_refs, *sems)
        for cp in remote:
            cp.wait()
        for cp in local:
            cp.wait()


def _two_level_gather_steps(in_refs, out_refs, send_sems, recv_sems, local_sems):
    n = len(in_refs)
    me = _my_place()
    x, y, c = me
    sibling = (x, y, 1 - c)
    chips = [(1 - x, y), (x, 1 - y), (1 - x, 1 - y)]

    def copy(a, k, place, to, src=None):
        block = out_refs[a].at[_block_id(place)]
        return pltpu.make_async_remote_copy(
            src_ref=block if src is None else src, dst_ref=block, send_sem=send_sems.at[7 * a + k],
            recv_sem=recv_sems.at[7 * a + k], device_id=to, device_id_type=_MESH)

    mine = [pltpu.make_async_copy(in_refs[a], out_refs[a].at[_block_id(me)], local_sems.at[a]) for a in range(n)]
    first = [copy(a, 0, me, sibling, src=in_refs[a]) for a in range(n)]
    first += [copy(a, 1 + j, me, (*chip, c), src=in_refs[a]) for a in range(n) for j, chip in enumerate(chips)]
    passed = [copy(a, 4 + j, (*chip, c), sibling) for a in range(n) for j, chip in enumerate(chips)]

    def send():
        for cp in mine + first:
            cp.start()

    def forward():
        for a in range(n):
            for j, chip in enumerate(chips):
                copy(a, 1 + j, (*chip, c), me).wait_recv()
                passed[3 * a + j].start()

    def finish():
        for a in range(n):
            copy(a, 0, sibling, me).wait_recv()
            for j, chip in enumerate(chips):
                copy(a, 4 + j, (*chip, 1 - c), me).wait_recv()
        for cp in first + passed:
            cp.wait_send()
        for cp in mine:
            cp.wait()

    return send, forward, finish


def _gather_two_level(arrays, name):
    n = len(arrays)

    def body(*refs):
        for step in _two_level_gather_steps(refs[:n], refs[n:2 * n], *refs[2 * n:]):
            step()

    return pl.pallas_call(
        body, name=name, out_shape=_comm_out_shapes("gather", arrays),
        in_specs=[_ANY] * n, out_specs=[_ANY] * n, scratch_shapes=_comm_scratch(n),
    )(*arrays)


def _hosted_gather(in_refs, out_refs, sems, first, middle, last):
    if not in_refs:
        return
    for when, index in ((first, 0), (middle, 1), (last, 2)):
        @pl.when(when)
        def _(index=index):
            _two_level_gather_steps(in_refs, out_refs, *sems)[index]()


def _comm(kind, arrays, name):
    n = len(arrays)

    def body(*refs):
        remote, local = _peer_copies(kind, refs[:n], refs[n:2 * n], *refs[2 * n:])
        for cp in local + remote:
            cp.start()
        for cp in remote:
            cp.wait()
        for cp in local:
            cp.wait()

    return pl.pallas_call(
        body, name=name, out_shape=_comm_out_shapes(kind, arrays),
        in_specs=[_ANY] * n, out_specs=[_ANY] * n, scratch_shapes=_comm_scratch(n),
    )(*arrays)


def _all_reduce_small(part):
    rows, lanes = part.shape
    vmem = pl.BlockSpec(memory_space=pltpu.VMEM)

    def body(x_ref, gath_ref, sum_ref, send_sems, recv_sems):
        me = _my_place()
        my = _block_id(me)
        gath_ref[my] = x_ref[...]
        copies = []
        for k in range(1, N_DEV):
            cp = pltpu.make_async_remote_copy(
                src_ref=x_ref, dst_ref=gath_ref.at[my], send_sem=send_sems.at[k - 1], recv_sem=recv_sems.at[k - 1],
                device_id=_flip(me, k), device_id_type=_MESH)
            cp.start()
            copies.append(cp)
        for cp in copies:
            cp.wait()
        acc = gath_ref[0]
        for i in range(1, N_DEV):
            acc = acc + gath_ref[i]
        sum_ref[...] = acc

    return pl.pallas_call(
        body, name="small_grad_all_reduce",
        out_shape=[jax.ShapeDtypeStruct((N_DEV, rows, lanes), F32), jax.ShapeDtypeStruct((rows, lanes), F32)],
        in_specs=[vmem], out_specs=[vmem, vmem],
        scratch_shapes=[pltpu.SemaphoreType.DMA((7,)), pltpu.SemaphoreType.DMA((7,))],
    )(part)[1]


_SHARDED = (("w_in", (D_MODEL, IN_PROJ // N_DEV)), ("w_uq", (Q_RANK // N_DEV, Q_RANK)), ("w_ukv", (KV_RANK, HEAD_PAD)),
            ("conv_w", (CONV_W, CONV_DIM // N_DEV)), ("w_out", (D_MODEL // N_DEV, D_MODEL)),
            ("w_up", (D_MODEL, D_FF // N_DEV)), ("w_down", (D_FF // N_DEV, D_MODEL)))
_SMALL = (("pre_mix_norm", D_MODEL), ("q_norm", Q_RANK), ("kv_norm", KV_RANK), ("conv_b", CONV_DIM), ("dt_bias", SSD_HEADS),
          ("a_log", SSD_HEADS), ("d_skip", SSD_HEADS), ("ssd_norm", SSD_INNER), ("post_mix_norm", D_MODEL),
          ("pre_mlp_norm", D_MODEL), ("post_mlp_norm", D_MODEL))
_WEIGHT_ORDER = ("pre_mix_norm", "w_in", "q_norm", "w_uq", "kv_norm", "w_ukv", "conv_w", "conv_b", "dt_bias", "a_log", "d_skip",
                 "ssd_norm", "w_out", "post_mix_norm", "pre_mlp_norm", "w_up", "w_down", "post_mlp_norm")
_EARLY = ("w_in", "w_uq", "w_ukv", "conv_w")
_LATE = ("w_out", "w_up", "w_down")


def _wire_shard(name, a):
    return lax.bitcast_convert_type(a, BF16).reshape(CONV_W, -1) if name == "conv_w" else a.astype(BF16)


def _from_wire(name, g):
    return lax.bitcast_convert_type(g.reshape(N_DEV, CONV_W, -1, 2), F32) if name == "conv_w" else g


def _cols(stacked):
    return jnp.transpose(stacked, (1, 0, 2)).reshape(stacked.shape[1], -1)


def _win_segments():
    s2, s3, s5 = Q_RANK + KV_RANK, Q_RANK + KV_RANK + QK_ROPE, IN_PROJ - SSD_HEADS
    return [(0, s2), (None, MISC_ROPE), (s2, s3), (s5, IN_PROJ), (None, HEAD_PAD - MISC_DT - SSD_HEADS), (s3, s5)]


def _win_from_shards(stacked):
    per = IN_PROJ // N_DEV
    parts = []
    for start, stop in _win_segments():
        if start is None:
            parts.append(jnp.zeros((D_MODEL, stop), stacked.dtype))
            continue
        while start < stop:
            j, a = divmod(start, per)
            b = min(per, a + stop - start)
            parts.append(stacked[j, :, a:b])
            start += b - a
    return jnp.concatenate(parts, axis=1)


def _win_grad_shards(dwin):
    per = IN_PROJ // N_DEV
    runs, at = [], 0
    for start, stop in _win_segments():
        if start is not None:
            runs.append((start, stop, at))
        at += stop if start is None else stop - start
    blocks = []
    for j in range(N_DEV):
        lo, hi = j * per, (j + 1) * per
        parts = [dwin[:, p + max(lo, a) - a:p + min(hi, b) - a] for a, b, p in sorted(runs) if max(lo, a) < min(hi, b)]
        blocks.append(jnp.concatenate(parts, axis=1))
    return jnp.stack(blocks)


def _early_weights(sh):
    win = _win_from_shards(sh["w_in"])
    w_uq = sh["w_uq"].reshape(Q_RANK, MLA_HEADS, QK_NOPE + QK_ROPE)
    wuq = jnp.pad(w_uq, ((0, 0), (0, 0), (0, HEAD_PAD - QK_NOPE - QK_ROPE))).reshape(Q_RANK, -1)
    w_ukv = _cols(sh["w_ukv"]).reshape(KV_RANK, MLA_HEADS, QK_NOPE + V_DIM)
    wkn = jnp.pad(w_ukv[..., :QK_NOPE], ((0, 0), (0, 0), (0, HEAD_PAD - QK_NOPE))).reshape(KV_RANK, -1)
    wv = w_ukv[..., QK_NOPE:].reshape(KV_RANK, 4, 2, 1, V_DIM) * jnp.eye(2, dtype=BF16).reshape(1, 1, 2, 2, 1)
    wkv = jnp.concatenate([wkn, wv.reshape(KV_RANK, -1)], axis=1)
    return dict(win=win, wuq=wuq, wkv=wkv, conv_w=_cols(sh["conv_w"]))


def _late_weights(sh):
    w_out = sh["w_out"].reshape(D_MODEL, D_MODEL)
    watt = w_out[:SSD_INNER].reshape(4, 2, 1, V_DIM, D_MODEL) * jnp.eye(2, dtype=BF16).reshape(1, 2, 2, 1, 1)
    wout = jnp.concatenate([watt.reshape(MLA_HEADS * HEAD_PAD, D_MODEL), w_out[SSD_INNER:]], axis=0)
    return dict(wout=wout, wup=sh["w_up"], wdown=sh["w_down"])


def _shard_grads(g):
    out = {}
    if "wup" in g:
        out["w_up"], out["w_down"] = g["wup"], g["wdown"]
        ae = g["wout_att"].reshape(4, 2, 2, V_DIM, D_MODEL)
        att = jnp.stack([ae[:, 0, 0], ae[:, 1, 1]], axis=1).reshape(SSD_INNER, D_MODEL)
        out["w_out"] = jnp.concatenate([att, g["wout_ssd"]], axis=0).astype(BF16).reshape(N_DEV, D_MODEL // N_DEV, D_MODEL)
    if "win" not in g:
        return out
    out["w_in"] = _win_grad_shards(g["win"].astype(BF16))
    w_uq = g["wuq"].astype(BF16).reshape(Q_RANK, MLA_HEADS, HEAD_PAD)[..., :QK_NOPE + QK_ROPE].reshape(Q_RANK, Q_RANK)
    out["w_uq"] = w_uq.reshape(N_DEV, Q_RANK // N_DEV, Q_RANK)
    wide = MLA_HEADS * HEAD_PAD
    wkv = g["wkv"].astype(BF16)
    kn = wkv[:, :wide].reshape(KV_RANK, MLA_HEADS, HEAD_PAD)[..., :QK_NOPE]
    ve = wkv[:, wide:].reshape(KV_RANK, 4, 2, 2, V_DIM)
    vv = jnp.stack([ve[:, :, 0, 0], ve[:, :, 1, 1]], axis=2).reshape(KV_RANK, MLA_HEADS, V_DIM)
    out["w_ukv"] = jnp.transpose(jnp.concatenate([kn, vv], axis=-1), (1, 0, 2))
    out["conv_w"] = jnp.transpose(g["conv_w"].astype(BF16).reshape(CONV_W, N_DEV, -1), (1, 0, 2))
    return out


def _small_rows(n):
    return -(-n // 1024) * 8


def _pack_small(vals):
    rows = []
    for l in range(DEPTH):
        for name, n in _SMALL:
            r = _small_rows(n)
            rows.append(jnp.pad(vals[name][l].reshape(-1), (0, r * 128 - n)).reshape(r, 128))
    return jnp.concatenate(rows, axis=0)


def _unpack_small(packed):
    out, off = {name: [] for name, _ in _SMALL}, 0
    for l in range(DEPTH):
        for name, n in _SMALL:
            r = _small_rows(n)
            out[name].append(packed[off:off + r].reshape(-1)[:n])
            off += r
    return {name: jnp.stack(v) for name, v in out.items()}


def _lane_rows(vec8):
    return jnp.repeat(vec8, SSD_P).reshape(1, SSD_INNER)


def _layer_fwd(h, kw, sm, l, cosf, sinf, consts, gather=(), after_gather=None, target=None):
    row = lambda name: sm[name][l].reshape(1, -1)
    t = {}
    t["h0"] = h
    (t["ub"], t["cq"], t["ckv"], t["misc"], t["z"], t["xraw"], t["cqn"], t["ckvn"], t["q"], t["k"], t["v"]) = _inproj_qkv_fwd(
        h, row("pre_mix_norm"), kw["win"], row("q_norm"), row("kv_norm"), kw["wuq"], kw["wkv"], cosf, sinf)
    t["oe"], t["lse"], gathered = _attn_fwd(t["q"], t["k"], t["v"], gather)
    if after_gather is not None:
        after_gather(gathered)
    t["dtb"] = _lane_rows(sm["dt_bias"][l])
    t["a_exp"] = _lane_rows(-jnp.exp(sm["a_log"][l]))
    t["d_exp"] = _lane_rows(sm["d_skip"][l])
    t["c"], t["prev"], t["ypre"], t["yssd"], t["mixed"], t["h1"] = _ssd_outproj_fwd(
        t["xraw"], t["misc"], t["z"], kw["conv_w"], row("conv_b"), t["dtb"], t["a_exp"], t["d_exp"], row("ssd_norm"), consts,
        t["oe"], h, kw["wout"], row("post_mix_norm"))
    t["mb"], t["ab"], t["d"], *out = _mlp_fwd(t["h1"], row("pre_mlp_norm"), kw["wup"], kw["wdown"], row("post_mlp_norm"), target)
    return out, t


def _layer_bwd(dh2, t, kw, sm, l, cosf, sinf, consts, exchange_of=None):
    row = lambda name: sm[name][l].reshape(1, -1)
    g, gs = {}, {}
    dh1, dab, ddb, gs["post_mlp_norm"], gs["pre_mlp_norm"] = _mlp_bwd(
        dh2, t["d"], t["h1"], t["ab"], row("pre_mlp_norm"), kw["wup"], kw["wdown"], row("post_mlp_norm"))
    g["wup"] = _matmul_tn_stacked(t["mb"], dab, f"dw_up_{l}", a_stacked=False)
    g["wdown"] = _matmul_tn_stacked(t["ab"], ddb, f"dw_down_{l}", a_stacked=True, square_a=True)
    (dmixb, doe, delta, dz, dxraw, dmisc_dt, gs["post_mix_norm"], gs["ssd_norm"], gd, galog, gdtb, g["conv_w"],
     gs["conv_b"]) = _outproj_ssd_bwd(
        dh1, t["mixed"], t["oe"], kw["wout"], row("post_mix_norm"), t["ypre"], t["z"], t["c"], t["xraw"], t["misc"], t["prev"],
        kw["conv_w"], t["dtb"], t["a_exp"], t["d_exp"], row("ssd_norm"), consts)
    g["wout_att"] = _matmul_tn(t["oe"], dmixb, f"dw_out_att_{l}")
    g["wout_ssd"] = _matmul_tn(t["yssd"], dmixb, f"dw_out_ssd_{l}")
    gs["d_skip"] = jnp.sum(gd.reshape(SSD_HEADS, SSD_P), axis=1)
    gs["a_log"] = galog[0, MISC_DT:MISC_DT + SSD_HEADS]
    gs["dt_bias"] = gdtb[0, MISC_DT:MISC_DT + SSD_HEADS]
    dq, dk, dv, exchanged = _attn_bwd(t["q"], t["k"], t["v"], doe, t["lse"], delta,
                                      exchange_of(g) if exchange_of is not None else ())
    dqb, dkvb, dprojb, dh0, gs["q_norm"], gs["kv_norm"], gs["pre_mix_norm"] = _qkv_inproj_bwd(
        dq, dk, dv, t["cq"], t["ckv"], dmisc_dt, dz, dxraw, t["h0"], dh1, row("q_norm"), row("kv_norm"),
        row("pre_mix_norm"), kw["wuq"], kw["wkv"], kw["win"], cosf, sinf)
    g["wuq"] = _matmul_tn(t["cqn"], dqb, f"dw_uq_{l}")
    g["wkv"] = _matmul_tn(t["ckvn"], dkvb, f"dw_kv_{l}")
    g["win"] = _matmul_tn(t["ub"], dprojb, f"dw_in_{l}")
    return dh0, g, {k: v.reshape(-1) for k, v in gs.items()}, exchanged


def _local_step(x, positions, kws, sm, target, gather=(), after_gather=None, exchange_of=None):
    inv_freq = ROPE_THETA ** (-jnp.arange(0, QK_ROPE, 2, dtype=F32) / QK_ROPE)
    invf = jnp.zeros((HEAD_PAD,), F32).at[MISC_ROPE:MISC_ROPE + QK_ROPE].set(jnp.concatenate([inv_freq, inv_freq]))
    cosf, sinf = _rope_tables(positions.reshape(-1, 1), invf.reshape(1, HEAD_PAD))
    consts = _ssd_consts()
    (h,), t0 = _layer_fwd(x, kws[0], sm, 0, cosf, sinf, consts, gather, after_gather)
    (dh, loss), t1 = _layer_fwd(h, kws[1], sm, 1, cosf, sinf, consts, target=target)
    saved = [t0, t1]
    grads, small, exchanged = [None] * DEPTH, [None] * DEPTH, []
    for l in reversed(range(DEPTH)):
        hook = (lambda g0: exchange_of(g0, grads[1])) if (l == 0 and exchange_of is not None) else None
        dh, grads[l], small[l], got = _layer_bwd(dh, saved[l], kws[l], sm, l, cosf, sinf, consts, hook)
        exchanged = got or exchanged
    return loss[0, 0], dh, grads, small, exchanged


def kernel(x, positions, pre_mix_norm, w_in, q_norm, w_uq, kv_norm, w_ukv, conv_w, conv_b, dt_bias, a_log, d_skip, ssd_norm, w_out, post_mix_norm, pre_mlp_norm, w_up, w_down, post_mlp_norm, loss_target, m_pre_mix_norm, m_w_in, m_q_norm, m_w_uq, m_kv_norm, m_w_ukv, m_conv_w, m_conv_b, m_dt_bias, m_a_log, m_d_skip, m_ssd_norm, m_w_out, m_post_mix_norm, m_pre_mlp_norm, m_w_up, m_w_down, m_post_mlp_norm, v_pre_mix_norm, v_w_in, v_q_norm, v_w_uq, v_kv_norm, v_w_ukv, v_conv_w, v_conv_b, v_dt_bias, v_a_log, v_d_skip, v_ssd_norm, v_w_out, v_post_mix_norm, v_pre_mlp_norm, v_w_up, v_w_down, v_post_mlp_norm):
    w = dict(pre_mix_norm=pre_mix_norm, w_in=w_in, q_norm=q_norm, w_uq=w_uq, kv_norm=kv_norm, w_ukv=w_ukv, conv_w=conv_w,
             conv_b=conv_b, dt_bias=dt_bias, a_log=a_log, d_skip=d_skip, ssd_norm=ssd_norm, w_out=w_out,
             post_mix_norm=post_mix_norm, pre_mlp_norm=pre_mlp_norm, w_up=w_up, w_down=w_down, post_mlp_norm=post_mlp_norm)
    m = dict(pre_mix_norm=m_pre_mix_norm, w_in=m_w_in, q_norm=m_q_norm, w_uq=m_w_uq, kv_norm=m_kv_norm, w_ukv=m_w_ukv,
             conv_w=m_conv_w, conv_b=m_conv_b, dt_bias=m_dt_bias, a_log=m_a_log, d_skip=m_d_skip, ssd_norm=m_ssd_norm,
             w_out=m_w_out, post_mix_norm=m_post_mix_norm, pre_mlp_norm=m_pre_mlp_norm, w_up=m_w_up, w_down=m_w_down,
             post_mlp_norm=m_post_mlp_norm)
    v = dict(pre_mix_norm=v_pre_mix_norm, w_in=v_w_in, q_norm=v_q_norm, w_uq=v_w_uq, kv_norm=v_kv_norm, w_ukv=v_w_ukv,
             conv_w=v_conv_w, conv_b=v_conv_b, dt_bias=v_dt_bias, a_log=v_a_log, d_skip=v_d_skip, ssd_norm=v_ssd_norm,
             w_out=v_w_out, post_mix_norm=v_post_mix_norm, pre_mlp_norm=v_pre_mlp_norm, w_up=v_w_up, w_down=v_w_down,
             post_mlp_norm=v_post_mlp_norm)
    sm = {name: w[name] for name, _ in _SMALL}

    wire = lambda name, l: _wire_shard(name, w[name][l])
    first = _gather_two_level([wire(name, 0) for name in _EARLY], "weight_gather_first")
    kws = [_early_weights({name: _from_wire(name, a) for name, a in zip(_EARLY, first)}), None]
    behind = [(name, 0) for name in _LATE] + [(name, 1) for name, _ in _SHARDED]

    def after_gather(gathered):
        got = {key: _from_wire(key[0], a) for key, a in zip(behind, gathered)}
        kws[0].update(_late_weights({name: got[name, 0] for name in _LATE}))
        kws[1] = {**_early_weights({name: got[name, 1] for name in _EARLY}),
                  **_late_weights({name: got[name, 1] for name in _LATE})}

    sent_behind = [(name, 1) for name, _ in _SHARDED] + [(name, 0) for name in _LATE]

    def exchange_of(g0, g1):
        blocks = {**{(name, 1): a for name, a in _shard_grads(g1).items()},
                  **{(name, 0): a for name, a in _shard_grads(g0).items()}}
        return [blocks[key] for key in sent_behind]

    loss_part, dx, grads, small, exchanged = _local_step(
        x[0], positions[0], kws, sm, loss_target[0], [wire(*key) for key in behind], after_gather, exchange_of)
    slots = dict(zip(sent_behind, exchanged))
    last = _shard_grads({k: grads[0][k] for k in ("win", "wuq", "wkv", "conv_w")})
    slots.update({(name, 0): a for name, a in zip(_EARLY, _comm("exchange", [last[name] for name in _EARLY], "grad_exchange_last"))})
    g_small = _unpack_small(_all_reduce_small(_pack_small({name: jnp.stack([small[l][name] for l in range(DEPTH)])
                                                           for name, _ in _SMALL})))
    loss = lax.psum(loss_part, ("x", "y", "c"))

    grad, delta, new_m, new_v = {}, {}, {}, {}
    for name, _ in _SHARDED:
        grad[name], delta[name], new_m[name], new_v[name] = _sum_adamw(
            [slots[name, 0], slots[name, 1]], w[name], m[name], v[name], f"sum_adamw_{name}")
    pk = lambda d: _pack_small({name: d[name] for name, _ in _SMALL})
    d_, m_, v_ = _adamw(pk(w), pk(g_small), pk(m), pk(v), "adamw_small")
    for dst, packed in ((delta, d_), (new_m, m_), (new_v, v_)):
        dst.update(_unpack_small(packed))
    grad.update(g_small)

    outs = [loss, dx[None]]
    for d in (grad, delta, new_m, new_v):
        outs += [d[name] for name in _WEIGHT_ORDER]
    return tuple(outs)
```

```python
import jax
import jax.numpy as jnp
import numpy as np
from jax import lax
from jax.experimental import pallas as pl
from jax.experimental.pallas import tpu as pltpu

F32 = jnp.float32
BF16 = jnp.bfloat16

D_MODEL = 1024
DEPTH = 2
N_DEV = 8
CHUNK = 64
EPS = 1e-6
MLA_HEADS = 8
QK_NOPE = 64
QK_ROPE = 32
V_DIM = 64
Q_RANK = 768
KV_RANK = 256
ROPE_THETA = 10000.0
SSD_HEADS = 8
SSD_P = 64
SSD_INNER = 512
SSD_GROUPS = 2
SSD_N = 128
CONV_W = 4
CONV_DIM = 1024
D_FF = 4096
IN_PROJ = 2600
HEAD_PAD = 128
IN_PAD = 2688
MISC_ROPE = 64
MISC_DT = 96
ATT_SCALE = (QK_NOPE + QK_ROPE) ** -0.5
LOG2E = 1.4426950408889634
ATT_SCALE_LOG2 = ATT_SCALE * LOG2E

ADAM_LR = 0.001
ADAM_B1 = 0.9
ADAM_B2 = 0.999
ADAM_EPS = 1e-08
ADAM_WD = 0.01
ADAM_STEP = 10

TM = 512
ATT_T = 512
ATT_G = 8
ATT_UNROLL = 4
SSD_ROWS = 512
TK_DW = 4096
V7X_VMEM_BYTES = 64 * 1024 * 1024
VMEM_LIMIT = V7X_VMEM_BYTES - 8 * 1024 * 1024

_NT = (((1,), (1,)), ((), ()))
_TN = (((0,), (0,)), ((), ()))


def _params(**kw):
    return pltpu.CompilerParams(vmem_limit_bytes=VMEM_LIMIT, **kw)


def _dot(a, b, precision=None):
    return jnp.dot(a, b, preferred_element_type=F32, precision=precision)


def _dot_nt(a, b, precision=None):
    return lax.dot_general(a, b, _NT, preferred_element_type=F32, precision=precision)


def _dot_tn(a, b, precision=None):
    return lax.dot_general(a, b, _TN, preferred_element_type=F32, precision=precision)


def _split3(x):
    hi = x.astype(BF16)
    r = x - hi.astype(F32)
    mid = r.astype(BF16)
    return hi, mid, (r - mid.astype(F32)).astype(BF16)


def _dot01(x, m01, dot=_dot, left=False):
    parts = [dot(m01, p) if left else dot(p, m01) for p in _split3(x)]
    return parts[0] + parts[1] + parts[2]


def _full(shape):
    n = len(shape)
    return pl.BlockSpec(shape, lambda *_: (0,) * n)


def _resident(shape):
    n = len(shape)
    return pl.BlockSpec(shape, lambda *_: (0,) * n, pipeline_mode=pl.Buffered(1))


def _rows(tm, width):
    return pl.BlockSpec((tm, width), lambda i: (i, 0))


def _rms_fwd(x, w):
    r = lax.rsqrt(jnp.mean(x * x, axis=-1, keepdims=True) + EPS)
    return (x * r) * w


def _rms_bwd(x, w, dy):
    r = lax.rsqrt(jnp.mean(x * x, axis=-1, keepdims=True) + EPS)
    xh = x * r
    dxn = dy * w
    dx = r * (dxn - xh * jnp.mean(dxn * xh, axis=-1, keepdims=True))
    return dx, dy * xh


def _acc_rows(ref, val, first):
    s = jnp.sum(val, axis=0, keepdims=True)

    @pl.when(first)
    def _():
        ref[...] = s

    @pl.when(jnp.logical_not(first))
    def _():
        ref[...] += s


def _rope(t, cosf, sinf, sign):
    lane = lax.broadcasted_iota(jnp.int32, t.shape, 1)
    rot = jnp.where(lane < MISC_ROPE + QK_ROPE // 2, -pltpu.roll(t, HEAD_PAD - QK_ROPE // 2, 1), pltpu.roll(t, QK_ROPE // 2, 1))
    return t * cosf + sign * (rot * sinf)


def _rope_tables(pos, invf):
    s = pos.shape[0]

    def body(pos_ref, invf_ref, cos_ref, sin_ref):
        ang = pos_ref[...].astype(F32) * invf_ref[...]
        cos_ref[...] = jnp.cos(ang)
        sin_ref[...] = jnp.sin(ang)

    return pl.pallas_call(
        body, name="rope_tables", grid=(s // TM,),
        in_specs=[_rows(TM, 1), _full((1, HEAD_PAD))],
        out_specs=[_rows(TM, HEAD_PAD), _rows(TM, HEAD_PAD)],
        out_shape=[jax.ShapeDtypeStruct((s, HEAD_PAD), F32)] * 2,
    )(pos, invf)


def _inproj_qkv_fwd(h, nw, win, qnw, kvnw, wuq, wkv, cosf, sinf):
    s = h.shape[0]

    def body(h_ref, nw_ref, w_ref, qnw_ref, kvnw_ref, wuq_ref, wkv_ref, cos_ref, sin_ref,
             ub_ref, cq_ref, ckv_ref, misc_ref, z_ref, xbc_ref, cqn_ref, ckvn_ref, q_ref, k_ref, v_ref):
        ub = _rms_fwd(h_ref[...], nw_ref[...]).astype(BF16)
        ub_ref[...] = ub
        proj = _dot(ub, w_ref[...])
        cq, ckv, m = proj[:, 0:768], proj[:, 768:1024], proj[:, 1024:1152]
        cq_ref[...] = cq
        ckv_ref[...] = ckv
        misc_ref[...] = m
        z_ref[...] = proj[:, 1152:1664]
        xbc_ref[...] = proj[:, 1664:2688]
        cosf, sinf = cos_ref[...], sin_ref[...]
        cqn = _rms_fwd(cq, qnw_ref[...]).astype(BF16)
        cqn_ref[...] = cqn
        q = _dot(cqn, wuq_ref[...])
        ckvn = _rms_fwd(ckv, kvnw_ref[...]).astype(BF16)
        ckvn_ref[...] = ckvn
        kv = _dot(ckvn, wkv_ref[...])
        lane = lax.broadcasted_iota(jnp.int32, m.shape, 1)
        in_rope = jnp.logical_and(lane >= MISC_ROPE, lane < MISC_ROPE + QK_ROPE)
        kr = jnp.where(in_rope, _rope(m, cosf, sinf, 1.0), 0.0)
        for hd in range(MLA_HEADS):
            cols = slice(hd * HEAD_PAD, (hd + 1) * HEAD_PAD)
            q_ref[:, cols] = _rope(q[:, cols], cosf, sinf, 1.0).astype(BF16)
            k_ref[:, cols] = (kv[:, cols] + kr).astype(BF16)
        vv = kv[:, MLA_HEADS * HEAD_PAD:]
        vlane = lax.broadcasted_iota(jnp.int32, vv.shape, 1)
        ones_at = jnp.where((vlane // HEAD_PAD) % 2 == 0, V_DIM, 0)
        v_ref[...] = jnp.where(vlane % HEAD_PAD == ones_at, 1.0, vv).astype(BF16)

    wide = MLA_HEADS * HEAD_PAD
    widths = (Q_RANK, KV_RANK, HEAD_PAD, SSD_INNER, CONV_DIM)
    return pl.pallas_call(
        body, name="inproj_qkv_fwd", grid=(s // TM,),
        in_specs=[_rows(TM, D_MODEL), _full((1, D_MODEL)), _resident((D_MODEL, IN_PAD)), _full((1, Q_RANK)), _full((1, KV_RANK)),
                  _resident((Q_RANK, wide)), _resident((KV_RANK, 2 * wide)), _rows(TM, HEAD_PAD), _rows(TM, HEAD_PAD)],
        out_specs=[_rows(TM, D_MODEL)] + [_rows(TM, w) for w in widths]
        + [_rows(TM, Q_RANK), _rows(TM, KV_RANK), _rows(TM, wide), _rows(TM, wide), _rows(TM, wide)],
        out_shape=[jax.ShapeDtypeStruct((s, D_MODEL), BF16)] + [jax.ShapeDtypeStruct((s, w), F32) for w in widths]
        + [jax.ShapeDtypeStruct((s, Q_RANK), BF16), jax.ShapeDtypeStruct((s, KV_RANK), BF16)]
        + [jax.ShapeDtypeStruct((s, wide), BF16)] * 3,
        compiler_params=_params(),
    )(h, nw, win, qnw, kvnw, wuq, wkv, cosf, sinf)


def _chunk_bias(t, keys_on_rows=False):
    row = lax.broadcasted_iota(jnp.int32, (t, 1), 0) // CHUNK
    col = lax.broadcasted_iota(jnp.int32, (1, t), 1) // CHUNK
    return jnp.where((row <= col) if keys_on_rows else (col <= row), 0.0, -jnp.inf).astype(F32)


def _attn_fwd(q, k, v, gather=()):
    s = q.shape[0]
    t = ATT_T
    nq = s // t
    pair = ATT_G * HEAD_PAD
    ng = len(gather)

    def body(q_ref, k_ref, v_ref, *rest):
        g_in, (o_ref, lse_ref), g_out = rest[:ng], rest[ng:ng + 2], rest[ng + 2:2 * ng + 2]
        m_s, acc_s, bias_s = rest[2 * ng + 2:2 * ng + 5]
        qi = pl.program_id(1)
        group, groups = pl.program_id(0), MLA_HEADS // ATT_G

        @pl.when(jnp.logical_and(group == 0, qi == 0))
        def _():
            bias_s[...] = _chunk_bias(t)

        _hosted_gather(g_in, g_out, rest[2 * ng + 5:],
                       jnp.logical_and(group == 0, qi == 0),
                       jnp.logical_and(group == groups - 1, qi == min(3 * nq // 4 + 1, nq - 1)),
                       jnp.logical_and(group == groups - 1, qi == nq - 1))
        m_s[...] = jnp.full(m_s.shape, -jnp.inf, F32)
        acc_s[...] = jnp.zeros(acc_s.shape, F32)

        def step(kb, masked):
            r0 = pl.multiple_of(kb * t, t)

            def scores(hh):
                cols = slice(hh * HEAD_PAD, (hh + 1) * HEAD_PAD)
                return _dot_nt(q_ref[:, cols], k_ref[pl.ds(r0, t), cols])

            def soft(hh, raw):
                sc = raw * ATT_SCALE_LOG2
                if masked:
                    sc = sc + bias_s[...]
                m_old = m_s[hh]
                m_new = jnp.maximum(m_old, jnp.max(sc, axis=-1, keepdims=True))
                alpha = jnp.exp2(m_old - m_new)
                p = jnp.exp2(sc - jnp.tile(m_new, (1, t // HEAD_PAD)))
                m_s[hh] = m_new
                return alpha, p.astype(BF16)

            def update(hh, alpha, p):
                cols = slice(hh * HEAD_PAD, (hh + 1) * HEAD_PAD)
                acc_s[hh] = alpha * acc_s[hh] + _dot(p, v_ref[pl.ds(r0, t), cols])

            raw, ap = [None] * ATT_G, [None] * ATT_G
            raw[0] = scores(0)
            for hh in range(ATT_G):
                if hh + 1 < ATT_G:
                    raw[hh + 1] = scores(hh + 1)
                ap[hh] = soft(hh, raw[hh])
                if hh >= 1:
                    update(hh - 1, *ap[hh - 1])
            update(ATT_G - 1, *ap[ATT_G - 1])

        def loop(i, c):
            step(2 * i, False)
            step(2 * i + 1, False)
            return c

        lax.fori_loop(0, qi // 2, loop, 0)

        @pl.when(qi % 2 == 1)
        def _():
            step(qi - 1, False)

        step(qi, True)
        for hh in range(ATT_G):
            cols = slice(hh * HEAD_PAD, (hh + 1) * HEAD_PAD)
            acc = acc_s[hh]
            ones_at = V_DIM * (1 - hh % 2)
            l = jnp.broadcast_to(acc[:, ones_at:ones_at + 1], acc.shape)
            o_ref[:, cols] = (acc / l).astype(BF16)
            lse_ref[hh] = (m_s[hh] + jnp.log(l) * LOG2E).T[0:8, :]

    outs = pl.pallas_call(
        body, name="attn_fwd_gather" if ng else "attn_fwd", grid=(MLA_HEADS // ATT_G, nq),
        in_specs=[pl.BlockSpec((t, pair), lambda h, i: (i, h)),
                  pl.BlockSpec((s, pair), lambda h, i: (0, h), pipeline_mode=pl.Buffered(1)),
                  pl.BlockSpec((s, pair), lambda h, i: (0, h), pipeline_mode=pl.Buffered(1))] + [_ANY] * ng,
        out_specs=[pl.BlockSpec((t, pair), lambda h, i: (i, h)),
                   pl.BlockSpec((ATT_G, 8, t), lambda h, i: (h, 0, i))] + [_ANY] * ng,
        out_shape=[jax.ShapeDtypeStruct((s, MLA_HEADS * HEAD_PAD), BF16), jax.ShapeDtypeStruct((MLA_HEADS, 8, s), F32)]
        + _comm_out_shapes("gather", gather),
        scratch_shapes=[pltpu.VMEM((ATT_G, t, HEAD_PAD), F32), pltpu.VMEM((ATT_G, t, HEAD_PAD), F32), pltpu.VMEM((t, t), F32)]
        + (_comm_scratch(ng) if ng else []),
        compiler_params=_params(),
    )(q, k, v, *gather)
    return outs[0], outs[1], list(outs[2:])


def _interleave(stages):
    live = list(stages)
    while live:
        still = []
        for g in live:
            try:
                next(g)
                still.append(g)
            except StopIteration:
                pass
        live = still


def _ssd_consts():
    emisc = np.zeros((HEAD_PAD, SSD_INNER), np.float32)
    for hd in range(SSD_HEADS):
        emisc[MISC_DT + hd, hd * SSD_P:(hd + 1) * SSD_P] = 1.0
    idx = np.arange(CHUNK)
    tri = (idx[:, None] >= idx[None, :]).astype(np.float32)
    return tuple(jnp.asarray(m, BF16) for m in (emisc, emisc.T.copy(), tri, tri.T.copy()))


def _ssd_chunk_common(cc, misc, emisc, tri, trit, dtb, a_exp):
    sig = jax.nn.sigmoid(cc)
    xa = cc * sig
    dt = jax.nn.softplus(_dot01(misc, emisc) + dtb)
    a = dt * a_exp
    acs = _dot01(a, tri, left=True)
    acs_t = _dot01(a, trit, dot=_dot_tn)
    alast = acs[CHUNK - 1:CHUNK, :]
    return xa, sig, dt, acs, acs_t, alast


def _decay(acs, acs_t, hd):
    row = lax.broadcasted_iota(jnp.int32, (CHUNK, CHUNK), 0)
    col = lax.broadcasted_iota(jnp.int32, (CHUNK, CHUNK), 1)
    diff = acs[:, hd * SSD_P:hd * SSD_P + 1] - acs_t[hd * SSD_P:hd * SSD_P + 1, :]
    return jnp.exp(jnp.where(row >= col, diff, -jnp.inf))


def _half_mask(hh):
    lane = lax.broadcasted_iota(jnp.int32, (CHUNK, 2 * SSD_P), 1)
    return (lane >= SSD_P) if hh else (lane < SSD_P)


def _gate_norm(y, zz):
    sg = jax.nn.sigmoid(zz)
    yz = y * (zz * sg)
    outs, rs = [], []
    half = SSD_INNER // SSD_GROUPS
    for g in range(SSD_GROUPS):
        yg = yz[:, g * half:(g + 1) * half]
        r = lax.rsqrt(jnp.mean(yg * yg, axis=-1, keepdims=True) + EPS)
        outs.append(yg * r)
        rs.append(r)
    return sg, jnp.concatenate(outs, axis=1), rs


def _ssd_outproj_fwd(xraw, misc, z, cw, cb, dtb, a_exp, d_exp, nw, consts, oe, h, wout, post_w):
    s = xraw.shape[0]
    nb = s // SSD_ROWS
    ncb = SSD_ROWS // CHUNK
    emisc, _, tri, trit = consts
    wide = MLA_HEADS * HEAD_PAD

    def body(x_ref, misc_ref, z_ref, cw_ref, cb_ref, dtb_ref, a_ref, d_ref, nw_ref, emisc_ref, tri_ref, trit_ref,
             oe_ref, h_ref, wout_ref, postw_ref, c_ref, prev_ref, ypre_ref, yssd_ref, mixed_ref, h1_ref, tail_s, state_s):
        i = pl.program_id(0)

        @pl.when(i == 0)
        def _():
            tail_s[...] = jnp.zeros(tail_s.shape, F32)
            state_s[...] = jnp.zeros(state_s.shape, F32)

        mixed_att = _dot(oe_ref[...], wout_ref[0:wide, :])
        x = x_ref[...]
        xext = jnp.concatenate([tail_s[...], x], axis=0)
        acc = x * cw_ref[CONV_W - 1:CONV_W, :] + cb_ref[...]
        for j in range(1, CONV_W):
            acc = acc + pltpu.roll(xext, j, 0)[8:, :] * cw_ref[CONV_W - 1 - j:CONV_W - j, :]
        tail_s[...] = x[SSD_ROWS - 8:, :]
        c_ref[...] = acc

        def chunk(ci):
            r0 = ci * CHUNK
            xa, _, dt, acs, acs_t, alast = _ssd_chunk_common(
                c_ref[pl.ds(r0, CHUNK), :], misc_ref[pl.ds(r0, CHUNK), :], emisc_ref[...], tri_ref[...], trit_ref[...],
                dtb_ref[...], a_ref[...])
            yield
            xs = xa[:, :SSD_INNER]
            xdt = xs * dt
            wgt = (xdt * jnp.exp(alast - acs)).astype(BF16)
            e = jnp.exp(acs)
            ys, new_states, cms = [], [], []
            for g in range(SSD_GROUPS):
                bm = xa[:, SSD_INNER + g * SSD_N:SSD_INNER + (g + 1) * SSD_N].astype(BF16)
                cm = xa[:, SSD_INNER + SSD_GROUPS * SSD_N + g * SSD_N:SSD_INNER + SSD_GROUPS * SSD_N + (g + 1) * SSD_N].astype(BF16)
                cms.append(cm)
                cb_g = _dot_nt(cm, bm)
                gl = slice(g * 256, (g + 1) * 256)
                new_states.append(_dot_tn(bm, wgt[:, gl]))
                for jj in range(2):
                    pair = 2 * g + jj
                    xp = xdt[:, pair * 128:(pair + 1) * 128]
                    yp = None
                    for hh in range(2):
                        sc = (cb_g * _decay(acs, acs_t, 2 * pair + hh)).astype(BF16)
                        term = _dot(sc, jnp.where(_half_mask(hh), xp, 0.0).astype(BF16))
                        yp = term if yp is None else yp + term
                    ys.append(yp)
                yield
            prev = state_s[...]
            prev_ref[ci] = prev
            yoff = jnp.concatenate([_dot(cms[g], prev[:, g * 256:(g + 1) * 256].astype(BF16)) for g in range(SSD_GROUPS)],
                                   axis=1) * e
            state_s[...] = prev * jnp.exp(alast) + jnp.concatenate(new_states, axis=1)
            yield
            y = jnp.concatenate(ys, axis=1) + yoff + d_ref[...] * xs
            ypre_ref[pl.ds(r0, CHUNK), :] = y
            _, yn, _ = _gate_norm(y, z_ref[pl.ds(r0, CHUNK), :])
            yssd_ref[pl.ds(r0, CHUNK), :] = (yn * nw_ref[...]).astype(BF16)

        _interleave([chunk(ci) for ci in range(ncb)])
        mixed = mixed_att + _dot(yssd_ref[...], wout_ref[wide:, :])
        mixed_ref[...] = mixed
        h1_ref[...] = h_ref[...] + _rms_fwd(mixed, postw_ref[...])

    return pl.pallas_call(
        body, name="ssd_outproj_fwd", grid=(nb,),
        in_specs=[_rows(SSD_ROWS, CONV_DIM), _rows(SSD_ROWS, HEAD_PAD), _rows(SSD_ROWS, SSD_INNER),
                  _full((CONV_W, CONV_DIM)), _full((1, CONV_DIM)), _full((1, SSD_INNER)), _full((1, SSD_INNER)),
                  _full((1, SSD_INNER)), _full((1, SSD_INNER)), _full((HEAD_PAD, SSD_INNER)), _full((CHUNK, CHUNK)),
                  _full((CHUNK, CHUNK)), _rows(SSD_ROWS, wide), _rows(SSD_ROWS, D_MODEL),
                  _resident((wide + SSD_INNER, D_MODEL)), _full((1, D_MODEL))],
        out_specs=[_rows(SSD_ROWS, CONV_DIM), pl.BlockSpec((ncb, SSD_N, SSD_INNER), lambda i: (i, 0, 0)),
                   _rows(SSD_ROWS, SSD_INNER), _rows(SSD_ROWS, SSD_INNER), _rows(SSD_ROWS, D_MODEL), _rows(SSD_ROWS, D_MODEL)],
        out_shape=[jax.ShapeDtypeStruct((s, CONV_DIM), F32), jax.ShapeDtypeStruct((s // CHUNK, SSD_N, SSD_INNER), F32),
                   jax.ShapeDtypeStruct((s, SSD_INNER), F32), jax.ShapeDtypeStruct((s, SSD_INNER), BF16),
                   jax.ShapeDtypeStruct((s, D_MODEL), F32), jax.ShapeDtypeStruct((s, D_MODEL), F32)],
        scratch_shapes=[pltpu.VMEM((8, CONV_DIM), F32), pltpu.VMEM((SSD_N, SSD_INNER), F32)],
        compiler_params=_params(),
    )(xraw, misc, z, cw, cb, dtb, a_exp, d_exp, nw, emisc, tri, trit, oe, h, wout, post_w)


def _mlp_fwd(h1, prew, wup, wdown, postw, target=None):
    s = h1.shape[0]
    fb = D_FF // N_DEV
    last = target is not None

    def body(h_ref, prew_ref, up_ref, down_ref, postw_ref, *rest):
        target_ref, (mb_ref, ab_ref, d_ref, out_ref) = (rest[0] if last else None), rest[last:last + 4]
        hh = h_ref[...]
        mb = _rms_fwd(hh, prew_ref[...]).astype(BF16)
        mb_ref[...] = mb
        d = jnp.zeros((TM, D_MODEL), F32)
        for j in range(N_DEV):
            a = jnp.maximum(_dot(mb, up_ref[j]), 0.0)
            ab_ref[j] = a.astype(BF16)
            d = d + _dot(jnp.square(a).astype(BF16), down_ref[j])
        d_ref[...] = d
        h2 = hh + _rms_fwd(d, postw_ref[...])
        if last:
            diff = h2 - target_ref[...]
            out_ref[...] = diff * (1.0 / D_MODEL)
            part = 0.5 * jnp.sum(jnp.mean(diff * diff, axis=-1, keepdims=True), axis=0, keepdims=True)
            _acc_rows(rest[-1], part, pl.program_id(0) == 0)
        else:
            out_ref[...] = h2

    stacked = pl.BlockSpec((N_DEV, TM, fb), lambda i: (0, i, 0))
    return pl.pallas_call(
        body, name="mlp_fwd_loss" if last else "mlp_fwd", grid=(s // TM,),
        in_specs=[_rows(TM, D_MODEL), _full((1, D_MODEL)), _resident((N_DEV, D_MODEL, fb)), _resident((N_DEV, fb, D_MODEL)),
                  _full((1, D_MODEL))] + ([_rows(TM, D_MODEL)] if last else []),
        out_specs=[_rows(TM, D_MODEL), stacked, _rows(TM, D_MODEL), _rows(TM, D_MODEL)] + ([_full((1, 1))] if last else []),
        out_shape=[jax.ShapeDtypeStruct((s, D_MODEL), BF16), jax.ShapeDtypeStruct((N_DEV, s, fb), BF16),
                   jax.ShapeDtypeStruct((s, D_MODEL), F32), jax.ShapeDtypeStruct((s, D_MODEL), F32)]
        + ([jax.ShapeDtypeStruct((1, 1), F32)] if last else []),
        compiler_params=_params(),
    )(h1, prew, wup, wdown, postw, *([target] if last else []))


def _mlp_bwd(dh2, d, h1, ab, prew, wup, wdown, postw):
    s = dh2.shape[0]
    fb = D_FF // N_DEV
    tm = TM // 2

    def body(dh2_ref, d_ref, h1_ref, ab_ref, prew_ref, up_ref, down_ref, postw_ref,
             dh1_ref, da_ref, dd_ref, gpost_ref, gpre_ref):
        first = pl.program_id(0) == 0
        dh2 = dh2_ref[...]
        dd, gpost = _rms_bwd(d_ref[...], postw_ref[...], dh2)
        _acc_rows(gpost_ref, gpost, first)
        ddb = dd.astype(BF16)
        dd_ref[...] = ddb

        def d_relu_squared(j):
            return _dot_nt(ddb, down_ref[j])

        def pointwise(j, dr):
            da = (dr * (2.0 * ab_ref[j].astype(F32))).astype(BF16)
            da_ref[j] = da
            return da

        dm = jnp.zeros((tm, D_MODEL), F32)
        nxt, da_prev = d_relu_squared(0), None
        for j in range(N_DEV):
            cur = nxt
            if j + 1 < N_DEV:
                nxt = d_relu_squared(j + 1)
            da = pointwise(j, cur)
            if da_prev is not None:
                dm = dm + _dot_nt(da_prev, up_ref[j - 1])
            da_prev = da
        dm = dm + _dot_nt(da_prev, up_ref[N_DEV - 1])
        dx, gpre = _rms_bwd(h1_ref[...], prew_ref[...], dm)
        _acc_rows(gpre_ref, gpre, first)
        dh1_ref[...] = dh2 + dx

    stacked = pl.BlockSpec((N_DEV, tm, fb), lambda i: (0, i, 0))
    return pl.pallas_call(
        body, name="mlp_bwd", grid=(s // tm,),
        in_specs=[_rows(tm, D_MODEL)] * 3 + [stacked, _full((1, D_MODEL)), _resident((N_DEV, D_MODEL, fb)),
                                              _resident((N_DEV, fb, D_MODEL)), _full((1, D_MODEL))],
        out_specs=[_rows(tm, D_MODEL), stacked, _rows(tm, D_MODEL), _full((1, D_MODEL)), _full((1, D_MODEL))],
        out_shape=[jax.ShapeDtypeStruct((s, D_MODEL), F32), jax.ShapeDtypeStruct((N_DEV, s, fb), BF16),
                   jax.ShapeDtypeStruct((s, D_MODEL), BF16), jax.ShapeDtypeStruct((1, D_MODEL), F32),
                   jax.ShapeDtypeStruct((1, D_MODEL), F32)],
        compiler_params=_params(),
    )(dh2, d, h1, ab, prew, wup, wdown, postw)


def _matmul_tn(a, b, name, tk=TK_DW):
    s, m = a.shape
    n = b.shape[1]
    tn = n if n <= 1024 else (n // 2 if (n // 2) % 128 == 0 else n // 3)
    tk = min(tk, s)
    assert n % tn == 0 and tn % 128 == 0 and s % tk == 0

    def body(a_ref, b_ref, o_ref):
        part = _dot_tn(a_ref[...], b_ref[...])

        @pl.when(pl.program_id(1) == 0)
        def _():
            o_ref[...] = part

        @pl.when(pl.program_id(1) != 0)
        def _():
            o_ref[...] += part

    return pl.pallas_call(
        body, name=name, grid=(n // tn, s // tk),
        in_specs=[pl.BlockSpec((tk, m), lambda j, k: (k, 0)), pl.BlockSpec((tk, tn), lambda j, k: (k, j))],
        out_specs=pl.BlockSpec((m, tn), lambda j, k: (0, j)),
        out_shape=jax.ShapeDtypeStruct((m, n), F32),
        compiler_params=_params(),
    )(a, b)


def _matmul_tn_stacked(a, b, name, a_stacked, square_a=False, tk=TK_DW):
    tk = min(tk, a.shape[-2])
    if a_stacked:
        _, s, m = a.shape
        n = b.shape[1]
        in_specs = [pl.BlockSpec((1, tk, m), lambda j, k: (j, k, 0)), pl.BlockSpec((tk, n), lambda j, k: (k, 0))]
    else:
        s, m = a.shape
        n = b.shape[2]
        in_specs = [pl.BlockSpec((tk, m), lambda j, k: (k, 0)), pl.BlockSpec((1, tk, n), lambda j, k: (j, k, 0))]

    nk = s // tk

    def body(a_ref, b_ref, o_ref, acc_s):
        av = a_ref[0] if a_stacked else a_ref[...]
        bv = b_ref[...] if a_stacked else b_ref[0]
        if square_a:
            av = jnp.square(av.astype(F32)).astype(BF16)
        part = _dot_tn(av, bv)
        k = pl.program_id(1)

        @pl.when(k == 0)
        def _():
            acc_s[...] = part

        @pl.when(jnp.logical_and(k != 0, k != nk - 1))
        def _():
            acc_s[...] += part

        @pl.when(k == nk - 1)
        def _():
            o_ref[0] = (part if nk == 1 else acc_s[...] + part).astype(BF16)

    return pl.pallas_call(
        body, name=name, grid=(N_DEV, nk),
        in_specs=in_specs,
        out_specs=pl.BlockSpec((1, m, n), lambda j, k: (j, 0, 0)),
        out_shape=jax.ShapeDtypeStruct((N_DEV, m, n), BF16),
        scratch_shapes=[pltpu.VMEM((m, n), F32)],
        compiler_params=_params(),
    )(a, b)


def _outproj_bwd(dh1, mixed, nw, wout, oe):
    s = dh1.shape[0]
    wide = MLA_HEADS * HEAD_PAD

    def body(dh1_ref, mixed_ref, nw_ref, w_ref, oe_ref, dmix_ref, doe_ref, dy_ref, gnw_ref, delta_ref):
        dmix, gnw = _rms_bwd(mixed_ref[...], nw_ref[...], dh1_ref[...])
        _acc_rows(gnw_ref, gnw, pl.program_id(0) == 0)
        dmb = dmix.astype(BF16)
        dmix_ref[...] = dmb
        doe_ref[...] = _dot_nt(dmb, w_ref[0:wide, :]).astype(BF16)
        dy_ref[...] = _dot_nt(dmb, w_ref[wide:, :])
        ones = jnp.ones((8, HEAD_PAD), BF16)
        for hd in range(MLA_HEADS):
            cols = slice(hd * HEAD_PAD, (hd + 1) * HEAD_PAD)
            prod = oe_ref[:, cols].astype(F32) * doe_ref[:, cols].astype(F32)
            delta_ref[hd] = _dot01(prod, ones, dot=_dot_nt, left=True)

    return pl.pallas_call(
        body, name="outproj_bwd", grid=(s // TM,),
        in_specs=[_rows(TM, D_MODEL), _rows(TM, D_MODEL), _full((1, D_MODEL)), _resident((wide + SSD_INNER, D_MODEL)),
                  _rows(TM, wide)],
        out_specs=[_rows(TM, D_MODEL), _rows(TM, wide), _rows(TM, SSD_INNER), _full((1, D_MODEL)),
                   pl.BlockSpec((MLA_HEADS, 8, TM), lambda i: (0, 0, i))],
        out_shape=[jax.ShapeDtypeStruct((s, D_MODEL), BF16), jax.ShapeDtypeStruct((s, wide), BF16),
                   jax.ShapeDtypeStruct((s, SSD_INNER), F32), jax.ShapeDtypeStruct((1, D_MODEL), F32),
                   jax.ShapeDtypeStruct((MLA_HEADS, 8, s), F32)],
        compiler_params=_params(),
    )(dh1, mixed, nw, wout, oe)


def _attn_bwd(q, k, v, do, lse, delta, exchange=()):
    s = q.shape[0]
    t = ATT_T
    nq = s // t
    pair = 2 * HEAD_PAD
    ne = len(exchange)

    def body(q_ref, k_ref, v_ref, do_ref, lse_ref, delta_ref, *rest):
        e_in, (dq_ref, dk_ref, dv_ref), e_out = rest[:ne], rest[ne:ne + 3], rest[ne + 3:2 * ne + 3]
        dk_s, dv_s, bias_s = rest[2 * ne + 3:2 * ne + 6]
        kb = pl.program_id(1)
        _hosted_comm("exchange", e_in, e_out, rest[2 * ne + 6:],
                     jnp.logical_and(pl.program_id(0) == 0, kb == 0),
                     jnp.logical_and(pl.program_id(0) == MLA_HEADS // 2 - 1, kb == nq - 1))

        @pl.when(jnp.logical_and(pl.program_id(0) == 0, kb == 0))
        def _():
            bias_s[...] = _chunk_bias(t, keys_on_rows=True)

        @pl.when(kb == 0)
        def _():
            dq_ref[...] = jnp.zeros(dq_ref.shape, F32)

        def step(qb, diagonal):
            r0 = pl.multiple_of(qb * t, t)
            for hh in range(2):
                cols = slice(hh * HEAD_PAD, (hh + 1) * HEAD_PAD)
                kk = k_ref[:, cols]
                qq = q_ref[pl.ds(r0, t), cols]
                dd = do_ref[pl.ds(r0, t), cols]
                sc = _dot_nt(kk, qq) * ATT_SCALE_LOG2
                if diagonal:
                    sc = sc + bias_s[...]
                p = jnp.exp2(sc - lse_ref[hh, 0:1, pl.ds(r0, t)])
                dv = _dot(p.astype(BF16), dd)
                dp = _dot_nt(v_ref[:, cols], dd)
                ds = (p * (dp - delta_ref[hh, 0:1, pl.ds(r0, t)]) * ATT_SCALE).astype(BF16)
                dk = _dot(ds, qq)
                if diagonal:
                    dv_s[:, cols] = dv
                    dk_s[:, cols] = dk
                else:
                    dv_s[:, cols] += dv
                    dk_s[:, cols] += dk
                dq_ref[pl.ds(r0, t), cols] += _dot_tn(ds, kk)

        def loop(i, c):
            for u in range(ATT_UNROLL):
                step(kb + 1 + u + ATT_UNROLL * i, False)
            return c

        step(kb, True)
        later_tiles = nq - 1 - kb
        lax.fori_loop(0, later_tiles // ATT_UNROLL, loop, 0)
        left = later_tiles % ATT_UNROLL
        for u in range(ATT_UNROLL - 1):
            @pl.when(left > u)
            def _(u=u):
                step(nq - left + u, False)

        dk_ref[...] = dk_s[...].astype(BF16)
        dv_ref[...] = dv_s[...].astype(BF16)

    whole = pl.BlockSpec((s, pair), lambda h, i: (0, h))
    tile = pl.BlockSpec((t, pair), lambda h, i: (i, h))
    rowvec = pl.BlockSpec((2, 8, s), lambda h, i: (h, 0, 0))
    wide = MLA_HEADS * HEAD_PAD
    outs = pl.pallas_call(
        body, name="attn_bwd_exchange" if ne else "attn_bwd", grid=(MLA_HEADS // 2, nq),
        in_specs=[whole, tile, tile, whole, rowvec, rowvec] + [_ANY] * ne,
        out_specs=[whole, tile, tile] + [_ANY] * ne,
        out_shape=[jax.ShapeDtypeStruct((s, wide), F32)] + [jax.ShapeDtypeStruct((s, wide), BF16)] * 2
        + _comm_out_shapes("exchange", exchange),
        scratch_shapes=[pltpu.VMEM((t, pair), F32), pltpu.VMEM((t, pair), F32), pltpu.VMEM((t, t), F32)]
        + (_comm_scratch(ne) if ne else []),
        compiler_params=_params(),
    )(q, k, v, do, lse, delta, *exchange)
    return outs[0], outs[1], outs[2], list(outs[3:])


def _ssd_bwd(dy, ypre, z, c, xraw, misc, prev, cw, dtb, a_exp, d_exp, nw, consts):
    s = dy.shape[0]
    nb = s // SSD_ROWS
    ncb = SSD_ROWS // CHUNK
    emisc, emisc_t, tri, trit = consts

    def body(dy_ref, ypre_ref, z_ref, c_ref, x_ref, misc_ref, prev_ref, cw_ref, dtb_ref, a_ref, d_ref, nw_ref,
             emisc_ref, emisct_ref, tri_ref, trit_ref,
             dz_ref, dx_ref, dmisc_ref, gnw_ref, gd_ref, galog_ref, gdtb_ref, gcw_ref, gcb_ref,
             dst_s, dc_s, head_s):
        i = pl.program_id(0)
        first = i == 0

        @pl.when(first)
        def _():
            dst_s[...] = jnp.zeros(dst_s.shape, F32)
            head_s[...] = jnp.zeros(head_s.shape, F32)
            gnw_ref[...] = jnp.zeros(gnw_ref.shape, F32)
            gd_ref[...] = jnp.zeros(gd_ref.shape, F32)
            galog_ref[...] = jnp.zeros(galog_ref.shape, F32)
            gdtb_ref[...] = jnp.zeros(gdtb_ref.shape, F32)

        a_exp_v = a_ref[...]
        a8 = _dot01(a_exp_v, emisct_ref[...]) * (1.0 / SSD_P)

        def chunk(ci):
            r0 = ci * CHUNK
            cc = c_ref[pl.ds(r0, CHUNK), :]
            mm = misc_ref[pl.ds(r0, CHUNK), :]
            xa, sig_c, dt, acs, acs_t, alast = _ssd_chunk_common(cc, mm, emisc_ref[...], tri_ref[...], trit_ref[...],
                                                              dtb_ref[...], a_exp_v)
            yield
            xs = xa[:, :SSD_INNER]
            xdt = xs * dt
            y = ypre_ref[pl.ds(r0, CHUNK), :]
            zz = z_ref[pl.ds(r0, CHUNK), :]
            sg, yn, rs = _gate_norm(y, zz)
            dyo = dy_ref[pl.ds(r0, CHUNK), :]
            gnw_ref[...] += jnp.sum(dyo * yn, axis=0, keepdims=True)
            dyn = dyo * nw_ref[...]
            half = SSD_INNER // SSD_GROUPS
            dyz_parts = []
            for g in range(SSD_GROUPS):
                gl = slice(g * half, (g + 1) * half)
                dyz_parts.append(rs[g] * (dyn[:, gl] - yn[:, gl] * jnp.mean(dyn[:, gl] * yn[:, gl], axis=-1, keepdims=True)))
            dyz = jnp.concatenate(dyz_parts, axis=1)
            dz_ref[pl.ds(r0, CHUNK), :] = dyz * y * (sg * (1.0 + zz * (1.0 - sg)))
            dyp = dyz * (zz * sg)
            dypb = dyp.astype(BF16)
            gd_ref[...] += jnp.sum(dyp * xs, axis=0, keepdims=True)
            yield
            prev = prev_ref[ci]
            cd = jnp.exp(alast)
            e = jnp.exp(acs)
            dsx = jnp.exp(alast - acs)
            wgt = (xdt * dsx).astype(BF16)
            dze = (dyp * e).astype(BF16)
            dprev_parts, diag_all, dbm, dcm, yoff_parts, bms = [], [], [], [], [], []
            lane8 = lax.broadcasted_iota(jnp.int32, (CHUNK, HEAD_PAD), 1)
            diag8 = jnp.zeros((CHUNK, HEAD_PAD), F32)
            for g in range(SSD_GROUPS):
                gl = slice(g * 256, (g + 1) * 256)
                bm = xa[:, SSD_INNER + g * SSD_N:SSD_INNER + (g + 1) * SSD_N].astype(BF16)
                cm = xa[:, SSD_INNER + SSD_GROUPS * SSD_N + g * SSD_N:SSD_INNER + SSD_GROUPS * SSD_N + (g + 1) * SSD_N].astype(BF16)
                bms.append(bm)
                prev_g = prev[:, gl].astype(BF16)
                dcm_g = _dot_nt(dze[:, gl], prev_g)
                dprev_parts.append(_dot_tn(cm, dze[:, gl]))
                cb_g = _dot_nt(cm, bm)
                dcb = jnp.zeros((CHUNK, CHUNK), F32)
                diag_parts = []
                for jj in range(2):
                    pair = 2 * g + jj
                    pl_ = slice(pair * 128, (pair + 1) * 128)
                    xp = xdt[:, pl_]
                    dyp_p = dypb[:, pl_]
                    dxp = jnp.zeros((CHUNK, 128), F32)
                    for hh in range(2):
                        hd = 2 * pair + hh
                        dec = _decay(acs, acs_t, hd)
                        xm = jnp.where(_half_mask(hh), xp, 0.0).astype(BF16)
                        dsc = _dot_nt(dyp_p, xm) * dec
                        dcb = dcb + dsc
                        sc = (cb_g * dec).astype(BF16)
                        dxp = dxp + jnp.where(_half_mask(hh), _dot_tn(sc, dyp_p), 0.0)
                        dm = dsc * cb_g
                        diag8 = diag8 + jnp.where(lane8 == MISC_DT + hd, jnp.sum(dm - dm.T, axis=1, keepdims=True), 0.0)
                    diag_parts.append(dxp)
                dcbb = dcb.astype(BF16)
                dcm.append(dcm_g + _dot(dcbb, bm))
                dbm.append(_dot_tn(dcbb, cm))
                diag_all.append(jnp.concatenate(diag_parts, axis=1))
                yoff_parts.append(_dot(cm, prev_g) * e[:, gl])
                yield
            dst = dst_s[...]
            glast = jnp.sum(dst * prev, axis=0, keepdims=True) * cd
            dxdt_state_parts = []
            for g in range(SSD_GROUPS):
                gl = slice(g * 256, (g + 1) * 256)
                dst_g = dst[:, gl].astype(BF16)
                dxdt_state_parts.append(_dot(bms[g], dst_g) * dsx[:, gl])
                dbm[g] = dbm[g] + _dot_nt(wgt[:, gl], dst_g)
            dst_s[...] = dst * cd + jnp.concatenate(dprev_parts, axis=1)
            yield
            dxdt_state = jnp.concatenate(dxdt_state_parts, axis=1)
            dxdt = jnp.concatenate(diag_all, axis=1) + dxdt_state
            dacs = dyp * jnp.concatenate(yoff_parts, axis=1) - xdt * dxdt_state
            last = jnp.sum(xdt * dxdt_state, axis=0, keepdims=True) + glast
            row = lax.broadcasted_iota(jnp.int32, (CHUNK, SSD_INNER), 0)
            dacs = dacs + jnp.where(row == CHUNK - 1, last, 0.0)
            dacs8 = _dot01(dacs, emisct_ref[...]) + diag8
            da8 = _dot01(dacs8, trit_ref[...], left=True)
            ddt8 = da8 * a8 + _dot01(dxdt * xs, emisct_ref[...])
            yield
            dtr8 = mm + _dot01(dtb_ref[...], emisct_ref[...]) * (1.0 / SSD_P)
            dt8 = jax.nn.softplus(dtr8)
            lane = lax.broadcasted_iota(jnp.int32, (CHUNK, HEAD_PAD), 1)
            on_dt = jnp.logical_and(lane >= MISC_DT, lane < MISC_DT + SSD_HEADS)
            ddtr8 = jnp.where(on_dt, ddt8 * jax.nn.sigmoid(dtr8), 0.0)
            dmisc_ref[pl.ds(r0, CHUNK), :] = ddtr8
            gdtb_ref[...] += jnp.sum(ddtr8, axis=0, keepdims=True)
            galog_ref[...] += jnp.sum(jnp.where(on_dt, da8 * dt8, 0.0), axis=0, keepdims=True) * a8
            dxs = d_ref[...] * dyp + dxdt * dt
            dxa = jnp.concatenate([dxs] + dbm + dcm, axis=1)
            dc_s[pl.ds(r0, CHUNK), :] = dxa * (sig_c * (1.0 + cc * (1.0 - sig_c)))

        _interleave([chunk(ci) for ci in reversed(range(ncb))])

        dc = dc_s[...]
        x = x_ref[...]
        dcext = jnp.concatenate([dc, head_s[...]], axis=0)
        dx = dc * cw_ref[CONV_W - 1:CONV_W, :]
        rows = [jnp.sum(dc * x, axis=0, keepdims=True)]
        for j in range(1, CONV_W):
            ahead = pltpu.roll(dcext, SSD_ROWS + 8 - j, 0)[:SSD_ROWS, :]
            dx = dx + ahead * cw_ref[CONV_W - 1 - j:CONV_W - j, :]
            rows.insert(0, jnp.sum(ahead * x, axis=0, keepdims=True))
        dx_ref[...] = dx
        head_s[...] = dc[:8, :]
        gcw = jnp.concatenate(rows, axis=0)

        @pl.when(first)
        def _():
            gcw_ref[...] = gcw
            gcb_ref[...] = jnp.sum(dc, axis=0, keepdims=True)

        @pl.when(jnp.logical_not(first))
        def _():
            gcw_ref[...] += gcw
            gcb_ref[...] += jnp.sum(dc, axis=0, keepdims=True)

    def rev(width):
        return pl.BlockSpec((SSD_ROWS, width), lambda i: (nb - 1 - i, 0))

    return pl.pallas_call(
        body, name="ssd_bwd", grid=(nb,),
        in_specs=[rev(SSD_INNER), rev(SSD_INNER), rev(SSD_INNER), rev(CONV_DIM), rev(CONV_DIM),
                  rev(HEAD_PAD), pl.BlockSpec((ncb, SSD_N, SSD_INNER), lambda i: (nb - 1 - i, 0, 0)),
                  _full((CONV_W, CONV_DIM)), _full((1, SSD_INNER)), _full((1, SSD_INNER)), _full((1, SSD_INNER)),
                  _full((1, SSD_INNER)), _full((HEAD_PAD, SSD_INNER)), _full((SSD_INNER, HEAD_PAD)), _full((CHUNK, CHUNK)),
                  _full((CHUNK, CHUNK))],
        out_specs=[rev(SSD_INNER), rev(CONV_DIM), rev(HEAD_PAD), _full((1, SSD_INNER)), _full((1, SSD_INNER)),
                   _full((1, HEAD_PAD)), _full((1, HEAD_PAD)), _full((CONV_W, CONV_DIM)), _full((1, CONV_DIM))],
        out_shape=[jax.ShapeDtypeStruct((s, SSD_INNER), F32), jax.ShapeDtypeStruct((s, CONV_DIM), F32),
                   jax.ShapeDtypeStruct((s, HEAD_PAD), F32), jax.ShapeDtypeStruct((1, SSD_INNER), F32),
                   jax.ShapeDtypeStruct((1, SSD_INNER), F32), jax.ShapeDtypeStruct((1, HEAD_PAD), F32),
                   jax.ShapeDtypeStruct((1, HEAD_PAD), F32), jax.ShapeDtypeStruct((CONV_W, CONV_DIM), F32),
                   jax.ShapeDtypeStruct((1, CONV_DIM), F32)],
        scratch_shapes=[pltpu.VMEM((SSD_N, SSD_INNER), F32), pltpu.VMEM((SSD_ROWS, CONV_DIM), F32), pltpu.VMEM((8, CONV_DIM), F32)],
        compiler_params=_params(),
    )(dy, ypre, z, c, xraw, misc, prev, cw, dtb, a_exp, d_exp, nw, emisc, emisc_t, tri, trit)


def _qkv_inproj_bwd(dq, dk, dv, cq, ckv, dmisc_dt, dz, dxbc, h, dh1, qnw, kvnw, nw, wuq, wkv, win, cosf, sinf):
    s = dq.shape[0]
    wide = MLA_HEADS * HEAD_PAD
    tm = TM

    def body(dq_ref, dk_ref, dv_ref, cq_ref, ckv_ref, dmdt_ref, dz_ref, dxbc_ref, h_ref, dh1_ref, qnw_ref, kvnw_ref, nw_ref,
             wuq_ref, wkv_ref, win_ref, cos_ref, sin_ref, dqb_ref, dkvb_ref, dproj_ref, dh0_ref, gq_ref, gkv_ref, gnw_ref):
        first = pl.program_id(0) == 0
        cosf, sinf = cos_ref[...], sin_ref[...]
        dkr = jnp.zeros((tm, HEAD_PAD), F32)
        for hd in range(MLA_HEADS):
            cols = slice(hd * HEAD_PAD, (hd + 1) * HEAD_PAD)
            dqb_ref[:, cols] = _rope(dq_ref[:, cols], cosf, sinf, -1.0).astype(BF16)
            dkh = dk_ref[:, cols]
            dkvb_ref[:, cols] = dkh
            dkr = dkr + dkh
        dkvb_ref[:, wide:] = dv_ref[...]
        lane = lax.broadcasted_iota(jnp.int32, dkr.shape, 1)
        in_rope = jnp.logical_and(lane >= MISC_ROPE, lane < MISC_ROPE + QK_ROPE)
        dmisc_rope = jnp.where(in_rope, _rope(jnp.where(in_rope, dkr, 0.0), cosf, sinf, -1.0), 0.0)
        dcq, gq = _rms_bwd(cq_ref[...], qnw_ref[...], _dot_nt(dqb_ref[...], wuq_ref[...]))
        _acc_rows(gq_ref, gq, first)
        dckv, gkv = _rms_bwd(ckv_ref[...], kvnw_ref[...], _dot_nt(dkvb_ref[...], wkv_ref[...]))
        _acc_rows(gkv_ref, gkv, first)
        dproj_ref[:, 0:768] = dcq.astype(BF16)
        dproj_ref[:, 768:1024] = dckv.astype(BF16)
        dproj_ref[:, 1024:1152] = (dmisc_rope + dmdt_ref[...]).astype(BF16)
        dproj_ref[:, 1152:1664] = dz_ref[...].astype(BF16)
        dproj_ref[:, 1664:2688] = dxbc_ref[...].astype(BF16)
        dx, gnw = _rms_bwd(h_ref[...], nw_ref[...], _dot_nt(dproj_ref[...], win_ref[...]))
        _acc_rows(gnw_ref, gnw, first)
        dh0_ref[...] = dh1_ref[...] + dx

    return pl.pallas_call(
        body, name="qkv_inproj_bwd", grid=(s // tm,),
        in_specs=[_rows(tm, wide)] * 3 + [_rows(tm, Q_RANK), _rows(tm, KV_RANK), _rows(tm, HEAD_PAD), _rows(tm, SSD_INNER),
                                          _rows(tm, CONV_DIM), _rows(tm, D_MODEL), _rows(tm, D_MODEL),
                                          _full((1, Q_RANK)), _full((1, KV_RANK)), _full((1, D_MODEL)),
                                          _resident((Q_RANK, wide)), _resident((KV_RANK, 2 * wide)), _resident((D_MODEL, IN_PAD)),
                                          _rows(tm, HEAD_PAD), _rows(tm, HEAD_PAD)],
        out_specs=[_rows(tm, wide), _rows(tm, 2 * wide), _rows(tm, IN_PAD), _rows(tm, D_MODEL),
                   _full((1, Q_RANK)), _full((1, KV_RANK)), _full((1, D_MODEL))],
        out_shape=[jax.ShapeDtypeStruct((s, wide), BF16), jax.ShapeDtypeStruct((s, 2 * wide), BF16),
                   jax.ShapeDtypeStruct((s, IN_PAD), BF16), jax.ShapeDtypeStruct((s, D_MODEL), F32),
                   jax.ShapeDtypeStruct((1, Q_RANK), F32), jax.ShapeDtypeStruct((1, KV_RANK), F32),
                   jax.ShapeDtypeStruct((1, D_MODEL), F32)],
        compiler_params=_params(),
    )(dq, dk, dv, cq, ckv, dmisc_dt, dz, dxbc, h, dh1, qnw, kvnw, nw, wuq, wkv, win, cosf, sinf)


def _row_tile(rows, cols):
    cap = max(8, (1 << 18) // max(cols, 128))
    best = None
    for t in range(8, rows + 1, 8):
        if rows % t == 0 and t <= cap:
            best = t
    return best if best is not None else rows


def _adamw(w, g, m, v, name):
    rows, cols = w.shape
    tr = _row_tile(rows, cols)

    def body(w_ref, g_ref, m_ref, v_ref, d_ref, m2_ref, v2_ref):
        gg = g_ref[...]
        m2 = ADAM_B1 * m_ref[...] + (1.0 - ADAM_B1) * gg
        v2 = ADAM_B2 * v_ref[...] + (1.0 - ADAM_B2) * jnp.square(gg)
        m_hat = m2 / (1.0 - ADAM_B1 ** ADAM_STEP)
        v_hat = v2 / (1.0 - ADAM_B2 ** ADAM_STEP)
        d_ref[...] = -ADAM_LR * (m_hat / (jnp.sqrt(v_hat) + ADAM_EPS) + ADAM_WD * w_ref[...])
        m2_ref[...] = m2
        v2_ref[...] = v2

    spec = pl.BlockSpec((tr, cols), lambda i: (i, 0))
    return pl.pallas_call(
        body, name=name, grid=(rows // tr,),
        in_specs=[spec] * 4, out_specs=[spec] * 3,
        out_shape=[jax.ShapeDtypeStruct((rows, cols), F32)] * 3,
    )(w, g, m, v)


def _sum_adamw(slots, w, m, v, name):
    _, rows, cols = w.shape
    tr = _row_tile(rows, cols)
    nb = rows // tr

    def body(s0_ref, s1_ref, w_ref, m_ref, v_ref, g_ref, d_ref, m2_ref, v2_ref):
        for l, ref in enumerate((s0_ref, s1_ref)):
            @pl.when(pl.program_id(0) == l)
            def _(ref=ref):
                acc = ref[0].astype(F32)
                for i in range(1, N_DEV):
                    acc = acc + ref[i].astype(F32)
                g_ref[...] = acc

        gg = g_ref[...]
        m2 = ADAM_B1 * m_ref[...] + (1.0 - ADAM_B1) * gg
        v2 = ADAM_B2 * v_ref[...] + (1.0 - ADAM_B2) * jnp.square(gg)
        m_hat = m2 / (1.0 - ADAM_B1 ** ADAM_STEP)
        v_hat = v2 / (1.0 - ADAM_B2 ** ADAM_STEP)
        d_ref[...] = -ADAM_LR * (m_hat / (jnp.sqrt(v_hat) + ADAM_EPS) + ADAM_WD * w_ref[...])
        m2_ref[...] = m2
        v2_ref[...] = v2

    slot_spec = lambda layer: pl.BlockSpec((N_DEV, tr, cols), lambda l, i: (0, jnp.where(l == layer, i, (nb - 1) * (1 - layer)), 0))
    spec = pl.BlockSpec((None, tr, cols), lambda l, i: (l, i, 0))
    return pl.pallas_call(
        body, name=name, grid=(DEPTH, nb),
        in_specs=[slot_spec(0), slot_spec(1), spec, spec, spec], out_specs=[spec] * 4,
        out_shape=[jax.ShapeDtypeStruct(w.shape, F32)] * 4,
        compiler_params=_params(),
    )(slots[0], slots[1], w, m, v)


_MESH = pl.DeviceIdType.MESH
_ANY = pl.BlockSpec(memory_space=pl.ANY)


def _my_place():
    return lax.axis_index("x"), lax.axis_index("y"), lax.axis_index("c")


def _flip(place, k):
    x, y, c = place
    return (1 - x if k & 4 else x, 1 - y if k & 2 else y, 1 - c if k & 1 else c)


def _block_id(place):
    return 4 * place[0] + 2 * place[1] + place[2]


def _peer_copies(kind, in_refs, out_refs, send_sems, recv_sems, local_sems):
    me = _my_place()
    my = _block_id(me)
    remote, local = [], []
    for a, (x_ref, out_ref) in enumerate(zip(in_refs, out_refs)):
        src_of = (lambda place, r=x_ref: r) if kind == "gather" else (lambda place, r=x_ref: r.at[_block_id(place)])
        local.append(pltpu.make_async_copy(src_of(me), out_ref.at[my], local_sems.at[a]))
        for k in range(1, N_DEV):
            peer = _flip(me, k)
            remote.append(pltpu.make_async_remote_copy(
                src_ref=src_of(peer), dst_ref=out_ref.at[my], send_sem=send_sems.at[a * 7 + k - 1],
                recv_sem=recv_sems.at[a * 7 + k - 1], device_id=peer, device_id_type=_MESH))
    return remote, local


def _comm_out_shapes(kind, arrays):
    return [jax.ShapeDtypeStruct((N_DEV, *a.shape) if kind == "gather" else a.shape, a.dtype) for a in arrays]


def _comm_scratch(n):
    return [pltpu.SemaphoreType.DMA((7 * n,)), pltpu.SemaphoreType.DMA((7 * n,)), pltpu.SemaphoreType.DMA((n,))]


def _hosted_comm(kind, in_refs, out_refs, sems, first, last):
    if not in_refs:
        return

    @pl.when(first)
    def _():
        remote, local = _peer_copies(kind, in_refs, out_refs, *sems)
        for cp in local + remote:
            cp.start()

    @pl.when(last)
    def _():
        remote, local = _peer_copies(kind, in_refs, out_refs, *sems)
        for cp in remote:
            cp.wait()
        for cp in local:
            cp.wait()


def _two_level_gather_steps(in_refs, out_refs, send_sems, recv_sems, local_sems):
    n = len(in_refs)
    me = _my_place()
    x, y, c = me
    sibling = (x, y, 1 - c)
    chips = [(1 - x, y), (x, 1 - y), (1 - x, 1 - y)]

    def copy(a, k, place, to, src=None):
        block = out_refs[a].at[_block_id(place)]
        return pltpu.make_async_remote_copy(
            src_ref=block if src is None else src, dst_ref=block, send_sem=send_sems.at[7 * a + k],
            recv_sem=recv_sems.at[7 * a + k], device_id=to, device_id_type=_MESH)

    mine = [pltpu.make_async_copy(in_refs[a], out_refs[a].at[_block_id(me)], local_sems.at[a]) for a in range(n)]
    first = [copy(a, 0, me, sibling, src=in_refs[a]) for a in range(n)]
    first += [copy(a, 1 + j, me, (*chip, c), src=in_refs[a]) for a in range(n) for j, chip in enumerate(chips)]
    passed = [copy(a, 4 + j, (*chip, c), sibling) for a in range(n) for j, chip in enumerate(chips)]

    def send():
        for cp in mine + first:
            cp.start()

    def forward():
        for a in range(n):
            for j, chip in enumerate(chips):
                copy(a, 1 + j, (*chip, c), me).wait_recv()
                passed[3 * a + j].start()

    def finish():
        for a in range(n):
            copy(a, 0, sibling, me).wait_recv()
            for j, chip in enumerate(chips):
                copy(a, 4 + j, (*chip, 1 - c), me).wait_recv()
        for cp in first + passed:
            cp.wait_send()
        for cp in mine:
            cp.wait()

    return send, forward, finish


def _gather_two_level(arrays, name):
    n = len(arrays)

    def body(*refs):
        for step in _two_level_gather_steps(refs[:n], refs[n:2 * n], *refs[2 * n:]):
            step()

    return pl.pallas_call(
        body, name=name, out_shape=_comm_out_shapes("gather", arrays),
        in_specs=[_ANY] * n, out_specs=[_ANY] * n, scratch_shapes=_comm_scratch(n),
    )(*arrays)


def _hosted_gather(in_refs, out_refs, sems, first, middle, last):
    if not in_refs:
        return
    for when, index in ((first, 0), (middle, 1), (last, 2)):
        @pl.when(when)
        def _(index=index):
            _two_level_gather_steps(in_refs, out_refs, *sems)[index]()


def _comm(kind, arrays, name):
    n = len(arrays)

    def body(*refs):
        remote, local = _peer_copies(kind, refs[:n], refs[n:2 * n], *refs[2 * n:])
        for cp in local + remote:
            cp.start()
        for cp in remote:
            cp.wait()
        for cp in local:
            cp.wait()

    return pl.pallas_call(
        body, name=name, out_shape=_comm_out_shapes(kind, arrays),
        in_specs=[_ANY] * n, out_specs=[_ANY] * n, scratch_shapes=_comm_scratch(n),
    )(*arrays)


def _all_reduce_small(part):
    rows, lanes = part.shape
    vmem = pl.BlockSpec(memory_space=pltpu.VMEM)

    def body(x_ref, gath_ref, sum_ref, send_sems, recv_sems):
        me = _my_place()
        my = _block_id(me)
        gath_ref[my] = x_ref[...]
        copies = []
        for k in range(1, N_DEV):
            cp = pltpu.make_async_remote_copy(
                src_ref=x_ref, dst_ref=gath_ref.at[my], send_sem=send_sems.at[k - 1], recv_sem=recv_sems.at[k - 1],
                device_id=_flip(me, k), device_id_type=_MESH)
            cp.start()
            copies.append(cp)
        for cp in copies:
            cp.wait()
        acc = gath_ref[0]
        for i in range(1, N_DEV):
            acc = acc + gath_ref[i]
        sum_ref[...] = acc

    return pl.pallas_call(
        body, name="small_grad_all_reduce",
        out_shape=[jax.ShapeDtypeStruct((N_DEV, rows, lanes), F32), jax.ShapeDtypeStruct((rows, lanes), F32)],
        in_specs=[vmem], out_specs=[vmem, vmem],
        scratch_shapes=[pltpu.SemaphoreType.DMA((7,)), pltpu.SemaphoreType.DMA((7,))],
    )(part)[1]


_SHARDED = (("w_in", (D_MODEL, IN_PROJ // N_DEV)), ("w_uq", (Q_RANK // N_DEV, Q_RANK)), ("w_ukv", (KV_RANK, HEAD_PAD)),
            ("conv_w", (CONV_W, CONV_DIM // N_DEV)), ("w_out", (D_MODEL // N_DEV, D_MODEL)),
            ("w_up", (D_MODEL, D_FF // N_DEV)), ("w_down", (D_FF // N_DEV, D_MODEL)))
_SMALL = (("pre_mix_norm", D_MODEL), ("q_norm", Q_RANK), ("kv_norm", KV_RANK), ("conv_b", CONV_DIM), ("dt_bias", SSD_HEADS),
          ("a_log", SSD_HEADS), ("d_skip", SSD_HEADS), ("ssd_norm", SSD_INNER), ("post_mix_norm", D_MODEL),
          ("pre_mlp_norm", D_MODEL), ("post_mlp_norm", D_MODEL))
_WEIGHT_ORDER = ("pre_mix_norm", "w_in", "q_norm", "w_uq", "kv_norm", "w_ukv", "conv_w", "conv_b", "dt_bias", "a_log", "d_skip",
                 "ssd_norm", "w_out", "post_mix_norm", "pre_mlp_norm", "w_up", "w_down", "post_mlp_norm")
_EARLY = ("w_in", "w_uq", "w_ukv", "conv_w")
_LATE = ("w_out", "w_up", "w_down")


def _wire_shard(name, a):
    return lax.bitcast_convert_type(a, BF16).reshape(CONV_W, -1) if name == "conv_w" else a.astype(BF16)


def _from_wire(name, g):
    return lax.bitcast_convert_type(g.reshape(N_DEV, CONV_W, -1, 2), F32) if name == "conv_w" else g


def _cols(stacked):
    return jnp.transpose(stacked, (1, 0, 2)).reshape(stacked.shape[1], -1)


def _win_segments():
    s2, s3, s5 = Q_RANK + KV_RANK, Q_RANK + KV_RANK + QK_ROPE, IN_PROJ - SSD_HEADS
    return [(0, s2), (None, MISC_ROPE), (s2, s3), (s5, IN_PROJ), (None, HEAD_PAD - MISC_DT - SSD_HEADS), (s3, s5)]


def _win_from_shards(stacked):
    per = IN_PROJ // N_DEV
    parts = []
    for start, stop in _win_segments():
        if start is None:
            parts.append(jnp.zeros((D_MODEL, stop), stacked.dtype))
            continue
        while start < stop:
            j, a = divmod(start, per)
            b = min(per, a + stop - start)
            parts.append(stacked[j, :, a:b])
            start += b - a
    return jnp.concatenate(parts, axis=1)


def _win_grad_shards(dwin):
    per = IN_PROJ // N_DEV
    runs, at = [], 0
    for start, stop in _win_segments():
        if start is not None:
            runs.append((start, stop, at))
        at += stop if start is None else stop - start
    blocks = []
    for j in range(N_DEV):
        lo, hi = j * per, (j + 1) * per
        parts = [dwin[:, p + max(lo, a) - a:p + min(hi, b) - a] for a, b, p in sorted(runs) if max(lo, a) < min(hi, b)]
        blocks.append(jnp.concatenate(parts, axis=1))
    return jnp.stack(blocks)


def _early_weights(sh):
    win = _win_from_shards(sh["w_in"])
    w_uq = sh["w_uq"].reshape(Q_RANK, MLA_HEADS, QK_NOPE + QK_ROPE)
    wuq = jnp.pad(w_uq, ((0, 0), (0, 0), (0, HEAD_PAD - QK_NOPE - QK_ROPE))).reshape(Q_RANK, -1)
    w_ukv = _cols(sh["w_ukv"]).reshape(KV_RANK, MLA_HEADS, QK_NOPE + V_DIM)
    wkn = jnp.pad(w_ukv[..., :QK_NOPE], ((0, 0), (0, 0), (0, HEAD_PAD - QK_NOPE))).reshape(KV_RANK, -1)
    wv = w_ukv[..., QK_NOPE:].reshape(KV_RANK, 4, 2, 1, V_DIM) * jnp.eye(2, dtype=BF16).reshape(1, 1, 2, 2, 1)
    wkv = jnp.concatenate([wkn, wv.reshape(KV_RANK, -1)], axis=1)
    return dict(win=win, wuq=wuq, wkv=wkv, conv_w=_cols(sh["conv_w"]))


def _late_weights(sh):
    w_out = sh["w_out"].reshape(D_MODEL, D_MODEL)
    watt = w_out[:SSD_INNER].reshape(4, 2, 1, V_DIM, D_MODEL) * jnp.eye(2, dtype=BF16).reshape(1, 2, 2, 1, 1)
    wout = jnp.concatenate([watt.reshape(MLA_HEADS * HEAD_PAD, D_MODEL), w_out[SSD_INNER:]], axis=0)
    return dict(wout=wout, wup=sh["w_up"], wdown=sh["w_down"])


def _shard_grads(g):
    out = {}
    if "wup" in g:
        out["w_up"], out["w_down"] = g["wup"], g["wdown"]
        ae = g["wout_att"].reshape(4, 2, 2, V_DIM, D_MODEL)
        att = jnp.stack([ae[:, 0, 0], ae[:, 1, 1]], axis=1).reshape(SSD_INNER, D_MODEL)
        out["w_out"] = jnp.concatenate([att, g["wout_ssd"]], axis=0).astype(BF16).reshape(N_DEV, D_MODEL // N_DEV, D_MODEL)
    if "win" not in g:
        return out
    out["w_in"] = _win_grad_shards(g["win"].astype(BF16))
    w_uq = g["wuq"].astype(BF16).reshape(Q_RANK, MLA_HEADS, HEAD_PAD)[..., :QK_NOPE + QK_ROPE].reshape(Q_RANK, Q_RANK)
    out["w_uq"] = w_uq.reshape(N_DEV, Q_RANK // N_DEV, Q_RANK)
    wide = MLA_HEADS * HEAD_PAD
    wkv = g["wkv"].astype(BF16)
    kn = wkv[:, :wide].reshape(KV_RANK, MLA_HEADS, HEAD_PAD)[..., :QK_NOPE]
    ve = wkv[:, wide:].reshape(KV_RANK, 4, 2, 2, V_DIM)
    vv = jnp.stack([ve[:, :, 0, 0], ve[:, :, 1, 1]], axis=2).reshape(KV_RANK, MLA_HEADS, V_DIM)
    out["w_ukv"] = jnp.transpose(jnp.concatenate([kn, vv], axis=-1), (1, 0, 2))
    out["conv_w"] = jnp.transpose(g["conv_w"].astype(BF16).reshape(CONV_W, N_DEV, -1), (1, 0, 2))
    return out


def _small_rows(n):
    return -(-n // 1024) * 8


def _pack_small(vals):
    rows = []
    for l in range(DEPTH):
        for name, n in _SMALL:
            r = _small_rows(n)
            rows.append(jnp.pad(vals[name][l].reshape(-1), (0, r * 128 - n)).reshape(r, 128))
    return jnp.concatenate(rows, axis=0)


def _unpack_small(packed):
    out, off = {name: [] for name, _ in _SMALL}, 0
    for l in range(DEPTH):
        for name, n in _SMALL:
            r = _small_rows(n)
            out[name].append(packed[off:off + r].reshape(-1)[:n])
            off += r
    return {name: jnp.stack(v) for name, v in out.items()}


def _lane_rows(vec8):
    return jnp.repeat(vec8, SSD_P).reshape(1, SSD_INNER)


def _layer_fwd(h, kw, sm, l, cosf, sinf, consts, gather=(), after_gather=None, target=None):
    row = lambda name: sm[name][l].reshape(1, -1)
    t = {}
    t["h0"] = h
    (t["ub"], t["cq"], t["ckv"], t["misc"], t["z"], t["xraw"], t["cqn"], t["ckvn"], t["q"], t["k"], t["v"]) = _inproj_qkv_fwd(
        h, row("pre_mix_norm"), kw["win"], row("q_norm"), row("kv_norm"), kw["wuq"], kw["wkv"], cosf, sinf)
    t["oe"], t["lse"], gathered = _attn_fwd(t["q"], t["k"], t["v"], gather)
    if after_gather is not None:
        after_gather(gathered)
    t["dtb"] = _lane_rows(sm["dt_bias"][l])
    t["a_exp"] = _lane_rows(-jnp.exp(sm["a_log"][l]))
    t["d_exp"] = _lane_rows(sm["d_skip"][l])
    t["c"], t["prev"], t["ypre"], t["yssd"], t["mixed"], t["h1"] = _ssd_outproj_fwd(
        t["xraw"], t["misc"], t["z"], kw["conv_w"], row("conv_b"), t["dtb"], t["a_exp"], t["d_exp"], row("ssd_norm"), consts,
        t["oe"], h, kw["wout"], row("post_mix_norm"))
    t["mb"], t["ab"], t["d"], *out = _mlp_fwd(t["h1"], row("pre_mlp_norm"), kw["wup"], kw["wdown"], row("post_mlp_norm"), target)
    return out, t


def _layer_bwd(dh2, t, kw, sm, l, cosf, sinf, consts, exchange_of=None):
    row = lambda name: sm[name][l].reshape(1, -1)
    g, gs = {}, {}
    dh1, dab, ddb, gs["post_mlp_norm"], gs["pre_mlp_norm"] = _mlp_bwd(
        dh2, t["d"], t["h1"], t["ab"], row("pre_mlp_norm"), kw["wup"], kw["wdown"], row("post_mlp_norm"))
    g["wup"] = _matmul_tn_stacked(t["mb"], dab, f"dw_up_{l}", a_stacked=False)
    g["wdown"] = _matmul_tn_stacked(t["ab"], ddb, f"dw_down_{l}", a_stacked=True, square_a=True)
    dmixb, doe, dyssd, gs["post_mix_norm"], delta = _outproj_bwd(dh1, t["mixed"], row("post_mix_norm"), kw["wout"], t["oe"])
    g["wout_att"] = _matmul_tn(t["oe"], dmixb, f"dw_out_att_{l}")
    g["wout_ssd"] = _matmul_tn(t["yssd"], dmixb, f"dw_out_ssd_{l}")
    dz, dxraw, dmisc_dt, gs["ssd_norm"], gd, galog, gdtb, g["conv_w"], gs["conv_b"] = _ssd_bwd(
        dyssd, t["ypre"], t["z"], t["c"], t["xraw"], t["misc"], t["prev"], kw["conv_w"], t["dtb"], t["a_exp"], t["d_exp"],
        row("ssd_norm"), consts)
    gs["d_skip"] = jnp.sum(gd.reshape(SSD_HEADS, SSD_P), axis=1)
    gs["a_log"] = galog[0, MISC_DT:MISC_DT + SSD_HEADS]
    gs["dt_bias"] = gdtb[0, MISC_DT:MISC_DT + SSD_HEADS]
    dq, dk, dv, exchanged = _attn_bwd(t["q"], t["k"], t["v"], doe, t["lse"], delta,
                                      exchange_of(g) if exchange_of is not None else ())
    dqb, dkvb, dprojb, dh0, gs["q_norm"], gs["kv_norm"], gs["pre_mix_norm"] = _qkv_inproj_bwd(
        dq, dk, dv, t["cq"], t["ckv"], dmisc_dt, dz, dxraw, t["h0"], dh1, row("q_norm"), row("kv_norm"),
        row("pre_mix_norm"), kw["wuq"], kw["wkv"], kw["win"], cosf, sinf)
    g["wuq"] = _matmul_tn(t["cqn"], dqb, f"dw_uq_{l}")
    g["wkv"] = _matmul_tn(t["ckvn"], dkvb, f"dw_kv_{l}")
    g["win"] = _matmul_tn(t["ub"], dprojb, f"dw_in_{l}")
    return dh0, g, {k: v.reshape(-1) for k, v in gs.items()}, exchanged


def _local_step(x, positions, kws, sm, target, gather=(), after_gather=None, exchange_of=None):
    inv_freq = ROPE_THETA ** (-jnp.arange(0, QK_ROPE, 2, dtype=F32) / QK_ROPE)
    invf = jnp.zeros((HEAD_PAD,), F32).at[MISC_ROPE:MISC_ROPE + QK_ROPE].set(jnp.concatenate([inv_freq, inv_freq]))
    cosf, sinf = _rope_tables(positions.reshape(-1, 1), invf.reshape(1, HEAD_PAD))
    consts = _ssd_consts()
    (h,), t0 = _layer_fwd(x, kws[0], sm, 0, cosf, sinf, consts, gather, after_gather)
    (dh, loss), t1 = _layer_fwd(h, kws[1], sm, 1, cosf, sinf, consts, target=target)
    saved = [t0, t1]
    grads, small, exchanged = [None] * DEPTH, [None] * DEPTH, []
    for l in reversed(range(DEPTH)):
        hook = (lambda g0: exchange_of(g0, grads[1])) if (l == 0 and exchange_of is not None) else None
        dh, grads[l], small[l], got = _layer_bwd(dh, saved[l], kws[l], sm, l, cosf, sinf, consts, hook)
        exchanged = got or exchanged
    return loss[0, 0], dh, grads, small, exchanged


def kernel(x, positions, pre_mix_norm, w_in, q_norm, w_uq, kv_norm, w_ukv, conv_w, conv_b, dt_bias, a_log, d_skip, ssd_norm, w_out, post_mix_norm, pre_mlp_norm, w_up, w_down, post_mlp_norm, loss_target, m_pre_mix_norm, m_w_in, m_q_norm, m_w_uq, m_kv_norm, m_w_ukv, m_conv_w, m_conv_b, m_dt_bias, m_a_log, m_d_skip, m_ssd_norm, m_w_out, m_post_mix_norm, m_pre_mlp_norm, m_w_up, m_w_down, m_post_mlp_norm, v_pre_mix_norm, v_w_in, v_q_norm, v_w_uq, v_kv_norm, v_w_ukv, v_conv_w, v_conv_b, v_dt_bias, v_a_log, v_d_skip, v_ssd_norm, v_w_out, v_post_mix_norm, v_pre_mlp_norm, v_w_up, v_w_down, v_post_mlp_norm):
    w = dict(pre_mix_norm=pre_mix_norm, w_in=w_in, q_norm=q_norm, w_uq=w_uq, kv_norm=kv_norm, w_ukv=w_ukv, conv_w=conv_w,
             conv_b=conv_b, dt_bias=dt_bias, a_log=a_log, d_skip=d_skip, ssd_norm=ssd_norm, w_out=w_out,
             post_mix_norm=post_mix_norm, pre_mlp_norm=pre_mlp_norm, w_up=w_up, w_down=w_down, post_mlp_norm=post_mlp_norm)
    m = dict(pre_mix_norm=m_pre_mix_norm, w_in=m_w_in, q_norm=m_q_norm, w_uq=m_w_uq, kv_norm=m_kv_norm, w_ukv=m_w_ukv,
             conv_w=m_conv_w, conv_b=m_conv_b, dt_bias=m_dt_bias, a_log=m_a_log, d_skip=m_d_skip, ssd_norm=m_ssd_norm,
             w_out=m_w_out, post_mix_norm=m_post_mix_norm, pre_mlp_norm=m_pre_mlp_norm, w_up=m_w_up, w_down=m_w_down,
             post_mlp_norm=m_post_mlp_norm)
    v = dict(pre_mix_norm=v_pre_mix_norm, w_in=v_w_in, q_norm=v_q_norm, w_uq=v_w_uq, kv_norm=v_kv_norm, w_ukv=v_w_ukv,
             conv_w=v_conv_w, conv_b=v_conv_b, dt_bias=v_dt_bias, a_log=v_a_log, d_skip=v_d_skip, ssd_norm=v_ssd_norm,
             w_out=v_w_out, post_mix_norm=v_post_mix_norm, pre_mlp_norm=v_pre_mlp_norm, w_up=v_w_up, w_down=v_w_down,
             post_mlp_norm=v_post_mlp_norm)
    sm = {name: w[name] for name, _ in _SMALL}

    wire = lambda name, l: _wire_shard(name, w[name][l])
    first = _gather_two_level([wire(name, 0) for name in _EARLY], "weight_gather_first")
    kws = [_early_weights({name: _from_wire(name, a) for name, a in zip(_EARLY, first)}), None]
    behind = [(name, 0) for name in _LATE] + [(name, 1) for name, _ in _SHARDED]

    def after_gather(gathered):
        got = {key: _from_wire(key[0], a) for key, a in zip(behind, gathered)}
        kws[0].update(_late_weights({name: got[name, 0] for name in _LATE}))
        kws[1] = {**_early_weights({name: got[name, 1] for name in _EARLY}),
                  **_late_weights({name: got[name, 1] for name in _LATE})}

    sent_behind = [(name, 1) for name, _ in _SHARDED] + [(name, 0) for name in _LATE]

    def exchange_of(g0, g1):
        blocks = {**{(name, 1): a for name, a in _shard_grads(g1).items()},
                  **{(name, 0): a for name, a in _shard_grads(g0).items()}}
        return [blocks[key] for key in sent_behind]

    loss_part, dx, grads, small, exchanged = _local_step(
        x[0], positions[0], kws, sm, loss_target[0], [wire(*key) for key in behind], after_gather, exchange_of)
    slots = dict(zip(sent_behind, exchanged))
    last = _shard_grads({k: grads[0][k] for k in ("win", "wuq", "wkv", "conv_w")})
    slots.update({(name, 0): a for name, a in zip(_EARLY, _comm("exchange", [last[name] for name in _EARLY], "grad_exchange_last"))})
    loss_tile = jnp.zeros((8, 128), F32).at[0, 0].set(loss_part)
    reduced = _all_reduce_small(jnp.concatenate(
        [_pack_small({name: jnp.stack([small[l][name] for l in range(DEPTH)]) for name, _ in _SMALL}), loss_tile], axis=0))
    g_small = _unpack_small(reduced)
    loss = reduced[-8, 0]

    grad, delta, new_m, new_v = {}, {}, {}, {}
    for name, _ in _SHARDED:
        grad[name], delta[name], new_m[name], new_v[name] = _sum_adamw(
            [slots[name, 0], slots[name, 1]], w[name], m[name], v[name], f"sum_adamw_{name}")
    pk = lambda d: _pack_small({name: d[name] for name, _ in _SMALL})
    d_, m_, v_ = _adamw(pk(w), pk(g_small), pk(m), pk(v), "adamw_small")
    for dst, packed in ((delta, d_), (new_m, m_), (new_v, v_)):
        dst.update(_unpack_small(packed))
    grad.update(g_small)

    outs = [loss, dx[None]]
    for d in (grad, delta, new_m, new_v):
        outs += [d[name] for name in _WEIGHT_ORDER]
    return tuple(outs)
```

```python
import jax
import jax.numpy as jnp
import numpy as np
from jax import lax
from jax.experimental import pallas as pl
from jax.experimental.pallas import tpu as pltpu

F32 = jnp.float32
BF16 = jnp.bfloat16

D_MODEL = 1024
DEPTH = 2
N_DEV = 8
CHUNK = 64
EPS = 1e-6
MLA_HEADS = 8
QK_NOPE = 64
QK_ROPE = 32
V_DIM = 64
Q_RANK = 768
KV_RANK = 256
ROPE_THETA = 10000.0
SSD_HEADS = 8
SSD_P = 64
SSD_INNER = 512
SSD_GROUPS = 2
SSD_N = 128
CONV_W = 4
CONV_DIM = 1024
D_FF = 4096
IN_PROJ = 2600
HEAD_PAD = 128
IN_PAD = 2688
MISC_ROPE = 64
MISC_DT = 96
ATT_SCALE = (QK_NOPE + QK_ROPE) ** -0.5
LOG2E = 1.4426950408889634
ATT_SCALE_LOG2 = ATT_SCALE * LOG2E

ADAM_LR = 0.001
ADAM_B1 = 0.9
ADAM_B2 = 0.999
ADAM_EPS = 1e-08
ADAM_WD = 0.01
ADAM_STEP = 10

TM = 512
ATT_T = 512
ATT_G = 8
ATT_UNROLL = 4
SSD_ROWS = 512
TK_DW = 4096
V7X_VMEM_BYTES = 64 * 1024 * 1024
VMEM_LIMIT = V7X_VMEM_BYTES - 8 * 1024 * 1024

_NT = (((1,), (1,)), ((), ()))
_TN = (((0,), (0,)), ((), ()))


def _params(**kw):
    return pltpu.CompilerParams(vmem_limit_bytes=VMEM_LIMIT, **kw)


def _dot(a, b, precision=None):
    return jnp.dot(a, b, preferred_element_type=F32, precision=precision)


def _dot_nt(a, b, precision=None):
    return lax.dot_general(a, b, _NT, preferred_element_type=F32, precision=precision)


def _dot_tn(a, b, precision=None):
    return lax.dot_general(a, b, _TN, preferred_element_type=F32, precision=precision)


def _split3(x):
    hi = x.astype(BF16)
    r = x - hi.astype(F32)
    mid = r.astype(BF16)
    return hi, mid, (r - mid.astype(F32)).astype(BF16)


def _dot01(x, m01, dot=_dot, left=False):
    parts = [dot(m01, p) if left else dot(p, m01) for p in _split3(x)]
    return parts[0] + parts[1] + parts[2]


def _full(shape):
    n = len(shape)
    return pl.BlockSpec(shape, lambda *_: (0,) * n)


def _resident(shape):
    n = len(shape)
    return pl.BlockSpec(shape, lambda *_: (0,) * n, pipeline_mode=pl.Buffered(1))


def _rows(tm, width):
    return pl.BlockSpec((tm, width), lambda i: (i, 0))


def _rms_fwd(x, w):
    r = lax.rsqrt(jnp.mean(x * x, axis=-1, keepdims=True) + EPS)
    return (x * r) * w


def _rms_bwd(x, w, dy):
    r = lax.rsqrt(jnp.mean(x * x, axis=-1, keepdims=True) + EPS)
    xh = x * r
    dxn = dy * w
    dx = r * (dxn - xh * jnp.mean(dxn * xh, axis=-1, keepdims=True))
    return dx, dy * xh


def _acc_rows(ref, val, first):
    s = jnp.sum(val, axis=0, keepdims=True)

    @pl.when(first)
    def _():
        ref[...] = s

    @pl.when(jnp.logical_not(first))
    def _():
        ref[...] += s


def _rope(t, cosf, sinf, sign):
    lane = lax.broadcasted_iota(jnp.int32, t.shape, 1)
    rot = jnp.where(lane < MISC_ROPE + QK_ROPE // 2, -pltpu.roll(t, HEAD_PAD - QK_ROPE // 2, 1), pltpu.roll(t, QK_ROPE // 2, 1))
    return t * cosf + sign * (rot * sinf)


def _rope_tables(pos, invf):
    s = pos.shape[0]

    def body(pos_ref, invf_ref, cos_ref, sin_ref):
        ang = pos_ref[...].astype(F32) * invf_ref[...]
        cos_ref[...] = jnp.cos(ang)
        sin_ref[...] = jnp.sin(ang)

    return pl.pallas_call(
        body, name="rope_tables", grid=(s // TM,),
        in_specs=[_rows(TM, 1), _full((1, HEAD_PAD))],
        out_specs=[_rows(TM, HEAD_PAD), _rows(TM, HEAD_PAD)],
        out_shape=[jax.ShapeDtypeStruct((s, HEAD_PAD), F32)] * 2,
    )(pos, invf)


def _inproj_qkv_fwd(h, nw, win, qnw, kvnw, wuq, wkv, cosf, sinf):
    s = h.shape[0]

    def body(h_ref, nw_ref, w_ref, qnw_ref, kvnw_ref, wuq_ref, wkv_ref, cos_ref, sin_ref,
             ub_ref, cq_ref, ckv_ref, misc_ref, z_ref, xbc_ref, cqn_ref, ckvn_ref, q_ref, k_ref, v_ref):
        ub = _rms_fwd(h_ref[...], nw_ref[...]).astype(BF16)
        ub_ref[...] = ub
        proj = _dot(ub, w_ref[...])
        cq, ckv, m = proj[:, 0:768], proj[:, 768:1024], proj[:, 1024:1152]
        cq_ref[...] = cq
        ckv_ref[...] = ckv
        misc_ref[...] = m
        z_ref[...] = proj[:, 1152:1664]
        xbc_ref[...] = proj[:, 1664:2688]
        cosf, sinf = cos_ref[...], sin_ref[...]
        cqn = _rms_fwd(cq, qnw_ref[...]).astype(BF16)
        cqn_ref[...] = cqn
        q = _dot(cqn, wuq_ref[...])
        ckvn = _rms_fwd(ckv, kvnw_ref[...]).astype(BF16)
        ckvn_ref[...] = ckvn
        kv = _dot(ckvn, wkv_ref[...])
        lane = lax.broadcasted_iota(jnp.int32, m.shape, 1)
        in_rope = jnp.logical_and(lane >= MISC_ROPE, lane < MISC_ROPE + QK_ROPE)
        kr = jnp.where(in_rope, _rope(m, cosf, sinf, 1.0), 0.0)
        for hd in range(MLA_HEADS):
            cols = slice(hd * HEAD_PAD, (hd + 1) * HEAD_PAD)
            q_ref[:, cols] = _rope(q[:, cols], cosf, sinf, 1.0).astype(BF16)
            k_ref[:, cols] = (kv[:, cols] + kr).astype(BF16)
        vv = kv[:, MLA_HEADS * HEAD_PAD:]
        vlane = lax.broadcasted_iota(jnp.int32, vv.shape, 1)
        ones_at = jnp.where((vlane // HEAD_PAD) % 2 == 0, V_DIM, 0)
        v_ref[...] = jnp.where(vlane % HEAD_PAD == ones_at, 1.0, vv).astype(BF16)

    wide = MLA_HEADS * HEAD_PAD
    widths = (Q_RANK, KV_RANK, HEAD_PAD, SSD_INNER, CONV_DIM)
    return pl.pallas_call(
        body, name="inproj_qkv_fwd", grid=(s // TM,),
        in_specs=[_rows(TM, D_MODEL), _full((1, D_MODEL)), _resident((D_MODEL, IN_PAD)), _full((1, Q_RANK)), _full((1, KV_RANK)),
                  _resident((Q_RANK, wide)), _resident((KV_RANK, 2 * wide)), _rows(TM, HEAD_PAD), _rows(TM, HEAD_PAD)],
        out_specs=[_rows(TM, D_MODEL)] + [_rows(TM, w) for w in widths]
        + [_rows(TM, Q_RANK), _rows(TM, KV_RANK), _rows(TM, wide), _rows(TM, wide), _rows(TM, wide)],
        out_shape=[jax.ShapeDtypeStruct((s, D_MODEL), BF16)] + [jax.ShapeDtypeStruct((s, w), F32) for w in widths]
        + [jax.ShapeDtypeStruct((s, Q_RANK), BF16), jax.ShapeDtypeStruct((s, KV_RANK), BF16)]
        + [jax.ShapeDtypeStruct((s, wide), BF16)] * 3,
        compiler_params=_params(),
    )(h, nw, win, qnw, kvnw, wuq, wkv, cosf, sinf)


def _chunk_bias(t, keys_on_rows=False):
    row = lax.broadcasted_iota(jnp.int32, (t, 1), 0) // CHUNK
    col = lax.broadcasted_iota(jnp.int32, (1, t), 1) // CHUNK
    return jnp.where((row <= col) if keys_on_rows else (col <= row), 0.0, -jnp.inf).astype(F32)


def _attn_fwd(q, k, v, gather=()):
    s = q.shape[0]
    t = ATT_T
    nq = s // t
    pair = ATT_G * HEAD_PAD
    ng = len(gather)

    def body(q_ref, k_ref, v_ref, *rest):
        g_in, (o_ref, lse_ref), g_out = rest[:ng], rest[ng:ng + 2], rest[ng + 2:2 * ng + 2]
        m_s, acc_s, bias_s = rest[2 * ng + 2:2 * ng + 5]
        qi = pl.program_id(1)
        group, groups = pl.program_id(0), MLA_HEADS // ATT_G

        @pl.when(jnp.logical_and(group == 0, qi == 0))
        def _():
            bias_s[...] = _chunk_bias(t)

        _hosted_gather(g_in, g_out, rest[2 * ng + 5:],
                       jnp.logical_and(group == 0, qi == 0),
                       jnp.logical_and(group == groups - 1, qi == min(3 * nq // 4 + 1, nq - 1)),
                       jnp.logical_and(group == groups - 1, qi == nq - 1))
        m_s[...] = jnp.full(m_s.shape, -jnp.inf, F32)
        acc_s[...] = jnp.zeros(acc_s.shape, F32)

        def step(kb, masked):
            r0 = pl.multiple_of(kb * t, t)

            def scores(hh):
                cols = slice(hh * HEAD_PAD, (hh + 1) * HEAD_PAD)
                return _dot_nt(q_ref[:, cols], k_ref[pl.ds(r0, t), cols])

            def soft(hh, raw):
                sc = raw * ATT_SCALE_LOG2
                if masked:
                    sc = sc + bias_s[...]
                m_old = m_s[hh]
                m_new = jnp.maximum(m_old, jnp.max(sc, axis=-1, keepdims=True))
                alpha = jnp.exp2(m_old - m_new)
                p = jnp.exp2(sc - jnp.tile(m_new, (1, t // HEAD_PAD)))
                m_s[hh] = m_new
                return alpha, p.astype(BF16)

            def update(hh, alpha, p):
                cols = slice(hh * HEAD_PAD, (hh + 1) * HEAD_PAD)
                acc_s[hh] = alpha * acc_s[hh] + _dot(p, v_ref[pl.ds(r0, t), cols])

            raw, ap = [None] * ATT_G, [None] * ATT_G
            raw[0] = scores(0)
            for hh in range(ATT_G):
                if hh + 1 < ATT_G:
                    raw[hh + 1] = scores(hh + 1)
                ap[hh] = soft(hh, raw[hh])
                if hh >= 1:
                    update(hh - 1, *ap[hh - 1])
            update(ATT_G - 1, *ap[ATT_G - 1])

        def loop(i, c):
            step(2 * i, False)
            step(2 * i + 1, False)
            return c

        lax.fori_loop(0, qi // 2, loop, 0)

        @pl.when(qi % 2 == 1)
        def _():
            step(qi - 1, False)

        step(qi, True)
        for hh in range(ATT_G):
            cols = slice(hh * HEAD_PAD, (hh + 1) * HEAD_PAD)
            acc = acc_s[hh]
            ones_at = V_DIM * (1 - hh % 2)
            l = jnp.broadcast_to(acc[:, ones_at:ones_at + 1], acc.shape)
            o_ref[:, cols] = (acc / l).astype(BF16)
            lse_ref[hh] = (m_s[hh] + jnp.log(l) * LOG2E).T[0:8, :]

    outs = pl.pallas_call(
        body, name="attn_fwd_gather" if ng else "attn_fwd", grid=(MLA_HEADS // ATT_G, nq),
        in_specs=[pl.BlockSpec((t, pair), lambda h, i: (i, h)),
                  pl.BlockSpec((s, pair), lambda h, i: (0, h), pipeline_mode=pl.Buffered(1)),
                  pl.BlockSpec((s, pair), lambda h, i: (0, h), pipeline_mode=pl.Buffered(1))] + [_ANY] * ng,
        out_specs=[pl.BlockSpec((t, pair), lambda h, i: (i, h)),
                   pl.BlockSpec((ATT_G, 8, t), lambda h, i: (h, 0, i))] + [_ANY] * ng,
        out_shape=[jax.ShapeDtypeStruct((s, MLA_HEADS * HEAD_PAD), BF16), jax.ShapeDtypeStruct((MLA_HEADS, 8, s), F32)]
        + _comm_out_shapes("gather", gather),
        scratch_shapes=[pltpu.VMEM((ATT_G, t, HEAD_PAD), F32), pltpu.VMEM((ATT_G, t, HEAD_PAD), F32), pltpu.VMEM((t, t), F32)]
        + (_comm_scratch(ng) if ng else []),
        compiler_params=_params(),
    )(q, k, v, *gather)
    return outs[0], outs[1], list(outs[2:])


def _interleave(stages):
    live = list(stages)
    while live:
        still = []
        for g in live:
            try:
                next(g)
                still.append(g)
            except StopIteration:
                pass
        live = still


def _ssd_consts():
    emisc = np.zeros((HEAD_PAD, SSD_INNER), np.float32)
    for hd in range(SSD_HEADS):
        emisc[MISC_DT + hd, hd * SSD_P:(hd + 1) * SSD_P] = 1.0
    idx = np.arange(CHUNK)
    tri = (idx[:, None] >= idx[None, :]).astype(np.float32)
    return tuple(jnp.asarray(m, BF16) for m in (emisc, emisc.T.copy(), tri, tri.T.copy()))


def _ssd_chunk_common(cc, misc, emisc, tri, trit, dtb, a_exp):
    sig = jax.nn.sigmoid(cc)
    xa = cc * sig
    dt = jax.nn.softplus(_dot01(misc, emisc) + dtb)
    a = dt * a_exp
    acs = _dot01(a, tri, left=True)
    acs_t = _dot01(a, trit, dot=_dot_tn)
    alast = acs[CHUNK - 1:CHUNK, :]
    return xa, sig, dt, acs, acs_t, alast


def _decay(acs, acs_t, hd):
    row = lax.broadcasted_iota(jnp.int32, (CHUNK, CHUNK), 0)
    col = lax.broadcasted_iota(jnp.int32, (CHUNK, CHUNK), 1)
    diff = acs[:, hd * SSD_P:hd * SSD_P + 1] - acs_t[hd * SSD_P:hd * SSD_P + 1, :]
    return jnp.exp(jnp.where(row >= col, diff, -jnp.inf))


def _half_mask(hh):
    lane = lax.broadcasted_iota(jnp.int32, (CHUNK, 2 * SSD_P), 1)
    return (lane >= SSD_P) if hh else (lane < SSD_P)


def _gate_norm(y, zz):
    sg = jax.nn.sigmoid(zz)
    yz = y * (zz * sg)
    outs, rs = [], []
    half = SSD_INNER // SSD_GROUPS
    for g in range(SSD_GROUPS):
        yg = yz[:, g * half:(g + 1) * half]
        r = lax.rsqrt(jnp.mean(yg * yg, axis=-1, keepdims=True) + EPS)
        outs.append(yg * r)
        rs.append(r)
    return sg, jnp.concatenate(outs, axis=1), rs


def _ssd_outproj_fwd(xraw, misc, z, cw, cb, dtb, a_exp, d_exp, nw, consts, oe, h, wout, post_w):
    s = xraw.shape[0]
    nb = s // SSD_ROWS
    ncb = SSD_ROWS // CHUNK
    emisc, _, tri, trit = consts
    wide = MLA_HEADS * HEAD_PAD

    def body(x_ref, misc_ref, z_ref, cw_ref, cb_ref, dtb_ref, a_ref, d_ref, nw_ref, emisc_ref, tri_ref, trit_ref,
             oe_ref, h_ref, wout_ref, postw_ref, c_ref, prev_ref, ypre_ref, yssd_ref, mixed_ref, h1_ref, tail_s, state_s):
        i = pl.program_id(0)

        @pl.when(i == 0)
        def _():
            tail_s[...] = jnp.zeros(tail_s.shape, F32)
            state_s[...] = jnp.zeros(state_s.shape, F32)

        mixed_att = _dot(oe_ref[...], wout_ref[0:wide, :])
        x = x_ref[...]
        xext = jnp.concatenate([tail_s[...], x], axis=0)
        acc = x * cw_ref[CONV_W - 1:CONV_W, :] + cb_ref[...]
        for j in range(1, CONV_W):
            acc = acc + pltpu.roll(xext, j, 0)[8:, :] * cw_ref[CONV_W - 1 - j:CONV_W - j, :]
        tail_s[...] = x[SSD_ROWS - 8:, :]
        c_ref[...] = acc

        def chunk(ci):
            r0 = ci * CHUNK
            xa, _, dt, acs, acs_t, alast = _ssd_chunk_common(
                c_ref[pl.ds(r0, CHUNK), :], misc_ref[pl.ds(r0, CHUNK), :], emisc_ref[...], tri_ref[...], trit_ref[...],
                dtb_ref[...], a_ref[...])
            yield
            xs = xa[:, :SSD_INNER]
            xdt = xs * dt
            wgt = (xdt * jnp.exp(alast - acs)).astype(BF16)
            e = jnp.exp(acs)
            ys, new_states, cms = [], [], []
            for g in range(SSD_GROUPS):
                bm = xa[:, SSD_INNER + g * SSD_N:SSD_INNER + (g + 1) * SSD_N].astype(BF16)
                cm = xa[:, SSD_INNER + SSD_GROUPS * SSD_N + g * SSD_N:SSD_INNER + SSD_GROUPS * SSD_N + (g + 1) * SSD_N].astype(BF16)
                cms.append(cm)
                cb_g = _dot_nt(cm, bm)
                gl = slice(g * 256, (g + 1) * 256)
                new_states.append(_dot_tn(bm, wgt[:, gl]))
                for jj in range(2):
                    pair = 2 * g + jj
                    xp = xdt[:, pair * 128:(pair + 1) * 128]
                    yp = None
                    for hh in range(2):
                        sc = (cb_g * _decay(acs, acs_t, 2 * pair + hh)).astype(BF16)
                        term = _dot(sc, jnp.where(_half_mask(hh), xp, 0.0).astype(BF16))
                        yp = term if yp is None else yp + term
                    ys.append(yp)
                yield
            prev = state_s[...]
            prev_ref[ci] = prev
            yoff = jnp.concatenate([_dot(cms[g], prev[:, g * 256:(g + 1) * 256].astype(BF16)) for g in range(SSD_GROUPS)],
                                   axis=1) * e
            state_s[...] = prev * jnp.exp(alast) + jnp.concatenate(new_states, axis=1)
            yield
            y = jnp.concatenate(ys, axis=1) + yoff + d_ref[...] * xs
            ypre_ref[pl.ds(r0, CHUNK), :] = y
            _, yn, _ = _gate_norm(y, z_ref[pl.ds(r0, CHUNK), :])
            yssd_ref[pl.ds(r0, CHUNK), :] = (yn * nw_ref[...]).astype(BF16)

        _interleave([chunk(ci) for ci in range(ncb)])
        mixed = mixed_att + _dot(yssd_ref[...], wout_ref[wide:, :])
        mixed_ref[...] = mixed
        h1_ref[...] = h_ref[...] + _rms_fwd(mixed, postw_ref[...])

    return pl.pallas_call(
        body, name="ssd_outproj_fwd", grid=(nb,),
        in_specs=[_rows(SSD_ROWS, CONV_DIM), _rows(SSD_ROWS, HEAD_PAD), _rows(SSD_ROWS, SSD_INNER),
                  _full((CONV_W, CONV_DIM)), _full((1, CONV_DIM)), _full((1, SSD_INNER)), _full((1, SSD_INNER)),
                  _full((1, SSD_INNER)), _full((1, SSD_INNER)), _full((HEAD_PAD, SSD_INNER)), _full((CHUNK, CHUNK)),
                  _full((CHUNK, CHUNK)), _rows(SSD_ROWS, wide), _rows(SSD_ROWS, D_MODEL),
                  _resident((wide + SSD_INNER, D_MODEL)), _full((1, D_MODEL))],
        out_specs=[_rows(SSD_ROWS, CONV_DIM), pl.BlockSpec((ncb, SSD_N, SSD_INNER), lambda i: (i, 0, 0)),
                   _rows(SSD_ROWS, SSD_INNER), _rows(SSD_ROWS, SSD_INNER), _rows(SSD_ROWS, D_MODEL), _rows(SSD_ROWS, D_MODEL)],
        out_shape=[jax.ShapeDtypeStruct((s, CONV_DIM), F32), jax.ShapeDtypeStruct((s // CHUNK, SSD_N, SSD_INNER), F32),
                   jax.ShapeDtypeStruct((s, SSD_INNER), F32), jax.ShapeDtypeStruct((s, SSD_INNER), BF16),
                   jax.ShapeDtypeStruct((s, D_MODEL), F32), jax.ShapeDtypeStruct((s, D_MODEL), F32)],
        scratch_shapes=[pltpu.VMEM((8, CONV_DIM), F32), pltpu.VMEM((SSD_N, SSD_INNER), F32)],
        compiler_params=_params(),
    )(xraw, misc, z, cw, cb, dtb, a_exp, d_exp, nw, emisc, tri, trit, oe, h, wout, post_w)


def _mlp_fwd(h1, prew, wup, wdown, postw, target=None):
    s = h1.shape[0]
    fb = D_FF // N_DEV
    last = target is not None

    def body(h_ref, prew_ref, up_ref, down_ref, postw_ref, *rest):
        target_ref, (mb_ref, ab_ref, d_ref, out_ref) = (rest[0] if last else None), rest[last:last + 4]
        hh = h_ref[...]
        mb = _rms_fwd(hh, prew_ref[...]).astype(BF16)
        mb_ref[...] = mb
        d = jnp.zeros((TM, D_MODEL), F32)
        for j in range(N_DEV):
            a = jnp.maximum(_dot(mb, up_ref[j]), 0.0)
            ab_ref[j] = a.astype(BF16)
            d = d + _dot(jnp.square(a).astype(BF16), down_ref[j])
        d_ref[...] = d
        h2 = hh + _rms_fwd(d, postw_ref[...])
        if last:
            diff = h2 - target_ref[...]
            out_ref[...] = diff * (1.0 / D_MODEL)
            part = 0.5 * jnp.sum(jnp.mean(diff * diff, axis=-1, keepdims=True), axis=0, keepdims=True)
            _acc_rows(rest[-1], part, pl.program_id(0) == 0)
        else:
            out_ref[...] = h2

    stacked = pl.BlockSpec((N_DEV, TM, fb), lambda i: (0, i, 0))
    return pl.pallas_call(
        body, name="mlp_fwd_loss" if last else "mlp_fwd", grid=(s // TM,),
        in_specs=[_rows(TM, D_MODEL), _full((1, D_MODEL)), _resident((N_DEV, D_MODEL, fb)), _resident((N_DEV, fb, D_MODEL)),
                  _full((1, D_MODEL))] + ([_rows(TM, D_MODEL)] if last else []),
        out_specs=[_rows(TM, D_MODEL), stacked, _rows(TM, D_MODEL), _rows(TM, D_MODEL)] + ([_full((1, 1))] if last else []),
        out_shape=[jax.ShapeDtypeStruct((s, D_MODEL), BF16), jax.ShapeDtypeStruct((N_DEV, s, fb), BF16),
                   jax.ShapeDtypeStruct((s, D_MODEL), F32), jax.ShapeDtypeStruct((s, D_MODEL), F32)]
        + ([jax.ShapeDtypeStruct((1, 1), F32)] if last else []),
        compiler_params=_params(),
    )(h1, prew, wup, wdown, postw, *([target] if last else []))


def _mlp_bwd(dh2, d, h1, ab, prew, wup, wdown, postw):
    s = dh2.shape[0]
    fb = D_FF // N_DEV
    tm = TM // 2

    def body(dh2_ref, d_ref, h1_ref, ab_ref, prew_ref, up_ref, down_ref, postw_ref,
             dh1_ref, da_ref, dd_ref, gpost_ref, gpre_ref):
        first = pl.program_id(0) == 0
        dh2 = dh2_ref[...]
        dd, gpost = _rms_bwd(d_ref[...], postw_ref[...], dh2)
        _acc_rows(gpost_ref, gpost, first)
        ddb = dd.astype(BF16)
        dd_ref[...] = ddb

        def d_relu_squared(j):
            return _dot_nt(ddb, down_ref[j])

        def pointwise(j, dr):
            da = (dr * (2.0 * ab_ref[j].astype(F32))).astype(BF16)
            da_ref[j] = da
            return da

        dm = jnp.zeros((tm, D_MODEL), F32)
        nxt, da_prev = d_relu_squared(0), None
        for j in range(N_DEV):
            cur = nxt
            if j + 1 < N_DEV:
                nxt = d_relu_squared(j + 1)
            da = pointwise(j, cur)
            if da_prev is not None:
                dm = dm + _dot_nt(da_prev, up_ref[j - 1])
            da_prev = da
        dm = dm + _dot_nt(da_prev, up_ref[N_DEV - 1])
        dx, gpre = _rms_bwd(h1_ref[...], prew_ref[...], dm)
        _acc_rows(gpre_ref, gpre, first)
        dh1_ref[...] = dh2 + dx

    stacked = pl.BlockSpec((N_DEV, tm, fb), lambda i: (0, i, 0))
    return pl.pallas_call(
        body, name="mlp_bwd", grid=(s // tm,),
        in_specs=[_rows(tm, D_MODEL)] * 3 + [stacked, _full((1, D_MODEL)), _resident((N_DEV, D_MODEL, fb)),
                                              _resident((N_DEV, fb, D_MODEL)), _full((1, D_MODEL))],
        out_specs=[_rows(tm, D_MODEL), stacked, _rows(tm, D_MODEL), _full((1, D_MODEL)), _full((1, D_MODEL))],
        out_shape=[jax.ShapeDtypeStruct((s, D_MODEL), F32), jax.ShapeDtypeStruct((N_DEV, s, fb), BF16),
                   jax.ShapeDtypeStruct((s, D_MODEL), BF16), jax.ShapeDtypeStruct((1, D_MODEL), F32),
                   jax.ShapeDtypeStruct((1, D_MODEL), F32)],
        compiler_params=_params(),
    )(dh2, d, h1, ab, prew, wup, wdown, postw)


def _matmul_tn(a, b, name, tk=TK_DW):
    s, m = a.shape
    n = b.shape[1]
    tn = n if n <= 1024 else (n // 2 if (n // 2) % 128 == 0 else n // 3)
    tk = min(tk, s)
    assert n % tn == 0 and tn % 128 == 0 and s % tk == 0

    def body(a_ref, b_ref, o_ref):
        part = _dot_tn(a_ref[...], b_ref[...])

        @pl.when(pl.program_id(1) == 0)
        def _():
            o_ref[...] = part

        @pl.when(pl.program_id(1) != 0)
        def _():
            o_ref[...] += part

    return pl.pallas_call(
        body, name=name, grid=(n // tn, s // tk),
        in_specs=[pl.BlockSpec((tk, m), lambda j, k: (k, 0)), pl.BlockSpec((tk, tn), lambda j, k: (k, j))],
        out_specs=pl.BlockSpec((m, tn), lambda j, k: (0, j)),
        out_shape=jax.ShapeDtypeStruct((m, n), F32),
        compiler_params=_params(),
    )(a, b)


def _matmul_tn_stacked(a, b, name, a_stacked, square_a=False, tk=TK_DW):
    tk = min(tk, a.shape[-2])
    if a_stacked:
        _, s, m = a.shape
        n = b.shape[1]
        in_specs = [pl.BlockSpec((1, tk, m), lambda j, k: (j, k, 0)), pl.BlockSpec((tk, n), lambda j, k: (k, 0))]
    else:
        s, m = a.shape
        n = b.shape[2]
        in_specs = [pl.BlockSpec((tk, m), lambda j, k: (k, 0)), pl.BlockSpec((1, tk, n), lambda j, k: (j, k, 0))]

    nk = s // tk

    def body(a_ref, b_ref, o_ref, acc_s):
        av = a_ref[0] if a_stacked else a_ref[...]
        bv = b_ref[...] if a_stacked else b_ref[0]
        if square_a:
            av = jnp.square(av.astype(F32)).astype(BF16)
        part = _dot_tn(av, bv)
        k = pl.program_id(1)

        @pl.when(k == 0)
        def _():
            acc_s[...] = part

        @pl.when(jnp.logical_and(k != 0, k != nk - 1))
        def _():
            acc_s[...] += part

        @pl.when(k == nk - 1)
        def _():
            o_ref[0] = (part if nk == 1 else acc_s[...] + part).astype(BF16)

    return pl.pallas_call(
        body, name=name, grid=(N_DEV, nk),
        in_specs=in_specs,
        out_specs=pl.BlockSpec((1, m, n), lambda j, k: (j, 0, 0)),
        out_shape=jax.ShapeDtypeStruct((N_DEV, m, n), BF16),
        scratch_shapes=[pltpu.VMEM((m, n), F32)],
        compiler_params=_params(),
    )(a, b)


def _outproj_bwd(dh1, mixed, nw, wout, oe):
    s = dh1.shape[0]
    wide = MLA_HEADS * HEAD_PAD

    def body(dh1_ref, mixed_ref, nw_ref, w_ref, oe_ref, dmix_ref, doe_ref, dy_ref, gnw_ref, delta_ref):
        dmix, gnw = _rms_bwd(mixed_ref[...], nw_ref[...], dh1_ref[...])
        _acc_rows(gnw_ref, gnw, pl.program_id(0) == 0)
        dmb = dmix.astype(BF16)
        dmix_ref[...] = dmb
        doe_ref[...] = _dot_nt(dmb, w_ref[0:wide, :]).astype(BF16)
        dy_ref[...] = _dot_nt(dmb, w_ref[wide:, :])
        ones = jnp.ones((8, HEAD_PAD), BF16)
        for hd in range(MLA_HEADS):
            cols = slice(hd * HEAD_PAD, (hd + 1) * HEAD_PAD)
            prod = oe_ref[:, cols].astype(F32) * doe_ref[:, cols].astype(F32)
            delta_ref[hd] = _dot01(prod, ones, dot=_dot_nt, left=True)

    return pl.pallas_call(
        body, name="outproj_bwd", grid=(s // TM,),
        in_specs=[_rows(TM, D_MODEL), _rows(TM, D_MODEL), _full((1, D_MODEL)), _resident((wide + SSD_INNER, D_MODEL)),
                  _rows(TM, wide)],
        out_specs=[_rows(TM, D_MODEL), _rows(TM, wide), _rows(TM, SSD_INNER), _full((1, D_MODEL)),
                   pl.BlockSpec((MLA_HEADS, 8, TM), lambda i: (0, 0, i))],
        out_shape=[jax.ShapeDtypeStruct((s, D_MODEL), BF16), jax.ShapeDtypeStruct((s, wide), BF16),
                   jax.ShapeDtypeStruct((s, SSD_INNER), F32), jax.ShapeDtypeStruct((1, D_MODEL), F32),
                   jax.ShapeDtypeStruct((MLA_HEADS, 8, s), F32)],
        compiler_params=_params(),
    )(dh1, mixed, nw, wout, oe)


def _attn_bwd(q, k, v, do, lse, delta, exchange=()):
    s = q.shape[0]
    t = ATT_T
    nq = s // t
    pair = 2 * HEAD_PAD
    ne = len(exchange)

    def body(q_ref, k_ref, v_ref, do_ref, lse_ref, delta_ref, *rest):
        e_in, (dq_ref, dk_ref, dv_ref), e_out = rest[:ne], rest[ne:ne + 3], rest[ne + 3:2 * ne + 3]
        dk_s, dv_s, bias_s = rest[2 * ne + 3:2 * ne + 6]
        kb = pl.program_id(1)
        _hosted_comm("exchange", e_in, e_out, rest[2 * ne + 6:],
                     jnp.logical_and(pl.program_id(0) == 0, kb == 0),
                     jnp.logical_and(pl.program_id(0) == MLA_HEADS // 2 - 1, kb == nq - 1))

        @pl.when(jnp.logical_and(pl.program_id(0) == 0, kb == 0))
        def _():
            bias_s[...] = _chunk_bias(t, keys_on_rows=True)

        @pl.when(kb == 0)
        def _():
            dq_ref[...] = jnp.zeros(dq_ref.shape, F32)

        def step(qb, diagonal):
            r0 = pl.multiple_of(qb * t, t)
            for hh in range(2):
                cols = slice(hh * HEAD_PAD, (hh + 1) * HEAD_PAD)
                kk = k_ref[:, cols]
                qq = q_ref[pl.ds(r0, t), cols]
                dd = do_ref[pl.ds(r0, t), cols]
                sc = _dot_nt(kk, qq) * ATT_SCALE_LOG2
                if diagonal:
                    sc = sc + bias_s[...]
                p = jnp.exp2(sc - lse_ref[hh, 0:1, pl.ds(r0, t)])
                dv = _dot(p.astype(BF16), dd)
                dp = _dot_nt(v_ref[:, cols], dd)
                ds = (p * (dp - delta_ref[hh, 0:1, pl.ds(r0, t)]) * ATT_SCALE).astype(BF16)
                dk = _dot(ds, qq)
                if diagonal:
                    dv_s[:, cols] = dv
                    dk_s[:, cols] = dk
                else:
                    dv_s[:, cols] += dv
                    dk_s[:, cols] += dk
                dq_ref[pl.ds(r0, t), cols] += _dot_tn(ds, kk)

        def loop(i, c):
            for u in range(ATT_UNROLL):
                step(kb + 1 + u + ATT_UNROLL * i, False)
            return c

        step(kb, True)
        later_tiles = nq - 1 - kb
        lax.fori_loop(0, later_tiles // ATT_UNROLL, loop, 0)
        left = later_tiles % ATT_UNROLL
        for u in range(ATT_UNROLL - 1):
            @pl.when(left > u)
            def _(u=u):
                step(nq - left + u, False)

        dk_ref[...] = dk_s[...].astype(BF16)
        dv_ref[...] = dv_s[...].astype(BF16)

    whole = pl.BlockSpec((s, pair), lambda h, i: (0, h))
    tile = pl.BlockSpec((t, pair), lambda h, i: (i, h))
    rowvec = pl.BlockSpec((2, 8, s), lambda h, i: (h, 0, 0))
    wide = MLA_HEADS * HEAD_PAD
    outs = pl.pallas_call(
        body, name="attn_bwd_exchange" if ne else "attn_bwd", grid=(MLA_HEADS // 2, nq),
        in_specs=[whole, tile, tile, whole, rowvec, rowvec] + [_ANY] * ne,
        out_specs=[whole, tile, tile] + [_ANY] * ne,
        out_shape=[jax.ShapeDtypeStruct((s, wide), F32)] + [jax.ShapeDtypeStruct((s, wide), BF16)] * 2
        + _comm_out_shapes("exchange", exchange),
        scratch_shapes=[pltpu.VMEM((t, pair), F32), pltpu.VMEM((t, pair), F32), pltpu.VMEM((t, t), F32)]
        + (_comm_scratch(ne) if ne else []),
        compiler_params=_params(),
    )(q, k, v, do, lse, delta, *exchange)
    return outs[0], outs[1], outs[2], list(outs[3:])


def _ssd_bwd(dy, ypre, z, c, xraw, misc, prev, cw, dtb, a_exp, d_exp, nw, consts):
    s = dy.shape[0]
    nb = s // SSD_ROWS
    ncb = SSD_ROWS // CHUNK
    emisc, emisc_t, tri, trit = consts
    gw = SSD_INNER // SSD_GROUPS
    hpg = SSD_HEADS // SSD_GROUPS
    lw = gw + 2 * SSD_N

    def body(dy_ref, ypre_ref, z_ref, cx_ref, cb_ref, cc_ref, xx_ref, xb_ref, xc_ref, misc_ref, prev_ref,
             wx_ref, wb_ref, wc_ref, dtb_ref, a_ref, d_ref, nw_ref, emisc_ref, emisct_ref, tri_ref, trit_ref,
             dz_ref, dxs_ref, dxb_ref, dxc_ref, dmisc_ref, gnw_ref, gd_ref, galog_ref, gdtb_ref,
             gwx_ref, gwb_ref, gwc_ref, gbx_ref, gbb_ref, gbc_ref, dst_s, dc_s, head_s):
        grp = pl.program_id(0)
        first = pl.program_id(1) == 0

        @pl.when(first)
        def _():
            dst_s[...] = jnp.zeros(dst_s.shape, F32)
            head_s[...] = jnp.zeros(head_s.shape, F32)
            gnw_ref[...] = jnp.zeros(gnw_ref.shape, F32)
            gd_ref[...] = jnp.zeros(gd_ref.shape, F32)
            galog_ref[...] = jnp.zeros(galog_ref.shape, F32)
            gdtb_ref[...] = jnp.zeros(gdtb_ref.shape, F32)

        a_exp_v = a_ref[...]
        a8 = _dot01(a_exp_v, emisct_ref[...]) * (1.0 / SSD_P)
        lane8 = lax.broadcasted_iota(jnp.int32, (CHUNK, HEAD_PAD), 1)
        dt_lo = MISC_DT + hpg * grp
        on_dt = jnp.logical_and(lane8 >= dt_lo, lane8 < dt_lo + hpg)

        def chunk(ci):
            r0 = ci * CHUNK
            rows = pl.ds(r0, CHUNK)
            cc = jnp.concatenate([cx_ref[rows, :], cb_ref[rows, :], cc_ref[rows, :]], axis=1)
            mm = misc_ref[rows, :]
            xa, sig_c, dt, acs, acs_t, alast = _ssd_chunk_common(cc, mm, emisc_ref[...], tri_ref[...], trit_ref[...],
                                                              dtb_ref[...], a_exp_v)
            yield
            xs = xa[:, :gw]
            bm = xa[:, gw:gw + SSD_N].astype(BF16)
            cm = xa[:, gw + SSD_N:].astype(BF16)
            xdt = xs * dt
            y = ypre_ref[rows, :]
            zz = z_ref[rows, :]
            sg = jax.nn.sigmoid(zz)
            yz = y * (zz * sg)
            r = lax.rsqrt(jnp.mean(yz * yz, axis=-1, keepdims=True) + EPS)
            yn = yz * r
            dyo = dy_ref[rows, :]
            gnw_ref[...] += jnp.sum(dyo * yn, axis=0, keepdims=True)
            dyn = dyo * nw_ref[...]
            dyz = r * (dyn - yn * jnp.mean(dyn * yn, axis=-1, keepdims=True))
            dz_ref[rows, :] = dyz * y * (sg * (1.0 + zz * (1.0 - sg)))
            dyp = dyz * (zz * sg)
            dypb = dyp.astype(BF16)
            gd_ref[...] += jnp.sum(dyp * xs, axis=0, keepdims=True)
            yield
            prev = prev_ref[ci]
            prev_b = prev.astype(BF16)
            cd = jnp.exp(alast)
            e = jnp.exp(acs)
            dsx = jnp.exp(alast - acs)
            wgt = (xdt * dsx).astype(BF16)
            dze = (dyp * e).astype(BF16)
            dcm = _dot_nt(dze, prev_b)
            dprev = _dot_tn(cm, dze)
            cb_g = _dot_nt(cm, bm)
            dcb = jnp.zeros((CHUNK, CHUNK), F32)
            diag8 = jnp.zeros((CHUNK, HEAD_PAD), F32)
            diag_parts = []
            for jj in range(hpg // 2):
                pl_ = slice(jj * 128, (jj + 1) * 128)
                xp = xdt[:, pl_]
                dyp_p = dypb[:, pl_]
                dxp = jnp.zeros((CHUNK, 128), F32)
                for hh in range(2):
                    hd = 2 * jj + hh
                    dec = _decay(acs, acs_t, hd)
                    xm = jnp.where(_half_mask(hh), xp, 0.0).astype(BF16)
                    dsc = _dot_nt(dyp_p, xm) * dec
                    dcb = dcb + dsc
                    sc = (cb_g * dec).astype(BF16)
                    dxp = dxp + jnp.where(_half_mask(hh), _dot_tn(sc, dyp_p), 0.0)
                    dm = dsc * cb_g
                    diag8 = diag8 + jnp.where(lane8 == dt_lo + hd, jnp.sum(dm - dm.T, axis=1, keepdims=True), 0.0)
                diag_parts.append(dxp)
            dcbb = dcb.astype(BF16)
            dcm = dcm + _dot(dcbb, bm)
            dbm = _dot_tn(dcbb, cm)
            yoff = _dot(cm, prev_b) * e
            yield
            dst = dst_s[...]
            dst_b = dst.astype(BF16)
            glast = jnp.sum(dst * prev, axis=0, keepdims=True) * cd
            dxdt_state = _dot(bm, dst_b) * dsx
            dbm = dbm + _dot_nt(wgt, dst_b)
            dst_s[...] = dst * cd + dprev
            yield
            dxdt = jnp.concatenate(diag_parts, axis=1) + dxdt_state
            dacs = dyp * yoff - xdt * dxdt_state
            last = jnp.sum(xdt * dxdt_state, axis=0, keepdims=True) + glast
            row = lax.broadcasted_iota(jnp.int32, (CHUNK, gw), 0)
            dacs = dacs + jnp.where(row == CHUNK - 1, last, 0.0)
            dacs8 = _dot01(dacs, emisct_ref[...]) + diag8
            da8 = _dot01(dacs8, trit_ref[...], left=True)
            ddt8 = da8 * a8 + _dot01(dxdt * xs, emisct_ref[...])
            yield
            dtr8 = mm + _dot01(dtb_ref[...], emisct_ref[...]) * (1.0 / SSD_P)
            dt8 = jax.nn.softplus(dtr8)
            ddtr8 = jnp.where(on_dt, ddt8 * jax.nn.sigmoid(dtr8), 0.0)
            dmisc_ref[0, rows, :] = ddtr8
            gdtb_ref[0] += jnp.sum(ddtr8, axis=0, keepdims=True)
            galog_ref[0] += jnp.sum(jnp.where(on_dt, da8 * dt8, 0.0), axis=0, keepdims=True) * a8
            dxs = d_ref[...] * dyp + dxdt * dt
            dxa = jnp.concatenate([dxs, dbm, dcm], axis=1)
            dc_s[rows, :] = dxa * (sig_c * (1.0 + cc * (1.0 - sig_c)))

        _interleave([chunk(ci) for ci in reversed(range(ncb))])

        dc = dc_s[...]
        x = jnp.concatenate([xx_ref[...], xb_ref[...], xc_ref[...]], axis=1)
        cwl = jnp.concatenate([wx_ref[...], wb_ref[...], wc_ref[...]], axis=1)
        dcext = jnp.concatenate([dc, head_s[...]], axis=0)
        dx = dc * cwl[CONV_W - 1:CONV_W, :]
        rows_w = [jnp.sum(dc * x, axis=0, keepdims=True)]
        for j in range(1, CONV_W):
            ahead = pltpu.roll(dcext, SSD_ROWS + 8 - j, 0)[:SSD_ROWS, :]
            dx = dx + ahead * cwl[CONV_W - 1 - j:CONV_W - j, :]
            rows_w.insert(0, jnp.sum(ahead * x, axis=0, keepdims=True))
        dxs_ref[...] = dx[:, :gw]
        dxb_ref[...] = dx[:, gw:gw + SSD_N]
        dxc_ref[...] = dx[:, gw + SSD_N:]
        head_s[...] = dc[:8, :]
        gcw = jnp.concatenate(rows_w, axis=0)
        gcb = jnp.sum(dc, axis=0, keepdims=True)
        parts = ((gwx_ref, gbx_ref, slice(0, gw)), (gwb_ref, gbb_ref, slice(gw, gw + SSD_N)), (gwc_ref, gbc_ref, slice(gw + SSD_N, lw)))

        @pl.when(first)
        def _():
            for w_ref, b_ref, sl in parts:
                w_ref[...] = gcw[:, sl]
                b_ref[...] = gcb[:, sl]

        @pl.when(jnp.logical_not(first))
        def _():
            for w_ref, b_ref, sl in parts:
                w_ref[...] += gcw[:, sl]
                b_ref[...] += gcb[:, sl]

    nbc = SSD_GROUPS * SSD_N // SSD_N
    b0, c0 = SSD_INNER // SSD_N, SSD_INNER // SSD_N + nbc

    def rows_cols(width, col):
        return pl.BlockSpec((SSD_ROWS, width), lambda g, i: (nb - 1 - i, col(g)))

    def cols_only(nrows, width, col):
        return pl.BlockSpec((nrows, width), lambda g, i: (0, col(g)))

    grp_x, grp_b, grp_c = (lambda g: g), (lambda g: b0 + g), (lambda g: c0 + g)
    per_group = pl.BlockSpec((1, 1, HEAD_PAD), lambda g, i: (g, 0, 0))
    outs = pl.pallas_call(
        body, name="ssd_bwd", grid=(SSD_GROUPS, nb),
        in_specs=[rows_cols(gw, grp_x), rows_cols(gw, grp_x), rows_cols(gw, grp_x),
                  rows_cols(gw, grp_x), rows_cols(SSD_N, grp_b), rows_cols(SSD_N, grp_c),
                  rows_cols(gw, grp_x), rows_cols(SSD_N, grp_b), rows_cols(SSD_N, grp_c),
                  pl.BlockSpec((SSD_ROWS, HEAD_PAD), lambda g, i: (nb - 1 - i, 0)),
                  pl.BlockSpec((ncb, SSD_N, gw), lambda g, i: (nb - 1 - i, 0, g)),
                  cols_only(CONV_W, gw, grp_x), cols_only(CONV_W, SSD_N, grp_b), cols_only(CONV_W, SSD_N, grp_c),
                  cols_only(1, gw, grp_x), cols_only(1, gw, grp_x), cols_only(1, gw, grp_x), cols_only(1, gw, grp_x),
                  cols_only(HEAD_PAD, gw, grp_x), pl.BlockSpec((gw, HEAD_PAD), lambda g, i: (g, 0)),
                  _full((CHUNK, CHUNK)), _full((CHUNK, CHUNK))],
        out_specs=[rows_cols(gw, grp_x), rows_cols(gw, grp_x), rows_cols(SSD_N, lambda g: g), rows_cols(SSD_N, lambda g: g),
                   pl.BlockSpec((1, SSD_ROWS, HEAD_PAD), lambda g, i: (g, nb - 1 - i, 0)),
                   cols_only(1, gw, grp_x), cols_only(1, gw, grp_x), per_group, per_group,
                   cols_only(CONV_W, gw, grp_x), cols_only(CONV_W, SSD_N, lambda g: g), cols_only(CONV_W, SSD_N, lambda g: g),
                   cols_only(1, gw, grp_x), cols_only(1, SSD_N, lambda g: g), cols_only(1, SSD_N, lambda g: g)],
        out_shape=[jax.ShapeDtypeStruct((s, SSD_INNER), F32), jax.ShapeDtypeStruct((s, SSD_INNER), F32),
                   jax.ShapeDtypeStruct((s, nbc * SSD_N), F32), jax.ShapeDtypeStruct((s, nbc * SSD_N), F32),
                   jax.ShapeDtypeStruct((SSD_GROUPS, s, HEAD_PAD), F32),
                   jax.ShapeDtypeStruct((1, SSD_INNER), F32), jax.ShapeDtypeStruct((1, SSD_INNER), F32),
                   jax.ShapeDtypeStruct((SSD_GROUPS, 1, HEAD_PAD), F32), jax.ShapeDtypeStruct((SSD_GROUPS, 1, HEAD_PAD), F32),
                   jax.ShapeDtypeStruct((CONV_W, SSD_INNER), F32), jax.ShapeDtypeStruct((CONV_W, nbc * SSD_N), F32),
                   jax.ShapeDtypeStruct((CONV_W, nbc * SSD_N), F32),
                   jax.ShapeDtypeStruct((1, SSD_INNER), F32), jax.ShapeDtypeStruct((1, nbc * SSD_N), F32),
                   jax.ShapeDtypeStruct((1, nbc * SSD_N), F32)],
        scratch_shapes=[pltpu.VMEM((SSD_N, gw), F32), pltpu.VMEM((SSD_ROWS, lw), F32), pltpu.VMEM((8, lw), F32)],
        compiler_params=_params(),
    )(dy, ypre, z, c, c, c, xraw, xraw, xraw, misc, prev, cw, cw, cw, dtb, a_exp, d_exp, nw, emisc, emisc_t, tri, trit)
    dz, dxs, dxb, dxc, dmisc, gnw, gd, galog, gdtb, gwx, gwb, gwc, gbx, gbb, gbc = outs
    return (dz, (dxs, dxb, dxc), dmisc[0] + dmisc[1], gnw, gd, galog[0] + galog[1], gdtb[0] + gdtb[1],
            jnp.concatenate([gwx, gwb, gwc], axis=1), jnp.concatenate([gbx, gbb, gbc], axis=1))


def _qkv_inproj_bwd(dq, dk, dv, cq, ckv, dmisc_dt, dz, dxbc, h, dh1, qnw, kvnw, nw, wuq, wkv, win, cosf, sinf):
    s = dq.shape[0]
    wide = MLA_HEADS * HEAD_PAD
    tm = TM

    def body(dq_ref, dk_ref, dv_ref, cq_ref, ckv_ref, dmdt_ref, dz_ref, dxs_ref, dxb_ref, dxc_ref, h_ref, dh1_ref, qnw_ref, kvnw_ref, nw_ref,
             wuq_ref, wkv_ref, win_ref, cos_ref, sin_ref, dqb_ref, dkvb_ref, dproj_ref, dh0_ref, gq_ref, gkv_ref, gnw_ref):
        first = pl.program_id(0) == 0
        cosf, sinf = cos_ref[...], sin_ref[...]
        dkr = jnp.zeros((tm, HEAD_PAD), F32)
        for hd in range(MLA_HEADS):
            cols = slice(hd * HEAD_PAD, (hd + 1) * HEAD_PAD)
            dqb_ref[:, cols] = _rope(dq_ref[:, cols], cosf, sinf, -1.0).astype(BF16)
            dkh = dk_ref[:, cols]
            dkvb_ref[:, cols] = dkh
            dkr = dkr + dkh
        dkvb_ref[:, wide:] = dv_ref[...]
        lane = lax.broadcasted_iota(jnp.int32, dkr.shape, 1)
        in_rope = jnp.logical_and(lane >= MISC_ROPE, lane < MISC_ROPE + QK_ROPE)
        dmisc_rope = jnp.where(in_rope, _rope(jnp.where(in_rope, dkr, 0.0), cosf, sinf, -1.0), 0.0)
        dcq, gq = _rms_bwd(cq_ref[...], qnw_ref[...], _dot_nt(dqb_ref[...], wuq_ref[...]))
        _acc_rows(gq_ref, gq, first)
        dckv, gkv = _rms_bwd(ckv_ref[...], kvnw_ref[...], _dot_nt(dkvb_ref[...], wkv_ref[...]))
        _acc_rows(gkv_ref, gkv, first)
        dproj_ref[:, 0:768] = dcq.astype(BF16)
        dproj_ref[:, 768:1024] = dckv.astype(BF16)
        dproj_ref[:, 1024:1152] = (dmisc_rope + dmdt_ref[...]).astype(BF16)
        dproj_ref[:, 1152:1664] = dz_ref[...].astype(BF16)
        dproj_ref[:, 1664:2176] = dxs_ref[...].astype(BF16)
        dproj_ref[:, 2176:2432] = dxb_ref[...].astype(BF16)
        dproj_ref[:, 2432:2688] = dxc_ref[...].astype(BF16)
        dx, gnw = _rms_bwd(h_ref[...], nw_ref[...], _dot_nt(dproj_ref[...], win_ref[...]))
        _acc_rows(gnw_ref, gnw, first)
        dh0_ref[...] = dh1_ref[...] + dx

    return pl.pallas_call(
        body, name="qkv_inproj_bwd", grid=(s // tm,),
        in_specs=[_rows(tm, wide)] * 3 + [_rows(tm, Q_RANK), _rows(tm, KV_RANK), _rows(tm, HEAD_PAD), _rows(tm, SSD_INNER),
                                          _rows(tm, SSD_INNER), _rows(tm, 2 * SSD_N), _rows(tm, 2 * SSD_N), _rows(tm, D_MODEL),
                                          _rows(tm, D_MODEL), _full((1, Q_RANK)), _full((1, KV_RANK)), _full((1, D_MODEL)),
                                          _resident((Q_RANK, wide)), _resident((KV_RANK, 2 * wide)), _resident((D_MODEL, IN_PAD)),
                                          _rows(tm, HEAD_PAD), _rows(tm, HEAD_PAD)],
        out_specs=[_rows(tm, wide), _rows(tm, 2 * wide), _rows(tm, IN_PAD), _rows(tm, D_MODEL),
                   _full((1, Q_RANK)), _full((1, KV_RANK)), _full((1, D_MODEL))],
        out_shape=[jax.ShapeDtypeStruct((s, wide), BF16), jax.ShapeDtypeStruct((s, 2 * wide), BF16),
                   jax.ShapeDtypeStruct((s, IN_PAD), BF16), jax.ShapeDtypeStruct((s, D_MODEL), F32),
                   jax.ShapeDtypeStruct((1, Q_RANK), F32), jax.ShapeDtypeStruct((1, KV_RANK), F32),
                   jax.ShapeDtypeStruct((1, D_MODEL), F32)],
        compiler_params=_params(),
    )(dq, dk, dv, cq, ckv, dmisc_dt, dz, *dxbc, h, dh1, qnw, kvnw, nw, wuq, wkv, win, cosf, sinf)


def _row_tile(rows, cols):
    cap = max(8, (1 << 18) // max(cols, 128))
    best = None
    for t in range(8, rows + 1, 8):
        if rows % t == 0 and t <= cap:
            best = t
    return best if best is not None else rows


def _adamw(w, g, m, v, name):
    rows, cols = w.shape
    tr = _row_tile(rows, cols)

    def body(w_ref, g_ref, m_ref, v_ref, d_ref, m2_ref, v2_ref):
        gg = g_ref[...]
        m2 = ADAM_B1 * m_ref[...] + (1.0 - ADAM_B1) * gg
        v2 = ADAM_B2 * v_ref[...] + (1.0 - ADAM_B2) * jnp.square(gg)
        m_hat = m2 / (1.0 - ADAM_B1 ** ADAM_STEP)
        v_hat = v2 / (1.0 - ADAM_B2 ** ADAM_STEP)
        d_ref[...] = -ADAM_LR * (m_hat / (jnp.sqrt(v_hat) + ADAM_EPS) + ADAM_WD * w_ref[...])
        m2_ref[...] = m2
        v2_ref[...] = v2

    spec = pl.BlockSpec((tr, cols), lambda i: (i, 0))
    return pl.pallas_call(
        body, name=name, grid=(rows // tr,),
        in_specs=[spec] * 4, out_specs=[spec] * 3,
        out_shape=[jax.ShapeDtypeStruct((rows, cols), F32)] * 3,
    )(w, g, m, v)


def _sum_adamw(slots, w, m, v, name):
    _, rows, cols = w.shape
    tr = _row_tile(rows, cols)
    nb = rows // tr

    def body(s0_ref, s1_ref, w_ref, m_ref, v_ref, g_ref, d_ref, m2_ref, v2_ref):
        for l, ref in enumerate((s0_ref, s1_ref)):
            @pl.when(pl.program_id(0) == l)
            def _(ref=ref):
                acc = ref[0].astype(F32)
                for i in range(1, N_DEV):
                    acc = acc + ref[i].astype(F32)
                g_ref[...] = acc

        gg = g_ref[...]
        m2 = ADAM_B1 * m_ref[...] + (1.0 - ADAM_B1) * gg
        v2 = ADAM_B2 * v_ref[...] + (1.0 - ADAM_B2) * jnp.square(gg)
        m_hat = m2 / (1.0 - ADAM_B1 ** ADAM_STEP)
        v_hat = v2 / (1.0 - ADAM_B2 ** ADAM_STEP)
        d_ref[...] = -ADAM_LR * (m_hat / (jnp.sqrt(v_hat) + ADAM_EPS) + ADAM_WD * w_ref[...])
        m2_ref[...] = m2
        v2_ref[...] = v2

    slot_spec = lambda layer: pl.BlockSpec((N_DEV, tr, cols), lambda l, i: (0, jnp.where(l == layer, i, (nb - 1) * (1 - layer)), 0))
    spec = pl.BlockSpec((None, tr, cols), lambda l, i: (l, i, 0))
    return pl.pallas_call(
        body, name=name, grid=(DEPTH, nb),
        in_specs=[slot_spec(0), slot_spec(1), spec, spec, spec], out_specs=[spec] * 4,
        out_shape=[jax.ShapeDtypeStruct(w.shape, F32)] * 4,
        compiler_params=_params(),
    )(slots[0], slots[1], w, m, v)


_MESH = pl.DeviceIdType.MESH
_ANY = pl.BlockSpec(memory_space=pl.ANY)


def _my_place():
    return lax.axis_index("x"), lax.axis_index("y"), lax.axis_index("c")


def _flip(place, k):
    x, y, c = place
    return (1 - x if k & 4 else x, 1 - y if k & 2 else y, 1 - c if k & 1 else c)


def _block_id(place):
    return 4 * place[0] + 2 * place[1] + place[2]


def _peer_copies(kind, in_refs, out_refs, send_sems, recv_sems, local_sems):
    me = _my_place()
    my = _block_id(me)
    remote, local = [], []
    for a, (x_ref, out_ref) in enumerate(zip(in_refs, out_refs)):
        src_of = (lambda place, r=x_ref: r) if kind == "gather" else (lambda place, r=x_ref: r.at[_block_id(place)])
        local.append(pltpu.make_async_copy(src_of(me), out_ref.at[my], local_sems.at[a]))
        for k in range(1, N_DEV):
            peer = _flip(me, k)
            remote.append(pltpu.make_async_remote_copy(
                src_ref=src_of(peer), dst_ref=out_ref.at[my], send_sem=send_sems.at[a * 7 + k - 1],
                recv_sem=recv_sems.at[a * 7 + k - 1], device_id=peer, device_id_type=_MESH))
    return remote, local


def _comm_out_shapes(kind, arrays):
    return [jax.ShapeDtypeStruct((N_DEV, *a.shape) if kind == "gather" else a.shape, a.dtype) for a in arrays]


def _comm_scratch(n):
    return [pltpu.SemaphoreType.DMA((7 * n,)), pltpu.SemaphoreType.DMA((7 * n,)), pltpu.SemaphoreType.DMA((n,))]


def _hosted_comm(kind, in_refs, out_refs, sems, first, last):
    if not in_refs:
        return

    @pl.when(first)
    def _():
        remote, local = _peer_copies(kind, in_refs, out_refs, *sems)
        for cp in local + remote:
            cp.start()

    @pl.when(last)
    def _():
        remote, local = _peer_copies(kind, in_refs, out_refs, *sems)
        for cp in remote:
            cp.wait()
        for cp in local:
            cp.wait()


def _two_level_gather_steps(in_refs, out_refs, send_sems, recv_sems, local_sems):
    n = len(in_refs)
    me = _my_place()
    x, y, c = me
    sibling = (x, y, 1 - c)
    chips = [(1 - x, y), (x, 1 - y), (1 - x, 1 - y)]

    def copy(a, k, place, to, src=None):
        block = out_refs[a].at[_block_id(place)]
        return pltpu.make_async_remote_copy(
            src_ref=block if src is None else src, dst_ref=block, send_sem=send_sems.at[7 * a + k],
            recv_sem=recv_sems.at[7 * a + k], device_id=to, device_id_type=_MESH)

    mine = [pltpu.make_async_copy(in_refs[a], out_refs[a].at[_block_id(me)], local_sems.at[a]) for a in range(n)]
    first = [copy(a, 0, me, sibling, src=in_refs[a]) for a in range(n)]
    first += [copy(a, 1 + j, me, (*chip, c), src=in_refs[a]) for a in range(n) for j, chip in enumerate(chips)]
    passed = [copy(a, 4 + j, (*chip, c), sibling) for a in range(n) for j, chip in enumerate(chips)]

    def send():
        for cp in mine + first:
            cp.start()

    def forward():
        for a in range(n):
            for j, chip in enumerate(chips):
                copy(a, 1 + j, (*chip, c), me).wait_recv()
                passed[3 * a + j].start()

    def finish():
        for a in range(n):
            copy(a, 0, sibling, me).wait_recv()
            for j, chip in enumerate(chips):
                copy(a, 4 + j, (*chip, 1 - c), me).wait_recv()
        for cp in first + passed:
            cp.wait_send()
        for cp in mine:
            cp.wait()

    return send, forward, finish


def _gather_two_level(arrays, name):
    n = len(arrays)

    def body(*refs):
        for step in _two_level_gather_steps(refs[:n], refs[n:2 * n], *refs[2 * n:]):
            step()

    return pl.pallas_call(
        body, name=name, out_shape=_comm_out_shapes("gather", arrays),
        in_specs=[_ANY] * n, out_specs=[_ANY] * n, scratch_shapes=_comm_scratch(n),
    )(*arrays)


def _hosted_gather(in_refs, out_refs, sems, first, middle, last):
    if not in_refs:
        return
    for when, index in ((first, 0), (middle, 1), (last, 2)):
        @pl.when(when)
        def _(index=index):
            _two_level_gather_steps(in_refs, out_refs, *sems)[index]()


def _comm(kind, arrays, name):
    n = len(arrays)

    def body(*refs):
        remote, local = _peer_copies(kind, refs[:n], refs[n:2 * n], *refs[2 * n:])
        for cp in local + remote:
            cp.start()
        for cp in remote:
            cp.wait()
        for cp in local:
            cp.wait()

    return pl.pallas_call(
        body, name=name, out_shape=_comm_out_shapes(kind, arrays),
        in_specs=[_ANY] * n, out_specs=[_ANY] * n, scratch_shapes=_comm_scratch(n),
    )(*arrays)


def _all_reduce_small(part):
    rows, lanes = part.shape
    vmem = pl.BlockSpec(memory_space=pltpu.VMEM)

    def body(x_ref, gath_ref, sum_ref, send_sems, recv_sems):
        me = _my_place()
        my = _block_id(me)
        gath_ref[my] = x_ref[...]
        copies = []
        for k in range(1, N_DEV):
            cp = pltpu.make_async_remote_copy(
                src_ref=x_ref, dst_ref=gath_ref.at[my], send_sem=send_sems.at[k - 1], recv_sem=recv_sems.at[k - 1],
                device_id=_flip(me, k), device_id_type=_MESH)
            cp.start()
            copies.append(cp)
        for cp in copies:
            cp.wait()
        acc = gath_ref[0]
        for i in range(1, N_DEV):
            acc = acc + gath_ref[i]
        sum_ref[...] = acc

    return pl.pallas_call(
        body, name="small_grad_all_reduce",
        out_shape=[jax.ShapeDtypeStruct((N_DEV, rows, lanes), F32), jax.ShapeDtypeStruct((rows, lanes), F32)],
        in_specs=[vmem], out_specs=[vmem, vmem],
        scratch_shapes=[pltpu.SemaphoreType.DMA((7,)), pltpu.SemaphoreType.DMA((7,))],
    )(part)[1]


_SHARDED = (("w_in", (D_MODEL, IN_PROJ // N_DEV)), ("w_uq", (Q_RANK // N_DEV, Q_RANK)), ("w_ukv", (KV_RANK, HEAD_PAD)),
            ("conv_w", (CONV_W, CONV_DIM // N_DEV)), ("w_out", (D_MODEL // N_DEV, D_MODEL)),
            ("w_up", (D_MODEL, D_FF // N_DEV)), ("w_down", (D_FF // N_DEV, D_MODEL)))
_SMALL = (("pre_mix_norm", D_MODEL), ("q_norm", Q_RANK), ("kv_norm", KV_RANK), ("conv_b", CONV_DIM), ("dt_bias", SSD_HEADS),
          ("a_log", SSD_HEADS), ("d_skip", SSD_HEADS), ("ssd_norm", SSD_INNER), ("post_mix_norm", D_MODEL),
          ("pre_mlp_norm", D_MODEL), ("post_mlp_norm", D_MODEL))
_WEIGHT_ORDER = ("pre_mix_norm", "w_in", "q_norm", "w_uq", "kv_norm", "w_ukv", "conv_w", "conv_b", "dt_bias", "a_log", "d_skip",
                 "ssd_norm", "w_out", "post_mix_norm", "pre_mlp_norm", "w_up", "w_down", "post_mlp_norm")
_EARLY = ("w_in", "w_uq", "w_ukv", "conv_w")
_LATE = ("w_out", "w_up", "w_down")


def _wire_shard(name, a):
    return lax.bitcast_convert_type(a, BF16).reshape(CONV_W, -1) if name == "conv_w" else a.astype(BF16)


def _from_wire(name, g):
    return lax.bitcast_convert_type(g.reshape(N_DEV, CONV_W, -1, 2), F32) if name == "conv_w" else g


def _cols(stacked):
    return jnp.transpose(stacked, (1, 0, 2)).reshape(stacked.shape[1], -1)


def _win_segments():
    s2, s3, s5 = Q_RANK + KV_RANK, Q_RANK + KV_RANK + QK_ROPE, IN_PROJ - SSD_HEADS
    return [(0, s2), (None, MISC_ROPE), (s2, s3), (s5, IN_PROJ), (None, HEAD_PAD - MISC_DT - SSD_HEADS), (s3, s5)]


def _win_from_shards(stacked):
    per = IN_PROJ // N_DEV
    parts = []
    for start, stop in _win_segments():
        if start is None:
            parts.append(jnp.zeros((D_MODEL, stop), stacked.dtype))
            continue
        while start < stop:
            j, a = divmod(start, per)
            b = min(per, a + stop - start)
            parts.append(stacked[j, :, a:b])
            start += b - a
    return jnp.concatenate(parts, axis=1)


def _win_grad_shards(dwin):
    per = IN_PROJ // N_DEV
    runs, at = [], 0
    for start, stop in _win_segments():
        if start is not None:
            runs.append((start, stop, at))
        at += stop if start is None else stop - start
    blocks = []
    for j in range(N_DEV):
        lo, hi = j * per, (j + 1) * per
        parts = [dwin[:, p + max(lo, a) - a:p + min(hi, b) - a] for a, b, p in sorted(runs) if max(lo, a) < min(hi, b)]
        blocks.append(jnp.concatenate(parts, axis=1))
    return jnp.stack(blocks)


def _early_weights(sh):
    win = _win_from_shards(sh["w_in"])
    w_uq = sh["w_uq"].reshape(Q_RANK, MLA_HEADS, QK_NOPE + QK_ROPE)
    wuq = jnp.pad(w_uq, ((0, 0), (0, 0), (0, HEAD_PAD - QK_NOPE - QK_ROPE))).reshape(Q_RANK, -1)
    w_ukv = _cols(sh["w_ukv"]).reshape(KV_RANK, MLA_HEADS, QK_NOPE + V_DIM)
    wkn = jnp.pad(w_ukv[..., :QK_NOPE], ((0, 0), (0, 0), (0, HEAD_PAD - QK_NOPE))).reshape(KV_RANK, -1)
    wv = w_ukv[..., QK_NOPE:].reshape(KV_RANK, 4, 2, 1, V_DIM) * jnp.eye(2, dtype=BF16).reshape(1, 1, 2, 2, 1)
    wkv = jnp.concatenate([wkn, wv.reshape(KV_RANK, -1)], axis=1)
    return dict(win=win, wuq=wuq, wkv=wkv, conv_w=_cols(sh["conv_w"]))


def _late_weights(sh):
    w_out = sh["w_out"].reshape(D_MODEL, D_MODEL)
    watt = w_out[:SSD_INNER].reshape(4, 2, 1, V_DIM, D_MODEL) * jnp.eye(2, dtype=BF16).reshape(1, 2, 2, 1, 1)
    wout = jnp.concatenate([watt.reshape(MLA_HEADS * HEAD_PAD, D_MODEL), w_out[SSD_INNER:]], axis=0)
    return dict(wout=wout, wup=sh["w_up"], wdown=sh["w_down"])


def _shard_grads(g):
    out = {}
    if "wup" in g:
        out["w_up"], out["w_down"] = g["wup"], g["wdown"]
        ae = g["wout_att"].reshape(4, 2, 2, V_DIM, D_MODEL)
        att = jnp.stack([ae[:, 0, 0], ae[:, 1, 1]], axis=1).reshape(SSD_INNER, D_MODEL)
        out["w_out"] = jnp.concatenate([att, g["wout_ssd"]], axis=0).astype(BF16).reshape(N_DEV, D_MODEL // N_DEV, D_MODEL)
    if "win" not in g:
        return out
    out["w_in"] = _win_grad_shards(g["win"].astype(BF16))
    w_uq = g["wuq"].astype(BF16).reshape(Q_RANK, MLA_HEADS, HEAD_PAD)[..., :QK_NOPE + QK_ROPE].reshape(Q_RANK, Q_RANK)
    out["w_uq"] = w_uq.reshape(N_DEV, Q_RANK // N_DEV, Q_RANK)
    wide = MLA_HEADS * HEAD_PAD
    wkv = g["wkv"].astype(BF16)
    kn = wkv[:, :wide].reshape(KV_RANK, MLA_HEADS, HEAD_PAD)[..., :QK_NOPE]
    ve = wkv[:, wide:].reshape(KV_RANK, 4, 2, 2, V_DIM)
    vv = jnp.stack([ve[:, :, 0, 0], ve[:, :, 1, 1]], axis=2).reshape(KV_RANK, MLA_HEADS, V_DIM)
    out["w_ukv"] = jnp.transpose(jnp.concatenate([kn, vv], axis=-1), (1, 0, 2))
    out["conv_w"] = jnp.transpose(g["conv_w"].astype(BF16).reshape(CONV_W, N_DEV, -1), (1, 0, 2))
    return out


def _small_rows(n):
    return -(-n // 1024) * 8


def _pack_small(vals):
    rows = []
    for l in range(DEPTH):
        for name, n in _SMALL:
            r = _small_rows(n)
            rows.append(jnp.pad(vals[name][l].reshape(-1), (0, r * 128 - n)).reshape(r, 128))
    return jnp.concatenate(rows, axis=0)


def _unpack_small(packed):
    out, off = {name: [] for name, _ in _SMALL}, 0
    for l in range(DEPTH):
        for name, n in _SMALL:
            r = _small_rows(n)
            out[name].append(packed[off:off + r].reshape(-1)[:n])
            off += r
    return {name: jnp.stack(v) for name, v in out.items()}


def _lane_rows(vec8):
    return jnp.repeat(vec8, SSD_P).reshape(1, SSD_INNER)


def _layer_fwd(h, kw, sm, l, cosf, sinf, consts, gather=(), after_gather=None, target=None):
    row = lambda name: sm[name][l].reshape(1, -1)
    t = {}
    t["h0"] = h
    (t["ub"], t["cq"], t["ckv"], t["misc"], t["z"], t["xraw"], t["cqn"], t["ckvn"], t["q"], t["k"], t["v"]) = _inproj_qkv_fwd(
        h, row("pre_mix_norm"), kw["win"], row("q_norm"), row("kv_norm"), kw["wuq"], kw["wkv"], cosf, sinf)
    t["oe"], t["lse"], gathered = _attn_fwd(t["q"], t["k"], t["v"], gather)
    if after_gather is not None:
        after_gather(gathered)
    t["dtb"] = _lane_rows(sm["dt_bias"][l])
    t["a_exp"] = _lane_rows(-jnp.exp(sm["a_log"][l]))
    t["d_exp"] = _lane_rows(sm["d_skip"][l])
    t["c"], t["prev"], t["ypre"], t["yssd"], t["mixed"], t["h1"] = _ssd_outproj_fwd(
        t["xraw"], t["misc"], t["z"], kw["conv_w"], row("conv_b"), t["dtb"], t["a_exp"], t["d_exp"], row("ssd_norm"), consts,
        t["oe"], h, kw["wout"], row("post_mix_norm"))
    t["mb"], t["ab"], t["d"], *out = _mlp_fwd(t["h1"], row("pre_mlp_norm"), kw["wup"], kw["wdown"], row("post_mlp_norm"), target)
    return out, t


def _layer_bwd(dh2, t, kw, sm, l, cosf, sinf, consts, exchange_of=None):
    row = lambda name: sm[name][l].reshape(1, -1)
    g, gs = {}, {}
    dh1, dab, ddb, gs["post_mlp_norm"], gs["pre_mlp_norm"] = _mlp_bwd(
        dh2, t["d"], t["h1"], t["ab"], row("pre_mlp_norm"), kw["wup"], kw["wdown"], row("post_mlp_norm"))
    g["wup"] = _matmul_tn_stacked(t["mb"], dab, f"dw_up_{l}", a_stacked=False)
    g["wdown"] = _matmul_tn_stacked(t["ab"], ddb, f"dw_down_{l}", a_stacked=True, square_a=True)
    dmixb, doe, dyssd, gs["post_mix_norm"], delta = _outproj_bwd(dh1, t["mixed"], row("post_mix_norm"), kw["wout"], t["oe"])
    g["wout_att"] = _matmul_tn(t["oe"], dmixb, f"dw_out_att_{l}")
    g["wout_ssd"] = _matmul_tn(t["yssd"], dmixb, f"dw_out_ssd_{l}")
    dz, dxraw, dmisc_dt, gs["ssd_norm"], gd, galog, gdtb, g["conv_w"], gs["conv_b"] = _ssd_bwd(
        dyssd, t["ypre"], t["z"], t["c"], t["xraw"], t["misc"], t["prev"], kw["conv_w"], t["dtb"], t["a_exp"], t["d_exp"],
        row("ssd_norm"), consts)
    gs["d_skip"] = jnp.sum(gd.reshape(SSD_HEADS, SSD_P), axis=1)
    gs["a_log"] = galog[0, MISC_DT:MISC_DT + SSD_HEADS]
    gs["dt_bias"] = gdtb[0, MISC_DT:MISC_DT + SSD_HEADS]
    dq, dk, dv, exchanged = _attn_bwd(t["q"], t["k"], t["v"], doe, t["lse"], delta,
                                      exchange_of(g) if exchange_of is not None else ())
    dqb, dkvb, dprojb, dh0, gs["q_norm"], gs["kv_norm"], gs["pre_mix_norm"] = _qkv_inproj_bwd(
        dq, dk, dv, t["cq"], t["ckv"], dmisc_dt, dz, dxraw, t["h0"], dh1, row("q_norm"), row("kv_norm"),
        row("pre_mix_norm"), kw["wuq"], kw["wkv"], kw["win"], cosf, sinf)
    g["wuq"] = _matmul_tn(t["cqn"], dqb, f"dw_uq_{l}")
    g["wkv"] = _matmul_tn(t["ckvn"], dkvb, f"dw_kv_{l}")
    g["win"] = _matmul_tn(t["ub"], dprojb, f"dw_in_{l}")
    return dh0, g, {k: v.reshape(-1) for k, v in gs.items()}, exchanged


def _local_step(x, positions, kws, sm, target, gather=(), after_gather=None, exchange_of=None):
    inv_freq = ROPE_THETA ** (-jnp.arange(0, QK_ROPE, 2, dtype=F32) / QK_ROPE)
    invf = jnp.zeros((HEAD_PAD,), F32).at[MISC_ROPE:MISC_ROPE + QK_ROPE].set(jnp.concatenate([inv_freq, inv_freq]))
    cosf, sinf = _rope_tables(positions.reshape(-1, 1), invf.reshape(1, HEAD_PAD))
    consts = _ssd_consts()
    (h,), t0 = _layer_fwd(x, kws[0], sm, 0, cosf, sinf, consts, gather, after_gather)
    (dh, loss), t1 = _layer_fwd(h, kws[1], sm, 1, cosf, sinf, consts, target=target)
    saved = [t0, t1]
    grads, small, exchanged = [None] * DEPTH, [None] * DEPTH, []
    for l in reversed(range(DEPTH)):
        hook = (lambda g0: exchange_of(g0, grads[1])) if (l == 0 and exchange_of is not None) else None
        dh, grads[l], small[l], got = _layer_bwd(dh, saved[l], kws[l], sm, l, cosf, sinf, consts, hook)
        exchanged = got or exchanged
    return loss[0, 0], dh, grads, small, exchanged


def kernel(x, positions, pre_mix_norm, w_in, q_norm, w_uq, kv_norm, w_ukv, conv_w, conv_b, dt_bias, a_log, d_skip, ssd_norm, w_out, post_mix_norm, pre_mlp_norm, w_up, w_down, post_mlp_norm, loss_target, m_pre_mix_norm, m_w_in, m_q_norm, m_w_uq, m_kv_norm, m_w_ukv, m_conv_w, m_conv_b, m_dt_bias, m_a_log, m_d_skip, m_ssd_norm, m_w_out, m_post_mix_norm, m_pre_mlp_norm, m_w_up, m_w_down, m_post_mlp_norm, v_pre_mix_norm, v_w_in, v_q_norm, v_w_uq, v_kv_norm, v_w_ukv, v_conv_w, v_conv_b, v_dt_bias, v_a_log, v_d_skip, v_ssd_norm, v_w_out, v_post_mix_norm, v_pre_mlp_norm, v_w_up, v_w_down, v_post_mlp_norm):
    w = dict(pre_mix_norm=pre_mix_norm, w_in=w_in, q_norm=q_norm, w_uq=w_uq, kv_norm=kv_norm, w_ukv=w_ukv, conv_w=conv_w,
             conv_b=conv_b, dt_bias=dt_bias, a_log=a_log, d_skip=d_skip, ssd_norm=ssd_norm, w_out=w_out,
             post_mix_norm=post_mix_norm, pre_mlp_norm=pre_mlp_norm, w_up=w_up, w_down=w_down, post_mlp_norm=post_mlp_norm)
    m = dict(pre_mix_norm=m_pre_mix_norm, w_in=m_w_in, q_norm=m_q_norm, w_uq=m_w_uq, kv_norm=m_kv_norm, w_ukv=m_w_ukv,
             conv_w=m_conv_w, conv_b=m_conv_b, dt_bias=m_dt_bias, a_log=m_a_log, d_skip=m_d_skip, ssd_norm=m_ssd_norm,
             w_out=m_w_out, post_mix_norm=m_post_mix_norm, pre_mlp_norm=m_pre_mlp_norm, w_up=m_w_up, w_down=m_w_down,
             post_mlp_norm=m_post_mlp_norm)
    v = dict(pre_mix_norm=v_pre_mix_norm, w_in=v_w_in, q_norm=v_q_norm, w_uq=v_w_uq, kv_norm=v_kv_norm, w_ukv=v_w_ukv,
             conv_w=v_conv_w, conv_b=v_conv_b, dt_bias=v_dt_bias, a_log=v_a_log, d_skip=v_d_skip, ssd_norm=v_ssd_norm,
             w_out=v_w_out, post_mix_norm=v_post_mix_norm, pre_mlp_norm=v_pre_mlp_norm, w_up=v_w_up, w_down=v_w_down,
             post_mlp_norm=v_post_mlp_norm)
    sm = {name: w[name] for name, _ in _SMALL}

    wire = lambda name, l: _wire_shard(name, w[name][l])
    first = _gather_two_level([wire(name, 0) for name in _EARLY], "weight_gather_first")
    kws = [_early_weights({name: _from_wire(name, a) for name, a in zip(_EARLY, first)}), None]
    behind = [(name, 0) for name in _LATE] + [(name, 1) for name, _ in _SHARDED]

    def after_gather(gathered):
        got = {key: _from_wire(key[0], a) for key, a in zip(behind, gathered)}
        kws[0].update(_late_weights({name: got[name, 0] for name in _LATE}))
        kws[1] = {**_early_weights({name: got[name, 1] for name in _EARLY}),
                  **_late_weights({name: got[name, 1] for name in _LATE})}

    sent_behind = [(name, 1) for name, _ in _SHARDED] + [(name, 0) for name in _LATE]

    def exchange_of(g0, g1):
        blocks = {**{(name, 1): a for name, a in _shard_grads(g1).items()},
                  **{(name, 0): a for name, a in _shard_grads(g0).items()}}
        return [blocks[key] for key in sent_behind]

    loss_part, dx, grads, small, exchanged = _local_step(
        x[0], positions[0], kws, sm, loss_target[0], [wire(*key) for key in behind], after_gather, exchange_of)
    slots = dict(zip(sent_behind, exchanged))
    last = _shard_grads({k: grads[0][k] for k in ("win", "wuq", "wkv", "conv_w")})
    slots.update({(name, 0): a for name, a in zip(_EARLY, _comm("exchange", [last[name] for name in _EARLY], "grad_exchange_last"))})
    g_small = _unpack_small(_all_reduce_small(_pack_small({name: jnp.stack([small[l][name] for l in range(DEPTH)])
                                                           for name, _ in _SMALL})))
    loss = lax.psum(loss_part, ("x", "y", "c"))

    grad, delta, new_m, new_v = {}, {}, {}, {}
    for name, _ in _SHARDED:
        grad[name], delta[name], new_m[name], new_v[name] = _sum_adamw(
            [slots[name, 0], slots[name, 1]], w[name], m[name], v[name], f"sum_adamw_{name}")
    pk = lambda d: _pack_small({name: d[name] for name, _ in _SMALL})
    d_, m_, v_ = _adamw(pk(w), pk(g_small), pk(m), pk(v), "adamw_small")
    for dst, packed in ((delta, d_), (new_m, m_), (new_v, v_)):
        dst.update(_unpack_small(packed))
    grad.update(g_small)

    outs = [loss, dx[None]]
    for d in (grad, delta, new_m, new_v):
        outs += [d[name] for name in _WEIGHT_ORDER]
    return tuple(outs)
```

```python
import jax
import jax.numpy as jnp
import numpy as np
from jax import lax
from jax.experimental import pallas as pl
from jax.experimental.pallas import tpu as pltpu

F32 = jnp.float32
BF16 = jnp.bfloat16

D_MODEL = 1024
DEPTH = 2
N_DEV = 8
CHUNK = 64
EPS = 1e-6
MLA_HEADS = 8
QK_NOPE = 64
QK_ROPE = 32
V_DIM = 64
Q_RANK = 768
KV_RANK = 256
ROPE_THETA = 10000.0
SSD_HEADS = 8
SSD_P = 64
SSD_INNER = 512
SSD_GROUPS = 2
SSD_N = 128
CONV_W = 4
CONV_DIM = 1024
D_FF = 4096
IN_PROJ = 2600
HEAD_PAD = 128
IN_PAD = 2688
MISC_ROPE = 64
MISC_DT = 96
ATT_SCALE = (QK_NOPE + QK_ROPE) ** -0.5
LOG2E = 1.4426950408889634
ATT_SCALE_LOG2 = ATT_SCALE * LOG2E

ADAM_LR = 0.001
ADAM_B1 = 0.9
ADAM_B2 = 0.999
ADAM_EPS = 1e-08
ADAM_WD = 0.01
ADAM_STEP = 10

TM = 512
ATT_T = 512
ATT_G = 8
ATT_UNROLL = 4
SSD_ROWS = 512
TK_DW = 4096
V7X_VMEM_BYTES = 64 * 1024 * 1024
VMEM_LIMIT = V7X_VMEM_BYTES - 8 * 1024 * 1024

_NT = (((1,), (1,)), ((), ()))
_TN = (((0,), (0,)), ((), ()))


def _params(**kw):
    return pltpu.CompilerParams(vmem_limit_bytes=VMEM_LIMIT, **kw)


def _dot(a, b, precision=None):
    return jnp.dot(a, b, preferred_element_type=F32, precision=precision)


def _dot_nt(a, b, precision=None):
    return lax.dot_general(a, b, _NT, preferred_element_type=F32, precision=precision)


def _dot_tn(a, b, precision=None):
    return lax.dot_general(a, b, _TN, preferred_element_type=F32, precision=precision)


def _split3(x):
    hi = x.astype(BF16)
    r = x - hi.astype(F32)
    mid = r.astype(BF16)
    return hi, mid, (r - mid.astype(F32)).astype(BF16)


def _dot01(x, m01, dot=_dot, left=False):
    parts = [dot(m01, p) if left else dot(p, m01) for p in _split3(x)]
    return parts[0] + parts[1] + parts[2]


def _full(shape):
    n = len(shape)
    return pl.BlockSpec(shape, lambda *_: (0,) * n)


def _resident(shape):
    n = len(shape)
    return pl.BlockSpec(shape, lambda *_: (0,) * n, pipeline_mode=pl.Buffered(1))


def _rows(tm, width):
    return pl.BlockSpec((tm, width), lambda i: (i, 0))


def _rms_fwd(x, w):
    r = lax.rsqrt(jnp.mean(x * x, axis=-1, keepdims=True) + EPS)
    return (x * r) * w


def _rms_bwd(x, w, dy):
    r = lax.rsqrt(jnp.mean(x * x, axis=-1, keepdims=True) + EPS)
    xh = x * r
    dxn = dy * w
    dx = r * (dxn - xh * jnp.mean(dxn * xh, axis=-1, keepdims=True))
    return dx, dy * xh


def _acc_rows(ref, val, first):
    s = jnp.sum(val, axis=0, keepdims=True)

    @pl.when(first)
    def _():
        ref[...] = s

    @pl.when(jnp.logical_not(first))
    def _():
        ref[...] += s


def _rope(t, cosf, sinf, sign):
    lane = lax.broadcasted_iota(jnp.int32, t.shape, 1)
    rot = jnp.where(lane < MISC_ROPE + QK_ROPE // 2, -pltpu.roll(t, HEAD_PAD - QK_ROPE // 2, 1), pltpu.roll(t, QK_ROPE // 2, 1))
    return t * cosf + sign * (rot * sinf)


def _rope_tables(pos, invf):
    s = pos.shape[0]

    def body(pos_ref, invf_ref, cos_ref, sin_ref):
        ang = pos_ref[...].astype(F32) * invf_ref[...]
        cos_ref[...] = jnp.cos(ang)
        sin_ref[...] = jnp.sin(ang)

    return pl.pallas_call(
        body, name="rope_tables", grid=(s // TM,),
        in_specs=[_rows(TM, 1), _full((1, HEAD_PAD))],
        out_specs=[_rows(TM, HEAD_PAD), _rows(TM, HEAD_PAD)],
        out_shape=[jax.ShapeDtypeStruct((s, HEAD_PAD), F32)] * 2,
    )(pos, invf)


def _inproj_qkv_fwd(h, nw, win, qnw, kvnw, wuq, wkv, cosf, sinf):
    s = h.shape[0]

    def body(h_ref, nw_ref, w_ref, qnw_ref, kvnw_ref, wuq_ref, wkv_ref, cos_ref, sin_ref,
             ub_ref, cq_ref, ckv_ref, misc_ref, z_ref, xbc_ref, cqn_ref, ckvn_ref, q_ref, k_ref, v_ref):
        ub = _rms_fwd(h_ref[...], nw_ref[...]).astype(BF16)
        ub_ref[...] = ub
        proj = _dot(ub, w_ref[...])
        cq, ckv, m = proj[:, 0:768], proj[:, 768:1024], proj[:, 1024:1152]
        cq_ref[...] = cq
        ckv_ref[...] = ckv
        misc_ref[...] = m
        z_ref[...] = proj[:, 1152:1664]
        xbc_ref[...] = proj[:, 1664:2688]
        cosf, sinf = cos_ref[...], sin_ref[...]
        cqn = _rms_fwd(cq, qnw_ref[...]).astype(BF16)
        cqn_ref[...] = cqn
        q = _dot(cqn, wuq_ref[...])
        ckvn = _rms_fwd(ckv, kvnw_ref[...]).astype(BF16)
        ckvn_ref[...] = ckvn
        kv = _dot(ckvn, wkv_ref[...])
        lane = lax.broadcasted_iota(jnp.int32, m.shape, 1)
        in_rope = jnp.logical_and(lane >= MISC_ROPE, lane < MISC_ROPE + QK_ROPE)
        kr = jnp.where(in_rope, _rope(m, cosf, sinf, 1.0), 0.0)
        for hd in range(MLA_HEADS):
            cols = slice(hd * HEAD_PAD, (hd + 1) * HEAD_PAD)
            q_ref[:, cols] = _rope(q[:, cols], cosf, sinf, 1.0).astype(BF16)
            k_ref[:, cols] = (kv[:, cols] + kr).astype(BF16)
        vv = kv[:, MLA_HEADS * HEAD_PAD:]
        vlane = lax.broadcasted_iota(jnp.int32, vv.shape, 1)
        ones_at = jnp.where((vlane // HEAD_PAD) % 2 == 0, V_DIM, 0)
        v_ref[...] = jnp.where(vlane % HEAD_PAD == ones_at, 1.0, vv).astype(BF16)

    wide = MLA_HEADS * HEAD_PAD
    widths = (Q_RANK, KV_RANK, HEAD_PAD, SSD_INNER, CONV_DIM)
    return pl.pallas_call(
        body, name="inproj_qkv_fwd", grid=(s // TM,),
        in_specs=[_rows(TM, D_MODEL), _full((1, D_MODEL)), _resident((D_MODEL, IN_PAD)), _full((1, Q_RANK)), _full((1, KV_RANK)),
                  _resident((Q_RANK, wide)), _resident((KV_RANK, 2 * wide)), _rows(TM, HEAD_PAD), _rows(TM, HEAD_PAD)],
        out_specs=[_rows(TM, D_MODEL)] + [_rows(TM, w) for w in widths]
        + [_rows(TM, Q_RANK), _rows(TM, KV_RANK), _rows(TM, wide), _rows(TM, wide), _rows(TM, wide)],
        out_shape=[jax.ShapeDtypeStruct((s, D_MODEL), BF16)] + [jax.ShapeDtypeStruct((s, w), F32) for w in widths]
        + [jax.ShapeDtypeStruct((s, Q_RANK), BF16), jax.ShapeDtypeStruct((s, KV_RANK), BF16)]
        + [jax.ShapeDtypeStruct((s, wide), BF16)] * 3,
        compiler_params=_params(),
    )(h, nw, win, qnw, kvnw, wuq, wkv, cosf, sinf)


def _chunk_bias(t, keys_on_rows=False):
    row = lax.broadcasted_iota(jnp.int32, (t, 1), 0) // CHUNK
    col = lax.broadcasted_iota(jnp.int32, (1, t), 1) // CHUNK
    return jnp.where((row <= col) if keys_on_rows else (col <= row), 0.0, -jnp.inf).astype(F32)


def _attn_fwd(q, k, v, gather=()):
    s = q.shape[0]
    t = ATT_T
    nq = s // t
    pair = ATT_G * HEAD_PAD
    ng = len(gather)

    def body(q_ref, k_ref, v_ref, *rest):
        g_in, (o_ref, lse_ref), g_out = rest[:ng], rest[ng:ng + 2], rest[ng + 2:2 * ng + 2]
        m_s, acc_s, bias_s = rest[2 * ng + 2:2 * ng + 5]
        qi = pl.program_id(1)
        group, groups = pl.program_id(0), MLA_HEADS // ATT_G

        @pl.when(jnp.logical_and(group == 0, qi == 0))
        def _():
            bias_s[...] = _chunk_bias(t)

        _hosted_gather(g_in, g_out, rest[2 * ng + 5:],
                       jnp.logical_and(group == 0, qi == 0),
                       jnp.logical_and(group == groups - 1, qi == min(3 * nq // 4 + 1, nq - 1)),
                       jnp.logical_and(group == groups - 1, qi == nq - 1))
        m_s[...] = jnp.full(m_s.shape, -jnp.inf, F32)
        acc_s[...] = jnp.zeros(acc_s.shape, F32)

        def step(kb, masked):
            r0 = pl.multiple_of(kb * t, t)

            def scores(hh):
                cols = slice(hh * HEAD_PAD, (hh + 1) * HEAD_PAD)
                return _dot_nt(q_ref[:, cols], k_ref[pl.ds(r0, t), cols])

            def soft(hh, raw):
                sc = raw * ATT_SCALE_LOG2
                if masked:
                    sc = sc + bias_s[...]
                m_old = m_s[hh]
                m_new = jnp.maximum(m_old, jnp.max(sc, axis=-1, keepdims=True))
                alpha = jnp.exp2(m_old - m_new)
                p = jnp.exp2(sc - jnp.tile(m_new, (1, t // HEAD_PAD)))
                m_s[hh] = m_new
                return alpha, p.astype(BF16)

            def update(hh, alpha, p):
                cols = slice(hh * HEAD_PAD, (hh + 1) * HEAD_PAD)
                acc_s[hh] = alpha * acc_s[hh] + _dot(p, v_ref[pl.ds(r0, t), cols])

            raw, ap = [None] * ATT_G, [None] * ATT_G
            raw[0] = scores(0)
            for hh in range(ATT_G):
                if hh + 1 < ATT_G:
                    raw[hh + 1] = scores(hh + 1)
                ap[hh] = soft(hh, raw[hh])
                if hh >= 1:
                    update(hh - 1, *ap[hh - 1])
            update(ATT_G - 1, *ap[ATT_G - 1])

        def loop(i, c):
            step(2 * i, False)
            step(2 * i + 1, False)
            return c

        lax.fori_loop(0, qi // 2, loop, 0)

        @pl.when(qi % 2 == 1)
        def _():
            step(qi - 1, False)

        step(qi, True)
        for hh in range(ATT_G):
            cols = slice(hh * HEAD_PAD, (hh + 1) * HEAD_PAD)
            acc = acc_s[hh]
            ones_at = V_DIM * (1 - hh % 2)
            l = jnp.broadcast_to(acc[:, ones_at:ones_at + 1], acc.shape)
            o_ref[:, cols] = (acc / l).astype(BF16)
            lse_ref[hh] = (m_s[hh] + jnp.log(l) * LOG2E).T[0:8, :]

    outs = pl.pallas_call(
        body, name="attn_fwd_gather" if ng else "attn_fwd", grid=(MLA_HEADS // ATT_G, nq),
        in_specs=[pl.BlockSpec((t, pair), lambda h, i: (i, h)),
                  pl.BlockSpec((s, pair), lambda h, i: (0, h), pipeline_mode=pl.Buffered(1)),
                  pl.BlockSpec((s, pair), lambda h, i: (0, h), pipeline_mode=pl.Buffered(1))] + [_ANY] * ng,
        out_specs=[pl.BlockSpec((t, pair), lambda h, i: (i, h)),
                   pl.BlockSpec((ATT_G, 8, t), lambda h, i: (h, 0, i))] + [_ANY] * ng,
        out_shape=[jax.ShapeDtypeStruct((s, MLA_HEADS * HEAD_PAD), BF16), jax.ShapeDtypeStruct((MLA_HEADS, 8, s), F32)]
        + _comm_out_shapes("gather", gather),
        scratch_shapes=[pltpu.VMEM((ATT_G, t, HEAD_PAD), F32), pltpu.VMEM((ATT_G, t, HEAD_PAD), F32), pltpu.VMEM((t, t), F32)]
        + (_comm_scratch(ng) if ng else []),
        compiler_params=_params(),
    )(q, k, v, *gather)
    return outs[0], outs[1], list(outs[2:])


def _interleave(stages):
    live = list(stages)
    while live:
        still = []
        for g in live:
            try:
                next(g)
                still.append(g)
            except StopIteration:
                pass
        live = still


def _ssd_consts():
    emisc = np.zeros((HEAD_PAD, SSD_INNER), np.float32)
    for hd in range(SSD_HEADS):
        emisc[MISC_DT + hd, hd * SSD_P:(hd + 1) * SSD_P] = 1.0
    idx = np.arange(CHUNK)
    tri = (idx[:, None] >= idx[None, :]).astype(np.float32)
    return tuple(jnp.asarray(m, BF16) for m in (emisc, emisc.T.copy(), tri, tri.T.copy()))


def _ssd_chunk_common(cc, misc, emisc, tri, trit, dtb, a_exp):
    sig = jax.nn.sigmoid(cc)
    xa = cc * sig
    dt = jax.nn.softplus(_dot01(misc, emisc) + dtb)
    a = dt * a_exp
    acs = _dot01(a, tri, left=True)
    acs_t = _dot01(a, trit, dot=_dot_tn)
    alast = acs[CHUNK - 1:CHUNK, :]
    return xa, sig, dt, acs, acs_t, alast


def _decay(acs, acs_t, hd):
    row = lax.broadcasted_iota(jnp.int32, (CHUNK, CHUNK), 0)
    col = lax.broadcasted_iota(jnp.int32, (CHUNK, CHUNK), 1)
    diff = acs[:, hd * SSD_P:hd * SSD_P + 1] - acs_t[hd * SSD_P:hd * SSD_P + 1, :]
    return jnp.exp(jnp.where(row >= col, diff, -jnp.inf))


def _half_mask(hh):
    lane = lax.broadcasted_iota(jnp.int32, (CHUNK, 2 * SSD_P), 1)
    return (lane >= SSD_P) if hh else (lane < SSD_P)


def _gate_norm(y, zz):
    sg = jax.nn.sigmoid(zz)
    yz = y * (zz * sg)
    outs, rs = [], []
    half = SSD_INNER // SSD_GROUPS
    for g in range(SSD_GROUPS):
        yg = yz[:, g * half:(g + 1) * half]
        r = lax.rsqrt(jnp.mean(yg * yg, axis=-1, keepdims=True) + EPS)
        outs.append(yg * r)
        rs.append(r)
    return sg, jnp.concatenate(outs, axis=1), rs


def _ssd_outproj_fwd(xraw, misc, z, cw, cb, dtb, a_exp, d_exp, nw, consts, oe, h, wout, post_w):
    s = xraw.shape[0]
    nb = s // SSD_ROWS
    ncb = SSD_ROWS // CHUNK
    emisc, _, tri, trit = consts
    wide = MLA_HEADS * HEAD_PAD

    def body(x_ref, misc_ref, z_ref, cw_ref, cb_ref, dtb_ref, a_ref, d_ref, nw_ref, emisc_ref, tri_ref, trit_ref,
             oe_ref, h_ref, wout_ref, postw_ref, c_ref, prev_ref, ypre_ref, yssd_ref, mixed_ref, h1_ref, tail_s, state_s):
        i = pl.program_id(0)

        @pl.when(i == 0)
        def _():
            tail_s[...] = jnp.zeros(tail_s.shape, F32)
            state_s[...] = jnp.zeros(state_s.shape, F32)

        mixed_att = _dot(oe_ref[...], wout_ref[0:wide, :])
        x = x_ref[...]
        xext = jnp.concatenate([tail_s[...], x], axis=0)
        acc = x * cw_ref[CONV_W - 1:CONV_W, :] + cb_ref[...]
        for j in range(1, CONV_W):
            acc = acc + pltpu.roll(xext, j, 0)[8:, :] * cw_ref[CONV_W - 1 - j:CONV_W - j, :]
        tail_s[...] = x[SSD_ROWS - 8:, :]
        c_ref[...] = acc

        def chunk(ci):
            r0 = ci * CHUNK
            xa, _, dt, acs, acs_t, alast = _ssd_chunk_common(
                c_ref[pl.ds(r0, CHUNK), :], misc_ref[pl.ds(r0, CHUNK), :], emisc_ref[...], tri_ref[...], trit_ref[...],
                dtb_ref[...], a_ref[...])
            yield
            xs = xa[:, :SSD_INNER]
            xdt = xs * dt
            wgt = (xdt * jnp.exp(alast - acs)).astype(BF16)
            e = jnp.exp(acs)
            ys, new_states, cms = [], [], []
            for g in range(SSD_GROUPS):
                bm = xa[:, SSD_INNER + g * SSD_N:SSD_INNER + (g + 1) * SSD_N].astype(BF16)
                cm = xa[:, SSD_INNER + SSD_GROUPS * SSD_N + g * SSD_N:SSD_INNER + SSD_GROUPS * SSD_N + (g + 1) * SSD_N].astype(BF16)
                cms.append(cm)
                cb_g = _dot_nt(cm, bm)
                gl = slice(g * 256, (g + 1) * 256)
                new_states.append(_dot_tn(bm, wgt[:, gl]))
                for jj in range(2):
                    pair = 2 * g + jj
                    xp = xdt[:, pair * 128:(pair + 1) * 128]
                    yp = None
                    for hh in range(2):
                        sc = (cb_g * _decay(acs, acs_t, 2 * pair + hh)).astype(BF16)
                        term = _dot(sc, jnp.where(_half_mask(hh), xp, 0.0).astype(BF16))
                        yp = term if yp is None else yp + term
                    ys.append(yp)
                yield
            prev = state_s[...]
            prev_ref[ci] = prev
            yoff = jnp.concatenate([_dot(cms[g], prev[:, g * 256:(g + 1) * 256].astype(BF16)) for g in range(SSD_GROUPS)],
                                   axis=1) * e
            state_s[...] = prev * jnp.exp(alast) + jnp.concatenate(new_states, axis=1)
            yield
            y = jnp.concatenate(ys, axis=1) + yoff + d_ref[...] * xs
            ypre_ref[pl.ds(r0, CHUNK), :] = y
            _, yn, _ = _gate_norm(y, z_ref[pl.ds(r0, CHUNK), :])
            yssd_ref[pl.ds(r0, CHUNK), :] = (yn * nw_ref[...]).astype(BF16)

        _interleave([chunk(ci) for ci in range(ncb)])
        mixed = mixed_att + _dot(yssd_ref[...], wout_ref[wide:, :])
        mixed_ref[...] = mixed
        h1_ref[...] = h_ref[...] + _rms_fwd(mixed, postw_ref[...])

    return pl.pallas_call(
        body, name="ssd_outproj_fwd", grid=(nb,),
        in_specs=[_rows(SSD_ROWS, CONV_DIM), _rows(SSD_ROWS, HEAD_PAD), _rows(SSD_ROWS, SSD_INNER),
                  _full((CONV_W, CONV_DIM)), _full((1, CONV_DIM)), _full((1, SSD_INNER)), _full((1, SSD_INNER)),
                  _full((1, SSD_INNER)), _full((1, SSD_INNER)), _full((HEAD_PAD, SSD_INNER)), _full((CHUNK, CHUNK)),
                  _full((CHUNK, CHUNK)), _rows(SSD_ROWS, wide), _rows(SSD_ROWS, D_MODEL),
                  _resident((wide + SSD_INNER, D_MODEL)), _full((1, D_MODEL))],
        out_specs=[_rows(SSD_ROWS, CONV_DIM), pl.BlockSpec((ncb, SSD_N, SSD_INNER), lambda i: (i, 0, 0)),
                   _rows(SSD_ROWS, SSD_INNER), _rows(SSD_ROWS, SSD_INNER), _rows(SSD_ROWS, D_MODEL), _rows(SSD_ROWS, D_MODEL)],
        out_shape=[jax.ShapeDtypeStruct((s, CONV_DIM), F32), jax.ShapeDtypeStruct((s // CHUNK, SSD_N, SSD_INNER), F32),
                   jax.ShapeDtypeStruct((s, SSD_INNER), F32), jax.ShapeDtypeStruct((s, SSD_INNER), BF16),
                   jax.ShapeDtypeStruct((s, D_MODEL), F32), jax.ShapeDtypeStruct((s, D_MODEL), F32)],
        scratch_shapes=[pltpu.VMEM((8, CONV_DIM), F32), pltpu.VMEM((SSD_N, SSD_INNER), F32)],
        compiler_params=_params(),
    )(xraw, misc, z, cw, cb, dtb, a_exp, d_exp, nw, emisc, tri, trit, oe, h, wout, post_w)


def _mlp_fwd(h1, prew, wup, wdown, postw, target=None):
    s = h1.shape[0]
    fb = D_FF // N_DEV
    last = target is not None

    def body(h_ref, prew_ref, up_ref, down_ref, postw_ref, *rest):
        target_ref, (mb_ref, ab_ref, d_ref, out_ref) = (rest[0] if last else None), rest[last:last + 4]
        hh = h_ref[...]
        mb = _rms_fwd(hh, prew_ref[...]).astype(BF16)
        mb_ref[...] = mb
        d = jnp.zeros((TM, D_MODEL), F32)
        for j in range(N_DEV):
            a = jnp.maximum(_dot(mb, up_ref[j]), 0.0)
            ab_ref[j] = a.astype(BF16)
            d = d + _dot(jnp.square(a).astype(BF16), down_ref[j])
        d_ref[...] = d
        h2 = hh + _rms_fwd(d, postw_ref[...])
        if last:
            diff = h2 - target_ref[...]
            out_ref[...] = diff * (1.0 / D_MODEL)
            part = 0.5 * jnp.sum(jnp.mean(diff * diff, axis=-1, keepdims=True), axis=0, keepdims=True)
            _acc_rows(rest[-1], part, pl.program_id(0) == 0)
        else:
            out_ref[...] = h2

    stacked = pl.BlockSpec((N_DEV, TM, fb), lambda i: (0, i, 0))
    return pl.pallas_call(
        body, name="mlp_fwd_loss" if last else "mlp_fwd", grid=(s // TM,),
        in_specs=[_rows(TM, D_MODEL), _full((1, D_MODEL)), _resident((N_DEV, D_MODEL, fb)), _resident((N_DEV, fb, D_MODEL)),
                  _full((1, D_MODEL))] + ([_rows(TM, D_MODEL)] if last else []),
        out_specs=[_rows(TM, D_MODEL), stacked, _rows(TM, D_MODEL), _rows(TM, D_MODEL)] + ([_full((1, 1))] if last else []),
        out_shape=[jax.ShapeDtypeStruct((s, D_MODEL), BF16), jax.ShapeDtypeStruct((N_DEV, s, fb), BF16),
                   jax.ShapeDtypeStruct((s, D_MODEL), F32), jax.ShapeDtypeStruct((s, D_MODEL), F32)]
        + ([jax.ShapeDtypeStruct((1, 1), F32)] if last else []),
        compiler_params=_params(),
    )(h1, prew, wup, wdown, postw, *([target] if last else []))


def _mlp_bwd(dh2, d, h1, ab, prew, wup, wdown, postw):
    s = dh2.shape[0]
    fb = D_FF // N_DEV
    tm = TM // 2

    def body(dh2_ref, d_ref, h1_ref, ab_ref, prew_ref, up_ref, down_ref, postw_ref,
             dh1_ref, da_ref, dd_ref, gpost_ref, gpre_ref):
        first = pl.program_id(0) == 0
        dh2 = dh2_ref[...]
        dd, gpost = _rms_bwd(d_ref[...], postw_ref[...], dh2)
        _acc_rows(gpost_ref, gpost, first)
        ddb = dd.astype(BF16)
        dd_ref[...] = ddb

        def d_relu_squared(j):
            return _dot_nt(ddb, down_ref[j])

        def pointwise(j, dr):
            da = (dr * (2.0 * ab_ref[j].astype(F32))).astype(BF16)
            da_ref[j] = da
            return da

        dm = jnp.zeros((tm, D_MODEL), F32)
        nxt, da_prev = d_relu_squared(0), None
        for j in range(N_DEV):
            cur = nxt
            if j + 1 < N_DEV:
                nxt = d_relu_squared(j + 1)
            da = pointwise(j, cur)
            if da_prev is not None:
                dm = dm + _dot_nt(da_prev, up_ref[j - 1])
            da_prev = da
        dm = dm + _dot_nt(da_prev, up_ref[N_DEV - 1])
        dx, gpre = _rms_bwd(h1_ref[...], prew_ref[...], dm)
        _acc_rows(gpre_ref, gpre, first)
        dh1_ref[...] = dh2 + dx

    stacked = pl.BlockSpec((N_DEV, tm, fb), lambda i: (0, i, 0))
    return pl.pallas_call(
        body, name="mlp_bwd", grid=(s // tm,),
        in_specs=[_rows(tm, D_MODEL)] * 3 + [stacked, _full((1, D_MODEL)), _resident((N_DEV, D_MODEL, fb)),
                                              _resident((N_DEV, fb, D_MODEL)), _full((1, D_MODEL))],
        out_specs=[_rows(tm, D_MODEL), stacked, _rows(tm, D_MODEL), _full((1, D_MODEL)), _full((1, D_MODEL))],
        out_shape=[jax.ShapeDtypeStruct((s, D_MODEL), F32), jax.ShapeDtypeStruct((N_DEV, s, fb), BF16),
                   jax.ShapeDtypeStruct((s, D_MODEL), BF16), jax.ShapeDtypeStruct((1, D_MODEL), F32),
                   jax.ShapeDtypeStruct((1, D_MODEL), F32)],
        compiler_params=_params(),
    )(dh2, d, h1, ab, prew, wup, wdown, postw)


def _matmul_tn(a, b, name, tk=TK_DW):
    s, m = a.shape
    n = b.shape[1]
    tn = n if n <= 1024 else (n // 2 if (n // 2) % 128 == 0 else n // 3)
    tk = min(tk, s)
    assert n % tn == 0 and tn % 128 == 0 and s % tk == 0

    def body(a_ref, b_ref, o_ref):
        part = _dot_tn(a_ref[...], b_ref[...])

        @pl.when(pl.program_id(1) == 0)
        def _():
            o_ref[...] = part

        @pl.when(pl.program_id(1) != 0)
        def _():
            o_ref[...] += part

    return pl.pallas_call(
        body, name=name, grid=(n // tn, s // tk),
        in_specs=[pl.BlockSpec((tk, m), lambda j, k: (k, 0)), pl.BlockSpec((tk, tn), lambda j, k: (k, j))],
        out_specs=pl.BlockSpec((m, tn), lambda j, k: (0, j)),
        out_shape=jax.ShapeDtypeStruct((m, n), F32),
        compiler_params=_params(),
    )(a, b)


def _matmul_tn_stacked(a, b, name, a_stacked, square_a=False, tk=TK_DW):
    tk = min(tk, a.shape[-2])
    if a_stacked:
        _, s, m = a.shape
        n = b.shape[1]
        in_specs = [pl.BlockSpec((1, tk, m), lambda j, k: (j, k, 0)), pl.BlockSpec((tk, n), lambda j, k: (k, 0))]
    else:
        s, m = a.shape
        n = b.shape[2]
        in_specs = [pl.BlockSpec((tk, m), lambda j, k: (k, 0)), pl.BlockSpec((1, tk, n), lambda j, k: (j, k, 0))]

    nk = s // tk

    def body(a_ref, b_ref, o_ref, acc_s):
        av = a_ref[0] if a_stacked else a_ref[...]
        bv = b_ref[...] if a_stacked else b_ref[0]
        if square_a:
            av = jnp.square(av.astype(F32)).astype(BF16)
        part = _dot_tn(av, bv)
        k = pl.program_id(1)

        @pl.when(k == 0)
        def _():
            acc_s[...] = part

        @pl.when(jnp.logical_and(k != 0, k != nk - 1))
        def _():
            acc_s[...] += part

        @pl.when(k == nk - 1)
        def _():
            o_ref[0] = (part if nk == 1 else acc_s[...] + part).astype(BF16)

    return pl.pallas_call(
        body, name=name, grid=(N_DEV, nk),
        in_specs=in_specs,
        out_specs=pl.BlockSpec((1, m, n), lambda j, k: (j, 0, 0)),
        out_shape=jax.ShapeDtypeStruct((N_DEV, m, n), BF16),
        scratch_shapes=[pltpu.VMEM((m, n), F32)],
        compiler_params=_params(),
    )(a, b)


def _outproj_bwd(dh1, mixed, nw, wout, oe):
    s = dh1.shape[0]
    wide = MLA_HEADS * HEAD_PAD

    def body(dh1_ref, mixed_ref, nw_ref, w_ref, oe_ref, dmix_ref, doe_ref, dy_ref, gnw_ref, delta_ref):
        dmix, gnw = _rms_bwd(mixed_ref[...], nw_ref[...], dh1_ref[...])
        _acc_rows(gnw_ref, gnw, pl.program_id(0) == 0)
        dmb = dmix.astype(BF16)
        dmix_ref[...] = dmb
        doe_ref[...] = _dot_nt(dmb, w_ref[0:wide, :]).astype(BF16)
        dy_ref[...] = _dot_nt(dmb, w_ref[wide:, :])
        ones = jnp.ones((8, HEAD_PAD), BF16)
        for hd in range(MLA_HEADS):
            cols = slice(hd * HEAD_PAD, (hd + 1) * HEAD_PAD)
            prod = oe_ref[:, cols].astype(F32) * doe_ref[:, cols].astype(F32)
            delta_ref[hd] = _dot01(prod, ones, dot=_dot_nt, left=True)

    return pl.pallas_call(
        body, name="outproj_bwd", grid=(s // TM,),
        in_specs=[_rows(TM, D_MODEL), _rows(TM, D_MODEL), _full((1, D_MODEL)), _resident((wide + SSD_INNER, D_MODEL)),
                  _rows(TM, wide)],
        out_specs=[_rows(TM, D_MODEL), _rows(TM, wide), _rows(TM, SSD_INNER), _full((1, D_MODEL)),
                   pl.BlockSpec((MLA_HEADS, 8, TM), lambda i: (0, 0, i))],
        out_shape=[jax.ShapeDtypeStruct((s, D_MODEL), BF16), jax.ShapeDtypeStruct((s, wide), BF16),
                   jax.ShapeDtypeStruct((s, SSD_INNER), F32), jax.ShapeDtypeStruct((1, D_MODEL), F32),
                   jax.ShapeDtypeStruct((MLA_HEADS, 8, s), F32)],
        compiler_params=_params(),
    )(dh1, mixed, nw, wout, oe)


def _attn_bwd(q, k, v, do, lse, delta, exchange=()):
    s = q.shape[0]
    t = ATT_T
    nq = s // t
    pair = 2 * HEAD_PAD
    ne = len(exchange)

    def body(q_ref, k_ref, v_ref, do_ref, lse_ref, delta_ref, *rest):
        e_in, (dq_ref, dk_ref, dv_ref), e_out = rest[:ne], rest[ne:ne + 3], rest[ne + 3:2 * ne + 3]
        dk_s, dv_s, bias_s = rest[2 * ne + 3:2 * ne + 6]
        kb = pl.program_id(1)
        _hosted_comm("exchange", e_in, e_out, rest[2 * ne + 6:],
                     jnp.logical_and(pl.program_id(0) == 0, kb == 0),
                     jnp.logical_and(pl.program_id(0) == MLA_HEADS // 2 - 1, kb == nq - 1))

        @pl.when(jnp.logical_and(pl.program_id(0) == 0, kb == 0))
        def _():
            bias_s[...] = _chunk_bias(t, keys_on_rows=True)

        @pl.when(kb == 0)
        def _():
            dq_ref[...] = jnp.zeros(dq_ref.shape, F32)

        def step(qb, diagonal):
            r0 = pl.multiple_of(qb * t, t)
            for hh in range(2):
                cols = slice(hh * HEAD_PAD, (hh + 1) * HEAD_PAD)
                kk = k_ref[:, cols]
                qq = q_ref[pl.ds(r0, t), cols]
                dd = do_ref[pl.ds(r0, t), cols]
                sc = _dot_nt(kk, qq) * ATT_SCALE_LOG2
                if diagonal:
                    sc = sc + bias_s[...]
                p = jnp.exp2(sc - lse_ref[hh, 0:1, pl.ds(r0, t)])
                dv = _dot(p.astype(BF16), dd)
                dp = _dot_nt(v_ref[:, cols], dd)
                ds = (p * (dp - delta_ref[hh, 0:1, pl.ds(r0, t)]) * ATT_SCALE).astype(BF16)
                dk = _dot(ds, qq)
                if diagonal:
                    dv_s[:, cols] = dv
                    dk_s[:, cols] = dk
                else:
                    dv_s[:, cols] += dv
                    dk_s[:, cols] += dk
                dq_ref[pl.ds(r0, t), cols] += _dot_tn(ds, kk)

        def loop(i, c):
            for u in range(ATT_UNROLL):
                step(kb + 1 + u + ATT_UNROLL * i, False)
            return c

        step(kb, True)
        later_tiles = nq - 1 - kb
        lax.fori_loop(0, later_tiles // ATT_UNROLL, loop, 0)
        left = later_tiles % ATT_UNROLL
        for u in range(ATT_UNROLL - 1):
            @pl.when(left > u)
            def _(u=u):
                step(nq - left + u, False)

        dk_ref[...] = dk_s[...].astype(BF16)
        dv_ref[...] = dv_s[...].astype(BF16)

    whole = pl.BlockSpec((s, pair), lambda h, i: (0, h))
    tile = pl.BlockSpec((t, pair), lambda h, i: (i, h))
    rowvec = pl.BlockSpec((2, 8, s), lambda h, i: (h, 0, 0))
    wide = MLA_HEADS * HEAD_PAD
    outs = pl.pallas_call(
        body, name="attn_bwd_exchange" if ne else "attn_bwd", grid=(MLA_HEADS // 2, nq),
        in_specs=[whole, tile, tile, whole, rowvec, rowvec] + [_ANY] * ne,
        out_specs=[whole, tile, tile] + [_ANY] * ne,
        out_shape=[jax.ShapeDtypeStruct((s, wide), F32)] + [jax.ShapeDtypeStruct((s, wide), BF16)] * 2
        + _comm_out_shapes("exchange", exchange),
        scratch_shapes=[pltpu.VMEM((t, pair), F32), pltpu.VMEM((t, pair), F32), pltpu.VMEM((t, t), F32)]
        + (_comm_scratch(ne) if ne else []),
        compiler_params=_params(),
    )(q, k, v, do, lse, delta, *exchange)
    return outs[0], outs[1], outs[2], list(outs[3:])


def _ssd_bwd(dy, ypre, z, c, xraw, misc, prev, cw, dtb, a_exp, d_exp, nw, consts):
    s = dy.shape[0]
    nb = s // SSD_ROWS
    ncb = SSD_ROWS // CHUNK
    emisc, emisc_t, tri, trit = consts

    def body(dy_ref, ypre_ref, z_ref, c_ref, x_ref, misc_ref, prev_ref, cw_ref, dtb_ref, a_ref, d_ref, nw_ref,
             emisc_ref, emisct_ref, tri_ref, trit_ref,
             dz_ref, dx_ref, dmisc_ref, gnw_ref, gd_ref, galog_ref, gdtb_ref, gcw_ref, gcb_ref,
             dst_s, dc_s, head_s):
        i = pl.program_id(0)
        first = i == 0

        @pl.when(first)
        def _():
            dst_s[...] = jnp.zeros(dst_s.shape, F32)
            head_s[...] = jnp.zeros(head_s.shape, F32)
            gnw_ref[...] = jnp.zeros(gnw_ref.shape, F32)
            gd_ref[...] = jnp.zeros(gd_ref.shape, F32)
            galog_ref[...] = jnp.zeros(galog_ref.shape, F32)
            gdtb_ref[...] = jnp.zeros(gdtb_ref.shape, F32)

        a_exp_v = a_ref[...]
        a8 = _dot01(a_exp_v, emisct_ref[...]) * (1.0 / SSD_P)

        def chunk(ci):
            r0 = ci * CHUNK
            cc = c_ref[pl.ds(r0, CHUNK), :]
            mm = misc_ref[pl.ds(r0, CHUNK), :]
            xa, sig_c, dt, acs, acs_t, alast = _ssd_chunk_common(cc, mm, emisc_ref[...], tri_ref[...], trit_ref[...],
                                                              dtb_ref[...], a_exp_v)
            yield
            xs = xa[:, :SSD_INNER]
            xdt = xs * dt
            y = ypre_ref[pl.ds(r0, CHUNK), :]
            zz = z_ref[pl.ds(r0, CHUNK), :]
            sg, yn, rs = _gate_norm(y, zz)
            dyo = dy_ref[pl.ds(r0, CHUNK), :]
            gnw_ref[...] += jnp.sum(dyo * yn, axis=0, keepdims=True)
            dyn = dyo * nw_ref[...]
            half = SSD_INNER // SSD_GROUPS
            dyz_parts = []
            for g in range(SSD_GROUPS):
                gl = slice(g * half, (g + 1) * half)
                dyz_parts.append(rs[g] * (dyn[:, gl] - yn[:, gl] * jnp.mean(dyn[:, gl] * yn[:, gl], axis=-1, keepdims=True)))
            dyz = jnp.concatenate(dyz_parts, axis=1)
            dz_ref[pl.ds(r0, CHUNK), :] = dyz * y * (sg * (1.0 + zz * (1.0 - sg)))
            dyp = dyz * (zz * sg)
            dypb = dyp.astype(BF16)
            gd_ref[...] += jnp.sum(dyp * xs, axis=0, keepdims=True)
            yield
            prev = prev_ref[ci]
            cd = jnp.exp(alast)
            e = jnp.exp(acs)
            dsx = jnp.exp(alast - acs)
            wgt = (xdt * dsx).astype(BF16)
            dze = (dyp * e).astype(BF16)
            dprev_parts, diag_all, dbm, dcm, yoff_parts, bms = [], [], [], [], [], []
            lane8 = lax.broadcasted_iota(jnp.int32, (CHUNK, HEAD_PAD), 1)
            diag8 = jnp.zeros((CHUNK, HEAD_PAD), F32)
            for g in range(SSD_GROUPS):
                gl = slice(g * 256, (g + 1) * 256)
                bm = xa[:, SSD_INNER + g * SSD_N:SSD_INNER + (g + 1) * SSD_N].astype(BF16)
                cm = xa[:, SSD_INNER + SSD_GROUPS * SSD_N + g * SSD_N:SSD_INNER + SSD_GROUPS * SSD_N + (g + 1) * SSD_N].astype(BF16)
                bms.append(bm)
                prev_g = prev[:, gl].astype(BF16)
                dcm_g = _dot_nt(dze[:, gl], prev_g)
                dprev_parts.append(_dot_tn(cm, dze[:, gl]))
                cb_g = _dot_nt(cm, bm)
                dcb = jnp.zeros((CHUNK, CHUNK), F32)
                diag_parts = []
                for jj in range(2):
                    pair = 2 * g + jj
                    pl_ = slice(pair * 128, (pair + 1) * 128)
                    xp = xdt[:, pl_]
                    dyp_p = dypb[:, pl_]
                    dxp = jnp.zeros((CHUNK, 128), F32)
                    for hh in range(2):
                        hd = 2 * pair + hh
                        dec = _decay(acs, acs_t, hd)
                        xm = jnp.where(_half_mask(hh), xp, 0.0).astype(BF16)
                        dsc = _dot_nt(dyp_p, xm) * dec
                        dcb = dcb + dsc
                        sc = (cb_g * dec).astype(BF16)
                        dxp = dxp + jnp.where(_half_mask(hh), _dot_tn(sc, dyp_p), 0.0)
                        dm = dsc * cb_g
                        diag8 = diag8 + jnp.where(lane8 == MISC_DT + hd, jnp.sum(dm - dm.T, axis=1, keepdims=True), 0.0)
                    diag_parts.append(dxp)
                dcbb = dcb.astype(BF16)
                dcm.append(dcm_g + _dot(dcbb, bm))
                dbm.append(_dot_tn(dcbb, cm))
                diag_all.append(jnp.concatenate(diag_parts, axis=1))
                yoff_parts.append(_dot(cm, prev_g) * e[:, gl])
                yield
            dst = dst_s[...]
            glast = jnp.sum(dst * prev, axis=0, keepdims=True) * cd
            dxdt_state_parts = []
            for g in range(SSD_GROUPS):
                gl = slice(g * 256, (g + 1) * 256)
                dst_g = dst[:, gl].astype(BF16)
                dxdt_state_parts.append(_dot(bms[g], dst_g) * dsx[:, gl])
                dbm[g] = dbm[g] + _dot_nt(wgt[:, gl], dst_g)
            dst_s[...] = dst * cd + jnp.concatenate(dprev_parts, axis=1)
            yield
            dxdt_state = jnp.concatenate(dxdt_state_parts, axis=1)
            dxdt = jnp.concatenate(diag_all, axis=1) + dxdt_state
            dacs = dyp * jnp.concatenate(yoff_parts, axis=1) - xdt * dxdt_state
            last = jnp.sum(xdt * dxdt_state, axis=0, keepdims=True) + glast
            row = lax.broadcasted_iota(jnp.int32, (CHUNK, SSD_INNER), 0)
            dacs = dacs + jnp.where(row == CHUNK - 1, last, 0.0)
            dacs8 = _dot01(dacs, emisct_ref[...]) + diag8
            da8 = _dot01(dacs8, trit_ref[...], left=True)
            ddt8 = da8 * a8 + _dot01(dxdt * xs, emisct_ref[...])
            yield
            dtr8 = mm + _dot01(dtb_ref[...], emisct_ref[...]) * (1.0 / SSD_P)
            dt8 = jax.nn.softplus(dtr8)
            lane = lax.broadcasted_iota(jnp.int32, (CHUNK, HEAD_PAD), 1)
            on_dt = jnp.logical_and(lane >= MISC_DT, lane < MISC_DT + SSD_HEADS)
            ddtr8 = jnp.where(on_dt, ddt8 * jax.nn.sigmoid(dtr8), 0.0)
            dmisc_ref[pl.ds(r0, CHUNK), :] = ddtr8
            gdtb_ref[...] += jnp.sum(ddtr8, axis=0, keepdims=True)
            galog_ref[...] += jnp.sum(jnp.where(on_dt, da8 * dt8, 0.0), axis=0, keepdims=True) * a8
            dxs = d_ref[...] * dyp + dxdt * dt
            dxa = jnp.concatenate([dxs] + dbm + dcm, axis=1)
            dc_s[pl.ds(r0, CHUNK), :] = dxa * (sig_c * (1.0 + cc * (1.0 - sig_c)))

        _interleave([chunk(ci) for ci in reversed(range(ncb))])

        dc = dc_s[...]
        x = x_ref[...]
        dcext = jnp.concatenate([dc, head_s[...]], axis=0)
        dx = dc * cw_ref[CONV_W - 1:CONV_W, :]
        rows = [jnp.sum(dc * x, axis=0, keepdims=True)]
        for j in range(1, CONV_W):
            ahead = pltpu.roll(dcext, SSD_ROWS + 8 - j, 0)[:SSD_ROWS, :]
            dx = dx + ahead * cw_ref[CONV_W - 1 - j:CONV_W - j, :]
            rows.insert(0, jnp.sum(ahead * x, axis=0, keepdims=True))
        dx_ref[...] = dx
        head_s[...] = dc[:8, :]
        gcw = jnp.concatenate(rows, axis=0)

        @pl.when(first)
        def _():
            gcw_ref[...] = gcw
            gcb_ref[...] = jnp.sum(dc, axis=0, keepdims=True)

        @pl.when(jnp.logical_not(first))
        def _():
            gcw_ref[...] += gcw
            gcb_ref[...] += jnp.sum(dc, axis=0, keepdims=True)

    def rev(width):
        return pl.BlockSpec((SSD_ROWS, width), lambda i: (nb - 1 - i, 0))

    return pl.pallas_call(
        body, name="ssd_bwd", grid=(nb,),
        in_specs=[rev(SSD_INNER), rev(SSD_INNER), rev(SSD_INNER), rev(CONV_DIM), rev(CONV_DIM),
                  rev(HEAD_PAD), pl.BlockSpec((ncb, SSD_N, SSD_INNER), lambda i: (nb - 1 - i, 0, 0)),
                  _full((CONV_W, CONV_DIM)), _full((1, SSD_INNER)), _full((1, SSD_INNER)), _full((1, SSD_INNER)),
                  _full((1, SSD_INNER)), _full((HEAD_PAD, SSD_INNER)), _full((SSD_INNER, HEAD_PAD)), _full((CHUNK, CHUNK)),
                  _full((CHUNK, CHUNK))],
        out_specs=[rev(SSD_INNER), rev(CONV_DIM), rev(HEAD_PAD), _full((1, SSD_INNER)), _full((1, SSD_INNER)),
                   _full((1, HEAD_PAD)), _full((1, HEAD_PAD)), _full((CONV_W, CONV_DIM)), _full((1, CONV_DIM))],
        out_shape=[jax.ShapeDtypeStruct((s, SSD_INNER), F32), jax.ShapeDtypeStruct((s, CONV_DIM), F32),
                   jax.ShapeDtypeStruct((s, HEAD_PAD), F32), jax.ShapeDtypeStruct((1, SSD_INNER), F32),
                   jax.ShapeDtypeStruct((1, SSD_INNER), F32), jax.ShapeDtypeStruct((1, HEAD_PAD), F32),
                   jax.ShapeDtypeStruct((1, HEAD_PAD), F32), jax.ShapeDtypeStruct((CONV_W, CONV_DIM), F32),
                   jax.ShapeDtypeStruct((1, CONV_DIM), F32)],
        scratch_shapes=[pltpu.VMEM((SSD_N, SSD_INNER), F32), pltpu.VMEM((SSD_ROWS, CONV_DIM), F32), pltpu.VMEM((8, CONV_DIM), F32)],
        compiler_params=_params(),
    )(dy, ypre, z, c, xraw, misc, prev, cw, dtb, a_exp, d_exp, nw, emisc, emisc_t, tri, trit)


def _qkv_inproj_bwd(dq, dk, dv, cq, ckv, dmisc_dt, dz, dxbc, h, dh1, qnw, kvnw, nw, wuq, wkv, win, cosf, sinf):
    s = dq.shape[0]
    wide = MLA_HEADS * HEAD_PAD
    tm = TM

    def body(dq_ref, dk_ref, dv_ref, cq_ref, ckv_ref, dmdt_ref, dz_ref, dxbc_ref, h_ref, dh1_ref, qnw_ref, kvnw_ref, nw_ref,
             wuq_ref, wkv_ref, win_ref, cos_ref, sin_ref, dqb_ref, dkvb_ref, dproj_ref, dh0_ref, gq_ref, gkv_ref, gnw_ref):
        first = pl.program_id(0) == 0
        cosf, sinf = cos_ref[...], sin_ref[...]
        dkr = jnp.zeros((tm, HEAD_PAD), F32)
        for hd in range(MLA_HEADS):
            cols = slice(hd * HEAD_PAD, (hd + 1) * HEAD_PAD)
            dqb_ref[:, cols] = _rope(dq_ref[:, cols], cosf, sinf, -1.0).astype(BF16)
            dkh = dk_ref[:, cols]
            dkvb_ref[:, cols] = dkh
            dkr = dkr + dkh
        dkvb_ref[:, wide:] = dv_ref[...]
        lane = lax.broadcasted_iota(jnp.int32, dkr.shape, 1)
        in_rope = jnp.logical_and(lane >= MISC_ROPE, lane < MISC_ROPE + QK_ROPE)
        dmisc_rope = jnp.where(in_rope, _rope(jnp.where(in_rope, dkr, 0.0), cosf, sinf, -1.0), 0.0)
        dcq, gq = _rms_bwd(cq_ref[...], qnw_ref[...], _dot_nt(dqb_ref[...], wuq_ref[...]))
        _acc_rows(gq_ref, gq, first)
        dckv, gkv = _rms_bwd(ckv_ref[...], kvnw_ref[...], _dot_nt(dkvb_ref[...], wkv_ref[...]))
        _acc_rows(gkv_ref, gkv, first)
        dproj_ref[:, 0:768] = dcq.astype(BF16)
        dproj_ref[:, 768:1024] = dckv.astype(BF16)
        dproj_ref[:, 1024:1152] = (dmisc_rope + dmdt_ref[...]).astype(BF16)
        dproj_ref[:, 1152:1664] = dz_ref[...].astype(BF16)
        dproj_ref[:, 1664:2688] = dxbc_ref[...].astype(BF16)
        dx, gnw = _rms_bwd(h_ref[...], nw_ref[...], _dot_nt(dproj_ref[...], win_ref[...]))
        _acc_rows(gnw_ref, gnw, first)
        dh0_ref[...] = dh1_ref[...] + dx

    return pl.pallas_call(
        body, name="qkv_inproj_bwd", grid=(s // tm,),
        in_specs=[_rows(tm, wide)] * 3 + [_rows(tm, Q_RANK), _rows(tm, KV_RANK), _rows(tm, HEAD_PAD), _rows(tm, SSD_INNER),
                                          _rows(tm, CONV_DIM), _rows(tm, D_MODEL), _rows(tm, D_MODEL),
                                          _full((1, Q_RANK)), _full((1, KV_RANK)), _full((1, D_MODEL)),
                                          _resident((Q_RANK, wide)), _resident((KV_RANK, 2 * wide)), _resident((D_MODEL, IN_PAD)),
                                          _rows(tm, HEAD_PAD), _rows(tm, HEAD_PAD)],
        out_specs=[_rows(tm, wide), _rows(tm, 2 * wide), _rows(tm, IN_PAD), _rows(tm, D_MODEL),
                   _full((1, Q_RANK)), _full((1, KV_RANK)), _full((1, D_MODEL))],
        out_shape=[jax.ShapeDtypeStruct((s, wide), BF16), jax.ShapeDtypeStruct((s, 2 * wide), BF16),
                   jax.ShapeDtypeStruct((s, IN_PAD), BF16), jax.ShapeDtypeStruct((s, D_MODEL), F32),
                   jax.ShapeDtypeStruct((1, Q_RANK), F32), jax.ShapeDtypeStruct((1, KV_RANK), F32),
                   jax.ShapeDtypeStruct((1, D_MODEL), F32)],
        compiler_params=_params(),
    )(dq, dk, dv, cq, ckv, dmisc_dt, dz, dxbc, h, dh1, qnw, kvnw, nw, wuq, wkv, win, cosf, sinf)


def _row_tile(rows, cols):
    cap = max(8, (1 << 18) // max(cols, 128))
    best = None
    for t in range(8, rows + 1, 8):
        if rows % t == 0 and t <= cap:
            best = t
    return best if best is not None else rows


def _adamw(w, g, m, v, name):
    rows, cols = w.shape
    tr = _row_tile(rows, cols)

    def body(w_ref, g_ref, m_ref, v_ref, d_ref, m2_ref, v2_ref):
        gg = g_ref[...]
        m2 = ADAM_B1 * m_ref[...] + (1.0 - ADAM_B1) * gg
        v2 = ADAM_B2 * v_ref[...] + (1.0 - ADAM_B2) * jnp.square(gg)
        m_hat = m2 / (1.0 - ADAM_B1 ** ADAM_STEP)
        v_hat = v2 / (1.0 - ADAM_B2 ** ADAM_STEP)
        d_ref[...] = -ADAM_LR * (m_hat / (jnp.sqrt(v_hat) + ADAM_EPS) + ADAM_WD * w_ref[...])
        m2_ref[...] = m2
        v2_ref[...] = v2

    spec = pl.BlockSpec((tr, cols), lambda i: (i, 0))
    return pl.pallas_call(
        body, name=name, grid=(rows // tr,),
        in_specs=[spec] * 4, out_specs=[spec] * 3,
        out_shape=[jax.ShapeDtypeStruct((rows, cols), F32)] * 3,
    )(w, g, m, v)


def _sum_adamw(slots, w, m, v, name, exchange=()):
    _, rows, cols = w.shape
    tr = _row_tile(rows, cols)
    nb = rows // tr
    ne = len(exchange)

    def body(s0_ref, s1_ref, w_ref, m_ref, v_ref, *rest):
        e_in, (g_ref, d_ref, m2_ref, v2_ref), e_out = rest[:ne], rest[ne:ne + 4], rest[ne + 4:2 * ne + 4]
        _hosted_comm("exchange", e_in, e_out, rest[2 * ne + 4:],
                     jnp.logical_and(pl.program_id(0) == 0, pl.program_id(1) == 0),
                     jnp.logical_and(pl.program_id(0) == DEPTH - 1, pl.program_id(1) == nb - 1))
        for l, ref in enumerate((s0_ref, s1_ref)):
            @pl.when(pl.program_id(0) == l)
            def _(ref=ref):
                acc = ref[0].astype(F32)
                for i in range(1, N_DEV):
                    acc = acc + ref[i].astype(F32)
                g_ref[...] = acc

        gg = g_ref[...]
        m2 = ADAM_B1 * m_ref[...] + (1.0 - ADAM_B1) * gg
        v2 = ADAM_B2 * v_ref[...] + (1.0 - ADAM_B2) * jnp.square(gg)
        m_hat = m2 / (1.0 - ADAM_B1 ** ADAM_STEP)
        v_hat = v2 / (1.0 - ADAM_B2 ** ADAM_STEP)
        d_ref[...] = -ADAM_LR * (m_hat / (jnp.sqrt(v_hat) + ADAM_EPS) + ADAM_WD * w_ref[...])
        m2_ref[...] = m2
        v2_ref[...] = v2

    slot_spec = lambda layer: pl.BlockSpec((N_DEV, tr, cols), lambda l, i: (0, jnp.where(l == layer, i, (nb - 1) * (1 - layer)), 0))
    spec = pl.BlockSpec((None, tr, cols), lambda l, i: (l, i, 0))
    outs = pl.pallas_call(
        body, name=name, grid=(DEPTH, nb),
        in_specs=[slot_spec(0), slot_spec(1), spec, spec, spec] + [_ANY] * ne, out_specs=[spec] * 4 + [_ANY] * ne,
        out_shape=[jax.ShapeDtypeStruct(w.shape, F32)] * 4 + _comm_out_shapes("exchange", exchange),
        scratch_shapes=_comm_scratch(ne) if ne else [],
        compiler_params=_params(),
    )(slots[0], slots[1], w, m, v, *exchange)
    return outs[0], outs[1], outs[2], outs[3], list(outs[4:])


_MESH = pl.DeviceIdType.MESH
_ANY = pl.BlockSpec(memory_space=pl.ANY)


def _my_place():
    return lax.axis_index("x"), lax.axis_index("y"), lax.axis_index("c")


def _flip(place, k):
    x, y, c = place
    return (1 - x if k & 4 else x, 1 - y if k & 2 else y, 1 - c if k & 1 else c)


def _block_id(place):
    return 4 * place[0] + 2 * place[1] + place[2]


def _peer_copies(kind, in_refs, out_refs, send_sems, recv_sems, local_sems):
    me = _my_place()
    my = _block_id(me)
    remote, local = [], []
    for a, (x_ref, out_ref) in enumerate(zip(in_refs, out_refs)):
        src_of = (lambda place, r=x_ref: r) if kind == "gather" else (lambda place, r=x_ref: r.at[_block_id(place)])
        local.append(pltpu.make_async_copy(src_of(me), out_ref.at[my], local_sems.at[a]))
        for k in range(1, N_DEV):
            peer = _flip(me, k)
            remote.append(pltpu.make_async_remote_copy(
                src_ref=src_of(peer), dst_ref=out_ref.at[my], send_sem=send_sems.at[a * 7 + k - 1],
                recv_sem=recv_sems.at[a * 7 + k - 1], device_id=peer, device_id_type=_MESH))
    return remote, local


def _comm_out_shapes(kind, arrays):
    return [jax.ShapeDtypeStruct((N_DEV, *a.shape) if kind == "gather" else a.shape, a.dtype) for a in arrays]


def _comm_scratch(n):
    return [pltpu.SemaphoreType.DMA((7 * n,)), pltpu.SemaphoreType.DMA((7 * n,)), pltpu.SemaphoreType.DMA((n,))]


def _hosted_comm(kind, in_refs, out_refs, sems, first, last):
    if not in_refs:
        return

    @pl.when(first)
    def _():
        remote, local = _peer_copies(kind, in_refs, out_refs, *sems)
        for cp in local + remote:
            cp.start()

    @pl.when(last)
    def _():
        remote, local = _peer_copies(kind, in_refs, out_refs, *sems)
        for cp in remote:
            cp.wait()
        for cp in local:
            cp.wait()


def _two_level_gather_steps(in_refs, out_refs, send_sems, recv_sems, local_sems):
    n = len(in_refs)
    me = _my_place()
    x, y, c = me
    sibling = (x, y, 1 - c)
    chips = [(1 - x, y), (x, 1 - y), (1 - x, 1 - y)]

    def copy(a, k, place, to, src=None):
        block = out_refs[a].at[_block_id(place)]
        return pltpu.make_async_remote_copy(
            src_ref=block if src is None else src, dst_ref=block, send_sem=send_sems.at[7 * a + k],
            recv_sem=recv_sems.at[7 * a + k], device_id=to, device_id_type=_MESH)

    mine = [pltpu.make_async_copy(in_refs[a], out_refs[a].at[_block_id(me)], local_sems.at[a]) for a in range(n)]
    first = [copy(a, 0, me, sibling, src=in_refs[a]) for a in range(n)]
    first += [copy(a, 1 + j, me, (*chip, c), src=in_refs[a]) for a in range(n) for j, chip in enumerate(chips)]
    passed = [copy(a, 4 + j, (*chip, c), sibling) for a in range(n) for j, chip in enumerate(chips)]

    def send():
        for cp in mine + first:
            cp.start()

    def forward():
        for a in range(n):
            for j, chip in enumerate(chips):
                copy(a, 1 + j, (*chip, c), me).wait_recv()
                passed[3 * a + j].start()

    def finish():
        for a in range(n):
            copy(a, 0, sibling, me).wait_recv()
            for j, chip in enumerate(chips):
                copy(a, 4 + j, (*chip, 1 - c), me).wait_recv()
        for cp in first + passed:
            cp.wait_send()
        for cp in mine:
            cp.wait()

    return send, forward, finish


def _gather_two_level(arrays, name):
    n = len(arrays)

    def body(*refs):
        for step in _two_level_gather_steps(refs[:n], refs[n:2 * n], *refs[2 * n:]):
            step()

    return pl.pallas_call(
        body, name=name, out_shape=_comm_out_shapes("gather", arrays),
        in_specs=[_ANY] * n, out_specs=[_ANY] * n, scratch_shapes=_comm_scratch(n),
    )(*arrays)


def _hosted_gather(in_refs, out_refs, sems, first, middle, last):
    if not in_refs:
        return
    for when, index in ((first, 0), (middle, 1), (last, 2)):
        @pl.when(when)
        def _(index=index):
            _two_level_gather_steps(in_refs, out_refs, *sems)[index]()


def _comm(kind, arrays, name):
    n = len(arrays)

    def body(*refs):
        remote, local = _peer_copies(kind, refs[:n], refs[n:2 * n], *refs[2 * n:])
        for cp in local + remote:
            cp.start()
        for cp in remote:
            cp.wait()
        for cp in local:
            cp.wait()

    return pl.pallas_call(
        body, name=name, out_shape=_comm_out_shapes(kind, arrays),
        in_specs=[_ANY] * n, out_specs=[_ANY] * n, scratch_shapes=_comm_scratch(n),
    )(*arrays)


def _all_reduce_small(part):
    rows, lanes = part.shape
    vmem = pl.BlockSpec(memory_space=pltpu.VMEM)

    def body(x_ref, gath_ref, sum_ref, send_sems, recv_sems):
        me = _my_place()
        my = _block_id(me)
        gath_ref[my] = x_ref[...]
        copies = []
        for k in range(1, N_DEV):
            cp = pltpu.make_async_remote_copy(
                src_ref=x_ref, dst_ref=gath_ref.at[my], send_sem=send_sems.at[k - 1], recv_sem=recv_sems.at[k - 1],
                device_id=_flip(me, k), device_id_type=_MESH)
            cp.start()
            copies.append(cp)
        for cp in copies:
            cp.wait()
        acc = gath_ref[0]
        for i in range(1, N_DEV):
            acc = acc + gath_ref[i]
        sum_ref[...] = acc

    return pl.pallas_call(
        body, name="small_grad_all_reduce",
        out_shape=[jax.ShapeDtypeStruct((N_DEV, rows, lanes), F32), jax.ShapeDtypeStruct((rows, lanes), F32)],
        in_specs=[vmem], out_specs=[vmem, vmem],
        scratch_shapes=[pltpu.SemaphoreType.DMA((7,)), pltpu.SemaphoreType.DMA((7,))],
    )(part)[1]


_SHARDED = (("w_in", (D_MODEL, IN_PROJ // N_DEV)), ("w_uq", (Q_RANK // N_DEV, Q_RANK)), ("w_ukv", (KV_RANK, HEAD_PAD)),
            ("conv_w", (CONV_W, CONV_DIM // N_DEV)), ("w_out", (D_MODEL // N_DEV, D_MODEL)),
            ("w_up", (D_MODEL, D_FF // N_DEV)), ("w_down", (D_FF // N_DEV, D_MODEL)))
_SMALL = (("pre_mix_norm", D_MODEL), ("q_norm", Q_RANK), ("kv_norm", KV_RANK), ("conv_b", CONV_DIM), ("dt_bias", SSD_HEADS),
          ("a_log", SSD_HEADS), ("d_skip", SSD_HEADS), ("ssd_norm", SSD_INNER), ("post_mix_norm", D_MODEL),
          ("pre_mlp_norm", D_MODEL), ("post_mlp_norm", D_MODEL))
_WEIGHT_ORDER = ("pre_mix_norm", "w_in", "q_norm", "w_uq", "kv_norm", "w_ukv", "conv_w", "conv_b", "dt_bias", "a_log", "d_skip",
                 "ssd_norm", "w_out", "post_mix_norm", "pre_mlp_norm", "w_up", "w_down", "post_mlp_norm")
_EARLY = ("w_in", "w_uq", "w_ukv", "conv_w")
_LATE = ("w_out", "w_up", "w_down")


def _wire_shard(name, a):
    return lax.bitcast_convert_type(a, BF16).reshape(CONV_W, -1) if name == "conv_w" else a.astype(BF16)


def _from_wire(name, g):
    return lax.bitcast_convert_type(g.reshape(N_DEV, CONV_W, -1, 2), F32) if name == "conv_w" else g


def _cols(stacked):
    return jnp.transpose(stacked, (1, 0, 2)).reshape(stacked.shape[1], -1)


def _win_segments():
    s2, s3, s5 = Q_RANK + KV_RANK, Q_RANK + KV_RANK + QK_ROPE, IN_PROJ - SSD_HEADS
    return [(0, s2), (None, MISC_ROPE), (s2, s3), (s5, IN_PROJ), (None, HEAD_PAD - MISC_DT - SSD_HEADS), (s3, s5)]


def _win_from_shards(stacked):
    per = IN_PROJ // N_DEV
    parts = []
    for start, stop in _win_segments():
        if start is None:
            parts.append(jnp.zeros((D_MODEL, stop), stacked.dtype))
            continue
        while start < stop:
            j, a = divmod(start, per)
            b = min(per, a + stop - start)
            parts.append(stacked[j, :, a:b])
            start += b - a
    return jnp.concatenate(parts, axis=1)


def _win_grad_shards(dwin):
    per = IN_PROJ // N_DEV
    runs, at = [], 0
    for start, stop in _win_segments():
        if start is not None:
            runs.append((start, stop, at))
        at += stop if start is None else stop - start
    blocks = []
    for j in range(N_DEV):
        lo, hi = j * per, (j + 1) * per
        parts = [dwin[:, p + max(lo, a) - a:p + min(hi, b) - a] for a, b, p in sorted(runs) if max(lo, a) < min(hi, b)]
        blocks.append(jnp.concatenate(parts, axis=1))
    return jnp.stack(blocks)


def _early_weights(sh):
    win = _win_from_shards(sh["w_in"])
    w_uq = sh["w_uq"].reshape(Q_RANK, MLA_HEADS, QK_NOPE + QK_ROPE)
    wuq = jnp.pad(w_uq, ((0, 0), (0, 0), (0, HEAD_PAD - QK_NOPE - QK_ROPE))).reshape(Q_RANK, -1)
    w_ukv = _cols(sh["w_ukv"]).reshape(KV_RANK, MLA_HEADS, QK_NOPE + V_DIM)
    wkn = jnp.pad(w_ukv[..., :QK_NOPE], ((0, 0), (0, 0), (0, HEAD_PAD - QK_NOPE))).reshape(KV_RANK, -1)
    wv = w_ukv[..., QK_NOPE:].reshape(KV_RANK, 4, 2, 1, V_DIM) * jnp.eye(2, dtype=BF16).reshape(1, 1, 2, 2, 1)
    wkv = jnp.concatenate([wkn, wv.reshape(KV_RANK, -1)], axis=1)
    return dict(win=win, wuq=wuq, wkv=wkv, conv_w=_cols(sh["conv_w"]))


def _late_weights(sh):
    w_out = sh["w_out"].reshape(D_MODEL, D_MODEL)
    watt = w_out[:SSD_INNER].reshape(4, 2, 1, V_DIM, D_MODEL) * jnp.eye(2, dtype=BF16).reshape(1, 2, 2, 1, 1)
    wout = jnp.concatenate([watt.reshape(MLA_HEADS * HEAD_PAD, D_MODEL), w_out[SSD_INNER:]], axis=0)
    return dict(wout=wout, wup=sh["w_up"], wdown=sh["w_down"])


def _shard_grads(g):
    out = {}
    if "wup" in g:
        out["w_up"], out["w_down"] = g["wup"], g["wdown"]
        ae = g["wout_att"].reshape(4, 2, 2, V_DIM, D_MODEL)
        att = jnp.stack([ae[:, 0, 0], ae[:, 1, 1]], axis=1).reshape(SSD_INNER, D_MODEL)
        out["w_out"] = jnp.concatenate([att, g["wout_ssd"]], axis=0).astype(BF16).reshape(N_DEV, D_MODEL // N_DEV, D_MODEL)
    if "win" not in g:
        return out
    out["w_in"] = _win_grad_shards(g["win"].astype(BF16))
    w_uq = g["wuq"].astype(BF16).reshape(Q_RANK, MLA_HEADS, HEAD_PAD)[..., :QK_NOPE + QK_ROPE].reshape(Q_RANK, Q_RANK)
    out["w_uq"] = w_uq.reshape(N_DEV, Q_RANK // N_DEV, Q_RANK)
    wide = MLA_HEADS * HEAD_PAD
    wkv = g["wkv"].astype(BF16)
    kn = wkv[:, :wide].reshape(KV_RANK, MLA_HEADS, HEAD_PAD)[..., :QK_NOPE]
    ve = wkv[:, wide:].reshape(KV_RANK, 4, 2, 2, V_DIM)
    vv = jnp.stack([ve[:, :, 0, 0], ve[:, :, 1, 1]], axis=2).reshape(KV_RANK, MLA_HEADS, V_DIM)
    out["w_ukv"] = jnp.transpose(jnp.concatenate([kn, vv], axis=-1), (1, 0, 2))
    out["conv_w"] = jnp.transpose(g["conv_w"].astype(BF16).reshape(CONV_W, N_DEV, -1), (1, 0, 2))
    return out


def _small_rows(n):
    return -(-n // 1024) * 8


def _pack_small(vals):
    rows = []
    for l in range(DEPTH):
        for name, n in _SMALL:
            r = _small_rows(n)
            rows.append(jnp.pad(vals[name][l].reshape(-1), (0, r * 128 - n)).reshape(r, 128))
    return jnp.concatenate(rows, axis=0)


def _unpack_small(packed):
    out, off = {name: [] for name, _ in _SMALL}, 0
    for l in range(DEPTH):
        for name, n in _SMALL:
            r = _small_rows(n)
            out[name].append(packed[off:off + r].reshape(-1)[:n])
            off += r
    return {name: jnp.stack(v) for name, v in out.items()}


def _lane_rows(vec8):
    return jnp.repeat(vec8, SSD_P).reshape(1, SSD_INNER)


def _layer_fwd(h, kw, sm, l, cosf, sinf, consts, gather=(), after_gather=None, target=None):
    row = lambda name: sm[name][l].reshape(1, -1)
    t = {}
    t["h0"] = h
    (t["ub"], t["cq"], t["ckv"], t["misc"], t["z"], t["xraw"], t["cqn"], t["ckvn"], t["q"], t["k"], t["v"]) = _inproj_qkv_fwd(
        h, row("pre_mix_norm"), kw["win"], row("q_norm"), row("kv_norm"), kw["wuq"], kw["wkv"], cosf, sinf)
    t["oe"], t["lse"], gathered = _attn_fwd(t["q"], t["k"], t["v"], gather)
    if after_gather is not None:
        after_gather(gathered)
    t["dtb"] = _lane_rows(sm["dt_bias"][l])
    t["a_exp"] = _lane_rows(-jnp.exp(sm["a_log"][l]))
    t["d_exp"] = _lane_rows(sm["d_skip"][l])
    t["c"], t["prev"], t["ypre"], t["yssd"], t["mixed"], t["h1"] = _ssd_outproj_fwd(
        t["xraw"], t["misc"], t["z"], kw["conv_w"], row("conv_b"), t["dtb"], t["a_exp"], t["d_exp"], row("ssd_norm"), consts,
        t["oe"], h, kw["wout"], row("post_mix_norm"))
    t["mb"], t["ab"], t["d"], *out = _mlp_fwd(t["h1"], row("pre_mlp_norm"), kw["wup"], kw["wdown"], row("post_mlp_norm"), target)
    return out, t


def _layer_bwd(dh2, t, kw, sm, l, cosf, sinf, consts, exchange_of=None):
    row = lambda name: sm[name][l].reshape(1, -1)
    g, gs = {}, {}
    dh1, dab, ddb, gs["post_mlp_norm"], gs["pre_mlp_norm"] = _mlp_bwd(
        dh2, t["d"], t["h1"], t["ab"], row("pre_mlp_norm"), kw["wup"], kw["wdown"], row("post_mlp_norm"))
    g["wup"] = _matmul_tn_stacked(t["mb"], dab, f"dw_up_{l}", a_stacked=False)
    g["wdown"] = _matmul_tn_stacked(t["ab"], ddb, f"dw_down_{l}", a_stacked=True, square_a=True)
    dmixb, doe, dyssd, gs["post_mix_norm"], delta = _outproj_bwd(dh1, t["mixed"], row("post_mix_norm"), kw["wout"], t["oe"])
    g["wout_att"] = _matmul_tn(t["oe"], dmixb, f"dw_out_att_{l}")
    g["wout_ssd"] = _matmul_tn(t["yssd"], dmixb, f"dw_out_ssd_{l}")
    dz, dxraw, dmisc_dt, gs["ssd_norm"], gd, galog, gdtb, g["conv_w"], gs["conv_b"] = _ssd_bwd(
        dyssd, t["ypre"], t["z"], t["c"], t["xraw"], t["misc"], t["prev"], kw["conv_w"], t["dtb"], t["a_exp"], t["d_exp"],
        row("ssd_norm"), consts)
    gs["d_skip"] = jnp.sum(gd.reshape(SSD_HEADS, SSD_P), axis=1)
    gs["a_log"] = galog[0, MISC_DT:MISC_DT + SSD_HEADS]
    gs["dt_bias"] = gdtb[0, MISC_DT:MISC_DT + SSD_HEADS]
    dq, dk, dv, exchanged = _attn_bwd(t["q"], t["k"], t["v"], doe, t["lse"], delta,
                                      exchange_of(g) if exchange_of is not None else ())
    dqb, dkvb, dprojb, dh0, gs["q_norm"], gs["kv_norm"], gs["pre_mix_norm"] = _qkv_inproj_bwd(
        dq, dk, dv, t["cq"], t["ckv"], dmisc_dt, dz, dxraw, t["h0"], dh1, row("q_norm"), row("kv_norm"),
        row("pre_mix_norm"), kw["wuq"], kw["wkv"], kw["win"], cosf, sinf)
    g["wuq"] = _matmul_tn(t["cqn"], dqb, f"dw_uq_{l}")
    g["wkv"] = _matmul_tn(t["ckvn"], dkvb, f"dw_kv_{l}")
    g["win"] = _matmul_tn(t["ub"], dprojb, f"dw_in_{l}")
    return dh0, g, {k: v.reshape(-1) for k, v in gs.items()}, exchanged


def _local_step(x, positions, kws, sm, target, gather=(), after_gather=None, exchange_of=None):
    inv_freq = ROPE_THETA ** (-jnp.arange(0, QK_ROPE, 2, dtype=F32) / QK_ROPE)
    invf = jnp.zeros((HEAD_PAD,), F32).at[MISC_ROPE:MISC_ROPE + QK_ROPE].set(jnp.concatenate([inv_freq, inv_freq]))
    cosf, sinf = _rope_tables(positions.reshape(-1, 1), invf.reshape(1, HEAD_PAD))
    consts = _ssd_consts()
    (h,), t0 = _layer_fwd(x, kws[0], sm, 0, cosf, sinf, consts, gather, after_gather)
    (dh, loss), t1 = _layer_fwd(h, kws[1], sm, 1, cosf, sinf, consts, target=target)
    saved = [t0, t1]
    grads, small, exchanged = [None] * DEPTH, [None] * DEPTH, []
    for l in reversed(range(DEPTH)):
        hook = (lambda g0: exchange_of(g0, grads[1])) if (l == 0 and exchange_of is not None) else None
        dh, grads[l], small[l], got = _layer_bwd(dh, saved[l], kws[l], sm, l, cosf, sinf, consts, hook)
        exchanged = got or exchanged
    return loss[0, 0], dh, grads, small, exchanged


def kernel(x, positions, pre_mix_norm, w_in, q_norm, w_uq, kv_norm, w_ukv, conv_w, conv_b, dt_bias, a_log, d_skip, ssd_norm, w_out, post_mix_norm, pre_mlp_norm, w_up, w_down, post_mlp_norm, loss_target, m_pre_mix_norm, m_w_in, m_q_norm, m_w_uq, m_kv_norm, m_w_ukv, m_conv_w, m_conv_b, m_dt_bias, m_a_log, m_d_skip, m_ssd_norm, m_w_out, m_post_mix_norm, m_pre_mlp_norm, m_w_up, m_w_down, m_post_mlp_norm, v_pre_mix_norm, v_w_in, v_q_norm, v_w_uq, v_kv_norm, v_w_ukv, v_conv_w, v_conv_b, v_dt_bias, v_a_log, v_d_skip, v_ssd_norm, v_w_out, v_post_mix_norm, v_pre_mlp_norm, v_w_up, v_w_down, v_post_mlp_norm):
    w = dict(pre_mix_norm=pre_mix_norm, w_in=w_in, q_norm=q_norm, w_uq=w_uq, kv_norm=kv_norm, w_ukv=w_ukv, conv_w=conv_w,
             conv_b=conv_b, dt_bias=dt_bias, a_log=a_log, d_skip=d_skip, ssd_norm=ssd_norm, w_out=w_out,
             post_mix_norm=post_mix_norm, pre_mlp_norm=pre_mlp_norm, w_up=w_up, w_down=w_down, post_mlp_norm=post_mlp_norm)
    m = dict(pre_mix_norm=m_pre_mix_norm, w_in=m_w_in, q_norm=m_q_norm, w_uq=m_w_uq, kv_norm=m_kv_norm, w_ukv=m_w_ukv,
             conv_w=m_conv_w, conv_b=m_conv_b, dt_bias=m_dt_bias, a_log=m_a_log, d_skip=m_d_skip, ssd_norm=m_ssd_norm,
             w_out=m_w_out, post_mix_norm=m_post_mix_norm, pre_mlp_norm=m_pre_mlp_norm, w_up=m_w_up, w_down=m_w_down,
             post_mlp_norm=m_post_mlp_norm)
    v = dict(pre_mix_norm=v_pre_mix_norm, w_in=v_w_in, q_norm=v_q_norm, w_uq=v_w_uq, kv_norm=v_kv_norm, w_ukv=v_w_ukv,
             conv_w=v_conv_w, conv_b=v_conv_b, dt_bias=v_dt_bias, a_log=v_a_log, d_skip=v_d_skip, ssd_norm=v_ssd_norm,
             w_out=v_w_out, post_mix_norm=v_post_mix_norm, pre_mlp_norm=v_pre_mlp_norm, w_up=v_w_up, w_down=v_w_down,
             post_mlp_norm=v_post_mlp_norm)
    sm = {name: w[name] for name, _ in _SMALL}

    wire = lambda name, l: _wire_shard(name, w[name][l])
    first = _gather_two_level([wire(name, 0) for name in _EARLY], "weight_gather_first")
    kws = [_early_weights({name: _from_wire(name, a) for name, a in zip(_EARLY, first)}), None]
    behind = [(name, 0) for name in _LATE] + [(name, 1) for name, _ in _SHARDED]

    def after_gather(gathered):
        got = {key: _from_wire(key[0], a) for key, a in zip(behind, gathered)}
        kws[0].update(_late_weights({name: got[name, 0] for name in _LATE}))
        kws[1] = {**_early_weights({name: got[name, 1] for name in _EARLY}),
                  **_late_weights({name: got[name, 1] for name in _LATE})}

    sent_behind = [(name, 1) for name, _ in _SHARDED] + [(name, 0) for name in _LATE]

    def exchange_of(g0, g1):
        blocks = {**{(name, 1): a for name, a in _shard_grads(g1).items()},
                  **{(name, 0): a for name, a in _shard_grads(g0).items()}}
        return [blocks[key] for key in sent_behind]

    loss_part, dx, grads, small, exchanged = _local_step(
        x[0], positions[0], kws, sm, loss_target[0], [wire(*key) for key in behind], after_gather, exchange_of)
    slots = dict(zip(sent_behind, exchanged))
    last = _shard_grads({k: grads[0][k] for k in ("win", "wuq", "wkv", "conv_w")})
    grad, delta, new_m, new_v = {}, {}, {}, {}

    def update(name, exchange=()):
        grad[name], delta[name], new_m[name], new_v[name], got = _sum_adamw(
            [slots[name, 0], slots[name, 1]], w[name], m[name], v[name], f"sum_adamw_{name}", exchange)
        return got

    slots.update({(name, 0): a for name, a in zip(_EARLY, update("w_up", [last[name] for name in _EARLY]))})
    g_small = _unpack_small(_all_reduce_small(_pack_small({name: jnp.stack([small[l][name] for l in range(DEPTH)])
                                                           for name, _ in _SMALL})))
    loss = lax.psum(loss_part, ("x", "y", "c"))
    for name, _ in _SHARDED:
        if name != "w_up":
            update(name)
    pk = lambda d: _pack_small({name: d[name] for name, _ in _SMALL})
    d_, m_, v_ = _adamw(pk(w), pk(g_small), pk(m), pk(v), "adamw_small")
    for dst, packed in ((delta, d_), (new_m, m_), (new_v, v_)):
        dst.update(_unpack_small(packed))
    grad.update(g_small)

    outs = [loss, dx[None]]
    for d in (grad, delta, new_m, new_v):
        outs += [d[name] for name in _WEIGHT_ORDER]
    return tuple(outs)
```

```python
import jax
import jax.numpy as jnp
import numpy as np
from jax import lax
from jax.experimental import pallas as pl
from jax.experimental.pallas import tpu as pltpu

F32 = jnp.float32
BF16 = jnp.bfloat16

D_MODEL = 1024
DEPTH = 2
N_DEV = 8
CHUNK = 64
EPS = 1e-6
MLA_HEADS = 8
QK_NOPE = 64
QK_ROPE = 32
V_DIM = 64
Q_RANK = 768
KV_RANK = 256
ROPE_THETA = 10000.0
SSD_HEADS = 8
SSD_P = 64
SSD_INNER = 512
SSD_GROUPS = 2
SSD_N = 128
CONV_W = 4
CONV_DIM = 1024
D_FF = 4096
IN_PROJ = 2600
HEAD_PAD = 128
IN_PAD = 2688
MISC_ROPE = 64
MISC_DT = 96
ATT_SCALE = (QK_NOPE + QK_ROPE) ** -0.5
LOG2E = 1.4426950408889634
ATT_SCALE_LOG2 = ATT_SCALE * LOG2E

ADAM_LR = 0.001
ADAM_B1 = 0.9
ADAM_B2 = 0.999
ADAM_EPS = 1e-08
ADAM_WD = 0.01
ADAM_STEP = 10

TM = 512
ATT_T = 512
ATT_G = 8
ATT_UNROLL = 4
SSD_ROWS = 512
TK_DW = 4096
V7X_VMEM_BYTES = 64 * 1024 * 1024
VMEM_LIMIT = V7X_VMEM_BYTES - 8 * 1024 * 1024

_NT = (((1,), (1,)), ((), ()))
_TN = (((0,), (0,)), ((), ()))


def _params(**kw):
    return pltpu.CompilerParams(vmem_limit_bytes=VMEM_LIMIT, **kw)


def _dot(a, b, precision=None):
    return jnp.dot(a, b, preferred_element_type=F32, precision=precision)


def _dot_nt(a, b, precision=None):
    return lax.dot_general(a, b, _NT, preferred_element_type=F32, precision=precision)


def _dot_tn(a, b, precision=None):
    return lax.dot_general(a, b, _TN, preferred_element_type=F32, precision=precision)


def _split3(x):
    hi = x.astype(BF16)
    r = x - hi.astype(F32)
    mid = r.astype(BF16)
    return hi, mid, (r - mid.astype(F32)).astype(BF16)


def _dot01(x, m01, dot=_dot, left=False):
    parts = [dot(m01, p) if left else dot(p, m01) for p in _split3(x)]
    return parts[0] + parts[1] + parts[2]


def _full(shape):
    n = len(shape)
    return pl.BlockSpec(shape, lambda *_: (0,) * n)


def _resident(shape):
    n = len(shape)
    return pl.BlockSpec(shape, lambda *_: (0,) * n, pipeline_mode=pl.Buffered(1))


def _rows(tm, width):
    return pl.BlockSpec((tm, width), lambda i: (i, 0))


def _rms_fwd(x, w):
    r = lax.rsqrt(jnp.mean(x * x, axis=-1, keepdims=True) + EPS)
    return (x * r) * w


def _rms_bwd(x, w, dy):
    r = lax.rsqrt(jnp.mean(x * x, axis=-1, keepdims=True) + EPS)
    xh = x * r
    dxn = dy * w
    dx = r * (dxn - xh * jnp.mean(dxn * xh, axis=-1, keepdims=True))
    return dx, dy * xh


def _acc_rows(ref, val, first):
    s = jnp.sum(val, axis=0, keepdims=True)

    @pl.when(first)
    def _():
        ref[...] = s

    @pl.when(jnp.logical_not(first))
    def _():
        ref[...] += s


def _rope(t, cosf, sinf, sign):
    lane = lax.broadcasted_iota(jnp.int32, t.shape, 1)
    rot = jnp.where(lane < MISC_ROPE + QK_ROPE // 2, -pltpu.roll(t, HEAD_PAD - QK_ROPE // 2, 1), pltpu.roll(t, QK_ROPE // 2, 1))
    return t * cosf + sign * (rot * sinf)


def _rope_tables(pos, invf):
    s = pos.shape[0]

    def body(pos_ref, invf_ref, cos_ref, sin_ref):
        ang = pos_ref[...].astype(F32) * invf_ref[...]
        cos_ref[...] = jnp.cos(ang)
        sin_ref[...] = jnp.sin(ang)

    return pl.pallas_call(
        body, name="rope_tables", grid=(s // TM,),
        in_specs=[_rows(TM, 1), _full((1, HEAD_PAD))],
        out_specs=[_rows(TM, HEAD_PAD), _rows(TM, HEAD_PAD)],
        out_shape=[jax.ShapeDtypeStruct((s, HEAD_PAD), F32)] * 2,
    )(pos, invf)


def _inproj_qkv_fwd(h, nw, win, qnw, kvnw, wuq, wkv, cosf, sinf):
    s = h.shape[0]

    def body(h_ref, nw_ref, w_ref, qnw_ref, kvnw_ref, wuq_ref, wkv_ref, cos_ref, sin_ref,
             ub_ref, cq_ref, ckv_ref, misc_ref, z_ref, xbc_ref, cqn_ref, ckvn_ref, q_ref, k_ref, v_ref):
        ub = _rms_fwd(h_ref[...], nw_ref[...]).astype(BF16)
        ub_ref[...] = ub
        proj = _dot(ub, w_ref[...])
        cq, ckv, m = proj[:, 0:768], proj[:, 768:1024], proj[:, 1024:1152]
        cq_ref[...] = cq
        ckv_ref[...] = ckv
        misc_ref[...] = m
        z_ref[...] = proj[:, 1152:1664]
        xbc_ref[...] = proj[:, 1664:2688]
        cosf, sinf = cos_ref[...], sin_ref[...]
        cqn = _rms_fwd(cq, qnw_ref[...]).astype(BF16)
        cqn_ref[...] = cqn
        q = _dot(cqn, wuq_ref[...])
        ckvn = _rms_fwd(ckv, kvnw_ref[...]).astype(BF16)
        ckvn_ref[...] = ckvn
        kv = _dot(ckvn, wkv_ref[...])
        lane = lax.broadcasted_iota(jnp.int32, m.shape, 1)
        in_rope = jnp.logical_and(lane >= MISC_ROPE, lane < MISC_ROPE + QK_ROPE)
        kr = jnp.where(in_rope, _rope(m, cosf, sinf, 1.0), 0.0)
        for hd in range(MLA_HEADS):
            cols = slice(hd * HEAD_PAD, (hd + 1) * HEAD_PAD)
            q_ref[:, cols] = _rope(q[:, cols], cosf, sinf, 1.0).astype(BF16)
            k_ref[:, cols] = (kv[:, cols] + kr).astype(BF16)
        vv = kv[:, MLA_HEADS * HEAD_PAD:]
        vlane = lax.broadcasted_iota(jnp.int32, vv.shape, 1)
        ones_at = jnp.where((vlane // HEAD_PAD) % 2 == 0, V_DIM, 0)
        v_ref[...] = jnp.where(vlane % HEAD_PAD == ones_at, 1.0, vv).astype(BF16)

    wide = MLA_HEADS * HEAD_PAD
    widths = (Q_RANK, KV_RANK, HEAD_PAD, SSD_INNER, CONV_DIM)
    return pl.pallas_call(
        body, name="inproj_qkv_fwd", grid=(s // TM,),
        in_specs=[_rows(TM, D_MODEL), _full((1, D_MODEL)), _resident((D_MODEL, IN_PAD)), _full((1, Q_RANK)), _full((1, KV_RANK)),
                  _resident((Q_RANK, wide)), _resident((KV_RANK, 2 * wide)), _rows(TM, HEAD_PAD), _rows(TM, HEAD_PAD)],
        out_specs=[_rows(TM, D_MODEL)] + [_rows(TM, w) for w in widths]
        + [_rows(TM, Q_RANK), _rows(TM, KV_RANK), _rows(TM, wide), _rows(TM, wide), _rows(TM, wide)],
        out_shape=[jax.ShapeDtypeStruct((s, D_MODEL), BF16)] + [jax.ShapeDtypeStruct((s, w), F32) for w in widths]
        + [jax.ShapeDtypeStruct((s, Q_RANK), BF16), jax.ShapeDtypeStruct((s, KV_RANK), BF16)]
        + [jax.ShapeDtypeStruct((s, wide), BF16)] * 3,
        compiler_params=_params(),
    )(h, nw, win, qnw, kvnw, wuq, wkv, cosf, sinf)


def _chunk_bias(t, keys_on_rows=False):
    row = lax.broadcasted_iota(jnp.int32, (t, 1), 0) // CHUNK
    col = lax.broadcasted_iota(jnp.int32, (1, t), 1) // CHUNK
    return jnp.where((row <= col) if keys_on_rows else (col <= row), 0.0, -jnp.inf).astype(F32)


def _attn_fwd(q, k, v, gather=()):
    s = q.shape[0]
    t = ATT_T
    nq = s // t
    pair = ATT_G * HEAD_PAD
    ng = len(gather)

    def body(q_ref, k_ref, v_ref, *rest):
        g_in, (o_ref, lse_ref), g_out = rest[:ng], rest[ng:ng + 2], rest[ng + 2:2 * ng + 2]
        m_s, acc_s, bias_s = rest[2 * ng + 2:2 * ng + 5]
        qi = pl.program_id(1)
        group, groups = pl.program_id(0), MLA_HEADS // ATT_G

        @pl.when(jnp.logical_and(group == 0, qi == 0))
        def _():
            bias_s[...] = _chunk_bias(t)

        _hosted_gather(g_in, g_out, rest[2 * ng + 5:],
                       jnp.logical_and(group == 0, qi == 0),
                       jnp.logical_and(group == groups - 1, qi == min(3 * nq // 4 + 1, nq - 1)),
                       jnp.logical_and(group == groups - 1, qi == nq - 1))
        m_s[...] = jnp.full(m_s.shape, -jnp.inf, F32)
        acc_s[...] = jnp.zeros(acc_s.shape, F32)

        def step(kb, masked):
            r0 = pl.multiple_of(kb * t, t)

            def scores(hh):
                cols = slice(hh * HEAD_PAD, (hh + 1) * HEAD_PAD)
                return _dot_nt(q_ref[:, cols], k_ref[pl.ds(r0, t), cols])

            def soft(hh, raw):
                sc = raw * ATT_SCALE_LOG2
                if masked:
                    sc = sc + bias_s[...]
                m_old = m_s[hh]
                m_new = jnp.maximum(m_old, jnp.max(sc, axis=-1, keepdims=True))
                alpha = jnp.exp2(m_old - m_new)
                p = jnp.exp2(sc - jnp.tile(m_new, (1, t // HEAD_PAD)))
                m_s[hh] = m_new
                return alpha, p.astype(BF16)

            def update(hh, alpha, p):
                cols = slice(hh * HEAD_PAD, (hh + 1) * HEAD_PAD)
                acc_s[hh] = alpha * acc_s[hh] + _dot(p, v_ref[pl.ds(r0, t), cols])

            raw, ap = [None] * ATT_G, [None] * ATT_G
            raw[0] = scores(0)
            for hh in range(ATT_G):
                if hh + 1 < ATT_G:
                    raw[hh + 1] = scores(hh + 1)
                ap[hh] = soft(hh, raw[hh])
                if hh >= 1:
                    update(hh - 1, *ap[hh - 1])
            update(ATT_G - 1, *ap[ATT_G - 1])

        def loop(i, c):
            step(2 * i, False)
            step(2 * i + 1, False)
            return c

        lax.fori_loop(0, qi // 2, loop, 0)

        @pl.when(qi % 2 == 1)
        def _():
            step(qi - 1, False)

        step(qi, True)
        for hh in range(ATT_G):
            cols = slice(hh * HEAD_PAD, (hh + 1) * HEAD_PAD)
            acc = acc_s[hh]
            ones_at = V_DIM * (1 - hh % 2)
            l = jnp.broadcast_to(acc[:, ones_at:ones_at + 1], acc.shape)
            o_ref[:, cols] = (acc / l).astype(BF16)
            lse_ref[hh] = (m_s[hh] + jnp.log(l) * LOG2E).T[0:8, :]

    outs = pl.pallas_call(
        body, name="attn_fwd_gather" if ng else "attn_fwd", grid=(MLA_HEADS // ATT_G, nq),
        in_specs=[pl.BlockSpec((t, pair), lambda h, i: (i, h)),
                  pl.BlockSpec((s, pair), lambda h, i: (0, h), pipeline_mode=pl.Buffered(1)),
                  pl.BlockSpec((s, pair), lambda h, i: (0, h), pipeline_mode=pl.Buffered(1))] + [_ANY] * ng,
        out_specs=[pl.BlockSpec((t, pair), lambda h, i: (i, h)),
                   pl.BlockSpec((ATT_G, 8, t), lambda h, i: (h, 0, i))] + [_ANY] * ng,
        out_shape=[jax.ShapeDtypeStruct((s, MLA_HEADS * HEAD_PAD), BF16), jax.ShapeDtypeStruct((MLA_HEADS, 8, s), F32)]
        + _comm_out_shapes("gather", gather),
        scratch_shapes=[pltpu.VMEM((ATT_G, t, HEAD_PAD), F32), pltpu.VMEM((ATT_G, t, HEAD_PAD), F32), pltpu.VMEM((t, t), F32)]
        + (_comm_scratch(ng) if ng else []),
        compiler_params=_params(),
    )(q, k, v, *gather)
    return outs[0], outs[1], list(outs[2:])


def _interleave(stages):
    live = list(stages)
    while live:
        still = []
        for g in live:
            try:
                next(g)
                still.append(g)
            except StopIteration:
                pass
        live = still


def _ssd_consts():
    emisc = np.zeros((HEAD_PAD, SSD_INNER), np.float32)
    for hd in range(SSD_HEADS):
        emisc[MISC_DT + hd, hd * SSD_P:(hd + 1) * SSD_P] = 1.0
    idx = np.arange(CHUNK)
    tri = (idx[:, None] >= idx[None, :]).astype(np.float32)
    return tuple(jnp.asarray(m, BF16) for m in (emisc, emisc.T.copy(), tri, tri.T.copy()))


def _ssd_chunk_common(cc, misc, emisc, tri, trit, dtb, a_exp):
    sig = jax.nn.sigmoid(cc)
    xa = cc * sig
    dt = jax.nn.softplus(_dot01(misc, emisc) + dtb)
    a = dt * a_exp
    acs = _dot01(a, tri, left=True)
    acs_t = _dot01(a, trit, dot=_dot_tn)
    alast = acs[CHUNK - 1:CHUNK, :]
    return xa, sig, dt, acs, acs_t, alast


def _decay(acs, acs_t, hd):
    row = lax.broadcasted_iota(jnp.int32, (CHUNK, CHUNK), 0)
    col = lax.broadcasted_iota(jnp.int32, (CHUNK, CHUNK), 1)
    diff = acs[:, hd * SSD_P:hd * SSD_P + 1] - acs_t[hd * SSD_P:hd * SSD_P + 1, :]
    return jnp.exp(jnp.where(row >= col, diff, -jnp.inf))


def _half_mask(hh):
    lane = lax.broadcasted_iota(jnp.int32, (CHUNK, 2 * SSD_P), 1)
    return (lane >= SSD_P) if hh else (lane < SSD_P)


def _gate_norm(y, zz):
    sg = jax.nn.sigmoid(zz)
    yz = y * (zz * sg)
    outs, rs = [], []
    half = SSD_INNER // SSD_GROUPS
    for g in range(SSD_GROUPS):
        yg = yz[:, g * half:(g + 1) * half]
        r = lax.rsqrt(jnp.mean(yg * yg, axis=-1, keepdims=True) + EPS)
        outs.append(yg * r)
        rs.append(r)
    return sg, jnp.concatenate(outs, axis=1), rs


def _ssd_outproj_fwd(xraw, misc, z, cw, cb, dtb, a_exp, d_exp, nw, consts, oe, h, wout, post_w):
    s = xraw.shape[0]
    nb = s // SSD_ROWS
    ncb = SSD_ROWS // CHUNK
    emisc, _, tri, trit = consts
    wide = MLA_HEADS * HEAD_PAD

    def body(x_ref, misc_ref, z_ref, cw_ref, cb_ref, dtb_ref, a_ref, d_ref, nw_ref, emisc_ref, tri_ref, trit_ref,
             oe_ref, h_ref, wout_ref, postw_ref, c_ref, prev_ref, ypre_ref, yssd_ref, mixed_ref, h1_ref, tail_s, state_s):
        i = pl.program_id(0)

        @pl.when(i == 0)
        def _():
            tail_s[...] = jnp.zeros(tail_s.shape, F32)
            state_s[...] = jnp.zeros(state_s.shape, F32)

        mixed_att = _dot(oe_ref[...], wout_ref[0:wide, :])
        x = x_ref[...]
        xext = jnp.concatenate([tail_s[...], x], axis=0)
        acc = x * cw_ref[CONV_W - 1:CONV_W, :] + cb_ref[...]
        for j in range(1, CONV_W):
            acc = acc + pltpu.roll(xext, j, 0)[8:, :] * cw_ref[CONV_W - 1 - j:CONV_W - j, :]
        tail_s[...] = x[SSD_ROWS - 8:, :]
        c_ref[...] = acc

        def chunk(ci):
            r0 = ci * CHUNK
            xa, _, dt, acs, acs_t, alast = _ssd_chunk_common(
                c_ref[pl.ds(r0, CHUNK), :], misc_ref[pl.ds(r0, CHUNK), :], emisc_ref[...], tri_ref[...], trit_ref[...],
                dtb_ref[...], a_ref[...])
            yield
            xs = xa[:, :SSD_INNER]
            xdt = xs * dt
            wgt = (xdt * jnp.exp(alast - acs)).astype(BF16)
            e = jnp.exp(acs)
            ys, new_states, cms = [], [], []
            for g in range(SSD_GROUPS):
                bm = xa[:, SSD_INNER + g * SSD_N:SSD_INNER + (g + 1) * SSD_N].astype(BF16)
                cm = xa[:, SSD_INNER + SSD_GROUPS * SSD_N + g * SSD_N:SSD_INNER + SSD_GROUPS * SSD_N + (g + 1) * SSD_N].astype(BF16)
                cms.append(cm)
                cb_g = _dot_nt(cm, bm)
                gl = slice(g * 256, (g + 1) * 256)
                new_states.append(_dot_tn(bm, wgt[:, gl]))
                for jj in range(2):
                    pair = 2 * g + jj
                    xp = xdt[:, pair * 128:(pair + 1) * 128]
                    yp = None
                    for hh in range(2):
                        sc = (cb_g * _decay(acs, acs_t, 2 * pair + hh)).astype(BF16)
                        term = _dot(sc, jnp.where(_half_mask(hh), xp, 0.0).astype(BF16))
                        yp = term if yp is None else yp + term
                    ys.append(yp)
                yield
            prev = state_s[...]
            prev_ref[ci] = prev
            yoff = jnp.concatenate([_dot(cms[g], prev[:, g * 256:(g + 1) * 256].astype(BF16)) for g in range(SSD_GROUPS)],
                                   axis=1) * e
            state_s[...] = prev * jnp.exp(alast) + jnp.concatenate(new_states, axis=1)
            yield
            y = jnp.concatenate(ys, axis=1) + yoff + d_ref[...] * xs
            ypre_ref[pl.ds(r0, CHUNK), :] = y
            _, yn, _ = _gate_norm(y, z_ref[pl.ds(r0, CHUNK), :])
            yssd_ref[pl.ds(r0, CHUNK), :] = (yn * nw_ref[...]).astype(BF16)

        _interleave([chunk(ci) for ci in range(ncb)])
        mixed = mixed_att + _dot(yssd_ref[...], wout_ref[wide:, :])
        mixed_ref[...] = mixed
        h1_ref[...] = h_ref[...] + _rms_fwd(mixed, postw_ref[...])

    return pl.pallas_call(
        body, name="ssd_outproj_fwd", grid=(nb,),
        in_specs=[_rows(SSD_ROWS, CONV_DIM), _rows(SSD_ROWS, HEAD_PAD), _rows(SSD_ROWS, SSD_INNER),
                  _full((CONV_W, CONV_DIM)), _full((1, CONV_DIM)), _full((1, SSD_INNER)), _full((1, SSD_INNER)),
                  _full((1, SSD_INNER)), _full((1, SSD_INNER)), _full((HEAD_PAD, SSD_INNER)), _full((CHUNK, CHUNK)),
                  _full((CHUNK, CHUNK)), _rows(SSD_ROWS, wide), _rows(SSD_ROWS, D_MODEL),
                  _resident((wide + SSD_INNER, D_MODEL)), _full((1, D_MODEL))],
        out_specs=[_rows(SSD_ROWS, CONV_DIM), pl.BlockSpec((ncb, SSD_N, SSD_INNER), lambda i: (i, 0, 0)),
                   _rows(SSD_ROWS, SSD_INNER), _rows(SSD_ROWS, SSD_INNER), _rows(SSD_ROWS, D_MODEL), _rows(SSD_ROWS, D_MODEL)],
        out_shape=[jax.ShapeDtypeStruct((s, CONV_DIM), F32), jax.ShapeDtypeStruct((s // CHUNK, SSD_N, SSD_INNER), F32),
                   jax.ShapeDtypeStruct((s, SSD_INNER), F32), jax.ShapeDtypeStruct((s, SSD_INNER), BF16),
                   jax.ShapeDtypeStruct((s, D_MODEL), F32), jax.ShapeDtypeStruct((s, D_MODEL), F32)],
        scratch_shapes=[pltpu.VMEM((8, CONV_DIM), F32), pltpu.VMEM((SSD_N, SSD_INNER), F32)],
        compiler_params=_params(),
    )(xraw, misc, z, cw, cb, dtb, a_exp, d_exp, nw, emisc, tri, trit, oe, h, wout, post_w)


def _mlp_fwd(h1, prew, wup, wdown, postw, target=None):
    s = h1.shape[0]
    fb = D_FF // N_DEV
    last = target is not None

    def body(h_ref, prew_ref, up_ref, down_ref, postw_ref, *rest):
        target_ref, (mb_ref, ab_ref, d_ref, out_ref) = (rest[0] if last else None), rest[last:last + 4]
        hh = h_ref[...]
        mb = _rms_fwd(hh, prew_ref[...]).astype(BF16)
        mb_ref[...] = mb
        d = jnp.zeros((TM, D_MODEL), F32)
        for j in range(N_DEV):
            a = jnp.maximum(_dot(mb, up_ref[j]), 0.0)
            ab_ref[j] = a.astype(BF16)
            d = d + _dot(jnp.square(a).astype(BF16), down_ref[j])
        d_ref[...] = d
        h2 = hh + _rms_fwd(d, postw_ref[...])
        if last:
            diff = h2 - target_ref[...]
            out_ref[...] = diff * (1.0 / D_MODEL)
            part = 0.5 * jnp.sum(jnp.mean(diff * diff, axis=-1, keepdims=True), axis=0, keepdims=True)
            _acc_rows(rest[-1], part, pl.program_id(0) == 0)
        else:
            out_ref[...] = h2

    stacked = pl.BlockSpec((N_DEV, TM, fb), lambda i: (0, i, 0))
    return pl.pallas_call(
        body, name="mlp_fwd_loss" if last else "mlp_fwd", grid=(s // TM,),
        in_specs=[_rows(TM, D_MODEL), _full((1, D_MODEL)), _resident((N_DEV, D_MODEL, fb)), _resident((N_DEV, fb, D_MODEL)),
                  _full((1, D_MODEL))] + ([_rows(TM, D_MODEL)] if last else []),
        out_specs=[_rows(TM, D_MODEL), stacked, _rows(TM, D_MODEL), _rows(TM, D_MODEL)] + ([_full((1, 1))] if last else []),
        out_shape=[jax.ShapeDtypeStruct((s, D_MODEL), BF16), jax.ShapeDtypeStruct((N_DEV, s, fb), BF16),
                   jax.ShapeDtypeStruct((s, D_MODEL), F32), jax.ShapeDtypeStruct((s, D_MODEL), F32)]
        + ([jax.ShapeDtypeStruct((1, 1), F32)] if last else []),
        compiler_params=_params(),
    )(h1, prew, wup, wdown, postw, *([target] if last else []))


def _mlp_bwd(dh2, d, h1, ab, prew, wup, wdown, postw):
    s = dh2.shape[0]
    fb = D_FF // N_DEV
    tm = TM // 2

    def body(dh2_ref, d_ref, h1_ref, ab_ref, prew_ref, up_ref, down_ref, postw_ref,
             dh1_ref, da_ref, dd_ref, gpost_ref, gpre_ref):
        first = pl.program_id(0) == 0
        dh2 = dh2_ref[...]
        dd, gpost = _rms_bwd(d_ref[...], postw_ref[...], dh2)
        _acc_rows(gpost_ref, gpost, first)
        ddb = dd.astype(BF16)
        dd_ref[...] = ddb

        def d_relu_squared(j):
            return _dot_nt(ddb, down_ref[j])

        def pointwise(j, dr):
            da = (dr * (2.0 * ab_ref[j].astype(F32))).astype(BF16)
            da_ref[j] = da
            return da

        dm = jnp.zeros((tm, D_MODEL), F32)
        nxt, da_prev = d_relu_squared(0), None
        for j in range(N_DEV):
            cur = nxt
            if j + 1 < N_DEV:
                nxt = d_relu_squared(j + 1)
            da = pointwise(j, cur)
            if da_prev is not None:
                dm = dm + _dot_nt(da_prev, up_ref[j - 1])
            da_prev = da
        dm = dm + _dot_nt(da_prev, up_ref[N_DEV - 1])
        dx, gpre = _rms_bwd(h1_ref[...], prew_ref[...], dm)
        _acc_rows(gpre_ref, gpre, first)
        dh1_ref[...] = dh2 + dx

    stacked = pl.BlockSpec((N_DEV, tm, fb), lambda i: (0, i, 0))
    return pl.pallas_call(
        body, name="mlp_bwd", grid=(s // tm,),
        in_specs=[_rows(tm, D_MODEL)] * 3 + [stacked, _full((1, D_MODEL)), _resident((N_DEV, D_MODEL, fb)),
                                              _resident((N_DEV, fb, D_MODEL)), _full((1, D_MODEL))],
        out_specs=[_rows(tm, D_MODEL), stacked, _rows(tm, D_MODEL), _full((1, D_MODEL)), _full((1, D_MODEL))],
        out_shape=[jax.ShapeDtypeStruct((s, D_MODEL), F32), jax.ShapeDtypeStruct((N_DEV, s, fb), BF16),
                   jax.ShapeDtypeStruct((s, D_MODEL), BF16), jax.ShapeDtypeStruct((1, D_MODEL), F32),
                   jax.ShapeDtypeStruct((1, D_MODEL), F32)],
        compiler_params=_params(),
    )(dh2, d, h1, ab, prew, wup, wdown, postw)


def _matmul_tn(a, b, name, tk=TK_DW):
    s, m = a.shape
    n = b.shape[1]
    tn = n if n <= 1024 else (n // 2 if (n // 2) % 128 == 0 else n // 3)
    tk = min(tk, s)
    assert n % tn == 0 and tn % 128 == 0 and s % tk == 0

    def body(a_ref, b_ref, o_ref):
        part = _dot_tn(a_ref[...], b_ref[...])

        @pl.when(pl.program_id(1) == 0)
        def _():
            o_ref[...] = part

        @pl.when(pl.program_id(1) != 0)
        def _():
            o_ref[...] += part

    return pl.pallas_call(
        body, name=name, grid=(n // tn, s // tk),
        in_specs=[pl.BlockSpec((tk, m), lambda j, k: (k, 0)), pl.BlockSpec((tk, tn), lambda j, k: (k, j))],
        out_specs=pl.BlockSpec((m, tn), lambda j, k: (0, j)),
        out_shape=jax.ShapeDtypeStruct((m, n), F32),
        compiler_params=_params(),
    )(a, b)


def _matmul_tn_stacked(a, b, name, a_stacked, square_a=False, tk=TK_DW):
    tk = min(tk, a.shape[-2])
    if a_stacked:
        _, s, m = a.shape
        n = b.shape[1]
        in_specs = [pl.BlockSpec((1, tk, m), lambda j, k: (j, k, 0)), pl.BlockSpec((tk, n), lambda j, k: (k, 0))]
    else:
        s, m = a.shape
        n = b.shape[2]
        in_specs = [pl.BlockSpec((tk, m), lambda j, k: (k, 0)), pl.BlockSpec((1, tk, n), lambda j, k: (j, k, 0))]

    nk = s // tk

    def body(a_ref, b_ref, o_ref, acc_s):
        av = a_ref[0] if a_stacked else a_ref[...]
        bv = b_ref[...] if a_stacked else b_ref[0]
        if square_a:
            av = jnp.square(av.astype(F32)).astype(BF16)
        part = _dot_tn(av, bv)
        k = pl.program_id(1)

        @pl.when(k == 0)
        def _():
            acc_s[...] = part

        @pl.when(jnp.logical_and(k != 0, k != nk - 1))
        def _():
            acc_s[...] += part

        @pl.when(k == nk - 1)
        def _():
            o_ref[0] = (part if nk == 1 else acc_s[...] + part).astype(BF16)

    return pl.pallas_call(
        body, name=name, grid=(N_DEV, nk),
        in_specs=in_specs,
        out_specs=pl.BlockSpec((1, m, n), lambda j, k: (j, 0, 0)),
        out_shape=jax.ShapeDtypeStruct((N_DEV, m, n), BF16),
        scratch_shapes=[pltpu.VMEM((m, n), F32)],
        compiler_params=_params(),
    )(a, b)


def _outproj_bwd(dh1, mixed, nw, wout, oe):
    s = dh1.shape[0]
    wide = MLA_HEADS * HEAD_PAD

    def body(dh1_ref, mixed_ref, nw_ref, w_ref, oe_ref, dmix_ref, doe_ref, dy_ref, gnw_ref, delta_ref):
        dmix, gnw = _rms_bwd(mixed_ref[...], nw_ref[...], dh1_ref[...])
        _acc_rows(gnw_ref, gnw, pl.program_id(0) == 0)
        dmb = dmix.astype(BF16)
        dmix_ref[...] = dmb
        doe_ref[...] = _dot_nt(dmb, w_ref[0:wide, :]).astype(BF16)
        dy_ref[...] = _dot_nt(dmb, w_ref[wide:, :])
        ones = jnp.ones((8, HEAD_PAD), BF16)
        for hd in range(MLA_HEADS):
            cols = slice(hd * HEAD_PAD, (hd + 1) * HEAD_PAD)
            prod = oe_ref[:, cols].astype(F32) * doe_ref[:, cols].astype(F32)
            delta_ref[hd] = _dot01(prod, ones, dot=_dot_nt, left=True)

    return pl.pallas_call(
        body, name="outproj_bwd", grid=(s // TM,),
        in_specs=[_rows(TM, D_MODEL), _rows(TM, D_MODEL), _full((1, D_MODEL)), _resident((wide + SSD_INNER, D_MODEL)),
                  _rows(TM, wide)],
        out_specs=[_rows(TM, D_MODEL), _rows(TM, wide), _rows(TM, SSD_INNER), _full((1, D_MODEL)),
                   pl.BlockSpec((MLA_HEADS, 8, TM), lambda i: (0, 0, i))],
        out_shape=[jax.ShapeDtypeStruct((s, D_MODEL), BF16), jax.ShapeDtypeStruct((s, wide), BF16),
                   jax.ShapeDtypeStruct((s, SSD_INNER), F32), jax.ShapeDtypeStruct((1, D_MODEL), F32),
                   jax.ShapeDtypeStruct((MLA_HEADS, 8, s), F32)],
        compiler_params=_params(),
    )(dh1, mixed, nw, wout, oe)


def _attn_bwd(q, k, v, do, lse, delta, exchange=()):
    s = q.shape[0]
    t = ATT_T
    nq = s // t
    pair = 2 * HEAD_PAD
    ne = len(exchange)

    def body(q_ref, k_ref, v_ref, do_ref, lse_ref, delta_ref, *rest):
        e_in, (dq_ref, dk_ref, dv_ref), e_out = rest[:ne], rest[ne:ne + 3], rest[ne + 3:2 * ne + 3]
        dk_s, dv_s, bias_s = rest[2 * ne + 3:2 * ne + 6]
        kb = pl.program_id(1)
        _hosted_comm("exchange", e_in, e_out, rest[2 * ne + 6:],
                     jnp.logical_and(pl.program_id(0) == 0, kb == 0),
                     jnp.logical_and(pl.program_id(0) == MLA_HEADS // 2 - 1, kb == nq - 1))

        @pl.when(jnp.logical_and(pl.program_id(0) == 0, kb == 0))
        def _():
            bias_s[...] = _chunk_bias(t, keys_on_rows=True)

        @pl.when(kb == 0)
        def _():
            dq_ref[...] = jnp.zeros(dq_ref.shape, F32)

        def step(qb, diagonal):
            r0 = pl.multiple_of(qb * t, t)
            for hh in range(2):
                cols = slice(hh * HEAD_PAD, (hh + 1) * HEAD_PAD)
                kk = k_ref[:, cols]
                qq = q_ref[pl.ds(r0, t), cols]
                dd = do_ref[pl.ds(r0, t), cols]
                sc = _dot_nt(kk, qq) * ATT_SCALE_LOG2
                if diagonal:
                    sc = sc + bias_s[...]
                p = jnp.exp2(sc - lse_ref[hh, 0:1, pl.ds(r0, t)])
                dv = _dot(p.astype(BF16), dd)
                dp = _dot_nt(v_ref[:, cols], dd)
                ds = (p * (dp - delta_ref[hh, 0:1, pl.ds(r0, t)]) * ATT_SCALE).astype(BF16)
                dk = _dot(ds, qq)
                if diagonal:
                    dv_s[:, cols] = dv
                    dk_s[:, cols] = dk
                else:
                    dv_s[:, cols] += dv
                    dk_s[:, cols] += dk
                dq_ref[pl.ds(r0, t), cols] += _dot_tn(ds, kk)

        def loop(i, c):
            for u in range(ATT_UNROLL):
                step(kb + 1 + u + ATT_UNROLL * i, False)
            return c

        step(kb, True)
        later_tiles = nq - 1 - kb
        lax.fori_loop(0, later_tiles // ATT_UNROLL, loop, 0)
        left = later_tiles % ATT_UNROLL
        for u in range(ATT_UNROLL - 1):
            @pl.when(left > u)
            def _(u=u):
                step(nq - left + u, False)

        dk_ref[...] = dk_s[...].astype(BF16)
        dv_ref[...] = dv_s[...].astype(BF16)

    whole = pl.BlockSpec((s, pair), lambda h, i: (0, h))
    tile = pl.BlockSpec((t, pair), lambda h, i: (i, h))
    rowvec = pl.BlockSpec((2, 8, s), lambda h, i: (h, 0, 0))
    wide = MLA_HEADS * HEAD_PAD
    outs = pl.pallas_call(
        body, name="attn_bwd_exchange" if ne else "attn_bwd", grid=(MLA_HEADS // 2, nq),
        in_specs=[whole, tile, tile, whole, rowvec, rowvec] + [_ANY] * ne,
        out_specs=[whole, tile, tile] + [_ANY] * ne,
        out_shape=[jax.ShapeDtypeStruct((s, wide), F32)] + [jax.ShapeDtypeStruct((s, wide), BF16)] * 2
        + _comm_out_shapes("exchange", exchange),
        scratch_shapes=[pltpu.VMEM((t, pair), F32), pltpu.VMEM((t, pair), F32), pltpu.VMEM((t, t), F32)]
        + (_comm_scratch(ne) if ne else []),
        compiler_params=_params(),
    )(q, k, v, do, lse, delta, *exchange)
    return outs[0], outs[1], outs[2], list(outs[3:])


def _ssd_bwd(dy, ypre, z, c, xraw, misc, prev, cw, dtb, a_exp, d_exp, nw, consts):
    s = dy.shape[0]
    nb = s // SSD_ROWS
    ncb = SSD_ROWS // CHUNK
    emisc, emisc_t, tri, trit = consts

    def body(dy_ref, ypre_ref, z_ref, c_ref, x_ref, misc_ref, prev_ref, cw_ref, dtb_ref, a_ref, d_ref, nw_ref,
             emisc_ref, emisct_ref, tri_ref, trit_ref,
             dz_ref, dx_ref, dmisc_ref, gnw_ref, gd_ref, galog_ref, gdtb_ref, gcw_ref, gcb_ref,
             dst_s, dc_s, head_s):
        i = pl.program_id(0)
        first = i == 0

        @pl.when(first)
        def _():
            dst_s[...] = jnp.zeros(dst_s.shape, F32)
            head_s[...] = jnp.zeros(head_s.shape, F32)
            gnw_ref[...] = jnp.zeros(gnw_ref.shape, F32)
            gd_ref[...] = jnp.zeros(gd_ref.shape, F32)
            galog_ref[...] = jnp.zeros(galog_ref.shape, F32)
            gdtb_ref[...] = jnp.zeros(gdtb_ref.shape, F32)

        a_exp_v = a_ref[...]
        a8 = _dot01(a_exp_v, emisct_ref[...]) * (1.0 / SSD_P)

        def chunk(ci):
            r0 = ci * CHUNK
            cc = c_ref[pl.ds(r0, CHUNK), :]
            mm = misc_ref[pl.ds(r0, CHUNK), :]
            xa, sig_c, dt, acs, acs_t, alast = _ssd_chunk_common(cc, mm, emisc_ref[...], tri_ref[...], trit_ref[...],
                                                              dtb_ref[...], a_exp_v)
            yield
            xs = xa[:, :SSD_INNER]
            xdt = xs * dt
            y = ypre_ref[pl.ds(r0, CHUNK), :]
            zz = z_ref[pl.ds(r0, CHUNK), :]
            sg, yn, rs = _gate_norm(y, zz)
            dyo = dy_ref[pl.ds(r0, CHUNK), :]
            gnw_ref[...] += jnp.sum(dyo * yn, axis=0, keepdims=True)
            dyn = dyo * nw_ref[...]
            half = SSD_INNER // SSD_GROUPS
            dyz_parts = []
            for g in range(SSD_GROUPS):
                gl = slice(g * half, (g + 1) * half)
                dyz_parts.append(rs[g] * (dyn[:, gl] - yn[:, gl] * jnp.mean(dyn[:, gl] * yn[:, gl], axis=-1, keepdims=True)))
            dyz = jnp.concatenate(dyz_parts, axis=1)
            dz_ref[pl.ds(r0, CHUNK), :] = dyz * y * (sg * (1.0 + zz * (1.0 - sg)))
            dyp = dyz * (zz * sg)
            dypb = dyp.astype(BF16)
            gd_ref[...] += jnp.sum(dyp * xs, axis=0, keepdims=True)
            yield
            prev = prev_ref[ci]
            cd = jnp.exp(alast)
            e = jnp.exp(acs)
            dsx = jnp.exp(alast - acs)
            wgt = (xdt * dsx).astype(BF16)
            dze = (dyp * e).astype(BF16)
            dprev_parts, diag_all, dbm, dcm, yoff_parts, bms = [], [], [], [], [], []
            lane8 = lax.broadcasted_iota(jnp.int32, (CHUNK, HEAD_PAD), 1)
            diag8 = jnp.zeros((CHUNK, HEAD_PAD), F32)
            for g in range(SSD_GROUPS):
                gl = slice(g * 256, (g + 1) * 256)
                bm = xa[:, SSD_INNER + g * SSD_N:SSD_INNER + (g + 1) * SSD_N].astype(BF16)
                cm = xa[:, SSD_INNER + SSD_GROUPS * SSD_N + g * SSD_N:SSD_INNER + SSD_GROUPS * SSD_N + (g + 1) * SSD_N].astype(BF16)
                bms.append(bm)
                prev_g = prev[:, gl].astype(BF16)
                dcm_g = _dot_nt(dze[:, gl], prev_g)
                dprev_parts.append(_dot_tn(cm, dze[:, gl]))
                cb_g = _dot_nt(cm, bm)
                dcb = jnp.zeros((CHUNK, CHUNK), F32)
                diag_parts = []
                for jj in range(2):
                    pair = 2 * g + jj
                    pl_ = slice(pair * 128, (pair + 1) * 128)
                    xp = xdt[:, pl_]
                    dyp_p = dypb[:, pl_]
                    dxp = jnp.zeros((CHUNK, 128), F32)
                    for hh in range(2):
                        hd = 2 * pair + hh
                        dec = _decay(acs, acs_t, hd)
                        xm = jnp.where(_half_mask(hh), xp, 0.0).astype(BF16)
                        dsc = _dot_nt(dyp_p, xm) * dec
                        dcb = dcb + dsc
                        sc = (cb_g * dec).astype(BF16)
                        dxp = dxp + jnp.where(_half_mask(hh), _dot_tn(sc, dyp_p), 0.0)
                        dm = dsc * cb_g
                        diag8 = diag8 + jnp.where(lane8 == MISC_DT + hd, jnp.sum(dm - dm.T, axis=1, keepdims=True), 0.0)
                    diag_parts.append(dxp)
                dcbb = dcb.astype(BF16)
                dcm.append(dcm_g + _dot(dcbb, bm))
                dbm.append(_dot_tn(dcbb, cm))
                diag_all.append(jnp.concatenate(diag_parts, axis=1))
                yoff_parts.append(_dot(cm, prev_g) * e[:, gl])
                yield
            dst = dst_s[...]
            glast = jnp.sum(dst * prev, axis=0, keepdims=True) * cd
            dxdt_state_parts = []
            for g in range(SSD_GROUPS):
                gl = slice(g * 256, (g + 1) * 256)
                dst_g = dst[:, gl].astype(BF16)
                dxdt_state_parts.append(_dot(bms[g], dst_g) * dsx[:, gl])
                dbm[g] = dbm[g] + _dot_nt(wgt[:, gl], dst_g)
            dst_s[...] = dst * cd + jnp.concatenate(dprev_parts, axis=1)
            yield
            dxdt_state = jnp.concatenate(dxdt_state_parts, axis=1)
            dxdt = jnp.concatenate(diag_all, axis=1) + dxdt_state
            dacs = dyp * jnp.concatenate(yoff_parts, axis=1) - xdt * dxdt_state
            last = jnp.sum(xdt * dxdt_state, axis=0, keepdims=True) + glast
            row = lax.broadcasted_iota(jnp.int32, (CHUNK, SSD_INNER), 0)
            dacs = dacs + jnp.where(row == CHUNK - 1, last, 0.0)
            dacs8 = _dot01(dacs, emisct_ref[...]) + diag8
            da8 = _dot01(dacs8, trit_ref[...], left=True)
            ddt8 = da8 * a8 + _dot01(dxdt * xs, emisct_ref[...])
            yield
            dtr8 = mm + _dot01(dtb_ref[...], emisct_ref[...]) * (1.0 / SSD_P)
            dt8 = jax.nn.softplus(dtr8)
            lane = lax.broadcasted_iota(jnp.int32, (CHUNK, HEAD_PAD), 1)
            on_dt = jnp.logical_and(lane >= MISC_DT, lane < MISC_DT + SSD_HEADS)
            ddtr8 = jnp.where(on_dt, ddt8 * jax.nn.sigmoid(dtr8), 0.0)
            dmisc_ref[pl.ds(r0, CHUNK), :] = ddtr8
            gdtb_ref[...] += jnp.sum(ddtr8, axis=0, keepdims=True)
            galog_ref[...] += jnp.sum(jnp.where(on_dt, da8 * dt8, 0.0), axis=0, keepdims=True) * a8
            dxs = d_ref[...] * dyp + dxdt * dt
            dxa = jnp.concatenate([dxs] + dbm + dcm, axis=1)
            dc_s[pl.ds(r0, CHUNK), :] = dxa * (sig_c * (1.0 + cc * (1.0 - sig_c)))

        _interleave([chunk(ci) for ci in reversed(range(ncb))])

        dc = dc_s[...]
        x = x_ref[...]
        dcext = jnp.concatenate([dc, head_s[...]], axis=0)
        dx = dc * cw_ref[CONV_W - 1:CONV_W, :]
        rows = [jnp.sum(dc * x, axis=0, keepdims=True)]
        for j in range(1, CONV_W):
            ahead = pltpu.roll(dcext, SSD_ROWS + 8 - j, 0)[:SSD_ROWS, :]
            dx = dx + ahead * cw_ref[CONV_W - 1 - j:CONV_W - j, :]
            rows.insert(0, jnp.sum(ahead * x, axis=0, keepdims=True))
        dx_ref[...] = dx
        head_s[...] = dc[:8, :]
        gcw = jnp.concatenate(rows, axis=0)

        @pl.when(first)
        def _():
            gcw_ref[...] = gcw
            gcb_ref[...] = jnp.sum(dc, axis=0, keepdims=True)

        @pl.when(jnp.logical_not(first))
        def _():
            gcw_ref[...] += gcw
            gcb_ref[...] += jnp.sum(dc, axis=0, keepdims=True)

    def rev(width):
        return pl.BlockSpec((SSD_ROWS, width), lambda i: (nb - 1 - i, 0))

    return pl.pallas_call(
        body, name="ssd_bwd", grid=(nb,),
        in_specs=[rev(SSD_INNER), rev(SSD_INNER), rev(SSD_INNER), rev(CONV_DIM), rev(CONV_DIM),
                  rev(HEAD_PAD), pl.BlockSpec((ncb, SSD_N, SSD_INNER), lambda i: (nb - 1 - i, 0, 0)),
                  _full((CONV_W, CONV_DIM)), _full((1, SSD_INNER)), _full((1, SSD_INNER)), _full((1, SSD_INNER)),
                  _full((1, SSD_INNER)), _full((HEAD_PAD, SSD_INNER)), _full((SSD_INNER, HEAD_PAD)), _full((CHUNK, CHUNK)),
                  _full((CHUNK, CHUNK))],
        out_specs=[rev(SSD_INNER), rev(CONV_DIM), rev(HEAD_PAD), _full((1, SSD_INNER)), _full((1, SSD_INNER)),
                   _full((1, HEAD_PAD)), _full((1, HEAD_PAD)), _full((CONV_W, CONV_DIM)), _full((1, CONV_DIM))],
        out_shape=[jax.ShapeDtypeStruct((s, SSD_INNER), F32), jax.ShapeDtypeStruct((s, CONV_DIM), F32),
                   jax.ShapeDtypeStruct((s, HEAD_PAD), F32), jax.ShapeDtypeStruct((1, SSD_INNER), F32),
                   jax.ShapeDtypeStruct((1, SSD_INNER), F32), jax.ShapeDtypeStruct((1, HEAD_PAD), F32),
                   jax.ShapeDtypeStruct((1, HEAD_PAD), F32), jax.ShapeDtypeStruct((CONV_W, CONV_DIM), F32),
                   jax.ShapeDtypeStruct((1, CONV_DIM), F32)],
        scratch_shapes=[pltpu.VMEM((SSD_N, SSD_INNER), F32), pltpu.VMEM((SSD_ROWS, CONV_DIM), F32), pltpu.VMEM((8, CONV_DIM), F32)],
        compiler_params=_params(),
    )(dy, ypre, z, c, xraw, misc, prev, cw, dtb, a_exp, d_exp, nw, emisc, emisc_t, tri, trit)


def _qkv_inproj_bwd(dq, dk, dv, cq, ckv, dmisc_dt, dz, dxbc, h, dh1, qnw, kvnw, nw, wuq, wkv, win, cosf, sinf):
    s = dq.shape[0]
    wide = MLA_HEADS * HEAD_PAD
    tm = TM

    def body(dq_ref, dk_ref, dv_ref, cq_ref, ckv_ref, dmdt_ref, dz_ref, dxbc_ref, h_ref, dh1_ref, qnw_ref, kvnw_ref, nw_ref,
             wuq_ref, wkv_ref, win_ref, cos_ref, sin_ref, dqb_ref, dkvb_ref, dproj_ref, dh0_ref, gq_ref, gkv_ref, gnw_ref):
        first = pl.program_id(0) == 0
        cosf, sinf = cos_ref[...], sin_ref[...]
        dkr = jnp.zeros((tm, HEAD_PAD), F32)
        for hd in range(MLA_HEADS):
            cols = slice(hd * HEAD_PAD, (hd + 1) * HEAD_PAD)
            dqb_ref[:, cols] = _rope(dq_ref[:, cols], cosf, sinf, -1.0).astype(BF16)
            dkh = dk_ref[:, cols]
            dkvb_ref[:, cols] = dkh
            dkr = dkr + dkh
        dkvb_ref[:, wide:] = dv_ref[...]
        lane = lax.broadcasted_iota(jnp.int32, dkr.shape, 1)
        in_rope = jnp.logical_and(lane >= MISC_ROPE, lane < MISC_ROPE + QK_ROPE)
        dmisc_rope = jnp.where(in_rope, _rope(jnp.where(in_rope, dkr, 0.0), cosf, sinf, -1.0), 0.0)
        dcq, gq = _rms_bwd(cq_ref[...], qnw_ref[...], _dot_nt(dqb_ref[...], wuq_ref[...]))
        _acc_rows(gq_ref, gq, first)
        dckv, gkv = _rms_bwd(ckv_ref[...], kvnw_ref[...], _dot_nt(dkvb_ref[...], wkv_ref[...]))
        _acc_rows(gkv_ref, gkv, first)
        dproj_ref[:, 0:768] = dcq.astype(BF16)
        dproj_ref[:, 768:1024] = dckv.astype(BF16)
        dproj_ref[:, 1024:1152] = (dmisc_rope + dmdt_ref[...]).astype(BF16)
        dproj_ref[:, 1152:1664] = dz_ref[...].astype(BF16)
        dproj_ref[:, 1664:2688] = dxbc_ref[...].astype(BF16)
        dx, gnw = _rms_bwd(h_ref[...], nw_ref[...], _dot_nt(dproj_ref[...], win_ref[...]))
        _acc_rows(gnw_ref, gnw, first)
        dh0_ref[...] = dh1_ref[...] + dx

    return pl.pallas_call(
        body, name="qkv_inproj_bwd", grid=(s // tm,),
        in_specs=[_rows(tm, wide)] * 3 + [_rows(tm, Q_RANK), _rows(tm, KV_RANK), _rows(tm, HEAD_PAD), _rows(tm, SSD_INNER),
                                          _rows(tm, CONV_DIM), _rows(tm, D_MODEL), _rows(tm, D_MODEL),
                                          _full((1, Q_RANK)), _full((1, KV_RANK)), _full((1, D_MODEL)),
                                          _resident((Q_RANK, wide)), _resident((KV_RANK, 2 * wide)), _resident((D_MODEL, IN_PAD)),
                                          _rows(tm, HEAD_PAD), _rows(tm, HEAD_PAD)],
        out_specs=[_rows(tm, wide), _rows(tm, 2 * wide), _rows(tm, IN_PAD), _rows(tm, D_MODEL),
                   _full((1, Q_RANK)), _full((1, KV_RANK)), _full((1, D_MODEL))],
        out_shape=[jax.ShapeDtypeStruct((s, wide), BF16), jax.ShapeDtypeStruct((s, 2 * wide), BF16),
                   jax.ShapeDtypeStruct((s, IN_PAD), BF16), jax.ShapeDtypeStruct((s, D_MODEL), F32),
                   jax.ShapeDtypeStruct((1, Q_RANK), F32), jax.ShapeDtypeStruct((1, KV_RANK), F32),
                   jax.ShapeDtypeStruct((1, D_MODEL), F32)],
        compiler_params=_params(),
    )(dq, dk, dv, cq, ckv, dmisc_dt, dz, dxbc, h, dh1, qnw, kvnw, nw, wuq, wkv, win, cosf, sinf)


def _row_tile(rows, cols):
    cap = max(8, (1 << 18) // max(cols, 128))
    best = None
    for t in range(8, rows + 1, 8):
        if rows % t == 0 and t <= cap:
            best = t
    return best if best is not None else rows


def _adamw(w, g, m, v, name):
    rows, cols = w.shape
    tr = _row_tile(rows, cols)

    def body(w_ref, g_ref, m_ref, v_ref, d_ref, m2_ref, v2_ref):
        gg = g_ref[...]
        m2 = ADAM_B1 * m_ref[...] + (1.0 - ADAM_B1) * gg
        v2 = ADAM_B2 * v_ref[...] + (1.0 - ADAM_B2) * jnp.square(gg)
        m_hat = m2 / (1.0 - ADAM_B1 ** ADAM_STEP)
        v_hat = v2 / (1.0 - ADAM_B2 ** ADAM_STEP)
        d_ref[...] = -ADAM_LR * (m_hat / (jnp.sqrt(v_hat) + ADAM_EPS) + ADAM_WD * w_ref[...])
        m2_ref[...] = m2
        v2_ref[...] = v2

    spec = pl.BlockSpec((tr, cols), lambda i: (i, 0))
    return pl.pallas_call(
        body, name=name, grid=(rows // tr,),
        in_specs=[spec] * 4, out_specs=[spec] * 3,
        out_shape=[jax.ShapeDtypeStruct((rows, cols), F32)] * 3,
    )(w, g, m, v)


def _sum_adamw(slots, w, m, v, name):
    _, rows, cols = w.shape
    tr = _row_tile(rows, 4 * cols)
    nb = rows // tr

    def body(s0_ref, s1_ref, w_ref, m_ref, v_ref, g_ref, d_ref, m2_ref, v2_ref):
        for l, ref in enumerate((s0_ref, s1_ref)):
            @pl.when(pl.program_id(0) == l)
            def _(ref=ref):
                acc = ref[0].astype(F32)
                for i in range(1, N_DEV):
                    acc = acc + ref[i].astype(F32)
                g_ref[...] = acc

        gg = g_ref[...]
        m2 = ADAM_B1 * m_ref[...] + (1.0 - ADAM_B1) * gg
        v2 = ADAM_B2 * v_ref[...] + (1.0 - ADAM_B2) * jnp.square(gg)
        m_hat = m2 / (1.0 - ADAM_B1 ** ADAM_STEP)
        v_hat = v2 / (1.0 - ADAM_B2 ** ADAM_STEP)
        d_ref[...] = -ADAM_LR * (m_hat / (jnp.sqrt(v_hat) + ADAM_EPS) + ADAM_WD * w_ref[...])
        m2_ref[...] = m2
        v2_ref[...] = v2

    slot_spec = lambda layer: pl.BlockSpec((N_DEV, tr, cols), lambda l, i: (0, jnp.where(l == layer, i, (nb - 1) * (1 - layer)), 0))
    spec = pl.BlockSpec((None, tr, cols), lambda l, i: (l, i, 0))
    return pl.pallas_call(
        body, name=name, grid=(DEPTH, nb),
        in_specs=[slot_spec(0), slot_spec(1), spec, spec, spec], out_specs=[spec] * 4,
        out_shape=[jax.ShapeDtypeStruct(w.shape, F32)] * 4,
        compiler_params=_params(),
    )(slots[0], slots[1], w, m, v)


_MESH = pl.DeviceIdType.MESH
_ANY = pl.BlockSpec(memory_space=pl.ANY)


def _my_place():
    return lax.axis_index("x"), lax.axis_index("y"), lax.axis_index("c")


def _flip(place, k):
    x, y, c = place
    return (1 - x if k & 4 else x, 1 - y if k & 2 else y, 1 - c if k & 1 else c)


def _block_id(place):
    return 4 * place[0] + 2 * place[1] + place[2]


def _peer_copies(kind, in_refs, out_refs, send_sems, recv_sems, local_sems):
    me = _my_place()
    my = _block_id(me)
    remote, local = [], []
    for a, (x_ref, out_ref) in enumerate(zip(in_refs, out_refs)):
        src_of = (lambda place, r=x_ref: r) if kind == "gather" else (lambda place, r=x_ref: r.at[_block_id(place)])
        local.append(pltpu.make_async_copy(src_of(me), out_ref.at[my], local_sems.at[a]))
        for k in range(1, N_DEV):
            peer = _flip(me, k)
            remote.append(pltpu.make_async_remote_copy(
                src_ref=src_of(peer), dst_ref=out_ref.at[my], send_sem=send_sems.at[a * 7 + k - 1],
                recv_sem=recv_sems.at[a * 7 + k - 1], device_id=peer, device_id_type=_MESH))
    return remote, local


def _comm_out_shapes(kind, arrays):
    return [jax.ShapeDtypeStruct((N_DEV, *a.shape) if kind == "gather" else a.shape, a.dtype) for a in arrays]


def _comm_scratch(n):
    return [pltpu.SemaphoreType.DMA((7 * n,)), pltpu.SemaphoreType.DMA((7 * n,)), pltpu.SemaphoreType.DMA((n,))]


def _hosted_comm(kind, in_refs, out_refs, sems, first, last):
    if not in_refs:
        return

    @pl.when(first)
    def _():
        remote, local = _peer_copies(kind, in_refs, out_refs, *sems)
        for cp in local + remote:
            cp.start()

    @pl.when(last)
    def _():
        remote, local = _peer_copies(kind, in_refs, out_refs, *sems)
        for cp in remote:
            cp.wait()
        for cp in local:
            cp.wait()


def _two_level_gather_steps(in_refs, out_refs, send_sems, recv_sems, local_sems):
    n = len(in_refs)
    me = _my_place()
    x, y, c = me
    sibling = (x, y, 1 - c)
    chips = [(1 - x, y), (x, 1 - y), (1 - x, 1 - y)]

    def copy(a, k, place, to, src=None):
        block = out_refs[a].at[_block_id(place)]
        return pltpu.make_async_remote_copy(
            src_ref=block if src is None else src, dst_ref=block, send_sem=send_sems.at[7 * a + k],
            recv_sem=recv_sems.at[7 * a + k], device_id=to, device_id_type=_MESH)

    mine = [pltpu.make_async_copy(in_refs[a], out_refs[a].at[_block_id(me)], local_sems.at[a]) for a in range(n)]
    first = [copy(a, 0, me, sibling, src=in_refs[a]) for a in range(n)]
    first += [copy(a, 1 + j, me, (*chip, c), src=in_refs[a]) for a in range(n) for j, chip in enumerate(chips)]
    passed = [copy(a, 4 + j, (*chip, c), sibling) for a in range(n) for j, chip in enumerate(chips)]

    def send():
        for cp in mine + first:
            cp.start()

    def forward():
        for a in range(n):
            for j, chip in enumerate(chips):
                copy(a, 1 + j, (*chip, c), me).wait_recv()
                passed[3 * a + j].start()

    def finish():
        for a in range(n):
            copy(a, 0, sibling, me).wait_recv()
            for j, chip in enumerate(chips):
                copy(a, 4 + j, (*chip, 1 - c), me).wait_recv()
        for cp in first + passed:
            cp.wait_send()
        for cp in mine:
            cp.wait()

    return send, forward, finish


def _gather_two_level(arrays, name):
    n = len(arrays)

    def body(*refs):
        for step in _two_level_gather_steps(refs[:n], refs[n:2 * n], *refs[2 * n:]):
            step()

    return pl.pallas_call(
        body, name=name, out_shape=_comm_out_shapes("gather", arrays),
        in_specs=[_ANY] * n, out_specs=[_ANY] * n, scratch_shapes=_comm_scratch(n),
    )(*arrays)


def _hosted_gather(in_refs, out_refs, sems, first, middle, last):
    if not in_refs:
        return
    for when, index in ((first, 0), (middle, 1), (last, 2)):
        @pl.when(when)
        def _(index=index):
            _two_level_gather_steps(in_refs, out_refs, *sems)[index]()


def _comm(kind, arrays, name):
    n = len(arrays)

    def body(*refs):
        remote, local = _peer_copies(kind, refs[:n], refs[n:2 * n], *refs[2 * n:])
        for cp in local + remote:
            cp.start()
        for cp in remote:
            cp.wait()
        for cp in local:
            cp.wait()

    return pl.pallas_call(
        body, name=name, out_shape=_comm_out_shapes(kind, arrays),
        in_specs=[_ANY] * n, out_specs=[_ANY] * n, scratch_shapes=_comm_scratch(n),
    )(*arrays)


def _all_reduce_small(part):
    rows, lanes = part.shape
    vmem = pl.BlockSpec(memory_space=pltpu.VMEM)

    def body(x_ref, gath_ref, sum_ref, send_sems, recv_sems):
        me = _my_place()
        my = _block_id(me)
        gath_ref[my] = x_ref[...]
        copies = []
        for k in range(1, N_DEV):
            cp = pltpu.make_async_remote_copy(
                src_ref=x_ref, dst_ref=gath_ref.at[my], send_sem=send_sems.at[k - 1], recv_sem=recv_sems.at[k - 1],
                device_id=_flip(me, k), device_id_type=_MESH)
            cp.start()
            copies.append(cp)
        for cp in copies:
            cp.wait()
        acc = gath_ref[0]
        for i in range(1, N_DEV):
            acc = acc + gath_ref[i]
        sum_ref[...] = acc

    return pl.pallas_call(
        body, name="small_grad_all_reduce",
        out_shape=[jax.ShapeDtypeStruct((N_DEV, rows, lanes), F32), jax.ShapeDtypeStruct((rows, lanes), F32)],
        in_specs=[vmem], out_specs=[vmem, vmem],
        scratch_shapes=[pltpu.SemaphoreType.DMA((7,)), pltpu.SemaphoreType.DMA((7,))],
    )(part)[1]


_SHARDED = (("w_in", (D_MODEL, IN_PROJ // N_DEV)), ("w_uq", (Q_RANK // N_DEV, Q_RANK)), ("w_ukv", (KV_RANK, HEAD_PAD)),
            ("conv_w", (CONV_W, CONV_DIM // N_DEV)), ("w_out", (D_MODEL // N_DEV, D_MODEL)),
            ("w_up", (D_MODEL, D_FF // N_DEV)), ("w_down", (D_FF // N_DEV, D_MODEL)))
_SMALL = (("pre_mix_norm", D_MODEL), ("q_norm", Q_RANK), ("kv_norm", KV_RANK), ("conv_b", CONV_DIM), ("dt_bias", SSD_HEADS),
          ("a_log", SSD_HEADS), ("d_skip", SSD_HEADS), ("ssd_norm", SSD_INNER), ("post_mix_norm", D_MODEL),
          ("pre_mlp_norm", D_MODEL), ("post_mlp_norm", D_MODEL))
_WEIGHT_ORDER = ("pre_mix_norm", "w_in", "q_norm", "w_uq", "kv_norm", "w_ukv", "conv_w", "conv_b", "dt_bias", "a_log", "d_skip",
                 "ssd_norm", "w_out", "post_mix_norm", "pre_mlp_norm", "w_up", "w_down", "post_mlp_norm")
_EARLY = ("w_in", "w_uq", "w_ukv", "conv_w")
_LATE = ("w_out", "w_up", "w_down")


def _wire_shard(name, a):
    return lax.bitcast_convert_type(a, BF16).reshape(CONV_W, -1) if name == "conv_w" else a.astype(BF16)


def _from_wire(name, g):
    return lax.bitcast_convert_type(g.reshape(N_DEV, CONV_W, -1, 2), F32) if name == "conv_w" else g


def _cols(stacked):
    return jnp.transpose(stacked, (1, 0, 2)).reshape(stacked.shape[1], -1)


def _win_segments():
    s2, s3, s5 = Q_RANK + KV_RANK, Q_RANK + KV_RANK + QK_ROPE, IN_PROJ - SSD_HEADS
    return [(0, s2), (None, MISC_ROPE), (s2, s3), (s5, IN_PROJ), (None, HEAD_PAD - MISC_DT - SSD_HEADS), (s3, s5)]


def _win_from_shards(stacked):
    per = IN_PROJ // N_DEV
    parts = []
    for start, stop in _win_segments():
        if start is None:
            parts.append(jnp.zeros((D_MODEL, stop), stacked.dtype))
            continue
        while start < stop:
            j, a = divmod(start, per)
            b = min(per, a + stop - start)
            parts.append(stacked[j, :, a:b])
            start += b - a
    return jnp.concatenate(parts, axis=1)


def _win_grad_shards(dwin):
    per = IN_PROJ // N_DEV
    runs, at = [], 0
    for start, stop in _win_segments():
        if start is not None:
            runs.append((start, stop, at))
        at += stop if start is None else stop - start
    blocks = []
    for j in range(N_DEV):
        lo, hi = j * per, (j + 1) * per
        parts = [dwin[:, p + max(lo, a) - a:p + min(hi, b) - a] for a, b, p in sorted(runs) if max(lo, a) < min(hi, b)]
        blocks.append(jnp.concatenate(parts, axis=1))
    return jnp.stack(blocks)


def _early_weights(sh):
    win = _win_from_shards(sh["w_in"])
    w_uq = sh["w_uq"].reshape(Q_RANK, MLA_HEADS, QK_NOPE + QK_ROPE)
    wuq = jnp.pad(w_uq, ((0, 0), (0, 0), (0, HEAD_PAD - QK_NOPE - QK_ROPE))).reshape(Q_RANK, -1)
    w_ukv = _cols(sh["w_ukv"]).reshape(KV_RANK, MLA_HEADS, QK_NOPE + V_DIM)
    wkn = jnp.pad(w_ukv[..., :QK_NOPE], ((0, 0), (0, 0), (0, HEAD_PAD - QK_NOPE))).reshape(KV_RANK, -1)
    wv = w_ukv[..., QK_NOPE:].reshape(KV_RANK, 4, 2, 1, V_DIM) * jnp.eye(2, dtype=BF16).reshape(1, 1, 2, 2, 1)
    wkv = jnp.concatenate([wkn, wv.reshape(KV_RANK, -1)], axis=1)
    return dict(win=win, wuq=wuq, wkv=wkv, conv_w=_cols(sh["conv_w"]))


def _late_weights(sh):
    w_out = sh["w_out"].reshape(D_MODEL, D_MODEL)
    watt = w_out[:SSD_INNER].reshape(4, 2, 1, V_DIM, D_MODEL) * jnp.eye(2, dtype=BF16).reshape(1, 2, 2, 1, 1)
    wout = jnp.concatenate([watt.reshape(MLA_HEADS * HEAD_PAD, D_MODEL), w_out[SSD_INNER:]], axis=0)
    return dict(wout=wout, wup=sh["w_up"], wdown=sh["w_down"])


def _shard_grads(g):
    out = {}
    if "wup" in g:
        out["w_up"], out["w_down"] = g["wup"], g["wdown"]
        ae = g["wout_att"].reshape(4, 2, 2, V_DIM, D_MODEL)
        att = jnp.stack([ae[:, 0, 0], ae[:, 1, 1]], axis=1).reshape(SSD_INNER, D_MODEL)
        out["w_out"] = jnp.concatenate([att, g["wout_ssd"]], axis=0).astype(BF16).reshape(N_DEV, D_MODEL // N_DEV, D_MODEL)
    if "win" not in g:
        return out
    out["w_in"] = _win_grad_shards(g["win"].astype(BF16))
    w_uq = g["wuq"].astype(BF16).reshape(Q_RANK, MLA_HEADS, HEAD_PAD)[..., :QK_NOPE + QK_ROPE].reshape(Q_RANK, Q_RANK)
    out["w_uq"] = w_uq.reshape(N_DEV, Q_RANK // N_DEV, Q_RANK)
    wide = MLA_HEADS * HEAD_PAD
    wkv = g["wkv"].astype(BF16)
    kn = wkv[:, :wide].reshape(KV_RANK, MLA_HEADS, HEAD_PAD)[..., :QK_NOPE]
    ve = wkv[:, wide:].reshape(KV_RANK, 4, 2, 2, V_DIM)
    vv = jnp.stack([ve[:, :, 0, 0], ve[:, :, 1, 1]], axis=2).reshape(KV_RANK, MLA_HEADS, V_DIM)
    out["w_ukv"] = jnp.transpose(jnp.concatenate([kn, vv], axis=-1), (1, 0, 2))
    out["conv_w"] = jnp.transpose(g["conv_w"].astype(BF16).reshape(CONV_W, N_DEV, -1), (1, 0, 2))
    return out


def _small_rows(n):
    return -(-n // 1024) * 8


def _pack_small(vals):
    rows = []
    for l in range(DEPTH):
        for name, n in _SMALL:
            r = _small_rows(n)
            rows.append(jnp.pad(vals[name][l].reshape(-1), (0, r * 128 - n)).reshape(r, 128))
    return jnp.concatenate(rows, axis=0)


def _unpack_small(packed):
    out, off = {name: [] for name, _ in _SMALL}, 0
    for l in range(DEPTH):
        for name, n in _SMALL:
            r = _small_rows(n)
            out[name].append(packed[off:off + r].reshape(-1)[:n])
            off += r
    return {name: jnp.stack(v) for name, v in out.items()}


def _lane_rows(vec8):
    return jnp.repeat(vec8, SSD_P).reshape(1, SSD_INNER)


def _layer_fwd(h, kw, sm, l, cosf, sinf, consts, gather=(), after_gather=None, target=None):
    row = lambda name: sm[name][l].reshape(1, -1)
    t = {}
    t["h0"] = h
    (t["ub"], t["cq"], t["ckv"], t["misc"], t["z"], t["xraw"], t["cqn"], t["ckvn"], t["q"], t["k"], t["v"]) = _inproj_qkv_fwd(
        h, row("pre_mix_norm"), kw["win"], row("q_norm"), row("kv_norm"), kw["wuq"], kw["wkv"], cosf, sinf)
    t["oe"], t["lse"], gathered = _attn_fwd(t["q"], t["k"], t["v"], gather)
    if after_gather is not None:
        after_gather(gathered)
    t["dtb"] = _lane_rows(sm["dt_bias"][l])
    t["a_exp"] = _lane_rows(-jnp.exp(sm["a_log"][l]))
    t["d_exp"] = _lane_rows(sm["d_skip"][l])
    t["c"], t["prev"], t["ypre"], t["yssd"], t["mixed"], t["h1"] = _ssd_outproj_fwd(
        t["xraw"], t["misc"], t["z"], kw["conv_w"], row("conv_b"), t["dtb"], t["a_exp"], t["d_exp"], row("ssd_norm"), consts,
        t["oe"], h, kw["wout"], row("post_mix_norm"))
    t["mb"], t["ab"], t["d"], *out = _mlp_fwd(t["h1"], row("pre_mlp_norm"), kw["wup"], kw["wdown"], row("post_mlp_norm"), target)
    return out, t


def _layer_bwd(dh2, t, kw, sm, l, cosf, sinf, consts, exchange_of=None):
    row = lambda name: sm[name][l].reshape(1, -1)
    g, gs = {}, {}
    dh1, dab, ddb, gs["post_mlp_norm"], gs["pre_mlp_norm"] = _mlp_bwd(
        dh2, t["d"], t["h1"], t["ab"], row("pre_mlp_norm"), kw["wup"], kw["wdown"], row("post_mlp_norm"))
    g["wup"] = _matmul_tn_stacked(t["mb"], dab, f"dw_up_{l}", a_stacked=False)
    g["wdown"] = _matmul_tn_stacked(t["ab"], ddb, f"dw_down_{l}", a_stacked=True, square_a=True)
    dmixb, doe, dyssd, gs["post_mix_norm"], delta = _outproj_bwd(dh1, t["mixed"], row("post_mix_norm"), kw["wout"], t["oe"])
    g["wout_att"] = _matmul_tn(t["oe"], dmixb, f"dw_out_att_{l}")
    g["wout_ssd"] = _matmul_tn(t["yssd"], dmixb, f"dw_out_ssd_{l}")
    dz, dxraw, dmisc_dt, gs["ssd_norm"], gd, galog, gdtb, g["conv_w"], gs["conv_b"] = _ssd_bwd(
        dyssd, t["ypre"], t["z"], t["c"], t["xraw"], t["misc"], t["prev"], kw["conv_w"], t["dtb"], t["a_exp"], t["d_exp"],
        row("ssd_norm"), consts)
    gs["d_skip"] = jnp.sum(gd.reshape(SSD_HEADS, SSD_P), axis=1)
    gs["a_log"] = galog[0, MISC_DT:MISC_DT + SSD_HEADS]
    gs["dt_bias"] = gdtb[0, MISC_DT:MISC_DT + SSD_HEADS]
    dq, dk, dv, exchanged = _attn_bwd(t["q"], t["k"], t["v"], doe, t["lse"], delta,
                                      exchange_of(g) if exchange_of is not None else ())
    dqb, dkvb, dprojb, dh0, gs["q_norm"], gs["kv_norm"], gs["pre_mix_norm"] = _qkv_inproj_bwd(
        dq, dk, dv, t["cq"], t["ckv"], dmisc_dt, dz, dxraw, t["h0"], dh1, row("q_norm"), row("kv_norm"),
        row("pre_mix_norm"), kw["wuq"], kw["wkv"], kw["win"], cosf, sinf)
    g["wuq"] = _matmul_tn(t["cqn"], dqb, f"dw_uq_{l}")
    g["wkv"] = _matmul_tn(t["ckvn"], dkvb, f"dw_kv_{l}")
    g["win"] = _matmul_tn(t["ub"], dprojb, f"dw_in_{l}")
    return dh0, g, {k: v.reshape(-1) for k, v in gs.items()}, exchanged


def _local_step(x, positions, kws, sm, target, gather=(), after_gather=None, exchange_of=None):
    inv_freq = ROPE_THETA ** (-jnp.arange(0, QK_ROPE, 2, dtype=F32) / QK_ROPE)
    invf = jnp.zeros((HEAD_PAD,), F32).at[MISC_ROPE:MISC_ROPE + QK_ROPE].set(jnp.concatenate([inv_freq, inv_freq]))
    cosf, sinf = _rope_tables(positions.reshape(-1, 1), invf.reshape(1, HEAD_PAD))
    consts = _ssd_consts()
    (h,), t0 = _layer_fwd(x, kws[0], sm, 0, cosf, sinf, consts, gather, after_gather)
    (dh, loss), t1 = _layer_fwd(h, kws[1], sm, 1, cosf, sinf, consts, target=target)
    saved = [t0, t1]
    grads, small, exchanged = [None] * DEPTH, [None] * DEPTH, []
    for l in reversed(range(DEPTH)):
        hook = (lambda g0: exchange_of(g0, grads[1])) if (l == 0 and exchange_of is not None) else None
        dh, grads[l], small[l], got = _layer_bwd(dh, saved[l], kws[l], sm, l, cosf, sinf, consts, hook)
        exchanged = got or exchanged
    return loss[0, 0], dh, grads, small, exchanged


def kernel(x, positions, pre_mix_norm, w_in, q_norm, w_uq, kv_norm, w_ukv, conv_w, conv_b, dt_bias, a_log, d_skip, ssd_norm, w_out, post_mix_norm, pre_mlp_norm, w_up, w_down, post_mlp_norm, loss_target, m_pre_mix_norm, m_w_in, m_q_norm, m_w_uq, m_kv_norm, m_w_ukv, m_conv_w, m_conv_b, m_dt_bias, m_a_log, m_d_skip, m_ssd_norm, m_w_out, m_post_mix_norm, m_pre_mlp_norm, m_w_up, m_w_down, m_post_mlp_norm, v_pre_mix_norm, v_w_in, v_q_norm, v_w_uq, v_kv_norm, v_w_ukv, v_conv_w, v_conv_b, v_dt_bias, v_a_log, v_d_skip, v_ssd_norm, v_w_out, v_post_mix_norm, v_pre_mlp_norm, v_w_up, v_w_down, v_post_mlp_norm):
    w = dict(pre_mix_norm=pre_mix_norm, w_in=w_in, q_norm=q_norm, w_uq=w_uq, kv_norm=kv_norm, w_ukv=w_ukv, conv_w=conv_w,
             conv_b=conv_b, dt_bias=dt_bias, a_log=a_log, d_skip=d_skip, ssd_norm=ssd_norm, w_out=w_out,
             post_mix_norm=post_mix_norm, pre_mlp_norm=pre_mlp_norm, w_up=w_up, w_down=w_down, post_mlp_norm=post_mlp_norm)
    m = dict(pre_mix_norm=m_pre_mix_norm, w_in=m_w_in, q_norm=m_q_norm, w_uq=m_w_uq, kv_norm=m_kv_norm, w_ukv=m_w_ukv,
             conv_w=m_conv_w, conv_b=m_conv_b, dt_bias=m_dt_bias, a_log=m_a_log, d_skip=m_d_skip, ssd_norm=m_ssd_norm,
             w_out=m_w_out, post_mix_norm=m_post_mix_norm, pre_mlp_norm=m_pre_mlp_norm, w_up=m_w_up, w_down=m_w_down,
             post_mlp_norm=m_post_mlp_norm)
    v = dict(pre_mix_norm=v_pre_mix_norm, w_in=v_w_in, q_norm=v_q_norm, w_uq=v_w_uq, kv_norm=v_kv_norm, w_ukv=v_w_ukv,
             conv_w=v_conv_w, conv_b=v_conv_b, dt_bias=v_dt_bias, a_log=v_a_log, d_skip=v_d_skip, ssd_norm=v_ssd_norm,
             w_out=v_w_out, post_mix_norm=v_post_mix_norm, pre_mlp_norm=v_pre_mlp_norm, w_up=v_w_up, w_down=v_w_down,
             post_mlp_norm=v_post_mlp_norm)
    sm = {name: w[name] for name, _ in _SMALL}

    wire = lambda name, l: _wire_shard(name, w[name][l])
    first = _gather_two_level([wire(name, 0) for name in _EARLY], "weight_gather_first")
    kws = [_early_weights({name: _from_wire(name, a) for name, a in zip(_EARLY, first)}), None]
    behind = [(name, 0) for name in _LATE] + [(name, 1) for name, _ in _SHARDED]

    def after_gather(gathered):
        got = {key: _from_wire(key[0], a) for key, a in zip(behind, gathered)}
        kws[0].update(_late_weights({name: got[name, 0] for name in _LATE}))
        kws[1] = {**_early_weights({name: got[name, 1] for name in _EARLY}),
                  **_late_weights({name: got[name, 1] for name in _LATE})}

    sent_behind = [(name, 1) for name, _ in _SHARDED] + [(name, 0) for name in _LATE]

    def exchange_of(g0, g1):
        blocks = {**{(name, 1): a for name, a in _shard_grads(g1).items()},
                  **{(name, 0): a for name, a in _shard_grads(g0).items()}}
        return [blocks[key] for key in sent_behind]

    loss_part, dx, grads, small, exchanged = _local_step(
        x[0], positions[0], kws, sm, loss_target[0], [wire(*key) for key in behind], after_gather, exchange_of)
    slots = dict(zip(sent_behind, exchanged))
    last = _shard_grads({k: grads[0][k] for k in ("win", "wuq", "wkv", "conv_w")})
    slots.update({(name, 0): a for name, a in zip(_EARLY, _comm("exchange", [last[name] for name in _EARLY], "grad_exchange_last"))})
    g_small = _unpack_small(_all_reduce_small(_pack_small({name: jnp.stack([small[l][name] for l in range(DEPTH)])
                                                           for name, _ in _SMALL})))
    loss = lax.psum(loss_part, ("x", "y", "c"))

    grad, delta, new_m, new_v = {}, {}, {}, {}
    for name, _ in _SHARDED:
        grad[name], delta[name], new_m[name], new_v[name] = _sum_adamw(
            [slots[name, 0], slots[name, 1]], w[name], m[name], v[name], f"sum_adamw_{name}")
    pk = lambda d: _pack_small({name: d[name] for name, _ in _SMALL})
    d_, m_, v_ = _adamw(pk(w), pk(g_small), pk(m), pk(v), "adamw_small")
    for dst, packed in ((delta, d_), (new_m, m_), (new_v, v_)):
        dst.update(_unpack_small(packed))
    grad.update(g_small)

    outs = [loss, dx[None]]
    for d in (grad, delta, new_m, new_v):
        outs += [d[name] for name in _WEIGHT_ORDER]
    return tuple(outs)
```
